```python
import math
import jax, jax.numpy as jnp
from jax import lax
import numpy as np

D_MODEL = 2048
BATCH = 8
SEQ = 2048
DEPTH = 1

D_MIX = D_MODEL
HEAD_DIM = 64
D_ATTN = D_MIX // 2
N_Q_HEADS = D_ATTN // HEAD_DIM
N_KV_HEADS = 4
Q_PER_KV = N_Q_HEADS // N_KV_HEADS
D_KV = N_KV_HEADS * HEAD_DIM
WINDOW = 128
BLOCK = 128
ROPE_THETA = 10000.0
D_SSM = D_MIX - D_ATTN
SSM_GROUP = 16
N_SSM_GROUPS = D_SSM // SSM_GROUP
SSM_STATE = 64
D_IN = D_ATTN + 2 * D_KV + D_SSM
D_FF = ((8 * D_MODEL // 3 + 255) // 256) * 256
RMS_EPS = 1e-6

kernel_name = 'hymba_swa_sink_s5_sandwich_block'


def rms_norm(x, g):
    xf = x.astype(jnp.float32)
    y = xf * lax.rsqrt(jnp.mean(xf * xf, axis=-1, keepdims=True) + RMS_EPS)
    return (y * g.astype(jnp.float32)).astype(x.dtype)


def rotary(t, positions):
    half = HEAD_DIM // 2
    inv_freq = ROPE_THETA ** (-jnp.arange(half, dtype=jnp.float32) / half)
    ang = positions.astype(jnp.float32)[:, :, None] * inv_freq
    cos = jnp.cos(ang)[:, :, None, :]
    sin = jnp.sin(ang)[:, :, None, :]
    tf = t.astype(jnp.float32)
    t1, t2 = tf[..., :half], tf[..., half:]
    return jnp.concatenate([t1 * cos - t2 * sin, t2 * cos + t1 * sin], axis=-1).astype(t.dtype)


def sliding_window_attention(q, k, v, sinks):
    B, L = q.shape[0], q.shape[1]
    nb = L // BLOCK
    qb = q.reshape(B, nb, BLOCK, N_KV_HEADS, Q_PER_KV, HEAD_DIM)
    kb = k.reshape(B, nb, BLOCK, N_KV_HEADS, HEAD_DIM)
    vb = v.reshape(B, nb, BLOCK, N_KV_HEADS, HEAD_DIM)
    pad = ((0, 0), (1, 0), (0, 0), (0, 0), (0, 0))
    kk = jnp.concatenate([jnp.pad(kb, pad)[:, :-1], kb], axis=2)
    vv = jnp.concatenate([jnp.pad(vb, pad)[:, :-1], vb], axis=2)
    scale = 1.0 / math.sqrt(HEAD_DIM)
    scores = jnp.einsum('bnqkgd,bnskd->bnkgqs', qb, kk).astype(jnp.float32) * scale
    blk = jnp.arange(nb, dtype=jnp.int32)[:, None] * BLOCK
    q_pos = blk + jnp.arange(BLOCK, dtype=jnp.int32)[None, :]
    k_pos = blk - BLOCK + jnp.arange(2 * BLOCK, dtype=jnp.int32)[None, :]
    diff = q_pos[:, :, None] - k_pos[:, None, :]
    mask = (diff >= 0) & (diff < WINDOW) & (k_pos[:, None, :] >= 0)
    scores = jnp.where(mask[None, :, None, None], scores, -jnp.inf)
    sink = sinks.astype(jnp.float32).reshape(N_KV_HEADS, Q_PER_KV)[None, None, :, :, None, None]
    m = jnp.maximum(jnp.max(scores, axis=-1, keepdims=True), sink)
    p = jnp.exp(scores - m)
    probs = p / (jnp.sum(p, axis=-1, keepdims=True) + jnp.exp(sink - m))
    out = jnp.einsum('bnkgqs,bnskd->bnqkgd', probs.astype(v.dtype), vv)
    return out.reshape(B, L, N_Q_HEADS * HEAD_DIM)


def s5_ssm(u, a_re, a_im, log_dt, b_re, b_im, c_re, c_im, d_skip):
    L = u.shape[1]
    uf = u.astype(jnp.float32)
    dt = jnp.exp(log_dt.astype(jnp.float32))[:, None]
    ar = a_re.astype(jnp.float32)
    ai = a_im.astype(jnp.float32)
    mag = jnp.exp(ar * dt)
    lam_re = mag * jnp.cos(ai * dt)
    lam_im = mag * jnp.sin(ai * dt)
    den = ar * ar + ai * ai
    nr = lam_re - 1.0
    ni = lam_im
    f_re = (nr * ar + ni * ai) / den
    f_im = (ni * ar - nr * ai) / den
    br = b_re.astype(jnp.float32)
    bi = b_im.astype(jnp.float32)
    bbar_re = f_re[..., None] * br - f_im[..., None] * bi
    bbar_im = f_re[..., None] * bi + f_im[..., None] * br
    bu_re = jnp.einsum('blgp,gnp->blgn', uf, bbar_re)
    bu_im = jnp.einsum('blgp,gnp->blgn', uf, bbar_im)
    shp = (1, L) + lam_re.shape
    a_seq_re = jnp.broadcast_to(lam_re[None, None], shp)
    a_seq_im = jnp.broadcast_to(lam_im[None, None], shp)

    def combine(earlier, later):
        a1r, a1i, b1r, b1i = earlier
        a2r, a2i, b2r, b2i = later
        return (a2r * a1r - a2i * a1i,
                a2r * a1i + a2i * a1r,
                a2r * b1r - a2i * b1i + b2r,
                a2r * b1i + a2i * b1r + b2i)

    _, _, s_re, s_im = lax.associative_scan(combine, (a_seq_re, a_seq_im, bu_re, bu_im), axis=1)
    y = (jnp.einsum('blgn,gpn->blgp', s_re, c_re.astype(jnp.float32))
         - jnp.einsum('blgn,gpn->blgp', s_im, c_im.astype(jnp.float32))
         + d_skip.astype(jnp.float32) * uf)
    return y.astype(u.dtype)


def hybrid_mixer(xn, positions, w_in, sinks, a_re, a_im, log_dt, b_re, b_im, c_re, c_im,
                 d_skip, w_glu, b_glu, g_attn_out, g_ssm_out, w_o):
    B, L = xn.shape[0], xn.shape[1]
    proj = jnp.einsum('bld,de->ble', xn, w_in)
    q, k, v, u = jnp.split(proj, [D_ATTN, D_ATTN + D_KV, D_ATTN + 2 * D_KV], axis=-1)
    q = rotary(q.reshape(B, L, N_Q_HEADS, HEAD_DIM), positions)
    k = rotary(k.reshape(B, L, N_KV_HEADS, HEAD_DIM), positions)
    v = v.reshape(B, L, N_KV_HEADS, HEAD_DIM)
    attn = sliding_window_attention(q, k, v, sinks)
    y = s5_ssm(u.reshape(B, L, N_SSM_GROUPS, SSM_GROUP), a_re, a_im, log_dt,
               b_re, b_im, c_re, c_im, d_skip).reshape(B, L, D_SSM)
    z = jax.nn.gelu(y)
    ssm = z * jax.nn.sigmoid(jnp.einsum('blc,ce->ble', z, w_glu) + b_glu)
    mixed = jnp.concatenate([rms_norm(attn, g_attn_out), rms_norm(ssm, g_ssm_out)], axis=-1)
    return jnp.einsum('blc,cd->bld', mixed, w_o)


def swiglu(xn, w_gate, w_up, w_down):
    hid = jax.nn.silu(jnp.einsum('bld,df->blf', xn, w_gate)) * jnp.einsum('bld,df->blf', xn, w_up)
    return jnp.einsum('blf,fd->bld', hid, w_down)


def _fwd_setup_inputs(seed: int = 0) -> dict:
    key = jax.random.key(seed)
    ks = jax.random.split(key, 24)
    f32 = jnp.float32

    def nrm(k, shape, scale):
        return jax.random.normal(k, shape, f32) * scale

    def gain(k, width):
        return 1.0 + nrm(k, (DEPTH, width), 0.05)

    G, N, P = N_SSM_GROUPS, SSM_STATE, SSM_GROUP
    x = nrm(ks[0], (BATCH, SEQ, D_MODEL), 1.0)
    positions = jnp.tile(jnp.arange(SEQ, dtype=jnp.int32)[None, :], (BATCH, 1))
    return {
        'x': x,
        'positions': positions,
        'g_pre_mix': gain(ks[1], D_MODEL),
        'w_in': nrm(ks[2], (DEPTH, D_MODEL, D_IN), D_MODEL ** -0.5),
        'sinks': nrm(ks[3], (DEPTH, N_Q_HEADS), 1.0),
        'a_re': -0.5 + nrm(ks[4], (DEPTH, G, N), 0.01),
        'a_im': math.pi * jnp.arange(N, dtype=f32)[None, None, :] + nrm(ks[5], (DEPTH, G, N), 0.01),
        'log_dt': jax.random.uniform(ks[6], (DEPTH, G), f32, math.log(1e-3), math.log(1e-1)),
        'b_re': nrm(ks[7], (DEPTH, G, N, P), (2 * P) ** -0.5),
        'b_im': nrm(ks[8], (DEPTH, G, N, P), (2 * P) ** -0.5),
        'c_re': nrm(ks[9], (DEPTH, G, P, N), (2 * N) ** -0.5),
        'c_im': nrm(ks[10], (DEPTH, G, P, N), (2 * N) ** -0.5),
        'd_skip': nrm(ks[11], (DEPTH, G, P), 1.0),
        'w_glu': nrm(ks[12], (DEPTH, D_SSM, D_SSM), D_SSM ** -0.5),
        'b_glu': nrm(ks[13], (DEPTH, D_SSM), 0.01),
        'g_attn_out': gain(ks[14], D_ATTN),
        'g_ssm_out': gain(ks[15], D_SSM),
        'w_o': nrm(ks[16], (DEPTH, D_MIX, D_MODEL), D_MIX ** -0.5),
        'g_post_mix': gain(ks[17], D_MODEL),
        'g_pre_ffn': gain(ks[18], D_MODEL),
        'w_gate': nrm(ks[19], (DEPTH, D_MODEL, D_FF), D_MODEL ** -0.5),
        'w_up': nrm(ks[20], (DEPTH, D_MODEL, D_FF), D_MODEL ** -0.5),
        'w_down': nrm(ks[21], (DEPTH, D_FF, D_MODEL), D_FF ** -0.5),
        'g_post_ffn': gain(ks[22], D_MODEL),
    }


def _fwd_reference(x, positions, g_pre_mix, w_in, sinks, a_re, a_im, log_dt, b_re, b_im, c_re, c_im,
              d_skip, w_glu, b_glu, g_attn_out, g_ssm_out, w_o, g_post_mix, g_pre_ffn,
              w_gate, w_up, w_down, g_post_ffn):
    h = x
    for i in range(DEPTH):
        mix = hybrid_mixer(rms_norm(h, g_pre_mix[i]), positions, w_in[i], sinks[i], a_re[i], a_im[i],
                           log_dt[i], b_re[i], b_im[i], c_re[i], c_im[i], d_skip[i], w_glu[i],
                           b_glu[i], g_attn_out[i], g_ssm_out[i], w_o[i])
        h = h + rms_norm(mix, g_post_mix[i])
        ff = swiglu(rms_norm(h, g_pre_ffn[i]), w_gate[i], w_up[i], w_down[i])
        h = h + rms_norm(ff, g_post_ffn[i])
    return h


import jax as _jax
import jax.numpy as _jnp

TWIN_FORMAT = 'train_step'
FWD_PARAMS = ['x', 'positions', 'g_pre_mix', 'w_in', 'sinks', 'a_re', 'a_im', 'log_dt', 'b_re', 'b_im', 'c_re', 'c_im', 'd_skip', 'w_glu', 'b_glu', 'g_attn_out', 'g_ssm_out', 'w_o', 'g_post_mix', 'g_pre_ffn', 'w_gate', 'w_up', 'w_down', 'g_post_ffn']
TWIN_WEIGHTS = ['g_pre_mix', 'w_in', 'sinks', 'a_re', 'a_im', 'log_dt', 'b_re', 'b_im', 'c_re', 'c_im', 'd_skip', 'w_glu', 'b_glu', 'g_attn_out', 'g_ssm_out', 'w_o', 'g_post_mix', 'g_pre_ffn', 'w_gate', 'w_up', 'w_down', 'g_post_ffn']
TWIN_DIFF_INPUT = 'x'
TWIN_INPUTS = ['x', 'positions', 'g_pre_mix', 'w_in', 'sinks', 'a_re', 'a_im', 'log_dt', 'b_re', 'b_im', 'c_re', 'c_im', 'd_skip', 'w_glu', 'b_glu', 'g_attn_out', 'g_ssm_out', 'w_o', 'g_post_mix', 'g_pre_ffn', 'w_gate', 'w_up', 'w_down', 'g_post_ffn', 'loss_target', 'm_g_pre_mix', 'm_w_in', 'm_sinks', 'm_a_re', 'm_a_im', 'm_log_dt', 'm_b_re', 'm_b_im', 'm_c_re', 'm_c_im', 'm_d_skip', 'm_w_glu', 'm_b_glu', 'm_g_attn_out', 'm_g_ssm_out', 'm_w_o', 'm_g_post_mix', 'm_g_pre_ffn', 'm_w_gate', 'm_w_up', 'm_w_down', 'm_g_post_ffn', 'v_g_pre_mix', 'v_w_in', 'v_sinks', 'v_a_re', 'v_a_im', 'v_log_dt', 'v_b_re', 'v_b_im', 'v_c_re', 'v_c_im', 'v_d_skip', 'v_w_glu', 'v_b_glu', 'v_g_attn_out', 'v_g_ssm_out', 'v_w_o', 'v_g_post_mix', 'v_g_pre_ffn', 'v_w_gate', 'v_w_up', 'v_w_down', 'v_g_post_ffn']
TWIN_OUTPUTS = ['loss', 'grad_x', 'grad_g_pre_mix', 'grad_w_in', 'grad_sinks', 'grad_a_re', 'grad_a_im', 'grad_log_dt', 'grad_b_re', 'grad_b_im', 'grad_c_re', 'grad_c_im', 'grad_d_skip', 'grad_w_glu', 'grad_b_glu', 'grad_g_attn_out', 'grad_g_ssm_out', 'grad_w_o', 'grad_g_post_mix', 'grad_g_pre_ffn', 'grad_w_gate', 'grad_w_up', 'grad_w_down', 'grad_g_post_ffn', 'delta_g_pre_mix', 'delta_w_in', 'delta_sinks', 'delta_a_re', 'delta_a_im', 'delta_log_dt', 'delta_b_re', 'delta_b_im', 'delta_c_re', 'delta_c_im', 'delta_d_skip', 'delta_w_glu', 'delta_b_glu', 'delta_g_attn_out', 'delta_g_ssm_out', 'delta_w_o', 'delta_g_post_mix', 'delta_g_pre_ffn', 'delta_w_gate', 'delta_w_up', 'delta_w_down', 'delta_g_post_ffn', 'new_m_g_pre_mix', 'new_m_w_in', 'new_m_sinks', 'new_m_a_re', 'new_m_a_im', 'new_m_log_dt', 'new_m_b_re', 'new_m_b_im', 'new_m_c_re', 'new_m_c_im', 'new_m_d_skip', 'new_m_w_glu', 'new_m_b_glu', 'new_m_g_attn_out', 'new_m_g_ssm_out', 'new_m_w_o', 'new_m_g_post_mix', 'new_m_g_pre_ffn', 'new_m_w_gate', 'new_m_w_up', 'new_m_w_down', 'new_m_g_post_ffn', 'new_v_g_pre_mix', 'new_v_w_in', 'new_v_sinks', 'new_v_a_re', 'new_v_a_im', 'new_v_log_dt', 'new_v_b_re', 'new_v_b_im', 'new_v_c_re', 'new_v_c_im', 'new_v_d_skip', 'new_v_w_glu', 'new_v_b_glu', 'new_v_g_attn_out', 'new_v_g_ssm_out', 'new_v_w_o', 'new_v_g_post_mix', 'new_v_g_pre_ffn', 'new_v_w_gate', 'new_v_w_up', 'new_v_w_down', 'new_v_g_post_ffn']
TWIN_LEAF_KINDS = {'loss': 'loss', 'grad_x': 'grad_x', 'grad_g_pre_mix': 'grad_w', 'grad_w_in': 'grad_w', 'grad_sinks': 'grad_w', 'grad_a_re': 'grad_w', 'grad_a_im': 'grad_w', 'grad_log_dt': 'grad_w', 'grad_b_re': 'grad_w', 'grad_b_im': 'grad_w', 'grad_c_re': 'grad_w', 'grad_c_im': 'grad_w', 'grad_d_skip': 'grad_w', 'grad_w_glu': 'grad_w', 'grad_b_glu': 'grad_w', 'grad_g_attn_out': 'grad_w', 'grad_g_ssm_out': 'grad_w', 'grad_w_o': 'grad_w', 'grad_g_post_mix': 'grad_w', 'grad_g_pre_ffn': 'grad_w', 'grad_w_gate': 'grad_w', 'grad_w_up': 'grad_w', 'grad_w_down': 'grad_w', 'grad_g_post_ffn': 'grad_w', 'delta_g_pre_mix': 'delta_w', 'delta_w_in': 'delta_w', 'delta_sinks': 'delta_w', 'delta_a_re': 'delta_w', 'delta_a_im': 'delta_w', 'delta_log_dt': 'delta_w', 'delta_b_re': 'delta_w', 'delta_b_im': 'delta_w', 'delta_c_re': 'delta_w', 'delta_c_im': 'delta_w', 'delta_d_skip': 'delta_w', 'delta_w_glu': 'delta_w', 'delta_b_glu': 'delta_w', 'delta_g_attn_out': 'delta_w', 'delta_g_ssm_out': 'delta_w', 'delta_w_o': 'delta_w', 'delta_g_post_mix': 'delta_w', 'delta_g_pre_ffn': 'delta_w', 'delta_w_gate': 'delta_w', 'delta_w_up': 'delta_w', 'delta_w_down': 'delta_w', 'delta_g_post_ffn': 'delta_w', 'new_m_g_pre_mix': 'new_m', 'new_m_w_in': 'new_m', 'new_m_sinks': 'new_m', 'new_m_a_re': 'new_m', 'new_m_a_im': 'new_m', 'new_m_log_dt': 'new_m', 'new_m_b_re': 'new_m', 'new_m_b_im': 'new_m', 'new_m_c_re': 'new_m', 'new_m_c_im': 'new_m', 'new_m_d_skip': 'new_m', 'new_m_w_glu': 'new_m', 'new_m_b_glu': 'new_m', 'new_m_g_attn_out': 'new_m', 'new_m_g_ssm_out': 'new_m', 'new_m_w_o': 'new_m', 'new_m_g_post_mix': 'new_m', 'new_m_g_pre_ffn': 'new_m', 'new_m_w_gate': 'new_m', 'new_m_w_up': 'new_m', 'new_m_w_down': 'new_m', 'new_m_g_post_ffn': 'new_m', 'new_v_g_pre_mix': 'new_v', 'new_v_w_in': 'new_v', 'new_v_sinks': 'new_v', 'new_v_a_re': 'new_v', 'new_v_a_im': 'new_v', 'new_v_log_dt': 'new_v', 'new_v_b_re': 'new_v', 'new_v_b_im': 'new_v', 'new_v_c_re': 'new_v', 'new_v_c_im': 'new_v', 'new_v_d_skip': 'new_v', 'new_v_w_glu': 'new_v', 'new_v_b_glu': 'new_v', 'new_v_g_attn_out': 'new_v', 'new_v_g_ssm_out': 'new_v', 'new_v_w_o': 'new_v', 'new_v_g_post_mix': 'new_v', 'new_v_g_pre_ffn': 'new_v', 'new_v_w_gate': 'new_v', 'new_v_w_up': 'new_v', 'new_v_w_down': 'new_v', 'new_v_g_post_ffn': 'new_v'}


def _forward(args):
    return _fwd_reference(*[args[k] for k in FWD_PARAMS])


def _output_shape():
    out = _jax.eval_shape(lambda: _forward(_fwd_setup_inputs(0)))
    return out.shape, out.dtype

N_MICROBATCH = 1
ADAM_LR = 0.001
ADAM_B1 = 0.9
ADAM_B2 = 0.999
ADAM_EPS = 1e-08
ADAM_WD = 0.01
ADAM_STEP = 10
PER_EXAMPLE_BATCH_AXIS = {'x': 0, 'positions': 0, 'loss_target': 0}
SHARED_INPUTS = []
_WEIGHT_DTYPES = {'g_pre_mix': _jnp.float32, 'w_in': _jnp.float32, 'sinks': _jnp.float32, 'a_re': _jnp.float32, 'a_im': _jnp.float32, 'log_dt': _jnp.float32, 'b_re': _jnp.float32, 'b_im': _jnp.float32, 'c_re': _jnp.float32, 'c_im': _jnp.float32, 'd_skip': _jnp.float32, 'w_glu': _jnp.float32, 'b_glu': _jnp.float32, 'g_attn_out': _jnp.float32, 'g_ssm_out': _jnp.float32, 'w_o': _jnp.float32, 'g_post_mix': _jnp.float32, 'g_pre_ffn': _jnp.float32, 'w_gate': _jnp.float32, 'w_up': _jnp.float32, 'w_down': _jnp.float32, 'g_post_ffn': _jnp.float32}
MOMENT_SCALE = {'g_pre_mix': 2.570649e-01, 'w_in': 2.213892e-01, 'sinks': 6.664091e-02, 'a_re': 7.498875e-03, 'a_im': 8.400150e-03, 'log_dt': 5.261537e+00, 'b_re': 5.555744e-03, 'b_im': 5.551940e-03, 'c_re': 1.113541e-02, 'c_im': 1.103888e-02, 'd_skip': 5.792262e-01, 'w_glu': 9.407759e-02, 'b_glu': 2.933977e-01, 'g_attn_out': 2.028388e-01, 'g_ssm_out': 6.830880e-01, 'w_o': 3.678367e-01, 'g_post_mix': 8.076889e+00, 'g_pre_ffn': 3.011886e-01, 'w_gate': 1.009429e-01, 'w_up': 1.530751e-01, 'w_down': 2.539191e-01, 'g_post_ffn': 8.007931e+00}


def _to_microbatches(a, axis):
    t = _jnp.moveaxis(a, axis, 0)
    t = t.reshape((N_MICROBATCH, t.shape[0] // N_MICROBATCH) + t.shape[1:])
    return _jnp.moveaxis(t, 1, axis + 1)


def setup_inputs(seed: int = 0) -> dict:
    inp = _fwd_setup_inputs(seed)
    key = _jax.random.fold_in(_jax.random.key(seed), 7919)
    shape, _ = _output_shape()
    out = dict(inp)
    out["loss_target"] = _jax.random.normal(_jax.random.fold_in(key, 0), shape, _jnp.float32)
    for i, name in enumerate(TWIN_WEIGHTS):
        w = inp[name].astype(_jnp.float32)
        if MOMENT_SCALE is None:
            s = _jnp.sqrt(_jnp.mean(_jnp.square(w)) + 1e-30)
        else:
            s = MOMENT_SCALE[name]
        km, kv = _jax.random.split(_jax.random.fold_in(key, i + 1))
        out[name] = w
        out["m_" + name] = s * _jax.random.normal(km, w.shape, _jnp.float32)
        out["v_" + name] = (s * s) * _jax.random.uniform(kv, w.shape, _jnp.float32, 0.5, 1.5)
    if N_MICROBATCH > 1:
        for name, axis in PER_EXAMPLE_BATCH_AXIS.items():
            out[name] = _to_microbatches(out[name], axis)
    return {'x': out['x'], 'positions': out['positions'], 'g_pre_mix': out['g_pre_mix'], 'w_in': out['w_in'], 'sinks': out['sinks'], 'a_re': out['a_re'], 'a_im': out['a_im'], 'log_dt': out['log_dt'], 'b_re': out['b_re'], 'b_im': out['b_im'], 'c_re': out['c_re'], 'c_im': out['c_im'], 'd_skip': out['d_skip'], 'w_glu': out['w_glu'], 'b_glu': out['b_glu'], 'g_attn_out': out['g_attn_out'], 'g_ssm_out': out['g_ssm_out'], 'w_o': out['w_o'], 'g_post_mix': out['g_post_mix'], 'g_pre_ffn': out['g_pre_ffn'], 'w_gate': out['w_gate'], 'w_up': out['w_up'], 'w_down': out['w_down'], 'g_post_ffn': out['g_post_ffn'], 'loss_target': out['loss_target'], 'm_g_pre_mix': out['m_g_pre_mix'], 'm_w_in': out['m_w_in'], 'm_sinks': out['m_sinks'], 'm_a_re': out['m_a_re'], 'm_a_im': out['m_a_im'], 'm_log_dt': out['m_log_dt'], 'm_b_re': out['m_b_re'], 'm_b_im': out['m_b_im'], 'm_c_re': out['m_c_re'], 'm_c_im': out['m_c_im'], 'm_d_skip': out['m_d_skip'], 'm_w_glu': out['m_w_glu'], 'm_b_glu': out['m_b_glu'], 'm_g_attn_out': out['m_g_attn_out'], 'm_g_ssm_out': out['m_g_ssm_out'], 'm_w_o': out['m_w_o'], 'm_g_post_mix': out['m_g_post_mix'], 'm_g_pre_ffn': out['m_g_pre_ffn'], 'm_w_gate': out['m_w_gate'], 'm_w_up': out['m_w_up'], 'm_w_down': out['m_w_down'], 'm_g_post_ffn': out['m_g_post_ffn'], 'v_g_pre_mix': out['v_g_pre_mix'], 'v_w_in': out['v_w_in'], 'v_sinks': out['v_sinks'], 'v_a_re': out['v_a_re'], 'v_a_im': out['v_a_im'], 'v_log_dt': out['v_log_dt'], 'v_b_re': out['v_b_re'], 'v_b_im': out['v_b_im'], 'v_c_re': out['v_c_re'], 'v_c_im': out['v_c_im'], 'v_d_skip': out['v_d_skip'], 'v_w_glu': out['v_w_glu'], 'v_b_glu': out['v_b_glu'], 'v_g_attn_out': out['v_g_attn_out'], 'v_g_ssm_out': out['v_g_ssm_out'], 'v_w_o': out['v_w_o'], 'v_g_post_mix': out['v_g_post_mix'], 'v_g_pre_ffn': out['v_g_pre_ffn'], 'v_w_gate': out['v_w_gate'], 'v_w_up': out['v_w_up'], 'v_w_down': out['v_w_down'], 'v_g_post_ffn': out['v_g_post_ffn']}


def _loss(weights, diff, rest, loss_target):
    with _jax.named_scope("forward"):
        args = {**rest, TWIN_DIFF_INPUT: diff, **{k: w.astype(_WEIGHT_DTYPES[k]) for k, w in weights.items()}}
        y = _forward(args)
    with _jax.named_scope("loss_head"):
        err = _jnp.square(y.astype(_jnp.float32) - loss_target)
        return 0.5 * _jnp.sum(_jnp.mean(err, axis=-1)) if err.ndim else 0.5 * err


def _adamw(w, g, m, v):
    m = ADAM_B1 * m + (1.0 - ADAM_B1) * g
    v = ADAM_B2 * v + (1.0 - ADAM_B2) * _jnp.square(g)
    m_hat = m / (1.0 - ADAM_B1 ** ADAM_STEP)
    v_hat = v / (1.0 - ADAM_B2 ** ADAM_STEP)
    delta = -ADAM_LR * (m_hat / (_jnp.sqrt(v_hat) + ADAM_EPS) + ADAM_WD * w)
    return delta, m, v


def reference(x, positions, g_pre_mix, w_in, sinks, a_re, a_im, log_dt, b_re, b_im, c_re, c_im, d_skip, w_glu, b_glu, g_attn_out, g_ssm_out, w_o, g_post_mix, g_pre_ffn, w_gate, w_up, w_down, g_post_ffn, loss_target, m_g_pre_mix, m_w_in, m_sinks, m_a_re, m_a_im, m_log_dt, m_b_re, m_b_im, m_c_re, m_c_im, m_d_skip, m_w_glu, m_b_glu, m_g_attn_out, m_g_ssm_out, m_w_o, m_g_post_mix, m_g_pre_ffn, m_w_gate, m_w_up, m_w_down, m_g_post_ffn, v_g_pre_mix, v_w_in, v_sinks, v_a_re, v_a_im, v_log_dt, v_b_re, v_b_im, v_c_re, v_c_im, v_d_skip, v_w_glu, v_b_glu, v_g_attn_out, v_g_ssm_out, v_w_o, v_g_post_mix, v_g_pre_ffn, v_w_gate, v_w_up, v_w_down, v_g_post_ffn):
    given = dict(x=x, positions=positions, g_pre_mix=g_pre_mix, w_in=w_in, sinks=sinks, a_re=a_re, a_im=a_im, log_dt=log_dt, b_re=b_re, b_im=b_im, c_re=c_re, c_im=c_im, d_skip=d_skip, w_glu=w_glu, b_glu=b_glu, g_attn_out=g_attn_out, g_ssm_out=g_ssm_out, w_o=w_o, g_post_mix=g_post_mix, g_pre_ffn=g_pre_ffn, w_gate=w_gate, w_up=w_up, w_down=w_down, g_post_ffn=g_post_ffn, loss_target=loss_target, m_g_pre_mix=m_g_pre_mix, m_w_in=m_w_in, m_sinks=m_sinks, m_a_re=m_a_re, m_a_im=m_a_im, m_log_dt=m_log_dt, m_b_re=m_b_re, m_b_im=m_b_im, m_c_re=m_c_re, m_c_im=m_c_im, m_d_skip=m_d_skip, m_w_glu=m_w_glu, m_b_glu=m_b_glu, m_g_attn_out=m_g_attn_out, m_g_ssm_out=m_g_ssm_out, m_w_o=m_w_o, m_g_post_mix=m_g_post_mix, m_g_pre_ffn=m_g_pre_ffn, m_w_gate=m_w_gate, m_w_up=m_w_up, m_w_down=m_w_down, m_g_post_ffn=m_g_post_ffn, v_g_pre_mix=v_g_pre_mix, v_w_in=v_w_in, v_sinks=v_sinks, v_a_re=v_a_re, v_a_im=v_a_im, v_log_dt=v_log_dt, v_b_re=v_b_re, v_b_im=v_b_im, v_c_re=v_c_re, v_c_im=v_c_im, v_d_skip=v_d_skip, v_w_glu=v_w_glu, v_b_glu=v_b_glu, v_g_attn_out=v_g_attn_out, v_g_ssm_out=v_g_ssm_out, v_w_o=v_w_o, v_g_post_mix=v_g_post_mix, v_g_pre_ffn=v_g_pre_ffn, v_w_gate=v_w_gate, v_w_up=v_w_up, v_w_down=v_w_down, v_g_post_ffn=v_g_post_ffn)
    weights = {n: given[n] for n in TWIN_WEIGHTS}
    shared = {n: given[n] for n in SHARED_INPUTS}
    per_example = {n: given[n] for n in ['x', 'positions']}
    grad_fn = _jax.value_and_grad(_loss, argnums=(0, 1))

    def one_microbatch(ex, loss_target):
        ex = dict(ex)
        diff = ex.pop(TWIN_DIFF_INPUT)
        return grad_fn(weights, diff, {**shared, **ex}, loss_target)

    if N_MICROBATCH == 1:
        loss, (grad_w, grad_x) = one_microbatch(per_example, given["loss_target"])
    else:
        def body(carry, xs):
            loss_sum, grad_sum = carry
            l_k, (gw_k, gx_k) = one_microbatch(xs[0], xs[1])
            with _jax.named_scope("update"):
                return (loss_sum + l_k, _jax.tree.map(_jnp.add, grad_sum, gw_k)), gx_k

        init = (_jnp.zeros((), _jnp.float32), _jax.tree.map(_jnp.zeros_like, weights))
        (loss, grad_w), grad_x = _jax.lax.scan(body, init, (per_example, given["loss_target"]))
    with _jax.named_scope("update"):
        delta_w, new_m, new_v = {}, {}, {}
        for n in TWIN_WEIGHTS:
            delta_w[n], new_m[n], new_v[n] = _adamw(weights[n], grad_w[n], given["m_" + n], given["v_" + n])
    return (loss, grad_x, *[grad_w[n] for n in TWIN_WEIGHTS], *[delta_w[n] for n in TWIN_WEIGHTS],
            *[new_m[n] for n in TWIN_WEIGHTS], *[new_v[n] for n in TWIN_WEIGHTS])
```

```python
import math

import jax
import jax.numpy as jnp
from jax import lax
from jax.experimental import pallas as pl
from jax.experimental.pallas import tpu as pltpu

F32 = jnp.float32
BF16 = jnp.bfloat16

HEAD_DIM = 64
N_KV_HEADS = 4
D_KV = N_KV_HEADS * HEAD_DIM
WINDOW = 128
BLOCK = 128
ROPE_THETA = 10000.0
SSM_GROUP = 16
SSM_STATE = 64
GROUPS_PER_BLOCK = 8
SSM_CH_BLOCK = GROUPS_PER_BLOCK * SSM_GROUP
SSM_ST_BLOCK = GROUPS_PER_BLOCK * SSM_STATE
RMS_EPS = 1e-6
N_DEV = 8
LANES = 128
SUBLANES = 8
MASKED = -1e30

ADAM_LR = 0.001
ADAM_B1 = 0.9
ADAM_B2 = 0.999
ADAM_EPS = 1e-08
ADAM_WD = 0.01
ADAM_STEP = 10

VMEM_LIMIT_BYTES = 56 * 1024 * 1024


def _call(body, *, name, out_shape, in_specs, out_specs, grid=(), scratch_shapes=(), semantics=None):
    params = dict(vmem_limit_bytes=VMEM_LIMIT_BYTES)
    if semantics is not None:
        params["dimension_semantics"] = semantics
    return pl.pallas_call(body, name=name, grid=grid, in_specs=in_specs, out_specs=out_specs, out_shape=out_shape,
                          scratch_shapes=scratch_shapes, compiler_params=pltpu.CompilerParams(**params))


def _sds(shape, dtype):
    return jax.ShapeDtypeStruct(tuple(shape), dtype)


def _dot(a, b, ca, cb):
    return lax.dot_general(a, b, (((ca,), (cb,)), ((), ())), preferred_element_type=F32)


def _rms(x):
    r = lax.rsqrt(jnp.mean(x * x, axis=-1, keepdims=True) + RMS_EPS)
    return x * r, r


def _rms_bwd(x, g, dy):
    xh, r = _rms(x)
    dxh = dy * g
    dx = r * (dxh - xh * jnp.mean(dxh * xh, axis=-1, keepdims=True))
    return dx, jnp.sum(dy * xh, axis=0, keepdims=True)


def _sigmoid(x):
    return 1.0 / (1.0 + jnp.exp(-x))


_GELU_C = math.sqrt(2.0 / math.pi)
_GELU_A = 0.044715


def _gelu(y):
    t = jnp.tanh(_GELU_C * (y + _GELU_A * y * y * y))
    return 0.5 * y * (1.0 + t)


def _gelu_grad(y):
    t = jnp.tanh(_GELU_C * (y + _GELU_A * y * y * y))
    return 0.5 * (1.0 + t) + 0.5 * y * (1.0 - t * t) * _GELU_C * (1.0 + 3.0 * _GELU_A * y * y)


def _rows(name, fn, row_ins, vec_ins, row_outs, acc_widths, tm):
    rows = row_ins[0].shape[0]
    assert rows % tm == 0, (name, rows, tm)
    n_row, n_vec, n_out, n_acc = len(row_ins), len(vec_ins), len(row_outs), len(acc_widths)

    def body(*refs):
        ins = [r[...] for r in refs[:n_row + n_vec]]
        outs = refs[n_row + n_vec:n_row + n_vec + n_out]
        accs = refs[n_row + n_vec + n_out:]
        row_vals, acc_vals = fn(*ins)
        for o, v in zip(outs, row_vals):
            o[...] = v.astype(o.dtype)
        if n_acc:
            @pl.when(pl.program_id(0) == 0)
            def _():
                for a in accs:
                    a[...] = jnp.zeros_like(a)
            for a, v in zip(accs, acc_vals):
                a[...] += v

    in_specs = [pl.BlockSpec((tm, a.shape[1]), lambda i: (i, 0)) for a in row_ins]
    in_specs += [pl.BlockSpec(v.shape, lambda i: (0, 0)) for v in vec_ins]
    out_specs = [pl.BlockSpec((tm, w), lambda i: (i, 0)) for w, _ in row_outs]
    out_specs += [pl.BlockSpec((1, w), lambda i: (0, 0)) for w in acc_widths]
    out_shape = [_sds((rows, w), dt) for w, dt in row_outs] + [_sds((1, w), F32) for w in acc_widths]
    return _call(body, name=name, grid=(rows // tm,), in_specs=in_specs, out_specs=out_specs, out_shape=out_shape,
                 semantics=("arbitrary",) if n_acc else ("parallel",))(*row_ins, *vec_ins)


def _matmul(name, operands, in_specs, product, grid, out_shape, out_spec, acc_shape):
    nk = grid[-1]
    n_in = len(operands)

    def body(*refs):
        ins = [r[...] for r in refs[:n_in]]
        o_ref = refs[n_in]
        if nk == 1:
            o_ref[...] = product(*ins).astype(o_ref.dtype)
            return
        acc = refs[n_in + 1]
        k = pl.program_id(len(grid) - 1)

        @pl.when(k == 0)
        def _():
            acc[...] = jnp.zeros_like(acc)

        acc[...] += product(*ins)

        @pl.when(k == nk - 1)
        def _():
            o_ref[...] = acc[...].astype(o_ref.dtype)

    return _call(body, name=name, grid=grid, in_specs=in_specs, out_specs=out_spec, out_shape=out_shape,
                 scratch_shapes=[] if nk == 1 else [pltpu.VMEM(acc_shape, F32)],
                 semantics=("parallel",) * (len(grid) - 1) + ("arbitrary",))(*operands)


def _mm_nn(name, a, b, out_dtype, tm=512, tn=None):
    m, k = a.shape
    n = b.shape[1]
    tm, tn = min(tm, m), n if tn is None else tn
    return _matmul(name, [a, b],
                   [pl.BlockSpec((tm, k), lambda i, j, s: (i, 0)), pl.BlockSpec((k, tn), lambda i, j, s: (0, j))],
                   lambda x, y: _dot(x, y, 1, 0), (m // tm, n // tn, 1), _sds((m, n), out_dtype),
                   pl.BlockSpec((tm, tn), lambda i, j, s: (i, j)), (tm, tn))


def _mm_nt(name, a, b, out_dtype, tm=512, tn=None):
    m, k = a.shape
    n = b.shape[0]
    tm, tn = min(tm, m), n if tn is None else tn
    return _matmul(name, [a, b],
                   [pl.BlockSpec((tm, k), lambda i, j, s: (i, 0)), pl.BlockSpec((tn, k), lambda i, j, s: (j, 0))],
                   lambda x, y: _dot(x, y, 1, 1), (m // tm, n // tn, 1), _sds((m, n), out_dtype),
                   pl.BlockSpec((tm, tn), lambda i, j, s: (i, j)), (tm, tn))


def _mm_tn(name, a, b, out_dtype, tm=512, tn=None, tk=512):
    k, m = a.shape
    n = b.shape[1]
    tm, tk, tn = min(tm, m), min(tk, k), n if tn is None else tn
    return _matmul(name, [a, b],
                   [pl.BlockSpec((tk, tm), lambda i, j, s: (s, i)), pl.BlockSpec((tk, tn), lambda i, j, s: (s, j))],
                   lambda x, y: _dot(x, y, 0, 0), (m // tm, n // tn, k // tk), _sds((m, n), out_dtype),
                   pl.BlockSpec((tm, tn), lambda i, j, s: (i, j)), (tm, tn))


def _mm_nn_slots(name, a, b, out_dtype, tm=512):
    m, k = a.shape
    s_, _, n = b.shape
    tm = min(tm, m)
    return _matmul(name, [a, b],
                   [pl.BlockSpec((tm, k), lambda s, i, z: (i, 0)), pl.BlockSpec((None, k, n), lambda s, i, z: (s, 0, 0))],
                   lambda x, y: _dot(x, y, 1, 0), (s_, m // tm, 1), _sds((s_, m, n), out_dtype),
                   pl.BlockSpec((None, tm, n), lambda s, i, z: (s, i, 0)), (tm, n))


def _mm_nt_slots(name, a, b, out_dtype, tm=512):
    m, k = a.shape
    s_, n, _ = b.shape
    tm = min(tm, m)
    return _matmul(name, [a, b],
                   [pl.BlockSpec((tm, k), lambda s, i, z: (i, 0)), pl.BlockSpec((None, n, k), lambda s, i, z: (s, 0, 0))],
                   lambda x, y: _dot(x, y, 1, 1), (s_, m // tm, 1), _sds((s_, m, n), out_dtype),
                   pl.BlockSpec((None, tm, n), lambda s, i, z: (s, i, 0)), (tm, n))


def _mm_contract_slots(name, pairs, out_dtype, tm=512, tn=2048):
    s_, m, k = pairs[0][0].shape
    n = pairs[0][1].shape[2]
    tm, tn = min(tm, m), min(tn, n)
    ops, specs = [], []
    for a, b in pairs:
        ops += [a, b]
        specs += [pl.BlockSpec((None, tm, k), lambda i, j, s: (s, i, 0)), pl.BlockSpec((None, k, tn), lambda i, j, s: (s, 0, j))]

    def product(*t):
        return sum(_dot(t[2 * p], t[2 * p + 1], 1, 0) for p in range(len(pairs)))

    return _matmul(name, ops, specs, product, (m // tm, n // tn, s_), _sds((m, n), out_dtype),
                   pl.BlockSpec((tm, tn), lambda i, j, s: (i, j)), (tm, tn))


def _mm_contract_slots_nt(name, pairs, out_dtype, tm=512, tn=2048):
    s_, m, k = pairs[0][0].shape
    n = pairs[0][1].shape[1]
    tm, tn = min(tm, m), min(tn, n)
    ops, specs = [], []
    for a, b in pairs:
        ops += [a, b]
        specs += [pl.BlockSpec((None, tm, k), lambda i, j, s: (s, i, 0)), pl.BlockSpec((None, tn, k), lambda i, j, s: (s, j, 0))]

    def product(*t):
        return sum(_dot(t[2 * p], t[2 * p + 1], 1, 1) for p in range(len(pairs)))

    return _matmul(name, ops, specs, product, (m // tm, n // tn, s_), _sds((m, n), out_dtype),
                   pl.BlockSpec((tm, tn), lambda i, j, s: (i, j)), (tm, tn))


def _mm_tn_slots(name, a, b, out_dtype, tm=2048, tk=512):
    k, m = a.shape
    s_, _, n = b.shape
    tm, tk = min(tm, m), min(tk, k)
    return _matmul(name, [a, b],
                   [pl.BlockSpec((tk, tm), lambda s, i, z: (z, i)), pl.BlockSpec((None, tk, n), lambda s, i, z: (s, z, 0))],
                   lambda x, y: _dot(x, y, 0, 0), (s_, m // tm, k // tk), _sds((s_, m, n), out_dtype),
                   pl.BlockSpec((None, tm, n), lambda s, i, z: (s, i, 0)), (tm, n))


def _mm_slots_tn(name, a, b, out_dtype, tn=2048, tk=512):
    s_, k, m = a.shape
    n = b.shape[1]
    tn, tk = min(tn, n), min(tk, k)
    return _matmul(name, [a, b],
                   [pl.BlockSpec((None, tk, m), lambda s, j, z: (s, z, 0)), pl.BlockSpec((tk, tn), lambda s, j, z: (z, j))],
                   lambda x, y: _dot(x, y, 0, 0), (s_, n // tn, k // tk), _sds((s_, m, n), out_dtype),
                   pl.BlockSpec((None, m, tn), lambda s, j, z: (s, 0, j)), (m, tn))


def _exchange(name, arrays, scatter):
    n = len(arrays)
    peers = N_DEV - 1

    def body(*refs):
        ins, outs = refs[:n], refs[n:2 * n]
        send_sems, recv_sems, local_sems = refs[2 * n:]
        x, y, c = lax.axis_index("x"), lax.axis_index("y"), lax.axis_index("c")
        me = 4 * x + 2 * y + c

        def flip(v, bit):
            return 1 - v if bit else v

        copies = []
        for a in range(n):
            local = pltpu.make_async_copy(ins[a].at[me] if scatter else ins[a], outs[a].at[me], local_sems.at[a])
            local.start()
            copies.append(local)
        remote = []
        for r in range(1, N_DEV):
            px, py, pc = flip(x, r & 4), flip(y, r & 2), flip(c, r & 1)
            peer = 4 * px + 2 * py + pc
            for a in range(n):
                s = a * peers + r - 1
                send = pltpu.make_async_remote_copy(
                    src_ref=ins[a].at[peer] if scatter else ins[a], dst_ref=outs[a].at[me],
                    send_sem=send_sems.at[s], recv_sem=recv_sems.at[s],
                    device_id=(px, py, pc), device_id_type=pl.DeviceIdType.MESH)
                send.start()
                arrival = pltpu.make_async_remote_copy(
                    src_ref=ins[a].at[peer] if scatter else ins[a], dst_ref=outs[a].at[peer],
                    send_sem=send_sems.at[s], recv_sem=recv_sems.at[s],
                    device_id=(px, py, pc), device_id_type=pl.DeviceIdType.MESH)
                remote.append((send, arrival))
        for send, arrival in remote:
            arrival.wait_recv()
        for send, arrival in remote:
            send.wait_send()
        for local in copies:
            local.wait()

    any_spec = pl.BlockSpec(memory_space=pl.ANY)
    out_shape = [_sds(a.shape if scatter else (N_DEV,) + a.shape, a.dtype) for a in arrays]
    return pl.pallas_call(
        body, name=name, in_specs=[any_spec] * n, out_specs=[any_spec] * n, out_shape=out_shape,
        scratch_shapes=[pltpu.SemaphoreType.DMA((n * peers,)), pltpu.SemaphoreType.DMA((n * peers,)),
                        pltpu.SemaphoreType.DMA((n,))],
    )(*arrays)


def _rope_tables(pos_col):
    t = pos_col.shape[0]
    half = HEAD_DIM // 2
    inv_freq = ROPE_THETA ** (-jnp.arange(half, dtype=F32) / half)
    inv_row = jnp.tile(inv_freq, LANES // half)[None, :]

    def body(pos_ref, inv_ref, cos_ref, sin_ref):
        ang = pos_ref[...] * inv_ref[...]
        cos_ref[...] = jnp.cos(ang)
        sin_ref[...] = jnp.sin(ang)

    tm = min(t, 512)
    return _call(body, name="rope_tables", grid=(t // tm,),
                 in_specs=[pl.BlockSpec((tm, 1), lambda i: (i, 0)), pl.BlockSpec((1, LANES), lambda i: (0, 0))],
                 out_specs=[pl.BlockSpec((tm, LANES), lambda i: (i, 0))] * 2,
                 out_shape=[_sds((t, LANES), F32)] * 2, semantics=("parallel",))(pos_col, inv_row)


def _rot_half(x):
    lane = lax.broadcasted_iota(jnp.int32, x.shape, 1)
    low = (lane % HEAD_DIM) < HEAD_DIM // 2
    return jnp.where(low, -pltpu.roll(x, LANES - HEAD_DIM // 2, 1), pltpu.roll(x, HEAD_DIM // 2, 1))


def _rope(x, cos, sin):
    return x * cos + _rot_half(x) * sin


def _unrope(d, cos, sin):
    return d * cos - _rot_half(d) * sin


def _band_mask(first_block):
    r = lax.broadcasted_iota(jnp.int32, (BLOCK, 2 * BLOCK), 0)
    c = lax.broadcasted_iota(jnp.int32, (BLOCK, 2 * BLOCK), 1)
    diff = r - c + BLOCK
    return (diff >= 0) & (diff < WINDOW) & ((c >= BLOCK) | jnp.logical_not(first_block))


def _attn_specs(t, d_attn, d_in):
    kb, vb = d_attn // D_KV, d_attn // D_KV + 1
    prev = lambda i: jnp.maximum(i - 1, 0)
    return [
        pl.BlockSpec((BLOCK, d_attn), lambda i: (i, 0)),
        pl.BlockSpec((BLOCK, D_KV), lambda i: (i, kb)),
        pl.BlockSpec((BLOCK, D_KV), lambda i: (i, vb)),
        pl.BlockSpec((BLOCK, D_KV), lambda i: (prev(i), kb)),
        pl.BlockSpec((BLOCK, D_KV), lambda i: (prev(i), vb)),
        pl.BlockSpec((BLOCK, LANES), lambda i: (i, 0)),
        pl.BlockSpec((BLOCK, LANES), lambda i: (i, 0)),
        pl.BlockSpec((BLOCK, LANES), lambda i: (prev(i), 0)),
        pl.BlockSpec((BLOCK, LANES), lambda i: (prev(i), 0)),
        pl.BlockSpec((1, LANES), lambda i: (0, 0)),
    ]


def _head(x, h):
    return x[:, h * HEAD_DIM:(h + 1) * HEAD_DIM]


def _attn_heads(q_ref, kc_ref, vc_ref, kp_ref, vp_ref, cq_ref, sq_ref, cp_ref, sp_ref, d_attn):
    cq, sq, cp, sp = cq_ref[...], sq_ref[...], cp_ref[...], sp_ref[...]
    q_rot = [_rope(q_ref[:, j * LANES:(j + 1) * LANES], cq, sq) for j in range(d_attn // LANES)]
    kc_rot = [_rope(kc_ref[:, j * LANES:(j + 1) * LANES], cq, sq) for j in range(D_KV // LANES)]
    kp_rot = [_rope(kp_ref[:, j * LANES:(j + 1) * LANES], cp, sp) for j in range(D_KV // LANES)]
    per = LANES // HEAD_DIM
    q_heads = [_head(q_rot[h // per], h % per).astype(BF16) for h in range(d_attn // HEAD_DIM)]
    kk = [jnp.concatenate([_head(kp_rot[g // per], g % per), _head(kc_rot[g // per], g % per)], axis=0).astype(BF16)
          for g in range(N_KV_HEADS)]
    vv = [jnp.concatenate([_head(vp_ref[...], g), _head(vc_ref[...], g)], axis=0).astype(BF16) for g in range(N_KV_HEADS)]
    return q_heads, kk, vv


def _softmax_with_sink(q, kk, sink, mask):
    s = _dot(q, kk, 1, 1) * (1.0 / math.sqrt(HEAD_DIM))
    s = jnp.where(mask, s, MASKED)
    m = jnp.maximum(jnp.max(s, axis=-1, keepdims=True), sink)
    p = jnp.exp(s - m)
    e_sink = jnp.exp(sink - m)
    inv = 1.0 / (jnp.sum(p, axis=-1, keepdims=True) + e_sink)
    return p * inv, e_sink * inv


def _attention_fwd(proj, cos, sin, sinks_row, d_attn):
    t, d_in = proj.shape
    n_heads = d_attn // HEAD_DIM
    q_per_kv = n_heads // N_KV_HEADS

    def body(q_ref, kc_ref, vc_ref, kp_ref, vp_ref, cq_ref, sq_ref, cp_ref, sp_ref, sink_ref, o_ref):
        mask = _band_mask(pl.program_id(0) == 0)
        q_heads, kk, vv = _attn_heads(q_ref, kc_ref, vc_ref, kp_ref, vp_ref, cq_ref, sq_ref, cp_ref, sp_ref, d_attn)
        for h in range(n_heads):
            g = h // q_per_kv
            probs, _ = _softmax_with_sink(q_heads[h], kk[g], sink_ref[:, h:h + 1], mask)
            o_ref[:, h * HEAD_DIM:(h + 1) * HEAD_DIM] = _dot(probs.astype(BF16), vv[g], 1, 0)

    return _call(body, name="attention_fwd", grid=(t // BLOCK,), in_specs=_attn_specs(t, d_attn, d_in),
                 out_specs=pl.BlockSpec((BLOCK, d_attn), lambda i: (i, 0)), out_shape=_sds((t, d_attn), F32),
                 semantics=("parallel",))(proj, proj, proj, proj, proj, cos, sin, cos, sin, sinks_row)


def _attention_bwd(proj, cos, sin, sinks_row, d_out, d_attn):
    t, d_in = proj.shape
    n_heads = d_attn // HEAD_DIM
    q_per_kv = n_heads // N_KV_HEADS
    nb = t // BLOCK
    per = LANES // HEAD_DIM

    def body(q_ref, kc_ref, vc_ref, kp_ref, vp_ref, cq_ref, sq_ref, cp_ref, sp_ref, sink_ref, do_ref,
             dq_ref, dk_ref, dv_ref, dsink_ref):
        i = pl.program_id(0)
        mask = _band_mask(i == 0)
        q_heads, kk, vv = _attn_heads(q_ref, kc_ref, vc_ref, kp_ref, vp_ref, cq_ref, sq_ref, cp_ref, sp_ref, d_attn)
        lane = lax.broadcasted_iota(jnp.int32, (1, LANES), 1)
        dsink = jnp.zeros((1, LANES), F32)
        dq_rot, dkk, dvv = [], [], []
        for g in range(N_KV_HEADS):
            dkk_g = jnp.zeros((2 * BLOCK, HEAD_DIM), F32)
            dvv_g = jnp.zeros((2 * BLOCK, HEAD_DIM), F32)
            for h in range(g * q_per_kv, (g + 1) * q_per_kv):
                probs, p_sink = _softmax_with_sink(q_heads[h], kk[g], sink_ref[:, h:h + 1], mask)
                do_h = do_ref[:, h * HEAD_DIM:(h + 1) * HEAD_DIM].astype(BF16)
                dp = _dot(do_h, vv[g], 1, 1)
                delta = jnp.sum(probs * dp, axis=-1, keepdims=True)
                ds = (probs * (dp - delta) * (1.0 / math.sqrt(HEAD_DIM))).astype(BF16)
                dq_rot.append(_dot(ds, kk[g], 1, 0))
                dkk_g += _dot(ds, q_heads[h], 0, 0)
                dvv_g += _dot(probs.astype(BF16), do_h, 0, 0)
                dsink += jnp.where(lane == h, -jnp.sum(p_sink * delta, axis=0, keepdims=True), 0.0)
            dkk.append(dkk_g)
            dvv.append(dvv_g)
        cq, sq, cp, sp = cq_ref[...], sq_ref[...], cp_ref[...], sp_ref[...]
        for j in range(d_attn // LANES):
            d = jnp.concatenate(dq_rot[j * per:(j + 1) * per], axis=1)
            dq_ref[:, j * LANES:(j + 1) * LANES] = _unrope(d, cq, sq)
        for j in range(D_KV // LANES):
            d = jnp.concatenate(dkk[j * per:(j + 1) * per], axis=1)
            dk_ref[0, :, j * LANES:(j + 1) * LANES] = _unrope(d[:BLOCK], cp, sp)
            dk_ref[1, :, j * LANES:(j + 1) * LANES] = _unrope(d[BLOCK:], cq, sq)
            d = jnp.concatenate(dvv[j * per:(j + 1) * per], axis=1)
            dv_ref[0, :, j * LANES:(j + 1) * LANES] = d[:BLOCK]
            dv_ref[1, :, j * LANES:(j + 1) * LANES] = d[BLOCK:]

        @pl.when(i == 0)
        def _():
            dsink_ref[...] = jnp.zeros_like(dsink_ref)

        dsink_ref[...] += dsink

    pair = pl.BlockSpec((2, BLOCK, D_KV), lambda i: (i, 0, 0))
    return _call(body, name="attention_bwd", grid=(nb,),
                 in_specs=_attn_specs(t, d_attn, d_in) + [pl.BlockSpec((BLOCK, d_attn), lambda i: (i, 0))],
                 out_specs=[pl.BlockSpec((BLOCK, d_attn), lambda i: (i, 0)), pair, pair,
                            pl.BlockSpec((1, LANES), lambda i: (0, 0))],
                 out_shape=[_sds((t, d_attn), F32), _sds((2 * nb, BLOCK, D_KV), F32), _sds((2 * nb, BLOCK, D_KV), F32),
                            _sds((1, LANES), F32)],
                 semantics=("arbitrary",))(proj, proj, proj, proj, proj, cos, sin, cos, sin, sinks_row, d_out)


def _assemble_dproj(dq, dk2, dv2, du, d_in):
    t, d_attn = dq.shape
    d_ssm = du.shape[1]
    nb = t // BLOCK

    def body(dq_ref, dk_own, dk_next, dv_own, dv_next, du_ref, o_ref):
        has_next = (pl.program_id(0) < nb - 1).astype(F32)
        o_ref[:, :d_attn] = dq_ref[...].astype(BF16)
        o_ref[:, d_attn:d_attn + D_KV] = (dk_own[...] + has_next * dk_next[...]).astype(BF16)
        o_ref[:, d_attn + D_KV:d_attn + 2 * D_KV] = (dv_own[...] + has_next * dv_next[...]).astype(BF16)
        o_ref[:, d_attn + 2 * D_KV:] = du_ref[...].astype(BF16)

    own = pl.BlockSpec((None, BLOCK, D_KV), lambda i: (2 * i + 1, 0, 0))
    nxt = pl.BlockSpec((None, BLOCK, D_KV), lambda i: (jnp.minimum(2 * i + 2, 2 * nb - 1), 0, 0))
    return _call(body, name="assemble_dproj", grid=(nb,),
                 in_specs=[pl.BlockSpec((BLOCK, d_attn), lambda i: (i, 0)), own, nxt, own, nxt,
                           pl.BlockSpec((BLOCK, d_ssm), lambda i: (i, 0))],
                 out_specs=pl.BlockSpec((BLOCK, d_in), lambda i: (i, 0)), out_shape=_sds((t, d_in), BF16),
                 semantics=("parallel",))(dq, dk2, dk2, dv2, dv2, du)


def _discretise(ar, ai, ldt, br, bi):
    dt = jnp.exp(ldt)
    mag = jnp.exp(ar * dt)
    lam_re = mag * jnp.cos(ai * dt)
    lam_im = mag * jnp.sin(ai * dt)
    den = ar * ar + ai * ai
    nr = lam_re - 1.0
    ni = lam_im
    f_re = (nr * ar + ni * ai) / den
    f_im = (ni * ar - nr * ai) / den
    return lam_re, lam_im, f_re[None] * br - f_im[None] * bi, f_re[None] * bi + f_im[None] * br


def _whole(arrays):
    return [pl.BlockSpec(a.shape, lambda *_, nd=len(a.shape): (0,) * nd) for a in arrays]


def _s5_discretise(ar, ai, ldt, br, bi):
    ins = [ar, ai, ldt, br, bi]

    def body(ar_ref, ai_ref, ldt_ref, br_ref, bi_ref, lr_ref, li_ref, bbr_ref, bbi_ref):
        out = _discretise(ar_ref[...], ai_ref[...], ldt_ref[...], br_ref[...], bi_ref[...])
        for ref, val in zip((lr_ref, li_ref, bbr_ref, bbi_ref), out):
            ref[...] = val

    outs = [_sds(ar.shape, F32), _sds(ar.shape, F32), _sds(br.shape, F32), _sds(br.shape, F32)]
    return _call(body, name="s5_discretise", in_specs=_whole(ins), out_specs=_whole(outs), out_shape=outs)(*ins)


def _s5_discretise_bwd(ar, ai, ldt, br, bi, d_lr, d_li, d_bbr, d_bbi):
    ins = [ar, ai, ldt, br, bi, d_lr, d_li, d_bbr, d_bbi]

    def body(ar_ref, ai_ref, ldt_ref, br_ref, bi_ref, dlr_ref, dli_ref, dbbr_ref, dbbi_ref, *out_refs):
        _, vjp = jax.vjp(_discretise, ar_ref[...], ai_ref[...], ldt_ref[...], br_ref[...], bi_ref[...])
        grads = vjp((dlr_ref[...], dli_ref[...], dbbr_ref[...], dbbi_ref[...]))
        for ref, val in zip(out_refs, grads):
            ref[...] = val

    outs = [_sds(a.shape, F32) for a in (ar, ai, ldt, br, bi)]
    return _call(body, name="s5_discretise_bwd", in_specs=_whole(ins), out_specs=_whole(outs), out_shape=outs)(*ins)


def _cmul(ar, ai, br, bi):
    return ar * br - ai * bi, ar * bi + ai * br


def _power_table(lr, li, reverse):
    pows = [(lr, li)]
    for _ in range(SUBLANES - 1):
        pows.append(_cmul(pows[-1][0], pows[-1][1], lr, li))
    row = lax.broadcasted_iota(jnp.int32, (SUBLANES, lr.shape[1]), 0)
    tr = jnp.zeros((SUBLANES, lr.shape[1]), F32)
    ti = jnp.zeros((SUBLANES, lr.shape[1]), F32)
    for r in range(SUBLANES):
        src = pows[SUBLANES - 1 - r] if reverse else pows[r]
        tr = jnp.where(row == r, src[0], tr)
        ti = jnp.where(row == r, src[1], ti)
    return pows[0], pows[1], pows[3], (tr, ti)


def _scan_tile(xr, xi, steps, table, carry, reverse):
    row = lax.broadcasted_iota(jnp.int32, xr.shape, 0)
    for k, (lr, li) in zip((1, 2, 4), steps):
        if reverse:
            keep = row < SUBLANES - k
            sr = jnp.where(keep, pltpu.roll(xr, SUBLANES - k, 0), 0.0)
            si = jnp.where(keep, pltpu.roll(xi, SUBLANES - k, 0), 0.0)
        else:
            keep = row >= k
            sr = jnp.where(keep, pltpu.roll(xr, k, 0), 0.0)
            si = jnp.where(keep, pltpu.roll(xi, k, 0), 0.0)
        pr, pi = _cmul(lr, li, sr, si)
        xr, xi = xr + pr, xi + pi
    pr, pi = _cmul(table[0], table[1], carry[0], carry[1])
    return xr + pr, xi + pi


def _scan(sr_ref, si_ref, lr, li, reverse, t, per_tile=None):
    l1, l2, l4, table = _power_table(lr, li, reverse)
    n_tiles = t // SUBLANES
    w = lr.shape[1]

    def step(i, carry):
        tile = (n_tiles - 1 - i) if reverse else i
        rows = pl.ds(pl.multiple_of(tile * SUBLANES, SUBLANES), SUBLANES)
        xr, xi = _scan_tile(sr_ref[rows, :], si_ref[rows, :], (l1, l2, l4), table, carry, reverse)
        sr_ref[rows, :] = xr
        si_ref[rows, :] = xi
        if per_tile is not None:
            per_tile(tile, xr, xi)
        edge = 0 if reverse else SUBLANES - 1
        return xr[edge:edge + 1, :], xi[edge:edge + 1, :]

    lax.fori_loop(0, n_tiles, step, (jnp.zeros((1, w), F32), jnp.zeros((1, w), F32)))


_S5_ROWS = 256


def _s5_in_specs(t, d_attn):
    u_block = (d_attn + 2 * D_KV) // SSM_CH_BLOCK
    blk3 = lambda shape: pl.BlockSpec((None,) + shape, lambda j: (j, 0, 0))
    return [
        pl.BlockSpec((t, SSM_CH_BLOCK), lambda j: (0, u_block + j)),
        blk3((SSM_CH_BLOCK, SSM_ST_BLOCK)), blk3((SSM_CH_BLOCK, SSM_ST_BLOCK)),
        blk3((1, SSM_ST_BLOCK)), blk3((1, SSM_ST_BLOCK)),
        blk3((SSM_ST_BLOCK, SSM_CH_BLOCK)), blk3((SSM_ST_BLOCK, SSM_CH_BLOCK)),
        pl.BlockSpec((1, SSM_CH_BLOCK), lambda j: (0, j)),
    ]


def _s5_states(u_ref, bre_ref, bim_ref, lr_ref, li_ref, sr_ref, si_ref, t):
    def fill(i, _):
        rows = pl.ds(pl.multiple_of(i * _S5_ROWS, _S5_ROWS), _S5_ROWS)
        ub = u_ref[rows, :].astype(BF16)
        sr_ref[rows, :] = _dot(ub, bre_ref[...], 1, 0)
        si_ref[rows, :] = _dot(ub, bim_ref[...], 1, 0)
        return 0

    lax.fori_loop(0, t // _S5_ROWS, fill, 0)
    _scan(sr_ref, si_ref, lr_ref[...], li_ref[...], False, t)


def _s5_fwd(proj, mats, dskip_row, d_attn, d_ssm):
    t = proj.shape[0]
    n_blocks = d_ssm // SSM_CH_BLOCK

    def body(u_ref, bre_ref, bim_ref, lr_ref, li_ref, cre_ref, cim_ref, d_ref, y_ref, z_ref, sr_ref, si_ref):
        _s5_states(u_ref, bre_ref, bim_ref, lr_ref, li_ref, sr_ref, si_ref, t)

        def emit(i, _):
            rows = pl.ds(pl.multiple_of(i * _S5_ROWS, _S5_ROWS), _S5_ROWS)
            y = (_dot(sr_ref[rows, :].astype(BF16), cre_ref[...], 1, 0)
                 - _dot(si_ref[rows, :].astype(BF16), cim_ref[...], 1, 0) + d_ref[...] * u_ref[rows, :])
            y_ref[rows, :] = y
            z_ref[rows, :] = _gelu(y).astype(BF16)
            return 0

        lax.fori_loop(0, t // _S5_ROWS, emit, 0)

    col = pl.BlockSpec((t, SSM_CH_BLOCK), lambda j: (0, j))
    return _call(body, name="s5_fwd", grid=(n_blocks,), in_specs=_s5_in_specs(t, d_attn), out_specs=[col, col],
                 out_shape=[_sds((t, d_ssm), F32), _sds((t, d_ssm), BF16)],
                 scratch_shapes=[pltpu.VMEM((t, SSM_ST_BLOCK), F32)] * 2,
                 semantics=("parallel",))(proj, *mats, dskip_row)


def _s5_bwd(proj, mats, dskip_row, y, dz_a, dz_b, d_attn, d_ssm):
    t = proj.shape[0]
    n_blocks = d_ssm // SSM_CH_BLOCK

    def body(u_ref, bre_ref, bim_ref, lr_ref, li_ref, cre_ref, cim_ref, d_ref, y_ref, dza_ref, dzb_ref,
             du_ref, dbre_ref, dbim_ref, dlr_ref, dli_ref, dcre_ref, dcim_ref, dd_ref,
             sr_ref, si_ref, gr_ref, gi_ref, dy_ref, acc_r, acc_i):
        _s5_states(u_ref, bre_ref, bim_ref, lr_ref, li_ref, sr_ref, si_ref, t)
        for ref in (dcre_ref, dcim_ref, dbre_ref, dbim_ref, dd_ref, acc_r, acc_i):
            ref[...] = jnp.zeros_like(ref)

        def through_c(i, _):
            rows = pl.ds(pl.multiple_of(i * _S5_ROWS, _S5_ROWS), _S5_ROWS)
            dy = (dza_ref[rows, :] + dzb_ref[rows, :]) * _gelu_grad(y_ref[rows, :])
            dy_ref[rows, :] = dy
            dd_ref[...] += jnp.sum(dy * u_ref[rows, :], axis=0, keepdims=True)
            dyb = dy.astype(BF16)
            gr_ref[rows, :] = _dot(dyb, cre_ref[...], 1, 1)
            gi_ref[rows, :] = -_dot(dyb, cim_ref[...], 1, 1)
            dcre_ref[...] += _dot(sr_ref[rows, :].astype(BF16), dyb, 0, 0)
            dcim_ref[...] -= _dot(si_ref[rows, :].astype(BF16), dyb, 0, 0)
            return 0

        lax.fori_loop(0, t // _S5_ROWS, through_c, 0)

        def lambda_grad(tile, g_re, g_im):
            rows = pl.ds(pl.multiple_of(tile * SUBLANES, SUBLANES), SUBLANES)
            before = pl.ds(pl.multiple_of(jnp.maximum(tile - 1, 0) * SUBLANES, SUBLANES), SUBLANES)
            row = lax.broadcasted_iota(jnp.int32, g_re.shape, 0)
            live = jnp.where(tile > 0, 1.0, 0.0)
            prev = []
            for ref in (sr_ref, si_ref):
                here = pltpu.roll(ref[rows, :], 1, 0)
                last = pltpu.roll(ref[before, :], 1, 0) * live
                prev.append(jnp.where(row == 0, last, here))
            acc_r[...] += g_re * prev[0] + g_im * prev[1]
            acc_i[...] += g_im * prev[0] - g_re * prev[1]

        _scan(gr_ref, gi_ref, lr_ref[...], -li_ref[...], True, t, per_tile=lambda_grad)
        dlr_ref[...] = jnp.sum(acc_r[...], axis=0, keepdims=True)
        dli_ref[...] = jnp.sum(acc_i[...], axis=0, keepdims=True)

        def through_b(i, _):
            rows = pl.ds(pl.multiple_of(i * _S5_ROWS, _S5_ROWS), _S5_ROWS)
            ub = u_ref[rows, :].astype(BF16)
            grb, gib = gr_ref[rows, :].astype(BF16), gi_ref[rows, :].astype(BF16)
            dbre_ref[...] += _dot(ub, grb, 0, 0)
            dbim_ref[...] += _dot(ub, gib, 0, 0)
            du_ref[rows, :] = _dot(grb, bre_ref[...], 1, 1) + _dot(gib, bim_ref[...], 1, 1) + d_ref[...] * dy_ref[rows, :]
            return 0

        lax.fori_loop(0, t // _S5_ROWS, through_b, 0)

    col = pl.BlockSpec((t, SSM_CH_BLOCK), lambda j: (0, j))
    blk3 = lambda shape: pl.BlockSpec((None,) + shape, lambda j: (j, 0, 0))
    state = pltpu.VMEM((t, SSM_ST_BLOCK), F32)
    return _call(
        body, name="s5_bwd", grid=(n_blocks,), in_specs=_s5_in_specs(t, d_attn) + [col, col, col],
        out_specs=[col, blk3((SSM_CH_BLOCK, SSM_ST_BLOCK)), blk3((SSM_CH_BLOCK, SSM_ST_BLOCK)),
                   blk3((1, SSM_ST_BLOCK)), blk3((1, SSM_ST_BLOCK)),
                   blk3((SSM_ST_BLOCK, SSM_CH_BLOCK)), blk3((SSM_ST_BLOCK, SSM_CH_BLOCK)),
                   pl.BlockSpec((1, SSM_CH_BLOCK), lambda j: (0, j))],
        out_shape=[_sds((t, d_ssm), F32),
                   _sds((n_blocks, SSM_CH_BLOCK, SSM_ST_BLOCK), F32), _sds((n_blocks, SSM_CH_BLOCK, SSM_ST_BLOCK), F32),
                   _sds((n_blocks, 1, SSM_ST_BLOCK), F32), _sds((n_blocks, 1, SSM_ST_BLOCK), F32),
                   _sds((n_blocks, SSM_ST_BLOCK, SSM_CH_BLOCK), F32), _sds((n_blocks, SSM_ST_BLOCK, SSM_CH_BLOCK), F32),
                   _sds((1, d_ssm), F32)],
        scratch_shapes=[state, state, state, state, pltpu.VMEM((t, SSM_CH_BLOCK), F32),
                        pltpu.VMEM((SUBLANES, SSM_ST_BLOCK), F32), pltpu.VMEM((SUBLANES, SSM_ST_BLOCK), F32)],
        semantics=("parallel",))(proj, *mats, dskip_row, y, dz_a, dz_b)


def _block_diag_in(bbar_pgn):
    p, g, n = bbar_pgn.shape
    b4 = bbar_pgn.reshape(p, g // GROUPS_PER_BLOCK, GROUPS_PER_BLOCK, n)
    eye = jnp.eye(GROUPS_PER_BLOCK, dtype=F32)
    return jnp.einsum("pjgn,gh->jgphn", b4, eye).reshape(g // GROUPS_PER_BLOCK, SSM_CH_BLOCK, SSM_ST_BLOCK)


def _block_diag_in_t(dense):
    j = dense.shape[0]
    d5 = dense.reshape(j, GROUPS_PER_BLOCK, SSM_GROUP, GROUPS_PER_BLOCK, SSM_STATE)
    eye = jnp.eye(GROUPS_PER_BLOCK, dtype=F32)
    return jnp.einsum("jgphn,gh->pjgn", d5, eye).reshape(SSM_GROUP, j * GROUPS_PER_BLOCK, SSM_STATE)


def _block_diag_out(c_gpn):
    g, p, n = c_gpn.shape
    c4 = c_gpn.reshape(g // GROUPS_PER_BLOCK, GROUPS_PER_BLOCK, p, n)
    eye = jnp.eye(GROUPS_PER_BLOCK, dtype=F32)
    return jnp.einsum("jgpn,gh->jgnhp", c4, eye).reshape(g // GROUPS_PER_BLOCK, SSM_ST_BLOCK, SSM_CH_BLOCK)


def _block_diag_out_t(dense):
    j = dense.shape[0]
    d5 = dense.reshape(j, GROUPS_PER_BLOCK, SSM_STATE, GROUPS_PER_BLOCK, SSM_GROUP)
    eye = jnp.eye(GROUPS_PER_BLOCK, dtype=F32)
    return jnp.einsum("jgnhp,gh->jgpn", d5, eye).reshape(j * GROUPS_PER_BLOCK, SSM_GROUP, SSM_STATE)


def _adamw(w, g, m, v):
    m = ADAM_B1 * m + (1.0 - ADAM_B1) * g
    v = ADAM_B2 * v + (1.0 - ADAM_B2) * (g * g)
    m_hat = m / (1.0 - ADAM_B1 ** ADAM_STEP)
    v_hat = v / (1.0 - ADAM_B2 ** ADAM_STEP)
    delta = -ADAM_LR * (m_hat / (jnp.sqrt(v_hat) + ADAM_EPS) + ADAM_WD * w)
    return delta, m, v


def _adam_sharded(name, parts, w, m, v, tr):
    r, c = w.shape
    assert r % tr == 0, (name, r, tr)

    def body(p_ref, w_ref, m_ref, v_ref, g_out, d_out, m_out, v_out):
        g = p_ref[0].astype(F32)
        for i in range(1, N_DEV):
            g = g + p_ref[i].astype(F32)
        delta, m_new, v_new = _adamw(w_ref[...], g, m_ref[...], v_ref[...])
        g_out[...] = g
        d_out[...] = delta
        m_out[...] = m_new
        v_out[...] = v_new

    tile = pl.BlockSpec((tr, c), lambda i: (i, 0))
    return _call(body, name=name, grid=(r // tr,),
                 in_specs=[pl.BlockSpec((N_DEV, tr, c), lambda i: (0, i, 0)), tile, tile, tile],
                 out_specs=[tile] * 4, out_shape=[_sds((r, c), F32)] * 4, semantics=("parallel",))(parts, w, m, v)


_SMALL = ("g_pre_mix", "sinks", "a_re", "a_im", "log_dt", "b_re", "b_im", "c_re", "c_im", "d_skip", "b_glu",
          "g_attn_out", "g_ssm_out", "g_post_mix", "g_pre_ffn", "g_post_ffn")
_BIG = ("w_in", "w_glu", "w_o", "w_gate", "w_up", "w_down")
_ORDER = ("g_pre_mix", "w_in", "sinks", "a_re", "a_im", "log_dt", "b_re", "b_im", "c_re", "c_im", "d_skip", "w_glu",
          "b_glu", "g_attn_out", "g_ssm_out", "w_o", "g_post_mix", "g_pre_ffn", "w_gate", "w_up", "w_down",
          "g_post_ffn")


def _pack(arrays):
    flat = jnp.concatenate([a.reshape(-1).astype(F32) for a in arrays])
    pad = (-flat.shape[0]) % (SUBLANES * LANES)
    return jnp.pad(flat, (0, pad)).reshape(-1, LANES)


def _unpack(packed, like):
    flat = packed.reshape(-1)
    out, at = [], 0
    for a in like:
        out.append(flat[at:at + a.size].reshape(a.shape))
        at += a.size
    return out


def kernel(x, positions, g_pre_mix, w_in, sinks, a_re, a_im, log_dt, b_re, b_im, c_re, c_im, d_skip, w_glu, b_glu, g_attn_out, g_ssm_out, w_o, g_post_mix, g_pre_ffn, w_gate, w_up, w_down, g_post_ffn, loss_target, m_g_pre_mix, m_w_in, m_sinks, m_a_re, m_a_im, m_log_dt, m_b_re, m_b_im, m_c_re, m_c_im, m_d_skip, m_w_glu, m_b_glu, m_g_attn_out, m_g_ssm_out, m_w_o, m_g_post_mix, m_g_pre_ffn, m_w_gate, m_w_up, m_w_down, m_g_post_ffn, v_g_pre_mix, v_w_in, v_sinks, v_a_re, v_a_im, v_log_dt, v_b_re, v_b_im, v_c_re, v_c_im, v_d_skip, v_w_glu, v_b_glu, v_g_attn_out, v_g_ssm_out, v_w_o, v_g_post_mix, v_g_pre_ffn, v_w_gate, v_w_up, v_w_down, v_g_post_ffn):
    given = dict(locals())
    weights = {n: given[n] for n in _ORDER}
    mom_m = {n: given["m_" + n] for n in _ORDER}
    mom_v = {n: given["v_" + n] for n in _ORDER}

    t, d = x.shape[1], x.shape[2]
    d_attn = d // 2
    d_ssm = d - d_attn
    d_in = d_attn + 2 * D_KV + d_ssm
    n_groups = d_ssm // SSM_GROUP
    n_heads = d_attn // HEAD_DIM
    tm = min(256, t)

    x2 = x[0]
    target = loss_target[0]

    shards = [w_in[0], w_glu[0], w_o[0], w_gate[0], w_up[0], w_down[0]]
    win_g, wglu_g, wo_g, wgate_g, wup_g, wdown_g = _exchange("gather_weights", [s.astype(BF16) for s in shards], False)
    w_in_full = win_g.transpose(1, 0, 2).reshape(d, d_in)
    w_glu_full = wglu_g.reshape(d_ssm, d_ssm)
    w_o_full = wo_g.reshape(d, d)

    xn, = _rows("norm_in", lambda xv, g: ([_rms(xv)[0] * g], []), [x2], [g_pre_mix], [(d, BF16)], [], tm)
    proj = _mm_nn("proj_in", xn, w_in_full, F32, tn=d_in // 4 if (d_in // 4) % LANES == 0 else None)

    cos, sin = _rope_tables(positions.reshape(t, 1).astype(F32))
    sinks_row = jnp.pad(sinks, ((0, 0), (0, LANES - n_heads)))
    attn = _attention_fwd(proj, cos, sin, sinks_row, d_attn)

    b_re_t, b_im_t = b_re[0].transpose(2, 0, 1), b_im[0].transpose(2, 0, 1)
    ldt_col = log_dt.reshape(n_groups, 1)
    lam_re, lam_im, bbar_re, bbar_im = _s5_discretise(a_re[0], a_im[0], ldt_col, b_re_t, b_im_t)
    n_blocks = n_groups // GROUPS_PER_BLOCK
    mats = [_block_diag_in(bbar_re).astype(BF16), _block_diag_in(bbar_im).astype(BF16),
            lam_re.reshape(n_blocks, 1, SSM_ST_BLOCK), lam_im.reshape(n_blocks, 1, SSM_ST_BLOCK),
            _block_diag_out(c_re[0]).astype(BF16), _block_diag_out(c_im[0]).astype(BF16)]
    dskip_row = d_skip.reshape(1, d_ssm)
    y_ssm, z_ssm = _s5_fwd(proj, mats, dskip_row, d_attn, d_ssm)
    glu_lin = _mm_nn("glu_gate", z_ssm, w_glu_full, F32)

    def mix_prep(av, yv, gl, bg, ga, gs):
        ssm = _gelu(yv) * _sigmoid(gl + bg)
        return [jnp.concatenate([_rms(av)[0] * ga, _rms(ssm)[0] * gs], axis=1)], []

    mixed, = _rows("mix_prep", mix_prep, [attn, y_ssm, glu_lin], [b_glu, g_attn_out, g_ssm_out], [(d, BF16)], [], tm)
    mix = _mm_nn("mix_out", mixed, w_o_full, F32, tn=d // 2 if (d // 2) % LANES == 0 else None)

    def post_mix(xv, mv, gpm, gpf):
        h = xv + _rms(mv)[0] * gpm
        return [h, _rms(h)[0] * gpf], []

    h, hn = _rows("post_mix", post_mix, [x2, mix], [g_post_mix, g_pre_ffn], [(d, F32), (d, BF16)], [], tm)
    f_sh = wgate_g.shape[2]
    gate = _mm_nn_slots("ffn_gate", hn, wgate_g, F32)
    up = _mm_nn_slots("ffn_up", hn, wup_g, F32)
    gate2, up2 = gate.reshape(N_DEV * t, f_sh), up.reshape(N_DEV * t, f_sh)
    hid2, = _rows("ffn_act", lambda gv, uv: ([gv * _sigmoid(gv) * uv], []), [gate2, up2], [], [(f_sh, BF16)], [], 512)
    hid = hid2.reshape(N_DEV, t, f_sh)
    ff = _mm_contract_slots("ffn_down", [(hid, wdown_g)], F32)

    def head(hv, fv, tv, gpo):
        out = hv + _rms(fv)[0] * gpo
        err = out - tv
        dout = err * (1.0 / d)
        dff, dg = _rms_bwd(fv, gpo, dout)
        loss = jnp.zeros((1, LANES), F32) + 0.5 * jnp.sum(err * err) * (1.0 / d)
        return [dff, dout], [dg, loss]

    dff, dh_out, dg_post_ffn, loss_row = _rows("loss_head", head, [h, ff, target], [g_post_ffn],
                                               [(d, BF16), (d, F32)], [d, LANES], tm)

    dhid = _mm_nt_slots("ffn_down_dx", dff, wdown_g, BF16)
    dw_down = _mm_slots_tn("ffn_down_dw", hid, dff, BF16)

    def ffn_bwd(dh_, gv, uv):
        sg = _sigmoid(gv)
        dh32 = dh_.astype(F32)
        return [dh32 * uv * sg * (1.0 + gv * (1.0 - sg)), dh32 * gv * sg], []

    dgate2, dup2 = _rows("ffn_act_bwd", ffn_bwd, [dhid.reshape(N_DEV * t, f_sh), gate2, up2], [],
                         [(f_sh, BF16), (f_sh, BF16)], [], 512)
    dgate, dup = dgate2.reshape(N_DEV, t, f_sh), dup2.reshape(N_DEV, t, f_sh)
    dhn = _mm_contract_slots_nt("ffn_in_dx", [(dgate, wgate_g), (dup, wup_g)], F32)
    dw_gate = _mm_tn_slots("ffn_gate_dw", hn, dgate, BF16)
    dw_up = _mm_tn_slots("ffn_up_dw", hn, dup, BF16)

    def mid_bwd(dho, dhn_, hv, mv, gpf, gpm):
        d1, dgpf = _rms_bwd(hv, gpf, dhn_)
        dh_ = dho + d1
        dmix_, dgpm = _rms_bwd(mv, gpm, dh_)
        return [dh_, dmix_], [dgpf, dgpm]

    dh, dmix, dg_pre_ffn, dg_post_mix = _rows("mid_bwd", mid_bwd, [dh_out, dhn, h, mix], [g_pre_ffn, g_post_mix],
                                              [(d, F32), (d, BF16)], [d, d], tm)

    dmixed = _mm_nt("mix_out_dx", dmix, w_o_full, F32, tn=d // 2 if (d // 2) % LANES == 0 else None)
    dw_o = _mm_tn("mix_out_dw", mixed, dmix, BF16, tn=d // 2 if (d // 2) % LANES == 0 else None)

    def mix_bwd(dm, av, yv, gl, bg, ga, gs):
        dattn_, dga = _rms_bwd(av, ga, dm[:, :d_attn])
        z = _gelu(yv)
        sg = _sigmoid(gl + bg)
        dssm, dgs = _rms_bwd(z * sg, gs, dm[:, d_attn:])
        dgl = dssm * z * sg * (1.0 - sg)
        return [dattn_, dssm * sg, dgl], [dga, dgs, jnp.sum(dgl, axis=0, keepdims=True)]

    dattn, dz_direct, dglu, dg_attn_out, dg_ssm_out, db_glu = _rows(
        "mix_bwd", mix_bwd, [dmixed, attn, y_ssm, glu_lin], [b_glu, g_attn_out, g_ssm_out],
        [(d_attn, F32), (d_ssm, F32), (d_ssm, BF16)], [d_attn, d_ssm, d_ssm], tm)
    dz_glu = _mm_nt("glu_gate_dx", dglu, w_glu_full, F32)
    dw_glu = _mm_tn("glu_gate_dw", z_ssm, dglu, BF16)

    du, db_re_dense, db_im_dense, dlam_re, dlam_im, dc_re_dense, dc_im_dense, dd_skip = _s5_bwd(
        proj, mats, dskip_row, y_ssm, dz_direct, dz_glu, d_attn, d_ssm)
    da_re, da_im, dlog_dt, db_re_t, db_im_t = _s5_discretise_bwd(
        a_re[0], a_im[0], ldt_col, b_re_t, b_im_t, dlam_re.reshape(n_groups, SSM_STATE),
        dlam_im.reshape(n_groups, SSM_STATE), _block_diag_in_t(db_re_dense), _block_diag_in_t(db_im_dense))
    dq, dk2, dv2, dsinks_row = _attention_bwd(proj, cos, sin, sinks_row, dattn, d_attn)
    dproj = _assemble_dproj(dq, dk2, dv2, du, d_in)

    dxn = _mm_nt("proj_in_dx", dproj, w_in_full, F32, tn=d // 2 if (d // 2) % LANES == 0 else None)
    c_sh = d_in // N_DEV
    dproj_sh = dproj.reshape(t, N_DEV, c_sh).transpose(1, 0, 2)
    dw_in = _mm_tn_slots("proj_in_dw", xn, dproj_sh, BF16)

    def x_bwd(dh_, dxn_, xv, g):
        dx, dg = _rms_bwd(xv, g, dxn_)
        return [dh_ + dx], [dg]

    grad_x, dg_pre_mix = _rows("norm_in_bwd", x_bwd, [dh, dxn, x2], [g_pre_mix], [(d, F32)], [d], tm)

    small_grads = {
        "g_pre_mix": dg_pre_mix, "sinks": dsinks_row[:, :n_heads], "a_re": da_re[None], "a_im": da_im[None],
        "log_dt": dlog_dt.reshape(1, n_groups), "b_re": db_re_t.transpose(1, 2, 0)[None],
        "b_im": db_im_t.transpose(1, 2, 0)[None], "c_re": _block_diag_out_t(dc_re_dense)[None],
        "c_im": _block_diag_out_t(dc_im_dense)[None], "d_skip": dd_skip.reshape(d_skip.shape), "b_glu": db_glu,
        "g_attn_out": dg_attn_out, "g_ssm_out": dg_ssm_out, "g_post_mix": dg_post_mix, "g_pre_ffn": dg_pre_ffn,
        "g_post_ffn": dg_post_ffn,
    }
    small_like = [weights[n] for n in _SMALL]
    packed = _pack([small_grads[n] for n in _SMALL])
    big_grads = [dw_in, dw_glu.reshape(N_DEV, d_ssm // N_DEV, d_ssm), dw_o.reshape(N_DEV, d // N_DEV, d),
                 dw_gate, dw_up, dw_down]
    recv = _exchange("scatter_grads", big_grads, True)
    small_all, = _exchange("gather_small_grads", [packed], False)

    results = {}
    for n, parts in zip(_BIG, recv):
        r = weights[n].shape[1]
        results[n] = _adam_sharded("adam_" + n, parts, weights[n][0], mom_m[n][0], mom_v[n][0],
                                   64 if r % 64 == 0 else r)
    g_s, d_s, m_s, v_s = _adam_sharded(
        "adam_small", small_all, _pack(small_like), _pack([mom_m[n] for n in _SMALL]),
        _pack([mom_v[n] for n in _SMALL]), packed.shape[0])
    for i, vals in enumerate(zip(*[_unpack(p, small_like) for p in (g_s, d_s, m_s, v_s)])):
        results[_SMALL[i]] = vals

    loss = lax.psum(loss_row[0, 0], ("x", "y", "c"))
    outs = [loss, grad_x[None]]
    for k in range(4):
        for n in _ORDER:
            val = results[n][k]
            outs.append(val[None] if n in _BIG else val)
    return tuple(outs)
```

```python
import math

import jax
import jax.numpy as jnp
from jax import lax
from jax.experimental import pallas as pl
from jax.experimental.pallas import tpu as pltpu

F32 = jnp.float32
BF16 = jnp.bfloat16

HEAD_DIM = 64
N_KV_HEADS = 4
D_KV = N_KV_HEADS * HEAD_DIM
WINDOW = 128
BLOCK = 128
ROPE_THETA = 10000.0
SSM_GROUP = 16
SSM_STATE = 64
GROUPS_PER_BLOCK = 8
SSM_CH_BLOCK = GROUPS_PER_BLOCK * SSM_GROUP
SSM_ST_BLOCK = GROUPS_PER_BLOCK * SSM_STATE
RMS_EPS = 1e-6
N_DEV = 8
LANES = 128
SUBLANES = 8
MASKED = -1e30

ADAM_LR = 0.001
ADAM_B1 = 0.9
ADAM_B2 = 0.999
ADAM_EPS = 1e-08
ADAM_WD = 0.01
ADAM_STEP = 10

VMEM_LIMIT_BYTES = 56 * 1024 * 1024


def _call(body, *, name, out_shape, in_specs, out_specs, grid=(), scratch_shapes=(), semantics=None, n_after=0):
    params = dict(vmem_limit_bytes=VMEM_LIMIT_BYTES)
    if semantics is not None:
        params["dimension_semantics"] = semantics
    n_in = len(in_specs)
    if n_after:
        inner = body

        def body(*refs):
            inner(*refs[:n_in], *refs[n_in + n_after:])

        in_specs = list(in_specs) + [pl.BlockSpec(memory_space=pl.ANY)] * n_after
    return pl.pallas_call(body, name=name, grid=grid, in_specs=in_specs, out_specs=out_specs, out_shape=out_shape,
                          scratch_shapes=scratch_shapes, compiler_params=pltpu.CompilerParams(**params))


def _sds(shape, dtype):
    return jax.ShapeDtypeStruct(tuple(shape), dtype)


def _dot(a, b, ca, cb):
    return lax.dot_general(a, b, (((ca,), (cb,)), ((), ())), preferred_element_type=F32)


def _rms(x):
    r = lax.rsqrt(jnp.mean(x * x, axis=-1, keepdims=True) + RMS_EPS)
    return x * r, r


def _rms_bwd(x, g, dy):
    xh, r = _rms(x)
    dxh = dy * g
    dx = r * (dxh - xh * jnp.mean(dxh * xh, axis=-1, keepdims=True))
    return dx, jnp.sum(dy * xh, axis=0, keepdims=True)


def _sigmoid(x):
    return 1.0 / (1.0 + jnp.exp(-x))


_GELU_C = math.sqrt(2.0 / math.pi)
_GELU_A = 0.044715


def _gelu(y):
    t = jnp.tanh(_GELU_C * (y + _GELU_A * y * y * y))
    return 0.5 * y * (1.0 + t)


def _gelu_grad(y):
    t = jnp.tanh(_GELU_C * (y + _GELU_A * y * y * y))
    return 0.5 * (1.0 + t) + 0.5 * y * (1.0 - t * t) * _GELU_C * (1.0 + 3.0 * _GELU_A * y * y)


def _rows(name, fn, row_ins, vec_ins, row_outs, acc_widths, tm, after=()):
    rows = row_ins[0].shape[0]
    assert rows % tm == 0, (name, rows, tm)
    n_row, n_vec, n_out, n_acc = len(row_ins), len(vec_ins), len(row_outs), len(acc_widths)

    def body(*refs):
        ins = [r[...] for r in refs[:n_row + n_vec]]
        outs = refs[n_row + n_vec:n_row + n_vec + n_out]
        accs = refs[n_row + n_vec + n_out:]
        row_vals, acc_vals = fn(*ins)
        for o, v in zip(outs, row_vals):
            o[...] = v.astype(o.dtype)
        if n_acc:
            @pl.when(pl.program_id(0) == 0)
            def _():
                for a in accs:
                    a[...] = jnp.zeros_like(a)
            for a, v in zip(accs, acc_vals):
                a[...] += v

    in_specs = [pl.BlockSpec((tm, a.shape[1]), lambda i: (i, 0)) for a in row_ins]
    in_specs += [pl.BlockSpec(v.shape, lambda i: (0, 0)) for v in vec_ins]
    out_specs = [pl.BlockSpec((tm, w), lambda i: (i, 0)) for w, _ in row_outs]
    out_specs += [pl.BlockSpec((1, w), lambda i: (0, 0)) for w in acc_widths]
    out_shape = [_sds((rows, w), dt) for w, dt in row_outs] + [_sds((1, w), F32) for w in acc_widths]
    return _call(body, name=name, grid=(rows // tm,), in_specs=in_specs, out_specs=out_specs, out_shape=out_shape,
                 semantics=("arbitrary",) if n_acc else ("parallel",), n_after=len(after))(*row_ins, *vec_ins, *after)


def _matmul(name, operands, in_specs, product, grid, out_shape, out_spec, acc_shape):
    nk = grid[-1]
    n_in = len(operands)

    def body(*refs):
        ins = [r[...] for r in refs[:n_in]]
        o_ref = refs[n_in]
        if nk == 1:
            o_ref[...] = product(*ins).astype(o_ref.dtype)
            return
        acc = refs[n_in + 1]
        k = pl.program_id(len(grid) - 1)

        @pl.when(k == 0)
        def _():
            acc[...] = jnp.zeros_like(acc)

        acc[...] += product(*ins)

        @pl.when(k == nk - 1)
        def _():
            o_ref[...] = acc[...].astype(o_ref.dtype)

    return _call(body, name=name, grid=grid, in_specs=in_specs, out_specs=out_spec, out_shape=out_shape,
                 scratch_shapes=[] if nk == 1 else [pltpu.VMEM(acc_shape, F32)],
                 semantics=("parallel",) * (len(grid) - 1) + ("arbitrary",))(*operands)


def _mm_nn(name, a, b, out_dtype, tm=512, tn=None):
    m, k = a.shape
    n = b.shape[1]
    tm, tn = min(tm, m), n if tn is None else tn
    return _matmul(name, [a, b],
                   [pl.BlockSpec((tm, k), lambda i, j, s: (i, 0)), pl.BlockSpec((k, tn), lambda i, j, s: (0, j))],
                   lambda x, y: _dot(x, y, 1, 0), (m // tm, n // tn, 1), _sds((m, n), out_dtype),
                   pl.BlockSpec((tm, tn), lambda i, j, s: (i, j)), (tm, tn))


def _mm_nt(name, a, b, out_dtype, tm=512, tn=None):
    m, k = a.shape
    n = b.shape[0]
    tm, tn = min(tm, m), n if tn is None else tn
    return _matmul(name, [a, b],
                   [pl.BlockSpec((tm, k), lambda i, j, s: (i, 0)), pl.BlockSpec((tn, k), lambda i, j, s: (j, 0))],
                   lambda x, y: _dot(x, y, 1, 1), (m // tm, n // tn, 1), _sds((m, n), out_dtype),
                   pl.BlockSpec((tm, tn), lambda i, j, s: (i, j)), (tm, tn))


def _mm_tn(name, a, b, out_dtype, tm=512, tn=None, tk=512):
    k, m = a.shape
    n = b.shape[1]
    tm, tk, tn = min(tm, m), min(tk, k), n if tn is None else tn
    return _matmul(name, [a, b],
                   [pl.BlockSpec((tk, tm), lambda i, j, s: (s, i)), pl.BlockSpec((tk, tn), lambda i, j, s: (s, j))],
                   lambda x, y: _dot(x, y, 0, 0), (m // tm, n // tn, k // tk), _sds((m, n), out_dtype),
                   pl.BlockSpec((tm, tn), lambda i, j, s: (i, j)), (tm, tn))


def _mm_nn_slots(name, a, b, out_dtype, tm=512):
    m, k = a.shape
    s_, _, n = b.shape
    tm = min(tm, m)
    return _matmul(name, [a, b],
                   [pl.BlockSpec((tm, k), lambda s, i, z: (i, 0)), pl.BlockSpec((None, k, n), lambda s, i, z: (s, 0, 0))],
                   lambda x, y: _dot(x, y, 1, 0), (s_, m // tm, 1), _sds((s_, m, n), out_dtype),
                   pl.BlockSpec((None, tm, n), lambda s, i, z: (s, i, 0)), (tm, n))


def _mm_nt_slots(name, a, b, out_dtype, tm=512):
    m, k = a.shape
    s_, n, _ = b.shape
    tm = min(tm, m)
    return _matmul(name, [a, b],
                   [pl.BlockSpec((tm, k), lambda s, i, z: (i, 0)), pl.BlockSpec((None, n, k), lambda s, i, z: (s, 0, 0))],
                   lambda x, y: _dot(x, y, 1, 1), (s_, m // tm, 1), _sds((s_, m, n), out_dtype),
                   pl.BlockSpec((None, tm, n), lambda s, i, z: (s, i, 0)), (tm, n))


def _mm_contract_slots(name, pairs, out_dtype, tm=512, tn=2048):
    s_, m, k = pairs[0][0].shape
    n = pairs[0][1].shape[2]
    tm, tn = min(tm, m), min(tn, n)
    ops, specs = [], []
    for a, b in pairs:
        ops += [a, b]
        specs += [pl.BlockSpec((None, tm, k), lambda i, j, s: (s, i, 0)), pl.BlockSpec((None, k, tn), lambda i, j, s: (s, 0, j))]

    def product(*t):
        return sum(_dot(t[2 * p], t[2 * p + 1], 1, 0) for p in range(len(pairs)))

    return _matmul(name, ops, specs, product, (m // tm, n // tn, s_), _sds((m, n), out_dtype),
                   pl.BlockSpec((tm, tn), lambda i, j, s: (i, j)), (tm, tn))


def _mm_contract_slots_nt(name, pairs, out_dtype, tm=512, tn=2048):
    s_, m, k = pairs[0][0].shape
    n = pairs[0][1].shape[1]
    tm, tn = min(tm, m), min(tn, n)
    ops, specs = [], []
    for a, b in pairs:
        ops += [a, b]
        specs += [pl.BlockSpec((None, tm, k), lambda i, j, s: (s, i, 0)), pl.BlockSpec((None, tn, k), lambda i, j, s: (s, j, 0))]

    def product(*t):
        return sum(_dot(t[2 * p], t[2 * p + 1], 1, 1) for p in range(len(pairs)))

    return _matmul(name, ops, specs, product, (m // tm, n // tn, s_), _sds((m, n), out_dtype),
                   pl.BlockSpec((tm, tn), lambda i, j, s: (i, j)), (tm, tn))


def _mm_tn_slots(name, a, b, out_dtype, tm=2048, tk=512):
    k, m = a.shape
    s_, _, n = b.shape
    tm, tk = min(tm, m), min(tk, k)
    return _matmul(name, [a, b],
                   [pl.BlockSpec((tk, tm), lambda s, i, z: (z, i)), pl.BlockSpec((None, tk, n), lambda s, i, z: (s, z, 0))],
                   lambda x, y: _dot(x, y, 0, 0), (s_, m // tm, k // tk), _sds((s_, m, n), out_dtype),
                   pl.BlockSpec((None, tm, n), lambda s, i, z: (s, i, 0)), (tm, n))


def _mm_slots_tn(name, a, b, out_dtype, tn=2048, tk=512):
    s_, k, m = a.shape
    n = b.shape[1]
    tn, tk = min(tn, n), min(tk, k)
    return _matmul(name, [a, b],
                   [pl.BlockSpec((None, tk, m), lambda s, j, z: (s, z, 0)), pl.BlockSpec((tk, tn), lambda s, j, z: (z, j))],
                   lambda x, y: _dot(x, y, 0, 0), (s_, n // tn, k // tk), _sds((s_, m, n), out_dtype),
                   pl.BlockSpec((None, m, tn), lambda s, j, z: (s, 0, j)), (m, tn))


def _exchange_copies(ins, lands, send_sems, recv_sems, scatter):
    x, y, c = lax.axis_index("x"), lax.axis_index("y"), lax.axis_index("c")
    me = 4 * x + 2 * y + c

    def flip(v, bit):
        return 1 - v if bit else v

    def copy(a, s, peer, pos, dst_slot):
        return pltpu.make_async_remote_copy(
            src_ref=ins[a].at[peer] if scatter else ins[a], dst_ref=lands[a].at[dst_slot],
            send_sem=send_sems.at[s], recv_sem=recv_sems.at[s], device_id=pos, device_id_type=pl.DeviceIdType.MESH)

    pairs = []
    for r in range(1, N_DEV):
        pos = (flip(x, r & 4), flip(y, r & 2), flip(c, r & 1))
        peer = 4 * pos[0] + 2 * pos[1] + pos[2]
        for a in range(len(ins)):
            s = a * (N_DEV - 1) + r - 1
            pairs.append((copy(a, s, peer, pos, me), copy(a, s, peer, pos, peer)))
    return me, pairs


def _exchange(name, arrays, scatter):
    n = len(arrays)

    def body(*refs):
        ins, outs = refs[:n], refs[n:2 * n]
        send_sems, recv_sems, local_sems = refs[2 * n:]
        me, pairs = _exchange_copies(ins, outs, send_sems, recv_sems, scatter)
        own = [pltpu.make_async_copy(ins[a].at[me] if scatter else ins[a], outs[a].at[me], local_sems.at[a])
               for a in range(n)]
        for cp in own:
            cp.start()
        for send, _ in pairs:
            send.start()
        for _, arrival in pairs:
            arrival.wait_recv()
        for send, _ in pairs:
            send.wait_send()
        for cp in own:
            cp.wait()

    any_spec = pl.BlockSpec(memory_space=pl.ANY)
    n_sem = n * (N_DEV - 1)
    out_shape = [_sds(a.shape if scatter else (N_DEV,) + a.shape, a.dtype) for a in arrays]
    return pl.pallas_call(
        body, name=name, in_specs=[any_spec] * n, out_specs=[any_spec] * n, out_shape=out_shape,
        scratch_shapes=[pltpu.SemaphoreType.DMA((n_sem,)), pltpu.SemaphoreType.DMA((n_sem,)),
                        pltpu.SemaphoreType.DMA((n,))],
    )(*arrays)


_HBM_SPEC = pl.BlockSpec(memory_space=pltpu.HBM)
_SEM_SPEC = pl.BlockSpec(memory_space=pltpu.SEMAPHORE)
_SIDE_EFFECT = pltpu.SideEffectType.DATAFLOW_SIDE_EFFECTING


def _exchange_start(name, arrays, scatter):
    n = len(arrays)
    n_sem = n * (N_DEV - 1)
    lands = [lax.empty(a.shape if scatter else (N_DEV,) + a.shape, a.dtype) for a in arrays]

    def body(*refs):
        ins, land_refs = refs[:n], refs[n:2 * n]
        send_sems, recv_sems, token = refs[2 * n], refs[2 * n + 1], refs[-1]
        _, pairs = _exchange_copies(ins, land_refs, send_sems, recv_sems, scatter)
        for send, _ in pairs:
            send.start()
        token[...] = jnp.zeros_like(token)

    out = pl.pallas_call(
        body, name=name,
        out_shape=(pltpu.SemaphoreType.DMA((n_sem,)), pltpu.SemaphoreType.DMA((n_sem,)),
                   *[pltpu.HBM(a.shape, a.dtype) for a in arrays + lands], _sds((SUBLANES, LANES), F32)),
        in_specs=[_HBM_SPEC] * (2 * n),
        out_specs=(_SEM_SPEC, _SEM_SPEC, *[_HBM_SPEC] * (2 * n), pl.BlockSpec(memory_space=pltpu.VMEM)),
        input_output_aliases={i: 2 + i for i in range(2 * n)},
        compiler_params=pltpu.CompilerParams(has_side_effects=_SIDE_EFFECT),
    )(*[pltpu.with_memory_space_constraint(a, pltpu.HBM) for a in arrays + lands])
    return dict(name=name, n=n, scatter=scatter, sems=out[:2], thru=out[2:2 + 2 * n]), out[-1]


def _exchange_wait(handle, after):
    n, scatter = handle["n"], handle["scatter"]
    thru = list(handle["thru"])

    def body(*refs):
        ins, land_refs = refs[:n], refs[n:2 * n]
        send_sems, recv_sems = refs[2 * n], refs[2 * n + 1]
        _, pairs = _exchange_copies(ins, land_refs, send_sems, recv_sems, scatter)
        for send, arrival in pairs:
            send.wait_send()
            arrival.wait_recv()

    out = pl.pallas_call(
        body, name=handle["name"] + "_wait", out_shape=[pltpu.HBM(a.shape, a.dtype) for a in thru],
        in_specs=[_HBM_SPEC] * (2 * n) + [_SEM_SPEC, _SEM_SPEC] + [pl.BlockSpec(memory_space=pl.ANY)] * len(after),
        out_specs=[_HBM_SPEC] * (2 * n), input_output_aliases={i: i for i in range(2 * n)},
        compiler_params=pltpu.CompilerParams(has_side_effects=_SIDE_EFFECT),
    )(*thru, *handle["sems"], *after)
    me = 4 * lax.axis_index("x") + 2 * lax.axis_index("y") + lax.axis_index("c")
    done = []
    for src, land in zip(out[:n], out[n:]):
        own = lax.dynamic_index_in_dim(src, me, 0, keepdims=True) if scatter else src[None]
        done.append(lax.dynamic_update_slice_in_dim(land, own, me, 0))
    return done


def _rope_tables(pos_col):
    t = pos_col.shape[0]
    half = HEAD_DIM // 2
    inv_freq = ROPE_THETA ** (-jnp.arange(half, dtype=F32) / half)
    inv_row = jnp.tile(inv_freq, LANES // half)[None, :]

    def body(pos_ref, inv_ref, cos_ref, sin_ref):
        ang = pos_ref[...] * inv_ref[...]
        cos_ref[...] = jnp.cos(ang)
        sin_ref[...] = jnp.sin(ang)

    tm = min(t, 512)
    return _call(body, name="rope_tables", grid=(t // tm,),
                 in_specs=[pl.BlockSpec((tm, 1), lambda i: (i, 0)), pl.BlockSpec((1, LANES), lambda i: (0, 0))],
                 out_specs=[pl.BlockSpec((tm, LANES), lambda i: (i, 0))] * 2,
                 out_shape=[_sds((t, LANES), F32)] * 2, semantics=("parallel",))(pos_col, inv_row)


def _rot_half(x):
    lane = lax.broadcasted_iota(jnp.int32, x.shape, 1)
    low = (lane % HEAD_DIM) < HEAD_DIM // 2
    return jnp.where(low, -pltpu.roll(x, LANES - HEAD_DIM // 2, 1), pltpu.roll(x, HEAD_DIM // 2, 1))


def _rope(x, cos, sin):
    return x * cos + _rot_half(x) * sin


def _unrope(d, cos, sin):
    return d * cos - _rot_half(d) * sin


def _band_mask(first_block):
    r = lax.broadcasted_iota(jnp.int32, (BLOCK, 2 * BLOCK), 0)
    c = lax.broadcasted_iota(jnp.int32, (BLOCK, 2 * BLOCK), 1)
    diff = r - c + BLOCK
    return (diff >= 0) & (diff < WINDOW) & ((c >= BLOCK) | jnp.logical_not(first_block))


def _attn_specs(t, d_attn, d_in):
    kb, vb = d_attn // D_KV, d_attn // D_KV + 1
    prev = lambda i: jnp.maximum(i - 1, 0)
    return [
        pl.BlockSpec((BLOCK, d_attn), lambda i: (i, 0)),
        pl.BlockSpec((BLOCK, D_KV), lambda i: (i, kb)),
        pl.BlockSpec((BLOCK, D_KV), lambda i: (i, vb)),
        pl.BlockSpec((BLOCK, D_KV), lambda i: (prev(i), kb)),
        pl.BlockSpec((BLOCK, D_KV), lambda i: (prev(i), vb)),
        pl.BlockSpec((BLOCK, LANES), lambda i: (i, 0)),
        pl.BlockSpec((BLOCK, LANES), lambda i: (i, 0)),
        pl.BlockSpec((BLOCK, LANES), lambda i: (prev(i), 0)),
        pl.BlockSpec((BLOCK, LANES), lambda i: (prev(i), 0)),
        pl.BlockSpec((1, LANES), lambda i: (0, 0)),
    ]


def _head(x, h):
    return x[:, h * HEAD_DIM:(h + 1) * HEAD_DIM]


def _attn_heads(q_ref, kc_ref, vc_ref, kp_ref, vp_ref, cq_ref, sq_ref, cp_ref, sp_ref, d_attn):
    cq, sq, cp, sp = cq_ref[...], sq_ref[...], cp_ref[...], sp_ref[...]
    q_rot = [_rope(q_ref[:, j * LANES:(j + 1) * LANES], cq, sq) for j in range(d_attn // LANES)]
    kc_rot = [_rope(kc_ref[:, j * LANES:(j + 1) * LANES], cq, sq) for j in range(D_KV // LANES)]
    kp_rot = [_rope(kp_ref[:, j * LANES:(j + 1) * LANES], cp, sp) for j in range(D_KV // LANES)]
    per = LANES // HEAD_DIM
    q_heads = [_head(q_rot[h // per], h % per).astype(BF16) for h in range(d_attn // HEAD_DIM)]
    kk = [jnp.concatenate([_head(kp_rot[g // per], g % per), _head(kc_rot[g // per], g % per)], axis=0).astype(BF16)
          for g in range(N_KV_HEADS)]
    vv = [jnp.concatenate([_head(vp_ref[...], g), _head(vc_ref[...], g)], axis=0).astype(BF16) for g in range(N_KV_HEADS)]
    return q_heads, kk, vv


def _softmax_with_sink(q, kk, sink, mask):
    s = _dot(q, kk, 1, 1) * (1.0 / math.sqrt(HEAD_DIM))
    s = jnp.where(mask, s, MASKED)
    m = jnp.maximum(jnp.max(s, axis=-1, keepdims=True), sink)
    p = jnp.exp(s - m)
    e_sink = jnp.exp(sink - m)
    inv = 1.0 / (jnp.sum(p, axis=-1, keepdims=True) + e_sink)
    return p * inv, e_sink * inv


def _attention_fwd(proj, cos, sin, sinks_row, d_attn):
    t, d_in = proj.shape
    n_heads = d_attn // HEAD_DIM
    q_per_kv = n_heads // N_KV_HEADS

    def body(q_ref, kc_ref, vc_ref, kp_ref, vp_ref, cq_ref, sq_ref, cp_ref, sp_ref, sink_ref, o_ref):
        mask = _band_mask(pl.program_id(0) == 0)
        q_heads, kk, vv = _attn_heads(q_ref, kc_ref, vc_ref, kp_ref, vp_ref, cq_ref, sq_ref, cp_ref, sp_ref, d_attn)
        for h in range(n_heads):
            g = h // q_per_kv
            probs, _ = _softmax_with_sink(q_heads[h], kk[g], sink_ref[:, h:h + 1], mask)
            o_ref[:, h * HEAD_DIM:(h + 1) * HEAD_DIM] = _dot(probs.astype(BF16), vv[g], 1, 0)

    return _call(body, name="attention_fwd", grid=(t // BLOCK,), in_specs=_attn_specs(t, d_attn, d_in),
                 out_specs=pl.BlockSpec((BLOCK, d_attn), lambda i: (i, 0)), out_shape=_sds((t, d_attn), F32),
                 semantics=("parallel",))(proj, proj, proj, proj, proj, cos, sin, cos, sin, sinks_row)


def _attention_bwd(proj, cos, sin, sinks_row, d_out, d_attn):
    t, d_in = proj.shape
    n_heads = d_attn // HEAD_DIM
    q_per_kv = n_heads // N_KV_HEADS
    nb = t // BLOCK
    per = LANES // HEAD_DIM

    def body(q_ref, kc_ref, vc_ref, kp_ref, vp_ref, cq_ref, sq_ref, cp_ref, sp_ref, sink_ref, do_ref,
             dq_ref, dk_ref, dv_ref, dsink_ref):
        i = pl.program_id(0)
        mask = _band_mask(i == 0)
        q_heads, kk, vv = _attn_heads(q_ref, kc_ref, vc_ref, kp_ref, vp_ref, cq_ref, sq_ref, cp_ref, sp_ref, d_attn)
        lane = lax.broadcasted_iota(jnp.int32, (1, LANES), 1)
        dsink = jnp.zeros((1, LANES), F32)
        dq_rot, dkk, dvv = [], [], []
        for g in range(N_KV_HEADS):
            dkk_g = jnp.zeros((2 * BLOCK, HEAD_DIM), F32)
            dvv_g = jnp.zeros((2 * BLOCK, HEAD_DIM), F32)
            for h in range(g * q_per_kv, (g + 1) * q_per_kv):
                probs, p_sink = _softmax_with_sink(q_heads[h], kk[g], sink_ref[:, h:h + 1], mask)
                do_h = do_ref[:, h * HEAD_DIM:(h + 1) * HEAD_DIM].astype(BF16)
                dp = _dot(do_h, vv[g], 1, 1)
                delta = jnp.sum(probs * dp, axis=-1, keepdims=True)
                ds = (probs * (dp - delta) * (1.0 / math.sqrt(HEAD_DIM))).astype(BF16)
                dq_rot.append(_dot(ds, kk[g], 1, 0))
                dkk_g += _dot(ds, q_heads[h], 0, 0)
                dvv_g += _dot(probs.astype(BF16), do_h, 0, 0)
                dsink += jnp.where(lane == h, -jnp.sum(p_sink * delta, axis=0, keepdims=True), 0.0)
            dkk.append(dkk_g)
            dvv.append(dvv_g)
        cq, sq, cp, sp = cq_ref[...], sq_ref[...], cp_ref[...], sp_ref[...]
        for j in range(d_attn // LANES):
            d = jnp.concatenate(dq_rot[j * per:(j + 1) * per], axis=1)
            dq_ref[:, j * LANES:(j + 1) * LANES] = _unrope(d, cq, sq)
        for j in range(D_KV // LANES):
            d = jnp.concatenate(dkk[j * per:(j + 1) * per], axis=1)
            dk_ref[0, :, j * LANES:(j + 1) * LANES] = _unrope(d[:BLOCK], cp, sp)
            dk_ref[1, :, j * LANES:(j + 1) * LANES] = _unrope(d[BLOCK:], cq, sq)
            d = jnp.concatenate(dvv[j * per:(j + 1) * per], axis=1)
            dv_ref[0, :, j * LANES:(j + 1) * LANES] = d[:BLOCK]
            dv_ref[1, :, j * LANES:(j + 1) * LANES] = d[BLOCK:]

        @pl.when(i == 0)
        def _():
            dsink_ref[...] = jnp.zeros_like(dsink_ref)

        dsink_ref[...] += dsink

    pair = pl.BlockSpec((2, BLOCK, D_KV), lambda i: (i, 0, 0))
    return _call(body, name="attention_bwd", grid=(nb,),
                 in_specs=_attn_specs(t, d_attn, d_in) + [pl.BlockSpec((BLOCK, d_attn), lambda i: (i, 0))],
                 out_specs=[pl.BlockSpec((BLOCK, d_attn), lambda i: (i, 0)), pair, pair,
                            pl.BlockSpec((1, LANES), lambda i: (0, 0))],
                 out_shape=[_sds((t, d_attn), F32), _sds((2 * nb, BLOCK, D_KV), F32), _sds((2 * nb, BLOCK, D_KV), F32),
                            _sds((1, LANES), F32)],
                 semantics=("arbitrary",))(proj, proj, proj, proj, proj, cos, sin, cos, sin, sinks_row, d_out)


def _assemble_dproj(dq, dk2, dv2, du, d_in):
    t, d_attn = dq.shape
    d_ssm = du.shape[1]
    nb = t // BLOCK

    def body(dq_ref, dk_own, dk_next, dv_own, dv_next, du_ref, o_ref):
        has_next = (pl.program_id(0) < nb - 1).astype(F32)
        o_ref[:, :d_attn] = dq_ref[...].astype(BF16)
        o_ref[:, d_attn:d_attn + D_KV] = (dk_own[...] + has_next * dk_next[...]).astype(BF16)
        o_ref[:, d_attn + D_KV:d_attn + 2 * D_KV] = (dv_own[...] + has_next * dv_next[...]).astype(BF16)
        o_ref[:, d_attn + 2 * D_KV:] = du_ref[...].astype(BF16)

    own = pl.BlockSpec((None, BLOCK, D_KV), lambda i: (2 * i + 1, 0, 0))
    nxt = pl.BlockSpec((None, BLOCK, D_KV), lambda i: (jnp.minimum(2 * i + 2, 2 * nb - 1), 0, 0))
    return _call(body, name="assemble_dproj", grid=(nb,),
                 in_specs=[pl.BlockSpec((BLOCK, d_attn), lambda i: (i, 0)), own, nxt, own, nxt,
                           pl.BlockSpec((BLOCK, d_ssm), lambda i: (i, 0))],
                 out_specs=pl.BlockSpec((BLOCK, d_in), lambda i: (i, 0)), out_shape=_sds((t, d_in), BF16),
                 semantics=("parallel",))(dq, dk2, dk2, dv2, dv2, du)


def _discretise(ar, ai, ldt, br, bi):
    dt = jnp.exp(ldt)
    mag = jnp.exp(ar * dt)
    lam_re = mag * jnp.cos(ai * dt)
    lam_im = mag * jnp.sin(ai * dt)
    den = ar * ar + ai * ai
    nr = lam_re - 1.0
    ni = lam_im
    f_re = (nr * ar + ni * ai) / den
    f_im = (ni * ar - nr * ai) / den
    return lam_re, lam_im, f_re[None] * br - f_im[None] * bi, f_re[None] * bi + f_im[None] * br


def _whole(arrays):
    return [pl.BlockSpec(a.shape, lambda *_, nd=len(a.shape): (0,) * nd) for a in arrays]


def _s5_discretise(ar, ai, ldt, br, bi):
    ins = [ar, ai, ldt, br, bi]

    def body(ar_ref, ai_ref, ldt_ref, br_ref, bi_ref, lr_ref, li_ref, bbr_ref, bbi_ref):
        out = _discretise(ar_ref[...], ai_ref[...], ldt_ref[...], br_ref[...], bi_ref[...])
        for ref, val in zip((lr_ref, li_ref, bbr_ref, bbi_ref), out):
            ref[...] = val

    outs = [_sds(ar.shape, F32), _sds(ar.shape, F32), _sds(br.shape, F32), _sds(br.shape, F32)]
    return _call(body, name="s5_discretise", in_specs=_whole(ins), out_specs=_whole(outs), out_shape=outs)(*ins)


def _s5_discretise_bwd(ar, ai, ldt, br, bi, d_lr, d_li, d_bbr, d_bbi):
    ins = [ar, ai, ldt, br, bi, d_lr, d_li, d_bbr, d_bbi]

    def body(ar_ref, ai_ref, ldt_ref, br_ref, bi_ref, dlr_ref, dli_ref, dbbr_ref, dbbi_ref, *out_refs):
        _, vjp = jax.vjp(_discretise, ar_ref[...], ai_ref[...], ldt_ref[...], br_ref[...], bi_ref[...])
        grads = vjp((dlr_ref[...], dli_ref[...], dbbr_ref[...], dbbi_ref[...]))
        for ref, val in zip(out_refs, grads):
            ref[...] = val

    outs = [_sds(a.shape, F32) for a in (ar, ai, ldt, br, bi)]
    return _call(body, name="s5_discretise_bwd", in_specs=_whole(ins), out_specs=_whole(outs), out_shape=outs)(*ins)


def _cmul(ar, ai, br, bi):
    return ar * br - ai * bi, ar * bi + ai * br


def _power_table(lr, li, reverse):
    pows = [(lr, li)]
    for _ in range(SUBLANES - 1):
        pows.append(_cmul(pows[-1][0], pows[-1][1], lr, li))
    row = lax.broadcasted_iota(jnp.int32, (SUBLANES, lr.shape[1]), 0)
    tr = jnp.zeros((SUBLANES, lr.shape[1]), F32)
    ti = jnp.zeros((SUBLANES, lr.shape[1]), F32)
    for r in range(SUBLANES):
        src = pows[SUBLANES - 1 - r] if reverse else pows[r]
        tr = jnp.where(row == r, src[0], tr)
        ti = jnp.where(row == r, src[1], ti)
    return pows[0], pows[1], pows[3], (tr, ti)


def _scan_tile(xr, xi, steps, table, carry, reverse):
    row = lax.broadcasted_iota(jnp.int32, xr.shape, 0)
    for k, (lr, li) in zip((1, 2, 4), steps):
        if reverse:
            keep = row < SUBLANES - k
            sr = jnp.where(keep, pltpu.roll(xr, SUBLANES - k, 0), 0.0)
            si = jnp.where(keep, pltpu.roll(xi, SUBLANES - k, 0), 0.0)
        else:
            keep = row >= k
            sr = jnp.where(keep, pltpu.roll(xr, k, 0), 0.0)
            si = jnp.where(keep, pltpu.roll(xi, k, 0), 0.0)
        pr, pi = _cmul(lr, li, sr, si)
        xr, xi = xr + pr, xi + pi
    pr, pi = _cmul(table[0], table[1], carry[0], carry[1])
    return xr + pr, xi + pi


def _scan(sr_ref, si_ref, lr, li, reverse, t, per_tile=None):
    l1, l2, l4, table = _power_table(lr, li, reverse)
    n_tiles = t // SUBLANES
    w = lr.shape[1]

    def step(i, carry):
        tile = (n_tiles - 1 - i) if reverse else i
        rows = pl.ds(pl.multiple_of(tile * SUBLANES, SUBLANES), SUBLANES)
        xr, xi = _scan_tile(sr_ref[rows, :], si_ref[rows, :], (l1, l2, l4), table, carry, reverse)
        sr_ref[rows, :] = xr
        si_ref[rows, :] = xi
        if per_tile is not None:
            per_tile(tile, xr, xi)
        edge = 0 if reverse else SUBLANES - 1
        return xr[edge:edge + 1, :], xi[edge:edge + 1, :]

    lax.fori_loop(0, n_tiles, step, (jnp.zeros((1, w), F32), jnp.zeros((1, w), F32)))


_S5_ROWS = 256


def _s5_in_specs(t, d_attn):
    u_block = (d_attn + 2 * D_KV) // SSM_CH_BLOCK
    blk3 = lambda shape: pl.BlockSpec((None,) + shape, lambda j: (j, 0, 0))
    return [
        pl.BlockSpec((t, SSM_CH_BLOCK), lambda j: (0, u_block + j)),
        blk3((SSM_CH_BLOCK, SSM_ST_BLOCK)), blk3((SSM_CH_BLOCK, SSM_ST_BLOCK)),
        blk3((1, SSM_ST_BLOCK)), blk3((1, SSM_ST_BLOCK)),
        blk3((SSM_ST_BLOCK, SSM_CH_BLOCK)), blk3((SSM_ST_BLOCK, SSM_CH_BLOCK)),
        pl.BlockSpec((1, SSM_CH_BLOCK), lambda j: (0, j)),
    ]


def _s5_states(u_ref, bre_ref, bim_ref, lr_ref, li_ref, sr_ref, si_ref, t):
    def fill(i, _):
        rows = pl.ds(pl.multiple_of(i * _S5_ROWS, _S5_ROWS), _S5_ROWS)
        ub = u_ref[rows, :].astype(BF16)
        sr_ref[rows, :] = _dot(ub, bre_ref[...], 1, 0)
        si_ref[rows, :] = _dot(ub, bim_ref[...], 1, 0)
        return 0

    lax.fori_loop(0, t // _S5_ROWS, fill, 0)
    _scan(sr_ref, si_ref, lr_ref[...], li_ref[...], False, t)


def _s5_fwd(proj, mats, dskip_row, d_attn, d_ssm):
    t = proj.shape[0]
    n_blocks = d_ssm // SSM_CH_BLOCK

    def body(u_ref, bre_ref, bim_ref, lr_ref, li_ref, cre_ref, cim_ref, d_ref, y_ref, z_ref, sr_ref, si_ref):
        _s5_states(u_ref, bre_ref, bim_ref, lr_ref, li_ref, sr_ref, si_ref, t)

        def emit(i, _):
            rows = pl.ds(pl.multiple_of(i * _S5_ROWS, _S5_ROWS), _S5_ROWS)
            y = (_dot(sr_ref[rows, :].astype(BF16), cre_ref[...], 1, 0)
                 - _dot(si_ref[rows, :].astype(BF16), cim_ref[...], 1, 0) + d_ref[...] * u_ref[rows, :])
            y_ref[rows, :] = y
            z_ref[rows, :] = _gelu(y).astype(BF16)
            return 0

        lax.fori_loop(0, t // _S5_ROWS, emit, 0)

    col = pl.BlockSpec((t, SSM_CH_BLOCK), lambda j: (0, j))
    return _call(body, name="s5_fwd", grid=(n_blocks,), in_specs=_s5_in_specs(t, d_attn), out_specs=[col, col],
                 out_shape=[_sds((t, d_ssm), F32), _sds((t, d_ssm), BF16)],
                 scratch_shapes=[pltpu.VMEM((t, SSM_ST_BLOCK), F32)] * 2,
                 semantics=("parallel",))(proj, *mats, dskip_row)


def _s5_bwd(proj, mats, dskip_row, y, dz_a, dz_b, d_attn, d_ssm):
    t = proj.shape[0]
    n_blocks = d_ssm // SSM_CH_BLOCK

    def body(u_ref, bre_ref, bim_ref, lr_ref, li_ref, cre_ref, cim_ref, d_ref, y_ref, dza_ref, dzb_ref,
             du_ref, dbre_ref, dbim_ref, dlr_ref, dli_ref, dcre_ref, dcim_ref, dd_ref,
             sr_ref, si_ref, gr_ref, gi_ref, dy_ref, acc_r, acc_i):
        _s5_states(u_ref, bre_ref, bim_ref, lr_ref, li_ref, sr_ref, si_ref, t)
        for ref in (dcre_ref, dcim_ref, dbre_ref, dbim_ref, dd_ref, acc_r, acc_i):
            ref[...] = jnp.zeros_like(ref)

        def through_c(i, _):
            rows = pl.ds(pl.multiple_of(i * _S5_ROWS, _S5_ROWS), _S5_ROWS)
            dy = (dza_ref[rows, :] + dzb_ref[rows, :]) * _gelu_grad(y_ref[rows, :])
            dy_ref[rows, :] = dy
            dd_ref[...] += jnp.sum(dy * u_ref[rows, :], axis=0, keepdims=True)
            dyb = dy.astype(BF16)
            gr_ref[rows, :] = _dot(dyb, cre_ref[...], 1, 1)
            gi_ref[rows, :] = -_dot(dyb, cim_ref[...], 1, 1)
            dcre_ref[...] += _dot(sr_ref[rows, :].astype(BF16), dyb, 0, 0)
            dcim_ref[...] -= _dot(si_ref[rows, :].astype(BF16), dyb, 0, 0)
            return 0

        lax.fori_loop(0, t // _S5_ROWS, through_c, 0)

        def lambda_grad(tile, g_re, g_im):
            rows = pl.ds(pl.multiple_of(tile * SUBLANES, SUBLANES), SUBLANES)
            before = pl.ds(pl.multiple_of(jnp.maximum(tile - 1, 0) * SUBLANES, SUBLANES), SUBLANES)
            row = lax.broadcasted_iota(jnp.int32, g_re.shape, 0)
            live = jnp.where(tile > 0, 1.0, 0.0)
            prev = []
            for ref in (sr_ref, si_ref):
                here = pltpu.roll(ref[rows, :], 1, 0)
                last = pltpu.roll(ref[before, :], 1, 0) * live
                prev.append(jnp.where(row == 0, last, here))
            acc_r[...] += g_re * prev[0] + g_im * prev[1]
            acc_i[...] += g_im * prev[0] - g_re * prev[1]

        _scan(gr_ref, gi_ref, lr_ref[...], -li_ref[...], True, t, per_tile=lambda_grad)
        dlr_ref[...] = jnp.sum(acc_r[...], axis=0, keepdims=True)
        dli_ref[...] = jnp.sum(acc_i[...], axis=0, keepdims=True)

        def through_b(i, _):
            rows = pl.ds(pl.multiple_of(i * _S5_ROWS, _S5_ROWS), _S5_ROWS)
            ub = u_ref[rows, :].astype(BF16)
            grb, gib = gr_ref[rows, :].astype(BF16), gi_ref[rows, :].astype(BF16)
            dbre_ref[...] += _dot(ub, grb, 0, 0)
            dbim_ref[...] += _dot(ub, gib, 0, 0)
            du_ref[rows, :] = _dot(grb, bre_ref[...], 1, 1) + _dot(gib, bim_ref[...], 1, 1) + d_ref[...] * dy_ref[rows, :]
            return 0

        lax.fori_loop(0, t // _S5_ROWS, through_b, 0)

    col = pl.BlockSpec((t, SSM_CH_BLOCK), lambda j: (0, j))
    blk3 = lambda shape: pl.BlockSpec((None,) + shape, lambda j: (j, 0, 0))
    state = pltpu.VMEM((t, SSM_ST_BLOCK), F32)
    return _call(
        body, name="s5_bwd", grid=(n_blocks,), in_specs=_s5_in_specs(t, d_attn) + [col, col, col],
        out_specs=[col, blk3((SSM_CH_BLOCK, SSM_ST_BLOCK)), blk3((SSM_CH_BLOCK, SSM_ST_BLOCK)),
                   blk3((1, SSM_ST_BLOCK)), blk3((1, SSM_ST_BLOCK)),
                   blk3((SSM_ST_BLOCK, SSM_CH_BLOCK)), blk3((SSM_ST_BLOCK, SSM_CH_BLOCK)),
                   pl.BlockSpec((1, SSM_CH_BLOCK), lambda j: (0, j))],
        out_shape=[_sds((t, d_ssm), F32),
                   _sds((n_blocks, SSM_CH_BLOCK, SSM_ST_BLOCK), F32), _sds((n_blocks, SSM_CH_BLOCK, SSM_ST_BLOCK), F32),
                   _sds((n_blocks, 1, SSM_ST_BLOCK), F32), _sds((n_blocks, 1, SSM_ST_BLOCK), F32),
                   _sds((n_blocks, SSM_ST_BLOCK, SSM_CH_BLOCK), F32), _sds((n_blocks, SSM_ST_BLOCK, SSM_CH_BLOCK), F32),
                   _sds((1, d_ssm), F32)],
        scratch_shapes=[state, state, state, state, pltpu.VMEM((t, SSM_CH_BLOCK), F32),
                        pltpu.VMEM((SUBLANES, SSM_ST_BLOCK), F32), pltpu.VMEM((SUBLANES, SSM_ST_BLOCK), F32)],
        semantics=("parallel",))(proj, *mats, dskip_row, y, dz_a, dz_b)


def _block_diag_in(bbar_pgn):
    p, g, n = bbar_pgn.shape
    b4 = bbar_pgn.reshape(p, g // GROUPS_PER_BLOCK, GROUPS_PER_BLOCK, n)
    eye = jnp.eye(GROUPS_PER_BLOCK, dtype=F32)
    return jnp.einsum("pjgn,gh->jgphn", b4, eye).reshape(g // GROUPS_PER_BLOCK, SSM_CH_BLOCK, SSM_ST_BLOCK)


def _block_diag_in_t(dense):
    j = dense.shape[0]
    d5 = dense.reshape(j, GROUPS_PER_BLOCK, SSM_GROUP, GROUPS_PER_BLOCK, SSM_STATE)
    eye = jnp.eye(GROUPS_PER_BLOCK, dtype=F32)
    return jnp.einsum("jgphn,gh->pjgn", d5, eye).reshape(SSM_GROUP, j * GROUPS_PER_BLOCK, SSM_STATE)


def _block_diag_out(c_gpn):
    g, p, n = c_gpn.shape
    c4 = c_gpn.reshape(g // GROUPS_PER_BLOCK, GROUPS_PER_BLOCK, p, n)
    eye = jnp.eye(GROUPS_PER_BLOCK, dtype=F32)
    return jnp.einsum("jgpn,gh->jgnhp", c4, eye).reshape(g // GROUPS_PER_BLOCK, SSM_ST_BLOCK, SSM_CH_BLOCK)


def _block_diag_out_t(dense):
    j = dense.shape[0]
    d5 = dense.reshape(j, GROUPS_PER_BLOCK, SSM_STATE, GROUPS_PER_BLOCK, SSM_GROUP)
    eye = jnp.eye(GROUPS_PER_BLOCK, dtype=F32)
    return jnp.einsum("jgnhp,gh->jgpn", d5, eye).reshape(j * GROUPS_PER_BLOCK, SSM_GROUP, SSM_STATE)


def _adamw(w, g, m, v):
    m = ADAM_B1 * m + (1.0 - ADAM_B1) * g
    v = ADAM_B2 * v + (1.0 - ADAM_B2) * (g * g)
    m_hat = m / (1.0 - ADAM_B1 ** ADAM_STEP)
    v_hat = v / (1.0 - ADAM_B2 ** ADAM_STEP)
    delta = -ADAM_LR * (m_hat / (jnp.sqrt(v_hat) + ADAM_EPS) + ADAM_WD * w)
    return delta, m, v


def _adam_sharded(name, parts, w, m, v, tr):
    r, c = w.shape
    assert r % tr == 0, (name, r, tr)

    def body(p_ref, w_ref, m_ref, v_ref, g_out, d_out, m_out, v_out):
        g = p_ref[0].astype(F32)
        for i in range(1, N_DEV):
            g = g + p_ref[i].astype(F32)
        delta, m_new, v_new = _adamw(w_ref[...], g, m_ref[...], v_ref[...])
        g_out[...] = g
        d_out[...] = delta
        m_out[...] = m_new
        v_out[...] = v_new

    tile = pl.BlockSpec((tr, c), lambda i: (i, 0))
    return _call(body, name=name, grid=(r // tr,),
                 in_specs=[pl.BlockSpec((N_DEV, tr, c), lambda i: (0, i, 0)), tile, tile, tile],
                 out_specs=[tile] * 4, out_shape=[_sds((r, c), F32)] * 4, semantics=("parallel",))(parts, w, m, v)


_SMALL = ("g_pre_mix", "sinks", "a_re", "a_im", "log_dt", "b_re", "b_im", "c_re", "c_im", "d_skip", "b_glu",
          "g_attn_out", "g_ssm_out", "g_post_mix", "g_pre_ffn", "g_post_ffn")
_BIG = ("w_in", "w_glu", "w_o", "w_gate", "w_up", "w_down")
_ORDER = ("g_pre_mix", "w_in", "sinks", "a_re", "a_im", "log_dt", "b_re", "b_im", "c_re", "c_im", "d_skip", "w_glu",
          "b_glu", "g_attn_out", "g_ssm_out", "w_o", "g_post_mix", "g_pre_ffn", "w_gate", "w_up", "w_down",
          "g_post_ffn")


def _pack(arrays):
    flat = jnp.concatenate([a.reshape(-1).astype(F32) for a in arrays])
    pad = (-flat.shape[0]) % (SUBLANES * LANES)
    return jnp.pad(flat, (0, pad)).reshape(-1, LANES)


def _unpack(packed, like):
    flat = packed.reshape(-1)
    out, at = [], 0
    for a in like:
        out.append(flat[at:at + a.size].reshape(a.shape))
        at += a.size
    return out


def kernel(x, positions, g_pre_mix, w_in, sinks, a_re, a_im, log_dt, b_re, b_im, c_re, c_im, d_skip, w_glu, b_glu, g_attn_out, g_ssm_out, w_o, g_post_mix, g_pre_ffn, w_gate, w_up, w_down, g_post_ffn, loss_target, m_g_pre_mix, m_w_in, m_sinks, m_a_re, m_a_im, m_log_dt, m_b_re, m_b_im, m_c_re, m_c_im, m_d_skip, m_w_glu, m_b_glu, m_g_attn_out, m_g_ssm_out, m_w_o, m_g_post_mix, m_g_pre_ffn, m_w_gate, m_w_up, m_w_down, m_g_post_ffn, v_g_pre_mix, v_w_in, v_sinks, v_a_re, v_a_im, v_log_dt, v_b_re, v_b_im, v_c_re, v_c_im, v_d_skip, v_w_glu, v_b_glu, v_g_attn_out, v_g_ssm_out, v_w_o, v_g_post_mix, v_g_pre_ffn, v_w_gate, v_w_up, v_w_down, v_g_post_ffn):
    given = dict(locals())
    weights = {n: given[n] for n in _ORDER}
    mom_m = {n: given["m_" + n] for n in _ORDER}
    mom_v = {n: given["v_" + n] for n in _ORDER}

    t, d = x.shape[1], x.shape[2]
    d_attn = d // 2
    d_ssm = d - d_attn
    d_in = d_attn + 2 * D_KV + d_ssm
    n_groups = d_ssm // SSM_GROUP
    n_heads = d_attn // HEAD_DIM
    tm = min(256, t)

    x2 = x[0]
    target = loss_target[0]

    ag_mix, tok_mix = _exchange_start("gather_mixer_weights", [w[0].astype(BF16) for w in (w_in, w_glu, w_o)], False)
    behind = tok_mix[0, 0].astype(BF16)
    ag_ffn, tok_ffn = _exchange_start("gather_ffn_weights",
                                      [w[0].astype(BF16) + behind for w in (w_gate, w_up, w_down)], False)

    xn, = _rows("norm_in", lambda xv, g: ([_rms(xv)[0] * g], []), [x2], [g_pre_mix], [(d, BF16)], [], tm,
                after=[tok_ffn])
    win_g, wglu_g, wo_g = _exchange_wait(ag_mix, [xn])
    w_in_full = win_g.transpose(1, 0, 2).reshape(d, d_in)
    w_glu_full = wglu_g.reshape(d_ssm, d_ssm)
    w_o_full = wo_g.reshape(d, d)
    proj = _mm_nn("proj_in", xn, w_in_full, F32, tn=d_in // 4 if (d_in // 4) % LANES == 0 else None)

    cos, sin = _rope_tables(positions.reshape(t, 1).astype(F32))
    sinks_row = jnp.pad(sinks, ((0, 0), (0, LANES - n_heads)))
    attn = _attention_fwd(proj, cos, sin, sinks_row, d_attn)

    b_re_t, b_im_t = b_re[0].transpose(2, 0, 1), b_im[0].transpose(2, 0, 1)
    ldt_col = log_dt.reshape(n_groups, 1)
    lam_re, lam_im, bbar_re, bbar_im = _s5_discretise(a_re[0], a_im[0], ldt_col, b_re_t, b_im_t)
    n_blocks = n_groups // GROUPS_PER_BLOCK
    mats = [_block_diag_in(bbar_re).astype(BF16), _block_diag_in(bbar_im).astype(BF16),
            lam_re.reshape(n_blocks, 1, SSM_ST_BLOCK), lam_im.reshape(n_blocks, 1, SSM_ST_BLOCK),
            _block_diag_out(c_re[0]).astype(BF16), _block_diag_out(c_im[0]).astype(BF16)]
    dskip_row = d_skip.reshape(1, d_ssm)
    y_ssm, z_ssm = _s5_fwd(proj, mats, dskip_row, d_attn, d_ssm)
    glu_lin = _mm_nn("glu_gate", z_ssm, w_glu_full, F32)

    def mix_prep(av, yv, gl, bg, ga, gs):
        ssm = _gelu(yv) * _sigmoid(gl + bg)
        return [jnp.concatenate([_rms(av)[0] * ga, _rms(ssm)[0] * gs], axis=1)], []

    mixed, = _rows("mix_prep", mix_prep, [attn, y_ssm, glu_lin], [b_glu, g_attn_out, g_ssm_out], [(d, BF16)], [], tm)
    mix = _mm_nn("mix_out", mixed, w_o_full, F32, tn=d // 2 if (d // 2) % LANES == 0 else None)

    def post_mix(xv, mv, gpm, gpf):
        h = xv + _rms(mv)[0] * gpm
        return [h, _rms(h)[0] * gpf], []

    h, hn = _rows("post_mix", post_mix, [x2, mix], [g_post_mix, g_pre_ffn], [(d, F32), (d, BF16)], [], tm)
    wgate_g, wup_g, wdown_g = _exchange_wait(ag_ffn, [hn])
    f_sh = wgate_g.shape[2]
    gate = _mm_nn_slots("ffn_gate", hn, wgate_g, F32)
    up = _mm_nn_slots("ffn_up", hn, wup_g, F32)
    gate2, up2 = gate.reshape(N_DEV * t, f_sh), up.reshape(N_DEV * t, f_sh)
    hid2, = _rows("ffn_act", lambda gv, uv: ([gv * _sigmoid(gv) * uv], []), [gate2, up2], [], [(f_sh, BF16)], [], 512)
    hid = hid2.reshape(N_DEV, t, f_sh)
    ff = _mm_contract_slots("ffn_down", [(hid, wdown_g)], F32)

    def head(hv, fv, tv, gpo):
        out = hv + _rms(fv)[0] * gpo
        err = out - tv
        dout = err * (1.0 / d)
        dff, dg = _rms_bwd(fv, gpo, dout)
        loss = jnp.zeros((1, LANES), F32) + 0.5 * jnp.sum(err * err) * (1.0 / d)
        return [dff, dout], [dg, loss]

    dff, dh_out, dg_post_ffn, loss_row = _rows("loss_head", head, [h, ff, target], [g_post_ffn],
                                               [(d, BF16), (d, F32)], [d, LANES], tm)

    dhid = _mm_nt_slots("ffn_down_dx", dff, wdown_g, BF16)
    dw_down = _mm_slots_tn("ffn_down_dw", hid, dff, BF16)
    rs_down, tok_down = _exchange_start("scatter_dw_down", [dw_down], True)

    def ffn_bwd(dh_, gv, uv):
        sg = _sigmoid(gv)
        dh32 = dh_.astype(F32)
        return [dh32 * uv * sg * (1.0 + gv * (1.0 - sg)), dh32 * gv * sg], []

    dgate2, dup2 = _rows("ffn_act_bwd", ffn_bwd, [dhid.reshape(N_DEV * t, f_sh), gate2, up2], [],
                         [(f_sh, BF16), (f_sh, BF16)], [], 512, after=[tok_down])
    dgate, dup = dgate2.reshape(N_DEV, t, f_sh), dup2.reshape(N_DEV, t, f_sh)
    dhn = _mm_contract_slots_nt("ffn_in_dx", [(dgate, wgate_g), (dup, wup_g)], F32)
    dw_gate = _mm_tn_slots("ffn_gate_dw", hn, dgate, BF16)
    dw_up = _mm_tn_slots("ffn_up_dw", hn, dup, BF16)
    rs_ffn_in, tok_ffn_in = _exchange_start("scatter_dw_gate_up", [dw_gate, dw_up], True)

    def mid_bwd(dho, dhn_, hv, mv, gpf, gpm):
        d1, dgpf = _rms_bwd(hv, gpf, dhn_)
        dh_ = dho + d1
        dmix_, dgpm = _rms_bwd(mv, gpm, dh_)
        return [dh_, dmix_], [dgpf, dgpm]

    dh, dmix, dg_pre_ffn, dg_post_mix = _rows("mid_bwd", mid_bwd, [dh_out, dhn, h, mix], [g_pre_ffn, g_post_mix],
                                              [(d, F32), (d, BF16)], [d, d], tm, after=[tok_ffn_in])

    dmixed = _mm_nt("mix_out_dx", dmix, w_o_full, F32, tn=d // 2 if (d // 2) % LANES == 0 else None)
    dw_o = _mm_tn("mix_out_dw", mixed, dmix, BF16, tn=d // 2 if (d // 2) % LANES == 0 else None)
    rs_o, tok_o = _exchange_start("scatter_dw_o", [dw_o.reshape(N_DEV, d // N_DEV, d)], True)

    def mix_bwd(dm, av, yv, gl, bg, ga, gs):
        dattn_, dga = _rms_bwd(av, ga, dm[:, :d_attn])
        z = _gelu(yv)
        sg = _sigmoid(gl + bg)
        dssm, dgs = _rms_bwd(z * sg, gs, dm[:, d_attn:])
        dgl = dssm * z * sg * (1.0 - sg)
        return [dattn_, dssm * sg, dgl], [dga, dgs, jnp.sum(dgl, axis=0, keepdims=True)]

    dattn, dz_direct, dglu, dg_attn_out, dg_ssm_out, db_glu = _rows(
        "mix_bwd", mix_bwd, [dmixed, attn, y_ssm, glu_lin], [b_glu, g_attn_out, g_ssm_out],
        [(d_attn, F32), (d_ssm, F32), (d_ssm, BF16)], [d_attn, d_ssm, d_ssm], tm, after=[tok_o])
    dz_glu = _mm_nt("glu_gate_dx", dglu, w_glu_full, F32)
    dw_glu = _mm_tn("glu_gate_dw", z_ssm, dglu, BF16)

    du, db_re_dense, db_im_dense, dlam_re, dlam_im, dc_re_dense, dc_im_dense, dd_skip = _s5_bwd(
        proj, mats, dskip_row, y_ssm, dz_direct, dz_glu, d_attn, d_ssm)
    da_re, da_im, dlog_dt, db_re_t, db_im_t = _s5_discretise_bwd(
        a_re[0], a_im[0], ldt_col, b_re_t, b_im_t, dlam_re.reshape(n_groups, SSM_STATE),
        dlam_im.reshape(n_groups, SSM_STATE), _block_diag_in_t(db_re_dense), _block_diag_in_t(db_im_dense))
    dq, dk2, dv2, dsinks_row = _attention_bwd(proj, cos, sin, sinks_row, dattn, d_attn)
    dproj = _assemble_dproj(dq, dk2, dv2, du, d_in)

    dxn = _mm_nt("proj_in_dx", dproj, w_in_full, F32, tn=d // 2 if (d // 2) % LANES == 0 else None)
    c_sh = d_in // N_DEV
    dproj_sh = dproj.reshape(t, N_DEV, c_sh).transpose(1, 0, 2)
    dw_in = _mm_tn_slots("proj_in_dw", xn, dproj_sh, BF16)

    def x_bwd(dh_, dxn_, xv, g):
        dx, dg = _rms_bwd(xv, g, dxn_)
        return [dh_ + dx], [dg]

    grad_x, dg_pre_mix = _rows("norm_in_bwd", x_bwd, [dh, dxn, x2], [g_pre_mix], [(d, F32)], [d], tm)

    small_grads = {
        "g_pre_mix": dg_pre_mix, "sinks": dsinks_row[:, :n_heads], "a_re": da_re[None], "a_im": da_im[None],
        "log_dt": dlog_dt.reshape(1, n_groups), "b_re": db_re_t.transpose(1, 2, 0)[None],
        "b_im": db_im_t.transpose(1, 2, 0)[None], "c_re": _block_diag_out_t(dc_re_dense)[None],
        "c_im": _block_diag_out_t(dc_im_dense)[None], "d_skip": dd_skip.reshape(d_skip.shape), "b_glu": db_glu,
        "g_attn_out": dg_attn_out, "g_ssm_out": dg_ssm_out, "g_post_mix": dg_post_mix, "g_pre_ffn": dg_pre_ffn,
        "g_post_ffn": dg_post_ffn,
    }
    small_like = [weights[n] for n in _SMALL]
    packed = _pack([small_grads[n] for n in _SMALL])
    rs_in, _ = _exchange_start("scatter_dw_in_glu", [dw_in, dw_glu.reshape(N_DEV, d_ssm // N_DEV, d_ssm)], True)
    small_all, = _exchange("gather_small_grads", [packed], False)
    last = [grad_x, small_all]
    p_down, = _exchange_wait(rs_down, last)
    p_gate, p_up = _exchange_wait(rs_ffn_in, last)
    p_o, = _exchange_wait(rs_o, last)
    p_in, p_glu = _exchange_wait(rs_in, last)

    results = {}
    for n, parts in zip(_BIG, [p_in, p_glu, p_o, p_gate, p_up, p_down]):
        r = weights[n].shape[1]
        results[n] = _adam_sharded("adam_" + n, parts, weights[n][0], mom_m[n][0], mom_v[n][0],
                                   64 if r % 64 == 0 else r)
    g_s, d_s, m_s, v_s = _adam_sharded(
        "adam_small", small_all, _pack(small_like), _pack([mom_m[n] for n in _SMALL]),
        _pack([mom_v[n] for n in _SMALL]), packed.shape[0])
    for i, vals in enumerate(zip(*[_unpack(p, small_like) for p in (g_s, d_s, m_s, v_s)])):
        results[_SMALL[i]] = vals

    loss = lax.psum(loss_row[0, 0], ("x", "y", "c"))
    outs = [loss, grad_x[None]]
    for k in range(4):
        for n in _ORDER:
            val = results[n][k]
            outs.append(val[None] if n in _BIG else val)
    return tuple(outs)
```

```python
import math

import jax
import jax.numpy as jnp
from jax import lax
from jax.experimental import pallas as pl
from jax.experimental.pallas import tpu as pltpu

F32 = jnp.float32
BF16 = jnp.bfloat16

HEAD_DIM = 64
N_KV_HEADS = 4
D_KV = N_KV_HEADS * HEAD_DIM
WINDOW = 128
BLOCK = 128
ROPE_THETA = 10000.0
SSM_GROUP = 16
SSM_STATE = 64
GROUPS_PER_BLOCK = 8
SSM_CH_BLOCK = GROUPS_PER_BLOCK * SSM_GROUP
SSM_ST_BLOCK = GROUPS_PER_BLOCK * SSM_STATE
RMS_EPS = 1e-6
N_DEV = 8
LANES = 128
SUBLANES = 8
MASKED = -1e30

ADAM_LR = 0.001
ADAM_B1 = 0.9
ADAM_B2 = 0.999
ADAM_EPS = 1e-08
ADAM_WD = 0.01
ADAM_STEP = 10

VMEM_LIMIT_BYTES = 56 * 1024 * 1024


def _call(body, *, name, out_shape, in_specs, out_specs, grid=(), scratch_shapes=(), semantics=None, n_after=0):
    params = dict(vmem_limit_bytes=VMEM_LIMIT_BYTES)
    if semantics is not None:
        params["dimension_semantics"] = semantics
    n_in = len(in_specs)
    if n_after:
        inner = body

        def body(*refs):
            inner(*refs[:n_in], *refs[n_in + n_after:])

        in_specs = list(in_specs) + [pl.BlockSpec(memory_space=pl.ANY)] * n_after
    return pl.pallas_call(body, name=name, grid=grid, in_specs=in_specs, out_specs=out_specs, out_shape=out_shape,
                          scratch_shapes=scratch_shapes, compiler_params=pltpu.CompilerParams(**params))


def _sds(shape, dtype):
    return jax.ShapeDtypeStruct(tuple(shape), dtype)


def _dot(a, b, ca, cb):
    return lax.dot_general(a, b, (((ca,), (cb,)), ((), ())), preferred_element_type=F32)


def _rms(x):
    r = lax.rsqrt(jnp.mean(x * x, axis=-1, keepdims=True) + RMS_EPS)
    return x * r, r


def _rms_bwd(x, g, dy):
    xh, r = _rms(x)
    dxh = dy * g
    dx = r * (dxh - xh * jnp.mean(dxh * xh, axis=-1, keepdims=True))
    return dx, jnp.sum(dy * xh, axis=0, keepdims=True)


def _sigmoid(x):
    return 1.0 / (1.0 + jnp.exp(-x))


_GELU_C = math.sqrt(2.0 / math.pi)
_GELU_A = 0.044715


def _gelu(y):
    t = jnp.tanh(_GELU_C * (y + _GELU_A * y * y * y))
    return 0.5 * y * (1.0 + t)


def _gelu_grad(y):
    t = jnp.tanh(_GELU_C * (y + _GELU_A * y * y * y))
    return 0.5 * (1.0 + t) + 0.5 * y * (1.0 - t * t) * _GELU_C * (1.0 + 3.0 * _GELU_A * y * y)


def _rows(name, fn, row_ins, vec_ins, row_outs, acc_widths, tm, after=()):
    rows = row_ins[0].shape[0]
    assert rows % tm == 0, (name, rows, tm)
    n_row, n_vec, n_out, n_acc = len(row_ins), len(vec_ins), len(row_outs), len(acc_widths)

    def body(*refs):
        ins = [r[...] for r in refs[:n_row + n_vec]]
        outs = refs[n_row + n_vec:n_row + n_vec + n_out]
        accs = refs[n_row + n_vec + n_out:]
        row_vals, acc_vals = fn(*ins)
        for o, v in zip(outs, row_vals):
            o[...] = v.astype(o.dtype)
        if n_acc:
            @pl.when(pl.program_id(0) == 0)
            def _():
                for a in accs:
                    a[...] = jnp.zeros_like(a)
            for a, v in zip(accs, acc_vals):
                a[...] += v

    in_specs = [pl.BlockSpec((tm, a.shape[1]), lambda i: (i, 0)) for a in row_ins]
    in_specs += [pl.BlockSpec(v.shape, lambda i: (0, 0)) for v in vec_ins]
    out_specs = [pl.BlockSpec((tm, w), lambda i: (i, 0)) for w, _ in row_outs]
    out_specs += [pl.BlockSpec((1, w), lambda i: (0, 0)) for w in acc_widths]
    out_shape = [_sds((rows, w), dt) for w, dt in row_outs] + [_sds((1, w), F32) for w in acc_widths]
    return _call(body, name=name, grid=(rows // tm,), in_specs=in_specs, out_specs=out_specs, out_shape=out_shape,
                 semantics=("arbitrary",) if n_acc else ("parallel",), n_after=len(after))(*row_ins, *vec_ins, *after)


def _matmul(name, operands, in_specs, product, grid, out_shape, out_spec, acc_shape):
    nk = grid[-1]
    n_in = len(operands)

    def body(*refs):
        ins = [r[...] for r in refs[:n_in]]
        o_ref = refs[n_in]
        if nk == 1:
            o_ref[...] = product(*ins).astype(o_ref.dtype)
            return
        acc = refs[n_in + 1]
        k = pl.program_id(len(grid) - 1)

        @pl.when(k == 0)
        def _():
            acc[...] = jnp.zeros_like(acc)

        acc[...] += product(*ins)

        @pl.when(k == nk - 1)
        def _():
            o_ref[...] = acc[...].astype(o_ref.dtype)

    return _call(body, name=name, grid=grid, in_specs=in_specs, out_specs=out_spec, out_shape=out_shape,
                 scratch_shapes=[] if nk == 1 else [pltpu.VMEM(acc_shape, F32)],
                 semantics=("parallel",) * (len(grid) - 1) + ("arbitrary",))(*operands)


def _mm_nn(name, a, b, out_dtype, tm=512, tn=None):
    m, k = a.shape
    n = b.shape[1]
    tm, tn = min(tm, m), n if tn is None else tn
    return _matmul(name, [a, b],
                   [pl.BlockSpec((tm, k), lambda i, j, s: (i, 0)), pl.BlockSpec((k, tn), lambda i, j, s: (0, j))],
                   lambda x, y: _dot(x, y, 1, 0), (m // tm, n // tn, 1), _sds((m, n), out_dtype),
                   pl.BlockSpec((tm, tn), lambda i, j, s: (i, j)), (tm, tn))


def _mm_nt(name, a, b, out_dtype, tm=512, tn=None):
    m, k = a.shape
    n = b.shape[0]
    tm, tn = min(tm, m), n if tn is None else tn
    return _matmul(name, [a, b],
                   [pl.BlockSpec((tm, k), lambda i, j, s: (i, 0)), pl.BlockSpec((tn, k), lambda i, j, s: (j, 0))],
                   lambda x, y: _dot(x, y, 1, 1), (m // tm, n // tn, 1), _sds((m, n), out_dtype),
                   pl.BlockSpec((tm, tn), lambda i, j, s: (i, j)), (tm, tn))


def _mm_tn(name, a, b, out_dtype, tm=512, tn=None, tk=512):
    k, m = a.shape
    n = b.shape[1]
    tm, tk, tn = min(tm, m), min(tk, k), n if tn is None else tn
    return _matmul(name, [a, b],
                   [pl.BlockSpec((tk, tm), lambda i, j, s: (s, i)), pl.BlockSpec((tk, tn), lambda i, j, s: (s, j))],
                   lambda x, y: _dot(x, y, 0, 0), (m // tm, n // tn, k // tk), _sds((m, n), out_dtype),
                   pl.BlockSpec((tm, tn), lambda i, j, s: (i, j)), (tm, tn))


def _mm_nn_slots(name, a, b, out_dtype, tm=512):
    m, k = a.shape
    s_, _, n = b.shape
    tm = min(tm, m)
    return _matmul(name, [a, b],
                   [pl.BlockSpec((tm, k), lambda s, i, z: (i, 0)), pl.BlockSpec((None, k, n), lambda s, i, z: (s, 0, 0))],
                   lambda x, y: _dot(x, y, 1, 0), (s_, m // tm, 1), _sds((s_, m, n), out_dtype),
                   pl.BlockSpec((None, tm, n), lambda s, i, z: (s, i, 0)), (tm, n))


def _mm_nt_slots(name, a, b, out_dtype, tm=512):
    m, k = a.shape
    s_, n, _ = b.shape
    tm = min(tm, m)
    return _matmul(name, [a, b],
                   [pl.BlockSpec((tm, k), lambda s, i, z: (i, 0)), pl.BlockSpec((None, n, k), lambda s, i, z: (s, 0, 0))],
                   lambda x, y: _dot(x, y, 1, 1), (s_, m // tm, 1), _sds((s_, m, n), out_dtype),
                   pl.BlockSpec((None, tm, n), lambda s, i, z: (s, i, 0)), (tm, n))


def _mm_contract_slots(name, pairs, out_dtype, tm=512, tn=2048):
    s_, m, k = pairs[0][0].shape
    n = pairs[0][1].shape[2]
    tm, tn = min(tm, m), min(tn, n)
    ops, specs = [], []
    for a, b in pairs:
        ops += [a, b]
        specs += [pl.BlockSpec((None, tm, k), lambda i, j, s: (s, i, 0)), pl.BlockSpec((None, k, tn), lambda i, j, s: (s, 0, j))]

    def product(*t):
        return sum(_dot(t[2 * p], t[2 * p + 1], 1, 0) for p in range(len(pairs)))

    return _matmul(name, ops, specs, product, (m // tm, n // tn, s_), _sds((m, n), out_dtype),
                   pl.BlockSpec((tm, tn), lambda i, j, s: (i, j)), (tm, tn))


def _mm_contract_slots_nt(name, pairs, out_dtype, tm=512, tn=2048):
    s_, m, k = pairs[0][0].shape
    n = pairs[0][1].shape[1]
    tm, tn = min(tm, m), min(tn, n)
    ops, specs = [], []
    for a, b in pairs:
        ops += [a, b]
        specs += [pl.BlockSpec((None, tm, k), lambda i, j, s: (s, i, 0)), pl.BlockSpec((None, tn, k), lambda i, j, s: (s, j, 0))]

    def product(*t):
        return sum(_dot(t[2 * p], t[2 * p + 1], 1, 1) for p in range(len(pairs)))

    return _matmul(name, ops, specs, product, (m // tm, n // tn, s_), _sds((m, n), out_dtype),
                   pl.BlockSpec((tm, tn), lambda i, j, s: (i, j)), (tm, tn))


def _mm_tn_slots(name, a, b, out_dtype, tm=2048, tk=512):
    k, m = a.shape
    s_, _, n = b.shape
    tm, tk = min(tm, m), min(tk, k)
    return _matmul(name, [a, b],
                   [pl.BlockSpec((tk, tm), lambda s, i, z: (z, i)), pl.BlockSpec((None, tk, n), lambda s, i, z: (s, z, 0))],
                   lambda x, y: _dot(x, y, 0, 0), (s_, m // tm, k // tk), _sds((s_, m, n), out_dtype),
                   pl.BlockSpec((None, tm, n), lambda s, i, z: (s, i, 0)), (tm, n))


def _mm_slots_tn(name, a, b, out_dtype, tn=2048, tk=512):
    s_, k, m = a.shape
    n = b.shape[1]
    tn, tk = min(tn, n), min(tk, k)
    return _matmul(name, [a, b],
                   [pl.BlockSpec((None, tk, m), lambda s, j, z: (s, z, 0)), pl.BlockSpec((tk, tn), lambda s, j, z: (z, j))],
                   lambda x, y: _dot(x, y, 0, 0), (s_, n // tn, k // tk), _sds((s_, m, n), out_dtype),
                   pl.BlockSpec((None, m, tn), lambda s, j, z: (s, 0, j)), (m, tn))


def _exchange_copies(ins, lands, send_sems, recv_sems, scatter):
    x, y, c = lax.axis_index("x"), lax.axis_index("y"), lax.axis_index("c")
    me = 4 * x + 2 * y + c

    def flip(v, bit):
        return 1 - v if bit else v

    def copy(a, s, peer, pos, dst_slot):
        return pltpu.make_async_remote_copy(
            src_ref=ins[a].at[peer] if scatter else ins[a], dst_ref=lands[a].at[dst_slot],
            send_sem=send_sems.at[s], recv_sem=recv_sems.at[s], device_id=pos, device_id_type=pl.DeviceIdType.MESH)

    pairs = []
    for r in range(1, N_DEV):
        pos = (flip(x, r & 4), flip(y, r & 2), flip(c, r & 1))
        peer = 4 * pos[0] + 2 * pos[1] + pos[2]
        for a in range(len(ins)):
            s = a * (N_DEV - 1) + r - 1
            pairs.append((copy(a, s, peer, pos, me), copy(a, s, peer, pos, peer)))
    return me, pairs


def _exchange(name, arrays, scatter):
    n = len(arrays)

    def body(*refs):
        ins, outs = refs[:n], refs[n:2 * n]
        send_sems, recv_sems, local_sems = refs[2 * n:]
        me, pairs = _exchange_copies(ins, outs, send_sems, recv_sems, scatter)
        own = [pltpu.make_async_copy(ins[a].at[me] if scatter else ins[a], outs[a].at[me], local_sems.at[a])
               for a in range(n)]
        for cp in own:
            cp.start()
        for send, _ in pairs:
            send.start()
        for _, arrival in pairs:
            arrival.wait_recv()
        for send, _ in pairs:
            send.wait_send()
        for cp in own:
            cp.wait()

    any_spec = pl.BlockSpec(memory_space=pl.ANY)
    n_sem = n * (N_DEV - 1)
    out_shape = [_sds(a.shape if scatter else (N_DEV,) + a.shape, a.dtype) for a in arrays]
    return pl.pallas_call(
        body, name=name, in_specs=[any_spec] * n, out_specs=[any_spec] * n, out_shape=out_shape,
        scratch_shapes=[pltpu.SemaphoreType.DMA((n_sem,)), pltpu.SemaphoreType.DMA((n_sem,)),
                        pltpu.SemaphoreType.DMA((n,))],
    )(*arrays)


_HBM_SPEC = pl.BlockSpec(memory_space=pltpu.HBM)
_SEM_SPEC = pl.BlockSpec(memory_space=pltpu.SEMAPHORE)
_SIDE_EFFECT = pltpu.SideEffectType.DATAFLOW_SIDE_EFFECTING


def _exchange_start(name, arrays, scatter):
    n = len(arrays)
    n_sem = n * (N_DEV - 1)
    lands = [lax.empty(a.shape if scatter else (N_DEV,) + a.shape, a.dtype) for a in arrays]

    def body(*refs):
        ins, land_refs = refs[:n], refs[n:2 * n]
        send_sems, recv_sems, token = refs[2 * n], refs[2 * n + 1], refs[-1]
        _, pairs = _exchange_copies(ins, land_refs, send_sems, recv_sems, scatter)
        for send, _ in pairs:
            send.start()
        token[...] = jnp.zeros_like(token)

    out = pl.pallas_call(
        body, name=name,
        out_shape=(pltpu.SemaphoreType.DMA((n_sem,)), pltpu.SemaphoreType.DMA((n_sem,)),
                   *[pltpu.HBM(a.shape, a.dtype) for a in arrays + lands], _sds((SUBLANES, LANES), F32)),
        in_specs=[_HBM_SPEC] * (2 * n),
        out_specs=(_SEM_SPEC, _SEM_SPEC, *[_HBM_SPEC] * (2 * n), pl.BlockSpec(memory_space=pltpu.VMEM)),
        input_output_aliases={i: 2 + i for i in range(2 * n)},
        compiler_params=pltpu.CompilerParams(has_side_effects=_SIDE_EFFECT),
    )(*[pltpu.with_memory_space_constraint(a, pltpu.HBM) for a in arrays + lands])
    return dict(name=name, n=n, scatter=scatter, sems=out[:2], thru=out[2:2 + 2 * n]), out[-1]


def _exchange_wait(handle, after):
    n, scatter = handle["n"], handle["scatter"]
    thru = list(handle["thru"])

    def body(*refs):
        ins, land_refs = refs[:n], refs[n:2 * n]
        send_sems, recv_sems = refs[2 * n], refs[2 * n + 1]
        _, pairs = _exchange_copies(ins, land_refs, send_sems, recv_sems, scatter)
        for send, arrival in pairs:
            send.wait_send()
            arrival.wait_recv()

    out = pl.pallas_call(
        body, name=handle["name"] + "_wait", out_shape=[pltpu.HBM(a.shape, a.dtype) for a in thru],
        in_specs=[_HBM_SPEC] * (2 * n) + [_SEM_SPEC, _SEM_SPEC] + [pl.BlockSpec(memory_space=pl.ANY)] * len(after),
        out_specs=[_HBM_SPEC] * (2 * n), input_output_aliases={i: i for i in range(2 * n)},
        compiler_params=pltpu.CompilerParams(has_side_effects=_SIDE_EFFECT),
    )(*thru, *handle["sems"], *after)
    me = 4 * lax.axis_index("x") + 2 * lax.axis_index("y") + lax.axis_index("c")
    done = []
    for src, land in zip(out[:n], out[n:]):
        own = lax.dynamic_index_in_dim(src, me, 0, keepdims=True) if scatter else src[None]
        done.append(lax.dynamic_update_slice_in_dim(land, own, me, 0))
    return done


def _rope_tables(pos_col):
    t = pos_col.shape[0]
    half = HEAD_DIM // 2
    inv_freq = ROPE_THETA ** (-jnp.arange(half, dtype=F32) / half)
    inv_row = jnp.tile(inv_freq, LANES // half)[None, :]

    def body(pos_ref, inv_ref, cos_ref, sin_ref):
        ang = pos_ref[...] * inv_ref[...]
        cos_ref[...] = jnp.cos(ang)
        sin_ref[...] = jnp.sin(ang)

    tm = min(t, 512)
    return _call(body, name="rope_tables", grid=(t // tm,),
                 in_specs=[pl.BlockSpec((tm, 1), lambda i: (i, 0)), pl.BlockSpec((1, LANES), lambda i: (0, 0))],
                 out_specs=[pl.BlockSpec((tm, LANES), lambda i: (i, 0))] * 2,
                 out_shape=[_sds((t, LANES), F32)] * 2, semantics=("parallel",))(pos_col, inv_row)


def _rot_half(x):
    lane = lax.broadcasted_iota(jnp.int32, x.shape, 1)
    low = (lane % HEAD_DIM) < HEAD_DIM // 2
    return jnp.where(low, -pltpu.roll(x, LANES - HEAD_DIM // 2, 1), pltpu.roll(x, HEAD_DIM // 2, 1))


def _rope(x, cos, sin):
    return x * cos + _rot_half(x) * sin


def _unrope(d, cos, sin):
    return d * cos - _rot_half(d) * sin


def _band_mask(first_block):
    r = lax.broadcasted_iota(jnp.int32, (BLOCK, 2 * BLOCK), 0)
    c = lax.broadcasted_iota(jnp.int32, (BLOCK, 2 * BLOCK), 1)
    diff = r - c + BLOCK
    return (diff >= 0) & (diff < WINDOW) & ((c >= BLOCK) | jnp.logical_not(first_block))


def _attn_specs(t, d_attn, d_in):
    kb, vb = d_attn // D_KV, d_attn // D_KV + 1
    prev = lambda i: jnp.maximum(i - 1, 0)
    return [
        pl.BlockSpec((BLOCK, d_attn), lambda i: (i, 0)),
        pl.BlockSpec((BLOCK, D_KV), lambda i: (i, kb)),
        pl.BlockSpec((BLOCK, D_KV), lambda i: (i, vb)),
        pl.BlockSpec((BLOCK, D_KV), lambda i: (prev(i), kb)),
        pl.BlockSpec((BLOCK, D_KV), lambda i: (prev(i), vb)),
        pl.BlockSpec((BLOCK, LANES), lambda i: (i, 0)),
        pl.BlockSpec((BLOCK, LANES), lambda i: (i, 0)),
        pl.BlockSpec((BLOCK, LANES), lambda i: (prev(i), 0)),
        pl.BlockSpec((BLOCK, LANES), lambda i: (prev(i), 0)),
        pl.BlockSpec((1, LANES), lambda i: (0, 0)),
    ]


def _head(x, h):
    return x[:, h * HEAD_DIM:(h + 1) * HEAD_DIM]


def _attn_heads(q_ref, kc_ref, vc_ref, kp_ref, vp_ref, cq_ref, sq_ref, cp_ref, sp_ref, d_attn):
    cq, sq, cp, sp = cq_ref[...], sq_ref[...], cp_ref[...], sp_ref[...]
    q_rot = [_rope(q_ref[:, j * LANES:(j + 1) * LANES], cq, sq) for j in range(d_attn // LANES)]
    kc_rot = [_rope(kc_ref[:, j * LANES:(j + 1) * LANES], cq, sq) for j in range(D_KV // LANES)]
    kp_rot = [_rope(kp_ref[:, j * LANES:(j + 1) * LANES], cp, sp) for j in range(D_KV // LANES)]
    per = LANES // HEAD_DIM
    q_heads = [_head(q_rot[h // per], h % per).astype(BF16) for h in range(d_attn // HEAD_DIM)]
    kk = [jnp.concatenate([_head(kp_rot[g // per], g % per), _head(kc_rot[g // per], g % per)], axis=0).astype(BF16)
          for g in range(N_KV_HEADS)]
    vv = [jnp.concatenate([_head(vp_ref[...], g), _head(vc_ref[...], g)], axis=0).astype(BF16) for g in range(N_KV_HEADS)]
    return q_heads, kk, vv


def _softmax_with_sink(q, kk, sink, mask):
    s = _dot(q, kk, 1, 1) * (1.0 / math.sqrt(HEAD_DIM))
    s = jnp.where(mask, s, MASKED)
    m = jnp.maximum(jnp.max(s, axis=-1, keepdims=True), sink)
    p = jnp.exp(s - m)
    e_sink = jnp.exp(sink - m)
    inv = 1.0 / (jnp.sum(p, axis=-1, keepdims=True) + e_sink)
    return p * inv, e_sink * inv


def _attention_fwd(proj, cos, sin, sinks_row, d_attn):
    t, d_in = proj.shape
    n_heads = d_attn // HEAD_DIM
    q_per_kv = n_heads // N_KV_HEADS

    def body(q_ref, kc_ref, vc_ref, kp_ref, vp_ref, cq_ref, sq_ref, cp_ref, sp_ref, sink_ref, o_ref):
        mask = _band_mask(pl.program_id(0) == 0)
        q_heads, kk, vv = _attn_heads(q_ref, kc_ref, vc_ref, kp_ref, vp_ref, cq_ref, sq_ref, cp_ref, sp_ref, d_attn)
        for h in range(n_heads):
            g = h // q_per_kv
            probs, _ = _softmax_with_sink(q_heads[h], kk[g], sink_ref[:, h:h + 1], mask)
            o_ref[:, h * HEAD_DIM:(h + 1) * HEAD_DIM] = _dot(probs.astype(BF16), vv[g], 1, 0)

    return _call(body, name="attention_fwd", grid=(t // BLOCK,), in_specs=_attn_specs(t, d_attn, d_in),
                 out_specs=pl.BlockSpec((BLOCK, d_attn), lambda i: (i, 0)), out_shape=_sds((t, d_attn), F32),
                 semantics=("parallel",))(proj, proj, proj, proj, proj, cos, sin, cos, sin, sinks_row)


def _attention_bwd(proj, cos, sin, sinks_row, d_out, d_attn):
    t, d_in = proj.shape
    n_heads = d_attn // HEAD_DIM
    q_per_kv = n_heads // N_KV_HEADS
    nb = t // BLOCK
    per = LANES // HEAD_DIM

    def body(q_ref, kc_ref, vc_ref, kp_ref, vp_ref, cq_ref, sq_ref, cp_ref, sp_ref, sink_ref, do_ref,
             dq_ref, dk_ref, dv_ref, dsink_ref):
        i = pl.program_id(0)
        mask = _band_mask(i == 0)
        q_heads, kk, vv = _attn_heads(q_ref, kc_ref, vc_ref, kp_ref, vp_ref, cq_ref, sq_ref, cp_ref, sp_ref, d_attn)
        lane = lax.broadcasted_iota(jnp.int32, (1, LANES), 1)
        dsink = jnp.zeros((1, LANES), F32)
        dq_rot, dkk, dvv = [], [], []
        for g in range(N_KV_HEADS):
            dkk_g = jnp.zeros((2 * BLOCK, HEAD_DIM), F32)
            dvv_g = jnp.zeros((2 * BLOCK, HEAD_DIM), F32)
            for h in range(g * q_per_kv, (g + 1) * q_per_kv):
                probs, p_sink = _softmax_with_sink(q_heads[h], kk[g], sink_ref[:, h:h + 1], mask)
                do_h = do_ref[:, h * HEAD_DIM:(h + 1) * HEAD_DIM].astype(BF16)
                dp = _dot(do_h, vv[g], 1, 1)
                delta = jnp.sum(probs * dp, axis=-1, keepdims=True)
                ds = (probs * (dp - delta) * (1.0 / math.sqrt(HEAD_DIM))).astype(BF16)
                dq_rot.append(_dot(ds, kk[g], 1, 0))
                dkk_g += _dot(ds, q_heads[h], 0, 0)
                dvv_g += _dot(probs.astype(BF16), do_h, 0, 0)
                dsink += jnp.where(lane == h, -jnp.sum(p_sink * delta, axis=0, keepdims=True), 0.0)
            dkk.append(dkk_g)
            dvv.append(dvv_g)
        cq, sq, cp, sp = cq_ref[...], sq_ref[...], cp_ref[...], sp_ref[...]
        for j in range(d_attn // LANES):
            d = jnp.concatenate(dq_rot[j * per:(j + 1) * per], axis=1)
            dq_ref[:, j * LANES:(j + 1) * LANES] = _unrope(d, cq, sq)
        for j in range(D_KV // LANES):
            d = jnp.concatenate(dkk[j * per:(j + 1) * per], axis=1)
            dk_ref[0, :, j * LANES:(j + 1) * LANES] = _unrope(d[:BLOCK], cp, sp)
            dk_ref[1, :, j * LANES:(j + 1) * LANES] = _unrope(d[BLOCK:], cq, sq)
            d = jnp.concatenate(dvv[j * per:(j + 1) * per], axis=1)
            dv_ref[0, :, j * LANES:(j + 1) * LANES] = d[:BLOCK]
            dv_ref[1, :, j * LANES:(j + 1) * LANES] = d[BLOCK:]

        @pl.when(i == 0)
        def _():
            dsink_ref[...] = jnp.zeros_like(dsink_ref)

        dsink_ref[...] += dsink

    pair = pl.BlockSpec((2, BLOCK, D_KV), lambda i: (i, 0, 0))
    return _call(body, name="attention_bwd", grid=(nb,),
                 in_specs=_attn_specs(t, d_attn, d_in) + [pl.BlockSpec((BLOCK, d_attn), lambda i: (i, 0))],
                 out_specs=[pl.BlockSpec((BLOCK, d_attn), lambda i: (i, 0)), pair, pair,
                            pl.BlockSpec((1, LANES), lambda i: (0, 0))],
                 out_shape=[_sds((t, d_attn), F32), _sds((2 * nb, BLOCK, D_KV), F32), _sds((2 * nb, BLOCK, D_KV), F32),
                            _sds((1, LANES), F32)],
                 semantics=("arbitrary",))(proj, proj, proj, proj, proj, cos, sin, cos, sin, sinks_row, d_out)


def _assemble_dproj(dq, dk2, dv2, du, d_in):
    t, d_attn = dq.shape
    d_ssm = du.shape[1]
    nb = t // BLOCK

    def body(dq_ref, dk_own, dk_next, dv_own, dv_next, du_ref, o_ref):
        has_next = (pl.program_id(0) < nb - 1).astype(F32)
        o_ref[:, :d_attn] = dq_ref[...].astype(BF16)
        o_ref[:, d_attn:d_attn + D_KV] = (dk_own[...] + has_next * dk_next[...]).astype(BF16)
        o_ref[:, d_attn + D_KV:d_attn + 2 * D_KV] = (dv_own[...] + has_next * dv_next[...]).astype(BF16)
        o_ref[:, d_attn + 2 * D_KV:] = du_ref[...].astype(BF16)

    own = pl.BlockSpec((None, BLOCK, D_KV), lambda i: (2 * i + 1, 0, 0))
    nxt = pl.BlockSpec((None, BLOCK, D_KV), lambda i: (jnp.minimum(2 * i + 2, 2 * nb - 1), 0, 0))
    return _call(body, name="assemble_dproj", grid=(nb,),
                 in_specs=[pl.BlockSpec((BLOCK, d_attn), lambda i: (i, 0)), own, nxt, own, nxt,
                           pl.BlockSpec((BLOCK, d_ssm), lambda i: (i, 0))],
                 out_specs=pl.BlockSpec((BLOCK, d_in), lambda i: (i, 0)), out_shape=_sds((t, d_in), BF16),
                 semantics=("parallel",))(dq, dk2, dk2, dv2, dv2, du)


def _discretise(ar, ai, ldt, br, bi):
    dt = jnp.exp(ldt)
    mag = jnp.exp(ar * dt)
    lam_re = mag * jnp.cos(ai * dt)
    lam_im = mag * jnp.sin(ai * dt)
    den = ar * ar + ai * ai
    nr = lam_re - 1.0
    ni = lam_im
    f_re = (nr * ar + ni * ai) / den
    f_im = (ni * ar - nr * ai) / den
    return lam_re, lam_im, f_re[None] * br - f_im[None] * bi, f_re[None] * bi + f_im[None] * br


def _whole(arrays):
    return [pl.BlockSpec(a.shape, lambda *_, nd=len(a.shape): (0,) * nd) for a in arrays]


def _s5_discretise(ar, ai, ldt, br, bi):
    ins = [ar, ai, ldt, br, bi]

    def body(ar_ref, ai_ref, ldt_ref, br_ref, bi_ref, lr_ref, li_ref, bbr_ref, bbi_ref):
        out = _discretise(ar_ref[...], ai_ref[...], ldt_ref[...], br_ref[...], bi_ref[...])
        for ref, val in zip((lr_ref, li_ref, bbr_ref, bbi_ref), out):
            ref[...] = val

    outs = [_sds(ar.shape, F32), _sds(ar.shape, F32), _sds(br.shape, F32), _sds(br.shape, F32)]
    return _call(body, name="s5_discretise", in_specs=_whole(ins), out_specs=_whole(outs), out_shape=outs)(*ins)


def _s5_discretise_bwd(ar, ai, ldt, br, bi, d_lr, d_li, d_bbr, d_bbi):
    ins = [ar, ai, ldt, br, bi, d_lr, d_li, d_bbr, d_bbi]

    def body(ar_ref, ai_ref, ldt_ref, br_ref, bi_ref, dlr_ref, dli_ref, dbbr_ref, dbbi_ref, *out_refs):
        _, vjp = jax.vjp(_discretise, ar_ref[...], ai_ref[...], ldt_ref[...], br_ref[...], bi_ref[...])
        grads = vjp((dlr_ref[...], dli_ref[...], dbbr_ref[...], dbbi_ref[...]))
        for ref, val in zip(out_refs, grads):
            ref[...] = val

    outs = [_sds(a.shape, F32) for a in (ar, ai, ldt, br, bi)]
    return _call(body, name="s5_discretise_bwd", in_specs=_whole(ins), out_specs=_whole(outs), out_shape=outs)(*ins)


def _cmul(ar, ai, br, bi):
    return ar * br - ai * bi, ar * bi + ai * br


def _power_table(lr, li, reverse):
    pows = [(lr, li)]
    for _ in range(SUBLANES - 1):
        pows.append(_cmul(pows[-1][0], pows[-1][1], lr, li))
    row = lax.broadcasted_iota(jnp.int32, (SUBLANES, lr.shape[1]), 0)
    tr = jnp.zeros((SUBLANES, lr.shape[1]), F32)
    ti = jnp.zeros((SUBLANES, lr.shape[1]), F32)
    for r in range(SUBLANES):
        src = pows[SUBLANES - 1 - r] if reverse else pows[r]
        tr = jnp.where(row == r, src[0], tr)
        ti = jnp.where(row == r, src[1], ti)
    return pows[0], pows[1], pows[3], (tr, ti)


def _scan_tile(xr, xi, steps, table, carry, reverse):
    row = lax.broadcasted_iota(jnp.int32, xr.shape, 0)
    for k, (lr, li) in zip((1, 2, 4), steps):
        if reverse:
            keep = row < SUBLANES - k
            sr = jnp.where(keep, pltpu.roll(xr, SUBLANES - k, 0), 0.0)
            si = jnp.where(keep, pltpu.roll(xi, SUBLANES - k, 0), 0.0)
        else:
            keep = row >= k
            sr = jnp.where(keep, pltpu.roll(xr, k, 0), 0.0)
            si = jnp.where(keep, pltpu.roll(xi, k, 0), 0.0)
        pr, pi = _cmul(lr, li, sr, si)
        xr, xi = xr + pr, xi + pi
    pr, pi = _cmul(table[0], table[1], carry[0], carry[1])
    return xr + pr, xi + pi


def _scan(sr_ref, si_ref, lr, li, reverse, t, per_tile=None):
    l1, l2, l4, table = _power_table(lr, li, reverse)
    n_tiles = t // SUBLANES
    w = lr.shape[1]

    def step(i, carry):
        tile = (n_tiles - 1 - i) if reverse else i
        rows = pl.ds(pl.multiple_of(tile * SUBLANES, SUBLANES), SUBLANES)
        xr, xi = _scan_tile(sr_ref[rows, :], si_ref[rows, :], (l1, l2, l4), table, carry, reverse)
        sr_ref[rows, :] = xr
        si_ref[rows, :] = xi
        if per_tile is not None:
            per_tile(tile, xr, xi)
        edge = 0 if reverse else SUBLANES - 1
        return xr[edge:edge + 1, :], xi[edge:edge + 1, :]

    lax.fori_loop(0, n_tiles, step, (jnp.zeros((1, w), F32), jnp.zeros((1, w), F32)))


_S5_ROWS = 256


def _s5_in_specs(t, d_attn):
    u_block = (d_attn + 2 * D_KV) // SSM_CH_BLOCK
    blk3 = lambda shape: pl.BlockSpec((None,) + shape, lambda j: (j, 0, 0))
    return [
        pl.BlockSpec((t, SSM_CH_BLOCK), lambda j: (0, u_block + j)),
        blk3((SSM_CH_BLOCK, SSM_ST_BLOCK)), blk3((SSM_CH_BLOCK, SSM_ST_BLOCK)),
        blk3((1, SSM_ST_BLOCK)), blk3((1, SSM_ST_BLOCK)),
        blk3((SSM_ST_BLOCK, SSM_CH_BLOCK)), blk3((SSM_ST_BLOCK, SSM_CH_BLOCK)),
        pl.BlockSpec((1, SSM_CH_BLOCK), lambda j: (0, j)),
    ]


def _s5_states(u_ref, bre_ref, bim_ref, lr_ref, li_ref, sr_ref, si_ref, t):
    def fill(i, _):
        rows = pl.ds(pl.multiple_of(i * _S5_ROWS, _S5_ROWS), _S5_ROWS)
        ub = u_ref[rows, :].astype(BF16)
        sr_ref[rows, :] = _dot(ub, bre_ref[...], 1, 0)
        si_ref[rows, :] = _dot(ub, bim_ref[...], 1, 0)
        return 0

    lax.fori_loop(0, t // _S5_ROWS, fill, 0)
    _scan(sr_ref, si_ref, lr_ref[...], li_ref[...], False, t)


def _s5_fwd(proj, mats, dskip_row, d_attn, d_ssm):
    t = proj.shape[0]
    n_blocks = d_ssm // SSM_CH_BLOCK

    def body(u_ref, bre_ref, bim_ref, lr_ref, li_ref, cre_ref, cim_ref, d_ref, y_ref, z_ref, sr_ref, si_ref):
        _s5_states(u_ref, bre_ref, bim_ref, lr_ref, li_ref, sr_ref, si_ref, t)

        def emit(i, _):
            rows = pl.ds(pl.multiple_of(i * _S5_ROWS, _S5_ROWS), _S5_ROWS)
            y = (_dot(sr_ref[rows, :].astype(BF16), cre_ref[...], 1, 0)
                 - _dot(si_ref[rows, :].astype(BF16), cim_ref[...], 1, 0) + d_ref[...] * u_ref[rows, :])
            y_ref[rows, :] = y
            z_ref[rows, :] = _gelu(y).astype(BF16)
            return 0

        lax.fori_loop(0, t // _S5_ROWS, emit, 0)

    col = pl.BlockSpec((t, SSM_CH_BLOCK), lambda j: (0, j))
    return _call(body, name="s5_fwd", grid=(n_blocks,), in_specs=_s5_in_specs(t, d_attn), out_specs=[col, col],
                 out_shape=[_sds((t, d_ssm), F32), _sds((t, d_ssm), BF16)],
                 scratch_shapes=[pltpu.VMEM((t, SSM_ST_BLOCK), F32)] * 2,
                 semantics=("parallel",))(proj, *mats, dskip_row)


def _s5_bwd(proj, mats, dskip_row, y, dz_a, dz_b, d_attn, d_ssm):
    t = proj.shape[0]
    n_blocks = d_ssm // SSM_CH_BLOCK

    def body(u_ref, bre_ref, bim_ref, lr_ref, li_ref, cre_ref, cim_ref, d_ref, y_ref, dza_ref, dzb_ref,
             du_ref, dbre_ref, dbim_ref, dlr_ref, dli_ref, dcre_ref, dcim_ref, dd_ref,
             sr_ref, si_ref, gr_ref, gi_ref, dy_ref, acc_r, acc_i):
        _s5_states(u_ref, bre_ref, bim_ref, lr_ref, li_ref, sr_ref, si_ref, t)
        for ref in (dcre_ref, dcim_ref, dbre_ref, dbim_ref, dd_ref, acc_r, acc_i):
            ref[...] = jnp.zeros_like(ref)

        def through_c(i, _):
            rows = pl.ds(pl.multiple_of(i * _S5_ROWS, _S5_ROWS), _S5_ROWS)
            dy = (dza_ref[rows, :] + dzb_ref[rows, :]) * _gelu_grad(y_ref[rows, :])
            dy_ref[rows, :] = dy
            dd_ref[...] += jnp.sum(dy * u_ref[rows, :], axis=0, keepdims=True)
            dyb = dy.astype(BF16)
            gr_ref[rows, :] = _dot(dyb, cre_ref[...], 1, 1)
            gi_ref[rows, :] = -_dot(dyb, cim_ref[...], 1, 1)
            dcre_ref[...] += _dot(sr_ref[rows, :].astype(BF16), dyb, 0, 0)
            dcim_ref[...] -= _dot(si_ref[rows, :].astype(BF16), dyb, 0, 0)
            return 0

        lax.fori_loop(0, t // _S5_ROWS, through_c, 0)

        def lambda_grad(tile, g_re, g_im):
            rows = pl.ds(pl.multiple_of(tile * SUBLANES, SUBLANES), SUBLANES)
            before = pl.ds(pl.multiple_of(jnp.maximum(tile - 1, 0) * SUBLANES, SUBLANES), SUBLANES)
            row = lax.broadcasted_iota(jnp.int32, g_re.shape, 0)
            live = jnp.where(tile > 0, 1.0, 0.0)
            prev = []
            for ref in (sr_ref, si_ref):
                here = pltpu.roll(ref[rows, :], 1, 0)
                last = pltpu.roll(ref[before, :], 1, 0) * live
                prev.append(jnp.where(row == 0, last, here))
            acc_r[...] += g_re * prev[0] + g_im * prev[1]
            acc_i[...] += g_im * prev[0] - g_re * prev[1]

        _scan(gr_ref, gi_ref, lr_ref[...], -li_ref[...], True, t, per_tile=lambda_grad)
        dlr_ref[...] = jnp.sum(acc_r[...], axis=0, keepdims=True)
        dli_ref[...] = jnp.sum(acc_i[...], axis=0, keepdims=True)

        def through_b(i, _):
            rows = pl.ds(pl.multiple_of(i * _S5_ROWS, _S5_ROWS), _S5_ROWS)
            ub = u_ref[rows, :].astype(BF16)
            grb, gib = gr_ref[rows, :].astype(BF16), gi_ref[rows, :].astype(BF16)
            dbre_ref[...] += _dot(ub, grb, 0, 0)
            dbim_ref[...] += _dot(ub, gib, 0, 0)
            du_ref[rows, :] = _dot(grb, bre_ref[...], 1, 1) + _dot(gib, bim_ref[...], 1, 1) + d_ref[...] * dy_ref[rows, :]
            return 0

        lax.fori_loop(0, t // _S5_ROWS, through_b, 0)

    col = pl.BlockSpec((t, SSM_CH_BLOCK), lambda j: (0, j))
    blk3 = lambda shape: pl.BlockSpec((None,) + shape, lambda j: (j, 0, 0))
    state = pltpu.VMEM((t, SSM_ST_BLOCK), F32)
    return _call(
        body, name="s5_bwd", grid=(n_blocks,), in_specs=_s5_in_specs(t, d_attn) + [col, col, col],
        out_specs=[col, blk3((SSM_CH_BLOCK, SSM_ST_BLOCK)), blk3((SSM_CH_BLOCK, SSM_ST_BLOCK)),
                   blk3((1, SSM_ST_BLOCK)), blk3((1, SSM_ST_BLOCK)),
                   blk3((SSM_ST_BLOCK, SSM_CH_BLOCK)), blk3((SSM_ST_BLOCK, SSM_CH_BLOCK)),
                   pl.BlockSpec((1, SSM_CH_BLOCK), lambda j: (0, j))],
        out_shape=[_sds((t, d_ssm), F32),
                   _sds((n_blocks, SSM_CH_BLOCK, SSM_ST_BLOCK), F32), _sds((n_blocks, SSM_CH_BLOCK, SSM_ST_BLOCK), F32),
                   _sds((n_blocks, 1, SSM_ST_BLOCK), F32), _sds((n_blocks, 1, SSM_ST_BLOCK), F32),
                   _sds((n_blocks, SSM_ST_BLOCK, SSM_CH_BLOCK), F32), _sds((n_blocks, SSM_ST_BLOCK, SSM_CH_BLOCK), F32),
                   _sds((1, d_ssm), F32)],
        scratch_shapes=[state, state, state, state, pltpu.VMEM((t, SSM_CH_BLOCK), F32),
                        pltpu.VMEM((SUBLANES, SSM_ST_BLOCK), F32), pltpu.VMEM((SUBLANES, SSM_ST_BLOCK), F32)],
        semantics=("parallel",))(proj, *mats, dskip_row, y, dz_a, dz_b)


def _block_diag_in(bbar_pgn):
    p, g, n = bbar_pgn.shape
    b4 = bbar_pgn.reshape(p, g // GROUPS_PER_BLOCK, GROUPS_PER_BLOCK, n)
    eye = jnp.eye(GROUPS_PER_BLOCK, dtype=F32)
    return jnp.einsum("pjgn,gh->jgphn", b4, eye).reshape(g // GROUPS_PER_BLOCK, SSM_CH_BLOCK, SSM_ST_BLOCK)


def _block_diag_in_t(dense):
    j = dense.shape[0]
    d5 = dense.reshape(j, GROUPS_PER_BLOCK, SSM_GROUP, GROUPS_PER_BLOCK, SSM_STATE)
    eye = jnp.eye(GROUPS_PER_BLOCK, dtype=F32)
    return jnp.einsum("jgphn,gh->pjgn", d5, eye).reshape(SSM_GROUP, j * GROUPS_PER_BLOCK, SSM_STATE)


def _block_diag_out(c_gpn):
    g, p, n = c_gpn.shape
    c4 = c_gpn.reshape(g // GROUPS_PER_BLOCK, GROUPS_PER_BLOCK, p, n)
    eye = jnp.eye(GROUPS_PER_BLOCK, dtype=F32)
    return jnp.einsum("jgpn,gh->jgnhp", c4, eye).reshape(g // GROUPS_PER_BLOCK, SSM_ST_BLOCK, SSM_CH_BLOCK)


def _block_diag_out_t(dense):
    j = dense.shape[0]
    d5 = dense.reshape(j, GROUPS_PER_BLOCK, SSM_STATE, GROUPS_PER_BLOCK, SSM_GROUP)
    eye = jnp.eye(GROUPS_PER_BLOCK, dtype=F32)
    return jnp.einsum("jgnhp,gh->jgpn", d5, eye).reshape(j * GROUPS_PER_BLOCK, SSM_GROUP, SSM_STATE)


def _adamw(w, g, m, v):
    m = ADAM_B1 * m + (1.0 - ADAM_B1) * g
    v = ADAM_B2 * v + (1.0 - ADAM_B2) * (g * g)
    m_hat = m / (1.0 - ADAM_B1 ** ADAM_STEP)
    v_hat = v / (1.0 - ADAM_B2 ** ADAM_STEP)
    delta = -ADAM_LR * (m_hat / (jnp.sqrt(v_hat) + ADAM_EPS) + ADAM_WD * w)
    return delta, m, v


def _adam_sharded(name, parts, w, m, v, tr):
    r, c = w.shape
    assert r % tr == 0, (name, r, tr)

    def body(p_ref, w_ref, m_ref, v_ref, g_out, d_out, m_out, v_out):
        g = p_ref[0].astype(F32)
        for i in range(1, N_DEV):
            g = g + p_ref[i].astype(F32)
        delta, m_new, v_new = _adamw(w_ref[...], g, m_ref[...], v_ref[...])
        g_out[...] = g
        d_out[...] = delta
        m_out[...] = m_new
        v_out[...] = v_new

    tile = pl.BlockSpec((tr, c), lambda i: (i, 0))
    return _call(body, name=name, grid=(r // tr,),
                 in_specs=[pl.BlockSpec((N_DEV, tr, c), lambda i: (0, i, 0)), tile, tile, tile],
                 out_specs=[tile] * 4, out_shape=[_sds((r, c), F32)] * 4, semantics=("parallel",))(parts, w, m, v)


_SMALL = ("g_pre_mix", "sinks", "a_re", "a_im", "log_dt", "b_re", "b_im", "c_re", "c_im", "d_skip", "b_glu",
          "g_attn_out", "g_ssm_out", "g_post_mix", "g_pre_ffn", "g_post_ffn")
_BIG = ("w_in", "w_glu", "w_o", "w_gate", "w_up", "w_down")
_ORDER = ("g_pre_mix", "w_in", "sinks", "a_re", "a_im", "log_dt", "b_re", "b_im", "c_re", "c_im", "d_skip", "w_glu",
          "b_glu", "g_attn_out", "g_ssm_out", "w_o", "g_post_mix", "g_pre_ffn", "w_gate", "w_up", "w_down",
          "g_post_ffn")


def _pack(arrays):
    flat = jnp.concatenate([a.reshape(-1).astype(F32) for a in arrays])
    pad = (-flat.shape[0]) % (SUBLANES * LANES)
    return jnp.pad(flat, (0, pad)).reshape(-1, LANES)


def _unpack(packed, like):
    flat = packed.reshape(-1)
    out, at = [], 0
    for a in like:
        out.append(flat[at:at + a.size].reshape(a.shape))
        at += a.size
    return out


def kernel(x, positions, g_pre_mix, w_in, sinks, a_re, a_im, log_dt, b_re, b_im, c_re, c_im, d_skip, w_glu, b_glu, g_attn_out, g_ssm_out, w_o, g_post_mix, g_pre_ffn, w_gate, w_up, w_down, g_post_ffn, loss_target, m_g_pre_mix, m_w_in, m_sinks, m_a_re, m_a_im, m_log_dt, m_b_re, m_b_im, m_c_re, m_c_im, m_d_skip, m_w_glu, m_b_glu, m_g_attn_out, m_g_ssm_out, m_w_o, m_g_post_mix, m_g_pre_ffn, m_w_gate, m_w_up, m_w_down, m_g_post_ffn, v_g_pre_mix, v_w_in, v_sinks, v_a_re, v_a_im, v_log_dt, v_b_re, v_b_im, v_c_re, v_c_im, v_d_skip, v_w_glu, v_b_glu, v_g_attn_out, v_g_ssm_out, v_w_o, v_g_post_mix, v_g_pre_ffn, v_w_gate, v_w_up, v_w_down, v_g_post_ffn):
    given = dict(locals())
    weights = {n: given[n] for n in _ORDER}
    mom_m = {n: given["m_" + n] for n in _ORDER}
    mom_v = {n: given["v_" + n] for n in _ORDER}

    t, d = x.shape[1], x.shape[2]
    d_attn = d // 2
    d_ssm = d - d_attn
    d_in = d_attn + 2 * D_KV + d_ssm
    n_groups = d_ssm // SSM_GROUP
    n_heads = d_attn // HEAD_DIM
    tm = min(256, t)

    x2 = x[0]
    target = loss_target[0]

    def start_gather(name, ws, token):
        behind = 0 if token is None else token[0, 0].astype(BF16)
        return _exchange_start(name, [w[0].astype(BF16) + behind for w in ws], False)

    ag_in, token = start_gather("gather_w_in", [w_in], None)
    ag_mix, token = start_gather("gather_w_glu_o", [w_glu, w_o], token)
    ag_ffn_in, token = start_gather("gather_w_gate_up", [w_gate, w_up], token)
    ag_down, token = start_gather("gather_w_down", [w_down], token)

    xn, = _rows("norm_in", lambda xv, g: ([_rms(xv)[0] * g], []), [x2], [g_pre_mix], [(d, BF16)], [], tm,
                after=[token])
    win_g, = _exchange_wait(ag_in, [xn])
    w_in_full = win_g.transpose(1, 0, 2).reshape(d, d_in)
    proj = _mm_nn("proj_in", xn, w_in_full, F32, tn=d_in // 4 if (d_in // 4) % LANES == 0 else None)

    cos, sin = _rope_tables(positions.reshape(t, 1).astype(F32))
    sinks_row = jnp.pad(sinks, ((0, 0), (0, LANES - n_heads)))
    attn = _attention_fwd(proj, cos, sin, sinks_row, d_attn)

    b_re_t, b_im_t = b_re[0].transpose(2, 0, 1), b_im[0].transpose(2, 0, 1)
    ldt_col = log_dt.reshape(n_groups, 1)
    lam_re, lam_im, bbar_re, bbar_im = _s5_discretise(a_re[0], a_im[0], ldt_col, b_re_t, b_im_t)
    n_blocks = n_groups // GROUPS_PER_BLOCK
    mats = [_block_diag_in(bbar_re).astype(BF16), _block_diag_in(bbar_im).astype(BF16),
            lam_re.reshape(n_blocks, 1, SSM_ST_BLOCK), lam_im.reshape(n_blocks, 1, SSM_ST_BLOCK),
            _block_diag_out(c_re[0]).astype(BF16), _block_diag_out(c_im[0]).astype(BF16)]
    dskip_row = d_skip.reshape(1, d_ssm)
    y_ssm, z_ssm = _s5_fwd(proj, mats, dskip_row, d_attn, d_ssm)
    wglu_g, wo_g = _exchange_wait(ag_mix, [attn, z_ssm])
    w_glu_full = wglu_g.reshape(d_ssm, d_ssm)
    w_o_full = wo_g.reshape(d, d)
    glu_lin = _mm_nn("glu_gate", z_ssm, w_glu_full, F32)

    def mix_prep(av, yv, gl, bg, ga, gs):
        ssm = _gelu(yv) * _sigmoid(gl + bg)
        return [jnp.concatenate([_rms(av)[0] * ga, _rms(ssm)[0] * gs], axis=1)], []

    mixed, = _rows("mix_prep", mix_prep, [attn, y_ssm, glu_lin], [b_glu, g_attn_out, g_ssm_out], [(d, BF16)], [], tm)
    mix = _mm_nn("mix_out", mixed, w_o_full, F32, tn=d // 2 if (d // 2) % LANES == 0 else None)

    def post_mix(xv, mv, gpm, gpf):
        h = xv + _rms(mv)[0] * gpm
        return [h, _rms(h)[0] * gpf], []

    h, hn = _rows("post_mix", post_mix, [x2, mix], [g_post_mix, g_pre_ffn], [(d, F32), (d, BF16)], [], tm)
    wgate_g, wup_g = _exchange_wait(ag_ffn_in, [hn])
    f_sh = wgate_g.shape[2]
    gate = _mm_nn_slots("ffn_gate", hn, wgate_g, F32)
    up = _mm_nn_slots("ffn_up", hn, wup_g, F32)
    gate2, up2 = gate.reshape(N_DEV * t, f_sh), up.reshape(N_DEV * t, f_sh)
    hid2, = _rows("ffn_act", lambda gv, uv: ([gv * _sigmoid(gv) * uv], []), [gate2, up2], [], [(f_sh, BF16)], [], 512)
    hid = hid2.reshape(N_DEV, t, f_sh)
    wdown_g, = _exchange_wait(ag_down, [hid2])
    ff = _mm_contract_slots("ffn_down", [(hid, wdown_g)], F32)

    def head(hv, fv, tv, gpo):
        out = hv + _rms(fv)[0] * gpo
        err = out - tv
        dout = err * (1.0 / d)
        dff, dg = _rms_bwd(fv, gpo, dout)
        loss = jnp.zeros((1, LANES), F32) + 0.5 * jnp.sum(err * err) * (1.0 / d)
        return [dff, dout], [dg, loss]

    dff, dh_out, dg_post_ffn, loss_row = _rows("loss_head", head, [h, ff, target], [g_post_ffn],
                                               [(d, BF16), (d, F32)], [d, LANES], tm)

    dhid = _mm_nt_slots("ffn_down_dx", dff, wdown_g, BF16)
    dw_down = _mm_slots_tn("ffn_down_dw", hid, dff, BF16)
    rs_down, tok_down = _exchange_start("scatter_dw_down", [dw_down], True)

    def ffn_bwd(dh_, gv, uv):
        sg = _sigmoid(gv)
        dh32 = dh_.astype(F32)
        return [dh32 * uv * sg * (1.0 + gv * (1.0 - sg)), dh32 * gv * sg], []

    dgate2, dup2 = _rows("ffn_act_bwd", ffn_bwd, [dhid.reshape(N_DEV * t, f_sh), gate2, up2], [],
                         [(f_sh, BF16), (f_sh, BF16)], [], 512, after=[tok_down])
    dgate, dup = dgate2.reshape(N_DEV, t, f_sh), dup2.reshape(N_DEV, t, f_sh)
    dhn = _mm_contract_slots_nt("ffn_in_dx", [(dgate, wgate_g), (dup, wup_g)], F32)
    dw_gate = _mm_tn_slots("ffn_gate_dw", hn, dgate, BF16)
    dw_up = _mm_tn_slots("ffn_up_dw", hn, dup, BF16)
    rs_ffn_in, tok_ffn_in = _exchange_start("scatter_dw_gate_up", [dw_gate, dw_up], True)

    def mid_bwd(dho, dhn_, hv, mv, gpf, gpm):
        d1, dgpf = _rms_bwd(hv, gpf, dhn_)
        dh_ = dho + d1
        dmix_, dgpm = _rms_bwd(mv, gpm, dh_)
        return [dh_, dmix_], [dgpf, dgpm]

    dh, dmix, dg_pre_ffn, dg_post_mix = _rows("mid_bwd", mid_bwd, [dh_out, dhn, h, mix], [g_pre_ffn, g_post_mix],
                                              [(d, F32), (d, BF16)], [d, d], tm, after=[tok_ffn_in])

    dmixed = _mm_nt("mix_out_dx", dmix, w_o_full, F32, tn=d // 2 if (d // 2) % LANES == 0 else None)
    dw_o = _mm_tn("mix_out_dw", mixed, dmix, BF16, tn=d // 2 if (d // 2) % LANES == 0 else None)
    rs_o, tok_o = _exchange_start("scatter_dw_o", [dw_o.reshape(N_DEV, d // N_DEV, d)], True)

    def mix_bwd(dm, av, yv, gl, bg, ga, gs):
        dattn_, dga = _rms_bwd(av, ga, dm[:, :d_attn])
        z = _gelu(yv)
        sg = _sigmoid(gl + bg)
        dssm, dgs = _rms_bwd(z * sg, gs, dm[:, d_attn:])
        dgl = dssm * z * sg * (1.0 - sg)
        return [dattn_, dssm * sg, dgl], [dga, dgs, jnp.sum(dgl, axis=0, keepdims=True)]

    dattn, dz_direct, dglu, dg_attn_out, dg_ssm_out, db_glu = _rows(
        "mix_bwd", mix_bwd, [dmixed, attn, y_ssm, glu_lin], [b_glu, g_attn_out, g_ssm_out],
        [(d_attn, F32), (d_ssm, F32), (d_ssm, BF16)], [d_attn, d_ssm, d_ssm], tm, after=[tok_o])
    dz_glu = _mm_nt("glu_gate_dx", dglu, w_glu_full, F32)
    dw_glu = _mm_tn("glu_gate_dw", z_ssm, dglu, BF16)

    du, db_re_dense, db_im_dense, dlam_re, dlam_im, dc_re_dense, dc_im_dense, dd_skip = _s5_bwd(
        proj, mats, dskip_row, y_ssm, dz_direct, dz_glu, d_attn, d_ssm)
    da_re, da_im, dlog_dt, db_re_t, db_im_t = _s5_discretise_bwd(
        a_re[0], a_im[0], ldt_col, b_re_t, b_im_t, dlam_re.reshape(n_groups, SSM_STATE),
        dlam_im.reshape(n_groups, SSM_STATE), _block_diag_in_t(db_re_dense), _block_diag_in_t(db_im_dense))
    dq, dk2, dv2, dsinks_row = _attention_bwd(proj, cos, sin, sinks_row, dattn, d_attn)
    dproj = _assemble_dproj(dq, dk2, dv2, du, d_in)

    dxn = _mm_nt("proj_in_dx", dproj, w_in_full, F32, tn=d // 2 if (d // 2) % LANES == 0 else None)
    c_sh = d_in // N_DEV
    dproj_sh = dproj.reshape(t, N_DEV, c_sh).transpose(1, 0, 2)
    dw_in = _mm_tn_slots("proj_in_dw", xn, dproj_sh, BF16)

    def x_bwd(dh_, dxn_, xv, g):
        dx, dg = _rms_bwd(xv, g, dxn_)
        return [dh_ + dx], [dg]

    grad_x, dg_pre_mix = _rows("norm_in_bwd", x_bwd, [dh, dxn, x2], [g_pre_mix], [(d, F32)], [d], tm)

    small_grads = {
        "g_pre_mix": dg_pre_mix, "sinks": dsinks_row[:, :n_heads], "a_re": da_re[None], "a_im": da_im[None],
        "log_dt": dlog_dt.reshape(1, n_groups), "b_re": db_re_t.transpose(1, 2, 0)[None],
        "b_im": db_im_t.transpose(1, 2, 0)[None], "c_re": _block_diag_out_t(dc_re_dense)[None],
        "c_im": _block_diag_out_t(dc_im_dense)[None], "d_skip": dd_skip.reshape(d_skip.shape), "b_glu": db_glu,
        "g_attn_out": dg_attn_out, "g_ssm_out": dg_ssm_out, "g_post_mix": dg_post_mix, "g_pre_ffn": dg_pre_ffn,
        "g_post_ffn": dg_post_ffn,
    }
    small_like = [weights[n] for n in _SMALL]
    packed = _pack([small_grads[n] for n in _SMALL])
    rs_in, token = _exchange_start("scatter_dw_in_glu", [dw_in, dw_glu.reshape(N_DEV, d_ssm // N_DEV, d_ssm)], True)
    ag_small, token = _exchange_start("gather_small_grads", [packed + token[:1, :]], False)

    results = {}

    def adam_big(n, parts):
        r = weights[n].shape[1]
        results[n] = _adam_sharded("adam_" + n, parts, weights[n][0], mom_m[n][0], mom_v[n][0],
                                   64 if r % 64 == 0 else r)
        return results[n][3]

    done = [grad_x, token]
    adam_big("w_down", _exchange_wait(rs_down, done)[0])
    p_gate, p_up = _exchange_wait(rs_ffn_in, done)
    done = [adam_big("w_gate", p_gate), adam_big("w_up", p_up), results["w_down"][3]]
    done = [adam_big("w_o", _exchange_wait(rs_o, done)[0])]
    p_in, p_glu = _exchange_wait(rs_in, done)
    done = [adam_big("w_in", p_in), adam_big("w_glu", p_glu)]
    small_all, = _exchange_wait(ag_small, done)
    g_s, d_s, m_s, v_s = _adam_sharded(
        "adam_small", small_all, _pack(small_like), _pack([mom_m[n] for n in _SMALL]),
        _pack([mom_v[n] for n in _SMALL]), packed.shape[0])
    for i, vals in enumerate(zip(*[_unpack(p, small_like) for p in (g_s, d_s, m_s, v_s)])):
        results[_SMALL[i]] = vals

    loss = lax.psum(loss_row[0, 0], ("x", "y", "c"))
    outs = [loss, grad_x[None]]
    for k in range(4):
        for n in _ORDER:
            val = results[n][k]
            outs.append(val[None] if n in _BIG else val)
    return tuple(outs)
```

```python
import math

import jax
import jax.numpy as jnp
from jax import lax
from jax.experimental import pallas as pl
from jax.experimental.pallas import tpu as pltpu

F32 = jnp.float32
BF16 = jnp.bfloat16

HEAD_DIM = 64
N_KV_HEADS = 4
D_KV = N_KV_HEADS * HEAD_DIM
WINDOW = 128
BLOCK = 128
ROPE_THETA = 10000.0
SSM_GROUP = 16
SSM_STATE = 64
GROUPS_PER_BLOCK = 8
SSM_CH_BLOCK = GROUPS_PER_BLOCK * SSM_GROUP
SSM_ST_BLOCK = GROUPS_PER_BLOCK * SSM_STATE
RMS_EPS = 1e-6
N_DEV = 8
LANES = 128
SUBLANES = 8
MASKED = -1e30

ADAM_LR = 0.001
ADAM_B1 = 0.9
ADAM_B2 = 0.999
ADAM_EPS = 1e-08
ADAM_WD = 0.01
ADAM_STEP = 10

VMEM_LIMIT_BYTES = 56 * 1024 * 1024


def _call(body, *, name, out_shape, in_specs, out_specs, grid=(), scratch_shapes=(), semantics=None, n_after=0):
    params = dict(vmem_limit_bytes=VMEM_LIMIT_BYTES)
    if semantics is not None:
        params["dimension_semantics"] = semantics
    n_in = len(in_specs)
    if n_after:
        inner = body

        def body(*refs):
            inner(*refs[:n_in], *refs[n_in + n_after:])

        in_specs = list(in_specs) + [pl.BlockSpec(memory_space=pl.ANY)] * n_after
    return pl.pallas_call(body, name=name, grid=grid, in_specs=in_specs, out_specs=out_specs, out_shape=out_shape,
                          scratch_shapes=scratch_shapes, compiler_params=pltpu.CompilerParams(**params))


def _sds(shape, dtype):
    return jax.ShapeDtypeStruct(tuple(shape), dtype)


def _dot(a, b, ca, cb):
    return lax.dot_general(a, b, (((ca,), (cb,)), ((), ())), preferred_element_type=F32)


def _rms(x):
    r = lax.rsqrt(jnp.mean(x * x, axis=-1, keepdims=True) + RMS_EPS)
    return x * r, r


def _rms_bwd(x, g, dy):
    xh, r = _rms(x)
    dxh = dy * g
    dx = r * (dxh - xh * jnp.mean(dxh * xh, axis=-1, keepdims=True))
    return dx, jnp.sum(dy * xh, axis=0, keepdims=True)


def _sigmoid(x):
    return 1.0 / (1.0 + jnp.exp(-x))


_GELU_C = math.sqrt(2.0 / math.pi)
_GELU_A = 0.044715


def _gelu(y):
    t = jnp.tanh(_GELU_C * (y + _GELU_A * y * y * y))
    return 0.5 * y * (1.0 + t)


def _gelu_grad(y):
    t = jnp.tanh(_GELU_C * (y + _GELU_A * y * y * y))
    return 0.5 * (1.0 + t) + 0.5 * y * (1.0 - t * t) * _GELU_C * (1.0 + 3.0 * _GELU_A * y * y)


def _rows(name, fn, row_ins, vec_ins, row_outs, acc_widths, tm, after=()):
    rows = row_ins[0].shape[0]
    assert rows % tm == 0, (name, rows, tm)
    n_row, n_vec, n_out, n_acc = len(row_ins), len(vec_ins), len(row_outs), len(acc_widths)

    def body(*refs):
        ins = [r[...] for r in refs[:n_row + n_vec]]
        outs = refs[n_row + n_vec:n_row + n_vec + n_out]
        accs = refs[n_row + n_vec + n_out:]
        row_vals, acc_vals = fn(*ins)
        for o, v in zip(outs, row_vals):
            o[...] = v.astype(o.dtype)
        if n_acc:
            @pl.when(pl.program_id(0) == 0)
            def _():
                for a in accs:
                    a[...] = jnp.zeros_like(a)
            for a, v in zip(accs, acc_vals):
                a[...] += v

    in_specs = [pl.BlockSpec((tm, a.shape[1]), lambda i: (i, 0)) for a in row_ins]
    in_specs += [pl.BlockSpec(v.shape, lambda i: (0, 0)) for v in vec_ins]
    out_specs = [pl.BlockSpec((tm, w), lambda i: (i, 0)) for w, _ in row_outs]
    out_specs += [pl.BlockSpec((1, w), lambda i: (0, 0)) for w in acc_widths]
    out_shape = [_sds((rows, w), dt) for w, dt in row_outs] + [_sds((1, w), F32) for w in acc_widths]
    return _call(body, name=name, grid=(rows // tm,), in_specs=in_specs, out_specs=out_specs, out_shape=out_shape,
                 semantics=("arbitrary",) if n_acc else ("parallel",), n_after=len(after))(*row_ins, *vec_ins, *after)


def _matmul(name, operands, in_specs, product, grid, out_shape, out_spec, acc_shape):
    nk = grid[-1]
    n_in = len(operands)

    def body(*refs):
        ins = [r[...] for r in refs[:n_in]]
        o_ref = refs[n_in]
        if nk == 1:
            o_ref[...] = product(*ins).astype(o_ref.dtype)
            return
        acc = refs[n_in + 1]
        k = pl.program_id(len(grid) - 1)

        @pl.when(k == 0)
        def _():
            acc[...] = jnp.zeros_like(acc)

        acc[...] += product(*ins)

        @pl.when(k == nk - 1)
        def _():
            o_ref[...] = acc[...].astype(o_ref.dtype)

    return _call(body, name=name, grid=grid, in_specs=in_specs, out_specs=out_spec, out_shape=out_shape,
                 scratch_shapes=[] if nk == 1 else [pltpu.VMEM(acc_shape, F32)],
                 semantics=("parallel",) * (len(grid) - 1) + ("arbitrary",))(*operands)


def _mm_nn(name, a, b, out_dtype, tm=512, tn=None):
    m, k = a.shape
    n = b.shape[1]
    tm, tn = min(tm, m), n if tn is None else tn
    return _matmul(name, [a, b],
                   [pl.BlockSpec((tm, k), lambda i, j, s: (i, 0)), pl.BlockSpec((k, tn), lambda i, j, s: (0, j))],
                   lambda x, y: _dot(x, y, 1, 0), (m // tm, n // tn, 1), _sds((m, n), out_dtype),
                   pl.BlockSpec((tm, tn), lambda i, j, s: (i, j)), (tm, tn))


def _mm_nt(name, a, b, out_dtype, tm=512, tn=None):
    m, k = a.shape
    n = b.shape[0]
    tm, tn = min(tm, m), n if tn is None else tn
    return _matmul(name, [a, b],
                   [pl.BlockSpec((tm, k), lambda i, j, s: (i, 0)), pl.BlockSpec((tn, k), lambda i, j, s: (j, 0))],
                   lambda x, y: _dot(x, y, 1, 1), (m // tm, n // tn, 1), _sds((m, n), out_dtype),
                   pl.BlockSpec((tm, tn), lambda i, j, s: (i, j)), (tm, tn))


def _mm_tn(name, a, b, out_dtype, tm=512, tn=None, tk=512):
    k, m = a.shape
    n = b.shape[1]
    tm, tk, tn = min(tm, m), min(tk, k), n if tn is None else tn
    return _matmul(name, [a, b],
                   [pl.BlockSpec((tk, tm), lambda i, j, s: (s, i)), pl.BlockSpec((tk, tn), lambda i, j, s: (s, j))],
                   lambda x, y: _dot(x, y, 0, 0), (m // tm, n // tn, k // tk), _sds((m, n), out_dtype),
                   pl.BlockSpec((tm, tn), lambda i, j, s: (i, j)), (tm, tn))


def _mm_contract_slots(name, pairs, out_dtype, tm=512, tn=2048):
    s_, m, k = pairs[0][0].shape
    n = pairs[0][1].shape[2]
    tm, tn = min(tm, m), min(tn, n)
    ops, specs = [], []
    for a, b in pairs:
        ops += [a, b]
        specs += [pl.BlockSpec((None, tm, k), lambda i, j, s: (s, i, 0)), pl.BlockSpec((None, k, tn), lambda i, j, s: (s, 0, j))]

    def product(*t):
        return sum(_dot(t[2 * p], t[2 * p + 1], 1, 0) for p in range(len(pairs)))

    return _matmul(name, ops, specs, product, (m // tm, n // tn, s_), _sds((m, n), out_dtype),
                   pl.BlockSpec((tm, tn), lambda i, j, s: (i, j)), (tm, tn))


def _mm_contract_slots_nt(name, pairs, out_dtype, tm=512, tn=2048):
    s_, m, k = pairs[0][0].shape
    n = pairs[0][1].shape[1]
    tm, tn = min(tm, m), min(tn, n)
    ops, specs = [], []
    for a, b in pairs:
        ops += [a, b]
        specs += [pl.BlockSpec((None, tm, k), lambda i, j, s: (s, i, 0)), pl.BlockSpec((None, tn, k), lambda i, j, s: (s, j, 0))]

    def product(*t):
        return sum(_dot(t[2 * p], t[2 * p + 1], 1, 1) for p in range(len(pairs)))

    return _matmul(name, ops, specs, product, (m // tm, n // tn, s_), _sds((m, n), out_dtype),
                   pl.BlockSpec((tm, tn), lambda i, j, s: (i, j)), (tm, tn))


def _mm_tn_slots(name, a, b, out_dtype, tm=2048, tk=512):
    k, m = a.shape
    s_, _, n = b.shape
    tm, tk = min(tm, m), min(tk, k)
    return _matmul(name, [a, b],
                   [pl.BlockSpec((tk, tm), lambda s, i, z: (z, i)), pl.BlockSpec((None, tk, n), lambda s, i, z: (s, z, 0))],
                   lambda x, y: _dot(x, y, 0, 0), (s_, m // tm, k // tk), _sds((s_, m, n), out_dtype),
                   pl.BlockSpec((None, tm, n), lambda s, i, z: (s, i, 0)), (tm, n))


def _mm_slots_tn(name, a, b, out_dtype, tn=2048, tk=512):
    s_, k, m = a.shape
    n = b.shape[1]
    tn, tk = min(tn, n), min(tk, k)
    return _matmul(name, [a, b],
                   [pl.BlockSpec((None, tk, m), lambda s, j, z: (s, z, 0)), pl.BlockSpec((tk, tn), lambda s, j, z: (z, j))],
                   lambda x, y: _dot(x, y, 0, 0), (s_, n // tn, k // tk), _sds((s_, m, n), out_dtype),
                   pl.BlockSpec((None, m, tn), lambda s, j, z: (s, 0, j)), (m, tn))


def _ffn_in(a, w_gate, w_up, tm=512):
    m, k = a.shape
    s_, _, n = w_gate.shape
    tm = min(tm, m)

    def body(a_ref, wg_ref, wu_ref, g_ref, u_ref, h_ref):
        x = a_ref[...]
        g = _dot(x, wg_ref[...], 1, 0)
        u = _dot(x, wu_ref[...], 1, 0)
        g_ref[...] = g.astype(BF16)
        u_ref[...] = u.astype(BF16)
        h_ref[...] = (g * _sigmoid(g) * u).astype(BF16)

    w_spec = pl.BlockSpec((None, k, n), lambda s, i: (s, 0, 0))
    o_spec = pl.BlockSpec((None, tm, n), lambda s, i: (s, i, 0))
    return _call(body, name="ffn_in", grid=(s_, m // tm),
                 in_specs=[pl.BlockSpec((tm, k), lambda s, i: (i, 0)), w_spec, w_spec], out_specs=[o_spec] * 3,
                 out_shape=[_sds((s_, m, n), BF16)] * 3, semantics=("parallel", "parallel"))(a, w_gate, w_up)


def _ffn_down_bwd(d_out, w_down, gate, up, after, tm=512):
    m, k = d_out.shape
    s_, n, _ = w_down.shape
    tm = min(tm, m)

    def body(d_ref, w_ref, g_ref, u_ref, dg_ref, du_ref):
        dh = _dot(d_ref[...], w_ref[...], 1, 1)
        g = g_ref[...].astype(F32)
        sg = _sigmoid(g)
        dg_ref[...] = (dh * u_ref[...].astype(F32) * sg * (1.0 + g * (1.0 - sg))).astype(BF16)
        du_ref[...] = (dh * g * sg).astype(BF16)

    t_spec = pl.BlockSpec((None, tm, n), lambda s, i: (s, i, 0))
    return _call(body, name="ffn_down_dx", grid=(s_, m // tm),
                 in_specs=[pl.BlockSpec((tm, k), lambda s, i: (i, 0)), pl.BlockSpec((None, n, k), lambda s, i: (s, 0, 0)),
                           t_spec, t_spec],
                 out_specs=[t_spec] * 2, out_shape=[_sds((s_, m, n), BF16)] * 2, semantics=("parallel", "parallel"),
                 n_after=len(after))(d_out, w_down, gate, up, *after)


def _exchange_copies(ins, lands, send_sems, recv_sems, scatter):
    x, y, c = lax.axis_index("x"), lax.axis_index("y"), lax.axis_index("c")
    me = 4 * x + 2 * y + c

    def flip(v, bit):
        return 1 - v if bit else v

    def copy(a, s, peer, pos, dst_slot):
        return pltpu.make_async_remote_copy(
            src_ref=ins[a].at[peer] if scatter else ins[a], dst_ref=lands[a].at[dst_slot],
            send_sem=send_sems.at[s], recv_sem=recv_sems.at[s], device_id=pos, device_id_type=pl.DeviceIdType.MESH)

    pairs = []
    for r in range(1, N_DEV):
        pos = (flip(x, r & 4), flip(y, r & 2), flip(c, r & 1))
        peer = 4 * pos[0] + 2 * pos[1] + pos[2]
        for a in range(len(ins)):
            s = a * (N_DEV - 1) + r - 1
            pairs.append((copy(a, s, peer, pos, me), copy(a, s, peer, pos, peer)))
    return me, pairs


def _exchange(name, arrays, scatter):
    n = len(arrays)

    def body(*refs):
        ins, outs = refs[:n], refs[n:2 * n]
        send_sems, recv_sems, local_sems = refs[2 * n:]
        me, pairs = _exchange_copies(ins, outs, send_sems, recv_sems, scatter)
        own = [pltpu.make_async_copy(ins[a].at[me] if scatter else ins[a], outs[a].at[me], local_sems.at[a])
               for a in range(n)]
        for cp in own:
            cp.start()
        for send, _ in pairs:
            send.start()
        for _, arrival in pairs:
            arrival.wait_recv()
        for send, _ in pairs:
            send.wait_send()
        for cp in own:
            cp.wait()

    any_spec = pl.BlockSpec(memory_space=pl.ANY)
    n_sem = n * (N_DEV - 1)
    out_shape = [_sds(a.shape if scatter else (N_DEV,) + a.shape, a.dtype) for a in arrays]
    return pl.pallas_call(
        body, name=name, in_specs=[any_spec] * n, out_specs=[any_spec] * n, out_shape=out_shape,
        scratch_shapes=[pltpu.SemaphoreType.DMA((n_sem,)), pltpu.SemaphoreType.DMA((n_sem,)),
                        pltpu.SemaphoreType.DMA((n,))],
    )(*arrays)


_HBM_SPEC = pl.BlockSpec(memory_space=pltpu.HBM)
_SEM_SPEC = pl.BlockSpec(memory_space=pltpu.SEMAPHORE)
_SIDE_EFFECT = pltpu.SideEffectType.DATAFLOW_SIDE_EFFECTING


def _exchange_start(name, arrays, scatter):
    n = len(arrays)
    n_sem = n * (N_DEV - 1)
    lands = [lax.empty(a.shape if scatter else (N_DEV,) + a.shape, a.dtype) for a in arrays]

    def body(*refs):
        ins, land_refs = refs[:n], refs[n:2 * n]
        send_sems, recv_sems, token = refs[2 * n], refs[2 * n + 1], refs[-1]
        _, pairs = _exchange_copies(ins, land_refs, send_sems, recv_sems, scatter)
        for send, _ in pairs:
            send.start()
        token[...] = jnp.zeros_like(token)

    out = pl.pallas_call(
        body, name=name,
        out_shape=(pltpu.SemaphoreType.DMA((n_sem,)), pltpu.SemaphoreType.DMA((n_sem,)),
                   *[pltpu.HBM(a.shape, a.dtype) for a in arrays + lands], _sds((SUBLANES, LANES), F32)),
        in_specs=[_HBM_SPEC] * (2 * n),
        out_specs=(_SEM_SPEC, _SEM_SPEC, *[_HBM_SPEC] * (2 * n), pl.BlockSpec(memory_space=pltpu.VMEM)),
        input_output_aliases={i: 2 + i for i in range(2 * n)},
        compiler_params=pltpu.CompilerParams(has_side_effects=_SIDE_EFFECT),
    )(*[pltpu.with_memory_space_constraint(a, pltpu.HBM) for a in arrays + lands])
    return dict(name=name, n=n, scatter=scatter, sems=out[:2], thru=out[2:2 + 2 * n]), out[-1]


def _exchange_wait(handle, after):
    n, scatter = handle["n"], handle["scatter"]
    thru = list(handle["thru"])

    def body(*refs):
        ins, land_refs = refs[:n], refs[n:2 * n]
        send_sems, recv_sems = refs[2 * n], refs[2 * n + 1]
        _, pairs = _exchange_copies(ins, land_refs, send_sems, recv_sems, scatter)
        for send, arrival in pairs:
            send.wait_send()
            arrival.wait_recv()

    out = pl.pallas_call(
        body, name=handle["name"] + "_wait", out_shape=[pltpu.HBM(a.shape, a.dtype) for a in thru],
        in_specs=[_HBM_SPEC] * (2 * n) + [_SEM_SPEC, _SEM_SPEC] + [pl.BlockSpec(memory_space=pl.ANY)] * len(after),
        out_specs=[_HBM_SPEC] * (2 * n), input_output_aliases={i: i for i in range(2 * n)},
        compiler_params=pltpu.CompilerParams(has_side_effects=_SIDE_EFFECT),
    )(*thru, *handle["sems"], *after)
    me = 4 * lax.axis_index("x") + 2 * lax.axis_index("y") + lax.axis_index("c")
    done = []
    for src, land in zip(out[:n], out[n:]):
        own = lax.dynamic_index_in_dim(src, me, 0, keepdims=True) if scatter else src[None]
        done.append(lax.dynamic_update_slice_in_dim(land, own, me, 0))
    return done


def _rope_tables(pos_col):
    t = pos_col.shape[0]
    half = HEAD_DIM // 2
    inv_freq = ROPE_THETA ** (-jnp.arange(half, dtype=F32) / half)
    inv_row = jnp.tile(inv_freq, LANES // half)[None, :]

    def body(pos_ref, inv_ref, cos_ref, sin_ref):
        ang = pos_ref[...] * inv_ref[...]
        cos_ref[...] = jnp.cos(ang)
        sin_ref[...] = jnp.sin(ang)

    tm = min(t, 512)
    return _call(body, name="rope_tables", grid=(t // tm,),
                 in_specs=[pl.BlockSpec((tm, 1), lambda i: (i, 0)), pl.BlockSpec((1, LANES), lambda i: (0, 0))],
                 out_specs=[pl.BlockSpec((tm, LANES), lambda i: (i, 0))] * 2,
                 out_shape=[_sds((t, LANES), F32)] * 2, semantics=("parallel",))(pos_col, inv_row)


def _rot_half(x):
    lane = lax.broadcasted_iota(jnp.int32, x.shape, 1)
    low = (lane % HEAD_DIM) < HEAD_DIM // 2
    return jnp.where(low, -pltpu.roll(x, LANES - HEAD_DIM // 2, 1), pltpu.roll(x, HEAD_DIM // 2, 1))


def _rope(x, cos, sin):
    return x * cos + _rot_half(x) * sin


def _unrope(d, cos, sin):
    return d * cos - _rot_half(d) * sin


def _band_mask(first_block, heads):
    r = lax.broadcasted_iota(jnp.int32, (heads * BLOCK, 2 * BLOCK), 0) % BLOCK
    c = lax.broadcasted_iota(jnp.int32, (heads * BLOCK, 2 * BLOCK), 1)
    diff = r - c + BLOCK
    return (diff >= 0) & (diff < WINDOW) & ((c >= BLOCK) | jnp.logical_not(first_block))


def _attn_specs(t, d_attn, d_in):
    kb, vb = d_attn // D_KV, d_attn // D_KV + 1
    prev = lambda i: jnp.maximum(i - 1, 0)
    return [
        pl.BlockSpec((BLOCK, d_attn), lambda i: (i, 0)),
        pl.BlockSpec((BLOCK, D_KV), lambda i: (i, kb)),
        pl.BlockSpec((BLOCK, D_KV), lambda i: (i, vb)),
        pl.BlockSpec((BLOCK, D_KV), lambda i: (prev(i), kb)),
        pl.BlockSpec((BLOCK, D_KV), lambda i: (prev(i), vb)),
        pl.BlockSpec((BLOCK, LANES), lambda i: (i, 0)),
        pl.BlockSpec((BLOCK, LANES), lambda i: (i, 0)),
        pl.BlockSpec((BLOCK, LANES), lambda i: (prev(i), 0)),
        pl.BlockSpec((BLOCK, LANES), lambda i: (prev(i), 0)),
        pl.BlockSpec((1, LANES), lambda i: (0, 0)),
    ]


def _head(x, h):
    return x[:, h * HEAD_DIM:(h + 1) * HEAD_DIM]


def _attn_heads(q_ref, kc_ref, vc_ref, kp_ref, vp_ref, cq_ref, sq_ref, cp_ref, sp_ref, d_attn):
    cq, sq, cp, sp = cq_ref[...], sq_ref[...], cp_ref[...], sp_ref[...]
    q_rot = [_rope(q_ref[:, j * LANES:(j + 1) * LANES], cq, sq) for j in range(d_attn // LANES)]
    kc_rot = [_rope(kc_ref[:, j * LANES:(j + 1) * LANES], cq, sq) for j in range(D_KV // LANES)]
    kp_rot = [_rope(kp_ref[:, j * LANES:(j + 1) * LANES], cp, sp) for j in range(D_KV // LANES)]
    per = LANES // HEAD_DIM
    q_heads = [_head(q_rot[h // per], h % per).astype(BF16) for h in range(d_attn // HEAD_DIM)]
    kk = [jnp.concatenate([_head(kp_rot[g // per], g % per), _head(kc_rot[g // per], g % per)], axis=0).astype(BF16)
          for g in range(N_KV_HEADS)]
    vv = [jnp.concatenate([_head(vp_ref[...], g), _head(vc_ref[...], g)], axis=0).astype(BF16) for g in range(N_KV_HEADS)]
    return q_heads, kk, vv


def _stack_group(q_heads, sink_ref, group):
    q_all = jnp.concatenate([q_heads[h] for h in group], axis=0)
    sink_all = jnp.concatenate([jnp.broadcast_to(sink_ref[:, h:h + 1], (BLOCK, 1)) for h in group], axis=0)
    return q_all, sink_all


def _softmax_with_sink(q, kk, sink, mask):
    s = _dot(q, kk, 1, 1) * (1.0 / math.sqrt(HEAD_DIM))
    s = jnp.where(mask, s, MASKED)
    m = jnp.maximum(jnp.max(s, axis=-1, keepdims=True), sink)
    p = jnp.exp(s - m)
    e_sink = jnp.exp(sink - m)
    inv = 1.0 / (jnp.sum(p, axis=-1, keepdims=True) + e_sink)
    return p * inv, e_sink * inv


def _attention_fwd(proj, cos, sin, sinks_row, d_attn):
    t, d_in = proj.shape
    n_heads = d_attn // HEAD_DIM
    q_per_kv = n_heads // N_KV_HEADS

    def body(q_ref, kc_ref, vc_ref, kp_ref, vp_ref, cq_ref, sq_ref, cp_ref, sp_ref, sink_ref, o_ref):
        mask = _band_mask(pl.program_id(0) == 0, q_per_kv)
        q_heads, kk, vv = _attn_heads(q_ref, kc_ref, vc_ref, kp_ref, vp_ref, cq_ref, sq_ref, cp_ref, sp_ref, d_attn)
        for g in range(N_KV_HEADS):
            group = range(g * q_per_kv, (g + 1) * q_per_kv)
            q_all, sink_all = _stack_group(q_heads, sink_ref, group)
            probs, _ = _softmax_with_sink(q_all, kk[g], sink_all, mask)
            o_all = _dot(probs.astype(BF16), vv[g], 1, 0)
            for k, h in enumerate(group):
                o_ref[:, h * HEAD_DIM:(h + 1) * HEAD_DIM] = o_all[k * BLOCK:(k + 1) * BLOCK]

    return _call(body, name="attention_fwd", grid=(t // BLOCK,), in_specs=_attn_specs(t, d_attn, d_in),
                 out_specs=pl.BlockSpec((BLOCK, d_attn), lambda i: (i, 0)), out_shape=_sds((t, d_attn), F32),
                 semantics=("parallel",))(proj, proj, proj, proj, proj, cos, sin, cos, sin, sinks_row)


def _attention_bwd(proj, cos, sin, sinks_row, d_out, d_attn):
    t, d_in = proj.shape
    n_heads = d_attn // HEAD_DIM
    q_per_kv = n_heads // N_KV_HEADS
    nb = t // BLOCK
    per = LANES // HEAD_DIM

    def body(q_ref, kc_ref, vc_ref, kp_ref, vp_ref, cq_ref, sq_ref, cp_ref, sp_ref, sink_ref, do_ref,
             dq_ref, dk_ref, dv_ref, dsink_ref):
        i = pl.program_id(0)
        mask = _band_mask(i == 0, q_per_kv)
        q_heads, kk, vv = _attn_heads(q_ref, kc_ref, vc_ref, kp_ref, vp_ref, cq_ref, sq_ref, cp_ref, sp_ref, d_attn)
        lane = lax.broadcasted_iota(jnp.int32, (1, LANES), 1)
        dsink = jnp.zeros((1, LANES), F32)
        dq_rot, dkk, dvv = [], [], []
        for g in range(N_KV_HEADS):
            group = range(g * q_per_kv, (g + 1) * q_per_kv)
            q_all, sink_all = _stack_group(q_heads, sink_ref, group)
            probs, p_sink = _softmax_with_sink(q_all, kk[g], sink_all, mask)
            do_all = jnp.concatenate([do_ref[:, h * HEAD_DIM:(h + 1) * HEAD_DIM] for h in group], axis=0).astype(BF16)
            dp = _dot(do_all, vv[g], 1, 1)
            delta = jnp.sum(probs * dp, axis=-1, keepdims=True)
            ds = (probs * (dp - delta) * (1.0 / math.sqrt(HEAD_DIM))).astype(BF16)
            dq_all = _dot(ds, kk[g], 1, 0)
            dkk.append(_dot(ds, q_all, 0, 0))
            dvv.append(_dot(probs.astype(BF16), do_all, 0, 0))
            sink_term = p_sink * delta
            for k, h in enumerate(group):
                dq_rot.append(dq_all[k * BLOCK:(k + 1) * BLOCK])
                part = jnp.sum(sink_term[k * BLOCK:(k + 1) * BLOCK], axis=0, keepdims=True)
                dsink += jnp.where(lane == h, -part, 0.0)
        cq, sq, cp, sp = cq_ref[...], sq_ref[...], cp_ref[...], sp_ref[...]
        for j in range(d_attn // LANES):
            d = jnp.concatenate(dq_rot[j * per:(j + 1) * per], axis=1)
            dq_ref[:, j * LANES:(j + 1) * LANES] = _unrope(d, cq, sq)
        for j in range(D_KV // LANES):
            d = jnp.concatenate(dkk[j * per:(j + 1) * per], axis=1)
            dk_ref[0, :, j * LANES:(j + 1) * LANES] = _unrope(d[:BLOCK], cp, sp)
            dk_ref[1, :, j * LANES:(j + 1) * LANES] = _unrope(d[BLOCK:], cq, sq)
            d = jnp.concatenate(dvv[j * per:(j + 1) * per], axis=1)
            dv_ref[0, :, j * LANES:(j + 1) * LANES] = d[:BLOCK]
            dv_ref[1, :, j * LANES:(j + 1) * LANES] = d[BLOCK:]

        @pl.when(i == 0)
        def _():
            dsink_ref[...] = jnp.zeros_like(dsink_ref)

        dsink_ref[...] += dsink

    pair = pl.BlockSpec((2, BLOCK, D_KV), lambda i: (i, 0, 0))
    return _call(body, name="attention_bwd", grid=(nb,),
                 in_specs=_attn_specs(t, d_attn, d_in) + [pl.BlockSpec((BLOCK, d_attn), lambda i: (i, 0))],
                 out_specs=[pl.BlockSpec((BLOCK, d_attn), lambda i: (i, 0)), pair, pair,
                            pl.BlockSpec((1, LANES), lambda i: (0, 0))],
                 out_shape=[_sds((t, d_attn), F32), _sds((2 * nb, BLOCK, D_KV), F32), _sds((2 * nb, BLOCK, D_KV), F32),
                            _sds((1, LANES), F32)],
                 semantics=("arbitrary",))(proj, proj, proj, proj, proj, cos, sin, cos, sin, sinks_row, d_out)


def _assemble_dproj(dq, dk2, dv2, du, d_in):
    t, d_attn = dq.shape
    d_ssm = du.shape[1]
    nb = t // BLOCK

    def body(dq_ref, dk_own, dk_next, dv_own, dv_next, du_ref, o_ref):
        has_next = (pl.program_id(0) < nb - 1).astype(F32)
        o_ref[:, :d_attn] = dq_ref[...].astype(BF16)
        o_ref[:, d_attn:d_attn + D_KV] = (dk_own[...] + has_next * dk_next[...]).astype(BF16)
        o_ref[:, d_attn + D_KV:d_attn + 2 * D_KV] = (dv_own[...] + has_next * dv_next[...]).astype(BF16)
        o_ref[:, d_attn + 2 * D_KV:] = du_ref[...].astype(BF16)

    own = pl.BlockSpec((None, BLOCK, D_KV), lambda i: (2 * i + 1, 0, 0))
    nxt = pl.BlockSpec((None, BLOCK, D_KV), lambda i: (jnp.minimum(2 * i + 2, 2 * nb - 1), 0, 0))
    return _call(body, name="assemble_dproj", grid=(nb,),
                 in_specs=[pl.BlockSpec((BLOCK, d_attn), lambda i: (i, 0)), own, nxt, own, nxt,
                           pl.BlockSpec((BLOCK, d_ssm), lambda i: (i, 0))],
                 out_specs=pl.BlockSpec((BLOCK, d_in), lambda i: (i, 0)), out_shape=_sds((t, d_in), BF16),
                 semantics=("parallel",))(dq, dk2, dk2, dv2, dv2, du)


def _discretise(ar, ai, ldt, br, bi):
    dt = jnp.exp(ldt)
    mag = jnp.exp(ar * dt)
    lam_re = mag * jnp.cos(ai * dt)
    lam_im = mag * jnp.sin(ai * dt)
    den = ar * ar + ai * ai
    nr = lam_re - 1.0
    ni = lam_im
    f_re = (nr * ar + ni * ai) / den
    f_im = (ni * ar - nr * ai) / den
    return lam_re, lam_im, f_re[None] * br - f_im[None] * bi, f_re[None] * bi + f_im[None] * br


def _whole(arrays):
    return [pl.BlockSpec(a.shape, lambda *_, nd=len(a.shape): (0,) * nd) for a in arrays]


def _s5_discretise(ar, ai, ldt, br, bi):
    ins = [ar, ai, ldt, br, bi]

    def body(ar_ref, ai_ref, ldt_ref, br_ref, bi_ref, lr_ref, li_ref, bbr_ref, bbi_ref):
        out = _discretise(ar_ref[...], ai_ref[...], ldt_ref[...], br_ref[...], bi_ref[...])
        for ref, val in zip((lr_ref, li_ref, bbr_ref, bbi_ref), out):
            ref[...] = val

    outs = [_sds(ar.shape, F32), _sds(ar.shape, F32), _sds(br.shape, F32), _sds(br.shape, F32)]
    return _call(body, name="s5_discretise", in_specs=_whole(ins), out_specs=_whole(outs), out_shape=outs)(*ins)


def _s5_discretise_bwd(ar, ai, ldt, br, bi, d_lr, d_li, d_bbr, d_bbi):
    ins = [ar, ai, ldt, br, bi, d_lr, d_li, d_bbr, d_bbi]

    def body(ar_ref, ai_ref, ldt_ref, br_ref, bi_ref, dlr_ref, dli_ref, dbbr_ref, dbbi_ref, *out_refs):
        _, vjp = jax.vjp(_discretise, ar_ref[...], ai_ref[...], ldt_ref[...], br_ref[...], bi_ref[...])
        grads = vjp((dlr_ref[...], dli_ref[...], dbbr_ref[...], dbbi_ref[...]))
        for ref, val in zip(out_refs, grads):
            ref[...] = val

    outs = [_sds(a.shape, F32) for a in (ar, ai, ldt, br, bi)]
    return _call(body, name="s5_discretise_bwd", in_specs=_whole(ins), out_specs=_whole(outs), out_shape=outs)(*ins)


def _cmul(ar, ai, br, bi):
    return ar * br - ai * bi, ar * bi + ai * br


def _power_table(lr, li, reverse):
    pows = [(lr, li)]
    for _ in range(SUBLANES - 1):
        pows.append(_cmul(pows[-1][0], pows[-1][1], lr, li))
    row = lax.broadcasted_iota(jnp.int32, (SUBLANES, lr.shape[1]), 0)
    tr = jnp.zeros((SUBLANES, lr.shape[1]), F32)
    ti = jnp.zeros((SUBLANES, lr.shape[1]), F32)
    for r in range(SUBLANES):
        src = pows[SUBLANES - 1 - r] if reverse else pows[r]
        tr = jnp.where(row == r, src[0], tr)
        ti = jnp.where(row == r, src[1], ti)
    return pows[0], pows[1], pows[3], (tr, ti)


def _scan_tile(xr, xi, steps, table, carry, reverse):
    row = lax.broadcasted_iota(jnp.int32, xr.shape, 0)
    for k, (lr, li) in zip((1, 2, 4), steps):
        if reverse:
            keep = row < SUBLANES - k
            sr = jnp.where(keep, pltpu.roll(xr, SUBLANES - k, 0), 0.0)
            si = jnp.where(keep, pltpu.roll(xi, SUBLANES - k, 0), 0.0)
        else:
            keep = row >= k
            sr = jnp.where(keep, pltpu.roll(xr, k, 0), 0.0)
            si = jnp.where(keep, pltpu.roll(xi, k, 0), 0.0)
        pr, pi = _cmul(lr, li, sr, si)
        xr, xi = xr + pr, xi + pi
    pr, pi = _cmul(table[0], table[1], carry[0], carry[1])
    return xr + pr, xi + pi


def _scan(sr_ref, si_ref, lr, li, reverse, t, per_tile=None):
    l1, l2, l4, table = _power_table(lr, li, reverse)
    n_tiles = t // SUBLANES
    w = lr.shape[1]

    def step(i, carry):
        tile = (n_tiles - 1 - i) if reverse else i
        rows = pl.ds(pl.multiple_of(tile * SUBLANES, SUBLANES), SUBLANES)
        xr, xi = _scan_tile(sr_ref[rows, :], si_ref[rows, :], (l1, l2, l4), table, carry, reverse)
        sr_ref[rows, :] = xr
        si_ref[rows, :] = xi
        if per_tile is not None:
            per_tile(tile, xr, xi)
        edge = 0 if reverse else SUBLANES - 1
        return xr[edge:edge + 1, :], xi[edge:edge + 1, :]

    lax.fori_loop(0, n_tiles, step, (jnp.zeros((1, w), F32), jnp.zeros((1, w), F32)))


_S5_ROWS = 256


def _s5_in_specs(t, d_attn):
    u_block = (d_attn + 2 * D_KV) // SSM_CH_BLOCK
    blk3 = lambda shape: pl.BlockSpec((None,) + shape, lambda j: (j, 0, 0))
    return [
        pl.BlockSpec((t, SSM_CH_BLOCK), lambda j: (0, u_block + j)),
        blk3((SSM_CH_BLOCK, SSM_ST_BLOCK)), blk3((SSM_CH_BLOCK, SSM_ST_BLOCK)),
        blk3((1, SSM_ST_BLOCK)), blk3((1, SSM_ST_BLOCK)),
        blk3((SSM_ST_BLOCK, SSM_CH_BLOCK)), blk3((SSM_ST_BLOCK, SSM_CH_BLOCK)),
        pl.BlockSpec((1, SSM_CH_BLOCK), lambda j: (0, j)),
    ]


def _s5_states(u_ref, bre_ref, bim_ref, lr_ref, li_ref, sr_ref, si_ref, t):
    def fill(i, _):
        rows = pl.ds(pl.multiple_of(i * _S5_ROWS, _S5_ROWS), _S5_ROWS)
        ub = u_ref[rows, :].astype(BF16)
        sr_ref[rows, :] = _dot(ub, bre_ref[...], 1, 0)
        si_ref[rows, :] = _dot(ub, bim_ref[...], 1, 0)
        return 0

    lax.fori_loop(0, t // _S5_ROWS, fill, 0)
    _scan(sr_ref, si_ref, lr_ref[...], li_ref[...], False, t)


def _s5_fwd(proj, mats, dskip_row, d_attn, d_ssm):
    t = proj.shape[0]
    n_blocks = d_ssm // SSM_CH_BLOCK

    def body(u_ref, bre_ref, bim_ref, lr_ref, li_ref, cre_ref, cim_ref, d_ref, y_ref, z_ref, sr_ref, si_ref):
        _s5_states(u_ref, bre_ref, bim_ref, lr_ref, li_ref, sr_ref, si_ref, t)

        def emit(i, _):
            rows = pl.ds(pl.multiple_of(i * _S5_ROWS, _S5_ROWS), _S5_ROWS)
            y = (_dot(sr_ref[rows, :].astype(BF16), cre_ref[...], 1, 0)
                 - _dot(si_ref[rows, :].astype(BF16), cim_ref[...], 1, 0) + d_ref[...] * u_ref[rows, :])
            y_ref[rows, :] = y
            z_ref[rows, :] = _gelu(y).astype(BF16)
            return 0

        lax.fori_loop(0, t // _S5_ROWS, emit, 0)

    col = pl.BlockSpec((t, SSM_CH_BLOCK), lambda j: (0, j))
    return _call(body, name="s5_fwd", grid=(n_blocks,), in_specs=_s5_in_specs(t, d_attn), out_specs=[col, col],
                 out_shape=[_sds((t, d_ssm), F32), _sds((t, d_ssm), BF16)],
                 scratch_shapes=[pltpu.VMEM((t, SSM_ST_BLOCK), F32)] * 2,
                 semantics=("parallel",))(proj, *mats, dskip_row)


def _s5_bwd(proj, mats, dskip_row, y, dz_a, dz_b, d_attn, d_ssm):
    t = proj.shape[0]
    n_blocks = d_ssm // SSM_CH_BLOCK

    def body(u_ref, bre_ref, bim_ref, lr_ref, li_ref, cre_ref, cim_ref, d_ref, y_ref, dza_ref, dzb_ref,
             du_ref, dbre_ref, dbim_ref, dlr_ref, dli_ref, dcre_ref, dcim_ref, dd_ref,
             sr_ref, si_ref, gr_ref, gi_ref, dy_ref, acc_r, acc_i):
        _s5_states(u_ref, bre_ref, bim_ref, lr_ref, li_ref, sr_ref, si_ref, t)
        for ref in (dcre_ref, dcim_ref, dbre_ref, dbim_ref, dd_ref, acc_r, acc_i):
            ref[...] = jnp.zeros_like(ref)

        def through_c(i, _):
            rows = pl.ds(pl.multiple_of(i * _S5_ROWS, _S5_ROWS), _S5_ROWS)
            dy = (dza_ref[rows, :] + dzb_ref[rows, :]) * _gelu_grad(y_ref[rows, :])
            dy_ref[rows, :] = dy
            dd_ref[...] += jnp.sum(dy * u_ref[rows, :], axis=0, keepdims=True)
            dyb = dy.astype(BF16)
            gr_ref[rows, :] = _dot(dyb, cre_ref[...], 1, 1)
            gi_ref[rows, :] = -_dot(dyb, cim_ref[...], 1, 1)
            dcre_ref[...] += _dot(sr_ref[rows, :].astype(BF16), dyb, 0, 0)
            dcim_ref[...] -= _dot(si_ref[rows, :].astype(BF16), dyb, 0, 0)
            return 0

        lax.fori_loop(0, t // _S5_ROWS, through_c, 0)

        def lambda_grad(tile, g_re, g_im):
            rows = pl.ds(pl.multiple_of(tile * SUBLANES, SUBLANES), SUBLANES)
            before = pl.ds(pl.multiple_of(jnp.maximum(tile - 1, 0) * SUBLANES, SUBLANES), SUBLANES)
            row = lax.broadcasted_iota(jnp.int32, g_re.shape, 0)
            live = jnp.where(tile > 0, 1.0, 0.0)
            prev = []
            for ref in (sr_ref, si_ref):
                here = pltpu.roll(ref[rows, :], 1, 0)
                last = pltpu.roll(ref[before, :], 1, 0) * live
                prev.append(jnp.where(row == 0, last, here))
            acc_r[...] += g_re * prev[0] + g_im * prev[1]
            acc_i[...] += g_im * prev[0] - g_re * prev[1]

        _scan(gr_ref, gi_ref, lr_ref[...], -li_ref[...], True, t, per_tile=lambda_grad)
        dlr_ref[...] = jnp.sum(acc_r[...], axis=0, keepdims=True)
        dli_ref[...] = jnp.sum(acc_i[...], axis=0, keepdims=True)

        def through_b(i, _):
            rows = pl.ds(pl.multiple_of(i * _S5_ROWS, _S5_ROWS), _S5_ROWS)
            ub = u_ref[rows, :].astype(BF16)
            grb, gib = gr_ref[rows, :].astype(BF16), gi_ref[rows, :].astype(BF16)
            dbre_ref[...] += _dot(ub, grb, 0, 0)
            dbim_ref[...] += _dot(ub, gib, 0, 0)
            du_ref[rows, :] = _dot(grb, bre_ref[...], 1, 1) + _dot(gib, bim_ref[...], 1, 1) + d_ref[...] * dy_ref[rows, :]
            return 0

        lax.fori_loop(0, t // _S5_ROWS, through_b, 0)

    col = pl.BlockSpec((t, SSM_CH_BLOCK), lambda j: (0, j))
    blk3 = lambda shape: pl.BlockSpec((None,) + shape, lambda j: (j, 0, 0))
    state = pltpu.VMEM((t, SSM_ST_BLOCK), F32)
    return _call(
        body, name="s5_bwd", grid=(n_blocks,), in_specs=_s5_in_specs(t, d_attn) + [col, col, col],
        out_specs=[col, blk3((SSM_CH_BLOCK, SSM_ST_BLOCK)), blk3((SSM_CH_BLOCK, SSM_ST_BLOCK)),
                   blk3((1, SSM_ST_BLOCK)), blk3((1, SSM_ST_BLOCK)),
                   blk3((SSM_ST_BLOCK, SSM_CH_BLOCK)), blk3((SSM_ST_BLOCK, SSM_CH_BLOCK)),
                   pl.BlockSpec((1, SSM_CH_BLOCK), lambda j: (0, j))],
        out_shape=[_sds((t, d_ssm), F32),
                   _sds((n_blocks, SSM_CH_BLOCK, SSM_ST_BLOCK), F32), _sds((n_blocks, SSM_CH_BLOCK, SSM_ST_BLOCK), F32),
                   _sds((n_blocks, 1, SSM_ST_BLOCK), F32), _sds((n_blocks, 1, SSM_ST_BLOCK), F32),
                   _sds((n_blocks, SSM_ST_BLOCK, SSM_CH_BLOCK), F32), _sds((n_blocks, SSM_ST_BLOCK, SSM_CH_BLOCK), F32),
                   _sds((1, d_ssm), F32)],
        scratch_shapes=[state, state, state, state, pltpu.VMEM((t, SSM_CH_BLOCK), F32),
                        pltpu.VMEM((SUBLANES, SSM_ST_BLOCK), F32), pltpu.VMEM((SUBLANES, SSM_ST_BLOCK), F32)],
        semantics=("parallel",))(proj, *mats, dskip_row, y, dz_a, dz_b)


def _block_diag_in(bbar_pgn):
    p, g, n = bbar_pgn.shape
    b4 = bbar_pgn.reshape(p, g // GROUPS_PER_BLOCK, GROUPS_PER_BLOCK, n)
    eye = jnp.eye(GROUPS_PER_BLOCK, dtype=F32)
    return jnp.einsum("pjgn,gh->jgphn", b4, eye).reshape(g // GROUPS_PER_BLOCK, SSM_CH_BLOCK, SSM_ST_BLOCK)


def _block_diag_in_t(dense):
    j = dense.shape[0]
    d5 = dense.reshape(j, GROUPS_PER_BLOCK, SSM_GROUP, GROUPS_PER_BLOCK, SSM_STATE)
    eye = jnp.eye(GROUPS_PER_BLOCK, dtype=F32)
    return jnp.einsum("jgphn,gh->pjgn", d5, eye).reshape(SSM_GROUP, j * GROUPS_PER_BLOCK, SSM_STATE)


def _block_diag_out(c_gpn):
    g, p, n = c_gpn.shape
    c4 = c_gpn.reshape(g // GROUPS_PER_BLOCK, GROUPS_PER_BLOCK, p, n)
    eye = jnp.eye(GROUPS_PER_BLOCK, dtype=F32)
    return jnp.einsum("jgpn,gh->jgnhp", c4, eye).reshape(g // GROUPS_PER_BLOCK, SSM_ST_BLOCK, SSM_CH_BLOCK)


def _block_diag_out_t(dense):
    j = dense.shape[0]
    d5 = dense.reshape(j, GROUPS_PER_BLOCK, SSM_STATE, GROUPS_PER_BLOCK, SSM_GROUP)
    eye = jnp.eye(GROUPS_PER_BLOCK, dtype=F32)
    return jnp.einsum("jgnhp,gh->jgpn", d5, eye).reshape(j * GROUPS_PER_BLOCK, SSM_GROUP, SSM_STATE)


def _adamw(w, g, m, v):
    m = ADAM_B1 * m + (1.0 - ADAM_B1) * g
    v = ADAM_B2 * v + (1.0 - ADAM_B2) * (g * g)
    m_hat = m / (1.0 - ADAM_B1 ** ADAM_STEP)
    v_hat = v / (1.0 - ADAM_B2 ** ADAM_STEP)
    delta = -ADAM_LR * (m_hat / (jnp.sqrt(v_hat) + ADAM_EPS) + ADAM_WD * w)
    return delta, m, v


def _adam_sharded(name, parts, w, m, v, tr):
    r, c = w.shape
    assert r % tr == 0, (name, r, tr)

    def body(p_ref, w_ref, m_ref, v_ref, g_out, d_out, m_out, v_out):
        g = p_ref[0].astype(F32)
        for i in range(1, N_DEV):
            g = g + p_ref[i].astype(F32)
        delta, m_new, v_new = _adamw(w_ref[...], g, m_ref[...], v_ref[...])
        g_out[...] = g
        d_out[...] = delta
        m_out[...] = m_new
        v_out[...] = v_new

    tile = pl.BlockSpec((tr, c), lambda i: (i, 0))
    return _call(body, name=name, grid=(r // tr,),
                 in_specs=[pl.BlockSpec((N_DEV, tr, c), lambda i: (0, i, 0)), tile, tile, tile],
                 out_specs=[tile] * 4, out_shape=[_sds((r, c), F32)] * 4, semantics=("parallel",))(parts, w, m, v)


_SMALL = ("g_pre_mix", "sinks", "a_re", "a_im", "log_dt", "b_re", "b_im", "c_re", "c_im", "d_skip", "b_glu",
          "g_attn_out", "g_ssm_out", "g_post_mix", "g_pre_ffn", "g_post_ffn")
_BIG = ("w_in", "w_glu", "w_o", "w_gate", "w_up", "w_down")
_ORDER = ("g_pre_mix", "w_in", "sinks", "a_re", "a_im", "log_dt", "b_re", "b_im", "c_re", "c_im", "d_skip", "w_glu",
          "b_glu", "g_attn_out", "g_ssm_out", "w_o", "g_post_mix", "g_pre_ffn", "w_gate", "w_up", "w_down",
          "g_post_ffn")


def _pack(arrays):
    flat = jnp.concatenate([a.reshape(-1).astype(F32) for a in arrays])
    pad = (-flat.shape[0]) % (SUBLANES * LANES)
    return jnp.pad(flat, (0, pad)).reshape(-1, LANES)


def _unpack(packed, like):
    flat = packed.reshape(-1)
    out, at = [], 0
    for a in like:
        out.append(flat[at:at + a.size].reshape(a.shape))
        at += a.size
    return out


def kernel(x, positions, g_pre_mix, w_in, sinks, a_re, a_im, log_dt, b_re, b_im, c_re, c_im, d_skip, w_glu, b_glu, g_attn_out, g_ssm_out, w_o, g_post_mix, g_pre_ffn, w_gate, w_up, w_down, g_post_ffn, loss_target, m_g_pre_mix, m_w_in, m_sinks, m_a_re, m_a_im, m_log_dt, m_b_re, m_b_im, m_c_re, m_c_im, m_d_skip, m_w_glu, m_b_glu, m_g_attn_out, m_g_ssm_out, m_w_o, m_g_post_mix, m_g_pre_ffn, m_w_gate, m_w_up, m_w_down, m_g_post_ffn, v_g_pre_mix, v_w_in, v_sinks, v_a_re, v_a_im, v_log_dt, v_b_re, v_b_im, v_c_re, v_c_im, v_d_skip, v_w_glu, v_b_glu, v_g_attn_out, v_g_ssm_out, v_w_o, v_g_post_mix, v_g_pre_ffn, v_w_gate, v_w_up, v_w_down, v_g_post_ffn):
    given = dict(locals())
    weights = {n: given[n] for n in _ORDER}
    mom_m = {n: given["m_" + n] for n in _ORDER}
    mom_v = {n: given["v_" + n] for n in _ORDER}

    t, d = x.shape[1], x.shape[2]
    d_attn = d // 2
    d_ssm = d - d_attn
    d_in = d_attn + 2 * D_KV + d_ssm
    n_groups = d_ssm // SSM_GROUP
    n_heads = d_attn // HEAD_DIM
    tm = min(256, t)

    x2 = x[0]
    target = loss_target[0]

    def start_gather(name, ws, token):
        behind = 0 if token is None else token[0, 0].astype(BF16)
        return _exchange_start(name, [w[0].astype(BF16) + behind for w in ws], False)

    ag_in, token = start_gather("gather_w_in", [w_in], None)
    ag_mix, token = start_gather("gather_w_glu_o", [w_glu, w_o], token)
    ag_ffn_in, token = start_gather("gather_w_gate_up", [w_gate, w_up], token)
    ag_down, token = start_gather("gather_w_down", [w_down], token)

    xn, = _rows("norm_in", lambda xv, g: ([_rms(xv)[0] * g], []), [x2], [g_pre_mix], [(d, BF16)], [], tm,
                after=[token])
    win_g, = _exchange_wait(ag_in, [xn])
    w_in_full = win_g.transpose(1, 0, 2).reshape(d, d_in)
    proj = _mm_nn("proj_in", xn, w_in_full, F32, tn=d_in // 4 if (d_in // 4) % LANES == 0 else None)

    cos, sin = _rope_tables(positions.reshape(t, 1).astype(F32))
    sinks_row = jnp.pad(sinks, ((0, 0), (0, LANES - n_heads)))
    attn = _attention_fwd(proj, cos, sin, sinks_row, d_attn)

    b_re_t, b_im_t = b_re[0].transpose(2, 0, 1), b_im[0].transpose(2, 0, 1)
    ldt_col = log_dt.reshape(n_groups, 1)
    lam_re, lam_im, bbar_re, bbar_im = _s5_discretise(a_re[0], a_im[0], ldt_col, b_re_t, b_im_t)
    n_blocks = n_groups // GROUPS_PER_BLOCK
    mats = [_block_diag_in(bbar_re).astype(BF16), _block_diag_in(bbar_im).astype(BF16),
            lam_re.reshape(n_blocks, 1, SSM_ST_BLOCK), lam_im.reshape(n_blocks, 1, SSM_ST_BLOCK),
            _block_diag_out(c_re[0]).astype(BF16), _block_diag_out(c_im[0]).astype(BF16)]
    dskip_row = d_skip.reshape(1, d_ssm)
    y_ssm, z_ssm = _s5_fwd(proj, mats, dskip_row, d_attn, d_ssm)
    wglu_g, wo_g = _exchange_wait(ag_mix, [attn, z_ssm])
    w_glu_full = wglu_g.reshape(d_ssm, d_ssm)
    w_o_full = wo_g.reshape(d, d)
    glu_lin = _mm_nn("glu_gate", z_ssm, w_glu_full, F32)

    def mix_prep(av, yv, gl, bg, ga, gs):
        ssm = _gelu(yv) * _sigmoid(gl + bg)
        return [jnp.concatenate([_rms(av)[0] * ga, _rms(ssm)[0] * gs], axis=1)], []

    mixed, = _rows("mix_prep", mix_prep, [attn, y_ssm, glu_lin], [b_glu, g_attn_out, g_ssm_out], [(d, BF16)], [], tm)
    mix = _mm_nn("mix_out", mixed, w_o_full, F32, tn=d // 2 if (d // 2) % LANES == 0 else None)

    def post_mix(xv, mv, gpm, gpf):
        h = xv + _rms(mv)[0] * gpm
        return [h, _rms(h)[0] * gpf], []

    h, hn = _rows("post_mix", post_mix, [x2, mix], [g_post_mix, g_pre_ffn], [(d, F32), (d, BF16)], [], tm)
    wgate_g, wup_g = _exchange_wait(ag_ffn_in, [hn])
    gate, up, hid = _ffn_in(hn, wgate_g, wup_g)
    wdown_g, = _exchange_wait(ag_down, [hid])
    ff = _mm_contract_slots("ffn_down", [(hid, wdown_g)], F32)

    def head(hv, fv, tv, gpo):
        out = hv + _rms(fv)[0] * gpo
        err = out - tv
        dout = err * (1.0 / d)
        dff, dg = _rms_bwd(fv, gpo, dout)
        loss = jnp.zeros((1, LANES), F32) + 0.5 * jnp.sum(err * err) * (1.0 / d)
        return [dff, dout], [dg, loss]

    dff, dh_out, dg_post_ffn, loss_row = _rows("loss_head", head, [h, ff, target], [g_post_ffn],
                                               [(d, BF16), (d, F32)], [d, LANES], tm)

    dw_down = _mm_slots_tn("ffn_down_dw", hid, dff, BF16)
    rs_down, tok_down = _exchange_start("scatter_dw_down", [dw_down], True)
    dgate, dup = _ffn_down_bwd(dff, wdown_g, gate, up, [tok_down])
    dhn = _mm_contract_slots_nt("ffn_in_dx", [(dgate, wgate_g), (dup, wup_g)], F32)
    dw_gate = _mm_tn_slots("ffn_gate_dw", hn, dgate, BF16)
    dw_up = _mm_tn_slots("ffn_up_dw", hn, dup, BF16)
    rs_ffn_in, tok_ffn_in = _exchange_start("scatter_dw_gate_up", [dw_gate, dw_up], True)

    def mid_bwd(dho, dhn_, hv, mv, gpf, gpm):
        d1, dgpf = _rms_bwd(hv, gpf, dhn_)
        dh_ = dho + d1
        dmix_, dgpm = _rms_bwd(mv, gpm, dh_)
        return [dh_, dmix_], [dgpf, dgpm]

    dh, dmix, dg_pre_ffn, dg_post_mix = _rows("mid_bwd", mid_bwd, [dh_out, dhn, h, mix], [g_pre_ffn, g_post_mix],
                                              [(d, F32), (d, BF16)], [d, d], tm, after=[tok_ffn_in])

    dmixed = _mm_nt("mix_out_dx", dmix, w_o_full, F32, tn=d // 2 if (d // 2) % LANES == 0 else None)
    dw_o = _mm_tn("mix_out_dw", mixed, dmix, BF16, tn=d // 2 if (d // 2) % LANES == 0 else None)
    rs_o, tok_o = _exchange_start("scatter_dw_o", [dw_o.reshape(N_DEV, d // N_DEV, d)], True)

    def mix_bwd(dm, av, yv, gl, bg, ga, gs):
        dattn_, dga = _rms_bwd(av, ga, dm[:, :d_attn])
        z = _gelu(yv)
        sg = _sigmoid(gl + bg)
        dssm, dgs = _rms_bwd(z * sg, gs, dm[:, d_attn:])
        dgl = dssm * z * sg * (1.0 - sg)
        return [dattn_, dssm * sg, dgl], [dga, dgs, jnp.sum(dgl, axis=0, keepdims=True)]

    dattn, dz_direct, dglu, dg_attn_out, dg_ssm_out, db_glu = _rows(
        "mix_bwd", mix_bwd, [dmixed, attn, y_ssm, glu_lin], [b_glu, g_attn_out, g_ssm_out],
        [(d_attn, F32), (d_ssm, F32), (d_ssm, BF16)], [d_attn, d_ssm, d_ssm], tm, after=[tok_o])
    dz_glu = _mm_nt("glu_gate_dx", dglu, w_glu_full, F32)
    dw_glu = _mm_tn("glu_gate_dw", z_ssm, dglu, BF16)

    du, db_re_dense, db_im_dense, dlam_re, dlam_im, dc_re_dense, dc_im_dense, dd_skip = _s5_bwd(
        proj, mats, dskip_row, y_ssm, dz_direct, dz_glu, d_attn, d_ssm)
    da_re, da_im, dlog_dt, db_re_t, db_im_t = _s5_discretise_bwd(
        a_re[0], a_im[0], ldt_col, b_re_t, b_im_t, dlam_re.reshape(n_groups, SSM_STATE),
        dlam_im.reshape(n_groups, SSM_STATE), _block_diag_in_t(db_re_dense), _block_diag_in_t(db_im_dense))
    dq, dk2, dv2, dsinks_row = _attention_bwd(proj, cos, sin, sinks_row, dattn, d_attn)
    dproj = _assemble_dproj(dq, dk2, dv2, du, d_in)

    dxn = _mm_nt("proj_in_dx", dproj, w_in_full, F32, tn=d // 2 if (d // 2) % LANES == 0 else None)
    c_sh = d_in // N_DEV
    dproj_sh = dproj.reshape(t, N_DEV, c_sh).transpose(1, 0, 2)
    dw_in = _mm_tn_slots("proj_in_dw", xn, dproj_sh, BF16)

    def x_bwd(dh_, dxn_, xv, g):
        dx, dg = _rms_bwd(xv, g, dxn_)
        return [dh_ + dx], [dg]

    grad_x, dg_pre_mix = _rows("norm_in_bwd", x_bwd, [dh, dxn, x2], [g_pre_mix], [(d, F32)], [d], tm)

    small_grads = {
        "g_pre_mix": dg_pre_mix, "sinks": dsinks_row[:, :n_heads], "a_re": da_re[None], "a_im": da_im[None],
        "log_dt": dlog_dt.reshape(1, n_groups), "b_re": db_re_t.transpose(1, 2, 0)[None],
        "b_im": db_im_t.transpose(1, 2, 0)[None], "c_re": _block_diag_out_t(dc_re_dense)[None],
        "c_im": _block_diag_out_t(dc_im_dense)[None], "d_skip": dd_skip.reshape(d_skip.shape), "b_glu": db_glu,
        "g_attn_out": dg_attn_out, "g_ssm_out": dg_ssm_out, "g_post_mix": dg_post_mix, "g_pre_ffn": dg_pre_ffn,
        "g_post_ffn": dg_post_ffn,
    }
    small_like = [weights[n] for n in _SMALL]
    packed = _pack([small_grads[n] for n in _SMALL])
    rs_in, token = _exchange_start("scatter_dw_in_glu", [dw_in, dw_glu.reshape(N_DEV, d_ssm // N_DEV, d_ssm)], True)
    ag_small, token = _exchange_start("gather_small_grads", [packed + token[:1, :]], False)

    results = {}

    def adam_big(n, parts):
        r = weights[n].shape[1]
        results[n] = _adam_sharded("adam_" + n, parts, weights[n][0], mom_m[n][0], mom_v[n][0],
                                   64 if r % 64 == 0 else r)
        return results[n][3]

    done = [grad_x, token]
    adam_big("w_down", _exchange_wait(rs_down, done)[0])
    p_gate, p_up = _exchange_wait(rs_ffn_in, done)
    done = [adam_big("w_gate", p_gate), adam_big("w_up", p_up), results["w_down"][3]]
    done = [adam_big("w_o", _exchange_wait(rs_o, done)[0])]
    p_in, p_glu = _exchange_wait(rs_in, done)
    done = [adam_big("w_in", p_in), adam_big("w_glu", p_glu)]
    small_all, = _exchange_wait(ag_small, done)
    g_s, d_s, m_s, v_s = _adam_sharded(
        "adam_small", small_all, _pack(small_like), _pack([mom_m[n] for n in _SMALL]),
        _pack([mom_v[n] for n in _SMALL]), packed.shape[0])
    for i, vals in enumerate(zip(*[_unpack(p, small_like) for p in (g_s, d_s, m_s, v_s)])):
        results[_SMALL[i]] = vals

    loss = lax.psum(loss_row[0, 0], ("x", "y", "c"))
    outs = [loss, grad_x[None]]
    for k in range(4):
        for n in _ORDER:
            val = results[n][k]
            outs.append(val[None] if n in _BIG else val)
    return tuple(outs)
```

```python
import math

import jax
import jax.numpy as jnp
from jax import lax
from jax.experimental import pallas as pl
from jax.experimental.pallas import tpu as pltpu

F32 = jnp.float32
BF16 = jnp.bfloat16

HEAD_DIM = 64
N_KV_HEADS = 4
D_KV = N_KV_HEADS * HEAD_DIM
WINDOW = 128
BLOCK = 128
ROPE_THETA = 10000.0
SSM_GROUP = 16
SSM_STATE = 64
GROUPS_PER_BLOCK = 8
SSM_CH_BLOCK = GROUPS_PER_BLOCK * SSM_GROUP
SSM_ST_BLOCK = GROUPS_PER_BLOCK * SSM_STATE
RMS_EPS = 1e-6
N_DEV = 8
LANES = 128
SUBLANES = 8
MASKED = -1e30

ADAM_LR = 0.001
ADAM_B1 = 0.9
ADAM_B2 = 0.999
ADAM_EPS = 1e-08
ADAM_WD = 0.01
ADAM_STEP = 10

VMEM_LIMIT_BYTES = 56 * 1024 * 1024


def _call(body, *, name, out_shape, in_specs, out_specs, grid=(), scratch_shapes=(), semantics=None, n_after=0):
    params = dict(vmem_limit_bytes=VMEM_LIMIT_BYTES)
    if semantics is not None:
        params["dimension_semantics"] = semantics
    n_in = len(in_specs)
    if n_after:
        inner = body

        def body(*refs):
            inner(*refs[:n_in], *refs[n_in + n_after:])

        in_specs = list(in_specs) + [pl.BlockSpec(memory_space=pl.ANY)] * n_after
    return pl.pallas_call(body, name=name, grid=grid, in_specs=in_specs, out_specs=out_specs, out_shape=out_shape,
                          scratch_shapes=scratch_shapes, compiler_params=pltpu.CompilerParams(**params))


def _sds(shape, dtype):
    return jax.ShapeDtypeStruct(tuple(shape), dtype)


def _dot(a, b, ca, cb):
    return lax.dot_general(a, b, (((ca,), (cb,)), ((), ())), preferred_element_type=F32)


def _rms(x):
    r = lax.rsqrt(jnp.mean(x * x, axis=-1, keepdims=True) + RMS_EPS)
    return x * r, r


def _rms_bwd(x, g, dy):
    xh, r = _rms(x)
    dxh = dy * g
    dx = r * (dxh - xh * jnp.mean(dxh * xh, axis=-1, keepdims=True))
    return dx, jnp.sum(dy * xh, axis=0, keepdims=True)


def _sigmoid(x):
    return 1.0 / (1.0 + jnp.exp(-x))


_GELU_C = math.sqrt(2.0 / math.pi)
_GELU_A = 0.044715


def _gelu(y):
    t = jnp.tanh(_GELU_C * (y + _GELU_A * y * y * y))
    return 0.5 * y * (1.0 + t)


def _gelu_grad(y):
    t = jnp.tanh(_GELU_C * (y + _GELU_A * y * y * y))
    return 0.5 * (1.0 + t) + 0.5 * y * (1.0 - t * t) * _GELU_C * (1.0 + 3.0 * _GELU_A * y * y)


def _rows(name, fn, row_ins, vec_ins, row_outs, acc_widths, tm, after=()):
    rows = row_ins[0].shape[0]
    assert rows % tm == 0, (name, rows, tm)
    n_row, n_vec, n_out, n_acc = len(row_ins), len(vec_ins), len(row_outs), len(acc_widths)

    def body(*refs):
        ins = [r[...] for r in refs[:n_row + n_vec]]
        outs = refs[n_row + n_vec:n_row + n_vec + n_out]
        accs = refs[n_row + n_vec + n_out:]
        row_vals, acc_vals = fn(*ins)
        for o, v in zip(outs, row_vals):
            o[...] = v.astype(o.dtype)
        if n_acc:
            @pl.when(pl.program_id(0) == 0)
            def _():
                for a in accs:
                    a[...] = jnp.zeros_like(a)
            for a, v in zip(accs, acc_vals):
                a[...] += v

    in_specs = [pl.BlockSpec((tm, a.shape[1]), lambda i: (i, 0)) for a in row_ins]
    in_specs += [pl.BlockSpec(v.shape, lambda i: (0, 0)) for v in vec_ins]
    out_specs = [pl.BlockSpec((tm, w), lambda i: (i, 0)) for w, _ in row_outs]
    out_specs += [pl.BlockSpec((1, w), lambda i: (0, 0)) for w in acc_widths]
    out_shape = [_sds((rows, w), dt) for w, dt in row_outs] + [_sds((1, w), F32) for w in acc_widths]
    return _call(body, name=name, grid=(rows // tm,), in_specs=in_specs, out_specs=out_specs, out_shape=out_shape,
                 semantics=("arbitrary",) if n_acc else ("parallel",), n_after=len(after))(*row_ins, *vec_ins, *after)


def _matmul(name, operands, in_specs, product, grid, out_shape, out_spec, acc_shape):
    nk = grid[-1]
    n_in = len(operands)

    def body(*refs):
        ins = [r[...] for r in refs[:n_in]]
        o_ref = refs[n_in]
        if nk == 1:
            o_ref[...] = product(*ins).astype(o_ref.dtype)
            return
        acc = refs[n_in + 1]
        k = pl.program_id(len(grid) - 1)

        @pl.when(k == 0)
        def _():
            acc[...] = jnp.zeros_like(acc)

        acc[...] += product(*ins)

        @pl.when(k == nk - 1)
        def _():
            o_ref[...] = acc[...].astype(o_ref.dtype)

    return _call(body, name=name, grid=grid, in_specs=in_specs, out_specs=out_spec, out_shape=out_shape,
                 scratch_shapes=[] if nk == 1 else [pltpu.VMEM(acc_shape, F32)],
                 semantics=("parallel",) * (len(grid) - 1) + ("arbitrary",))(*operands)


def _mm_nn(name, a, b, out_dtype, tm=512, tn=None):
    m, k = a.shape
    n = b.shape[1]
    tm, tn = min(tm, m), n if tn is None else tn
    return _matmul(name, [a, b],
                   [pl.BlockSpec((tm, k), lambda i, j, s: (i, 0)), pl.BlockSpec((k, tn), lambda i, j, s: (0, j))],
                   lambda x, y: _dot(x, y, 1, 0), (m // tm, n // tn, 1), _sds((m, n), out_dtype),
                   pl.BlockSpec((tm, tn), lambda i, j, s: (i, j)), (tm, tn))


def _mm_nt(name, a, b, out_dtype, tm=512, tn=None):
    m, k = a.shape
    n = b.shape[0]
    tm, tn = min(tm, m), n if tn is None else tn
    return _matmul(name, [a, b],
                   [pl.BlockSpec((tm, k), lambda i, j, s: (i, 0)), pl.BlockSpec((tn, k), lambda i, j, s: (j, 0))],
                   lambda x, y: _dot(x, y, 1, 1), (m // tm, n // tn, 1), _sds((m, n), out_dtype),
                   pl.BlockSpec((tm, tn), lambda i, j, s: (i, j)), (tm, tn))


def _mm_tn(name, a, b, out_dtype, tm=512, tn=None, tk=512):
    k, m = a.shape
    n = b.shape[1]
    tm, tk, tn = min(tm, m), min(tk, k), n if tn is None else tn
    return _matmul(name, [a, b],
                   [pl.BlockSpec((tk, tm), lambda i, j, s: (s, i)), pl.BlockSpec((tk, tn), lambda i, j, s: (s, j))],
                   lambda x, y: _dot(x, y, 0, 0), (m // tm, n // tn, k // tk), _sds((m, n), out_dtype),
                   pl.BlockSpec((tm, tn), lambda i, j, s: (i, j)), (tm, tn))


def _mm_contract_slots(name, pairs, out_dtype, tm=512, tn=2048):
    s_, m, k = pairs[0][0].shape
    n = pairs[0][1].shape[2]
    tm, tn = min(tm, m), min(tn, n)
    ops, specs = [], []
    for a, b in pairs:
        ops += [a, b]
        specs += [pl.BlockSpec((None, tm, k), lambda i, j, s: (s, i, 0)), pl.BlockSpec((None, k, tn), lambda i, j, s: (s, 0, j))]

    def product(*t):
        return sum(_dot(t[2 * p], t[2 * p + 1], 1, 0) for p in range(len(pairs)))

    return _matmul(name, ops, specs, product, (m // tm, n // tn, s_), _sds((m, n), out_dtype),
                   pl.BlockSpec((tm, tn), lambda i, j, s: (i, j)), (tm, tn))


def _mm_contract_slots_nt(name, pairs, out_dtype, tm=512, tn=2048):
    s_, m, k = pairs[0][0].shape
    n = pairs[0][1].shape[1]
    tm, tn = min(tm, m), min(tn, n)
    ops, specs = [], []
    for a, b in pairs:
        ops += [a, b]
        specs += [pl.BlockSpec((None, tm, k), lambda i, j, s: (s, i, 0)), pl.BlockSpec((None, tn, k), lambda i, j, s: (s, j, 0))]

    def product(*t):
        return sum(_dot(t[2 * p], t[2 * p + 1], 1, 1) for p in range(len(pairs)))

    return _matmul(name, ops, specs, product, (m // tm, n // tn, s_), _sds((m, n), out_dtype),
                   pl.BlockSpec((tm, tn), lambda i, j, s: (i, j)), (tm, tn))


def _mm_tn_slots(name, a, b, out_dtype, tm=2048, tk=512):
    k, m = a.shape
    s_, _, n = b.shape
    tm, tk = min(tm, m), min(tk, k)
    return _matmul(name, [a, b],
                   [pl.BlockSpec((tk, tm), lambda s, i, z: (z, i)), pl.BlockSpec((None, tk, n), lambda s, i, z: (s, z, 0))],
                   lambda x, y: _dot(x, y, 0, 0), (s_, m // tm, k // tk), _sds((s_, m, n), out_dtype),
                   pl.BlockSpec((None, tm, n), lambda s, i, z: (s, i, 0)), (tm, n))


def _mm_slots_tn(name, a, b, out_dtype, tn=2048, tk=512):
    s_, k, m = a.shape
    n = b.shape[1]
    tn, tk = min(tn, n), min(tk, k)
    return _matmul(name, [a, b],
                   [pl.BlockSpec((None, tk, m), lambda s, j, z: (s, z, 0)), pl.BlockSpec((tk, tn), lambda s, j, z: (z, j))],
                   lambda x, y: _dot(x, y, 0, 0), (s_, n // tn, k // tk), _sds((s_, m, n), out_dtype),
                   pl.BlockSpec((None, m, tn), lambda s, j, z: (s, 0, j)), (m, tn))


def _ffn_in(a, w_gate, w_up, tm=512):
    m, k = a.shape
    s_, _, n = w_gate.shape
    tm = min(tm, m)

    def body(a_ref, wg_ref, wu_ref, g_ref, u_ref, h_ref):
        x = a_ref[...]
        g = _dot(x, wg_ref[...], 1, 0)
        u = _dot(x, wu_ref[...], 1, 0)
        g_ref[...] = g.astype(BF16)
        u_ref[...] = u.astype(BF16)
        h_ref[...] = (g * _sigmoid(g) * u).astype(BF16)

    w_spec = pl.BlockSpec((None, k, n), lambda s, i: (s, 0, 0))
    o_spec = pl.BlockSpec((None, tm, n), lambda s, i: (s, i, 0))
    return _call(body, name="ffn_in", grid=(s_, m // tm),
                 in_specs=[pl.BlockSpec((tm, k), lambda s, i: (i, 0)), w_spec, w_spec], out_specs=[o_spec] * 3,
                 out_shape=[_sds((s_, m, n), BF16)] * 3, semantics=("parallel", "parallel"))(a, w_gate, w_up)


def _ffn_down_bwd(d_out, w_down, gate, up, after, tm=512):
    m, k = d_out.shape
    s_, n, _ = w_down.shape
    tm = min(tm, m)

    def body(d_ref, w_ref, g_ref, u_ref, dg_ref, du_ref):
        dh = _dot(d_ref[...], w_ref[...], 1, 1)
        g = g_ref[...].astype(F32)
        sg = _sigmoid(g)
        dg_ref[...] = (dh * u_ref[...].astype(F32) * sg * (1.0 + g * (1.0 - sg))).astype(BF16)
        du_ref[...] = (dh * g * sg).astype(BF16)

    t_spec = pl.BlockSpec((None, tm, n), lambda s, i: (s, i, 0))
    return _call(body, name="ffn_down_dx", grid=(s_, m // tm),
                 in_specs=[pl.BlockSpec((tm, k), lambda s, i: (i, 0)), pl.BlockSpec((None, n, k), lambda s, i: (s, 0, 0)),
                           t_spec, t_spec],
                 out_specs=[t_spec] * 2, out_shape=[_sds((s_, m, n), BF16)] * 2, semantics=("parallel", "parallel"),
                 n_after=len(after))(d_out, w_down, gate, up, *after)


ALL_PEERS = (1, 2, 3, 4, 5, 6, 7)
CHIP_PEERS = (2, 4, 6)
SIBLING = 1


def _peer(relation):
    x, y, c = lax.axis_index("x"), lax.axis_index("y"), lax.axis_index("c")
    pos = (1 - x if relation & 4 else x, 1 - y if relation & 2 else y, 1 - c if relation & 1 else c)
    return pos, 4 * pos[0] + 2 * pos[1] + pos[2]


def _exchange_copies(ins, lands, send_sems, recv_sems, scatter, relations):
    _, me = _peer(0)

    def copy(a, s, peer, pos, dst_slot):
        return pltpu.make_async_remote_copy(
            src_ref=ins[a].at[peer] if scatter else ins[a], dst_ref=lands[a].at[dst_slot],
            send_sem=send_sems.at[s], recv_sem=recv_sems.at[s], device_id=pos, device_id_type=pl.DeviceIdType.MESH)

    pairs = []
    for k, r in enumerate(relations):
        pos, peer = _peer(r)
        for a in range(len(ins)):
            s = a * len(relations) + k
            pairs.append((copy(a, s, peer, pos, me), copy(a, s, peer, pos, peer)))
    return me, pairs


def _forward_copies(lands, send_sems, recv_sems):
    sibling, _ = _peer(SIBLING)

    def copy(a, s, slot):
        return pltpu.make_async_remote_copy(
            src_ref=lands[a].at[slot], dst_ref=lands[a].at[slot], send_sem=send_sems.at[s], recv_sem=recv_sems.at[s],
            device_id=sibling, device_id_type=pl.DeviceIdType.MESH)

    pairs = []
    for k, r in enumerate(CHIP_PEERS):
        _, mine = _peer(r)
        _, theirs = _peer(r | SIBLING)
        for a in range(len(lands)):
            s = a * len(CHIP_PEERS) + k
            pairs.append((copy(a, s, mine), copy(a, s, theirs)))
    return pairs


_HBM_SPEC = pl.BlockSpec(memory_space=pltpu.HBM)
_SEM_SPEC = pl.BlockSpec(memory_space=pltpu.SEMAPHORE)
_SIDE_EFFECT = pltpu.SideEffectType.DATAFLOW_SIDE_EFFECTING


def _split_start(name, operands, n_sem, make_pairs):
    k = len(operands)

    def body(*refs):
        send_sems, recv_sems, token = refs[k], refs[k + 1], refs[-1]
        for send, _ in make_pairs(refs[:k], send_sems, recv_sems):
            send.start()
        token[...] = jnp.zeros_like(token)

    out = pl.pallas_call(
        body, name=name,
        out_shape=(pltpu.SemaphoreType.DMA((n_sem,)), pltpu.SemaphoreType.DMA((n_sem,)),
                   *[pltpu.HBM(a.shape, a.dtype) for a in operands], _sds((SUBLANES, LANES), F32)),
        in_specs=[_HBM_SPEC] * k,
        out_specs=(_SEM_SPEC, _SEM_SPEC, *[_HBM_SPEC] * k, pl.BlockSpec(memory_space=pltpu.VMEM)),
        input_output_aliases={i: 2 + i for i in range(k)},
        compiler_params=pltpu.CompilerParams(has_side_effects=_SIDE_EFFECT),
    )(*[pltpu.with_memory_space_constraint(a, pltpu.HBM) for a in operands])
    return dict(name=name, sems=out[:2], thru=list(out[2:2 + k]), make_pairs=make_pairs), out[-1]


def _split_wait(handle, after):
    thru, make_pairs = handle["thru"], handle["make_pairs"]
    k = len(thru)

    def body(*refs):
        for send, arrival in make_pairs(refs[:k], refs[k], refs[k + 1]):
            send.wait_send()
            arrival.wait_recv()

    return pl.pallas_call(
        body, name=handle["name"] + "_wait", out_shape=[pltpu.HBM(a.shape, a.dtype) for a in thru],
        in_specs=[_HBM_SPEC] * k + [_SEM_SPEC, _SEM_SPEC] + [pl.BlockSpec(memory_space=pl.ANY)] * len(after),
        out_specs=[_HBM_SPEC] * k, input_output_aliases={i: i for i in range(k)},
        compiler_params=pltpu.CompilerParams(has_side_effects=_SIDE_EFFECT),
    )(*thru, *handle["sems"], *after)


def _exchange_start(name, arrays, scatter, relations=ALL_PEERS):
    n = len(arrays)
    lands = [lax.empty(a.shape if scatter else (N_DEV,) + a.shape, a.dtype) for a in arrays]

    def make_pairs(refs, send_sems, recv_sems):
        return _exchange_copies(refs[:n], refs[n:], send_sems, recv_sems, scatter, relations)[1]

    handle, token = _split_start(name, list(arrays) + lands, n * len(relations), make_pairs)
    handle.update(n=n, scatter=scatter)
    return handle, token


def _forward_start(name, lands):
    return _split_start(name, list(lands), len(lands) * len(CHIP_PEERS), _forward_copies)


def _exchange_wait(handle, after):
    n, scatter = handle["n"], handle["scatter"]
    out = _split_wait(handle, after)
    me = 4 * lax.axis_index("x") + 2 * lax.axis_index("y") + lax.axis_index("c")
    done = []
    for src, land in zip(out[:n], out[n:]):
        own = lax.dynamic_index_in_dim(src, me, 0, keepdims=True) if scatter else src[None]
        done.append(lax.dynamic_update_slice_in_dim(land, own, me, 0))
    return done


def _rope_tables(pos_col):
    t = pos_col.shape[0]
    half = HEAD_DIM // 2
    inv_freq = ROPE_THETA ** (-jnp.arange(half, dtype=F32) / half)
    inv_row = jnp.tile(inv_freq, LANES // half)[None, :]

    def body(pos_ref, inv_ref, cos_ref, sin_ref):
        ang = pos_ref[...] * inv_ref[...]
        cos_ref[...] = jnp.cos(ang)
        sin_ref[...] = jnp.sin(ang)

    tm = min(t, 512)
    return _call(body, name="rope_tables", grid=(t // tm,),
                 in_specs=[pl.BlockSpec((tm, 1), lambda i: (i, 0)), pl.BlockSpec((1, LANES), lambda i: (0, 0))],
                 out_specs=[pl.BlockSpec((tm, LANES), lambda i: (i, 0))] * 2,
                 out_shape=[_sds((t, LANES), F32)] * 2, semantics=("parallel",))(pos_col, inv_row)


def _rot_half(x):
    lane = lax.broadcasted_iota(jnp.int32, x.shape, 1)
    low = (lane % HEAD_DIM) < HEAD_DIM // 2
    return jnp.where(low, -pltpu.roll(x, LANES - HEAD_DIM // 2, 1), pltpu.roll(x, HEAD_DIM // 2, 1))


def _rope(x, cos, sin):
    return x * cos + _rot_half(x) * sin


def _unrope(d, cos, sin):
    return d * cos - _rot_half(d) * sin


def _band_mask(first_block, heads):
    r = lax.broadcasted_iota(jnp.int32, (heads * BLOCK, 2 * BLOCK), 0) % BLOCK
    c = lax.broadcasted_iota(jnp.int32, (heads * BLOCK, 2 * BLOCK), 1)
    diff = r - c + BLOCK
    return (diff >= 0) & (diff < WINDOW) & ((c >= BLOCK) | jnp.logical_not(first_block))


def _attn_specs(t, d_attn, d_in):
    kb, vb = d_attn // D_KV, d_attn // D_KV + 1
    prev = lambda i: jnp.maximum(i - 1, 0)
    return [
        pl.BlockSpec((BLOCK, d_attn), lambda i: (i, 0)),
        pl.BlockSpec((BLOCK, D_KV), lambda i: (i, kb)),
        pl.BlockSpec((BLOCK, D_KV), lambda i: (i, vb)),
        pl.BlockSpec((BLOCK, D_KV), lambda i: (prev(i), kb)),
        pl.BlockSpec((BLOCK, D_KV), lambda i: (prev(i), vb)),
        pl.BlockSpec((BLOCK, LANES), lambda i: (i, 0)),
        pl.BlockSpec((BLOCK, LANES), lambda i: (i, 0)),
        pl.BlockSpec((BLOCK, LANES), lambda i: (prev(i), 0)),
        pl.BlockSpec((BLOCK, LANES), lambda i: (prev(i), 0)),
        pl.BlockSpec((1, LANES), lambda i: (0, 0)),
    ]


def _head(x, h):
    return x[:, h * HEAD_DIM:(h + 1) * HEAD_DIM]


def _attn_heads(q_ref, kc_ref, vc_ref, kp_ref, vp_ref, cq_ref, sq_ref, cp_ref, sp_ref, d_attn):
    cq, sq, cp, sp = cq_ref[...], sq_ref[...], cp_ref[...], sp_ref[...]
    q_rot = [_rope(q_ref[:, j * LANES:(j + 1) * LANES], cq, sq) for j in range(d_attn // LANES)]
    kc_rot = [_rope(kc_ref[:, j * LANES:(j + 1) * LANES], cq, sq) for j in range(D_KV // LANES)]
    kp_rot = [_rope(kp_ref[:, j * LANES:(j + 1) * LANES], cp, sp) for j in range(D_KV // LANES)]
    per = LANES // HEAD_DIM
    q_heads = [_head(q_rot[h // per], h % per).astype(BF16) for h in range(d_attn // HEAD_DIM)]
    kk = [jnp.concatenate([_head(kp_rot[g // per], g % per), _head(kc_rot[g // per], g % per)], axis=0).astype(BF16)
          for g in range(N_KV_HEADS)]
    vv = [jnp.concatenate([_head(vp_ref[...], g), _head(vc_ref[...], g)], axis=0).astype(BF16) for g in range(N_KV_HEADS)]
    return q_heads, kk, vv


def _stack_group(q_heads, sink_ref, group):
    q_all = jnp.concatenate([q_heads[h] for h in group], axis=0)
    sink_all = jnp.concatenate([jnp.broadcast_to(sink_ref[:, h:h + 1], (BLOCK, 1)) for h in group], axis=0)
    return q_all, sink_all


def _softmax_with_sink(q, kk, sink, mask):
    s = _dot(q, kk, 1, 1) * (1.0 / math.sqrt(HEAD_DIM))
    s = jnp.where(mask, s, MASKED)
    m = jnp.maximum(jnp.max(s, axis=-1, keepdims=True), sink)
    p = jnp.exp(s - m)
    e_sink = jnp.exp(sink - m)
    inv = 1.0 / (jnp.sum(p, axis=-1, keepdims=True) + e_sink)
    return p * inv, e_sink * inv


def _attention_fwd(proj, cos, sin, sinks_row, d_attn):
    t, d_in = proj.shape
    n_heads = d_attn // HEAD_DIM
    q_per_kv = n_heads // N_KV_HEADS

    def body(q_ref, kc_ref, vc_ref, kp_ref, vp_ref, cq_ref, sq_ref, cp_ref, sp_ref, sink_ref, o_ref):
        mask = _band_mask(pl.program_id(0) == 0, q_per_kv)
        q_heads, kk, vv = _attn_heads(q_ref, kc_ref, vc_ref, kp_ref, vp_ref, cq_ref, sq_ref, cp_ref, sp_ref, d_attn)
        for g in range(N_KV_HEADS):
            group = range(g * q_per_kv, (g + 1) * q_per_kv)
            q_all, sink_all = _stack_group(q_heads, sink_ref, group)
            probs, _ = _softmax_with_sink(q_all, kk[g], sink_all, mask)
            o_all = _dot(probs.astype(BF16), vv[g], 1, 0)
            for k, h in enumerate(group):
                o_ref[:, h * HEAD_DIM:(h + 1) * HEAD_DIM] = o_all[k * BLOCK:(k + 1) * BLOCK]

    return _call(body, name="attention_fwd", grid=(t // BLOCK,), in_specs=_attn_specs(t, d_attn, d_in),
                 out_specs=pl.BlockSpec((BLOCK, d_attn), lambda i: (i, 0)), out_shape=_sds((t, d_attn), F32),
                 semantics=("parallel",))(proj, proj, proj, proj, proj, cos, sin, cos, sin, sinks_row)


def _attention_bwd(proj, cos, sin, sinks_row, d_out, d_attn):
    t, d_in = proj.shape
    n_heads = d_attn // HEAD_DIM
    q_per_kv = n_heads // N_KV_HEADS
    nb = t // BLOCK
    per = LANES // HEAD_DIM

    def body(q_ref, kc_ref, vc_ref, kp_ref, vp_ref, cq_ref, sq_ref, cp_ref, sp_ref, sink_ref, do_ref,
             dq_ref, dk_ref, dv_ref, dsink_ref):
        i = pl.program_id(0)
        mask = _band_mask(i == 0, q_per_kv)
        q_heads, kk, vv = _attn_heads(q_ref, kc_ref, vc_ref, kp_ref, vp_ref, cq_ref, sq_ref, cp_ref, sp_ref, d_attn)
        lane = lax.broadcasted_iota(jnp.int32, (1, LANES), 1)
        dsink = jnp.zeros((1, LANES), F32)
        dq_rot, dkk, dvv = [], [], []
        for g in range(N_KV_HEADS):
            group = range(g * q_per_kv, (g + 1) * q_per_kv)
            q_all, sink_all = _stack_group(q_heads, sink_ref, group)
            probs, p_sink = _softmax_with_sink(q_all, kk[g], sink_all, mask)
            do_all = jnp.concatenate([do_ref[:, h * HEAD_DIM:(h + 1) * HEAD_DIM] for h in group], axis=0).astype(BF16)
            dp = _dot(do_all, vv[g], 1, 1)
            delta = jnp.sum(probs * dp, axis=-1, keepdims=True)
            ds = (probs * (dp - delta) * (1.0 / math.sqrt(HEAD_DIM))).astype(BF16)
            dq_all = _dot(ds, kk[g], 1, 0)
            dkk.append(_dot(ds, q_all, 0, 0))
            dvv.append(_dot(probs.astype(BF16), do_all, 0, 0))
            sink_term = p_sink * delta
            for k, h in enumerate(group):
                dq_rot.append(dq_all[k * BLOCK:(k + 1) * BLOCK])
                part = jnp.sum(sink_term[k * BLOCK:(k + 1) * BLOCK], axis=0, keepdims=True)
                dsink += jnp.where(lane == h, -part, 0.0)
        cq, sq, cp, sp = cq_ref[...], sq_ref[...], cp_ref[...], sp_ref[...]
        for j in range(d_attn // LANES):
            d = jnp.concatenate(dq_rot[j * per:(j + 1) * per], axis=1)
            dq_ref[:, j * LANES:(j + 1) * LANES] = _unrope(d, cq, sq)
        for j in range(D_KV // LANES):
            d = jnp.concatenate(dkk[j * per:(j + 1) * per], axis=1)
            dk_ref[0, :, j * LANES:(j + 1) * LANES] = _unrope(d[:BLOCK], cp, sp)
            dk_ref[1, :, j * LANES:(j + 1) * LANES] = _unrope(d[BLOCK:], cq, sq)
            d = jnp.concatenate(dvv[j * per:(j + 1) * per], axis=1)
            dv_ref[0, :, j * LANES:(j + 1) * LANES] = d[:BLOCK]
            dv_ref[1, :, j * LANES:(j + 1) * LANES] = d[BLOCK:]

        @pl.when(i == 0)
        def _():
            dsink_ref[...] = jnp.zeros_like(dsink_ref)

        dsink_ref[...] += dsink

    pair = pl.BlockSpec((2, BLOCK, D_KV), lambda i: (i, 0, 0))
    return _call(body, name="attention_bwd", grid=(nb,),
                 in_specs=_attn_specs(t, d_attn, d_in) + [pl.BlockSpec((BLOCK, d_attn), lambda i: (i, 0))],
                 out_specs=[pl.BlockSpec((BLOCK, d_attn), lambda i: (i, 0)), pair, pair,
                            pl.BlockSpec((1, LANES), lambda i: (0, 0))],
                 out_shape=[_sds((t, d_attn), F32), _sds((2 * nb, BLOCK, D_KV), F32), _sds((2 * nb, BLOCK, D_KV), F32),
                            _sds((1, LANES), F32)],
                 semantics=("arbitrary",))(proj, proj, proj, proj, proj, cos, sin, cos, sin, sinks_row, d_out)


def _assemble_dproj(dq, dk2, dv2, du, d_in):
    t, d_attn = dq.shape
    d_ssm = du.shape[1]
    nb = t // BLOCK

    def body(dq_ref, dk_own, dk_next, dv_own, dv_next, du_ref, o_ref):
        has_next = (pl.program_id(0) < nb - 1).astype(F32)
        o_ref[:, :d_attn] = dq_ref[...].astype(BF16)
        o_ref[:, d_attn:d_attn + D_KV] = (dk_own[...] + has_next * dk_next[...]).astype(BF16)
        o_ref[:, d_attn + D_KV:d_attn + 2 * D_KV] = (dv_own[...] + has_next * dv_next[...]).astype(BF16)
        o_ref[:, d_attn + 2 * D_KV:] = du_ref[...].astype(BF16)

    own = pl.BlockSpec((None, BLOCK, D_KV), lambda i: (2 * i + 1, 0, 0))
    nxt = pl.BlockSpec((None, BLOCK, D_KV), lambda i: (jnp.minimum(2 * i + 2, 2 * nb - 1), 0, 0))
    return _call(body, name="assemble_dproj", grid=(nb,),
                 in_specs=[pl.BlockSpec((BLOCK, d_attn), lambda i: (i, 0)), own, nxt, own, nxt,
                           pl.BlockSpec((BLOCK, d_ssm), lambda i: (i, 0))],
                 out_specs=pl.BlockSpec((BLOCK, d_in), lambda i: (i, 0)), out_shape=_sds((t, d_in), BF16),
                 semantics=("parallel",))(dq, dk2, dk2, dv2, dv2, du)


def _discretise(ar, ai, ldt, br, bi):
    dt = jnp.exp(ldt)
    mag = jnp.exp(ar * dt)
    lam_re = mag * jnp.cos(ai * dt)
    lam_im = mag * jnp.sin(ai * dt)
    den = ar * ar + ai * ai
    nr = lam_re - 1.0
    ni = lam_im
    f_re = (nr * ar + ni * ai) / den
    f_im = (ni * ar - nr * ai) / den
    return lam_re, lam_im, f_re[None] * br - f_im[None] * bi, f_re[None] * bi + f_im[None] * br


def _whole(arrays):
    return [pl.BlockSpec(a.shape, lambda *_, nd=len(a.shape): (0,) * nd) for a in arrays]


def _s5_discretise(ar, ai, ldt, br, bi):
    ins = [ar, ai, ldt, br, bi]

    def body(ar_ref, ai_ref, ldt_ref, br_ref, bi_ref, lr_ref, li_ref, bbr_ref, bbi_ref):
        out = _discretise(ar_ref[...], ai_ref[...], ldt_ref[...], br_ref[...], bi_ref[...])
        for ref, val in zip((lr_ref, li_ref, bbr_ref, bbi_ref), out):
            ref[...] = val

    outs = [_sds(ar.shape, F32), _sds(ar.shape, F32), _sds(br.shape, F32), _sds(br.shape, F32)]
    return _call(body, name="s5_discretise", in_specs=_whole(ins), out_specs=_whole(outs), out_shape=outs)(*ins)


def _s5_discretise_bwd(ar, ai, ldt, br, bi, d_lr, d_li, d_bbr, d_bbi):
    ins = [ar, ai, ldt, br, bi, d_lr, d_li, d_bbr, d_bbi]

    def body(ar_ref, ai_ref, ldt_ref, br_ref, bi_ref, dlr_ref, dli_ref, dbbr_ref, dbbi_ref, *out_refs):
        _, vjp = jax.vjp(_discretise, ar_ref[...], ai_ref[...], ldt_ref[...], br_ref[...], bi_ref[...])
        grads = vjp((dlr_ref[...], dli_ref[...], dbbr_ref[...], dbbi_ref[...]))
        for ref, val in zip(out_refs, grads):
            ref[...] = val

    outs = [_sds(a.shape, F32) for a in (ar, ai, ldt, br, bi)]
    return _call(body, name="s5_discretise_bwd", in_specs=_whole(ins), out_specs=_whole(outs), out_shape=outs)(*ins)


def _cmul(ar, ai, br, bi):
    return ar * br - ai * bi, ar * bi + ai * br


def _power_table(lr, li, reverse):
    pows = [(lr, li)]
    for _ in range(SUBLANES - 1):
        pows.append(_cmul(pows[-1][0], pows[-1][1], lr, li))
    row = lax.broadcasted_iota(jnp.int32, (SUBLANES, lr.shape[1]), 0)
    tr = jnp.zeros((SUBLANES, lr.shape[1]), F32)
    ti = jnp.zeros((SUBLANES, lr.shape[1]), F32)
    for r in range(SUBLANES):
        src = pows[SUBLANES - 1 - r] if reverse else pows[r]
        tr = jnp.where(row == r, src[0], tr)
        ti = jnp.where(row == r, src[1], ti)
    return pows[0], pows[1], pows[3], (tr, ti)


def _scan_tile(xr, xi, steps, table, carry, reverse):
    row = lax.broadcasted_iota(jnp.int32, xr.shape, 0)
    for k, (lr, li) in zip((1, 2, 4), steps):
        if reverse:
            keep = row < SUBLANES - k
            sr = jnp.where(keep, pltpu.roll(xr, SUBLANES - k, 0), 0.0)
            si = jnp.where(keep, pltpu.roll(xi, SUBLANES - k, 0), 0.0)
        else:
            keep = row >= k
            sr = jnp.where(keep, pltpu.roll(xr, k, 0), 0.0)
            si = jnp.where(keep, pltpu.roll(xi, k, 0), 0.0)
        pr, pi = _cmul(lr, li, sr, si)
        xr, xi = xr + pr, xi + pi
    pr, pi = _cmul(table[0], table[1], carry[0], carry[1])
    return xr + pr, xi + pi


def _scan(sr_ref, si_ref, lr, li, reverse, t, per_tile=None):
    l1, l2, l4, table = _power_table(lr, li, reverse)
    n_tiles = t // SUBLANES
    w = lr.shape[1]

    def step(i, carry):
        tile = (n_tiles - 1 - i) if reverse else i
        rows = pl.ds(pl.multiple_of(tile * SUBLANES, SUBLANES), SUBLANES)
        xr, xi = _scan_tile(sr_ref[rows, :], si_ref[rows, :], (l1, l2, l4), table, carry, reverse)
        sr_ref[rows, :] = xr
        si_ref[rows, :] = xi
        if per_tile is not None:
            per_tile(tile, xr, xi)
        edge = 0 if reverse else SUBLANES - 1
        return xr[edge:edge + 1, :], xi[edge:edge + 1, :]

    lax.fori_loop(0, n_tiles, step, (jnp.zeros((1, w), F32), jnp.zeros((1, w), F32)))


_S5_ROWS = 256


def _s5_in_specs(t, d_attn):
    u_block = (d_attn + 2 * D_KV) // SSM_CH_BLOCK
    blk3 = lambda shape: pl.BlockSpec((None,) + shape, lambda j: (j, 0, 0))
    return [
        pl.BlockSpec((t, SSM_CH_BLOCK), lambda j: (0, u_block + j)),
        blk3((SSM_CH_BLOCK, SSM_ST_BLOCK)), blk3((SSM_CH_BLOCK, SSM_ST_BLOCK)),
        blk3((1, SSM_ST_BLOCK)), blk3((1, SSM_ST_BLOCK)),
        blk3((SSM_ST_BLOCK, SSM_CH_BLOCK)), blk3((SSM_ST_BLOCK, SSM_CH_BLOCK)),
        pl.BlockSpec((1, SSM_CH_BLOCK), lambda j: (0, j)),
    ]


def _s5_states(u_ref, bre_ref, bim_ref, lr_ref, li_ref, sr_ref, si_ref, t):
    def fill(i, _):
        rows = pl.ds(pl.multiple_of(i * _S5_ROWS, _S5_ROWS), _S5_ROWS)
        ub = u_ref[rows, :].astype(BF16)
        sr_ref[rows, :] = _dot(ub, bre_ref[...], 1, 0)
        si_ref[rows, :] = _dot(ub, bim_ref[...], 1, 0)
        return 0

    lax.fori_loop(0, t // _S5_ROWS, fill, 0)
    _scan(sr_ref, si_ref, lr_ref[...], li_ref[...], False, t)


def _s5_fwd(proj, mats, dskip_row, d_attn, d_ssm):
    t = proj.shape[0]
    n_blocks = d_ssm // SSM_CH_BLOCK

    def body(u_ref, bre_ref, bim_ref, lr_ref, li_ref, cre_ref, cim_ref, d_ref, y_ref, z_ref, sr_ref, si_ref):
        _s5_states(u_ref, bre_ref, bim_ref, lr_ref, li_ref, sr_ref, si_ref, t)

        def emit(i, _):
            rows = pl.ds(pl.multiple_of(i * _S5_ROWS, _S5_ROWS), _S5_ROWS)
            y = (_dot(sr_ref[rows, :].astype(BF16), cre_ref[...], 1, 0)
                 - _dot(si_ref[rows, :].astype(BF16), cim_ref[...], 1, 0) + d_ref[...] * u_ref[rows, :])
            y_ref[rows, :] = y
            z_ref[rows, :] = _gelu(y).astype(BF16)
            return 0

        lax.fori_loop(0, t // _S5_ROWS, emit, 0)

    col = pl.BlockSpec((t, SSM_CH_BLOCK), lambda j: (0, j))
    return _call(body, name="s5_fwd", grid=(n_blocks,), in_specs=_s5_in_specs(t, d_attn), out_specs=[col, col],
                 out_shape=[_sds((t, d_ssm), F32), _sds((t, d_ssm), BF16)],
                 scratch_shapes=[pltpu.VMEM((t, SSM_ST_BLOCK), F32)] * 2,
                 semantics=("parallel",))(proj, *mats, dskip_row)


def _s5_bwd(proj, mats, dskip_row, y, dz_a, dz_b, d_attn, d_ssm):
    t = proj.shape[0]
    n_blocks = d_ssm // SSM_CH_BLOCK

    def body(u_ref, bre_ref, bim_ref, lr_ref, li_ref, cre_ref, cim_ref, d_ref, y_ref, dza_ref, dzb_ref,
             du_ref, dbre_ref, dbim_ref, dlr_ref, dli_ref, dcre_ref, dcim_ref, dd_ref,
             sr_ref, si_ref, gr_ref, gi_ref, dy_ref, acc_r, acc_i):
        _s5_states(u_ref, bre_ref, bim_ref, lr_ref, li_ref, sr_ref, si_ref, t)
        for ref in (dcre_ref, dcim_ref, dbre_ref, dbim_ref, dd_ref, acc_r, acc_i):
            ref[...] = jnp.zeros_like(ref)

        def through_c(i, _):
            rows = pl.ds(pl.multiple_of(i * _S5_ROWS, _S5_ROWS), _S5_ROWS)
            dy = (dza_ref[rows, :] + dzb_ref[rows, :]) * _gelu_grad(y_ref[rows, :])
            dy_ref[rows, :] = dy
            dd_ref[...] += jnp.sum(dy * u_ref[rows, :], axis=0, keepdims=True)
            dyb = dy.astype(BF16)
            gr_ref[rows, :] = _dot(dyb, cre_ref[...], 1, 1)
            gi_ref[rows, :] = -_dot(dyb, cim_ref[...], 1, 1)
            dcre_ref[...] += _dot(sr_ref[rows, :].astype(BF16), dyb, 0, 0)
            dcim_ref[...] -= _dot(si_ref[rows, :].astype(BF16), dyb, 0, 0)
            return 0

        lax.fori_loop(0, t // _S5_ROWS, through_c, 0)

        def lambda_grad(tile, g_re, g_im):
            rows = pl.ds(pl.multiple_of(tile * SUBLANES, SUBLANES), SUBLANES)
            before = pl.ds(pl.multiple_of(jnp.maximum(tile - 1, 0) * SUBLANES, SUBLANES), SUBLANES)
            row = lax.broadcasted_iota(jnp.int32, g_re.shape, 0)
            live = jnp.where(tile > 0, 1.0, 0.0)
            prev = []
            for ref in (sr_ref, si_ref):
                here = pltpu.roll(ref[rows, :], 1, 0)
                last = pltpu.roll(ref[before, :], 1, 0) * live
                prev.append(jnp.where(row == 0, last, here))
            acc_r[...] += g_re * prev[0] + g_im * prev[1]
            acc_i[...] += g_im * prev[0] - g_re * prev[1]

        _scan(gr_ref, gi_ref, lr_ref[...], -li_ref[...], True, t, per_tile=lambda_grad)
        dlr_ref[...] = jnp.sum(acc_r[...], axis=0, keepdims=True)
        dli_ref[...] = jnp.sum(acc_i[...], axis=0, keepdims=True)

        def through_b(i, _):
            rows = pl.ds(pl.multiple_of(i * _S5_ROWS, _S5_ROWS), _S5_ROWS)
            ub = u_ref[rows, :].astype(BF16)
            grb, gib = gr_ref[rows, :].astype(BF16), gi_ref[rows, :].astype(BF16)
            dbre_ref[...] += _dot(ub, grb, 0, 0)
            dbim_ref[...] += _dot(ub, gib, 0, 0)
            du_ref[rows, :] = _dot(grb, bre_ref[...], 1, 1) + _dot(gib, bim_ref[...], 1, 1) + d_ref[...] * dy_ref[rows, :]
            return 0

        lax.fori_loop(0, t // _S5_ROWS, through_b, 0)

    col = pl.BlockSpec((t, SSM_CH_BLOCK), lambda j: (0, j))
    blk3 = lambda shape: pl.BlockSpec((None,) + shape, lambda j: (j, 0, 0))
    state = pltpu.VMEM((t, SSM_ST_BLOCK), F32)
    return _call(
        body, name="s5_bwd", grid=(n_blocks,), in_specs=_s5_in_specs(t, d_attn) + [col, col, col],
        out_specs=[col, blk3((SSM_CH_BLOCK, SSM_ST_BLOCK)), blk3((SSM_CH_BLOCK, SSM_ST_BLOCK)),
                   blk3((1, SSM_ST_BLOCK)), blk3((1, SSM_ST_BLOCK)),
                   blk3((SSM_ST_BLOCK, SSM_CH_BLOCK)), blk3((SSM_ST_BLOCK, SSM_CH_BLOCK)),
                   pl.BlockSpec((1, SSM_CH_BLOCK), lambda j: (0, j))],
        out_shape=[_sds((t, d_ssm), F32),
                   _sds((n_blocks, SSM_CH_BLOCK, SSM_ST_BLOCK), F32), _sds((n_blocks, SSM_CH_BLOCK, SSM_ST_BLOCK), F32),
                   _sds((n_blocks, 1, SSM_ST_BLOCK), F32), _sds((n_blocks, 1, SSM_ST_BLOCK), F32),
                   _sds((n_blocks, SSM_ST_BLOCK, SSM_CH_BLOCK), F32), _sds((n_blocks, SSM_ST_BLOCK, SSM_CH_BLOCK), F32),
                   _sds((1, d_ssm), F32)],
        scratch_shapes=[state, state, state, state, pltpu.VMEM((t, SSM_CH_BLOCK), F32),
                        pltpu.VMEM((SUBLANES, SSM_ST_BLOCK), F32), pltpu.VMEM((SUBLANES, SSM_ST_BLOCK), F32)],
        semantics=("parallel",))(proj, *mats, dskip_row, y, dz_a, dz_b)


def _block_diag_in(bbar_pgn):
    p, g, n = bbar_pgn.shape
    b4 = bbar_pgn.reshape(p, g // GROUPS_PER_BLOCK, GROUPS_PER_BLOCK, n)
    eye = jnp.eye(GROUPS_PER_BLOCK, dtype=F32)
    return jnp.einsum("pjgn,gh->jgphn", b4, eye).reshape(g // GROUPS_PER_BLOCK, SSM_CH_BLOCK, SSM_ST_BLOCK)


def _block_diag_in_t(dense):
    j = dense.shape[0]
    d5 = dense.reshape(j, GROUPS_PER_BLOCK, SSM_GROUP, GROUPS_PER_BLOCK, SSM_STATE)
    eye = jnp.eye(GROUPS_PER_BLOCK, dtype=F32)
    return jnp.einsum("jgphn,gh->pjgn", d5, eye).reshape(SSM_GROUP, j * GROUPS_PER_BLOCK, SSM_STATE)


def _block_diag_out(c_gpn):
    g, p, n = c_gpn.shape
    c4 = c_gpn.reshape(g // GROUPS_PER_BLOCK, GROUPS_PER_BLOCK, p, n)
    eye = jnp.eye(GROUPS_PER_BLOCK, dtype=F32)
    return jnp.einsum("jgpn,gh->jgnhp", c4, eye).reshape(g // GROUPS_PER_BLOCK, SSM_ST_BLOCK, SSM_CH_BLOCK)


def _block_diag_out_t(dense):
    j = dense.shape[0]
    d5 = dense.reshape(j, GROUPS_PER_BLOCK, SSM_STATE, GROUPS_PER_BLOCK, SSM_GROUP)
    eye = jnp.eye(GROUPS_PER_BLOCK, dtype=F32)
    return jnp.einsum("jgnhp,gh->jgpn", d5, eye).reshape(j * GROUPS_PER_BLOCK, SSM_GROUP, SSM_STATE)


def _adamw(w, g, m, v):
    m = ADAM_B1 * m + (1.0 - ADAM_B1) * g
    v = ADAM_B2 * v + (1.0 - ADAM_B2) * (g * g)
    m_hat = m / (1.0 - ADAM_B1 ** ADAM_STEP)
    v_hat = v / (1.0 - ADAM_B2 ** ADAM_STEP)
    delta = -ADAM_LR * (m_hat / (jnp.sqrt(v_hat) + ADAM_EPS) + ADAM_WD * w)
    return delta, m, v


def _adam_sharded(name, parts, w, m, v, tr):
    r, c = w.shape
    assert r % tr == 0, (name, r, tr)

    def body(p_ref, w_ref, m_ref, v_ref, g_out, d_out, m_out, v_out):
        g = p_ref[0].astype(F32)
        for i in range(1, N_DEV):
            g = g + p_ref[i].astype(F32)
        delta, m_new, v_new = _adamw(w_ref[...], g, m_ref[...], v_ref[...])
        g_out[...] = g
        d_out[...] = delta
        m_out[...] = m_new
        v_out[...] = v_new

    tile = pl.BlockSpec((tr, c), lambda i: (i, 0))
    return _call(body, name=name, grid=(r // tr,),
                 in_specs=[pl.BlockSpec((N_DEV, tr, c), lambda i: (0, i, 0)), tile, tile, tile],
                 out_specs=[tile] * 4, out_shape=[_sds((r, c), F32)] * 4, semantics=("parallel",))(parts, w, m, v)


_SMALL = ("g_pre_mix", "sinks", "a_re", "a_im", "log_dt", "b_re", "b_im", "c_re", "c_im", "d_skip", "b_glu",
          "g_attn_out", "g_ssm_out", "g_post_mix", "g_pre_ffn", "g_post_ffn")
_BIG = ("w_in", "w_glu", "w_o", "w_gate", "w_up", "w_down")
_ORDER = ("g_pre_mix", "w_in", "sinks", "a_re", "a_im", "log_dt", "b_re", "b_im", "c_re", "c_im", "d_skip", "w_glu",
          "b_glu", "g_attn_out", "g_ssm_out", "w_o", "g_post_mix", "g_pre_ffn", "w_gate", "w_up", "w_down",
          "g_post_ffn")


def _pack(arrays):
    flat = jnp.concatenate([a.reshape(-1).astype(F32) for a in arrays])
    pad = (-flat.shape[0]) % (SUBLANES * LANES)
    return jnp.pad(flat, (0, pad)).reshape(-1, LANES)


def _unpack(packed, like):
    flat = packed.reshape(-1)
    out, at = [], 0
    for a in like:
        out.append(flat[at:at + a.size].reshape(a.shape))
        at += a.size
    return out


def kernel(x, positions, g_pre_mix, w_in, sinks, a_re, a_im, log_dt, b_re, b_im, c_re, c_im, d_skip, w_glu, b_glu, g_attn_out, g_ssm_out, w_o, g_post_mix, g_pre_ffn, w_gate, w_up, w_down, g_post_ffn, loss_target, m_g_pre_mix, m_w_in, m_sinks, m_a_re, m_a_im, m_log_dt, m_b_re, m_b_im, m_c_re, m_c_im, m_d_skip, m_w_glu, m_b_glu, m_g_attn_out, m_g_ssm_out, m_w_o, m_g_post_mix, m_g_pre_ffn, m_w_gate, m_w_up, m_w_down, m_g_post_ffn, v_g_pre_mix, v_w_in, v_sinks, v_a_re, v_a_im, v_log_dt, v_b_re, v_b_im, v_c_re, v_c_im, v_d_skip, v_w_glu, v_b_glu, v_g_attn_out, v_g_ssm_out, v_w_o, v_g_post_mix, v_g_pre_ffn, v_w_gate, v_w_up, v_w_down, v_g_post_ffn):
    given = dict(locals())
    weights = {n: given[n] for n in _ORDER}
    mom_m = {n: given["m_" + n] for n in _ORDER}
    mom_v = {n: given["v_" + n] for n in _ORDER}

    t, d = x.shape[1], x.shape[2]
    d_attn = d // 2
    d_ssm = d - d_attn
    d_in = d_attn + 2 * D_KV + d_ssm
    n_groups = d_ssm // SSM_GROUP
    n_heads = d_attn // HEAD_DIM
    tm = min(256, t)

    x2 = x[0]
    target = loss_target[0]

    def start_gather(name, ws, token):
        behind = 0 if token is None else token[0, 0].astype(BF16)
        return _exchange_start(name, [w[0].astype(BF16) + behind for w in ws], False, (SIBLING,) + CHIP_PEERS)

    def finish_gather(handle, after):
        forward, _ = _forward_start(handle["name"] + "_forward", _exchange_wait(handle, after))
        return _split_wait(forward, [])

    ag_in, token = start_gather("gather_w_in", [w_in], None)
    ag_mix, token = start_gather("gather_w_glu_o", [w_glu, w_o], token)
    ag_ffn_in, token = start_gather("gather_w_gate_up", [w_gate, w_up], token)
    ag_down, token = start_gather("gather_w_down", [w_down], token)

    xn, = _rows("norm_in", lambda xv, g: ([_rms(xv)[0] * g], []), [x2], [g_pre_mix], [(d, BF16)], [], tm,
                after=[token])
    win_g, = finish_gather(ag_in, [xn])
    w_in_full = win_g.transpose(1, 0, 2).reshape(d, d_in)
    proj = _mm_nn("proj_in", xn, w_in_full, F32, tn=d_in // 4 if (d_in // 4) % LANES == 0 else None)

    cos, sin = _rope_tables(positions.reshape(t, 1).astype(F32))
    sinks_row = jnp.pad(sinks, ((0, 0), (0, LANES - n_heads)))
    attn = _attention_fwd(proj, cos, sin, sinks_row, d_attn)

    b_re_t, b_im_t = b_re[0].transpose(2, 0, 1), b_im[0].transpose(2, 0, 1)
    ldt_col = log_dt.reshape(n_groups, 1)
    lam_re, lam_im, bbar_re, bbar_im = _s5_discretise(a_re[0], a_im[0], ldt_col, b_re_t, b_im_t)
    n_blocks = n_groups // GROUPS_PER_BLOCK
    mats = [_block_diag_in(bbar_re).astype(BF16), _block_diag_in(bbar_im).astype(BF16),
            lam_re.reshape(n_blocks, 1, SSM_ST_BLOCK), lam_im.reshape(n_blocks, 1, SSM_ST_BLOCK),
            _block_diag_out(c_re[0]).astype(BF16), _block_diag_out(c_im[0]).astype(BF16)]
    dskip_row = d_skip.reshape(1, d_ssm)
    y_ssm, z_ssm = _s5_fwd(proj, mats, dskip_row, d_attn, d_ssm)
    wglu_g, wo_g = finish_gather(ag_mix, [attn, z_ssm])
    w_glu_full = wglu_g.reshape(d_ssm, d_ssm)
    w_o_full = wo_g.reshape(d, d)
    glu_lin = _mm_nn("glu_gate", z_ssm, w_glu_full, F32)

    def mix_prep(av, yv, gl, bg, ga, gs):
        ssm = _gelu(yv) * _sigmoid(gl + bg)
        return [jnp.concatenate([_rms(av)[0] * ga, _rms(ssm)[0] * gs], axis=1)], []

    mixed, = _rows("mix_prep", mix_prep, [attn, y_ssm, glu_lin], [b_glu, g_attn_out, g_ssm_out], [(d, BF16)], [], tm)
    mix = _mm_nn("mix_out", mixed, w_o_full, F32, tn=d // 2 if (d // 2) % LANES == 0 else None)

    def post_mix(xv, mv, gpm, gpf):
        h = xv + _rms(mv)[0] * gpm
        return [h, _rms(h)[0] * gpf], []

    h, hn = _rows("post_mix", post_mix, [x2, mix], [g_post_mix, g_pre_ffn], [(d, F32), (d, BF16)], [], tm)
    wgate_g, wup_g = finish_gather(ag_ffn_in, [hn])
    gate, up, hid = _ffn_in(hn, wgate_g, wup_g)
    wdown_g, = finish_gather(ag_down, [hid])
    ff = _mm_contract_slots("ffn_down", [(hid, wdown_g)], F32)

    def head(hv, fv, tv, gpo):
        out = hv + _rms(fv)[0] * gpo
        err = out - tv
        dout = err * (1.0 / d)
        dff, dg = _rms_bwd(fv, gpo, dout)
        loss = jnp.zeros((1, LANES), F32) + 0.5 * jnp.sum(err * err) * (1.0 / d)
        return [dff, dout], [dg, loss]

    dff, dh_out, dg_post_ffn, loss_row = _rows("loss_head", head, [h, ff, target], [g_post_ffn],
                                               [(d, BF16), (d, F32)], [d, LANES], tm)

    dw_down = _mm_slots_tn("ffn_down_dw", hid, dff, BF16)
    rs_down, tok_down = _exchange_start("scatter_dw_down", [dw_down], True)
    dgate, dup = _ffn_down_bwd(dff, wdown_g, gate, up, [tok_down])
    dhn = _mm_contract_slots_nt("ffn_in_dx", [(dgate, wgate_g), (dup, wup_g)], F32)
    dw_gate = _mm_tn_slots("ffn_gate_dw", hn, dgate, BF16)
    dw_up = _mm_tn_slots("ffn_up_dw", hn, dup, BF16)
    rs_ffn_in, tok_ffn_in = _exchange_start("scatter_dw_gate_up", [dw_gate, dw_up], True)

    def mid_bwd(dho, dhn_, hv, mv, gpf, gpm):
        d1, dgpf = _rms_bwd(hv, gpf, dhn_)
        dh_ = dho + d1
        dmix_, dgpm = _rms_bwd(mv, gpm, dh_)
        return [dh_, dmix_], [dgpf, dgpm]

    dh, dmix, dg_pre_ffn, dg_post_mix = _rows("mid_bwd", mid_bwd, [dh_out, dhn, h, mix], [g_pre_ffn, g_post_mix],
                                              [(d, F32), (d, BF16)], [d, d], tm, after=[tok_ffn_in])

    dmixed = _mm_nt("mix_out_dx", dmix, w_o_full, F32, tn=d // 2 if (d // 2) % LANES == 0 else None)
    dw_o = _mm_tn("mix_out_dw", mixed, dmix, BF16, tn=d // 2 if (d // 2) % LANES == 0 else None)
    rs_o, tok_o = _exchange_start("scatter_dw_o", [dw_o.reshape(N_DEV, d // N_DEV, d)], True)

    def mix_bwd(dm, av, yv, gl, bg, ga, gs):
        dattn_, dga = _rms_bwd(av, ga, dm[:, :d_attn])
        z = _gelu(yv)
        sg = _sigmoid(gl + bg)
        dssm, dgs = _rms_bwd(z * sg, gs, dm[:, d_attn:])
        dgl = dssm * z * sg * (1.0 - sg)
        return [dattn_, dssm * sg, dgl], [dga, dgs, jnp.sum(dgl, axis=0, keepdims=True)]

    dattn, dz_direct, dglu, dg_attn_out, dg_ssm_out, db_glu = _rows(
        "mix_bwd", mix_bwd, [dmixed, attn, y_ssm, glu_lin], [b_glu, g_attn_out, g_ssm_out],
        [(d_attn, F32), (d_ssm, F32), (d_ssm, BF16)], [d_attn, d_ssm, d_ssm], tm, after=[tok_o])
    dz_glu = _mm_nt("glu_gate_dx", dglu, w_glu_full, F32)
    dw_glu = _mm_tn("glu_gate_dw", z_ssm, dglu, BF16)

    du, db_re_dense, db_im_dense, dlam_re, dlam_im, dc_re_dense, dc_im_dense, dd_skip = _s5_bwd(
        proj, mats, dskip_row, y_ssm, dz_direct, dz_glu, d_attn, d_ssm)
    da_re, da_im, dlog_dt, db_re_t, db_im_t = _s5_discretise_bwd(
        a_re[0], a_im[0], ldt_col, b_re_t, b_im_t, dlam_re.reshape(n_groups, SSM_STATE),
        dlam_im.reshape(n_groups, SSM_STATE), _block_diag_in_t(db_re_dense), _block_diag_in_t(db_im_dense))
    dq, dk2, dv2, dsinks_row = _attention_bwd(proj, cos, sin, sinks_row, dattn, d_attn)
    dproj = _assemble_dproj(dq, dk2, dv2, du, d_in)

    dxn = _mm_nt("proj_in_dx", dproj, w_in_full, F32, tn=d // 2 if (d // 2) % LANES == 0 else None)
    c_sh = d_in // N_DEV
    dproj_sh = dproj.reshape(t, N_DEV, c_sh).transpose(1, 0, 2)
    dw_in = _mm_tn_slots("proj_in_dw", xn, dproj_sh, BF16)

    def x_bwd(dh_, dxn_, xv, g):
        dx, dg = _rms_bwd(xv, g, dxn_)
        return [dh_ + dx], [dg]

    grad_x, dg_pre_mix = _rows("norm_in_bwd", x_bwd, [dh, dxn, x2], [g_pre_mix], [(d, F32)], [d], tm)

    small_grads = {
        "g_pre_mix": dg_pre_mix, "sinks": dsinks_row[:, :n_heads], "a_re": da_re[None], "a_im": da_im[None],
        "log_dt": dlog_dt.reshape(1, n_groups), "b_re": db_re_t.transpose(1, 2, 0)[None],
        "b_im": db_im_t.transpose(1, 2, 0)[None], "c_re": _block_diag_out_t(dc_re_dense)[None],
        "c_im": _block_diag_out_t(dc_im_dense)[None], "d_skip": dd_skip.reshape(d_skip.shape), "b_glu": db_glu,
        "g_attn_out": dg_attn_out, "g_ssm_out": dg_ssm_out, "g_post_mix": dg_post_mix, "g_pre_ffn": dg_pre_ffn,
        "g_post_ffn": dg_post_ffn,
    }
    small_like = [weights[n] for n in _SMALL]
    packed = _pack([small_grads[n] for n in _SMALL])
    rs_in, token = _exchange_start("scatter_dw_in_glu", [dw_in, dw_glu.reshape(N_DEV, d_ssm // N_DEV, d_ssm)], True)
    ag_small, token = _exchange_start("gather_small_grads", [packed + token[:1, :]], False)

    results = {}

    def adam_big(n, parts):
        r = weights[n].shape[1]
        results[n] = _adam_sharded("adam_" + n, parts, weights[n][0], mom_m[n][0], mom_v[n][0],
                                   64 if r % 64 == 0 else r)
        return results[n][3]

    done = [grad_x, token]
    adam_big("w_down", _exchange_wait(rs_down, done)[0])
    p_gate, p_up = _exchange_wait(rs_ffn_in, done)
    done = [adam_big("w_gate", p_gate), adam_big("w_up", p_up), results["w_down"][3]]
    done = [adam_big("w_o", _exchange_wait(rs_o, done)[0])]
    p_in, p_glu = _exchange_wait(rs_in, done)
    done = [adam_big("w_in", p_in), adam_big("w_glu", p_glu)]
    small_all, = _exchange_wait(ag_small, done)
    g_s, d_s, m_s, v_s = _adam_sharded(
        "adam_small", small_all, _pack(small_like), _pack([mom_m[n] for n in _SMALL]),
        _pack([mom_v[n] for n in _SMALL]), packed.shape[0])
    for i, vals in enumerate(zip(*[_unpack(p, small_like) for p in (g_s, d_s, m_s, v_s)])):
        results[_SMALL[i]] = vals

    loss = lax.psum(loss_row[0, 0], ("x", "y", "c"))
    outs = [loss, grad_x[None]]
    for k in range(4):
        for n in _ORDER:
            val = results[n][k]
            outs.append(val[None] if n in _BIG else val)
    return tuple(outs)
```

```python
import math

import jax
import jax.numpy as jnp
from jax import lax
from jax.experimental import pallas as pl
from jax.experimental.pallas import tpu as pltpu

F32 = jnp.float32
BF16 = jnp.bfloat16

HEAD_DIM = 64
N_KV_HEADS = 4
D_KV = N_KV_HEADS * HEAD_DIM
WINDOW = 128
BLOCK = 128
ROPE_THETA = 10000.0
SSM_GROUP = 16
SSM_STATE = 64
GROUPS_PER_BLOCK = 8
SSM_CH_BLOCK = GROUPS_PER_BLOCK * SSM_GROUP
SSM_ST_BLOCK = GROUPS_PER_BLOCK * SSM_STATE
RMS_EPS = 1e-6
N_DEV = 8
LANES = 128
SUBLANES = 8
MASKED = -1e30

ADAM_LR = 0.001
ADAM_B1 = 0.9
ADAM_B2 = 0.999
ADAM_EPS = 1e-08
ADAM_WD = 0.01
ADAM_STEP = 10

VMEM_LIMIT_BYTES = 56 * 1024 * 1024


def _call(body, *, name, out_shape, in_specs, out_specs, grid=(), scratch_shapes=(), semantics=None, n_after=0):
    params = dict(vmem_limit_bytes=VMEM_LIMIT_BYTES)
    if semantics is not None:
        params["dimension_semantics"] = semantics
    n_in = len(in_specs)
    if n_after:
        inner = body

        def body(*refs):
            inner(*refs[:n_in], *refs[n_in + n_after:])

        in_specs = list(in_specs) + [pl.BlockSpec(memory_space=pl.ANY)] * n_after
    return pl.pallas_call(body, name=name, grid=grid, in_specs=in_specs, out_specs=out_specs, out_shape=out_shape,
                          scratch_shapes=scratch_shapes, compiler_params=pltpu.CompilerParams(**params))


def _sds(shape, dtype):
    return jax.ShapeDtypeStruct(tuple(shape), dtype)


def _dot(a, b, ca, cb):
    return lax.dot_general(a, b, (((ca,), (cb,)), ((), ())), preferred_element_type=F32)


def _rms(x):
    r = lax.rsqrt(jnp.mean(x * x, axis=-1, keepdims=True) + RMS_EPS)
    return x * r, r


def _rms_bwd(x, g, dy):
    xh, r = _rms(x)
    dxh = dy * g
    dx = r * (dxh - xh * jnp.mean(dxh * xh, axis=-1, keepdims=True))
    return dx, jnp.sum(dy * xh, axis=0, keepdims=True)


def _sigmoid(x):
    return 1.0 / (1.0 + jnp.exp(-x))


_GELU_C = math.sqrt(2.0 / math.pi)
_GELU_A = 0.044715


def _gelu(y):
    t = jnp.tanh(_GELU_C * (y + _GELU_A * y * y * y))
    return 0.5 * y * (1.0 + t)


def _gelu_grad(y):
    t = jnp.tanh(_GELU_C * (y + _GELU_A * y * y * y))
    return 0.5 * (1.0 + t) + 0.5 * y * (1.0 - t * t) * _GELU_C * (1.0 + 3.0 * _GELU_A * y * y)


def _rows(name, fn, row_ins, vec_ins, row_outs, acc_widths, tm, after=()):
    rows = row_ins[0].shape[0]
    assert rows % tm == 0, (name, rows, tm)
    n_row, n_vec, n_out, n_acc = len(row_ins), len(vec_ins), len(row_outs), len(acc_widths)

    def body(*refs):
        ins = [r[...] for r in refs[:n_row + n_vec]]
        outs = refs[n_row + n_vec:n_row + n_vec + n_out]
        accs = refs[n_row + n_vec + n_out:]
        row_vals, acc_vals = fn(*ins)
        for o, v in zip(outs, row_vals):
            o[...] = v.astype(o.dtype)
        if n_acc:
            @pl.when(pl.program_id(0) == 0)
            def _():
                for a in accs:
                    a[...] = jnp.zeros_like(a)
            for a, v in zip(accs, acc_vals):
                a[...] += v

    in_specs = [pl.BlockSpec((tm, a.shape[1]), lambda i: (i, 0)) for a in row_ins]
    in_specs += [pl.BlockSpec(v.shape, lambda i: (0, 0)) for v in vec_ins]
    out_specs = [pl.BlockSpec((tm, w), lambda i: (i, 0)) for w, _ in row_outs]
    out_specs += [pl.BlockSpec((1, w), lambda i: (0, 0)) for w in acc_widths]
    out_shape = [_sds((rows, w), dt) for w, dt in row_outs] + [_sds((1, w), F32) for w in acc_widths]
    return _call(body, name=name, grid=(rows // tm,), in_specs=in_specs, out_specs=out_specs, out_shape=out_shape,
                 semantics=("arbitrary",) if n_acc else ("parallel",), n_after=len(after))(*row_ins, *vec_ins, *after)


def _matmul(name, operands, in_specs, product, grid, out_shape, out_spec, acc_shape):
    nk = grid[-1]
    n_in = len(operands)
    in_place = out_shape.dtype == F32

    def body(*refs):
        ins = [r[...] for r in refs[:n_in]]
        o_ref = refs[n_in]
        if nk == 1:
            o_ref[...] = product(*ins).astype(o_ref.dtype)
            return
        acc = o_ref if in_place else refs[n_in + 1]
        k = pl.program_id(len(grid) - 1)

        @pl.when(k == 0)
        def _():
            acc[...] = jnp.zeros_like(acc)

        acc[...] += product(*ins)

        if not in_place:
            @pl.when(k == nk - 1)
            def _():
                o_ref[...] = acc[...].astype(o_ref.dtype)

    return _call(body, name=name, grid=grid, in_specs=in_specs, out_specs=out_spec, out_shape=out_shape,
                 scratch_shapes=[] if nk == 1 or in_place else [pltpu.VMEM(acc_shape, F32)],
                 semantics=("parallel",) * (len(grid) - 1) + ("arbitrary",))(*operands)


def _mm_nn(name, a, b, out_dtype, tm=512, tn=None):
    m, k = a.shape
    n = b.shape[1]
    tm, tn = min(tm, m), n if tn is None else tn
    return _matmul(name, [a, b],
                   [pl.BlockSpec((tm, k), lambda i, j, s: (i, 0)), pl.BlockSpec((k, tn), lambda i, j, s: (0, j))],
                   lambda x, y: _dot(x, y, 1, 0), (m // tm, n // tn, 1), _sds((m, n), out_dtype),
                   pl.BlockSpec((tm, tn), lambda i, j, s: (i, j)), (tm, tn))


def _mm_nt(name, a, b, out_dtype, tm=512, tn=None):
    m, k = a.shape
    n = b.shape[0]
    tm, tn = min(tm, m), n if tn is None else tn
    return _matmul(name, [a, b],
                   [pl.BlockSpec((tm, k), lambda i, j, s: (i, 0)), pl.BlockSpec((tn, k), lambda i, j, s: (j, 0))],
                   lambda x, y: _dot(x, y, 1, 1), (m // tm, n // tn, 1), _sds((m, n), out_dtype),
                   pl.BlockSpec((tm, tn), lambda i, j, s: (i, j)), (tm, tn))


def _mm_tn(name, a, b, out_dtype, tm=512, tn=None, tk=2048):
    k, m = a.shape
    n = b.shape[1]
    tm, tk, tn = min(tm, m), min(tk, k), n if tn is None else tn
    return _matmul(name, [a, b],
                   [pl.BlockSpec((tk, tm), lambda i, j, s: (s, i)), pl.BlockSpec((tk, tn), lambda i, j, s: (s, j))],
                   lambda x, y: _dot(x, y, 0, 0), (m // tm, n // tn, k // tk), _sds((m, n), out_dtype),
                   pl.BlockSpec((tm, tn), lambda i, j, s: (i, j)), (tm, tn))


def _mm_contract_slots(name, pairs, out_dtype, tm=512, tn=2048):
    s_, m, k = pairs[0][0].shape
    n = pairs[0][1].shape[2]
    tm, tn = min(tm, m), min(tn, n)
    ops, specs = [], []
    for a, b in pairs:
        ops += [a, b]
        specs += [pl.BlockSpec((None, tm, k), lambda i, j, s: (s, i, 0)), pl.BlockSpec((None, k, tn), lambda i, j, s: (s, 0, j))]

    def product(*t):
        return sum(_dot(t[2 * p], t[2 * p + 1], 1, 0) for p in range(len(pairs)))

    return _matmul(name, ops, specs, product, (m // tm, n // tn, s_), _sds((m, n), out_dtype),
                   pl.BlockSpec((tm, tn), lambda i, j, s: (i, j)), (tm, tn))


def _mm_contract_slots_nt(name, pairs, out_dtype, tm=512, tn=2048):
    s_, m, k = pairs[0][0].shape
    n = pairs[0][1].shape[1]
    tm, tn = min(tm, m), min(tn, n)
    ops, specs = [], []
    for a, b in pairs:
        ops += [a, b]
        specs += [pl.BlockSpec((None, tm, k), lambda i, j, s: (s, i, 0)), pl.BlockSpec((None, tn, k), lambda i, j, s: (s, j, 0))]

    def product(*t):
        return sum(_dot(t[2 * p], t[2 * p + 1], 1, 1) for p in range(len(pairs)))

    return _matmul(name, ops, specs, product, (m // tm, n // tn, s_), _sds((m, n), out_dtype),
                   pl.BlockSpec((tm, tn), lambda i, j, s: (i, j)), (tm, tn))


def _mm_tn_slots(name, a, b, out_dtype, tm=2048, tk=2048):
    k, m = a.shape
    s_, _, n = b.shape
    tm, tk = min(tm, m), min(tk, k)
    return _matmul(name, [a, b],
                   [pl.BlockSpec((tk, tm), lambda s, i, z: (z, i)), pl.BlockSpec((None, tk, n), lambda s, i, z: (s, z, 0))],
                   lambda x, y: _dot(x, y, 0, 0), (s_, m // tm, k // tk), _sds((s_, m, n), out_dtype),
                   pl.BlockSpec((None, tm, n), lambda s, i, z: (s, i, 0)), (tm, n))


def _mm_slots_tn(name, a, b, out_dtype, tn=2048, tk=2048):
    s_, k, m = a.shape
    n = b.shape[1]
    tn, tk = min(tn, n), min(tk, k)
    return _matmul(name, [a, b],
                   [pl.BlockSpec((None, tk, m), lambda s, j, z: (s, z, 0)), pl.BlockSpec((tk, tn), lambda s, j, z: (z, j))],
                   lambda x, y: _dot(x, y, 0, 0), (s_, n // tn, k // tk), _sds((s_, m, n), out_dtype),
                   pl.BlockSpec((None, m, tn), lambda s, j, z: (s, 0, j)), (m, tn))


def _ffn_in(a, w_gate, w_up, tm=512):
    m, k = a.shape
    s_, _, n = w_gate.shape
    tm = min(tm, m)

    def body(a_ref, wg_ref, wu_ref, g_ref, u_ref, h_ref):
        x = a_ref[...]
        g = _dot(x, wg_ref[...], 1, 0)
        u = _dot(x, wu_ref[...], 1, 0)
        g_ref[...] = g.astype(BF16)
        u_ref[...] = u.astype(BF16)
        h_ref[...] = (g * _sigmoid(g) * u).astype(BF16)

    w_spec = pl.BlockSpec((None, k, n), lambda s, i: (s, 0, 0))
    o_spec = pl.BlockSpec((None, tm, n), lambda s, i: (s, i, 0))
    return _call(body, name="ffn_in", grid=(s_, m // tm),
                 in_specs=[pl.BlockSpec((tm, k), lambda s, i: (i, 0)), w_spec, w_spec], out_specs=[o_spec] * 3,
                 out_shape=[_sds((s_, m, n), BF16)] * 3, semantics=("parallel", "parallel"))(a, w_gate, w_up)


def _ffn_down_bwd(d_out, w_down, gate, up, after, tm=512):
    m, k = d_out.shape
    s_, n, _ = w_down.shape
    tm = min(tm, m)

    def body(d_ref, w_ref, g_ref, u_ref, dg_ref, du_ref):
        dh = _dot(d_ref[...], w_ref[...], 1, 1)
        g = g_ref[...].astype(F32)
        sg = _sigmoid(g)
        dg_ref[...] = (dh * u_ref[...].astype(F32) * sg * (1.0 + g * (1.0 - sg))).astype(BF16)
        du_ref[...] = (dh * g * sg).astype(BF16)

    t_spec = pl.BlockSpec((None, tm, n), lambda s, i: (s, i, 0))
    return _call(body, name="ffn_down_dx", grid=(s_, m // tm),
                 in_specs=[pl.BlockSpec((tm, k), lambda s, i: (i, 0)), pl.BlockSpec((None, n, k), lambda s, i: (s, 0, 0)),
                           t_spec, t_spec],
                 out_specs=[t_spec] * 2, out_shape=[_sds((s_, m, n), BF16)] * 2, semantics=("parallel", "parallel"),
                 n_after=len(after))(d_out, w_down, gate, up, *after)


ALL_PEERS = (1, 2, 3, 4, 5, 6, 7)
CHIP_PEERS = (2, 4, 6)
SIBLING = 1


def _peer(relation):
    x, y, c = lax.axis_index("x"), lax.axis_index("y"), lax.axis_index("c")
    pos = (1 - x if relation & 4 else x, 1 - y if relation & 2 else y, 1 - c if relation & 1 else c)
    return pos, 4 * pos[0] + 2 * pos[1] + pos[2]


def _exchange_copies(ins, lands, send_sems, recv_sems, scatter, relations):
    _, me = _peer(0)

    def copy(a, s, peer, pos, dst_slot):
        return pltpu.make_async_remote_copy(
            src_ref=ins[a].at[peer] if scatter else ins[a], dst_ref=lands[a].at[dst_slot],
            send_sem=send_sems.at[s], recv_sem=recv_sems.at[s], device_id=pos, device_id_type=pl.DeviceIdType.MESH)

    pairs = []
    for k, r in enumerate(relations):
        pos, peer = _peer(r)
        for a in range(len(ins)):
            s = a * len(relations) + k
            pairs.append((copy(a, s, peer, pos, me), copy(a, s, peer, pos, peer)))
    return me, pairs


def _forward_copies(lands, send_sems, recv_sems):
    sibling, _ = _peer(SIBLING)

    def copy(a, s, slot):
        return pltpu.make_async_remote_copy(
            src_ref=lands[a].at[slot], dst_ref=lands[a].at[slot], send_sem=send_sems.at[s], recv_sem=recv_sems.at[s],
            device_id=sibling, device_id_type=pl.DeviceIdType.MESH)

    pairs = []
    for k, r in enumerate(CHIP_PEERS):
        _, mine = _peer(r)
        _, theirs = _peer(r | SIBLING)
        for a in range(len(lands)):
            s = a * len(CHIP_PEERS) + k
            pairs.append((copy(a, s, mine), copy(a, s, theirs)))
    return pairs


_HBM_SPEC = pl.BlockSpec(memory_space=pltpu.HBM)
_SEM_SPEC = pl.BlockSpec(memory_space=pltpu.SEMAPHORE)
_SIDE_EFFECT = pltpu.SideEffectType.DATAFLOW_SIDE_EFFECTING


def _split_start(name, operands, n_sem, make_pairs):
    k = len(operands)

    def body(*refs):
        send_sems, recv_sems, token = refs[k], refs[k + 1], refs[-1]
        for send, _ in make_pairs(refs[:k], send_sems, recv_sems):
            send.start()
        token[...] = jnp.zeros_like(token)

    out = pl.pallas_call(
        body, name=name,
        out_shape=(pltpu.SemaphoreType.DMA((n_sem,)), pltpu.SemaphoreType.DMA((n_sem,)),
                   *[pltpu.HBM(a.shape, a.dtype) for a in operands], _sds((SUBLANES, LANES), F32)),
        in_specs=[_HBM_SPEC] * k,
        out_specs=(_SEM_SPEC, _SEM_SPEC, *[_HBM_SPEC] * k, pl.BlockSpec(memory_space=pltpu.VMEM)),
        input_output_aliases={i: 2 + i for i in range(k)},
        compiler_params=pltpu.CompilerParams(has_side_effects=_SIDE_EFFECT),
    )(*[pltpu.with_memory_space_constraint(a, pltpu.HBM) for a in operands])
    return dict(name=name, sems=out[:2], thru=list(out[2:2 + k]), make_pairs=make_pairs), out[-1]


def _split_wait(handle, after):
    thru, make_pairs = handle["thru"], handle["make_pairs"]
    k = len(thru)

    def body(*refs):
        for send, arrival in make_pairs(refs[:k], refs[k], refs[k + 1]):
            send.wait_send()
            arrival.wait_recv()

    return pl.pallas_call(
        body, name=handle["name"] + "_wait", out_shape=[pltpu.HBM(a.shape, a.dtype) for a in thru],
        in_specs=[_HBM_SPEC] * k + [_SEM_SPEC, _SEM_SPEC] + [pl.BlockSpec(memory_space=pl.ANY)] * len(after),
        out_specs=[_HBM_SPEC] * k, input_output_aliases={i: i for i in range(k)},
        compiler_params=pltpu.CompilerParams(has_side_effects=_SIDE_EFFECT),
    )(*thru, *handle["sems"], *after)


def _exchange_start(name, arrays, scatter, relations=ALL_PEERS):
    n = len(arrays)
    lands = [lax.empty(a.shape if scatter else (N_DEV,) + a.shape, a.dtype) for a in arrays]

    def make_pairs(refs, send_sems, recv_sems):
        return _exchange_copies(refs[:n], refs[n:], send_sems, recv_sems, scatter, relations)[1]

    handle, token = _split_start(name, list(arrays) + lands, n * len(relations), make_pairs)
    handle.update(n=n, scatter=scatter)
    return handle, token


def _forward_start(name, lands):
    return _split_start(name, list(lands), len(lands) * len(CHIP_PEERS), _forward_copies)


def _exchange_wait(handle, after):
    n, scatter = handle["n"], handle["scatter"]
    out = _split_wait(handle, after)
    me = 4 * lax.axis_index("x") + 2 * lax.axis_index("y") + lax.axis_index("c")
    done = []
    for src, land in zip(out[:n], out[n:]):
        own = lax.dynamic_index_in_dim(src, me, 0, keepdims=True) if scatter else src[None]
        done.append(lax.dynamic_update_slice_in_dim(land, own, me, 0))
    return done


def _rope_tables(pos_col):
    t = pos_col.shape[0]
    half = HEAD_DIM // 2
    inv_freq = ROPE_THETA ** (-jnp.arange(half, dtype=F32) / half)
    inv_row = jnp.tile(inv_freq, LANES // half)[None, :]

    def body(pos_ref, inv_ref, cos_ref, sin_ref):
        ang = pos_ref[...] * inv_ref[...]
        cos_ref[...] = jnp.cos(ang)
        sin_ref[...] = jnp.sin(ang)

    tm = min(t, 512)
    return _call(body, name="rope_tables", grid=(t // tm,),
                 in_specs=[pl.BlockSpec((tm, 1), lambda i: (i, 0)), pl.BlockSpec((1, LANES), lambda i: (0, 0))],
                 out_specs=[pl.BlockSpec((tm, LANES), lambda i: (i, 0))] * 2,
                 out_shape=[_sds((t, LANES), F32)] * 2, semantics=("parallel",))(pos_col, inv_row)


def _rot_half(x):
    lane = lax.broadcasted_iota(jnp.int32, x.shape, 1)
    low = (lane % HEAD_DIM) < HEAD_DIM // 2
    return jnp.where(low, -pltpu.roll(x, LANES - HEAD_DIM // 2, 1), pltpu.roll(x, HEAD_DIM // 2, 1))


def _rope(x, cos, sin):
    return x * cos + _rot_half(x) * sin


def _unrope(d, cos, sin):
    return d * cos - _rot_half(d) * sin


def _band_mask(first_block, heads):
    r = lax.broadcasted_iota(jnp.int32, (heads * BLOCK, 2 * BLOCK), 0) % BLOCK
    c = lax.broadcasted_iota(jnp.int32, (heads * BLOCK, 2 * BLOCK), 1)
    diff = r - c + BLOCK
    return (diff >= 0) & (diff < WINDOW) & ((c >= BLOCK) | jnp.logical_not(first_block))


def _attn_specs(t, d_attn, d_in):
    kb, vb = d_attn // D_KV, d_attn // D_KV + 1
    prev = lambda i: jnp.maximum(i - 1, 0)
    return [
        pl.BlockSpec((BLOCK, d_attn), lambda i: (i, 0)),
        pl.BlockSpec((BLOCK, D_KV), lambda i: (i, kb)),
        pl.BlockSpec((BLOCK, D_KV), lambda i: (i, vb)),
        pl.BlockSpec((BLOCK, D_KV), lambda i: (prev(i), kb)),
        pl.BlockSpec((BLOCK, D_KV), lambda i: (prev(i), vb)),
        pl.BlockSpec((BLOCK, LANES), lambda i: (i, 0)),
        pl.BlockSpec((BLOCK, LANES), lambda i: (i, 0)),
        pl.BlockSpec((BLOCK, LANES), lambda i: (prev(i), 0)),
        pl.BlockSpec((BLOCK, LANES), lambda i: (prev(i), 0)),
        pl.BlockSpec((1, LANES), lambda i: (0, 0)),
    ]


def _head(x, h):
    return x[:, h * HEAD_DIM:(h + 1) * HEAD_DIM]


def _attn_heads(q_ref, kc_ref, vc_ref, kp_ref, vp_ref, cq_ref, sq_ref, cp_ref, sp_ref, d_attn):
    cq, sq, cp, sp = cq_ref[...], sq_ref[...], cp_ref[...], sp_ref[...]
    q_rot = [_rope(q_ref[:, j * LANES:(j + 1) * LANES], cq, sq) for j in range(d_attn // LANES)]
    kc_rot = [_rope(kc_ref[:, j * LANES:(j + 1) * LANES], cq, sq) for j in range(D_KV // LANES)]
    kp_rot = [_rope(kp_ref[:, j * LANES:(j + 1) * LANES], cp, sp) for j in range(D_KV // LANES)]
    per = LANES // HEAD_DIM
    q_heads = [_head(q_rot[h // per], h % per).astype(BF16) for h in range(d_attn // HEAD_DIM)]
    kk = [jnp.concatenate([_head(kp_rot[g // per], g % per), _head(kc_rot[g // per], g % per)], axis=0).astype(BF16)
          for g in range(N_KV_HEADS)]
    vv = [jnp.concatenate([_head(vp_ref[...], g), _head(vc_ref[...], g)], axis=0).astype(BF16) for g in range(N_KV_HEADS)]
    return q_heads, kk, vv


def _stack_group(q_heads, sink_ref, group):
    q_all = jnp.concatenate([q_heads[h] for h in group], axis=0)
    sink_all = jnp.concatenate([jnp.broadcast_to(sink_ref[:, h:h + 1], (BLOCK, 1)) for h in group], axis=0)
    return q_all, sink_all


def _softmax_with_sink(q, kk, sink, mask):
    s = _dot(q, kk, 1, 1) * (1.0 / math.sqrt(HEAD_DIM))
    s = jnp.where(mask, s, MASKED)
    m = jnp.maximum(jnp.max(s, axis=-1, keepdims=True), sink)
    p = jnp.exp(s - m)
    e_sink = jnp.exp(sink - m)
    inv = 1.0 / (jnp.sum(p, axis=-1, keepdims=True) + e_sink)
    return p * inv, e_sink * inv


def _attention_fwd(proj, cos, sin, sinks_row, d_attn):
    t, d_in = proj.shape
    n_heads = d_attn // HEAD_DIM
    q_per_kv = n_heads // N_KV_HEADS

    def body(q_ref, kc_ref, vc_ref, kp_ref, vp_ref, cq_ref, sq_ref, cp_ref, sp_ref, sink_ref, o_ref):
        mask = _band_mask(pl.program_id(0) == 0, q_per_kv)
        q_heads, kk, vv = _attn_heads(q_ref, kc_ref, vc_ref, kp_ref, vp_ref, cq_ref, sq_ref, cp_ref, sp_ref, d_attn)
        for g in range(N_KV_HEADS):
            group = range(g * q_per_kv, (g + 1) * q_per_kv)
            q_all, sink_all = _stack_group(q_heads, sink_ref, group)
            probs, _ = _softmax_with_sink(q_all, kk[g], sink_all, mask)
            o_all = _dot(probs.astype(BF16), vv[g], 1, 0)
            for k, h in enumerate(group):
                o_ref[:, h * HEAD_DIM:(h + 1) * HEAD_DIM] = o_all[k * BLOCK:(k + 1) * BLOCK]

    return _call(body, name="attention_fwd", grid=(t // BLOCK,), in_specs=_attn_specs(t, d_attn, d_in),
                 out_specs=pl.BlockSpec((BLOCK, d_attn), lambda i: (i, 0)), out_shape=_sds((t, d_attn), F32),
                 semantics=("parallel",))(proj, proj, proj, proj, proj, cos, sin, cos, sin, sinks_row)


def _attention_bwd(proj, cos, sin, sinks_row, d_out, d_attn):
    t, d_in = proj.shape
    n_heads = d_attn // HEAD_DIM
    q_per_kv = n_heads // N_KV_HEADS
    nb = t // BLOCK
    per = LANES // HEAD_DIM

    def body(q_ref, kc_ref, vc_ref, kp_ref, vp_ref, cq_ref, sq_ref, cp_ref, sp_ref, sink_ref, do_ref,
             dq_ref, dk_ref, dv_ref, dsink_ref):
        i = pl.program_id(0)
        mask = _band_mask(i == 0, q_per_kv)
        q_heads, kk, vv = _attn_heads(q_ref, kc_ref, vc_ref, kp_ref, vp_ref, cq_ref, sq_ref, cp_ref, sp_ref, d_attn)
        lane = lax.broadcasted_iota(jnp.int32, (1, LANES), 1)
        dsink = jnp.zeros((1, LANES), F32)
        dq_rot, dkk, dvv = [], [], []
        for g in range(N_KV_HEADS):
            group = range(g * q_per_kv, (g + 1) * q_per_kv)
            q_all, sink_all = _stack_group(q_heads, sink_ref, group)
            probs, p_sink = _softmax_with_sink(q_all, kk[g], sink_all, mask)
            do_all = jnp.concatenate([do_ref[:, h * HEAD_DIM:(h + 1) * HEAD_DIM] for h in group], axis=0).astype(BF16)
            dp = _dot(do_all, vv[g], 1, 1)
            delta = jnp.sum(probs * dp, axis=-1, keepdims=True)
            ds = (probs * (dp - delta) * (1.0 / math.sqrt(HEAD_DIM))).astype(BF16)
            dq_all = _dot(ds, kk[g], 1, 0)
            dkk.append(_dot(ds, q_all, 0, 0))
            dvv.append(_dot(probs.astype(BF16), do_all, 0, 0))
            sink_term = p_sink * delta
            for k, h in enumerate(group):
                dq_rot.append(dq_all[k * BLOCK:(k + 1) * BLOCK])
                part = jnp.sum(sink_term[k * BLOCK:(k + 1) * BLOCK], axis=0, keepdims=True)
                dsink += jnp.where(lane == h, -part, 0.0)
        cq, sq, cp, sp = cq_ref[...], sq_ref[...], cp_ref[...], sp_ref[...]
        for j in range(d_attn // LANES):
            d = jnp.concatenate(dq_rot[j * per:(j + 1) * per], axis=1)
            dq_ref[:, j * LANES:(j + 1) * LANES] = _unrope(d, cq, sq)
        for j in range(D_KV // LANES):
            d = jnp.concatenate(dkk[j * per:(j + 1) * per], axis=1)
            dk_ref[0, :, j * LANES:(j + 1) * LANES] = _unrope(d[:BLOCK], cp, sp)
            dk_ref[1, :, j * LANES:(j + 1) * LANES] = _unrope(d[BLOCK:], cq, sq)
            d = jnp.concatenate(dvv[j * per:(j + 1) * per], axis=1)
            dv_ref[0, :, j * LANES:(j + 1) * LANES] = d[:BLOCK]
            dv_ref[1, :, j * LANES:(j + 1) * LANES] = d[BLOCK:]

        @pl.when(i == 0)
        def _():
            dsink_ref[...] = jnp.zeros_like(dsink_ref)

        dsink_ref[...] += dsink

    pair = pl.BlockSpec((2, BLOCK, D_KV), lambda i: (i, 0, 0))
    return _call(body, name="attention_bwd", grid=(nb,),
                 in_specs=_attn_specs(t, d_attn, d_in) + [pl.BlockSpec((BLOCK, d_attn), lambda i: (i, 0))],
                 out_specs=[pl.BlockSpec((BLOCK, d_attn), lambda i: (i, 0)), pair, pair,
                            pl.BlockSpec((1, LANES), lambda i: (0, 0))],
                 out_shape=[_sds((t, d_attn), F32), _sds((2 * nb, BLOCK, D_KV), F32), _sds((2 * nb, BLOCK, D_KV), F32),
                            _sds((1, LANES), F32)],
                 semantics=("arbitrary",))(proj, proj, proj, proj, proj, cos, sin, cos, sin, sinks_row, d_out)


def _assemble_dproj(dq, dk2, dv2, du, d_in):
    t, d_attn = dq.shape
    d_ssm = du.shape[1]
    nb = t // BLOCK

    def body(dq_ref, dk_own, dk_next, dv_own, dv_next, du_ref, o_ref):
        has_next = (pl.program_id(0) < nb - 1).astype(F32)
        o_ref[:, :d_attn] = dq_ref[...].astype(BF16)
        o_ref[:, d_attn:d_attn + D_KV] = (dk_own[...] + has_next * dk_next[...]).astype(BF16)
        o_ref[:, d_attn + D_KV:d_attn + 2 * D_KV] = (dv_own[...] + has_next * dv_next[...]).astype(BF16)
        o_ref[:, d_attn + 2 * D_KV:] = du_ref[...].astype(BF16)

    own = pl.BlockSpec((None, BLOCK, D_KV), lambda i: (2 * i + 1, 0, 0))
    nxt = pl.BlockSpec((None, BLOCK, D_KV), lambda i: (jnp.minimum(2 * i + 2, 2 * nb - 1), 0, 0))
    return _call(body, name="assemble_dproj", grid=(nb,),
                 in_specs=[pl.BlockSpec((BLOCK, d_attn), lambda i: (i, 0)), own, nxt, own, nxt,
                           pl.BlockSpec((BLOCK, d_ssm), lambda i: (i, 0))],
                 out_specs=pl.BlockSpec((BLOCK, d_in), lambda i: (i, 0)), out_shape=_sds((t, d_in), BF16),
                 semantics=("parallel",))(dq, dk2, dk2, dv2, dv2, du)


def _discretise(ar, ai, ldt, br, bi):
    dt = jnp.exp(ldt)
    mag = jnp.exp(ar * dt)
    lam_re = mag * jnp.cos(ai * dt)
    lam_im = mag * jnp.sin(ai * dt)
    den = ar * ar + ai * ai
    nr = lam_re - 1.0
    ni = lam_im
    f_re = (nr * ar + ni * ai) / den
    f_im = (ni * ar - nr * ai) / den
    return lam_re, lam_im, f_re[None] * br - f_im[None] * bi, f_re[None] * bi + f_im[None] * br


def _whole(arrays):
    return [pl.BlockSpec(a.shape, lambda *_, nd=len(a.shape): (0,) * nd) for a in arrays]


def _s5_discretise(ar, ai, ldt, br, bi):
    ins = [ar, ai, ldt, br, bi]

    def body(ar_ref, ai_ref, ldt_ref, br_ref, bi_ref, lr_ref, li_ref, bbr_ref, bbi_ref):
        out = _discretise(ar_ref[...], ai_ref[...], ldt_ref[...], br_ref[...], bi_ref[...])
        for ref, val in zip((lr_ref, li_ref, bbr_ref, bbi_ref), out):
            ref[...] = val

    outs = [_sds(ar.shape, F32), _sds(ar.shape, F32), _sds(br.shape, F32), _sds(br.shape, F32)]
    return _call(body, name="s5_discretise", in_specs=_whole(ins), out_specs=_whole(outs), out_shape=outs)(*ins)


def _s5_discretise_bwd(ar, ai, ldt, br, bi, d_lr, d_li, d_bbr, d_bbi):
    ins = [ar, ai, ldt, br, bi, d_lr, d_li, d_bbr, d_bbi]

    def body(ar_ref, ai_ref, ldt_ref, br_ref, bi_ref, dlr_ref, dli_ref, dbbr_ref, dbbi_ref, *out_refs):
        _, vjp = jax.vjp(_discretise, ar_ref[...], ai_ref[...], ldt_ref[...], br_ref[...], bi_ref[...])
        grads = vjp((dlr_ref[...], dli_ref[...], dbbr_ref[...], dbbi_ref[...]))
        for ref, val in zip(out_refs, grads):
            ref[...] = val

    outs = [_sds(a.shape, F32) for a in (ar, ai, ldt, br, bi)]
    return _call(body, name="s5_discretise_bwd", in_specs=_whole(ins), out_specs=_whole(outs), out_shape=outs)(*ins)


def _cmul(ar, ai, br, bi):
    return ar * br - ai * bi, ar * bi + ai * br


def _power_table(lr, li, reverse):
    pows = [(lr, li)]
    for _ in range(SUBLANES - 1):
        pows.append(_cmul(pows[-1][0], pows[-1][1], lr, li))
    row = lax.broadcasted_iota(jnp.int32, (SUBLANES, lr.shape[1]), 0)
    tr = jnp.zeros((SUBLANES, lr.shape[1]), F32)
    ti = jnp.zeros((SUBLANES, lr.shape[1]), F32)
    for r in range(SUBLANES):
        src = pows[SUBLANES - 1 - r] if reverse else pows[r]
        tr = jnp.where(row == r, src[0], tr)
        ti = jnp.where(row == r, src[1], ti)
    return pows[0], pows[1], pows[3], (tr, ti)


def _scan_tile(xr, xi, steps, table, carry, reverse):
    row = lax.broadcasted_iota(jnp.int32, xr.shape, 0)
    for k, (lr, li) in zip((1, 2, 4), steps):
        if reverse:
            keep = row < SUBLANES - k
            sr = jnp.where(keep, pltpu.roll(xr, SUBLANES - k, 0), 0.0)
            si = jnp.where(keep, pltpu.roll(xi, SUBLANES - k, 0), 0.0)
        else:
            keep = row >= k
            sr = jnp.where(keep, pltpu.roll(xr, k, 0), 0.0)
            si = jnp.where(keep, pltpu.roll(xi, k, 0), 0.0)
        pr, pi = _cmul(lr, li, sr, si)
        xr, xi = xr + pr, xi + pi
    pr, pi = _cmul(table[0], table[1], carry[0], carry[1])
    return xr + pr, xi + pi


def _scan(sr_ref, si_ref, lr, li, reverse, t, per_tile=None):
    l1, l2, l4, table = _power_table(lr, li, reverse)
    n_tiles = t // SUBLANES
    w = lr.shape[1]

    def step(i, carry):
        tile = (n_tiles - 1 - i) if reverse else i
        rows = pl.ds(pl.multiple_of(tile * SUBLANES, SUBLANES), SUBLANES)
        xr, xi = _scan_tile(sr_ref[rows, :], si_ref[rows, :], (l1, l2, l4), table, carry, reverse)
        sr_ref[rows, :] = xr
        si_ref[rows, :] = xi
        if per_tile is not None:
            per_tile(tile, xr, xi)
        edge = 0 if reverse else SUBLANES - 1
        return xr[edge:edge + 1, :], xi[edge:edge + 1, :]

    lax.fori_loop(0, n_tiles, step, (jnp.zeros((1, w), F32), jnp.zeros((1, w), F32)))


_S5_ROWS = 256


def _s5_in_specs(t, d_attn):
    u_block = (d_attn + 2 * D_KV) // SSM_CH_BLOCK
    blk3 = lambda shape: pl.BlockSpec((None,) + shape, lambda j: (j, 0, 0))
    return [
        pl.BlockSpec((t, SSM_CH_BLOCK), lambda j: (0, u_block + j)),
        blk3((SSM_CH_BLOCK, SSM_ST_BLOCK)), blk3((SSM_CH_BLOCK, SSM_ST_BLOCK)),
        blk3((1, SSM_ST_BLOCK)), blk3((1, SSM_ST_BLOCK)),
        blk3((SSM_ST_BLOCK, SSM_CH_BLOCK)), blk3((SSM_ST_BLOCK, SSM_CH_BLOCK)),
        pl.BlockSpec((1, SSM_CH_BLOCK), lambda j: (0, j)),
    ]


def _s5_states(u_ref, bre_ref, bim_ref, lr_ref, li_ref, sr_ref, si_ref, t):
    def fill(i, _):
        rows = pl.ds(pl.multiple_of(i * _S5_ROWS, _S5_ROWS), _S5_ROWS)
        ub = u_ref[rows, :].astype(BF16)
        sr_ref[rows, :] = _dot(ub, bre_ref[...], 1, 0)
        si_ref[rows, :] = _dot(ub, bim_ref[...], 1, 0)
        return 0

    lax.fori_loop(0, t // _S5_ROWS, fill, 0)
    _scan(sr_ref, si_ref, lr_ref[...], li_ref[...], False, t)


def _s5_fwd(proj, mats, dskip_row, d_attn, d_ssm):
    t = proj.shape[0]
    n_blocks = d_ssm // SSM_CH_BLOCK

    def body(u_ref, bre_ref, bim_ref, lr_ref, li_ref, cre_ref, cim_ref, d_ref, y_ref, z_ref, sr_ref, si_ref):
        _s5_states(u_ref, bre_ref, bim_ref, lr_ref, li_ref, sr_ref, si_ref, t)

        def emit(i, _):
            rows = pl.ds(pl.multiple_of(i * _S5_ROWS, _S5_ROWS), _S5_ROWS)
            y = (_dot(sr_ref[rows, :].astype(BF16), cre_ref[...], 1, 0)
                 - _dot(si_ref[rows, :].astype(BF16), cim_ref[...], 1, 0) + d_ref[...] * u_ref[rows, :])
            y_ref[rows, :] = y
            z_ref[rows, :] = _gelu(y).astype(BF16)
            return 0

        lax.fori_loop(0, t // _S5_ROWS, emit, 0)

    col = pl.BlockSpec((t, SSM_CH_BLOCK), lambda j: (0, j))
    return _call(body, name="s5_fwd", grid=(n_blocks,), in_specs=_s5_in_specs(t, d_attn), out_specs=[col, col],
                 out_shape=[_sds((t, d_ssm), F32), _sds((t, d_ssm), BF16)],
                 scratch_shapes=[pltpu.VMEM((t, SSM_ST_BLOCK), F32)] * 2,
                 semantics=("parallel",))(proj, *mats, dskip_row)


def _s5_bwd(proj, mats, dskip_row, y, dz_a, dz_b, d_attn, d_ssm):
    t = proj.shape[0]
    n_blocks = d_ssm // SSM_CH_BLOCK

    def body(u_ref, bre_ref, bim_ref, lr_ref, li_ref, cre_ref, cim_ref, d_ref, y_ref, dza_ref, dzb_ref,
             du_ref, dbre_ref, dbim_ref, dlr_ref, dli_ref, dcre_ref, dcim_ref, dd_ref,
             sr_ref, si_ref, gr_ref, gi_ref, dy_ref, acc_r, acc_i):
        _s5_states(u_ref, bre_ref, bim_ref, lr_ref, li_ref, sr_ref, si_ref, t)
        for ref in (dcre_ref, dcim_ref, dbre_ref, dbim_ref, dd_ref, acc_r, acc_i):
            ref[...] = jnp.zeros_like(ref)

        def through_c(i, _):
            rows = pl.ds(pl.multiple_of(i * _S5_ROWS, _S5_ROWS), _S5_ROWS)
            dy = (dza_ref[rows, :] + dzb_ref[rows, :]) * _gelu_grad(y_ref[rows, :])
            dy_ref[rows, :] = dy
            dd_ref[...] += jnp.sum(dy * u_ref[rows, :], axis=0, keepdims=True)
            dyb = dy.astype(BF16)
            gr_ref[rows, :] = _dot(dyb, cre_ref[...], 1, 1)
            gi_ref[rows, :] = -_dot(dyb, cim_ref[...], 1, 1)
            dcre_ref[...] += _dot(sr_ref[rows, :].astype(BF16), dyb, 0, 0)
            dcim_ref[...] -= _dot(si_ref[rows, :].astype(BF16), dyb, 0, 0)
            return 0

        lax.fori_loop(0, t // _S5_ROWS, through_c, 0)

        def lambda_grad(tile, g_re, g_im):
            rows = pl.ds(pl.multiple_of(tile * SUBLANES, SUBLANES), SUBLANES)
            before = pl.ds(pl.multiple_of(jnp.maximum(tile - 1, 0) * SUBLANES, SUBLANES), SUBLANES)
            row = lax.broadcasted_iota(jnp.int32, g_re.shape, 0)
            live = jnp.where(tile > 0, 1.0, 0.0)
            prev = []
            for ref in (sr_ref, si_ref):
                here = pltpu.roll(ref[rows, :], 1, 0)
                last = pltpu.roll(ref[before, :], 1, 0) * live
                prev.append(jnp.where(row == 0, last, here))
            acc_r[...] += g_re * prev[0] + g_im * prev[1]
            acc_i[...] += g_im * prev[0] - g_re * prev[1]

        _scan(gr_ref, gi_ref, lr_ref[...], -li_ref[...], True, t, per_tile=lambda_grad)
        dlr_ref[...] = jnp.sum(acc_r[...], axis=0, keepdims=True)
        dli_ref[...] = jnp.sum(acc_i[...], axis=0, keepdims=True)

        def through_b(i, _):
            rows = pl.ds(pl.multiple_of(i * _S5_ROWS, _S5_ROWS), _S5_ROWS)
            ub = u_ref[rows, :].astype(BF16)
            grb, gib = gr_ref[rows, :].astype(BF16), gi_ref[rows, :].astype(BF16)
            dbre_ref[...] += _dot(ub, grb, 0, 0)
            dbim_ref[...] += _dot(ub, gib, 0, 0)
            du_ref[rows, :] = _dot(grb, bre_ref[...], 1, 1) + _dot(gib, bim_ref[...], 1, 1) + d_ref[...] * dy_ref[rows, :]
            return 0

        lax.fori_loop(0, t // _S5_ROWS, through_b, 0)

    col = pl.BlockSpec((t, SSM_CH_BLOCK), lambda j: (0, j))
    blk3 = lambda shape: pl.BlockSpec((None,) + shape, lambda j: (j, 0, 0))
    state = pltpu.VMEM((t, SSM_ST_BLOCK), F32)
    return _call(
        body, name="s5_bwd", grid=(n_blocks,), in_specs=_s5_in_specs(t, d_attn) + [col, col, col],
        out_specs=[col, blk3((SSM_CH_BLOCK, SSM_ST_BLOCK)), blk3((SSM_CH_BLOCK, SSM_ST_BLOCK)),
                   blk3((1, SSM_ST_BLOCK)), blk3((1, SSM_ST_BLOCK)),
                   blk3((SSM_ST_BLOCK, SSM_CH_BLOCK)), blk3((SSM_ST_BLOCK, SSM_CH_BLOCK)),
                   pl.BlockSpec((1, SSM_CH_BLOCK), lambda j: (0, j))],
        out_shape=[_sds((t, d_ssm), F32),
                   _sds((n_blocks, SSM_CH_BLOCK, SSM_ST_BLOCK), F32), _sds((n_blocks, SSM_CH_BLOCK, SSM_ST_BLOCK), F32),
                   _sds((n_blocks, 1, SSM_ST_BLOCK), F32), _sds((n_blocks, 1, SSM_ST_BLOCK), F32),
                   _sds((n_blocks, SSM_ST_BLOCK, SSM_CH_BLOCK), F32), _sds((n_blocks, SSM_ST_BLOCK, SSM_CH_BLOCK), F32),
                   _sds((1, d_ssm), F32)],
        scratch_shapes=[state, state, state, state, pltpu.VMEM((t, SSM_CH_BLOCK), F32),
                        pltpu.VMEM((SUBLANES, SSM_ST_BLOCK), F32), pltpu.VMEM((SUBLANES, SSM_ST_BLOCK), F32)],
        semantics=("parallel",))(proj, *mats, dskip_row, y, dz_a, dz_b)


def _block_diag_in(bbar_pgn):
    p, g, n = bbar_pgn.shape
    b4 = bbar_pgn.reshape(p, g // GROUPS_PER_BLOCK, GROUPS_PER_BLOCK, n)
    eye = jnp.eye(GROUPS_PER_BLOCK, dtype=F32)
    return jnp.einsum("pjgn,gh->jgphn", b4, eye).reshape(g // GROUPS_PER_BLOCK, SSM_CH_BLOCK, SSM_ST_BLOCK)


def _block_diag_in_t(dense):
    j = dense.shape[0]
    d5 = dense.reshape(j, GROUPS_PER_BLOCK, SSM_GROUP, GROUPS_PER_BLOCK, SSM_STATE)
    eye = jnp.eye(GROUPS_PER_BLOCK, dtype=F32)
    return jnp.einsum("jgphn,gh->pjgn", d5, eye).reshape(SSM_GROUP, j * GROUPS_PER_BLOCK, SSM_STATE)


def _block_diag_out(c_gpn):
    g, p, n = c_gpn.shape
    c4 = c_gpn.reshape(g // GROUPS_PER_BLOCK, GROUPS_PER_BLOCK, p, n)
    eye = jnp.eye(GROUPS_PER_BLOCK, dtype=F32)
    return jnp.einsum("jgpn,gh->jgnhp", c4, eye).reshape(g // GROUPS_PER_BLOCK, SSM_ST_BLOCK, SSM_CH_BLOCK)


def _block_diag_out_t(dense):
    j = dense.shape[0]
    d5 = dense.reshape(j, GROUPS_PER_BLOCK, SSM_STATE, GROUPS_PER_BLOCK, SSM_GROUP)
    eye = jnp.eye(GROUPS_PER_BLOCK, dtype=F32)
    return jnp.einsum("jgnhp,gh->jgpn", d5, eye).reshape(j * GROUPS_PER_BLOCK, SSM_GROUP, SSM_STATE)


def _adamw(w, g, m, v):
    m = ADAM_B1 * m + (1.0 - ADAM_B1) * g
    v = ADAM_B2 * v + (1.0 - ADAM_B2) * (g * g)
    m_hat = m / (1.0 - ADAM_B1 ** ADAM_STEP)
    v_hat = v / (1.0 - ADAM_B2 ** ADAM_STEP)
    delta = -ADAM_LR * (m_hat / (jnp.sqrt(v_hat) + ADAM_EPS) + ADAM_WD * w)
    return delta, m, v


def _adam_sharded(name, parts, w, m, v, tr):
    r, c = w.shape
    assert r % tr == 0, (name, r, tr)

    def body(p_ref, w_ref, m_ref, v_ref, g_out, d_out, m_out, v_out):
        g = p_ref[0].astype(F32)
        for i in range(1, N_DEV):
            g = g + p_ref[i].astype(F32)
        delta, m_new, v_new = _adamw(w_ref[...], g, m_ref[...], v_ref[...])
        g_out[...] = g
        d_out[...] = delta
        m_out[...] = m_new
        v_out[...] = v_new

    tile = pl.BlockSpec((tr, c), lambda i: (i, 0))
    return _call(body, name=name, grid=(r // tr,),
                 in_specs=[pl.BlockSpec((N_DEV, tr, c), lambda i: (0, i, 0)), tile, tile, tile],
                 out_specs=[tile] * 4, out_shape=[_sds((r, c), F32)] * 4, semantics=("parallel",))(parts, w, m, v)


_SMALL = ("g_pre_mix", "sinks", "a_re", "a_im", "log_dt", "b_re", "b_im", "c_re", "c_im", "d_skip", "b_glu",
          "g_attn_out", "g_ssm_out", "g_post_mix", "g_pre_ffn", "g_post_ffn")
_BIG = ("w_in", "w_glu", "w_o", "w_gate", "w_up", "w_down")
_ORDER = ("g_pre_mix", "w_in", "sinks", "a_re", "a_im", "log_dt", "b_re", "b_im", "c_re", "c_im", "d_skip", "w_glu",
          "b_glu", "g_attn_out", "g_ssm_out", "w_o", "g_post_mix", "g_pre_ffn", "w_gate", "w_up", "w_down",
          "g_post_ffn")


def _pack(arrays):
    flat = jnp.concatenate([a.reshape(-1).astype(F32) for a in arrays])
    pad = (-flat.shape[0]) % (SUBLANES * LANES)
    return jnp.pad(flat, (0, pad)).reshape(-1, LANES)


def _unpack(packed, like):
    flat = packed.reshape(-1)
    out, at = [], 0
    for a in like:
        out.append(flat[at:at + a.size].reshape(a.shape))
        at += a.size
    return out


def kernel(x, positions, g_pre_mix, w_in, sinks, a_re, a_im, log_dt, b_re, b_im, c_re, c_im, d_skip, w_glu, b_glu, g_attn_out, g_ssm_out, w_o, g_post_mix, g_pre_ffn, w_gate, w_up, w_down, g_post_ffn, loss_target, m_g_pre_mix, m_w_in, m_sinks, m_a_re, m_a_im, m_log_dt, m_b_re, m_b_im, m_c_re, m_c_im, m_d_skip, m_w_glu, m_b_glu, m_g_attn_out, m_g_ssm_out, m_w_o, m_g_post_mix, m_g_pre_ffn, m_w_gate, m_w_up, m_w_down, m_g_post_ffn, v_g_pre_mix, v_w_in, v_sinks, v_a_re, v_a_im, v_log_dt, v_b_re, v_b_im, v_c_re, v_c_im, v_d_skip, v_w_glu, v_b_glu, v_g_attn_out, v_g_ssm_out, v_w_o, v_g_post_mix, v_g_pre_ffn, v_w_gate, v_w_up, v_w_down, v_g_post_ffn):
    given = dict(locals())
    weights = {n: given[n] for n in _ORDER}
    mom_m = {n: given["m_" + n] for n in _ORDER}
    mom_v = {n: given["v_" + n] for n in _ORDER}

    t, d = x.shape[1], x.shape[2]
    d_attn = d // 2
    d_ssm = d - d_attn
    d_in = d_attn + 2 * D_KV + d_ssm
    n_groups = d_ssm // SSM_GROUP
    n_heads = d_attn // HEAD_DIM
    tm = min(256, t)

    x2 = x[0]
    target = loss_target[0]

    def start_gather(name, ws, token):
        behind = 0 if token is None else token[0, 0].astype(BF16)
        return _exchange_start(name, [w[0].astype(BF16) + behind for w in ws], False, (SIBLING,) + CHIP_PEERS)

    def finish_gather(handle, after):
        forward, _ = _forward_start(handle["name"] + "_forward", _exchange_wait(handle, after))
        return _split_wait(forward, [])

    ag_in, token = start_gather("gather_w_in", [w_in], None)
    ag_mix, token = start_gather("gather_w_glu_o", [w_glu, w_o], token)
    ag_ffn_in, token = start_gather("gather_w_gate_up", [w_gate, w_up], token)
    ag_down, token = start_gather("gather_w_down", [w_down], token)

    xn, = _rows("norm_in", lambda xv, g: ([_rms(xv)[0] * g], []), [x2], [g_pre_mix], [(d, BF16)], [], tm,
                after=[token])
    win_g, = finish_gather(ag_in, [xn])
    w_in_full = win_g.transpose(1, 0, 2).reshape(d, d_in)
    proj = _mm_nn("proj_in", xn, w_in_full, F32, tn=d_in // 4 if (d_in // 4) % LANES == 0 else None)

    cos, sin = _rope_tables(positions.reshape(t, 1).astype(F32))
    sinks_row = jnp.pad(sinks, ((0, 0), (0, LANES - n_heads)))
    attn = _attention_fwd(proj, cos, sin, sinks_row, d_attn)

    b_re_t, b_im_t = b_re[0].transpose(2, 0, 1), b_im[0].transpose(2, 0, 1)
    ldt_col = log_dt.reshape(n_groups, 1)
    lam_re, lam_im, bbar_re, bbar_im = _s5_discretise(a_re[0], a_im[0], ldt_col, b_re_t, b_im_t)
    n_blocks = n_groups // GROUPS_PER_BLOCK
    mats = [_block_diag_in(bbar_re).astype(BF16), _block_diag_in(bbar_im).astype(BF16),
            lam_re.reshape(n_blocks, 1, SSM_ST_BLOCK), lam_im.reshape(n_blocks, 1, SSM_ST_BLOCK),
            _block_diag_out(c_re[0]).astype(BF16), _block_diag_out(c_im[0]).astype(BF16)]
    dskip_row = d_skip.reshape(1, d_ssm)
    y_ssm, z_ssm = _s5_fwd(proj, mats, dskip_row, d_attn, d_ssm)
    wglu_g, wo_g = finish_gather(ag_mix, [attn, z_ssm])
    w_glu_full = wglu_g.reshape(d_ssm, d_ssm)
    w_o_full = wo_g.reshape(d, d)
    glu_lin = _mm_nn("glu_gate", z_ssm, w_glu_full, F32)

    def mix_prep(av, yv, gl, bg, ga, gs):
        ssm = _gelu(yv) * _sigmoid(gl + bg)
        return [jnp.concatenate([_rms(av)[0] * ga, _rms(ssm)[0] * gs], axis=1)], []

    mixed, = _rows("mix_prep", mix_prep, [attn, y_ssm, glu_lin], [b_glu, g_attn_out, g_ssm_out], [(d, BF16)], [], tm)
    mix = _mm_nn("mix_out", mixed, w_o_full, F32, tn=d // 2 if (d // 2) % LANES == 0 else None)

    def post_mix(xv, mv, gpm, gpf):
        h = xv + _rms(mv)[0] * gpm
        return [h, _rms(h)[0] * gpf], []

    h, hn = _rows("post_mix", post_mix, [x2, mix], [g_post_mix, g_pre_ffn], [(d, F32), (d, BF16)], [], tm)
    wgate_g, wup_g = finish_gather(ag_ffn_in, [hn])
    gate, up, hid = _ffn_in(hn, wgate_g, wup_g)
    wdown_g, = finish_gather(ag_down, [hid])
    ff = _mm_contract_slots("ffn_down", [(hid, wdown_g)], F32)

    def head(hv, fv, tv, gpo):
        out = hv + _rms(fv)[0] * gpo
        err = out - tv
        dout = err * (1.0 / d)
        dff, dg = _rms_bwd(fv, gpo, dout)
        loss = jnp.zeros((1, LANES), F32) + 0.5 * jnp.sum(err * err) * (1.0 / d)
        return [dff, dout], [dg, loss]

    dff, dh_out, dg_post_ffn, loss_row = _rows("loss_head", head, [h, ff, target], [g_post_ffn],
                                               [(d, BF16), (d, F32)], [d, LANES], tm)

    dw_down = _mm_slots_tn("ffn_down_dw", hid, dff, BF16)
    rs_down, tok_down = _exchange_start("scatter_dw_down", [dw_down], True)
    dgate, dup = _ffn_down_bwd(dff, wdown_g, gate, up, [tok_down])
    dhn = _mm_contract_slots_nt("ffn_in_dx", [(dgate, wgate_g), (dup, wup_g)], F32)
    dw_gate = _mm_tn_slots("ffn_gate_dw", hn, dgate, BF16)
    dw_up = _mm_tn_slots("ffn_up_dw", hn, dup, BF16)
    rs_ffn_in, tok_ffn_in = _exchange_start("scatter_dw_gate_up", [dw_gate, dw_up], True)

    def mid_bwd(dho, dhn_, hv, mv, gpf, gpm):
        d1, dgpf = _rms_bwd(hv, gpf, dhn_)
        dh_ = dho + d1
        dmix_, dgpm = _rms_bwd(mv, gpm, dh_)
        return [dh_, dmix_], [dgpf, dgpm]

    dh, dmix, dg_pre_ffn, dg_post_mix = _rows("mid_bwd", mid_bwd, [dh_out, dhn, h, mix], [g_pre_ffn, g_post_mix],
                                              [(d, F32), (d, BF16)], [d, d], tm, after=[tok_ffn_in])

    dmixed = _mm_nt("mix_out_dx", dmix, w_o_full, F32, tn=d // 2 if (d // 2) % LANES == 0 else None)
    dw_o = _mm_tn("mix_out_dw", mixed, dmix, BF16, tn=d // 2 if (d // 2) % LANES == 0 else None)
    rs_o, tok_o = _exchange_start("scatter_dw_o", [dw_o.reshape(N_DEV, d // N_DEV, d)], True)

    def mix_bwd(dm, av, yv, gl, bg, ga, gs):
        dattn_, dga = _rms_bwd(av, ga, dm[:, :d_attn])
        z = _gelu(yv)
        sg = _sigmoid(gl + bg)
        dssm, dgs = _rms_bwd(z * sg, gs, dm[:, d_attn:])
        dgl = dssm * z * sg * (1.0 - sg)
        return [dattn_, dssm * sg, dgl], [dga, dgs, jnp.sum(dgl, axis=0, keepdims=True)]

    dattn, dz_direct, dglu, dg_attn_out, dg_ssm_out, db_glu = _rows(
        "mix_bwd", mix_bwd, [dmixed, attn, y_ssm, glu_lin], [b_glu, g_attn_out, g_ssm_out],
        [(d_attn, F32), (d_ssm, F32), (d_ssm, BF16)], [d_attn, d_ssm, d_ssm], tm, after=[tok_o])
    dz_glu = _mm_nt("glu_gate_dx", dglu, w_glu_full, F32)
    dw_glu = _mm_tn("glu_gate_dw", z_ssm, dglu, BF16)

    du, db_re_dense, db_im_dense, dlam_re, dlam_im, dc_re_dense, dc_im_dense, dd_skip = _s5_bwd(
        proj, mats, dskip_row, y_ssm, dz_direct, dz_glu, d_attn, d_ssm)
    da_re, da_im, dlog_dt, db_re_t, db_im_t = _s5_discretise_bwd(
        a_re[0], a_im[0], ldt_col, b_re_t, b_im_t, dlam_re.reshape(n_groups, SSM_STATE),
        dlam_im.reshape(n_groups, SSM_STATE), _block_diag_in_t(db_re_dense), _block_diag_in_t(db_im_dense))
    dq, dk2, dv2, dsinks_row = _attention_bwd(proj, cos, sin, sinks_row, dattn, d_attn)
    dproj = _assemble_dproj(dq, dk2, dv2, du, d_in)

    dxn = _mm_nt("proj_in_dx", dproj, w_in_full, F32, tn=d // 2 if (d // 2) % LANES == 0 else None)
    c_sh = d_in // N_DEV
    dproj_sh = dproj.reshape(t, N_DEV, c_sh).transpose(1, 0, 2)
    dw_in = _mm_tn_slots("proj_in_dw", xn, dproj_sh, BF16)

    def x_bwd(dh_, dxn_, xv, g):
        dx, dg = _rms_bwd(xv, g, dxn_)
        return [dh_ + dx], [dg]

    grad_x, dg_pre_mix = _rows("norm_in_bwd", x_bwd, [dh, dxn, x2], [g_pre_mix], [(d, F32)], [d], tm)

    small_grads = {
        "g_pre_mix": dg_pre_mix, "sinks": dsinks_row[:, :n_heads], "a_re": da_re[None], "a_im": da_im[None],
        "log_dt": dlog_dt.reshape(1, n_groups), "b_re": db_re_t.transpose(1, 2, 0)[None],
        "b_im": db_im_t.transpose(1, 2, 0)[None], "c_re": _block_diag_out_t(dc_re_dense)[None],
        "c_im": _block_diag_out_t(dc_im_dense)[None], "d_skip": dd_skip.reshape(d_skip.shape), "b_glu": db_glu,
        "g_attn_out": dg_attn_out, "g_ssm_out": dg_ssm_out, "g_post_mix": dg_post_mix, "g_pre_ffn": dg_pre_ffn,
        "g_post_ffn": dg_post_ffn,
    }
    small_like = [weights[n] for n in _SMALL]
    packed = _pack([small_grads[n] for n in _SMALL])
    rs_in, token = _exchange_start("scatter_dw_in_glu", [dw_in, dw_glu.reshape(N_DEV, d_ssm // N_DEV, d_ssm)], True)
    ag_small, token = _exchange_start("gather_small_grads", [packed + token[:1, :]], False)

    results = {}

    def adam_big(n, parts):
        r = weights[n].shape[1]
        results[n] = _adam_sharded("adam_" + n, parts, weights[n][0], mom_m[n][0], mom_v[n][0],
                                   64 if r % 64 == 0 else r)
        return results[n][3]

    done = [grad_x, token]
    adam_big("w_down", _exchange_wait(rs_down, done)[0])
    p_gate, p_up = _exchange_wait(rs_ffn_in, done)
    done = [adam_big("w_gate", p_gate), adam_big("w_up", p_up), results["w_down"][3]]
    done = [adam_big("w_o", _exchange_wait(rs_o, done)[0])]
    p_in, p_glu = _exchange_wait(rs_in, done)
    done = [adam_big("w_in", p_in), adam_big("w_glu", p_glu)]
    small_all, = _exchange_wait(ag_small, done)
    g_s, d_s, m_s, v_s = _adam_sharded(
        "adam_small", small_all, _pack(small_like), _pack([mom_m[n] for n in _SMALL]),
        _pack([mom_v[n] for n in _SMALL]), packed.shape[0])
    for i, vals in enumerate(zip(*[_unpack(p, small_like) for p in (g_s, d_s, m_s, v_s)])):
        results[_SMALL[i]] = vals

    loss = lax.psum(loss_row[0, 0], ("x", "y", "c"))
    outs = [loss, grad_x[None]]
    for k in range(4):
        for n in _ORDER:
            val = results[n][k]
            outs.append(val[None] if n in _BIG else val)
    return tuple(outs)
```

```python
import math

import jax
import jax.numpy as jnp
from jax import lax
from jax.experimental import pallas as pl
from jax.experimental.pallas import tpu as pltpu

F32 = jnp.float32
BF16 = jnp.bfloat16

HEAD_DIM = 64
N_KV_HEADS = 4
D_KV = N_KV_HEADS * HEAD_DIM
WINDOW = 128
BLOCK = 128
ROPE_THETA = 10000.0
SSM_GROUP = 16
SSM_STATE = 64
GROUPS_PER_BLOCK = 8
SSM_CH_BLOCK = GROUPS_PER_BLOCK * SSM_GROUP
SSM_ST_BLOCK = GROUPS_PER_BLOCK * SSM_STATE
RMS_EPS = 1e-6
N_DEV = 8
LANES = 128
SUBLANES = 8
MASKED = -1e30

ADAM_LR = 0.001
ADAM_B1 = 0.9
ADAM_B2 = 0.999
ADAM_EPS = 1e-08
ADAM_WD = 0.01
ADAM_STEP = 10

VMEM_LIMIT_BYTES = 56 * 1024 * 1024


def _call(body, *, name, out_shape, in_specs, out_specs, grid=(), scratch_shapes=(), semantics=None, n_after=0):
    params = dict(vmem_limit_bytes=VMEM_LIMIT_BYTES)
    if semantics is not None:
        params["dimension_semantics"] = semantics
    n_in = len(in_specs)
    if n_after:
        inner = body

        def body(*refs):
            inner(*refs[:n_in], *refs[n_in + n_after:])

        in_specs = list(in_specs) + [pl.BlockSpec(memory_space=pl.ANY)] * n_after
    return pl.pallas_call(body, name=name, grid=grid, in_specs=in_specs, out_specs=out_specs, out_shape=out_shape,
                          scratch_shapes=scratch_shapes, compiler_params=pltpu.CompilerParams(**params))


def _sds(shape, dtype):
    return jax.ShapeDtypeStruct(tuple(shape), dtype)


def _dot(a, b, ca, cb):
    return lax.dot_general(a, b, (((ca,), (cb,)), ((), ())), preferred_element_type=F32)


def _rms(x):
    r = lax.rsqrt(jnp.mean(x * x, axis=-1, keepdims=True) + RMS_EPS)
    return x * r, r


def _rms_bwd(x, g, dy):
    xh, r = _rms(x)
    dxh = dy * g
    dx = r * (dxh - xh * jnp.mean(dxh * xh, axis=-1, keepdims=True))
    return dx, jnp.sum(dy * xh, axis=0, keepdims=True)


def _sigmoid(x):
    return 1.0 / (1.0 + jnp.exp(-x))


_GELU_C = math.sqrt(2.0 / math.pi)
_GELU_A = 0.044715


def _gelu(y):
    t = jnp.tanh(_GELU_C * (y + _GELU_A * y * y * y))
    return 0.5 * y * (1.0 + t)


def _gelu_grad(y):
    t = jnp.tanh(_GELU_C * (y + _GELU_A * y * y * y))
    return 0.5 * (1.0 + t) + 0.5 * y * (1.0 - t * t) * _GELU_C * (1.0 + 3.0 * _GELU_A * y * y)


def _rows(name, fn, row_ins, vec_ins, row_outs, acc_widths, tm, after=()):
    rows = row_ins[0].shape[0]
    assert rows % tm == 0, (name, rows, tm)
    n_row, n_vec, n_out, n_acc = len(row_ins), len(vec_ins), len(row_outs), len(acc_widths)

    def body(*refs):
        ins = [r[...] for r in refs[:n_row + n_vec]]
        outs = refs[n_row + n_vec:n_row + n_vec + n_out]
        accs = refs[n_row + n_vec + n_out:]
        row_vals, acc_vals = fn(*ins)
        for o, v in zip(outs, row_vals):
            o[...] = v.astype(o.dtype)
        if n_acc:
            @pl.when(pl.program_id(0) == 0)
            def _():
                for a in accs:
                    a[...] = jnp.zeros_like(a)
            for a, v in zip(accs, acc_vals):
                a[...] += v

    in_specs = [pl.BlockSpec((tm, a.shape[1]), lambda i: (i, 0)) for a in row_ins]
    in_specs += [pl.BlockSpec(v.shape, lambda i: (0, 0)) for v in vec_ins]
    out_specs = [pl.BlockSpec((tm, w), lambda i: (i, 0)) for w, _ in row_outs]
    out_specs += [pl.BlockSpec((1, w), lambda i: (0, 0)) for w in acc_widths]
    out_shape = [_sds((rows, w), dt) for w, dt in row_outs] + [_sds((1, w), F32) for w in acc_widths]
    return _call(body, name=name, grid=(rows // tm,), in_specs=in_specs, out_specs=out_specs, out_shape=out_shape,
                 semantics=("arbitrary",) if n_acc else ("parallel",), n_after=len(after))(*row_ins, *vec_ins, *after)


def _matmul(name, operands, in_specs, product, grid, out_shape, out_spec, acc_shape):
    nk = grid[-1]
    n_in = len(operands)
    in_place = out_shape.dtype == F32

    def body(*refs):
        ins = [r[...] for r in refs[:n_in]]
        o_ref = refs[n_in]
        if nk == 1:
            o_ref[...] = product(*ins).astype(o_ref.dtype)
            return
        acc = o_ref if in_place else refs[n_in + 1]
        k = pl.program_id(len(grid) - 1)

        @pl.when(k == 0)
        def _():
            acc[...] = jnp.zeros_like(acc)

        acc[...] += product(*ins)

        if not in_place:
            @pl.when(k == nk - 1)
            def _():
                o_ref[...] = acc[...].astype(o_ref.dtype)

    return _call(body, name=name, grid=grid, in_specs=in_specs, out_specs=out_spec, out_shape=out_shape,
                 scratch_shapes=[] if nk == 1 or in_place else [pltpu.VMEM(acc_shape, F32)],
                 semantics=("parallel",) * (len(grid) - 1) + ("arbitrary",))(*operands)


def _mm_nn(name, a, b, out_dtype, tm=512, tn=None, a_fn=lambda x: x):
    m, k = a.shape
    n = b.shape[1]
    tm, tn = min(tm, m), n if tn is None else tn
    return _matmul(name, [a, b],
                   [pl.BlockSpec((tm, k), lambda i, j, s: (i, 0)), pl.BlockSpec((k, tn), lambda i, j, s: (0, j))],
                   lambda x, y: _dot(a_fn(x), y, 1, 0), (m // tm, n // tn, 1), _sds((m, n), out_dtype),
                   pl.BlockSpec((tm, tn), lambda i, j, s: (i, j)), (tm, tn))


def _mm_nt(name, a, b, out_dtype, tm=512, tn=None):
    m, k = a.shape
    n = b.shape[0]
    tm, tn = min(tm, m), n if tn is None else tn
    return _matmul(name, [a, b],
                   [pl.BlockSpec((tm, k), lambda i, j, s: (i, 0)), pl.BlockSpec((tn, k), lambda i, j, s: (j, 0))],
                   lambda x, y: _dot(x, y, 1, 1), (m // tm, n // tn, 1), _sds((m, n), out_dtype),
                   pl.BlockSpec((tm, tn), lambda i, j, s: (i, j)), (tm, tn))


def _mm_tn(name, a, b, out_dtype, tm=512, tn=None, tk=2048, a_fn=lambda x: x):
    k, m = a.shape
    n = b.shape[1]
    tm, tk, tn = min(tm, m), min(tk, k), n if tn is None else tn
    return _matmul(name, [a, b],
                   [pl.BlockSpec((tk, tm), lambda i, j, s: (s, i)), pl.BlockSpec((tk, tn), lambda i, j, s: (s, j))],
                   lambda x, y: _dot(a_fn(x), y, 0, 0), (m // tm, n // tn, k // tk), _sds((m, n), out_dtype),
                   pl.BlockSpec((tm, tn), lambda i, j, s: (i, j)), (tm, tn))


def _mm_contract_slots(name, pairs, out_dtype, tm=512, tn=2048):
    s_, m, k = pairs[0][0].shape
    n = pairs[0][1].shape[2]
    tm, tn = min(tm, m), min(tn, n)
    ops, specs = [], []
    for a, b in pairs:
        ops += [a, b]
        specs += [pl.BlockSpec((None, tm, k), lambda i, j, s: (s, i, 0)), pl.BlockSpec((None, k, tn), lambda i, j, s: (s, 0, j))]

    def product(*t):
        return sum(_dot(t[2 * p], t[2 * p + 1], 1, 0) for p in range(len(pairs)))

    return _matmul(name, ops, specs, product, (m // tm, n // tn, s_), _sds((m, n), out_dtype),
                   pl.BlockSpec((tm, tn), lambda i, j, s: (i, j)), (tm, tn))


def _mm_contract_slots_nt(name, pairs, out_dtype, tm=512, tn=2048):
    s_, m, k = pairs[0][0].shape
    n = pairs[0][1].shape[1]
    tm, tn = min(tm, m), min(tn, n)
    ops, specs = [], []
    for a, b in pairs:
        ops += [a, b]
        specs += [pl.BlockSpec((None, tm, k), lambda i, j, s: (s, i, 0)), pl.BlockSpec((None, tn, k), lambda i, j, s: (s, j, 0))]

    def product(*t):
        return sum(_dot(t[2 * p], t[2 * p + 1], 1, 1) for p in range(len(pairs)))

    return _matmul(name, ops, specs, product, (m // tm, n // tn, s_), _sds((m, n), out_dtype),
                   pl.BlockSpec((tm, tn), lambda i, j, s: (i, j)), (tm, tn))


def _mm_tn_slots(name, a, b, out_dtype, tm=2048, tk=2048):
    k, m = a.shape
    s_, _, n = b.shape
    tm, tk = min(tm, m), min(tk, k)
    return _matmul(name, [a, b],
                   [pl.BlockSpec((tk, tm), lambda s, i, z: (z, i)), pl.BlockSpec((None, tk, n), lambda s, i, z: (s, z, 0))],
                   lambda x, y: _dot(x, y, 0, 0), (s_, m // tm, k // tk), _sds((s_, m, n), out_dtype),
                   pl.BlockSpec((None, tm, n), lambda s, i, z: (s, i, 0)), (tm, n))


def _mm_slots_tn(name, a, b, out_dtype, tn=2048, tk=2048):
    s_, k, m = a.shape
    n = b.shape[1]
    tn, tk = min(tn, n), min(tk, k)
    return _matmul(name, [a, b],
                   [pl.BlockSpec((None, tk, m), lambda s, j, z: (s, z, 0)), pl.BlockSpec((tk, tn), lambda s, j, z: (z, j))],
                   lambda x, y: _dot(x, y, 0, 0), (s_, n // tn, k // tk), _sds((s_, m, n), out_dtype),
                   pl.BlockSpec((None, m, tn), lambda s, j, z: (s, 0, j)), (m, tn))


def _ffn_in(a, w_gate, w_up, tm=512):
    m, k = a.shape
    s_, _, n = w_gate.shape
    tm = min(tm, m)

    def body(a_ref, wg_ref, wu_ref, g_ref, u_ref, h_ref):
        x = a_ref[...]
        g = _dot(x, wg_ref[...], 1, 0)
        u = _dot(x, wu_ref[...], 1, 0)
        g_ref[...] = g.astype(BF16)
        u_ref[...] = u.astype(BF16)
        h_ref[...] = (g * _sigmoid(g) * u).astype(BF16)

    w_spec = pl.BlockSpec((None, k, n), lambda s, i: (s, 0, 0))
    o_spec = pl.BlockSpec((None, tm, n), lambda s, i: (s, i, 0))
    return _call(body, name="ffn_in", grid=(s_, m // tm),
                 in_specs=[pl.BlockSpec((tm, k), lambda s, i: (i, 0)), w_spec, w_spec], out_specs=[o_spec] * 3,
                 out_shape=[_sds((s_, m, n), BF16)] * 3, semantics=("parallel", "parallel"))(a, w_gate, w_up)


def _ffn_down_bwd(d_out, w_down, gate, up, after, tm=512):
    m, k = d_out.shape
    s_, n, _ = w_down.shape
    tm = min(tm, m)

    def body(d_ref, w_ref, g_ref, u_ref, dg_ref, du_ref):
        dh = _dot(d_ref[...], w_ref[...], 1, 1)
        g = g_ref[...].astype(F32)
        sg = _sigmoid(g)
        dg_ref[...] = (dh * u_ref[...].astype(F32) * sg * (1.0 + g * (1.0 - sg))).astype(BF16)
        du_ref[...] = (dh * g * sg).astype(BF16)

    t_spec = pl.BlockSpec((None, tm, n), lambda s, i: (s, i, 0))
    return _call(body, name="ffn_down_dx", grid=(s_, m // tm),
                 in_specs=[pl.BlockSpec((tm, k), lambda s, i: (i, 0)), pl.BlockSpec((None, n, k), lambda s, i: (s, 0, 0)),
                           t_spec, t_spec],
                 out_specs=[t_spec] * 2, out_shape=[_sds((s_, m, n), BF16)] * 2, semantics=("parallel", "parallel"),
                 n_after=len(after))(d_out, w_down, gate, up, *after)


ALL_PEERS = (1, 2, 3, 4, 5, 6, 7)
CHIP_PEERS = (2, 4, 6)
SIBLING = 1


def _peer(relation):
    x, y, c = lax.axis_index("x"), lax.axis_index("y"), lax.axis_index("c")
    pos = (1 - x if relation & 4 else x, 1 - y if relation & 2 else y, 1 - c if relation & 1 else c)
    return pos, 4 * pos[0] + 2 * pos[1] + pos[2]


def _exchange_copies(ins, lands, send_sems, recv_sems, scatter, relations):
    _, me = _peer(0)

    def copy(a, s, peer, pos, dst_slot):
        return pltpu.make_async_remote_copy(
            src_ref=ins[a].at[peer] if scatter else ins[a], dst_ref=lands[a].at[dst_slot],
            send_sem=send_sems.at[s], recv_sem=recv_sems.at[s], device_id=pos, device_id_type=pl.DeviceIdType.MESH)

    pairs = []
    for k, r in enumerate(relations):
        pos, peer = _peer(r)
        for a in range(len(ins)):
            s = a * len(relations) + k
            pairs.append((copy(a, s, peer, pos, me), copy(a, s, peer, pos, peer)))
    return me, pairs


def _forward_copies(lands, send_sems, recv_sems):
    sibling, _ = _peer(SIBLING)

    def copy(a, s, slot):
        return pltpu.make_async_remote_copy(
            src_ref=lands[a].at[slot], dst_ref=lands[a].at[slot], send_sem=send_sems.at[s], recv_sem=recv_sems.at[s],
            device_id=sibling, device_id_type=pl.DeviceIdType.MESH)

    pairs = []
    for k, r in enumerate(CHIP_PEERS):
        _, mine = _peer(r)
        _, theirs = _peer(r | SIBLING)
        for a in range(len(lands)):
            s = a * len(CHIP_PEERS) + k
            pairs.append((copy(a, s, mine), copy(a, s, theirs)))
    return pairs


_HBM_SPEC = pl.BlockSpec(memory_space=pltpu.HBM)
_SEM_SPEC = pl.BlockSpec(memory_space=pltpu.SEMAPHORE)
_SIDE_EFFECT = pltpu.SideEffectType.DATAFLOW_SIDE_EFFECTING


def _split_start(name, operands, n_sem, make_pairs):
    k = len(operands)

    def body(*refs):
        send_sems, recv_sems, token = refs[k], refs[k + 1], refs[-1]
        for send, _ in make_pairs(refs[:k], send_sems, recv_sems):
            send.start()
        token[...] = jnp.zeros_like(token)

    out = pl.pallas_call(
        body, name=name,
        out_shape=(pltpu.SemaphoreType.DMA((n_sem,)), pltpu.SemaphoreType.DMA((n_sem,)),
                   *[pltpu.HBM(a.shape, a.dtype) for a in operands], _sds((SUBLANES, LANES), F32)),
        in_specs=[_HBM_SPEC] * k,
        out_specs=(_SEM_SPEC, _SEM_SPEC, *[_HBM_SPEC] * k, pl.BlockSpec(memory_space=pltpu.VMEM)),
        input_output_aliases={i: 2 + i for i in range(k)},
        compiler_params=pltpu.CompilerParams(has_side_effects=_SIDE_EFFECT),
    )(*[pltpu.with_memory_space_constraint(a, pltpu.HBM) for a in operands])
    return dict(name=name, sems=out[:2], thru=list(out[2:2 + k]), make_pairs=make_pairs), out[-1]


def _split_wait(handle, after):
    thru, make_pairs = handle["thru"], handle["make_pairs"]
    k = len(thru)

    def body(*refs):
        for send, arrival in make_pairs(refs[:k], refs[k], refs[k + 1]):
            send.wait_send()
            arrival.wait_recv()

    return pl.pallas_call(
        body, name=handle["name"] + "_wait", out_shape=[pltpu.HBM(a.shape, a.dtype) for a in thru],
        in_specs=[_HBM_SPEC] * k + [_SEM_SPEC, _SEM_SPEC] + [pl.BlockSpec(memory_space=pl.ANY)] * len(after),
        out_specs=[_HBM_SPEC] * k, input_output_aliases={i: i for i in range(k)},
        compiler_params=pltpu.CompilerParams(has_side_effects=_SIDE_EFFECT),
    )(*thru, *handle["sems"], *after)


def _exchange_start(name, arrays, scatter, relations=ALL_PEERS):
    n = len(arrays)
    lands = [lax.empty(a.shape if scatter else (N_DEV,) + a.shape, a.dtype) for a in arrays]

    def make_pairs(refs, send_sems, recv_sems):
        return _exchange_copies(refs[:n], refs[n:], send_sems, recv_sems, scatter, relations)[1]

    handle, token = _split_start(name, list(arrays) + lands, n * len(relations), make_pairs)
    handle.update(n=n, scatter=scatter)
    return handle, token


def _forward_start(name, lands):
    return _split_start(name, list(lands), len(lands) * len(CHIP_PEERS), _forward_copies)


def _exchange_wait(handle, after):
    n, scatter = handle["n"], handle["scatter"]
    out = _split_wait(handle, after)
    me = 4 * lax.axis_index("x") + 2 * lax.axis_index("y") + lax.axis_index("c")
    done = []
    for src, land in zip(out[:n], out[n:]):
        own = lax.dynamic_index_in_dim(src, me, 0, keepdims=True) if scatter else src[None]
        done.append(lax.dynamic_update_slice_in_dim(land, own, me, 0))
    return done


def _rope_tables(pos_col):
    t = pos_col.shape[0]
    half = HEAD_DIM // 2
    inv_freq = ROPE_THETA ** (-jnp.arange(half, dtype=F32) / half)
    inv_row = jnp.tile(inv_freq, LANES // half)[None, :]

    def body(pos_ref, inv_ref, cos_ref, sin_ref):
        ang = pos_ref[...] * inv_ref[...]
        cos_ref[...] = jnp.cos(ang)
        sin_ref[...] = jnp.sin(ang)

    tm = min(t, 512)
    return _call(body, name="rope_tables", grid=(t // tm,),
                 in_specs=[pl.BlockSpec((tm, 1), lambda i: (i, 0)), pl.BlockSpec((1, LANES), lambda i: (0, 0))],
                 out_specs=[pl.BlockSpec((tm, LANES), lambda i: (i, 0))] * 2,
                 out_shape=[_sds((t, LANES), F32)] * 2, semantics=("parallel",))(pos_col, inv_row)


def _rot_half(x):
    lane = lax.broadcasted_iota(jnp.int32, x.shape, 1)
    low = (lane % HEAD_DIM) < HEAD_DIM // 2
    return jnp.where(low, -pltpu.roll(x, LANES - HEAD_DIM // 2, 1), pltpu.roll(x, HEAD_DIM // 2, 1))


def _rope(x, cos, sin):
    return x * cos + _rot_half(x) * sin


def _unrope(d, cos, sin):
    return d * cos - _rot_half(d) * sin


def _band_mask(first_block, heads):
    r = lax.broadcasted_iota(jnp.int32, (heads * BLOCK, 2 * BLOCK), 0) % BLOCK
    c = lax.broadcasted_iota(jnp.int32, (heads * BLOCK, 2 * BLOCK), 1)
    diff = r - c + BLOCK
    return (diff >= 0) & (diff < WINDOW) & ((c >= BLOCK) | jnp.logical_not(first_block))


def _attn_specs(t, d_attn, d_in):
    kb, vb = d_attn // D_KV, d_attn // D_KV + 1
    prev = lambda i: jnp.maximum(i - 1, 0)
    return [
        pl.BlockSpec((BLOCK, d_attn), lambda i: (i, 0)),
        pl.BlockSpec((BLOCK, D_KV), lambda i: (i, kb)),
        pl.BlockSpec((BLOCK, D_KV), lambda i: (i, vb)),
        pl.BlockSpec((BLOCK, D_KV), lambda i: (prev(i), kb)),
        pl.BlockSpec((BLOCK, D_KV), lambda i: (prev(i), vb)),
        pl.BlockSpec((BLOCK, LANES), lambda i: (i, 0)),
        pl.BlockSpec((BLOCK, LANES), lambda i: (i, 0)),
        pl.BlockSpec((BLOCK, LANES), lambda i: (prev(i), 0)),
        pl.BlockSpec((BLOCK, LANES), lambda i: (prev(i), 0)),
        pl.BlockSpec((1, LANES), lambda i: (0, 0)),
    ]


def _head(x, h):
    return x[:, h * HEAD_DIM:(h + 1) * HEAD_DIM]


def _attn_heads(q_ref, kc_ref, vc_ref, kp_ref, vp_ref, cq_ref, sq_ref, cp_ref, sp_ref, d_attn):
    cq, sq, cp, sp = cq_ref[...], sq_ref[...], cp_ref[...], sp_ref[...]
    q_rot = [_rope(q_ref[:, j * LANES:(j + 1) * LANES], cq, sq) for j in range(d_attn // LANES)]
    kc_rot = [_rope(kc_ref[:, j * LANES:(j + 1) * LANES], cq, sq) for j in range(D_KV // LANES)]
    kp_rot = [_rope(kp_ref[:, j * LANES:(j + 1) * LANES], cp, sp) for j in range(D_KV // LANES)]
    per = LANES // HEAD_DIM
    q_heads = [_head(q_rot[h // per], h % per).astype(BF16) for h in range(d_attn // HEAD_DIM)]
    kk = [jnp.concatenate([_head(kp_rot[g // per], g % per), _head(kc_rot[g // per], g % per)], axis=0).astype(BF16)
          for g in range(N_KV_HEADS)]
    vv = [jnp.concatenate([_head(vp_ref[...], g), _head(vc_ref[...], g)], axis=0).astype(BF16) for g in range(N_KV_HEADS)]
    return q_heads, kk, vv


def _stack_group(q_heads, sink_ref, group):
    q_all = jnp.concatenate([q_heads[h] for h in group], axis=0)
    sink_all = jnp.concatenate([jnp.broadcast_to(sink_ref[:, h:h + 1], (BLOCK, 1)) for h in group], axis=0)
    return q_all, sink_all


def _softmax_with_sink(q, kk, sink, mask):
    s = _dot(q, kk, 1, 1) * (1.0 / math.sqrt(HEAD_DIM))
    s = jnp.where(mask, s, MASKED)
    m = jnp.maximum(jnp.max(s, axis=-1, keepdims=True), sink)
    p = jnp.exp(s - m)
    e_sink = jnp.exp(sink - m)
    inv = 1.0 / (jnp.sum(p, axis=-1, keepdims=True) + e_sink)
    return p * inv, e_sink * inv


def _attention_fwd(proj, cos, sin, sinks_row, d_attn):
    t, d_in = proj.shape
    n_heads = d_attn // HEAD_DIM
    q_per_kv = n_heads // N_KV_HEADS

    def body(q_ref, kc_ref, vc_ref, kp_ref, vp_ref, cq_ref, sq_ref, cp_ref, sp_ref, sink_ref, o_ref):
        mask = _band_mask(pl.program_id(0) == 0, q_per_kv)
        q_heads, kk, vv = _attn_heads(q_ref, kc_ref, vc_ref, kp_ref, vp_ref, cq_ref, sq_ref, cp_ref, sp_ref, d_attn)
        for g in range(N_KV_HEADS):
            group = range(g * q_per_kv, (g + 1) * q_per_kv)
            q_all, sink_all = _stack_group(q_heads, sink_ref, group)
            probs, _ = _softmax_with_sink(q_all, kk[g], sink_all, mask)
            o_all = _dot(probs.astype(BF16), vv[g], 1, 0)
            for k, h in enumerate(group):
                o_ref[:, h * HEAD_DIM:(h + 1) * HEAD_DIM] = o_all[k * BLOCK:(k + 1) * BLOCK]

    return _call(body, name="attention_fwd", grid=(t // BLOCK,), in_specs=_attn_specs(t, d_attn, d_in),
                 out_specs=pl.BlockSpec((BLOCK, d_attn), lambda i: (i, 0)), out_shape=_sds((t, d_attn), F32),
                 semantics=("parallel",))(proj, proj, proj, proj, proj, cos, sin, cos, sin, sinks_row)


def _attention_bwd(proj, cos, sin, sinks_row, d_out, d_attn):
    t, d_in = proj.shape
    n_heads = d_attn // HEAD_DIM
    q_per_kv = n_heads // N_KV_HEADS
    nb = t // BLOCK
    per = LANES // HEAD_DIM

    def body(q_ref, kc_ref, vc_ref, kp_ref, vp_ref, cq_ref, sq_ref, cp_ref, sp_ref, sink_ref, do_ref,
             dq_ref, dk_ref, dv_ref, dsink_ref):
        i = pl.program_id(0)
        mask = _band_mask(i == 0, q_per_kv)
        q_heads, kk, vv = _attn_heads(q_ref, kc_ref, vc_ref, kp_ref, vp_ref, cq_ref, sq_ref, cp_ref, sp_ref, d_attn)
        lane = lax.broadcasted_iota(jnp.int32, (1, LANES), 1)
        dsink = jnp.zeros((1, LANES), F32)
        dq_rot, dkk, dvv = [], [], []
        for g in range(N_KV_HEADS):
            group = range(g * q_per_kv, (g + 1) * q_per_kv)
            q_all, sink_all = _stack_group(q_heads, sink_ref, group)
            probs, p_sink = _softmax_with_sink(q_all, kk[g], sink_all, mask)
            do_all = jnp.concatenate([do_ref[:, h * HEAD_DIM:(h + 1) * HEAD_DIM] for h in group], axis=0).astype(BF16)
            dp = _dot(do_all, vv[g], 1, 1)
            delta = jnp.sum(probs * dp, axis=-1, keepdims=True)
            ds = (probs * (dp - delta) * (1.0 / math.sqrt(HEAD_DIM))).astype(BF16)
            dq_all = _dot(ds, kk[g], 1, 0)
            dkk.append(_dot(ds, q_all, 0, 0))
            dvv.append(_dot(probs.astype(BF16), do_all, 0, 0))
            sink_term = p_sink * delta
            for k, h in enumerate(group):
                dq_rot.append(dq_all[k * BLOCK:(k + 1) * BLOCK])
                part = jnp.sum(sink_term[k * BLOCK:(k + 1) * BLOCK], axis=0, keepdims=True)
                dsink += jnp.where(lane == h, -part, 0.0)
        cq, sq, cp, sp = cq_ref[...], sq_ref[...], cp_ref[...], sp_ref[...]
        for j in range(d_attn // LANES):
            d = jnp.concatenate(dq_rot[j * per:(j + 1) * per], axis=1)
            dq_ref[:, j * LANES:(j + 1) * LANES] = _unrope(d, cq, sq)
        for j in range(D_KV // LANES):
            d = jnp.concatenate(dkk[j * per:(j + 1) * per], axis=1)
            dk_ref[0, :, j * LANES:(j + 1) * LANES] = _unrope(d[:BLOCK], cp, sp)
            dk_ref[1, :, j * LANES:(j + 1) * LANES] = _unrope(d[BLOCK:], cq, sq)
            d = jnp.concatenate(dvv[j * per:(j + 1) * per], axis=1)
            dv_ref[0, :, j * LANES:(j + 1) * LANES] = d[:BLOCK]
            dv_ref[1, :, j * LANES:(j + 1) * LANES] = d[BLOCK:]

        @pl.when(i == 0)
        def _():
            dsink_ref[...] = jnp.zeros_like(dsink_ref)

        dsink_ref[...] += dsink

    pair = pl.BlockSpec((2, BLOCK, D_KV), lambda i: (i, 0, 0))
    return _call(body, name="attention_bwd", grid=(nb,),
                 in_specs=_attn_specs(t, d_attn, d_in) + [pl.BlockSpec((BLOCK, d_attn), lambda i: (i, 0))],
                 out_specs=[pl.BlockSpec((BLOCK, d_attn), lambda i: (i, 0)), pair, pair,
                            pl.BlockSpec((1, LANES), lambda i: (0, 0))],
                 out_shape=[_sds((t, d_attn), F32), _sds((2 * nb, BLOCK, D_KV), F32), _sds((2 * nb, BLOCK, D_KV), F32),
                            _sds((1, LANES), F32)],
                 semantics=("arbitrary",))(proj, proj, proj, proj, proj, cos, sin, cos, sin, sinks_row, d_out)


def _assemble_dproj(dq, dk2, dv2, du, d_in):
    t, d_attn = dq.shape
    d_ssm = du.shape[1]
    nb = t // BLOCK

    def body(dq_ref, dk_own, dk_next, dv_own, dv_next, du_ref, o_ref):
        has_next = (pl.program_id(0) < nb - 1).astype(F32)
        o_ref[:, :d_attn] = dq_ref[...].astype(BF16)
        o_ref[:, d_attn:d_attn + D_KV] = (dk_own[...] + has_next * dk_next[...]).astype(BF16)
        o_ref[:, d_attn + D_KV:d_attn + 2 * D_KV] = (dv_own[...] + has_next * dv_next[...]).astype(BF16)
        o_ref[:, d_attn + 2 * D_KV:] = du_ref[...].astype(BF16)

    own = pl.BlockSpec((None, BLOCK, D_KV), lambda i: (2 * i + 1, 0, 0))
    nxt = pl.BlockSpec((None, BLOCK, D_KV), lambda i: (jnp.minimum(2 * i + 2, 2 * nb - 1), 0, 0))
    return _call(body, name="assemble_dproj", grid=(nb,),
                 in_specs=[pl.BlockSpec((BLOCK, d_attn), lambda i: (i, 0)), own, nxt, own, nxt,
                           pl.BlockSpec((BLOCK, d_ssm), lambda i: (i, 0))],
                 out_specs=pl.BlockSpec((BLOCK, d_in), lambda i: (i, 0)), out_shape=_sds((t, d_in), BF16),
                 semantics=("parallel",))(dq, dk2, dk2, dv2, dv2, du)


def _discretise(ar, ai, ldt, br, bi):
    dt = jnp.exp(ldt)
    mag = jnp.exp(ar * dt)
    lam_re = mag * jnp.cos(ai * dt)
    lam_im = mag * jnp.sin(ai * dt)
    den = ar * ar + ai * ai
    nr = lam_re - 1.0
    ni = lam_im
    f_re = (nr * ar + ni * ai) / den
    f_im = (ni * ar - nr * ai) / den
    return lam_re, lam_im, f_re[None] * br - f_im[None] * bi, f_re[None] * bi + f_im[None] * br


def _whole(arrays):
    return [pl.BlockSpec(a.shape, lambda *_, nd=len(a.shape): (0,) * nd) for a in arrays]


def _s5_discretise(ar, ai, ldt, br, bi):
    ins = [ar, ai, ldt, br, bi]

    def body(ar_ref, ai_ref, ldt_ref, br_ref, bi_ref, lr_ref, li_ref, bbr_ref, bbi_ref):
        out = _discretise(ar_ref[...], ai_ref[...], ldt_ref[...], br_ref[...], bi_ref[...])
        for ref, val in zip((lr_ref, li_ref, bbr_ref, bbi_ref), out):
            ref[...] = val

    outs = [_sds(ar.shape, F32), _sds(ar.shape, F32), _sds(br.shape, F32), _sds(br.shape, F32)]
    return _call(body, name="s5_discretise", in_specs=_whole(ins), out_specs=_whole(outs), out_shape=outs)(*ins)


def _s5_discretise_bwd(ar, ai, ldt, br, bi, d_lr, d_li, d_bbr, d_bbi):
    ins = [ar, ai, ldt, br, bi, d_lr, d_li, d_bbr, d_bbi]

    def body(ar_ref, ai_ref, ldt_ref, br_ref, bi_ref, dlr_ref, dli_ref, dbbr_ref, dbbi_ref, *out_refs):
        _, vjp = jax.vjp(_discretise, ar_ref[...], ai_ref[...], ldt_ref[...], br_ref[...], bi_ref[...])
        grads = vjp((dlr_ref[...], dli_ref[...], dbbr_ref[...], dbbi_ref[...]))
        for ref, val in zip(out_refs, grads):
            ref[...] = val

    outs = [_sds(a.shape, F32) for a in (ar, ai, ldt, br, bi)]
    return _call(body, name="s5_discretise_bwd", in_specs=_whole(ins), out_specs=_whole(outs), out_shape=outs)(*ins)


def _cmul(ar, ai, br, bi):
    return ar * br - ai * bi, ar * bi + ai * br


def _load_segmented(ref, tile0, n_tiles, seg):
    return jnp.concatenate([ref[pl.ds(tile0 + j, SUBLANES, stride=seg), :] for j in range(n_tiles)], axis=0)


def _store_segmented(ref, tile0, seg, value):
    for j in range(value.shape[0] // SUBLANES):
        ref[pl.ds(tile0 + j, SUBLANES, stride=seg), :] = value[j * SUBLANES:(j + 1) * SUBLANES, :]


def _fill_powers(lr, li, pr_ref, pi_ref, seg):
    pows = [(lr, li)]
    for _ in range(SUBLANES - 1):
        pows.append(_cmul(pows[-1][0], pows[-1][1], lr, li))
    row = lax.broadcasted_iota(jnp.int32, (SUBLANES, lr.shape[1]), 0)
    tr = jnp.zeros((SUBLANES, lr.shape[1]), F32)
    ti = jnp.zeros((SUBLANES, lr.shape[1]), F32)
    for r in range(SUBLANES):
        tr = jnp.where(row == r, pows[r][0], tr)
        ti = jnp.where(row == r, pows[r][1], ti)
    pr_ref[0:SUBLANES, :] = tr
    pi_ref[0:SUBLANES, :] = ti
    k = SUBLANES
    while k < seg:
        fr, fi = pr_ref[k - 1:k, :], pi_ref[k - 1:k, :]
        for t0 in range(0, k, SUBLANES):
            nr, ni = _cmul(pr_ref[t0:t0 + SUBLANES, :], pi_ref[t0:t0 + SUBLANES, :], fr, fi)
            pr_ref[k + t0:k + t0 + SUBLANES, :] = nr
            pi_ref[k + t0:k + t0 + SUBLANES, :] = ni
        k *= 2


def _scan_segments(sr_ref, si_ref, pr_ref, pi_ref, lr, li, seg, reverse, per_tile=None):
    w = lr.shape[1]
    sign = -1.0 if reverse else 1.0
    lrb = jnp.broadcast_to(lr, (SUBLANES, w))
    lib = jnp.broadcast_to(sign * li, (SUBLANES, w))
    zero = jnp.zeros((SUBLANES, w), F32)

    def tile_rows(j):
        return pl.ds(pl.multiple_of(j * SUBLANES, SUBLANES), SUBLANES)

    def local(i, carry):
        rows = tile_rows(seg - 1 - i if reverse else i)
        pr, pi = _cmul(lrb, lib, carry[0], carry[1])
        xr, xi = sr_ref[rows, :] + pr, si_ref[rows, :] + pi
        sr_ref[rows, :] = xr
        si_ref[rows, :] = xi
        return xr, xi

    end_r, end_i = lax.fori_loop(0, seg, local, (zero, zero))
    full_r, full_i = pr_ref[seg - 1:seg, :], sign * pi_ref[seg - 1:seg, :]
    row = lax.broadcasted_iota(jnp.int32, (SUBLANES, w), 0)
    in_r, in_i = zero, zero
    cur_r, cur_i = jnp.zeros((1, w), F32), jnp.zeros((1, w), F32)
    for r in (range(SUBLANES - 2, -1, -1) if reverse else range(1, SUBLANES)):
        src = r + 1 if reverse else r - 1
        pr, pi = _cmul(full_r, full_i, cur_r, cur_i)
        cur_r, cur_i = end_r[src:src + 1, :] + pr, end_i[src:src + 1, :] + pi
        in_r = jnp.where(row == r, cur_r, in_r)
        in_i = jnp.where(row == r, cur_i, in_i)

    def carry_in(j, _):
        rows = tile_rows(j)
        k = seg - 1 - j if reverse else j
        pr, pi = _cmul(pr_ref[pl.ds(k, 1), :], sign * pi_ref[pl.ds(k, 1), :], in_r, in_i)
        xr, xi = sr_ref[rows, :] + pr, si_ref[rows, :] + pi
        sr_ref[rows, :] = xr
        si_ref[rows, :] = xi
        if per_tile is not None:
            per_tile(j, xr, xi)
        return 0

    lax.fori_loop(0, seg, carry_in, 0)


_S5_ROWS = 256


def _s5_in_specs(t, d_attn):
    u_block = (d_attn + 2 * D_KV) // SSM_CH_BLOCK
    blk3 = lambda shape: pl.BlockSpec((None,) + shape, lambda j: (j, 0, 0))
    return [
        pl.BlockSpec((t, SSM_CH_BLOCK), lambda j: (0, u_block + j)),
        blk3((SSM_CH_BLOCK, SSM_ST_BLOCK)), blk3((SSM_CH_BLOCK, SSM_ST_BLOCK)),
        blk3((1, SSM_ST_BLOCK)), blk3((1, SSM_ST_BLOCK)),
        blk3((SSM_ST_BLOCK, SSM_CH_BLOCK)), blk3((SSM_ST_BLOCK, SSM_CH_BLOCK)),
        pl.BlockSpec((1, SSM_CH_BLOCK), lambda j: (0, j)),
    ]


def _chunks(t):
    rows = min(_S5_ROWS, t)
    return rows, lambda i: pl.ds(pl.multiple_of(i * rows, rows), rows)


def _s5_states(u_ref, us_ref, bre_ref, bim_ref, lr_ref, li_ref, sr_ref, si_ref, pr_ref, pi_ref, t):
    seg = t // SUBLANES
    rows, chunk = _chunks(t)
    for c in range(t // rows):
        us_ref[c * rows:(c + 1) * rows, :] = _load_segmented(u_ref, c * rows // SUBLANES, rows // SUBLANES, seg)

    def fill(i, _):
        ub = us_ref[chunk(i), :].astype(BF16)
        sr_ref[chunk(i), :] = _dot(ub, bre_ref[...], 1, 0)
        si_ref[chunk(i), :] = _dot(ub, bim_ref[...], 1, 0)
        return 0

    lax.fori_loop(0, t // rows, fill, 0)
    _fill_powers(lr_ref[...], li_ref[...], pr_ref, pi_ref, seg)
    _scan_segments(sr_ref, si_ref, pr_ref, pi_ref, lr_ref[...], li_ref[...], seg, False)


def _s5_scratch(t):
    state = pltpu.VMEM((t, SSM_ST_BLOCK), F32)
    powers = pltpu.VMEM((t // SUBLANES, SSM_ST_BLOCK), F32)
    return state, powers, pltpu.VMEM((t, SSM_CH_BLOCK), F32)


def _s5_fwd(proj, mats, dskip_row, d_attn, d_ssm):
    t = proj.shape[0]
    seg = t // SUBLANES
    n_blocks = d_ssm // SSM_CH_BLOCK
    rows, chunk = _chunks(t)

    def body(u_ref, bre_ref, bim_ref, lr_ref, li_ref, cre_ref, cim_ref, d_ref, y_ref,
             sr_ref, si_ref, pr_ref, pi_ref, us_ref, ys_ref):
        _s5_states(u_ref, us_ref, bre_ref, bim_ref, lr_ref, li_ref, sr_ref, si_ref, pr_ref, pi_ref, t)

        def emit(i, _):
            ys_ref[chunk(i), :] = (_dot(sr_ref[chunk(i), :].astype(BF16), cre_ref[...], 1, 0)
                                   - _dot(si_ref[chunk(i), :].astype(BF16), cim_ref[...], 1, 0)
                                   + d_ref[...] * us_ref[chunk(i), :])
            return 0

        lax.fori_loop(0, t // rows, emit, 0)
        for c in range(t // rows):
            _store_segmented(y_ref, c * rows // SUBLANES, seg, ys_ref[c * rows:(c + 1) * rows, :])

    state, powers, channels = _s5_scratch(t)
    col = pl.BlockSpec((t, SSM_CH_BLOCK), lambda j: (0, j))
    return _call(body, name="s5_fwd", grid=(n_blocks,), in_specs=_s5_in_specs(t, d_attn), out_specs=col,
                 out_shape=_sds((t, d_ssm), F32), scratch_shapes=[state, state, powers, powers, channels, channels],
                 semantics=("parallel",))(proj, *mats, dskip_row)


def _s5_bwd(proj, mats, dskip_row, y, dz_a, dz_b, d_attn, d_ssm):
    t = proj.shape[0]
    seg = t // SUBLANES
    n_blocks = d_ssm // SSM_CH_BLOCK
    rows, chunk = _chunks(t)

    def body(u_ref, bre_ref, bim_ref, lr_ref, li_ref, cre_ref, cim_ref, d_ref, y_ref, dza_ref, dzb_ref,
             du_ref, dbre_ref, dbim_ref, dlr_ref, dli_ref, dcre_ref, dcim_ref, dd_ref,
             sr_ref, si_ref, gr_ref, gi_ref, pr_ref, pi_ref, us_ref, dys_ref, dus_ref, acc_r, acc_i):
        _s5_states(u_ref, us_ref, bre_ref, bim_ref, lr_ref, li_ref, sr_ref, si_ref, pr_ref, pi_ref, t)
        for ref in (dcre_ref, dcim_ref, dbre_ref, dbim_ref, dd_ref, acc_r, acc_i):
            ref[...] = jnp.zeros_like(ref)
        for c in range(t // rows):
            tile0, n_tiles = c * rows // SUBLANES, rows // SUBLANES
            dz = _load_segmented(dza_ref, tile0, n_tiles, seg) + _load_segmented(dzb_ref, tile0, n_tiles, seg)
            dys_ref[c * rows:(c + 1) * rows, :] = dz * _gelu_grad(_load_segmented(y_ref, tile0, n_tiles, seg))

        def through_c(i, _):
            dy = dys_ref[chunk(i), :]
            dd_ref[...] += jnp.sum(dy * us_ref[chunk(i), :], axis=0, keepdims=True)
            dyb = dy.astype(BF16)
            gr_ref[chunk(i), :] = _dot(dyb, cre_ref[...], 1, 1)
            gi_ref[chunk(i), :] = -_dot(dyb, cim_ref[...], 1, 1)
            dcre_ref[...] += _dot(sr_ref[chunk(i), :].astype(BF16), dyb, 0, 0)
            dcim_ref[...] -= _dot(si_ref[chunk(i), :].astype(BF16), dyb, 0, 0)
            return 0

        lax.fori_loop(0, t // rows, through_c, 0)

        row = lax.broadcasted_iota(jnp.int32, (SUBLANES, SSM_ST_BLOCK), 0)
        last = pl.ds((seg - 1) * SUBLANES, SUBLANES)
        wrap = [jnp.where(row == 0, 0.0, pltpu.roll(ref[last, :], 1, 0)) for ref in (sr_ref, si_ref)]

        def lambda_grad(j, g_re, g_im):
            before = pl.ds(pl.multiple_of(jnp.maximum(j - 1, 0) * SUBLANES, SUBLANES), SUBLANES)
            prev_r = jnp.where(j > 0, sr_ref[before, :], wrap[0])
            prev_i = jnp.where(j > 0, si_ref[before, :], wrap[1])
            acc_r[...] += g_re * prev_r + g_im * prev_i
            acc_i[...] += g_im * prev_r - g_re * prev_i

        _scan_segments(gr_ref, gi_ref, pr_ref, pi_ref, lr_ref[...], li_ref[...], seg, True, per_tile=lambda_grad)
        dlr_ref[...] = jnp.sum(acc_r[...], axis=0, keepdims=True)
        dli_ref[...] = jnp.sum(acc_i[...], axis=0, keepdims=True)

        def through_b(i, _):
            ub = us_ref[chunk(i), :].astype(BF16)
            grb, gib = gr_ref[chunk(i), :].astype(BF16), gi_ref[chunk(i), :].astype(BF16)
            dbre_ref[...] += _dot(ub, grb, 0, 0)
            dbim_ref[...] += _dot(ub, gib, 0, 0)
            dus_ref[chunk(i), :] = (_dot(grb, bre_ref[...], 1, 1) + _dot(gib, bim_ref[...], 1, 1)
                                    + d_ref[...] * dys_ref[chunk(i), :])
            return 0

        lax.fori_loop(0, t // rows, through_b, 0)
        for c in range(t // rows):
            _store_segmented(du_ref, c * rows // SUBLANES, seg, dus_ref[c * rows:(c + 1) * rows, :])

    col = pl.BlockSpec((t, SSM_CH_BLOCK), lambda j: (0, j))
    blk3 = lambda shape: pl.BlockSpec((None,) + shape, lambda j: (j, 0, 0))
    state, powers, channels = _s5_scratch(t)
    return _call(
        body, name="s5_bwd", grid=(n_blocks,), in_specs=_s5_in_specs(t, d_attn) + [col, col, col],
        out_specs=[col, blk3((SSM_CH_BLOCK, SSM_ST_BLOCK)), blk3((SSM_CH_BLOCK, SSM_ST_BLOCK)),
                   blk3((1, SSM_ST_BLOCK)), blk3((1, SSM_ST_BLOCK)),
                   blk3((SSM_ST_BLOCK, SSM_CH_BLOCK)), blk3((SSM_ST_BLOCK, SSM_CH_BLOCK)),
                   pl.BlockSpec((1, SSM_CH_BLOCK), lambda j: (0, j))],
        out_shape=[_sds((t, d_ssm), F32),
                   _sds((n_blocks, SSM_CH_BLOCK, SSM_ST_BLOCK), F32), _sds((n_blocks, SSM_CH_BLOCK, SSM_ST_BLOCK), F32),
                   _sds((n_blocks, 1, SSM_ST_BLOCK), F32), _sds((n_blocks, 1, SSM_ST_BLOCK), F32),
                   _sds((n_blocks, SSM_ST_BLOCK, SSM_CH_BLOCK), F32), _sds((n_blocks, SSM_ST_BLOCK, SSM_CH_BLOCK), F32),
                   _sds((1, d_ssm), F32)],
        scratch_shapes=[state, state, state, state, powers, powers, channels, channels, channels,
                        pltpu.VMEM((SUBLANES, SSM_ST_BLOCK), F32), pltpu.VMEM((SUBLANES, SSM_ST_BLOCK), F32)],
        semantics=("parallel",))(proj, *mats, dskip_row, y, dz_a, dz_b)


def _block_diag_in(bbar_pgn):
    p, g, n = bbar_pgn.shape
    b4 = bbar_pgn.reshape(p, g // GROUPS_PER_BLOCK, GROUPS_PER_BLOCK, n)
    eye = jnp.eye(GROUPS_PER_BLOCK, dtype=F32)
    return jnp.einsum("pjgn,gh->jgphn", b4, eye).reshape(g // GROUPS_PER_BLOCK, SSM_CH_BLOCK, SSM_ST_BLOCK)


def _block_diag_in_t(dense):
    j = dense.shape[0]
    d5 = dense.reshape(j, GROUPS_PER_BLOCK, SSM_GROUP, GROUPS_PER_BLOCK, SSM_STATE)
    eye = jnp.eye(GROUPS_PER_BLOCK, dtype=F32)
    return jnp.einsum("jgphn,gh->pjgn", d5, eye).reshape(SSM_GROUP, j * GROUPS_PER_BLOCK, SSM_STATE)


def _block_diag_out(c_gpn):
    g, p, n = c_gpn.shape
    c4 = c_gpn.reshape(g // GROUPS_PER_BLOCK, GROUPS_PER_BLOCK, p, n)
    eye = jnp.eye(GROUPS_PER_BLOCK, dtype=F32)
    return jnp.einsum("jgpn,gh->jgnhp", c4, eye).reshape(g // GROUPS_PER_BLOCK, SSM_ST_BLOCK, SSM_CH_BLOCK)


def _block_diag_out_t(dense):
    j = dense.shape[0]
    d5 = dense.reshape(j, GROUPS_PER_BLOCK, SSM_STATE, GROUPS_PER_BLOCK, SSM_GROUP)
    eye = jnp.eye(GROUPS_PER_BLOCK, dtype=F32)
    return jnp.einsum("jgnhp,gh->jgpn", d5, eye).reshape(j * GROUPS_PER_BLOCK, SSM_GROUP, SSM_STATE)


def _adamw(w, g, m, v):
    m = ADAM_B1 * m + (1.0 - ADAM_B1) * g
    v = ADAM_B2 * v + (1.0 - ADAM_B2) * (g * g)
    m_hat = m / (1.0 - ADAM_B1 ** ADAM_STEP)
    v_hat = v / (1.0 - ADAM_B2 ** ADAM_STEP)
    delta = -ADAM_LR * (m_hat / (jnp.sqrt(v_hat) + ADAM_EPS) + ADAM_WD * w)
    return delta, m, v


def _adam_sharded(name, parts, w, m, v, tr):
    r, c = w.shape
    assert r % tr == 0, (name, r, tr)

    def body(p_ref, w_ref, m_ref, v_ref, g_out, d_out, m_out, v_out):
        g = p_ref[0].astype(F32)
        for i in range(1, N_DEV):
            g = g + p_ref[i].astype(F32)
        delta, m_new, v_new = _adamw(w_ref[...], g, m_ref[...], v_ref[...])
        g_out[...] = g
        d_out[...] = delta
        m_out[...] = m_new
        v_out[...] = v_new

    tile = pl.BlockSpec((tr, c), lambda i: (i, 0))
    return _call(body, name=name, grid=(r // tr,),
                 in_specs=[pl.BlockSpec((N_DEV, tr, c), lambda i: (0, i, 0)), tile, tile, tile],
                 out_specs=[tile] * 4, out_shape=[_sds((r, c), F32)] * 4, semantics=("parallel",))(parts, w, m, v)


_SMALL = ("g_pre_mix", "sinks", "a_re", "a_im", "log_dt", "b_re", "b_im", "c_re", "c_im", "d_skip", "b_glu",
          "g_attn_out", "g_ssm_out", "g_post_mix", "g_pre_ffn", "g_post_ffn")
_BIG = ("w_in", "w_glu", "w_o", "w_gate", "w_up", "w_down")
_ORDER = ("g_pre_mix", "w_in", "sinks", "a_re", "a_im", "log_dt", "b_re", "b_im", "c_re", "c_im", "d_skip", "w_glu",
          "b_glu", "g_attn_out", "g_ssm_out", "w_o", "g_post_mix", "g_pre_ffn", "w_gate", "w_up", "w_down",
          "g_post_ffn")


def _pack(arrays):
    flat = jnp.concatenate([a.reshape(-1).astype(F32) for a in arrays])
    pad = (-flat.shape[0]) % (SUBLANES * LANES)
    return jnp.pad(flat, (0, pad)).reshape(-1, LANES)


def _unpack(packed, like):
    flat = packed.reshape(-1)
    out, at = [], 0
    for a in like:
        out.append(flat[at:at + a.size].reshape(a.shape))
        at += a.size
    return out


def kernel(x, positions, g_pre_mix, w_in, sinks, a_re, a_im, log_dt, b_re, b_im, c_re, c_im, d_skip, w_glu, b_glu, g_attn_out, g_ssm_out, w_o, g_post_mix, g_pre_ffn, w_gate, w_up, w_down, g_post_ffn, loss_target, m_g_pre_mix, m_w_in, m_sinks, m_a_re, m_a_im, m_log_dt, m_b_re, m_b_im, m_c_re, m_c_im, m_d_skip, m_w_glu, m_b_glu, m_g_attn_out, m_g_ssm_out, m_w_o, m_g_post_mix, m_g_pre_ffn, m_w_gate, m_w_up, m_w_down, m_g_post_ffn, v_g_pre_mix, v_w_in, v_sinks, v_a_re, v_a_im, v_log_dt, v_b_re, v_b_im, v_c_re, v_c_im, v_d_skip, v_w_glu, v_b_glu, v_g_attn_out, v_g_ssm_out, v_w_o, v_g_post_mix, v_g_pre_ffn, v_w_gate, v_w_up, v_w_down, v_g_post_ffn):
    given = dict(locals())
    weights = {n: given[n] for n in _ORDER}
    mom_m = {n: given["m_" + n] for n in _ORDER}
    mom_v = {n: given["v_" + n] for n in _ORDER}

    t, d = x.shape[1], x.shape[2]
    d_attn = d // 2
    d_ssm = d - d_attn
    d_in = d_attn + 2 * D_KV + d_ssm
    n_groups = d_ssm // SSM_GROUP
    n_heads = d_attn // HEAD_DIM
    tm = min(256, t)

    x2 = x[0]
    target = loss_target[0]

    def start_gather(name, ws, token):
        behind = 0 if token is None else token[0, 0].astype(BF16)
        return _exchange_start(name, [w[0].astype(BF16) + behind for w in ws], False, (SIBLING,) + CHIP_PEERS)

    def finish_gather(handle, after):
        forward, _ = _forward_start(handle["name"] + "_forward", _exchange_wait(handle, after))
        return _split_wait(forward, [])

    ag_in, token = start_gather("gather_w_in", [w_in], None)
    ag_mix, token = start_gather("gather_w_glu_o", [w_glu, w_o], token)
    ag_ffn_in, token = start_gather("gather_w_gate_up", [w_gate, w_up], token)
    ag_down, token = start_gather("gather_w_down", [w_down], token)

    xn, = _rows("norm_in", lambda xv, g: ([_rms(xv)[0] * g], []), [x2], [g_pre_mix], [(d, BF16)], [], tm,
                after=[token])
    win_g, = finish_gather(ag_in, [xn])
    w_in_full = win_g.transpose(1, 0, 2).reshape(d, d_in)
    proj = _mm_nn("proj_in", xn, w_in_full, F32, tn=d_in // 4 if (d_in // 4) % LANES == 0 else None)

    cos, sin = _rope_tables(positions.reshape(t, 1).astype(F32))
    sinks_row = jnp.pad(sinks, ((0, 0), (0, LANES - n_heads)))
    attn = _attention_fwd(proj, cos, sin, sinks_row, d_attn)

    b_re_t, b_im_t = b_re[0].transpose(2, 0, 1), b_im[0].transpose(2, 0, 1)
    ldt_col = log_dt.reshape(n_groups, 1)
    lam_re, lam_im, bbar_re, bbar_im = _s5_discretise(a_re[0], a_im[0], ldt_col, b_re_t, b_im_t)
    n_blocks = n_groups // GROUPS_PER_BLOCK
    mats = [_block_diag_in(bbar_re).astype(BF16), _block_diag_in(bbar_im).astype(BF16),
            lam_re.reshape(n_blocks, 1, SSM_ST_BLOCK), lam_im.reshape(n_blocks, 1, SSM_ST_BLOCK),
            _block_diag_out(c_re[0]).astype(BF16), _block_diag_out(c_im[0]).astype(BF16)]
    dskip_row = d_skip.reshape(1, d_ssm)
    y_ssm = _s5_fwd(proj, mats, dskip_row, d_attn, d_ssm)
    gelu_bf16 = lambda yv: _gelu(yv).astype(BF16)
    wglu_g, wo_g = finish_gather(ag_mix, [attn, y_ssm])
    w_glu_full = wglu_g.reshape(d_ssm, d_ssm)
    w_o_full = wo_g.reshape(d, d)
    glu_lin = _mm_nn("glu_gate", y_ssm, w_glu_full, F32, a_fn=gelu_bf16)

    def mix_prep(av, yv, gl, bg, ga, gs):
        ssm = _gelu(yv) * _sigmoid(gl + bg)
        return [jnp.concatenate([_rms(av)[0] * ga, _rms(ssm)[0] * gs], axis=1)], []

    mixed, = _rows("mix_prep", mix_prep, [attn, y_ssm, glu_lin], [b_glu, g_attn_out, g_ssm_out], [(d, BF16)], [], tm)
    mix = _mm_nn("mix_out", mixed, w_o_full, F32, tn=d // 2 if (d // 2) % LANES == 0 else None)

    def post_mix(xv, mv, gpm, gpf):
        h = xv + _rms(mv)[0] * gpm
        return [h, _rms(h)[0] * gpf], []

    h, hn = _rows("post_mix", post_mix, [x2, mix], [g_post_mix, g_pre_ffn], [(d, F32), (d, BF16)], [], tm)
    wgate_g, wup_g = finish_gather(ag_ffn_in, [hn])
    gate, up, hid = _ffn_in(hn, wgate_g, wup_g)
    wdown_g, = finish_gather(ag_down, [hid])
    ff = _mm_contract_slots("ffn_down", [(hid, wdown_g)], F32)

    def head(hv, fv, tv, gpo):
        out = hv + _rms(fv)[0] * gpo
        err = out - tv
        dout = err * (1.0 / d)
        dff, dg = _rms_bwd(fv, gpo, dout)
        loss = jnp.zeros((1, LANES), F32) + 0.5 * jnp.sum(err * err) * (1.0 / d)
        return [dff, dout], [dg, loss]

    dff, dh_out, dg_post_ffn, loss_row = _rows("loss_head", head, [h, ff, target], [g_post_ffn],
                                               [(d, BF16), (d, F32)], [d, LANES], tm)

    dw_down = _mm_slots_tn("ffn_down_dw", hid, dff, BF16)
    rs_down, tok_down = _exchange_start("scatter_dw_down", [dw_down], True)
    dgate, dup = _ffn_down_bwd(dff, wdown_g, gate, up, [tok_down])
    dhn = _mm_contract_slots_nt("ffn_in_dx", [(dgate, wgate_g), (dup, wup_g)], F32)
    dw_gate = _mm_tn_slots("ffn_gate_dw", hn, dgate, BF16)
    dw_up = _mm_tn_slots("ffn_up_dw", hn, dup, BF16)
    rs_ffn_in, tok_ffn_in = _exchange_start("scatter_dw_gate_up", [dw_gate, dw_up], True)

    def mid_bwd(dho, dhn_, hv, mv, gpf, gpm):
        d1, dgpf = _rms_bwd(hv, gpf, dhn_)
        dh_ = dho + d1
        dmix_, dgpm = _rms_bwd(mv, gpm, dh_)
        return [dh_, dmix_], [dgpf, dgpm]

    dh, dmix, dg_pre_ffn, dg_post_mix = _rows("mid_bwd", mid_bwd, [dh_out, dhn, h, mix], [g_pre_ffn, g_post_mix],
                                              [(d, F32), (d, BF16)], [d, d], tm, after=[tok_ffn_in])

    dmixed = _mm_nt("mix_out_dx", dmix, w_o_full, F32, tn=d // 2 if (d // 2) % LANES == 0 else None)
    dw_o = _mm_tn("mix_out_dw", mixed, dmix, BF16, tn=d // 2 if (d // 2) % LANES == 0 else None)
    rs_o, tok_o = _exchange_start("scatter_dw_o", [dw_o.reshape(N_DEV, d // N_DEV, d)], True)

    def mix_bwd(dm, av, yv, gl, bg, ga, gs):
        dattn_, dga = _rms_bwd(av, ga, dm[:, :d_attn])
        z = _gelu(yv)
        sg = _sigmoid(gl + bg)
        dssm, dgs = _rms_bwd(z * sg, gs, dm[:, d_attn:])
        dgl = dssm * z * sg * (1.0 - sg)
        return [dattn_, dssm * sg, dgl], [dga, dgs, jnp.sum(dgl, axis=0, keepdims=True)]

    dattn, dz_direct, dglu, dg_attn_out, dg_ssm_out, db_glu = _rows(
        "mix_bwd", mix_bwd, [dmixed, attn, y_ssm, glu_lin], [b_glu, g_attn_out, g_ssm_out],
        [(d_attn, F32), (d_ssm, F32), (d_ssm, BF16)], [d_attn, d_ssm, d_ssm], tm, after=[tok_o])
    dz_glu = _mm_nt("glu_gate_dx", dglu, w_glu_full, F32)
    dw_glu = _mm_tn("glu_gate_dw", y_ssm, dglu, BF16, a_fn=gelu_bf16)

    du, db_re_dense, db_im_dense, dlam_re, dlam_im, dc_re_dense, dc_im_dense, dd_skip = _s5_bwd(
        proj, mats, dskip_row, y_ssm, dz_direct, dz_glu, d_attn, d_ssm)
    da_re, da_im, dlog_dt, db_re_t, db_im_t = _s5_discretise_bwd(
        a_re[0], a_im[0], ldt_col, b_re_t, b_im_t, dlam_re.reshape(n_groups, SSM_STATE),
        dlam_im.reshape(n_groups, SSM_STATE), _block_diag_in_t(db_re_dense), _block_diag_in_t(db_im_dense))
    dq, dk2, dv2, dsinks_row = _attention_bwd(proj, cos, sin, sinks_row, dattn, d_attn)
    dproj = _assemble_dproj(dq, dk2, dv2, du, d_in)

    dxn = _mm_nt("proj_in_dx", dproj, w_in_full, F32, tn=d // 2 if (d // 2) % LANES == 0 else None)
    c_sh = d_in // N_DEV
    dproj_sh = dproj.reshape(t, N_DEV, c_sh).transpose(1, 0, 2)
    dw_in = _mm_tn_slots("proj_in_dw", xn, dproj_sh, BF16)

    def x_bwd(dh_, dxn_, xv, g):
        dx, dg = _rms_bwd(xv, g, dxn_)
        return [dh_ + dx], [dg]

    grad_x, dg_pre_mix = _rows("norm_in_bwd", x_bwd, [dh, dxn, x2], [g_pre_mix], [(d, F32)], [d], tm)

    small_grads = {
        "g_pre_mix": dg_pre_mix, "sinks": dsinks_row[:, :n_heads], "a_re": da_re[None], "a_im": da_im[None],
        "log_dt": dlog_dt.reshape(1, n_groups), "b_re": db_re_t.transpose(1, 2, 0)[None],
        "b_im": db_im_t.transpose(1, 2, 0)[None], "c_re": _block_diag_out_t(dc_re_dense)[None],
        "c_im": _block_diag_out_t(dc_im_dense)[None], "d_skip": dd_skip.reshape(d_skip.shape), "b_glu": db_glu,
        "g_attn_out": dg_attn_out, "g_ssm_out": dg_ssm_out, "g_post_mix": dg_post_mix, "g_pre_ffn": dg_pre_ffn,
        "g_post_ffn": dg_post_ffn,
    }
    small_like = [weights[n] for n in _SMALL]
    packed = _pack([small_grads[n] for n in _SMALL])
    rs_in, token = _exchange_start("scatter_dw_in_glu", [dw_in, dw_glu.reshape(N_DEV, d_ssm // N_DEV, d_ssm)], True)
    ag_small, token = _exchange_start("gather_small_grads", [packed + token[:1, :]], False)

    results = {}

    def adam_big(n, parts):
        r = weights[n].shape[1]
        results[n] = _adam_sharded("adam_" + n, parts, weights[n][0], mom_m[n][0], mom_v[n][0],
                                   64 if r % 64 == 0 else r)
        return results[n][3]

    done = [grad_x, token]
    adam_big("w_down", _exchange_wait(rs_down, done)[0])
    p_gate, p_up = _exchange_wait(rs_ffn_in, done)
    done = [adam_big("w_gate", p_gate), adam_big("w_up", p_up), results["w_down"][3]]
    done = [adam_big("w_o", _exchange_wait(rs_o, done)[0])]
    p_in, p_glu = _exchange_wait(rs_in, done)
    done = [adam_big("w_in", p_in), adam_big("w_glu", p_glu)]
    small_all, = _exchange_wait(ag_small, done)
    g_s, d_s, m_s, v_s = _adam_sharded(
        "adam_small", small_all, _pack(small_like), _pack([mom_m[n] for n in _SMALL]),
        _pack([mom_v[n] for n in _SMALL]), packed.shape[0])
    for i, vals in enumerate(zip(*[_unpack(p, small_like) for p in (g_s, d_s, m_s, v_s)])):
        results[_SMALL[i]] = vals

    loss = lax.psum(loss_row[0, 0], ("x", "y", "c"))
    outs = [loss, grad_x[None]]
    for k in range(4):
        for n in _ORDER:
            val = results[n][k]
            outs.append(val[None] if n in _BIG else val)
    return tuple(outs)
```

```python
import math

import jax
import jax.numpy as jnp
from jax import lax
from jax.experimental import pallas as pl
from jax.experimental.pallas import tpu as pltpu

F32 = jnp.float32
BF16 = jnp.bfloat16

HEAD_DIM = 64
N_KV_HEADS = 4
D_KV = N_KV_HEADS * HEAD_DIM
WINDOW = 128
BLOCK = 128
ROPE_THETA = 10000.0
SSM_GROUP = 16
SSM_STATE = 64
GROUPS_PER_BLOCK = 8
SSM_CH_BLOCK = GROUPS_PER_BLOCK * SSM_GROUP
SSM_ST_BLOCK = GROUPS_PER_BLOCK * SSM_STATE
RMS_EPS = 1e-6
N_DEV = 8
LANES = 128
SUBLANES = 8
MASKED = -1e30

ADAM_LR = 0.001
ADAM_B1 = 0.9
ADAM_B2 = 0.999
ADAM_EPS = 1e-08
ADAM_WD = 0.01
ADAM_STEP = 10

VMEM_LIMIT_BYTES = 56 * 1024 * 1024


def _call(body, *, name, out_shape, in_specs, out_specs, grid=(), scratch_shapes=(), semantics=None, n_after=0):
    params = dict(vmem_limit_bytes=VMEM_LIMIT_BYTES)
    if semantics is not None:
        params["dimension_semantics"] = semantics
    n_in = len(in_specs)
    if n_after:
        inner = body

        def body(*refs):
            inner(*refs[:n_in], *refs[n_in + n_after:])

        in_specs = list(in_specs) + [pl.BlockSpec(memory_space=pl.ANY)] * n_after
    return pl.pallas_call(body, name=name, grid=grid, in_specs=in_specs, out_specs=out_specs, out_shape=out_shape,
                          scratch_shapes=scratch_shapes, compiler_params=pltpu.CompilerParams(**params))


def _sds(shape, dtype):
    return jax.ShapeDtypeStruct(tuple(shape), dtype)


def _dot(a, b, ca, cb):
    return lax.dot_general(a, b, (((ca,), (cb,)), ((), ())), preferred_element_type=F32)


def _rms(x):
    r = lax.rsqrt(jnp.mean(x * x, axis=-1, keepdims=True) + RMS_EPS)
    return x * r, r


def _rms_bwd(x, g, dy):
    xh, r = _rms(x)
    dxh = dy * g
    dx = r * (dxh - xh * jnp.mean(dxh * xh, axis=-1, keepdims=True))
    return dx, jnp.sum(dy * xh, axis=0, keepdims=True)


def _sigmoid(x):
    return 1.0 / (1.0 + jnp.exp(-x))


_GELU_C = math.sqrt(2.0 / math.pi)
_GELU_A = 0.044715


def _gelu(y):
    t = jnp.tanh(_GELU_C * (y + _GELU_A * y * y * y))
    return 0.5 * y * (1.0 + t)


def _gelu_grad(y):
    t = jnp.tanh(_GELU_C * (y + _GELU_A * y * y * y))
    return 0.5 * (1.0 + t) + 0.5 * y * (1.0 - t * t) * _GELU_C * (1.0 + 3.0 * _GELU_A * y * y)


def _rows(name, fn, row_ins, vec_ins, row_outs, acc_widths, tm, after=()):
    rows = row_ins[0].shape[0]
    assert rows % tm == 0, (name, rows, tm)
    n_row, n_vec, n_out, n_acc = len(row_ins), len(vec_ins), len(row_outs), len(acc_widths)

    def body(*refs):
        ins = [r[...] for r in refs[:n_row + n_vec]]
        outs = refs[n_row + n_vec:n_row + n_vec + n_out]
        accs = refs[n_row + n_vec + n_out:]
        row_vals, acc_vals = fn(*ins)
        for o, v in zip(outs, row_vals):
            o[...] = v.astype(o.dtype)
        if n_acc:
            @pl.when(pl.program_id(0) == 0)
            def _():
                for a in accs:
                    a[...] = jnp.zeros_like(a)
            for a, v in zip(accs, acc_vals):
                a[...] += v

    in_specs = [pl.BlockSpec((tm, a.shape[1]), lambda i: (i, 0)) for a in row_ins]
    in_specs += [pl.BlockSpec(v.shape, lambda i: (0, 0)) for v in vec_ins]
    out_specs = [pl.BlockSpec((tm, w), lambda i: (i, 0)) for w, _ in row_outs]
    out_specs += [pl.BlockSpec((1, w), lambda i: (0, 0)) for w in acc_widths]
    out_shape = [_sds((rows, w), dt) for w, dt in row_outs] + [_sds((1, w), F32) for w in acc_widths]
    return _call(body, name=name, grid=(rows // tm,), in_specs=in_specs, out_specs=out_specs, out_shape=out_shape,
                 semantics=("arbitrary",) if n_acc else ("parallel",), n_after=len(after))(*row_ins, *vec_ins, *after)


def _matmul(name, operands, in_specs, product, grid, out_shape, out_spec, acc_shape):
    nk = grid[-1]
    n_in = len(operands)
    in_place = out_shape.dtype == F32

    def body(*refs):
        ins = [r[...] for r in refs[:n_in]]
        o_ref = refs[n_in]
        if nk == 1:
            o_ref[...] = product(*ins).astype(o_ref.dtype)
            return
        acc = o_ref if in_place else refs[n_in + 1]
        k = pl.program_id(len(grid) - 1)

        @pl.when(k == 0)
        def _():
            acc[...] = jnp.zeros_like(acc)

        acc[...] += product(*ins)

        if not in_place:
            @pl.when(k == nk - 1)
            def _():
                o_ref[...] = acc[...].astype(o_ref.dtype)

    return _call(body, name=name, grid=grid, in_specs=in_specs, out_specs=out_spec, out_shape=out_shape,
                 scratch_shapes=[] if nk == 1 or in_place else [pltpu.VMEM(acc_shape, F32)],
                 semantics=("parallel",) * (len(grid) - 1) + ("arbitrary",))(*operands)


def _mm_nn(name, a, b, out_dtype, tm=512, tn=None, a_fn=lambda x: x):
    m, k = a.shape
    n = b.shape[1]
    tm, tn = min(tm, m), n if tn is None else tn
    return _matmul(name, [a, b],
                   [pl.BlockSpec((tm, k), lambda i, j, s: (i, 0)), pl.BlockSpec((k, tn), lambda i, j, s: (0, j))],
                   lambda x, y: _dot(a_fn(x), y, 1, 0), (m // tm, n // tn, 1), _sds((m, n), out_dtype),
                   pl.BlockSpec((tm, tn), lambda i, j, s: (i, j)), (tm, tn))


def _mm_nt(name, a, b, out_dtype, tm=512, tn=None):
    m, k = a.shape
    n = b.shape[0]
    tm, tn = min(tm, m), n if tn is None else tn
    return _matmul(name, [a, b],
                   [pl.BlockSpec((tm, k), lambda i, j, s: (i, 0)), pl.BlockSpec((tn, k), lambda i, j, s: (j, 0))],
                   lambda x, y: _dot(x, y, 1, 1), (m // tm, n // tn, 1), _sds((m, n), out_dtype),
                   pl.BlockSpec((tm, tn), lambda i, j, s: (i, j)), (tm, tn))


def _mm_tn(name, a, b, out_dtype, tm=512, tn=None, tk=2048, a_fn=lambda x: x):
    k, m = a.shape
    n = b.shape[1]
    tm, tk, tn = min(tm, m), min(tk, k), n if tn is None else tn
    return _matmul(name, [a, b],
                   [pl.BlockSpec((tk, tm), lambda i, j, s: (s, i)), pl.BlockSpec((tk, tn), lambda i, j, s: (s, j))],
                   lambda x, y: _dot(a_fn(x), y, 0, 0), (m // tm, n // tn, k // tk), _sds((m, n), out_dtype),
                   pl.BlockSpec((tm, tn), lambda i, j, s: (i, j)), (tm, tn))


def _mm_contract_slots(name, pairs, out_dtype, per_step, tm=512, tn=2048):
    s_, m, k = pairs[0][0].shape
    n = pairs[0][1].shape[2]
    tm, tn = min(tm, m), min(tn, n)
    ops, specs = [], []
    for a, b in pairs:
        ops += [a, b]
        specs += [pl.BlockSpec((per_step, tm, k), lambda i, j, s: (s, i, 0)),
                  pl.BlockSpec((per_step, k, tn), lambda i, j, s: (s, 0, j))]

    def product(*t):
        return sum(_dot(t[2 * p][q], t[2 * p + 1][q], 1, 0) for p in range(len(pairs)) for q in range(per_step))

    return _matmul(name, ops, specs, product, (m // tm, n // tn, s_ // per_step), _sds((m, n), out_dtype),
                   pl.BlockSpec((tm, tn), lambda i, j, s: (i, j)), (tm, tn))


def _mm_slots_tn(name, a, b, out_dtype, tn=2048, tk=2048):
    s_, k, m = a.shape
    n = b.shape[1]
    tn, tk = min(tn, n), min(tk, k)
    return _matmul(name, [a, b],
                   [pl.BlockSpec((None, tk, m), lambda s, j, z: (s, z, 0)), pl.BlockSpec((tk, tn), lambda s, j, z: (z, j))],
                   lambda x, y: _dot(x, y, 0, 0), (s_, n // tn, k // tk), _sds((s_, m, n), out_dtype),
                   pl.BlockSpec((None, m, tn), lambda s, j, z: (s, 0, j)), (m, tn))


def _ffn_in(a, w_gate, w_up, tm=512):
    m, k = a.shape
    s_, n, _ = w_gate.shape
    tm = min(tm, m)

    def body(a_ref, wg_ref, wu_ref, g_ref, u_ref, h_ref):
        x = a_ref[...]
        g = _dot(x, wg_ref[...], 1, 1)
        u = _dot(x, wu_ref[...], 1, 1)
        g_ref[...] = g.astype(BF16)
        u_ref[...] = u.astype(BF16)
        h_ref[...] = (g * _sigmoid(g) * u).astype(BF16)

    w_spec = pl.BlockSpec((None, n, k), lambda s, i: (s, 0, 0))
    o_spec = pl.BlockSpec((None, tm, n), lambda s, i: (s, i, 0))
    return _call(body, name="ffn_in", grid=(s_, m // tm),
                 in_specs=[pl.BlockSpec((tm, k), lambda s, i: (i, 0)), w_spec, w_spec], out_specs=[o_spec] * 3,
                 out_shape=[_sds((s_, m, n), BF16)] * 3, semantics=("parallel", "parallel"))(a, w_gate, w_up)


def _ffn_down_bwd(d_out, w_down, gate, up, after, tm=512):
    m, k = d_out.shape
    s_, n, _ = w_down.shape
    tm = min(tm, m)

    def body(d_ref, w_ref, g_ref, u_ref, dg_ref, du_ref):
        dh = _dot(d_ref[...], w_ref[...], 1, 1)
        g = g_ref[...].astype(F32)
        sg = _sigmoid(g)
        dg_ref[...] = (dh * u_ref[...].astype(F32) * sg * (1.0 + g * (1.0 - sg))).astype(BF16)
        du_ref[...] = (dh * g * sg).astype(BF16)

    t_spec = pl.BlockSpec((None, tm, n), lambda s, i: (s, i, 0))
    return _call(body, name="ffn_down_dx", grid=(s_, m // tm),
                 in_specs=[pl.BlockSpec((tm, k), lambda s, i: (i, 0)), pl.BlockSpec((None, n, k), lambda s, i: (s, 0, 0)),
                           t_spec, t_spec],
                 out_specs=[t_spec] * 2, out_shape=[_sds((s_, m, n), BF16)] * 2, semantics=("parallel", "parallel"),
                 n_after=len(after))(d_out, w_down, gate, up, *after)


ALL_PEERS = (1, 2, 3, 4, 5, 6, 7)
CHIP_PEERS = (2, 4, 6)
SIBLING = 1


def _peer(relation):
    x, y, c = lax.axis_index("x"), lax.axis_index("y"), lax.axis_index("c")
    pos = (1 - x if relation & 4 else x, 1 - y if relation & 2 else y, 1 - c if relation & 1 else c)
    return pos, 4 * pos[0] + 2 * pos[1] + pos[2]


def _exchange_copies(ins, lands, send_sems, recv_sems, scatter, relations):
    _, me = _peer(0)

    def copy(a, s, peer, pos, dst_slot):
        return pltpu.make_async_remote_copy(
            src_ref=ins[a].at[peer] if scatter else ins[a], dst_ref=lands[a].at[dst_slot],
            send_sem=send_sems.at[s], recv_sem=recv_sems.at[s], device_id=pos, device_id_type=pl.DeviceIdType.MESH)

    pairs = []
    for k, r in enumerate(relations):
        pos, peer = _peer(r)
        for a in range(len(ins)):
            s = a * len(relations) + k
            pairs.append((copy(a, s, peer, pos, me), copy(a, s, peer, pos, peer)))
    return me, pairs


def _forward_copies(lands, send_sems, recv_sems):
    sibling, _ = _peer(SIBLING)

    def copy(a, s, slot):
        return pltpu.make_async_remote_copy(
            src_ref=lands[a].at[slot], dst_ref=lands[a].at[slot], send_sem=send_sems.at[s], recv_sem=recv_sems.at[s],
            device_id=sibling, device_id_type=pl.DeviceIdType.MESH)

    pairs = []
    for k, r in enumerate(CHIP_PEERS):
        _, mine = _peer(r)
        _, theirs = _peer(r | SIBLING)
        for a in range(len(lands)):
            s = a * len(CHIP_PEERS) + k
            pairs.append((copy(a, s, mine), copy(a, s, theirs)))
    return pairs


_HBM_SPEC = pl.BlockSpec(memory_space=pltpu.HBM)
_SEM_SPEC = pl.BlockSpec(memory_space=pltpu.SEMAPHORE)
_SIDE_EFFECT = pltpu.SideEffectType.DATAFLOW_SIDE_EFFECTING


def _split_start(name, operands, n_sem, make_pairs):
    k = len(operands)

    def body(*refs):
        send_sems, recv_sems, token = refs[k], refs[k + 1], refs[-1]
        for send, _ in make_pairs(refs[:k], send_sems, recv_sems):
            send.start()
        token[...] = jnp.zeros_like(token)

    out = pl.pallas_call(
        body, name=name,
        out_shape=(pltpu.SemaphoreType.DMA((n_sem,)), pltpu.SemaphoreType.DMA((n_sem,)),
                   *[pltpu.HBM(a.shape, a.dtype) for a in operands], _sds((SUBLANES, LANES), F32)),
        in_specs=[_HBM_SPEC] * k,
        out_specs=(_SEM_SPEC, _SEM_SPEC, *[_HBM_SPEC] * k, pl.BlockSpec(memory_space=pltpu.VMEM)),
        input_output_aliases={i: 2 + i for i in range(k)},
        compiler_params=pltpu.CompilerParams(has_side_effects=_SIDE_EFFECT),
    )(*[pltpu.with_memory_space_constraint(a, pltpu.HBM) for a in operands])
    return dict(name=name, sems=out[:2], thru=list(out[2:2 + k]), make_pairs=make_pairs), out[-1]


def _split_wait(handle, after):
    thru, make_pairs = handle["thru"], handle["make_pairs"]
    k = len(thru)

    def body(*refs):
        for send, arrival in make_pairs(refs[:k], refs[k], refs[k + 1]):
            send.wait_send()
            arrival.wait_recv()

    return pl.pallas_call(
        body, name=handle["name"] + "_wait", out_shape=[pltpu.HBM(a.shape, a.dtype) for a in thru],
        in_specs=[_HBM_SPEC] * k + [_SEM_SPEC, _SEM_SPEC] + [pl.BlockSpec(memory_space=pl.ANY)] * len(after),
        out_specs=[_HBM_SPEC] * k, input_output_aliases={i: i for i in range(k)},
        compiler_params=pltpu.CompilerParams(has_side_effects=_SIDE_EFFECT),
    )(*thru, *handle["sems"], *after)


def _exchange_start(name, arrays, scatter, relations=ALL_PEERS):
    n = len(arrays)
    lands = [lax.empty(a.shape if scatter else (N_DEV,) + a.shape, a.dtype) for a in arrays]

    def make_pairs(refs, send_sems, recv_sems):
        return _exchange_copies(refs[:n], refs[n:], send_sems, recv_sems, scatter, relations)[1]

    handle, token = _split_start(name, list(arrays) + lands, n * len(relations), make_pairs)
    handle.update(n=n, scatter=scatter)
    return handle, token


def _forward_start(name, lands):
    return _split_start(name, list(lands), len(lands) * len(CHIP_PEERS), _forward_copies)


def _exchange_wait(handle, after):
    n, scatter = handle["n"], handle["scatter"]
    out = _split_wait(handle, after)
    me = 4 * lax.axis_index("x") + 2 * lax.axis_index("y") + lax.axis_index("c")
    done = []
    for src, land in zip(out[:n], out[n:]):
        own = lax.dynamic_index_in_dim(src, me, 0, keepdims=True) if scatter else src[None]
        done.append(lax.dynamic_update_slice_in_dim(land, own, me, 0))
    return done


def _rope_tables(pos_col):
    t = pos_col.shape[0]
    half = HEAD_DIM // 2
    inv_freq = ROPE_THETA ** (-jnp.arange(half, dtype=F32) / half)
    inv_row = jnp.tile(inv_freq, LANES // half)[None, :]

    def body(pos_ref, inv_ref, cos_ref, sin_ref):
        ang = pos_ref[...] * inv_ref[...]
        cos_ref[...] = jnp.cos(ang)
        sin_ref[...] = jnp.sin(ang)

    tm = min(t, 512)
    return _call(body, name="rope_tables", grid=(t // tm,),
                 in_specs=[pl.BlockSpec((tm, 1), lambda i: (i, 0)), pl.BlockSpec((1, LANES), lambda i: (0, 0))],
                 out_specs=[pl.BlockSpec((tm, LANES), lambda i: (i, 0))] * 2,
                 out_shape=[_sds((t, LANES), F32)] * 2, semantics=("parallel",))(pos_col, inv_row)


def _rot_half(x):
    lane = lax.broadcasted_iota(jnp.int32, x.shape, 1)
    low = (lane % HEAD_DIM) < HEAD_DIM // 2
    return jnp.where(low, -pltpu.roll(x, LANES - HEAD_DIM // 2, 1), pltpu.roll(x, HEAD_DIM // 2, 1))


def _rope(x, cos, sin):
    return x * cos + _rot_half(x) * sin


def _unrope(d, cos, sin):
    return d * cos - _rot_half(d) * sin


def _band_mask(first_block, heads):
    r = lax.broadcasted_iota(jnp.int32, (heads * BLOCK, 2 * BLOCK), 0) % BLOCK
    c = lax.broadcasted_iota(jnp.int32, (heads * BLOCK, 2 * BLOCK), 1)
    diff = r - c + BLOCK
    return (diff >= 0) & (diff < WINDOW) & ((c >= BLOCK) | jnp.logical_not(first_block))


def _attn_specs(t, d_attn, d_in):
    kb, vb = d_attn // D_KV, d_attn // D_KV + 1
    prev = lambda i: jnp.maximum(i - 1, 0)
    return [
        pl.BlockSpec((BLOCK, d_attn), lambda i: (i, 0)),
        pl.BlockSpec((BLOCK, D_KV), lambda i: (i, kb)),
        pl.BlockSpec((BLOCK, D_KV), lambda i: (i, vb)),
        pl.BlockSpec((BLOCK, D_KV), lambda i: (prev(i), kb)),
        pl.BlockSpec((BLOCK, D_KV), lambda i: (prev(i), vb)),
        pl.BlockSpec((BLOCK, LANES), lambda i: (i, 0)),
        pl.BlockSpec((BLOCK, LANES), lambda i: (i, 0)),
        pl.BlockSpec((BLOCK, LANES), lambda i: (prev(i), 0)),
        pl.BlockSpec((BLOCK, LANES), lambda i: (prev(i), 0)),
        pl.BlockSpec((1, LANES), lambda i: (0, 0)),
    ]


def _head(x, h):
    return x[:, h * HEAD_DIM:(h + 1) * HEAD_DIM]


def _attn_heads(q_ref, kc_ref, vc_ref, kp_ref, vp_ref, cq_ref, sq_ref, cp_ref, sp_ref, d_attn):
    cq, sq, cp, sp = cq_ref[...], sq_ref[...], cp_ref[...], sp_ref[...]
    q_rot = [_rope(q_ref[:, j * LANES:(j + 1) * LANES], cq, sq) for j in range(d_attn // LANES)]
    kc_rot = [_rope(kc_ref[:, j * LANES:(j + 1) * LANES], cq, sq) for j in range(D_KV // LANES)]
    kp_rot = [_rope(kp_ref[:, j * LANES:(j + 1) * LANES], cp, sp) for j in range(D_KV // LANES)]
    per = LANES // HEAD_DIM
    q_heads = [_head(q_rot[h // per], h % per).astype(BF16) for h in range(d_attn // HEAD_DIM)]
    kk = [jnp.concatenate([_head(kp_rot[g // per], g % per), _head(kc_rot[g // per], g % per)], axis=0).astype(BF16)
          for g in range(N_KV_HEADS)]
    vv = [jnp.concatenate([_head(vp_ref[...], g), _head(vc_ref[...], g)], axis=0).astype(BF16) for g in range(N_KV_HEADS)]
    return q_heads, kk, vv


def _stack_group(q_heads, sink_ref, group):
    q_all = jnp.concatenate([q_heads[h] for h in group], axis=0)
    sink_all = jnp.concatenate([jnp.broadcast_to(sink_ref[:, h:h + 1], (BLOCK, 1)) for h in group], axis=0)
    return q_all, sink_all


def _softmax_with_sink(q, kk, sink, mask):
    s = _dot(q, kk, 1, 1) * (1.0 / math.sqrt(HEAD_DIM))
    s = jnp.where(mask, s, MASKED)
    m = jnp.maximum(jnp.max(s, axis=-1, keepdims=True), sink)
    p = jnp.exp(s - m)
    e_sink = jnp.exp(sink - m)
    inv = 1.0 / (jnp.sum(p, axis=-1, keepdims=True) + e_sink)
    return p * inv, e_sink * inv


def _attention_fwd(proj, cos, sin, sinks_row, d_attn):
    t, d_in = proj.shape
    n_heads = d_attn // HEAD_DIM
    q_per_kv = n_heads // N_KV_HEADS

    def body(q_ref, kc_ref, vc_ref, kp_ref, vp_ref, cq_ref, sq_ref, cp_ref, sp_ref, sink_ref, o_ref):
        mask = _band_mask(pl.program_id(0) == 0, q_per_kv)
        q_heads, kk, vv = _attn_heads(q_ref, kc_ref, vc_ref, kp_ref, vp_ref, cq_ref, sq_ref, cp_ref, sp_ref, d_attn)
        for g in range(N_KV_HEADS):
            group = range(g * q_per_kv, (g + 1) * q_per_kv)
            q_all, sink_all = _stack_group(q_heads, sink_ref, group)
            probs, _ = _softmax_with_sink(q_all, kk[g], sink_all, mask)
            o_all = _dot(probs.astype(BF16), vv[g], 1, 0)
            for k, h in enumerate(group):
                o_ref[:, h * HEAD_DIM:(h + 1) * HEAD_DIM] = o_all[k * BLOCK:(k + 1) * BLOCK]

    return _call(body, name="attention_fwd", grid=(t // BLOCK,), in_specs=_attn_specs(t, d_attn, d_in),
                 out_specs=pl.BlockSpec((BLOCK, d_attn), lambda i: (i, 0)), out_shape=_sds((t, d_attn), F32),
                 semantics=("parallel",))(proj, proj, proj, proj, proj, cos, sin, cos, sin, sinks_row)


def _attention_bwd(proj, cos, sin, sinks_row, d_out, d_attn):
    t, d_in = proj.shape
    n_heads = d_attn // HEAD_DIM
    q_per_kv = n_heads // N_KV_HEADS
    nb = t // BLOCK
    per = LANES // HEAD_DIM

    def body(q_ref, kc_ref, vc_ref, kp_ref, vp_ref, cq_ref, sq_ref, cp_ref, sp_ref, sink_ref, do_ref,
             dq_ref, dk_ref, dv_ref, dsink_ref):
        i = pl.program_id(0)
        mask = _band_mask(i == 0, q_per_kv)
        q_heads, kk, vv = _attn_heads(q_ref, kc_ref, vc_ref, kp_ref, vp_ref, cq_ref, sq_ref, cp_ref, sp_ref, d_attn)
        lane = lax.broadcasted_iota(jnp.int32, (1, LANES), 1)
        dsink = jnp.zeros((1, LANES), F32)
        dq_rot, dkk, dvv = [], [], []
        for g in range(N_KV_HEADS):
            group = range(g * q_per_kv, (g + 1) * q_per_kv)
            q_all, sink_all = _stack_group(q_heads, sink_ref, group)
            probs, p_sink = _softmax_with_sink(q_all, kk[g], sink_all, mask)
            do_all = jnp.concatenate([do_ref[:, h * HEAD_DIM:(h + 1) * HEAD_DIM] for h in group], axis=0).astype(BF16)
            dp = _dot(do_all, vv[g], 1, 1)
            delta = jnp.sum(probs * dp, axis=-1, keepdims=True)
            ds = (probs * (dp - delta) * (1.0 / math.sqrt(HEAD_DIM))).astype(BF16)
            dq_all = _dot(ds, kk[g], 1, 0)
            dkk.append(_dot(ds, q_all, 0, 0))
            dvv.append(_dot(probs.astype(BF16), do_all, 0, 0))
            sink_term = p_sink * delta
            for k, h in enumerate(group):
                dq_rot.append(dq_all[k * BLOCK:(k + 1) * BLOCK])
                part = jnp.sum(sink_term[k * BLOCK:(k + 1) * BLOCK], axis=0, keepdims=True)
                dsink += jnp.where(lane == h, -part, 0.0)
        cq, sq, cp, sp = cq_ref[...], sq_ref[...], cp_ref[...], sp_ref[...]
        for j in range(d_attn // LANES):
            d = jnp.concatenate(dq_rot[j * per:(j + 1) * per], axis=1)
            dq_ref[:, j * LANES:(j + 1) * LANES] = _unrope(d, cq, sq)
        for j in range(D_KV // LANES):
            d = jnp.concatenate(dkk[j * per:(j + 1) * per], axis=1)
            dk_ref[0, :, j * LANES:(j + 1) * LANES] = _unrope(d[:BLOCK], cp, sp)
            dk_ref[1, :, j * LANES:(j + 1) * LANES] = _unrope(d[BLOCK:], cq, sq)
            d = jnp.concatenate(dvv[j * per:(j + 1) * per], axis=1)
            dv_ref[0, :, j * LANES:(j + 1) * LANES] = d[:BLOCK]
            dv_ref[1, :, j * LANES:(j + 1) * LANES] = d[BLOCK:]

        @pl.when(i == 0)
        def _():
            dsink_ref[...] = jnp.zeros_like(dsink_ref)

        dsink_ref[...] += dsink

    pair = pl.BlockSpec((2, BLOCK, D_KV), lambda i: (i, 0, 0))
    return _call(body, name="attention_bwd", grid=(nb,),
                 in_specs=_attn_specs(t, d_attn, d_in) + [pl.BlockSpec((BLOCK, d_attn), lambda i: (i, 0))],
                 out_specs=[pl.BlockSpec((BLOCK, d_attn), lambda i: (i, 0)), pair, pair,
                            pl.BlockSpec((1, LANES), lambda i: (0, 0))],
                 out_shape=[_sds((t, d_attn), F32), _sds((2 * nb, BLOCK, D_KV), F32), _sds((2 * nb, BLOCK, D_KV), F32),
                            _sds((1, LANES), F32)],
                 semantics=("arbitrary",))(proj, proj, proj, proj, proj, cos, sin, cos, sin, sinks_row, d_out)


def _assemble_dproj(dq, dk2, dv2, du, d_in):
    t, d_attn = dq.shape
    d_ssm = du.shape[1]
    nb = t // BLOCK

    def body(dq_ref, dk_own, dk_next, dv_own, dv_next, du_ref, o_ref):
        has_next = (pl.program_id(0) < nb - 1).astype(F32)
        o_ref[:, :d_attn] = dq_ref[...].astype(BF16)
        o_ref[:, d_attn:d_attn + D_KV] = (dk_own[...] + has_next * dk_next[...]).astype(BF16)
        o_ref[:, d_attn + D_KV:d_attn + 2 * D_KV] = (dv_own[...] + has_next * dv_next[...]).astype(BF16)
        o_ref[:, d_attn + 2 * D_KV:] = du_ref[...].astype(BF16)

    own = pl.BlockSpec((None, BLOCK, D_KV), lambda i: (2 * i + 1, 0, 0))
    nxt = pl.BlockSpec((None, BLOCK, D_KV), lambda i: (jnp.minimum(2 * i + 2, 2 * nb - 1), 0, 0))
    return _call(body, name="assemble_dproj", grid=(nb,),
                 in_specs=[pl.BlockSpec((BLOCK, d_attn), lambda i: (i, 0)), own, nxt, own, nxt,
                           pl.BlockSpec((BLOCK, d_ssm), lambda i: (i, 0))],
                 out_specs=pl.BlockSpec((BLOCK, d_in), lambda i: (i, 0)), out_shape=_sds((t, d_in), BF16),
                 semantics=("parallel",))(dq, dk2, dk2, dv2, dv2, du)


def _discretise(ar, ai, ldt, br, bi):
    dt = jnp.exp(ldt)
    mag = jnp.exp(ar * dt)
    lam_re = mag * jnp.cos(ai * dt)
    lam_im = mag * jnp.sin(ai * dt)
    den = ar * ar + ai * ai
    nr = lam_re - 1.0
    ni = lam_im
    f_re = (nr * ar + ni * ai) / den
    f_im = (ni * ar - nr * ai) / den
    return lam_re, lam_im, f_re[None] * br - f_im[None] * bi, f_re[None] * bi + f_im[None] * br


def _whole(arrays):
    return [pl.BlockSpec(a.shape, lambda *_, nd=len(a.shape): (0,) * nd) for a in arrays]


def _s5_discretise(ar, ai, ldt, br, bi):
    ins = [ar, ai, ldt, br, bi]

    def body(ar_ref, ai_ref, ldt_ref, br_ref, bi_ref, lr_ref, li_ref, bbr_ref, bbi_ref):
        out = _discretise(ar_ref[...], ai_ref[...], ldt_ref[...], br_ref[...], bi_ref[...])
        for ref, val in zip((lr_ref, li_ref, bbr_ref, bbi_ref), out):
            ref[...] = val

    outs = [_sds(ar.shape, F32), _sds(ar.shape, F32), _sds(br.shape, F32), _sds(br.shape, F32)]
    return _call(body, name="s5_discretise", in_specs=_whole(ins), out_specs=_whole(outs), out_shape=outs)(*ins)


def _s5_discretise_bwd(ar, ai, ldt, br, bi, d_lr, d_li, d_bbr, d_bbi):
    ins = [ar, ai, ldt, br, bi, d_lr, d_li, d_bbr, d_bbi]

    def body(ar_ref, ai_ref, ldt_ref, br_ref, bi_ref, dlr_ref, dli_ref, dbbr_ref, dbbi_ref, *out_refs):
        _, vjp = jax.vjp(_discretise, ar_ref[...], ai_ref[...], ldt_ref[...], br_ref[...], bi_ref[...])
        grads = vjp((dlr_ref[...], dli_ref[...], dbbr_ref[...], dbbi_ref[...]))
        for ref, val in zip(out_refs, grads):
            ref[...] = val

    outs = [_sds(a.shape, F32) for a in (ar, ai, ldt, br, bi)]
    return _call(body, name="s5_discretise_bwd", in_specs=_whole(ins), out_specs=_whole(outs), out_shape=outs)(*ins)


def _cmul(ar, ai, br, bi):
    return ar * br - ai * bi, ar * bi + ai * br


def _load_segmented(ref, tile0, n_tiles, seg):
    return jnp.concatenate([ref[pl.ds(tile0 + j, SUBLANES, stride=seg), :] for j in range(n_tiles)], axis=0)


def _store_segmented(ref, tile0, seg, value):
    for j in range(value.shape[0] // SUBLANES):
        ref[pl.ds(tile0 + j, SUBLANES, stride=seg), :] = value[j * SUBLANES:(j + 1) * SUBLANES, :]


def _fill_powers(lr, li, pr_ref, pi_ref, seg):
    pows = [(lr, li)]
    for _ in range(SUBLANES - 1):
        pows.append(_cmul(pows[-1][0], pows[-1][1], lr, li))
    row = lax.broadcasted_iota(jnp.int32, (SUBLANES, lr.shape[1]), 0)
    tr = jnp.zeros((SUBLANES, lr.shape[1]), F32)
    ti = jnp.zeros((SUBLANES, lr.shape[1]), F32)
    for r in range(SUBLANES):
        tr = jnp.where(row == r, pows[r][0], tr)
        ti = jnp.where(row == r, pows[r][1], ti)
    pr_ref[0:SUBLANES, :] = tr
    pi_ref[0:SUBLANES, :] = ti
    k = SUBLANES
    while k < seg:
        fr, fi = pr_ref[k - 1:k, :], pi_ref[k - 1:k, :]
        for t0 in range(0, k, SUBLANES):
            nr, ni = _cmul(pr_ref[t0:t0 + SUBLANES, :], pi_ref[t0:t0 + SUBLANES, :], fr, fi)
            pr_ref[k + t0:k + t0 + SUBLANES, :] = nr
            pi_ref[k + t0:k + t0 + SUBLANES, :] = ni
        k *= 2


def _scan_segments(sr_ref, si_ref, pr_ref, pi_ref, lr, li, seg, reverse, per_tile=None):
    w = lr.shape[1]
    sign = -1.0 if reverse else 1.0
    lrb = jnp.broadcast_to(lr, (SUBLANES, w))
    lib = jnp.broadcast_to(sign * li, (SUBLANES, w))
    zero = jnp.zeros((SUBLANES, w), F32)

    def tile_rows(j):
        return pl.ds(pl.multiple_of(j * SUBLANES, SUBLANES), SUBLANES)

    def local(i, carry):
        rows = tile_rows(seg - 1 - i if reverse else i)
        pr, pi = _cmul(lrb, lib, carry[0], carry[1])
        xr, xi = sr_ref[rows, :] + pr, si_ref[rows, :] + pi
        sr_ref[rows, :] = xr
        si_ref[rows, :] = xi
        return xr, xi

    end_r, end_i = lax.fori_loop(0, seg, local, (zero, zero))
    full_r, full_i = pr_ref[seg - 1:seg, :], sign * pi_ref[seg - 1:seg, :]
    row = lax.broadcasted_iota(jnp.int32, (SUBLANES, w), 0)
    in_r, in_i = zero, zero
    cur_r, cur_i = jnp.zeros((1, w), F32), jnp.zeros((1, w), F32)
    for r in (range(SUBLANES - 2, -1, -1) if reverse else range(1, SUBLANES)):
        src = r + 1 if reverse else r - 1
        pr, pi = _cmul(full_r, full_i, cur_r, cur_i)
        cur_r, cur_i = end_r[src:src + 1, :] + pr, end_i[src:src + 1, :] + pi
        in_r = jnp.where(row == r, cur_r, in_r)
        in_i = jnp.where(row == r, cur_i, in_i)

    def carry_in(j, _):
        rows = tile_rows(j)
        k = seg - 1 - j if reverse else j
        pr, pi = _cmul(pr_ref[pl.ds(k, 1), :], sign * pi_ref[pl.ds(k, 1), :], in_r, in_i)
        xr, xi = sr_ref[rows, :] + pr, si_ref[rows, :] + pi
        sr_ref[rows, :] = xr
        si_ref[rows, :] = xi
        if per_tile is not None:
            per_tile(j, xr, xi)
        return 0

    lax.fori_loop(0, seg, carry_in, 0)


_S5_ROWS = 256


def _s5_in_specs(t, d_attn):
    u_block = (d_attn + 2 * D_KV) // SSM_CH_BLOCK
    blk3 = lambda shape: pl.BlockSpec((None,) + shape, lambda j: (j, 0, 0))
    return [
        pl.BlockSpec((t, SSM_CH_BLOCK), lambda j: (0, u_block + j)),
        blk3((SSM_CH_BLOCK, SSM_ST_BLOCK)), blk3((SSM_CH_BLOCK, SSM_ST_BLOCK)),
        blk3((1, SSM_ST_BLOCK)), blk3((1, SSM_ST_BLOCK)),
        blk3((SSM_ST_BLOCK, SSM_CH_BLOCK)), blk3((SSM_ST_BLOCK, SSM_CH_BLOCK)),
        pl.BlockSpec((1, SSM_CH_BLOCK), lambda j: (0, j)),
    ]


def _chunks(t):
    rows = min(_S5_ROWS, t)
    return rows, lambda i: pl.ds(pl.multiple_of(i * rows, rows), rows)


def _s5_states(u_ref, us_ref, bre_ref, bim_ref, lr_ref, li_ref, sr_ref, si_ref, pr_ref, pi_ref, t):
    seg = t // SUBLANES
    rows, chunk = _chunks(t)
    for c in range(t // rows):
        us_ref[c * rows:(c + 1) * rows, :] = _load_segmented(u_ref, c * rows // SUBLANES, rows // SUBLANES, seg)

    def fill(i, _):
        ub = us_ref[chunk(i), :].astype(BF16)
        sr_ref[chunk(i), :] = _dot(ub, bre_ref[...], 1, 0)
        si_ref[chunk(i), :] = _dot(ub, bim_ref[...], 1, 0)
        return 0

    lax.fori_loop(0, t // rows, fill, 0)
    _fill_powers(lr_ref[...], li_ref[...], pr_ref, pi_ref, seg)
    _scan_segments(sr_ref, si_ref, pr_ref, pi_ref, lr_ref[...], li_ref[...], seg, False)


def _s5_scratch(t):
    state = pltpu.VMEM((t, SSM_ST_BLOCK), F32)
    powers = pltpu.VMEM((t // SUBLANES, SSM_ST_BLOCK), F32)
    return state, powers, pltpu.VMEM((t, SSM_CH_BLOCK), F32)


def _s5_fwd(proj, mats, dskip_row, d_attn, d_ssm):
    t = proj.shape[0]
    seg = t // SUBLANES
    n_blocks = d_ssm // SSM_CH_BLOCK
    rows, chunk = _chunks(t)

    def body(u_ref, bre_ref, bim_ref, lr_ref, li_ref, cre_ref, cim_ref, d_ref, y_ref,
             sr_ref, si_ref, pr_ref, pi_ref, us_ref, ys_ref):
        _s5_states(u_ref, us_ref, bre_ref, bim_ref, lr_ref, li_ref, sr_ref, si_ref, pr_ref, pi_ref, t)

        def emit(i, _):
            ys_ref[chunk(i), :] = (_dot(sr_ref[chunk(i), :].astype(BF16), cre_ref[...], 1, 0)
                                   - _dot(si_ref[chunk(i), :].astype(BF16), cim_ref[...], 1, 0)
                                   + d_ref[...] * us_ref[chunk(i), :])
            return 0

        lax.fori_loop(0, t // rows, emit, 0)
        for c in range(t // rows):
            _store_segmented(y_ref, c * rows // SUBLANES, seg, ys_ref[c * rows:(c + 1) * rows, :])

    state, powers, channels = _s5_scratch(t)
    col = pl.BlockSpec((t, SSM_CH_BLOCK), lambda j: (0, j))
    return _call(body, name="s5_fwd", grid=(n_blocks,), in_specs=_s5_in_specs(t, d_attn), out_specs=col,
                 out_shape=_sds((t, d_ssm), F32), scratch_shapes=[state, state, powers, powers, channels, channels],
                 semantics=("parallel",))(proj, *mats, dskip_row)


def _s5_bwd(proj, mats, dskip_row, y, dz_a, dz_b, d_attn, d_ssm):
    t = proj.shape[0]
    seg = t // SUBLANES
    n_blocks = d_ssm // SSM_CH_BLOCK
    rows, chunk = _chunks(t)

    def body(u_ref, bre_ref, bim_ref, lr_ref, li_ref, cre_ref, cim_ref, d_ref, y_ref, dza_ref, dzb_ref,
             du_ref, dbre_ref, dbim_ref, dlr_ref, dli_ref, dcre_ref, dcim_ref, dd_ref,
             sr_ref, si_ref, gr_ref, gi_ref, pr_ref, pi_ref, us_ref, dys_ref, dus_ref, acc_r, acc_i):
        _s5_states(u_ref, us_ref, bre_ref, bim_ref, lr_ref, li_ref, sr_ref, si_ref, pr_ref, pi_ref, t)
        for ref in (dcre_ref, dcim_ref, dbre_ref, dbim_ref, dd_ref, acc_r, acc_i):
            ref[...] = jnp.zeros_like(ref)
        for c in range(t // rows):
            tile0, n_tiles = c * rows // SUBLANES, rows // SUBLANES
            dz = _load_segmented(dza_ref, tile0, n_tiles, seg) + _load_segmented(dzb_ref, tile0, n_tiles, seg)
            dys_ref[c * rows:(c + 1) * rows, :] = dz * _gelu_grad(_load_segmented(y_ref, tile0, n_tiles, seg))

        def through_c(i, _):
            dy = dys_ref[chunk(i), :]
            dd_ref[...] += jnp.sum(dy * us_ref[chunk(i), :], axis=0, keepdims=True)
            dyb = dy.astype(BF16)
            gr_ref[chunk(i), :] = _dot(dyb, cre_ref[...], 1, 1)
            gi_ref[chunk(i), :] = -_dot(dyb, cim_ref[...], 1, 1)
            dcre_ref[...] += _dot(sr_ref[chunk(i), :].astype(BF16), dyb, 0, 0)
            dcim_ref[...] -= _dot(si_ref[chunk(i), :].astype(BF16), dyb, 0, 0)
            return 0

        lax.fori_loop(0, t // rows, through_c, 0)

        row = lax.broadcasted_iota(jnp.int32, (SUBLANES, SSM_ST_BLOCK), 0)
        last = pl.ds((seg - 1) * SUBLANES, SUBLANES)
        wrap = [jnp.where(row == 0, 0.0, pltpu.roll(ref[last, :], 1, 0)) for ref in (sr_ref, si_ref)]

        def lambda_grad(j, g_re, g_im):
            before = pl.ds(pl.multiple_of(jnp.maximum(j - 1, 0) * SUBLANES, SUBLANES), SUBLANES)
            prev_r = jnp.where(j > 0, sr_ref[before, :], wrap[0])
            prev_i = jnp.where(j > 0, si_ref[before, :], wrap[1])
            acc_r[...] += g_re * prev_r + g_im * prev_i
            acc_i[...] += g_im * prev_r - g_re * prev_i

        _scan_segments(gr_ref, gi_ref, pr_ref, pi_ref, lr_ref[...], li_ref[...], seg, True, per_tile=lambda_grad)
        dlr_ref[...] = jnp.sum(acc_r[...], axis=0, keepdims=True)
        dli_ref[...] = jnp.sum(acc_i[...], axis=0, keepdims=True)

        def through_b(i, _):
            ub = us_ref[chunk(i), :].astype(BF16)
            grb, gib = gr_ref[chunk(i), :].astype(BF16), gi_ref[chunk(i), :].astype(BF16)
            dbre_ref[...] += _dot(ub, grb, 0, 0)
            dbim_ref[...] += _dot(ub, gib, 0, 0)
            dus_ref[chunk(i), :] = (_dot(grb, bre_ref[...], 1, 1) + _dot(gib, bim_ref[...], 1, 1)
                                    + d_ref[...] * dys_ref[chunk(i), :])
            return 0

        lax.fori_loop(0, t // rows, through_b, 0)
        for c in range(t // rows):
            _store_segmented(du_ref, c * rows // SUBLANES, seg, dus_ref[c * rows:(c + 1) * rows, :])

    col = pl.BlockSpec((t, SSM_CH_BLOCK), lambda j: (0, j))
    blk3 = lambda shape: pl.BlockSpec((None,) + shape, lambda j: (j, 0, 0))
    state, powers, channels = _s5_scratch(t)
    return _call(
        body, name="s5_bwd", grid=(n_blocks,), in_specs=_s5_in_specs(t, d_attn) + [col, col, col],
        out_specs=[col, blk3((SSM_CH_BLOCK, SSM_ST_BLOCK)), blk3((SSM_CH_BLOCK, SSM_ST_BLOCK)),
                   blk3((1, SSM_ST_BLOCK)), blk3((1, SSM_ST_BLOCK)),
                   blk3((SSM_ST_BLOCK, SSM_CH_BLOCK)), blk3((SSM_ST_BLOCK, SSM_CH_BLOCK)),
                   pl.BlockSpec((1, SSM_CH_BLOCK), lambda j: (0, j))],
        out_shape=[_sds((t, d_ssm), F32),
                   _sds((n_blocks, SSM_CH_BLOCK, SSM_ST_BLOCK), F32), _sds((n_blocks, SSM_CH_BLOCK, SSM_ST_BLOCK), F32),
                   _sds((n_blocks, 1, SSM_ST_BLOCK), F32), _sds((n_blocks, 1, SSM_ST_BLOCK), F32),
                   _sds((n_blocks, SSM_ST_BLOCK, SSM_CH_BLOCK), F32), _sds((n_blocks, SSM_ST_BLOCK, SSM_CH_BLOCK), F32),
                   _sds((1, d_ssm), F32)],
        scratch_shapes=[state, state, state, state, powers, powers, channels, channels, channels,
                        pltpu.VMEM((SUBLANES, SSM_ST_BLOCK), F32), pltpu.VMEM((SUBLANES, SSM_ST_BLOCK), F32)],
        semantics=("parallel",))(proj, *mats, dskip_row, y, dz_a, dz_b)


def _block_diag_in(bbar_pgn):
    p, g, n = bbar_pgn.shape
    b4 = bbar_pgn.reshape(p, g // GROUPS_PER_BLOCK, GROUPS_PER_BLOCK, n)
    eye = jnp.eye(GROUPS_PER_BLOCK, dtype=F32)
    return jnp.einsum("pjgn,gh->jgphn", b4, eye).reshape(g // GROUPS_PER_BLOCK, SSM_CH_BLOCK, SSM_ST_BLOCK)


def _block_diag_in_t(dense):
    j = dense.shape[0]
    d5 = dense.reshape(j, GROUPS_PER_BLOCK, SSM_GROUP, GROUPS_PER_BLOCK, SSM_STATE)
    eye = jnp.eye(GROUPS_PER_BLOCK, dtype=F32)
    return jnp.einsum("jgphn,gh->pjgn", d5, eye).reshape(SSM_GROUP, j * GROUPS_PER_BLOCK, SSM_STATE)


def _block_diag_out(c_gpn):
    g, p, n = c_gpn.shape
    c4 = c_gpn.reshape(g // GROUPS_PER_BLOCK, GROUPS_PER_BLOCK, p, n)
    eye = jnp.eye(GROUPS_PER_BLOCK, dtype=F32)
    return jnp.einsum("jgpn,gh->jgnhp", c4, eye).reshape(g // GROUPS_PER_BLOCK, SSM_ST_BLOCK, SSM_CH_BLOCK)


def _block_diag_out_t(dense):
    j = dense.shape[0]
    d5 = dense.reshape(j, GROUPS_PER_BLOCK, SSM_STATE, GROUPS_PER_BLOCK, SSM_GROUP)
    eye = jnp.eye(GROUPS_PER_BLOCK, dtype=F32)
    return jnp.einsum("jgnhp,gh->jgpn", d5, eye).reshape(j * GROUPS_PER_BLOCK, SSM_GROUP, SSM_STATE)


def _adamw(w, g, m, v):
    m = ADAM_B1 * m + (1.0 - ADAM_B1) * g
    v = ADAM_B2 * v + (1.0 - ADAM_B2) * (g * g)
    m_hat = m / (1.0 - ADAM_B1 ** ADAM_STEP)
    v_hat = v / (1.0 - ADAM_B2 ** ADAM_STEP)
    delta = -ADAM_LR * (m_hat / (jnp.sqrt(v_hat) + ADAM_EPS) + ADAM_WD * w)
    return delta, m, v


def _adam_sharded(name, parts, w, m, v, tr):
    r, c = w.shape
    assert r % tr == 0, (name, r, tr)

    def body(p_ref, w_ref, m_ref, v_ref, g_out, d_out, m_out, v_out):
        g = p_ref[0].astype(F32)
        for i in range(1, N_DEV):
            g = g + p_ref[i].astype(F32)
        delta, m_new, v_new = _adamw(w_ref[...], g, m_ref[...], v_ref[...])
        g_out[...] = g
        d_out[...] = delta
        m_out[...] = m_new
        v_out[...] = v_new

    tile = pl.BlockSpec((tr, c), lambda i: (i, 0))
    return _call(body, name=name, grid=(r // tr,),
                 in_specs=[pl.BlockSpec((N_DEV, tr, c), lambda i: (0, i, 0)), tile, tile, tile],
                 out_specs=[tile] * 4, out_shape=[_sds((r, c), F32)] * 4, semantics=("parallel",))(parts, w, m, v)


_SMALL = ("g_pre_mix", "sinks", "a_re", "a_im", "log_dt", "b_re", "b_im", "c_re", "c_im", "d_skip", "b_glu",
          "g_attn_out", "g_ssm_out", "g_post_mix", "g_pre_ffn", "g_post_ffn")
_BIG = ("w_in", "w_glu", "w_o", "w_gate", "w_up", "w_down")
_BY_COLUMNS = ("w_in", "w_gate", "w_up")
_ORDER = ("g_pre_mix", "w_in", "sinks", "a_re", "a_im", "log_dt", "b_re", "b_im", "c_re", "c_im", "d_skip", "w_glu",
          "b_glu", "g_attn_out", "g_ssm_out", "w_o", "g_post_mix", "g_pre_ffn", "w_gate", "w_up", "w_down",
          "g_post_ffn")


def _pack(arrays):
    flat = jnp.concatenate([a.reshape(-1).astype(F32) for a in arrays])
    pad = (-flat.shape[0]) % (SUBLANES * LANES)
    return jnp.pad(flat, (0, pad)).reshape(-1, LANES)


def _unpack(packed, like):
    flat = packed.reshape(-1)
    out, at = [], 0
    for a in like:
        out.append(flat[at:at + a.size].reshape(a.shape))
        at += a.size
    return out


def kernel(x, positions, g_pre_mix, w_in, sinks, a_re, a_im, log_dt, b_re, b_im, c_re, c_im, d_skip, w_glu, b_glu, g_attn_out, g_ssm_out, w_o, g_post_mix, g_pre_ffn, w_gate, w_up, w_down, g_post_ffn, loss_target, m_g_pre_mix, m_w_in, m_sinks, m_a_re, m_a_im, m_log_dt, m_b_re, m_b_im, m_c_re, m_c_im, m_d_skip, m_w_glu, m_b_glu, m_g_attn_out, m_g_ssm_out, m_w_o, m_g_post_mix, m_g_pre_ffn, m_w_gate, m_w_up, m_w_down, m_g_post_ffn, v_g_pre_mix, v_w_in, v_sinks, v_a_re, v_a_im, v_log_dt, v_b_re, v_b_im, v_c_re, v_c_im, v_d_skip, v_w_glu, v_b_glu, v_g_attn_out, v_g_ssm_out, v_w_o, v_g_post_mix, v_g_pre_ffn, v_w_gate, v_w_up, v_w_down, v_g_post_ffn):
    given = dict(locals())
    weights = {n: given[n] for n in _ORDER}
    mom_m = {n: given["m_" + n] for n in _ORDER}
    mom_v = {n: given["v_" + n] for n in _ORDER}

    t, d = x.shape[1], x.shape[2]
    d_attn = d // 2
    d_ssm = d - d_attn
    d_in = d_attn + 2 * D_KV + d_ssm
    n_groups = d_ssm // SSM_GROUP
    n_heads = d_attn // HEAD_DIM
    tm = min(256, t)

    x2 = x[0]
    target = loss_target[0]

    def by_rows(n, a):
        return a[0].T if n in _BY_COLUMNS else a[0]

    def start_gather(name, ns, token):
        behind = 0 if token is None else token[0, 0].astype(BF16)
        shards = [by_rows(n, weights[n]).astype(BF16) + behind for n in ns]
        return _exchange_start(name, shards, False, (SIBLING,) + CHIP_PEERS)

    def finish_gather(handle, after):
        forward, _ = _forward_start(handle["name"] + "_forward", _exchange_wait(handle, after))
        return _split_wait(forward, [])

    ag_in, token = start_gather("gather_w_in", ["w_in"], None)
    ag_mix, token = start_gather("gather_w_glu_o", ["w_glu", "w_o"], token)
    ag_ffn_in, token = start_gather("gather_w_gate_up", ["w_gate", "w_up"], token)
    ag_down, token = start_gather("gather_w_down", ["w_down"], token)

    xn, = _rows("norm_in", lambda xv, g: ([_rms(xv)[0] * g], []), [x2], [g_pre_mix], [(d, BF16)], [], tm,
                after=[token])
    win_g, = finish_gather(ag_in, [xn])
    w_in_t = win_g.reshape(d_in, d)
    proj = _mm_nt("proj_in", xn, w_in_t, F32, tn=d_in // 4 if (d_in // 4) % LANES == 0 else None)

    cos, sin = _rope_tables(positions.reshape(t, 1).astype(F32))
    sinks_row = jnp.pad(sinks, ((0, 0), (0, LANES - n_heads)))
    attn = _attention_fwd(proj, cos, sin, sinks_row, d_attn)

    b_re_t, b_im_t = b_re[0].transpose(2, 0, 1), b_im[0].transpose(2, 0, 1)
    ldt_col = log_dt.reshape(n_groups, 1)
    lam_re, lam_im, bbar_re, bbar_im = _s5_discretise(a_re[0], a_im[0], ldt_col, b_re_t, b_im_t)
    n_blocks = n_groups // GROUPS_PER_BLOCK
    mats = [_block_diag_in(bbar_re).astype(BF16), _block_diag_in(bbar_im).astype(BF16),
            lam_re.reshape(n_blocks, 1, SSM_ST_BLOCK), lam_im.reshape(n_blocks, 1, SSM_ST_BLOCK),
            _block_diag_out(c_re[0]).astype(BF16), _block_diag_out(c_im[0]).astype(BF16)]
    dskip_row = d_skip.reshape(1, d_ssm)
    y_ssm = _s5_fwd(proj, mats, dskip_row, d_attn, d_ssm)
    gelu_bf16 = lambda yv: _gelu(yv).astype(BF16)
    wglu_g, wo_g = finish_gather(ag_mix, [attn, y_ssm])
    w_glu_full = wglu_g.reshape(d_ssm, d_ssm)
    w_o_full = wo_g.reshape(d, d)
    glu_lin = _mm_nn("glu_gate", y_ssm, w_glu_full, F32, a_fn=gelu_bf16)

    def mix_prep(av, yv, gl, bg, ga, gs):
        ssm = _gelu(yv) * _sigmoid(gl + bg)
        return [jnp.concatenate([_rms(av)[0] * ga, _rms(ssm)[0] * gs], axis=1)], []

    mixed, = _rows("mix_prep", mix_prep, [attn, y_ssm, glu_lin], [b_glu, g_attn_out, g_ssm_out], [(d, BF16)], [], tm)
    mix = _mm_nn("mix_out", mixed, w_o_full, F32, tn=d // 2 if (d // 2) % LANES == 0 else None)

    def post_mix(xv, mv, gpm, gpf):
        h = xv + _rms(mv)[0] * gpm
        return [h, _rms(h)[0] * gpf], []

    h, hn = _rows("post_mix", post_mix, [x2, mix], [g_post_mix, g_pre_ffn], [(d, F32), (d, BF16)], [], tm)
    wgate_g, wup_g = finish_gather(ag_ffn_in, [hn])
    gate, up, hid = _ffn_in(hn, wgate_g, wup_g)
    wdown_g, = finish_gather(ag_down, [hid])
    ff = _mm_contract_slots("ffn_down", [(hid, wdown_g)], F32, per_step=4)

    def head(hv, fv, tv, gpo):
        out = hv + _rms(fv)[0] * gpo
        err = out - tv
        dout = err * (1.0 / d)
        dff, dg = _rms_bwd(fv, gpo, dout)
        loss = jnp.zeros((1, LANES), F32) + 0.5 * jnp.sum(err * err) * (1.0 / d)
        return [dff, dout], [dg, loss]

    dff, dh_out, dg_post_ffn, loss_row = _rows("loss_head", head, [h, ff, target], [g_post_ffn],
                                               [(d, BF16), (d, F32)], [d, LANES], tm)

    dw_down = _mm_slots_tn("ffn_down_dw", hid, dff, BF16)
    rs_down, tok_down = _exchange_start("scatter_dw_down", [dw_down], True)
    dgate, dup = _ffn_down_bwd(dff, wdown_g, gate, up, [tok_down])
    dhn = _mm_contract_slots("ffn_in_dx", [(dgate, wgate_g), (dup, wup_g)], F32, per_step=2)
    dw_gate = _mm_slots_tn("ffn_gate_dw", dgate, hn, BF16)
    dw_up = _mm_slots_tn("ffn_up_dw", dup, hn, BF16)
    rs_ffn_in, tok_ffn_in = _exchange_start("scatter_dw_gate_up", [dw_gate, dw_up], True)

    def mid_bwd(dho, dhn_, hv, mv, gpf, gpm):
        d1, dgpf = _rms_bwd(hv, gpf, dhn_)
        dh_ = dho + d1
        dmix_, dgpm = _rms_bwd(mv, gpm, dh_)
        return [dh_, dmix_], [dgpf, dgpm]

    dh, dmix, dg_pre_ffn, dg_post_mix = _rows("mid_bwd", mid_bwd, [dh_out, dhn, h, mix], [g_pre_ffn, g_post_mix],
                                              [(d, F32), (d, BF16)], [d, d], tm, after=[tok_ffn_in])

    dmixed = _mm_nt("mix_out_dx", dmix, w_o_full, F32, tn=d // 2 if (d // 2) % LANES == 0 else None)
    dw_o = _mm_tn("mix_out_dw", mixed, dmix, BF16, tn=d // 2 if (d // 2) % LANES == 0 else None)
    rs_o, tok_o = _exchange_start("scatter_dw_o", [dw_o.reshape(N_DEV, d // N_DEV, d)], True)

    def mix_bwd(dm, av, yv, gl, bg, ga, gs):
        dattn_, dga = _rms_bwd(av, ga, dm[:, :d_attn])
        z = _gelu(yv)
        sg = _sigmoid(gl + bg)
        dssm, dgs = _rms_bwd(z * sg, gs, dm[:, d_attn:])
        dgl = dssm * z * sg * (1.0 - sg)
        return [dattn_, dssm * sg, dgl], [dga, dgs, jnp.sum(dgl, axis=0, keepdims=True)]

    dattn, dz_direct, dglu, dg_attn_out, dg_ssm_out, db_glu = _rows(
        "mix_bwd", mix_bwd, [dmixed, attn, y_ssm, glu_lin], [b_glu, g_attn_out, g_ssm_out],
        [(d_attn, F32), (d_ssm, F32), (d_ssm, BF16)], [d_attn, d_ssm, d_ssm], tm, after=[tok_o])
    dz_glu = _mm_nt("glu_gate_dx", dglu, w_glu_full, F32)
    dw_glu = _mm_tn("glu_gate_dw", y_ssm, dglu, BF16, a_fn=gelu_bf16)

    du, db_re_dense, db_im_dense, dlam_re, dlam_im, dc_re_dense, dc_im_dense, dd_skip = _s5_bwd(
        proj, mats, dskip_row, y_ssm, dz_direct, dz_glu, d_attn, d_ssm)
    da_re, da_im, dlog_dt, db_re_t, db_im_t = _s5_discretise_bwd(
        a_re[0], a_im[0], ldt_col, b_re_t, b_im_t, dlam_re.reshape(n_groups, SSM_STATE),
        dlam_im.reshape(n_groups, SSM_STATE), _block_diag_in_t(db_re_dense), _block_diag_in_t(db_im_dense))
    dq, dk2, dv2, dsinks_row = _attention_bwd(proj, cos, sin, sinks_row, dattn, d_attn)
    dproj = _assemble_dproj(dq, dk2, dv2, du, d_in)

    dxn = _mm_nn("proj_in_dx", dproj, w_in_t, F32, tn=d // 2 if (d // 2) % LANES == 0 else None)
    dw_in = _mm_tn("proj_in_dw", dproj, xn, BF16).reshape(N_DEV, d_in // N_DEV, d)

    def x_bwd(dh_, dxn_, xv, g):
        dx, dg = _rms_bwd(xv, g, dxn_)
        return [dh_ + dx], [dg]

    grad_x, dg_pre_mix = _rows("norm_in_bwd", x_bwd, [dh, dxn, x2], [g_pre_mix], [(d, F32)], [d], tm)

    small_grads = {
        "g_pre_mix": dg_pre_mix, "sinks": dsinks_row[:, :n_heads], "a_re": da_re[None], "a_im": da_im[None],
        "log_dt": dlog_dt.reshape(1, n_groups), "b_re": db_re_t.transpose(1, 2, 0)[None],
        "b_im": db_im_t.transpose(1, 2, 0)[None], "c_re": _block_diag_out_t(dc_re_dense)[None],
        "c_im": _block_diag_out_t(dc_im_dense)[None], "d_skip": dd_skip.reshape(d_skip.shape), "b_glu": db_glu,
        "g_attn_out": dg_attn_out, "g_ssm_out": dg_ssm_out, "g_post_mix": dg_post_mix, "g_pre_ffn": dg_pre_ffn,
        "g_post_ffn": dg_post_ffn,
    }
    small_like = [weights[n] for n in _SMALL]
    packed = _pack([small_grads[n] for n in _SMALL])
    rs_in, token = _exchange_start("scatter_dw_in_glu", [dw_in, dw_glu.reshape(N_DEV, d_ssm // N_DEV, d_ssm)], True)
    ag_small, token = _exchange_start("gather_small_grads", [packed + token[:1, :]], False)

    results = {}

    def adam_big(n, parts):
        r = parts.shape[1]
        results[n] = _adam_sharded("adam_" + n, parts, by_rows(n, weights[n]), by_rows(n, mom_m[n]),
                                   by_rows(n, mom_v[n]), 64 if r % 64 == 0 else r)
        return results[n][3]

    done = [grad_x, token]
    adam_big("w_down", _exchange_wait(rs_down, done)[0])
    p_gate, p_up = _exchange_wait(rs_ffn_in, done)
    done = [adam_big("w_gate", p_gate), adam_big("w_up", p_up), results["w_down"][3]]
    done = [adam_big("w_o", _exchange_wait(rs_o, done)[0])]
    p_in, p_glu = _exchange_wait(rs_in, done)
    done = [adam_big("w_in", p_in), adam_big("w_glu", p_glu)]
    small_all, = _exchange_wait(ag_small, done)
    g_s, d_s, m_s, v_s = _adam_sharded(
        "adam_small", small_all, _pack(small_like), _pack([mom_m[n] for n in _SMALL]),
        _pack([mom_v[n] for n in _SMALL]), packed.shape[0])
    for i, vals in enumerate(zip(*[_unpack(p, small_like) for p in (g_s, d_s, m_s, v_s)])):
        results[_SMALL[i]] = vals

    loss = lax.psum(loss_row[0, 0], ("x", "y", "c"))
    outs = [loss, grad_x[None]]
    for k in range(4):
        for n in _ORDER:
            val = results[n][k]
            outs.append(val.T[None] if n in _BY_COLUMNS else val[None] if n in _BIG else val)
    return tuple(outs)
```

```python
import math

import jax
import jax.numpy as jnp
from jax import lax
from jax.experimental import pallas as pl
from jax.experimental.pallas import tpu as pltpu

F32 = jnp.float32
BF16 = jnp.bfloat16

HEAD_DIM = 64
N_KV_HEADS = 4
D_KV = N_KV_HEADS * HEAD_DIM
WINDOW = 128
BLOCK = 128
ROPE_THETA = 10000.0
SSM_GROUP = 16
SSM_STATE = 64
GROUPS_PER_BLOCK = 8
SSM_CH_BLOCK = GROUPS_PER_BLOCK * SSM_GROUP
SSM_ST_BLOCK = GROUPS_PER_BLOCK * SSM_STATE
RMS_EPS = 1e-6
N_DEV = 8
LANES = 128
SUBLANES = 8
MASKED = -1e30

ADAM_LR = 0.001
ADAM_B1 = 0.9
ADAM_B2 = 0.999
ADAM_EPS = 1e-08
ADAM_WD = 0.01
ADAM_STEP = 10

VMEM_LIMIT_BYTES = 56 * 1024 * 1024


def _call(body, *, name, out_shape, in_specs, out_specs, grid=(), scratch_shapes=(), semantics=None, n_after=0):
    params = dict(vmem_limit_bytes=VMEM_LIMIT_BYTES)
    if semantics is not None:
        params["dimension_semantics"] = semantics
    n_in = len(in_specs)
    if n_after:
        inner = body

        def body(*refs):
            inner(*refs[:n_in], *refs[n_in + n_after:])

        in_specs = list(in_specs) + [pl.BlockSpec(memory_space=pl.ANY)] * n_after
    return pl.pallas_call(body, name=name, grid=grid, in_specs=in_specs, out_specs=out_specs, out_shape=out_shape,
                          scratch_shapes=scratch_shapes, compiler_params=pltpu.CompilerParams(**params))


def _sds(shape, dtype):
    return jax.ShapeDtypeStruct(tuple(shape), dtype)


def _dot(a, b, ca, cb):
    return lax.dot_general(a, b, (((ca,), (cb,)), ((), ())), preferred_element_type=F32)


def _rms(x):
    r = lax.rsqrt(jnp.mean(x * x, axis=-1, keepdims=True) + RMS_EPS)
    return x * r, r


def _rms_bwd(x, g, dy):
    xh, r = _rms(x)
    dxh = dy * g
    dx = r * (dxh - xh * jnp.mean(dxh * xh, axis=-1, keepdims=True))
    return dx, jnp.sum(dy * xh, axis=0, keepdims=True)


def _sigmoid(x):
    return 1.0 / (1.0 + jnp.exp(-x))


_GELU_C = math.sqrt(2.0 / math.pi)
_GELU_A = 0.044715


def _gelu(y):
    t = jnp.tanh(_GELU_C * (y + _GELU_A * y * y * y))
    return 0.5 * y * (1.0 + t)


def _gelu_grad(y):
    t = jnp.tanh(_GELU_C * (y + _GELU_A * y * y * y))
    return 0.5 * (1.0 + t) + 0.5 * y * (1.0 - t * t) * _GELU_C * (1.0 + 3.0 * _GELU_A * y * y)


def _rows(name, fn, row_ins, vec_ins, row_outs, acc_widths, tm, after=()):
    rows = row_ins[0].shape[0]
    assert rows % tm == 0, (name, rows, tm)
    n_row, n_vec, n_out, n_acc = len(row_ins), len(vec_ins), len(row_outs), len(acc_widths)

    def body(*refs):
        ins = [r[...] for r in refs[:n_row + n_vec]]
        outs = refs[n_row + n_vec:n_row + n_vec + n_out]
        accs = refs[n_row + n_vec + n_out:]
        row_vals, acc_vals = fn(*ins)
        for o, v in zip(outs, row_vals):
            o[...] = v.astype(o.dtype)
        if n_acc:
            @pl.when(pl.program_id(0) == 0)
            def _():
                for a in accs:
                    a[...] = jnp.zeros_like(a)
            for a, v in zip(accs, acc_vals):
                a[...] += v

    in_specs = [pl.BlockSpec((tm, a.shape[1]), lambda i: (i, 0)) for a in row_ins]
    in_specs += [pl.BlockSpec(v.shape, lambda i: (0, 0)) for v in vec_ins]
    out_specs = [pl.BlockSpec((tm, w), lambda i: (i, 0)) for w, _ in row_outs]
    out_specs += [pl.BlockSpec((1, w), lambda i: (0, 0)) for w in acc_widths]
    out_shape = [_sds((rows, w), dt) for w, dt in row_outs] + [_sds((1, w), F32) for w in acc_widths]
    return _call(body, name=name, grid=(rows // tm,), in_specs=in_specs, out_specs=out_specs, out_shape=out_shape,
                 semantics=("arbitrary",) if n_acc else ("parallel",), n_after=len(after))(*row_ins, *vec_ins, *after)


def _matmul(name, operands, in_specs, product, grid, out_shape, out_spec, acc_shape, after=()):
    nk = grid[-1]
    n_in = len(operands)
    in_place = out_shape.dtype == F32

    def body(*refs):
        ins = [r[...] for r in refs[:n_in]]
        o_ref = refs[n_in]
        if nk == 1:
            o_ref[...] = product(*ins).astype(o_ref.dtype)
            return
        acc = o_ref if in_place else refs[n_in + 1]
        k = pl.program_id(len(grid) - 1)

        @pl.when(k == 0)
        def _():
            acc[...] = jnp.zeros_like(acc)

        acc[...] += product(*ins)

        if not in_place:
            @pl.when(k == nk - 1)
            def _():
                o_ref[...] = acc[...].astype(o_ref.dtype)

    return _call(body, name=name, grid=grid, in_specs=in_specs, out_specs=out_spec, out_shape=out_shape,
                 scratch_shapes=[] if nk == 1 or in_place else [pltpu.VMEM(acc_shape, F32)],
                 semantics=("parallel",) * (len(grid) - 1) + ("arbitrary",), n_after=len(after))(*operands, *after)


def _mm_nn(name, a, b, out_dtype, tm=512, tn=None, a_fn=lambda x: x):
    m, k = a.shape
    n = b.shape[1]
    tm, tn = min(tm, m), n if tn is None else tn
    return _matmul(name, [a, b],
                   [pl.BlockSpec((tm, k), lambda i, j, s: (i, 0)), pl.BlockSpec((k, tn), lambda i, j, s: (0, j))],
                   lambda x, y: _dot(a_fn(x), y, 1, 0), (m // tm, n // tn, 1), _sds((m, n), out_dtype),
                   pl.BlockSpec((tm, tn), lambda i, j, s: (i, j)), (tm, tn))


def _mm_nt(name, a, b, out_dtype, tm=512, tn=None):
    m, k = a.shape
    n = b.shape[0]
    tm, tn = min(tm, m), n if tn is None else tn
    return _matmul(name, [a, b],
                   [pl.BlockSpec((tm, k), lambda i, j, s: (i, 0)), pl.BlockSpec((tn, k), lambda i, j, s: (j, 0))],
                   lambda x, y: _dot(x, y, 1, 1), (m // tm, n // tn, 1), _sds((m, n), out_dtype),
                   pl.BlockSpec((tm, tn), lambda i, j, s: (i, j)), (tm, tn))


def _mm_tn(name, a, b, out_dtype, tm=512, tn=None, tk=2048, a_fn=lambda x: x, after=()):
    k, m = a.shape
    n = b.shape[1]
    tm, tk, tn = min(tm, m), min(tk, k), n if tn is None else tn
    return _matmul(name, [a, b],
                   [pl.BlockSpec((tk, tm), lambda i, j, s: (s, i)), pl.BlockSpec((tk, tn), lambda i, j, s: (s, j))],
                   lambda x, y: _dot(a_fn(x), y, 0, 0), (m // tm, n // tn, k // tk), _sds((m, n), out_dtype),
                   pl.BlockSpec((tm, tn), lambda i, j, s: (i, j)), (tm, tn), after)


def _mm_contract_slots(name, pairs, out_dtype, per_step, tm=512, tn=2048):
    s_, m, k = pairs[0][0].shape
    n = pairs[0][1].shape[2]
    tm, tn = min(tm, m), min(tn, n)
    ops, specs = [], []
    for a, b in pairs:
        ops += [a, b]
        specs += [pl.BlockSpec((per_step, tm, k), lambda i, j, s: (s, i, 0)),
                  pl.BlockSpec((per_step, k, tn), lambda i, j, s: (s, 0, j))]

    def product(*t):
        return sum(_dot(t[2 * p][q], t[2 * p + 1][q], 1, 0) for p in range(len(pairs)) for q in range(per_step))

    return _matmul(name, ops, specs, product, (m // tm, n // tn, s_ // per_step), _sds((m, n), out_dtype),
                   pl.BlockSpec((tm, tn), lambda i, j, s: (i, j)), (tm, tn))


def _mm_slots_tn(name, a, b, out_dtype, tn=2048, tk=2048):
    s_, k, m = a.shape
    n = b.shape[1]
    tn, tk = min(tn, n), min(tk, k)
    return _matmul(name, [a, b],
                   [pl.BlockSpec((None, tk, m), lambda s, j, z: (s, z, 0)), pl.BlockSpec((tk, tn), lambda s, j, z: (z, j))],
                   lambda x, y: _dot(x, y, 0, 0), (s_, n // tn, k // tk), _sds((s_, m, n), out_dtype),
                   pl.BlockSpec((None, m, tn), lambda s, j, z: (s, 0, j)), (m, tn))


def _ffn_in(a, w_gate, w_up, tm=512):
    m, k = a.shape
    s_, n, _ = w_gate.shape
    tm = min(tm, m)

    def body(a_ref, wg_ref, wu_ref, g_ref, u_ref, h_ref):
        x = a_ref[...]
        g = _dot(x, wg_ref[...], 1, 1)
        u = _dot(x, wu_ref[...], 1, 1)
        g_ref[...] = g.astype(BF16)
        u_ref[...] = u.astype(BF16)
        h_ref[...] = (g * _sigmoid(g) * u).astype(BF16)

    w_spec = pl.BlockSpec((None, n, k), lambda s, i: (s, 0, 0))
    o_spec = pl.BlockSpec((None, tm, n), lambda s, i: (s, i, 0))
    return _call(body, name="ffn_in", grid=(s_, m // tm),
                 in_specs=[pl.BlockSpec((tm, k), lambda s, i: (i, 0)), w_spec, w_spec], out_specs=[o_spec] * 3,
                 out_shape=[_sds((s_, m, n), BF16)] * 3, semantics=("parallel", "parallel"))(a, w_gate, w_up)


def _ffn_down_bwd(d_out, w_down, gate, up, after, tm=512):
    m, k = d_out.shape
    s_, n, _ = w_down.shape
    tm = min(tm, m)

    def body(d_ref, w_ref, g_ref, u_ref, dg_ref, du_ref):
        dh = _dot(d_ref[...], w_ref[...], 1, 1)
        g = g_ref[...].astype(F32)
        sg = _sigmoid(g)
        dg_ref[...] = (dh * u_ref[...].astype(F32) * sg * (1.0 + g * (1.0 - sg))).astype(BF16)
        du_ref[...] = (dh * g * sg).astype(BF16)

    t_spec = pl.BlockSpec((None, tm, n), lambda s, i: (s, i, 0))
    return _call(body, name="ffn_down_dx", grid=(s_, m // tm),
                 in_specs=[pl.BlockSpec((tm, k), lambda s, i: (i, 0)), pl.BlockSpec((None, n, k), lambda s, i: (s, 0, 0)),
                           t_spec, t_spec],
                 out_specs=[t_spec] * 2, out_shape=[_sds((s_, m, n), BF16)] * 2, semantics=("parallel", "parallel"),
                 n_after=len(after))(d_out, w_down, gate, up, *after)


ALL_PEERS = (1, 2, 3, 4, 5, 6, 7)
CHIP_PEERS = (2, 4, 6)
SIBLING = 1


def _peer(relation):
    x, y, c = lax.axis_index("x"), lax.axis_index("y"), lax.axis_index("c")
    pos = (1 - x if relation & 4 else x, 1 - y if relation & 2 else y, 1 - c if relation & 1 else c)
    return pos, 4 * pos[0] + 2 * pos[1] + pos[2]


def _exchange_copies(ins, lands, send_sems, recv_sems, scatter, relations):
    _, me = _peer(0)

    def copy(a, s, peer, pos, dst_slot):
        return pltpu.make_async_remote_copy(
            src_ref=ins[a].at[peer] if scatter else ins[a], dst_ref=lands[a].at[dst_slot],
            send_sem=send_sems.at[s], recv_sem=recv_sems.at[s], device_id=pos, device_id_type=pl.DeviceIdType.MESH)

    pairs = []
    for k, r in enumerate(relations):
        pos, peer = _peer(r)
        for a in range(len(ins)):
            s = a * len(relations) + k
            pairs.append((copy(a, s, peer, pos, me), copy(a, s, peer, pos, peer)))
    return me, pairs


def _forward_copies(lands, send_sems, recv_sems):
    sibling, _ = _peer(SIBLING)

    def copy(a, s, slot):
        return pltpu.make_async_remote_copy(
            src_ref=lands[a].at[slot], dst_ref=lands[a].at[slot], send_sem=send_sems.at[s], recv_sem=recv_sems.at[s],
            device_id=sibling, device_id_type=pl.DeviceIdType.MESH)

    pairs = []
    for k, r in enumerate(CHIP_PEERS):
        _, mine = _peer(r)
        _, theirs = _peer(r | SIBLING)
        for a in range(len(lands)):
            s = a * len(CHIP_PEERS) + k
            pairs.append((copy(a, s, mine), copy(a, s, theirs)))
    return pairs


_HBM_SPEC = pl.BlockSpec(memory_space=pltpu.HBM)
_SEM_SPEC = pl.BlockSpec(memory_space=pltpu.SEMAPHORE)
_SIDE_EFFECT = pltpu.SideEffectType.DATAFLOW_SIDE_EFFECTING


def _split_start(name, operands, n_sem, make_pairs):
    k = len(operands)

    def body(*refs):
        send_sems, recv_sems, token = refs[k], refs[k + 1], refs[-1]
        for send, _ in make_pairs(refs[:k], send_sems, recv_sems):
            send.start()
        token[...] = jnp.zeros_like(token)

    out = pl.pallas_call(
        body, name=name,
        out_shape=(pltpu.SemaphoreType.DMA((n_sem,)), pltpu.SemaphoreType.DMA((n_sem,)),
                   *[pltpu.HBM(a.shape, a.dtype) for a in operands], _sds((SUBLANES, LANES), F32)),
        in_specs=[_HBM_SPEC] * k,
        out_specs=(_SEM_SPEC, _SEM_SPEC, *[_HBM_SPEC] * k, pl.BlockSpec(memory_space=pltpu.VMEM)),
        input_output_aliases={i: 2 + i for i in range(k)},
        compiler_params=pltpu.CompilerParams(has_side_effects=_SIDE_EFFECT),
    )(*[pltpu.with_memory_space_constraint(a, pltpu.HBM) for a in operands])
    return dict(name=name, sems=out[:2], thru=list(out[2:2 + k]), make_pairs=make_pairs), out[-1]


def _split_wait(handle, after):
    thru, make_pairs = handle["thru"], handle["make_pairs"]
    k = len(thru)

    def body(*refs):
        for send, arrival in make_pairs(refs[:k], refs[k], refs[k + 1]):
            send.wait_send()
            arrival.wait_recv()

    return pl.pallas_call(
        body, name=handle["name"] + "_wait", out_shape=[pltpu.HBM(a.shape, a.dtype) for a in thru],
        in_specs=[_HBM_SPEC] * k + [_SEM_SPEC, _SEM_SPEC] + [pl.BlockSpec(memory_space=pl.ANY)] * len(after),
        out_specs=[_HBM_SPEC] * k, input_output_aliases={i: i for i in range(k)},
        compiler_params=pltpu.CompilerParams(has_side_effects=_SIDE_EFFECT),
    )(*thru, *handle["sems"], *after)


def _exchange_start(name, arrays, scatter, relations=ALL_PEERS):
    n = len(arrays)
    lands = [lax.empty(a.shape if scatter else (N_DEV,) + a.shape, a.dtype) for a in arrays]

    def make_pairs(refs, send_sems, recv_sems):
        return _exchange_copies(refs[:n], refs[n:], send_sems, recv_sems, scatter, relations)[1]

    handle, token = _split_start(name, list(arrays) + lands, n * len(relations), make_pairs)
    handle.update(n=n, scatter=scatter)
    return handle, token


def _forward_start(name, lands):
    return _split_start(name, list(lands), len(lands) * len(CHIP_PEERS), _forward_copies)


def _exchange_wait(handle, after):
    n, scatter = handle["n"], handle["scatter"]
    out = _split_wait(handle, after)
    me = 4 * lax.axis_index("x") + 2 * lax.axis_index("y") + lax.axis_index("c")
    done = []
    for src, land in zip(out[:n], out[n:]):
        own = lax.dynamic_index_in_dim(src, me, 0, keepdims=True) if scatter else src[None]
        done.append(lax.dynamic_update_slice_in_dim(land, own, me, 0))
    return done


def _rope_tables(pos_col):
    t = pos_col.shape[0]
    half = HEAD_DIM // 2
    inv_freq = ROPE_THETA ** (-jnp.arange(half, dtype=F32) / half)
    inv_row = jnp.tile(inv_freq, LANES // half)[None, :]

    def body(pos_ref, inv_ref, cos_ref, sin_ref):
        ang = pos_ref[...] * inv_ref[...]
        cos_ref[...] = jnp.cos(ang)
        sin_ref[...] = jnp.sin(ang)

    tm = min(t, 512)
    return _call(body, name="rope_tables", grid=(t // tm,),
                 in_specs=[pl.BlockSpec((tm, 1), lambda i: (i, 0)), pl.BlockSpec((1, LANES), lambda i: (0, 0))],
                 out_specs=[pl.BlockSpec((tm, LANES), lambda i: (i, 0))] * 2,
                 out_shape=[_sds((t, LANES), F32)] * 2, semantics=("parallel",))(pos_col, inv_row)


def _rot_half(x):
    lane = lax.broadcasted_iota(jnp.int32, x.shape, 1)
    low = (lane % HEAD_DIM) < HEAD_DIM // 2
    return jnp.where(low, -pltpu.roll(x, LANES - HEAD_DIM // 2, 1), pltpu.roll(x, HEAD_DIM // 2, 1))


def _rope(x, cos, sin):
    return x * cos + _rot_half(x) * sin


def _unrope(d, cos, sin):
    return d * cos - _rot_half(d) * sin


def _band_mask(first_block, heads):
    r = lax.broadcasted_iota(jnp.int32, (heads * BLOCK, 2 * BLOCK), 0) % BLOCK
    c = lax.broadcasted_iota(jnp.int32, (heads * BLOCK, 2 * BLOCK), 1)
    diff = r - c + BLOCK
    return (diff >= 0) & (diff < WINDOW) & ((c >= BLOCK) | jnp.logical_not(first_block))


def _attn_specs(t, d_attn, d_in):
    kb, vb = d_attn // D_KV, d_attn // D_KV + 1
    prev = lambda i: jnp.maximum(i - 1, 0)
    return [
        pl.BlockSpec((BLOCK, d_attn), lambda i: (i, 0)),
        pl.BlockSpec((BLOCK, D_KV), lambda i: (i, kb)),
        pl.BlockSpec((BLOCK, D_KV), lambda i: (i, vb)),
        pl.BlockSpec((BLOCK, D_KV), lambda i: (prev(i), kb)),
        pl.BlockSpec((BLOCK, D_KV), lambda i: (prev(i), vb)),
        pl.BlockSpec((BLOCK, LANES), lambda i: (i, 0)),
        pl.BlockSpec((BLOCK, LANES), lambda i: (i, 0)),
        pl.BlockSpec((BLOCK, LANES), lambda i: (prev(i), 0)),
        pl.BlockSpec((BLOCK, LANES), lambda i: (prev(i), 0)),
        pl.BlockSpec((1, LANES), lambda i: (0, 0)),
    ]


def _head(x, h):
    return x[:, h * HEAD_DIM:(h + 1) * HEAD_DIM]


def _attn_heads(q_ref, kc_ref, vc_ref, kp_ref, vp_ref, cq_ref, sq_ref, cp_ref, sp_ref, d_attn):
    cq, sq, cp, sp = cq_ref[...], sq_ref[...], cp_ref[...], sp_ref[...]
    q_rot = [_rope(q_ref[:, j * LANES:(j + 1) * LANES], cq, sq) for j in range(d_attn // LANES)]
    kc_rot = [_rope(kc_ref[:, j * LANES:(j + 1) * LANES], cq, sq) for j in range(D_KV // LANES)]
    kp_rot = [_rope(kp_ref[:, j * LANES:(j + 1) * LANES], cp, sp) for j in range(D_KV // LANES)]
    per = LANES // HEAD_DIM
    q_heads = [_head(q_rot[h // per], h % per).astype(BF16) for h in range(d_attn // HEAD_DIM)]
    kk = [jnp.concatenate([_head(kp_rot[g // per], g % per), _head(kc_rot[g // per], g % per)], axis=0).astype(BF16)
          for g in range(N_KV_HEADS)]
    vv = [jnp.concatenate([_head(vp_ref[...], g), _head(vc_ref[...], g)], axis=0).astype(BF16) for g in range(N_KV_HEADS)]
    return q_heads, kk, vv


def _stack_group(q_heads, sink_ref, group):
    q_all = jnp.concatenate([q_heads[h] for h in group], axis=0)
    sink_all = jnp.concatenate([jnp.broadcast_to(sink_ref[:, h:h + 1], (BLOCK, 1)) for h in group], axis=0)
    return q_all, sink_all


def _softmax_with_sink(q, kk, sink, mask):
    s = _dot(q, kk, 1, 1) * (1.0 / math.sqrt(HEAD_DIM))
    s = jnp.where(mask, s, MASKED)
    m = jnp.maximum(jnp.max(s, axis=-1, keepdims=True), sink)
    p = jnp.exp(s - m)
    e_sink = jnp.exp(sink - m)
    inv = 1.0 / (jnp.sum(p, axis=-1, keepdims=True) + e_sink)
    return p * inv, e_sink * inv


def _attention_fwd(proj, cos, sin, sinks_row, d_attn):
    t, d_in = proj.shape
    n_heads = d_attn // HEAD_DIM
    q_per_kv = n_heads // N_KV_HEADS

    def body(q_ref, kc_ref, vc_ref, kp_ref, vp_ref, cq_ref, sq_ref, cp_ref, sp_ref, sink_ref, o_ref):
        mask = _band_mask(pl.program_id(0) == 0, q_per_kv)
        q_heads, kk, vv = _attn_heads(q_ref, kc_ref, vc_ref, kp_ref, vp_ref, cq_ref, sq_ref, cp_ref, sp_ref, d_attn)
        for g in range(N_KV_HEADS):
            group = range(g * q_per_kv, (g + 1) * q_per_kv)
            q_all, sink_all = _stack_group(q_heads, sink_ref, group)
            probs, _ = _softmax_with_sink(q_all, kk[g], sink_all, mask)
            o_all = _dot(probs.astype(BF16), vv[g], 1, 0)
            for k, h in enumerate(group):
                o_ref[:, h * HEAD_DIM:(h + 1) * HEAD_DIM] = o_all[k * BLOCK:(k + 1) * BLOCK]

    return _call(body, name="attention_fwd", grid=(t // BLOCK,), in_specs=_attn_specs(t, d_attn, d_in),
                 out_specs=pl.BlockSpec((BLOCK, d_attn), lambda i: (i, 0)), out_shape=_sds((t, d_attn), F32),
                 semantics=("parallel",))(proj, proj, proj, proj, proj, cos, sin, cos, sin, sinks_row)


def _attention_bwd(proj, cos, sin, sinks_row, d_out, d_attn):
    t, d_in = proj.shape
    n_heads = d_attn // HEAD_DIM
    q_per_kv = n_heads // N_KV_HEADS
    nb = t // BLOCK
    per = LANES // HEAD_DIM

    def body(q_ref, kc_ref, vc_ref, kp_ref, vp_ref, cq_ref, sq_ref, cp_ref, sp_ref, sink_ref, do_ref,
             dq_ref, dk_ref, dv_ref, dsink_ref):
        i = pl.program_id(0)
        mask = _band_mask(i == 0, q_per_kv)
        q_heads, kk, vv = _attn_heads(q_ref, kc_ref, vc_ref, kp_ref, vp_ref, cq_ref, sq_ref, cp_ref, sp_ref, d_attn)
        lane = lax.broadcasted_iota(jnp.int32, (1, LANES), 1)
        dsink = jnp.zeros((1, LANES), F32)
        dq_rot, dkk, dvv = [], [], []
        for g in range(N_KV_HEADS):
            group = range(g * q_per_kv, (g + 1) * q_per_kv)
            q_all, sink_all = _stack_group(q_heads, sink_ref, group)
            probs, p_sink = _softmax_with_sink(q_all, kk[g], sink_all, mask)
            do_all = jnp.concatenate([do_ref[:, h * HEAD_DIM:(h + 1) * HEAD_DIM] for h in group], axis=0).astype(BF16)
            dp = _dot(do_all, vv[g], 1, 1)
            delta = jnp.sum(probs * dp, axis=-1, keepdims=True)
            ds = (probs * (dp - delta) * (1.0 / math.sqrt(HEAD_DIM))).astype(BF16)
            dq_all = _dot(ds, kk[g], 1, 0)
            dkk.append(_dot(ds, q_all, 0, 0))
            dvv.append(_dot(probs.astype(BF16), do_all, 0, 0))
            sink_term = p_sink * delta
            for k, h in enumerate(group):
                dq_rot.append(dq_all[k * BLOCK:(k + 1) * BLOCK])
                part = jnp.sum(sink_term[k * BLOCK:(k + 1) * BLOCK], axis=0, keepdims=True)
                dsink += jnp.where(lane == h, -part, 0.0)
        cq, sq, cp, sp = cq_ref[...], sq_ref[...], cp_ref[...], sp_ref[...]
        for j in range(d_attn // LANES):
            d = jnp.concatenate(dq_rot[j * per:(j + 1) * per], axis=1)
            dq_ref[:, j * LANES:(j + 1) * LANES] = _unrope(d, cq, sq)
        for j in range(D_KV // LANES):
            d = jnp.concatenate(dkk[j * per:(j + 1) * per], axis=1)
            dk_ref[0, :, j * LANES:(j + 1) * LANES] = _unrope(d[:BLOCK], cp, sp)
            dk_ref[1, :, j * LANES:(j + 1) * LANES] = _unrope(d[BLOCK:], cq, sq)
            d = jnp.concatenate(dvv[j * per:(j + 1) * per], axis=1)
            dv_ref[0, :, j * LANES:(j + 1) * LANES] = d[:BLOCK]
            dv_ref[1, :, j * LANES:(j + 1) * LANES] = d[BLOCK:]

        @pl.when(i == 0)
        def _():
            dsink_ref[...] = jnp.zeros_like(dsink_ref)

        dsink_ref[...] += dsink

    pair = pl.BlockSpec((2, BLOCK, D_KV), lambda i: (i, 0, 0))
    return _call(body, name="attention_bwd", grid=(nb,),
                 in_specs=_attn_specs(t, d_attn, d_in) + [pl.BlockSpec((BLOCK, d_attn), lambda i: (i, 0))],
                 out_specs=[pl.BlockSpec((BLOCK, d_attn), lambda i: (i, 0)), pair, pair,
                            pl.BlockSpec((1, LANES), lambda i: (0, 0))],
                 out_shape=[_sds((t, d_attn), F32), _sds((2 * nb, BLOCK, D_KV), F32), _sds((2 * nb, BLOCK, D_KV), F32),
                            _sds((1, LANES), F32)],
                 semantics=("arbitrary",))(proj, proj, proj, proj, proj, cos, sin, cos, sin, sinks_row, d_out)


def _assemble_dproj(dq, dk2, dv2, du, d_in, after):
    t, d_attn = dq.shape
    d_ssm = du.shape[1]
    nb = t // BLOCK

    def body(dq_ref, dk_own, dk_next, dv_own, dv_next, du_ref, o_ref):
        has_next = (pl.program_id(0) < nb - 1).astype(F32)
        o_ref[:, :d_attn] = dq_ref[...].astype(BF16)
        o_ref[:, d_attn:d_attn + D_KV] = (dk_own[...] + has_next * dk_next[...]).astype(BF16)
        o_ref[:, d_attn + D_KV:d_attn + 2 * D_KV] = (dv_own[...] + has_next * dv_next[...]).astype(BF16)
        o_ref[:, d_attn + 2 * D_KV:] = du_ref[...].astype(BF16)

    own = pl.BlockSpec((None, BLOCK, D_KV), lambda i: (2 * i + 1, 0, 0))
    nxt = pl.BlockSpec((None, BLOCK, D_KV), lambda i: (jnp.minimum(2 * i + 2, 2 * nb - 1), 0, 0))
    return _call(body, name="assemble_dproj", grid=(nb,),
                 in_specs=[pl.BlockSpec((BLOCK, d_attn), lambda i: (i, 0)), own, nxt, own, nxt,
                           pl.BlockSpec((BLOCK, d_ssm), lambda i: (i, 0))],
                 out_specs=pl.BlockSpec((BLOCK, d_in), lambda i: (i, 0)), out_shape=_sds((t, d_in), BF16),
                 semantics=("parallel",), n_after=len(after))(dq, dk2, dk2, dv2, dv2, du, *after)


def _discretise(ar, ai, ldt, br, bi):
    dt = jnp.exp(ldt)
    mag = jnp.exp(ar * dt)
    lam_re = mag * jnp.cos(ai * dt)
    lam_im = mag * jnp.sin(ai * dt)
    den = ar * ar + ai * ai
    nr = lam_re - 1.0
    ni = lam_im
    f_re = (nr * ar + ni * ai) / den
    f_im = (ni * ar - nr * ai) / den
    return (lam_re, lam_im, [f_re * r - f_im * i for r, i in zip(br, bi)], [f_re * i + f_im * r for r, i in zip(br, bi)])


def _whole(arrays):
    return [pl.BlockSpec(a.shape, lambda *_, nd=len(a.shape): (0,) * nd) for a in arrays]


def _channels(ref):
    groups = ref.shape[0] // SSM_GROUP
    return [ref[pl.ds(p, groups, stride=SSM_GROUP), :] for p in range(SSM_GROUP)]


def _store_channels(ref, values):
    groups = ref.shape[0] // SSM_GROUP
    for p, val in enumerate(values):
        ref[pl.ds(p, groups, stride=SSM_GROUP), :] = val


def _s5_discretise(ar, ai, ldt, br, bi):
    ins = [ar, ai, ldt, br, bi]

    def body(ar_ref, ai_ref, ldt_ref, br_ref, bi_ref, lr_ref, li_ref, bbr_ref, bbi_ref):
        lr, li, bbr, bbi = _discretise(ar_ref[...], ai_ref[...], ldt_ref[...], _channels(br_ref), _channels(bi_ref))
        lr_ref[...] = lr
        li_ref[...] = li
        _store_channels(bbr_ref, bbr)
        _store_channels(bbi_ref, bbi)

    outs = [_sds(ar.shape, F32), _sds(ar.shape, F32), _sds(br.shape, F32), _sds(br.shape, F32)]
    return _call(body, name="s5_discretise", in_specs=_whole(ins), out_specs=_whole(outs), out_shape=outs)(*ins)


def _s5_discretise_bwd(ar, ai, ldt, br, bi, d_lr, d_li, d_bbr, d_bbi):
    ins = [ar, ai, ldt, br, bi, d_lr, d_li, d_bbr, d_bbi]

    def body(ar_ref, ai_ref, ldt_ref, br_ref, bi_ref, dlr_ref, dli_ref, dbbr_ref, dbbi_ref,
             dar_ref, dai_ref, dldt_ref, dbr_ref, dbi_ref):
        _, vjp = jax.vjp(_discretise, ar_ref[...], ai_ref[...], ldt_ref[...], _channels(br_ref), _channels(bi_ref))
        dar, dai, dldt, dbr, dbi = vjp((dlr_ref[...], dli_ref[...], _channels(dbbr_ref), _channels(dbbi_ref)))
        dar_ref[...] = dar
        dai_ref[...] = dai
        dldt_ref[...] = dldt
        _store_channels(dbr_ref, dbr)
        _store_channels(dbi_ref, dbi)

    outs = [_sds(a.shape, F32) for a in (ar, ai, ldt, br, bi)]
    return _call(body, name="s5_discretise_bwd", in_specs=_whole(ins), out_specs=_whole(outs), out_shape=outs)(*ins)


def _cmul(ar, ai, br, bi):
    return ar * br - ai * bi, ar * bi + ai * br


def _load_segmented(ref, tile0, n_tiles, seg):
    return jnp.concatenate([ref[pl.ds(tile0 + j, SUBLANES, stride=seg), :] for j in range(n_tiles)], axis=0)


def _store_segmented(ref, tile0, seg, value):
    for j in range(value.shape[0] // SUBLANES):
        ref[pl.ds(tile0 + j, SUBLANES, stride=seg), :] = value[j * SUBLANES:(j + 1) * SUBLANES, :]


def _fill_powers(lr, li, pr_ref, pi_ref, seg):
    pows = [(lr, li)]
    for _ in range(SUBLANES - 1):
        pows.append(_cmul(pows[-1][0], pows[-1][1], lr, li))
    row = lax.broadcasted_iota(jnp.int32, (SUBLANES, lr.shape[1]), 0)
    tr = jnp.zeros((SUBLANES, lr.shape[1]), F32)
    ti = jnp.zeros((SUBLANES, lr.shape[1]), F32)
    for r in range(SUBLANES):
        tr = jnp.where(row == r, pows[r][0], tr)
        ti = jnp.where(row == r, pows[r][1], ti)
    pr_ref[0:SUBLANES, :] = tr
    pi_ref[0:SUBLANES, :] = ti
    k = SUBLANES
    while k < seg:
        fr, fi = pr_ref[k - 1:k, :], pi_ref[k - 1:k, :]
        for t0 in range(0, k, SUBLANES):
            nr, ni = _cmul(pr_ref[t0:t0 + SUBLANES, :], pi_ref[t0:t0 + SUBLANES, :], fr, fi)
            pr_ref[k + t0:k + t0 + SUBLANES, :] = nr
            pi_ref[k + t0:k + t0 + SUBLANES, :] = ni
        k *= 2


def _scan_segments(sr_ref, si_ref, pr_ref, pi_ref, lr, li, seg, reverse, per_tile=None):
    w = lr.shape[1]
    sign = -1.0 if reverse else 1.0
    lrb = jnp.broadcast_to(lr, (SUBLANES, w))
    lib = jnp.broadcast_to(sign * li, (SUBLANES, w))
    zero = jnp.zeros((SUBLANES, w), F32)

    def tile_rows(j):
        return pl.ds(pl.multiple_of(j * SUBLANES, SUBLANES), SUBLANES)

    def local(i, carry):
        rows = tile_rows(seg - 1 - i if reverse else i)
        pr, pi = _cmul(lrb, lib, carry[0], carry[1])
        xr, xi = sr_ref[rows, :] + pr, si_ref[rows, :] + pi
        sr_ref[rows, :] = xr
        si_ref[rows, :] = xi
        return xr, xi

    end_r, end_i = lax.fori_loop(0, seg, local, (zero, zero))
    full_r, full_i = pr_ref[seg - 1:seg, :], sign * pi_ref[seg - 1:seg, :]
    row = lax.broadcasted_iota(jnp.int32, (SUBLANES, w), 0)
    in_r, in_i = zero, zero
    cur_r, cur_i = jnp.zeros((1, w), F32), jnp.zeros((1, w), F32)
    for r in (range(SUBLANES - 2, -1, -1) if reverse else range(1, SUBLANES)):
        src = r + 1 if reverse else r - 1
        pr, pi = _cmul(full_r, full_i, cur_r, cur_i)
        cur_r, cur_i = end_r[src:src + 1, :] + pr, end_i[src:src + 1, :] + pi
        in_r = jnp.where(row == r, cur_r, in_r)
        in_i = jnp.where(row == r, cur_i, in_i)

    def carry_in(j, _):
        rows = tile_rows(j)
        k = seg - 1 - j if reverse else j
        pr, pi = _cmul(pr_ref[pl.ds(k, 1), :], sign * pi_ref[pl.ds(k, 1), :], in_r, in_i)
        xr, xi = sr_ref[rows, :] + pr, si_ref[rows, :] + pi
        sr_ref[rows, :] = xr
        si_ref[rows, :] = xi
        if per_tile is not None:
            per_tile(j, xr, xi)
        return 0

    lax.fori_loop(0, seg, carry_in, 0)


_S5_ROWS = 256


def _s5_in_specs(t, d_attn):
    u_block = (d_attn + 2 * D_KV) // SSM_CH_BLOCK
    blk3 = lambda shape: pl.BlockSpec((None,) + shape, lambda j: (j, 0, 0))
    return [
        pl.BlockSpec((t, SSM_CH_BLOCK), lambda j: (0, u_block + j)),
        blk3((SSM_CH_BLOCK, SSM_ST_BLOCK)), blk3((SSM_CH_BLOCK, SSM_ST_BLOCK)),
        blk3((1, SSM_ST_BLOCK)), blk3((1, SSM_ST_BLOCK)),
        blk3((SSM_ST_BLOCK, SSM_CH_BLOCK)), blk3((SSM_ST_BLOCK, SSM_CH_BLOCK)),
        pl.BlockSpec((1, SSM_CH_BLOCK), lambda j: (0, j)),
    ]


def _chunks(t):
    rows = min(_S5_ROWS, t)
    return rows, lambda i: pl.ds(pl.multiple_of(i * rows, rows), rows)


def _s5_states(u_ref, us_ref, bre_ref, bim_ref, lr_ref, li_ref, sr_ref, si_ref, pr_ref, pi_ref, t):
    seg = t // SUBLANES
    rows, chunk = _chunks(t)
    for c in range(t // rows):
        us_ref[c * rows:(c + 1) * rows, :] = _load_segmented(u_ref, c * rows // SUBLANES, rows // SUBLANES, seg)

    def fill(i, _):
        ub = us_ref[chunk(i), :].astype(BF16)
        sr_ref[chunk(i), :] = _dot(ub, bre_ref[...], 1, 0)
        si_ref[chunk(i), :] = _dot(ub, bim_ref[...], 1, 0)
        return 0

    lax.fori_loop(0, t // rows, fill, 0)
    _fill_powers(lr_ref[...], li_ref[...], pr_ref, pi_ref, seg)
    _scan_segments(sr_ref, si_ref, pr_ref, pi_ref, lr_ref[...], li_ref[...], seg, False)


def _s5_scratch(t):
    state = pltpu.VMEM((t, SSM_ST_BLOCK), F32)
    powers = pltpu.VMEM((t // SUBLANES, SSM_ST_BLOCK), F32)
    return state, powers, pltpu.VMEM((t, SSM_CH_BLOCK), F32)


def _s5_fwd(proj, mats, dskip_row, d_attn, d_ssm):
    t = proj.shape[0]
    seg = t // SUBLANES
    n_blocks = d_ssm // SSM_CH_BLOCK
    rows, chunk = _chunks(t)

    def body(u_ref, bre_ref, bim_ref, lr_ref, li_ref, cre_ref, cim_ref, d_ref, y_ref,
             sr_ref, si_ref, pr_ref, pi_ref, us_ref, ys_ref):
        _s5_states(u_ref, us_ref, bre_ref, bim_ref, lr_ref, li_ref, sr_ref, si_ref, pr_ref, pi_ref, t)

        def emit(i, _):
            ys_ref[chunk(i), :] = (_dot(sr_ref[chunk(i), :].astype(BF16), cre_ref[...], 1, 0)
                                   - _dot(si_ref[chunk(i), :].astype(BF16), cim_ref[...], 1, 0)
                                   + d_ref[...] * us_ref[chunk(i), :])
            return 0

        lax.fori_loop(0, t // rows, emit, 0)
        for c in range(t // rows):
            _store_segmented(y_ref, c * rows // SUBLANES, seg, ys_ref[c * rows:(c + 1) * rows, :])

    state, powers, channels = _s5_scratch(t)
    col = pl.BlockSpec((t, SSM_CH_BLOCK), lambda j: (0, j))
    return _call(body, name="s5_fwd", grid=(n_blocks,), in_specs=_s5_in_specs(t, d_attn), out_specs=col,
                 out_shape=_sds((t, d_ssm), F32), scratch_shapes=[state, state, powers, powers, channels, channels],
                 semantics=("parallel",))(proj, *mats, dskip_row)


def _s5_bwd(proj, mats, dskip_row, y, dz_a, dz_b, d_attn, d_ssm):
    t = proj.shape[0]
    seg = t // SUBLANES
    n_blocks = d_ssm // SSM_CH_BLOCK
    rows, chunk = _chunks(t)

    def body(u_ref, bre_ref, bim_ref, lr_ref, li_ref, cre_ref, cim_ref, d_ref, y_ref, dza_ref, dzb_ref,
             du_ref, dbre_ref, dbim_ref, dlr_ref, dli_ref, dcre_ref, dcim_ref, dd_ref,
             sr_ref, si_ref, gr_ref, gi_ref, pr_ref, pi_ref, us_ref, dys_ref, dus_ref, acc_r, acc_i):
        _s5_states(u_ref, us_ref, bre_ref, bim_ref, lr_ref, li_ref, sr_ref, si_ref, pr_ref, pi_ref, t)
        for ref in (dcre_ref, dcim_ref, dbre_ref, dbim_ref, dd_ref, acc_r, acc_i):
            ref[...] = jnp.zeros_like(ref)
        for c in range(t // rows):
            tile0, n_tiles = c * rows // SUBLANES, rows // SUBLANES
            dz = _load_segmented(dza_ref, tile0, n_tiles, seg) + _load_segmented(dzb_ref, tile0, n_tiles, seg)
            dys_ref[c * rows:(c + 1) * rows, :] = dz * _gelu_grad(_load_segmented(y_ref, tile0, n_tiles, seg))

        def through_c(i, _):
            dy = dys_ref[chunk(i), :]
            dd_ref[...] += jnp.sum(dy * us_ref[chunk(i), :], axis=0, keepdims=True)
            dyb = dy.astype(BF16)
            gr_ref[chunk(i), :] = _dot(dyb, cre_ref[...], 1, 1)
            gi_ref[chunk(i), :] = -_dot(dyb, cim_ref[...], 1, 1)
            dcre_ref[...] += _dot(sr_ref[chunk(i), :].astype(BF16), dyb, 0, 0)
            dcim_ref[...] -= _dot(si_ref[chunk(i), :].astype(BF16), dyb, 0, 0)
            return 0

        lax.fori_loop(0, t // rows, through_c, 0)

        row = lax.broadcasted_iota(jnp.int32, (SUBLANES, SSM_ST_BLOCK), 0)
        last = pl.ds((seg - 1) * SUBLANES, SUBLANES)
        wrap = [jnp.where(row == 0, 0.0, pltpu.roll(ref[last, :], 1, 0)) for ref in (sr_ref, si_ref)]

        def lambda_grad(j, g_re, g_im):
            before = pl.ds(pl.multiple_of(jnp.maximum(j - 1, 0) * SUBLANES, SUBLANES), SUBLANES)
            prev_r = jnp.where(j > 0, sr_ref[before, :], wrap[0])
            prev_i = jnp.where(j > 0, si_ref[before, :], wrap[1])
            acc_r[...] += g_re * prev_r + g_im * prev_i
            acc_i[...] += g_im * prev_r - g_re * prev_i

        _scan_segments(gr_ref, gi_ref, pr_ref, pi_ref, lr_ref[...], li_ref[...], seg, True, per_tile=lambda_grad)
        dlr_ref[...] = jnp.sum(acc_r[...], axis=0, keepdims=True)
        dli_ref[...] = jnp.sum(acc_i[...], axis=0, keepdims=True)

        def through_b(i, _):
            ub = us_ref[chunk(i), :].astype(BF16)
            grb, gib = gr_ref[chunk(i), :].astype(BF16), gi_ref[chunk(i), :].astype(BF16)
            dbre_ref[...] += _dot(ub, grb, 0, 0)
            dbim_ref[...] += _dot(ub, gib, 0, 0)
            dus_ref[chunk(i), :] = (_dot(grb, bre_ref[...], 1, 1) + _dot(gib, bim_ref[...], 1, 1)
                                    + d_ref[...] * dys_ref[chunk(i), :])
            return 0

        lax.fori_loop(0, t // rows, through_b, 0)
        for c in range(t // rows):
            _store_segmented(du_ref, c * rows // SUBLANES, seg, dus_ref[c * rows:(c + 1) * rows, :])

    col = pl.BlockSpec((t, SSM_CH_BLOCK), lambda j: (0, j))
    blk3 = lambda shape: pl.BlockSpec((None,) + shape, lambda j: (j, 0, 0))
    state, powers, channels = _s5_scratch(t)
    return _call(
        body, name="s5_bwd", grid=(n_blocks,), in_specs=_s5_in_specs(t, d_attn) + [col, col, col],
        out_specs=[col, blk3((SSM_CH_BLOCK, SSM_ST_BLOCK)), blk3((SSM_CH_BLOCK, SSM_ST_BLOCK)),
                   blk3((1, SSM_ST_BLOCK)), blk3((1, SSM_ST_BLOCK)),
                   blk3((SSM_ST_BLOCK, SSM_CH_BLOCK)), blk3((SSM_ST_BLOCK, SSM_CH_BLOCK)),
                   pl.BlockSpec((1, SSM_CH_BLOCK), lambda j: (0, j))],
        out_shape=[_sds((t, d_ssm), F32),
                   _sds((n_blocks, SSM_CH_BLOCK, SSM_ST_BLOCK), F32), _sds((n_blocks, SSM_CH_BLOCK, SSM_ST_BLOCK), F32),
                   _sds((n_blocks, 1, SSM_ST_BLOCK), F32), _sds((n_blocks, 1, SSM_ST_BLOCK), F32),
                   _sds((n_blocks, SSM_ST_BLOCK, SSM_CH_BLOCK), F32), _sds((n_blocks, SSM_ST_BLOCK, SSM_CH_BLOCK), F32),
                   _sds((1, d_ssm), F32)],
        scratch_shapes=[state, state, state, state, powers, powers, channels, channels, channels,
                        pltpu.VMEM((SUBLANES, SSM_ST_BLOCK), F32), pltpu.VMEM((SUBLANES, SSM_ST_BLOCK), F32)],
        semantics=("parallel",))(proj, *mats, dskip_row, y, dz_a, dz_b)


def _by_block(gp_n):
    return gp_n.reshape(-1, GROUPS_PER_BLOCK, SSM_GROUP, SSM_STATE)


def _block_diag_in(bbar):
    eye = jnp.eye(GROUPS_PER_BLOCK, dtype=F32)
    return jnp.einsum("jgpn,gh->jgphn", _by_block(bbar), eye).reshape(-1, SSM_CH_BLOCK, SSM_ST_BLOCK)


def _block_diag_in_t(dense):
    d5 = dense.reshape(-1, GROUPS_PER_BLOCK, SSM_GROUP, GROUPS_PER_BLOCK, SSM_STATE)
    eye = jnp.eye(GROUPS_PER_BLOCK, dtype=F32)
    return jnp.einsum("jgphn,gh->jgpn", d5, eye).reshape(-1, SSM_STATE)


def _block_diag_out(c):
    eye = jnp.eye(GROUPS_PER_BLOCK, dtype=F32)
    return jnp.einsum("jgpn,gh->jgnhp", _by_block(c), eye).reshape(-1, SSM_ST_BLOCK, SSM_CH_BLOCK)


def _block_diag_out_t(dense):
    d5 = dense.reshape(-1, GROUPS_PER_BLOCK, SSM_STATE, GROUPS_PER_BLOCK, SSM_GROUP)
    eye = jnp.eye(GROUPS_PER_BLOCK, dtype=F32)
    return jnp.einsum("jgnhp,gh->jgpn", d5, eye).reshape(-1, SSM_STATE)


def _adamw(w, g, m, v):
    m = ADAM_B1 * m + (1.0 - ADAM_B1) * g
    v = ADAM_B2 * v + (1.0 - ADAM_B2) * (g * g)
    m_hat = m / (1.0 - ADAM_B1 ** ADAM_STEP)
    v_hat = v / (1.0 - ADAM_B2 ** ADAM_STEP)
    delta = -ADAM_LR * (m_hat / (jnp.sqrt(v_hat) + ADAM_EPS) + ADAM_WD * w)
    return delta, m, v


def _adam_sharded(name, parts, w, m, v, tr):
    r, c = w.shape
    assert r % tr == 0, (name, r, tr)

    def body(p_ref, w_ref, m_ref, v_ref, g_out, d_out, m_out, v_out):
        g = p_ref[0].astype(F32)
        for i in range(1, N_DEV):
            g = g + p_ref[i].astype(F32)
        delta, m_new, v_new = _adamw(w_ref[...], g, m_ref[...], v_ref[...])
        g_out[...] = g
        d_out[...] = delta
        m_out[...] = m_new
        v_out[...] = v_new

    tile = pl.BlockSpec((tr, c), lambda i: (i, 0))
    return _call(body, name=name, grid=(r // tr,),
                 in_specs=[pl.BlockSpec((N_DEV, tr, c), lambda i: (0, i, 0)), tile, tile, tile],
                 out_specs=[tile] * 4, out_shape=[_sds((r, c), F32)] * 4, semantics=("parallel",))(parts, w, m, v)


_SMALL = ("g_pre_mix", "sinks", "a_re", "a_im", "log_dt", "b_re", "b_im", "c_re", "c_im", "d_skip", "b_glu",
          "g_attn_out", "g_ssm_out", "g_post_mix", "g_pre_ffn", "g_post_ffn")
_BIG = ("w_in", "w_glu", "w_o", "w_gate", "w_up", "w_down")
_BY_COLUMNS = ("w_in", "w_gate", "w_up")
_SSM_MATRICES = ("b_re", "b_im", "c_re", "c_im")
_ORDER = ("g_pre_mix", "w_in", "sinks", "a_re", "a_im", "log_dt", "b_re", "b_im", "c_re", "c_im", "d_skip", "w_glu",
          "b_glu", "g_attn_out", "g_ssm_out", "w_o", "g_post_mix", "g_pre_ffn", "w_gate", "w_up", "w_down",
          "g_post_ffn")


def _adam_replicated(parts, w, m, v):
    n = len(w)

    def body(*refs):
        ins, outs = refs[:4 * n], refs[4 * n:]
        for i in range(n):
            p_ref, w_ref, m_ref, v_ref = ins[i], ins[n + i], ins[2 * n + i], ins[3 * n + i]
            g = p_ref[0]
            for k in range(1, N_DEV):
                g = g + p_ref[k]
            g = g[:, :w_ref.shape[1]]
            delta, m_new, v_new = _adamw(w_ref[...], g, m_ref[...], v_ref[...])
            for o, val in zip(outs[4 * i:4 * i + 4], (g, delta, m_new, v_new)):
                o[...] = val

    ins = list(parts) + list(w) + list(m) + list(v)
    outs = [_sds(a.shape, F32) for a in w for _ in range(4)]
    flat = _call(body, name="adam_replicated", in_specs=_whole(ins), out_specs=_whole(outs), out_shape=outs)(*ins)
    return [tuple(flat[4 * i:4 * i + 4]) for i in range(n)]


def kernel(x, positions, g_pre_mix, w_in, sinks, a_re, a_im, log_dt, b_re, b_im, c_re, c_im, d_skip, w_glu, b_glu, g_attn_out, g_ssm_out, w_o, g_post_mix, g_pre_ffn, w_gate, w_up, w_down, g_post_ffn, loss_target, m_g_pre_mix, m_w_in, m_sinks, m_a_re, m_a_im, m_log_dt, m_b_re, m_b_im, m_c_re, m_c_im, m_d_skip, m_w_glu, m_b_glu, m_g_attn_out, m_g_ssm_out, m_w_o, m_g_post_mix, m_g_pre_ffn, m_w_gate, m_w_up, m_w_down, m_g_post_ffn, v_g_pre_mix, v_w_in, v_sinks, v_a_re, v_a_im, v_log_dt, v_b_re, v_b_im, v_c_re, v_c_im, v_d_skip, v_w_glu, v_b_glu, v_g_attn_out, v_g_ssm_out, v_w_o, v_g_post_mix, v_g_pre_ffn, v_w_gate, v_w_up, v_w_down, v_g_post_ffn):
    given = dict(locals())
    weights = {n: given[n] for n in _ORDER}
    mom_m = {n: given["m_" + n] for n in _ORDER}
    mom_v = {n: given["v_" + n] for n in _ORDER}

    t, d = x.shape[1], x.shape[2]
    d_attn = d // 2
    d_ssm = d - d_attn
    d_in = d_attn + 2 * D_KV + d_ssm
    n_groups = d_ssm // SSM_GROUP
    n_heads = d_attn // HEAD_DIM
    tm = min(256, t)

    x2 = x[0]
    target = loss_target[0]

    def by_rows(n, a):
        return a[0].T if n in _BY_COLUMNS else a[0]

    def start_gather(name, ns, token):
        behind = 0 if token is None else token[0, 0].astype(BF16)
        shards = [by_rows(n, weights[n]).astype(BF16) + behind for n in ns]
        return _exchange_start(name, shards, False, (SIBLING,) + CHIP_PEERS)

    def finish_gather(handle, after):
        forward, _ = _forward_start(handle["name"] + "_forward", _exchange_wait(handle, after))
        return _split_wait(forward, [])

    ag_in, token = start_gather("gather_w_in", ["w_in"], None)
    ag_mix, token = start_gather("gather_w_glu_o", ["w_glu", "w_o"], token)
    ag_ffn_in, token = start_gather("gather_w_gate_up", ["w_gate", "w_up"], token)
    ag_down, token = start_gather("gather_w_down", ["w_down"], token)

    xn, = _rows("norm_in", lambda xv, g: ([_rms(xv)[0] * g], []), [x2], [g_pre_mix], [(d, BF16)], [], tm,
                after=[token])
    win_g, = finish_gather(ag_in, [xn])
    w_in_t = win_g.reshape(d_in, d)
    proj = _mm_nt("proj_in", xn, w_in_t, F32, tn=d_in // 4 if (d_in // 4) % LANES == 0 else None)

    cos, sin = _rope_tables(positions.reshape(t, 1).astype(F32))
    sinks_row = jnp.pad(sinks, ((0, 0), (0, LANES - n_heads)))
    attn = _attention_fwd(proj, cos, sin, sinks_row, d_attn)

    def view(n, a):
        if n in ("b_re", "b_im"):
            return jnp.transpose(a[0], (0, 2, 1)).reshape(-1, SSM_STATE)
        if n in ("c_re", "c_im"):
            return a[0].reshape(-1, SSM_STATE)
        return a[0].T if n == "d_skip" else a[0] if a.ndim == 3 else a

    def unview(n, val):
        if n in ("b_re", "b_im"):
            return jnp.transpose(val.reshape(n_groups, SSM_GROUP, SSM_STATE), (0, 2, 1))[None]
        if n in ("c_re", "c_im"):
            return val.reshape(1, n_groups, SSM_GROUP, SSM_STATE)
        return val.T[None] if n == "d_skip" else val[None] if weights[n].ndim == 3 else val

    b_re_v, b_im_v = view("b_re", b_re), view("b_im", b_im)
    ldt_col = log_dt.reshape(n_groups, 1)
    lam_re, lam_im, bbar_re, bbar_im = _s5_discretise(a_re[0], a_im[0], ldt_col, b_re_v, b_im_v)
    n_blocks = n_groups // GROUPS_PER_BLOCK
    mats = [_block_diag_in(bbar_re).astype(BF16), _block_diag_in(bbar_im).astype(BF16),
            lam_re.reshape(n_blocks, 1, SSM_ST_BLOCK), lam_im.reshape(n_blocks, 1, SSM_ST_BLOCK),
            _block_diag_out(view("c_re", c_re)).astype(BF16), _block_diag_out(view("c_im", c_im)).astype(BF16)]
    dskip_row = d_skip.reshape(1, d_ssm)
    y_ssm = _s5_fwd(proj, mats, dskip_row, d_attn, d_ssm)
    gelu_bf16 = lambda yv: _gelu(yv).astype(BF16)
    wglu_g, wo_g = finish_gather(ag_mix, [attn, y_ssm])
    w_glu_full = wglu_g.reshape(d_ssm, d_ssm)
    w_o_full = wo_g.reshape(d, d)
    glu_lin = _mm_nn("glu_gate", y_ssm, w_glu_full, F32, a_fn=gelu_bf16)

    def mix_prep(av, yv, gl, bg, ga, gs):
        ssm = _gelu(yv) * _sigmoid(gl + bg)
        return [jnp.concatenate([_rms(av)[0] * ga, _rms(ssm)[0] * gs], axis=1)], []

    mixed, = _rows("mix_prep", mix_prep, [attn, y_ssm, glu_lin], [b_glu, g_attn_out, g_ssm_out], [(d, BF16)], [], tm)
    mix = _mm_nn("mix_out", mixed, w_o_full, F32, tn=d // 2 if (d // 2) % LANES == 0 else None)

    def post_mix(xv, mv, gpm, gpf):
        h = xv + _rms(mv)[0] * gpm
        return [h, _rms(h)[0] * gpf], []

    h, hn = _rows("post_mix", post_mix, [x2, mix], [g_post_mix, g_pre_ffn], [(d, F32), (d, BF16)], [], tm)
    wgate_g, wup_g = finish_gather(ag_ffn_in, [hn])
    gate, up, hid = _ffn_in(hn, wgate_g, wup_g)
    wdown_g, = finish_gather(ag_down, [hid])
    ff = _mm_contract_slots("ffn_down", [(hid, wdown_g)], F32, per_step=4)

    def head(hv, fv, tv, gpo):
        out = hv + _rms(fv)[0] * gpo
        err = out - tv
        dout = err * (1.0 / d)
        dff, dg = _rms_bwd(fv, gpo, dout)
        loss = jnp.zeros((1, LANES), F32) + 0.5 * jnp.sum(err * err) * (1.0 / d)
        return [dff, dout], [dg, loss]

    dff, dh_out, dg_post_ffn, loss_row = _rows("loss_head", head, [h, ff, target], [g_post_ffn],
                                               [(d, BF16), (d, F32)], [d, LANES], tm)

    dw_down = _mm_slots_tn("ffn_down_dw", hid, dff, BF16)
    rs_down, tok_down = _exchange_start("scatter_dw_down", [dw_down], True)
    dgate, dup = _ffn_down_bwd(dff, wdown_g, gate, up, [tok_down])
    dhn = _mm_contract_slots("ffn_in_dx", [(dgate, wgate_g), (dup, wup_g)], F32, per_step=2)
    dw_gate = _mm_slots_tn("ffn_gate_dw", dgate, hn, BF16)
    dw_up = _mm_slots_tn("ffn_up_dw", dup, hn, BF16)
    rs_ffn_in, tok_ffn_in = _exchange_start("scatter_dw_gate_up", [dw_gate, dw_up], True)

    def mid_bwd(dho, dhn_, hv, mv, gpf, gpm):
        d1, dgpf = _rms_bwd(hv, gpf, dhn_)
        dh_ = dho + d1
        dmix_, dgpm = _rms_bwd(mv, gpm, dh_)
        return [dh_, dmix_], [dgpf, dgpm]

    dh, dmix, dg_pre_ffn, dg_post_mix = _rows("mid_bwd", mid_bwd, [dh_out, dhn, h, mix], [g_pre_ffn, g_post_mix],
                                              [(d, F32), (d, BF16)], [d, d], tm, after=[tok_ffn_in])

    dmixed = _mm_nt("mix_out_dx", dmix, w_o_full, F32, tn=d // 2 if (d // 2) % LANES == 0 else None)
    dw_o = _mm_tn("mix_out_dw", mixed, dmix, BF16, tn=d // 2 if (d // 2) % LANES == 0 else None)
    rs_o, tok_o = _exchange_start("scatter_dw_o", [dw_o.reshape(N_DEV, d // N_DEV, d)], True)

    def mix_bwd(dm, av, yv, gl, bg, ga, gs):
        dattn_, dga = _rms_bwd(av, ga, dm[:, :d_attn])
        z = _gelu(yv)
        sg = _sigmoid(gl + bg)
        dssm, dgs = _rms_bwd(z * sg, gs, dm[:, d_attn:])
        dgl = dssm * z * sg * (1.0 - sg)
        return [dattn_, dssm * sg, dgl], [dga, dgs, jnp.sum(dgl, axis=0, keepdims=True)]

    dattn, dz_direct, dglu, dg_attn_out, dg_ssm_out, db_glu = _rows(
        "mix_bwd", mix_bwd, [dmixed, attn, y_ssm, glu_lin], [b_glu, g_attn_out, g_ssm_out],
        [(d_attn, F32), (d_ssm, F32), (d_ssm, BF16)], [d_attn, d_ssm, d_ssm], tm, after=[tok_o])
    dz_glu = _mm_nt("glu_gate_dx", dglu, w_glu_full, F32)
    dw_glu = _mm_tn("glu_gate_dw", y_ssm, dglu, BF16, a_fn=gelu_bf16)

    du, db_re_dense, db_im_dense, dlam_re, dlam_im, dc_re_dense, dc_im_dense, dd_skip = _s5_bwd(
        proj, mats, dskip_row, y_ssm, dz_direct, dz_glu, d_attn, d_ssm)
    da_re, da_im, dlog_dt, db_re_v, db_im_v = _s5_discretise_bwd(
        a_re[0], a_im[0], ldt_col, b_re_v, b_im_v, dlam_re.reshape(n_groups, SSM_STATE),
        dlam_im.reshape(n_groups, SSM_STATE), _block_diag_in_t(db_re_dense), _block_diag_in_t(db_im_dense))
    dq, dk2, dv2, dsinks_row = _attention_bwd(proj, cos, sin, sinks_row, dattn, d_attn)

    small_grads = {
        "sinks": dsinks_row, "a_re": da_re, "a_im": da_im, "log_dt": dlog_dt.reshape(1, n_groups),
        "b_re": db_re_v, "b_im": db_im_v, "c_re": _block_diag_out_t(dc_re_dense),
        "c_im": _block_diag_out_t(dc_im_dense), "d_skip": dd_skip.reshape(n_groups, SSM_GROUP).T, "b_glu": db_glu,
        "g_attn_out": dg_attn_out, "g_ssm_out": dg_ssm_out, "g_post_mix": dg_post_mix, "g_pre_ffn": dg_pre_ffn,
        "g_post_ffn": dg_post_ffn,
    }
    early = [n for n in _SMALL if n != "g_pre_mix"]
    ag_small, token = _exchange_start("gather_small_grads", [small_grads[n] for n in early], False)
    dproj = _assemble_dproj(dq, dk2, dv2, du, d_in, [token])

    dxn = _mm_nn("proj_in_dx", dproj, w_in_t, F32, tn=d // 2 if (d // 2) % LANES == 0 else None)

    def x_bwd(dh_, dxn_, xv, g):
        dx, dg = _rms_bwd(xv, g, dxn_)
        return [dh_ + dx], [dg]

    grad_x, dg_pre_mix = _rows("norm_in_bwd", x_bwd, [dh, dxn, x2], [g_pre_mix], [(d, F32)], [d], tm)
    ag_last, token = _exchange_start("gather_g_pre_mix_grad", [dg_pre_mix], False)
    dw_in = _mm_tn("proj_in_dw", dproj, xn, BF16, after=[token]).reshape(N_DEV, d_in // N_DEV, d)

    rs_in, token = _exchange_start("scatter_dw_in_glu", [dw_in, dw_glu.reshape(N_DEV, d_ssm // N_DEV, d_ssm)], True)

    results = {}

    def adam_big(n, parts):
        r = parts.shape[1]
        results[n] = _adam_sharded("adam_" + n, parts, by_rows(n, weights[n]), by_rows(n, mom_m[n]),
                                   by_rows(n, mom_v[n]), 64 if r % 64 == 0 else r)
        return results[n][3]

    done = [grad_x, token]
    adam_big("w_down", _exchange_wait(rs_down, done)[0])
    p_gate, p_up = _exchange_wait(rs_ffn_in, done)
    done = [adam_big("w_gate", p_gate), adam_big("w_up", p_up), results["w_down"][3]]
    done = [adam_big("w_o", _exchange_wait(rs_o, done)[0])]
    p_in, p_glu = _exchange_wait(rs_in, done)
    done = [adam_big("w_in", p_in), adam_big("w_glu", p_glu)]
    parts = dict(zip(early, _exchange_wait(ag_small, done)))
    parts["g_pre_mix"], = _exchange_wait(ag_last, done)
    for n in _SSM_MATRICES:
        results[n] = _adam_sharded("adam_" + n, parts[n], view(n, weights[n]), view(n, mom_m[n]), view(n, mom_v[n]), 64)
    rest = [n for n in _SMALL if n not in _SSM_MATRICES]
    updated = _adam_replicated([parts[n] for n in rest], [view(n, weights[n]) for n in rest],
                               [view(n, mom_m[n]) for n in rest], [view(n, mom_v[n]) for n in rest])
    results.update(zip(rest, updated))

    loss = lax.psum(loss_row[0, 0], ("x", "y", "c"))
    outs = [loss, grad_x[None]]
    for k in range(4):
        for n in _ORDER:
            val = results[n][k]
            outs.append(val.T[None] if n in _BY_COLUMNS else val[None] if n in _BIG else unview(n, val))
    return tuple(outs)
```

```python
import math

import jax
import jax.numpy as jnp
from jax import lax
from jax.experimental import pallas as pl
from jax.experimental.pallas import tpu as pltpu

F32 = jnp.float32
BF16 = jnp.bfloat16

HEAD_DIM = 64
N_KV_HEADS = 4
D_KV = N_KV_HEADS * HEAD_DIM
WINDOW = 128
BLOCK = 128
ROPE_THETA = 10000.0
SSM_GROUP = 16
SSM_STATE = 64
GROUPS_PER_BLOCK = 8
SSM_CH_BLOCK = GROUPS_PER_BLOCK * SSM_GROUP
SSM_ST_BLOCK = GROUPS_PER_BLOCK * SSM_STATE
RMS_EPS = 1e-6
N_DEV = 8
LANES = 128
SUBLANES = 8
MASKED = -1e30

ADAM_LR = 0.001
ADAM_B1 = 0.9
ADAM_B2 = 0.999
ADAM_EPS = 1e-08
ADAM_WD = 0.01
ADAM_STEP = 10

VMEM_LIMIT_BYTES = 56 * 1024 * 1024


def _call(body, *, name, out_shape, in_specs, out_specs, grid=(), scratch_shapes=(), semantics=None, n_after=0):
    params = dict(vmem_limit_bytes=VMEM_LIMIT_BYTES)
    if semantics is not None:
        params["dimension_semantics"] = semantics
    n_in = len(in_specs)
    if n_after:
        inner = body

        def body(*refs):
            inner(*refs[:n_in], *refs[n_in + n_after:])

        in_specs = list(in_specs) + [pl.BlockSpec(memory_space=pl.ANY)] * n_after
    return pl.pallas_call(body, name=name, grid=grid, in_specs=in_specs, out_specs=out_specs, out_shape=out_shape,
                          scratch_shapes=scratch_shapes, compiler_params=pltpu.CompilerParams(**params))


def _sds(shape, dtype):
    return jax.ShapeDtypeStruct(tuple(shape), dtype)


def _dot(a, b, ca, cb):
    return lax.dot_general(a, b, (((ca,), (cb,)), ((), ())), preferred_element_type=F32)


def _rms(x):
    r = lax.rsqrt(jnp.mean(x * x, axis=-1, keepdims=True) + RMS_EPS)
    return x * r, r


def _rms_bwd(x, g, dy):
    xh, r = _rms(x)
    dxh = dy * g
    dx = r * (dxh - xh * jnp.mean(dxh * xh, axis=-1, keepdims=True))
    return dx, jnp.sum(dy * xh, axis=0, keepdims=True)


def _sigmoid(x):
    return 1.0 / (1.0 + jnp.exp(-x))


_GELU_C = math.sqrt(2.0 / math.pi)
_GELU_A = 0.044715


def _gelu(y):
    t = jnp.tanh(_GELU_C * (y + _GELU_A * y * y * y))
    return 0.5 * y * (1.0 + t)


def _gelu_grad(y):
    t = jnp.tanh(_GELU_C * (y + _GELU_A * y * y * y))
    return 0.5 * (1.0 + t) + 0.5 * y * (1.0 - t * t) * _GELU_C * (1.0 + 3.0 * _GELU_A * y * y)


def _rows(name, fn, row_ins, vec_ins, row_outs, acc_widths, tm, after=()):
    rows = row_ins[0].shape[0]
    assert rows % tm == 0, (name, rows, tm)
    n_row, n_vec, n_out, n_acc = len(row_ins), len(vec_ins), len(row_outs), len(acc_widths)

    def body(*refs):
        ins = [r[...] for r in refs[:n_row + n_vec]]
        outs = refs[n_row + n_vec:n_row + n_vec + n_out]
        accs = refs[n_row + n_vec + n_out:]
        row_vals, acc_vals = fn(*ins)
        for o, v in zip(outs, row_vals):
            o[...] = v.astype(o.dtype)
        if n_acc:
            @pl.when(pl.program_id(0) == 0)
            def _():
                for a in accs:
                    a[...] = jnp.zeros_like(a)
            for a, v in zip(accs, acc_vals):
                a[...] += v

    in_specs = [pl.BlockSpec((tm, a.shape[1]), lambda i: (i, 0)) for a in row_ins]
    in_specs += [pl.BlockSpec(v.shape, lambda i: (0, 0)) for v in vec_ins]
    out_specs = [pl.BlockSpec((tm, w), lambda i: (i, 0)) for w, _ in row_outs]
    out_specs += [pl.BlockSpec((1, w), lambda i: (0, 0)) for w in acc_widths]
    out_shape = [_sds((rows, w), dt) for w, dt in row_outs] + [_sds((1, w), F32) for w in acc_widths]
    return _call(body, name=name, grid=(rows // tm,), in_specs=in_specs, out_specs=out_specs, out_shape=out_shape,
                 semantics=("arbitrary",) if n_acc else ("parallel",), n_after=len(after))(*row_ins, *vec_ins, *after)


def _matmul(name, operands, in_specs, product, grid, out_shape, out_spec, acc_shape, after=()):
    nk = grid[-1]
    n_in = len(operands)
    in_place = out_shape.dtype == F32

    def body(*refs):
        ins = [r[...] for r in refs[:n_in]]
        o_ref = refs[n_in]
        if nk == 1:
            o_ref[...] = product(*ins).astype(o_ref.dtype)
            return
        acc = o_ref if in_place else refs[n_in + 1]
        k = pl.program_id(len(grid) - 1)

        @pl.when(k == 0)
        def _():
            acc[...] = jnp.zeros_like(acc)

        acc[...] += product(*ins)

        if not in_place:
            @pl.when(k == nk - 1)
            def _():
                o_ref[...] = acc[...].astype(o_ref.dtype)

    return _call(body, name=name, grid=grid, in_specs=in_specs, out_specs=out_spec, out_shape=out_shape,
                 scratch_shapes=[] if nk == 1 or in_place else [pltpu.VMEM(acc_shape, F32)],
                 semantics=("parallel",) * (len(grid) - 1) + ("arbitrary",), n_after=len(after))(*operands, *after)


def _mm_nn(name, a, b, out_dtype, tm=512, tn=None, a_fn=lambda x: x):
    m, k = a.shape
    n = b.shape[1]
    tm, tn = min(tm, m), n if tn is None else tn
    return _matmul(name, [a, b],
                   [pl.BlockSpec((tm, k), lambda i, j, s: (i, 0)), pl.BlockSpec((k, tn), lambda i, j, s: (0, j))],
                   lambda x, y: _dot(a_fn(x), y, 1, 0), (m // tm, n // tn, 1), _sds((m, n), out_dtype),
                   pl.BlockSpec((tm, tn), lambda i, j, s: (i, j)), (tm, tn))


def _mm_nt(name, a, b, out_dtype, tm=512, tn=None):
    m, k = a.shape
    n = b.shape[0]
    tm, tn = min(tm, m), n if tn is None else tn
    return _matmul(name, [a, b],
                   [pl.BlockSpec((tm, k), lambda i, j, s: (i, 0)), pl.BlockSpec((tn, k), lambda i, j, s: (j, 0))],
                   lambda x, y: _dot(x, y, 1, 1), (m // tm, n // tn, 1), _sds((m, n), out_dtype),
                   pl.BlockSpec((tm, tn), lambda i, j, s: (i, j)), (tm, tn))


def _mm_tn(name, a, b, out_dtype, tm=512, tn=None, tk=2048, a_fn=lambda x: x, after=()):
    k, m = a.shape
    n = b.shape[1]
    tm, tk, tn = min(tm, m), min(tk, k), n if tn is None else tn
    return _matmul(name, [a, b],
                   [pl.BlockSpec((tk, tm), lambda i, j, s: (s, i)), pl.BlockSpec((tk, tn), lambda i, j, s: (s, j))],
                   lambda x, y: _dot(a_fn(x), y, 0, 0), (m // tm, n // tn, k // tk), _sds((m, n), out_dtype),
                   pl.BlockSpec((tm, tn), lambda i, j, s: (i, j)), (tm, tn), after)


def _mm_contract_slots(name, pairs, out_dtype, per_step, tm=512, tn=2048):
    s_, m, k = pairs[0][0].shape
    n = pairs[0][1].shape[2]
    tm, tn = min(tm, m), min(tn, n)
    ops, specs = [], []
    for a, b in pairs:
        ops += [a, b]
        specs += [pl.BlockSpec((per_step, tm, k), lambda i, j, s: (s, i, 0)),
                  pl.BlockSpec((per_step, k, tn), lambda i, j, s: (s, 0, j))]

    def product(*t):
        return sum(_dot(t[2 * p][q], t[2 * p + 1][q], 1, 0) for p in range(len(pairs)) for q in range(per_step))

    return _matmul(name, ops, specs, product, (m // tm, n // tn, s_ // per_step), _sds((m, n), out_dtype),
                   pl.BlockSpec((tm, tn), lambda i, j, s: (i, j)), (tm, tn))


def _mm_slots_tn(name, a, b, out_dtype, tn=2048, tk=2048):
    s_, k, m = a.shape
    n = b.shape[1]
    tn, tk = min(tn, n), min(tk, k)
    return _matmul(name, [a, b],
                   [pl.BlockSpec((None, tk, m), lambda s, j, z: (s, z, 0)), pl.BlockSpec((tk, tn), lambda s, j, z: (z, j))],
                   lambda x, y: _dot(x, y, 0, 0), (s_, n // tn, k // tk), _sds((s_, m, n), out_dtype),
                   pl.BlockSpec((None, m, tn), lambda s, j, z: (s, 0, j)), (m, tn))


def _ffn_in(a, w_gate, w_up, tm=512):
    m, k = a.shape
    s_, n, _ = w_gate.shape
    tm = min(tm, m)

    def body(a_ref, wg_ref, wu_ref, g_ref, u_ref, h_ref):
        x = a_ref[...]
        g = _dot(x, wg_ref[...], 1, 1)
        u = _dot(x, wu_ref[...], 1, 1)
        g_ref[...] = g.astype(BF16)
        u_ref[...] = u.astype(BF16)
        h_ref[...] = (g * _sigmoid(g) * u).astype(BF16)

    w_spec = pl.BlockSpec((None, n, k), lambda s, i: (s, 0, 0))
    o_spec = pl.BlockSpec((None, tm, n), lambda s, i: (s, i, 0))
    return _call(body, name="ffn_in", grid=(s_, m // tm),
                 in_specs=[pl.BlockSpec((tm, k), lambda s, i: (i, 0)), w_spec, w_spec], out_specs=[o_spec] * 3,
                 out_shape=[_sds((s_, m, n), BF16)] * 3, semantics=("parallel", "parallel"))(a, w_gate, w_up)


def _ffn_down_bwd(d_out, w_down, gate, up, after, tm=512):
    m, k = d_out.shape
    s_, n, _ = w_down.shape
    tm = min(tm, m)

    def body(d_ref, w_ref, g_ref, u_ref, dg_ref, du_ref):
        dh = _dot(d_ref[...], w_ref[...], 1, 1)
        g = g_ref[...].astype(F32)
        sg = _sigmoid(g)
        dg_ref[...] = (dh * u_ref[...].astype(F32) * sg * (1.0 + g * (1.0 - sg))).astype(BF16)
        du_ref[...] = (dh * g * sg).astype(BF16)

    t_spec = pl.BlockSpec((None, tm, n), lambda s, i: (s, i, 0))
    return _call(body, name="ffn_down_dx", grid=(s_, m // tm),
                 in_specs=[pl.BlockSpec((tm, k), lambda s, i: (i, 0)), pl.BlockSpec((None, n, k), lambda s, i: (s, 0, 0)),
                           t_spec, t_spec],
                 out_specs=[t_spec] * 2, out_shape=[_sds((s_, m, n), BF16)] * 2, semantics=("parallel", "parallel"),
                 n_after=len(after))(d_out, w_down, gate, up, *after)


ALL_PEERS = (1, 2, 3, 4, 5, 6, 7)
CHIP_PEERS = (2, 4, 6)
SIBLING = 1


def _peer(relation):
    x, y, c = lax.axis_index("x"), lax.axis_index("y"), lax.axis_index("c")
    pos = (1 - x if relation & 4 else x, 1 - y if relation & 2 else y, 1 - c if relation & 1 else c)
    return pos, 4 * pos[0] + 2 * pos[1] + pos[2]


def _exchange_copies(ins, lands, send_sems, recv_sems, scatter, relations):
    _, me = _peer(0)

    def copy(a, s, peer, pos, dst_slot):
        return pltpu.make_async_remote_copy(
            src_ref=ins[a].at[peer] if scatter else ins[a], dst_ref=lands[a].at[dst_slot],
            send_sem=send_sems.at[s], recv_sem=recv_sems.at[s], device_id=pos, device_id_type=pl.DeviceIdType.MESH)

    pairs = []
    for k, r in enumerate(relations):
        pos, peer = _peer(r)
        for a in range(len(ins)):
            s = a * len(relations) + k
            pairs.append((copy(a, s, peer, pos, me), copy(a, s, peer, pos, peer)))
    return me, pairs


def _forward_copies(lands, send_sems, recv_sems):
    sibling, _ = _peer(SIBLING)

    def copy(a, s, slot):
        return pltpu.make_async_remote_copy(
            src_ref=lands[a].at[slot], dst_ref=lands[a].at[slot], send_sem=send_sems.at[s], recv_sem=recv_sems.at[s],
            device_id=sibling, device_id_type=pl.DeviceIdType.MESH)

    pairs = []
    for k, r in enumerate(CHIP_PEERS):
        _, mine = _peer(r)
        _, theirs = _peer(r | SIBLING)
        for a in range(len(lands)):
            s = a * len(CHIP_PEERS) + k
            pairs.append((copy(a, s, mine), copy(a, s, theirs)))
    return pairs


_HBM_SPEC = pl.BlockSpec(memory_space=pltpu.HBM)
_SEM_SPEC = pl.BlockSpec(memory_space=pltpu.SEMAPHORE)
_SIDE_EFFECT = pltpu.SideEffectType.DATAFLOW_SIDE_EFFECTING


def _split_start(name, operands, n_sem, make_pairs):
    k = len(operands)

    def body(*refs):
        send_sems, recv_sems, token = refs[k], refs[k + 1], refs[-1]
        for send, _ in make_pairs(refs[:k], send_sems, recv_sems):
            send.start()
        token[...] = jnp.zeros_like(token)

    out = pl.pallas_call(
        body, name=name,
        out_shape=(pltpu.SemaphoreType.DMA((n_sem,)), pltpu.SemaphoreType.DMA((n_sem,)),
                   *[pltpu.HBM(a.shape, a.dtype) for a in operands], _sds((SUBLANES, LANES), F32)),
        in_specs=[_HBM_SPEC] * k,
        out_specs=(_SEM_SPEC, _SEM_SPEC, *[_HBM_SPEC] * k, pl.BlockSpec(memory_space=pltpu.VMEM)),
        input_output_aliases={i: 2 + i for i in range(k)},
        compiler_params=pltpu.CompilerParams(has_side_effects=_SIDE_EFFECT),
    )(*[pltpu.with_memory_space_constraint(a, pltpu.HBM) for a in operands])
    return dict(name=name, sems=out[:2], thru=list(out[2:2 + k]), make_pairs=make_pairs), out[-1]


def _split_wait(handle, after):
    thru, make_pairs = handle["thru"], handle["make_pairs"]
    k = len(thru)

    def body(*refs):
        for send, arrival in make_pairs(refs[:k], refs[k], refs[k + 1]):
            send.wait_send()
            arrival.wait_recv()

    return pl.pallas_call(
        body, name=handle["name"] + "_wait", out_shape=[pltpu.HBM(a.shape, a.dtype) for a in thru],
        in_specs=[_HBM_SPEC] * k + [_SEM_SPEC, _SEM_SPEC] + [pl.BlockSpec(memory_space=pl.ANY)] * len(after),
        out_specs=[_HBM_SPEC] * k, input_output_aliases={i: i for i in range(k)},
        compiler_params=pltpu.CompilerParams(has_side_effects=_SIDE_EFFECT),
    )(*thru, *handle["sems"], *after)


def _exchange_start(name, arrays, scatter, relations=ALL_PEERS):
    n = len(arrays)
    lands = [lax.empty(a.shape if scatter else (N_DEV,) + a.shape, a.dtype) for a in arrays]

    def make_pairs(refs, send_sems, recv_sems):
        return _exchange_copies(refs[:n], refs[n:], send_sems, recv_sems, scatter, relations)[1]

    handle, token = _split_start(name, list(arrays) + lands, n * len(relations), make_pairs)
    handle.update(n=n, scatter=scatter)
    return handle, token


def _forward_start(name, lands):
    return _split_start(name, list(lands), len(lands) * len(CHIP_PEERS), _forward_copies)


def _exchange_wait(handle, after):
    n, scatter = handle["n"], handle["scatter"]
    out = _split_wait(handle, after)
    me = 4 * lax.axis_index("x") + 2 * lax.axis_index("y") + lax.axis_index("c")
    done = []
    for src, land in zip(out[:n], out[n:]):
        own = lax.dynamic_index_in_dim(src, me, 0, keepdims=True) if scatter else src[None]
        done.append(lax.dynamic_update_slice_in_dim(land, own, me, 0))
    return done


def _rope_tables(pos_col):
    t = pos_col.shape[0]
    half = HEAD_DIM // 2
    inv_freq = ROPE_THETA ** (-jnp.arange(half, dtype=F32) / half)
    inv_row = jnp.tile(inv_freq, LANES // half)[None, :]

    def body(pos_ref, inv_ref, cos_ref, sin_ref):
        ang = pos_ref[...] * inv_ref[...]
        cos_ref[...] = jnp.cos(ang)
        sin_ref[...] = jnp.sin(ang)

    tm = min(t, 512)
    return _call(body, name="rope_tables", grid=(t // tm,),
                 in_specs=[pl.BlockSpec((tm, 1), lambda i: (i, 0)), pl.BlockSpec((1, LANES), lambda i: (0, 0))],
                 out_specs=[pl.BlockSpec((tm, LANES), lambda i: (i, 0))] * 2,
                 out_shape=[_sds((t, LANES), F32)] * 2, semantics=("parallel",))(pos_col, inv_row)


def _rot_half(x):
    lane = lax.broadcasted_iota(jnp.int32, x.shape, 1)
    low = (lane % HEAD_DIM) < HEAD_DIM // 2
    return jnp.where(low, -pltpu.roll(x, LANES - HEAD_DIM // 2, 1), pltpu.roll(x, HEAD_DIM // 2, 1))


def _rope(x, cos, sin):
    return x * cos + _rot_half(x) * sin


def _unrope(d, cos, sin):
    return d * cos - _rot_half(d) * sin


def _band_mask(first_block, heads):
    r = lax.broadcasted_iota(jnp.int32, (heads * BLOCK, 2 * BLOCK), 0) % BLOCK
    c = lax.broadcasted_iota(jnp.int32, (heads * BLOCK, 2 * BLOCK), 1)
    diff = r - c + BLOCK
    return (diff >= 0) & (diff < WINDOW) & ((c >= BLOCK) | jnp.logical_not(first_block))


def _attn_specs(t, d_attn, d_in):
    kb, vb = d_attn // D_KV, d_attn // D_KV + 1
    prev = lambda i: jnp.maximum(i - 1, 0)
    return [
        pl.BlockSpec((BLOCK, d_attn), lambda i: (i, 0)),
        pl.BlockSpec((BLOCK, D_KV), lambda i: (i, kb)),
        pl.BlockSpec((BLOCK, D_KV), lambda i: (i, vb)),
        pl.BlockSpec((BLOCK, D_KV), lambda i: (prev(i), kb)),
        pl.BlockSpec((BLOCK, D_KV), lambda i: (prev(i), vb)),
        pl.BlockSpec((BLOCK, LANES), lambda i: (i, 0)),
        pl.BlockSpec((BLOCK, LANES), lambda i: (i, 0)),
        pl.BlockSpec((BLOCK, LANES), lambda i: (prev(i), 0)),
        pl.BlockSpec((BLOCK, LANES), lambda i: (prev(i), 0)),
        pl.BlockSpec((1, LANES), lambda i: (0, 0)),
    ]


def _head(x, h):
    return x[:, h * HEAD_DIM:(h + 1) * HEAD_DIM]


def _attn_heads(q_ref, kc_ref, vc_ref, kp_ref, vp_ref, cq_ref, sq_ref, cp_ref, sp_ref, d_attn):
    cq, sq, cp, sp = cq_ref[...], sq_ref[...], cp_ref[...], sp_ref[...]
    q_rot = [_rope(q_ref[:, j * LANES:(j + 1) * LANES], cq, sq) for j in range(d_attn // LANES)]
    kc_rot = [_rope(kc_ref[:, j * LANES:(j + 1) * LANES], cq, sq) for j in range(D_KV // LANES)]
    kp_rot = [_rope(kp_ref[:, j * LANES:(j + 1) * LANES], cp, sp) for j in range(D_KV // LANES)]
    per = LANES // HEAD_DIM
    q_heads = [_head(q_rot[h // per], h % per).astype(BF16) for h in range(d_attn // HEAD_DIM)]
    kk = [jnp.concatenate([_head(kp_rot[g // per], g % per), _head(kc_rot[g // per], g % per)], axis=0).astype(BF16)
          for g in range(N_KV_HEADS)]
    vv = [jnp.concatenate([_head(vp_ref[...], g), _head(vc_ref[...], g)], axis=0).astype(BF16) for g in range(N_KV_HEADS)]
    return q_heads, kk, vv


def _stack_group(q_heads, sink_ref, group):
    q_all = jnp.concatenate([q_heads[h] for h in group], axis=0)
    sink_all = jnp.concatenate([jnp.broadcast_to(sink_ref[:, h:h + 1], (BLOCK, 1)) for h in group], axis=0)
    return q_all, sink_all


def _softmax_with_sink(q, kk, sink, mask):
    s = _dot(q, kk, 1, 1) * (1.0 / math.sqrt(HEAD_DIM))
    s = jnp.where(mask, s, MASKED)
    m = jnp.maximum(jnp.max(s, axis=-1, keepdims=True), sink)
    p = jnp.exp(s - m)
    e_sink = jnp.exp(sink - m)
    inv = 1.0 / (jnp.sum(p, axis=-1, keepdims=True) + e_sink)
    return p * inv, e_sink * inv


def _attention_fwd(proj, cos, sin, sinks_row, d_attn):
    t, d_in = proj.shape
    n_heads = d_attn // HEAD_DIM
    q_per_kv = n_heads // N_KV_HEADS

    def body(q_ref, kc_ref, vc_ref, kp_ref, vp_ref, cq_ref, sq_ref, cp_ref, sp_ref, sink_ref, o_ref):
        mask = _band_mask(pl.program_id(0) == 0, q_per_kv)
        q_heads, kk, vv = _attn_heads(q_ref, kc_ref, vc_ref, kp_ref, vp_ref, cq_ref, sq_ref, cp_ref, sp_ref, d_attn)
        for g in range(N_KV_HEADS):
            group = range(g * q_per_kv, (g + 1) * q_per_kv)
            q_all, sink_all = _stack_group(q_heads, sink_ref, group)
            probs, _ = _softmax_with_sink(q_all, kk[g], sink_all, mask)
            o_all = _dot(probs.astype(BF16), vv[g], 1, 0)
            for k, h in enumerate(group):
                o_ref[:, h * HEAD_DIM:(h + 1) * HEAD_DIM] = o_all[k * BLOCK:(k + 1) * BLOCK]

    return _call(body, name="attention_fwd", grid=(t // BLOCK,), in_specs=_attn_specs(t, d_attn, d_in),
                 out_specs=pl.BlockSpec((BLOCK, d_attn), lambda i: (i, 0)), out_shape=_sds((t, d_attn), F32),
                 semantics=("parallel",))(proj, proj, proj, proj, proj, cos, sin, cos, sin, sinks_row)


def _attention_bwd(proj, cos, sin, sinks_row, d_out, d_attn):
    t, d_in = proj.shape
    n_heads = d_attn // HEAD_DIM
    q_per_kv = n_heads // N_KV_HEADS
    nb = t // BLOCK
    per = LANES // HEAD_DIM

    def body(q_ref, kc_ref, vc_ref, kp_ref, vp_ref, cq_ref, sq_ref, cp_ref, sp_ref, sink_ref, do_ref,
             dq_ref, dk_ref, dv_ref, dsink_ref):
        i = pl.program_id(0)
        mask = _band_mask(i == 0, q_per_kv)
        q_heads, kk, vv = _attn_heads(q_ref, kc_ref, vc_ref, kp_ref, vp_ref, cq_ref, sq_ref, cp_ref, sp_ref, d_attn)
        lane = lax.broadcasted_iota(jnp.int32, (1, LANES), 1)
        dsink = jnp.zeros((1, LANES), F32)
        dq_rot, dkk, dvv = [], [], []
        for g in range(N_KV_HEADS):
            group = range(g * q_per_kv, (g + 1) * q_per_kv)
            q_all, sink_all = _stack_group(q_heads, sink_ref, group)
            probs, p_sink = _softmax_with_sink(q_all, kk[g], sink_all, mask)
            do_all = jnp.concatenate([do_ref[:, h * HEAD_DIM:(h + 1) * HEAD_DIM] for h in group], axis=0).astype(BF16)
            dp = _dot(do_all, vv[g], 1, 1)
            delta = jnp.sum(probs * dp, axis=-1, keepdims=True)
            ds = (probs * (dp - delta) * (1.0 / math.sqrt(HEAD_DIM))).astype(BF16)
            dq_all = _dot(ds, kk[g], 1, 0)
            dkk.append(_dot(ds, q_all, 0, 0))
            dvv.append(_dot(probs.astype(BF16), do_all, 0, 0))
            sink_term = p_sink * delta
            for k, h in enumerate(group):
                dq_rot.append(dq_all[k * BLOCK:(k + 1) * BLOCK])
                part = jnp.sum(sink_term[k * BLOCK:(k + 1) * BLOCK], axis=0, keepdims=True)
                dsink += jnp.where(lane == h, -part, 0.0)
        cq, sq, cp, sp = cq_ref[...], sq_ref[...], cp_ref[...], sp_ref[...]
        for j in range(d_attn // LANES):
            d = jnp.concatenate(dq_rot[j * per:(j + 1) * per], axis=1)
            dq_ref[:, j * LANES:(j + 1) * LANES] = _unrope(d, cq, sq)
        for j in range(D_KV // LANES):
            d = jnp.concatenate(dkk[j * per:(j + 1) * per], axis=1)
            dk_ref[0, :, j * LANES:(j + 1) * LANES] = _unrope(d[:BLOCK], cp, sp)
            dk_ref[1, :, j * LANES:(j + 1) * LANES] = _unrope(d[BLOCK:], cq, sq)
            d = jnp.concatenate(dvv[j * per:(j + 1) * per], axis=1)
            dv_ref[0, :, j * LANES:(j + 1) * LANES] = d[:BLOCK]
            dv_ref[1, :, j * LANES:(j + 1) * LANES] = d[BLOCK:]

        @pl.when(i == 0)
        def _():
            dsink_ref[...] = jnp.zeros_like(dsink_ref)

        dsink_ref[...] += dsink

    pair = pl.BlockSpec((2, BLOCK, D_KV), lambda i: (i, 0, 0))
    return _call(body, name="attention_bwd", grid=(nb,),
                 in_specs=_attn_specs(t, d_attn, d_in) + [pl.BlockSpec((BLOCK, d_attn), lambda i: (i, 0))],
                 out_specs=[pl.BlockSpec((BLOCK, d_attn), lambda i: (i, 0)), pair, pair,
                            pl.BlockSpec((1, LANES), lambda i: (0, 0))],
                 out_shape=[_sds((t, d_attn), F32), _sds((2 * nb, BLOCK, D_KV), F32), _sds((2 * nb, BLOCK, D_KV), F32),
                            _sds((1, LANES), F32)],
                 semantics=("arbitrary",))(proj, proj, proj, proj, proj, cos, sin, cos, sin, sinks_row, d_out)


def _assemble_dproj(dq, dk2, dv2, du, d_in, after):
    t, d_attn = dq.shape
    d_ssm = du.shape[1]
    nb = t // BLOCK

    def body(dq_ref, dk_own, dk_next, dv_own, dv_next, du_ref, o_ref):
        has_next = (pl.program_id(0) < nb - 1).astype(F32)
        o_ref[:, :d_attn] = dq_ref[...].astype(BF16)
        o_ref[:, d_attn:d_attn + D_KV] = (dk_own[...] + has_next * dk_next[...]).astype(BF16)
        o_ref[:, d_attn + D_KV:d_attn + 2 * D_KV] = (dv_own[...] + has_next * dv_next[...]).astype(BF16)
        o_ref[:, d_attn + 2 * D_KV:] = du_ref[...].astype(BF16)

    own = pl.BlockSpec((None, BLOCK, D_KV), lambda i: (2 * i + 1, 0, 0))
    nxt = pl.BlockSpec((None, BLOCK, D_KV), lambda i: (jnp.minimum(2 * i + 2, 2 * nb - 1), 0, 0))
    return _call(body, name="assemble_dproj", grid=(nb,),
                 in_specs=[pl.BlockSpec((BLOCK, d_attn), lambda i: (i, 0)), own, nxt, own, nxt,
                           pl.BlockSpec((BLOCK, d_ssm), lambda i: (i, 0))],
                 out_specs=pl.BlockSpec((BLOCK, d_in), lambda i: (i, 0)), out_shape=_sds((t, d_in), BF16),
                 semantics=("parallel",), n_after=len(after))(dq, dk2, dk2, dv2, dv2, du, *after)


def _discretise(ar, ai, ldt, br, bi):
    dt = jnp.exp(ldt)
    mag = jnp.exp(ar * dt)
    lam_re = mag * jnp.cos(ai * dt)
    lam_im = mag * jnp.sin(ai * dt)
    den = ar * ar + ai * ai
    nr = lam_re - 1.0
    ni = lam_im
    f_re = (nr * ar + ni * ai) / den
    f_im = (ni * ar - nr * ai) / den
    return (lam_re, lam_im, [f_re * r - f_im * i for r, i in zip(br, bi)], [f_re * i + f_im * r for r, i in zip(br, bi)])


def _whole(arrays):
    return [pl.BlockSpec(a.shape, lambda *_, nd=len(a.shape): (0,) * nd) for a in arrays]


def _channels(ref):
    groups = ref.shape[0] // SSM_GROUP
    return [ref[pl.ds(p, groups, stride=SSM_GROUP), :] for p in range(SSM_GROUP)]


def _store_channels(ref, values):
    groups = ref.shape[0] // SSM_GROUP
    for p, val in enumerate(values):
        ref[pl.ds(p, groups, stride=SSM_GROUP), :] = val


def _s5_discretise(ar, ai, ldt, br, bi):
    ins = [ar, ai, ldt, br, bi]

    def body(ar_ref, ai_ref, ldt_ref, br_ref, bi_ref, lr_ref, li_ref, bbr_ref, bbi_ref):
        lr, li, bbr, bbi = _discretise(ar_ref[...], ai_ref[...], ldt_ref[...], _channels(br_ref), _channels(bi_ref))
        lr_ref[...] = lr
        li_ref[...] = li
        _store_channels(bbr_ref, bbr)
        _store_channels(bbi_ref, bbi)

    outs = [_sds(ar.shape, F32), _sds(ar.shape, F32), _sds(br.shape, F32), _sds(br.shape, F32)]
    return _call(body, name="s5_discretise", in_specs=_whole(ins), out_specs=_whole(outs), out_shape=outs)(*ins)


def _s5_discretise_bwd(ar, ai, ldt, br, bi, d_lr, d_li, d_bbr, d_bbi):
    ins = [ar, ai, ldt, br, bi, d_lr, d_li, d_bbr, d_bbi]

    def body(ar_ref, ai_ref, ldt_ref, br_ref, bi_ref, dlr_ref, dli_ref, dbbr_ref, dbbi_ref,
             dar_ref, dai_ref, dldt_ref, dbr_ref, dbi_ref):
        _, vjp = jax.vjp(_discretise, ar_ref[...], ai_ref[...], ldt_ref[...], _channels(br_ref), _channels(bi_ref))
        dar, dai, dldt, dbr, dbi = vjp((dlr_ref[...], dli_ref[...], _channels(dbbr_ref), _channels(dbbi_ref)))
        dar_ref[...] = dar
        dai_ref[...] = dai
        dldt_ref[...] = dldt
        _store_channels(dbr_ref, dbr)
        _store_channels(dbi_ref, dbi)

    outs = [_sds(a.shape, F32) for a in (ar, ai, ldt, br, bi)]
    return _call(body, name="s5_discretise_bwd", in_specs=_whole(ins), out_specs=_whole(outs), out_shape=outs)(*ins)


def _cmul(ar, ai, br, bi):
    return ar * br - ai * bi, ar * bi + ai * br


def _load_segmented(ref, tile0, n_tiles, seg):
    return jnp.concatenate([ref[pl.ds(tile0 + j, SUBLANES, stride=seg), :] for j in range(n_tiles)], axis=0)


def _store_segmented(ref, tile0, seg, value):
    for j in range(value.shape[0] // SUBLANES):
        ref[pl.ds(tile0 + j, SUBLANES, stride=seg), :] = value[j * SUBLANES:(j + 1) * SUBLANES, :]


def _fill_powers(lr, li, pr_ref, pi_ref, seg):
    pows = [(lr, li)]
    for _ in range(SUBLANES - 1):
        pows.append(_cmul(pows[-1][0], pows[-1][1], lr, li))
    row = lax.broadcasted_iota(jnp.int32, (SUBLANES, lr.shape[1]), 0)
    tr = jnp.zeros((SUBLANES, lr.shape[1]), F32)
    ti = jnp.zeros((SUBLANES, lr.shape[1]), F32)
    for r in range(SUBLANES):
        tr = jnp.where(row == r, pows[r][0], tr)
        ti = jnp.where(row == r, pows[r][1], ti)
    pr_ref[0:SUBLANES, :] = tr
    pi_ref[0:SUBLANES, :] = ti
    k = SUBLANES
    while k < seg:
        fr, fi = pr_ref[k - 1:k, :], pi_ref[k - 1:k, :]
        for t0 in range(0, k, SUBLANES):
            nr, ni = _cmul(pr_ref[t0:t0 + SUBLANES, :], pi_ref[t0:t0 + SUBLANES, :], fr, fi)
            pr_ref[k + t0:k + t0 + SUBLANES, :] = nr
            pi_ref[k + t0:k + t0 + SUBLANES, :] = ni
        k *= 2


def _scan_segments(sr_ref, si_ref, pr_ref, pi_ref, lr, li, seg, reverse, per_tile=None):
    w = lr.shape[1]
    sign = -1.0 if reverse else 1.0
    lrb = jnp.broadcast_to(lr, (SUBLANES, w))
    lib = jnp.broadcast_to(sign * li, (SUBLANES, w))
    zero = jnp.zeros((SUBLANES, w), F32)

    def tile_rows(j):
        return pl.ds(pl.multiple_of(j * SUBLANES, SUBLANES), SUBLANES)

    def local(i, carry):
        rows = tile_rows(seg - 1 - i if reverse else i)
        pr, pi = _cmul(lrb, lib, carry[0], carry[1])
        xr, xi = sr_ref[rows, :] + pr, si_ref[rows, :] + pi
        sr_ref[rows, :] = xr
        si_ref[rows, :] = xi
        return xr, xi

    end_r, end_i = lax.fori_loop(0, seg, local, (zero, zero))
    full_r, full_i = pr_ref[seg - 1:seg, :], sign * pi_ref[seg - 1:seg, :]
    row = lax.broadcasted_iota(jnp.int32, (SUBLANES, w), 0)
    in_r, in_i = zero, zero
    cur_r, cur_i = jnp.zeros((1, w), F32), jnp.zeros((1, w), F32)
    for r in (range(SUBLANES - 2, -1, -1) if reverse else range(1, SUBLANES)):
        src = r + 1 if reverse else r - 1
        pr, pi = _cmul(full_r, full_i, cur_r, cur_i)
        cur_r, cur_i = end_r[src:src + 1, :] + pr, end_i[src:src + 1, :] + pi
        in_r = jnp.where(row == r, cur_r, in_r)
        in_i = jnp.where(row == r, cur_i, in_i)

    def carry_in(j, _):
        rows = tile_rows(j)
        k = seg - 1 - j if reverse else j
        pr, pi = _cmul(pr_ref[pl.ds(k, 1), :], sign * pi_ref[pl.ds(k, 1), :], in_r, in_i)
        xr, xi = sr_ref[rows, :] + pr, si_ref[rows, :] + pi
        sr_ref[rows, :] = xr
        si_ref[rows, :] = xi
        if per_tile is not None:
            per_tile(j, xr, xi)
        return 0

    lax.fori_loop(0, seg, carry_in, 0)


_S5_ROWS = 256


def _s5_in_specs(t, d_attn):
    u_block = (d_attn + 2 * D_KV) // SSM_CH_BLOCK
    blk3 = lambda shape: pl.BlockSpec((None,) + shape, lambda j: (j, 0, 0))
    return [
        pl.BlockSpec((t, SSM_CH_BLOCK), lambda j: (0, u_block + j)),
        blk3((SSM_CH_BLOCK, SSM_ST_BLOCK)), blk3((SSM_CH_BLOCK, SSM_ST_BLOCK)),
        blk3((1, SSM_ST_BLOCK)), blk3((1, SSM_ST_BLOCK)),
        blk3((SSM_ST_BLOCK, SSM_CH_BLOCK)), blk3((SSM_ST_BLOCK, SSM_CH_BLOCK)),
        pl.BlockSpec((1, SSM_CH_BLOCK), lambda j: (0, j)),
    ]


def _chunks(t):
    rows = min(_S5_ROWS, t)
    return rows, lambda i: pl.ds(pl.multiple_of(i * rows, rows), rows)


def _s5_states(u_ref, us_ref, bre_ref, bim_ref, lr_ref, li_ref, sr_ref, si_ref, pr_ref, pi_ref, t):
    seg = t // SUBLANES
    rows, chunk = _chunks(t)
    for c in range(t // rows):
        us_ref[c * rows:(c + 1) * rows, :] = _load_segmented(u_ref, c * rows // SUBLANES, rows // SUBLANES, seg)

    def fill(i, _):
        ub = us_ref[chunk(i), :].astype(BF16)
        sr_ref[chunk(i), :] = _dot(ub, bre_ref[...], 1, 0)
        si_ref[chunk(i), :] = _dot(ub, bim_ref[...], 1, 0)
        return 0

    lax.fori_loop(0, t // rows, fill, 0)
    _fill_powers(lr_ref[...], li_ref[...], pr_ref, pi_ref, seg)
    _scan_segments(sr_ref, si_ref, pr_ref, pi_ref, lr_ref[...], li_ref[...], seg, False)


def _s5_scratch(t):
    state = pltpu.VMEM((t, SSM_ST_BLOCK), F32)
    powers = pltpu.VMEM((t // SUBLANES, SSM_ST_BLOCK), F32)
    return state, powers, pltpu.VMEM((t, SSM_CH_BLOCK), F32)


def _s5_fwd(proj, mats, dskip_row, d_attn, d_ssm):
    t = proj.shape[0]
    seg = t // SUBLANES
    n_blocks = d_ssm // SSM_CH_BLOCK
    rows, chunk = _chunks(t)

    def body(u_ref, bre_ref, bim_ref, lr_ref, li_ref, cre_ref, cim_ref, d_ref, y_ref,
             sr_ref, si_ref, pr_ref, pi_ref, us_ref, ys_ref):
        _s5_states(u_ref, us_ref, bre_ref, bim_ref, lr_ref, li_ref, sr_ref, si_ref, pr_ref, pi_ref, t)

        def emit(i, _):
            ys_ref[chunk(i), :] = (_dot(sr_ref[chunk(i), :].astype(BF16), cre_ref[...], 1, 0)
                                   - _dot(si_ref[chunk(i), :].astype(BF16), cim_ref[...], 1, 0)
                                   + d_ref[...] * us_ref[chunk(i), :])
            return 0

        lax.fori_loop(0, t // rows, emit, 0)
        for c in range(t // rows):
            _store_segmented(y_ref, c * rows // SUBLANES, seg, ys_ref[c * rows:(c + 1) * rows, :])

    state, powers, channels = _s5_scratch(t)
    col = pl.BlockSpec((t, SSM_CH_BLOCK), lambda j: (0, j))
    return _call(body, name="s5_fwd", grid=(n_blocks,), in_specs=_s5_in_specs(t, d_attn), out_specs=col,
                 out_shape=_sds((t, d_ssm), F32), scratch_shapes=[state, state, powers, powers, channels, channels],
                 semantics=("parallel",))(proj, *mats, dskip_row)


def _s5_bwd(proj, mats, dskip_row, y, dz_a, dz_b, d_attn, d_ssm):
    t = proj.shape[0]
    seg = t // SUBLANES
    n_blocks = d_ssm // SSM_CH_BLOCK
    rows, chunk = _chunks(t)

    def body(u_ref, bre_ref, bim_ref, lr_ref, li_ref, cre_ref, cim_ref, d_ref, y_ref, dza_ref, dzb_ref,
             du_ref, dbre_ref, dbim_ref, dlr_ref, dli_ref, dcre_ref, dcim_ref, dd_ref,
             sr_ref, si_ref, gr_ref, gi_ref, pr_ref, pi_ref, us_ref, dys_ref, dus_ref, acc_r, acc_i):
        _s5_states(u_ref, us_ref, bre_ref, bim_ref, lr_ref, li_ref, sr_ref, si_ref, pr_ref, pi_ref, t)
        for ref in (dcre_ref, dcim_ref, dbre_ref, dbim_ref, dd_ref, acc_r, acc_i):
            ref[...] = jnp.zeros_like(ref)
        for c in range(t // rows):
            tile0, n_tiles = c * rows // SUBLANES, rows // SUBLANES
            dz = _load_segmented(dza_ref, tile0, n_tiles, seg) + _load_segmented(dzb_ref, tile0, n_tiles, seg)
            dys_ref[c * rows:(c + 1) * rows, :] = dz * _gelu_grad(_load_segmented(y_ref, tile0, n_tiles, seg))

        def through_c(i, _):
            dy = dys_ref[chunk(i), :]
            dd_ref[...] += jnp.sum(dy * us_ref[chunk(i), :], axis=0, keepdims=True)
            dyb = dy.astype(BF16)
            gr_ref[chunk(i), :] = _dot(dyb, cre_ref[...], 1, 1)
            gi_ref[chunk(i), :] = -_dot(dyb, cim_ref[...], 1, 1)
            dcre_ref[...] += _dot(sr_ref[chunk(i), :].astype(BF16), dyb, 0, 0)
            dcim_ref[...] -= _dot(si_ref[chunk(i), :].astype(BF16), dyb, 0, 0)
            return 0

        lax.fori_loop(0, t // rows, through_c, 0)

        row = lax.broadcasted_iota(jnp.int32, (SUBLANES, SSM_ST_BLOCK), 0)
        last = pl.ds((seg - 1) * SUBLANES, SUBLANES)
        wrap = [jnp.where(row == 0, 0.0, pltpu.roll(ref[last, :], 1, 0)) for ref in (sr_ref, si_ref)]

        def lambda_grad(j, g_re, g_im):
            before = pl.ds(pl.multiple_of(jnp.maximum(j - 1, 0) * SUBLANES, SUBLANES), SUBLANES)
            prev_r = jnp.where(j > 0, sr_ref[before, :], wrap[0])
            prev_i = jnp.where(j > 0, si_ref[before, :], wrap[1])
            acc_r[...] += g_re * prev_r + g_im * prev_i
            acc_i[...] += g_im * prev_r - g_re * prev_i

        _scan_segments(gr_ref, gi_ref, pr_ref, pi_ref, lr_ref[...], li_ref[...], seg, True, per_tile=lambda_grad)
        dlr_ref[...] = jnp.sum(acc_r[...], axis=0, keepdims=True)
        dli_ref[...] = jnp.sum(acc_i[...], axis=0, keepdims=True)

        def through_b(i, _):
            ub = us_ref[chunk(i), :].astype(BF16)
            grb, gib = gr_ref[chunk(i), :].astype(BF16), gi_ref[chunk(i), :].astype(BF16)
            dbre_ref[...] += _dot(ub, grb, 0, 0)
            dbim_ref[...] += _dot(ub, gib, 0, 0)
            dus_ref[chunk(i), :] = (_dot(grb, bre_ref[...], 1, 1) + _dot(gib, bim_ref[...], 1, 1)
                                    + d_ref[...] * dys_ref[chunk(i), :])
            return 0

        lax.fori_loop(0, t // rows, through_b, 0)
        for c in range(t // rows):
            _store_segmented(du_ref, c * rows // SUBLANES, seg, dus_ref[c * rows:(c + 1) * rows, :])

    col = pl.BlockSpec((t, SSM_CH_BLOCK), lambda j: (0, j))
    blk3 = lambda shape: pl.BlockSpec((None,) + shape, lambda j: (j, 0, 0))
    state, powers, channels = _s5_scratch(t)
    return _call(
        body, name="s5_bwd", grid=(n_blocks,), in_specs=_s5_in_specs(t, d_attn) + [col, col, col],
        out_specs=[col, blk3((SSM_CH_BLOCK, SSM_ST_BLOCK)), blk3((SSM_CH_BLOCK, SSM_ST_BLOCK)),
                   blk3((1, SSM_ST_BLOCK)), blk3((1, SSM_ST_BLOCK)),
                   blk3((SSM_ST_BLOCK, SSM_CH_BLOCK)), blk3((SSM_ST_BLOCK, SSM_CH_BLOCK)),
                   pl.BlockSpec((1, SSM_CH_BLOCK), lambda j: (0, j))],
        out_shape=[_sds((t, d_ssm), F32),
                   _sds((n_blocks, SSM_CH_BLOCK, SSM_ST_BLOCK), F32), _sds((n_blocks, SSM_CH_BLOCK, SSM_ST_BLOCK), F32),
                   _sds((n_blocks, 1, SSM_ST_BLOCK), F32), _sds((n_blocks, 1, SSM_ST_BLOCK), F32),
                   _sds((n_blocks, SSM_ST_BLOCK, SSM_CH_BLOCK), F32), _sds((n_blocks, SSM_ST_BLOCK, SSM_CH_BLOCK), F32),
                   _sds((1, d_ssm), F32)],
        scratch_shapes=[state, state, state, state, powers, powers, channels, channels, channels,
                        pltpu.VMEM((SUBLANES, SSM_ST_BLOCK), F32), pltpu.VMEM((SUBLANES, SSM_ST_BLOCK), F32)],
        semantics=("parallel",))(proj, *mats, dskip_row, y, dz_a, dz_b)


def _by_block(gp_n):
    return gp_n.reshape(-1, GROUPS_PER_BLOCK, SSM_GROUP, SSM_STATE)


def _block_diag_in(bbar):
    eye = jnp.eye(GROUPS_PER_BLOCK, dtype=F32)
    return jnp.einsum("jgpn,gh->jgphn", _by_block(bbar), eye).reshape(-1, SSM_CH_BLOCK, SSM_ST_BLOCK)


def _block_diag_in_t(dense):
    d5 = dense.reshape(-1, GROUPS_PER_BLOCK, SSM_GROUP, GROUPS_PER_BLOCK, SSM_STATE)
    eye = jnp.eye(GROUPS_PER_BLOCK, dtype=F32)
    return jnp.einsum("jgphn,gh->jgpn", d5, eye).reshape(-1, SSM_STATE)


def _block_diag_out(c):
    eye = jnp.eye(GROUPS_PER_BLOCK, dtype=F32)
    return jnp.einsum("jgpn,gh->jgnhp", _by_block(c), eye).reshape(-1, SSM_ST_BLOCK, SSM_CH_BLOCK)


def _block_diag_out_t(dense):
    d5 = dense.reshape(-1, GROUPS_PER_BLOCK, SSM_STATE, GROUPS_PER_BLOCK, SSM_GROUP)
    eye = jnp.eye(GROUPS_PER_BLOCK, dtype=F32)
    return jnp.einsum("jgnhp,gh->jgpn", d5, eye).reshape(-1, SSM_STATE)


def _adamw(w, g, m, v):
    m = ADAM_B1 * m + (1.0 - ADAM_B1) * g
    v = ADAM_B2 * v + (1.0 - ADAM_B2) * (g * g)
    m_hat = m / (1.0 - ADAM_B1 ** ADAM_STEP)
    v_hat = v / (1.0 - ADAM_B2 ** ADAM_STEP)
    delta = -ADAM_LR * (m_hat / (jnp.sqrt(v_hat) + ADAM_EPS) + ADAM_WD * w)
    return delta, m, v


def _adam_sharded(name, parts, w, m, v, tr, row0=0):
    r, c = w.shape
    assert r % tr == 0 and row0 % tr == 0, (name, r, tr, row0)

    def body(p_ref, w_ref, m_ref, v_ref, g_out, d_out, m_out, v_out):
        g = p_ref[0].astype(F32)
        for i in range(1, N_DEV):
            g = g + p_ref[i].astype(F32)
        delta, m_new, v_new = _adamw(w_ref[...], g, m_ref[...], v_ref[...])
        g_out[...] = g
        d_out[...] = delta
        m_out[...] = m_new
        v_out[...] = v_new

    tile = pl.BlockSpec((tr, c), lambda i: (i, 0))
    return _call(body, name=name, grid=(r // tr,),
                 in_specs=[pl.BlockSpec((N_DEV, tr, c), lambda i: (0, i + row0 // tr, 0)), tile, tile, tile],
                 out_specs=[tile] * 4, out_shape=[_sds((r, c), F32)] * 4, semantics=("parallel",))(parts, w, m, v)


_BIG =("w_in", "w_glu", "w_o", "w_gate", "w_up", "w_down")
_BY_COLUMNS = ("w_in", "w_gate", "w_up")
_SMALL_VECTORS = ("sinks", "log_dt", "b_glu", "g_attn_out", "g_ssm_out", "g_post_mix", "g_pre_ffn", "g_post_ffn")
_SMALL_MATRICES = ("b_re", "b_im", "c_re", "c_im", "a_re", "a_im")
_ORDER = ("g_pre_mix", "w_in", "sinks", "a_re", "a_im", "log_dt", "b_re", "b_im", "c_re", "c_im", "d_skip", "w_glu",
          "b_glu", "g_attn_out", "g_ssm_out", "w_o", "g_post_mix", "g_pre_ffn", "w_gate", "w_up", "w_down",
          "g_post_ffn")


def _pack_grads(vectors, matrices):
    width = max(a.shape[1] for a in vectors)
    slots, row, lane = [], 0, 0
    for a in vectors:
        span = -(-a.shape[1] // LANES) * LANES
        if lane + span > width:
            row, lane = row + 1, 0
        slots.append((row, lane, a.shape[1]))
        lane += span
    firsts, at = [], 0
    for a in matrices:
        firsts.append(at)
        at += a.shape[0]
    nv = len(vectors)

    def body(*refs):
        vec_out, mat_out = refs[-2], refs[-1]
        vec_out[...] = jnp.zeros_like(vec_out)
        for ref, (r, l, w) in zip(refs[:nv], slots):
            vec_out[r:r + 1, l:l + w] = ref[...]
        for ref, r0 in zip(refs[nv:-2], firsts):
            mat_out[r0:r0 + ref.shape[0], :] = ref[...]

    ins = list(vectors) + list(matrices)
    outs = [_sds((-(-(row + 1) // SUBLANES) * SUBLANES, width), F32), _sds((at, matrices[0].shape[1]), F32)]
    vec_pack, mat_pack = _call(body, name="pack_small_grads", in_specs=_whole(ins), out_specs=_whole(outs),
                               out_shape=outs)(*ins)
    return vec_pack, slots, mat_pack, firsts


def _adam_replicated(sources, found_at, w, m, v):
    ns, n = len(sources), len(w)

    def body(*refs):
        ins, outs = refs[ns:ns + 3 * n], refs[ns + 3 * n:]
        summed = []
        for p_ref in refs[:ns]:
            g = p_ref[0]
            for k in range(1, N_DEV):
                g = g + p_ref[k]
            summed.append(g)
        for i, (src, row, lane) in enumerate(found_at):
            w_ref, m_ref, v_ref = ins[i], ins[n + i], ins[2 * n + i]
            rows, cols = w_ref.shape
            g = summed[src][row:row + rows, lane:lane + cols]
            delta, m_new, v_new = _adamw(w_ref[...], g, m_ref[...], v_ref[...])
            for o, val in zip(outs[4 * i:4 * i + 4], (g, delta, m_new, v_new)):
                o[...] = val

    ins = list(sources) + list(w) + list(m) + list(v)
    outs = [_sds(a.shape, F32) for a in w for _ in range(4)]
    flat = _call(body, name="adam_replicated", in_specs=_whole(ins), out_specs=_whole(outs), out_shape=outs)(*ins)
    return [tuple(flat[4 * i:4 * i + 4]) for i in range(n)]


def kernel(x, positions, g_pre_mix, w_in, sinks, a_re, a_im, log_dt, b_re, b_im, c_re, c_im, d_skip, w_glu, b_glu, g_attn_out, g_ssm_out, w_o, g_post_mix, g_pre_ffn, w_gate, w_up, w_down, g_post_ffn, loss_target, m_g_pre_mix, m_w_in, m_sinks, m_a_re, m_a_im, m_log_dt, m_b_re, m_b_im, m_c_re, m_c_im, m_d_skip, m_w_glu, m_b_glu, m_g_attn_out, m_g_ssm_out, m_w_o, m_g_post_mix, m_g_pre_ffn, m_w_gate, m_w_up, m_w_down, m_g_post_ffn, v_g_pre_mix, v_w_in, v_sinks, v_a_re, v_a_im, v_log_dt, v_b_re, v_b_im, v_c_re, v_c_im, v_d_skip, v_w_glu, v_b_glu, v_g_attn_out, v_g_ssm_out, v_w_o, v_g_post_mix, v_g_pre_ffn, v_w_gate, v_w_up, v_w_down, v_g_post_ffn):
    given = dict(locals())
    weights = {n: given[n] for n in _ORDER}
    mom_m = {n: given["m_" + n] for n in _ORDER}
    mom_v = {n: given["v_" + n] for n in _ORDER}

    t, d = x.shape[1], x.shape[2]
    d_attn = d // 2
    d_ssm = d - d_attn
    d_in = d_attn + 2 * D_KV + d_ssm
    n_groups = d_ssm // SSM_GROUP
    n_heads = d_attn // HEAD_DIM
    tm = min(256, t)

    x2 = x[0]
    target = loss_target[0]

    def by_rows(n, a):
        return a[0].T if n in _BY_COLUMNS else a[0]

    def start_gather(name, ns, token):
        behind = 0 if token is None else token[0, 0].astype(BF16)
        shards = [by_rows(n, weights[n]).astype(BF16) + behind for n in ns]
        return _exchange_start(name, shards, False, (SIBLING,) + CHIP_PEERS)

    def finish_gather(handle, after):
        forward, _ = _forward_start(handle["name"] + "_forward", _exchange_wait(handle, after))
        return _split_wait(forward, [])

    ag_in, token = start_gather("gather_w_in", ["w_in"], None)
    ag_mix, token = start_gather("gather_w_glu_o", ["w_glu", "w_o"], token)
    ag_ffn_in, token = start_gather("gather_w_gate_up", ["w_gate", "w_up"], token)
    ag_down, token = start_gather("gather_w_down", ["w_down"], token)

    xn, = _rows("norm_in", lambda xv, g: ([_rms(xv)[0] * g], []), [x2], [g_pre_mix], [(d, BF16)], [], tm,
                after=[token])
    win_g, = finish_gather(ag_in, [xn])
    w_in_t = win_g.reshape(d_in, d)
    proj = _mm_nt("proj_in", xn, w_in_t, F32, tn=d_in // 4 if (d_in // 4) % LANES == 0 else None)

    cos, sin = _rope_tables(positions.reshape(t, 1).astype(F32))
    sinks_row = jnp.pad(sinks, ((0, 0), (0, LANES - n_heads)))
    attn = _attention_fwd(proj, cos, sin, sinks_row, d_attn)

    def view(n, a):
        if n in ("b_re", "b_im"):
            return jnp.transpose(a[0], (0, 2, 1)).reshape(-1, SSM_STATE)
        if n in ("c_re", "c_im"):
            return a[0].reshape(-1, SSM_STATE)
        return a[0].T if n == "d_skip" else a[0] if a.ndim == 3 else a

    def unview(n, val):
        if n in ("b_re", "b_im"):
            return jnp.transpose(val.reshape(n_groups, SSM_GROUP, SSM_STATE), (0, 2, 1))[None]
        if n in ("c_re", "c_im"):
            return val.reshape(1, n_groups, SSM_GROUP, SSM_STATE)
        return val.T[None] if n == "d_skip" else val[None] if weights[n].ndim == 3 else val

    b_re_v, b_im_v = view("b_re", b_re), view("b_im", b_im)
    ldt_col = log_dt.reshape(n_groups, 1)
    lam_re, lam_im, bbar_re, bbar_im = _s5_discretise(a_re[0], a_im[0], ldt_col, b_re_v, b_im_v)
    n_blocks = n_groups // GROUPS_PER_BLOCK
    mats = [_block_diag_in(bbar_re).astype(BF16), _block_diag_in(bbar_im).astype(BF16),
            lam_re.reshape(n_blocks, 1, SSM_ST_BLOCK), lam_im.reshape(n_blocks, 1, SSM_ST_BLOCK),
            _block_diag_out(view("c_re", c_re)).astype(BF16), _block_diag_out(view("c_im", c_im)).astype(BF16)]
    dskip_row = d_skip.reshape(1, d_ssm)
    y_ssm = _s5_fwd(proj, mats, dskip_row, d_attn, d_ssm)
    gelu_bf16 = lambda yv: _gelu(yv).astype(BF16)
    wglu_g, wo_g = finish_gather(ag_mix, [attn, y_ssm])
    w_glu_full = wglu_g.reshape(d_ssm, d_ssm)
    w_o_full = wo_g.reshape(d, d)
    glu_lin = _mm_nn("glu_gate", y_ssm, w_glu_full, F32, a_fn=gelu_bf16)

    def mix_prep(av, yv, gl, bg, ga, gs):
        ssm = _gelu(yv) * _sigmoid(gl + bg)
        return [jnp.concatenate([_rms(av)[0] * ga, _rms(ssm)[0] * gs], axis=1)], []

    mixed, = _rows("mix_prep", mix_prep, [attn, y_ssm, glu_lin], [b_glu, g_attn_out, g_ssm_out], [(d, BF16)], [], tm)
    mix = _mm_nn("mix_out", mixed, w_o_full, F32, tn=d // 2 if (d // 2) % LANES == 0 else None)

    def post_mix(xv, mv, gpm, gpf):
        h = xv + _rms(mv)[0] * gpm
        return [h, _rms(h)[0] * gpf], []

    h, hn = _rows("post_mix", post_mix, [x2, mix], [g_post_mix, g_pre_ffn], [(d, F32), (d, BF16)], [], tm)
    wgate_g, wup_g = finish_gather(ag_ffn_in, [hn])
    gate, up, hid = _ffn_in(hn, wgate_g, wup_g)
    wdown_g, = finish_gather(ag_down, [hid])
    ff = _mm_contract_slots("ffn_down", [(hid, wdown_g)], F32, per_step=4)

    def head(hv, fv, tv, gpo):
        out = hv + _rms(fv)[0] * gpo
        err = out - tv
        dout = err * (1.0 / d)
        dff, dg = _rms_bwd(fv, gpo, dout)
        loss = jnp.zeros((1, LANES), F32) + 0.5 * jnp.sum(err * err) * (1.0 / d)
        return [dff, dout], [dg, loss]

    dff, dh_out, dg_post_ffn, loss_row = _rows("loss_head", head, [h, ff, target], [g_post_ffn],
                                               [(d, BF16), (d, F32)], [d, LANES], tm)

    dw_down = _mm_slots_tn("ffn_down_dw", hid, dff, BF16)
    rs_down, tok_down = _exchange_start("scatter_dw_down", [dw_down], True)
    dgate, dup = _ffn_down_bwd(dff, wdown_g, gate, up, [tok_down])
    dhn = _mm_contract_slots("ffn_in_dx", [(dgate, wgate_g), (dup, wup_g)], F32, per_step=2)
    dw_gate = _mm_slots_tn("ffn_gate_dw", dgate, hn, BF16)
    dw_up = _mm_slots_tn("ffn_up_dw", dup, hn, BF16)
    rs_ffn_in, tok_ffn_in = _exchange_start("scatter_dw_gate_up", [dw_gate, dw_up], True)

    def mid_bwd(dho, dhn_, hv, mv, gpf, gpm):
        d1, dgpf = _rms_bwd(hv, gpf, dhn_)
        dh_ = dho + d1
        dmix_, dgpm = _rms_bwd(mv, gpm, dh_)
        return [dh_, dmix_], [dgpf, dgpm]

    dh, dmix, dg_pre_ffn, dg_post_mix = _rows("mid_bwd", mid_bwd, [dh_out, dhn, h, mix], [g_pre_ffn, g_post_mix],
                                              [(d, F32), (d, BF16)], [d, d], tm, after=[tok_ffn_in])

    dmixed = _mm_nt("mix_out_dx", dmix, w_o_full, F32, tn=d // 2 if (d // 2) % LANES == 0 else None)
    dw_o = _mm_tn("mix_out_dw", mixed, dmix, BF16, tn=d // 2 if (d // 2) % LANES == 0 else None)
    rs_o, tok_o = _exchange_start("scatter_dw_o", [dw_o.reshape(N_DEV, d // N_DEV, d)], True)

    def mix_bwd(dm, av, yv, gl, bg, ga, gs):
        dattn_, dga = _rms_bwd(av, ga, dm[:, :d_attn])
        z = _gelu(yv)
        sg = _sigmoid(gl + bg)
        dssm, dgs = _rms_bwd(z * sg, gs, dm[:, d_attn:])
        dgl = dssm * z * sg * (1.0 - sg)
        return [dattn_, dssm * sg, dgl], [dga, dgs, jnp.sum(dgl, axis=0, keepdims=True)]

    dattn, dz_direct, dglu, dg_attn_out, dg_ssm_out, db_glu = _rows(
        "mix_bwd", mix_bwd, [dmixed, attn, y_ssm, glu_lin], [b_glu, g_attn_out, g_ssm_out],
        [(d_attn, F32), (d_ssm, F32), (d_ssm, BF16)], [d_attn, d_ssm, d_ssm], tm, after=[tok_o])
    dz_glu = _mm_nt("glu_gate_dx", dglu, w_glu_full, F32)
    dw_glu = _mm_tn("glu_gate_dw", y_ssm, dglu, BF16, a_fn=gelu_bf16)

    du, db_re_dense, db_im_dense, dlam_re, dlam_im, dc_re_dense, dc_im_dense, dd_skip = _s5_bwd(
        proj, mats, dskip_row, y_ssm, dz_direct, dz_glu, d_attn, d_ssm)
    da_re, da_im, dlog_dt, db_re_v, db_im_v = _s5_discretise_bwd(
        a_re[0], a_im[0], ldt_col, b_re_v, b_im_v, dlam_re.reshape(n_groups, SSM_STATE),
        dlam_im.reshape(n_groups, SSM_STATE), _block_diag_in_t(db_re_dense), _block_diag_in_t(db_im_dense))
    dq, dk2, dv2, dsinks_row = _attention_bwd(proj, cos, sin, sinks_row, dattn, d_attn)

    small_grads = {
        "sinks": dsinks_row, "a_re": da_re, "a_im": da_im, "log_dt": dlog_dt.reshape(1, n_groups),
        "b_re": db_re_v, "b_im": db_im_v, "c_re": _block_diag_out_t(dc_re_dense),
        "c_im": _block_diag_out_t(dc_im_dense), "d_skip": dd_skip.reshape(n_groups, SSM_GROUP).T, "b_glu": db_glu,
        "g_attn_out": dg_attn_out, "g_ssm_out": dg_ssm_out, "g_post_mix": dg_post_mix, "g_pre_ffn": dg_pre_ffn,
        "g_post_ffn": dg_post_ffn,
    }
    vec_pack, vec_slots, mat_pack, mat_rows = _pack_grads([small_grads[n] for n in _SMALL_VECTORS],
                                                          [small_grads[n] for n in _SMALL_MATRICES])
    ag_small, token = _exchange_start("gather_small_grads", [vec_pack, mat_pack, small_grads["d_skip"]], False)
    dproj = _assemble_dproj(dq, dk2, dv2, du, d_in, [token])

    dxn = _mm_nn("proj_in_dx", dproj, w_in_t, F32, tn=d // 2 if (d // 2) % LANES == 0 else None)
    dw_in = _mm_tn("proj_in_dw", dproj, xn, BF16).reshape(N_DEV, d_in // N_DEV, d)
    rs_in, token = _exchange_start("scatter_dw_in_glu", [dw_in, dw_glu.reshape(N_DEV, d_ssm // N_DEV, d_ssm)], True)

    def x_bwd(dh_, dxn_, xv, g):
        dx, dg = _rms_bwd(xv, g, dxn_)
        return [dh_ + dx], [dg]

    grad_x, dg_pre_mix = _rows("norm_in_bwd", x_bwd, [dh, dxn, x2], [g_pre_mix], [(d, F32)], [d], tm, after=[token])
    ag_last, token = _exchange_start("gather_g_pre_mix_grad", [dg_pre_mix], False)

    results = {}

    def adam_big(n, parts):
        r = parts.shape[1]
        results[n] = _adam_sharded("adam_" + n, parts, by_rows(n, weights[n]), by_rows(n, mom_m[n]),
                                   by_rows(n, mom_v[n]), 64 if r % 64 == 0 else r)
        return results[n][3]

    done = [grad_x, token]
    adam_big("w_down", _exchange_wait(rs_down, done)[0])
    p_gate, p_up = _exchange_wait(rs_ffn_in, done)
    done = [adam_big("w_gate", p_gate), adam_big("w_up", p_up), results["w_down"][3]]
    done = [adam_big("w_o", _exchange_wait(rs_o, done)[0])]
    p_in, p_glu = _exchange_wait(rs_in, done)
    done = [adam_big("w_in", p_in), adam_big("w_glu", p_glu)]
    vec_parts, mat_parts, dskip_parts = _exchange_wait(ag_small, done)
    first_gain_parts, = _exchange_wait(ag_last, done)
    for n, row0 in zip(_SMALL_MATRICES, mat_rows):
        rows = view(n, weights[n]).shape[0]
        results[n] = _adam_sharded("adam_" + n, mat_parts, view(n, weights[n]), view(n, mom_m[n]), view(n, mom_v[n]),
                                   min(64, rows), row0)
    rest = _SMALL_VECTORS + ("d_skip", "g_pre_mix")
    found_at = [(0, row, lane) for row, lane, _ in vec_slots] + [(1, 0, 0), (2, 0, 0)]
    updated = _adam_replicated([vec_parts, dskip_parts, first_gain_parts], found_at,
                               [view(n, weights[n]) for n in rest], [view(n, mom_m[n]) for n in rest],
                               [view(n, mom_v[n]) for n in rest])
    results.update(zip(rest, updated))

    loss = lax.psum(loss_row[0, 0], ("x", "y", "c"))
    outs = [loss, grad_x[None]]
    for k in range(4):
        for n in _ORDER:
            val = results[n][k]
            outs.append(val.T[None] if n in _BY_COLUMNS else val[None] if n in _BIG else unview(n, val))
    return tuple(outs)
```

```python
import math

import jax
import jax.numpy as jnp
from jax import lax
from jax.experimental import pallas as pl
from jax.experimental.pallas import tpu as pltpu

F32 = jnp.float32
BF16 = jnp.bfloat16

HEAD_DIM = 64
N_KV_HEADS = 4
D_KV = N_KV_HEADS * HEAD_DIM
WINDOW = 128
BLOCK = 128
ROPE_THETA = 10000.0
SSM_GROUP = 16
SSM_STATE = 64
GROUPS_PER_BLOCK = 8
SSM_CH_BLOCK = GROUPS_PER_BLOCK * SSM_GROUP
SSM_ST_BLOCK = GROUPS_PER_BLOCK * SSM_STATE
RMS_EPS = 1e-6
N_DEV = 8
LANES = 128
SUBLANES = 8
MASKED = -1e30

ADAM_LR = 0.001
ADAM_B1 = 0.9
ADAM_B2 = 0.999
ADAM_EPS = 1e-08
ADAM_WD = 0.01
ADAM_STEP = 10

VMEM_LIMIT_BYTES = 56 * 1024 * 1024


def _call(body, *, name, out_shape, in_specs, out_specs, grid=(), scratch_shapes=(), semantics=None, n_after=0):
    params = dict(vmem_limit_bytes=VMEM_LIMIT_BYTES)
    if semantics is not None:
        params["dimension_semantics"] = semantics
    n_in = len(in_specs)
    if n_after:
        inner = body

        def body(*refs):
            inner(*refs[:n_in], *refs[n_in + n_after:])

        in_specs = list(in_specs) + [pl.BlockSpec(memory_space=pl.ANY)] * n_after
    return pl.pallas_call(body, name=name, grid=grid, in_specs=in_specs, out_specs=out_specs, out_shape=out_shape,
                          scratch_shapes=scratch_shapes, compiler_params=pltpu.CompilerParams(**params))


def _sds(shape, dtype):
    return jax.ShapeDtypeStruct(tuple(shape), dtype)


def _dot(a, b, ca, cb):
    return lax.dot_general(a, b, (((ca,), (cb,)), ((), ())), preferred_element_type=F32)


def _rms(x):
    r = lax.rsqrt(jnp.mean(x * x, axis=-1, keepdims=True) + RMS_EPS)
    return x * r, r


def _rms_bwd(x, g, dy):
    xh, r = _rms(x)
    dxh = dy * g
    dx = r * (dxh - xh * jnp.mean(dxh * xh, axis=-1, keepdims=True))
    return dx, jnp.sum(dy * xh, axis=0, keepdims=True)


def _sigmoid(x):
    return 1.0 / (1.0 + jnp.exp(-x))


_GELU_C = math.sqrt(2.0 / math.pi)
_GELU_A = 0.044715


def _gelu(y):
    t = jnp.tanh(_GELU_C * (y + _GELU_A * y * y * y))
    return 0.5 * y * (1.0 + t)


def _gelu_grad(y):
    t = jnp.tanh(_GELU_C * (y + _GELU_A * y * y * y))
    return 0.5 * (1.0 + t) + 0.5 * y * (1.0 - t * t) * _GELU_C * (1.0 + 3.0 * _GELU_A * y * y)


def _rows(name, fn, row_ins, vec_ins, row_outs, acc_widths, tm, after=()):
    rows = row_ins[0].shape[0]
    assert rows % tm == 0, (name, rows, tm)
    n_row, n_vec, n_out, n_acc = len(row_ins), len(vec_ins), len(row_outs), len(acc_widths)

    def body(*refs):
        ins = [r[...] for r in refs[:n_row + n_vec]]
        outs = refs[n_row + n_vec:n_row + n_vec + n_out]
        accs = refs[n_row + n_vec + n_out:]
        row_vals, acc_vals = fn(*ins)
        for o, v in zip(outs, row_vals):
            o[...] = v.astype(o.dtype)
        if n_acc:
            @pl.when(pl.program_id(0) == 0)
            def _():
                for a in accs:
                    a[...] = jnp.zeros_like(a)
            for a, v in zip(accs, acc_vals):
                a[...] += v

    in_specs = [pl.BlockSpec((tm, a.shape[1]), lambda i: (i, 0)) for a in row_ins]
    in_specs += [pl.BlockSpec(v.shape, lambda i: (0, 0)) for v in vec_ins]
    out_specs = [pl.BlockSpec((tm, w), lambda i: (i, 0)) for w, _ in row_outs]
    out_specs += [pl.BlockSpec((1, w), lambda i: (0, 0)) for w in acc_widths]
    out_shape = [_sds((rows, w), dt) for w, dt in row_outs] + [_sds((1, w), F32) for w in acc_widths]
    return _call(body, name=name, grid=(rows // tm,), in_specs=in_specs, out_specs=out_specs, out_shape=out_shape,
                 semantics=("arbitrary",) if n_acc else ("parallel",), n_after=len(after))(*row_ins, *vec_ins, *after)


def _matmul(name, operands, in_specs, product, grid, out_shape, out_spec, acc_shape, after=()):
    nk = grid[-1]
    n_in = len(operands)
    in_place = out_shape.dtype == F32

    def body(*refs):
        ins = [r[...] for r in refs[:n_in]]
        o_ref = refs[n_in]
        if nk == 1:
            o_ref[...] = product(*ins).astype(o_ref.dtype)
            return
        acc = o_ref if in_place else refs[n_in + 1]
        k = pl.program_id(len(grid) - 1)

        @pl.when(k == 0)
        def _():
            acc[...] = jnp.zeros_like(acc)

        acc[...] += product(*ins)

        if not in_place:
            @pl.when(k == nk - 1)
            def _():
                o_ref[...] = acc[...].astype(o_ref.dtype)

    return _call(body, name=name, grid=grid, in_specs=in_specs, out_specs=out_spec, out_shape=out_shape,
                 scratch_shapes=[] if nk == 1 or in_place else [pltpu.VMEM(acc_shape, F32)],
                 semantics=("parallel",) * (len(grid) - 1) + ("arbitrary",), n_after=len(after))(*operands, *after)


def _mm_nn(name, a, b, out_dtype, tm=512, tn=None, a_fn=lambda x: x):
    m, k = a.shape
    n = b.shape[1]
    tm, tn = min(tm, m), n if tn is None else tn
    return _matmul(name, [a, b],
                   [pl.BlockSpec((tm, k), lambda i, j, s: (i, 0)), pl.BlockSpec((k, tn), lambda i, j, s: (0, j))],
                   lambda x, y: _dot(a_fn(x), y, 1, 0), (m // tm, n // tn, 1), _sds((m, n), out_dtype),
                   pl.BlockSpec((tm, tn), lambda i, j, s: (i, j)), (tm, tn))


def _mm_nt(name, a, b, out_dtype, tm=512, tn=None):
    m, k = a.shape
    n = b.shape[0]
    tm, tn = min(tm, m), n if tn is None else tn
    return _matmul(name, [a, b],
                   [pl.BlockSpec((tm, k), lambda i, j, s: (i, 0)), pl.BlockSpec((tn, k), lambda i, j, s: (j, 0))],
                   lambda x, y: _dot(x, y, 1, 1), (m // tm, n // tn, 1), _sds((m, n), out_dtype),
                   pl.BlockSpec((tm, tn), lambda i, j, s: (i, j)), (tm, tn))


def _mm_tn(name, a, b, out_dtype, tm=512, tn=None, tk=2048, a_fn=lambda x: x, after=()):
    k, m = a.shape
    n = b.shape[1]
    tm, tk, tn = min(tm, m), min(tk, k), n if tn is None else tn
    return _matmul(name, [a, b],
                   [pl.BlockSpec((tk, tm), lambda i, j, s: (s, i)), pl.BlockSpec((tk, tn), lambda i, j, s: (s, j))],
                   lambda x, y: _dot(a_fn(x), y, 0, 0), (m // tm, n // tn, k // tk), _sds((m, n), out_dtype),
                   pl.BlockSpec((tm, tn), lambda i, j, s: (i, j)), (tm, tn), after)


def _mm_contract_slots(name, pairs, out_dtype, per_step, tm=512, tn=2048, after=()):
    s_, m, k = pairs[0][0].shape
    n = pairs[0][1].shape[2]
    tm, tn = min(tm, m), min(tn, n)
    ops, specs = [], []
    for a, b in pairs:
        ops += [a, b]
        specs += [pl.BlockSpec((per_step, tm, k), lambda i, j, s: (s, i, 0)),
                  pl.BlockSpec((per_step, k, tn), lambda i, j, s: (s, 0, j))]

    def product(*t):
        return sum(_dot(t[2 * p][q], t[2 * p + 1][q], 1, 0) for p in range(len(pairs)) for q in range(per_step))

    return _matmul(name, ops, specs, product, (m // tm, n // tn, s_ // per_step), _sds((m, n), out_dtype),
                   pl.BlockSpec((tm, tn), lambda i, j, s: (i, j)), (tm, tn), after)


def _mm_slots_tn(name, a, b, out_dtype, tn=2048, tk=2048):
    s_, k, m = a.shape
    n = b.shape[1]
    tn, tk = min(tn, n), min(tk, k)
    return _matmul(name, [a, b],
                   [pl.BlockSpec((None, tk, m), lambda s, j, z: (s, z, 0)), pl.BlockSpec((tk, tn), lambda s, j, z: (z, j))],
                   lambda x, y: _dot(x, y, 0, 0), (s_, n // tn, k // tk), _sds((s_, m, n), out_dtype),
                   pl.BlockSpec((None, m, tn), lambda s, j, z: (s, 0, j)), (m, tn))


def _ffn_in(a, w_gate, w_up, tm=512):
    m, k = a.shape
    s_, n, _ = w_gate.shape
    tm = min(tm, m)

    def body(a_ref, wg_ref, wu_ref, g_ref, u_ref, h_ref):
        x = a_ref[...]
        g = _dot(x, wg_ref[...], 1, 1)
        u = _dot(x, wu_ref[...], 1, 1)
        g_ref[...] = g.astype(BF16)
        u_ref[...] = u.astype(BF16)
        h_ref[...] = (g * _sigmoid(g) * u).astype(BF16)

    w_spec = pl.BlockSpec((None, n, k), lambda s, i: (s, 0, 0))
    o_spec = pl.BlockSpec((None, tm, n), lambda s, i: (s, i, 0))
    return _call(body, name="ffn_in", grid=(s_, m // tm),
                 in_specs=[pl.BlockSpec((tm, k), lambda s, i: (i, 0)), w_spec, w_spec], out_specs=[o_spec] * 3,
                 out_shape=[_sds((s_, m, n), BF16)] * 3, semantics=("parallel", "parallel"))(a, w_gate, w_up)


def _ffn_down_bwd(d_out, w_down, gate, up, after, tm=512):
    m, k = d_out.shape
    s_, n, _ = w_down.shape
    tm = min(tm, m)

    def body(d_ref, w_ref, g_ref, u_ref, dg_ref, du_ref):
        dh = _dot(d_ref[...], w_ref[...], 1, 1)
        g = g_ref[...].astype(F32)
        sg = _sigmoid(g)
        dg_ref[...] = (dh * u_ref[...].astype(F32) * sg * (1.0 + g * (1.0 - sg))).astype(BF16)
        du_ref[...] = (dh * g * sg).astype(BF16)

    t_spec = pl.BlockSpec((None, tm, n), lambda s, i: (s, i, 0))
    return _call(body, name="ffn_down_dx", grid=(s_, m // tm),
                 in_specs=[pl.BlockSpec((tm, k), lambda s, i: (i, 0)), pl.BlockSpec((None, n, k), lambda s, i: (s, 0, 0)),
                           t_spec, t_spec],
                 out_specs=[t_spec] * 2, out_shape=[_sds((s_, m, n), BF16)] * 2, semantics=("parallel", "parallel"),
                 n_after=len(after))(d_out, w_down, gate, up, *after)


ALL_PEERS = (1, 2, 3, 4, 5, 6, 7)
CHIP_PEERS = (2, 4, 6)
SIBLING = 1


def _peer(relation):
    x, y, c = lax.axis_index("x"), lax.axis_index("y"), lax.axis_index("c")
    pos = (1 - x if relation & 4 else x, 1 - y if relation & 2 else y, 1 - c if relation & 1 else c)
    return pos, 4 * pos[0] + 2 * pos[1] + pos[2]


def _slot(relation, by_chip):
    pos, device = _peer(relation)
    return 2 * pos[0] + pos[1] if by_chip else device


def _exchange_copies(ins, lands, send_sems, recv_sems, scatter, relations, by_chip=False):
    me = _slot(0, by_chip)

    def copy(a, s, peer, pos, dst_slot):
        return pltpu.make_async_remote_copy(
            src_ref=ins[a].at[peer] if scatter else ins[a], dst_ref=lands[a].at[dst_slot],
            send_sem=send_sems.at[s], recv_sem=recv_sems.at[s], device_id=pos, device_id_type=pl.DeviceIdType.MESH)

    pairs = []
    for k, r in enumerate(relations):
        pos, peer = _peer(r)[0], _slot(r, by_chip)
        for a in range(len(ins)):
            s = a * len(relations) + k
            pairs.append((copy(a, s, peer, pos, me), copy(a, s, peer, pos, peer)))
    return pairs


def _halves_copies(arrays, lands, send_sems, recv_sems):
    sibling, _ = _peer(SIBLING)
    core = lax.axis_index("c")
    pairs = []
    for a, (ref, land) in enumerate(zip(arrays, lands)):
        send = pltpu.make_async_remote_copy(
            src_ref=ref.at[:, pl.ds(1 - core, 1)], dst_ref=land, send_sem=send_sems.at[a], recv_sem=recv_sems.at[a],
            device_id=sibling, device_id_type=pl.DeviceIdType.MESH)
        pairs.append((send, send))
    return pairs


def _forward_copies(lands, send_sems, recv_sems):
    sibling, _ = _peer(SIBLING)

    def copy(a, s, slot):
        return pltpu.make_async_remote_copy(
            src_ref=lands[a].at[slot], dst_ref=lands[a].at[slot], send_sem=send_sems.at[s], recv_sem=recv_sems.at[s],
            device_id=sibling, device_id_type=pl.DeviceIdType.MESH)

    pairs = []
    for k, r in enumerate(CHIP_PEERS):
        _, mine = _peer(r)
        _, theirs = _peer(r | SIBLING)
        for a in range(len(lands)):
            s = a * len(CHIP_PEERS) + k
            pairs.append((copy(a, s, mine), copy(a, s, theirs)))
    return pairs


_HBM_SPEC = pl.BlockSpec(memory_space=pltpu.HBM)
_SEM_SPEC = pl.BlockSpec(memory_space=pltpu.SEMAPHORE)
_SIDE_EFFECT = pltpu.SideEffectType.DATAFLOW_SIDE_EFFECTING


def _split_start(name, operands, n_sem, make_pairs):
    k = len(operands)

    def body(*refs):
        send_sems, recv_sems, token = refs[k], refs[k + 1], refs[-1]
        for send, _ in make_pairs(refs[:k], send_sems, recv_sems):
            send.start()
        token[...] = jnp.zeros_like(token)

    out = pl.pallas_call(
        body, name=name,
        out_shape=(pltpu.SemaphoreType.DMA((n_sem,)), pltpu.SemaphoreType.DMA((n_sem,)),
                   *[pltpu.HBM(a.shape, a.dtype) for a in operands], _sds((SUBLANES, LANES), F32)),
        in_specs=[_HBM_SPEC] * k,
        out_specs=(_SEM_SPEC, _SEM_SPEC, *[_HBM_SPEC] * k, pl.BlockSpec(memory_space=pltpu.VMEM)),
        input_output_aliases={i: 2 + i for i in range(k)},
        compiler_params=pltpu.CompilerParams(has_side_effects=_SIDE_EFFECT),
    )(*[pltpu.with_memory_space_constraint(a, pltpu.HBM) for a in operands])
    return dict(name=name, sems=out[:2], thru=list(out[2:2 + k]), make_pairs=make_pairs), out[-1]


def _split_wait(handle, after):
    thru, make_pairs = handle["thru"], handle["make_pairs"]
    k = len(thru)

    def body(*refs):
        for send, arrival in make_pairs(refs[:k], refs[k], refs[k + 1]):
            send.wait_send()
            arrival.wait_recv()

    return pl.pallas_call(
        body, name=handle["name"] + "_wait", out_shape=[pltpu.HBM(a.shape, a.dtype) for a in thru],
        in_specs=[_HBM_SPEC] * k + [_SEM_SPEC, _SEM_SPEC] + [pl.BlockSpec(memory_space=pl.ANY)] * len(after),
        out_specs=[_HBM_SPEC] * k, input_output_aliases={i: i for i in range(k)},
        compiler_params=pltpu.CompilerParams(has_side_effects=_SIDE_EFFECT),
    )(*thru, *handle["sems"], *after)


def _exchange_start(name, arrays, scatter, relations=ALL_PEERS, by_chip=False):
    n = len(arrays)
    lands = [lax.empty(a.shape if scatter else (N_DEV,) + a.shape, a.dtype) for a in arrays]

    def make_pairs(refs, send_sems, recv_sems):
        return _exchange_copies(refs[:n], refs[n:], send_sems, recv_sems, scatter, relations, by_chip)

    handle, token = _split_start(name, list(arrays) + lands, n * len(relations), make_pairs)
    handle.update(n=n, scatter=scatter, by_chip=by_chip)
    return handle, token


def _halves_start(name, arrays):
    lands = [lax.empty((a.shape[0], 1) + a.shape[2:], a.dtype) for a in arrays]
    n = len(arrays)

    def make_pairs(refs, send_sems, recv_sems):
        return _halves_copies(refs[:n], refs[n:], send_sems, recv_sems)

    return _split_start(name, list(arrays) + lands, n, make_pairs)


def _chip_sum(name, array, landed):
    chips, _, r, c = array.shape
    tr = 64 if r % 64 == 0 else r

    def body(a_ref, b_ref, o_ref):
        mine = a_ref[lax.axis_index("c")].astype(F32)
        o_ref[...] = (mine + b_ref[...].astype(F32)).astype(o_ref.dtype)

    return _call(body, name=name, grid=(chips, r // tr),
                 in_specs=[pl.BlockSpec((None, 2, tr, c), lambda k, i: (k, 0, i, 0)),
                           pl.BlockSpec((None, None, tr, c), lambda k, i: (k, 0, i, 0))],
                 out_specs=pl.BlockSpec((None, tr, c), lambda k, i: (k, i, 0)),
                 out_shape=_sds((chips, r, c), BF16), semantics=("parallel", "parallel"))(array, landed)


def _forward_start(name, lands):
    return _split_start(name, list(lands), len(lands) * len(CHIP_PEERS), _forward_copies)


def _exchange_wait(handle, after):
    n, scatter = handle["n"], handle["scatter"]
    out = _split_wait(handle, after)
    me = 2 * lax.axis_index("x") + lax.axis_index("y")
    if not handle["by_chip"]:
        me = 2 * me + lax.axis_index("c")
    done = []
    for src, land in zip(out[:n], out[n:]):
        own = lax.dynamic_index_in_dim(src, me, 0, keepdims=True) if scatter else src[None]
        done.append(lax.dynamic_update_slice_in_dim(land, own, me, 0))
    return done


def _rope_tables(pos_col):
    t = pos_col.shape[0]
    half = HEAD_DIM // 2
    inv_freq = ROPE_THETA ** (-jnp.arange(half, dtype=F32) / half)
    inv_row = jnp.tile(inv_freq, LANES // half)[None, :]

    def body(pos_ref, inv_ref, cos_ref, sin_ref):
        ang = pos_ref[...] * inv_ref[...]
        cos_ref[...] = jnp.cos(ang)
        sin_ref[...] = jnp.sin(ang)

    tm = min(t, 512)
    return _call(body, name="rope_tables", grid=(t // tm,),
                 in_specs=[pl.BlockSpec((tm, 1), lambda i: (i, 0)), pl.BlockSpec((1, LANES), lambda i: (0, 0))],
                 out_specs=[pl.BlockSpec((tm, LANES), lambda i: (i, 0))] * 2,
                 out_shape=[_sds((t, LANES), F32)] * 2, semantics=("parallel",))(pos_col, inv_row)


def _rot_half(x):
    lane = lax.broadcasted_iota(jnp.int32, x.shape, 1)
    low = (lane % HEAD_DIM) < HEAD_DIM // 2
    return jnp.where(low, -pltpu.roll(x, LANES - HEAD_DIM // 2, 1), pltpu.roll(x, HEAD_DIM // 2, 1))


def _rope(x, cos, sin):
    return x * cos + _rot_half(x) * sin


def _unrope(d, cos, sin):
    return d * cos - _rot_half(d) * sin


def _band_mask(first_block, heads):
    r = lax.broadcasted_iota(jnp.int32, (heads * BLOCK, 2 * BLOCK), 0) % BLOCK
    c = lax.broadcasted_iota(jnp.int32, (heads * BLOCK, 2 * BLOCK), 1)
    diff = r - c + BLOCK
    return (diff >= 0) & (diff < WINDOW) & ((c >= BLOCK) | jnp.logical_not(first_block))


def _attn_specs(t, d_attn, d_in):
    kb, vb = d_attn // D_KV, d_attn // D_KV + 1
    prev = lambda i: jnp.maximum(i - 1, 0)
    return [
        pl.BlockSpec((BLOCK, d_attn), lambda i: (i, 0)),
        pl.BlockSpec((BLOCK, D_KV), lambda i: (i, kb)),
        pl.BlockSpec((BLOCK, D_KV), lambda i: (i, vb)),
        pl.BlockSpec((BLOCK, D_KV), lambda i: (prev(i), kb)),
        pl.BlockSpec((BLOCK, D_KV), lambda i: (prev(i), vb)),
        pl.BlockSpec((BLOCK, LANES), lambda i: (i, 0)),
        pl.BlockSpec((BLOCK, LANES), lambda i: (i, 0)),
        pl.BlockSpec((BLOCK, LANES), lambda i: (prev(i), 0)),
        pl.BlockSpec((BLOCK, LANES), lambda i: (prev(i), 0)),
        pl.BlockSpec((1, LANES), lambda i: (0, 0)),
    ]


def _head(x, h):
    return x[:, h * HEAD_DIM:(h + 1) * HEAD_DIM]


def _attn_heads(q_ref, kc_ref, vc_ref, kp_ref, vp_ref, cq_ref, sq_ref, cp_ref, sp_ref, d_attn):
    cq, sq, cp, sp = cq_ref[...], sq_ref[...], cp_ref[...], sp_ref[...]
    q_rot = [_rope(q_ref[:, j * LANES:(j + 1) * LANES], cq, sq) for j in range(d_attn // LANES)]
    kc_rot = [_rope(kc_ref[:, j * LANES:(j + 1) * LANES], cq, sq) for j in range(D_KV // LANES)]
    kp_rot = [_rope(kp_ref[:, j * LANES:(j + 1) * LANES], cp, sp) for j in range(D_KV // LANES)]
    per = LANES // HEAD_DIM
    q_heads = [_head(q_rot[h // per], h % per).astype(BF16) for h in range(d_attn // HEAD_DIM)]
    kk = [jnp.concatenate([_head(kp_rot[g // per], g % per), _head(kc_rot[g // per], g % per)], axis=0).astype(BF16)
          for g in range(N_KV_HEADS)]
    vv = [jnp.concatenate([_head(vp_ref[...], g), _head(vc_ref[...], g)], axis=0).astype(BF16) for g in range(N_KV_HEADS)]
    return q_heads, kk, vv


def _stack_group(q_heads, sink_ref, group):
    q_all = jnp.concatenate([q_heads[h] for h in group], axis=0)
    sink_all = jnp.concatenate([jnp.broadcast_to(sink_ref[:, h:h + 1], (BLOCK, 1)) for h in group], axis=0)
    return q_all, sink_all


def _softmax_with_sink(q, kk, sink, mask):
    s = _dot(q, kk, 1, 1) * (1.0 / math.sqrt(HEAD_DIM))
    s = jnp.where(mask, s, MASKED)
    m = jnp.maximum(jnp.max(s, axis=-1, keepdims=True), sink)
    p = jnp.exp(s - m)
    e_sink = jnp.exp(sink - m)
    inv = 1.0 / (jnp.sum(p, axis=-1, keepdims=True) + e_sink)
    return p * inv, e_sink * inv


def _attention_fwd(proj, cos, sin, sinks_row, d_attn):
    t, d_in = proj.shape
    n_heads = d_attn // HEAD_DIM
    q_per_kv = n_heads // N_KV_HEADS

    def body(q_ref, kc_ref, vc_ref, kp_ref, vp_ref, cq_ref, sq_ref, cp_ref, sp_ref, sink_ref, o_ref):
        mask = _band_mask(pl.program_id(0) == 0, q_per_kv)
        q_heads, kk, vv = _attn_heads(q_ref, kc_ref, vc_ref, kp_ref, vp_ref, cq_ref, sq_ref, cp_ref, sp_ref, d_attn)
        for g in range(N_KV_HEADS):
            group = range(g * q_per_kv, (g + 1) * q_per_kv)
            q_all, sink_all = _stack_group(q_heads, sink_ref, group)
            probs, _ = _softmax_with_sink(q_all, kk[g], sink_all, mask)
            o_all = _dot(probs.astype(BF16), vv[g], 1, 0)
            for k, h in enumerate(group):
                o_ref[:, h * HEAD_DIM:(h + 1) * HEAD_DIM] = o_all[k * BLOCK:(k + 1) * BLOCK]

    return _call(body, name="attention_fwd", grid=(t // BLOCK,), in_specs=_attn_specs(t, d_attn, d_in),
                 out_specs=pl.BlockSpec((BLOCK, d_attn), lambda i: (i, 0)), out_shape=_sds((t, d_attn), F32),
                 semantics=("parallel",))(proj, proj, proj, proj, proj, cos, sin, cos, sin, sinks_row)


def _attention_bwd(proj, cos, sin, sinks_row, d_out, d_attn):
    t, d_in = proj.shape
    n_heads = d_attn // HEAD_DIM
    q_per_kv = n_heads // N_KV_HEADS
    nb = t // BLOCK
    per = LANES // HEAD_DIM

    def body(q_ref, kc_ref, vc_ref, kp_ref, vp_ref, cq_ref, sq_ref, cp_ref, sp_ref, sink_ref, do_ref,
             dq_ref, dk_ref, dv_ref, dsink_ref):
        i = pl.program_id(0)
        mask = _band_mask(i == 0, q_per_kv)
        q_heads, kk, vv = _attn_heads(q_ref, kc_ref, vc_ref, kp_ref, vp_ref, cq_ref, sq_ref, cp_ref, sp_ref, d_attn)
        lane = lax.broadcasted_iota(jnp.int32, (1, LANES), 1)
        dsink = jnp.zeros((1, LANES), F32)
        dq_rot, dkk, dvv = [], [], []
        for g in range(N_KV_HEADS):
            group = range(g * q_per_kv, (g + 1) * q_per_kv)
            q_all, sink_all = _stack_group(q_heads, sink_ref, group)
            probs, p_sink = _softmax_with_sink(q_all, kk[g], sink_all, mask)
            do_all = jnp.concatenate([do_ref[:, h * HEAD_DIM:(h + 1) * HEAD_DIM] for h in group], axis=0).astype(BF16)
            dp = _dot(do_all, vv[g], 1, 1)
            delta = jnp.sum(probs * dp, axis=-1, keepdims=True)
            ds = (probs * (dp - delta) * (1.0 / math.sqrt(HEAD_DIM))).astype(BF16)
            dq_all = _dot(ds, kk[g], 1, 0)
            dkk.append(_dot(ds, q_all, 0, 0))
            dvv.append(_dot(probs.astype(BF16), do_all, 0, 0))
            sink_term = p_sink * delta
            for k, h in enumerate(group):
                dq_rot.append(dq_all[k * BLOCK:(k + 1) * BLOCK])
                part = jnp.sum(sink_term[k * BLOCK:(k + 1) * BLOCK], axis=0, keepdims=True)
                dsink += jnp.where(lane == h, -part, 0.0)
        cq, sq, cp, sp = cq_ref[...], sq_ref[...], cp_ref[...], sp_ref[...]
        for j in range(d_attn // LANES):
            d = jnp.concatenate(dq_rot[j * per:(j + 1) * per], axis=1)
            dq_ref[:, j * LANES:(j + 1) * LANES] = _unrope(d, cq, sq)
        for j in range(D_KV // LANES):
            d = jnp.concatenate(dkk[j * per:(j + 1) * per], axis=1)
            dk_ref[0, :, j * LANES:(j + 1) * LANES] = _unrope(d[:BLOCK], cp, sp)
            dk_ref[1, :, j * LANES:(j + 1) * LANES] = _unrope(d[BLOCK:], cq, sq)
            d = jnp.concatenate(dvv[j * per:(j + 1) * per], axis=1)
            dv_ref[0, :, j * LANES:(j + 1) * LANES] = d[:BLOCK]
            dv_ref[1, :, j * LANES:(j + 1) * LANES] = d[BLOCK:]

        @pl.when(i == 0)
        def _():
            dsink_ref[...] = jnp.zeros_like(dsink_ref)

        dsink_ref[...] += dsink

    pair = pl.BlockSpec((2, BLOCK, D_KV), lambda i: (i, 0, 0))
    return _call(body, name="attention_bwd", grid=(nb,),
                 in_specs=_attn_specs(t, d_attn, d_in) + [pl.BlockSpec((BLOCK, d_attn), lambda i: (i, 0))],
                 out_specs=[pl.BlockSpec((BLOCK, d_attn), lambda i: (i, 0)), pair, pair,
                            pl.BlockSpec((1, LANES), lambda i: (0, 0))],
                 out_shape=[_sds((t, d_attn), F32), _sds((2 * nb, BLOCK, D_KV), F32), _sds((2 * nb, BLOCK, D_KV), F32),
                            _sds((1, LANES), F32)],
                 semantics=("arbitrary",))(proj, proj, proj, proj, proj, cos, sin, cos, sin, sinks_row, d_out)


def _assemble_dproj(dq, dk2, dv2, du, d_in, after):
    t, d_attn = dq.shape
    d_ssm = du.shape[1]
    nb = t // BLOCK

    def body(dq_ref, dk_own, dk_next, dv_own, dv_next, du_ref, o_ref):
        has_next = (pl.program_id(0) < nb - 1).astype(F32)
        o_ref[:, :d_attn] = dq_ref[...].astype(BF16)
        o_ref[:, d_attn:d_attn + D_KV] = (dk_own[...] + has_next * dk_next[...]).astype(BF16)
        o_ref[:, d_attn + D_KV:d_attn + 2 * D_KV] = (dv_own[...] + has_next * dv_next[...]).astype(BF16)
        o_ref[:, d_attn + 2 * D_KV:] = du_ref[...].astype(BF16)

    own = pl.BlockSpec((None, BLOCK, D_KV), lambda i: (2 * i + 1, 0, 0))
    nxt = pl.BlockSpec((None, BLOCK, D_KV), lambda i: (jnp.minimum(2 * i + 2, 2 * nb - 1), 0, 0))
    return _call(body, name="assemble_dproj", grid=(nb,),
                 in_specs=[pl.BlockSpec((BLOCK, d_attn), lambda i: (i, 0)), own, nxt, own, nxt,
                           pl.BlockSpec((BLOCK, d_ssm), lambda i: (i, 0))],
                 out_specs=pl.BlockSpec((BLOCK, d_in), lambda i: (i, 0)), out_shape=_sds((t, d_in), BF16),
                 semantics=("parallel",), n_after=len(after))(dq, dk2, dk2, dv2, dv2, du, *after)


def _discretise(ar, ai, ldt, br, bi):
    dt = jnp.exp(ldt)
    mag = jnp.exp(ar * dt)
    lam_re = mag * jnp.cos(ai * dt)
    lam_im = mag * jnp.sin(ai * dt)
    den = ar * ar + ai * ai
    nr = lam_re - 1.0
    ni = lam_im
    f_re = (nr * ar + ni * ai) / den
    f_im = (ni * ar - nr * ai) / den
    return (lam_re, lam_im, [f_re * r - f_im * i for r, i in zip(br, bi)], [f_re * i + f_im * r for r, i in zip(br, bi)])


def _whole(arrays):
    return [pl.BlockSpec(a.shape, lambda *_, nd=len(a.shape): (0,) * nd) for a in arrays]


def _channels(ref):
    groups = ref.shape[0] // SSM_GROUP
    return [ref[pl.ds(p, groups, stride=SSM_GROUP), :] for p in range(SSM_GROUP)]


def _store_channels(ref, values):
    groups = ref.shape[0] // SSM_GROUP
    for p, val in enumerate(values):
        ref[pl.ds(p, groups, stride=SSM_GROUP), :] = val


def _s5_discretise(ar, ai, ldt, br, bi):
    ins = [ar, ai, ldt, br, bi]

    def body(ar_ref, ai_ref, ldt_ref, br_ref, bi_ref, lr_ref, li_ref, bbr_ref, bbi_ref):
        lr, li, bbr, bbi = _discretise(ar_ref[...], ai_ref[...], ldt_ref[...], _channels(br_ref), _channels(bi_ref))
        lr_ref[...] = lr
        li_ref[...] = li
        _store_channels(bbr_ref, bbr)
        _store_channels(bbi_ref, bbi)

    outs = [_sds(ar.shape, F32), _sds(ar.shape, F32), _sds(br.shape, F32), _sds(br.shape, F32)]
    return _call(body, name="s5_discretise", in_specs=_whole(ins), out_specs=_whole(outs), out_shape=outs)(*ins)


def _s5_discretise_bwd(ar, ai, ldt, br, bi, d_lr, d_li, d_bbr, d_bbi):
    ins = [ar, ai, ldt, br, bi, d_lr, d_li, d_bbr, d_bbi]

    def body(ar_ref, ai_ref, ldt_ref, br_ref, bi_ref, dlr_ref, dli_ref, dbbr_ref, dbbi_ref,
             dar_ref, dai_ref, dldt_ref, dbr_ref, dbi_ref):
        _, vjp = jax.vjp(_discretise, ar_ref[...], ai_ref[...], ldt_ref[...], _channels(br_ref), _channels(bi_ref))
        dar, dai, dldt, dbr, dbi = vjp((dlr_ref[...], dli_ref[...], _channels(dbbr_ref), _channels(dbbi_ref)))
        dar_ref[...] = dar
        dai_ref[...] = dai
        dldt_ref[...] = dldt
        _store_channels(dbr_ref, dbr)
        _store_channels(dbi_ref, dbi)

    outs = [_sds(a.shape, F32) for a in (ar, ai, ldt, br, bi)]
    return _call(body, name="s5_discretise_bwd", in_specs=_whole(ins), out_specs=_whole(outs), out_shape=outs)(*ins)


def _cmul(ar, ai, br, bi):
    return ar * br - ai * bi, ar * bi + ai * br


def _load_segmented(ref, tile0, n_tiles, seg):
    return jnp.concatenate([ref[pl.ds(tile0 + j, SUBLANES, stride=seg), :] for j in range(n_tiles)], axis=0)


def _store_segmented(ref, tile0, seg, value):
    for j in range(value.shape[0] // SUBLANES):
        ref[pl.ds(tile0 + j, SUBLANES, stride=seg), :] = value[j * SUBLANES:(j + 1) * SUBLANES, :]


def _fill_powers(lr, li, pr_ref, pi_ref, seg):
    pows = [(lr, li)]
    for _ in range(SUBLANES - 1):
        pows.append(_cmul(pows[-1][0], pows[-1][1], lr, li))
    row = lax.broadcasted_iota(jnp.int32, (SUBLANES, lr.shape[1]), 0)
    tr = jnp.zeros((SUBLANES, lr.shape[1]), F32)
    ti = jnp.zeros((SUBLANES, lr.shape[1]), F32)
    for r in range(SUBLANES):
        tr = jnp.where(row == r, pows[r][0], tr)
        ti = jnp.where(row == r, pows[r][1], ti)
    pr_ref[0:SUBLANES, :] = tr
    pi_ref[0:SUBLANES, :] = ti
    k = SUBLANES
    while k < seg:
        fr, fi = pr_ref[k - 1:k, :], pi_ref[k - 1:k, :]
        for t0 in range(0, k, SUBLANES):
            nr, ni = _cmul(pr_ref[t0:t0 + SUBLANES, :], pi_ref[t0:t0 + SUBLANES, :], fr, fi)
            pr_ref[k + t0:k + t0 + SUBLANES, :] = nr
            pi_ref[k + t0:k + t0 + SUBLANES, :] = ni
        k *= 2


def _scan_segments(sr_ref, si_ref, pr_ref, pi_ref, lr, li, seg, reverse, per_tile=None):
    w = lr.shape[1]
    sign = -1.0 if reverse else 1.0
    lrb = jnp.broadcast_to(lr, (SUBLANES, w))
    lib = jnp.broadcast_to(sign * li, (SUBLANES, w))
    zero = jnp.zeros((SUBLANES, w), F32)

    def tile_rows(j):
        return pl.ds(pl.multiple_of(j * SUBLANES, SUBLANES), SUBLANES)

    def local(i, carry):
        rows = tile_rows(seg - 1 - i if reverse else i)
        pr, pi = _cmul(lrb, lib, carry[0], carry[1])
        xr, xi = sr_ref[rows, :] + pr, si_ref[rows, :] + pi
        sr_ref[rows, :] = xr
        si_ref[rows, :] = xi
        return xr, xi

    end_r, end_i = lax.fori_loop(0, seg, local, (zero, zero))
    full_r, full_i = pr_ref[seg - 1:seg, :], sign * pi_ref[seg - 1:seg, :]
    row = lax.broadcasted_iota(jnp.int32, (SUBLANES, w), 0)
    in_r, in_i = zero, zero
    cur_r, cur_i = jnp.zeros((1, w), F32), jnp.zeros((1, w), F32)
    for r in (range(SUBLANES - 2, -1, -1) if reverse else range(1, SUBLANES)):
        src = r + 1 if reverse else r - 1
        pr, pi = _cmul(full_r, full_i, cur_r, cur_i)
        cur_r, cur_i = end_r[src:src + 1, :] + pr, end_i[src:src + 1, :] + pi
        in_r = jnp.where(row == r, cur_r, in_r)
        in_i = jnp.where(row == r, cur_i, in_i)

    def carry_in(j, _):
        rows = tile_rows(j)
        k = seg - 1 - j if reverse else j
        pr, pi = _cmul(pr_ref[pl.ds(k, 1), :], sign * pi_ref[pl.ds(k, 1), :], in_r, in_i)
        xr, xi = sr_ref[rows, :] + pr, si_ref[rows, :] + pi
        sr_ref[rows, :] = xr
        si_ref[rows, :] = xi
        if per_tile is not None:
            per_tile(j, xr, xi)
        return 0

    lax.fori_loop(0, seg, carry_in, 0)


_S5_ROWS = 256


def _s5_in_specs(t, d_attn):
    u_block = (d_attn + 2 * D_KV) // SSM_CH_BLOCK
    blk3 = lambda shape: pl.BlockSpec((None,) + shape, lambda j: (j, 0, 0))
    return [
        pl.BlockSpec((t, SSM_CH_BLOCK), lambda j: (0, u_block + j)),
        blk3((SSM_CH_BLOCK, SSM_ST_BLOCK)), blk3((SSM_CH_BLOCK, SSM_ST_BLOCK)),
        blk3((1, SSM_ST_BLOCK)), blk3((1, SSM_ST_BLOCK)),
        blk3((SSM_ST_BLOCK, SSM_CH_BLOCK)), blk3((SSM_ST_BLOCK, SSM_CH_BLOCK)),
        pl.BlockSpec((1, SSM_CH_BLOCK), lambda j: (0, j)),
    ]


def _chunks(t):
    rows = min(_S5_ROWS, t)
    return rows, lambda i: pl.ds(pl.multiple_of(i * rows, rows), rows)


def _s5_states(u_ref, us_ref, bre_ref, bim_ref, lr_ref, li_ref, sr_ref, si_ref, pr_ref, pi_ref, t):
    seg = t // SUBLANES
    rows, chunk = _chunks(t)
    for c in range(t // rows):
        us_ref[c * rows:(c + 1) * rows, :] = _load_segmented(u_ref, c * rows // SUBLANES, rows // SUBLANES, seg)

    def fill(i, _):
        ub = us_ref[chunk(i), :].astype(BF16)
        sr_ref[chunk(i), :] = _dot(ub, bre_ref[...], 1, 0)
        si_ref[chunk(i), :] = _dot(ub, bim_ref[...], 1, 0)
        return 0

    lax.fori_loop(0, t // rows, fill, 0)
    _fill_powers(lr_ref[...], li_ref[...], pr_ref, pi_ref, seg)
    _scan_segments(sr_ref, si_ref, pr_ref, pi_ref, lr_ref[...], li_ref[...], seg, False)


def _s5_scratch(t):
    state = pltpu.VMEM((t, SSM_ST_BLOCK), F32)
    powers = pltpu.VMEM((t // SUBLANES, SSM_ST_BLOCK), F32)
    return state, powers, pltpu.VMEM((t, SSM_CH_BLOCK), F32)


def _s5_fwd(proj, mats, dskip_row, d_attn, d_ssm):
    t = proj.shape[0]
    seg = t // SUBLANES
    n_blocks = d_ssm // SSM_CH_BLOCK
    rows, chunk = _chunks(t)

    def body(u_ref, bre_ref, bim_ref, lr_ref, li_ref, cre_ref, cim_ref, d_ref, y_ref,
             sr_ref, si_ref, pr_ref, pi_ref, us_ref, ys_ref):
        _s5_states(u_ref, us_ref, bre_ref, bim_ref, lr_ref, li_ref, sr_ref, si_ref, pr_ref, pi_ref, t)

        def emit(i, _):
            ys_ref[chunk(i), :] = (_dot(sr_ref[chunk(i), :].astype(BF16), cre_ref[...], 1, 0)
                                   - _dot(si_ref[chunk(i), :].astype(BF16), cim_ref[...], 1, 0)
                                   + d_ref[...] * us_ref[chunk(i), :])
            return 0

        lax.fori_loop(0, t // rows, emit, 0)
        for c in range(t // rows):
            _store_segmented(y_ref, c * rows // SUBLANES, seg, ys_ref[c * rows:(c + 1) * rows, :])

    state, powers, channels = _s5_scratch(t)
    col = pl.BlockSpec((t, SSM_CH_BLOCK), lambda j: (0, j))
    return _call(body, name="s5_fwd", grid=(n_blocks,), in_specs=_s5_in_specs(t, d_attn), out_specs=col,
                 out_shape=_sds((t, d_ssm), F32), scratch_shapes=[state, state, powers, powers, channels, channels],
                 semantics=("parallel",))(proj, *mats, dskip_row)


def _s5_bwd(proj, mats, dskip_row, y, dz_a, dz_b, d_attn, d_ssm):
    t = proj.shape[0]
    seg = t // SUBLANES
    n_blocks = d_ssm // SSM_CH_BLOCK
    rows, chunk = _chunks(t)

    def body(u_ref, bre_ref, bim_ref, lr_ref, li_ref, cre_ref, cim_ref, d_ref, y_ref, dza_ref, dzb_ref,
             du_ref, dbre_ref, dbim_ref, dlr_ref, dli_ref, dcre_ref, dcim_ref, dd_ref,
             sr_ref, si_ref, gr_ref, gi_ref, pr_ref, pi_ref, us_ref, dys_ref, dus_ref, acc_r, acc_i):
        _s5_states(u_ref, us_ref, bre_ref, bim_ref, lr_ref, li_ref, sr_ref, si_ref, pr_ref, pi_ref, t)
        for ref in (dcre_ref, dcim_ref, dbre_ref, dbim_ref, dd_ref, acc_r, acc_i):
            ref[...] = jnp.zeros_like(ref)
        for c in range(t // rows):
            tile0, n_tiles = c * rows // SUBLANES, rows // SUBLANES
            dz = _load_segmented(dza_ref, tile0, n_tiles, seg) + _load_segmented(dzb_ref, tile0, n_tiles, seg)
            dys_ref[c * rows:(c + 1) * rows, :] = dz * _gelu_grad(_load_segmented(y_ref, tile0, n_tiles, seg))

        def through_c(i, _):
            dy = dys_ref[chunk(i), :]
            dd_ref[...] += jnp.sum(dy * us_ref[chunk(i), :], axis=0, keepdims=True)
            dyb = dy.astype(BF16)
            gr_ref[chunk(i), :] = _dot(dyb, cre_ref[...], 1, 1)
            gi_ref[chunk(i), :] = -_dot(dyb, cim_ref[...], 1, 1)
            dcre_ref[...] += _dot(sr_ref[chunk(i), :].astype(BF16), dyb, 0, 0)
            dcim_ref[...] -= _dot(si_ref[chunk(i), :].astype(BF16), dyb, 0, 0)
            return 0

        lax.fori_loop(0, t // rows, through_c, 0)

        row = lax.broadcasted_iota(jnp.int32, (SUBLANES, SSM_ST_BLOCK), 0)
        last = pl.ds((seg - 1) * SUBLANES, SUBLANES)
        wrap = [jnp.where(row == 0, 0.0, pltpu.roll(ref[last, :], 1, 0)) for ref in (sr_ref, si_ref)]

        def lambda_grad(j, g_re, g_im):
            before = pl.ds(pl.multiple_of(jnp.maximum(j - 1, 0) * SUBLANES, SUBLANES), SUBLANES)
            prev_r = jnp.where(j > 0, sr_ref[before, :], wrap[0])
            prev_i = jnp.where(j > 0, si_ref[before, :], wrap[1])
            acc_r[...] += g_re * prev_r + g_im * prev_i
            acc_i[...] += g_im * prev_r - g_re * prev_i

        _scan_segments(gr_ref, gi_ref, pr_ref, pi_ref, lr_ref[...], li_ref[...], seg, True, per_tile=lambda_grad)
        dlr_ref[...] = jnp.sum(acc_r[...], axis=0, keepdims=True)
        dli_ref[...] = jnp.sum(acc_i[...], axis=0, keepdims=True)

        def through_b(i, _):
            ub = us_ref[chunk(i), :].astype(BF16)
            grb, gib = gr_ref[chunk(i), :].astype(BF16), gi_ref[chunk(i), :].astype(BF16)
            dbre_ref[...] += _dot(ub, grb, 0, 0)
            dbim_ref[...] += _dot(ub, gib, 0, 0)
            dus_ref[chunk(i), :] = (_dot(grb, bre_ref[...], 1, 1) + _dot(gib, bim_ref[...], 1, 1)
                                    + d_ref[...] * dys_ref[chunk(i), :])
            return 0

        lax.fori_loop(0, t // rows, through_b, 0)
        for c in range(t // rows):
            _store_segmented(du_ref, c * rows // SUBLANES, seg, dus_ref[c * rows:(c + 1) * rows, :])

    col = pl.BlockSpec((t, SSM_CH_BLOCK), lambda j: (0, j))
    blk3 = lambda shape: pl.BlockSpec((None,) + shape, lambda j: (j, 0, 0))
    state, powers, channels = _s5_scratch(t)
    return _call(
        body, name="s5_bwd", grid=(n_blocks,), in_specs=_s5_in_specs(t, d_attn) + [col, col, col],
        out_specs=[col, blk3((SSM_CH_BLOCK, SSM_ST_BLOCK)), blk3((SSM_CH_BLOCK, SSM_ST_BLOCK)),
                   blk3((1, SSM_ST_BLOCK)), blk3((1, SSM_ST_BLOCK)),
                   blk3((SSM_ST_BLOCK, SSM_CH_BLOCK)), blk3((SSM_ST_BLOCK, SSM_CH_BLOCK)),
                   pl.BlockSpec((1, SSM_CH_BLOCK), lambda j: (0, j))],
        out_shape=[_sds((t, d_ssm), F32),
                   _sds((n_blocks, SSM_CH_BLOCK, SSM_ST_BLOCK), F32), _sds((n_blocks, SSM_CH_BLOCK, SSM_ST_BLOCK), F32),
                   _sds((n_blocks, 1, SSM_ST_BLOCK), F32), _sds((n_blocks, 1, SSM_ST_BLOCK), F32),
                   _sds((n_blocks, SSM_ST_BLOCK, SSM_CH_BLOCK), F32), _sds((n_blocks, SSM_ST_BLOCK, SSM_CH_BLOCK), F32),
                   _sds((1, d_ssm), F32)],
        scratch_shapes=[state, state, state, state, powers, powers, channels, channels, channels,
                        pltpu.VMEM((SUBLANES, SSM_ST_BLOCK), F32), pltpu.VMEM((SUBLANES, SSM_ST_BLOCK), F32)],
        semantics=("parallel",))(proj, *mats, dskip_row, y, dz_a, dz_b)


def _by_block(gp_n):
    return gp_n.reshape(-1, GROUPS_PER_BLOCK, SSM_GROUP, SSM_STATE)


def _block_diag_in(bbar):
    eye = jnp.eye(GROUPS_PER_BLOCK, dtype=F32)
    return jnp.einsum("jgpn,gh->jgphn", _by_block(bbar), eye).reshape(-1, SSM_CH_BLOCK, SSM_ST_BLOCK)


def _block_diag_in_t(dense):
    d5 = dense.reshape(-1, GROUPS_PER_BLOCK, SSM_GROUP, GROUPS_PER_BLOCK, SSM_STATE)
    eye = jnp.eye(GROUPS_PER_BLOCK, dtype=F32)
    return jnp.einsum("jgphn,gh->jgpn", d5, eye).reshape(-1, SSM_STATE)


def _block_diag_out(c):
    eye = jnp.eye(GROUPS_PER_BLOCK, dtype=F32)
    return jnp.einsum("jgpn,gh->jgnhp", _by_block(c), eye).reshape(-1, SSM_ST_BLOCK, SSM_CH_BLOCK)


def _block_diag_out_t(dense):
    d5 = dense.reshape(-1, GROUPS_PER_BLOCK, SSM_STATE, GROUPS_PER_BLOCK, SSM_GROUP)
    eye = jnp.eye(GROUPS_PER_BLOCK, dtype=F32)
    return jnp.einsum("jgnhp,gh->jgpn", d5, eye).reshape(-1, SSM_STATE)


def _adamw(w, g, m, v):
    m = ADAM_B1 * m + (1.0 - ADAM_B1) * g
    v = ADAM_B2 * v + (1.0 - ADAM_B2) * (g * g)
    m_hat = m / (1.0 - ADAM_B1 ** ADAM_STEP)
    v_hat = v / (1.0 - ADAM_B2 ** ADAM_STEP)
    delta = -ADAM_LR * (m_hat / (jnp.sqrt(v_hat) + ADAM_EPS) + ADAM_WD * w)
    return delta, m, v


def _adam_sharded(name, parts, w, m, v, tr, row0=0):
    r, c = w.shape
    assert r % tr == 0 and row0 % tr == 0, (name, r, tr, row0)

    def body(p_ref, w_ref, m_ref, v_ref, g_out, d_out, m_out, v_out):
        g = p_ref[0].astype(F32)
        for i in range(1, p_ref.shape[0]):
            g = g + p_ref[i].astype(F32)
        delta, m_new, v_new = _adamw(w_ref[...], g, m_ref[...], v_ref[...])
        g_out[...] = g
        d_out[...] = delta
        m_out[...] = m_new
        v_out[...] = v_new

    tile = pl.BlockSpec((tr, c), lambda i: (i, 0))
    return _call(body, name=name, grid=(r // tr,),
                 in_specs=[pl.BlockSpec((parts.shape[0], tr, c), lambda i: (0, i + row0 // tr, 0)), tile, tile, tile],
                 out_specs=[tile] * 4, out_shape=[_sds((r, c), F32)] * 4, semantics=("parallel",))(parts, w, m, v)


_BIG = ("w_in", "w_glu", "w_o", "w_gate", "w_up", "w_down")
_BY_COLUMNS = ("w_in", "w_gate", "w_up")
_SMALL_VECTORS = ("sinks", "log_dt", "b_glu", "g_attn_out", "g_ssm_out", "g_post_mix", "g_pre_ffn", "g_post_ffn")
_SMALL_MATRICES = ("b_re", "b_im", "c_re", "c_im", "a_re", "a_im")
_ORDER = ("g_pre_mix", "w_in", "sinks", "a_re", "a_im", "log_dt", "b_re", "b_im", "c_re", "c_im", "d_skip", "w_glu",
          "b_glu", "g_attn_out", "g_ssm_out", "w_o", "g_post_mix", "g_pre_ffn", "w_gate", "w_up", "w_down",
          "g_post_ffn")


def _pack_grads(vectors, matrices):
    width = max(a.shape[1] for a in vectors)
    slots, row, lane = [], 0, 0
    for a in vectors:
        span = -(-a.shape[1] // LANES) * LANES
        if lane + span > width:
            row, lane = row + 1, 0
        slots.append((row, lane, a.shape[1]))
        lane += span
    firsts, at = [], 0
    for a in matrices:
        firsts.append(at)
        at += a.shape[0]
    nv = len(vectors)

    def body(*refs):
        vec_out, mat_out = refs[-2], refs[-1]
        vec_out[...] = jnp.zeros_like(vec_out)
        for ref, (r, l, w) in zip(refs[:nv], slots):
            vec_out[r:r + 1, l:l + w] = ref[...]
        for ref, r0 in zip(refs[nv:-2], firsts):
            mat_out[r0:r0 + ref.shape[0], :] = ref[...]

    ins = list(vectors) + list(matrices)
    outs = [_sds((-(-(row + 1) // SUBLANES) * SUBLANES, width), F32), _sds((at, matrices[0].shape[1]), F32)]
    vec_pack, mat_pack = _call(body, name="pack_small_grads", in_specs=_whole(ins), out_specs=_whole(outs),
                               out_shape=outs)(*ins)
    return vec_pack, slots, mat_pack, firsts


def _adam_replicated(sources, found_at, w, m, v):
    ns, n = len(sources), len(w)

    def body(*refs):
        ins, outs = refs[ns:ns + 3 * n], refs[ns + 3 * n:]
        summed = []
        for p_ref in refs[:ns]:
            g = p_ref[0]
            for k in range(1, N_DEV):
                g = g + p_ref[k]
            summed.append(g)
        for i, (src, row, lane) in enumerate(found_at):
            w_ref, m_ref, v_ref = ins[i], ins[n + i], ins[2 * n + i]
            rows, cols = w_ref.shape
            g = summed[src][row:row + rows, lane:lane + cols]
            delta, m_new, v_new = _adamw(w_ref[...], g, m_ref[...], v_ref[...])
            for o, val in zip(outs[4 * i:4 * i + 4], (g, delta, m_new, v_new)):
                o[...] = val

    ins = list(sources) + list(w) + list(m) + list(v)
    outs = [_sds(a.shape, F32) for a in w for _ in range(4)]
    flat = _call(body, name="adam_replicated", in_specs=_whole(ins), out_specs=_whole(outs), out_shape=outs)(*ins)
    return [tuple(flat[4 * i:4 * i + 4]) for i in range(n)]


def kernel(x, positions, g_pre_mix, w_in, sinks, a_re, a_im, log_dt, b_re, b_im, c_re, c_im, d_skip, w_glu, b_glu, g_attn_out, g_ssm_out, w_o, g_post_mix, g_pre_ffn, w_gate, w_up, w_down, g_post_ffn, loss_target, m_g_pre_mix, m_w_in, m_sinks, m_a_re, m_a_im, m_log_dt, m_b_re, m_b_im, m_c_re, m_c_im, m_d_skip, m_w_glu, m_b_glu, m_g_attn_out, m_g_ssm_out, m_w_o, m_g_post_mix, m_g_pre_ffn, m_w_gate, m_w_up, m_w_down, m_g_post_ffn, v_g_pre_mix, v_w_in, v_sinks, v_a_re, v_a_im, v_log_dt, v_b_re, v_b_im, v_c_re, v_c_im, v_d_skip, v_w_glu, v_b_glu, v_g_attn_out, v_g_ssm_out, v_w_o, v_g_post_mix, v_g_pre_ffn, v_w_gate, v_w_up, v_w_down, v_g_post_ffn):
    given = dict(locals())
    weights = {n: given[n] for n in _ORDER}
    mom_m = {n: given["m_" + n] for n in _ORDER}
    mom_v = {n: given["v_" + n] for n in _ORDER}

    t, d = x.shape[1], x.shape[2]
    d_attn = d // 2
    d_ssm = d - d_attn
    d_in = d_attn + 2 * D_KV + d_ssm
    n_groups = d_ssm // SSM_GROUP
    n_heads = d_attn // HEAD_DIM
    tm = min(256, t)

    x2 = x[0]
    target = loss_target[0]

    def by_rows(n, a):
        return a[0].T if n in _BY_COLUMNS else a[0]

    def start_gather(name, ns, token):
        behind = 0 if token is None else token[0, 0].astype(BF16)
        shards = [by_rows(n, weights[n]).astype(BF16) + behind for n in ns]
        return _exchange_start(name, shards, False, (SIBLING,) + CHIP_PEERS)

    def finish_gather(handle, after):
        forward, _ = _forward_start(handle["name"] + "_forward", _exchange_wait(handle, after))
        return _split_wait(forward, [])

    ag_in, token = start_gather("gather_w_in", ["w_in"], None)
    ag_mix, token = start_gather("gather_w_glu_o", ["w_glu", "w_o"], token)
    ag_ffn_in, token = start_gather("gather_w_gate_up", ["w_gate", "w_up"], token)
    ag_down, token = start_gather("gather_w_down", ["w_down"], token)

    xn, = _rows("norm_in", lambda xv, g: ([_rms(xv)[0] * g], []), [x2], [g_pre_mix], [(d, BF16)], [], tm,
                after=[token])
    win_g, = finish_gather(ag_in, [xn])
    w_in_t = win_g.reshape(d_in, d)
    proj = _mm_nt("proj_in", xn, w_in_t, F32, tn=d_in // 4 if (d_in // 4) % LANES == 0 else None)

    cos, sin = _rope_tables(positions.reshape(t, 1).astype(F32))
    sinks_row = jnp.pad(sinks, ((0, 0), (0, LANES - n_heads)))
    attn = _attention_fwd(proj, cos, sin, sinks_row, d_attn)

    def view(n, a):
        if n in ("b_re", "b_im"):
            return jnp.transpose(a[0], (0, 2, 1)).reshape(-1, SSM_STATE)
        if n in ("c_re", "c_im"):
            return a[0].reshape(-1, SSM_STATE)
        return a[0].T if n == "d_skip" else a[0] if a.ndim == 3 else a

    def unview(n, val):
        if n in ("b_re", "b_im"):
            return jnp.transpose(val.reshape(n_groups, SSM_GROUP, SSM_STATE), (0, 2, 1))[None]
        if n in ("c_re", "c_im"):
            return val.reshape(1, n_groups, SSM_GROUP, SSM_STATE)
        return val.T[None] if n == "d_skip" else val[None] if weights[n].ndim == 3 else val

    b_re_v, b_im_v = view("b_re", b_re), view("b_im", b_im)
    ldt_col = log_dt.reshape(n_groups, 1)
    lam_re, lam_im, bbar_re, bbar_im = _s5_discretise(a_re[0], a_im[0], ldt_col, b_re_v, b_im_v)
    n_blocks = n_groups // GROUPS_PER_BLOCK
    mats = [_block_diag_in(bbar_re).astype(BF16), _block_diag_in(bbar_im).astype(BF16),
            lam_re.reshape(n_blocks, 1, SSM_ST_BLOCK), lam_im.reshape(n_blocks, 1, SSM_ST_BLOCK),
            _block_diag_out(view("c_re", c_re)).astype(BF16), _block_diag_out(view("c_im", c_im)).astype(BF16)]
    dskip_row = d_skip.reshape(1, d_ssm)
    y_ssm = _s5_fwd(proj, mats, dskip_row, d_attn, d_ssm)
    gelu_bf16 = lambda yv: _gelu(yv).astype(BF16)
    wglu_g, wo_g = finish_gather(ag_mix, [attn, y_ssm])
    w_glu_full = wglu_g.reshape(d_ssm, d_ssm)
    w_o_full = wo_g.reshape(d, d)
    glu_lin = _mm_nn("glu_gate", y_ssm, w_glu_full, F32, a_fn=gelu_bf16)

    def mix_prep(av, yv, gl, bg, ga, gs):
        ssm = _gelu(yv) * _sigmoid(gl + bg)
        return [jnp.concatenate([_rms(av)[0] * ga, _rms(ssm)[0] * gs], axis=1)], []

    mixed, = _rows("mix_prep", mix_prep, [attn, y_ssm, glu_lin], [b_glu, g_attn_out, g_ssm_out], [(d, BF16)], [], tm)
    mix = _mm_nn("mix_out", mixed, w_o_full, F32, tn=d // 2 if (d // 2) % LANES == 0 else None)

    def post_mix(xv, mv, gpm, gpf):
        h = xv + _rms(mv)[0] * gpm
        return [h, _rms(h)[0] * gpf], []

    h, hn = _rows("post_mix", post_mix, [x2, mix], [g_post_mix, g_pre_ffn], [(d, F32), (d, BF16)], [], tm)
    wgate_g, wup_g = finish_gather(ag_ffn_in, [hn])
    gate, up, hid = _ffn_in(hn, wgate_g, wup_g)
    wdown_g, = finish_gather(ag_down, [hid])
    ff = _mm_contract_slots("ffn_down", [(hid, wdown_g)], F32, per_step=4)

    def head(hv, fv, tv, gpo):
        out = hv + _rms(fv)[0] * gpo
        err = out - tv
        dout = err * (1.0 / d)
        dff, dg = _rms_bwd(fv, gpo, dout)
        loss = jnp.zeros((1, LANES), F32) + 0.5 * jnp.sum(err * err) * (1.0 / d)
        return [dff, dout], [dg, loss]

    dff, dh_out, dg_post_ffn, loss_row = _rows("loss_head", head, [h, ff, target], [g_post_ffn],
                                               [(d, BF16), (d, F32)], [d, LANES], tm)

    def swap_halves(name, grads):
        return _halves_start("swap_" + name, [g.reshape(N_DEV // 2, 2, *g.shape[1:]) for g in grads])

    def scatter_chip_sums(name, swap, after):
        both = _split_wait(swap, after)
        half = len(both) // 2
        sums = [_chip_sum("chip_sum_%s_%d" % (name, i), both[i], both[half + i]) for i in range(half)]
        return _exchange_start("scatter_" + name, sums, True, CHIP_PEERS, by_chip=True)

    dw_down = _mm_slots_tn("ffn_down_dw", hid, dff, BF16)
    swap_down, token = swap_halves("dw_down", [dw_down])
    dgate, dup = _ffn_down_bwd(dff, wdown_g, gate, up, [token])
    rs_down, token = scatter_chip_sums("dw_down", swap_down, [dgate])
    dhn = _mm_contract_slots("ffn_in_dx", [(dgate, wgate_g), (dup, wup_g)], F32, per_step=2, after=[token])
    dw_gate = _mm_slots_tn("ffn_gate_dw", dgate, hn, BF16)
    dw_up = _mm_slots_tn("ffn_up_dw", dup, hn, BF16)
    swap_ffn_in, tok_ffn_in = swap_halves("dw_gate_up", [dw_gate, dw_up])

    def mid_bwd(dho, dhn_, hv, mv, gpf, gpm):
        d1, dgpf = _rms_bwd(hv, gpf, dhn_)
        dh_ = dho + d1
        dmix_, dgpm = _rms_bwd(mv, gpm, dh_)
        return [dh_, dmix_], [dgpf, dgpm]

    dh, dmix, dg_pre_ffn, dg_post_mix = _rows("mid_bwd", mid_bwd, [dh_out, dhn, h, mix], [g_pre_ffn, g_post_mix],
                                              [(d, F32), (d, BF16)], [d, d], tm, after=[tok_ffn_in])

    dmixed = _mm_nt("mix_out_dx", dmix, w_o_full, F32, tn=d // 2 if (d // 2) % LANES == 0 else None)
    rs_ffn_in, token = scatter_chip_sums("dw_gate_up", swap_ffn_in, [dmixed])
    dw_o = _mm_tn("mix_out_dw", mixed, dmix, BF16, tn=d // 2 if (d // 2) % LANES == 0 else None, after=[token])
    swap_o, tok_o = swap_halves("dw_o", [dw_o.reshape(N_DEV, d // N_DEV, d)])

    def mix_bwd(dm, av, yv, gl, bg, ga, gs):
        dattn_, dga = _rms_bwd(av, ga, dm[:, :d_attn])
        z = _gelu(yv)
        sg = _sigmoid(gl + bg)
        dssm, dgs = _rms_bwd(z * sg, gs, dm[:, d_attn:])
        dgl = dssm * z * sg * (1.0 - sg)
        return [dattn_, dssm * sg, dgl], [dga, dgs, jnp.sum(dgl, axis=0, keepdims=True)]

    dattn, dz_direct, dglu, dg_attn_out, dg_ssm_out, db_glu = _rows(
        "mix_bwd", mix_bwd, [dmixed, attn, y_ssm, glu_lin], [b_glu, g_attn_out, g_ssm_out],
        [(d_attn, F32), (d_ssm, F32), (d_ssm, BF16)], [d_attn, d_ssm, d_ssm], tm, after=[tok_o])
    dz_glu = _mm_nt("glu_gate_dx", dglu, w_glu_full, F32)
    dw_glu = _mm_tn("glu_gate_dw", y_ssm, dglu, BF16, a_fn=gelu_bf16)
    rs_o, _ = scatter_chip_sums("dw_o", swap_o, [dz_glu, dw_glu])

    du, db_re_dense, db_im_dense, dlam_re, dlam_im, dc_re_dense, dc_im_dense, dd_skip = _s5_bwd(
        proj, mats, dskip_row, y_ssm, dz_direct, dz_glu, d_attn, d_ssm)
    da_re, da_im, dlog_dt, db_re_v, db_im_v = _s5_discretise_bwd(
        a_re[0], a_im[0], ldt_col, b_re_v, b_im_v, dlam_re.reshape(n_groups, SSM_STATE),
        dlam_im.reshape(n_groups, SSM_STATE), _block_diag_in_t(db_re_dense), _block_diag_in_t(db_im_dense))
    dq, dk2, dv2, dsinks_row = _attention_bwd(proj, cos, sin, sinks_row, dattn, d_attn)

    small_grads = {
        "sinks": dsinks_row, "a_re": da_re, "a_im": da_im, "log_dt": dlog_dt.reshape(1, n_groups),
        "b_re": db_re_v, "b_im": db_im_v, "c_re": _block_diag_out_t(dc_re_dense),
        "c_im": _block_diag_out_t(dc_im_dense), "d_skip": dd_skip.reshape(n_groups, SSM_GROUP).T, "b_glu": db_glu,
        "g_attn_out": dg_attn_out, "g_ssm_out": dg_ssm_out, "g_post_mix": dg_post_mix, "g_pre_ffn": dg_pre_ffn,
        "g_post_ffn": dg_post_ffn,
    }
    vec_pack, vec_slots, mat_pack, mat_rows = _pack_grads([small_grads[n] for n in _SMALL_VECTORS],
                                                          [small_grads[n] for n in _SMALL_MATRICES])
    ag_small, token = _exchange_start("gather_small_grads", [vec_pack, mat_pack, small_grads["d_skip"]], False)
    dproj = _assemble_dproj(dq, dk2, dv2, du, d_in, [token])

    dxn = _mm_nn("proj_in_dx", dproj, w_in_t, F32, tn=d // 2 if (d // 2) % LANES == 0 else None)
    dw_in = _mm_tn("proj_in_dw", dproj, xn, BF16).reshape(N_DEV, d_in // N_DEV, d)
    swap_in, token = swap_halves("dw_in_glu", [dw_in, dw_glu.reshape(N_DEV, d_ssm // N_DEV, d_ssm)])

    def x_bwd(dh_, dxn_, xv, g):
        dx, dg = _rms_bwd(xv, g, dxn_)
        return [dh_ + dx], [dg]

    grad_x, dg_pre_mix = _rows("norm_in_bwd", x_bwd, [dh, dxn, x2], [g_pre_mix], [(d, F32)], [d], tm, after=[token])
    ag_last, token = _exchange_start("gather_g_pre_mix_grad", [dg_pre_mix], False)
    rs_in, token = scatter_chip_sums("dw_in_glu", swap_in, [grad_x, token])

    results = {}

    def adam_big(n, parts):
        r = parts.shape[1]
        results[n] = _adam_sharded("adam_" + n, parts, by_rows(n, weights[n]), by_rows(n, mom_m[n]),
                                   by_rows(n, mom_v[n]), 64 if r % 64 == 0 else r)
        return results[n][3]

    done = [grad_x, token]
    adam_big("w_down", _exchange_wait(rs_down, done)[0])
    p_gate, p_up = _exchange_wait(rs_ffn_in, done)
    done = [adam_big("w_gate", p_gate), adam_big("w_up", p_up), results["w_down"][3]]
    done = [adam_big("w_o", _exchange_wait(rs_o, done)[0])]
    p_in, p_glu = _exchange_wait(rs_in, done)
    done = [adam_big("w_in", p_in), adam_big("w_glu", p_glu)]
    vec_parts, mat_parts, dskip_parts = _exchange_wait(ag_small, done)
    first_gain_parts, = _exchange_wait(ag_last, done)
    for n, row0 in zip(_SMALL_MATRICES, mat_rows):
        rows = view(n, weights[n]).shape[0]
        results[n] = _adam_sharded("adam_" + n, mat_parts, view(n, weights[n]), view(n, mom_m[n]), view(n, mom_v[n]),
                                   min(64, rows), row0)
    rest = _SMALL_VECTORS + ("d_skip", "g_pre_mix")
    found_at = [(0, row, lane) for row, lane, _ in vec_slots] + [(1, 0, 0), (2, 0, 0)]
    updated = _adam_replicated([vec_parts, dskip_parts, first_gain_parts], found_at,
                               [view(n, weights[n]) for n in rest], [view(n, mom_m[n]) for n in rest],
                               [view(n, mom_v[n]) for n in rest])
    results.update(zip(rest, updated))

    loss = lax.psum(loss_row[0, 0], ("x", "y", "c"))
    outs = [loss, grad_x[None]]
    for k in range(4):
        for n in _ORDER:
            val = results[n][k]
            outs.append(val.T[None] if n in _BY_COLUMNS else val[None] if n in _BIG else unview(n, val))
    return tuple(outs)
```

```python
import math

import jax
import jax.numpy as jnp
from jax import lax
from jax.experimental import pallas as pl
from jax.experimental.pallas import tpu as pltpu

F32 = jnp.float32
BF16 = jnp.bfloat16

HEAD_DIM = 64
N_KV_HEADS = 4
D_KV = N_KV_HEADS * HEAD_DIM
WINDOW = 128
BLOCK = 128
ROPE_THETA = 10000.0
SSM_GROUP = 16
SSM_STATE = 64
GROUPS_PER_BLOCK = 8
SSM_CH_BLOCK = GROUPS_PER_BLOCK * SSM_GROUP
SSM_ST_BLOCK = GROUPS_PER_BLOCK * SSM_STATE
RMS_EPS = 1e-6
N_DEV = 8
LANES = 128
SUBLANES = 8
MASKED = -1e30

ADAM_LR = 0.001
ADAM_B1 = 0.9
ADAM_B2 = 0.999
ADAM_EPS = 1e-08
ADAM_WD = 0.01
ADAM_STEP = 10

VMEM_LIMIT_BYTES = 56 * 1024 * 1024


def _call(body, *, name, out_shape, in_specs, out_specs, grid=(), scratch_shapes=(), semantics=None, n_after=0):
    params = dict(vmem_limit_bytes=VMEM_LIMIT_BYTES)
    if semantics is not None:
        params["dimension_semantics"] = semantics
    n_in = len(in_specs)
    if n_after:
        inner = body

        def body(*refs):
            inner(*refs[:n_in], *refs[n_in + n_after:])

        in_specs = list(in_specs) + [pl.BlockSpec(memory_space=pl.ANY)] * n_after
    return pl.pallas_call(body, name=name, grid=grid, in_specs=in_specs, out_specs=out_specs, out_shape=out_shape,
                          scratch_shapes=scratch_shapes, compiler_params=pltpu.CompilerParams(**params))


def _sds(shape, dtype):
    return jax.ShapeDtypeStruct(tuple(shape), dtype)


def _dot(a, b, ca, cb):
    return lax.dot_general(a, b, (((ca,), (cb,)), ((), ())), preferred_element_type=F32)


def _rms(x):
    r = lax.rsqrt(jnp.mean(x * x, axis=-1, keepdims=True) + RMS_EPS)
    return x * r, r


def _rms_bwd(x, g, dy):
    xh, r = _rms(x)
    dxh = dy * g
    dx = r * (dxh - xh * jnp.mean(dxh * xh, axis=-1, keepdims=True))
    return dx, jnp.sum(dy * xh, axis=0, keepdims=True)


def _sigmoid(x):
    return 1.0 / (1.0 + jnp.exp(-x))


_GELU_C = math.sqrt(2.0 / math.pi)
_GELU_A = 0.044715


def _gelu(y):
    t = jnp.tanh(_GELU_C * (y + _GELU_A * y * y * y))
    return 0.5 * y * (1.0 + t)


def _gelu_grad(y):
    t = jnp.tanh(_GELU_C * (y + _GELU_A * y * y * y))
    return 0.5 * (1.0 + t) + 0.5 * y * (1.0 - t * t) * _GELU_C * (1.0 + 3.0 * _GELU_A * y * y)


def _rows(name, fn, row_ins, vec_ins, row_outs, acc_widths, tm, after=()):
    rows = row_ins[0].shape[0]
    assert rows % tm == 0, (name, rows, tm)
    n_row, n_vec, n_out, n_acc = len(row_ins), len(vec_ins), len(row_outs), len(acc_widths)

    def body(*refs):
        ins = [r[...] for r in refs[:n_row + n_vec]]
        outs = refs[n_row + n_vec:n_row + n_vec + n_out]
        accs = refs[n_row + n_vec + n_out:]
        row_vals, acc_vals = fn(*ins)
        for o, v in zip(outs, row_vals):
            o[...] = v.astype(o.dtype)
        if n_acc:
            @pl.when(pl.program_id(0) == 0)
            def _():
                for a in accs:
                    a[...] = jnp.zeros_like(a)
            for a, v in zip(accs, acc_vals):
                a[...] += v

    in_specs = [pl.BlockSpec((tm, a.shape[1]), lambda i: (i, 0)) for a in row_ins]
    in_specs += [pl.BlockSpec(v.shape, lambda i: (0, 0)) for v in vec_ins]
    out_specs = [pl.BlockSpec((tm, w), lambda i: (i, 0)) for w, _ in row_outs]
    out_specs += [pl.BlockSpec((1, w), lambda i: (0, 0)) for w in acc_widths]
    out_shape = [_sds((rows, w), dt) for w, dt in row_outs] + [_sds((1, w), F32) for w in acc_widths]
    return _call(body, name=name, grid=(rows // tm,), in_specs=in_specs, out_specs=out_specs, out_shape=out_shape,
                 semantics=("arbitrary",) if n_acc else ("parallel",), n_after=len(after))(*row_ins, *vec_ins, *after)


def _matmul(name, operands, in_specs, product, grid, out_shape, out_spec, acc_shape, after=()):
    nk = grid[-1]
    n_in = len(operands)
    in_place = out_shape.dtype == F32

    def body(*refs):
        ins = [r[...] for r in refs[:n_in]]
        o_ref = refs[n_in]
        if nk == 1:
            o_ref[...] = product(*ins).astype(o_ref.dtype)
            return
        acc = o_ref if in_place else refs[n_in + 1]
        k = pl.program_id(len(grid) - 1)

        @pl.when(k == 0)
        def _():
            acc[...] = jnp.zeros_like(acc)

        acc[...] += product(*ins)

        if not in_place:
            @pl.when(k == nk - 1)
            def _():
                o_ref[...] = acc[...].astype(o_ref.dtype)

    return _call(body, name=name, grid=grid, in_specs=in_specs, out_specs=out_spec, out_shape=out_shape,
                 scratch_shapes=[] if nk == 1 or in_place else [pltpu.VMEM(acc_shape, F32)],
                 semantics=("parallel",) * (len(grid) - 1) + ("arbitrary",), n_after=len(after))(*operands, *after)


def _mm_nn(name, a, b, out_dtype, tm=512, tn=None, a_fn=lambda x: x):
    m, k = a.shape
    n = b.shape[1]
    tm, tn = min(tm, m), n if tn is None else tn
    return _matmul(name, [a, b],
                   [pl.BlockSpec((tm, k), lambda i, j, s: (i, 0)), pl.BlockSpec((k, tn), lambda i, j, s: (0, j))],
                   lambda x, y: _dot(a_fn(x), y, 1, 0), (m // tm, n // tn, 1), _sds((m, n), out_dtype),
                   pl.BlockSpec((tm, tn), lambda i, j, s: (i, j)), (tm, tn))


def _mm_nt(name, a, b, out_dtype, tm=512, tn=None):
    m, k = a.shape
    n = b.shape[0]
    tm, tn = min(tm, m), n if tn is None else tn
    return _matmul(name, [a, b],
                   [pl.BlockSpec((tm, k), lambda i, j, s: (i, 0)), pl.BlockSpec((tn, k), lambda i, j, s: (j, 0))],
                   lambda x, y: _dot(x, y, 1, 1), (m // tm, n // tn, 1), _sds((m, n), out_dtype),
                   pl.BlockSpec((tm, tn), lambda i, j, s: (i, j)), (tm, tn))


def _mm_tn(name, a, b, out_dtype, tm=512, tn=None, tk=2048, a_fn=lambda x: x, after=()):
    k, m = a.shape
    n = b.shape[1]
    tm, tk, tn = min(tm, m), min(tk, k), n if tn is None else tn
    return _matmul(name, [a, b],
                   [pl.BlockSpec((tk, tm), lambda i, j, s: (s, i)), pl.BlockSpec((tk, tn), lambda i, j, s: (s, j))],
                   lambda x, y: _dot(a_fn(x), y, 0, 0), (m // tm, n // tn, k // tk), _sds((m, n), out_dtype),
                   pl.BlockSpec((tm, tn), lambda i, j, s: (i, j)), (tm, tn), after)


def _mm_contract_slots(name, pairs, out_dtype, per_step, tm=512, tn=2048, after=()):
    s_, m, k = pairs[0][0].shape
    n = pairs[0][1].shape[2]
    tm, tn = min(tm, m), min(tn, n)
    ops, specs = [], []
    for a, b in pairs:
        ops += [a, b]
        specs += [pl.BlockSpec((per_step, tm, k), lambda i, j, s: (s, i, 0)),
                  pl.BlockSpec((per_step, k, tn), lambda i, j, s: (s, 0, j))]

    def product(*t):
        return sum(_dot(t[2 * p][q], t[2 * p + 1][q], 1, 0) for p in range(len(pairs)) for q in range(per_step))

    return _matmul(name, ops, specs, product, (m // tm, n // tn, s_ // per_step), _sds((m, n), out_dtype),
                   pl.BlockSpec((tm, tn), lambda i, j, s: (i, j)), (tm, tn), after)


def _mm_slots_tn(name, a, b, out_dtype, tn=2048, tk=2048):
    s_, k, m = a.shape
    n = b.shape[1]
    tn, tk = min(tn, n), min(tk, k)
    return _matmul(name, [a, b],
                   [pl.BlockSpec((None, tk, m), lambda s, j, z: (s, z, 0)), pl.BlockSpec((tk, tn), lambda s, j, z: (z, j))],
                   lambda x, y: _dot(x, y, 0, 0), (s_, n // tn, k // tk), _sds((s_, m, n), out_dtype),
                   pl.BlockSpec((None, m, tn), lambda s, j, z: (s, 0, j)), (m, tn))


def _ffn_in(a, w_gate, w_up, tm=512):
    m, k = a.shape
    s_, n, _ = w_gate.shape
    tm = min(tm, m)

    def body(a_ref, wg_ref, wu_ref, g_ref, u_ref, h_ref):
        x = a_ref[...]
        g = _dot(x, wg_ref[...], 1, 1)
        u = _dot(x, wu_ref[...], 1, 1)
        g_ref[...] = g.astype(BF16)
        u_ref[...] = u.astype(BF16)
        h_ref[...] = (g * _sigmoid(g) * u).astype(BF16)

    w_spec = pl.BlockSpec((None, n, k), lambda s, i: (s, 0, 0))
    o_spec = pl.BlockSpec((None, tm, n), lambda s, i: (s, i, 0))
    return _call(body, name="ffn_in", grid=(s_, m // tm),
                 in_specs=[pl.BlockSpec((tm, k), lambda s, i: (i, 0)), w_spec, w_spec], out_specs=[o_spec] * 3,
                 out_shape=[_sds((s_, m, n), BF16)] * 3, semantics=("parallel", "parallel"))(a, w_gate, w_up)


def _ffn_down_bwd(d_out, w_down, gate, up, after, tm=512):
    m, k = d_out.shape
    s_, n, _ = w_down.shape
    tm = min(tm, m)

    def body(d_ref, w_ref, g_ref, u_ref, dg_ref, du_ref):
        dh = _dot(d_ref[...], w_ref[...], 1, 1)
        g = g_ref[...].astype(F32)
        sg = _sigmoid(g)
        dg_ref[...] = (dh * u_ref[...].astype(F32) * sg * (1.0 + g * (1.0 - sg))).astype(BF16)
        du_ref[...] = (dh * g * sg).astype(BF16)

    t_spec = pl.BlockSpec((None, tm, n), lambda s, i: (s, i, 0))
    return _call(body, name="ffn_down_dx", grid=(s_, m // tm),
                 in_specs=[pl.BlockSpec((tm, k), lambda s, i: (i, 0)), pl.BlockSpec((None, n, k), lambda s, i: (s, 0, 0)),
                           t_spec, t_spec],
                 out_specs=[t_spec] * 2, out_shape=[_sds((s_, m, n), BF16)] * 2, semantics=("parallel", "parallel"),
                 n_after=len(after))(d_out, w_down, gate, up, *after)


ALL_PEERS = (1, 2, 3, 4, 5, 6, 7)
CHIP_PEERS = (2, 4, 6)
SIBLING = 1


def _peer(relation):
    x, y, c = lax.axis_index("x"), lax.axis_index("y"), lax.axis_index("c")
    pos = (1 - x if relation & 4 else x, 1 - y if relation & 2 else y, 1 - c if relation & 1 else c)
    return pos, 4 * pos[0] + 2 * pos[1] + pos[2]


def _slot(relation, by_chip):
    pos, device = _peer(relation)
    return 2 * pos[0] + pos[1] if by_chip else device


def _exchange_copies(ins, lands, send_sems, recv_sems, scatter, relations, by_chip=False):
    me = _slot(0, by_chip)

    def copy(a, s, peer, pos, dst_slot):
        return pltpu.make_async_remote_copy(
            src_ref=ins[a].at[peer] if scatter else ins[a], dst_ref=lands[a].at[dst_slot],
            send_sem=send_sems.at[s], recv_sem=recv_sems.at[s], device_id=pos, device_id_type=pl.DeviceIdType.MESH)

    pairs = []
    for k, r in enumerate(relations):
        pos, peer = _peer(r)[0], _slot(r, by_chip)
        for a in range(len(ins)):
            s = a * len(relations) + k
            pairs.append((copy(a, s, peer, pos, me), copy(a, s, peer, pos, peer)))
    return pairs


def _halves_copies(arrays, lands, send_sems, recv_sems):
    sibling, _ = _peer(SIBLING)
    core = lax.axis_index("c")
    pairs = []
    for a, (ref, land) in enumerate(zip(arrays, lands)):
        send = pltpu.make_async_remote_copy(
            src_ref=ref.at[:, pl.ds(1 - core, 1)], dst_ref=land, send_sem=send_sems.at[a], recv_sem=recv_sems.at[a],
            device_id=sibling, device_id_type=pl.DeviceIdType.MESH)
        pairs.append((send, send))
    return pairs


def _forward_copies(lands, send_sems, recv_sems):
    sibling, _ = _peer(SIBLING)

    def copy(a, s, slot):
        return pltpu.make_async_remote_copy(
            src_ref=lands[a].at[slot], dst_ref=lands[a].at[slot], send_sem=send_sems.at[s], recv_sem=recv_sems.at[s],
            device_id=sibling, device_id_type=pl.DeviceIdType.MESH)

    pairs = []
    for k, r in enumerate(CHIP_PEERS):
        _, mine = _peer(r)
        _, theirs = _peer(r | SIBLING)
        for a in range(len(lands)):
            s = a * len(CHIP_PEERS) + k
            pairs.append((copy(a, s, mine), copy(a, s, theirs)))
    return pairs


_HBM_SPEC = pl.BlockSpec(memory_space=pltpu.HBM)
_SEM_SPEC = pl.BlockSpec(memory_space=pltpu.SEMAPHORE)
_SIDE_EFFECT = pltpu.SideEffectType.DATAFLOW_SIDE_EFFECTING


def _split_start(name, operands, n_sem, make_pairs):
    k = len(operands)

    def body(*refs):
        send_sems, recv_sems, token = refs[k], refs[k + 1], refs[-1]
        for send, _ in make_pairs(refs[:k], send_sems, recv_sems):
            send.start()
        token[...] = jnp.zeros_like(token)

    out = pl.pallas_call(
        body, name=name,
        out_shape=(pltpu.SemaphoreType.DMA((n_sem,)), pltpu.SemaphoreType.DMA((n_sem,)),
                   *[pltpu.HBM(a.shape, a.dtype) for a in operands], _sds((SUBLANES, LANES), F32)),
        in_specs=[_HBM_SPEC] * k,
        out_specs=(_SEM_SPEC, _SEM_SPEC, *[_HBM_SPEC] * k, pl.BlockSpec(memory_space=pltpu.VMEM)),
        input_output_aliases={i: 2 + i for i in range(k)},
        compiler_params=pltpu.CompilerParams(has_side_effects=_SIDE_EFFECT),
    )(*[pltpu.with_memory_space_constraint(a, pltpu.HBM) for a in operands])
    return dict(name=name, sems=out[:2], thru=list(out[2:2 + k]), make_pairs=make_pairs), out[-1]


def _split_wait(handle, after):
    thru, make_pairs = handle["thru"], handle["make_pairs"]
    k = len(thru)

    def body(*refs):
        for send, arrival in make_pairs(refs[:k], refs[k], refs[k + 1]):
            send.wait_send()
            arrival.wait_recv()

    return pl.pallas_call(
        body, name=handle["name"] + "_wait", out_shape=[pltpu.HBM(a.shape, a.dtype) for a in thru],
        in_specs=[_HBM_SPEC] * k + [_SEM_SPEC, _SEM_SPEC] + [pl.BlockSpec(memory_space=pl.ANY)] * len(after),
        out_specs=[_HBM_SPEC] * k, input_output_aliases={i: i for i in range(k)},
        compiler_params=pltpu.CompilerParams(has_side_effects=_SIDE_EFFECT),
    )(*thru, *handle["sems"], *after)


def _exchange_start(name, arrays, scatter, relations=ALL_PEERS, by_chip=False):
    n = len(arrays)
    lands = [lax.empty(a.shape if scatter else (N_DEV,) + a.shape, a.dtype) for a in arrays]

    def make_pairs(refs, send_sems, recv_sems):
        return _exchange_copies(refs[:n], refs[n:], send_sems, recv_sems, scatter, relations, by_chip)

    handle, token = _split_start(name, list(arrays) + lands, n * len(relations), make_pairs)
    handle.update(n=n, scatter=scatter, by_chip=by_chip)
    return handle, token


def _halves_start(name, arrays):
    lands = [lax.empty((a.shape[0], 1) + a.shape[2:], a.dtype) for a in arrays]
    n = len(arrays)

    def make_pairs(refs, send_sems, recv_sems):
        return _halves_copies(refs[:n], refs[n:], send_sems, recv_sems)

    return _split_start(name, list(arrays) + lands, n, make_pairs)


def _chip_sum(name, array, landed):
    chips, _, r, c = array.shape
    tr = r // 2 if r > 512 and r % 32 == 0 else r

    def body(a_ref, b_ref, o_ref):
        mine = a_ref[lax.axis_index("c")].astype(F32)
        o_ref[...] = (mine + b_ref[...].astype(F32)).astype(o_ref.dtype)

    return _call(body, name=name, grid=(chips, r // tr),
                 in_specs=[pl.BlockSpec((None, 2, tr, c), lambda k, i: (k, 0, i, 0)),
                           pl.BlockSpec((None, None, tr, c), lambda k, i: (k, 0, i, 0))],
                 out_specs=pl.BlockSpec((None, tr, c), lambda k, i: (k, i, 0)),
                 out_shape=_sds((chips, r, c), BF16), semantics=("parallel", "parallel"))(array, landed)


def _forward_start(name, lands):
    return _split_start(name, list(lands), len(lands) * len(CHIP_PEERS), _forward_copies)


def _exchange_wait(handle, after):
    n, scatter = handle["n"], handle["scatter"]
    out = _split_wait(handle, after)
    me = 2 * lax.axis_index("x") + lax.axis_index("y")
    if not handle["by_chip"]:
        me = 2 * me + lax.axis_index("c")
    done = []
    for src, land in zip(out[:n], out[n:]):
        own = lax.dynamic_index_in_dim(src, me, 0, keepdims=True) if scatter else src[None]
        done.append(lax.dynamic_update_slice_in_dim(land, own, me, 0))
    return done


def _rope_tables(pos_col):
    t = pos_col.shape[0]
    half = HEAD_DIM // 2
    inv_freq = ROPE_THETA ** (-jnp.arange(half, dtype=F32) / half)
    inv_row = jnp.tile(inv_freq, LANES // half)[None, :]

    def body(pos_ref, inv_ref, cos_ref, sin_ref):
        ang = pos_ref[...] * inv_ref[...]
        cos_ref[...] = jnp.cos(ang)
        sin_ref[...] = jnp.sin(ang)

    tm = min(t, 512)
    return _call(body, name="rope_tables", grid=(t // tm,),
                 in_specs=[pl.BlockSpec((tm, 1), lambda i: (i, 0)), pl.BlockSpec((1, LANES), lambda i: (0, 0))],
                 out_specs=[pl.BlockSpec((tm, LANES), lambda i: (i, 0))] * 2,
                 out_shape=[_sds((t, LANES), F32)] * 2, semantics=("parallel",))(pos_col, inv_row)


def _rot_half(x):
    lane = lax.broadcasted_iota(jnp.int32, x.shape, 1)
    low = (lane % HEAD_DIM) < HEAD_DIM // 2
    return jnp.where(low, -pltpu.roll(x, LANES - HEAD_DIM // 2, 1), pltpu.roll(x, HEAD_DIM // 2, 1))


def _rope(x, cos, sin):
    return x * cos + _rot_half(x) * sin


def _unrope(d, cos, sin):
    return d * cos - _rot_half(d) * sin


def _band_mask(first_block, heads):
    r = lax.broadcasted_iota(jnp.int32, (heads * BLOCK, 2 * BLOCK), 0) % BLOCK
    c = lax.broadcasted_iota(jnp.int32, (heads * BLOCK, 2 * BLOCK), 1)
    diff = r - c + BLOCK
    return (diff >= 0) & (diff < WINDOW) & ((c >= BLOCK) | jnp.logical_not(first_block))


def _attn_specs(t, d_attn, d_in):
    kb, vb = d_attn // D_KV, d_attn // D_KV + 1
    prev = lambda i: jnp.maximum(i - 1, 0)
    return [
        pl.BlockSpec((BLOCK, d_attn), lambda i: (i, 0)),
        pl.BlockSpec((BLOCK, D_KV), lambda i: (i, kb)),
        pl.BlockSpec((BLOCK, D_KV), lambda i: (i, vb)),
        pl.BlockSpec((BLOCK, D_KV), lambda i: (prev(i), kb)),
        pl.BlockSpec((BLOCK, D_KV), lambda i: (prev(i), vb)),
        pl.BlockSpec((BLOCK, LANES), lambda i: (i, 0)),
        pl.BlockSpec((BLOCK, LANES), lambda i: (i, 0)),
        pl.BlockSpec((BLOCK, LANES), lambda i: (prev(i), 0)),
        pl.BlockSpec((BLOCK, LANES), lambda i: (prev(i), 0)),
        pl.BlockSpec((1, LANES), lambda i: (0, 0)),
    ]


def _head(x, h):
    return x[:, h * HEAD_DIM:(h + 1) * HEAD_DIM]


def _attn_heads(q_ref, kc_ref, vc_ref, kp_ref, vp_ref, cq_ref, sq_ref, cp_ref, sp_ref, d_attn):
    cq, sq, cp, sp = cq_ref[...], sq_ref[...], cp_ref[...], sp_ref[...]
    q_rot = [_rope(q_ref[:, j * LANES:(j + 1) * LANES], cq, sq) for j in range(d_attn // LANES)]
    kc_rot = [_rope(kc_ref[:, j * LANES:(j + 1) * LANES], cq, sq) for j in range(D_KV // LANES)]
    kp_rot = [_rope(kp_ref[:, j * LANES:(j + 1) * LANES], cp, sp) for j in range(D_KV // LANES)]
    per = LANES // HEAD_DIM
    q_heads = [_head(q_rot[h // per], h % per).astype(BF16) for h in range(d_attn // HEAD_DIM)]
    kk = [jnp.concatenate([_head(kp_rot[g // per], g % per), _head(kc_rot[g // per], g % per)], axis=0).astype(BF16)
          for g in range(N_KV_HEADS)]
    vv = [jnp.concatenate([_head(vp_ref[...], g), _head(vc_ref[...], g)], axis=0).astype(BF16) for g in range(N_KV_HEADS)]
    return q_heads, kk, vv


def _stack_group(q_heads, sink_ref, group):
    q_all = jnp.concatenate([q_heads[h] for h in group], axis=0)
    sink_all = jnp.concatenate([jnp.broadcast_to(sink_ref[:, h:h + 1], (BLOCK, 1)) for h in group], axis=0)
    return q_all, sink_all


def _softmax_with_sink(q, kk, sink, mask):
    s = _dot(q, kk, 1, 1) * (1.0 / math.sqrt(HEAD_DIM))
    s = jnp.where(mask, s, MASKED)
    m = jnp.maximum(jnp.max(s, axis=-1, keepdims=True), sink)
    p = jnp.exp(s - m)
    e_sink = jnp.exp(sink - m)
    inv = 1.0 / (jnp.sum(p, axis=-1, keepdims=True) + e_sink)
    return p * inv, e_sink * inv


def _attention_fwd(proj, cos, sin, sinks_row, d_attn):
    t, d_in = proj.shape
    n_heads = d_attn // HEAD_DIM
    q_per_kv = n_heads // N_KV_HEADS

    def body(q_ref, kc_ref, vc_ref, kp_ref, vp_ref, cq_ref, sq_ref, cp_ref, sp_ref, sink_ref, o_ref):
        mask = _band_mask(pl.program_id(0) == 0, q_per_kv)
        q_heads, kk, vv = _attn_heads(q_ref, kc_ref, vc_ref, kp_ref, vp_ref, cq_ref, sq_ref, cp_ref, sp_ref, d_attn)
        for g in range(N_KV_HEADS):
            group = range(g * q_per_kv, (g + 1) * q_per_kv)
            q_all, sink_all = _stack_group(q_heads, sink_ref, group)
            probs, _ = _softmax_with_sink(q_all, kk[g], sink_all, mask)
            o_all = _dot(probs.astype(BF16), vv[g], 1, 0)
            for k, h in enumerate(group):
                o_ref[:, h * HEAD_DIM:(h + 1) * HEAD_DIM] = o_all[k * BLOCK:(k + 1) * BLOCK]

    return _call(body, name="attention_fwd", grid=(t // BLOCK,), in_specs=_attn_specs(t, d_attn, d_in),
                 out_specs=pl.BlockSpec((BLOCK, d_attn), lambda i: (i, 0)), out_shape=_sds((t, d_attn), F32),
                 semantics=("parallel",))(proj, proj, proj, proj, proj, cos, sin, cos, sin, sinks_row)


def _attention_bwd(proj, cos, sin, sinks_row, d_out, d_attn):
    t, d_in = proj.shape
    n_heads = d_attn // HEAD_DIM
    q_per_kv = n_heads // N_KV_HEADS
    nb = t // BLOCK
    per = LANES // HEAD_DIM

    def body(q_ref, kc_ref, vc_ref, kp_ref, vp_ref, cq_ref, sq_ref, cp_ref, sp_ref, sink_ref, do_ref,
             dq_ref, dk_ref, dv_ref, dsink_ref):
        i = pl.program_id(0)
        mask = _band_mask(i == 0, q_per_kv)
        q_heads, kk, vv = _attn_heads(q_ref, kc_ref, vc_ref, kp_ref, vp_ref, cq_ref, sq_ref, cp_ref, sp_ref, d_attn)
        lane = lax.broadcasted_iota(jnp.int32, (1, LANES), 1)
        dsink = jnp.zeros((1, LANES), F32)
        dq_rot, dkk, dvv = [], [], []
        for g in range(N_KV_HEADS):
            group = range(g * q_per_kv, (g + 1) * q_per_kv)
            q_all, sink_all = _stack_group(q_heads, sink_ref, group)
            probs, p_sink = _softmax_with_sink(q_all, kk[g], sink_all, mask)
            do_all = jnp.concatenate([do_ref[:, h * HEAD_DIM:(h + 1) * HEAD_DIM] for h in group], axis=0).astype(BF16)
            dp = _dot(do_all, vv[g], 1, 1)
            delta = jnp.sum(probs * dp, axis=-1, keepdims=True)
            ds = (probs * (dp - delta) * (1.0 / math.sqrt(HEAD_DIM))).astype(BF16)
            dq_all = _dot(ds, kk[g], 1, 0)
            dkk.append(_dot(ds, q_all, 0, 0))
            dvv.append(_dot(probs.astype(BF16), do_all, 0, 0))
            sink_term = p_sink * delta
            for k, h in enumerate(group):
                dq_rot.append(dq_all[k * BLOCK:(k + 1) * BLOCK])
                part = jnp.sum(sink_term[k * BLOCK:(k + 1) * BLOCK], axis=0, keepdims=True)
                dsink += jnp.where(lane == h, -part, 0.0)
        cq, sq, cp, sp = cq_ref[...], sq_ref[...], cp_ref[...], sp_ref[...]
        for j in range(d_attn // LANES):
            d = jnp.concatenate(dq_rot[j * per:(j + 1) * per], axis=1)
            dq_ref[:, j * LANES:(j + 1) * LANES] = _unrope(d, cq, sq)
        for j in range(D_KV // LANES):
            d = jnp.concatenate(dkk[j * per:(j + 1) * per], axis=1)
            dk_ref[0, :, j * LANES:(j + 1) * LANES] = _unrope(d[:BLOCK], cp, sp)
            dk_ref[1, :, j * LANES:(j + 1) * LANES] = _unrope(d[BLOCK:], cq, sq)
            d = jnp.concatenate(dvv[j * per:(j + 1) * per], axis=1)
            dv_ref[0, :, j * LANES:(j + 1) * LANES] = d[:BLOCK]
            dv_ref[1, :, j * LANES:(j + 1) * LANES] = d[BLOCK:]

        @pl.when(i == 0)
        def _():
            dsink_ref[...] = jnp.zeros_like(dsink_ref)

        dsink_ref[...] += dsink

    pair = pl.BlockSpec((2, BLOCK, D_KV), lambda i: (i, 0, 0))
    return _call(body, name="attention_bwd", grid=(nb,),
                 in_specs=_attn_specs(t, d_attn, d_in) + [pl.BlockSpec((BLOCK, d_attn), lambda i: (i, 0))],
                 out_specs=[pl.BlockSpec((BLOCK, d_attn), lambda i: (i, 0)), pair, pair,
                            pl.BlockSpec((1, LANES), lambda i: (0, 0))],
                 out_shape=[_sds((t, d_attn), F32), _sds((2 * nb, BLOCK, D_KV), F32), _sds((2 * nb, BLOCK, D_KV), F32),
                            _sds((1, LANES), F32)],
                 semantics=("arbitrary",))(proj, proj, proj, proj, proj, cos, sin, cos, sin, sinks_row, d_out)


def _assemble_dproj(dq, dk2, dv2, du, d_in, after):
    t, d_attn = dq.shape
    d_ssm = du.shape[1]
    nb = t // BLOCK

    def body(dq_ref, dk_own, dk_next, dv_own, dv_next, du_ref, o_ref):
        has_next = (pl.program_id(0) < nb - 1).astype(F32)
        o_ref[:, :d_attn] = dq_ref[...].astype(BF16)
        o_ref[:, d_attn:d_attn + D_KV] = (dk_own[...] + has_next * dk_next[...]).astype(BF16)
        o_ref[:, d_attn + D_KV:d_attn + 2 * D_KV] = (dv_own[...] + has_next * dv_next[...]).astype(BF16)
        o_ref[:, d_attn + 2 * D_KV:] = du_ref[...].astype(BF16)

    own = pl.BlockSpec((None, BLOCK, D_KV), lambda i: (2 * i + 1, 0, 0))
    nxt = pl.BlockSpec((None, BLOCK, D_KV), lambda i: (jnp.minimum(2 * i + 2, 2 * nb - 1), 0, 0))
    return _call(body, name="assemble_dproj", grid=(nb,),
                 in_specs=[pl.BlockSpec((BLOCK, d_attn), lambda i: (i, 0)), own, nxt, own, nxt,
                           pl.BlockSpec((BLOCK, d_ssm), lambda i: (i, 0))],
                 out_specs=pl.BlockSpec((BLOCK, d_in), lambda i: (i, 0)), out_shape=_sds((t, d_in), BF16),
                 semantics=("parallel",), n_after=len(after))(dq, dk2, dk2, dv2, dv2, du, *after)


def _discretise(ar, ai, ldt, br, bi):
    dt = jnp.exp(ldt)
    mag = jnp.exp(ar * dt)
    lam_re = mag * jnp.cos(ai * dt)
    lam_im = mag * jnp.sin(ai * dt)
    den = ar * ar + ai * ai
    nr = lam_re - 1.0
    ni = lam_im
    f_re = (nr * ar + ni * ai) / den
    f_im = (ni * ar - nr * ai) / den
    return (lam_re, lam_im, [f_re * r - f_im * i for r, i in zip(br, bi)], [f_re * i + f_im * r for r, i in zip(br, bi)])


def _whole(arrays):
    return [pl.BlockSpec(a.shape, lambda *_, nd=len(a.shape): (0,) * nd) for a in arrays]


def _channels(ref):
    groups = ref.shape[0] // SSM_GROUP
    return [ref[pl.ds(p, groups, stride=SSM_GROUP), :] for p in range(SSM_GROUP)]


def _store_channels(ref, values):
    groups = ref.shape[0] // SSM_GROUP
    for p, val in enumerate(values):
        ref[pl.ds(p, groups, stride=SSM_GROUP), :] = val


def _s5_discretise(ar, ai, ldt, br, bi):
    ins = [ar, ai, ldt, br, bi]

    def body(ar_ref, ai_ref, ldt_ref, br_ref, bi_ref, lr_ref, li_ref, bbr_ref, bbi_ref):
        lr, li, bbr, bbi = _discretise(ar_ref[...], ai_ref[...], ldt_ref[...], _channels(br_ref), _channels(bi_ref))
        lr_ref[...] = lr
        li_ref[...] = li
        _store_channels(bbr_ref, bbr)
        _store_channels(bbi_ref, bbi)

    outs = [_sds(ar.shape, F32), _sds(ar.shape, F32), _sds(br.shape, F32), _sds(br.shape, F32)]
    return _call(body, name="s5_discretise", in_specs=_whole(ins), out_specs=_whole(outs), out_shape=outs)(*ins)


def _s5_discretise_bwd(ar, ai, ldt, br, bi, d_lr, d_li, d_bbr, d_bbi):
    ins = [ar, ai, ldt, br, bi, d_lr, d_li, d_bbr, d_bbi]

    def body(ar_ref, ai_ref, ldt_ref, br_ref, bi_ref, dlr_ref, dli_ref, dbbr_ref, dbbi_ref,
             dar_ref, dai_ref, dldt_ref, dbr_ref, dbi_ref):
        _, vjp = jax.vjp(_discretise, ar_ref[...], ai_ref[...], ldt_ref[...], _channels(br_ref), _channels(bi_ref))
        dar, dai, dldt, dbr, dbi = vjp((dlr_ref[...], dli_ref[...], _channels(dbbr_ref), _channels(dbbi_ref)))
        dar_ref[...] = dar
        dai_ref[...] = dai
        dldt_ref[...] = dldt
        _store_channels(dbr_ref, dbr)
        _store_channels(dbi_ref, dbi)

    outs = [_sds(a.shape, F32) for a in (ar, ai, ldt, br, bi)]
    return _call(body, name="s5_discretise_bwd", in_specs=_whole(ins), out_specs=_whole(outs), out_shape=outs)(*ins)


def _cmul(ar, ai, br, bi):
    return ar * br - ai * bi, ar * bi + ai * br


def _load_segmented(ref, tile0, n_tiles, seg):
    return jnp.concatenate([ref[pl.ds(tile0 + j, SUBLANES, stride=seg), :] for j in range(n_tiles)], axis=0)


def _store_segmented(ref, tile0, seg, value):
    for j in range(value.shape[0] // SUBLANES):
        ref[pl.ds(tile0 + j, SUBLANES, stride=seg), :] = value[j * SUBLANES:(j + 1) * SUBLANES, :]


def _fill_powers(lr, li, pr_ref, pi_ref, seg):
    pows = [(lr, li)]
    for _ in range(SUBLANES - 1):
        pows.append(_cmul(pows[-1][0], pows[-1][1], lr, li))
    row = lax.broadcasted_iota(jnp.int32, (SUBLANES, lr.shape[1]), 0)
    tr = jnp.zeros((SUBLANES, lr.shape[1]), F32)
    ti = jnp.zeros((SUBLANES, lr.shape[1]), F32)
    for r in range(SUBLANES):
        tr = jnp.where(row == r, pows[r][0], tr)
        ti = jnp.where(row == r, pows[r][1], ti)
    pr_ref[0:SUBLANES, :] = tr
    pi_ref[0:SUBLANES, :] = ti
    k = SUBLANES
    while k < seg:
        fr, fi = pr_ref[k - 1:k, :], pi_ref[k - 1:k, :]
        for t0 in range(0, k, SUBLANES):
            nr, ni = _cmul(pr_ref[t0:t0 + SUBLANES, :], pi_ref[t0:t0 + SUBLANES, :], fr, fi)
            pr_ref[k + t0:k + t0 + SUBLANES, :] = nr
            pi_ref[k + t0:k + t0 + SUBLANES, :] = ni
        k *= 2


def _scan_segments(sr_ref, si_ref, pr_ref, pi_ref, lr, li, seg, reverse, per_tile=None):
    w = lr.shape[1]
    sign = -1.0 if reverse else 1.0
    lrb = jnp.broadcast_to(lr, (SUBLANES, w))
    lib = jnp.broadcast_to(sign * li, (SUBLANES, w))
    zero = jnp.zeros((SUBLANES, w), F32)

    def tile_rows(j):
        return pl.ds(pl.multiple_of(j * SUBLANES, SUBLANES), SUBLANES)

    def local(i, carry):
        rows = tile_rows(seg - 1 - i if reverse else i)
        pr, pi = _cmul(lrb, lib, carry[0], carry[1])
        xr, xi = sr_ref[rows, :] + pr, si_ref[rows, :] + pi
        sr_ref[rows, :] = xr
        si_ref[rows, :] = xi
        return xr, xi

    end_r, end_i = lax.fori_loop(0, seg, local, (zero, zero))
    full_r, full_i = pr_ref[seg - 1:seg, :], sign * pi_ref[seg - 1:seg, :]
    row = lax.broadcasted_iota(jnp.int32, (SUBLANES, w), 0)
    in_r, in_i = zero, zero
    cur_r, cur_i = jnp.zeros((1, w), F32), jnp.zeros((1, w), F32)
    for r in (range(SUBLANES - 2, -1, -1) if reverse else range(1, SUBLANES)):
        src = r + 1 if reverse else r - 1
        pr, pi = _cmul(full_r, full_i, cur_r, cur_i)
        cur_r, cur_i = end_r[src:src + 1, :] + pr, end_i[src:src + 1, :] + pi
        in_r = jnp.where(row == r, cur_r, in_r)
        in_i = jnp.where(row == r, cur_i, in_i)

    def carry_in(j, _):
        rows = tile_rows(j)
        k = seg - 1 - j if reverse else j
        pr, pi = _cmul(pr_ref[pl.ds(k, 1), :], sign * pi_ref[pl.ds(k, 1), :], in_r, in_i)
        xr, xi = sr_ref[rows, :] + pr, si_ref[rows, :] + pi
        sr_ref[rows, :] = xr
        si_ref[rows, :] = xi
        if per_tile is not None:
            per_tile(j, xr, xi)
        return 0

    lax.fori_loop(0, seg, carry_in, 0)


_S5_ROWS = 256


def _s5_in_specs(t, d_attn):
    u_block = (d_attn + 2 * D_KV) // SSM_CH_BLOCK
    blk3 = lambda shape: pl.BlockSpec((None,) + shape, lambda j: (j, 0, 0))
    return [
        pl.BlockSpec((t, SSM_CH_BLOCK), lambda j: (0, u_block + j)),
        blk3((SSM_CH_BLOCK, SSM_ST_BLOCK)), blk3((SSM_CH_BLOCK, SSM_ST_BLOCK)),
        blk3((1, SSM_ST_BLOCK)), blk3((1, SSM_ST_BLOCK)),
        blk3((SSM_ST_BLOCK, SSM_CH_BLOCK)), blk3((SSM_ST_BLOCK, SSM_CH_BLOCK)),
        pl.BlockSpec((1, SSM_CH_BLOCK), lambda j: (0, j)),
    ]


def _chunks(t):
    rows = min(_S5_ROWS, t)
    return rows, lambda i: pl.ds(pl.multiple_of(i * rows, rows), rows)


def _s5_states(u_ref, us_ref, bre_ref, bim_ref, lr_ref, li_ref, sr_ref, si_ref, pr_ref, pi_ref, t):
    seg = t // SUBLANES
    rows, chunk = _chunks(t)
    for c in range(t // rows):
        us_ref[c * rows:(c + 1) * rows, :] = _load_segmented(u_ref, c * rows // SUBLANES, rows // SUBLANES, seg)

    def fill(i, _):
        ub = us_ref[chunk(i), :].astype(BF16)
        sr_ref[chunk(i), :] = _dot(ub, bre_ref[...], 1, 0)
        si_ref[chunk(i), :] = _dot(ub, bim_ref[...], 1, 0)
        return 0

    lax.fori_loop(0, t // rows, fill, 0)
    _fill_powers(lr_ref[...], li_ref[...], pr_ref, pi_ref, seg)
    _scan_segments(sr_ref, si_ref, pr_ref, pi_ref, lr_ref[...], li_ref[...], seg, False)


def _s5_scratch(t):
    state = pltpu.VMEM((t, SSM_ST_BLOCK), F32)
    powers = pltpu.VMEM((t // SUBLANES, SSM_ST_BLOCK), F32)
    return state, powers, pltpu.VMEM((t, SSM_CH_BLOCK), F32)


def _s5_fwd(proj, mats, dskip_row, d_attn, d_ssm):
    t = proj.shape[0]
    seg = t // SUBLANES
    n_blocks = d_ssm // SSM_CH_BLOCK
    rows, chunk = _chunks(t)

    def body(u_ref, bre_ref, bim_ref, lr_ref, li_ref, cre_ref, cim_ref, d_ref, y_ref,
             sr_ref, si_ref, pr_ref, pi_ref, us_ref, ys_ref):
        _s5_states(u_ref, us_ref, bre_ref, bim_ref, lr_ref, li_ref, sr_ref, si_ref, pr_ref, pi_ref, t)

        def emit(i, _):
            ys_ref[chunk(i), :] = (_dot(sr_ref[chunk(i), :].astype(BF16), cre_ref[...], 1, 0)
                                   - _dot(si_ref[chunk(i), :].astype(BF16), cim_ref[...], 1, 0)
                                   + d_ref[...] * us_ref[chunk(i), :])
            return 0

        lax.fori_loop(0, t // rows, emit, 0)
        for c in range(t // rows):
            _store_segmented(y_ref, c * rows // SUBLANES, seg, ys_ref[c * rows:(c + 1) * rows, :])

    state, powers, channels = _s5_scratch(t)
    col = pl.BlockSpec((t, SSM_CH_BLOCK), lambda j: (0, j))
    return _call(body, name="s5_fwd", grid=(n_blocks,), in_specs=_s5_in_specs(t, d_attn), out_specs=col,
                 out_shape=_sds((t, d_ssm), F32), scratch_shapes=[state, state, powers, powers, channels, channels],
                 semantics=("parallel",))(proj, *mats, dskip_row)


def _s5_bwd(proj, mats, dskip_row, y, dz_a, dz_b, d_attn, d_ssm, after):
    t = proj.shape[0]
    seg = t // SUBLANES
    n_blocks = d_ssm // SSM_CH_BLOCK
    rows, chunk = _chunks(t)

    def body(u_ref, bre_ref, bim_ref, lr_ref, li_ref, cre_ref, cim_ref, d_ref, y_ref, dza_ref, dzb_ref,
             du_ref, dbre_ref, dbim_ref, dlr_ref, dli_ref, dcre_ref, dcim_ref, dd_ref,
             sr_ref, si_ref, gr_ref, gi_ref, pr_ref, pi_ref, us_ref, dys_ref, dus_ref, acc_r, acc_i):
        _s5_states(u_ref, us_ref, bre_ref, bim_ref, lr_ref, li_ref, sr_ref, si_ref, pr_ref, pi_ref, t)
        for ref in (dcre_ref, dcim_ref, dbre_ref, dbim_ref, dd_ref, acc_r, acc_i):
            ref[...] = jnp.zeros_like(ref)
        for c in range(t // rows):
            tile0, n_tiles = c * rows // SUBLANES, rows // SUBLANES
            dz = _load_segmented(dza_ref, tile0, n_tiles, seg) + _load_segmented(dzb_ref, tile0, n_tiles, seg)
            dys_ref[c * rows:(c + 1) * rows, :] = dz * _gelu_grad(_load_segmented(y_ref, tile0, n_tiles, seg))

        def through_c(i, _):
            dy = dys_ref[chunk(i), :]
            dd_ref[...] += jnp.sum(dy * us_ref[chunk(i), :], axis=0, keepdims=True)
            dyb = dy.astype(BF16)
            gr_ref[chunk(i), :] = _dot(dyb, cre_ref[...], 1, 1)
            gi_ref[chunk(i), :] = -_dot(dyb, cim_ref[...], 1, 1)
            dcre_ref[...] += _dot(sr_ref[chunk(i), :].astype(BF16), dyb, 0, 0)
            dcim_ref[...] -= _dot(si_ref[chunk(i), :].astype(BF16), dyb, 0, 0)
            return 0

        lax.fori_loop(0, t // rows, through_c, 0)

        row = lax.broadcasted_iota(jnp.int32, (SUBLANES, SSM_ST_BLOCK), 0)
        last = pl.ds((seg - 1) * SUBLANES, SUBLANES)
        wrap = [jnp.where(row == 0, 0.0, pltpu.roll(ref[last, :], 1, 0)) for ref in (sr_ref, si_ref)]

        def lambda_grad(j, g_re, g_im):
            before = pl.ds(pl.multiple_of(jnp.maximum(j - 1, 0) * SUBLANES, SUBLANES), SUBLANES)
            prev_r = jnp.where(j > 0, sr_ref[before, :], wrap[0])
            prev_i = jnp.where(j > 0, si_ref[before, :], wrap[1])
            acc_r[...] += g_re * prev_r + g_im * prev_i
            acc_i[...] += g_im * prev_r - g_re * prev_i

        _scan_segments(gr_ref, gi_ref, pr_ref, pi_ref, lr_ref[...], li_ref[...], seg, True, per_tile=lambda_grad)
        dlr_ref[...] = jnp.sum(acc_r[...], axis=0, keepdims=True)
        dli_ref[...] = jnp.sum(acc_i[...], axis=0, keepdims=True)

        def through_b(i, _):
            ub = us_ref[chunk(i), :].astype(BF16)
            grb, gib = gr_ref[chunk(i), :].astype(BF16), gi_ref[chunk(i), :].astype(BF16)
            dbre_ref[...] += _dot(ub, grb, 0, 0)
            dbim_ref[...] += _dot(ub, gib, 0, 0)
            dus_ref[chunk(i), :] = (_dot(grb, bre_ref[...], 1, 1) + _dot(gib, bim_ref[...], 1, 1)
                                    + d_ref[...] * dys_ref[chunk(i), :])
            return 0

        lax.fori_loop(0, t // rows, through_b, 0)
        for c in range(t // rows):
            _store_segmented(du_ref, c * rows // SUBLANES, seg, dus_ref[c * rows:(c + 1) * rows, :])

    col = pl.BlockSpec((t, SSM_CH_BLOCK), lambda j: (0, j))
    blk3 = lambda shape: pl.BlockSpec((None,) + shape, lambda j: (j, 0, 0))
    state, powers, channels = _s5_scratch(t)
    return _call(
        body, name="s5_bwd", grid=(n_blocks,), in_specs=_s5_in_specs(t, d_attn) + [col, col, col],
        out_specs=[col, blk3((SSM_CH_BLOCK, SSM_ST_BLOCK)), blk3((SSM_CH_BLOCK, SSM_ST_BLOCK)),
                   blk3((1, SSM_ST_BLOCK)), blk3((1, SSM_ST_BLOCK)),
                   blk3((SSM_ST_BLOCK, SSM_CH_BLOCK)), blk3((SSM_ST_BLOCK, SSM_CH_BLOCK)),
                   pl.BlockSpec((1, SSM_CH_BLOCK), lambda j: (0, j))],
        out_shape=[_sds((t, d_ssm), F32),
                   _sds((n_blocks, SSM_CH_BLOCK, SSM_ST_BLOCK), F32), _sds((n_blocks, SSM_CH_BLOCK, SSM_ST_BLOCK), F32),
                   _sds((n_blocks, 1, SSM_ST_BLOCK), F32), _sds((n_blocks, 1, SSM_ST_BLOCK), F32),
                   _sds((n_blocks, SSM_ST_BLOCK, SSM_CH_BLOCK), F32), _sds((n_blocks, SSM_ST_BLOCK, SSM_CH_BLOCK), F32),
                   _sds((1, d_ssm), F32)],
        scratch_shapes=[state, state, state, state, powers, powers, channels, channels, channels,
                        pltpu.VMEM((SUBLANES, SSM_ST_BLOCK), F32), pltpu.VMEM((SUBLANES, SSM_ST_BLOCK), F32)],
        semantics=("parallel",), n_after=len(after))(proj, *mats, dskip_row, y, dz_a, dz_b, *after)


def _by_block(gp_n):
    return gp_n.reshape(-1, GROUPS_PER_BLOCK, SSM_GROUP, SSM_STATE)


def _block_diag_in(bbar):
    eye = jnp.eye(GROUPS_PER_BLOCK, dtype=F32)
    return jnp.einsum("jgpn,gh->jgphn", _by_block(bbar), eye).reshape(-1, SSM_CH_BLOCK, SSM_ST_BLOCK)


def _block_diag_in_t(dense):
    d5 = dense.reshape(-1, GROUPS_PER_BLOCK, SSM_GROUP, GROUPS_PER_BLOCK, SSM_STATE)
    eye = jnp.eye(GROUPS_PER_BLOCK, dtype=F32)
    return jnp.einsum("jgphn,gh->jgpn", d5, eye).reshape(-1, SSM_STATE)


def _block_diag_out(c):
    eye = jnp.eye(GROUPS_PER_BLOCK, dtype=F32)
    return jnp.einsum("jgpn,gh->jgnhp", _by_block(c), eye).reshape(-1, SSM_ST_BLOCK, SSM_CH_BLOCK)


def _block_diag_out_t(dense):
    d5 = dense.reshape(-1, GROUPS_PER_BLOCK, SSM_STATE, GROUPS_PER_BLOCK, SSM_GROUP)
    eye = jnp.eye(GROUPS_PER_BLOCK, dtype=F32)
    return jnp.einsum("jgnhp,gh->jgpn", d5, eye).reshape(-1, SSM_STATE)


def _adamw(w, g, m, v):
    m = ADAM_B1 * m + (1.0 - ADAM_B1) * g
    v = ADAM_B2 * v + (1.0 - ADAM_B2) * (g * g)
    m_hat = m / (1.0 - ADAM_B1 ** ADAM_STEP)
    v_hat = v / (1.0 - ADAM_B2 ** ADAM_STEP)
    delta = -ADAM_LR * (m_hat / (jnp.sqrt(v_hat) + ADAM_EPS) + ADAM_WD * w)
    return delta, m, v


def _adam_sharded(name, parts, w, m, v, tr, row0=0):
    r, c = w.shape
    assert r % tr == 0 and row0 % tr == 0, (name, r, tr, row0)

    def body(p_ref, w_ref, m_ref, v_ref, g_out, d_out, m_out, v_out):
        g = p_ref[0].astype(F32)
        for i in range(1, p_ref.shape[0]):
            g = g + p_ref[i].astype(F32)
        delta, m_new, v_new = _adamw(w_ref[...], g, m_ref[...], v_ref[...])
        g_out[...] = g
        d_out[...] = delta
        m_out[...] = m_new
        v_out[...] = v_new

    tile = pl.BlockSpec((tr, c), lambda i: (i, 0))
    return _call(body, name=name, grid=(r // tr,),
                 in_specs=[pl.BlockSpec((parts.shape[0], tr, c), lambda i: (0, i + row0 // tr, 0)), tile, tile, tile],
                 out_specs=[tile] * 4, out_shape=[_sds((r, c), F32)] * 4, semantics=("parallel",))(parts, w, m, v)


_BIG = ("w_in", "w_glu", "w_o", "w_gate", "w_up", "w_down")
_BY_COLUMNS = ("w_in", "w_gate", "w_up")
_SMALL_VECTORS = ("sinks", "log_dt", "b_glu", "g_attn_out", "g_ssm_out", "g_post_mix", "g_pre_ffn", "g_post_ffn")
_SMALL_MATRICES = ("b_re", "b_im", "c_re", "c_im", "a_re", "a_im")
_ORDER = ("g_pre_mix", "w_in", "sinks", "a_re", "a_im", "log_dt", "b_re", "b_im", "c_re", "c_im", "d_skip", "w_glu",
          "b_glu", "g_attn_out", "g_ssm_out", "w_o", "g_post_mix", "g_pre_ffn", "w_gate", "w_up", "w_down",
          "g_post_ffn")


def _pack_grads(vectors, matrices):
    width = max(a.shape[1] for a in vectors)
    slots, row, lane = [], 0, 0
    for a in vectors:
        span = -(-a.shape[1] // LANES) * LANES
        if lane + span > width:
            row, lane = row + 1, 0
        slots.append((row, lane, a.shape[1]))
        lane += span
    firsts, at = [], 0
    for a in matrices:
        firsts.append(at)
        at += a.shape[0]
    nv = len(vectors)

    def body(*refs):
        vec_out, mat_out = refs[-2], refs[-1]
        vec_out[...] = jnp.zeros_like(vec_out)
        for ref, (r, l, w) in zip(refs[:nv], slots):
            vec_out[r:r + 1, l:l + w] = ref[...]
        for ref, r0 in zip(refs[nv:-2], firsts):
            mat_out[r0:r0 + ref.shape[0], :] = ref[...]

    ins = list(vectors) + list(matrices)
    outs = [_sds((-(-(row + 1) // SUBLANES) * SUBLANES, width), F32), _sds((at, matrices[0].shape[1]), F32)]
    vec_pack, mat_pack = _call(body, name="pack_small_grads", in_specs=_whole(ins), out_specs=_whole(outs),
                               out_shape=outs)(*ins)
    return vec_pack, slots, mat_pack, firsts


def _adam_replicated(sources, found_at, w, m, v):
    ns, n = len(sources), len(w)

    def body(*refs):
        ins, outs = refs[ns:ns + 3 * n], refs[ns + 3 * n:]
        summed = []
        for p_ref in refs[:ns]:
            g = p_ref[0]
            for k in range(1, N_DEV):
                g = g + p_ref[k]
            summed.append(g)
        for i, (src, row, lane) in enumerate(found_at):
            w_ref, m_ref, v_ref = ins[i], ins[n + i], ins[2 * n + i]
            rows, cols = w_ref.shape
            g = summed[src][row:row + rows, lane:lane + cols]
            delta, m_new, v_new = _adamw(w_ref[...], g, m_ref[...], v_ref[...])
            for o, val in zip(outs[4 * i:4 * i + 4], (g, delta, m_new, v_new)):
                o[...] = val

    ins = list(sources) + list(w) + list(m) + list(v)
    outs = [_sds(a.shape, F32) for a in w for _ in range(4)]
    flat = _call(body, name="adam_replicated", in_specs=_whole(ins), out_specs=_whole(outs), out_shape=outs)(*ins)
    return [tuple(flat[4 * i:4 * i + 4]) for i in range(n)]


def kernel(x, positions, g_pre_mix, w_in, sinks, a_re, a_im, log_dt, b_re, b_im, c_re, c_im, d_skip, w_glu, b_glu, g_attn_out, g_ssm_out, w_o, g_post_mix, g_pre_ffn, w_gate, w_up, w_down, g_post_ffn, loss_target, m_g_pre_mix, m_w_in, m_sinks, m_a_re, m_a_im, m_log_dt, m_b_re, m_b_im, m_c_re, m_c_im, m_d_skip, m_w_glu, m_b_glu, m_g_attn_out, m_g_ssm_out, m_w_o, m_g_post_mix, m_g_pre_ffn, m_w_gate, m_w_up, m_w_down, m_g_post_ffn, v_g_pre_mix, v_w_in, v_sinks, v_a_re, v_a_im, v_log_dt, v_b_re, v_b_im, v_c_re, v_c_im, v_d_skip, v_w_glu, v_b_glu, v_g_attn_out, v_g_ssm_out, v_w_o, v_g_post_mix, v_g_pre_ffn, v_w_gate, v_w_up, v_w_down, v_g_post_ffn):
    given = dict(locals())
    weights = {n: given[n] for n in _ORDER}
    mom_m = {n: given["m_" + n] for n in _ORDER}
    mom_v = {n: given["v_" + n] for n in _ORDER}

    t, d = x.shape[1], x.shape[2]
    d_attn = d // 2
    d_ssm = d - d_attn
    d_in = d_attn + 2 * D_KV + d_ssm
    n_groups = d_ssm // SSM_GROUP
    n_heads = d_attn // HEAD_DIM
    tm = min(256, t)

    x2 = x[0]
    target = loss_target[0]

    def by_rows(n, a):
        return a[0].T if n in _BY_COLUMNS else a[0]

    def start_gather(name, ns, token):
        behind = 0 if token is None else token[0, 0].astype(BF16)
        shards = [by_rows(n, weights[n]).astype(BF16) + behind for n in ns]
        return _exchange_start(name, shards, False, (SIBLING,) + CHIP_PEERS)

    def finish_gather(handle, after):
        forward, _ = _forward_start(handle["name"] + "_forward", _exchange_wait(handle, after))
        return _split_wait(forward, [])

    ag_in, token = start_gather("gather_w_in", ["w_in"], None)
    ag_mix, token = start_gather("gather_w_glu_o", ["w_glu", "w_o"], token)
    ag_ffn_in, token = start_gather("gather_w_gate_up", ["w_gate", "w_up"], token)
    ag_down, token = start_gather("gather_w_down", ["w_down"], token)

    xn, = _rows("norm_in", lambda xv, g: ([_rms(xv)[0] * g], []), [x2], [g_pre_mix], [(d, BF16)], [], tm,
                after=[token])
    win_g, = finish_gather(ag_in, [xn])
    w_in_t = win_g.reshape(d_in, d)
    proj = _mm_nt("proj_in", xn, w_in_t, F32, tn=d_in // 4 if (d_in // 4) % LANES == 0 else None)

    cos, sin = _rope_tables(positions.reshape(t, 1).astype(F32))
    sinks_row = jnp.pad(sinks, ((0, 0), (0, LANES - n_heads)))
    attn = _attention_fwd(proj, cos, sin, sinks_row, d_attn)

    def view(n, a):
        if n in ("b_re", "b_im"):
            return jnp.transpose(a[0], (0, 2, 1)).reshape(-1, SSM_STATE)
        if n in ("c_re", "c_im"):
            return a[0].reshape(-1, SSM_STATE)
        return a[0].T if n == "d_skip" else a[0] if a.ndim == 3 else a

    def unview(n, val):
        if n in ("b_re", "b_im"):
            return jnp.transpose(val.reshape(n_groups, SSM_GROUP, SSM_STATE), (0, 2, 1))[None]
        if n in ("c_re", "c_im"):
            return val.reshape(1, n_groups, SSM_GROUP, SSM_STATE)
        return val.T[None] if n == "d_skip" else val[None] if weights[n].ndim == 3 else val

    b_re_v, b_im_v = view("b_re", b_re), view("b_im", b_im)
    ldt_col = log_dt.reshape(n_groups, 1)
    lam_re, lam_im, bbar_re, bbar_im = _s5_discretise(a_re[0], a_im[0], ldt_col, b_re_v, b_im_v)
    n_blocks = n_groups // GROUPS_PER_BLOCK
    mats = [_block_diag_in(bbar_re).astype(BF16), _block_diag_in(bbar_im).astype(BF16),
            lam_re.reshape(n_blocks, 1, SSM_ST_BLOCK), lam_im.reshape(n_blocks, 1, SSM_ST_BLOCK),
            _block_diag_out(view("c_re", c_re)).astype(BF16), _block_diag_out(view("c_im", c_im)).astype(BF16)]
    dskip_row = d_skip.reshape(1, d_ssm)
    y_ssm = _s5_fwd(proj, mats, dskip_row, d_attn, d_ssm)
    gelu_bf16 = lambda yv: _gelu(yv).astype(BF16)
    wglu_g, wo_g = finish_gather(ag_mix, [attn, y_ssm])
    w_glu_full = wglu_g.reshape(d_ssm, d_ssm)
    w_o_full = wo_g.reshape(d, d)
    glu_lin = _mm_nn("glu_gate", y_ssm, w_glu_full, F32, a_fn=gelu_bf16)

    def mix_prep(av, yv, gl, bg, ga, gs):
        ssm = _gelu(yv) * _sigmoid(gl + bg)
        return [jnp.concatenate([_rms(av)[0] * ga, _rms(ssm)[0] * gs], axis=1)], []

    mixed, = _rows("mix_prep", mix_prep, [attn, y_ssm, glu_lin], [b_glu, g_attn_out, g_ssm_out], [(d, BF16)], [], tm)
    mix = _mm_nn("mix_out", mixed, w_o_full, F32, tn=d // 2 if (d // 2) % LANES == 0 else None)

    def post_mix(xv, mv, gpm, gpf):
        h = xv + _rms(mv)[0] * gpm
        return [h, _rms(h)[0] * gpf], []

    h, hn = _rows("post_mix", post_mix, [x2, mix], [g_post_mix, g_pre_ffn], [(d, F32), (d, BF16)], [], tm)
    wgate_g, wup_g = finish_gather(ag_ffn_in, [hn])
    gate, up, hid = _ffn_in(hn, wgate_g, wup_g)
    wdown_g, = finish_gather(ag_down, [hid])
    ff = _mm_contract_slots("ffn_down", [(hid, wdown_g)], F32, per_step=4)

    def head(hv, fv, tv, gpo):
        out = hv + _rms(fv)[0] * gpo
        err = out - tv
        dout = err * (1.0 / d)
        dff, dg = _rms_bwd(fv, gpo, dout)
        loss = jnp.zeros((1, LANES), F32) + 0.5 * jnp.sum(err * err) * (1.0 / d)
        return [dff, dout], [dg, loss]

    dff, dh_out, dg_post_ffn, loss_row = _rows("loss_head", head, [h, ff, target], [g_post_ffn],
                                               [(d, BF16), (d, F32)], [d, LANES], tm)

    def swap_halves(name, grads):
        return _halves_start("swap_" + name, [g.reshape(N_DEV // 2, 2, *g.shape[1:]) for g in grads])

    def scatter_chip_sums(name, swap, after):
        both = _split_wait(swap, after)
        half = len(both) // 2
        sums = [_chip_sum("chip_sum_%s_%d" % (name, i), both[i], both[half + i]) for i in range(half)]
        return _exchange_start("scatter_" + name, sums, True, CHIP_PEERS, by_chip=True)

    dw_down = _mm_slots_tn("ffn_down_dw", hid, dff, BF16)
    swap_down, token = swap_halves("dw_down", [dw_down])
    dgate, dup = _ffn_down_bwd(dff, wdown_g, gate, up, [token])
    rs_down, token = scatter_chip_sums("dw_down", swap_down, [dgate])
    dhn = _mm_contract_slots("ffn_in_dx", [(dgate, wgate_g), (dup, wup_g)], F32, per_step=2, after=[token])
    dw_gate = _mm_slots_tn("ffn_gate_dw", dgate, hn, BF16)
    dw_up = _mm_slots_tn("ffn_up_dw", dup, hn, BF16)
    swap_ffn_in, tok_ffn_in = swap_halves("dw_gate_up", [dw_gate, dw_up])

    def mid_bwd(dho, dhn_, hv, mv, gpf, gpm):
        d1, dgpf = _rms_bwd(hv, gpf, dhn_)
        dh_ = dho + d1
        dmix_, dgpm = _rms_bwd(mv, gpm, dh_)
        return [dh_, dmix_], [dgpf, dgpm]

    dh, dmix, dg_pre_ffn, dg_post_mix = _rows("mid_bwd", mid_bwd, [dh_out, dhn, h, mix], [g_pre_ffn, g_post_mix],
                                              [(d, F32), (d, BF16)], [d, d], tm, after=[tok_ffn_in])

    dmixed = _mm_nt("mix_out_dx", dmix, w_o_full, F32, tn=d // 2 if (d // 2) % LANES == 0 else None)
    rs_ffn_in, token = scatter_chip_sums("dw_gate_up", swap_ffn_in, [dmixed])
    dw_o = _mm_tn("mix_out_dw", mixed, dmix, BF16, tn=d // 2 if (d // 2) % LANES == 0 else None, after=[token])
    swap_o, tok_o = swap_halves("dw_o", [dw_o.reshape(N_DEV, d // N_DEV, d)])

    def mix_bwd(dm, av, yv, gl, bg, ga, gs):
        dattn_, dga = _rms_bwd(av, ga, dm[:, :d_attn])
        z = _gelu(yv)
        sg = _sigmoid(gl + bg)
        dssm, dgs = _rms_bwd(z * sg, gs, dm[:, d_attn:])
        dgl = dssm * z * sg * (1.0 - sg)
        return [dattn_, dssm * sg, dgl], [dga, dgs, jnp.sum(dgl, axis=0, keepdims=True)]

    dattn, dz_direct, dglu, dg_attn_out, dg_ssm_out, db_glu = _rows(
        "mix_bwd", mix_bwd, [dmixed, attn, y_ssm, glu_lin], [b_glu, g_attn_out, g_ssm_out],
        [(d_attn, F32), (d_ssm, F32), (d_ssm, BF16)], [d_attn, d_ssm, d_ssm], tm, after=[tok_o])
    dz_glu = _mm_nt("glu_gate_dx", dglu, w_glu_full, F32)
    dw_glu = _mm_tn("glu_gate_dw", y_ssm, dglu, BF16, a_fn=gelu_bf16)
    rs_o, token = scatter_chip_sums("dw_o", swap_o, [dz_glu, dw_glu])

    du, db_re_dense, db_im_dense, dlam_re, dlam_im, dc_re_dense, dc_im_dense, dd_skip = _s5_bwd(
        proj, mats, dskip_row, y_ssm, dz_direct, dz_glu, d_attn, d_ssm, [token])
    da_re, da_im, dlog_dt, db_re_v, db_im_v = _s5_discretise_bwd(
        a_re[0], a_im[0], ldt_col, b_re_v, b_im_v, dlam_re.reshape(n_groups, SSM_STATE),
        dlam_im.reshape(n_groups, SSM_STATE), _block_diag_in_t(db_re_dense), _block_diag_in_t(db_im_dense))
    dq, dk2, dv2, dsinks_row = _attention_bwd(proj, cos, sin, sinks_row, dattn, d_attn)

    small_grads = {
        "sinks": dsinks_row, "a_re": da_re, "a_im": da_im, "log_dt": dlog_dt.reshape(1, n_groups),
        "b_re": db_re_v, "b_im": db_im_v, "c_re": _block_diag_out_t(dc_re_dense),
        "c_im": _block_diag_out_t(dc_im_dense), "d_skip": dd_skip.reshape(n_groups, SSM_GROUP).T, "b_glu": db_glu,
        "g_attn_out": dg_attn_out, "g_ssm_out": dg_ssm_out, "g_post_mix": dg_post_mix, "g_pre_ffn": dg_pre_ffn,
        "g_post_ffn": dg_post_ffn,
    }
    vec_pack, vec_slots, mat_pack, mat_rows = _pack_grads([small_grads[n] for n in _SMALL_VECTORS],
                                                          [small_grads[n] for n in _SMALL_MATRICES])
    ag_small, token = _exchange_start("gather_small_grads", [vec_pack, mat_pack, small_grads["d_skip"]], False)
    dproj = _assemble_dproj(dq, dk2, dv2, du, d_in, [token])

    dxn = _mm_nn("proj_in_dx", dproj, w_in_t, F32, tn=d // 2 if (d // 2) % LANES == 0 else None)
    dw_in = _mm_tn("proj_in_dw", dproj, xn, BF16).reshape(N_DEV, d_in // N_DEV, d)
    swap_in, token = swap_halves("dw_in_glu", [dw_in, dw_glu.reshape(N_DEV, d_ssm // N_DEV, d_ssm)])

    def x_bwd(dh_, dxn_, xv, g):
        dx, dg = _rms_bwd(xv, g, dxn_)
        return [dh_ + dx], [dg]

    grad_x, dg_pre_mix = _rows("norm_in_bwd", x_bwd, [dh, dxn, x2], [g_pre_mix], [(d, F32)], [d], tm, after=[token])
    ag_last, token = _exchange_start("gather_g_pre_mix_grad", [dg_pre_mix], False)
    rs_in, token = scatter_chip_sums("dw_in_glu", swap_in, [grad_x, token])

    results = {}

    def adam_big(n, parts):
        r = parts.shape[1]
        results[n] = _adam_sharded("adam_" + n, parts, by_rows(n, weights[n]), by_rows(n, mom_m[n]),
                                   by_rows(n, mom_v[n]), 64 if r % 64 == 0 else r)
        return results[n][3]

    done = [grad_x, token]
    adam_big("w_down", _exchange_wait(rs_down, done)[0])
    p_gate, p_up = _exchange_wait(rs_ffn_in, done)
    done = [adam_big("w_gate", p_gate), adam_big("w_up", p_up), results["w_down"][3]]
    done = [adam_big("w_o", _exchange_wait(rs_o, done)[0])]
    p_in, p_glu = _exchange_wait(rs_in, done)
    done = [adam_big("w_in", p_in), adam_big("w_glu", p_glu)]
    vec_parts, mat_parts, dskip_parts = _exchange_wait(ag_small, done)
    first_gain_parts, = _exchange_wait(ag_last, done)
    for n, row0 in zip(_SMALL_MATRICES, mat_rows):
        rows = view(n, weights[n]).shape[0]
        results[n] = _adam_sharded("adam_" + n, mat_parts, view(n, weights[n]), view(n, mom_m[n]), view(n, mom_v[n]),
                                   rows, row0)
    rest = _SMALL_VECTORS + ("d_skip", "g_pre_mix")
    found_at = [(0, row, lane) for row, lane, _ in vec_slots] + [(1, 0, 0), (2, 0, 0)]
    updated = _adam_replicated([vec_parts, dskip_parts, first_gain_parts], found_at,
                               [view(n, weights[n]) for n in rest], [view(n, mom_m[n]) for n in rest],
                               [view(n, mom_v[n]) for n in rest])
    results.update(zip(rest, updated))

    loss = lax.psum(loss_row[0, 0], ("x", "y", "c"))
    outs = [loss, grad_x[None]]
    for k in range(4):
        for n in _ORDER:
            val = results[n][k]
            outs.append(val.T[None] if n in _BY_COLUMNS else val[None] if n in _BIG else unview(n, val))
    return tuple(outs)
```

```python
import math

import jax
import jax.numpy as jnp
from jax import lax
from jax.experimental import pallas as pl
from jax.experimental.pallas import tpu as pltpu

F32 = jnp.float32
BF16 = jnp.bfloat16

HEAD_DIM = 64
N_KV_HEADS = 4
D_KV = N_KV_HEADS * HEAD_DIM
WINDOW = 128
BLOCK = 128
ROPE_THETA = 10000.0
SSM_GROUP = 16
SSM_STATE = 64
GROUPS_PER_BLOCK = 8
SSM_CH_BLOCK = GROUPS_PER_BLOCK * SSM_GROUP
SSM_ST_BLOCK = GROUPS_PER_BLOCK * SSM_STATE
RMS_EPS = 1e-6
N_DEV = 8
LANES = 128
SUBLANES = 8
MASKED = -1e30

ADAM_LR = 0.001
ADAM_B1 = 0.9
ADAM_B2 = 0.999
ADAM_EPS = 1e-08
ADAM_WD = 0.01
ADAM_STEP = 10

VMEM_LIMIT_BYTES = 56 * 1024 * 1024


def _call(body, *, name, out_shape, in_specs, out_specs, grid=(), scratch_shapes=(), semantics=None, n_after=0):
    params = dict(vmem_limit_bytes=VMEM_LIMIT_BYTES)
    if semantics is not None:
        params["dimension_semantics"] = semantics
    n_in = len(in_specs)
    if n_after:
        inner = body

        def body(*refs):
            inner(*refs[:n_in], *refs[n_in + n_after:])

        in_specs = list(in_specs) + [pl.BlockSpec(memory_space=pl.ANY)] * n_after
    return pl.pallas_call(body, name=name, grid=grid, in_specs=in_specs, out_specs=out_specs, out_shape=out_shape,
                          scratch_shapes=scratch_shapes, compiler_params=pltpu.CompilerParams(**params))


def _sds(shape, dtype):
    return jax.ShapeDtypeStruct(tuple(shape), dtype)


def _dot(a, b, ca, cb):
    return lax.dot_general(a, b, (((ca,), (cb,)), ((), ())), preferred_element_type=F32)


def _rms(x):
    r = lax.rsqrt(jnp.mean(x * x, axis=-1, keepdims=True) + RMS_EPS)
    return x * r, r


def _rms_bwd(x, g, dy):
    xh, r = _rms(x)
    dxh = dy * g
    dx = r * (dxh - xh * jnp.mean(dxh * xh, axis=-1, keepdims=True))
    return dx, jnp.sum(dy * xh, axis=0, keepdims=True)


def _sigmoid(x):
    return 1.0 / (1.0 + jnp.exp(-x))


_GELU_C = math.sqrt(2.0 / math.pi)
_GELU_A = 0.044715


def _gelu(y):
    t = jnp.tanh(_GELU_C * (y + _GELU_A * y * y * y))
    return 0.5 * y * (1.0 + t)


def _gelu_grad(y):
    t = jnp.tanh(_GELU_C * (y + _GELU_A * y * y * y))
    return 0.5 * (1.0 + t) + 0.5 * y * (1.0 - t * t) * _GELU_C * (1.0 + 3.0 * _GELU_A * y * y)


def _rows(name, fn, row_ins, vec_ins, row_outs, acc_widths, tm, after=()):
    rows = row_ins[0].shape[0]
    assert rows % tm == 0, (name, rows, tm)
    n_row, n_vec, n_out, n_acc = len(row_ins), len(vec_ins), len(row_outs), len(acc_widths)

    def body(*refs):
        ins = [r[...] for r in refs[:n_row + n_vec]]
        outs = refs[n_row + n_vec:n_row + n_vec + n_out]
        accs = refs[n_row + n_vec + n_out:]
        row_vals, acc_vals = fn(*ins)
        for o, v in zip(outs, row_vals):
            o[...] = v.astype(o.dtype)
        if n_acc:
            @pl.when(pl.program_id(0) == 0)
            def _():
                for a in accs:
                    a[...] = jnp.zeros_like(a)
            for a, v in zip(accs, acc_vals):
                a[...] += v

    in_specs = [pl.BlockSpec((tm, a.shape[1]), lambda i: (i, 0)) for a in row_ins]
    in_specs += [pl.BlockSpec(v.shape, lambda i: (0, 0)) for v in vec_ins]
    out_specs = [pl.BlockSpec((tm, w), lambda i: (i, 0)) for w, _ in row_outs]
    out_specs += [pl.BlockSpec((1, w), lambda i: (0, 0)) for w in acc_widths]
    out_shape = [_sds((rows, w), dt) for w, dt in row_outs] + [_sds((1, w), F32) for w in acc_widths]
    return _call(body, name=name, grid=(rows // tm,), in_specs=in_specs, out_specs=out_specs, out_shape=out_shape,
                 semantics=("arbitrary",) if n_acc else ("parallel",), n_after=len(after))(*row_ins, *vec_ins, *after)


def _matmul(name, operands, in_specs, product, grid, out_shape, out_spec, acc_shape, after=()):
    nk = grid[-1]
    n_in = len(operands)
    in_place = out_shape.dtype == F32

    def body(*refs):
        ins = [r[...] for r in refs[:n_in]]
        o_ref = refs[n_in]
        if nk == 1:
            o_ref[...] = product(*ins).astype(o_ref.dtype)
            return
        acc = o_ref if in_place else refs[n_in + 1]
        k = pl.program_id(len(grid) - 1)

        @pl.when(k == 0)
        def _():
            acc[...] = jnp.zeros_like(acc)

        acc[...] += product(*ins)

        if not in_place:
            @pl.when(k == nk - 1)
            def _():
                o_ref[...] = acc[...].astype(o_ref.dtype)

    return _call(body, name=name, grid=grid, in_specs=in_specs, out_specs=out_spec, out_shape=out_shape,
                 scratch_shapes=[] if nk == 1 or in_place else [pltpu.VMEM(acc_shape, F32)],
                 semantics=("parallel",) * (len(grid) - 1) + ("arbitrary",), n_after=len(after))(*operands, *after)


def _mm_nn(name, a, b, out_dtype, tm=512, tn=None, a_fn=lambda x: x):
    m, k = a.shape
    n = b.shape[1]
    tm, tn = min(tm, m), n if tn is None else tn
    return _matmul(name, [a, b],
                   [pl.BlockSpec((tm, k), lambda i, j, s: (i, 0)), pl.BlockSpec((k, tn), lambda i, j, s: (0, j))],
                   lambda x, y: _dot(a_fn(x), y, 1, 0), (m // tm, n // tn, 1), _sds((m, n), out_dtype),
                   pl.BlockSpec((tm, tn), lambda i, j, s: (i, j)), (tm, tn))


def _mm_nt(name, a, b, out_dtype, tm=512, tn=None):
    m, k = a.shape
    n = b.shape[0]
    tm, tn = min(tm, m), n if tn is None else tn
    return _matmul(name, [a, b],
                   [pl.BlockSpec((tm, k), lambda i, j, s: (i, 0)), pl.BlockSpec((tn, k), lambda i, j, s: (j, 0))],
                   lambda x, y: _dot(x, y, 1, 1), (m // tm, n // tn, 1), _sds((m, n), out_dtype),
                   pl.BlockSpec((tm, tn), lambda i, j, s: (i, j)), (tm, tn))


def _mm_tn(name, a, b, out_dtype, tm=512, tn=None, tk=2048, a_fn=lambda x: x, after=()):
    k, m = a.shape
    n = b.shape[1]
    tm, tk, tn = min(tm, m), min(tk, k), n if tn is None else tn
    return _matmul(name, [a, b],
                   [pl.BlockSpec((tk, tm), lambda i, j, s: (s, i)), pl.BlockSpec((tk, tn), lambda i, j, s: (s, j))],
                   lambda x, y: _dot(a_fn(x), y, 0, 0), (m // tm, n // tn, k // tk), _sds((m, n), out_dtype),
                   pl.BlockSpec((tm, tn), lambda i, j, s: (i, j)), (tm, tn), after)


def _mm_contract_slots(name, pairs, out_dtype, per_step, tm=512, tn=2048, after=()):
    s_, m, k = pairs[0][0].shape
    n = pairs[0][1].shape[2]
    tm, tn = min(tm, m), min(tn, n)
    ops, specs = [], []
    for a, b in pairs:
        ops += [a, b]
        specs += [pl.BlockSpec((per_step, tm, k), lambda i, j, s: (s, i, 0)),
                  pl.BlockSpec((per_step, k, tn), lambda i, j, s: (s, 0, j))]

    def product(*t):
        return sum(_dot(t[2 * p][q], t[2 * p + 1][q], 1, 0) for p in range(len(pairs)) for q in range(per_step))

    return _matmul(name, ops, specs, product, (m // tm, n // tn, s_ // per_step), _sds((m, n), out_dtype),
                   pl.BlockSpec((tm, tn), lambda i, j, s: (i, j)), (tm, tn), after)


def _mm_slots_tn(name, a, b, out_dtype, tn=2048, tk=2048):
    s_, k, m = a.shape
    n = b.shape[1]
    tn, tk = min(tn, n), min(tk, k)
    return _matmul(name, [a, b],
                   [pl.BlockSpec((None, tk, m), lambda s, j, z: (s, z, 0)), pl.BlockSpec((tk, tn), lambda s, j, z: (z, j))],
                   lambda x, y: _dot(x, y, 0, 0), (s_, n // tn, k // tk), _sds((s_, m, n), out_dtype),
                   pl.BlockSpec((None, m, tn), lambda s, j, z: (s, 0, j)), (m, tn))


def _ffn_in(a, w_gate, w_up, tm=512):
    m, k = a.shape
    s_, n, _ = w_gate.shape
    tm = min(tm, m)

    def body(a_ref, wg_ref, wu_ref, g_ref, u_ref, h_ref):
        x = a_ref[...]
        g = _dot(x, wg_ref[...], 1, 1)
        u = _dot(x, wu_ref[...], 1, 1)
        g_ref[...] = g.astype(BF16)
        u_ref[...] = u.astype(BF16)
        h_ref[...] = (g * _sigmoid(g) * u).astype(BF16)

    w_spec = pl.BlockSpec((None, n, k), lambda s, i: (s, 0, 0))
    o_spec = pl.BlockSpec((None, tm, n), lambda s, i: (s, i, 0))
    return _call(body, name="ffn_in", grid=(s_, m // tm),
                 in_specs=[pl.BlockSpec((tm, k), lambda s, i: (i, 0)), w_spec, w_spec], out_specs=[o_spec] * 3,
                 out_shape=[_sds((s_, m, n), BF16)] * 3, semantics=("parallel", "parallel"))(a, w_gate, w_up)


def _ffn_down_bwd(d_out, w_down, gate, up, after, tm=512):
    m, k = d_out.shape
    s_, n, _ = w_down.shape
    tm = min(tm, m)

    def body(d_ref, w_ref, g_ref, u_ref, dg_ref, du_ref):
        dh = _dot(d_ref[...], w_ref[...], 1, 1)
        g = g_ref[...].astype(F32)
        sg = _sigmoid(g)
        dg_ref[...] = (dh * u_ref[...].astype(F32) * sg * (1.0 + g * (1.0 - sg))).astype(BF16)
        du_ref[...] = (dh * g * sg).astype(BF16)

    t_spec = pl.BlockSpec((None, tm, n), lambda s, i: (s, i, 0))
    return _call(body, name="ffn_down_dx", grid=(s_, m // tm),
                 in_specs=[pl.BlockSpec((tm, k), lambda s, i: (i, 0)), pl.BlockSpec((None, n, k), lambda s, i: (s, 0, 0)),
                           t_spec, t_spec],
                 out_specs=[t_spec] * 2, out_shape=[_sds((s_, m, n), BF16)] * 2, semantics=("parallel", "parallel"),
                 n_after=len(after))(d_out, w_down, gate, up, *after)


ALL_PEERS = (1, 2, 3, 4, 5, 6, 7)
CHIP_PEERS = (2, 4, 6)
SIBLING = 1


def _peer(relation):
    x, y, c = lax.axis_index("x"), lax.axis_index("y"), lax.axis_index("c")
    pos = (1 - x if relation & 4 else x, 1 - y if relation & 2 else y, 1 - c if relation & 1 else c)
    return pos, 4 * pos[0] + 2 * pos[1] + pos[2]


def _slot(relation, by_chip):
    pos, device = _peer(relation)
    return 2 * pos[0] + pos[1] if by_chip else device


def _exchange_copies(ins, lands, send_sems, recv_sems, scatter, relations, by_chip=False):
    me = _slot(0, by_chip)

    def copy(a, s, peer, pos, dst_slot):
        return pltpu.make_async_remote_copy(
            src_ref=ins[a].at[peer] if scatter else ins[a], dst_ref=lands[a].at[dst_slot],
            send_sem=send_sems.at[s], recv_sem=recv_sems.at[s], device_id=pos, device_id_type=pl.DeviceIdType.MESH)

    pairs = []
    for k, r in enumerate(relations):
        pos, peer = _peer(r)[0], _slot(r, by_chip)
        for a in range(len(ins)):
            s = a * len(relations) + k
            pairs.append((copy(a, s, peer, pos, me), copy(a, s, peer, pos, peer)))
    return pairs


def _halves_copies(arrays, lands, send_sems, recv_sems):
    sibling, _ = _peer(SIBLING)
    core = lax.axis_index("c")
    pairs = []
    for a, (ref, land) in enumerate(zip(arrays, lands)):
        send = pltpu.make_async_remote_copy(
            src_ref=ref.at[:, pl.ds(1 - core, 1)], dst_ref=land, send_sem=send_sems.at[a], recv_sem=recv_sems.at[a],
            device_id=sibling, device_id_type=pl.DeviceIdType.MESH)
        pairs.append((send, send))
    return pairs


def _forward_copies(lands, send_sems, recv_sems):
    sibling, _ = _peer(SIBLING)

    def copy(a, s, slot):
        return pltpu.make_async_remote_copy(
            src_ref=lands[a].at[slot], dst_ref=lands[a].at[slot], send_sem=send_sems.at[s], recv_sem=recv_sems.at[s],
            device_id=sibling, device_id_type=pl.DeviceIdType.MESH)

    pairs = []
    for k, r in enumerate(CHIP_PEERS):
        _, mine = _peer(r)
        _, theirs = _peer(r | SIBLING)
        for a in range(len(lands)):
            s = a * len(CHIP_PEERS) + k
            pairs.append((copy(a, s, mine), copy(a, s, theirs)))
    return pairs


_HBM_SPEC = pl.BlockSpec(memory_space=pltpu.HBM)
_SEM_SPEC = pl.BlockSpec(memory_space=pltpu.SEMAPHORE)
_SIDE_EFFECT = pltpu.SideEffectType.DATAFLOW_SIDE_EFFECTING


def _split_start(name, operands, n_sem, make_pairs):
    k = len(operands)

    def body(*refs):
        send_sems, recv_sems, token = refs[k], refs[k + 1], refs[-1]
        for send, _ in make_pairs(refs[:k], send_sems, recv_sems):
            send.start()
        token[...] = jnp.zeros_like(token)

    out = pl.pallas_call(
        body, name=name,
        out_shape=(pltpu.SemaphoreType.DMA((n_sem,)), pltpu.SemaphoreType.DMA((n_sem,)),
                   *[pltpu.HBM(a.shape, a.dtype) for a in operands], _sds((SUBLANES, LANES), F32)),
        in_specs=[_HBM_SPEC] * k,
        out_specs=(_SEM_SPEC, _SEM_SPEC, *[_HBM_SPEC] * k, pl.BlockSpec(memory_space=pltpu.VMEM)),
        input_output_aliases={i: 2 + i for i in range(k)},
        compiler_params=pltpu.CompilerParams(has_side_effects=_SIDE_EFFECT),
    )(*[pltpu.with_memory_space_constraint(a, pltpu.HBM) for a in operands])
    return dict(name=name, sems=out[:2], thru=list(out[2:2 + k]), make_pairs=make_pairs), out[-1]


def _split_wait(handle, after):
    thru, make_pairs = handle["thru"], handle["make_pairs"]
    k = len(thru)

    def body(*refs):
        for send, arrival in make_pairs(refs[:k], refs[k], refs[k + 1]):
            send.wait_send()
            arrival.wait_recv()

    return pl.pallas_call(
        body, name=handle["name"] + "_wait", out_shape=[pltpu.HBM(a.shape, a.dtype) for a in thru],
        in_specs=[_HBM_SPEC] * k + [_SEM_SPEC, _SEM_SPEC] + [pl.BlockSpec(memory_space=pl.ANY)] * len(after),
        out_specs=[_HBM_SPEC] * k, input_output_aliases={i: i for i in range(k)},
        compiler_params=pltpu.CompilerParams(has_side_effects=_SIDE_EFFECT),
    )(*thru, *handle["sems"], *after)


def _exchange_start(name, arrays, scatter, relations=ALL_PEERS, by_chip=False):
    n = len(arrays)
    lands = [lax.empty(a.shape if scatter else (N_DEV,) + a.shape, a.dtype) for a in arrays]

    def make_pairs(refs, send_sems, recv_sems):
        return _exchange_copies(refs[:n], refs[n:], send_sems, recv_sems, scatter, relations, by_chip)

    handle, token = _split_start(name, list(arrays) + lands, n * len(relations), make_pairs)
    handle.update(n=n, scatter=scatter, by_chip=by_chip)
    return handle, token


def _halves_start(name, arrays):
    lands = [lax.empty((a.shape[0], 1) + a.shape[2:], a.dtype) for a in arrays]
    n = len(arrays)

    def make_pairs(refs, send_sems, recv_sems):
        return _halves_copies(refs[:n], refs[n:], send_sems, recv_sems)

    return _split_start(name, list(arrays) + lands, n, make_pairs)


def _chip_sum(name, array, landed):
    chips, _, r, c = array.shape
    tr = r // 2 if r > 512 and r % 32 == 0 else r

    def body(a_ref, b_ref, o_ref):
        mine = a_ref[lax.axis_index("c")].astype(F32)
        o_ref[...] = (mine + b_ref[...].astype(F32)).astype(o_ref.dtype)

    return _call(body, name=name, grid=(chips, r // tr),
                 in_specs=[pl.BlockSpec((None, 2, tr, c), lambda k, i: (k, 0, i, 0)),
                           pl.BlockSpec((None, None, tr, c), lambda k, i: (k, 0, i, 0))],
                 out_specs=pl.BlockSpec((None, tr, c), lambda k, i: (k, i, 0)),
                 out_shape=_sds((chips, r, c), BF16), semantics=("parallel", "parallel"))(array, landed)


def _forward_start(name, lands):
    return _split_start(name, list(lands), len(lands) * len(CHIP_PEERS), _forward_copies)


def _exchange_wait(handle, after):
    n, scatter = handle["n"], handle["scatter"]
    out = _split_wait(handle, after)
    me = 2 * lax.axis_index("x") + lax.axis_index("y")
    if not handle["by_chip"]:
        me = 2 * me + lax.axis_index("c")
    done = []
    for src, land in zip(out[:n], out[n:]):
        own = lax.dynamic_index_in_dim(src, me, 0, keepdims=True) if scatter else src[None]
        done.append(lax.dynamic_update_slice_in_dim(land, own, me, 0))
    return done


def _rope_tables(pos_col):
    t = pos_col.shape[0]
    half = HEAD_DIM // 2
    inv_freq = ROPE_THETA ** (-jnp.arange(half, dtype=F32) / half)
    inv_row = jnp.tile(inv_freq, LANES // half)[None, :]

    def body(pos_ref, inv_ref, cos_ref, sin_ref):
        ang = pos_ref[...] * inv_ref[...]
        cos_ref[...] = jnp.cos(ang)
        sin_ref[...] = jnp.sin(ang)

    tm = min(t, 512)
    return _call(body, name="rope_tables", grid=(t // tm,),
                 in_specs=[pl.BlockSpec((tm, 1), lambda i: (i, 0)), pl.BlockSpec((1, LANES), lambda i: (0, 0))],
                 out_specs=[pl.BlockSpec((tm, LANES), lambda i: (i, 0))] * 2,
                 out_shape=[_sds((t, LANES), F32)] * 2, semantics=("parallel",))(pos_col, inv_row)


def _rot_half(x):
    lane = lax.broadcasted_iota(jnp.int32, x.shape, 1)
    low = (lane % HEAD_DIM) < HEAD_DIM // 2
    return jnp.where(low, -pltpu.roll(x, LANES - HEAD_DIM // 2, 1), pltpu.roll(x, HEAD_DIM // 2, 1))


def _rope(x, cos, sin):
    return x * cos + _rot_half(x) * sin


def _unrope(d, cos, sin):
    return d * cos - _rot_half(d) * sin


def _band_mask(first_block, heads):
    r = lax.broadcasted_iota(jnp.int32, (heads * BLOCK, 2 * BLOCK), 0) % BLOCK
    c = lax.broadcasted_iota(jnp.int32, (heads * BLOCK, 2 * BLOCK), 1)
    diff = r - c + BLOCK
    return (diff >= 0) & (diff < WINDOW) & ((c >= BLOCK) | jnp.logical_not(first_block))


def _attn_specs(t, d_attn, d_in):
    kb, vb = d_attn // D_KV, d_attn // D_KV + 1
    prev = lambda i: jnp.maximum(i - 1, 0)
    return [
        pl.BlockSpec((BLOCK, d_attn), lambda i: (i, 0)),
        pl.BlockSpec((BLOCK, D_KV), lambda i: (i, kb)),
        pl.BlockSpec((BLOCK, D_KV), lambda i: (i, vb)),
        pl.BlockSpec((BLOCK, D_KV), lambda i: (prev(i), kb)),
        pl.BlockSpec((BLOCK, D_KV), lambda i: (prev(i), vb)),
        pl.BlockSpec((BLOCK, LANES), lambda i: (i, 0)),
        pl.BlockSpec((BLOCK, LANES), lambda i: (i, 0)),
        pl.BlockSpec((BLOCK, LANES), lambda i: (prev(i), 0)),
        pl.BlockSpec((BLOCK, LANES), lambda i: (prev(i), 0)),
        pl.BlockSpec((1, LANES), lambda i: (0, 0)),
    ]


def _head(x, h):
    return x[:, h * HEAD_DIM:(h + 1) * HEAD_DIM]


def _attn_heads(q_ref, kc_ref, vc_ref, kp_ref, vp_ref, cq_ref, sq_ref, cp_ref, sp_ref, d_attn):
    cq, sq, cp, sp = cq_ref[...], sq_ref[...], cp_ref[...], sp_ref[...]
    q_rot = [_rope(q_ref[:, j * LANES:(j + 1) * LANES], cq, sq) for j in range(d_attn // LANES)]
    kc_rot = [_rope(kc_ref[:, j * LANES:(j + 1) * LANES], cq, sq) for j in range(D_KV // LANES)]
    kp_rot = [_rope(kp_ref[:, j * LANES:(j + 1) * LANES], cp, sp) for j in range(D_KV // LANES)]
    per = LANES // HEAD_DIM
    q_heads = [_head(q_rot[h // per], h % per).astype(BF16) for h in range(d_attn // HEAD_DIM)]
    kk = [jnp.concatenate([_head(kp_rot[g // per], g % per), _head(kc_rot[g // per], g % per)], axis=0).astype(BF16)
          for g in range(N_KV_HEADS)]
    vv = [jnp.concatenate([_head(vp_ref[...], g), _head(vc_ref[...], g)], axis=0).astype(BF16) for g in range(N_KV_HEADS)]
    return q_heads, kk, vv


def _stack_group(q_heads, sink_ref, group):
    q_all = jnp.concatenate([q_heads[h] for h in group], axis=0)
    sink_all = jnp.concatenate([jnp.broadcast_to(sink_ref[:, h:h + 1], (BLOCK, 1)) for h in group], axis=0)
    return q_all, sink_all


def _softmax_with_sink(q, kk, sink, mask):
    s = _dot(q, kk, 1, 1) * (1.0 / math.sqrt(HEAD_DIM))
    s = jnp.where(mask, s, MASKED)
    m = jnp.maximum(jnp.max(s, axis=-1, keepdims=True), sink)
    p = jnp.exp(s - m)
    e_sink = jnp.exp(sink - m)
    inv = 1.0 / (jnp.sum(p, axis=-1, keepdims=True) + e_sink)
    return p * inv, e_sink * inv


def _attention_fwd(proj, cos, sin, sinks_row, d_attn):
    t, d_in = proj.shape
    n_heads = d_attn // HEAD_DIM
    q_per_kv = n_heads // N_KV_HEADS

    def body(q_ref, kc_ref, vc_ref, kp_ref, vp_ref, cq_ref, sq_ref, cp_ref, sp_ref, sink_ref, o_ref):
        mask = _band_mask(pl.program_id(0) == 0, q_per_kv)
        q_heads, kk, vv = _attn_heads(q_ref, kc_ref, vc_ref, kp_ref, vp_ref, cq_ref, sq_ref, cp_ref, sp_ref, d_attn)
        for g in range(N_KV_HEADS):
            group = range(g * q_per_kv, (g + 1) * q_per_kv)
            q_all, sink_all = _stack_group(q_heads, sink_ref, group)
            probs, _ = _softmax_with_sink(q_all, kk[g], sink_all, mask)
            o_all = _dot(probs.astype(BF16), vv[g], 1, 0)
            for k, h in enumerate(group):
                o_ref[:, h * HEAD_DIM:(h + 1) * HEAD_DIM] = o_all[k * BLOCK:(k + 1) * BLOCK]

    return _call(body, name="attention_fwd", grid=(t // BLOCK,), in_specs=_attn_specs(t, d_attn, d_in),
                 out_specs=pl.BlockSpec((BLOCK, d_attn), lambda i: (i, 0)), out_shape=_sds((t, d_attn), F32),
                 semantics=("parallel",))(proj, proj, proj, proj, proj, cos, sin, cos, sin, sinks_row)


def _attention_bwd(proj, cos, sin, sinks_row, d_out, d_attn):
    t, d_in = proj.shape
    n_heads = d_attn // HEAD_DIM
    q_per_kv = n_heads // N_KV_HEADS
    nb = t // BLOCK
    per = LANES // HEAD_DIM

    def body(q_ref, kc_ref, vc_ref, kp_ref, vp_ref, cq_ref, sq_ref, cp_ref, sp_ref, sink_ref, do_ref,
             dq_ref, dk_ref, dv_ref, dsink_ref):
        i = pl.program_id(0)
        mask = _band_mask(i == 0, q_per_kv)
        q_heads, kk, vv = _attn_heads(q_ref, kc_ref, vc_ref, kp_ref, vp_ref, cq_ref, sq_ref, cp_ref, sp_ref, d_attn)
        lane = lax.broadcasted_iota(jnp.int32, (1, LANES), 1)
        dsink = jnp.zeros((1, LANES), F32)
        dq_rot, dkk, dvv = [], [], []
        for g in range(N_KV_HEADS):
            group = range(g * q_per_kv, (g + 1) * q_per_kv)
            q_all, sink_all = _stack_group(q_heads, sink_ref, group)
            probs, p_sink = _softmax_with_sink(q_all, kk[g], sink_all, mask)
            do_all = jnp.concatenate([do_ref[:, h * HEAD_DIM:(h + 1) * HEAD_DIM] for h in group], axis=0).astype(BF16)
            dp = _dot(do_all, vv[g], 1, 1)
            delta = jnp.sum(probs * dp, axis=-1, keepdims=True)
            ds = (probs * (dp - delta) * (1.0 / math.sqrt(HEAD_DIM))).astype(BF16)
            dq_all = _dot(ds, kk[g], 1, 0)
            dkk.append(_dot(ds, q_all, 0, 0))
            dvv.append(_dot(probs.astype(BF16), do_all, 0, 0))
            sink_term = p_sink * delta
            for k, h in enumerate(group):
                dq_rot.append(dq_all[k * BLOCK:(k + 1) * BLOCK])
                part = jnp.sum(sink_term[k * BLOCK:(k + 1) * BLOCK], axis=0, keepdims=True)
                dsink += jnp.where(lane == h, -part, 0.0)
        cq, sq, cp, sp = cq_ref[...], sq_ref[...], cp_ref[...], sp_ref[...]
        for j in range(d_attn // LANES):
            d = jnp.concatenate(dq_rot[j * per:(j + 1) * per], axis=1)
            dq_ref[:, j * LANES:(j + 1) * LANES] = _unrope(d, cq, sq)
        for j in range(D_KV // LANES):
            d = jnp.concatenate(dkk[j * per:(j + 1) * per], axis=1)
            dk_ref[0, :, j * LANES:(j + 1) * LANES] = _unrope(d[:BLOCK], cp, sp)
            dk_ref[1, :, j * LANES:(j + 1) * LANES] = _unrope(d[BLOCK:], cq, sq)
            d = jnp.concatenate(dvv[j * per:(j + 1) * per], axis=1)
            dv_ref[0, :, j * LANES:(j + 1) * LANES] = d[:BLOCK]
            dv_ref[1, :, j * LANES:(j + 1) * LANES] = d[BLOCK:]

        @pl.when(i == 0)
        def _():
            dsink_ref[...] = jnp.zeros_like(dsink_ref)

        dsink_ref[...] += dsink

    pair = pl.BlockSpec((2, BLOCK, D_KV), lambda i: (i, 0, 0))
    return _call(body, name="attention_bwd", grid=(nb,),
                 in_specs=_attn_specs(t, d_attn, d_in) + [pl.BlockSpec((BLOCK, d_attn), lambda i: (i, 0))],
                 out_specs=[pl.BlockSpec((BLOCK, d_attn), lambda i: (i, 0)), pair, pair,
                            pl.BlockSpec((1, LANES), lambda i: (0, 0))],
                 out_shape=[_sds((t, d_attn), F32), _sds((2 * nb, BLOCK, D_KV), F32), _sds((2 * nb, BLOCK, D_KV), F32),
                            _sds((1, LANES), F32)],
                 semantics=("arbitrary",))(proj, proj, proj, proj, proj, cos, sin, cos, sin, sinks_row, d_out)


def _assemble_dproj(dq, dk2, dv2, du, d_in, after):
    t, d_attn = dq.shape
    d_ssm = du.shape[1]
    nb = t // BLOCK

    def body(dq_ref, dk_own, dk_next, dv_own, dv_next, du_ref, o_ref):
        has_next = (pl.program_id(0) < nb - 1).astype(F32)
        o_ref[:, :d_attn] = dq_ref[...].astype(BF16)
        o_ref[:, d_attn:d_attn + D_KV] = (dk_own[...] + has_next * dk_next[...]).astype(BF16)
        o_ref[:, d_attn + D_KV:d_attn + 2 * D_KV] = (dv_own[...] + has_next * dv_next[...]).astype(BF16)
        o_ref[:, d_attn + 2 * D_KV:] = du_ref[...].astype(BF16)

    own = pl.BlockSpec((None, BLOCK, D_KV), lambda i: (2 * i + 1, 0, 0))
    nxt = pl.BlockSpec((None, BLOCK, D_KV), lambda i: (jnp.minimum(2 * i + 2, 2 * nb - 1), 0, 0))
    return _call(body, name="assemble_dproj", grid=(nb,),
                 in_specs=[pl.BlockSpec((BLOCK, d_attn), lambda i: (i, 0)), own, nxt, own, nxt,
                           pl.BlockSpec((BLOCK, d_ssm), lambda i: (i, 0))],
                 out_specs=pl.BlockSpec((BLOCK, d_in), lambda i: (i, 0)), out_shape=_sds((t, d_in), BF16),
                 semantics=("parallel",), n_after=len(after))(dq, dk2, dk2, dv2, dv2, du, *after)


def _discretise(ar, ai, ldt, br, bi):
    dt = jnp.exp(ldt)
    mag = jnp.exp(ar * dt)
    lam_re = mag * jnp.cos(ai * dt)
    lam_im = mag * jnp.sin(ai * dt)
    den = ar * ar + ai * ai
    nr = lam_re - 1.0
    ni = lam_im
    f_re = (nr * ar + ni * ai) / den
    f_im = (ni * ar - nr * ai) / den
    return (lam_re, lam_im, [f_re * r - f_im * i for r, i in zip(br, bi)], [f_re * i + f_im * r for r, i in zip(br, bi)])


def _whole(arrays):
    return [pl.BlockSpec(a.shape, lambda *_, nd=len(a.shape): (0,) * nd) for a in arrays]


def _channels(ref):
    groups = ref.shape[0] // SSM_GROUP
    return [ref[pl.ds(p, groups, stride=SSM_GROUP), :] for p in range(SSM_GROUP)]


def _store_channels(ref, values):
    groups = ref.shape[0] // SSM_GROUP
    for p, val in enumerate(values):
        ref[pl.ds(p, groups, stride=SSM_GROUP), :] = val


def _s5_discretise(ar, ai, ldt, br, bi):
    ins = [ar, ai, ldt, br, bi]

    def body(ar_ref, ai_ref, ldt_ref, br_ref, bi_ref, lr_ref, li_ref, bbr_ref, bbi_ref):
        lr, li, bbr, bbi = _discretise(ar_ref[...], ai_ref[...], ldt_ref[...], _channels(br_ref), _channels(bi_ref))
        lr_ref[...] = lr
        li_ref[...] = li
        _store_channels(bbr_ref, bbr)
        _store_channels(bbi_ref, bbi)

    outs = [_sds(ar.shape, F32), _sds(ar.shape, F32), _sds(br.shape, F32), _sds(br.shape, F32)]
    return _call(body, name="s5_discretise", in_specs=_whole(ins), out_specs=_whole(outs), out_shape=outs)(*ins)


def _s5_discretise_bwd(ar, ai, ldt, br, bi, d_lr, d_li, d_bbr, d_bbi):
    ins = [ar, ai, ldt, br, bi, d_lr, d_li, d_bbr, d_bbi]

    def body(ar_ref, ai_ref, ldt_ref, br_ref, bi_ref, dlr_ref, dli_ref, dbbr_ref, dbbi_ref,
             dar_ref, dai_ref, dldt_ref, dbr_ref, dbi_ref):
        _, vjp = jax.vjp(_discretise, ar_ref[...], ai_ref[...], ldt_ref[...], _channels(br_ref), _channels(bi_ref))
        dar, dai, dldt, dbr, dbi = vjp((dlr_ref[...], dli_ref[...], _channels(dbbr_ref), _channels(dbbi_ref)))
        dar_ref[...] = dar
        dai_ref[...] = dai
        dldt_ref[...] = dldt
        _store_channels(dbr_ref, dbr)
        _store_channels(dbi_ref, dbi)

    outs = [_sds(a.shape, F32) for a in (ar, ai, ldt, br, bi)]
    return _call(body, name="s5_discretise_bwd", in_specs=_whole(ins), out_specs=_whole(outs), out_shape=outs)(*ins)


def _cmul(ar, ai, br, bi):
    return ar * br - ai * bi, ar * bi + ai * br


def _load_segmented(ref, tile0, n_tiles, seg):
    return jnp.concatenate([ref[pl.ds(tile0 + j, SUBLANES, stride=seg), :] for j in range(n_tiles)], axis=0)


def _store_segmented(ref, tile0, seg, value):
    for j in range(value.shape[0] // SUBLANES):
        ref[pl.ds(tile0 + j, SUBLANES, stride=seg), :] = value[j * SUBLANES:(j + 1) * SUBLANES, :]


def _fill_powers(lr, li, pr_ref, pi_ref, seg):
    pows = [(lr, li)]
    for _ in range(SUBLANES - 1):
        pows.append(_cmul(pows[-1][0], pows[-1][1], lr, li))
    row = lax.broadcasted_iota(jnp.int32, (SUBLANES, lr.shape[1]), 0)
    tr = jnp.zeros((SUBLANES, lr.shape[1]), F32)
    ti = jnp.zeros((SUBLANES, lr.shape[1]), F32)
    for r in range(SUBLANES):
        tr = jnp.where(row == r, pows[r][0], tr)
        ti = jnp.where(row == r, pows[r][1], ti)
    pr_ref[0:SUBLANES, :] = tr
    pi_ref[0:SUBLANES, :] = ti
    k = SUBLANES
    while k < seg:
        fr, fi = pr_ref[k - 1:k, :], pi_ref[k - 1:k, :]
        for t0 in range(0, k, SUBLANES):
            nr, ni = _cmul(pr_ref[t0:t0 + SUBLANES, :], pi_ref[t0:t0 + SUBLANES, :], fr, fi)
            pr_ref[k + t0:k + t0 + SUBLANES, :] = nr
            pi_ref[k + t0:k + t0 + SUBLANES, :] = ni
        k *= 2


def _scan_segments(sr_ref, si_ref, pr_ref, pi_ref, lr, li, seg, reverse, per_tile=None):
    w = lr.shape[1]
    sign = -1.0 if reverse else 1.0
    lrb = jnp.broadcast_to(lr, (SUBLANES, w))
    lib = jnp.broadcast_to(sign * li, (SUBLANES, w))
    zero = jnp.zeros((SUBLANES, w), F32)

    def tile_rows(j):
        return pl.ds(pl.multiple_of(j * SUBLANES, SUBLANES), SUBLANES)

    def local(i, carry):
        rows = tile_rows(seg - 1 - i if reverse else i)
        pr, pi = _cmul(lrb, lib, carry[0], carry[1])
        xr, xi = sr_ref[rows, :] + pr, si_ref[rows, :] + pi
        sr_ref[rows, :] = xr
        si_ref[rows, :] = xi
        return xr, xi

    end_r, end_i = lax.fori_loop(0, seg, local, (zero, zero))
    full_r, full_i = pr_ref[seg - 1:seg, :], sign * pi_ref[seg - 1:seg, :]
    row = lax.broadcasted_iota(jnp.int32, (SUBLANES, w), 0)
    in_r, in_i = zero, zero
    cur_r, cur_i = jnp.zeros((1, w), F32), jnp.zeros((1, w), F32)
    for r in (range(SUBLANES - 2, -1, -1) if reverse else range(1, SUBLANES)):
        src = r + 1 if reverse else r - 1
        pr, pi = _cmul(full_r, full_i, cur_r, cur_i)
        cur_r, cur_i = end_r[src:src + 1, :] + pr, end_i[src:src + 1, :] + pi
        in_r = jnp.where(row == r, cur_r, in_r)
        in_i = jnp.where(row == r, cur_i, in_i)

    def carry_in(j, _):
        rows = tile_rows(j)
        k = seg - 1 - j if reverse else j
        pr, pi = _cmul(pr_ref[pl.ds(k, 1), :], sign * pi_ref[pl.ds(k, 1), :], in_r, in_i)
        xr, xi = sr_ref[rows, :] + pr, si_ref[rows, :] + pi
        sr_ref[rows, :] = xr
        si_ref[rows, :] = xi
        if per_tile is not None:
            per_tile(j, xr, xi)
        return 0

    lax.fori_loop(0, seg, carry_in, 0)


_S5_ROWS = 256


def _s5_in_specs(t, d_attn):
    u_block = (d_attn + 2 * D_KV) // SSM_CH_BLOCK
    blk3 = lambda shape: pl.BlockSpec((None,) + shape, lambda j: (j, 0, 0))
    return [
        pl.BlockSpec((t, SSM_CH_BLOCK), lambda j: (0, u_block + j)),
        blk3((SSM_CH_BLOCK, SSM_ST_BLOCK)), blk3((SSM_CH_BLOCK, SSM_ST_BLOCK)),
        blk3((1, SSM_ST_BLOCK)), blk3((1, SSM_ST_BLOCK)),
        blk3((SSM_ST_BLOCK, SSM_CH_BLOCK)), blk3((SSM_ST_BLOCK, SSM_CH_BLOCK)),
        pl.BlockSpec((1, SSM_CH_BLOCK), lambda j: (0, j)),
    ]


def _chunks(t):
    rows = min(_S5_ROWS, t)
    return rows, lambda i: pl.ds(pl.multiple_of(i * rows, rows), rows)


def _s5_states(u_ref, us_ref, bre_ref, bim_ref, lr_ref, li_ref, sr_ref, si_ref, pr_ref, pi_ref, t):
    seg = t // SUBLANES
    rows, chunk = _chunks(t)
    for c in range(t // rows):
        us_ref[c * rows:(c + 1) * rows, :] = _load_segmented(u_ref, c * rows // SUBLANES, rows // SUBLANES, seg)

    def fill(i, _):
        ub = us_ref[chunk(i), :].astype(BF16)
        sr_ref[chunk(i), :] = _dot(ub, bre_ref[...], 1, 0)
        si_ref[chunk(i), :] = _dot(ub, bim_ref[...], 1, 0)
        return 0

    lax.fori_loop(0, t // rows, fill, 0)
    _fill_powers(lr_ref[...], li_ref[...], pr_ref, pi_ref, seg)
    _scan_segments(sr_ref, si_ref, pr_ref, pi_ref, lr_ref[...], li_ref[...], seg, False)


def _s5_scratch(t):
    state = pltpu.VMEM((t, SSM_ST_BLOCK), F32)
    powers = pltpu.VMEM((t // SUBLANES, SSM_ST_BLOCK), F32)
    return state, powers, pltpu.VMEM((t, SSM_CH_BLOCK), F32)


def _s5_fwd(proj, mats, dskip_row, d_attn, d_ssm):
    t = proj.shape[0]
    seg = t // SUBLANES
    n_blocks = d_ssm // SSM_CH_BLOCK
    rows, chunk = _chunks(t)

    def body(u_ref, bre_ref, bim_ref, lr_ref, li_ref, cre_ref, cim_ref, d_ref, y_ref,
             sr_ref, si_ref, pr_ref, pi_ref, us_ref, ys_ref):
        _s5_states(u_ref, us_ref, bre_ref, bim_ref, lr_ref, li_ref, sr_ref, si_ref, pr_ref, pi_ref, t)

        def emit(i, _):
            ys_ref[chunk(i), :] = (_dot(sr_ref[chunk(i), :].astype(BF16), cre_ref[...], 1, 0)
                                   - _dot(si_ref[chunk(i), :].astype(BF16), cim_ref[...], 1, 0)
                                   + d_ref[...] * us_ref[chunk(i), :])
            return 0

        lax.fori_loop(0, t // rows, emit, 0)
        for c in range(t // rows):
            _store_segmented(y_ref, c * rows // SUBLANES, seg, ys_ref[c * rows:(c + 1) * rows, :])

    state, powers, channels = _s5_scratch(t)
    col = pl.BlockSpec((t, SSM_CH_BLOCK), lambda j: (0, j))
    return _call(body, name="s5_fwd", grid=(n_blocks,), in_specs=_s5_in_specs(t, d_attn), out_specs=col,
                 out_shape=_sds((t, d_ssm), F32), scratch_shapes=[state, state, powers, powers, channels, channels],
                 semantics=("parallel",))(proj, *mats, dskip_row)


def _s5_bwd(proj, mats, dskip_row, y, dz_a, dz_b, d_attn, d_ssm, after):
    t = proj.shape[0]
    seg = t // SUBLANES
    n_blocks = d_ssm // SSM_CH_BLOCK
    rows, chunk = _chunks(t)

    def body(u_ref, bre_ref, bim_ref, lr_ref, li_ref, cre_ref, cim_ref, d_ref, y_ref, dza_ref, dzb_ref,
             du_ref, dbre_ref, dbim_ref, dlr_ref, dli_ref, dcre_ref, dcim_ref, dd_ref,
             sr_ref, si_ref, gr_ref, gi_ref, pr_ref, pi_ref, us_ref, dys_ref, dus_ref, acc_r, acc_i):
        _s5_states(u_ref, us_ref, bre_ref, bim_ref, lr_ref, li_ref, sr_ref, si_ref, pr_ref, pi_ref, t)
        for ref in (dcre_ref, dcim_ref, dbre_ref, dbim_ref, dd_ref, acc_r, acc_i):
            ref[...] = jnp.zeros_like(ref)
        for c in range(t // rows):
            tile0, n_tiles = c * rows // SUBLANES, rows // SUBLANES
            dz = _load_segmented(dza_ref, tile0, n_tiles, seg) + _load_segmented(dzb_ref, tile0, n_tiles, seg)
            dys_ref[c * rows:(c + 1) * rows, :] = dz * _gelu_grad(_load_segmented(y_ref, tile0, n_tiles, seg))

        def through_c(i, _):
            dy = dys_ref[chunk(i), :]
            dd_ref[...] += jnp.sum(dy * us_ref[chunk(i), :], axis=0, keepdims=True)
            dyb = dy.astype(BF16)
            gr_ref[chunk(i), :] = _dot(dyb, cre_ref[...], 1, 1)
            gi_ref[chunk(i), :] = -_dot(dyb, cim_ref[...], 1, 1)
            dcre_ref[...] += _dot(sr_ref[chunk(i), :].astype(BF16), dyb, 0, 0)
            dcim_ref[...] -= _dot(si_ref[chunk(i), :].astype(BF16), dyb, 0, 0)
            return 0

        lax.fori_loop(0, t // rows, through_c, 0)

        row = lax.broadcasted_iota(jnp.int32, (SUBLANES, SSM_ST_BLOCK), 0)
        last = pl.ds((seg - 1) * SUBLANES, SUBLANES)
        wrap = [jnp.where(row == 0, 0.0, pltpu.roll(ref[last, :], 1, 0)) for ref in (sr_ref, si_ref)]

        def lambda_grad(j, g_re, g_im):
            before = pl.ds(pl.multiple_of(jnp.maximum(j - 1, 0) * SUBLANES, SUBLANES), SUBLANES)
            prev_r = jnp.where(j > 0, sr_ref[before, :], wrap[0])
            prev_i = jnp.where(j > 0, si_ref[before, :], wrap[1])
            acc_r[...] += g_re * prev_r + g_im * prev_i
            acc_i[...] += g_im * prev_r - g_re * prev_i

        _scan_segments(gr_ref, gi_ref, pr_ref, pi_ref, lr_ref[...], li_ref[...], seg, True, per_tile=lambda_grad)
        dlr_ref[...] = jnp.sum(acc_r[...], axis=0, keepdims=True)
        dli_ref[...] = jnp.sum(acc_i[...], axis=0, keepdims=True)

        def through_b(i, _):
            ub = us_ref[chunk(i), :].astype(BF16)
            grb, gib = gr_ref[chunk(i), :].astype(BF16), gi_ref[chunk(i), :].astype(BF16)
            dbre_ref[...] += _dot(ub, grb, 0, 0)
            dbim_ref[...] += _dot(ub, gib, 0, 0)
            dus_ref[chunk(i), :] = (_dot(grb, bre_ref[...], 1, 1) + _dot(gib, bim_ref[...], 1, 1)
                                    + d_ref[...] * dys_ref[chunk(i), :])
            return 0

        lax.fori_loop(0, t // rows, through_b, 0)
        for c in range(t // rows):
            _store_segmented(du_ref, c * rows // SUBLANES, seg, dus_ref[c * rows:(c + 1) * rows, :])

    col = pl.BlockSpec((t, SSM_CH_BLOCK), lambda j: (0, j))
    blk3 = lambda shape: pl.BlockSpec((None,) + shape, lambda j: (j, 0, 0))
    state, powers, channels = _s5_scratch(t)
    return _call(
        body, name="s5_bwd", grid=(n_blocks,), in_specs=_s5_in_specs(t, d_attn) + [col, col, col],
        out_specs=[col, blk3((SSM_CH_BLOCK, SSM_ST_BLOCK)), blk3((SSM_CH_BLOCK, SSM_ST_BLOCK)),
                   blk3((1, SSM_ST_BLOCK)), blk3((1, SSM_ST_BLOCK)),
                   blk3((SSM_ST_BLOCK, SSM_CH_BLOCK)), blk3((SSM_ST_BLOCK, SSM_CH_BLOCK)),
                   pl.BlockSpec((1, SSM_CH_BLOCK), lambda j: (0, j))],
        out_shape=[_sds((t, d_ssm), F32),
                   _sds((n_blocks, SSM_CH_BLOCK, SSM_ST_BLOCK), F32), _sds((n_blocks, SSM_CH_BLOCK, SSM_ST_BLOCK), F32),
                   _sds((n_blocks, 1, SSM_ST_BLOCK), F32), _sds((n_blocks, 1, SSM_ST_BLOCK), F32),
                   _sds((n_blocks, SSM_ST_BLOCK, SSM_CH_BLOCK), F32), _sds((n_blocks, SSM_ST_BLOCK, SSM_CH_BLOCK), F32),
                   _sds((1, d_ssm), F32)],
        scratch_shapes=[state, state, state, state, powers, powers, channels, channels, channels,
                        pltpu.VMEM((SUBLANES, SSM_ST_BLOCK), F32), pltpu.VMEM((SUBLANES, SSM_ST_BLOCK), F32)],
        semantics=("parallel",), n_after=len(after))(proj, *mats, dskip_row, y, dz_a, dz_b, *after)


def _by_block(gp_n):
    return gp_n.reshape(-1, GROUPS_PER_BLOCK, SSM_GROUP, SSM_STATE)


def _block_diag_in(bbar):
    eye = jnp.eye(GROUPS_PER_BLOCK, dtype=F32)
    return jnp.einsum("jgpn,gh->jgphn", _by_block(bbar), eye).reshape(-1, SSM_CH_BLOCK, SSM_ST_BLOCK)


def _block_diag_in_t(dense):
    d5 = dense.reshape(-1, GROUPS_PER_BLOCK, SSM_GROUP, GROUPS_PER_BLOCK, SSM_STATE)
    eye = jnp.eye(GROUPS_PER_BLOCK, dtype=F32)
    return jnp.einsum("jgphn,gh->jgpn", d5, eye).reshape(-1, SSM_STATE)


def _block_diag_out(c):
    eye = jnp.eye(GROUPS_PER_BLOCK, dtype=F32)
    return jnp.einsum("jgpn,gh->jgnhp", _by_block(c), eye).reshape(-1, SSM_ST_BLOCK, SSM_CH_BLOCK)


def _block_diag_out_t(dense):
    d5 = dense.reshape(-1, GROUPS_PER_BLOCK, SSM_STATE, GROUPS_PER_BLOCK, SSM_GROUP)
    eye = jnp.eye(GROUPS_PER_BLOCK, dtype=F32)
    return jnp.einsum("jgnhp,gh->jgpn", d5, eye).reshape(-1, SSM_STATE)


def _adamw(w, g, m, v):
    m = ADAM_B1 * m + (1.0 - ADAM_B1) * g
    v = ADAM_B2 * v + (1.0 - ADAM_B2) * (g * g)
    m_hat = m / (1.0 - ADAM_B1 ** ADAM_STEP)
    v_hat = v / (1.0 - ADAM_B2 ** ADAM_STEP)
    delta = -ADAM_LR * (m_hat / (jnp.sqrt(v_hat) + ADAM_EPS) + ADAM_WD * w)
    return delta, m, v


def _adam_sharded(name, parts, w, m, v, tr, row0=0):
    r, c = w.shape
    assert r % tr == 0 and row0 % tr == 0, (name, r, tr, row0)

    def body(p_ref, w_ref, m_ref, v_ref, g_out, d_out, m_out, v_out):
        g = p_ref[0].astype(F32)
        for i in range(1, p_ref.shape[0]):
            g = g + p_ref[i].astype(F32)
        delta, m_new, v_new = _adamw(w_ref[...], g, m_ref[...], v_ref[...])
        g_out[...] = g
        d_out[...] = delta
        m_out[...] = m_new
        v_out[...] = v_new

    tile = pl.BlockSpec((tr, c), lambda i: (i, 0))
    return _call(body, name=name, grid=(r // tr,),
                 in_specs=[pl.BlockSpec((parts.shape[0], tr, c), lambda i: (0, i + row0 // tr, 0)), tile, tile, tile],
                 out_specs=[tile] * 4, out_shape=[_sds((r, c), F32)] * 4, semantics=("parallel",))(parts, w, m, v)


_BIG = ("w_in", "w_glu", "w_o", "w_gate", "w_up", "w_down")
_BY_COLUMNS = ("w_in", "w_gate", "w_up")
_SMALL_VECTORS = ("sinks", "log_dt", "b_glu", "g_attn_out", "g_ssm_out", "g_post_mix", "g_pre_ffn", "g_post_ffn")
_SMALL_MATRICES = ("b_re", "b_im", "c_re", "c_im", "a_re", "a_im")
_ORDER = ("g_pre_mix", "w_in", "sinks", "a_re", "a_im", "log_dt", "b_re", "b_im", "c_re", "c_im", "d_skip", "w_glu",
          "b_glu", "g_attn_out", "g_ssm_out", "w_o", "g_post_mix", "g_pre_ffn", "w_gate", "w_up", "w_down",
          "g_post_ffn")


def _pack_grads(vectors, matrices):
    width = max(a.shape[1] for a in vectors)
    slots, row, lane = [], 0, 0
    for a in vectors:
        span = -(-a.shape[1] // LANES) * LANES
        if lane + span > width:
            row, lane = row + 1, 0
        slots.append((row, lane, a.shape[1]))
        lane += span
    firsts, at = [], 0
    for a in matrices:
        firsts.append(at)
        at += a.shape[0]
    nv = len(vectors)

    def body(*refs):
        vec_out, mat_out = refs[-2], refs[-1]
        vec_out[...] = jnp.zeros_like(vec_out)
        for ref, (r, l, w) in zip(refs[:nv], slots):
            vec_out[r:r + 1, l:l + w] = ref[...]
        for ref, r0 in zip(refs[nv:-2], firsts):
            mat_out[r0:r0 + ref.shape[0], :] = ref[...]

    ins = list(vectors) + list(matrices)
    outs = [_sds((-(-(row + 1) // SUBLANES) * SUBLANES, width), F32), _sds((at, matrices[0].shape[1]), F32)]
    vec_pack, mat_pack = _call(body, name="pack_small_grads", in_specs=_whole(ins), out_specs=_whole(outs),
                               out_shape=outs)(*ins)
    return vec_pack, slots, mat_pack, firsts


def _adam_replicated(sources, found_at, w, m, v):
    ns, n = len(sources), len(w)

    def body(*refs):
        ins, outs = refs[ns:ns + 3 * n], refs[ns + 3 * n:]
        summed = []
        for p_ref in refs[:ns]:
            g = p_ref[0]
            for k in range(1, N_DEV):
                g = g + p_ref[k]
            summed.append(g)
        for i, (src, row, lane) in enumerate(found_at):
            w_ref, m_ref, v_ref = ins[i], ins[n + i], ins[2 * n + i]
            rows, cols = w_ref.shape
            g = summed[src][row:row + rows, lane:lane + cols]
            delta, m_new, v_new = _adamw(w_ref[...], g, m_ref[...], v_ref[...])
            for o, val in zip(outs[4 * i:4 * i + 4], (g, delta, m_new, v_new)):
                o[...] = val

    ins = list(sources) + list(w) + list(m) + list(v)
    outs = [_sds(a.shape, F32) for a in w for _ in range(4)]
    flat = _call(body, name="adam_replicated", in_specs=_whole(ins), out_specs=_whole(outs), out_shape=outs)(*ins)
    return [tuple(flat[4 * i:4 * i + 4]) for i in range(n)]


def kernel(x, positions, g_pre_mix, w_in, sinks, a_re, a_im, log_dt, b_re, b_im, c_re, c_im, d_skip, w_glu, b_glu, g_attn_out, g_ssm_out, w_o, g_post_mix, g_pre_ffn, w_gate, w_up, w_down, g_post_ffn, loss_target, m_g_pre_mix, m_w_in, m_sinks, m_a_re, m_a_im, m_log_dt, m_b_re, m_b_im, m_c_re, m_c_im, m_d_skip, m_w_glu, m_b_glu, m_g_attn_out, m_g_ssm_out, m_w_o, m_g_post_mix, m_g_pre_ffn, m_w_gate, m_w_up, m_w_down, m_g_post_ffn, v_g_pre_mix, v_w_in, v_sinks, v_a_re, v_a_im, v_log_dt, v_b_re, v_b_im, v_c_re, v_c_im, v_d_skip, v_w_glu, v_b_glu, v_g_attn_out, v_g_ssm_out, v_w_o, v_g_post_mix, v_g_pre_ffn, v_w_gate, v_w_up, v_w_down, v_g_post_ffn):
    given = dict(locals())
    weights = {n: given[n] for n in _ORDER}
    mom_m = {n: given["m_" + n] for n in _ORDER}
    mom_v = {n: given["v_" + n] for n in _ORDER}

    t, d = x.shape[1], x.shape[2]
    d_attn = d // 2
    d_ssm = d - d_attn
    d_in = d_attn + 2 * D_KV + d_ssm
    n_groups = d_ssm // SSM_GROUP
    n_heads = d_attn // HEAD_DIM
    tm = min(256, t)

    x2 = x[0]
    target = loss_target[0]

    def by_rows(n, a):
        return a[0].T if n in _BY_COLUMNS else a[0]

    def start_gather(name, ns, token):
        behind = 0 if token is None else token[0, 0].astype(BF16)
        shards = [by_rows(n, weights[n]).astype(BF16) + behind for n in ns]
        return _exchange_start(name, shards, False, (SIBLING,) + CHIP_PEERS)

    def forward_gather(handle, after):
        return _forward_start(handle["name"] + "_forward", _exchange_wait(handle, after))

    def finish_gather(handle, after):
        return _split_wait(forward_gather(handle, after)[0], [])

    ag_in, token = start_gather("gather_w_in", ["w_in"], None)
    ag_mix, token = start_gather("gather_w_glu_o", ["w_glu", "w_o"], token)
    ag_ffn_in, token = start_gather("gather_w_gate_up", ["w_gate", "w_up"], token)
    ag_down, token = start_gather("gather_w_down", ["w_down"], token)

    xn, = _rows("norm_in", lambda xv, g: ([_rms(xv)[0] * g], []), [x2], [g_pre_mix], [(d, BF16)], [], tm,
                after=[token])
    win_g, = finish_gather(ag_in, [xn])
    w_in_t = win_g.reshape(d_in, d)
    proj = _mm_nt("proj_in", xn, w_in_t, F32, tn=d_in // 4 if (d_in // 4) % LANES == 0 else None)

    cos, sin = _rope_tables(positions.reshape(t, 1).astype(F32))
    sinks_row = jnp.pad(sinks, ((0, 0), (0, LANES - n_heads)))
    attn = _attention_fwd(proj, cos, sin, sinks_row, d_attn)

    def view(n, a):
        if n in ("b_re", "b_im"):
            return jnp.transpose(a[0], (0, 2, 1)).reshape(-1, SSM_STATE)
        if n in ("c_re", "c_im"):
            return a[0].reshape(-1, SSM_STATE)
        return a[0].T if n == "d_skip" else a[0] if a.ndim == 3 else a

    def unview(n, val):
        if n in ("b_re", "b_im"):
            return jnp.transpose(val.reshape(n_groups, SSM_GROUP, SSM_STATE), (0, 2, 1))[None]
        if n in ("c_re", "c_im"):
            return val.reshape(1, n_groups, SSM_GROUP, SSM_STATE)
        return val.T[None] if n == "d_skip" else val[None] if weights[n].ndim == 3 else val

    b_re_v, b_im_v = view("b_re", b_re), view("b_im", b_im)
    ldt_col = log_dt.reshape(n_groups, 1)
    lam_re, lam_im, bbar_re, bbar_im = _s5_discretise(a_re[0], a_im[0], ldt_col, b_re_v, b_im_v)
    n_blocks = n_groups // GROUPS_PER_BLOCK
    mats = [_block_diag_in(bbar_re).astype(BF16), _block_diag_in(bbar_im).astype(BF16),
            lam_re.reshape(n_blocks, 1, SSM_ST_BLOCK), lam_im.reshape(n_blocks, 1, SSM_ST_BLOCK),
            _block_diag_out(view("c_re", c_re)).astype(BF16), _block_diag_out(view("c_im", c_im)).astype(BF16)]
    dskip_row = d_skip.reshape(1, d_ssm)
    forward_mix, _ = forward_gather(ag_mix, [attn])
    y_ssm = _s5_fwd(proj, mats, dskip_row, d_attn, d_ssm)
    gelu_bf16 = lambda yv: _gelu(yv).astype(BF16)
    wglu_g, wo_g = _split_wait(forward_mix, [y_ssm])
    w_glu_full = wglu_g.reshape(d_ssm, d_ssm)
    w_o_full = wo_g.reshape(d, d)
    glu_lin = _mm_nn("glu_gate", y_ssm, w_glu_full, F32, a_fn=gelu_bf16)

    def mix_prep(av, yv, gl, bg, ga, gs):
        ssm = _gelu(yv) * _sigmoid(gl + bg)
        return [jnp.concatenate([_rms(av)[0] * ga, _rms(ssm)[0] * gs], axis=1)], []

    mixed, = _rows("mix_prep", mix_prep, [attn, y_ssm, glu_lin], [b_glu, g_attn_out, g_ssm_out], [(d, BF16)], [], tm)
    mix = _mm_nn("mix_out", mixed, w_o_full, F32, tn=d // 2 if (d // 2) % LANES == 0 else None)

    def post_mix(xv, mv, gpm, gpf):
        h = xv + _rms(mv)[0] * gpm
        return [h, _rms(h)[0] * gpf], []

    forward_ffn_in, token = forward_gather(ag_ffn_in, [mix])
    h, hn = _rows("post_mix", post_mix, [x2, mix], [g_post_mix, g_pre_ffn], [(d, F32), (d, BF16)], [], tm,
                  after=[token])
    wgate_g, wup_g = _split_wait(forward_ffn_in, [hn])
    gate, up, hid = _ffn_in(hn, wgate_g, wup_g)
    wdown_g, = finish_gather(ag_down, [hid])
    ff = _mm_contract_slots("ffn_down", [(hid, wdown_g)], F32, per_step=4)

    def head(hv, fv, tv, gpo):
        out = hv + _rms(fv)[0] * gpo
        err = out - tv
        dout = err * (1.0 / d)
        dff, dg = _rms_bwd(fv, gpo, dout)
        loss = jnp.zeros((1, LANES), F32) + 0.5 * jnp.sum(err * err) * (1.0 / d)
        return [dff, dout], [dg, loss]

    dff, dh_out, dg_post_ffn, loss_row = _rows("loss_head", head, [h, ff, target], [g_post_ffn],
                                               [(d, BF16), (d, F32)], [d, LANES], tm)

    def swap_halves(name, grads):
        return _halves_start("swap_" + name, [g.reshape(N_DEV // 2, 2, *g.shape[1:]) for g in grads])

    def scatter_chip_sums(name, swap, after):
        both = _split_wait(swap, after)
        half = len(both) // 2
        sums = [_chip_sum("chip_sum_%s_%d" % (name, i), both[i], both[half + i]) for i in range(half)]
        return _exchange_start("scatter_" + name, sums, True, CHIP_PEERS, by_chip=True)

    dw_down = _mm_slots_tn("ffn_down_dw", hid, dff, BF16)
    swap_down, token = swap_halves("dw_down", [dw_down])
    dgate, dup = _ffn_down_bwd(dff, wdown_g, gate, up, [token])
    rs_down, token = scatter_chip_sums("dw_down", swap_down, [dgate])
    dhn = _mm_contract_slots("ffn_in_dx", [(dgate, wgate_g), (dup, wup_g)], F32, per_step=2, tm=1024, tn=1024,
                             after=[token])
    dw_gate = _mm_slots_tn("ffn_gate_dw", dgate, hn, BF16)
    dw_up = _mm_slots_tn("ffn_up_dw", dup, hn, BF16)
    swap_ffn_in, tok_ffn_in = swap_halves("dw_gate_up", [dw_gate, dw_up])

    def mid_bwd(dho, dhn_, hv, mv, gpf, gpm):
        d1, dgpf = _rms_bwd(hv, gpf, dhn_)
        dh_ = dho + d1
        dmix_, dgpm = _rms_bwd(mv, gpm, dh_)
        return [dh_, dmix_], [dgpf, dgpm]

    dh, dmix, dg_pre_ffn, dg_post_mix = _rows("mid_bwd", mid_bwd, [dh_out, dhn, h, mix], [g_pre_ffn, g_post_mix],
                                              [(d, F32), (d, BF16)], [d, d], tm, after=[tok_ffn_in])

    dmixed = _mm_nt("mix_out_dx", dmix, w_o_full, F32, tn=d // 2 if (d // 2) % LANES == 0 else None)
    rs_ffn_in, token = scatter_chip_sums("dw_gate_up", swap_ffn_in, [dmixed])
    dw_o = _mm_tn("mix_out_dw", mixed, dmix, BF16, tn=d // 2 if (d // 2) % LANES == 0 else None, after=[token])
    swap_o, tok_o = swap_halves("dw_o", [dw_o.reshape(N_DEV, d // N_DEV, d)])

    def mix_bwd(dm, av, yv, gl, bg, ga, gs):
        dattn_, dga = _rms_bwd(av, ga, dm[:, :d_attn])
        z = _gelu(yv)
        sg = _sigmoid(gl + bg)
        dssm, dgs = _rms_bwd(z * sg, gs, dm[:, d_attn:])
        dgl = dssm * z * sg * (1.0 - sg)
        return [dattn_, dssm * sg, dgl], [dga, dgs, jnp.sum(dgl, axis=0, keepdims=True)]

    dattn, dz_direct, dglu, dg_attn_out, dg_ssm_out, db_glu = _rows(
        "mix_bwd", mix_bwd, [dmixed, attn, y_ssm, glu_lin], [b_glu, g_attn_out, g_ssm_out],
        [(d_attn, F32), (d_ssm, F32), (d_ssm, BF16)], [d_attn, d_ssm, d_ssm], tm, after=[tok_o])
    dz_glu = _mm_nt("glu_gate_dx", dglu, w_glu_full, F32)
    dw_glu = _mm_tn("glu_gate_dw", y_ssm, dglu, BF16, a_fn=gelu_bf16)
    rs_o, token = scatter_chip_sums("dw_o", swap_o, [dz_glu, dw_glu])

    du, db_re_dense, db_im_dense, dlam_re, dlam_im, dc_re_dense, dc_im_dense, dd_skip = _s5_bwd(
        proj, mats, dskip_row, y_ssm, dz_direct, dz_glu, d_attn, d_ssm, [token])
    da_re, da_im, dlog_dt, db_re_v, db_im_v = _s5_discretise_bwd(
        a_re[0], a_im[0], ldt_col, b_re_v, b_im_v, dlam_re.reshape(n_groups, SSM_STATE),
        dlam_im.reshape(n_groups, SSM_STATE), _block_diag_in_t(db_re_dense), _block_diag_in_t(db_im_dense))
    dq, dk2, dv2, dsinks_row = _attention_bwd(proj, cos, sin, sinks_row, dattn, d_attn)

    small_grads = {
        "sinks": dsinks_row, "a_re": da_re, "a_im": da_im, "log_dt": dlog_dt.reshape(1, n_groups),
        "b_re": db_re_v, "b_im": db_im_v, "c_re": _block_diag_out_t(dc_re_dense),
        "c_im": _block_diag_out_t(dc_im_dense), "d_skip": dd_skip.reshape(n_groups, SSM_GROUP).T, "b_glu": db_glu,
        "g_attn_out": dg_attn_out, "g_ssm_out": dg_ssm_out, "g_post_mix": dg_post_mix, "g_pre_ffn": dg_pre_ffn,
        "g_post_ffn": dg_post_ffn,
    }
    vec_pack, vec_slots, mat_pack, mat_rows = _pack_grads([small_grads[n] for n in _SMALL_VECTORS],
                                                          [small_grads[n] for n in _SMALL_MATRICES])
    ag_small, token = _exchange_start("gather_small_grads", [vec_pack, mat_pack, small_grads["d_skip"]], False)
    dproj = _assemble_dproj(dq, dk2, dv2, du, d_in, [token])

    dxn = _mm_nn("proj_in_dx", dproj, w_in_t, F32, tn=d // 2 if (d // 2) % LANES == 0 else None)
    dw_in = _mm_tn("proj_in_dw", dproj, xn, BF16).reshape(N_DEV, d_in // N_DEV, d)
    swap_in, token = swap_halves("dw_in_glu", [dw_in, dw_glu.reshape(N_DEV, d_ssm // N_DEV, d_ssm)])

    def x_bwd(dh_, dxn_, xv, g):
        dx, dg = _rms_bwd(xv, g, dxn_)
        return [dh_ + dx], [dg]

    grad_x, dg_pre_mix = _rows("norm_in_bwd", x_bwd, [dh, dxn, x2], [g_pre_mix], [(d, F32)], [d], tm, after=[token])
    ag_last, token = _exchange_start("gather_g_pre_mix_grad", [dg_pre_mix], False)
    rs_in, token = scatter_chip_sums("dw_in_glu", swap_in, [grad_x, token])

    results = {}

    def adam_big(n, parts):
        r = parts.shape[1]
        results[n] = _adam_sharded("adam_" + n, parts, by_rows(n, weights[n]), by_rows(n, mom_m[n]),
                                   by_rows(n, mom_v[n]), 64 if r % 64 == 0 else r)
        return results[n][3]

    done = [grad_x, token]
    adam_big("w_down", _exchange_wait(rs_down, done)[0])
    p_gate, p_up = _exchange_wait(rs_ffn_in, done)
    done = [adam_big("w_gate", p_gate), adam_big("w_up", p_up), results["w_down"][3]]
    done = [adam_big("w_o", _exchange_wait(rs_o, done)[0])]
    vec_parts, mat_parts, dskip_parts = _exchange_wait(ag_small, done)
    first_gain_parts, = _exchange_wait(ag_last, done)
    for n, row0 in zip(_SMALL_MATRICES, mat_rows):
        rows = view(n, weights[n]).shape[0]
        results[n] = _adam_sharded("adam_" + n, mat_parts, view(n, weights[n]), view(n, mom_m[n]), view(n, mom_v[n]),
                                   rows, row0)
    rest = _SMALL_VECTORS + ("d_skip", "g_pre_mix")
    found_at = [(0, row, lane) for row, lane, _ in vec_slots] + [(1, 0, 0), (2, 0, 0)]
    updated = _adam_replicated([vec_parts, dskip_parts, first_gain_parts], found_at,
                               [view(n, weights[n]) for n in rest], [view(n, mom_m[n]) for n in rest],
                               [view(n, mom_v[n]) for n in rest])
    results.update(zip(rest, updated))
    p_in, p_glu = _exchange_wait(rs_in, [results[n][3] for n in _SMALL_MATRICES] + [updated[0][3]])
    adam_big("w_in", p_in)
    adam_big("w_glu", p_glu)

    loss = lax.psum(loss_row[0, 0], ("x", "y", "c"))
    outs = [loss, grad_x[None]]
    for k in range(4):
        for n in _ORDER:
            val = results[n][k]
            outs.append(val.T[None] if n in _BY_COLUMNS else val[None] if n in _BIG else unview(n, val))
    return tuple(outs)
```

```python
import math

import jax
import jax.numpy as jnp
from jax import lax
from jax.experimental import pallas as pl
from jax.experimental.pallas import tpu as pltpu

F32 = jnp.float32
BF16 = jnp.bfloat16

HEAD_DIM = 64
N_KV_HEADS = 4
D_KV = N_KV_HEADS * HEAD_DIM
WINDOW = 128
BLOCK = 128
ROPE_THETA = 10000.0
SSM_GROUP = 16
SSM_STATE = 64
GROUPS_PER_BLOCK = 8
SSM_CH_BLOCK = GROUPS_PER_BLOCK * SSM_GROUP
SSM_ST_BLOCK = GROUPS_PER_BLOCK * SSM_STATE
RMS_EPS = 1e-6
N_DEV = 8
LANES = 128
SUBLANES = 8
MASKED = -1e30

ADAM_LR = 0.001
ADAM_B1 = 0.9
ADAM_B2 = 0.999
ADAM_EPS = 1e-08
ADAM_WD = 0.01
ADAM_STEP = 10

VMEM_LIMIT_BYTES = 56 * 1024 * 1024


def _call(body, *, name, out_shape, in_specs, out_specs, grid=(), scratch_shapes=(), semantics=None, n_after=0):
    params = dict(vmem_limit_bytes=VMEM_LIMIT_BYTES)
    if semantics is not None:
        params["dimension_semantics"] = semantics
    n_in = len(in_specs)
    if n_after:
        inner = body

        def body(*refs):
            inner(*refs[:n_in], *refs[n_in + n_after:])

        in_specs = list(in_specs) + [pl.BlockSpec(memory_space=pl.ANY)] * n_after
    return pl.pallas_call(body, name=name, grid=grid, in_specs=in_specs, out_specs=out_specs, out_shape=out_shape,
                          scratch_shapes=scratch_shapes, compiler_params=pltpu.CompilerParams(**params))


def _sds(shape, dtype):
    return jax.ShapeDtypeStruct(tuple(shape), dtype)


def _dot(a, b, ca, cb):
    return lax.dot_general(a, b, (((ca,), (cb,)), ((), ())), preferred_element_type=F32)


def _rms(x):
    r = lax.rsqrt(jnp.mean(x * x, axis=-1, keepdims=True) + RMS_EPS)
    return x * r, r


def _rms_bwd(x, g, dy):
    xh, r = _rms(x)
    dxh = dy * g
    dx = r * (dxh - xh * jnp.mean(dxh * xh, axis=-1, keepdims=True))
    return dx, jnp.sum(dy * xh, axis=0, keepdims=True)


def _sigmoid(x):
    return 1.0 / (1.0 + jnp.exp(-x))


_GELU_C = math.sqrt(2.0 / math.pi)
_GELU_A = 0.044715


def _gelu(y):
    t = jnp.tanh(_GELU_C * (y + _GELU_A * y * y * y))
    return 0.5 * y * (1.0 + t)


def _gelu_grad(y):
    t = jnp.tanh(_GELU_C * (y + _GELU_A * y * y * y))
    return 0.5 * (1.0 + t) + 0.5 * y * (1.0 - t * t) * _GELU_C * (1.0 + 3.0 * _GELU_A * y * y)


def _rows(name, fn, row_ins, vec_ins, row_outs, acc_widths, tm, after=()):
    rows = row_ins[0].shape[0]
    assert rows % tm == 0, (name, rows, tm)
    n_row, n_vec, n_out, n_acc = len(row_ins), len(vec_ins), len(row_outs), len(acc_widths)

    def body(*refs):
        ins = [r[...] for r in refs[:n_row + n_vec]]
        outs = refs[n_row + n_vec:n_row + n_vec + n_out]
        accs = refs[n_row + n_vec + n_out:]
        row_vals, acc_vals = fn(*ins)
        for o, v in zip(outs, row_vals):
            o[...] = v.astype(o.dtype)
        if n_acc:
            @pl.when(pl.program_id(0) == 0)
            def _():
                for a in accs:
                    a[...] = jnp.zeros_like(a)
            for a, v in zip(accs, acc_vals):
                a[...] += v

    in_specs = [pl.BlockSpec((tm, a.shape[1]), lambda i: (i, 0)) for a in row_ins]
    in_specs += [pl.BlockSpec(v.shape, lambda i: (0, 0)) for v in vec_ins]
    out_specs = [pl.BlockSpec((tm, w), lambda i: (i, 0)) for w, _ in row_outs]
    out_specs += [pl.BlockSpec((1, w), lambda i: (0, 0)) for w in acc_widths]
    out_shape = [_sds((rows, w), dt) for w, dt in row_outs] + [_sds((1, w), F32) for w in acc_widths]
    return _call(body, name=name, grid=(rows // tm,), in_specs=in_specs, out_specs=out_specs, out_shape=out_shape,
                 semantics=("arbitrary",) if n_acc else ("parallel",), n_after=len(after))(*row_ins, *vec_ins, *after)


def _matmul(name, operands, in_specs, product, grid, out_shape, out_spec, acc_shape, after=()):
    nk = grid[-1]
    n_in = len(operands)
    in_place = out_shape.dtype == F32

    def body(*refs):
        ins = [r[...] for r in refs[:n_in]]
        o_ref = refs[n_in]
        if nk == 1:
            o_ref[...] = product(*ins).astype(o_ref.dtype)
            return
        acc = o_ref if in_place else refs[n_in + 1]
        k = pl.program_id(len(grid) - 1)

        @pl.when(k == 0)
        def _():
            acc[...] = jnp.zeros_like(acc)

        acc[...] += product(*ins)

        if not in_place:
            @pl.when(k == nk - 1)
            def _():
                o_ref[...] = acc[...].astype(o_ref.dtype)

    return _call(body, name=name, grid=grid, in_specs=in_specs, out_specs=out_spec, out_shape=out_shape,
                 scratch_shapes=[] if nk == 1 or in_place else [pltpu.VMEM(acc_shape, F32)],
                 semantics=("parallel",) * (len(grid) - 1) + ("arbitrary",), n_after=len(after))(*operands, *after)


def _mm_nn(name, a, b, out_dtype, tm=512, tn=None, a_fn=lambda x: x):
    m, k = a.shape
    n = b.shape[1]
    tm, tn = min(tm, m), n if tn is None else tn
    return _matmul(name, [a, b],
                   [pl.BlockSpec((tm, k), lambda i, j, s: (i, 0)), pl.BlockSpec((k, tn), lambda i, j, s: (0, j))],
                   lambda x, y: _dot(a_fn(x), y, 1, 0), (m // tm, n // tn, 1), _sds((m, n), out_dtype),
                   pl.BlockSpec((tm, tn), lambda i, j, s: (i, j)), (tm, tn))


def _mm_nt(name, a, b, out_dtype, tm=512, tn=None):
    m, k = a.shape
    n = b.shape[0]
    tm, tn = min(tm, m), n if tn is None else tn
    return _matmul(name, [a, b],
                   [pl.BlockSpec((tm, k), lambda i, j, s: (i, 0)), pl.BlockSpec((tn, k), lambda i, j, s: (j, 0))],
                   lambda x, y: _dot(x, y, 1, 1), (m // tm, n // tn, 1), _sds((m, n), out_dtype),
                   pl.BlockSpec((tm, tn), lambda i, j, s: (i, j)), (tm, tn))


def _mm_tn(name, a, b, out_dtype, tm=512, tn=None, tk=2048, a_fn=lambda x: x, after=()):
    k, m = a.shape
    n = b.shape[1]
    tm, tk, tn = min(tm, m), min(tk, k), n if tn is None else tn
    return _matmul(name, [a, b],
                   [pl.BlockSpec((tk, tm), lambda i, j, s: (s, i)), pl.BlockSpec((tk, tn), lambda i, j, s: (s, j))],
                   lambda x, y: _dot(a_fn(x), y, 0, 0), (m // tm, n // tn, k // tk), _sds((m, n), out_dtype),
                   pl.BlockSpec((tm, tn), lambda i, j, s: (i, j)), (tm, tn), after)


def _mm_contract_slots(name, pairs, out_dtype, per_step, tm=512, tn=2048, after=()):
    s_, m, k = pairs[0][0].shape
    n = pairs[0][1].shape[2]
    tm, tn = min(tm, m), min(tn, n)
    ops, specs = [], []
    for a, b in pairs:
        ops += [a, b]
        specs += [pl.BlockSpec((per_step, tm, k), lambda i, j, s: (s, i, 0)),
                  pl.BlockSpec((per_step, k, tn), lambda i, j, s: (s, 0, j))]

    def product(*t):
        return sum(_dot(t[2 * p][q], t[2 * p + 1][q], 1, 0) for p in range(len(pairs)) for q in range(per_step))

    return _matmul(name, ops, specs, product, (m // tm, n // tn, s_ // per_step), _sds((m, n), out_dtype),
                   pl.BlockSpec((tm, tn), lambda i, j, s: (i, j)), (tm, tn), after)


def _mm_slots_tn(name, a, b, out_dtype, tn=2048, tk=2048):
    s_, k, m = a.shape
    n = b.shape[1]
    tn, tk = min(tn, n), min(tk, k)
    return _matmul(name, [a, b],
                   [pl.BlockSpec((None, tk, m), lambda s, j, z: (s, z, 0)), pl.BlockSpec((tk, tn), lambda s, j, z: (z, j))],
                   lambda x, y: _dot(x, y, 0, 0), (s_, n // tn, k // tk), _sds((s_, m, n), out_dtype),
                   pl.BlockSpec((None, m, tn), lambda s, j, z: (s, 0, j)), (m, tn))


def _ffn_in(a, w_gate, w_up, tm=512):
    m, k = a.shape
    s_, n, _ = w_gate.shape
    tm = min(tm, m)

    def body(a_ref, wg_ref, wu_ref, g_ref, u_ref, h_ref):
        x = a_ref[...]
        g = _dot(x, wg_ref[...], 1, 1)
        u = _dot(x, wu_ref[...], 1, 1)
        g_ref[...] = g.astype(BF16)
        u_ref[...] = u.astype(BF16)
        h_ref[...] = (g * _sigmoid(g) * u).astype(BF16)

    w_spec = pl.BlockSpec((None, n, k), lambda s, i: (s, 0, 0))
    o_spec = pl.BlockSpec((None, tm, n), lambda s, i: (s, i, 0))
    return _call(body, name="ffn_in", grid=(s_, m // tm),
                 in_specs=[pl.BlockSpec((tm, k), lambda s, i: (i, 0)), w_spec, w_spec], out_specs=[o_spec] * 3,
                 out_shape=[_sds((s_, m, n), BF16)] * 3, semantics=("parallel", "parallel"))(a, w_gate, w_up)


def _ffn_down_bwd(d_out, w_down, gate, up, after, tm=512):
    m, k = d_out.shape
    s_, n, _ = w_down.shape
    tm = min(tm, m)

    def body(d_ref, w_ref, g_ref, u_ref, dg_ref, du_ref):
        rows = pl.ds(pl.multiple_of(pl.program_id(1) * tm, tm), tm)
        dh = _dot(d_ref[rows, :], w_ref[...], 1, 1)
        g = g_ref[...].astype(F32)
        sg = _sigmoid(g)
        dg_ref[...] = (dh * u_ref[...].astype(F32) * sg * (1.0 + g * (1.0 - sg))).astype(BF16)
        du_ref[...] = (dh * g * sg).astype(BF16)

    t_spec = pl.BlockSpec((None, tm, n), lambda s, i: (s, i, 0))
    return _call(body, name="ffn_down_dx", grid=(s_, m // tm),
                 in_specs=[pl.BlockSpec((m, k), lambda s, i: (0, 0)), pl.BlockSpec((None, n, k), lambda s, i: (s, 0, 0)),
                           t_spec, t_spec],
                 out_specs=[t_spec] * 2, out_shape=[_sds((s_, m, n), BF16)] * 2, semantics=("parallel", "parallel"),
                 n_after=len(after))(d_out, w_down, gate, up, *after)


ALL_PEERS = (1, 2, 3, 4, 5, 6, 7)
CHIP_PEERS = (2, 4, 6)
SIBLING = 1
OWN = 0


def _peer(relation):
    x, y, c = lax.axis_index("x"), lax.axis_index("y"), lax.axis_index("c")
    pos = (1 - x if relation & 4 else x, 1 - y if relation & 2 else y, 1 - c if relation & 1 else c)
    return pos, 4 * pos[0] + 2 * pos[1] + pos[2]


def _slot(relation, by_chip):
    pos, device = _peer(relation)
    return 2 * pos[0] + pos[1] if by_chip else device


def _exchange_copies(ins, lands, send_sems, recv_sems, scatter, relations, by_chip=False):
    me = _slot(0, by_chip)

    def copy(a, s, peer, pos, dst_slot):
        return pltpu.make_async_remote_copy(
            src_ref=ins[a].at[peer] if scatter else ins[a], dst_ref=lands[a].at[dst_slot],
            send_sem=send_sems.at[s], recv_sem=recv_sems.at[s], device_id=pos, device_id_type=pl.DeviceIdType.MESH)

    pairs = []
    for k, r in enumerate(relations):
        pos, peer = _peer(r)[0], _slot(r, by_chip)
        for a in range(len(ins)):
            s = a * len(relations) + k
            pairs.append((copy(a, s, peer, pos, me), copy(a, s, peer, pos, peer)))
    return pairs


def _halves_copies(arrays, lands, send_sems, recv_sems):
    sibling, _ = _peer(SIBLING)
    core = lax.axis_index("c")
    pairs = []
    for a, (ref, land) in enumerate(zip(arrays, lands)):
        send = pltpu.make_async_remote_copy(
            src_ref=ref.at[:, pl.ds(1 - core, 1)], dst_ref=land, send_sem=send_sems.at[a], recv_sem=recv_sems.at[a],
            device_id=sibling, device_id_type=pl.DeviceIdType.MESH)
        pairs.append((send, send))
    return pairs


def _forward_copies(lands, send_sems, recv_sems):
    sibling, _ = _peer(SIBLING)

    def copy(a, s, slot):
        return pltpu.make_async_remote_copy(
            src_ref=lands[a].at[slot], dst_ref=lands[a].at[slot], send_sem=send_sems.at[s], recv_sem=recv_sems.at[s],
            device_id=sibling, device_id_type=pl.DeviceIdType.MESH)

    pairs = []
    for k, r in enumerate(CHIP_PEERS):
        _, mine = _peer(r)
        _, theirs = _peer(r | SIBLING)
        for a in range(len(lands)):
            s = a * len(CHIP_PEERS) + k
            pairs.append((copy(a, s, mine), copy(a, s, theirs)))
    return pairs


_HBM_SPEC = pl.BlockSpec(memory_space=pltpu.HBM)
_SEM_SPEC = pl.BlockSpec(memory_space=pltpu.SEMAPHORE)
_SIDE_EFFECT = pltpu.SideEffectType.DATAFLOW_SIDE_EFFECTING


def _split_start(name, operands, n_sem, make_pairs):
    k = len(operands)

    def body(*refs):
        send_sems, recv_sems, token = refs[k], refs[k + 1], refs[-1]
        for send, _ in make_pairs(refs[:k], send_sems, recv_sems):
            send.start()
        token[...] = jnp.zeros_like(token)

    out = pl.pallas_call(
        body, name=name,
        out_shape=(pltpu.SemaphoreType.DMA((n_sem,)), pltpu.SemaphoreType.DMA((n_sem,)),
                   *[pltpu.HBM(a.shape, a.dtype) for a in operands], _sds((SUBLANES, LANES), F32)),
        in_specs=[_HBM_SPEC] * k,
        out_specs=(_SEM_SPEC, _SEM_SPEC, *[_HBM_SPEC] * k, pl.BlockSpec(memory_space=pltpu.VMEM)),
        input_output_aliases={i: 2 + i for i in range(k)},
        compiler_params=pltpu.CompilerParams(has_side_effects=_SIDE_EFFECT),
    )(*[pltpu.with_memory_space_constraint(a, pltpu.HBM) for a in operands])
    return dict(name=name, sems=out[:2], thru=list(out[2:2 + k]), make_pairs=make_pairs), out[-1]


def _split_wait(handle, after):
    thru, make_pairs = handle["thru"], handle["make_pairs"]
    k = len(thru)

    def body(*refs):
        for send, arrival in make_pairs(refs[:k], refs[k], refs[k + 1]):
            send.wait_send()
            arrival.wait_recv()

    return pl.pallas_call(
        body, name=handle["name"] + "_wait", out_shape=[pltpu.HBM(a.shape, a.dtype) for a in thru],
        in_specs=[_HBM_SPEC] * k + [_SEM_SPEC, _SEM_SPEC] + [pl.BlockSpec(memory_space=pl.ANY)] * len(after),
        out_specs=[_HBM_SPEC] * k, input_output_aliases={i: i for i in range(k)},
        compiler_params=pltpu.CompilerParams(has_side_effects=_SIDE_EFFECT),
    )(*thru, *handle["sems"], *after)


def _exchange_start(name, arrays, scatter, relations, by_chip=False):
    n = len(arrays)
    lands = [lax.empty(a.shape if scatter else (N_DEV,) + a.shape, a.dtype) for a in arrays]

    def make_pairs(refs, send_sems, recv_sems):
        return _exchange_copies(refs[:n], refs[n:], send_sems, recv_sems, scatter, relations, by_chip)

    handle, token = _split_start(name, list(arrays) + lands, n * len(relations), make_pairs)
    handle.update(n=n)
    return handle, token


def _halves_start(name, arrays):
    lands = [lax.empty((a.shape[0], 1) + a.shape[2:], a.dtype) for a in arrays]
    n = len(arrays)

    def make_pairs(refs, send_sems, recv_sems):
        return _halves_copies(refs[:n], refs[n:], send_sems, recv_sems)

    return _split_start(name, list(arrays) + lands, n, make_pairs)


def _chip_sum(name, array, landed):
    chips, _, r, c = array.shape
    tr = r // 2 if r > 512 and r % 32 == 0 else r

    def body(a_ref, b_ref, o_ref):
        mine = a_ref[lax.axis_index("c")].astype(F32)
        o_ref[...] = (mine + b_ref[...].astype(F32)).astype(o_ref.dtype)

    return _call(body, name=name, grid=(chips, r // tr),
                 in_specs=[pl.BlockSpec((None, 2, tr, c), lambda k, i: (k, 0, i, 0)),
                           pl.BlockSpec((None, None, tr, c), lambda k, i: (k, 0, i, 0))],
                 out_specs=pl.BlockSpec((None, tr, c), lambda k, i: (k, i, 0)),
                 out_shape=_sds((chips, r, c), BF16), semantics=("parallel", "parallel"))(array, landed)


def _forward_start(name, lands):
    return _split_start(name, list(lands), len(lands) * len(CHIP_PEERS), _forward_copies)


def _exchange_wait(handle, after):
    return _split_wait(handle, after)[handle["n"]:]


def _rope_tables(pos_col):
    t = pos_col.shape[0]
    half = HEAD_DIM // 2
    inv_freq = ROPE_THETA ** (-jnp.arange(half, dtype=F32) / half)
    inv_row = jnp.tile(inv_freq, LANES // half)[None, :]

    def body(pos_ref, inv_ref, cos_ref, sin_ref):
        ang = pos_ref[...] * inv_ref[...]
        cos_ref[...] = jnp.cos(ang)
        sin_ref[...] = jnp.sin(ang)

    tm = min(t, 512)
    return _call(body, name="rope_tables", grid=(t // tm,),
                 in_specs=[pl.BlockSpec((tm, 1), lambda i: (i, 0)), pl.BlockSpec((1, LANES), lambda i: (0, 0))],
                 out_specs=[pl.BlockSpec((tm, LANES), lambda i: (i, 0))] * 2,
                 out_shape=[_sds((t, LANES), F32)] * 2, semantics=("parallel",))(pos_col, inv_row)


def _rot_half(x):
    lane = lax.broadcasted_iota(jnp.int32, x.shape, 1)
    low = (lane % HEAD_DIM) < HEAD_DIM // 2
    return jnp.where(low, -pltpu.roll(x, LANES - HEAD_DIM // 2, 1), pltpu.roll(x, HEAD_DIM // 2, 1))


def _rope(x, cos, sin):
    return x * cos + _rot_half(x) * sin


def _unrope(d, cos, sin):
    return d * cos - _rot_half(d) * sin


def _band_mask(first_block, heads):
    r = lax.broadcasted_iota(jnp.int32, (heads * BLOCK, 2 * BLOCK), 0) % BLOCK
    c = lax.broadcasted_iota(jnp.int32, (heads * BLOCK, 2 * BLOCK), 1)
    diff = r - c + BLOCK
    return (diff >= 0) & (diff < WINDOW) & ((c >= BLOCK) | jnp.logical_not(first_block))


def _attn_specs(t, d_attn, d_in):
    kb, vb = d_attn // D_KV, d_attn // D_KV + 1
    prev = lambda i: jnp.maximum(i - 1, 0)
    return [
        pl.BlockSpec((BLOCK, d_attn), lambda i: (i, 0)),
        pl.BlockSpec((BLOCK, D_KV), lambda i: (i, kb)),
        pl.BlockSpec((BLOCK, D_KV), lambda i: (i, vb)),
        pl.BlockSpec((BLOCK, D_KV), lambda i: (prev(i), kb)),
        pl.BlockSpec((BLOCK, D_KV), lambda i: (prev(i), vb)),
        pl.BlockSpec((BLOCK, LANES), lambda i: (i, 0)),
        pl.BlockSpec((BLOCK, LANES), lambda i: (i, 0)),
        pl.BlockSpec((BLOCK, LANES), lambda i: (prev(i), 0)),
        pl.BlockSpec((BLOCK, LANES), lambda i: (prev(i), 0)),
        pl.BlockSpec((1, LANES), lambda i: (0, 0)),
    ]


def _head(x, h):
    return x[:, h * HEAD_DIM:(h + 1) * HEAD_DIM]


def _attn_heads(q_ref, kc_ref, vc_ref, kp_ref, vp_ref, cq_ref, sq_ref, cp_ref, sp_ref, d_attn):
    cq, sq, cp, sp = cq_ref[...], sq_ref[...], cp_ref[...], sp_ref[...]
    q_rot = [_rope(q_ref[:, j * LANES:(j + 1) * LANES], cq, sq) for j in range(d_attn // LANES)]
    kc_rot = [_rope(kc_ref[:, j * LANES:(j + 1) * LANES], cq, sq) for j in range(D_KV // LANES)]
    kp_rot = [_rope(kp_ref[:, j * LANES:(j + 1) * LANES], cp, sp) for j in range(D_KV // LANES)]
    per = LANES // HEAD_DIM
    q_heads = [_head(q_rot[h // per], h % per).astype(BF16) for h in range(d_attn // HEAD_DIM)]
    kk = [jnp.concatenate([_head(kp_rot[g // per], g % per), _head(kc_rot[g // per], g % per)], axis=0).astype(BF16)
          for g in range(N_KV_HEADS)]
    vv = [jnp.concatenate([_head(vp_ref[...], g), _head(vc_ref[...], g)], axis=0).astype(BF16) for g in range(N_KV_HEADS)]
    return q_heads, kk, vv


def _stack_group(q_heads, sink_ref, group):
    q_all = jnp.concatenate([q_heads[h] for h in group], axis=0)
    sink_all = jnp.concatenate([jnp.broadcast_to(sink_ref[:, h:h + 1], (BLOCK, 1)) for h in group], axis=0)
    return q_all, sink_all


def _softmax_with_sink(q, kk, sink, mask):
    s = _dot(q, kk, 1, 1) * (1.0 / math.sqrt(HEAD_DIM))
    s = jnp.where(mask, s, MASKED)
    m = jnp.maximum(jnp.max(s, axis=-1, keepdims=True), sink)
    p = jnp.exp(s - m)
    e_sink = jnp.exp(sink - m)
    inv = 1.0 / (jnp.sum(p, axis=-1, keepdims=True) + e_sink)
    return p * inv, e_sink * inv


def _attention_fwd(proj, cos, sin, sinks_row, d_attn):
    t, d_in = proj.shape
    n_heads = d_attn // HEAD_DIM
    q_per_kv = n_heads // N_KV_HEADS

    def body(q_ref, kc_ref, vc_ref, kp_ref, vp_ref, cq_ref, sq_ref, cp_ref, sp_ref, sink_ref, o_ref):
        mask = _band_mask(pl.program_id(0) == 0, q_per_kv)
        q_heads, kk, vv = _attn_heads(q_ref, kc_ref, vc_ref, kp_ref, vp_ref, cq_ref, sq_ref, cp_ref, sp_ref, d_attn)
        for g in range(N_KV_HEADS):
            group = range(g * q_per_kv, (g + 1) * q_per_kv)
            q_all, sink_all = _stack_group(q_heads, sink_ref, group)
            probs, _ = _softmax_with_sink(q_all, kk[g], sink_all, mask)
            o_all = _dot(probs.astype(BF16), vv[g], 1, 0)
            for k, h in enumerate(group):
                o_ref[:, h * HEAD_DIM:(h + 1) * HEAD_DIM] = o_all[k * BLOCK:(k + 1) * BLOCK]

    return _call(body, name="attention_fwd", grid=(t // BLOCK,), in_specs=_attn_specs(t, d_attn, d_in),
                 out_specs=pl.BlockSpec((BLOCK, d_attn), lambda i: (i, 0)), out_shape=_sds((t, d_attn), F32),
                 semantics=("parallel",))(proj, proj, proj, proj, proj, cos, sin, cos, sin, sinks_row)


def _attention_bwd(proj, cos, sin, sinks_row, d_out, d_attn):
    t, d_in = proj.shape
    n_heads = d_attn // HEAD_DIM
    q_per_kv = n_heads // N_KV_HEADS
    nb = t // BLOCK
    per = LANES // HEAD_DIM

    def body(q_ref, kc_ref, vc_ref, kp_ref, vp_ref, cq_ref, sq_ref, cp_ref, sp_ref, sink_ref, do_ref,
             dq_ref, dk_ref, dv_ref, dsink_ref):
        i = pl.program_id(0)
        mask = _band_mask(i == 0, q_per_kv)
        q_heads, kk, vv = _attn_heads(q_ref, kc_ref, vc_ref, kp_ref, vp_ref, cq_ref, sq_ref, cp_ref, sp_ref, d_attn)
        lane = lax.broadcasted_iota(jnp.int32, (1, LANES), 1)
        dsink = jnp.zeros((1, LANES), F32)
        dq_rot, dkk, dvv = [], [], []
        for g in range(N_KV_HEADS):
            group = range(g * q_per_kv, (g + 1) * q_per_kv)
            q_all, sink_all = _stack_group(q_heads, sink_ref, group)
            probs, p_sink = _softmax_with_sink(q_all, kk[g], sink_all, mask)
            do_all = jnp.concatenate([do_ref[:, h * HEAD_DIM:(h + 1) * HEAD_DIM] for h in group], axis=0).astype(BF16)
            dp = _dot(do_all, vv[g], 1, 1)
            delta = jnp.sum(probs * dp, axis=-1, keepdims=True)
            ds = (probs * (dp - delta) * (1.0 / math.sqrt(HEAD_DIM))).astype(BF16)
            dq_all = _dot(ds, kk[g], 1, 0)
            dkk.append(_dot(ds, q_all, 0, 0))
            dvv.append(_dot(probs.astype(BF16), do_all, 0, 0))
            sink_term = p_sink * delta
            for k, h in enumerate(group):
                dq_rot.append(dq_all[k * BLOCK:(k + 1) * BLOCK])
                part = jnp.sum(sink_term[k * BLOCK:(k + 1) * BLOCK], axis=0, keepdims=True)
                dsink += jnp.where(lane == h, -part, 0.0)
        cq, sq, cp, sp = cq_ref[...], sq_ref[...], cp_ref[...], sp_ref[...]
        for j in range(d_attn // LANES):
            d = jnp.concatenate(dq_rot[j * per:(j + 1) * per], axis=1)
            dq_ref[:, j * LANES:(j + 1) * LANES] = _unrope(d, cq, sq)
        for j in range(D_KV // LANES):
            d = jnp.concatenate(dkk[j * per:(j + 1) * per], axis=1)
            dk_ref[0, :, j * LANES:(j + 1) * LANES] = _unrope(d[:BLOCK], cp, sp)
            dk_ref[1, :, j * LANES:(j + 1) * LANES] = _unrope(d[BLOCK:], cq, sq)
            d = jnp.concatenate(dvv[j * per:(j + 1) * per], axis=1)
            dv_ref[0, :, j * LANES:(j + 1) * LANES] = d[:BLOCK]
            dv_ref[1, :, j * LANES:(j + 1) * LANES] = d[BLOCK:]

        @pl.when(i == 0)
        def _():
            dsink_ref[...] = jnp.zeros_like(dsink_ref)

        dsink_ref[...] += dsink

    pair = pl.BlockSpec((2, BLOCK, D_KV), lambda i: (i, 0, 0))
    return _call(body, name="attention_bwd", grid=(nb,),
                 in_specs=_attn_specs(t, d_attn, d_in) + [pl.BlockSpec((BLOCK, d_attn), lambda i: (i, 0))],
                 out_specs=[pl.BlockSpec((BLOCK, d_attn), lambda i: (i, 0)), pair, pair,
                            pl.BlockSpec((1, LANES), lambda i: (0, 0))],
                 out_shape=[_sds((t, d_attn), F32), _sds((2 * nb, BLOCK, D_KV), F32), _sds((2 * nb, BLOCK, D_KV), F32),
                            _sds((1, LANES), F32)],
                 semantics=("arbitrary",))(proj, proj, proj, proj, proj, cos, sin, cos, sin, sinks_row, d_out)


def _assemble_dproj(dq, dk2, dv2, du, d_in, after):
    t, d_attn = dq.shape
    d_ssm = du.shape[1]
    nb = t // BLOCK

    def body(dq_ref, dk_own, dk_next, dv_own, dv_next, du_ref, o_ref):
        has_next = (pl.program_id(0) < nb - 1).astype(F32)
        o_ref[:, :d_attn] = dq_ref[...].astype(BF16)
        o_ref[:, d_attn:d_attn + D_KV] = (dk_own[...] + has_next * dk_next[...]).astype(BF16)
        o_ref[:, d_attn + D_KV:d_attn + 2 * D_KV] = (dv_own[...] + has_next * dv_next[...]).astype(BF16)
        o_ref[:, d_attn + 2 * D_KV:] = du_ref[...].astype(BF16)

    own = pl.BlockSpec((None, BLOCK, D_KV), lambda i: (2 * i + 1, 0, 0))
    nxt = pl.BlockSpec((None, BLOCK, D_KV), lambda i: (jnp.minimum(2 * i + 2, 2 * nb - 1), 0, 0))
    return _call(body, name="assemble_dproj", grid=(nb,),
                 in_specs=[pl.BlockSpec((BLOCK, d_attn), lambda i: (i, 0)), own, nxt, own, nxt,
                           pl.BlockSpec((BLOCK, d_ssm), lambda i: (i, 0))],
                 out_specs=pl.BlockSpec((BLOCK, d_in), lambda i: (i, 0)), out_shape=_sds((t, d_in), BF16),
                 semantics=("parallel",), n_after=len(after))(dq, dk2, dk2, dv2, dv2, du, *after)


def _discretise(ar, ai, ldt, br, bi):
    dt = jnp.exp(ldt)
    mag = jnp.exp(ar * dt)
    lam_re = mag * jnp.cos(ai * dt)
    lam_im = mag * jnp.sin(ai * dt)
    den = ar * ar + ai * ai
    nr = lam_re - 1.0
    ni = lam_im
    f_re = (nr * ar + ni * ai) / den
    f_im = (ni * ar - nr * ai) / den
    return (lam_re, lam_im, [f_re * r - f_im * i for r, i in zip(br, bi)], [f_re * i + f_im * r for r, i in zip(br, bi)])


def _whole(arrays):
    return [pl.BlockSpec(a.shape, lambda *_, nd=len(a.shape): (0,) * nd) for a in arrays]


def _channels(ref):
    groups = ref.shape[0] // SSM_GROUP
    return [ref[pl.ds(p, groups, stride=SSM_GROUP), :] for p in range(SSM_GROUP)]


def _store_channels(ref, values):
    groups = ref.shape[0] // SSM_GROUP
    for p, val in enumerate(values):
        ref[pl.ds(p, groups, stride=SSM_GROUP), :] = val


def _s5_discretise(ar, ai, ldt, br, bi):
    ins = [ar, ai, ldt, br, bi]

    def body(ar_ref, ai_ref, ldt_ref, br_ref, bi_ref, lr_ref, li_ref, bbr_ref, bbi_ref):
        lr, li, bbr, bbi = _discretise(ar_ref[...], ai_ref[...], ldt_ref[...], _channels(br_ref), _channels(bi_ref))
        lr_ref[...] = lr
        li_ref[...] = li
        _store_channels(bbr_ref, bbr)
        _store_channels(bbi_ref, bbi)

    outs = [_sds(ar.shape, F32), _sds(ar.shape, F32), _sds(br.shape, F32), _sds(br.shape, F32)]
    return _call(body, name="s5_discretise", in_specs=_whole(ins), out_specs=_whole(outs), out_shape=outs)(*ins)


def _s5_discretise_bwd(ar, ai, ldt, br, bi, d_lr, d_li, d_bbr, d_bbi):
    ins = [ar, ai, ldt, br, bi, d_lr, d_li, d_bbr, d_bbi]

    def body(ar_ref, ai_ref, ldt_ref, br_ref, bi_ref, dlr_ref, dli_ref, dbbr_ref, dbbi_ref,
             dar_ref, dai_ref, dldt_ref, dbr_ref, dbi_ref):
        _, vjp = jax.vjp(_discretise, ar_ref[...], ai_ref[...], ldt_ref[...], _channels(br_ref), _channels(bi_ref))
        dar, dai, dldt, dbr, dbi = vjp((dlr_ref[...], dli_ref[...], _channels(dbbr_ref), _channels(dbbi_ref)))
        dar_ref[...] = dar
        dai_ref[...] = dai
        dldt_ref[...] = dldt
        _store_channels(dbr_ref, dbr)
        _store_channels(dbi_ref, dbi)

    outs = [_sds(a.shape, F32) for a in (ar, ai, ldt, br, bi)]
    return _call(body, name="s5_discretise_bwd", in_specs=_whole(ins), out_specs=_whole(outs), out_shape=outs)(*ins)


def _cmul(ar, ai, br, bi):
    return ar * br - ai * bi, ar * bi + ai * br


def _load_segmented(ref, tile0, n_tiles, seg):
    return jnp.concatenate([ref[pl.ds(tile0 + j, SUBLANES, stride=seg), :] for j in range(n_tiles)], axis=0)


def _store_segmented(ref, tile0, seg, value):
    for j in range(value.shape[0] // SUBLANES):
        ref[pl.ds(tile0 + j, SUBLANES, stride=seg), :] = value[j * SUBLANES:(j + 1) * SUBLANES, :]


def _fill_powers(lr, li, pr_ref, pi_ref, seg):
    pows = [(lr, li)]
    for _ in range(SUBLANES - 1):
        pows.append(_cmul(pows[-1][0], pows[-1][1], lr, li))
    row = lax.broadcasted_iota(jnp.int32, (SUBLANES, lr.shape[1]), 0)
    tr = jnp.zeros((SUBLANES, lr.shape[1]), F32)
    ti = jnp.zeros((SUBLANES, lr.shape[1]), F32)
    for r in range(SUBLANES):
        tr = jnp.where(row == r, pows[r][0], tr)
        ti = jnp.where(row == r, pows[r][1], ti)
    pr_ref[0:SUBLANES, :] = tr
    pi_ref[0:SUBLANES, :] = ti
    k = SUBLANES
    while k < seg:
        fr, fi = pr_ref[k - 1:k, :], pi_ref[k - 1:k, :]
        for t0 in range(0, k, SUBLANES):
            nr, ni = _cmul(pr_ref[t0:t0 + SUBLANES, :], pi_ref[t0:t0 + SUBLANES, :], fr, fi)
            pr_ref[k + t0:k + t0 + SUBLANES, :] = nr
            pi_ref[k + t0:k + t0 + SUBLANES, :] = ni
        k *= 2


def _scan_segments(sr_ref, si_ref, pr_ref, pi_ref, lr, li, seg, reverse, per_tile=None):
    w = lr.shape[1]
    sign = -1.0 if reverse else 1.0
    lrb = jnp.broadcast_to(lr, (SUBLANES, w))
    lib = jnp.broadcast_to(sign * li, (SUBLANES, w))
    zero = jnp.zeros((SUBLANES, w), F32)

    def tile_rows(j):
        return pl.ds(pl.multiple_of(j * SUBLANES, SUBLANES), SUBLANES)

    def local(i, carry):
        rows = tile_rows(seg - 1 - i if reverse else i)
        pr, pi = _cmul(lrb, lib, carry[0], carry[1])
        xr, xi = sr_ref[rows, :] + pr, si_ref[rows, :] + pi
        sr_ref[rows, :] = xr
        si_ref[rows, :] = xi
        return xr, xi

    end_r, end_i = lax.fori_loop(0, seg, local, (zero, zero))
    full_r, full_i = pr_ref[seg - 1:seg, :], sign * pi_ref[seg - 1:seg, :]
    row = lax.broadcasted_iota(jnp.int32, (SUBLANES, w), 0)
    in_r, in_i = zero, zero
    cur_r, cur_i = jnp.zeros((1, w), F32), jnp.zeros((1, w), F32)
    for r in (range(SUBLANES - 2, -1, -1) if reverse else range(1, SUBLANES)):
        src = r + 1 if reverse else r - 1
        pr, pi = _cmul(full_r, full_i, cur_r, cur_i)
        cur_r, cur_i = end_r[src:src + 1, :] + pr, end_i[src:src + 1, :] + pi
        in_r = jnp.where(row == r, cur_r, in_r)
        in_i = jnp.where(row == r, cur_i, in_i)

    def carry_in(j, _):
        rows = tile_rows(j)
        k = seg - 1 - j if reverse else j
        pr, pi = _cmul(pr_ref[pl.ds(k, 1), :], sign * pi_ref[pl.ds(k, 1), :], in_r, in_i)
        xr, xi = sr_ref[rows, :] + pr, si_ref[rows, :] + pi
        sr_ref[rows, :] = xr
        si_ref[rows, :] = xi
        if per_tile is not None:
            per_tile(j, xr, xi)
        return 0

    lax.fori_loop(0, seg, carry_in, 0)


_S5_ROWS = 256


def _s5_in_specs(t, d_attn):
    u_block = (d_attn + 2 * D_KV) // SSM_CH_BLOCK
    blk3 = lambda shape: pl.BlockSpec((None,) + shape, lambda j: (j, 0, 0))
    return [
        pl.BlockSpec((t, SSM_CH_BLOCK), lambda j: (0, u_block + j)),
        blk3((SSM_CH_BLOCK, SSM_ST_BLOCK)), blk3((SSM_CH_BLOCK, SSM_ST_BLOCK)),
        blk3((1, SSM_ST_BLOCK)), blk3((1, SSM_ST_BLOCK)),
        blk3((SSM_ST_BLOCK, SSM_CH_BLOCK)), blk3((SSM_ST_BLOCK, SSM_CH_BLOCK)),
        pl.BlockSpec((1, SSM_CH_BLOCK), lambda j: (0, j)),
    ]


def _chunks(t):
    rows = min(_S5_ROWS, t)
    return rows, lambda i: pl.ds(pl.multiple_of(i * rows, rows), rows)


def _s5_states(u_ref, us_ref, bre_ref, bim_ref, lr_ref, li_ref, sr_ref, si_ref, pr_ref, pi_ref, t):
    seg = t // SUBLANES
    rows, chunk = _chunks(t)
    for c in range(t // rows):
        us_ref[c * rows:(c + 1) * rows, :] = _load_segmented(u_ref, c * rows // SUBLANES, rows // SUBLANES, seg)

    def fill(i, _):
        ub = us_ref[chunk(i), :].astype(BF16)
        sr_ref[chunk(i), :] = _dot(ub, bre_ref[...], 1, 0)
        si_ref[chunk(i), :] = _dot(ub, bim_ref[...], 1, 0)
        return 0

    lax.fori_loop(0, t // rows, fill, 0)
    _fill_powers(lr_ref[...], li_ref[...], pr_ref, pi_ref, seg)
    _scan_segments(sr_ref, si_ref, pr_ref, pi_ref, lr_ref[...], li_ref[...], seg, False)


def _s5_scratch(t):
    state = pltpu.VMEM((t, SSM_ST_BLOCK), F32)
    powers = pltpu.VMEM((t // SUBLANES, SSM_ST_BLOCK), F32)
    return state, powers, pltpu.VMEM((t, SSM_CH_BLOCK), F32)


def _s5_fwd(proj, mats, dskip_row, d_attn, d_ssm):
    t = proj.shape[0]
    seg = t // SUBLANES
    n_blocks = d_ssm // SSM_CH_BLOCK
    rows, chunk = _chunks(t)

    def body(u_ref, bre_ref, bim_ref, lr_ref, li_ref, cre_ref, cim_ref, d_ref, y_ref,
             sr_ref, si_ref, pr_ref, pi_ref, us_ref, ys_ref):
        _s5_states(u_ref, us_ref, bre_ref, bim_ref, lr_ref, li_ref, sr_ref, si_ref, pr_ref, pi_ref, t)

        def emit(i, _):
            ys_ref[chunk(i), :] = (_dot(sr_ref[chunk(i), :].astype(BF16), cre_ref[...], 1, 0)
                                   - _dot(si_ref[chunk(i), :].astype(BF16), cim_ref[...], 1, 0)
                                   + d_ref[...] * us_ref[chunk(i), :])
            return 0

        lax.fori_loop(0, t // rows, emit, 0)
        for c in range(t // rows):
            _store_segmented(y_ref, c * rows // SUBLANES, seg, ys_ref[c * rows:(c + 1) * rows, :])

    state, powers, channels = _s5_scratch(t)
    col = pl.BlockSpec((t, SSM_CH_BLOCK), lambda j: (0, j))
    return _call(body, name="s5_fwd", grid=(n_blocks,), in_specs=_s5_in_specs(t, d_attn), out_specs=col,
                 out_shape=_sds((t, d_ssm), F32), scratch_shapes=[state, state, powers, powers, channels, channels],
                 semantics=("parallel",))(proj, *mats, dskip_row)


def _s5_bwd(proj, mats, dskip_row, y, dz_a, dz_b, d_attn, d_ssm, after):
    t = proj.shape[0]
    seg = t // SUBLANES
    n_blocks = d_ssm // SSM_CH_BLOCK
    rows, chunk = _chunks(t)

    def body(u_ref, bre_ref, bim_ref, lr_ref, li_ref, cre_ref, cim_ref, d_ref, y_ref, dza_ref, dzb_ref,
             du_ref, dbre_ref, dbim_ref, dlr_ref, dli_ref, dcre_ref, dcim_ref, dd_ref,
             sr_ref, si_ref, gr_ref, gi_ref, pr_ref, pi_ref, us_ref, dys_ref, dus_ref, acc_r, acc_i):
        _s5_states(u_ref, us_ref, bre_ref, bim_ref, lr_ref, li_ref, sr_ref, si_ref, pr_ref, pi_ref, t)
        for ref in (dcre_ref, dcim_ref, dbre_ref, dbim_ref, dd_ref, acc_r, acc_i):
            ref[...] = jnp.zeros_like(ref)
        for c in range(t // rows):
            tile0, n_tiles = c * rows // SUBLANES, rows // SUBLANES
            dz = _load_segmented(dza_ref, tile0, n_tiles, seg) + _load_segmented(dzb_ref, tile0, n_tiles, seg)
            dys_ref[c * rows:(c + 1) * rows, :] = dz * _gelu_grad(_load_segmented(y_ref, tile0, n_tiles, seg))

        def through_c(i, _):
            dy = dys_ref[chunk(i), :]
            dd_ref[...] += jnp.sum(dy * us_ref[chunk(i), :], axis=0, keepdims=True)
            dyb = dy.astype(BF16)
            gr_ref[chunk(i), :] = _dot(dyb, cre_ref[...], 1, 1)
            gi_ref[chunk(i), :] = -_dot(dyb, cim_ref[...], 1, 1)
            dcre_ref[...] += _dot(sr_ref[chunk(i), :].astype(BF16), dyb, 0, 0)
            dcim_ref[...] -= _dot(si_ref[chunk(i), :].astype(BF16), dyb, 0, 0)
            return 0

        lax.fori_loop(0, t // rows, through_c, 0)

        row = lax.broadcasted_iota(jnp.int32, (SUBLANES, SSM_ST_BLOCK), 0)
        last = pl.ds((seg - 1) * SUBLANES, SUBLANES)
        wrap = [jnp.where(row == 0, 0.0, pltpu.roll(ref[last, :], 1, 0)) for ref in (sr_ref, si_ref)]

        def lambda_grad(j, g_re, g_im):
            before = pl.ds(pl.multiple_of(jnp.maximum(j - 1, 0) * SUBLANES, SUBLANES), SUBLANES)
            prev_r = jnp.where(j > 0, sr_ref[before, :], wrap[0])
            prev_i = jnp.where(j > 0, si_ref[before, :], wrap[1])
            acc_r[...] += g_re * prev_r + g_im * prev_i
            acc_i[...] += g_im * prev_r - g_re * prev_i

        _scan_segments(gr_ref, gi_ref, pr_ref, pi_ref, lr_ref[...], li_ref[...], seg, True, per_tile=lambda_grad)
        dlr_ref[...] = jnp.sum(acc_r[...], axis=0, keepdims=True)
        dli_ref[...] = jnp.sum(acc_i[...], axis=0, keepdims=True)

        def through_b(i, _):
            ub = us_ref[chunk(i), :].astype(BF16)
            grb, gib = gr_ref[chunk(i), :].astype(BF16), gi_ref[chunk(i), :].astype(BF16)
            dbre_ref[...] += _dot(ub, grb, 0, 0)
            dbim_ref[...] += _dot(ub, gib, 0, 0)
            dus_ref[chunk(i), :] = (_dot(grb, bre_ref[...], 1, 1) + _dot(gib, bim_ref[...], 1, 1)
                                    + d_ref[...] * dys_ref[chunk(i), :])
            return 0

        lax.fori_loop(0, t // rows, through_b, 0)
        for c in range(t // rows):
            _store_segmented(du_ref, c * rows // SUBLANES, seg, dus_ref[c * rows:(c + 1) * rows, :])

    col = pl.BlockSpec((t, SSM_CH_BLOCK), lambda j: (0, j))
    blk3 = lambda shape: pl.BlockSpec((None,) + shape, lambda j: (j, 0, 0))
    state, powers, channels = _s5_scratch(t)
    return _call(
        body, name="s5_bwd", grid=(n_blocks,), in_specs=_s5_in_specs(t, d_attn) + [col, col, col],
        out_specs=[col, blk3((SSM_CH_BLOCK, SSM_ST_BLOCK)), blk3((SSM_CH_BLOCK, SSM_ST_BLOCK)),
                   blk3((1, SSM_ST_BLOCK)), blk3((1, SSM_ST_BLOCK)),
                   blk3((SSM_ST_BLOCK, SSM_CH_BLOCK)), blk3((SSM_ST_BLOCK, SSM_CH_BLOCK)),
                   pl.BlockSpec((1, SSM_CH_BLOCK), lambda j: (0, j))],
        out_shape=[_sds((t, d_ssm), F32),
                   _sds((n_blocks, SSM_CH_BLOCK, SSM_ST_BLOCK), F32), _sds((n_blocks, SSM_CH_BLOCK, SSM_ST_BLOCK), F32),
                   _sds((n_blocks, 1, SSM_ST_BLOCK), F32), _sds((n_blocks, 1, SSM_ST_BLOCK), F32),
                   _sds((n_blocks, SSM_ST_BLOCK, SSM_CH_BLOCK), F32), _sds((n_blocks, SSM_ST_BLOCK, SSM_CH_BLOCK), F32),
                   _sds((1, d_ssm), F32)],
        scratch_shapes=[state, state, state, state, powers, powers, channels, channels, channels,
                        pltpu.VMEM((SUBLANES, SSM_ST_BLOCK), F32), pltpu.VMEM((SUBLANES, SSM_ST_BLOCK), F32)],
        semantics=("parallel",), n_after=len(after))(proj, *mats, dskip_row, y, dz_a, dz_b, *after)


def _by_block(gp_n):
    return gp_n.reshape(-1, GROUPS_PER_BLOCK, SSM_GROUP, SSM_STATE)


def _block_diag_in(bbar):
    eye = jnp.eye(GROUPS_PER_BLOCK, dtype=F32)
    return jnp.einsum("jgpn,gh->jgphn", _by_block(bbar), eye).reshape(-1, SSM_CH_BLOCK, SSM_ST_BLOCK)


def _block_diag_in_t(dense):
    d5 = dense.reshape(-1, GROUPS_PER_BLOCK, SSM_GROUP, GROUPS_PER_BLOCK, SSM_STATE)
    eye = jnp.eye(GROUPS_PER_BLOCK, dtype=F32)
    return jnp.einsum("jgphn,gh->jgpn", d5, eye).reshape(-1, SSM_STATE)


def _block_diag_out(c):
    eye = jnp.eye(GROUPS_PER_BLOCK, dtype=F32)
    return jnp.einsum("jgpn,gh->jgnhp", _by_block(c), eye).reshape(-1, SSM_ST_BLOCK, SSM_CH_BLOCK)


def _block_diag_out_t(dense):
    d5 = dense.reshape(-1, GROUPS_PER_BLOCK, SSM_STATE, GROUPS_PER_BLOCK, SSM_GROUP)
    eye = jnp.eye(GROUPS_PER_BLOCK, dtype=F32)
    return jnp.einsum("jgnhp,gh->jgpn", d5, eye).reshape(-1, SSM_STATE)


def _adamw(w, g, m, v):
    m = ADAM_B1 * m + (1.0 - ADAM_B1) * g
    v = ADAM_B2 * v + (1.0 - ADAM_B2) * (g * g)
    m_hat = m / (1.0 - ADAM_B1 ** ADAM_STEP)
    v_hat = v / (1.0 - ADAM_B2 ** ADAM_STEP)
    delta = -ADAM_LR * (m_hat / (jnp.sqrt(v_hat) + ADAM_EPS) + ADAM_WD * w)
    return delta, m, v


def _adam_sharded(name, parts, w, m, v, tr, row0=0):
    r, c = w.shape
    assert r % tr == 0 and row0 % tr == 0, (name, r, tr, row0)

    def body(p_ref, w_ref, m_ref, v_ref, g_out, d_out, m_out, v_out):
        g = p_ref[0].astype(F32)
        for i in range(1, p_ref.shape[0]):
            g = g + p_ref[i].astype(F32)
        delta, m_new, v_new = _adamw(w_ref[...], g, m_ref[...], v_ref[...])
        g_out[...] = g
        d_out[...] = delta
        m_out[...] = m_new
        v_out[...] = v_new

    tile = pl.BlockSpec((tr, c), lambda i: (i, 0))
    return _call(body, name=name, grid=(r // tr,),
                 in_specs=[pl.BlockSpec((parts.shape[0], tr, c), lambda i: (0, i + row0 // tr, 0)), tile, tile, tile],
                 out_specs=[tile] * 4, out_shape=[_sds((r, c), F32)] * 4, semantics=("parallel",))(parts, w, m, v)


_BIG = ("w_in", "w_glu", "w_o", "w_gate", "w_up", "w_down")
_BY_COLUMNS = ("w_in", "w_gate", "w_up")
_SMALL_VECTORS = ("sinks", "log_dt", "b_glu", "g_attn_out", "g_ssm_out", "g_post_mix", "g_pre_ffn", "g_post_ffn")
_SMALL_MATRICES = ("b_re", "b_im", "c_re", "c_im", "a_re", "a_im")
_ORDER = ("g_pre_mix", "w_in", "sinks", "a_re", "a_im", "log_dt", "b_re", "b_im", "c_re", "c_im", "d_skip", "w_glu",
          "b_glu", "g_attn_out", "g_ssm_out", "w_o", "g_post_mix", "g_pre_ffn", "w_gate", "w_up", "w_down",
          "g_post_ffn")


def _pack_grads(vectors, matrices):
    width = max(a.shape[1] for a in vectors)
    slots, row, lane = [], 0, 0
    for a in vectors:
        span = -(-a.shape[1] // LANES) * LANES
        if lane + span > width:
            row, lane = row + 1, 0
        slots.append((row, lane, a.shape[1]))
        lane += span
    firsts, at = [], 0
    for a in matrices:
        firsts.append(at)
        at += a.shape[0]
    nv = len(vectors)

    def body(*refs):
        vec_out, mat_out = refs[-2], refs[-1]
        vec_out[...] = jnp.zeros_like(vec_out)
        for ref, (r, l, w) in zip(refs[:nv], slots):
            vec_out[r:r + 1, l:l + w] = ref[...]
        for ref, r0 in zip(refs[nv:-2], firsts):
            mat_out[r0:r0 + ref.shape[0], :] = ref[...]

    ins = list(vectors) + list(matrices)
    outs = [_sds((-(-(row + 1) // SUBLANES) * SUBLANES, width), F32), _sds((at, matrices[0].shape[1]), F32)]
    vec_pack, mat_pack = _call(body, name="pack_small_grads", in_specs=_whole(ins), out_specs=_whole(outs),
                               out_shape=outs)(*ins)
    return vec_pack, slots, mat_pack, firsts


def _adam_replicated(sources, found_at, w, m, v):
    ns, n = len(sources), len(w)

    def body(*refs):
        ins, outs = refs[ns:ns + 3 * n], refs[ns + 3 * n:]
        summed = []
        for p_ref in refs[:ns]:
            g = p_ref[0]
            for k in range(1, N_DEV):
                g = g + p_ref[k]
            summed.append(g)
        for i, (src, row, lane) in enumerate(found_at):
            w_ref, m_ref, v_ref = ins[i], ins[n + i], ins[2 * n + i]
            rows, cols = w_ref.shape
            g = summed[src][row:row + rows, lane:lane + cols]
            delta, m_new, v_new = _adamw(w_ref[...], g, m_ref[...], v_ref[...])
            for o, val in zip(outs[4 * i:4 * i + 4], (g, delta, m_new, v_new)):
                o[...] = val

    ins = list(sources) + list(w) + list(m) + list(v)
    outs = [_sds(a.shape, F32) for a in w for _ in range(4)]
    flat = _call(body, name="adam_replicated", in_specs=_whole(ins), out_specs=_whole(outs), out_shape=outs)(*ins)
    return [tuple(flat[4 * i:4 * i + 4]) for i in range(n)]


def kernel(x, positions, g_pre_mix, w_in, sinks, a_re, a_im, log_dt, b_re, b_im, c_re, c_im, d_skip, w_glu, b_glu, g_attn_out, g_ssm_out, w_o, g_post_mix, g_pre_ffn, w_gate, w_up, w_down, g_post_ffn, loss_target, m_g_pre_mix, m_w_in, m_sinks, m_a_re, m_a_im, m_log_dt, m_b_re, m_b_im, m_c_re, m_c_im, m_d_skip, m_w_glu, m_b_glu, m_g_attn_out, m_g_ssm_out, m_w_o, m_g_post_mix, m_g_pre_ffn, m_w_gate, m_w_up, m_w_down, m_g_post_ffn, v_g_pre_mix, v_w_in, v_sinks, v_a_re, v_a_im, v_log_dt, v_b_re, v_b_im, v_c_re, v_c_im, v_d_skip, v_w_glu, v_b_glu, v_g_attn_out, v_g_ssm_out, v_w_o, v_g_post_mix, v_g_pre_ffn, v_w_gate, v_w_up, v_w_down, v_g_post_ffn):
    given = dict(locals())
    weights = {n: given[n] for n in _ORDER}
    mom_m = {n: given["m_" + n] for n in _ORDER}
    mom_v = {n: given["v_" + n] for n in _ORDER}

    t, d = x.shape[1], x.shape[2]
    d_attn = d // 2
    d_ssm = d - d_attn
    d_in = d_attn + 2 * D_KV + d_ssm
    n_groups = d_ssm // SSM_GROUP
    n_heads = d_attn // HEAD_DIM
    tm = min(256, t)

    x2 = x[0]
    target = loss_target[0]

    def by_rows(n, a):
        return a[0].T if n in _BY_COLUMNS else a[0]

    def start_gather(name, ns, token):
        behind = 0 if token is None else token[0, 0].astype(BF16)
        shards = [by_rows(n, weights[n]).astype(BF16) + behind for n in ns]
        return _exchange_start(name, shards, False, (OWN, SIBLING) + CHIP_PEERS)

    def forward_gather(handle, after):
        return _forward_start(handle["name"] + "_forward", _exchange_wait(handle, after))

    def finish_gather(handle, after):
        return _split_wait(forward_gather(handle, after)[0], [])

    ag_in, token = start_gather("gather_w_in", ["w_in"], None)
    ag_mix, token = start_gather("gather_w_glu_o", ["w_glu", "w_o"], token)
    ag_ffn_in, token = start_gather("gather_w_gate_up", ["w_gate", "w_up"], token)
    ag_down, token = start_gather("gather_w_down", ["w_down"], token)

    xn, = _rows("norm_in", lambda xv, g: ([_rms(xv)[0] * g], []), [x2], [g_pre_mix], [(d, BF16)], [], tm,
                after=[token])
    win_g, = finish_gather(ag_in, [xn])
    w_in_t = win_g.reshape(d_in, d)
    proj = _mm_nt("proj_in", xn, w_in_t, F32, tn=d_in // 4 if (d_in // 4) % LANES == 0 else None)

    cos, sin = _rope_tables(positions.reshape(t, 1).astype(F32))
    sinks_row = jnp.pad(sinks, ((0, 0), (0, LANES - n_heads)))
    attn = _attention_fwd(proj, cos, sin, sinks_row, d_attn)

    def view(n, a):
        if n in ("b_re", "b_im"):
            return jnp.transpose(a[0], (0, 2, 1)).reshape(-1, SSM_STATE)
        if n in ("c_re", "c_im"):
            return a[0].reshape(-1, SSM_STATE)
        return a[0].T if n == "d_skip" else a[0] if a.ndim == 3 else a

    def unview(n, val):
        if n in ("b_re", "b_im"):
            return jnp.transpose(val.reshape(n_groups, SSM_GROUP, SSM_STATE), (0, 2, 1))[None]
        if n in ("c_re", "c_im"):
            return val.reshape(1, n_groups, SSM_GROUP, SSM_STATE)
        return val.T[None] if n == "d_skip" else val[None] if weights[n].ndim == 3 else val

    b_re_v, b_im_v = view("b_re", b_re), view("b_im", b_im)
    ldt_col = log_dt.reshape(n_groups, 1)
    lam_re, lam_im, bbar_re, bbar_im = _s5_discretise(a_re[0], a_im[0], ldt_col, b_re_v, b_im_v)
    n_blocks = n_groups // GROUPS_PER_BLOCK
    mats = [_block_diag_in(bbar_re).astype(BF16), _block_diag_in(bbar_im).astype(BF16),
            lam_re.reshape(n_blocks, 1, SSM_ST_BLOCK), lam_im.reshape(n_blocks, 1, SSM_ST_BLOCK),
            _block_diag_out(view("c_re", c_re)).astype(BF16), _block_diag_out(view("c_im", c_im)).astype(BF16)]
    dskip_row = d_skip.reshape(1, d_ssm)
    forward_mix, _ = forward_gather(ag_mix, [attn])
    y_ssm = _s5_fwd(proj, mats, dskip_row, d_attn, d_ssm)
    gelu_bf16 = lambda yv: _gelu(yv).astype(BF16)
    wglu_g, wo_g = _split_wait(forward_mix, [y_ssm])
    w_glu_full = wglu_g.reshape(d_ssm, d_ssm)
    w_o_full = wo_g.reshape(d, d)
    glu_lin = _mm_nn("glu_gate", y_ssm, w_glu_full, F32, a_fn=gelu_bf16)

    def mix_prep(av, yv, gl, bg, ga, gs):
        ssm = _gelu(yv) * _sigmoid(gl + bg)
        return [jnp.concatenate([_rms(av)[0] * ga, _rms(ssm)[0] * gs], axis=1)], []

    mixed, = _rows("mix_prep", mix_prep, [attn, y_ssm, glu_lin], [b_glu, g_attn_out, g_ssm_out], [(d, BF16)], [], tm)
    mix = _mm_nn("mix_out", mixed, w_o_full, F32, tn=d // 2 if (d // 2) % LANES == 0 else None)

    def post_mix(xv, mv, gpm, gpf):
        h = xv + _rms(mv)[0] * gpm
        return [h, _rms(h)[0] * gpf], []

    forward_ffn_in, token = forward_gather(ag_ffn_in, [mix])
    h, hn = _rows("post_mix", post_mix, [x2, mix], [g_post_mix, g_pre_ffn], [(d, F32), (d, BF16)], [], tm,
                  after=[token])
    wgate_g, wup_g = _split_wait(forward_ffn_in, [hn])
    gate, up, hid = _ffn_in(hn, wgate_g, wup_g)
    wdown_g, = finish_gather(ag_down, [hid])
    ff = _mm_contract_slots("ffn_down", [(hid, wdown_g)], F32, per_step=4)

    def head(hv, fv, tv, gpo):
        out = hv + _rms(fv)[0] * gpo
        err = out - tv
        dout = err * (1.0 / d)
        dff, dg = _rms_bwd(fv, gpo, dout)
        loss = jnp.zeros((1, LANES), F32) + 0.5 * jnp.sum(err * err) * (1.0 / d)
        return [dff, dout], [dg, loss]

    dff, dh_out, dg_post_ffn, loss_row = _rows("loss_head", head, [h, ff, target], [g_post_ffn],
                                               [(d, BF16), (d, F32)], [d, LANES], tm)

    def swap_halves(name, grads):
        return _halves_start("swap_" + name, [g.reshape(N_DEV // 2, 2, *g.shape[1:]) for g in grads])

    def scatter_chip_sums(name, swap, after):
        both = _split_wait(swap, after)
        half = len(both) // 2
        sums = [_chip_sum("chip_sum_%s_%d" % (name, i), both[i], both[half + i]) for i in range(half)]
        return _exchange_start("scatter_" + name, sums, True, (OWN,) + CHIP_PEERS, by_chip=True)

    dw_down = _mm_slots_tn("ffn_down_dw", hid, dff, BF16)
    swap_down, token = swap_halves("dw_down", [dw_down])
    dgate, dup = _ffn_down_bwd(dff, wdown_g, gate, up, [token])
    rs_down, token = scatter_chip_sums("dw_down", swap_down, [dgate])
    dhn = _mm_contract_slots("ffn_in_dx", [(dgate, wgate_g), (dup, wup_g)], F32, per_step=2, tm=1024, tn=1024,
                             after=[token])
    dw_gate = _mm_slots_tn("ffn_gate_dw", dgate, hn, BF16)
    dw_up = _mm_slots_tn("ffn_up_dw", dup, hn, BF16)
    swap_ffn_in, tok_ffn_in = swap_halves("dw_gate_up", [dw_gate, dw_up])

    def mid_bwd(dho, dhn_, hv, mv, gpf, gpm):
        d1, dgpf = _rms_bwd(hv, gpf, dhn_)
        dh_ = dho + d1
        dmix_, dgpm = _rms_bwd(mv, gpm, dh_)
        return [dh_, dmix_], [dgpf, dgpm]

    dh, dmix, dg_pre_ffn, dg_post_mix = _rows("mid_bwd", mid_bwd, [dh_out, dhn, h, mix], [g_pre_ffn, g_post_mix],
                                              [(d, F32), (d, BF16)], [d, d], tm, after=[tok_ffn_in])

    dmixed = _mm_nt("mix_out_dx", dmix, w_o_full, F32, tn=d // 2 if (d // 2) % LANES == 0 else None)
    rs_ffn_in, token = scatter_chip_sums("dw_gate_up", swap_ffn_in, [dmixed])
    dw_o = _mm_tn("mix_out_dw", mixed, dmix, BF16, tn=d // 2 if (d // 2) % LANES == 0 else None, after=[token])
    swap_o, tok_o = swap_halves("dw_o", [dw_o.reshape(N_DEV, d // N_DEV, d)])

    def mix_bwd(dm, av, yv, gl, bg, ga, gs):
        dattn_, dga = _rms_bwd(av, ga, dm[:, :d_attn])
        z = _gelu(yv)
        sg = _sigmoid(gl + bg)
        dssm, dgs = _rms_bwd(z * sg, gs, dm[:, d_attn:])
        dgl = dssm * z * sg * (1.0 - sg)
        return [dattn_, dssm * sg, dgl], [dga, dgs, jnp.sum(dgl, axis=0, keepdims=True)]

    dattn, dz_direct, dglu, dg_attn_out, dg_ssm_out, db_glu = _rows(
        "mix_bwd", mix_bwd, [dmixed, attn, y_ssm, glu_lin], [b_glu, g_attn_out, g_ssm_out],
        [(d_attn, F32), (d_ssm, F32), (d_ssm, BF16)], [d_attn, d_ssm, d_ssm], tm, after=[tok_o])
    dz_glu = _mm_nt("glu_gate_dx", dglu, w_glu_full, F32)
    dw_glu = _mm_tn("glu_gate_dw", y_ssm, dglu, BF16, a_fn=gelu_bf16)
    rs_o, token = scatter_chip_sums("dw_o", swap_o, [dz_glu, dw_glu])

    du, db_re_dense, db_im_dense, dlam_re, dlam_im, dc_re_dense, dc_im_dense, dd_skip = _s5_bwd(
        proj, mats, dskip_row, y_ssm, dz_direct, dz_glu, d_attn, d_ssm, [token])
    da_re, da_im, dlog_dt, db_re_v, db_im_v = _s5_discretise_bwd(
        a_re[0], a_im[0], ldt_col, b_re_v, b_im_v, dlam_re.reshape(n_groups, SSM_STATE),
        dlam_im.reshape(n_groups, SSM_STATE), _block_diag_in_t(db_re_dense), _block_diag_in_t(db_im_dense))
    dq, dk2, dv2, dsinks_row = _attention_bwd(proj, cos, sin, sinks_row, dattn, d_attn)

    small_grads = {
        "sinks": dsinks_row, "a_re": da_re, "a_im": da_im, "log_dt": dlog_dt.reshape(1, n_groups),
        "b_re": db_re_v, "b_im": db_im_v, "c_re": _block_diag_out_t(dc_re_dense),
        "c_im": _block_diag_out_t(dc_im_dense), "d_skip": dd_skip.reshape(n_groups, SSM_GROUP).T, "b_glu": db_glu,
        "g_attn_out": dg_attn_out, "g_ssm_out": dg_ssm_out, "g_post_mix": dg_post_mix, "g_pre_ffn": dg_pre_ffn,
        "g_post_ffn": dg_post_ffn,
    }
    vec_pack, vec_slots, mat_pack, mat_rows = _pack_grads([small_grads[n] for n in _SMALL_VECTORS],
                                                          [small_grads[n] for n in _SMALL_MATRICES])
    ag_small, token = _exchange_start("gather_small_grads", [vec_pack, mat_pack, small_grads["d_skip"]], False,
                                      (OWN,) + ALL_PEERS)
    dproj = _assemble_dproj(dq, dk2, dv2, du, d_in, [token])

    dxn = _mm_nn("proj_in_dx", dproj, w_in_t, F32, tn=d // 2 if (d // 2) % LANES == 0 else None)
    dw_in = _mm_tn("proj_in_dw", dproj, xn, BF16).reshape(N_DEV, d_in // N_DEV, d)
    swap_in, token = swap_halves("dw_in_glu", [dw_in, dw_glu.reshape(N_DEV, d_ssm // N_DEV, d_ssm)])

    def x_bwd(dh_, dxn_, xv, g):
        dx, dg = _rms_bwd(xv, g, dxn_)
        return [dh_ + dx], [dg]

    grad_x, dg_pre_mix = _rows("norm_in_bwd", x_bwd, [dh, dxn, x2], [g_pre_mix], [(d, F32)], [d], tm, after=[token])
    ag_last, token = _exchange_start("gather_g_pre_mix_grad", [dg_pre_mix], False, (OWN,) + ALL_PEERS)
    rs_in, token = scatter_chip_sums("dw_in_glu", swap_in, [grad_x, token])

    results = {}

    def adam_big(n, parts):
        r = parts.shape[1]
        results[n] = _adam_sharded("adam_" + n, parts, by_rows(n, weights[n]), by_rows(n, mom_m[n]),
                                   by_rows(n, mom_v[n]), 64 if r % 64 == 0 else r)
        return results[n][3]

    done = [grad_x, token]
    adam_big("w_down", _exchange_wait(rs_down, done)[0])
    p_gate, p_up = _exchange_wait(rs_ffn_in, done)
    done = [adam_big("w_gate", p_gate), adam_big("w_up", p_up), results["w_down"][3]]
    done = [adam_big("w_o", _exchange_wait(rs_o, done)[0])]
    vec_parts, mat_parts, dskip_parts = _exchange_wait(ag_small, done)
    first_gain_parts, = _exchange_wait(ag_last, done)
    for n, row0 in zip(_SMALL_MATRICES, mat_rows):
        rows = view(n, weights[n]).shape[0]
        results[n] = _adam_sharded("adam_" + n, mat_parts, view(n, weights[n]), view(n, mom_m[n]), view(n, mom_v[n]),
                                   rows, row0)
    rest = _SMALL_VECTORS + ("d_skip", "g_pre_mix")
    found_at = [(0, row, lane) for row, lane, _ in vec_slots] + [(1, 0, 0), (2, 0, 0)]
    updated = _adam_replicated([vec_parts, dskip_parts, first_gain_parts], found_at,
                               [view(n, weights[n]) for n in rest], [view(n, mom_m[n]) for n in rest],
                               [view(n, mom_v[n]) for n in rest])
    results.update(zip(rest, updated))
    p_in, p_glu = _exchange_wait(rs_in, [results[n][3] for n in _SMALL_MATRICES] + [updated[0][3]])
    adam_big("w_in", p_in)
    adam_big("w_glu", p_glu)

    loss = lax.psum(loss_row[0, 0], ("x", "y", "c"))
    outs = [loss, grad_x[None]]
    for k in range(4):
        for n in _ORDER:
            val = results[n][k]
            outs.append(val.T[None] if n in _BY_COLUMNS else val[None] if n in _BIG else unview(n, val))
    return tuple(outs)
```

```python
import math

import jax
import jax.numpy as jnp
from jax import lax
from jax.experimental import pallas as pl
from jax.experimental.pallas import tpu as pltpu

F32 = jnp.float32
BF16 = jnp.bfloat16

HEAD_DIM = 64
N_KV_HEADS = 4
D_KV = N_KV_HEADS * HEAD_DIM
WINDOW = 128
BLOCK = 128
ROPE_THETA = 10000.0
SSM_GROUP = 16
SSM_STATE = 64
GROUPS_PER_BLOCK = 8
SSM_CH_BLOCK = GROUPS_PER_BLOCK * SSM_GROUP
SSM_ST_BLOCK = GROUPS_PER_BLOCK * SSM_STATE
RMS_EPS = 1e-6
N_DEV = 8
LANES = 128
SUBLANES = 8
MASKED = -1e30

ADAM_LR = 0.001
ADAM_B1 = 0.9
ADAM_B2 = 0.999
ADAM_EPS = 1e-08
ADAM_WD = 0.01
ADAM_STEP = 10

VMEM_LIMIT_BYTES = 56 * 1024 * 1024


def _call(body, *, name, out_shape, in_specs, out_specs, grid=(), scratch_shapes=(), semantics=None, n_after=0):
    params = dict(vmem_limit_bytes=VMEM_LIMIT_BYTES)
    if semantics is not None:
        params["dimension_semantics"] = semantics
    n_in = len(in_specs)
    if n_after:
        inner = body

        def body(*refs):
            inner(*refs[:n_in], *refs[n_in + n_after:])

        in_specs = list(in_specs) + [pl.BlockSpec(memory_space=pl.ANY)] * n_after
    return pl.pallas_call(body, name=name, grid=grid, in_specs=in_specs, out_specs=out_specs, out_shape=out_shape,
                          scratch_shapes=scratch_shapes, compiler_params=pltpu.CompilerParams(**params))


def _sds(shape, dtype):
    return jax.ShapeDtypeStruct(tuple(shape), dtype)


def _dot(a, b, ca, cb):
    return lax.dot_general(a, b, (((ca,), (cb,)), ((), ())), preferred_element_type=F32)


def _rms(x):
    r = lax.rsqrt(jnp.mean(x * x, axis=-1, keepdims=True) + RMS_EPS)
    return x * r, r


def _rms_bwd(x, g, dy):
    xh, r = _rms(x)
    dxh = dy * g
    dx = r * (dxh - xh * jnp.mean(dxh * xh, axis=-1, keepdims=True))
    return dx, jnp.sum(dy * xh, axis=0, keepdims=True)


def _sigmoid(x):
    return 1.0 / (1.0 + jnp.exp(-x))


_GELU_C = math.sqrt(2.0 / math.pi)
_GELU_A = 0.044715


def _gelu(y):
    t = jnp.tanh(_GELU_C * (y + _GELU_A * y * y * y))
    return 0.5 * y * (1.0 + t)


def _gelu_grad(y):
    t = jnp.tanh(_GELU_C * (y + _GELU_A * y * y * y))
    return 0.5 * (1.0 + t) + 0.5 * y * (1.0 - t * t) * _GELU_C * (1.0 + 3.0 * _GELU_A * y * y)


def _rows(name, fn, row_ins, vec_ins, row_outs, acc_widths, tm, after=()):
    rows = row_ins[0].shape[0]
    assert rows % tm == 0, (name, rows, tm)
    n_row, n_vec, n_out, n_acc = len(row_ins), len(vec_ins), len(row_outs), len(acc_widths)

    def body(*refs):
        ins = [r[...] for r in refs[:n_row + n_vec]]
        outs = refs[n_row + n_vec:n_row + n_vec + n_out]
        accs = refs[n_row + n_vec + n_out:]
        row_vals, acc_vals = fn(*ins)
        for o, v in zip(outs, row_vals):
            o[...] = v.astype(o.dtype)
        if n_acc:
            @pl.when(pl.program_id(0) == 0)
            def _():
                for a in accs:
                    a[...] = jnp.zeros_like(a)
            for a, v in zip(accs, acc_vals):
                a[...] += v

    in_specs = [pl.BlockSpec((tm, a.shape[1]), lambda i: (i, 0)) for a in row_ins]
    in_specs += [pl.BlockSpec(v.shape, lambda i: (0, 0)) for v in vec_ins]
    out_specs = [pl.BlockSpec((tm, w), lambda i: (i, 0)) for w, _ in row_outs]
    out_specs += [pl.BlockSpec((1, w), lambda i: (0, 0)) for w in acc_widths]
    out_shape = [_sds((rows, w), dt) for w, dt in row_outs] + [_sds((1, w), F32) for w in acc_widths]
    return _call(body, name=name, grid=(rows // tm,), in_specs=in_specs, out_specs=out_specs, out_shape=out_shape,
                 semantics=("arbitrary",) if n_acc else ("parallel",), n_after=len(after))(*row_ins, *vec_ins, *after)


def _matmul(name, operands, in_specs, product, grid, out_shape, out_spec, acc_shape, after=()):
    nk = grid[-1]
    n_in = len(operands)
    in_place = out_shape.dtype == F32

    def body(*refs):
        ins = [r[...] for r in refs[:n_in]]
        o_ref = refs[n_in]
        if nk == 1:
            o_ref[...] = product(*ins).astype(o_ref.dtype)
            return
        acc = o_ref if in_place else refs[n_in + 1]
        k = pl.program_id(len(grid) - 1)

        @pl.when(k == 0)
        def _():
            acc[...] = jnp.zeros_like(acc)

        acc[...] += product(*ins)

        if not in_place:
            @pl.when(k == nk - 1)
            def _():
                o_ref[...] = acc[...].astype(o_ref.dtype)

    return _call(body, name=name, grid=grid, in_specs=in_specs, out_specs=out_spec, out_shape=out_shape,
                 scratch_shapes=[] if nk == 1 or in_place else [pltpu.VMEM(acc_shape, F32)],
                 semantics=("parallel",) * (len(grid) - 1) + ("arbitrary",), n_after=len(after))(*operands, *after)


def _mm_nn(name, a, b, out_dtype, tm=512, tn=None, a_fn=lambda x: x):
    m, k = a.shape
    n = b.shape[1]
    tm, tn = min(tm, m), n if tn is None else tn
    return _matmul(name, [a, b],
                   [pl.BlockSpec((tm, k), lambda i, j, s: (i, 0)), pl.BlockSpec((k, tn), lambda i, j, s: (0, j))],
                   lambda x, y: _dot(a_fn(x), y, 1, 0), (m // tm, n // tn, 1), _sds((m, n), out_dtype),
                   pl.BlockSpec((tm, tn), lambda i, j, s: (i, j)), (tm, tn))


def _mm_nt(name, a, b, out_dtype, tm=512, tn=None):
    m, k = a.shape
    n = b.shape[0]
    tm, tn = min(tm, m), n if tn is None else tn
    return _matmul(name, [a, b],
                   [pl.BlockSpec((tm, k), lambda i, j, s: (i, 0)), pl.BlockSpec((tn, k), lambda i, j, s: (j, 0))],
                   lambda x, y: _dot(x, y, 1, 1), (m // tm, n // tn, 1), _sds((m, n), out_dtype),
                   pl.BlockSpec((tm, tn), lambda i, j, s: (i, j)), (tm, tn))


def _mm_tn(name, a, b, out_dtype, tm=512, tn=None, tk=2048, a_fn=lambda x: x, after=()):
    k, m = a.shape
    n = b.shape[1]
    tm, tk, tn = min(tm, m), min(tk, k), n if tn is None else tn
    return _matmul(name, [a, b],
                   [pl.BlockSpec((tk, tm), lambda i, j, s: (s, i)), pl.BlockSpec((tk, tn), lambda i, j, s: (s, j))],
                   lambda x, y: _dot(a_fn(x), y, 0, 0), (m // tm, n // tn, k // tk), _sds((m, n), out_dtype),
                   pl.BlockSpec((tm, tn), lambda i, j, s: (i, j)), (tm, tn), after)


def _mm_contract_slots(name, pairs, out_dtype, per_step, tm=512, tn=2048, after=()):
    s_, m, k = pairs[0][0].shape
    n = pairs[0][1].shape[2]
    tm, tn = min(tm, m), min(tn, n)
    ops, specs = [], []
    for a, b in pairs:
        ops += [a, b]
        specs += [pl.BlockSpec((per_step, tm, k), lambda i, j, s: (s, i, 0)),
                  pl.BlockSpec((per_step, k, tn), lambda i, j, s: (s, 0, j))]

    def product(*t):
        return sum(_dot(t[2 * p][q], t[2 * p + 1][q], 1, 0) for p in range(len(pairs)) for q in range(per_step))

    return _matmul(name, ops, specs, product, (m // tm, n // tn, s_ // per_step), _sds((m, n), out_dtype),
                   pl.BlockSpec((tm, tn), lambda i, j, s: (i, j)), (tm, tn), after)


def _mm_slots_tn(name, a, b, out_dtype, tn=2048, tk=2048):
    s_, k, m = a.shape
    n = b.shape[1]
    tn, tk = min(tn, n), min(tk, k)
    return _matmul(name, [a, b],
                   [pl.BlockSpec((None, tk, m), lambda s, j, z: (s, z, 0)), pl.BlockSpec((tk, tn), lambda s, j, z: (z, j))],
                   lambda x, y: _dot(x, y, 0, 0), (s_, n // tn, k // tk), _sds((s_, m, n), out_dtype),
                   pl.BlockSpec((None, m, tn), lambda s, j, z: (s, 0, j)), (m, tn))


def _ffn_in(a, w_gate, w_up, tm=512):
    m, k = a.shape
    s_, n, _ = w_gate.shape
    tm = min(tm, m)

    def body(a_ref, wg_ref, wu_ref, g_ref, u_ref, h_ref):
        x = a_ref[...]
        g = _dot(x, wg_ref[...], 1, 1)
        u = _dot(x, wu_ref[...], 1, 1)
        g_ref[...] = g.astype(BF16)
        u_ref[...] = u.astype(BF16)
        h_ref[...] = (g * _sigmoid(g) * u).astype(BF16)

    w_spec = pl.BlockSpec((None, n, k), lambda s, i: (s, 0, 0))
    o_spec = pl.BlockSpec((None, tm, n), lambda s, i: (s, i, 0))
    return _call(body, name="ffn_in", grid=(s_, m // tm),
                 in_specs=[pl.BlockSpec((tm, k), lambda s, i: (i, 0)), w_spec, w_spec], out_specs=[o_spec] * 3,
                 out_shape=[_sds((s_, m, n), BF16)] * 3, semantics=("parallel", "parallel"))(a, w_gate, w_up)


def _ffn_down_bwd(d_out, w_down, gate, up, after, tm=512):
    m, k = d_out.shape
    s_, n, _ = w_down.shape
    tm = min(tm, m)

    def body(d_ref, w_ref, g_ref, u_ref, dg_ref, du_ref):
        rows = pl.ds(pl.multiple_of(pl.program_id(1) * tm, tm), tm)
        dh = _dot(d_ref[rows, :], w_ref[...], 1, 1)
        g = g_ref[...].astype(F32)
        sg = _sigmoid(g)
        dg_ref[...] = (dh * u_ref[...].astype(F32) * sg * (1.0 + g * (1.0 - sg))).astype(BF16)
        du_ref[...] = (dh * g * sg).astype(BF16)

    t_spec = pl.BlockSpec((None, tm, n), lambda s, i: (s, i, 0))
    return _call(body, name="ffn_down_dx", grid=(s_, m // tm),
                 in_specs=[pl.BlockSpec((m, k), lambda s, i: (0, 0)), pl.BlockSpec((None, n, k), lambda s, i: (s, 0, 0)),
                           t_spec, t_spec],
                 out_specs=[t_spec] * 2, out_shape=[_sds((s_, m, n), BF16)] * 2, semantics=("parallel", "parallel"),
                 n_after=len(after))(d_out, w_down, gate, up, *after)


ALL_PEERS = (1, 2, 3, 4, 5, 6, 7)
CHIP_PEERS = (2, 4, 6)
SIBLING = 1
OWN = 0


def _peer(relation):
    x, y, c = lax.axis_index("x"), lax.axis_index("y"), lax.axis_index("c")
    pos = (1 - x if relation & 4 else x, 1 - y if relation & 2 else y, 1 - c if relation & 1 else c)
    return pos, 4 * pos[0] + 2 * pos[1] + pos[2]


def _slot(relation, by_chip):
    pos, device = _peer(relation)
    return 2 * pos[0] + pos[1] if by_chip else device


def _exchange_copies(ins, lands, send_sems, recv_sems, scatter, relations, by_chip=False):
    me = _slot(0, by_chip)

    def copy(a, s, peer, pos, dst_slot):
        return pltpu.make_async_remote_copy(
            src_ref=ins[a].at[peer] if scatter else ins[a], dst_ref=lands[a].at[dst_slot],
            send_sem=send_sems.at[s], recv_sem=recv_sems.at[s], device_id=pos, device_id_type=pl.DeviceIdType.MESH)

    pairs = []
    for k, r in enumerate(relations):
        pos, peer = _peer(r)[0], _slot(r, by_chip)
        for a in range(len(ins)):
            s = a * len(relations) + k
            pairs.append((copy(a, s, peer, pos, me), copy(a, s, peer, pos, peer)))
    return pairs


def _halves_copies(arrays, lands, send_sems, recv_sems):
    sibling, _ = _peer(SIBLING)
    core = lax.axis_index("c")
    pairs = []
    for a, (ref, land) in enumerate(zip(arrays, lands)):
        send = pltpu.make_async_remote_copy(
            src_ref=ref.at[:, pl.ds(1 - core, 1)], dst_ref=land, send_sem=send_sems.at[a], recv_sem=recv_sems.at[a],
            device_id=sibling, device_id_type=pl.DeviceIdType.MESH)
        pairs.append((send, send))
    return pairs


def _forward_copies(lands, send_sems, recv_sems):
    sibling, _ = _peer(SIBLING)

    def copy(a, s, slot):
        return pltpu.make_async_remote_copy(
            src_ref=lands[a].at[slot], dst_ref=lands[a].at[slot], send_sem=send_sems.at[s], recv_sem=recv_sems.at[s],
            device_id=sibling, device_id_type=pl.DeviceIdType.MESH)

    pairs = []
    for k, r in enumerate(CHIP_PEERS):
        _, mine = _peer(r)
        _, theirs = _peer(r | SIBLING)
        for a in range(len(lands)):
            s = a * len(CHIP_PEERS) + k
            pairs.append((copy(a, s, mine), copy(a, s, theirs)))
    return pairs


_HBM_SPEC = pl.BlockSpec(memory_space=pltpu.HBM)
_SEM_SPEC = pl.BlockSpec(memory_space=pltpu.SEMAPHORE)
_SIDE_EFFECT = pltpu.SideEffectType.DATAFLOW_SIDE_EFFECTING


def _split_start(name, operands, n_sem, make_pairs):
    k = len(operands)

    def body(*refs):
        send_sems, recv_sems, token = refs[k], refs[k + 1], refs[-1]
        for send, _ in make_pairs(refs[:k], send_sems, recv_sems):
            send.start()
        token[...] = jnp.zeros_like(token)

    out = pl.pallas_call(
        body, name=name,
        out_shape=(pltpu.SemaphoreType.DMA((n_sem,)), pltpu.SemaphoreType.DMA((n_sem,)),
                   *[pltpu.HBM(a.shape, a.dtype) for a in operands], _sds((SUBLANES, LANES), F32)),
        in_specs=[_HBM_SPEC] * k,
        out_specs=(_SEM_SPEC, _SEM_SPEC, *[_HBM_SPEC] * k, pl.BlockSpec(memory_space=pltpu.VMEM)),
        input_output_aliases={i: 2 + i for i in range(k)},
        compiler_params=pltpu.CompilerParams(has_side_effects=_SIDE_EFFECT),
    )(*[pltpu.with_memory_space_constraint(a, pltpu.HBM) for a in operands])
    return dict(name=name, sems=out[:2], thru=list(out[2:2 + k]), make_pairs=make_pairs), out[-1]


def _split_wait(handle, after):
    thru, make_pairs = handle["thru"], handle["make_pairs"]
    k = len(thru)

    def body(*refs):
        for send, arrival in make_pairs(refs[:k], refs[k], refs[k + 1]):
            send.wait_send()
            arrival.wait_recv()

    return pl.pallas_call(
        body, name=handle["name"] + "_wait", out_shape=[pltpu.HBM(a.shape, a.dtype) for a in thru],
        in_specs=[_HBM_SPEC] * k + [_SEM_SPEC, _SEM_SPEC] + [pl.BlockSpec(memory_space=pl.ANY)] * len(after),
        out_specs=[_HBM_SPEC] * k, input_output_aliases={i: i for i in range(k)},
        compiler_params=pltpu.CompilerParams(has_side_effects=_SIDE_EFFECT),
    )(*thru, *handle["sems"], *after)


def _exchange_start(name, arrays, scatter, relations, by_chip=False):
    n = len(arrays)
    lands = [lax.empty(a.shape if scatter else (N_DEV,) + a.shape, a.dtype) for a in arrays]

    def make_pairs(refs, send_sems, recv_sems):
        return _exchange_copies(refs[:n], refs[n:], send_sems, recv_sems, scatter, relations, by_chip)

    handle, token = _split_start(name, list(arrays) + lands, n * len(relations), make_pairs)
    handle.update(n=n)
    return handle, token


def _halves_start(name, arrays):
    lands = [lax.empty((a.shape[0], 1) + a.shape[2:], a.dtype) for a in arrays]
    n = len(arrays)

    def make_pairs(refs, send_sems, recv_sems):
        return _halves_copies(refs[:n], refs[n:], send_sems, recv_sems)

    return _split_start(name, list(arrays) + lands, n, make_pairs)


def _chip_sum(name, array, landed):
    chips, _, r, c = array.shape
    tr = r // 2 if r > 512 and r % 32 == 0 else r

    def body(a_ref, b_ref, o_ref):
        mine = a_ref[lax.axis_index("c")].astype(F32)
        o_ref[...] = (mine + b_ref[...].astype(F32)).astype(o_ref.dtype)

    return _call(body, name=name, grid=(chips, r // tr),
                 in_specs=[pl.BlockSpec((None, 2, tr, c), lambda k, i: (k, 0, i, 0)),
                           pl.BlockSpec((None, None, tr, c), lambda k, i: (k, 0, i, 0))],
                 out_specs=pl.BlockSpec((None, tr, c), lambda k, i: (k, i, 0)),
                 out_shape=_sds((chips, r, c), BF16), semantics=("parallel", "parallel"))(array, landed)


def _forward_start(name, lands):
    return _split_start(name, list(lands), len(lands) * len(CHIP_PEERS), _forward_copies)


def _exchange_wait(handle, after):
    return _split_wait(handle, after)[handle["n"]:]


def _rope_tables(pos_col):
    t = pos_col.shape[0]
    half = HEAD_DIM // 2
    inv_freq = ROPE_THETA ** (-jnp.arange(half, dtype=F32) / half)
    inv_row = jnp.tile(inv_freq, LANES // half)[None, :]

    def body(pos_ref, inv_ref, cos_ref, sin_ref):
        ang = pos_ref[...] * inv_ref[...]
        cos_ref[...] = jnp.cos(ang)
        sin_ref[...] = jnp.sin(ang)

    tm = min(t, 512)
    return _call(body, name="rope_tables", grid=(t // tm,),
                 in_specs=[pl.BlockSpec((tm, 1), lambda i: (i, 0)), pl.BlockSpec((1, LANES), lambda i: (0, 0))],
                 out_specs=[pl.BlockSpec((tm, LANES), lambda i: (i, 0))] * 2,
                 out_shape=[_sds((t, LANES), F32)] * 2, semantics=("parallel",))(pos_col, inv_row)


def _rot_half(x):
    lane = lax.broadcasted_iota(jnp.int32, x.shape, 1)
    low = (lane % HEAD_DIM) < HEAD_DIM // 2
    return jnp.where(low, -pltpu.roll(x, LANES - HEAD_DIM // 2, 1), pltpu.roll(x, HEAD_DIM // 2, 1))


def _rope(x, cos, sin):
    return x * cos + _rot_half(x) * sin


def _unrope(d, cos, sin):
    return d * cos - _rot_half(d) * sin


def _band_mask(first_block, heads):
    r = lax.broadcasted_iota(jnp.int32, (heads * BLOCK, 2 * BLOCK), 0) % BLOCK
    c = lax.broadcasted_iota(jnp.int32, (heads * BLOCK, 2 * BLOCK), 1)
    diff = r - c + BLOCK
    return (diff >= 0) & (diff < WINDOW) & ((c >= BLOCK) | jnp.logical_not(first_block))


def _attn_specs(t, d_attn, d_in):
    kb, vb = d_attn // D_KV, d_attn // D_KV + 1
    prev = lambda i: jnp.maximum(i - 1, 0)
    return [
        pl.BlockSpec((BLOCK, d_attn), lambda i: (i, 0)),
        pl.BlockSpec((BLOCK, D_KV), lambda i: (i, kb)),
        pl.BlockSpec((BLOCK, D_KV), lambda i: (i, vb)),
        pl.BlockSpec((BLOCK, D_KV), lambda i: (prev(i), kb)),
        pl.BlockSpec((BLOCK, D_KV), lambda i: (prev(i), vb)),
        pl.BlockSpec((BLOCK, LANES), lambda i: (i, 0)),
        pl.BlockSpec((BLOCK, LANES), lambda i: (i, 0)),
        pl.BlockSpec((BLOCK, LANES), lambda i: (prev(i), 0)),
        pl.BlockSpec((BLOCK, LANES), lambda i: (prev(i), 0)),
        pl.BlockSpec((1, LANES), lambda i: (0, 0)),
    ]


def _head(x, h):
    return x[:, h * HEAD_DIM:(h + 1) * HEAD_DIM]


def _attn_heads(q_ref, kc_ref, vc_ref, kp_ref, vp_ref, cq_ref, sq_ref, cp_ref, sp_ref, d_attn):
    cq, sq, cp, sp = cq_ref[...], sq_ref[...], cp_ref[...], sp_ref[...]
    q_rot = [_rope(q_ref[:, j * LANES:(j + 1) * LANES], cq, sq) for j in range(d_attn // LANES)]
    kc_rot = [_rope(kc_ref[:, j * LANES:(j + 1) * LANES], cq, sq) for j in range(D_KV // LANES)]
    kp_rot = [_rope(kp_ref[:, j * LANES:(j + 1) * LANES], cp, sp) for j in range(D_KV // LANES)]
    per = LANES // HEAD_DIM
    q_heads = [_head(q_rot[h // per], h % per).astype(BF16) for h in range(d_attn // HEAD_DIM)]
    kk = [jnp.concatenate([_head(kp_rot[g // per], g % per), _head(kc_rot[g // per], g % per)], axis=0).astype(BF16)
          for g in range(N_KV_HEADS)]
    vv = [jnp.concatenate([_head(vp_ref[...], g), _head(vc_ref[...], g)], axis=0).astype(BF16) for g in range(N_KV_HEADS)]
    return q_heads, kk, vv


def _stack_group(q_heads, sink_ref, group):
    q_all = jnp.concatenate([q_heads[h] for h in group], axis=0)
    sink_all = jnp.concatenate([jnp.broadcast_to(sink_ref[:, h:h + 1], (BLOCK, 1)) for h in group], axis=0)
    return q_all, sink_all


def _softmax_with_sink(q, kk, sink, mask):
    s = _dot(q, kk, 1, 1) * (1.0 / math.sqrt(HEAD_DIM))
    s = jnp.where(mask, s, MASKED)
    m = jnp.maximum(jnp.max(s, axis=-1, keepdims=True), sink)
    p = jnp.exp(s - m)
    e_sink = jnp.exp(sink - m)
    inv = 1.0 / (jnp.sum(p, axis=-1, keepdims=True) + e_sink)
    return p * inv, e_sink * inv


def _attention_fwd(proj, cos, sin, sinks_row, d_attn):
    t, d_in = proj.shape
    n_heads = d_attn // HEAD_DIM
    q_per_kv = n_heads // N_KV_HEADS

    def body(q_ref, kc_ref, vc_ref, kp_ref, vp_ref, cq_ref, sq_ref, cp_ref, sp_ref, sink_ref, o_ref):
        mask = _band_mask(pl.program_id(0) == 0, q_per_kv)
        q_heads, kk, vv = _attn_heads(q_ref, kc_ref, vc_ref, kp_ref, vp_ref, cq_ref, sq_ref, cp_ref, sp_ref, d_attn)
        for g in range(N_KV_HEADS):
            group = range(g * q_per_kv, (g + 1) * q_per_kv)
            q_all, sink_all = _stack_group(q_heads, sink_ref, group)
            probs, _ = _softmax_with_sink(q_all, kk[g], sink_all, mask)
            o_all = _dot(probs.astype(BF16), vv[g], 1, 0)
            for k, h in enumerate(group):
                o_ref[:, h * HEAD_DIM:(h + 1) * HEAD_DIM] = o_all[k * BLOCK:(k + 1) * BLOCK]

    return _call(body, name="attention_fwd", grid=(t // BLOCK,), in_specs=_attn_specs(t, d_attn, d_in),
                 out_specs=pl.BlockSpec((BLOCK, d_attn), lambda i: (i, 0)), out_shape=_sds((t, d_attn), F32),
                 semantics=("parallel",))(proj, proj, proj, proj, proj, cos, sin, cos, sin, sinks_row)


def _attention_bwd(proj, cos, sin, sinks_row, d_out, d_attn):
    t, d_in = proj.shape
    n_heads = d_attn // HEAD_DIM
    q_per_kv = n_heads // N_KV_HEADS
    nb = t // BLOCK
    per = LANES // HEAD_DIM

    def body(q_ref, kc_ref, vc_ref, kp_ref, vp_ref, cq_ref, sq_ref, cp_ref, sp_ref, sink_ref, do_ref,
             dq_ref, dk_ref, dv_ref, dsink_ref):
        i = pl.program_id(0)
        mask = _band_mask(i == 0, q_per_kv)
        q_heads, kk, vv = _attn_heads(q_ref, kc_ref, vc_ref, kp_ref, vp_ref, cq_ref, sq_ref, cp_ref, sp_ref, d_attn)
        lane = lax.broadcasted_iota(jnp.int32, (1, LANES), 1)
        dsink = jnp.zeros((1, LANES), F32)
        dq_rot, dkk, dvv = [], [], []
        for g in range(N_KV_HEADS):
            group = range(g * q_per_kv, (g + 1) * q_per_kv)
            q_all, sink_all = _stack_group(q_heads, sink_ref, group)
            probs, p_sink = _softmax_with_sink(q_all, kk[g], sink_all, mask)
            do_all = jnp.concatenate([do_ref[:, h * HEAD_DIM:(h + 1) * HEAD_DIM] for h in group], axis=0).astype(BF16)
            dp = _dot(do_all, vv[g], 1, 1)
            delta = jnp.sum(probs * dp, axis=-1, keepdims=True)
            ds = (probs * (dp - delta) * (1.0 / math.sqrt(HEAD_DIM))).astype(BF16)
            dq_all = _dot(ds, kk[g], 1, 0)
            dkk.append(_dot(ds, q_all, 0, 0))
            dvv.append(_dot(probs.astype(BF16), do_all, 0, 0))
            sink_term = p_sink * delta
            for k, h in enumerate(group):
                dq_rot.append(dq_all[k * BLOCK:(k + 1) * BLOCK])
                part = jnp.sum(sink_term[k * BLOCK:(k + 1) * BLOCK], axis=0, keepdims=True)
                dsink += jnp.where(lane == h, -part, 0.0)
        cq, sq, cp, sp = cq_ref[...], sq_ref[...], cp_ref[...], sp_ref[...]
        for j in range(d_attn // LANES):
            d = jnp.concatenate(dq_rot[j * per:(j + 1) * per], axis=1)
            dq_ref[:, j * LANES:(j + 1) * LANES] = _unrope(d, cq, sq)
        for j in range(D_KV // LANES):
            d = jnp.concatenate(dkk[j * per:(j + 1) * per], axis=1)
            dk_ref[0, :, j * LANES:(j + 1) * LANES] = _unrope(d[:BLOCK], cp, sp)
            dk_ref[1, :, j * LANES:(j + 1) * LANES] = _unrope(d[BLOCK:], cq, sq)
            d = jnp.concatenate(dvv[j * per:(j + 1) * per], axis=1)
            dv_ref[0, :, j * LANES:(j + 1) * LANES] = d[:BLOCK]
            dv_ref[1, :, j * LANES:(j + 1) * LANES] = d[BLOCK:]

        @pl.when(i == 0)
        def _():
            dsink_ref[...] = jnp.zeros_like(dsink_ref)

        dsink_ref[...] += dsink

    pair = pl.BlockSpec((2, BLOCK, D_KV), lambda i: (i, 0, 0))
    return _call(body, name="attention_bwd", grid=(nb,),
                 in_specs=_attn_specs(t, d_attn, d_in) + [pl.BlockSpec((BLOCK, d_attn), lambda i: (i, 0))],
                 out_specs=[pl.BlockSpec((BLOCK, d_attn), lambda i: (i, 0)), pair, pair,
                            pl.BlockSpec((1, LANES), lambda i: (0, 0))],
                 out_shape=[_sds((t, d_attn), F32), _sds((2 * nb, BLOCK, D_KV), F32), _sds((2 * nb, BLOCK, D_KV), F32),
                            _sds((1, LANES), F32)],
                 semantics=("arbitrary",))(proj, proj, proj, proj, proj, cos, sin, cos, sin, sinks_row, d_out)


def _assemble_dproj(dq, dk2, dv2, du, d_in, after):
    t, d_attn = dq.shape
    d_ssm = du.shape[1]
    nb = t // BLOCK

    def body(dq_ref, dk_own, dk_next, dv_own, dv_next, du_ref, o_ref):
        has_next = (pl.program_id(0) < nb - 1).astype(F32)
        o_ref[:, :d_attn] = dq_ref[...].astype(BF16)
        o_ref[:, d_attn:d_attn + D_KV] = (dk_own[...] + has_next * dk_next[...]).astype(BF16)
        o_ref[:, d_attn + D_KV:d_attn + 2 * D_KV] = (dv_own[...] + has_next * dv_next[...]).astype(BF16)
        o_ref[:, d_attn + 2 * D_KV:] = du_ref[...].astype(BF16)

    own = pl.BlockSpec((None, BLOCK, D_KV), lambda i: (2 * i + 1, 0, 0))
    nxt = pl.BlockSpec((None, BLOCK, D_KV), lambda i: (jnp.minimum(2 * i + 2, 2 * nb - 1), 0, 0))
    return _call(body, name="assemble_dproj", grid=(nb,),
                 in_specs=[pl.BlockSpec((BLOCK, d_attn), lambda i: (i, 0)), own, nxt, own, nxt,
                           pl.BlockSpec((BLOCK, d_ssm), lambda i: (i, 0))],
                 out_specs=pl.BlockSpec((BLOCK, d_in), lambda i: (i, 0)), out_shape=_sds((t, d_in), BF16),
                 semantics=("parallel",), n_after=len(after))(dq, dk2, dk2, dv2, dv2, du, *after)


def _discretise(ar, ai, ldt, br, bi):
    dt = jnp.exp(ldt)
    mag = jnp.exp(ar * dt)
    lam_re = mag * jnp.cos(ai * dt)
    lam_im = mag * jnp.sin(ai * dt)
    den = ar * ar + ai * ai
    nr = lam_re - 1.0
    ni = lam_im
    f_re = (nr * ar + ni * ai) / den
    f_im = (ni * ar - nr * ai) / den
    return (lam_re, lam_im, [f_re * r - f_im * i for r, i in zip(br, bi)], [f_re * i + f_im * r for r, i in zip(br, bi)])


def _whole(arrays):
    return [pl.BlockSpec(a.shape, lambda *_, nd=len(a.shape): (0,) * nd) for a in arrays]


def _channels(ref):
    groups = ref.shape[0] // SSM_GROUP
    return [ref[pl.ds(p, groups, stride=SSM_GROUP), :] for p in range(SSM_GROUP)]


def _store_channels(ref, values):
    groups = ref.shape[0] // SSM_GROUP
    for p, val in enumerate(values):
        ref[pl.ds(p, groups, stride=SSM_GROUP), :] = val


def _s5_discretise(ar, ai, ldt, br, bi):
    ins = [ar, ai, ldt, br, bi]

    def body(ar_ref, ai_ref, ldt_ref, br_ref, bi_ref, lr_ref, li_ref, bbr_ref, bbi_ref):
        lr, li, bbr, bbi = _discretise(ar_ref[...], ai_ref[...], ldt_ref[...], _channels(br_ref), _channels(bi_ref))
        lr_ref[...] = lr
        li_ref[...] = li
        _store_channels(bbr_ref, bbr)
        _store_channels(bbi_ref, bbi)

    outs = [_sds(ar.shape, F32), _sds(ar.shape, F32), _sds(br.shape, F32), _sds(br.shape, F32)]
    return _call(body, name="s5_discretise", in_specs=_whole(ins), out_specs=_whole(outs), out_shape=outs)(*ins)


def _s5_discretise_bwd(ar, ai, ldt, br, bi, d_lr, d_li, d_bbr, d_bbi):
    ins = [ar, ai, ldt, br, bi, d_lr, d_li, d_bbr, d_bbi]

    def body(ar_ref, ai_ref, ldt_ref, br_ref, bi_ref, dlr_ref, dli_ref, dbbr_ref, dbbi_ref,
             dar_ref, dai_ref, dldt_ref, dbr_ref, dbi_ref):
        _, vjp = jax.vjp(_discretise, ar_ref[...], ai_ref[...], ldt_ref[...], _channels(br_ref), _channels(bi_ref))
        dar, dai, dldt, dbr, dbi = vjp((dlr_ref[...], dli_ref[...], _channels(dbbr_ref), _channels(dbbi_ref)))
        dar_ref[...] = dar
        dai_ref[...] = dai
        dldt_ref[...] = dldt
        _store_channels(dbr_ref, dbr)
        _store_channels(dbi_ref, dbi)

    outs = [_sds(a.shape, F32) for a in (ar, ai, ldt, br, bi)]
    return _call(body, name="s5_discretise_bwd", in_specs=_whole(ins), out_specs=_whole(outs), out_shape=outs)(*ins)


def _cmul(ar, ai, br, bi):
    return ar * br - ai * bi, ar * bi + ai * br


def _load_segmented(ref, tile0, n_tiles, seg):
    return jnp.concatenate([ref[pl.ds(tile0 + j, SUBLANES, stride=seg), :] for j in range(n_tiles)], axis=0)


def _store_segmented(ref, tile0, seg, value):
    for j in range(value.shape[0] // SUBLANES):
        ref[pl.ds(tile0 + j, SUBLANES, stride=seg), :] = value[j * SUBLANES:(j + 1) * SUBLANES, :]


def _fill_powers(lr, li, pr_ref, pi_ref, seg):
    pows = [(lr, li)]
    for _ in range(SUBLANES - 1):
        pows.append(_cmul(pows[-1][0], pows[-1][1], lr, li))
    row = lax.broadcasted_iota(jnp.int32, (SUBLANES, lr.shape[1]), 0)
    tr = jnp.zeros((SUBLANES, lr.shape[1]), F32)
    ti = jnp.zeros((SUBLANES, lr.shape[1]), F32)
    for r in range(SUBLANES):
        tr = jnp.where(row == r, pows[r][0], tr)
        ti = jnp.where(row == r, pows[r][1], ti)
    pr_ref[0:SUBLANES, :] = tr
    pi_ref[0:SUBLANES, :] = ti
    k = SUBLANES
    while k < seg:
        fr, fi = pr_ref[k - 1:k, :], pi_ref[k - 1:k, :]
        for t0 in range(0, k, SUBLANES):
            nr, ni = _cmul(pr_ref[t0:t0 + SUBLANES, :], pi_ref[t0:t0 + SUBLANES, :], fr, fi)
            pr_ref[k + t0:k + t0 + SUBLANES, :] = nr
            pi_ref[k + t0:k + t0 + SUBLANES, :] = ni
        k *= 2


def _scan_segments(sr_ref, si_ref, pr_ref, pi_ref, lr, li, seg, reverse, per_tile=None):
    w = lr.shape[1]
    sign = -1.0 if reverse else 1.0
    lrb = jnp.broadcast_to(lr, (SUBLANES, w))
    lib = jnp.broadcast_to(sign * li, (SUBLANES, w))
    zero = jnp.zeros((SUBLANES, w), F32)

    def tile_rows(j):
        return pl.ds(pl.multiple_of(j * SUBLANES, SUBLANES), SUBLANES)

    def local(i, carry):
        rows = tile_rows(seg - 1 - i if reverse else i)
        pr, pi = _cmul(lrb, lib, carry[0], carry[1])
        xr, xi = sr_ref[rows, :] + pr, si_ref[rows, :] + pi
        sr_ref[rows, :] = xr
        si_ref[rows, :] = xi
        return xr, xi

    end_r, end_i = lax.fori_loop(0, seg, local, (zero, zero))
    full_r, full_i = pr_ref[seg - 1:seg, :], sign * pi_ref[seg - 1:seg, :]
    row = lax.broadcasted_iota(jnp.int32, (SUBLANES, w), 0)
    in_r, in_i = zero, zero
    cur_r, cur_i = jnp.zeros((1, w), F32), jnp.zeros((1, w), F32)
    for r in (range(SUBLANES - 2, -1, -1) if reverse else range(1, SUBLANES)):
        src = r + 1 if reverse else r - 1
        pr, pi = _cmul(full_r, full_i, cur_r, cur_i)
        cur_r, cur_i = end_r[src:src + 1, :] + pr, end_i[src:src + 1, :] + pi
        in_r = jnp.where(row == r, cur_r, in_r)
        in_i = jnp.where(row == r, cur_i, in_i)

    def carry_in(j, _):
        rows = tile_rows(j)
        k = seg - 1 - j if reverse else j
        pr, pi = _cmul(pr_ref[pl.ds(k, 1), :], sign * pi_ref[pl.ds(k, 1), :], in_r, in_i)
        xr, xi = sr_ref[rows, :] + pr, si_ref[rows, :] + pi
        sr_ref[rows, :] = xr
        si_ref[rows, :] = xi
        if per_tile is not None:
            per_tile(j, xr, xi)
        return 0

    lax.fori_loop(0, seg, carry_in, 0)


_S5_ROWS = 256


def _s5_in_specs(t, d_attn):
    u_block = (d_attn + 2 * D_KV) // SSM_CH_BLOCK
    blk3 = lambda shape: pl.BlockSpec((None,) + shape, lambda j: (j, 0, 0))
    return [
        pl.BlockSpec((t, SSM_CH_BLOCK), lambda j: (0, u_block + j)),
        blk3((SSM_CH_BLOCK, SSM_ST_BLOCK)), blk3((SSM_CH_BLOCK, SSM_ST_BLOCK)),
        blk3((1, SSM_ST_BLOCK)), blk3((1, SSM_ST_BLOCK)),
        blk3((SSM_ST_BLOCK, SSM_CH_BLOCK)), blk3((SSM_ST_BLOCK, SSM_CH_BLOCK)),
        pl.BlockSpec((1, SSM_CH_BLOCK), lambda j: (0, j)),
    ]


def _chunks(t):
    rows = min(_S5_ROWS, t)
    return rows, lambda i: pl.ds(pl.multiple_of(i * rows, rows), rows)


def _s5_states(u_ref, us_ref, bre_ref, bim_ref, lr_ref, li_ref, sr_ref, si_ref, pr_ref, pi_ref, t):
    seg = t // SUBLANES
    rows, chunk = _chunks(t)
    for c in range(t // rows):
        us_ref[c * rows:(c + 1) * rows, :] = _load_segmented(u_ref, c * rows // SUBLANES, rows // SUBLANES, seg)

    def fill(i, _):
        ub = us_ref[chunk(i), :].astype(BF16)
        sr_ref[chunk(i), :] = _dot(ub, bre_ref[...], 1, 0)
        si_ref[chunk(i), :] = _dot(ub, bim_ref[...], 1, 0)
        return 0

    lax.fori_loop(0, t // rows, fill, 0)
    _fill_powers(lr_ref[...], li_ref[...], pr_ref, pi_ref, seg)
    _scan_segments(sr_ref, si_ref, pr_ref, pi_ref, lr_ref[...], li_ref[...], seg, False)


def _s5_scratch(t):
    state = pltpu.VMEM((t, SSM_ST_BLOCK), F32)
    powers = pltpu.VMEM((t // SUBLANES, SSM_ST_BLOCK), F32)
    return state, powers, pltpu.VMEM((t, SSM_CH_BLOCK), F32)


def _s5_fwd(proj, mats, dskip_row, d_attn, d_ssm):
    t = proj.shape[0]
    seg = t // SUBLANES
    n_blocks = d_ssm // SSM_CH_BLOCK
    rows, chunk = _chunks(t)

    def body(u_ref, bre_ref, bim_ref, lr_ref, li_ref, cre_ref, cim_ref, d_ref, y_ref,
             sr_ref, si_ref, pr_ref, pi_ref, us_ref, ys_ref):
        _s5_states(u_ref, us_ref, bre_ref, bim_ref, lr_ref, li_ref, sr_ref, si_ref, pr_ref, pi_ref, t)

        def emit(i, _):
            ys_ref[chunk(i), :] = (_dot(sr_ref[chunk(i), :].astype(BF16), cre_ref[...], 1, 0)
                                   - _dot(si_ref[chunk(i), :].astype(BF16), cim_ref[...], 1, 0)
                                   + d_ref[...] * us_ref[chunk(i), :])
            return 0

        lax.fori_loop(0, t // rows, emit, 0)
        for c in range(t // rows):
            _store_segmented(y_ref, c * rows // SUBLANES, seg, ys_ref[c * rows:(c + 1) * rows, :])

    state, powers, channels = _s5_scratch(t)
    col = pl.BlockSpec((t, SSM_CH_BLOCK), lambda j: (0, j))
    return _call(body, name="s5_fwd", grid=(n_blocks,), in_specs=_s5_in_specs(t, d_attn), out_specs=col,
                 out_shape=_sds((t, d_ssm), F32), scratch_shapes=[state, state, powers, powers, channels, channels],
                 semantics=("parallel",))(proj, *mats, dskip_row)


def _s5_bwd(proj, mats, dskip_row, y, dz_a, dz_b, d_attn, d_ssm, after):
    t = proj.shape[0]
    seg = t // SUBLANES
    n_blocks = d_ssm // SSM_CH_BLOCK
    rows, chunk = _chunks(t)

    def body(u_ref, bre_ref, bim_ref, lr_ref, li_ref, cre_ref, cim_ref, d_ref, y_ref, dza_ref, dzb_ref,
             du_ref, dbre_ref, dbim_ref, dlr_ref, dli_ref, dcre_ref, dcim_ref, dd_ref,
             sr_ref, si_ref, gr_ref, gi_ref, pr_ref, pi_ref, us_ref, dys_ref, dus_ref, acc_r, acc_i):
        _s5_states(u_ref, us_ref, bre_ref, bim_ref, lr_ref, li_ref, sr_ref, si_ref, pr_ref, pi_ref, t)
        for ref in (dcre_ref, dcim_ref, dbre_ref, dbim_ref, dd_ref, acc_r, acc_i):
            ref[...] = jnp.zeros_like(ref)
        for c in range(t // rows):
            tile0, n_tiles = c * rows // SUBLANES, rows // SUBLANES
            dz = _load_segmented(dza_ref, tile0, n_tiles, seg) + _load_segmented(dzb_ref, tile0, n_tiles, seg)
            dys_ref[c * rows:(c + 1) * rows, :] = dz * _gelu_grad(_load_segmented(y_ref, tile0, n_tiles, seg))

        def through_c(i, _):
            dy = dys_ref[chunk(i), :]
            dd_ref[...] += jnp.sum(dy * us_ref[chunk(i), :], axis=0, keepdims=True)
            dyb = dy.astype(BF16)
            gr_ref[chunk(i), :] = _dot(dyb, cre_ref[...], 1, 1)
            gi_ref[chunk(i), :] = -_dot(dyb, cim_ref[...], 1, 1)
            dcre_ref[...] += _dot(sr_ref[chunk(i), :].astype(BF16), dyb, 0, 0)
            dcim_ref[...] -= _dot(si_ref[chunk(i), :].astype(BF16), dyb, 0, 0)
            return 0

        lax.fori_loop(0, t // rows, through_c, 0)

        row = lax.broadcasted_iota(jnp.int32, (SUBLANES, SSM_ST_BLOCK), 0)
        last = pl.ds((seg - 1) * SUBLANES, SUBLANES)
        wrap = [jnp.where(row == 0, 0.0, pltpu.roll(ref[last, :], 1, 0)) for ref in (sr_ref, si_ref)]

        def lambda_grad(j, g_re, g_im):
            before = pl.ds(pl.multiple_of(jnp.maximum(j - 1, 0) * SUBLANES, SUBLANES), SUBLANES)
            prev_r = jnp.where(j > 0, sr_ref[before, :], wrap[0])
            prev_i = jnp.where(j > 0, si_ref[before, :], wrap[1])
            acc_r[...] += g_re * prev_r + g_im * prev_i
            acc_i[...] += g_im * prev_r - g_re * prev_i

        _scan_segments(gr_ref, gi_ref, pr_ref, pi_ref, lr_ref[...], li_ref[...], seg, True, per_tile=lambda_grad)
        dlr_ref[...] = jnp.sum(acc_r[...], axis=0, keepdims=True)
        dli_ref[...] = jnp.sum(acc_i[...], axis=0, keepdims=True)

        def through_b(i, _):
            ub = us_ref[chunk(i), :].astype(BF16)
            grb, gib = gr_ref[chunk(i), :].astype(BF16), gi_ref[chunk(i), :].astype(BF16)
            dbre_ref[...] += _dot(ub, grb, 0, 0)
            dbim_ref[...] += _dot(ub, gib, 0, 0)
            dus_ref[chunk(i), :] = (_dot(grb, bre_ref[...], 1, 1) + _dot(gib, bim_ref[...], 1, 1)
                                    + d_ref[...] * dys_ref[chunk(i), :])
            return 0

        lax.fori_loop(0, t // rows, through_b, 0)
        for c in range(t // rows):
            _store_segmented(du_ref, c * rows // SUBLANES, seg, dus_ref[c * rows:(c + 1) * rows, :])

    col = pl.BlockSpec((t, SSM_CH_BLOCK), lambda j: (0, j))
    blk3 = lambda shape: pl.BlockSpec((None,) + shape, lambda j: (j, 0, 0))
    state, powers, channels = _s5_scratch(t)
    return _call(
        body, name="s5_bwd", grid=(n_blocks,), in_specs=_s5_in_specs(t, d_attn) + [col, col, col],
        out_specs=[col, blk3((SSM_CH_BLOCK, SSM_ST_BLOCK)), blk3((SSM_CH_BLOCK, SSM_ST_BLOCK)),
                   blk3((1, SSM_ST_BLOCK)), blk3((1, SSM_ST_BLOCK)),
                   blk3((SSM_ST_BLOCK, SSM_CH_BLOCK)), blk3((SSM_ST_BLOCK, SSM_CH_BLOCK)),
                   pl.BlockSpec((1, SSM_CH_BLOCK), lambda j: (0, j))],
        out_shape=[_sds((t, d_ssm), F32),
                   _sds((n_blocks, SSM_CH_BLOCK, SSM_ST_BLOCK), F32), _sds((n_blocks, SSM_CH_BLOCK, SSM_ST_BLOCK), F32),
                   _sds((n_blocks, 1, SSM_ST_BLOCK), F32), _sds((n_blocks, 1, SSM_ST_BLOCK), F32),
                   _sds((n_blocks, SSM_ST_BLOCK, SSM_CH_BLOCK), F32), _sds((n_blocks, SSM_ST_BLOCK, SSM_CH_BLOCK), F32),
                   _sds((1, d_ssm), F32)],
        scratch_shapes=[state, state, state, state, powers, powers, channels, channels, channels,
                        pltpu.VMEM((SUBLANES, SSM_ST_BLOCK), F32), pltpu.VMEM((SUBLANES, SSM_ST_BLOCK), F32)],
        semantics=("parallel",), n_after=len(after))(proj, *mats, dskip_row, y, dz_a, dz_b, *after)


def _by_block(gp_n):
    return gp_n.reshape(-1, GROUPS_PER_BLOCK, SSM_GROUP, SSM_STATE)


def _block_diag_in(bbar):
    eye = jnp.eye(GROUPS_PER_BLOCK, dtype=F32)
    return jnp.einsum("jgpn,gh->jgphn", _by_block(bbar), eye).reshape(-1, SSM_CH_BLOCK, SSM_ST_BLOCK)


def _block_diag_in_t(dense):
    d5 = dense.reshape(-1, GROUPS_PER_BLOCK, SSM_GROUP, GROUPS_PER_BLOCK, SSM_STATE)
    eye = jnp.eye(GROUPS_PER_BLOCK, dtype=F32)
    return jnp.einsum("jgphn,gh->jgpn", d5, eye).reshape(-1, SSM_STATE)


def _block_diag_out(c):
    eye = jnp.eye(GROUPS_PER_BLOCK, dtype=F32)
    return jnp.einsum("jgpn,gh->jgnhp", _by_block(c), eye).reshape(-1, SSM_ST_BLOCK, SSM_CH_BLOCK)


def _block_diag_out_t(dense):
    d5 = dense.reshape(-1, GROUPS_PER_BLOCK, SSM_STATE, GROUPS_PER_BLOCK, SSM_GROUP)
    eye = jnp.eye(GROUPS_PER_BLOCK, dtype=F32)
    return jnp.einsum("jgnhp,gh->jgpn", d5, eye).reshape(-1, SSM_STATE)


def _adamw(w, g, m, v):
    m = ADAM_B1 * m + (1.0 - ADAM_B1) * g
    v = ADAM_B2 * v + (1.0 - ADAM_B2) * (g * g)
    m_hat = m / (1.0 - ADAM_B1 ** ADAM_STEP)
    v_hat = v / (1.0 - ADAM_B2 ** ADAM_STEP)
    delta = -ADAM_LR * (m_hat / (jnp.sqrt(v_hat) + ADAM_EPS) + ADAM_WD * w)
    return delta, m, v


def _adam_sharded(name, parts, w, m, v, tr, row0=0):
    r, c = w.shape
    assert r % tr == 0 and row0 % tr == 0, (name, r, tr, row0)

    def body(p_ref, w_ref, m_ref, v_ref, g_out, d_out, m_out, v_out):
        g = p_ref[0].astype(F32)
        for i in range(1, p_ref.shape[0]):
            g = g + p_ref[i].astype(F32)
        delta, m_new, v_new = _adamw(w_ref[...], g, m_ref[...], v_ref[...])
        g_out[...] = g
        d_out[...] = delta
        m_out[...] = m_new
        v_out[...] = v_new

    tile = pl.BlockSpec((tr, c), lambda i: (i, 0))
    return _call(body, name=name, grid=(r // tr,),
                 in_specs=[pl.BlockSpec((parts.shape[0], tr, c), lambda i: (0, i + row0 // tr, 0)), tile, tile, tile],
                 out_specs=[tile] * 4, out_shape=[_sds((r, c), F32)] * 4, semantics=("parallel",))(parts, w, m, v)


_BIG = ("w_in", "w_glu", "w_o", "w_gate", "w_up", "w_down")
_BY_COLUMNS = ("w_in", "w_gate", "w_up")
_SMALL_VECTORS = ("sinks", "log_dt", "b_glu", "g_attn_out", "g_ssm_out", "g_post_mix", "g_pre_ffn", "g_post_ffn")
_SMALL_MATRICES = ("b_re", "b_im", "c_re", "c_im", "a_re", "a_im")
_ORDER = ("g_pre_mix", "w_in", "sinks", "a_re", "a_im", "log_dt", "b_re", "b_im", "c_re", "c_im", "d_skip", "w_glu",
          "b_glu", "g_attn_out", "g_ssm_out", "w_o", "g_post_mix", "g_pre_ffn", "w_gate", "w_up", "w_down",
          "g_post_ffn")


def _pack_grads(vectors, matrices):
    width = max(a.shape[1] for a in vectors)
    slots, row, lane = [], 0, 0
    for a in vectors:
        span = -(-a.shape[1] // LANES) * LANES
        if lane + span > width:
            row, lane = row + 1, 0
        slots.append((row, lane, a.shape[1]))
        lane += span
    firsts, at = [], 0
    for a in matrices:
        firsts.append(at)
        at += a.shape[0]
    nv = len(vectors)

    def body(*refs):
        vec_out, mat_out = refs[-2], refs[-1]
        vec_out[...] = jnp.zeros_like(vec_out)
        for ref, (r, l, w) in zip(refs[:nv], slots):
            vec_out[r:r + 1, l:l + w] = ref[...]
        for ref, r0 in zip(refs[nv:-2], firsts):
            mat_out[r0:r0 + ref.shape[0], :] = ref[...]

    ins = list(vectors) + list(matrices)
    outs = [_sds((-(-(row + 1) // SUBLANES) * SUBLANES, width), F32), _sds((at, matrices[0].shape[1]), F32)]
    vec_pack, mat_pack = _call(body, name="pack_small_grads", in_specs=_whole(ins), out_specs=_whole(outs),
                               out_shape=outs)(*ins)
    return vec_pack, slots, mat_pack, firsts


def _adam_replicated(sources, found_at, w, m, v):
    ns, n = len(sources), len(w)

    def body(*refs):
        ins, outs = refs[ns:ns + 3 * n], refs[ns + 3 * n:]
        summed = []
        for p_ref in refs[:ns]:
            g = p_ref[0]
            for k in range(1, N_DEV):
                g = g + p_ref[k]
            summed.append(g)
        for i, (src, row, lane) in enumerate(found_at):
            w_ref, m_ref, v_ref = ins[i], ins[n + i], ins[2 * n + i]
            rows, cols = w_ref.shape
            g = summed[src][row:row + rows, lane:lane + cols]
            delta, m_new, v_new = _adamw(w_ref[...], g, m_ref[...], v_ref[...])
            for o, val in zip(outs[4 * i:4 * i + 4], (g, delta, m_new, v_new)):
                o[...] = val

    ins = list(sources) + list(w) + list(m) + list(v)
    outs = [_sds(a.shape, F32) for a in w for _ in range(4)]
    flat = _call(body, name="adam_replicated", in_specs=_whole(ins), out_specs=_whole(outs), out_shape=outs)(*ins)
    return [tuple(flat[4 * i:4 * i + 4]) for i in range(n)]


def kernel(x, positions, g_pre_mix, w_in, sinks, a_re, a_im, log_dt, b_re, b_im, c_re, c_im, d_skip, w_glu, b_glu, g_attn_out, g_ssm_out, w_o, g_post_mix, g_pre_ffn, w_gate, w_up, w_down, g_post_ffn, loss_target, m_g_pre_mix, m_w_in, m_sinks, m_a_re, m_a_im, m_log_dt, m_b_re, m_b_im, m_c_re, m_c_im, m_d_skip, m_w_glu, m_b_glu, m_g_attn_out, m_g_ssm_out, m_w_o, m_g_post_mix, m_g_pre_ffn, m_w_gate, m_w_up, m_w_down, m_g_post_ffn, v_g_pre_mix, v_w_in, v_sinks, v_a_re, v_a_im, v_log_dt, v_b_re, v_b_im, v_c_re, v_c_im, v_d_skip, v_w_glu, v_b_glu, v_g_attn_out, v_g_ssm_out, v_w_o, v_g_post_mix, v_g_pre_ffn, v_w_gate, v_w_up, v_w_down, v_g_post_ffn):
    given = dict(locals())
    weights = {n: given[n] for n in _ORDER}
    mom_m = {n: given["m_" + n] for n in _ORDER}
    mom_v = {n: given["v_" + n] for n in _ORDER}

    t, d = x.shape[1], x.shape[2]
    d_attn = d // 2
    d_ssm = d - d_attn
    d_in = d_attn + 2 * D_KV + d_ssm
    n_groups = d_ssm // SSM_GROUP
    n_heads = d_attn // HEAD_DIM
    tm = min(256, t)

    x2 = x[0]
    target = loss_target[0]

    def by_rows(n, a):
        return a[0].T if n in _BY_COLUMNS else a[0]

    def start_gather(name, ns, token):
        behind = 0 if token is None else token[0, 0].astype(BF16)
        shards = [by_rows(n, weights[n]).astype(BF16) + behind for n in ns]
        return _exchange_start(name, shards, False, (OWN, SIBLING) + CHIP_PEERS)

    def forward_gather(handle, after):
        return _forward_start(handle["name"] + "_forward", _exchange_wait(handle, after))

    def finish_gather(handle, after):
        return _split_wait(forward_gather(handle, after)[0], [])

    ag_in, token = start_gather("gather_w_in", ["w_in"], None)
    ag_mix, token = start_gather("gather_w_glu_o", ["w_glu", "w_o"], token)
    ag_ffn_in, token = start_gather("gather_w_gate_up", ["w_gate", "w_up"], token)
    ag_down, token = start_gather("gather_w_down", ["w_down"], token)

    xn, = _rows("norm_in", lambda xv, g: ([_rms(xv)[0] * g], []), [x2], [g_pre_mix], [(d, BF16)], [], tm,
                after=[token])
    win_g, = finish_gather(ag_in, [xn])
    w_in_t = win_g.reshape(d_in, d)
    proj = _mm_nt("proj_in", xn, w_in_t, F32)

    cos, sin = _rope_tables(positions.reshape(t, 1).astype(F32))
    sinks_row = jnp.pad(sinks, ((0, 0), (0, LANES - n_heads)))
    attn = _attention_fwd(proj, cos, sin, sinks_row, d_attn)

    def view(n, a):
        if n in ("b_re", "b_im"):
            return jnp.transpose(a[0], (0, 2, 1)).reshape(-1, SSM_STATE)
        if n in ("c_re", "c_im"):
            return a[0].reshape(-1, SSM_STATE)
        return a[0].T if n == "d_skip" else a[0] if a.ndim == 3 else a

    def unview(n, val):
        if n in ("b_re", "b_im"):
            return jnp.transpose(val.reshape(n_groups, SSM_GROUP, SSM_STATE), (0, 2, 1))[None]
        if n in ("c_re", "c_im"):
            return val.reshape(1, n_groups, SSM_GROUP, SSM_STATE)
        return val.T[None] if n == "d_skip" else val[None] if weights[n].ndim == 3 else val

    b_re_v, b_im_v = view("b_re", b_re), view("b_im", b_im)
    ldt_col = log_dt.reshape(n_groups, 1)
    lam_re, lam_im, bbar_re, bbar_im = _s5_discretise(a_re[0], a_im[0], ldt_col, b_re_v, b_im_v)
    n_blocks = n_groups // GROUPS_PER_BLOCK
    mats = [_block_diag_in(bbar_re).astype(BF16), _block_diag_in(bbar_im).astype(BF16),
            lam_re.reshape(n_blocks, 1, SSM_ST_BLOCK), lam_im.reshape(n_blocks, 1, SSM_ST_BLOCK),
            _block_diag_out(view("c_re", c_re)).astype(BF16), _block_diag_out(view("c_im", c_im)).astype(BF16)]
    dskip_row = d_skip.reshape(1, d_ssm)
    forward_mix, _ = forward_gather(ag_mix, [attn])
    y_ssm = _s5_fwd(proj, mats, dskip_row, d_attn, d_ssm)
    gelu_bf16 = lambda yv: _gelu(yv).astype(BF16)
    wglu_g, wo_g = _split_wait(forward_mix, [y_ssm])
    w_glu_full = wglu_g.reshape(d_ssm, d_ssm)
    w_o_full = wo_g.reshape(d, d)
    glu_lin = _mm_nn("glu_gate", y_ssm, w_glu_full, F32, a_fn=gelu_bf16)

    def mix_prep(av, yv, gl, bg, ga, gs):
        ssm = _gelu(yv) * _sigmoid(gl + bg)
        return [jnp.concatenate([_rms(av)[0] * ga, _rms(ssm)[0] * gs], axis=1)], []

    mixed, = _rows("mix_prep", mix_prep, [attn, y_ssm, glu_lin], [b_glu, g_attn_out, g_ssm_out], [(d, BF16)], [], tm)
    mix = _mm_nn("mix_out", mixed, w_o_full, F32)

    def post_mix(xv, mv, gpm, gpf):
        h = xv + _rms(mv)[0] * gpm
        return [h, _rms(h)[0] * gpf], []

    forward_ffn_in, token = forward_gather(ag_ffn_in, [mix])
    h, hn = _rows("post_mix", post_mix, [x2, mix], [g_post_mix, g_pre_ffn], [(d, F32), (d, BF16)], [], tm,
                  after=[token])
    wgate_g, wup_g = _split_wait(forward_ffn_in, [hn])
    gate, up, hid = _ffn_in(hn, wgate_g, wup_g)
    wdown_g, = finish_gather(ag_down, [hid])
    ff = _mm_contract_slots("ffn_down", [(hid, wdown_g)], F32, per_step=2, tm=1024)

    def head(hv, fv, tv, gpo):
        out = hv + _rms(fv)[0] * gpo
        err = out - tv
        dout = err * (1.0 / d)
        dff, dg = _rms_bwd(fv, gpo, dout)
        loss = jnp.zeros((1, LANES), F32) + 0.5 * jnp.sum(err * err) * (1.0 / d)
        return [dff, dout], [dg, loss]

    dff, dh_out, dg_post_ffn, loss_row = _rows("loss_head", head, [h, ff, target], [g_post_ffn],
                                               [(d, BF16), (d, F32)], [d, LANES], tm)

    def swap_halves(name, grads):
        return _halves_start("swap_" + name, [g.reshape(N_DEV // 2, 2, *g.shape[1:]) for g in grads])

    def scatter_chip_sums(name, swap, after):
        both = _split_wait(swap, after)
        half = len(both) // 2
        sums = [_chip_sum("chip_sum_%s_%d" % (name, i), both[i], both[half + i]) for i in range(half)]
        return _exchange_start("scatter_" + name, sums, True, (OWN,) + CHIP_PEERS, by_chip=True)

    dw_down = _mm_slots_tn("ffn_down_dw", hid, dff, BF16)
    swap_down, token = swap_halves("dw_down", [dw_down])
    dgate, dup = _ffn_down_bwd(dff, wdown_g, gate, up, [token])
    rs_down, token = scatter_chip_sums("dw_down", swap_down, [dgate])
    dhn = _mm_contract_slots("ffn_in_dx", [(dgate, wgate_g), (dup, wup_g)], F32, per_step=2, tm=1024, tn=1024,
                             after=[token])
    dw_gate = _mm_slots_tn("ffn_gate_dw", dgate, hn, BF16)
    dw_up = _mm_slots_tn("ffn_up_dw", dup, hn, BF16)
    swap_ffn_in, tok_ffn_in = swap_halves("dw_gate_up", [dw_gate, dw_up])

    def mid_bwd(dho, dhn_, hv, mv, gpf, gpm):
        d1, dgpf = _rms_bwd(hv, gpf, dhn_)
        dh_ = dho + d1
        dmix_, dgpm = _rms_bwd(mv, gpm, dh_)
        return [dh_, dmix_], [dgpf, dgpm]

    dh, dmix, dg_pre_ffn, dg_post_mix = _rows("mid_bwd", mid_bwd, [dh_out, dhn, h, mix], [g_pre_ffn, g_post_mix],
                                              [(d, F32), (d, BF16)], [d, d], tm, after=[tok_ffn_in])

    dmixed = _mm_nt("mix_out_dx", dmix, w_o_full, F32)
    rs_ffn_in, token = scatter_chip_sums("dw_gate_up", swap_ffn_in, [dmixed])
    dw_o = _mm_tn("mix_out_dw", mixed, dmix, BF16, after=[token])
    swap_o, tok_o = swap_halves("dw_o", [dw_o.reshape(N_DEV, d // N_DEV, d)])

    def mix_bwd(dm, av, yv, gl, bg, ga, gs):
        dattn_, dga = _rms_bwd(av, ga, dm[:, :d_attn])
        z = _gelu(yv)
        sg = _sigmoid(gl + bg)
        dssm, dgs = _rms_bwd(z * sg, gs, dm[:, d_attn:])
        dgl = dssm * z * sg * (1.0 - sg)
        return [dattn_, dssm * sg, dgl], [dga, dgs, jnp.sum(dgl, axis=0, keepdims=True)]

    dattn, dz_direct, dglu, dg_attn_out, dg_ssm_out, db_glu = _rows(
        "mix_bwd", mix_bwd, [dmixed, attn, y_ssm, glu_lin], [b_glu, g_attn_out, g_ssm_out],
        [(d_attn, F32), (d_ssm, F32), (d_ssm, BF16)], [d_attn, d_ssm, d_ssm], tm, after=[tok_o])
    dz_glu = _mm_nt("glu_gate_dx", dglu, w_glu_full, F32)
    dw_glu = _mm_tn("glu_gate_dw", y_ssm, dglu, BF16, a_fn=gelu_bf16)
    rs_o, token = scatter_chip_sums("dw_o", swap_o, [dz_glu, dw_glu])

    du, db_re_dense, db_im_dense, dlam_re, dlam_im, dc_re_dense, dc_im_dense, dd_skip = _s5_bwd(
        proj, mats, dskip_row, y_ssm, dz_direct, dz_glu, d_attn, d_ssm, [token])
    da_re, da_im, dlog_dt, db_re_v, db_im_v = _s5_discretise_bwd(
        a_re[0], a_im[0], ldt_col, b_re_v, b_im_v, dlam_re.reshape(n_groups, SSM_STATE),
        dlam_im.reshape(n_groups, SSM_STATE), _block_diag_in_t(db_re_dense), _block_diag_in_t(db_im_dense))
    dq, dk2, dv2, dsinks_row = _attention_bwd(proj, cos, sin, sinks_row, dattn, d_attn)

    small_grads = {
        "sinks": dsinks_row, "a_re": da_re, "a_im": da_im, "log_dt": dlog_dt.reshape(1, n_groups),
        "b_re": db_re_v, "b_im": db_im_v, "c_re": _block_diag_out_t(dc_re_dense),
        "c_im": _block_diag_out_t(dc_im_dense), "d_skip": dd_skip.reshape(n_groups, SSM_GROUP).T, "b_glu": db_glu,
        "g_attn_out": dg_attn_out, "g_ssm_out": dg_ssm_out, "g_post_mix": dg_post_mix, "g_pre_ffn": dg_pre_ffn,
        "g_post_ffn": dg_post_ffn,
    }
    vec_pack, vec_slots, mat_pack, mat_rows = _pack_grads([small_grads[n] for n in _SMALL_VECTORS],
                                                          [small_grads[n] for n in _SMALL_MATRICES])
    ag_small, token = _exchange_start("gather_small_grads", [vec_pack, mat_pack, small_grads["d_skip"]], False,
                                      (OWN,) + ALL_PEERS)
    dproj = _assemble_dproj(dq, dk2, dv2, du, d_in, [token])

    dxn = _mm_nn("proj_in_dx", dproj, w_in_t, F32)
    dw_in = _mm_tn("proj_in_dw", dproj, xn, BF16).reshape(N_DEV, d_in // N_DEV, d)
    swap_in, token = swap_halves("dw_in_glu", [dw_in, dw_glu.reshape(N_DEV, d_ssm // N_DEV, d_ssm)])

    def x_bwd(dh_, dxn_, xv, g):
        dx, dg = _rms_bwd(xv, g, dxn_)
        return [dh_ + dx], [dg]

    grad_x, dg_pre_mix = _rows("norm_in_bwd", x_bwd, [dh, dxn, x2], [g_pre_mix], [(d, F32)], [d], tm, after=[token])
    ag_last, token = _exchange_start("gather_g_pre_mix_grad", [dg_pre_mix], False, (OWN,) + ALL_PEERS)
    rs_in, token = scatter_chip_sums("dw_in_glu", swap_in, [grad_x, token])

    results = {}

    def adam_big(n, parts):
        r = parts.shape[1]
        results[n] = _adam_sharded("adam_" + n, parts, by_rows(n, weights[n]), by_rows(n, mom_m[n]),
                                   by_rows(n, mom_v[n]), 64 if r % 64 == 0 else r)
        return results[n][3]

    done = [grad_x, token]
    adam_big("w_down", _exchange_wait(rs_down, done)[0])
    p_gate, p_up = _exchange_wait(rs_ffn_in, done)
    done = [adam_big("w_gate", p_gate), adam_big("w_up", p_up), results["w_down"][3]]
    done = [adam_big("w_o", _exchange_wait(rs_o, done)[0])]
    vec_parts, mat_parts, dskip_parts = _exchange_wait(ag_small, done)
    first_gain_parts, = _exchange_wait(ag_last, done)
    for n, row0 in zip(_SMALL_MATRICES, mat_rows):
        rows = view(n, weights[n]).shape[0]
        results[n] = _adam_sharded("adam_" + n, mat_parts, view(n, weights[n]), view(n, mom_m[n]), view(n, mom_v[n]),
                                   rows, row0)
    rest = _SMALL_VECTORS + ("d_skip", "g_pre_mix")
    found_at = [(0, row, lane) for row, lane, _ in vec_slots] + [(1, 0, 0), (2, 0, 0)]
    updated = _adam_replicated([vec_parts, dskip_parts, first_gain_parts], found_at,
                               [view(n, weights[n]) for n in rest], [view(n, mom_m[n]) for n in rest],
                               [view(n, mom_v[n]) for n in rest])
    results.update(zip(rest, updated))
    p_in, p_glu = _exchange_wait(rs_in, [results[n][3] for n in _SMALL_MATRICES] + [updated[0][3]])
    adam_big("w_in", p_in)
    adam_big("w_glu", p_glu)

    loss = lax.psum(loss_row[0, 0], ("x", "y", "c"))
    outs = [loss, grad_x[None]]
    for k in range(4):
        for n in _ORDER:
            val = results[n][k]
            outs.append(val.T[None] if n in _BY_COLUMNS else val[None] if n in _BIG else unview(n, val))
    return tuple(outs)
```

```python
import math

import jax
import jax.numpy as jnp
from jax import lax
from jax.experimental import pallas as pl
from jax.experimental.pallas import tpu as pltpu

F32 = jnp.float32
BF16 = jnp.bfloat16

HEAD_DIM = 64
N_KV_HEADS = 4
D_KV = N_KV_HEADS * HEAD_DIM
WINDOW = 128
BLOCK = 128
ROPE_THETA = 10000.0
SSM_GROUP = 16
SSM_STATE = 64
GROUPS_PER_BLOCK = 8
SSM_CH_BLOCK = GROUPS_PER_BLOCK * SSM_GROUP
SSM_ST_BLOCK = GROUPS_PER_BLOCK * SSM_STATE
RMS_EPS = 1e-6
N_DEV = 8
LANES = 128
SUBLANES = 8
MASKED = -1e30

ADAM_LR = 0.001
ADAM_B1 = 0.9
ADAM_B2 = 0.999
ADAM_EPS = 1e-08
ADAM_WD = 0.01
ADAM_STEP = 10

VMEM_LIMIT_BYTES = 56 * 1024 * 1024


def _call(body, *, name, out_shape, in_specs, out_specs, grid=(), scratch_shapes=(), semantics=None, n_after=0):
    params = dict(vmem_limit_bytes=VMEM_LIMIT_BYTES)
    if semantics is not None:
        params["dimension_semantics"] = semantics
    n_in = len(in_specs)
    if n_after:
        inner = body

        def body(*refs):
            inner(*refs[:n_in], *refs[n_in + n_after:])

        in_specs = list(in_specs) + [pl.BlockSpec(memory_space=pl.ANY)] * n_after
    return pl.pallas_call(body, name=name, grid=grid, in_specs=in_specs, out_specs=out_specs, out_shape=out_shape,
                          scratch_shapes=scratch_shapes, compiler_params=pltpu.CompilerParams(**params))


def _sds(shape, dtype):
    return jax.ShapeDtypeStruct(tuple(shape), dtype)


def _dot(a, b, ca, cb):
    return lax.dot_general(a, b, (((ca,), (cb,)), ((), ())), preferred_element_type=F32)


def _rms(x):
    r = lax.rsqrt(jnp.mean(x * x, axis=-1, keepdims=True) + RMS_EPS)
    return x * r, r


def _rms_bwd(x, g, dy):
    xh, r = _rms(x)
    dxh = dy * g
    dx = r * (dxh - xh * jnp.mean(dxh * xh, axis=-1, keepdims=True))
    return dx, jnp.sum(dy * xh, axis=0, keepdims=True)


def _sigmoid(x):
    return 1.0 / (1.0 + jnp.exp(-x))


_GELU_C = math.sqrt(2.0 / math.pi)
_GELU_A = 0.044715


def _gelu(y):
    t = jnp.tanh(_GELU_C * (y + _GELU_A * y * y * y))
    return 0.5 * y * (1.0 + t)


def _gelu_grad(y):
    t = jnp.tanh(_GELU_C * (y + _GELU_A * y * y * y))
    return 0.5 * (1.0 + t) + 0.5 * y * (1.0 - t * t) * _GELU_C * (1.0 + 3.0 * _GELU_A * y * y)


def _rows(name, fn, row_ins, vec_ins, row_outs, acc_widths, tm, after=()):
    rows = row_ins[0].shape[0]
    assert rows % tm == 0, (name, rows, tm)
    n_row, n_vec, n_out, n_acc = len(row_ins), len(vec_ins), len(row_outs), len(acc_widths)

    def body(*refs):
        ins = [r[...] for r in refs[:n_row + n_vec]]
        outs = refs[n_row + n_vec:n_row + n_vec + n_out]
        accs = refs[n_row + n_vec + n_out:]
        row_vals, acc_vals = fn(*ins)
        for o, v in zip(outs, row_vals):
            o[...] = v.astype(o.dtype)
        if n_acc:
            @pl.when(pl.program_id(0) == 0)
            def _():
                for a in accs:
                    a[...] = jnp.zeros_like(a)
            for a, v in zip(accs, acc_vals):
                a[...] += v

    in_specs = [pl.BlockSpec((tm, a.shape[1]), lambda i: (i, 0)) for a in row_ins]
    in_specs += [pl.BlockSpec(v.shape, lambda i: (0, 0)) for v in vec_ins]
    out_specs = [pl.BlockSpec((tm, w), lambda i: (i, 0)) for w, _ in row_outs]
    out_specs += [pl.BlockSpec((1, w), lambda i: (0, 0)) for w in acc_widths]
    out_shape = [_sds((rows, w), dt) for w, dt in row_outs] + [_sds((1, w), F32) for w in acc_widths]
    return _call(body, name=name, grid=(rows // tm,), in_specs=in_specs, out_specs=out_specs, out_shape=out_shape,
                 semantics=("arbitrary",) if n_acc else ("parallel",), n_after=len(after))(*row_ins, *vec_ins, *after)


def _matmul(name, operands, in_specs, product, grid, out_shape, out_spec, acc_shape, after=()):
    nk = grid[-1]
    n_in = len(operands)
    in_place = out_shape.dtype == F32

    def body(*refs):
        ins = [r[...] for r in refs[:n_in]]
        o_ref = refs[n_in]
        if nk == 1:
            o_ref[...] = product(*ins).astype(o_ref.dtype)
            return
        acc = o_ref if in_place else refs[n_in + 1]
        k = pl.program_id(len(grid) - 1)

        @pl.when(k == 0)
        def _():
            acc[...] = jnp.zeros_like(acc)

        acc[...] += product(*ins)

        if not in_place:
            @pl.when(k == nk - 1)
            def _():
                o_ref[...] = acc[...].astype(o_ref.dtype)

    return _call(body, name=name, grid=grid, in_specs=in_specs, out_specs=out_spec, out_shape=out_shape,
                 scratch_shapes=[] if nk == 1 or in_place else [pltpu.VMEM(acc_shape, F32)],
                 semantics=("parallel",) * (len(grid) - 1) + ("arbitrary",), n_after=len(after))(*operands, *after)


def _mm_nn(name, a, b, out_dtype, tm=512, tn=None, a_fn=lambda x: x):
    m, k = a.shape
    n = b.shape[1]
    tm, tn = min(tm, m), n if tn is None else tn
    return _matmul(name, [a, b],
                   [pl.BlockSpec((tm, k), lambda i, j, s: (i, 0)), pl.BlockSpec((k, tn), lambda i, j, s: (0, j))],
                   lambda x, y: _dot(a_fn(x), y, 1, 0), (m // tm, n // tn, 1), _sds((m, n), out_dtype),
                   pl.BlockSpec((tm, tn), lambda i, j, s: (i, j)), (tm, tn))


def _mm_nt(name, a, b, out_dtype, tm=512, tn=None):
    m, k = a.shape
    n = b.shape[0]
    tm, tn = min(tm, m), n if tn is None else tn
    return _matmul(name, [a, b],
                   [pl.BlockSpec((tm, k), lambda i, j, s: (i, 0)), pl.BlockSpec((tn, k), lambda i, j, s: (j, 0))],
                   lambda x, y: _dot(x, y, 1, 1), (m // tm, n // tn, 1), _sds((m, n), out_dtype),
                   pl.BlockSpec((tm, tn), lambda i, j, s: (i, j)), (tm, tn))


def _mm_tn(name, a, b, out_dtype, tm=512, tn=None, tk=2048, a_fn=lambda x: x, after=()):
    k, m = a.shape
    n = b.shape[1]
    tm, tk, tn = min(tm, m), min(tk, k), n if tn is None else tn
    return _matmul(name, [a, b],
                   [pl.BlockSpec((tk, tm), lambda i, j, s: (s, i)), pl.BlockSpec((tk, tn), lambda i, j, s: (s, j))],
                   lambda x, y: _dot(a_fn(x), y, 0, 0), (m // tm, n // tn, k // tk), _sds((m, n), out_dtype),
                   pl.BlockSpec((tm, tn), lambda i, j, s: (i, j)), (tm, tn), after)


def _mm_contract_slots(name, pairs, out_dtype, per_step, tm=512, tn=2048, after=()):
    s_, m, k = pairs[0][0].shape
    n = pairs[0][1].shape[2]
    tm, tn = min(tm, m), min(tn, n)
    ops, specs = [], []
    for a, b in pairs:
        ops += [a, b]
        specs += [pl.BlockSpec((per_step, tm, k), lambda i, j, s: (s, i, 0)),
                  pl.BlockSpec((per_step, k, tn), lambda i, j, s: (s, 0, j))]

    def product(*t):
        return sum(_dot(t[2 * p][q], t[2 * p + 1][q], 1, 0) for p in range(len(pairs)) for q in range(per_step))

    return _matmul(name, ops, specs, product, (m // tm, n // tn, s_ // per_step), _sds((m, n), out_dtype),
                   pl.BlockSpec((tm, tn), lambda i, j, s: (i, j)), (tm, tn), after)


def _mm_slots_tn(name, a, b, out_dtype, tn=2048, tk=2048):
    s_, k, m = a.shape
    n = b.shape[1]
    tn, tk = min(tn, n), min(tk, k)
    return _matmul(name, [a, b],
                   [pl.BlockSpec((None, tk, m), lambda s, j, z: (s, z, 0)), pl.BlockSpec((tk, tn), lambda s, j, z: (z, j))],
                   lambda x, y: _dot(x, y, 0, 0), (s_, n // tn, k // tk), _sds((s_, m, n), out_dtype),
                   pl.BlockSpec((None, m, tn), lambda s, j, z: (s, 0, j)), (m, tn))


def _ffn_in(a, w_gate, w_up, tm=512):
    m, k = a.shape
    s_, n, _ = w_gate.shape
    tm = min(tm, m)

    def body(a_ref, wg_ref, wu_ref, g_ref, u_ref, h_ref):
        x = a_ref[...]
        g = _dot(x, wg_ref[...], 1, 1)
        u = _dot(x, wu_ref[...], 1, 1)
        g_ref[...] = g.astype(BF16)
        u_ref[...] = u.astype(BF16)
        h_ref[...] = (g * _sigmoid(g) * u).astype(BF16)

    w_spec = pl.BlockSpec((None, n, k), lambda s, i: (s, 0, 0))
    o_spec = pl.BlockSpec((None, tm, n), lambda s, i: (s, i, 0))
    return _call(body, name="ffn_in", grid=(s_, m // tm),
                 in_specs=[pl.BlockSpec((tm, k), lambda s, i: (i, 0)), w_spec, w_spec], out_specs=[o_spec] * 3,
                 out_shape=[_sds((s_, m, n), BF16)] * 3, semantics=("parallel", "parallel"))(a, w_gate, w_up)


def _ffn_down_bwd(d_out, w_down, gate, up, after, tm=512):
    m, k = d_out.shape
    s_, n, _ = w_down.shape
    tm = min(tm, m)

    def body(d_ref, w_ref, g_ref, u_ref, dg_ref, du_ref):
        rows = pl.ds(pl.multiple_of(pl.program_id(1) * tm, tm), tm)
        dh = _dot(d_ref[rows, :], w_ref[...], 1, 1)
        g = g_ref[...].astype(F32)
        sg = _sigmoid(g)
        dg_ref[...] = (dh * u_ref[...].astype(F32) * sg * (1.0 + g * (1.0 - sg))).astype(BF16)
        du_ref[...] = (dh * g * sg).astype(BF16)

    t_spec = pl.BlockSpec((None, tm, n), lambda s, i: (s, i, 0))
    return _call(body, name="ffn_down_dx", grid=(s_, m // tm),
                 in_specs=[pl.BlockSpec((m, k), lambda s, i: (0, 0)), pl.BlockSpec((None, n, k), lambda s, i: (s, 0, 0)),
                           t_spec, t_spec],
                 out_specs=[t_spec] * 2, out_shape=[_sds((s_, m, n), BF16)] * 2, semantics=("parallel", "parallel"),
                 n_after=len(after))(d_out, w_down, gate, up, *after)


ALL_PEERS = (1, 2, 3, 4, 5, 6, 7)
CHIP_PEERS = (2, 4, 6)
SIBLING = 1
OWN = 0


def _peer(relation):
    x, y, c = lax.axis_index("x"), lax.axis_index("y"), lax.axis_index("c")
    pos = (1 - x if relation & 4 else x, 1 - y if relation & 2 else y, 1 - c if relation & 1 else c)
    return pos, 4 * pos[0] + 2 * pos[1] + pos[2]


def _slot(relation, by_chip):
    pos, device = _peer(relation)
    return 2 * pos[0] + pos[1] if by_chip else device


def _exchange_copies(ins, lands, send_sems, recv_sems, scatter, relations, by_chip=False):
    me = _slot(0, by_chip)

    def copy(a, s, peer, pos, dst_slot):
        return pltpu.make_async_remote_copy(
            src_ref=ins[a].at[peer] if scatter else ins[a], dst_ref=lands[a].at[dst_slot],
            send_sem=send_sems.at[s], recv_sem=recv_sems.at[s], device_id=pos, device_id_type=pl.DeviceIdType.MESH)

    pairs = []
    for k, r in enumerate(relations):
        pos, peer = _peer(r)[0], _slot(r, by_chip)
        for a in range(len(ins)):
            s = a * len(relations) + k
            pairs.append((copy(a, s, peer, pos, me), copy(a, s, peer, pos, peer)))
    return pairs


def _halves_copies(arrays, lands, send_sems, recv_sems):
    sibling, _ = _peer(SIBLING)
    core = lax.axis_index("c")
    pairs = []
    for a, (ref, land) in enumerate(zip(arrays, lands)):
        send = pltpu.make_async_remote_copy(
            src_ref=ref.at[:, pl.ds(1 - core, 1)], dst_ref=land, send_sem=send_sems.at[a], recv_sem=recv_sems.at[a],
            device_id=sibling, device_id_type=pl.DeviceIdType.MESH)
        pairs.append((send, send))
    return pairs


def _forward_copies(lands, send_sems, recv_sems):
    sibling, _ = _peer(SIBLING)

    def copy(a, s, slot):
        return pltpu.make_async_remote_copy(
            src_ref=lands[a].at[slot], dst_ref=lands[a].at[slot], send_sem=send_sems.at[s], recv_sem=recv_sems.at[s],
            device_id=sibling, device_id_type=pl.DeviceIdType.MESH)

    pairs = []
    for k, r in enumerate(CHIP_PEERS):
        _, mine = _peer(r)
        _, theirs = _peer(r | SIBLING)
        for a in range(len(lands)):
            s = a * len(CHIP_PEERS) + k
            pairs.append((copy(a, s, mine), copy(a, s, theirs)))
    return pairs


_HBM_SPEC = pl.BlockSpec(memory_space=pltpu.HBM)
_SEM_SPEC = pl.BlockSpec(memory_space=pltpu.SEMAPHORE)
_SIDE_EFFECT = pltpu.SideEffectType.DATAFLOW_SIDE_EFFECTING


def _split_start(name, operands, n_sem, make_pairs):
    k = len(operands)

    def body(*refs):
        send_sems, recv_sems, token = refs[k], refs[k + 1], refs[-1]
        for send, _ in make_pairs(refs[:k], send_sems, recv_sems):
            send.start()
        token[...] = jnp.zeros_like(token)

    out = pl.pallas_call(
        body, name=name,
        out_shape=(pltpu.SemaphoreType.DMA((n_sem,)), pltpu.SemaphoreType.DMA((n_sem,)),
                   *[pltpu.HBM(a.shape, a.dtype) for a in operands], _sds((SUBLANES, LANES), F32)),
        in_specs=[_HBM_SPEC] * k,
        out_specs=(_SEM_SPEC, _SEM_SPEC, *[_HBM_SPEC] * k, pl.BlockSpec(memory_space=pltpu.VMEM)),
        input_output_aliases={i: 2 + i for i in range(k)},
        compiler_params=pltpu.CompilerParams(has_side_effects=_SIDE_EFFECT),
    )(*[pltpu.with_memory_space_constraint(a, pltpu.HBM) for a in operands])
    return dict(name=name, sems=out[:2], thru=list(out[2:2 + k]), make_pairs=make_pairs), out[-1]


def _split_wait(handle, after):
    thru, make_pairs = handle["thru"], handle["make_pairs"]
    k = len(thru)

    def body(*refs):
        for send, arrival in make_pairs(refs[:k], refs[k], refs[k + 1]):
            send.wait_send()
            arrival.wait_recv()

    return pl.pallas_call(
        body, name=handle["name"] + "_wait", out_shape=[pltpu.HBM(a.shape, a.dtype) for a in thru],
        in_specs=[_HBM_SPEC] * k + [_SEM_SPEC, _SEM_SPEC] + [pl.BlockSpec(memory_space=pl.ANY)] * len(after),
        out_specs=[_HBM_SPEC] * k, input_output_aliases={i: i for i in range(k)},
        compiler_params=pltpu.CompilerParams(has_side_effects=_SIDE_EFFECT),
    )(*thru, *handle["sems"], *after)


def _exchange_start(name, arrays, scatter, relations, by_chip=False):
    n = len(arrays)
    lands = [lax.empty(a.shape if scatter else (N_DEV,) + a.shape, a.dtype) for a in arrays]

    def make_pairs(refs, send_sems, recv_sems):
        return _exchange_copies(refs[:n], refs[n:], send_sems, recv_sems, scatter, relations, by_chip)

    handle, token = _split_start(name, list(arrays) + lands, n * len(relations), make_pairs)
    handle.update(n=n)
    return handle, token


def _halves_start(name, arrays):
    lands = [lax.empty((a.shape[0], 1) + a.shape[2:], a.dtype) for a in arrays]
    n = len(arrays)

    def make_pairs(refs, send_sems, recv_sems):
        return _halves_copies(refs[:n], refs[n:], send_sems, recv_sems)

    return _split_start(name, list(arrays) + lands, n, make_pairs)


def _chip_sum(name, array, landed):
    chips, _, r, c = array.shape
    tr = r // 2 if r > 512 and r % 32 == 0 else r

    def body(a_ref, b_ref, o_ref):
        mine = a_ref[lax.axis_index("c")].astype(F32)
        o_ref[...] = (mine + b_ref[...].astype(F32)).astype(o_ref.dtype)

    return _call(body, name=name, grid=(chips, r // tr),
                 in_specs=[pl.BlockSpec((None, 2, tr, c), lambda k, i: (k, 0, i, 0)),
                           pl.BlockSpec((None, None, tr, c), lambda k, i: (k, 0, i, 0))],
                 out_specs=pl.BlockSpec((None, tr, c), lambda k, i: (k, i, 0)),
                 out_shape=_sds((chips, r, c), BF16), semantics=("parallel", "parallel"))(array, landed)


def _forward_start(name, lands):
    return _split_start(name, list(lands), len(lands) * len(CHIP_PEERS), _forward_copies)


def _exchange_wait(handle, after):
    return _split_wait(handle, after)[handle["n"]:]


def _rope_tables(pos_col):
    t = pos_col.shape[0]
    half = HEAD_DIM // 2
    inv_freq = ROPE_THETA ** (-jnp.arange(half, dtype=F32) / half)
    inv_row = jnp.tile(inv_freq, LANES // half)[None, :]

    def body(pos_ref, inv_ref, cos_ref, sin_ref):
        ang = pos_ref[...] * inv_ref[...]
        cos_ref[...] = jnp.cos(ang)
        sin_ref[...] = jnp.sin(ang)

    tm = min(t, 512)
    return _call(body, name="rope_tables", grid=(t // tm,),
                 in_specs=[pl.BlockSpec((tm, 1), lambda i: (i, 0)), pl.BlockSpec((1, LANES), lambda i: (0, 0))],
                 out_specs=[pl.BlockSpec((tm, LANES), lambda i: (i, 0))] * 2,
                 out_shape=[_sds((t, LANES), F32)] * 2, semantics=("parallel",))(pos_col, inv_row)


def _rot_half(x):
    lane = lax.broadcasted_iota(jnp.int32, x.shape, 1)
    low = (lane % HEAD_DIM) < HEAD_DIM // 2
    return jnp.where(low, -pltpu.roll(x, LANES - HEAD_DIM // 2, 1), pltpu.roll(x, HEAD_DIM // 2, 1))


def _rope(x, cos, sin):
    return x * cos + _rot_half(x) * sin


def _unrope(d, cos, sin):
    return d * cos - _rot_half(d) * sin


def _band_mask(first_block, heads):
    r = lax.broadcasted_iota(jnp.int32, (heads * BLOCK, 2 * BLOCK), 0) % BLOCK
    c = lax.broadcasted_iota(jnp.int32, (heads * BLOCK, 2 * BLOCK), 1)
    diff = r - c + BLOCK
    return (diff >= 0) & (diff < WINDOW) & ((c >= BLOCK) | jnp.logical_not(first_block))


def _attn_specs(t, d_attn, d_in):
    kb, vb = d_attn // D_KV, d_attn // D_KV + 1
    prev = lambda i: jnp.maximum(i - 1, 0)
    return [
        pl.BlockSpec((BLOCK, d_attn), lambda i: (i, 0)),
        pl.BlockSpec((BLOCK, D_KV), lambda i: (i, kb)),
        pl.BlockSpec((BLOCK, D_KV), lambda i: (i, vb)),
        pl.BlockSpec((BLOCK, D_KV), lambda i: (prev(i), kb)),
        pl.BlockSpec((BLOCK, D_KV), lambda i: (prev(i), vb)),
        pl.BlockSpec((1, LANES), lambda i: (0, 0)),
    ]


def _head(x, h):
    return x[:, h * HEAD_DIM:(h + 1) * HEAD_DIM]


def _attn_heads(q_ref, kc_ref, vc_ref, kp_ref, vp_ref, d_attn):
    q_heads = [_head(q_ref[...], h).astype(BF16) for h in range(d_attn // HEAD_DIM)]
    kk = [jnp.concatenate([_head(kp_ref[...], g), _head(kc_ref[...], g)], axis=0).astype(BF16) for g in range(N_KV_HEADS)]
    vv = [jnp.concatenate([_head(vp_ref[...], g), _head(vc_ref[...], g)], axis=0).astype(BF16) for g in range(N_KV_HEADS)]
    return q_heads, kk, vv


def _stack_group(q_heads, sink_ref, group):
    q_all = jnp.concatenate([q_heads[h] for h in group], axis=0)
    sink_all = jnp.concatenate([jnp.broadcast_to(sink_ref[:, h:h + 1], (BLOCK, 1)) for h in group], axis=0)
    return q_all, sink_all


def _softmax_with_sink(q, kk, sink, mask):
    s = _dot(q, kk, 1, 1) * (1.0 / math.sqrt(HEAD_DIM))
    s = jnp.where(mask, s, MASKED)
    m = jnp.maximum(jnp.max(s, axis=-1, keepdims=True), sink)
    p = jnp.exp(s - m)
    e_sink = jnp.exp(sink - m)
    inv = 1.0 / (jnp.sum(p, axis=-1, keepdims=True) + e_sink)
    return p * inv, e_sink * inv


def _attention_fwd(proj, sinks_row, d_attn):
    t, d_in = proj.shape
    n_heads = d_attn // HEAD_DIM
    q_per_kv = n_heads // N_KV_HEADS

    def body(q_ref, kc_ref, vc_ref, kp_ref, vp_ref, sink_ref, o_ref):
        mask = _band_mask(pl.program_id(0) == 0, q_per_kv)
        q_heads, kk, vv = _attn_heads(q_ref, kc_ref, vc_ref, kp_ref, vp_ref, d_attn)
        for g in range(N_KV_HEADS):
            group = range(g * q_per_kv, (g + 1) * q_per_kv)
            q_all, sink_all = _stack_group(q_heads, sink_ref, group)
            probs, _ = _softmax_with_sink(q_all, kk[g], sink_all, mask)
            o_all = _dot(probs.astype(BF16), vv[g], 1, 0)
            for k, h in enumerate(group):
                o_ref[:, h * HEAD_DIM:(h + 1) * HEAD_DIM] = o_all[k * BLOCK:(k + 1) * BLOCK]

    return _call(body, name="attention_fwd", grid=(t // BLOCK,), in_specs=_attn_specs(t, d_attn, d_in),
                 out_specs=pl.BlockSpec((BLOCK, d_attn), lambda i: (i, 0)), out_shape=_sds((t, d_attn), F32),
                 semantics=("parallel",))(proj, proj, proj, proj, proj, sinks_row)


def _attention_bwd(proj, sinks_row, d_out, d_attn):
    t, d_in = proj.shape
    n_heads = d_attn // HEAD_DIM
    q_per_kv = n_heads // N_KV_HEADS
    nb = t // BLOCK
    per = LANES // HEAD_DIM

    def body(q_ref, kc_ref, vc_ref, kp_ref, vp_ref, sink_ref, do_ref, dq_ref, dk_ref, dv_ref, dsink_ref):
        i = pl.program_id(0)
        mask = _band_mask(i == 0, q_per_kv)
        q_heads, kk, vv = _attn_heads(q_ref, kc_ref, vc_ref, kp_ref, vp_ref, d_attn)
        lane = lax.broadcasted_iota(jnp.int32, (1, LANES), 1)
        dsink = jnp.zeros((1, LANES), F32)
        dq_rot, dkk, dvv = [], [], []
        for g in range(N_KV_HEADS):
            group = range(g * q_per_kv, (g + 1) * q_per_kv)
            q_all, sink_all = _stack_group(q_heads, sink_ref, group)
            probs, p_sink = _softmax_with_sink(q_all, kk[g], sink_all, mask)
            do_all = jnp.concatenate([do_ref[:, h * HEAD_DIM:(h + 1) * HEAD_DIM] for h in group], axis=0).astype(BF16)
            dp = _dot(do_all, vv[g], 1, 1)
            delta = jnp.sum(probs * dp, axis=-1, keepdims=True)
            ds = (probs * (dp - delta) * (1.0 / math.sqrt(HEAD_DIM))).astype(BF16)
            dq_all = _dot(ds, kk[g], 1, 0)
            dkk.append(_dot(ds, q_all, 0, 0))
            dvv.append(_dot(probs.astype(BF16), do_all, 0, 0))
            sink_term = p_sink * delta
            for k, h in enumerate(group):
                dq_rot.append(dq_all[k * BLOCK:(k + 1) * BLOCK])
                part = jnp.sum(sink_term[k * BLOCK:(k + 1) * BLOCK], axis=0, keepdims=True)
                dsink += jnp.where(lane == h, -part, 0.0)
        for j in range(d_attn // LANES):
            dq_ref[:, j * LANES:(j + 1) * LANES] = jnp.concatenate(dq_rot[j * per:(j + 1) * per], axis=1)
        for j in range(D_KV // LANES):
            d = jnp.concatenate(dkk[j * per:(j + 1) * per], axis=1)
            dk_ref[0, :, j * LANES:(j + 1) * LANES] = d[:BLOCK]
            dk_ref[1, :, j * LANES:(j + 1) * LANES] = d[BLOCK:]
            d = jnp.concatenate(dvv[j * per:(j + 1) * per], axis=1)
            dv_ref[0, :, j * LANES:(j + 1) * LANES] = d[:BLOCK]
            dv_ref[1, :, j * LANES:(j + 1) * LANES] = d[BLOCK:]

        @pl.when(i == 0)
        def _():
            dsink_ref[...] = jnp.zeros_like(dsink_ref)

        dsink_ref[...] += dsink

    pair = pl.BlockSpec((2, BLOCK, D_KV), lambda i: (i, 0, 0))
    return _call(body, name="attention_bwd", grid=(nb,),
                 in_specs=_attn_specs(t, d_attn, d_in) + [pl.BlockSpec((BLOCK, d_attn), lambda i: (i, 0))],
                 out_specs=[pl.BlockSpec((BLOCK, d_attn), lambda i: (i, 0)), pair, pair,
                            pl.BlockSpec((1, LANES), lambda i: (0, 0))],
                 out_shape=[_sds((t, d_attn), F32), _sds((2 * nb, BLOCK, D_KV), F32), _sds((2 * nb, BLOCK, D_KV), F32),
                            _sds((1, LANES), F32)],
                 semantics=("arbitrary",))(proj, proj, proj, proj, proj, sinks_row, d_out)


def _assemble_dproj(dq, dk2, dv2, du, cos, sin, d_in, after):
    t, d_attn = dq.shape
    d_ssm = du.shape[1]
    nb = t // BLOCK

    def body(dq_ref, dk_own, dk_next, dv_own, dv_next, du_ref, cos_ref, sin_ref, o_ref):
        has_next = (pl.program_id(0) < nb - 1).astype(F32)
        c, s = cos_ref[...], sin_ref[...]
        for j in range(d_attn // LANES):
            o_ref[:, j * LANES:(j + 1) * LANES] = _unrope(dq_ref[:, j * LANES:(j + 1) * LANES], c, s).astype(BF16)
        dk = dk_own[...] + has_next * dk_next[...]
        for j in range(D_KV // LANES):
            at = d_attn + j * LANES
            o_ref[:, at:at + LANES] = _unrope(dk[:, j * LANES:(j + 1) * LANES], c, s).astype(BF16)
        o_ref[:, d_attn + D_KV:d_attn + 2 * D_KV] = (dv_own[...] + has_next * dv_next[...]).astype(BF16)
        o_ref[:, d_attn + 2 * D_KV:] = du_ref[...].astype(BF16)

    own = pl.BlockSpec((None, BLOCK, D_KV), lambda i: (2 * i + 1, 0, 0))
    nxt = pl.BlockSpec((None, BLOCK, D_KV), lambda i: (jnp.minimum(2 * i + 2, 2 * nb - 1), 0, 0))
    table = pl.BlockSpec((BLOCK, LANES), lambda i: (i, 0))
    return _call(body, name="assemble_dproj", grid=(nb,),
                 in_specs=[pl.BlockSpec((BLOCK, d_attn), lambda i: (i, 0)), own, nxt, own, nxt,
                           pl.BlockSpec((BLOCK, d_ssm), lambda i: (i, 0)), table, table],
                 out_specs=pl.BlockSpec((BLOCK, d_in), lambda i: (i, 0)), out_shape=_sds((t, d_in), BF16),
                 semantics=("parallel",), n_after=len(after))(dq, dk2, dk2, dv2, dv2, du, cos, sin, *after)


def _proj_in(xn, w_in_t, cos, sin, d_rot, tm=512):
    t, d = xn.shape
    d_in = w_in_t.shape[0]
    tm = min(tm, t)

    def product(x, w, c, s):
        y = _dot(x, w, 1, 1)
        rotated = [_rope(y[:, j * LANES:(j + 1) * LANES], c, s) for j in range(d_rot // LANES)]
        return jnp.concatenate(rotated + [y[:, d_rot:]], axis=1)

    table = pl.BlockSpec((tm, LANES), lambda i, j, k: (i, 0))
    return _matmul("proj_in", [xn, w_in_t, cos, sin],
                   [pl.BlockSpec((tm, d), lambda i, j, k: (i, 0)), pl.BlockSpec((d_in, d), lambda i, j, k: (0, 0)),
                    table, table],
                   product, (t // tm, 1, 1), _sds((t, d_in), F32), pl.BlockSpec((tm, d_in), lambda i, j, k: (i, 0)),
                   (tm, d_in))


def _discretise(ar, ai, ldt, br, bi):
    dt = jnp.exp(ldt)
    mag = jnp.exp(ar * dt)
    lam_re = mag * jnp.cos(ai * dt)
    lam_im = mag * jnp.sin(ai * dt)
    den = ar * ar + ai * ai
    nr = lam_re - 1.0
    ni = lam_im
    f_re = (nr * ar + ni * ai) / den
    f_im = (ni * ar - nr * ai) / den
    return (lam_re, lam_im, [f_re * r - f_im * i for r, i in zip(br, bi)], [f_re * i + f_im * r for r, i in zip(br, bi)])


def _whole(arrays):
    return [pl.BlockSpec(a.shape, lambda *_, nd=len(a.shape): (0,) * nd) for a in arrays]


def _channels(ref):
    groups = ref.shape[0] // SSM_GROUP
    return [ref[pl.ds(p, groups, stride=SSM_GROUP), :] for p in range(SSM_GROUP)]


def _store_channels(ref, values):
    groups = ref.shape[0] // SSM_GROUP
    for p, val in enumerate(values):
        ref[pl.ds(p, groups, stride=SSM_GROUP), :] = val


def _s5_discretise(ar, ai, ldt, br, bi):
    ins = [ar, ai, ldt, br, bi]

    def body(ar_ref, ai_ref, ldt_ref, br_ref, bi_ref, lr_ref, li_ref, bbr_ref, bbi_ref):
        lr, li, bbr, bbi = _discretise(ar_ref[...], ai_ref[...], ldt_ref[...], _channels(br_ref), _channels(bi_ref))
        lr_ref[...] = lr
        li_ref[...] = li
        _store_channels(bbr_ref, bbr)
        _store_channels(bbi_ref, bbi)

    outs = [_sds(ar.shape, F32), _sds(ar.shape, F32), _sds(br.shape, F32), _sds(br.shape, F32)]
    return _call(body, name="s5_discretise", in_specs=_whole(ins), out_specs=_whole(outs), out_shape=outs)(*ins)


def _s5_discretise_bwd(ar, ai, ldt, br, bi, d_lr, d_li, d_bbr, d_bbi):
    ins = [ar, ai, ldt, br, bi, d_lr, d_li, d_bbr, d_bbi]

    def body(ar_ref, ai_ref, ldt_ref, br_ref, bi_ref, dlr_ref, dli_ref, dbbr_ref, dbbi_ref,
             dar_ref, dai_ref, dldt_ref, dbr_ref, dbi_ref):
        _, vjp = jax.vjp(_discretise, ar_ref[...], ai_ref[...], ldt_ref[...], _channels(br_ref), _channels(bi_ref))
        dar, dai, dldt, dbr, dbi = vjp((dlr_ref[...], dli_ref[...], _channels(dbbr_ref), _channels(dbbi_ref)))
        dar_ref[...] = dar
        dai_ref[...] = dai
        dldt_ref[...] = dldt
        _store_channels(dbr_ref, dbr)
        _store_channels(dbi_ref, dbi)

    outs = [_sds(a.shape, F32) for a in (ar, ai, ldt, br, bi)]
    return _call(body, name="s5_discretise_bwd", in_specs=_whole(ins), out_specs=_whole(outs), out_shape=outs)(*ins)


def _cmul(ar, ai, br, bi):
    return ar * br - ai * bi, ar * bi + ai * br


def _load_segmented(ref, tile0, n_tiles, seg):
    return jnp.concatenate([ref[pl.ds(tile0 + j, SUBLANES, stride=seg), :] for j in range(n_tiles)], axis=0)


def _store_segmented(ref, tile0, seg, value):
    for j in range(value.shape[0] // SUBLANES):
        ref[pl.ds(tile0 + j, SUBLANES, stride=seg), :] = value[j * SUBLANES:(j + 1) * SUBLANES, :]


def _fill_powers(lr, li, pr_ref, pi_ref, seg):
    pows = [(lr, li)]
    for _ in range(SUBLANES - 1):
        pows.append(_cmul(pows[-1][0], pows[-1][1], lr, li))
    row = lax.broadcasted_iota(jnp.int32, (SUBLANES, lr.shape[1]), 0)
    tr = jnp.zeros((SUBLANES, lr.shape[1]), F32)
    ti = jnp.zeros((SUBLANES, lr.shape[1]), F32)
    for r in range(SUBLANES):
        tr = jnp.where(row == r, pows[r][0], tr)
        ti = jnp.where(row == r, pows[r][1], ti)
    pr_ref[0:SUBLANES, :] = tr
    pi_ref[0:SUBLANES, :] = ti
    k = SUBLANES
    while k < seg:
        fr, fi = pr_ref[k - 1:k, :], pi_ref[k - 1:k, :]
        for t0 in range(0, k, SUBLANES):
            nr, ni = _cmul(pr_ref[t0:t0 + SUBLANES, :], pi_ref[t0:t0 + SUBLANES, :], fr, fi)
            pr_ref[k + t0:k + t0 + SUBLANES, :] = nr
            pi_ref[k + t0:k + t0 + SUBLANES, :] = ni
        k *= 2


def _scan_segments(sr_ref, si_ref, pr_ref, pi_ref, lr, li, seg, reverse, per_tile=None):
    w = lr.shape[1]
    sign = -1.0 if reverse else 1.0
    lrb = jnp.broadcast_to(lr, (SUBLANES, w))
    lib = jnp.broadcast_to(sign * li, (SUBLANES, w))
    zero = jnp.zeros((SUBLANES, w), F32)

    def tile_rows(j):
        return pl.ds(pl.multiple_of(j * SUBLANES, SUBLANES), SUBLANES)

    def local(i, carry):
        rows = tile_rows(seg - 1 - i if reverse else i)
        pr, pi = _cmul(lrb, lib, carry[0], carry[1])
        xr, xi = sr_ref[rows, :] + pr, si_ref[rows, :] + pi
        sr_ref[rows, :] = xr
        si_ref[rows, :] = xi
        return xr, xi

    end_r, end_i = lax.fori_loop(0, seg, local, (zero, zero))
    full_r, full_i = pr_ref[seg - 1:seg, :], sign * pi_ref[seg - 1:seg, :]
    row = lax.broadcasted_iota(jnp.int32, (SUBLANES, w), 0)
    in_r, in_i = zero, zero
    cur_r, cur_i = jnp.zeros((1, w), F32), jnp.zeros((1, w), F32)
    for r in (range(SUBLANES - 2, -1, -1) if reverse else range(1, SUBLANES)):
        src = r + 1 if reverse else r - 1
        pr, pi = _cmul(full_r, full_i, cur_r, cur_i)
        cur_r, cur_i = end_r[src:src + 1, :] + pr, end_i[src:src + 1, :] + pi
        in_r = jnp.where(row == r, cur_r, in_r)
        in_i = jnp.where(row == r, cur_i, in_i)

    def carry_in(j, _):
        rows = tile_rows(j)
        k = seg - 1 - j if reverse else j
        pr, pi = _cmul(pr_ref[pl.ds(k, 1), :], sign * pi_ref[pl.ds(k, 1), :], in_r, in_i)
        xr, xi = sr_ref[rows, :] + pr, si_ref[rows, :] + pi
        sr_ref[rows, :] = xr
        si_ref[rows, :] = xi
        if per_tile is not None:
            per_tile(j, xr, xi)
        return 0

    lax.fori_loop(0, seg, carry_in, 0)


_S5_ROWS = 256


def _s5_in_specs(t, d_attn):
    u_block = (d_attn + 2 * D_KV) // SSM_CH_BLOCK
    blk3 = lambda shape: pl.BlockSpec((None,) + shape, lambda j: (j, 0, 0))
    return [
        pl.BlockSpec((t, SSM_CH_BLOCK), lambda j: (0, u_block + j)),
        blk3((SSM_CH_BLOCK, SSM_ST_BLOCK)), blk3((SSM_CH_BLOCK, SSM_ST_BLOCK)),
        blk3((1, SSM_ST_BLOCK)), blk3((1, SSM_ST_BLOCK)),
        blk3((SSM_ST_BLOCK, SSM_CH_BLOCK)), blk3((SSM_ST_BLOCK, SSM_CH_BLOCK)),
        pl.BlockSpec((1, SSM_CH_BLOCK), lambda j: (0, j)),
    ]


def _chunks(t):
    rows = min(_S5_ROWS, t)
    return rows, lambda i: pl.ds(pl.multiple_of(i * rows, rows), rows)


def _s5_states(u_ref, us_ref, bre_ref, bim_ref, lr_ref, li_ref, sr_ref, si_ref, pr_ref, pi_ref, t):
    seg = t // SUBLANES
    rows, chunk = _chunks(t)
    for c in range(t // rows):
        us_ref[c * rows:(c + 1) * rows, :] = _load_segmented(u_ref, c * rows // SUBLANES, rows // SUBLANES, seg)

    def fill(i, _):
        ub = us_ref[chunk(i), :].astype(BF16)
        sr_ref[chunk(i), :] = _dot(ub, bre_ref[...], 1, 0)
        si_ref[chunk(i), :] = _dot(ub, bim_ref[...], 1, 0)
        return 0

    lax.fori_loop(0, t // rows, fill, 0)
    _fill_powers(lr_ref[...], li_ref[...], pr_ref, pi_ref, seg)
    _scan_segments(sr_ref, si_ref, pr_ref, pi_ref, lr_ref[...], li_ref[...], seg, False)


def _s5_scratch(t):
    state = pltpu.VMEM((t, SSM_ST_BLOCK), F32)
    powers = pltpu.VMEM((t // SUBLANES, SSM_ST_BLOCK), F32)
    return state, powers, pltpu.VMEM((t, SSM_CH_BLOCK), F32)


def _s5_fwd(proj, mats, dskip_row, d_attn, d_ssm):
    t = proj.shape[0]
    seg = t // SUBLANES
    n_blocks = d_ssm // SSM_CH_BLOCK
    rows, chunk = _chunks(t)

    def body(u_ref, bre_ref, bim_ref, lr_ref, li_ref, cre_ref, cim_ref, d_ref, y_ref,
             sr_ref, si_ref, pr_ref, pi_ref, us_ref, ys_ref):
        _s5_states(u_ref, us_ref, bre_ref, bim_ref, lr_ref, li_ref, sr_ref, si_ref, pr_ref, pi_ref, t)

        def emit(i, _):
            ys_ref[chunk(i), :] = (_dot(sr_ref[chunk(i), :].astype(BF16), cre_ref[...], 1, 0)
                                   - _dot(si_ref[chunk(i), :].astype(BF16), cim_ref[...], 1, 0)
                                   + d_ref[...] * us_ref[chunk(i), :])
            return 0

        lax.fori_loop(0, t // rows, emit, 0)
        for c in range(t // rows):
            _store_segmented(y_ref, c * rows // SUBLANES, seg, ys_ref[c * rows:(c + 1) * rows, :])

    state, powers, channels = _s5_scratch(t)
    col = pl.BlockSpec((t, SSM_CH_BLOCK), lambda j: (0, j))
    return _call(body, name="s5_fwd", grid=(n_blocks,), in_specs=_s5_in_specs(t, d_attn), out_specs=col,
                 out_shape=_sds((t, d_ssm), F32), scratch_shapes=[state, state, powers, powers, channels, channels],
                 semantics=("parallel",))(proj, *mats, dskip_row)


def _s5_bwd(proj, mats, dskip_row, y, dz_a, dz_b, d_attn, d_ssm, after):
    t = proj.shape[0]
    seg = t // SUBLANES
    n_blocks = d_ssm // SSM_CH_BLOCK
    rows, chunk = _chunks(t)

    def body(u_ref, bre_ref, bim_ref, lr_ref, li_ref, cre_ref, cim_ref, d_ref, y_ref, dza_ref, dzb_ref,
             du_ref, dbre_ref, dbim_ref, dlr_ref, dli_ref, dcre_ref, dcim_ref, dd_ref,
             sr_ref, si_ref, gr_ref, gi_ref, pr_ref, pi_ref, us_ref, dys_ref, dus_ref, acc_r, acc_i):
        _s5_states(u_ref, us_ref, bre_ref, bim_ref, lr_ref, li_ref, sr_ref, si_ref, pr_ref, pi_ref, t)
        for ref in (dcre_ref, dcim_ref, dbre_ref, dbim_ref, dd_ref, acc_r, acc_i):
            ref[...] = jnp.zeros_like(ref)
        for c in range(t // rows):
            tile0, n_tiles = c * rows // SUBLANES, rows // SUBLANES
            dz = _load_segmented(dza_ref, tile0, n_tiles, seg) + _load_segmented(dzb_ref, tile0, n_tiles, seg)
            dys_ref[c * rows:(c + 1) * rows, :] = dz * _gelu_grad(_load_segmented(y_ref, tile0, n_tiles, seg))

        def through_c(i, _):
            dy = dys_ref[chunk(i), :]
            dd_ref[...] += jnp.sum(dy * us_ref[chunk(i), :], axis=0, keepdims=True)
            dyb = dy.astype(BF16)
            gr_ref[chunk(i), :] = _dot(dyb, cre_ref[...], 1, 1)
            gi_ref[chunk(i), :] = -_dot(dyb, cim_ref[...], 1, 1)
            dcre_ref[...] += _dot(sr_ref[chunk(i), :].astype(BF16), dyb, 0, 0)
            dcim_ref[...] -= _dot(si_ref[chunk(i), :].astype(BF16), dyb, 0, 0)
            return 0

        lax.fori_loop(0, t // rows, through_c, 0)

        row = lax.broadcasted_iota(jnp.int32, (SUBLANES, SSM_ST_BLOCK), 0)
        last = pl.ds((seg - 1) * SUBLANES, SUBLANES)
        wrap = [jnp.where(row == 0, 0.0, pltpu.roll(ref[last, :], 1, 0)) for ref in (sr_ref, si_ref)]

        def lambda_grad(j, g_re, g_im):
            before = pl.ds(pl.multiple_of(jnp.maximum(j - 1, 0) * SUBLANES, SUBLANES), SUBLANES)
            prev_r = jnp.where(j > 0, sr_ref[before, :], wrap[0])
            prev_i = jnp.where(j > 0, si_ref[before, :], wrap[1])
            acc_r[...] += g_re * prev_r + g_im * prev_i
            acc_i[...] += g_im * prev_r - g_re * prev_i

        _scan_segments(gr_ref, gi_ref, pr_ref, pi_ref, lr_ref[...], li_ref[...], seg, True, per_tile=lambda_grad)
        dlr_ref[...] = jnp.sum(acc_r[...], axis=0, keepdims=True)
        dli_ref[...] = jnp.sum(acc_i[...], axis=0, keepdims=True)

        def through_b(i, _):
            ub = us_ref[chunk(i), :].astype(BF16)
            grb, gib = gr_ref[chunk(i), :].astype(BF16), gi_ref[chunk(i), :].astype(BF16)
            dbre_ref[...] += _dot(ub, grb, 0, 0)
            dbim_ref[...] += _dot(ub, gib, 0, 0)
            dus_ref[chunk(i), :] = (_dot(grb, bre_ref[...], 1, 1) + _dot(gib, bim_ref[...], 1, 1)
                                    + d_ref[...] * dys_ref[chunk(i), :])
            return 0

        lax.fori_loop(0, t // rows, through_b, 0)
        for c in range(t // rows):
            _store_segmented(du_ref, c * rows // SUBLANES, seg, dus_ref[c * rows:(c + 1) * rows, :])

    col = pl.BlockSpec((t, SSM_CH_BLOCK), lambda j: (0, j))
    blk3 = lambda shape: pl.BlockSpec((None,) + shape, lambda j: (j, 0, 0))
    state, powers, channels = _s5_scratch(t)
    return _call(
        body, name="s5_bwd", grid=(n_blocks,), in_specs=_s5_in_specs(t, d_attn) + [col, col, col],
        out_specs=[col, blk3((SSM_CH_BLOCK, SSM_ST_BLOCK)), blk3((SSM_CH_BLOCK, SSM_ST_BLOCK)),
                   blk3((1, SSM_ST_BLOCK)), blk3((1, SSM_ST_BLOCK)),
                   blk3((SSM_ST_BLOCK, SSM_CH_BLOCK)), blk3((SSM_ST_BLOCK, SSM_CH_BLOCK)),
                   pl.BlockSpec((1, SSM_CH_BLOCK), lambda j: (0, j))],
        out_shape=[_sds((t, d_ssm), F32),
                   _sds((n_blocks, SSM_CH_BLOCK, SSM_ST_BLOCK), F32), _sds((n_blocks, SSM_CH_BLOCK, SSM_ST_BLOCK), F32),
                   _sds((n_blocks, 1, SSM_ST_BLOCK), F32), _sds((n_blocks, 1, SSM_ST_BLOCK), F32),
                   _sds((n_blocks, SSM_ST_BLOCK, SSM_CH_BLOCK), F32), _sds((n_blocks, SSM_ST_BLOCK, SSM_CH_BLOCK), F32),
                   _sds((1, d_ssm), F32)],
        scratch_shapes=[state, state, state, state, powers, powers, channels, channels, channels,
                        pltpu.VMEM((SUBLANES, SSM_ST_BLOCK), F32), pltpu.VMEM((SUBLANES, SSM_ST_BLOCK), F32)],
        semantics=("parallel",), n_after=len(after))(proj, *mats, dskip_row, y, dz_a, dz_b, *after)


def _by_block(gp_n):
    return gp_n.reshape(-1, GROUPS_PER_BLOCK, SSM_GROUP, SSM_STATE)


def _block_diag_in(bbar):
    eye = jnp.eye(GROUPS_PER_BLOCK, dtype=F32)
    return jnp.einsum("jgpn,gh->jgphn", _by_block(bbar), eye).reshape(-1, SSM_CH_BLOCK, SSM_ST_BLOCK)


def _block_diag_in_t(dense):
    d5 = dense.reshape(-1, GROUPS_PER_BLOCK, SSM_GROUP, GROUPS_PER_BLOCK, SSM_STATE)
    eye = jnp.eye(GROUPS_PER_BLOCK, dtype=F32)
    return jnp.einsum("jgphn,gh->jgpn", d5, eye).reshape(-1, SSM_STATE)


def _block_diag_out(c):
    eye = jnp.eye(GROUPS_PER_BLOCK, dtype=F32)
    return jnp.einsum("jgpn,gh->jgnhp", _by_block(c), eye).reshape(-1, SSM_ST_BLOCK, SSM_CH_BLOCK)


def _block_diag_out_t(dense):
    d5 = dense.reshape(-1, GROUPS_PER_BLOCK, SSM_STATE, GROUPS_PER_BLOCK, SSM_GROUP)
    eye = jnp.eye(GROUPS_PER_BLOCK, dtype=F32)
    return jnp.einsum("jgnhp,gh->jgpn", d5, eye).reshape(-1, SSM_STATE)


def _adamw(w, g, m, v):
    m = ADAM_B1 * m + (1.0 - ADAM_B1) * g
    v = ADAM_B2 * v + (1.0 - ADAM_B2) * (g * g)
    m_hat = m / (1.0 - ADAM_B1 ** ADAM_STEP)
    v_hat = v / (1.0 - ADAM_B2 ** ADAM_STEP)
    delta = -ADAM_LR * (m_hat / (jnp.sqrt(v_hat) + ADAM_EPS) + ADAM_WD * w)
    return delta, m, v


def _adam_sharded(name, parts, w, m, v, tr, row0=0):
    r, c = w.shape
    assert r % tr == 0 and row0 % tr == 0, (name, r, tr, row0)

    def body(p_ref, w_ref, m_ref, v_ref, g_out, d_out, m_out, v_out):
        g = p_ref[0].astype(F32)
        for i in range(1, p_ref.shape[0]):
            g = g + p_ref[i].astype(F32)
        delta, m_new, v_new = _adamw(w_ref[...], g, m_ref[...], v_ref[...])
        g_out[...] = g
        d_out[...] = delta
        m_out[...] = m_new
        v_out[...] = v_new

    tile = pl.BlockSpec((tr, c), lambda i: (i, 0))
    return _call(body, name=name, grid=(r // tr,),
                 in_specs=[pl.BlockSpec((parts.shape[0], tr, c), lambda i: (0, i + row0 // tr, 0)), tile, tile, tile],
                 out_specs=[tile] * 4, out_shape=[_sds((r, c), F32)] * 4, semantics=("parallel",))(parts, w, m, v)


_BIG = ("w_in", "w_glu", "w_o", "w_gate", "w_up", "w_down")
_BY_COLUMNS = ("w_in", "w_gate", "w_up")
_SMALL_VECTORS = ("sinks", "log_dt", "b_glu", "g_attn_out", "g_ssm_out", "g_post_mix", "g_pre_ffn", "g_post_ffn")
_SMALL_MATRICES = ("b_re", "b_im", "c_re", "c_im", "a_re", "a_im")
_ORDER = ("g_pre_mix", "w_in", "sinks", "a_re", "a_im", "log_dt", "b_re", "b_im", "c_re", "c_im", "d_skip", "w_glu",
          "b_glu", "g_attn_out", "g_ssm_out", "w_o", "g_post_mix", "g_pre_ffn", "w_gate", "w_up", "w_down",
          "g_post_ffn")


def _pack_grads(vectors, matrices):
    width = max(a.shape[1] for a in vectors)
    slots, row, lane = [], 0, 0
    for a in vectors:
        span = -(-a.shape[1] // LANES) * LANES
        if lane + span > width:
            row, lane = row + 1, 0
        slots.append((row, lane, a.shape[1]))
        lane += span
    firsts, at = [], 0
    for a in matrices:
        firsts.append(at)
        at += a.shape[0]
    nv = len(vectors)

    def body(*refs):
        vec_out, mat_out = refs[-2], refs[-1]
        vec_out[...] = jnp.zeros_like(vec_out)
        for ref, (r, l, w) in zip(refs[:nv], slots):
            vec_out[r:r + 1, l:l + w] = ref[...]
        for ref, r0 in zip(refs[nv:-2], firsts):
            mat_out[r0:r0 + ref.shape[0], :] = ref[...]

    ins = list(vectors) + list(matrices)
    outs = [_sds((-(-(row + 1) // SUBLANES) * SUBLANES, width), F32), _sds((at, matrices[0].shape[1]), F32)]
    vec_pack, mat_pack = _call(body, name="pack_small_grads", in_specs=_whole(ins), out_specs=_whole(outs),
                               out_shape=outs)(*ins)
    return vec_pack, slots, mat_pack, firsts


def _adam_replicated(sources, found_at, w, m, v):
    ns, n = len(sources), len(w)

    def body(*refs):
        ins, outs = refs[ns:ns + 3 * n], refs[ns + 3 * n:]
        summed = []
        for p_ref in refs[:ns]:
            g = p_ref[0]
            for k in range(1, N_DEV):
                g = g + p_ref[k]
            summed.append(g)
        for i, (src, row, lane) in enumerate(found_at):
            w_ref, m_ref, v_ref = ins[i], ins[n + i], ins[2 * n + i]
            rows, cols = w_ref.shape
            g = summed[src][row:row + rows, lane:lane + cols]
            delta, m_new, v_new = _adamw(w_ref[...], g, m_ref[...], v_ref[...])
            for o, val in zip(outs[4 * i:4 * i + 4], (g, delta, m_new, v_new)):
                o[...] = val

    ins = list(sources) + list(w) + list(m) + list(v)
    outs = [_sds(a.shape, F32) for a in w for _ in range(4)]
    flat = _call(body, name="adam_replicated", in_specs=_whole(ins), out_specs=_whole(outs), out_shape=outs)(*ins)
    return [tuple(flat[4 * i:4 * i + 4]) for i in range(n)]


def kernel(x, positions, g_pre_mix, w_in, sinks, a_re, a_im, log_dt, b_re, b_im, c_re, c_im, d_skip, w_glu, b_glu, g_attn_out, g_ssm_out, w_o, g_post_mix, g_pre_ffn, w_gate, w_up, w_down, g_post_ffn, loss_target, m_g_pre_mix, m_w_in, m_sinks, m_a_re, m_a_im, m_log_dt, m_b_re, m_b_im, m_c_re, m_c_im, m_d_skip, m_w_glu, m_b_glu, m_g_attn_out, m_g_ssm_out, m_w_o, m_g_post_mix, m_g_pre_ffn, m_w_gate, m_w_up, m_w_down, m_g_post_ffn, v_g_pre_mix, v_w_in, v_sinks, v_a_re, v_a_im, v_log_dt, v_b_re, v_b_im, v_c_re, v_c_im, v_d_skip, v_w_glu, v_b_glu, v_g_attn_out, v_g_ssm_out, v_w_o, v_g_post_mix, v_g_pre_ffn, v_w_gate, v_w_up, v_w_down, v_g_post_ffn):
    given = dict(locals())
    weights = {n: given[n] for n in _ORDER}
    mom_m = {n: given["m_" + n] for n in _ORDER}
    mom_v = {n: given["v_" + n] for n in _ORDER}

    t, d = x.shape[1], x.shape[2]
    d_attn = d // 2
    d_ssm = d - d_attn
    d_in = d_attn + 2 * D_KV + d_ssm
    n_groups = d_ssm // SSM_GROUP
    n_heads = d_attn // HEAD_DIM
    tm = min(256, t)

    x2 = x[0]
    target = loss_target[0]

    def by_rows(n, a):
        return a[0].T if n in _BY_COLUMNS else a[0]

    def start_gather(name, ns, token):
        behind = 0 if token is None else token[0, 0].astype(BF16)
        shards = [by_rows(n, weights[n]).astype(BF16) + behind for n in ns]
        return _exchange_start(name, shards, False, (OWN, SIBLING) + CHIP_PEERS)

    def forward_gather(handle, after):
        return _forward_start(handle["name"] + "_forward", _exchange_wait(handle, after))

    def finish_gather(handle, after):
        return _split_wait(forward_gather(handle, after)[0], [])

    ag_in, token = start_gather("gather_w_in", ["w_in"], None)
    ag_mix, token = start_gather("gather_w_glu_o", ["w_glu", "w_o"], token)
    ag_ffn_in, token = start_gather("gather_w_gate_up", ["w_gate", "w_up"], token)
    ag_down, token = start_gather("gather_w_down", ["w_down"], token)

    xn, = _rows("norm_in", lambda xv, g: ([_rms(xv)[0] * g], []), [x2], [g_pre_mix], [(d, BF16)], [], tm,
                after=[token])
    win_g, = finish_gather(ag_in, [xn])
    w_in_t = win_g.reshape(d_in, d)
    cos, sin = _rope_tables(positions.reshape(t, 1).astype(F32))
    proj = _proj_in(xn, w_in_t, cos, sin, d_attn + D_KV)
    sinks_row = jnp.pad(sinks, ((0, 0), (0, LANES - n_heads)))
    attn = _attention_fwd(proj, sinks_row, d_attn)

    def view(n, a):
        if n in ("b_re", "b_im"):
            return jnp.transpose(a[0], (0, 2, 1)).reshape(-1, SSM_STATE)
        if n in ("c_re", "c_im"):
            return a[0].reshape(-1, SSM_STATE)
        return a[0].T if n == "d_skip" else a[0] if a.ndim == 3 else a

    def unview(n, val):
        if n in ("b_re", "b_im"):
            return jnp.transpose(val.reshape(n_groups, SSM_GROUP, SSM_STATE), (0, 2, 1))[None]
        if n in ("c_re", "c_im"):
            return val.reshape(1, n_groups, SSM_GROUP, SSM_STATE)
        return val.T[None] if n == "d_skip" else val[None] if weights[n].ndim == 3 else val

    b_re_v, b_im_v = view("b_re", b_re), view("b_im", b_im)
    ldt_col = log_dt.reshape(n_groups, 1)
    lam_re, lam_im, bbar_re, bbar_im = _s5_discretise(a_re[0], a_im[0], ldt_col, b_re_v, b_im_v)
    n_blocks = n_groups // GROUPS_PER_BLOCK
    mats = [_block_diag_in(bbar_re).astype(BF16), _block_diag_in(bbar_im).astype(BF16),
            lam_re.reshape(n_blocks, 1, SSM_ST_BLOCK), lam_im.reshape(n_blocks, 1, SSM_ST_BLOCK),
            _block_diag_out(view("c_re", c_re)).astype(BF16), _block_diag_out(view("c_im", c_im)).astype(BF16)]
    dskip_row = d_skip.reshape(1, d_ssm)
    forward_mix, _ = forward_gather(ag_mix, [attn])
    y_ssm = _s5_fwd(proj, mats, dskip_row, d_attn, d_ssm)
    gelu_bf16 = lambda yv: _gelu(yv).astype(BF16)
    wglu_g, wo_g = _split_wait(forward_mix, [y_ssm])
    w_glu_full = wglu_g.reshape(d_ssm, d_ssm)
    w_o_full = wo_g.reshape(d, d)
    glu_lin = _mm_nn("glu_gate", y_ssm, w_glu_full, F32, a_fn=gelu_bf16)

    def mix_prep(av, yv, gl, bg, ga, gs):
        ssm = _gelu(yv) * _sigmoid(gl + bg)
        return [jnp.concatenate([_rms(av)[0] * ga, _rms(ssm)[0] * gs], axis=1)], []

    mixed, = _rows("mix_prep", mix_prep, [attn, y_ssm, glu_lin], [b_glu, g_attn_out, g_ssm_out], [(d, BF16)], [], tm)
    mix = _mm_nn("mix_out", mixed, w_o_full, F32)

    def post_mix(xv, mv, gpm, gpf):
        h = xv + _rms(mv)[0] * gpm
        return [h, _rms(h)[0] * gpf], []

    forward_ffn_in, token = forward_gather(ag_ffn_in, [mix])
    h, hn = _rows("post_mix", post_mix, [x2, mix], [g_post_mix, g_pre_ffn], [(d, F32), (d, BF16)], [], tm,
                  after=[token])
    wgate_g, wup_g = _split_wait(forward_ffn_in, [hn])
    gate, up, hid = _ffn_in(hn, wgate_g, wup_g)
    wdown_g, = finish_gather(ag_down, [hid])
    ff = _mm_contract_slots("ffn_down", [(hid, wdown_g)], F32, per_step=2, tm=1024)

    def head(hv, fv, tv, gpo):
        out = hv + _rms(fv)[0] * gpo
        err = out - tv
        dout = err * (1.0 / d)
        dff, dg = _rms_bwd(fv, gpo, dout)
        loss = jnp.zeros((1, LANES), F32) + 0.5 * jnp.sum(err * err) * (1.0 / d)
        return [dff, dout], [dg, loss]

    dff, dh_out, dg_post_ffn, loss_row = _rows("loss_head", head, [h, ff, target], [g_post_ffn],
                                               [(d, BF16), (d, F32)], [d, LANES], tm)

    def swap_halves(name, grads):
        return _halves_start("swap_" + name, [g.reshape(N_DEV // 2, 2, *g.shape[1:]) for g in grads])

    def scatter_chip_sums(name, swap, after):
        both = _split_wait(swap, after)
        half = len(both) // 2
        sums = [_chip_sum("chip_sum_%s_%d" % (name, i), both[i], both[half + i]) for i in range(half)]
        return _exchange_start("scatter_" + name, sums, True, (OWN,) + CHIP_PEERS, by_chip=True)

    dw_down = _mm_slots_tn("ffn_down_dw", hid, dff, BF16)
    swap_down, token = swap_halves("dw_down", [dw_down])
    dgate, dup = _ffn_down_bwd(dff, wdown_g, gate, up, [token])
    rs_down, token = scatter_chip_sums("dw_down", swap_down, [dgate])
    dhn = _mm_contract_slots("ffn_in_dx", [(dgate, wgate_g), (dup, wup_g)], F32, per_step=2, tm=1024, tn=1024,
                             after=[token])
    dw_gate = _mm_slots_tn("ffn_gate_dw", dgate, hn, BF16)
    dw_up = _mm_slots_tn("ffn_up_dw", dup, hn, BF16)
    swap_ffn_in, tok_ffn_in = swap_halves("dw_gate_up", [dw_gate, dw_up])

    def mid_bwd(dho, dhn_, hv, mv, gpf, gpm):
        d1, dgpf = _rms_bwd(hv, gpf, dhn_)
        dh_ = dho + d1
        dmix_, dgpm = _rms_bwd(mv, gpm, dh_)
        return [dh_, dmix_], [dgpf, dgpm]

    dh, dmix, dg_pre_ffn, dg_post_mix = _rows("mid_bwd", mid_bwd, [dh_out, dhn, h, mix], [g_pre_ffn, g_post_mix],
                                              [(d, F32), (d, BF16)], [d, d], tm, after=[tok_ffn_in])

    dmixed = _mm_nt("mix_out_dx", dmix, w_o_full, F32)
    rs_ffn_in, token = scatter_chip_sums("dw_gate_up", swap_ffn_in, [dmixed])
    dw_o = _mm_tn("mix_out_dw", mixed, dmix, BF16, after=[token])
    swap_o, tok_o = swap_halves("dw_o", [dw_o.reshape(N_DEV, d // N_DEV, d)])

    def mix_bwd(dm, av, yv, gl, bg, ga, gs):
        dattn_, dga = _rms_bwd(av, ga, dm[:, :d_attn])
        z = _gelu(yv)
        sg = _sigmoid(gl + bg)
        dssm, dgs = _rms_bwd(z * sg, gs, dm[:, d_attn:])
        dgl = dssm * z * sg * (1.0 - sg)
        return [dattn_, dssm * sg, dgl], [dga, dgs, jnp.sum(dgl, axis=0, keepdims=True)]

    dattn, dz_direct, dglu, dg_attn_out, dg_ssm_out, db_glu = _rows(
        "mix_bwd", mix_bwd, [dmixed, attn, y_ssm, glu_lin], [b_glu, g_attn_out, g_ssm_out],
        [(d_attn, F32), (d_ssm, F32), (d_ssm, BF16)], [d_attn, d_ssm, d_ssm], tm, after=[tok_o])
    dz_glu = _mm_nt("glu_gate_dx", dglu, w_glu_full, F32)
    dw_glu = _mm_tn("glu_gate_dw", y_ssm, dglu, BF16, a_fn=gelu_bf16)
    rs_o, token = scatter_chip_sums("dw_o", swap_o, [dz_glu, dw_glu])

    du, db_re_dense, db_im_dense, dlam_re, dlam_im, dc_re_dense, dc_im_dense, dd_skip = _s5_bwd(
        proj, mats, dskip_row, y_ssm, dz_direct, dz_glu, d_attn, d_ssm, [token])
    da_re, da_im, dlog_dt, db_re_v, db_im_v = _s5_discretise_bwd(
        a_re[0], a_im[0], ldt_col, b_re_v, b_im_v, dlam_re.reshape(n_groups, SSM_STATE),
        dlam_im.reshape(n_groups, SSM_STATE), _block_diag_in_t(db_re_dense), _block_diag_in_t(db_im_dense))
    dq, dk2, dv2, dsinks_row = _attention_bwd(proj, sinks_row, dattn, d_attn)

    small_grads = {
        "sinks": dsinks_row, "a_re": da_re, "a_im": da_im, "log_dt": dlog_dt.reshape(1, n_groups),
        "b_re": db_re_v, "b_im": db_im_v, "c_re": _block_diag_out_t(dc_re_dense),
        "c_im": _block_diag_out_t(dc_im_dense), "d_skip": dd_skip.reshape(n_groups, SSM_GROUP).T, "b_glu": db_glu,
        "g_attn_out": dg_attn_out, "g_ssm_out": dg_ssm_out, "g_post_mix": dg_post_mix, "g_pre_ffn": dg_pre_ffn,
        "g_post_ffn": dg_post_ffn,
    }
    vec_pack, vec_slots, mat_pack, mat_rows = _pack_grads([small_grads[n] for n in _SMALL_VECTORS],
                                                          [small_grads[n] for n in _SMALL_MATRICES])
    ag_small, token = _exchange_start("gather_small_grads", [vec_pack, mat_pack, small_grads["d_skip"]], False,
                                      (OWN,) + ALL_PEERS)
    dproj = _assemble_dproj(dq, dk2, dv2, du, cos, sin, d_in, [token])

    dxn = _mm_nn("proj_in_dx", dproj, w_in_t, F32)
    dw_in = _mm_tn("proj_in_dw", dproj, xn, BF16).reshape(N_DEV, d_in // N_DEV, d)
    swap_in, token = swap_halves("dw_in_glu", [dw_in, dw_glu.reshape(N_DEV, d_ssm // N_DEV, d_ssm)])

    def x_bwd(dh_, dxn_, xv, g):
        dx, dg = _rms_bwd(xv, g, dxn_)
        return [dh_ + dx], [dg]

    grad_x, dg_pre_mix = _rows("norm_in_bwd", x_bwd, [dh, dxn, x2], [g_pre_mix], [(d, F32)], [d], tm, after=[token])
    ag_last, token = _exchange_start("gather_g_pre_mix_grad", [dg_pre_mix], False, (OWN,) + ALL_PEERS)
    rs_in, token = scatter_chip_sums("dw_in_glu", swap_in, [grad_x, token])

    results = {}

    def adam_big(n, parts):
        r = parts.shape[1]
        results[n] = _adam_sharded("adam_" + n, parts, by_rows(n, weights[n]), by_rows(n, mom_m[n]),
                                   by_rows(n, mom_v[n]), 64 if r % 64 == 0 else r)
        return results[n][3]

    done = [grad_x, token]
    adam_big("w_down", _exchange_wait(rs_down, done)[0])
    p_gate, p_up = _exchange_wait(rs_ffn_in, done)
    done = [adam_big("w_gate", p_gate), adam_big("w_up", p_up), results["w_down"][3]]
    done = [adam_big("w_o", _exchange_wait(rs_o, done)[0])]
    vec_parts, mat_parts, dskip_parts = _exchange_wait(ag_small, done)
    first_gain_parts, = _exchange_wait(ag_last, done)
    for n, row0 in zip(_SMALL_MATRICES, mat_rows):
        rows = view(n, weights[n]).shape[0]
        results[n] = _adam_sharded("adam_" + n, mat_parts, view(n, weights[n]), view(n, mom_m[n]), view(n, mom_v[n]),
                                   rows, row0)
    rest = _SMALL_VECTORS + ("d_skip", "g_pre_mix")
    found_at = [(0, row, lane) for row, lane, _ in vec_slots] + [(1, 0, 0), (2, 0, 0)]
    updated = _adam_replicated([vec_parts, dskip_parts, first_gain_parts], found_at,
                               [view(n, weights[n]) for n in rest], [view(n, mom_m[n]) for n in rest],
                               [view(n, mom_v[n]) for n in rest])
    results.update(zip(rest, updated))
    p_in, p_glu = _exchange_wait(rs_in, [results[n][3] for n in _SMALL_MATRICES] + [updated[0][3]])
    adam_big("w_in", p_in)
    adam_big("w_glu", p_glu)

    loss = lax.psum(loss_row[0, 0], ("x", "y", "c"))
    outs = [loss, grad_x[None]]
    for k in range(4):
        for n in _ORDER:
            val = results[n][k]
            outs.append(val.T[None] if n in _BY_COLUMNS else val[None] if n in _BIG else unview(n, val))
    return tuple(outs)
```

```python
import math

import jax
import jax.numpy as jnp
from jax import lax
from jax.experimental import pallas as pl
from jax.experimental.pallas import tpu as pltpu

F32 = jnp.float32
BF16 = jnp.bfloat16

HEAD_DIM = 64
N_KV_HEADS = 4
D_KV = N_KV_HEADS * HEAD_DIM
WINDOW = 128
BLOCK = 128
ROPE_THETA = 10000.0
SSM_GROUP = 16
SSM_STATE = 64
GROUPS_PER_BLOCK = 8
SSM_CH_BLOCK = GROUPS_PER_BLOCK * SSM_GROUP
SSM_ST_BLOCK = GROUPS_PER_BLOCK * SSM_STATE
RMS_EPS = 1e-6
N_DEV = 8
LANES = 128
SUBLANES = 8
MASKED = -1e30

ADAM_LR = 0.001
ADAM_B1 = 0.9
ADAM_B2 = 0.999
ADAM_EPS = 1e-08
ADAM_WD = 0.01
ADAM_STEP = 10

VMEM_LIMIT_BYTES = 56 * 1024 * 1024


def _call(body, *, name, out_shape, in_specs, out_specs, grid=(), scratch_shapes=(), semantics=None, n_after=0):
    params = dict(vmem_limit_bytes=VMEM_LIMIT_BYTES)
    if semantics is not None:
        params["dimension_semantics"] = semantics
    n_in = len(in_specs)
    if n_after:
        inner = body

        def body(*refs):
            inner(*refs[:n_in], *refs[n_in + n_after:])

        in_specs = list(in_specs) + [pl.BlockSpec(memory_space=pl.ANY)] * n_after
    return pl.pallas_call(body, name=name, grid=grid, in_specs=in_specs, out_specs=out_specs, out_shape=out_shape,
                          scratch_shapes=scratch_shapes, compiler_params=pltpu.CompilerParams(**params))


def _sds(shape, dtype):
    return jax.ShapeDtypeStruct(tuple(shape), dtype)


def _dot(a, b, ca, cb):
    return lax.dot_general(a, b, (((ca,), (cb,)), ((), ())), preferred_element_type=F32)


def _rms(x):
    r = lax.rsqrt(jnp.mean(x * x, axis=-1, keepdims=True) + RMS_EPS)
    return x * r, r


def _rms_bwd(x, g, dy):
    xh, r = _rms(x)
    dxh = dy * g
    dx = r * (dxh - xh * jnp.mean(dxh * xh, axis=-1, keepdims=True))
    return dx, jnp.sum(dy * xh, axis=0, keepdims=True)


def _sigmoid(x):
    return 1.0 / (1.0 + jnp.exp(-x))


_GELU_C = math.sqrt(2.0 / math.pi)
_GELU_A = 0.044715


def _gelu(y):
    t = jnp.tanh(_GELU_C * (y + _GELU_A * y * y * y))
    return 0.5 * y * (1.0 + t)


def _gelu_grad(y):
    t = jnp.tanh(_GELU_C * (y + _GELU_A * y * y * y))
    return 0.5 * (1.0 + t) + 0.5 * y * (1.0 - t * t) * _GELU_C * (1.0 + 3.0 * _GELU_A * y * y)


def _rows(name, fn, row_ins, vec_ins, row_outs, acc_widths, tm, after=()):
    rows = row_ins[0].shape[0]
    assert rows % tm == 0, (name, rows, tm)
    n_row, n_vec, n_out, n_acc = len(row_ins), len(vec_ins), len(row_outs), len(acc_widths)

    def body(*refs):
        ins = [r[...] for r in refs[:n_row + n_vec]]
        outs = refs[n_row + n_vec:n_row + n_vec + n_out]
        accs = refs[n_row + n_vec + n_out:]
        row_vals, acc_vals = fn(*ins)
        for o, v in zip(outs, row_vals):
            o[...] = v.astype(o.dtype)
        if n_acc:
            @pl.when(pl.program_id(0) == 0)
            def _():
                for a in accs:
                    a[...] = jnp.zeros_like(a)
            for a, v in zip(accs, acc_vals):
                a[...] += v

    in_specs = [pl.BlockSpec((tm, a.shape[1]), lambda i: (i, 0)) for a in row_ins]
    in_specs += [pl.BlockSpec(v.shape, lambda i: (0, 0)) for v in vec_ins]
    out_specs = [pl.BlockSpec((tm, w), lambda i: (i, 0)) for w, _ in row_outs]
    out_specs += [pl.BlockSpec((1, w), lambda i: (0, 0)) for w in acc_widths]
    out_shape = [_sds((rows, w), dt) for w, dt in row_outs] + [_sds((1, w), F32) for w in acc_widths]
    return _call(body, name=name, grid=(rows // tm,), in_specs=in_specs, out_specs=out_specs, out_shape=out_shape,
                 semantics=("arbitrary",) if n_acc else ("parallel",), n_after=len(after))(*row_ins, *vec_ins, *after)


def _matmul(name, operands, in_specs, product, grid, out_shape, out_spec, acc_shape, after=()):
    nk = grid[-1]
    n_in = len(operands)
    in_place = out_shape.dtype == F32

    def body(*refs):
        ins = [r[...] for r in refs[:n_in]]
        o_ref = refs[n_in]
        if nk == 1:
            o_ref[...] = product(*ins).astype(o_ref.dtype)
            return
        acc = o_ref if in_place else refs[n_in + 1]
        k = pl.program_id(len(grid) - 1)

        @pl.when(k == 0)
        def _():
            acc[...] = jnp.zeros_like(acc)

        acc[...] += product(*ins)

        if not in_place:
            @pl.when(k == nk - 1)
            def _():
                o_ref[...] = acc[...].astype(o_ref.dtype)

    return _call(body, name=name, grid=grid, in_specs=in_specs, out_specs=out_spec, out_shape=out_shape,
                 scratch_shapes=[] if nk == 1 or in_place else [pltpu.VMEM(acc_shape, F32)],
                 semantics=("parallel",) * (len(grid) - 1) + ("arbitrary",), n_after=len(after))(*operands, *after)


def _mm_nn(name, a, b, out_dtype, tm=512, tn=None, a_fn=lambda x: x):
    m, k = a.shape
    n = b.shape[1]
    tm, tn = min(tm, m), n if tn is None else tn
    return _matmul(name, [a, b],
                   [pl.BlockSpec((tm, k), lambda i, j, s: (i, 0)), pl.BlockSpec((k, tn), lambda i, j, s: (0, j))],
                   lambda x, y: _dot(a_fn(x), y, 1, 0), (m // tm, n // tn, 1), _sds((m, n), out_dtype),
                   pl.BlockSpec((tm, tn), lambda i, j, s: (i, j)), (tm, tn))


def _mm_nt(name, a, b, out_dtype, tm=512, tn=None):
    m, k = a.shape
    n = b.shape[0]
    tm, tn = min(tm, m), n if tn is None else tn
    return _matmul(name, [a, b],
                   [pl.BlockSpec((tm, k), lambda i, j, s: (i, 0)), pl.BlockSpec((tn, k), lambda i, j, s: (j, 0))],
                   lambda x, y: _dot(x, y, 1, 1), (m // tm, n // tn, 1), _sds((m, n), out_dtype),
                   pl.BlockSpec((tm, tn), lambda i, j, s: (i, j)), (tm, tn))


def _mm_tn(name, a, b, out_dtype, tm=512, tn=None, tk=2048, a_fn=lambda x: x, after=()):
    k, m = a.shape
    n = b.shape[1]
    tm, tk, tn = min(tm, m), min(tk, k), n if tn is None else tn
    return _matmul(name, [a, b],
                   [pl.BlockSpec((tk, tm), lambda i, j, s: (s, i)), pl.BlockSpec((tk, tn), lambda i, j, s: (s, j))],
                   lambda x, y: _dot(a_fn(x), y, 0, 0), (m // tm, n // tn, k // tk), _sds((m, n), out_dtype),
                   pl.BlockSpec((tm, tn), lambda i, j, s: (i, j)), (tm, tn), after)


def _mm_contract_slots(name, pairs, out_dtype, per_step, tm=512, tn=2048, after=()):
    s_, m, k = pairs[0][0].shape
    n = pairs[0][1].shape[2]
    tm, tn = min(tm, m), min(tn, n)
    ops, specs = [], []
    for a, b in pairs:
        ops += [a, b]
        specs += [pl.BlockSpec((per_step, tm, k), lambda i, j, s: (s, i, 0)),
                  pl.BlockSpec((per_step, k, tn), lambda i, j, s: (s, 0, j))]

    def product(*t):
        return sum(_dot(t[2 * p][q], t[2 * p + 1][q], 1, 0) for p in range(len(pairs)) for q in range(per_step))

    return _matmul(name, ops, specs, product, (m // tm, n // tn, s_ // per_step), _sds((m, n), out_dtype),
                   pl.BlockSpec((tm, tn), lambda i, j, s: (i, j)), (tm, tn), after)


def _mm_slots_tn(name, a, b, out_dtype, tn=2048, tk=2048):
    s_, k, m = a.shape
    n = b.shape[1]
    tn, tk = min(tn, n), min(tk, k)
    return _matmul(name, [a, b],
                   [pl.BlockSpec((None, tk, m), lambda s, j, z: (s, z, 0)), pl.BlockSpec((tk, tn), lambda s, j, z: (z, j))],
                   lambda x, y: _dot(x, y, 0, 0), (s_, n // tn, k // tk), _sds((s_, m, n), out_dtype),
                   pl.BlockSpec((None, m, tn), lambda s, j, z: (s, 0, j)), (m, tn))


def _ffn_in(a, w_gate, w_up, tm=512):
    m, k = a.shape
    s_, n, _ = w_gate.shape
    tm = min(tm, m)

    def body(a_ref, wg_ref, wu_ref, g_ref, u_ref, h_ref):
        x = a_ref[...]
        g = _dot(x, wg_ref[...], 1, 1)
        u = _dot(x, wu_ref[...], 1, 1)
        g_ref[...] = g.astype(BF16)
        u_ref[...] = u.astype(BF16)
        h_ref[...] = (g * _sigmoid(g) * u).astype(BF16)

    w_spec = pl.BlockSpec((None, n, k), lambda s, i: (s, 0, 0))
    o_spec = pl.BlockSpec((None, tm, n), lambda s, i: (s, i, 0))
    return _call(body, name="ffn_in", grid=(s_, m // tm),
                 in_specs=[pl.BlockSpec((tm, k), lambda s, i: (i, 0)), w_spec, w_spec], out_specs=[o_spec] * 3,
                 out_shape=[_sds((s_, m, n), BF16)] * 3, semantics=("parallel", "parallel"))(a, w_gate, w_up)


def _ffn_down_bwd(d_out, w_down, gate, up, after, tm=512):
    m, k = d_out.shape
    s_, n, _ = w_down.shape
    tm = min(tm, m)

    def body(d_ref, w_ref, g_ref, u_ref, dg_ref, du_ref):
        rows = pl.ds(pl.multiple_of(pl.program_id(1) * tm, tm), tm)
        dh = _dot(d_ref[rows, :], w_ref[...], 1, 1)
        g = g_ref[...].astype(F32)
        sg = _sigmoid(g)
        dg_ref[...] = (dh * u_ref[...].astype(F32) * sg * (1.0 + g * (1.0 - sg))).astype(BF16)
        du_ref[...] = (dh * g * sg).astype(BF16)

    t_spec = pl.BlockSpec((None, tm, n), lambda s, i: (s, i, 0))
    return _call(body, name="ffn_down_dx", grid=(s_, m // tm),
                 in_specs=[pl.BlockSpec((m, k), lambda s, i: (0, 0)), pl.BlockSpec((None, n, k), lambda s, i: (s, 0, 0)),
                           t_spec, t_spec],
                 out_specs=[t_spec] * 2, out_shape=[_sds((s_, m, n), BF16)] * 2, semantics=("parallel", "parallel"),
                 n_after=len(after))(d_out, w_down, gate, up, *after)


ALL_PEERS = (1, 2, 3, 4, 5, 6, 7)
CHIP_PEERS = (2, 4, 6)
SIBLING = 1
OWN = 0


def _peer(relation):
    x, y, c = lax.axis_index("x"), lax.axis_index("y"), lax.axis_index("c")
    pos = (1 - x if relation & 4 else x, 1 - y if relation & 2 else y, 1 - c if relation & 1 else c)
    return pos, 4 * pos[0] + 2 * pos[1] + pos[2]


def _slot(relation, by_chip):
    pos, device = _peer(relation)
    return 2 * pos[0] + pos[1] if by_chip else device


def _exchange_copies(ins, lands, send_sems, recv_sems, scatter, relations, by_chip=False):
    me = _slot(0, by_chip)

    def copy(a, s, peer, pos, dst_slot):
        return pltpu.make_async_remote_copy(
            src_ref=ins[a].at[peer] if scatter else ins[a], dst_ref=lands[a].at[dst_slot],
            send_sem=send_sems.at[s], recv_sem=recv_sems.at[s], device_id=pos, device_id_type=pl.DeviceIdType.MESH)

    pairs = []
    for k, r in enumerate(relations):
        pos, peer = _peer(r)[0], _slot(r, by_chip)
        for a in range(len(ins)):
            s = a * len(relations) + k
            pairs.append((copy(a, s, peer, pos, me), copy(a, s, peer, pos, peer)))
    return pairs


def _halves_copies(arrays, lands, send_sems, recv_sems):
    sibling, _ = _peer(SIBLING)
    core = lax.axis_index("c")
    pairs = []
    for a, (ref, land) in enumerate(zip(arrays, lands)):
        send = pltpu.make_async_remote_copy(
            src_ref=ref.at[:, pl.ds(1 - core, 1)], dst_ref=land, send_sem=send_sems.at[a], recv_sem=recv_sems.at[a],
            device_id=sibling, device_id_type=pl.DeviceIdType.MESH)
        pairs.append((send, send))
    return pairs


def _forward_copies(lands, send_sems, recv_sems):
    sibling, _ = _peer(SIBLING)

    def copy(a, s, slot):
        return pltpu.make_async_remote_copy(
            src_ref=lands[a].at[slot], dst_ref=lands[a].at[slot], send_sem=send_sems.at[s], recv_sem=recv_sems.at[s],
            device_id=sibling, device_id_type=pl.DeviceIdType.MESH)

    pairs = []
    for k, r in enumerate(CHIP_PEERS):
        _, mine = _peer(r)
        _, theirs = _peer(r | SIBLING)
        for a in range(len(lands)):
            s = a * len(CHIP_PEERS) + k
            pairs.append((copy(a, s, mine), copy(a, s, theirs)))
    return pairs


_HBM_SPEC = pl.BlockSpec(memory_space=pltpu.HBM)
_SEM_SPEC = pl.BlockSpec(memory_space=pltpu.SEMAPHORE)
_SIDE_EFFECT = pltpu.SideEffectType.DATAFLOW_SIDE_EFFECTING


def _split_start(name, operands, n_sem, make_pairs):
    k = len(operands)

    def body(*refs):
        send_sems, recv_sems, token = refs[k], refs[k + 1], refs[-1]
        for send, _ in make_pairs(refs[:k], send_sems, recv_sems):
            send.start()
        token[...] = jnp.zeros_like(token)

    out = pl.pallas_call(
        body, name=name,
        out_shape=(pltpu.SemaphoreType.DMA((n_sem,)), pltpu.SemaphoreType.DMA((n_sem,)),
                   *[pltpu.HBM(a.shape, a.dtype) for a in operands], _sds((SUBLANES, LANES), F32)),
        in_specs=[_HBM_SPEC] * k,
        out_specs=(_SEM_SPEC, _SEM_SPEC, *[_HBM_SPEC] * k, pl.BlockSpec(memory_space=pltpu.VMEM)),
        input_output_aliases={i: 2 + i for i in range(k)},
        compiler_params=pltpu.CompilerParams(has_side_effects=_SIDE_EFFECT),
    )(*[pltpu.with_memory_space_constraint(a, pltpu.HBM) for a in operands])
    return dict(name=name, sems=out[:2], thru=list(out[2:2 + k]), make_pairs=make_pairs), out[-1]


def _split_wait(handle, after):
    thru, make_pairs = handle["thru"], handle["make_pairs"]
    k = len(thru)

    def body(*refs):
        for send, arrival in make_pairs(refs[:k], refs[k], refs[k + 1]):
            send.wait_send()
            arrival.wait_recv()

    return pl.pallas_call(
        body, name=handle["name"] + "_wait", out_shape=[pltpu.HBM(a.shape, a.dtype) for a in thru],
        in_specs=[_HBM_SPEC] * k + [_SEM_SPEC, _SEM_SPEC] + [pl.BlockSpec(memory_space=pl.ANY)] * len(after),
        out_specs=[_HBM_SPEC] * k, input_output_aliases={i: i for i in range(k)},
        compiler_params=pltpu.CompilerParams(has_side_effects=_SIDE_EFFECT),
    )(*thru, *handle["sems"], *after)


def _exchange_start(name, arrays, scatter, relations, by_chip=False):
    n = len(arrays)
    lands = [lax.empty(a.shape if scatter else (N_DEV,) + a.shape, a.dtype) for a in arrays]

    def make_pairs(refs, send_sems, recv_sems):
        return _exchange_copies(refs[:n], refs[n:], send_sems, recv_sems, scatter, relations, by_chip)

    handle, token = _split_start(name, list(arrays) + lands, n * len(relations), make_pairs)
    handle.update(n=n)
    return handle, token


def _halves_start(name, arrays):
    lands = [lax.empty((a.shape[0], 1) + a.shape[2:], a.dtype) for a in arrays]
    n = len(arrays)

    def make_pairs(refs, send_sems, recv_sems):
        return _halves_copies(refs[:n], refs[n:], send_sems, recv_sems)

    return _split_start(name, list(arrays) + lands, n, make_pairs)


def _chip_sum(name, array, landed):
    chips, _, r, c = array.shape
    tr = r // 2 if r > 512 and r % 32 == 0 else r

    def body(a_ref, b_ref, o_ref):
        mine = a_ref[lax.axis_index("c")].astype(F32)
        o_ref[...] = (mine + b_ref[...].astype(F32)).astype(o_ref.dtype)

    return _call(body, name=name, grid=(chips, r // tr),
                 in_specs=[pl.BlockSpec((None, 2, tr, c), lambda k, i: (k, 0, i, 0)),
                           pl.BlockSpec((None, None, tr, c), lambda k, i: (k, 0, i, 0))],
                 out_specs=pl.BlockSpec((None, tr, c), lambda k, i: (k, i, 0)),
                 out_shape=_sds((chips, r, c), BF16), semantics=("parallel", "parallel"))(array, landed)


def _forward_start(name, lands):
    return _split_start(name, list(lands), len(lands) * len(CHIP_PEERS), _forward_copies)


def _exchange_wait(handle, after):
    return _split_wait(handle, after)[handle["n"]:]


def _rope_tables(pos_col):
    t = pos_col.shape[0]
    half = HEAD_DIM // 2
    inv_freq = ROPE_THETA ** (-jnp.arange(half, dtype=F32) / half)
    inv_row = jnp.tile(inv_freq, LANES // half)[None, :]

    def body(pos_ref, inv_ref, cos_ref, sin_ref):
        ang = pos_ref[...] * inv_ref[...]
        cos_ref[...] = jnp.cos(ang)
        sin_ref[...] = jnp.sin(ang)

    tm = min(t, 512)
    return _call(body, name="rope_tables", grid=(t // tm,),
                 in_specs=[pl.BlockSpec((tm, 1), lambda i: (i, 0)), pl.BlockSpec((1, LANES), lambda i: (0, 0))],
                 out_specs=[pl.BlockSpec((tm, LANES), lambda i: (i, 0))] * 2,
                 out_shape=[_sds((t, LANES), F32)] * 2, semantics=("parallel",))(pos_col, inv_row)


def _rot_half(x):
    lane = lax.broadcasted_iota(jnp.int32, x.shape, 1)
    low = (lane % HEAD_DIM) < HEAD_DIM // 2
    return jnp.where(low, -pltpu.roll(x, LANES - HEAD_DIM // 2, 1), pltpu.roll(x, HEAD_DIM // 2, 1))


def _rope(x, cos, sin):
    return x * cos + _rot_half(x) * sin


def _unrope(d, cos, sin):
    return d * cos - _rot_half(d) * sin


def _band_mask(first_block, heads):
    r = lax.broadcasted_iota(jnp.int32, (heads * BLOCK, 2 * BLOCK), 0) % BLOCK
    c = lax.broadcasted_iota(jnp.int32, (heads * BLOCK, 2 * BLOCK), 1)
    diff = r - c + BLOCK
    return (diff >= 0) & (diff < WINDOW) & ((c >= BLOCK) | jnp.logical_not(first_block))


def _attn_specs(t, d_attn, d_in):
    kb, vb = d_attn // D_KV, d_attn // D_KV + 1
    prev = lambda i: jnp.maximum(i - 1, 0)
    return [
        pl.BlockSpec((BLOCK, d_attn), lambda i: (i, 0)),
        pl.BlockSpec((BLOCK, D_KV), lambda i: (i, kb)),
        pl.BlockSpec((BLOCK, D_KV), lambda i: (i, vb)),
        pl.BlockSpec((BLOCK, D_KV), lambda i: (prev(i), kb)),
        pl.BlockSpec((BLOCK, D_KV), lambda i: (prev(i), vb)),
        pl.BlockSpec((BLOCK, LANES), lambda i: (i, 0)),
        pl.BlockSpec((BLOCK, LANES), lambda i: (i, 0)),
        pl.BlockSpec((BLOCK, LANES), lambda i: (prev(i), 0)),
        pl.BlockSpec((BLOCK, LANES), lambda i: (prev(i), 0)),
        pl.BlockSpec((1, LANES), lambda i: (0, 0)),
    ]


def _head(x, h):
    return x[:, h * HEAD_DIM:(h + 1) * HEAD_DIM]


def _attn_heads(q_ref, kc_ref, vc_ref, kp_ref, vp_ref, cq_ref, sq_ref, cp_ref, sp_ref, d_attn):
    cq, sq, cp, sp = cq_ref[...], sq_ref[...], cp_ref[...], sp_ref[...]
    q_rot = [_rope(q_ref[:, j * LANES:(j + 1) * LANES], cq, sq) for j in range(d_attn // LANES)]
    kc_rot = [_rope(kc_ref[:, j * LANES:(j + 1) * LANES], cq, sq) for j in range(D_KV // LANES)]
    kp_rot = [_rope(kp_ref[:, j * LANES:(j + 1) * LANES], cp, sp) for j in range(D_KV // LANES)]
    per = LANES // HEAD_DIM
    q_heads = [_head(q_rot[h // per], h % per).astype(BF16) for h in range(d_attn // HEAD_DIM)]
    kk = [jnp.concatenate([_head(kp_rot[g // per], g % per), _head(kc_rot[g // per], g % per)], axis=0).astype(BF16)
          for g in range(N_KV_HEADS)]
    vv = [jnp.concatenate([_head(vp_ref[...], g), _head(vc_ref[...], g)], axis=0).astype(BF16) for g in range(N_KV_HEADS)]
    return q_heads, kk, vv


def _stack_group(q_heads, sink_ref, group):
    q_all = jnp.concatenate([q_heads[h] for h in group], axis=0)
    sink_all = jnp.concatenate([jnp.broadcast_to(sink_ref[:, h:h + 1], (BLOCK, 1)) for h in group], axis=0)
    return q_all, sink_all


def _softmax_with_sink(q, kk, sink, mask):
    s = _dot(q, kk, 1, 1) * (1.0 / math.sqrt(HEAD_DIM))
    s = jnp.where(mask, s, MASKED)
    m = jnp.maximum(jnp.max(s, axis=-1, keepdims=True), sink)
    p = jnp.exp(s - m)
    e_sink = jnp.exp(sink - m)
    inv = 1.0 / (jnp.sum(p, axis=-1, keepdims=True) + e_sink)
    return p * inv, e_sink * inv


def _attention_fwd(proj, cos, sin, sinks_row, d_attn):
    t, d_in = proj.shape
    n_heads = d_attn // HEAD_DIM
    q_per_kv = n_heads // N_KV_HEADS

    def body(q_ref, kc_ref, vc_ref, kp_ref, vp_ref, cq_ref, sq_ref, cp_ref, sp_ref, sink_ref, o_ref):
        mask = _band_mask(pl.program_id(0) == 0, q_per_kv)
        q_heads, kk, vv = _attn_heads(q_ref, kc_ref, vc_ref, kp_ref, vp_ref, cq_ref, sq_ref, cp_ref, sp_ref, d_attn)
        for g in range(N_KV_HEADS):
            group = range(g * q_per_kv, (g + 1) * q_per_kv)
            q_all, sink_all = _stack_group(q_heads, sink_ref, group)
            probs, _ = _softmax_with_sink(q_all, kk[g], sink_all, mask)
            o_all = _dot(probs.astype(BF16), vv[g], 1, 0)
            for k, h in enumerate(group):
                o_ref[:, h * HEAD_DIM:(h + 1) * HEAD_DIM] = o_all[k * BLOCK:(k + 1) * BLOCK]

    return _call(body, name="attention_fwd", grid=(t // BLOCK,), in_specs=_attn_specs(t, d_attn, d_in),
                 out_specs=pl.BlockSpec((BLOCK, d_attn), lambda i: (i, 0)), out_shape=_sds((t, d_attn), F32),
                 semantics=("parallel",))(proj, proj, proj, proj, proj, cos, sin, cos, sin, sinks_row)


def _attention_bwd(proj, cos, sin, sinks_row, d_out, d_attn):
    t, d_in = proj.shape
    n_heads = d_attn // HEAD_DIM
    q_per_kv = n_heads // N_KV_HEADS
    nb = t // BLOCK
    per = LANES // HEAD_DIM

    def body(q_ref, kc_ref, vc_ref, kp_ref, vp_ref, cq_ref, sq_ref, cp_ref, sp_ref, sink_ref, do_ref,
             dq_ref, dk_ref, dv_ref, dsink_ref):
        i = pl.program_id(0)
        mask = _band_mask(i == 0, q_per_kv)
        q_heads, kk, vv = _attn_heads(q_ref, kc_ref, vc_ref, kp_ref, vp_ref, cq_ref, sq_ref, cp_ref, sp_ref, d_attn)
        lane = lax.broadcasted_iota(jnp.int32, (1, LANES), 1)
        dsink = jnp.zeros((1, LANES), F32)
        dq_rot, dkk, dvv = [], [], []
        for g in range(N_KV_HEADS):
            group = range(g * q_per_kv, (g + 1) * q_per_kv)
            q_all, sink_all = _stack_group(q_heads, sink_ref, group)
            probs, p_sink = _softmax_with_sink(q_all, kk[g], sink_all, mask)
            do_all = jnp.concatenate([do_ref[:, h * HEAD_DIM:(h + 1) * HEAD_DIM] for h in group], axis=0).astype(BF16)
            dp = _dot(do_all, vv[g], 1, 1)
            delta = jnp.sum(probs * dp, axis=-1, keepdims=True)
            ds = (probs * (dp - delta) * (1.0 / math.sqrt(HEAD_DIM))).astype(BF16)
            dq_all = _dot(ds, kk[g], 1, 0)
            dkk.append(_dot(ds, q_all, 0, 0))
            dvv.append(_dot(probs.astype(BF16), do_all, 0, 0))
            sink_term = p_sink * delta
            for k, h in enumerate(group):
                dq_rot.append(dq_all[k * BLOCK:(k + 1) * BLOCK])
                part = jnp.sum(sink_term[k * BLOCK:(k + 1) * BLOCK], axis=0, keepdims=True)
                dsink += jnp.where(lane == h, -part, 0.0)
        cq, sq, cp, sp = cq_ref[...], sq_ref[...], cp_ref[...], sp_ref[...]
        for j in range(d_attn // LANES):
            d = jnp.concatenate(dq_rot[j * per:(j + 1) * per], axis=1)
            dq_ref[:, j * LANES:(j + 1) * LANES] = _unrope(d, cq, sq)
        for j in range(D_KV // LANES):
            d = jnp.concatenate(dkk[j * per:(j + 1) * per], axis=1)
            dk_ref[0, :, j * LANES:(j + 1) * LANES] = _unrope(d[:BLOCK], cp, sp)
            dk_ref[1, :, j * LANES:(j + 1) * LANES] = _unrope(d[BLOCK:], cq, sq)
            d = jnp.concatenate(dvv[j * per:(j + 1) * per], axis=1)
            dv_ref[0, :, j * LANES:(j + 1) * LANES] = d[:BLOCK]
            dv_ref[1, :, j * LANES:(j + 1) * LANES] = d[BLOCK:]

        @pl.when(i == 0)
        def _():
            dsink_ref[...] = jnp.zeros_like(dsink_ref)

        dsink_ref[...] += dsink

    pair = pl.BlockSpec((2, BLOCK, D_KV), lambda i: (i, 0, 0))
    return _call(body, name="attention_bwd", grid=(nb,),
                 in_specs=_attn_specs(t, d_attn, d_in) + [pl.BlockSpec((BLOCK, d_attn), lambda i: (i, 0))],
                 out_specs=[pl.BlockSpec((BLOCK, d_attn), lambda i: (i, 0)), pair, pair,
                            pl.BlockSpec((1, LANES), lambda i: (0, 0))],
                 out_shape=[_sds((t, d_attn), F32), _sds((2 * nb, BLOCK, D_KV), F32), _sds((2 * nb, BLOCK, D_KV), F32),
                            _sds((1, LANES), F32)],
                 semantics=("arbitrary",))(proj, proj, proj, proj, proj, cos, sin, cos, sin, sinks_row, d_out)


def _assemble_dproj(dq, dk2, dv2, du, d_in, after):
    t, d_attn = dq.shape
    d_ssm = du.shape[1]
    nb = t // BLOCK

    def body(dq_ref, dk_own, dk_next, dv_own, dv_next, du_ref, o_ref):
        has_next = (pl.program_id(0) < nb - 1).astype(F32)
        o_ref[:, :d_attn] = dq_ref[...].astype(BF16)
        o_ref[:, d_attn:d_attn + D_KV] = (dk_own[...] + has_next * dk_next[...]).astype(BF16)
        o_ref[:, d_attn + D_KV:d_attn + 2 * D_KV] = (dv_own[...] + has_next * dv_next[...]).astype(BF16)
        o_ref[:, d_attn + 2 * D_KV:] = du_ref[...].astype(BF16)

    own = pl.BlockSpec((None, BLOCK, D_KV), lambda i: (2 * i + 1, 0, 0))
    nxt = pl.BlockSpec((None, BLOCK, D_KV), lambda i: (jnp.minimum(2 * i + 2, 2 * nb - 1), 0, 0))
    return _call(body, name="assemble_dproj", grid=(nb,),
                 in_specs=[pl.BlockSpec((BLOCK, d_attn), lambda i: (i, 0)), own, nxt, own, nxt,
                           pl.BlockSpec((BLOCK, d_ssm), lambda i: (i, 0))],
                 out_specs=pl.BlockSpec((BLOCK, d_in), lambda i: (i, 0)), out_shape=_sds((t, d_in), BF16),
                 semantics=("parallel",), n_after=len(after))(dq, dk2, dk2, dv2, dv2, du, *after)


def _discretise(ar, ai, ldt, br, bi):
    dt = jnp.exp(ldt)
    mag = jnp.exp(ar * dt)
    lam_re = mag * jnp.cos(ai * dt)
    lam_im = mag * jnp.sin(ai * dt)
    den = ar * ar + ai * ai
    nr = lam_re - 1.0
    ni = lam_im
    f_re = (nr * ar + ni * ai) / den
    f_im = (ni * ar - nr * ai) / den
    return (lam_re, lam_im, [f_re * r - f_im * i for r, i in zip(br, bi)], [f_re * i + f_im * r for r, i in zip(br, bi)])


def _whole(arrays):
    return [pl.BlockSpec(a.shape, lambda *_, nd=len(a.shape): (0,) * nd) for a in arrays]


def _channels(ref):
    groups = ref.shape[0] // SSM_GROUP
    return [ref[pl.ds(p, groups, stride=SSM_GROUP), :] for p in range(SSM_GROUP)]


def _store_channels(ref, values):
    groups = ref.shape[0] // SSM_GROUP
    for p, val in enumerate(values):
        ref[pl.ds(p, groups, stride=SSM_GROUP), :] = val


def _s5_discretise(ar, ai, ldt, br, bi):
    ins = [ar, ai, ldt, br, bi]

    def body(ar_ref, ai_ref, ldt_ref, br_ref, bi_ref, lr_ref, li_ref, bbr_ref, bbi_ref):
        lr, li, bbr, bbi = _discretise(ar_ref[...], ai_ref[...], ldt_ref[...], _channels(br_ref), _channels(bi_ref))
        lr_ref[...] = lr
        li_ref[...] = li
        _store_channels(bbr_ref, bbr)
        _store_channels(bbi_ref, bbi)

    outs = [_sds(ar.shape, F32), _sds(ar.shape, F32), _sds(br.shape, F32), _sds(br.shape, F32)]
    return _call(body, name="s5_discretise", in_specs=_whole(ins), out_specs=_whole(outs), out_shape=outs)(*ins)


def _s5_discretise_bwd(ar, ai, ldt, br, bi, d_lr, d_li, d_bbr, d_bbi):
    ins = [ar, ai, ldt, br, bi, d_lr, d_li, d_bbr, d_bbi]

    def body(ar_ref, ai_ref, ldt_ref, br_ref, bi_ref, dlr_ref, dli_ref, dbbr_ref, dbbi_ref,
             dar_ref, dai_ref, dldt_ref, dbr_ref, dbi_ref):
        _, vjp = jax.vjp(_discretise, ar_ref[...], ai_ref[...], ldt_ref[...], _channels(br_ref), _channels(bi_ref))
        dar, dai, dldt, dbr, dbi = vjp((dlr_ref[...], dli_ref[...], _channels(dbbr_ref), _channels(dbbi_ref)))
        dar_ref[...] = dar
        dai_ref[...] = dai
        dldt_ref[...] = dldt
        _store_channels(dbr_ref, dbr)
        _store_channels(dbi_ref, dbi)

    outs = [_sds(a.shape, F32) for a in (ar, ai, ldt, br, bi)]
    return _call(body, name="s5_discretise_bwd", in_specs=_whole(ins), out_specs=_whole(outs), out_shape=outs)(*ins)


def _cmul(ar, ai, br, bi):
    return ar * br - ai * bi, ar * bi + ai * br


def _load_segmented(ref, tile0, n_tiles, seg):
    return jnp.concatenate([ref[pl.ds(tile0 + j, SUBLANES, stride=seg), :] for j in range(n_tiles)], axis=0)


def _store_segmented(ref, tile0, seg, value):
    for j in range(value.shape[0] // SUBLANES):
        ref[pl.ds(tile0 + j, SUBLANES, stride=seg), :] = value[j * SUBLANES:(j + 1) * SUBLANES, :]


def _fill_powers(lr, li, pr_ref, pi_ref, seg):
    pows = [(lr, li)]
    for _ in range(SUBLANES - 1):
        pows.append(_cmul(pows[-1][0], pows[-1][1], lr, li))
    row = lax.broadcasted_iota(jnp.int32, (SUBLANES, lr.shape[1]), 0)
    tr = jnp.zeros((SUBLANES, lr.shape[1]), F32)
    ti = jnp.zeros((SUBLANES, lr.shape[1]), F32)
    for r in range(SUBLANES):
        tr = jnp.where(row == r, pows[r][0], tr)
        ti = jnp.where(row == r, pows[r][1], ti)
    pr_ref[0:SUBLANES, :] = tr
    pi_ref[0:SUBLANES, :] = ti
    k = SUBLANES
    while k < seg:
        fr, fi = pr_ref[k - 1:k, :], pi_ref[k - 1:k, :]
        for t0 in range(0, k, SUBLANES):
            nr, ni = _cmul(pr_ref[t0:t0 + SUBLANES, :], pi_ref[t0:t0 + SUBLANES, :], fr, fi)
            pr_ref[k + t0:k + t0 + SUBLANES, :] = nr
            pi_ref[k + t0:k + t0 + SUBLANES, :] = ni
        k *= 2


def _scan_segments(sr_ref, si_ref, pr_ref, pi_ref, lr, li, seg, reverse, per_tile=None):
    w = lr.shape[1]
    sign = -1.0 if reverse else 1.0
    lrb = jnp.broadcast_to(lr, (SUBLANES, w))
    lib = jnp.broadcast_to(sign * li, (SUBLANES, w))
    zero = jnp.zeros((SUBLANES, w), F32)

    def tile_rows(j):
        return pl.ds(pl.multiple_of(j * SUBLANES, SUBLANES), SUBLANES)

    def local(i, carry):
        rows = tile_rows(seg - 1 - i if reverse else i)
        pr, pi = _cmul(lrb, lib, carry[0], carry[1])
        xr, xi = sr_ref[rows, :] + pr, si_ref[rows, :] + pi
        sr_ref[rows, :] = xr
        si_ref[rows, :] = xi
        return xr, xi

    end_r, end_i = lax.fori_loop(0, seg, local, (zero, zero), unroll=2)
    full_r, full_i = pr_ref[seg - 1:seg, :], sign * pi_ref[seg - 1:seg, :]
    row = lax.broadcasted_iota(jnp.int32, (SUBLANES, w), 0)
    in_r, in_i = zero, zero
    cur_r, cur_i = jnp.zeros((1, w), F32), jnp.zeros((1, w), F32)
    for r in (range(SUBLANES - 2, -1, -1) if reverse else range(1, SUBLANES)):
        src = r + 1 if reverse else r - 1
        pr, pi = _cmul(full_r, full_i, cur_r, cur_i)
        cur_r, cur_i = end_r[src:src + 1, :] + pr, end_i[src:src + 1, :] + pi
        in_r = jnp.where(row == r, cur_r, in_r)
        in_i = jnp.where(row == r, cur_i, in_i)

    def carry_in(j, _):
        rows = tile_rows(j)
        k = seg - 1 - j if reverse else j
        pr, pi = _cmul(pr_ref[pl.ds(k, 1), :], sign * pi_ref[pl.ds(k, 1), :], in_r, in_i)
        xr, xi = sr_ref[rows, :] + pr, si_ref[rows, :] + pi
        sr_ref[rows, :] = xr
        si_ref[rows, :] = xi
        if per_tile is not None:
            per_tile(j, xr, xi)
        return 0

    lax.fori_loop(0, seg, carry_in, 0, unroll=4)


_S5_ROWS = 256


def _s5_in_specs(t, d_attn):
    u_block = (d_attn + 2 * D_KV) // SSM_CH_BLOCK
    blk3 = lambda shape: pl.BlockSpec((None,) + shape, lambda j: (j, 0, 0))
    return [
        pl.BlockSpec((t, SSM_CH_BLOCK), lambda j: (0, u_block + j)),
        blk3((SSM_CH_BLOCK, SSM_ST_BLOCK)), blk3((SSM_CH_BLOCK, SSM_ST_BLOCK)),
        blk3((1, SSM_ST_BLOCK)), blk3((1, SSM_ST_BLOCK)),
        blk3((SSM_ST_BLOCK, SSM_CH_BLOCK)), blk3((SSM_ST_BLOCK, SSM_CH_BLOCK)),
        pl.BlockSpec((1, SSM_CH_BLOCK), lambda j: (0, j)),
    ]


def _chunks(t):
    rows = min(_S5_ROWS, t)
    return rows, lambda i: pl.ds(pl.multiple_of(i * rows, rows), rows)


def _s5_states(u_ref, us_ref, bre_ref, bim_ref, lr_ref, li_ref, sr_ref, si_ref, pr_ref, pi_ref, t):
    seg = t // SUBLANES
    rows, chunk = _chunks(t)
    for c in range(t // rows):
        us_ref[c * rows:(c + 1) * rows, :] = _load_segmented(u_ref, c * rows // SUBLANES, rows // SUBLANES, seg)

    def fill(i, _):
        ub = us_ref[chunk(i), :].astype(BF16)
        sr_ref[chunk(i), :] = _dot(ub, bre_ref[...], 1, 0)
        si_ref[chunk(i), :] = _dot(ub, bim_ref[...], 1, 0)
        return 0

    lax.fori_loop(0, t // rows, fill, 0)
    _fill_powers(lr_ref[...], li_ref[...], pr_ref, pi_ref, seg)
    _scan_segments(sr_ref, si_ref, pr_ref, pi_ref, lr_ref[...], li_ref[...], seg, False)


def _s5_scratch(t):
    state = pltpu.VMEM((t, SSM_ST_BLOCK), F32)
    powers = pltpu.VMEM((t // SUBLANES, SSM_ST_BLOCK), F32)
    return state, powers, pltpu.VMEM((t, SSM_CH_BLOCK), F32)


def _s5_fwd(proj, mats, dskip_row, d_attn, d_ssm):
    t = proj.shape[0]
    seg = t // SUBLANES
    n_blocks = d_ssm // SSM_CH_BLOCK
    rows, chunk = _chunks(t)

    def body(u_ref, bre_ref, bim_ref, lr_ref, li_ref, cre_ref, cim_ref, d_ref, y_ref,
             sr_ref, si_ref, pr_ref, pi_ref, us_ref, ys_ref):
        _s5_states(u_ref, us_ref, bre_ref, bim_ref, lr_ref, li_ref, sr_ref, si_ref, pr_ref, pi_ref, t)

        def emit(i, _):
            ys_ref[chunk(i), :] = (_dot(sr_ref[chunk(i), :].astype(BF16), cre_ref[...], 1, 0)
                                   - _dot(si_ref[chunk(i), :].astype(BF16), cim_ref[...], 1, 0)
                                   + d_ref[...] * us_ref[chunk(i), :])
            return 0

        lax.fori_loop(0, t // rows, emit, 0)
        for c in range(t // rows):
            _store_segmented(y_ref, c * rows // SUBLANES, seg, ys_ref[c * rows:(c + 1) * rows, :])

    state, powers, channels = _s5_scratch(t)
    col = pl.BlockSpec((t, SSM_CH_BLOCK), lambda j: (0, j))
    return _call(body, name="s5_fwd", grid=(n_blocks,), in_specs=_s5_in_specs(t, d_attn), out_specs=col,
                 out_shape=_sds((t, d_ssm), F32), scratch_shapes=[state, state, powers, powers, channels, channels],
                 semantics=("parallel",))(proj, *mats, dskip_row)


def _s5_bwd(proj, mats, dskip_row, y, dz_a, dz_b, d_attn, d_ssm, after):
    t = proj.shape[0]
    seg = t // SUBLANES
    n_blocks = d_ssm // SSM_CH_BLOCK
    rows, chunk = _chunks(t)

    def body(u_ref, bre_ref, bim_ref, lr_ref, li_ref, cre_ref, cim_ref, d_ref, y_ref, dza_ref, dzb_ref,
             du_ref, dbre_ref, dbim_ref, dlr_ref, dli_ref, dcre_ref, dcim_ref, dd_ref,
             sr_ref, si_ref, gr_ref, gi_ref, pr_ref, pi_ref, us_ref, dys_ref, dus_ref, acc_r, acc_i):
        _s5_states(u_ref, us_ref, bre_ref, bim_ref, lr_ref, li_ref, sr_ref, si_ref, pr_ref, pi_ref, t)
        for ref in (dcre_ref, dcim_ref, dbre_ref, dbim_ref, dd_ref, acc_r, acc_i):
            ref[...] = jnp.zeros_like(ref)
        for c in range(t // rows):
            tile0, n_tiles = c * rows // SUBLANES, rows // SUBLANES
            dz = _load_segmented(dza_ref, tile0, n_tiles, seg) + _load_segmented(dzb_ref, tile0, n_tiles, seg)
            dys_ref[c * rows:(c + 1) * rows, :] = dz * _gelu_grad(_load_segmented(y_ref, tile0, n_tiles, seg))

        def through_c(i, _):
            dy = dys_ref[chunk(i), :]
            dd_ref[...] += jnp.sum(dy * us_ref[chunk(i), :], axis=0, keepdims=True)
            dyb = dy.astype(BF16)
            gr_ref[chunk(i), :] = _dot(dyb, cre_ref[...], 1, 1)
            gi_ref[chunk(i), :] = -_dot(dyb, cim_ref[...], 1, 1)
            dcre_ref[...] += _dot(sr_ref[chunk(i), :].astype(BF16), dyb, 0, 0)
            dcim_ref[...] -= _dot(si_ref[chunk(i), :].astype(BF16), dyb, 0, 0)
            return 0

        lax.fori_loop(0, t // rows, through_c, 0)

        row = lax.broadcasted_iota(jnp.int32, (SUBLANES, SSM_ST_BLOCK), 0)
        last = pl.ds((seg - 1) * SUBLANES, SUBLANES)
        wrap = [jnp.where(row == 0, 0.0, pltpu.roll(ref[last, :], 1, 0)) for ref in (sr_ref, si_ref)]

        def lambda_grad(j, g_re, g_im):
            before = pl.ds(pl.multiple_of(jnp.maximum(j - 1, 0) * SUBLANES, SUBLANES), SUBLANES)
            prev_r = jnp.where(j > 0, sr_ref[before, :], wrap[0])
            prev_i = jnp.where(j > 0, si_ref[before, :], wrap[1])
            acc_r[...] += g_re * prev_r + g_im * prev_i
            acc_i[...] += g_im * prev_r - g_re * prev_i

        _scan_segments(gr_ref, gi_ref, pr_ref, pi_ref, lr_ref[...], li_ref[...], seg, True, per_tile=lambda_grad)
        dlr_ref[...] = jnp.sum(acc_r[...], axis=0, keepdims=True)
        dli_ref[...] = jnp.sum(acc_i[...], axis=0, keepdims=True)

        def through_b(i, _):
            ub = us_ref[chunk(i), :].astype(BF16)
            grb, gib = gr_ref[chunk(i), :].astype(BF16), gi_ref[chunk(i), :].astype(BF16)
            dbre_ref[...] += _dot(ub, grb, 0, 0)
            dbim_ref[...] += _dot(ub, gib, 0, 0)
            dus_ref[chunk(i), :] = (_dot(grb, bre_ref[...], 1, 1) + _dot(gib, bim_ref[...], 1, 1)
                                    + d_ref[...] * dys_ref[chunk(i), :])
            return 0

        lax.fori_loop(0, t // rows, through_b, 0)
        for c in range(t // rows):
            _store_segmented(du_ref, c * rows // SUBLANES, seg, dus_ref[c * rows:(c + 1) * rows, :])

    col = pl.BlockSpec((t, SSM_CH_BLOCK), lambda j: (0, j))
    blk3 = lambda shape: pl.BlockSpec((None,) + shape, lambda j: (j, 0, 0))
    state, powers, channels = _s5_scratch(t)
    return _call(
        body, name="s5_bwd", grid=(n_blocks,), in_specs=_s5_in_specs(t, d_attn) + [col, col, col],
        out_specs=[col, blk3((SSM_CH_BLOCK, SSM_ST_BLOCK)), blk3((SSM_CH_BLOCK, SSM_ST_BLOCK)),
                   blk3((1, SSM_ST_BLOCK)), blk3((1, SSM_ST_BLOCK)),
                   blk3((SSM_ST_BLOCK, SSM_CH_BLOCK)), blk3((SSM_ST_BLOCK, SSM_CH_BLOCK)),
                   pl.BlockSpec((1, SSM_CH_BLOCK), lambda j: (0, j))],
        out_shape=[_sds((t, d_ssm), F32),
                   _sds((n_blocks, SSM_CH_BLOCK, SSM_ST_BLOCK), F32), _sds((n_blocks, SSM_CH_BLOCK, SSM_ST_BLOCK), F32),
                   _sds((n_blocks, 1, SSM_ST_BLOCK), F32), _sds((n_blocks, 1, SSM_ST_BLOCK), F32),
                   _sds((n_blocks, SSM_ST_BLOCK, SSM_CH_BLOCK), F32), _sds((n_blocks, SSM_ST_BLOCK, SSM_CH_BLOCK), F32),
                   _sds((1, d_ssm), F32)],
        scratch_shapes=[state, state, state, state, powers, powers, channels, channels, channels,
                        pltpu.VMEM((SUBLANES, SSM_ST_BLOCK), F32), pltpu.VMEM((SUBLANES, SSM_ST_BLOCK), F32)],
        semantics=("parallel",), n_after=len(after))(proj, *mats, dskip_row, y, dz_a, dz_b, *after)


def _by_block(gp_n):
    return gp_n.reshape(-1, GROUPS_PER_BLOCK, SSM_GROUP, SSM_STATE)


def _block_diag_in(bbar):
    eye = jnp.eye(GROUPS_PER_BLOCK, dtype=F32)
    return jnp.einsum("jgpn,gh->jgphn", _by_block(bbar), eye).reshape(-1, SSM_CH_BLOCK, SSM_ST_BLOCK)


def _block_diag_in_t(dense):
    d5 = dense.reshape(-1, GROUPS_PER_BLOCK, SSM_GROUP, GROUPS_PER_BLOCK, SSM_STATE)
    eye = jnp.eye(GROUPS_PER_BLOCK, dtype=F32)
    return jnp.einsum("jgphn,gh->jgpn", d5, eye).reshape(-1, SSM_STATE)


def _block_diag_out(c):
    eye = jnp.eye(GROUPS_PER_BLOCK, dtype=F32)
    return jnp.einsum("jgpn,gh->jgnhp", _by_block(c), eye).reshape(-1, SSM_ST_BLOCK, SSM_CH_BLOCK)


def _block_diag_out_t(dense):
    d5 = dense.reshape(-1, GROUPS_PER_BLOCK, SSM_STATE, GROUPS_PER_BLOCK, SSM_GROUP)
    eye = jnp.eye(GROUPS_PER_BLOCK, dtype=F32)
    return jnp.einsum("jgnhp,gh->jgpn", d5, eye).reshape(-1, SSM_STATE)


def _adamw(w, g, m, v):
    m = ADAM_B1 * m + (1.0 - ADAM_B1) * g
    v = ADAM_B2 * v + (1.0 - ADAM_B2) * (g * g)
    m_hat = m / (1.0 - ADAM_B1 ** ADAM_STEP)
    v_hat = v / (1.0 - ADAM_B2 ** ADAM_STEP)
    delta = -ADAM_LR * (m_hat / (jnp.sqrt(v_hat) + ADAM_EPS) + ADAM_WD * w)
    return delta, m, v


def _adam_sharded(name, parts, w, m, v, tr, row0=0):
    r, c = w.shape
    assert r % tr == 0 and row0 % tr == 0, (name, r, tr, row0)

    def body(p_ref, w_ref, m_ref, v_ref, g_out, d_out, m_out, v_out):
        g = p_ref[0].astype(F32)
        for i in range(1, p_ref.shape[0]):
            g = g + p_ref[i].astype(F32)
        delta, m_new, v_new = _adamw(w_ref[...], g, m_ref[...], v_ref[...])
        g_out[...] = g
        d_out[...] = delta
        m_out[...] = m_new
        v_out[...] = v_new

    tile = pl.BlockSpec((tr, c), lambda i: (i, 0))
    return _call(body, name=name, grid=(r // tr,),
                 in_specs=[pl.BlockSpec((parts.shape[0], tr, c), lambda i: (0, i + row0 // tr, 0)), tile, tile, tile],
                 out_specs=[tile] * 4, out_shape=[_sds((r, c), F32)] * 4, semantics=("parallel",))(parts, w, m, v)


_BIG = ("w_in", "w_glu", "w_o", "w_gate", "w_up", "w_down")
_BY_COLUMNS = ("w_in", "w_gate", "w_up")
_SMALL_VECTORS = ("sinks", "log_dt", "b_glu", "g_attn_out", "g_ssm_out", "g_post_mix", "g_pre_ffn", "g_post_ffn")
_SMALL_MATRICES = ("b_re", "b_im", "c_re", "c_im", "a_re", "a_im")
_ORDER = ("g_pre_mix", "w_in", "sinks", "a_re", "a_im", "log_dt", "b_re", "b_im", "c_re", "c_im", "d_skip", "w_glu",
          "b_glu", "g_attn_out", "g_ssm_out", "w_o", "g_post_mix", "g_pre_ffn", "w_gate", "w_up", "w_down",
          "g_post_ffn")


def _pack_grads(vectors, matrices):
    width = max(a.shape[1] for a in vectors)
    slots, row, lane = [], 0, 0
    for a in vectors:
        span = -(-a.shape[1] // LANES) * LANES
        if lane + span > width:
            row, lane = row + 1, 0
        slots.append((row, lane, a.shape[1]))
        lane += span
    firsts, at = [], 0
    for a in matrices:
        firsts.append(at)
        at += a.shape[0]
    nv = len(vectors)

    def body(*refs):
        vec_out, mat_out = refs[-2], refs[-1]
        vec_out[...] = jnp.zeros_like(vec_out)
        for ref, (r, l, w) in zip(refs[:nv], slots):
            vec_out[r:r + 1, l:l + w] = ref[...]
        for ref, r0 in zip(refs[nv:-2], firsts):
            mat_out[r0:r0 + ref.shape[0], :] = ref[...]

    ins = list(vectors) + list(matrices)
    outs = [_sds((-(-(row + 1) // SUBLANES) * SUBLANES, width), F32), _sds((at, matrices[0].shape[1]), F32)]
    vec_pack, mat_pack = _call(body, name="pack_small_grads", in_specs=_whole(ins), out_specs=_whole(outs),
                               out_shape=outs)(*ins)
    return vec_pack, slots, mat_pack, firsts


def _adam_replicated(sources, found_at, w, m, v):
    ns, n = len(sources), len(w)

    def body(*refs):
        ins, outs = refs[ns:ns + 3 * n], refs[ns + 3 * n:]
        summed = []
        for p_ref in refs[:ns]:
            g = p_ref[0]
            for k in range(1, N_DEV):
                g = g + p_ref[k]
            summed.append(g)
        for i, (src, row, lane) in enumerate(found_at):
            w_ref, m_ref, v_ref = ins[i], ins[n + i], ins[2 * n + i]
            rows, cols = w_ref.shape
            g = summed[src][row:row + rows, lane:lane + cols]
            delta, m_new, v_new = _adamw(w_ref[...], g, m_ref[...], v_ref[...])
            for o, val in zip(outs[4 * i:4 * i + 4], (g, delta, m_new, v_new)):
                o[...] = val

    ins = list(sources) + list(w) + list(m) + list(v)
    outs = [_sds(a.shape, F32) for a in w for _ in range(4)]
    flat = _call(body, name="adam_replicated", in_specs=_whole(ins), out_specs=_whole(outs), out_shape=outs)(*ins)
    return [tuple(flat[4 * i:4 * i + 4]) for i in range(n)]


def kernel(x, positions, g_pre_mix, w_in, sinks, a_re, a_im, log_dt, b_re, b_im, c_re, c_im, d_skip, w_glu, b_glu, g_attn_out, g_ssm_out, w_o, g_post_mix, g_pre_ffn, w_gate, w_up, w_down, g_post_ffn, loss_target, m_g_pre_mix, m_w_in, m_sinks, m_a_re, m_a_im, m_log_dt, m_b_re, m_b_im, m_c_re, m_c_im, m_d_skip, m_w_glu, m_b_glu, m_g_attn_out, m_g_ssm_out, m_w_o, m_g_post_mix, m_g_pre_ffn, m_w_gate, m_w_up, m_w_down, m_g_post_ffn, v_g_pre_mix, v_w_in, v_sinks, v_a_re, v_a_im, v_log_dt, v_b_re, v_b_im, v_c_re, v_c_im, v_d_skip, v_w_glu, v_b_glu, v_g_attn_out, v_g_ssm_out, v_w_o, v_g_post_mix, v_g_pre_ffn, v_w_gate, v_w_up, v_w_down, v_g_post_ffn):
    given = dict(locals())
    weights = {n: given[n] for n in _ORDER}
    mom_m = {n: given["m_" + n] for n in _ORDER}
    mom_v = {n: given["v_" + n] for n in _ORDER}

    t, d = x.shape[1], x.shape[2]
    d_attn = d // 2
    d_ssm = d - d_attn
    d_in = d_attn + 2 * D_KV + d_ssm
    n_groups = d_ssm // SSM_GROUP
    n_heads = d_attn // HEAD_DIM
    tm = min(256, t)

    x2 = x[0]
    target = loss_target[0]

    def by_rows(n, a):
        return a[0].T if n in _BY_COLUMNS else a[0]

    def start_gather(name, ns, token):
        behind = 0 if token is None else token[0, 0].astype(BF16)
        shards = [by_rows(n, weights[n]).astype(BF16) + behind for n in ns]
        return _exchange_start(name, shards, False, (OWN, SIBLING) + CHIP_PEERS)

    def forward_gather(handle, after):
        return _forward_start(handle["name"] + "_forward", _exchange_wait(handle, after))

    def finish_gather(handle, after):
        return _split_wait(forward_gather(handle, after)[0], [])

    ag_in, token = start_gather("gather_w_in", ["w_in"], None)
    ag_mix, token = start_gather("gather_w_glu_o", ["w_glu", "w_o"], token)
    ag_ffn_in, token = start_gather("gather_w_gate_up", ["w_gate", "w_up"], token)
    ag_down, token = start_gather("gather_w_down", ["w_down"], token)

    xn, = _rows("norm_in", lambda xv, g: ([_rms(xv)[0] * g], []), [x2], [g_pre_mix], [(d, BF16)], [], tm,
                after=[token])
    win_g, = finish_gather(ag_in, [xn])
    w_in_t = win_g.reshape(d_in, d)
    proj = _mm_nt("proj_in", xn, w_in_t, F32)

    cos, sin = _rope_tables(positions.reshape(t, 1).astype(F32))
    sinks_row = jnp.pad(sinks, ((0, 0), (0, LANES - n_heads)))
    attn = _attention_fwd(proj, cos, sin, sinks_row, d_attn)

    def view(n, a):
        if n in ("b_re", "b_im"):
            return jnp.transpose(a[0], (0, 2, 1)).reshape(-1, SSM_STATE)
        if n in ("c_re", "c_im"):
            return a[0].reshape(-1, SSM_STATE)
        return a[0].T if n == "d_skip" else a[0] if a.ndim == 3 else a

    def unview(n, val):
        if n in ("b_re", "b_im"):
            return jnp.transpose(val.reshape(n_groups, SSM_GROUP, SSM_STATE), (0, 2, 1))[None]
        if n in ("c_re", "c_im"):
            return val.reshape(1, n_groups, SSM_GROUP, SSM_STATE)
        return val.T[None] if n == "d_skip" else val[None] if weights[n].ndim == 3 else val

    b_re_v, b_im_v = view("b_re", b_re), view("b_im", b_im)
    ldt_col = log_dt.reshape(n_groups, 1)
    lam_re, lam_im, bbar_re, bbar_im = _s5_discretise(a_re[0], a_im[0], ldt_col, b_re_v, b_im_v)
    n_blocks = n_groups // GROUPS_PER_BLOCK
    mats = [_block_diag_in(bbar_re).astype(BF16), _block_diag_in(bbar_im).astype(BF16),
            lam_re.reshape(n_blocks, 1, SSM_ST_BLOCK), lam_im.reshape(n_blocks, 1, SSM_ST_BLOCK),
            _block_diag_out(view("c_re", c_re)).astype(BF16), _block_diag_out(view("c_im", c_im)).astype(BF16)]
    dskip_row = d_skip.reshape(1, d_ssm)
    forward_mix, _ = forward_gather(ag_mix, [attn])
    y_ssm = _s5_fwd(proj, mats, dskip_row, d_attn, d_ssm)
    gelu_bf16 = lambda yv: _gelu(yv).astype(BF16)
    wglu_g, wo_g = _split_wait(forward_mix, [y_ssm])
    w_glu_full = wglu_g.reshape(d_ssm, d_ssm)
    w_o_full = wo_g.reshape(d, d)
    glu_lin = _mm_nn("glu_gate", y_ssm, w_glu_full, F32, a_fn=gelu_bf16)

    def mix_prep(av, yv, gl, bg, ga, gs):
        ssm = _gelu(yv) * _sigmoid(gl + bg)
        return [jnp.concatenate([_rms(av)[0] * ga, _rms(ssm)[0] * gs], axis=1)], []

    mixed, = _rows("mix_prep", mix_prep, [attn, y_ssm, glu_lin], [b_glu, g_attn_out, g_ssm_out], [(d, BF16)], [], tm)
    mix = _mm_nn("mix_out", mixed, w_o_full, F32)

    def post_mix(xv, mv, gpm, gpf):
        h = xv + _rms(mv)[0] * gpm
        return [h, _rms(h)[0] * gpf], []

    forward_ffn_in, token = forward_gather(ag_ffn_in, [mix])
    h, hn = _rows("post_mix", post_mix, [x2, mix], [g_post_mix, g_pre_ffn], [(d, F32), (d, BF16)], [], tm,
                  after=[token])
    wgate_g, wup_g = _split_wait(forward_ffn_in, [hn])
    gate, up, hid = _ffn_in(hn, wgate_g, wup_g)
    wdown_g, = finish_gather(ag_down, [hid])
    ff = _mm_contract_slots("ffn_down", [(hid, wdown_g)], F32, per_step=2, tm=1024)

    def head(hv, fv, tv, gpo):
        out = hv + _rms(fv)[0] * gpo
        err = out - tv
        dout = err * (1.0 / d)
        dff, dg = _rms_bwd(fv, gpo, dout)
        loss = jnp.zeros((1, LANES), F32) + 0.5 * jnp.sum(err * err) * (1.0 / d)
        return [dff, dout], [dg, loss]

    dff, dh_out, dg_post_ffn, loss_row = _rows("loss_head", head, [h, ff, target], [g_post_ffn],
                                               [(d, BF16), (d, F32)], [d, LANES], tm)

    def swap_halves(name, grads):
        return _halves_start("swap_" + name, [g.reshape(N_DEV // 2, 2, *g.shape[1:]) for g in grads])

    def scatter_chip_sums(name, swap, after):
        both = _split_wait(swap, after)
        half = len(both) // 2
        sums = [_chip_sum("chip_sum_%s_%d" % (name, i), both[i], both[half + i]) for i in range(half)]
        return _exchange_start("scatter_" + name, sums, True, (OWN,) + CHIP_PEERS, by_chip=True)

    dw_down = _mm_slots_tn("ffn_down_dw", hid, dff, BF16)
    swap_down, token = swap_halves("dw_down", [dw_down])
    dgate, dup = _ffn_down_bwd(dff, wdown_g, gate, up, [token])
    rs_down, token = scatter_chip_sums("dw_down", swap_down, [dgate])
    dhn = _mm_contract_slots("ffn_in_dx", [(dgate, wgate_g), (dup, wup_g)], F32, per_step=2, tm=1024, tn=1024,
                             after=[token])
    dw_gate = _mm_slots_tn("ffn_gate_dw", dgate, hn, BF16)
    dw_up = _mm_slots_tn("ffn_up_dw", dup, hn, BF16)
    swap_ffn_in, tok_ffn_in = swap_halves("dw_gate_up", [dw_gate, dw_up])

    def mid_bwd(dho, dhn_, hv, mv, gpf, gpm):
        d1, dgpf = _rms_bwd(hv, gpf, dhn_)
        dh_ = dho + d1
        dmix_, dgpm = _rms_bwd(mv, gpm, dh_)
        return [dh_, dmix_], [dgpf, dgpm]

    dh, dmix, dg_pre_ffn, dg_post_mix = _rows("mid_bwd", mid_bwd, [dh_out, dhn, h, mix], [g_pre_ffn, g_post_mix],
                                              [(d, F32), (d, BF16)], [d, d], tm, after=[tok_ffn_in])

    dmixed = _mm_nt("mix_out_dx", dmix, w_o_full, F32)
    rs_ffn_in, token = scatter_chip_sums("dw_gate_up", swap_ffn_in, [dmixed])
    dw_o = _mm_tn("mix_out_dw", mixed, dmix, BF16, after=[token])
    swap_o, tok_o = swap_halves("dw_o", [dw_o.reshape(N_DEV, d // N_DEV, d)])

    def mix_bwd(dm, av, yv, gl, bg, ga, gs):
        dattn_, dga = _rms_bwd(av, ga, dm[:, :d_attn])
        z = _gelu(yv)
        sg = _sigmoid(gl + bg)
        dssm, dgs = _rms_bwd(z * sg, gs, dm[:, d_attn:])
        dgl = dssm * z * sg * (1.0 - sg)
        return [dattn_, dssm * sg, dgl], [dga, dgs, jnp.sum(dgl, axis=0, keepdims=True)]

    dattn, dz_direct, dglu, dg_attn_out, dg_ssm_out, db_glu = _rows(
        "mix_bwd", mix_bwd, [dmixed, attn, y_ssm, glu_lin], [b_glu, g_attn_out, g_ssm_out],
        [(d_attn, F32), (d_ssm, F32), (d_ssm, BF16)], [d_attn, d_ssm, d_ssm], tm, after=[tok_o])
    dz_glu = _mm_nt("glu_gate_dx", dglu, w_glu_full, F32)
    dw_glu = _mm_tn("glu_gate_dw", y_ssm, dglu, BF16, a_fn=gelu_bf16)
    rs_o, token = scatter_chip_sums("dw_o", swap_o, [dz_glu, dw_glu])

    du, db_re_dense, db_im_dense, dlam_re, dlam_im, dc_re_dense, dc_im_dense, dd_skip = _s5_bwd(
        proj, mats, dskip_row, y_ssm, dz_direct, dz_glu, d_attn, d_ssm, [token])
    da_re, da_im, dlog_dt, db_re_v, db_im_v = _s5_discretise_bwd(
        a_re[0], a_im[0], ldt_col, b_re_v, b_im_v, dlam_re.reshape(n_groups, SSM_STATE),
        dlam_im.reshape(n_groups, SSM_STATE), _block_diag_in_t(db_re_dense), _block_diag_in_t(db_im_dense))
    dq, dk2, dv2, dsinks_row = _attention_bwd(proj, cos, sin, sinks_row, dattn, d_attn)

    small_grads = {
        "sinks": dsinks_row, "a_re": da_re, "a_im": da_im, "log_dt": dlog_dt.reshape(1, n_groups),
        "b_re": db_re_v, "b_im": db_im_v, "c_re": _block_diag_out_t(dc_re_dense),
        "c_im": _block_diag_out_t(dc_im_dense), "d_skip": dd_skip.reshape(n_groups, SSM_GROUP).T, "b_glu": db_glu,
        "g_attn_out": dg_attn_out, "g_ssm_out": dg_ssm_out, "g_post_mix": dg_post_mix, "g_pre_ffn": dg_pre_ffn,
        "g_post_ffn": dg_post_ffn,
    }
    vec_pack, vec_slots, mat_pack, mat_rows = _pack_grads([small_grads[n] for n in _SMALL_VECTORS],
                                                          [small_grads[n] for n in _SMALL_MATRICES])
    ag_small, token = _exchange_start("gather_small_grads", [vec_pack, mat_pack, small_grads["d_skip"]], False,
                                      (OWN,) + ALL_PEERS)
    dproj = _assemble_dproj(dq, dk2, dv2, du, d_in, [token])

    dxn = _mm_nn("proj_in_dx", dproj, w_in_t, F32)
    dw_in = _mm_tn("proj_in_dw", dproj, xn, BF16).reshape(N_DEV, d_in // N_DEV, d)
    swap_in, token = swap_halves("dw_in_glu", [dw_in, dw_glu.reshape(N_DEV, d_ssm // N_DEV, d_ssm)])

    def x_bwd(dh_, dxn_, xv, g):
        dx, dg = _rms_bwd(xv, g, dxn_)
        return [dh_ + dx], [dg]

    grad_x, dg_pre_mix = _rows("norm_in_bwd", x_bwd, [dh, dxn, x2], [g_pre_mix], [(d, F32)], [d], tm, after=[token])
    ag_last, token = _exchange_start("gather_g_pre_mix_grad", [dg_pre_mix], False, (OWN,) + ALL_PEERS)
    rs_in, token = scatter_chip_sums("dw_in_glu", swap_in, [grad_x, token])

    results = {}

    def adam_big(n, parts):
        r = parts.shape[1]
        tr = next((c for c in range(192, 15, -16) if r % c == 0), r)
        results[n] = _adam_sharded("adam_" + n, parts, by_rows(n, weights[n]), by_rows(n, mom_m[n]),
                                   by_rows(n, mom_v[n]), tr)
        return results[n][3]

    done = [grad_x, token]
    adam_big("w_down", _exchange_wait(rs_down, done)[0])
    p_gate, p_up = _exchange_wait(rs_ffn_in, done)
    done = [adam_big("w_gate", p_gate), adam_big("w_up", p_up), results["w_down"][3]]
    done = [adam_big("w_o", _exchange_wait(rs_o, done)[0])]
    vec_parts, mat_parts, dskip_parts = _exchange_wait(ag_small, done)
    first_gain_parts, = _exchange_wait(ag_last, done)
    for n, row0 in zip(_SMALL_MATRICES, mat_rows):
        rows = view(n, weights[n]).shape[0]
        results[n] = _adam_sharded("adam_" + n, mat_parts, view(n, weights[n]), view(n, mom_m[n]), view(n, mom_v[n]),
                                   rows, row0)
    rest = _SMALL_VECTORS + ("d_skip", "g_pre_mix")
    found_at = [(0, row, lane) for row, lane, _ in vec_slots] + [(1, 0, 0), (2, 0, 0)]
    updated = _adam_replicated([vec_parts, dskip_parts, first_gain_parts], found_at,
                               [view(n, weights[n]) for n in rest], [view(n, mom_m[n]) for n in rest],
                               [view(n, mom_v[n]) for n in rest])
    results.update(zip(rest, updated))
    p_in, p_glu = _exchange_wait(rs_in, [results[n][3] for n in _SMALL_MATRICES] + [updated[0][3]])
    adam_big("w_in", p_in)
    adam_big("w_glu", p_glu)

    loss = lax.psum(loss_row[0, 0], ("x", "y", "c"))
    outs = [loss, grad_x[None]]
    for k in range(4):
        for n in _ORDER:
            val = results[n][k]
            outs.append(val.T[None] if n in _BY_COLUMNS else val[None] if n in _BIG else unview(n, val))
    return tuple(outs)
```

```python
import math

import jax
import jax.numpy as jnp
from jax import lax
from jax.experimental import pallas as pl
from jax.experimental.pallas import tpu as pltpu

F32 = jnp.float32
BF16 = jnp.bfloat16

HEAD_DIM = 64
N_KV_HEADS = 4
D_KV = N_KV_HEADS * HEAD_DIM
WINDOW = 128
BLOCK = 128
ROPE_THETA = 10000.0
SSM_GROUP = 16
SSM_STATE = 64
GROUPS_PER_BLOCK = 8
SSM_CH_BLOCK = GROUPS_PER_BLOCK * SSM_GROUP
SSM_ST_BLOCK = GROUPS_PER_BLOCK * SSM_STATE
RMS_EPS = 1e-6
N_DEV = 8
LANES = 128
SUBLANES = 8
MASKED = -1e30

ADAM_LR = 0.001
ADAM_B1 = 0.9
ADAM_B2 = 0.999
ADAM_EPS = 1e-08
ADAM_WD = 0.01
ADAM_STEP = 10

VMEM_LIMIT_BYTES = 56 * 1024 * 1024


def _call(body, *, name, out_shape, in_specs, out_specs, grid=(), scratch_shapes=(), semantics=None, n_after=0):
    params = dict(vmem_limit_bytes=VMEM_LIMIT_BYTES)
    if semantics is not None:
        params["dimension_semantics"] = semantics
    n_in = len(in_specs)
    if n_after:
        inner = body

        def body(*refs):
            inner(*refs[:n_in], *refs[n_in + n_after:])

        in_specs = list(in_specs) + [pl.BlockSpec(memory_space=pl.ANY)] * n_after
    return pl.pallas_call(body, name=name, grid=grid, in_specs=in_specs, out_specs=out_specs, out_shape=out_shape,
                          scratch_shapes=scratch_shapes, compiler_params=pltpu.CompilerParams(**params))


def _sds(shape, dtype):
    return jax.ShapeDtypeStruct(tuple(shape), dtype)


def _dot(a, b, ca, cb):
    return lax.dot_general(a, b, (((ca,), (cb,)), ((), ())), preferred_element_type=F32)


def _rms(x):
    r = lax.rsqrt(jnp.mean(x * x, axis=-1, keepdims=True) + RMS_EPS)
    return x * r, r


def _rms_bwd(x, g, dy):
    xh, r = _rms(x)
    dxh = dy * g
    dx = r * (dxh - xh * jnp.mean(dxh * xh, axis=-1, keepdims=True))
    return dx, jnp.sum(dy * xh, axis=0, keepdims=True)


def _sigmoid(x):
    return 1.0 / (1.0 + jnp.exp(-x))


_GELU_C = math.sqrt(2.0 / math.pi)
_GELU_A = 0.044715


def _gelu(y):
    t = jnp.tanh(_GELU_C * (y + _GELU_A * y * y * y))
    return 0.5 * y * (1.0 + t)


def _gelu_grad(y):
    t = jnp.tanh(_GELU_C * (y + _GELU_A * y * y * y))
    return 0.5 * (1.0 + t) + 0.5 * y * (1.0 - t * t) * _GELU_C * (1.0 + 3.0 * _GELU_A * y * y)


def _rows(name, fn, row_ins, vec_ins, row_outs, acc_widths, tm, after=()):
    rows = row_ins[0].shape[0]
    assert rows % tm == 0, (name, rows, tm)
    n_row, n_vec, n_out, n_acc = len(row_ins), len(vec_ins), len(row_outs), len(acc_widths)

    def body(*refs):
        ins = [r[...] for r in refs[:n_row + n_vec]]
        outs = refs[n_row + n_vec:n_row + n_vec + n_out]
        accs = refs[n_row + n_vec + n_out:]
        row_vals, acc_vals = fn(*ins)
        for o, v in zip(outs, row_vals):
            o[...] = v.astype(o.dtype)
        if n_acc:
            @pl.when(pl.program_id(0) == 0)
            def _():
                for a in accs:
                    a[...] = jnp.zeros_like(a)
            for a, v in zip(accs, acc_vals):
                a[...] += v

    in_specs = [pl.BlockSpec((tm, a.shape[1]), lambda i: (i, 0)) for a in row_ins]
    in_specs += [pl.BlockSpec(v.shape, lambda i: (0, 0)) for v in vec_ins]
    out_specs = [pl.BlockSpec((tm, w), lambda i: (i, 0)) for w, _ in row_outs]
    out_specs += [pl.BlockSpec((1, w), lambda i: (0, 0)) for w in acc_widths]
    out_shape = [_sds((rows, w), dt) for w, dt in row_outs] + [_sds((1, w), F32) for w in acc_widths]
    return _call(body, name=name, grid=(rows // tm,), in_specs=in_specs, out_specs=out_specs, out_shape=out_shape,
                 semantics=("arbitrary",) if n_acc else ("parallel",), n_after=len(after))(*row_ins, *vec_ins, *after)


def _matmul(name, operands, in_specs, product, grid, out_shape, out_spec, acc_shape, after=()):
    nk = grid[-1]
    n_in = len(operands)
    in_place = out_shape.dtype == F32

    def body(*refs):
        ins = [r[...] for r in refs[:n_in]]
        o_ref = refs[n_in]
        if nk == 1:
            o_ref[...] = product(*ins).astype(o_ref.dtype)
            return
        acc = o_ref if in_place else refs[n_in + 1]
        k = pl.program_id(len(grid) - 1)

        @pl.when(k == 0)
        def _():
            acc[...] = jnp.zeros_like(acc)

        acc[...] += product(*ins)

        if not in_place:
            @pl.when(k == nk - 1)
            def _():
                o_ref[...] = acc[...].astype(o_ref.dtype)

    return _call(body, name=name, grid=grid, in_specs=in_specs, out_specs=out_spec, out_shape=out_shape,
                 scratch_shapes=[] if nk == 1 or in_place else [pltpu.VMEM(acc_shape, F32)],
                 semantics=("parallel",) * (len(grid) - 1) + ("arbitrary",), n_after=len(after))(*operands, *after)


def _mm_nn(name, a, b, out_dtype, tm=512, tn=None, a_fn=lambda x: x):
    m, k = a.shape
    n = b.shape[1]
    tm, tn = min(tm, m), n if tn is None else tn
    return _matmul(name, [a, b],
                   [pl.BlockSpec((tm, k), lambda i, j, s: (i, 0)), pl.BlockSpec((k, tn), lambda i, j, s: (0, j))],
                   lambda x, y: _dot(a_fn(x), y, 1, 0), (m // tm, n // tn, 1), _sds((m, n), out_dtype),
                   pl.BlockSpec((tm, tn), lambda i, j, s: (i, j)), (tm, tn))


def _mm_nt(name, a, b, out_dtype, tm=512, tn=None):
    m, k = a.shape
    n = b.shape[0]
    tm, tn = min(tm, m), n if tn is None else tn
    return _matmul(name, [a, b],
                   [pl.BlockSpec((tm, k), lambda i, j, s: (i, 0)), pl.BlockSpec((tn, k), lambda i, j, s: (j, 0))],
                   lambda x, y: _dot(x, y, 1, 1), (m // tm, n // tn, 1), _sds((m, n), out_dtype),
                   pl.BlockSpec((tm, tn), lambda i, j, s: (i, j)), (tm, tn))


def _mm_tn(name, a, b, out_dtype, tm=512, tn=None, tk=2048, a_fn=lambda x: x, after=()):
    k, m = a.shape
    n = b.shape[1]
    tm, tk, tn = min(tm, m), min(tk, k), n if tn is None else tn
    return _matmul(name, [a, b],
                   [pl.BlockSpec((tk, tm), lambda i, j, s: (s, i)), pl.BlockSpec((tk, tn), lambda i, j, s: (s, j))],
                   lambda x, y: _dot(a_fn(x), y, 0, 0), (m // tm, n // tn, k // tk), _sds((m, n), out_dtype),
                   pl.BlockSpec((tm, tn), lambda i, j, s: (i, j)), (tm, tn), after)


def _mm_contract_slots(name, pairs, out_dtype, per_step, tm=512, tn=2048, after=()):
    s_, m, k = pairs[0][0].shape
    n = pairs[0][1].shape[2]
    tm, tn = min(tm, m), min(tn, n)
    ops, specs = [], []
    for a, b in pairs:
        ops += [a, b]
        specs += [pl.BlockSpec((per_step, tm, k), lambda i, j, s: (s, i, 0)),
                  pl.BlockSpec((per_step, k, tn), lambda i, j, s: (s, 0, j))]

    def product(*t):
        return sum(_dot(t[2 * p][q], t[2 * p + 1][q], 1, 0) for p in range(len(pairs)) for q in range(per_step))

    return _matmul(name, ops, specs, product, (m // tm, n // tn, s_ // per_step), _sds((m, n), out_dtype),
                   pl.BlockSpec((tm, tn), lambda i, j, s: (i, j)), (tm, tn), after)


def _mm_slots_tn(name, a, b, out_dtype, tn=2048, tk=2048):
    s_, k, m = a.shape
    n = b.shape[1]
    tn, tk = min(tn, n), min(tk, k)
    return _matmul(name, [a, b],
                   [pl.BlockSpec((None, tk, m), lambda s, j, z: (s, z, 0)), pl.BlockSpec((tk, tn), lambda s, j, z: (z, j))],
                   lambda x, y: _dot(x, y, 0, 0), (s_, n // tn, k // tk), _sds((s_, m, n), out_dtype),
                   pl.BlockSpec((None, m, tn), lambda s, j, z: (s, 0, j)), (m, tn))


def _ffn_in(a, w_gate, w_up, tm=1024):
    m, k = a.shape
    s_, n, _ = w_gate.shape
    tm = min(tm, m)

    def body(a_ref, wg_ref, wu_ref, g_ref, u_ref, h_ref):
        x = a_ref[...]
        g = _dot(x, wg_ref[...], 1, 1)
        u = _dot(x, wu_ref[...], 1, 1)
        g_ref[...] = g.astype(BF16)
        u_ref[...] = u.astype(BF16)
        h_ref[...] = (g * _sigmoid(g) * u).astype(BF16)

    w_spec = pl.BlockSpec((None, n, k), lambda s, i: (s, 0, 0))
    o_spec = pl.BlockSpec((None, tm, n), lambda s, i: (s, i, 0))
    return _call(body, name="ffn_in", grid=(s_, m // tm),
                 in_specs=[pl.BlockSpec((tm, k), lambda s, i: (i, 0)), w_spec, w_spec], out_specs=[o_spec] * 3,
                 out_shape=[_sds((s_, m, n), BF16)] * 3, semantics=("parallel", "parallel"))(a, w_gate, w_up)


def _ffn_down_bwd(d_out, w_down, gate, up, after, tm=1024):
    m, k = d_out.shape
    s_, n, _ = w_down.shape
    tm = min(tm, m)

    def body(d_ref, w_ref, g_ref, u_ref, dg_ref, du_ref):
        rows = pl.ds(pl.multiple_of(pl.program_id(1) * tm, tm), tm)
        dh = _dot(d_ref[rows, :], w_ref[...], 1, 1)
        g = g_ref[...].astype(F32)
        sg = _sigmoid(g)
        dg_ref[...] = (dh * u_ref[...].astype(F32) * sg * (1.0 + g * (1.0 - sg))).astype(BF16)
        du_ref[...] = (dh * g * sg).astype(BF16)

    t_spec = pl.BlockSpec((None, tm, n), lambda s, i: (s, i, 0))
    return _call(body, name="ffn_down_dx", grid=(s_, m // tm),
                 in_specs=[pl.BlockSpec((m, k), lambda s, i: (0, 0)), pl.BlockSpec((None, n, k), lambda s, i: (s, 0, 0)),
                           t_spec, t_spec],
                 out_specs=[t_spec] * 2, out_shape=[_sds((s_, m, n), BF16)] * 2, semantics=("parallel", "parallel"),
                 n_after=len(after))(d_out, w_down, gate, up, *after)


ALL_PEERS = (1, 2, 3, 4, 5, 6, 7)
CHIP_PEERS = (2, 4, 6)
SIBLING = 1
OWN = 0


def _peer(relation):
    x, y, c = lax.axis_index("x"), lax.axis_index("y"), lax.axis_index("c")
    pos = (1 - x if relation & 4 else x, 1 - y if relation & 2 else y, 1 - c if relation & 1 else c)
    return pos, 4 * pos[0] + 2 * pos[1] + pos[2]


def _slot(relation, by_chip):
    pos, device = _peer(relation)
    return 2 * pos[0] + pos[1] if by_chip else device


def _exchange_copies(ins, lands, send_sems, recv_sems, scatter, relations, by_chip=False):
    me = _slot(0, by_chip)

    def copy(a, s, peer, pos, dst_slot):
        return pltpu.make_async_remote_copy(
            src_ref=ins[a].at[peer] if scatter else ins[a], dst_ref=lands[a].at[dst_slot],
            send_sem=send_sems.at[s], recv_sem=recv_sems.at[s], device_id=pos, device_id_type=pl.DeviceIdType.MESH)

    pairs = []
    for k, r in enumerate(relations):
        pos, peer = _peer(r)[0], _slot(r, by_chip)
        for a in range(len(ins)):
            s = a * len(relations) + k
            pairs.append((copy(a, s, peer, pos, me), copy(a, s, peer, pos, peer)))
    return pairs


def _halves_copies(arrays, lands, send_sems, recv_sems):
    sibling, _ = _peer(SIBLING)
    core = lax.axis_index("c")
    pairs = []
    for a, (ref, land) in enumerate(zip(arrays, lands)):
        send = pltpu.make_async_remote_copy(
            src_ref=ref.at[:, pl.ds(1 - core, 1)], dst_ref=land, send_sem=send_sems.at[a], recv_sem=recv_sems.at[a],
            device_id=sibling, device_id_type=pl.DeviceIdType.MESH)
        pairs.append((send, send))
    return pairs


def _forward_copies(lands, send_sems, recv_sems):
    sibling, _ = _peer(SIBLING)

    def copy(a, s, slot):
        return pltpu.make_async_remote_copy(
            src_ref=lands[a].at[slot], dst_ref=lands[a].at[slot], send_sem=send_sems.at[s], recv_sem=recv_sems.at[s],
            device_id=sibling, device_id_type=pl.DeviceIdType.MESH)

    pairs = []
    for k, r in enumerate(CHIP_PEERS):
        _, mine = _peer(r)
        _, theirs = _peer(r | SIBLING)
        for a in range(len(lands)):
            s = a * len(CHIP_PEERS) + k
            pairs.append((copy(a, s, mine), copy(a, s, theirs)))
    return pairs


_HBM_SPEC = pl.BlockSpec(memory_space=pltpu.HBM)
_SEM_SPEC = pl.BlockSpec(memory_space=pltpu.SEMAPHORE)
_SIDE_EFFECT = pltpu.SideEffectType.DATAFLOW_SIDE_EFFECTING


def _split_start(name, operands, n_sem, make_pairs):
    k = len(operands)

    def body(*refs):
        send_sems, recv_sems, token = refs[k], refs[k + 1], refs[-1]
        for send, _ in make_pairs(refs[:k], send_sems, recv_sems):
            send.start()
        token[...] = jnp.zeros_like(token)

    out = pl.pallas_call(
        body, name=name,
        out_shape=(pltpu.SemaphoreType.DMA((n_sem,)), pltpu.SemaphoreType.DMA((n_sem,)),
                   *[pltpu.HBM(a.shape, a.dtype) for a in operands], _sds((SUBLANES, LANES), F32)),
        in_specs=[_HBM_SPEC] * k,
        out_specs=(_SEM_SPEC, _SEM_SPEC, *[_HBM_SPEC] * k, pl.BlockSpec(memory_space=pltpu.VMEM)),
        input_output_aliases={i: 2 + i for i in range(k)},
        compiler_params=pltpu.CompilerParams(has_side_effects=_SIDE_EFFECT),
    )(*[pltpu.with_memory_space_constraint(a, pltpu.HBM) for a in operands])
    return dict(name=name, sems=out[:2], thru=list(out[2:2 + k]), make_pairs=make_pairs), out[-1]


def _split_wait(handle, after):
    thru, make_pairs = handle["thru"], handle["make_pairs"]
    k = len(thru)

    def body(*refs):
        for send, arrival in make_pairs(refs[:k], refs[k], refs[k + 1]):
            send.wait_send()
            arrival.wait_recv()

    return pl.pallas_call(
        body, name=handle["name"] + "_wait", out_shape=[pltpu.HBM(a.shape, a.dtype) for a in thru],
        in_specs=[_HBM_SPEC] * k + [_SEM_SPEC, _SEM_SPEC] + [pl.BlockSpec(memory_space=pl.ANY)] * len(after),
        out_specs=[_HBM_SPEC] * k, input_output_aliases={i: i for i in range(k)},
        compiler_params=pltpu.CompilerParams(has_side_effects=_SIDE_EFFECT),
    )(*thru, *handle["sems"], *after)


def _exchange_start(name, arrays, scatter, relations, by_chip=False):
    n = len(arrays)
    lands = [lax.empty(a.shape if scatter else (N_DEV,) + a.shape, a.dtype) for a in arrays]

    def make_pairs(refs, send_sems, recv_sems):
        return _exchange_copies(refs[:n], refs[n:], send_sems, recv_sems, scatter, relations, by_chip)

    handle, token = _split_start(name, list(arrays) + lands, n * len(relations), make_pairs)
    handle.update(n=n)
    return handle, token


def _halves_start(name, arrays):
    lands = [lax.empty((a.shape[0], 1) + a.shape[2:], a.dtype) for a in arrays]
    n = len(arrays)

    def make_pairs(refs, send_sems, recv_sems):
        return _halves_copies(refs[:n], refs[n:], send_sems, recv_sems)

    return _split_start(name, list(arrays) + lands, n, make_pairs)


def _chip_sum(name, array, landed):
    chips, _, r, c = array.shape
    tr = r // 2 if r > 512 and r % 32 == 0 else r

    def body(a_ref, b_ref, o_ref):
        mine = a_ref[lax.axis_index("c")].astype(F32)
        o_ref[...] = (mine + b_ref[...].astype(F32)).astype(o_ref.dtype)

    return _call(body, name=name, grid=(chips, r // tr),
                 in_specs=[pl.BlockSpec((None, 2, tr, c), lambda k, i: (k, 0, i, 0)),
                           pl.BlockSpec((None, None, tr, c), lambda k, i: (k, 0, i, 0))],
                 out_specs=pl.BlockSpec((None, tr, c), lambda k, i: (k, i, 0)),
                 out_shape=_sds((chips, r, c), BF16), semantics=("parallel", "parallel"))(array, landed)


def _forward_start(name, lands):
    return _split_start(name, list(lands), len(lands) * len(CHIP_PEERS), _forward_copies)


def _exchange_wait(handle, after):
    return _split_wait(handle, after)[handle["n"]:]


def _rope_tables(pos_col):
    t = pos_col.shape[0]
    half = HEAD_DIM // 2
    inv_freq = ROPE_THETA ** (-jnp.arange(half, dtype=F32) / half)
    inv_row = jnp.tile(inv_freq, LANES // half)[None, :]

    def body(pos_ref, inv_ref, cos_ref, sin_ref):
        ang = pos_ref[...] * inv_ref[...]
        cos_ref[...] = jnp.cos(ang)
        sin_ref[...] = jnp.sin(ang)

    tm = min(t, 512)
    return _call(body, name="rope_tables", grid=(t // tm,),
                 in_specs=[pl.BlockSpec((tm, 1), lambda i: (i, 0)), pl.BlockSpec((1, LANES), lambda i: (0, 0))],
                 out_specs=[pl.BlockSpec((tm, LANES), lambda i: (i, 0))] * 2,
                 out_shape=[_sds((t, LANES), F32)] * 2, semantics=("parallel",))(pos_col, inv_row)


def _rot_half(x):
    lane = lax.broadcasted_iota(jnp.int32, x.shape, 1)
    low = (lane % HEAD_DIM) < HEAD_DIM // 2
    return jnp.where(low, -pltpu.roll(x, LANES - HEAD_DIM // 2, 1), pltpu.roll(x, HEAD_DIM // 2, 1))


def _rope(x, cos, sin):
    return x * cos + _rot_half(x) * sin


def _unrope(d, cos, sin):
    return d * cos - _rot_half(d) * sin


def _band_mask(first_block, heads):
    r = lax.broadcasted_iota(jnp.int32, (heads * BLOCK, 2 * BLOCK), 0) % BLOCK
    c = lax.broadcasted_iota(jnp.int32, (heads * BLOCK, 2 * BLOCK), 1)
    diff = r - c + BLOCK
    return (diff >= 0) & (diff < WINDOW) & ((c >= BLOCK) | jnp.logical_not(first_block))


def _attn_specs(t, d_attn, d_in):
    kb, vb = d_attn // D_KV, d_attn // D_KV + 1
    prev = lambda i: jnp.maximum(i - 1, 0)
    return [
        pl.BlockSpec((BLOCK, d_attn), lambda i: (i, 0)),
        pl.BlockSpec((BLOCK, D_KV), lambda i: (i, kb)),
        pl.BlockSpec((BLOCK, D_KV), lambda i: (i, vb)),
        pl.BlockSpec((BLOCK, D_KV), lambda i: (prev(i), kb)),
        pl.BlockSpec((BLOCK, D_KV), lambda i: (prev(i), vb)),
        pl.BlockSpec((BLOCK, LANES), lambda i: (i, 0)),
        pl.BlockSpec((BLOCK, LANES), lambda i: (i, 0)),
        pl.BlockSpec((BLOCK, LANES), lambda i: (prev(i), 0)),
        pl.BlockSpec((BLOCK, LANES), lambda i: (prev(i), 0)),
        pl.BlockSpec((1, LANES), lambda i: (0, 0)),
    ]


def _head(x, h):
    return x[:, h * HEAD_DIM:(h + 1) * HEAD_DIM]


def _attn_heads(q_ref, kc_ref, vc_ref, kp_ref, vp_ref, cq_ref, sq_ref, cp_ref, sp_ref, d_attn):
    cq, sq, cp, sp = cq_ref[...], sq_ref[...], cp_ref[...], sp_ref[...]
    q_rot = [_rope(q_ref[:, j * LANES:(j + 1) * LANES], cq, sq) for j in range(d_attn // LANES)]
    kc_rot = [_rope(kc_ref[:, j * LANES:(j + 1) * LANES], cq, sq) for j in range(D_KV // LANES)]
    kp_rot = [_rope(kp_ref[:, j * LANES:(j + 1) * LANES], cp, sp) for j in range(D_KV // LANES)]
    per = LANES // HEAD_DIM
    q_heads = [_head(q_rot[h // per], h % per).astype(BF16) for h in range(d_attn // HEAD_DIM)]
    kk = [jnp.concatenate([_head(kp_rot[g // per], g % per), _head(kc_rot[g // per], g % per)], axis=0).astype(BF16)
          for g in range(N_KV_HEADS)]
    vv = [jnp.concatenate([_head(vp_ref[...], g), _head(vc_ref[...], g)], axis=0).astype(BF16) for g in range(N_KV_HEADS)]
    return q_heads, kk, vv


def _stack_group(q_heads, sink_ref, group):
    q_all = jnp.concatenate([q_heads[h] for h in group], axis=0)
    sink_all = jnp.concatenate([jnp.broadcast_to(sink_ref[:, h:h + 1], (BLOCK, 1)) for h in group], axis=0)
    return q_all, sink_all


def _softmax_with_sink(q, kk, sink, mask):
    s = _dot(q, kk, 1, 1) * (1.0 / math.sqrt(HEAD_DIM))
    s = jnp.where(mask, s, MASKED)
    m = jnp.maximum(jnp.max(s, axis=-1, keepdims=True), sink)
    p = jnp.exp(s - m)
    e_sink = jnp.exp(sink - m)
    inv = 1.0 / (jnp.sum(p, axis=-1, keepdims=True) + e_sink)
    return p * inv, e_sink * inv


def _attention_fwd(proj, cos, sin, sinks_row, d_attn):
    t, d_in = proj.shape
    n_heads = d_attn // HEAD_DIM
    q_per_kv = n_heads // N_KV_HEADS

    def body(q_ref, kc_ref, vc_ref, kp_ref, vp_ref, cq_ref, sq_ref, cp_ref, sp_ref, sink_ref, o_ref):
        mask = _band_mask(pl.program_id(0) == 0, q_per_kv)
        q_heads, kk, vv = _attn_heads(q_ref, kc_ref, vc_ref, kp_ref, vp_ref, cq_ref, sq_ref, cp_ref, sp_ref, d_attn)
        for g in range(N_KV_HEADS):
            group = range(g * q_per_kv, (g + 1) * q_per_kv)
            q_all, sink_all = _stack_group(q_heads, sink_ref, group)
            probs, _ = _softmax_with_sink(q_all, kk[g], sink_all, mask)
            o_all = _dot(probs.astype(BF16), vv[g], 1, 0)
            for k, h in enumerate(group):
                o_ref[:, h * HEAD_DIM:(h + 1) * HEAD_DIM] = o_all[k * BLOCK:(k + 1) * BLOCK]

    return _call(body, name="attention_fwd", grid=(t // BLOCK,), in_specs=_attn_specs(t, d_attn, d_in),
                 out_specs=pl.BlockSpec((BLOCK, d_attn), lambda i: (i, 0)), out_shape=_sds((t, d_attn), F32),
                 semantics=("parallel",))(proj, proj, proj, proj, proj, cos, sin, cos, sin, sinks_row)


def _attention_bwd(proj, cos, sin, sinks_row, d_out, d_attn):
    t, d_in = proj.shape
    n_heads = d_attn // HEAD_DIM
    q_per_kv = n_heads // N_KV_HEADS
    nb = t // BLOCK
    per = LANES // HEAD_DIM
    stack = q_per_kv

    def body(q_ref, kc_ref, vc_ref, kp_ref, vp_ref, cq_ref, sq_ref, cp_ref, sp_ref, sink_ref, do_ref,
             dq_ref, dk_ref, dv_ref, dsink_ref):
        i = pl.program_id(0)
        mask = _band_mask(i == 0, stack)
        q_heads, kk, vv = _attn_heads(q_ref, kc_ref, vc_ref, kp_ref, vp_ref, cq_ref, sq_ref, cp_ref, sp_ref, d_attn)
        lane = lax.broadcasted_iota(jnp.int32, (1, LANES), 1)
        dsink = jnp.zeros((1, LANES), F32)
        dq_rot, dkk, dvv = [], [], []
        for g in range(N_KV_HEADS):
            dkk_g = jnp.zeros((2 * BLOCK, HEAD_DIM), F32)
            dvv_g = jnp.zeros((2 * BLOCK, HEAD_DIM), F32)
            for first in range(g * q_per_kv, (g + 1) * q_per_kv, stack):
                group = range(first, first + stack)
                q_all, sink_all = _stack_group(q_heads, sink_ref, group)
                probs, p_sink = _softmax_with_sink(q_all, kk[g], sink_all, mask)
                do_all = jnp.concatenate([do_ref[:, h * HEAD_DIM:(h + 1) * HEAD_DIM] for h in group],
                                         axis=0).astype(BF16)
                dp = _dot(do_all, vv[g], 1, 1)
                delta = jnp.sum(probs * dp, axis=-1, keepdims=True)
                ds = (probs * (dp - delta) * (1.0 / math.sqrt(HEAD_DIM))).astype(BF16)
                dq_all = _dot(ds, kk[g], 1, 0)
                dkk_g += _dot(ds, q_all, 0, 0)
                dvv_g += _dot(probs.astype(BF16), do_all, 0, 0)
                sink_term = p_sink * delta
                for k, h in enumerate(group):
                    dq_rot.append(dq_all[k * BLOCK:(k + 1) * BLOCK])
                    part = jnp.sum(sink_term[k * BLOCK:(k + 1) * BLOCK], axis=0, keepdims=True)
                    dsink += jnp.where(lane == h, -part, 0.0)
            dkk.append(dkk_g)
            dvv.append(dvv_g)
        cq, sq, cp, sp = cq_ref[...], sq_ref[...], cp_ref[...], sp_ref[...]
        for j in range(d_attn // LANES):
            d = jnp.concatenate(dq_rot[j * per:(j + 1) * per], axis=1)
            dq_ref[:, j * LANES:(j + 1) * LANES] = _unrope(d, cq, sq)
        for j in range(D_KV // LANES):
            d = jnp.concatenate(dkk[j * per:(j + 1) * per], axis=1)
            dk_ref[0, :, j * LANES:(j + 1) * LANES] = _unrope(d[:BLOCK], cp, sp)
            dk_ref[1, :, j * LANES:(j + 1) * LANES] = _unrope(d[BLOCK:], cq, sq)
            d = jnp.concatenate(dvv[j * per:(j + 1) * per], axis=1)
            dv_ref[0, :, j * LANES:(j + 1) * LANES] = d[:BLOCK]
            dv_ref[1, :, j * LANES:(j + 1) * LANES] = d[BLOCK:]

        @pl.when(i == 0)
        def _():
            dsink_ref[...] = jnp.zeros_like(dsink_ref)

        dsink_ref[...] += dsink

    pair = pl.BlockSpec((2, BLOCK, D_KV), lambda i: (i, 0, 0))
    return _call(body, name="attention_bwd", grid=(nb,),
                 in_specs=_attn_specs(t, d_attn, d_in) + [pl.BlockSpec((BLOCK, d_attn), lambda i: (i, 0))],
                 out_specs=[pl.BlockSpec((BLOCK, d_attn), lambda i: (i, 0)), pair, pair,
                            pl.BlockSpec((1, LANES), lambda i: (0, 0))],
                 out_shape=[_sds((t, d_attn), F32), _sds((2 * nb, BLOCK, D_KV), F32), _sds((2 * nb, BLOCK, D_KV), F32),
                            _sds((1, LANES), F32)],
                 semantics=("arbitrary",))(proj, proj, proj, proj, proj, cos, sin, cos, sin, sinks_row, d_out)


def _assemble_dproj(dq, dk2, dv2, du, d_in, after):
    t, d_attn = dq.shape
    d_ssm = du.shape[1]
    nb = t // BLOCK

    def body(dq_ref, dk_own, dk_next, dv_own, dv_next, du_ref, o_ref):
        has_next = (pl.program_id(0) < nb - 1).astype(F32)
        o_ref[:, :d_attn] = dq_ref[...].astype(BF16)
        o_ref[:, d_attn:d_attn + D_KV] = (dk_own[...] + has_next * dk_next[...]).astype(BF16)
        o_ref[:, d_attn + D_KV:d_attn + 2 * D_KV] = (dv_own[...] + has_next * dv_next[...]).astype(BF16)
        o_ref[:, d_attn + 2 * D_KV:] = du_ref[...].astype(BF16)

    own = pl.BlockSpec((None, BLOCK, D_KV), lambda i: (2 * i + 1, 0, 0))
    nxt = pl.BlockSpec((None, BLOCK, D_KV), lambda i: (jnp.minimum(2 * i + 2, 2 * nb - 1), 0, 0))
    return _call(body, name="assemble_dproj", grid=(nb,),
                 in_specs=[pl.BlockSpec((BLOCK, d_attn), lambda i: (i, 0)), own, nxt, own, nxt,
                           pl.BlockSpec((BLOCK, d_ssm), lambda i: (i, 0))],
                 out_specs=pl.BlockSpec((BLOCK, d_in), lambda i: (i, 0)), out_shape=_sds((t, d_in), BF16),
                 semantics=("parallel",), n_after=len(after))(dq, dk2, dk2, dv2, dv2, du, *after)


def _discretise(ar, ai, ldt, br, bi):
    dt = jnp.exp(ldt)
    mag = jnp.exp(ar * dt)
    lam_re = mag * jnp.cos(ai * dt)
    lam_im = mag * jnp.sin(ai * dt)
    den = ar * ar + ai * ai
    nr = lam_re - 1.0
    ni = lam_im
    f_re = (nr * ar + ni * ai) / den
    f_im = (ni * ar - nr * ai) / den
    return (lam_re, lam_im, [f_re * r - f_im * i for r, i in zip(br, bi)], [f_re * i + f_im * r for r, i in zip(br, bi)])


def _whole(arrays):
    return [pl.BlockSpec(a.shape, lambda *_, nd=len(a.shape): (0,) * nd) for a in arrays]


def _channels(ref):
    groups = ref.shape[0] // SSM_GROUP
    return [ref[pl.ds(p, groups, stride=SSM_GROUP), :] for p in range(SSM_GROUP)]


def _store_channels(ref, values):
    groups = ref.shape[0] // SSM_GROUP
    for p, val in enumerate(values):
        ref[pl.ds(p, groups, stride=SSM_GROUP), :] = val


def _s5_discretise(ar, ai, ldt, br, bi):
    ins = [ar, ai, ldt, br, bi]

    def body(ar_ref, ai_ref, ldt_ref, br_ref, bi_ref, lr_ref, li_ref, bbr_ref, bbi_ref):
        lr, li, bbr, bbi = _discretise(ar_ref[...], ai_ref[...], ldt_ref[...], _channels(br_ref), _channels(bi_ref))
        lr_ref[...] = lr
        li_ref[...] = li
        _store_channels(bbr_ref, bbr)
        _store_channels(bbi_ref, bbi)

    outs = [_sds(ar.shape, F32), _sds(ar.shape, F32), _sds(br.shape, F32), _sds(br.shape, F32)]
    return _call(body, name="s5_discretise", in_specs=_whole(ins), out_specs=_whole(outs), out_shape=outs)(*ins)


def _s5_discretise_bwd(ar, ai, ldt, br, bi, d_lr, d_li, d_bbr, d_bbi):
    ins = [ar, ai, ldt, br, bi, d_lr, d_li, d_bbr, d_bbi]

    def body(ar_ref, ai_ref, ldt_ref, br_ref, bi_ref, dlr_ref, dli_ref, dbbr_ref, dbbi_ref,
             dar_ref, dai_ref, dldt_ref, dbr_ref, dbi_ref):
        _, vjp = jax.vjp(_discretise, ar_ref[...], ai_ref[...], ldt_ref[...], _channels(br_ref), _channels(bi_ref))
        dar, dai, dldt, dbr, dbi = vjp((dlr_ref[...], dli_ref[...], _channels(dbbr_ref), _channels(dbbi_ref)))
        dar_ref[...] = dar
        dai_ref[...] = dai
        dldt_ref[...] = dldt
        _store_channels(dbr_ref, dbr)
        _store_channels(dbi_ref, dbi)

    outs = [_sds(a.shape, F32) for a in (ar, ai, ldt, br, bi)]
    return _call(body, name="s5_discretise_bwd", in_specs=_whole(ins), out_specs=_whole(outs), out_shape=outs)(*ins)


def _cmul(ar, ai, br, bi):
    return ar * br - ai * bi, ar * bi + ai * br


def _load_segmented(ref, tile0, n_tiles, seg):
    return jnp.concatenate([ref[pl.ds(tile0 + j, SUBLANES, stride=seg), :] for j in range(n_tiles)], axis=0)


def _store_segmented(ref, tile0, seg, value):
    for j in range(value.shape[0] // SUBLANES):
        ref[pl.ds(tile0 + j, SUBLANES, stride=seg), :] = value[j * SUBLANES:(j + 1) * SUBLANES, :]


def _fill_powers(lr, li, pr_ref, pi_ref, seg):
    pows = [(lr, li)]
    for _ in range(SUBLANES - 1):
        pows.append(_cmul(pows[-1][0], pows[-1][1], lr, li))
    row = lax.broadcasted_iota(jnp.int32, (SUBLANES, lr.shape[1]), 0)
    tr = jnp.zeros((SUBLANES, lr.shape[1]), F32)
    ti = jnp.zeros((SUBLANES, lr.shape[1]), F32)
    for r in range(SUBLANES):
        tr = jnp.where(row == r, pows[r][0], tr)
        ti = jnp.where(row == r, pows[r][1], ti)
    pr_ref[0:SUBLANES, :] = tr
    pi_ref[0:SUBLANES, :] = ti
    k = SUBLANES
    while k < seg:
        fr, fi = pr_ref[k - 1:k, :], pi_ref[k - 1:k, :]
        for t0 in range(0, k, SUBLANES):
            nr, ni = _cmul(pr_ref[t0:t0 + SUBLANES, :], pi_ref[t0:t0 + SUBLANES, :], fr, fi)
            pr_ref[k + t0:k + t0 + SUBLANES, :] = nr
            pi_ref[k + t0:k + t0 + SUBLANES, :] = ni
        k *= 2


def _scan_segments(sr_ref, si_ref, pr_ref, pi_ref, lr, li, seg, reverse, per_tile=None):
    w = lr.shape[1]
    sign = -1.0 if reverse else 1.0
    lrb = jnp.broadcast_to(lr, (SUBLANES, w))
    lib = jnp.broadcast_to(sign * li, (SUBLANES, w))
    zero = jnp.zeros((SUBLANES, w), F32)

    def tile_rows(j):
        return pl.ds(pl.multiple_of(j * SUBLANES, SUBLANES), SUBLANES)

    steps = 4 if seg % 4 == 0 else 1

    def local(i, carry):
        for u in range(steps):
            j = i * steps + u
            rows = tile_rows(seg - 1 - j if reverse else j)
            pr, pi = _cmul(lrb, lib, carry[0], carry[1])
            carry = (sr_ref[rows, :] + pr, si_ref[rows, :] + pi)
            sr_ref[rows, :] = carry[0]
            si_ref[rows, :] = carry[1]
        return carry

    end_r, end_i = lax.fori_loop(0, seg // steps, local, (zero, zero))
    full_r, full_i = pr_ref[seg - 1:seg, :], sign * pi_ref[seg - 1:seg, :]
    row = lax.broadcasted_iota(jnp.int32, (SUBLANES, w), 0)
    in_r, in_i = zero, zero
    cur_r, cur_i = jnp.zeros((1, w), F32), jnp.zeros((1, w), F32)
    for r in (range(SUBLANES - 2, -1, -1) if reverse else range(1, SUBLANES)):
        src = r + 1 if reverse else r - 1
        pr, pi = _cmul(full_r, full_i, cur_r, cur_i)
        cur_r, cur_i = end_r[src:src + 1, :] + pr, end_i[src:src + 1, :] + pi
        in_r = jnp.where(row == r, cur_r, in_r)
        in_i = jnp.where(row == r, cur_i, in_i)

    def carry_in(j, _):
        rows = tile_rows(j)
        k = seg - 1 - j if reverse else j
        pr, pi = _cmul(pr_ref[pl.ds(k, 1), :], sign * pi_ref[pl.ds(k, 1), :], in_r, in_i)
        xr, xi = sr_ref[rows, :] + pr, si_ref[rows, :] + pi
        sr_ref[rows, :] = xr
        si_ref[rows, :] = xi
        if per_tile is not None:
            per_tile(j, xr, xi)
        return 0

    lax.fori_loop(0, seg, carry_in, 0, unroll=4)


_S5_ROWS = 256


def _s5_in_specs(t, d_attn):
    u_block = (d_attn + 2 * D_KV) // SSM_CH_BLOCK
    blk3 = lambda shape: pl.BlockSpec((None,) + shape, lambda j: (j, 0, 0))
    return [
        pl.BlockSpec((t, SSM_CH_BLOCK), lambda j: (0, u_block + j)),
        blk3((SSM_CH_BLOCK, SSM_ST_BLOCK)), blk3((SSM_CH_BLOCK, SSM_ST_BLOCK)),
        blk3((1, SSM_ST_BLOCK)), blk3((1, SSM_ST_BLOCK)),
        blk3((SSM_ST_BLOCK, SSM_CH_BLOCK)), blk3((SSM_ST_BLOCK, SSM_CH_BLOCK)),
        pl.BlockSpec((1, SSM_CH_BLOCK), lambda j: (0, j)),
    ]


def _chunks(t):
    rows = min(_S5_ROWS, t)
    return rows, lambda i: pl.ds(pl.multiple_of(i * rows, rows), rows)


def _s5_states(u_ref, us_ref, bre_ref, bim_ref, lr_ref, li_ref, sr_ref, si_ref, pr_ref, pi_ref, t):
    seg = t // SUBLANES
    rows, chunk = _chunks(t)
    for c in range(t // rows):
        us_ref[c * rows:(c + 1) * rows, :] = _load_segmented(u_ref, c * rows // SUBLANES, rows // SUBLANES, seg)

    def fill(i, _):
        ub = us_ref[chunk(i), :].astype(BF16)
        sr_ref[chunk(i), :] = _dot(ub, bre_ref[...], 1, 0)
        si_ref[chunk(i), :] = _dot(ub, bim_ref[...], 1, 0)
        return 0

    lax.fori_loop(0, t // rows, fill, 0)
    _fill_powers(lr_ref[...], li_ref[...], pr_ref, pi_ref, seg)
    _scan_segments(sr_ref, si_ref, pr_ref, pi_ref, lr_ref[...], li_ref[...], seg, False)


def _s5_scratch(t):
    state = pltpu.VMEM((t, SSM_ST_BLOCK), F32)
    powers = pltpu.VMEM((t // SUBLANES, SSM_ST_BLOCK), F32)
    return state, powers, pltpu.VMEM((t, SSM_CH_BLOCK), F32)


def _s5_fwd(proj, mats, dskip_row, d_attn, d_ssm):
    t = proj.shape[0]
    seg = t // SUBLANES
    n_blocks = d_ssm // SSM_CH_BLOCK
    rows, chunk = _chunks(t)

    def body(u_ref, bre_ref, bim_ref, lr_ref, li_ref, cre_ref, cim_ref, d_ref, y_ref,
             sr_ref, si_ref, pr_ref, pi_ref, us_ref, ys_ref):
        _s5_states(u_ref, us_ref, bre_ref, bim_ref, lr_ref, li_ref, sr_ref, si_ref, pr_ref, pi_ref, t)

        def emit(i, _):
            ys_ref[chunk(i), :] = (_dot(sr_ref[chunk(i), :].astype(BF16), cre_ref[...], 1, 0)
                                   - _dot(si_ref[chunk(i), :].astype(BF16), cim_ref[...], 1, 0)
                                   + d_ref[...] * us_ref[chunk(i), :])
            return 0

        lax.fori_loop(0, t // rows, emit, 0)
        for c in range(t // rows):
            _store_segmented(y_ref, c * rows // SUBLANES, seg, ys_ref[c * rows:(c + 1) * rows, :])

    state, powers, channels = _s5_scratch(t)
    col = pl.BlockSpec((t, SSM_CH_BLOCK), lambda j: (0, j))
    return _call(body, name="s5_fwd", grid=(n_blocks,), in_specs=_s5_in_specs(t, d_attn), out_specs=col,
                 out_shape=_sds((t, d_ssm), F32), scratch_shapes=[state, state, powers, powers, channels, channels],
                 semantics=("parallel",))(proj, *mats, dskip_row)


def _s5_bwd(proj, mats, dskip_row, y, dz_a, dz_b, d_attn, d_ssm, after):
    t = proj.shape[0]
    seg = t // SUBLANES
    n_blocks = d_ssm // SSM_CH_BLOCK
    rows, chunk = _chunks(t)

    def body(u_ref, bre_ref, bim_ref, lr_ref, li_ref, cre_ref, cim_ref, d_ref, y_ref, dza_ref, dzb_ref,
             du_ref, dbre_ref, dbim_ref, dlr_ref, dli_ref, dcre_ref, dcim_ref, dd_ref,
             sr_ref, si_ref, gr_ref, gi_ref, pr_ref, pi_ref, us_ref, dys_ref, dus_ref, acc_r, acc_i):
        _s5_states(u_ref, us_ref, bre_ref, bim_ref, lr_ref, li_ref, sr_ref, si_ref, pr_ref, pi_ref, t)
        for ref in (dcre_ref, dcim_ref, dbre_ref, dbim_ref, dd_ref, acc_r, acc_i):
            ref[...] = jnp.zeros_like(ref)
        for c in range(t // rows):
            tile0, n_tiles = c * rows // SUBLANES, rows // SUBLANES
            dz = _load_segmented(dza_ref, tile0, n_tiles, seg) + _load_segmented(dzb_ref, tile0, n_tiles, seg)
            dys_ref[c * rows:(c + 1) * rows, :] = dz * _gelu_grad(_load_segmented(y_ref, tile0, n_tiles, seg))

        def through_c(i, _):
            dy = dys_ref[chunk(i), :]
            dd_ref[...] += jnp.sum(dy * us_ref[chunk(i), :], axis=0, keepdims=True)
            dyb = dy.astype(BF16)
            gr_ref[chunk(i), :] = _dot(dyb, cre_ref[...], 1, 1)
            gi_ref[chunk(i), :] = -_dot(dyb, cim_ref[...], 1, 1)
            dcre_ref[...] += _dot(sr_ref[chunk(i), :].astype(BF16), dyb, 0, 0)
            dcim_ref[...] -= _dot(si_ref[chunk(i), :].astype(BF16), dyb, 0, 0)
            return 0

        lax.fori_loop(0, t // rows, through_c, 0)

        row = lax.broadcasted_iota(jnp.int32, (SUBLANES, SSM_ST_BLOCK), 0)
        last = pl.ds((seg - 1) * SUBLANES, SUBLANES)
        wrap = [jnp.where(row == 0, 0.0, pltpu.roll(ref[last, :], 1, 0)) for ref in (sr_ref, si_ref)]

        def lambda_grad(j, g_re, g_im):
            before = pl.ds(pl.multiple_of(jnp.maximum(j - 1, 0) * SUBLANES, SUBLANES), SUBLANES)
            prev_r = jnp.where(j > 0, sr_ref[before, :], wrap[0])
            prev_i = jnp.where(j > 0, si_ref[before, :], wrap[1])
            acc_r[...] += g_re * prev_r + g_im * prev_i
            acc_i[...] += g_im * prev_r - g_re * prev_i

        _scan_segments(gr_ref, gi_ref, pr_ref, pi_ref, lr_ref[...], li_ref[...], seg, True, per_tile=lambda_grad)
        dlr_ref[...] = jnp.sum(acc_r[...], axis=0, keepdims=True)
        dli_ref[...] = jnp.sum(acc_i[...], axis=0, keepdims=True)

        def through_b(i, _):
            ub = us_ref[chunk(i), :].astype(BF16)
            grb, gib = gr_ref[chunk(i), :].astype(BF16), gi_ref[chunk(i), :].astype(BF16)
            dbre_ref[...] += _dot(ub, grb, 0, 0)
            dbim_ref[...] += _dot(ub, gib, 0, 0)
            dus_ref[chunk(i), :] = (_dot(grb, bre_ref[...], 1, 1) + _dot(gib, bim_ref[...], 1, 1)
                                    + d_ref[...] * dys_ref[chunk(i), :])
            return 0

        lax.fori_loop(0, t // rows, through_b, 0)
        for c in range(t // rows):
            _store_segmented(du_ref, c * rows // SUBLANES, seg, dus_ref[c * rows:(c + 1) * rows, :])

    col = pl.BlockSpec((t, SSM_CH_BLOCK), lambda j: (0, j))
    blk3 = lambda shape: pl.BlockSpec((None,) + shape, lambda j: (j, 0, 0))
    state, powers, channels = _s5_scratch(t)
    return _call(
        body, name="s5_bwd", grid=(n_blocks,), in_specs=_s5_in_specs(t, d_attn) + [col, col, col],
        out_specs=[col, blk3((SSM_CH_BLOCK, SSM_ST_BLOCK)), blk3((SSM_CH_BLOCK, SSM_ST_BLOCK)),
                   blk3((1, SSM_ST_BLOCK)), blk3((1, SSM_ST_BLOCK)),
                   blk3((SSM_ST_BLOCK, SSM_CH_BLOCK)), blk3((SSM_ST_BLOCK, SSM_CH_BLOCK)),
                   pl.BlockSpec((1, SSM_CH_BLOCK), lambda j: (0, j))],
        out_shape=[_sds((t, d_ssm), F32),
                   _sds((n_blocks, SSM_CH_BLOCK, SSM_ST_BLOCK), F32), _sds((n_blocks, SSM_CH_BLOCK, SSM_ST_BLOCK), F32),
                   _sds((n_blocks, 1, SSM_ST_BLOCK), F32), _sds((n_blocks, 1, SSM_ST_BLOCK), F32),
                   _sds((n_blocks, SSM_ST_BLOCK, SSM_CH_BLOCK), F32), _sds((n_blocks, SSM_ST_BLOCK, SSM_CH_BLOCK), F32),
                   _sds((1, d_ssm), F32)],
        scratch_shapes=[state, state, state, state, powers, powers, channels, channels, channels,
                        pltpu.VMEM((SUBLANES, SSM_ST_BLOCK), F32), pltpu.VMEM((SUBLANES, SSM_ST_BLOCK), F32)],
        semantics=("parallel",), n_after=len(after))(proj, *mats, dskip_row, y, dz_a, dz_b, *after)


def _by_block(gp_n):
    return gp_n.reshape(-1, GROUPS_PER_BLOCK, SSM_GROUP, SSM_STATE)


def _block_diag_in(bbar):
    eye = jnp.eye(GROUPS_PER_BLOCK, dtype=F32)
    return jnp.einsum("jgpn,gh->jgphn", _by_block(bbar), eye).reshape(-1, SSM_CH_BLOCK, SSM_ST_BLOCK)


def _block_diag_in_t(dense):
    d5 = dense.reshape(-1, GROUPS_PER_BLOCK, SSM_GROUP, GROUPS_PER_BLOCK, SSM_STATE)
    eye = jnp.eye(GROUPS_PER_BLOCK, dtype=F32)
    return jnp.einsum("jgphn,gh->jgpn", d5, eye).reshape(-1, SSM_STATE)


def _block_diag_out(c):
    eye = jnp.eye(GROUPS_PER_BLOCK, dtype=F32)
    return jnp.einsum("jgpn,gh->jgnhp", _by_block(c), eye).reshape(-1, SSM_ST_BLOCK, SSM_CH_BLOCK)


def _block_diag_out_t(dense):
    d5 = dense.reshape(-1, GROUPS_PER_BLOCK, SSM_STATE, GROUPS_PER_BLOCK, SSM_GROUP)
    eye = jnp.eye(GROUPS_PER_BLOCK, dtype=F32)
    return jnp.einsum("jgnhp,gh->jgpn", d5, eye).reshape(-1, SSM_STATE)


def _adamw(w, g, m, v):
    m = ADAM_B1 * m + (1.0 - ADAM_B1) * g
    v = ADAM_B2 * v + (1.0 - ADAM_B2) * (g * g)
    m_hat = m / (1.0 - ADAM_B1 ** ADAM_STEP)
    v_hat = v / (1.0 - ADAM_B2 ** ADAM_STEP)
    delta = -ADAM_LR * (m_hat / (jnp.sqrt(v_hat) + ADAM_EPS) + ADAM_WD * w)
    return delta, m, v


def _adam_sharded(name, parts, w, m, v, tr, row0=0):
    r, c = w.shape
    assert r % tr == 0 and row0 % tr == 0, (name, r, tr, row0)

    def body(p_ref, w_ref, m_ref, v_ref, g_out, d_out, m_out, v_out):
        g = p_ref[0].astype(F32)
        for i in range(1, p_ref.shape[0]):
            g = g + p_ref[i].astype(F32)
        delta, m_new, v_new = _adamw(w_ref[...], g, m_ref[...], v_ref[...])
        g_out[...] = g
        d_out[...] = delta
        m_out[...] = m_new
        v_out[...] = v_new

    tile = pl.BlockSpec((tr, c), lambda i: (i, 0))
    return _call(body, name=name, grid=(r // tr,),
                 in_specs=[pl.BlockSpec((parts.shape[0], tr, c), lambda i: (0, i + row0 // tr, 0)), tile, tile, tile],
                 out_specs=[tile] * 4, out_shape=[_sds((r, c), F32)] * 4, semantics=("parallel",))(parts, w, m, v)


_BIG = ("w_in", "w_glu", "w_o", "w_gate", "w_up", "w_down")
_BY_COLUMNS = ("w_in", "w_gate", "w_up")
_SMALL_VECTORS = ("sinks", "log_dt", "b_glu", "g_attn_out", "g_ssm_out", "g_post_mix", "g_pre_ffn", "g_post_ffn")
_SMALL_MATRICES = ("b_re", "b_im", "c_re", "c_im", "a_re", "a_im")
_ORDER = ("g_pre_mix", "w_in", "sinks", "a_re", "a_im", "log_dt", "b_re", "b_im", "c_re", "c_im", "d_skip", "w_glu",
          "b_glu", "g_attn_out", "g_ssm_out", "w_o", "g_post_mix", "g_pre_ffn", "w_gate", "w_up", "w_down",
          "g_post_ffn")


def _pack_grads(vectors, matrices):
    width = max(a.shape[1] for a in vectors)
    slots, row, lane = [], 0, 0
    for a in vectors:
        span = -(-a.shape[1] // LANES) * LANES
        if lane + span > width:
            row, lane = row + 1, 0
        slots.append((row, lane, a.shape[1]))
        lane += span
    firsts, at = [], 0
    for a in matrices:
        firsts.append(at)
        at += a.shape[0]
    nv = len(vectors)

    def body(*refs):
        vec_out, mat_out = refs[-2], refs[-1]
        vec_out[...] = jnp.zeros_like(vec_out)
        for ref, (r, l, w) in zip(refs[:nv], slots):
            vec_out[r:r + 1, l:l + w] = ref[...]
        for ref, r0 in zip(refs[nv:-2], firsts):
            mat_out[r0:r0 + ref.shape[0], :] = ref[...]

    ins = list(vectors) + list(matrices)
    outs = [_sds((-(-(row + 1) // SUBLANES) * SUBLANES, width), F32), _sds((at, matrices[0].shape[1]), F32)]
    vec_pack, mat_pack = _call(body, name="pack_small_grads", in_specs=_whole(ins), out_specs=_whole(outs),
                               out_shape=outs)(*ins)
    return vec_pack, slots, mat_pack, firsts


def _adam_replicated(sources, found_at, w, m, v, total_at):
    ns, n = len(sources), len(w)

    def body(*refs):
        ins, outs = refs[ns:ns + 3 * n], refs[ns + 3 * n:]
        summed = []
        for p_ref in refs[:ns]:
            g = p_ref[0]
            for k in range(1, N_DEV):
                g = g + p_ref[k]
            summed.append(g)
        for i, (src, row, lane) in enumerate(found_at):
            w_ref, m_ref, v_ref = ins[i], ins[n + i], ins[2 * n + i]
            rows, cols = w_ref.shape
            g = summed[src][row:row + rows, lane:lane + cols]
            delta, m_new, v_new = _adamw(w_ref[...], g, m_ref[...], v_ref[...])
            for o, val in zip(outs[4 * i:4 * i + 4], (g, delta, m_new, v_new)):
                o[...] = val
        t_src, t_row, t_lane, t_width = total_at
        outs[-1][...] = summed[t_src][t_row:t_row + 1, t_lane:t_lane + t_width]

    ins = list(sources) + list(w) + list(m) + list(v)
    outs = [_sds(a.shape, F32) for a in w for _ in range(4)] + [_sds((1, total_at[3]), F32)]
    flat = _call(body, name="adam_replicated", in_specs=_whole(ins), out_specs=_whole(outs), out_shape=outs)(*ins)
    return [tuple(flat[4 * i:4 * i + 4]) for i in range(n)], flat[-1]


def kernel(x, positions, g_pre_mix, w_in, sinks, a_re, a_im, log_dt, b_re, b_im, c_re, c_im, d_skip, w_glu, b_glu, g_attn_out, g_ssm_out, w_o, g_post_mix, g_pre_ffn, w_gate, w_up, w_down, g_post_ffn, loss_target, m_g_pre_mix, m_w_in, m_sinks, m_a_re, m_a_im, m_log_dt, m_b_re, m_b_im, m_c_re, m_c_im, m_d_skip, m_w_glu, m_b_glu, m_g_attn_out, m_g_ssm_out, m_w_o, m_g_post_mix, m_g_pre_ffn, m_w_gate, m_w_up, m_w_down, m_g_post_ffn, v_g_pre_mix, v_w_in, v_sinks, v_a_re, v_a_im, v_log_dt, v_b_re, v_b_im, v_c_re, v_c_im, v_d_skip, v_w_glu, v_b_glu, v_g_attn_out, v_g_ssm_out, v_w_o, v_g_post_mix, v_g_pre_ffn, v_w_gate, v_w_up, v_w_down, v_g_post_ffn):
    given = dict(locals())
    weights = {n: given[n] for n in _ORDER}
    mom_m = {n: given["m_" + n] for n in _ORDER}
    mom_v = {n: given["v_" + n] for n in _ORDER}

    t, d = x.shape[1], x.shape[2]
    d_attn = d // 2
    d_ssm = d - d_attn
    d_in = d_attn + 2 * D_KV + d_ssm
    n_groups = d_ssm // SSM_GROUP
    n_heads = d_attn // HEAD_DIM
    tm = min(256, t)

    x2 = x[0]
    target = loss_target[0]

    def by_rows(n, a):
        return a[0].T if n in _BY_COLUMNS else a[0]

    def start_gather(name, ns, token):
        behind = 0 if token is None else token[0, 0].astype(BF16)
        shards = [by_rows(n, weights[n]).astype(BF16) + behind for n in ns]
        return _exchange_start(name, shards, False, (OWN, SIBLING) + CHIP_PEERS)

    def forward_gather(handle, after):
        return _forward_start(handle["name"] + "_forward", _exchange_wait(handle, after))

    def finish_gather(handle, after):
        return _split_wait(forward_gather(handle, after)[0], [])

    ag_in, token = start_gather("gather_w_in", ["w_in"], None)
    ag_mix, token = start_gather("gather_w_glu_o", ["w_glu", "w_o"], token)
    ag_ffn_in, token = start_gather("gather_w_gate_up", ["w_gate", "w_up"], token)
    ag_down, token = start_gather("gather_w_down", ["w_down"], token)

    xn, = _rows("norm_in", lambda xv, g: ([_rms(xv)[0] * g], []), [x2], [g_pre_mix], [(d, BF16)], [], tm,
                after=[token])
    win_g, = finish_gather(ag_in, [xn])
    w_in_t = win_g.reshape(d_in, d)
    proj = _mm_nt("proj_in", xn, w_in_t, F32)

    cos, sin = _rope_tables(positions.reshape(t, 1).astype(F32))
    sinks_row = jnp.pad(sinks, ((0, 0), (0, LANES - n_heads)))
    attn = _attention_fwd(proj, cos, sin, sinks_row, d_attn)

    def view(n, a):
        if n in ("b_re", "b_im"):
            return jnp.transpose(a[0], (0, 2, 1)).reshape(-1, SSM_STATE)
        if n in ("c_re", "c_im"):
            return a[0].reshape(-1, SSM_STATE)
        return a[0].T if n == "d_skip" else a[0] if a.ndim == 3 else a

    def unview(n, val):
        if n in ("b_re", "b_im"):
            return jnp.transpose(val.reshape(n_groups, SSM_GROUP, SSM_STATE), (0, 2, 1))[None]
        if n in ("c_re", "c_im"):
            return val.reshape(1, n_groups, SSM_GROUP, SSM_STATE)
        return val.T[None] if n == "d_skip" else val[None] if weights[n].ndim == 3 else val

    b_re_v, b_im_v = view("b_re", b_re), view("b_im", b_im)
    ldt_col = log_dt.reshape(n_groups, 1)
    lam_re, lam_im, bbar_re, bbar_im = _s5_discretise(a_re[0], a_im[0], ldt_col, b_re_v, b_im_v)
    n_blocks = n_groups // GROUPS_PER_BLOCK
    mats = [_block_diag_in(bbar_re).astype(BF16), _block_diag_in(bbar_im).astype(BF16),
            lam_re.reshape(n_blocks, 1, SSM_ST_BLOCK), lam_im.reshape(n_blocks, 1, SSM_ST_BLOCK),
            _block_diag_out(view("c_re", c_re)).astype(BF16), _block_diag_out(view("c_im", c_im)).astype(BF16)]
    dskip_row = d_skip.reshape(1, d_ssm)
    forward_mix, _ = forward_gather(ag_mix, [attn])
    y_ssm = _s5_fwd(proj, mats, dskip_row, d_attn, d_ssm)
    gelu_bf16 = lambda yv: _gelu(yv).astype(BF16)
    wglu_g, wo_g = _split_wait(forward_mix, [y_ssm])
    w_glu_full = wglu_g.reshape(d_ssm, d_ssm)
    w_o_full = wo_g.reshape(d, d)
    glu_lin = _mm_nn("glu_gate", y_ssm, w_glu_full, F32, a_fn=gelu_bf16)

    def mix_prep(av, yv, gl, bg, ga, gs):
        ssm = _gelu(yv) * _sigmoid(gl + bg)
        return [jnp.concatenate([_rms(av)[0] * ga, _rms(ssm)[0] * gs], axis=1)], []

    mixed, = _rows("mix_prep", mix_prep, [attn, y_ssm, glu_lin], [b_glu, g_attn_out, g_ssm_out], [(d, BF16)], [], tm)
    mix = _mm_nn("mix_out", mixed, w_o_full, F32)

    def post_mix(xv, mv, gpm, gpf):
        h = xv + _rms(mv)[0] * gpm
        return [h, _rms(h)[0] * gpf], []

    forward_ffn_in, token = forward_gather(ag_ffn_in, [mix])
    h, hn = _rows("post_mix", post_mix, [x2, mix], [g_post_mix, g_pre_ffn], [(d, F32), (d, BF16)], [], tm,
                  after=[token])
    wgate_g, wup_g = _split_wait(forward_ffn_in, [hn])
    gate, up, hid = _ffn_in(hn, wgate_g, wup_g)
    wdown_g, = finish_gather(ag_down, [hid])
    ff = _mm_contract_slots("ffn_down", [(hid, wdown_g)], F32, per_step=2, tm=1024)

    def head(hv, fv, tv, gpo):
        out = hv + _rms(fv)[0] * gpo
        err = out - tv
        dout = err * (1.0 / d)
        dff, dg = _rms_bwd(fv, gpo, dout)
        loss = jnp.zeros((1, LANES), F32) + 0.5 * jnp.sum(err * err) * (1.0 / d)
        return [dff, dout], [dg, loss]

    dff, dh_out, dg_post_ffn, loss_row = _rows("loss_head", head, [h, ff, target], [g_post_ffn],
                                               [(d, BF16), (d, F32)], [d, LANES], tm)

    def swap_halves(name, grads):
        return _halves_start("swap_" + name, [g.reshape(N_DEV // 2, 2, *g.shape[1:]) for g in grads])

    def scatter_chip_sums(name, swap, after):
        both = _split_wait(swap, after)
        half = len(both) // 2
        sums = [_chip_sum("chip_sum_%s_%d" % (name, i), both[i], both[half + i]) for i in range(half)]
        return _exchange_start("scatter_" + name, sums, True, (OWN,) + CHIP_PEERS, by_chip=True)

    dw_down = _mm_slots_tn("ffn_down_dw", hid, dff, BF16)
    swap_down, token = swap_halves("dw_down", [dw_down])
    dgate, dup = _ffn_down_bwd(dff, wdown_g, gate, up, [token])
    rs_down, token = scatter_chip_sums("dw_down", swap_down, [dgate])
    dhn = _mm_contract_slots("ffn_in_dx", [(dgate, wgate_g), (dup, wup_g)], F32, per_step=2, tm=1024, tn=1024,
                             after=[token])
    dw_gate = _mm_slots_tn("ffn_gate_dw", dgate, hn, BF16)
    dw_up = _mm_slots_tn("ffn_up_dw", dup, hn, BF16)
    swap_ffn_in, tok_ffn_in = swap_halves("dw_gate_up", [dw_gate, dw_up])

    def mid_bwd(dho, dhn_, hv, mv, gpf, gpm):
        d1, dgpf = _rms_bwd(hv, gpf, dhn_)
        dh_ = dho + d1
        dmix_, dgpm = _rms_bwd(mv, gpm, dh_)
        return [dh_, dmix_], [dgpf, dgpm]

    dh, dmix, dg_pre_ffn, dg_post_mix = _rows("mid_bwd", mid_bwd, [dh_out, dhn, h, mix], [g_pre_ffn, g_post_mix],
                                              [(d, F32), (d, BF16)], [d, d], tm, after=[tok_ffn_in])

    dmixed = _mm_nt("mix_out_dx", dmix, w_o_full, F32)
    rs_ffn_in, token = scatter_chip_sums("dw_gate_up", swap_ffn_in, [dmixed])
    dw_o = _mm_tn("mix_out_dw", mixed, dmix, BF16, after=[token])
    swap_o, tok_o = swap_halves("dw_o", [dw_o.reshape(N_DEV, d // N_DEV, d)])

    def mix_bwd(dm, av, yv, gl, bg, ga, gs):
        dattn_, dga = _rms_bwd(av, ga, dm[:, :d_attn])
        z = _gelu(yv)
        sg = _sigmoid(gl + bg)
        dssm, dgs = _rms_bwd(z * sg, gs, dm[:, d_attn:])
        dgl = dssm * z * sg * (1.0 - sg)
        return [dattn_, dssm * sg, dgl], [dga, dgs, jnp.sum(dgl, axis=0, keepdims=True)]

    dattn, dz_direct, dglu, dg_attn_out, dg_ssm_out, db_glu = _rows(
        "mix_bwd", mix_bwd, [dmixed, attn, y_ssm, glu_lin], [b_glu, g_attn_out, g_ssm_out],
        [(d_attn, F32), (d_ssm, F32), (d_ssm, BF16)], [d_attn, d_ssm, d_ssm], tm, after=[tok_o])
    dz_glu = _mm_nt("glu_gate_dx", dglu, w_glu_full, F32)
    dw_glu = _mm_tn("glu_gate_dw", y_ssm, dglu, BF16, a_fn=gelu_bf16)
    rs_o, token = scatter_chip_sums("dw_o", swap_o, [dz_glu, dw_glu])

    du, db_re_dense, db_im_dense, dlam_re, dlam_im, dc_re_dense, dc_im_dense, dd_skip = _s5_bwd(
        proj, mats, dskip_row, y_ssm, dz_direct, dz_glu, d_attn, d_ssm, [token])
    da_re, da_im, dlog_dt, db_re_v, db_im_v = _s5_discretise_bwd(
        a_re[0], a_im[0], ldt_col, b_re_v, b_im_v, dlam_re.reshape(n_groups, SSM_STATE),
        dlam_im.reshape(n_groups, SSM_STATE), _block_diag_in_t(db_re_dense), _block_diag_in_t(db_im_dense))
    dq, dk2, dv2, dsinks_row = _attention_bwd(proj, cos, sin, sinks_row, dattn, d_attn)

    small_grads = {
        "sinks": dsinks_row, "a_re": da_re, "a_im": da_im, "log_dt": dlog_dt.reshape(1, n_groups),
        "b_re": db_re_v, "b_im": db_im_v, "c_re": _block_diag_out_t(dc_re_dense),
        "c_im": _block_diag_out_t(dc_im_dense), "d_skip": dd_skip.reshape(n_groups, SSM_GROUP).T, "b_glu": db_glu,
        "g_attn_out": dg_attn_out, "g_ssm_out": dg_ssm_out, "g_post_mix": dg_post_mix, "g_pre_ffn": dg_pre_ffn,
        "g_post_ffn": dg_post_ffn,
    }
    vec_pack, vec_slots, mat_pack, mat_rows = _pack_grads([small_grads[n] for n in _SMALL_VECTORS] + [loss_row],
                                                          [small_grads[n] for n in _SMALL_MATRICES])
    ag_small, token = _exchange_start("gather_small_grads", [vec_pack, mat_pack, small_grads["d_skip"]], False,
                                      (OWN,) + ALL_PEERS)
    dproj = _assemble_dproj(dq, dk2, dv2, du, d_in, [token])

    dxn = _mm_nn("proj_in_dx", dproj, w_in_t, F32)
    dw_in = _mm_tn("proj_in_dw", dproj, xn, BF16).reshape(N_DEV, d_in // N_DEV, d)
    swap_in, token = swap_halves("dw_in_glu", [dw_in, dw_glu.reshape(N_DEV, d_ssm // N_DEV, d_ssm)])

    def x_bwd(dh_, dxn_, xv, g):
        dx, dg = _rms_bwd(xv, g, dxn_)
        return [dh_ + dx], [dg]

    grad_x, dg_pre_mix = _rows("norm_in_bwd", x_bwd, [dh, dxn, x2], [g_pre_mix], [(d, F32)], [d], tm, after=[token])
    ag_last, token = _exchange_start("gather_g_pre_mix_grad", [dg_pre_mix], False, (OWN,) + ALL_PEERS)
    rs_in, token = scatter_chip_sums("dw_in_glu", swap_in, [grad_x, token])

    results = {}

    def adam_big(n, parts):
        r = parts.shape[1]
        tr = next((c for c in range(192, 15, -16) if r % c == 0), r)
        results[n] = _adam_sharded("adam_" + n, parts, by_rows(n, weights[n]), by_rows(n, mom_m[n]),
                                   by_rows(n, mom_v[n]), tr)
        return results[n][3]

    done = [grad_x, token]
    adam_big("w_down", _exchange_wait(rs_down, done)[0])
    p_gate, p_up = _exchange_wait(rs_ffn_in, done)
    done = [adam_big("w_gate", p_gate), adam_big("w_up", p_up), results["w_down"][3]]
    done = [adam_big("w_o", _exchange_wait(rs_o, done)[0])]
    vec_parts, mat_parts, dskip_parts = _exchange_wait(ag_small, done)
    first_gain_parts, = _exchange_wait(ag_last, done)
    for n, row0 in zip(_SMALL_MATRICES, mat_rows):
        rows = view(n, weights[n]).shape[0]
        results[n] = _adam_sharded("adam_" + n, mat_parts, view(n, weights[n]), view(n, mom_m[n]), view(n, mom_v[n]),
                                   rows, row0)
    rest = _SMALL_VECTORS + ("d_skip", "g_pre_mix")
    found_at = [(0, row, lane) for row, lane, _ in vec_slots[:-1]] + [(1, 0, 0), (2, 0, 0)]
    updated, loss_sum = _adam_replicated([vec_parts, dskip_parts, first_gain_parts], found_at,
                                         [view(n, weights[n]) for n in rest], [view(n, mom_m[n]) for n in rest],
                                         [view(n, mom_v[n]) for n in rest], (0,) + vec_slots[-1])
    results.update(zip(rest, updated))
    p_in, p_glu = _exchange_wait(rs_in, [results[n][3] for n in _SMALL_MATRICES] + [updated[0][3]])
    adam_big("w_in", p_in)
    adam_big("w_glu", p_glu)

    outs = [loss_sum[0, 0], grad_x[None]]
    for k in range(4):
        for n in _ORDER:
            val = results[n][k]
            outs.append(val.T[None] if n in _BY_COLUMNS else val[None] if n in _BIG else unview(n, val))
    return tuple(outs)
```

```python
import math

import jax
import jax.numpy as jnp
from jax import lax
from jax.experimental import pallas as pl
from jax.experimental.pallas import tpu as pltpu

F32 = jnp.float32
BF16 = jnp.bfloat16

HEAD_DIM = 64
N_KV_HEADS = 4
D_KV = N_KV_HEADS * HEAD_DIM
WINDOW = 128
BLOCK = 128
ROPE_THETA = 10000.0
SSM_GROUP = 16
SSM_STATE = 64
GROUPS_PER_BLOCK = 8
SSM_CH_BLOCK = GROUPS_PER_BLOCK * SSM_GROUP
SSM_ST_BLOCK = GROUPS_PER_BLOCK * SSM_STATE
RMS_EPS = 1e-6
N_DEV = 8
LANES = 128
SUBLANES = 8
MASKED = -1e30

ADAM_LR = 0.001
ADAM_B1 = 0.9
ADAM_B2 = 0.999
ADAM_EPS = 1e-08
ADAM_WD = 0.01
ADAM_STEP = 10

VMEM_LIMIT_BYTES = 56 * 1024 * 1024


def _call(body, *, name, out_shape, in_specs, out_specs, grid=(), scratch_shapes=(), semantics=None, n_after=0):
    params = dict(vmem_limit_bytes=VMEM_LIMIT_BYTES)
    if semantics is not None:
        params["dimension_semantics"] = semantics
    n_in = len(in_specs)
    if n_after:
        inner = body

        def body(*refs):
            inner(*refs[:n_in], *refs[n_in + n_after:])

        in_specs = list(in_specs) + [pl.BlockSpec(memory_space=pl.ANY)] * n_after
    return pl.pallas_call(body, name=name, grid=grid, in_specs=in_specs, out_specs=out_specs, out_shape=out_shape,
                          scratch_shapes=scratch_shapes, compiler_params=pltpu.CompilerParams(**params))


def _sds(shape, dtype):
    return jax.ShapeDtypeStruct(tuple(shape), dtype)


def _dot(a, b, ca, cb):
    return lax.dot_general(a, b, (((ca,), (cb,)), ((), ())), preferred_element_type=F32)


def _rms(x):
    r = lax.rsqrt(jnp.mean(x * x, axis=-1, keepdims=True) + RMS_EPS)
    return x * r, r


def _rms_bwd(x, g, dy):
    xh, r = _rms(x)
    dxh = dy * g
    dx = r * (dxh - xh * jnp.mean(dxh * xh, axis=-1, keepdims=True))
    return dx, jnp.sum(dy * xh, axis=0, keepdims=True)


def _sigmoid(x):
    return 1.0 / (1.0 + jnp.exp(-x))


_GELU_C = math.sqrt(2.0 / math.pi)
_GELU_A = 0.044715


def _gelu(y):
    t = jnp.tanh(_GELU_C * (y + _GELU_A * y * y * y))
    return 0.5 * y * (1.0 + t)


def _gelu_grad(y):
    t = jnp.tanh(_GELU_C * (y + _GELU_A * y * y * y))
    return 0.5 * (1.0 + t) + 0.5 * y * (1.0 - t * t) * _GELU_C * (1.0 + 3.0 * _GELU_A * y * y)


def _rows(name, fn, row_ins, vec_ins, row_outs, acc_widths, tm, after=()):
    rows = row_ins[0].shape[0]
    assert rows % tm == 0, (name, rows, tm)
    n_row, n_vec, n_out, n_acc = len(row_ins), len(vec_ins), len(row_outs), len(acc_widths)

    def body(*refs):
        ins = [r[...] for r in refs[:n_row + n_vec]]
        outs = refs[n_row + n_vec:n_row + n_vec + n_out]
        accs = refs[n_row + n_vec + n_out:]
        row_vals, acc_vals = fn(*ins)
        for o, v in zip(outs, row_vals):
            o[...] = v.astype(o.dtype)
        if n_acc:
            @pl.when(pl.program_id(0) == 0)
            def _():
                for a in accs:
                    a[...] = jnp.zeros_like(a)
            for a, v in zip(accs, acc_vals):
                a[...] += v

    in_specs = [pl.BlockSpec((tm, a.shape[1]), lambda i: (i, 0)) for a in row_ins]
    in_specs += [pl.BlockSpec(v.shape, lambda i: (0, 0)) for v in vec_ins]
    out_specs = [pl.BlockSpec((tm, w), lambda i: (i, 0)) for w, _ in row_outs]
    out_specs += [pl.BlockSpec((1, w), lambda i: (0, 0)) for w in acc_widths]
    out_shape = [_sds((rows, w), dt) for w, dt in row_outs] + [_sds((1, w), F32) for w in acc_widths]
    return _call(body, name=name, grid=(rows // tm,), in_specs=in_specs, out_specs=out_specs, out_shape=out_shape,
                 semantics=("arbitrary",) if n_acc else ("parallel",), n_after=len(after))(*row_ins, *vec_ins, *after)


def _matmul(name, operands, in_specs, product, grid, out_shape, out_spec, acc_shape, after=()):
    nk = grid[-1]
    n_in = len(operands)
    in_place = out_shape.dtype == F32

    def body(*refs):
        ins = [r[...] for r in refs[:n_in]]
        o_ref = refs[n_in]
        if nk == 1:
            o_ref[...] = product(*ins).astype(o_ref.dtype)
            return
        acc = o_ref if in_place else refs[n_in + 1]
        k = pl.program_id(len(grid) - 1)

        @pl.when(k == 0)
        def _():
            acc[...] = jnp.zeros_like(acc)

        acc[...] += product(*ins)

        if not in_place:
            @pl.when(k == nk - 1)
            def _():
                o_ref[...] = acc[...].astype(o_ref.dtype)

    return _call(body, name=name, grid=grid, in_specs=in_specs, out_specs=out_spec, out_shape=out_shape,
                 scratch_shapes=[] if nk == 1 or in_place else [pltpu.VMEM(acc_shape, F32)],
                 semantics=("parallel",) * (len(grid) - 1) + ("arbitrary",), n_after=len(after))(*operands, *after)


def _mm_nn(name, a, b, out_dtype, tm=512, tn=None, a_fn=lambda x: x, after=()):
    m, k = a.shape
    n = b.shape[1]
    tm, tn = min(tm, m), n if tn is None else tn
    return _matmul(name, [a, b],
                   [pl.BlockSpec((tm, k), lambda i, j, s: (i, 0)), pl.BlockSpec((k, tn), lambda i, j, s: (0, j))],
                   lambda x, y: _dot(a_fn(x), y, 1, 0), (m // tm, n // tn, 1), _sds((m, n), out_dtype),
                   pl.BlockSpec((tm, tn), lambda i, j, s: (i, j)), (tm, tn), after)


def _mm_nt(name, a, b, out_dtype, tm=512, tn=None):
    m, k = a.shape
    n = b.shape[0]
    tm, tn = min(tm, m), n if tn is None else tn
    return _matmul(name, [a, b],
                   [pl.BlockSpec((tm, k), lambda i, j, s: (i, 0)), pl.BlockSpec((tn, k), lambda i, j, s: (j, 0))],
                   lambda x, y: _dot(x, y, 1, 1), (m // tm, n // tn, 1), _sds((m, n), out_dtype),
                   pl.BlockSpec((tm, tn), lambda i, j, s: (i, j)), (tm, tn))


def _mm_tn(name, a, b, out_dtype, tm=512, tn=None, tk=2048, a_fn=lambda x: x, after=()):
    k, m = a.shape
    n = b.shape[1]
    tm, tk, tn = min(tm, m), min(tk, k), n if tn is None else tn
    return _matmul(name, [a, b],
                   [pl.BlockSpec((tk, tm), lambda i, j, s: (s, i)), pl.BlockSpec((tk, tn), lambda i, j, s: (s, j))],
                   lambda x, y: _dot(a_fn(x), y, 0, 0), (m // tm, n // tn, k // tk), _sds((m, n), out_dtype),
                   pl.BlockSpec((tm, tn), lambda i, j, s: (i, j)), (tm, tn), after)


def _mm_contract_slots(name, pairs, out_dtype, per_step, tm=512, tn=2048, after=()):
    s_, m, k = pairs[0][0].shape
    n = pairs[0][1].shape[2]
    tm, tn = min(tm, m), min(tn, n)
    ops, specs = [], []
    for a, b in pairs:
        ops += [a, b]
        specs += [pl.BlockSpec((per_step, tm, k), lambda i, j, s: (s, i, 0)),
                  pl.BlockSpec((per_step, k, tn), lambda i, j, s: (s, 0, j))]

    def product(*t):
        return sum(_dot(t[2 * p][q], t[2 * p + 1][q], 1, 0) for p in range(len(pairs)) for q in range(per_step))

    return _matmul(name, ops, specs, product, (m // tm, n // tn, s_ // per_step), _sds((m, n), out_dtype),
                   pl.BlockSpec((tm, tn), lambda i, j, s: (i, j)), (tm, tn), after)


def _mm_slots_tn(name, a, b, out_dtype, tn=2048, tk=2048):
    s_, k, m = a.shape
    n = b.shape[1]
    tn, tk = min(tn, n), min(tk, k)
    return _matmul(name, [a, b],
                   [pl.BlockSpec((None, tk, m), lambda s, j, z: (s, z, 0)), pl.BlockSpec((tk, tn), lambda s, j, z: (z, j))],
                   lambda x, y: _dot(x, y, 0, 0), (s_, n // tn, k // tk), _sds((s_, m, n), out_dtype),
                   pl.BlockSpec((None, m, tn), lambda s, j, z: (s, 0, j)), (m, tn))


def _ffn_in(a, w_gate, w_up, tm=1024):
    m, k = a.shape
    s_, n, _ = w_gate.shape
    tm = min(tm, m)

    def body(a_ref, wg_ref, wu_ref, g_ref, u_ref, h_ref):
        x = a_ref[...]
        g = _dot(x, wg_ref[...], 1, 1)
        u = _dot(x, wu_ref[...], 1, 1)
        g_ref[...] = g.astype(BF16)
        u_ref[...] = u.astype(BF16)
        h_ref[...] = (g * _sigmoid(g) * u).astype(BF16)

    w_spec = pl.BlockSpec((None, n, k), lambda s, i: (s, 0, 0))
    o_spec = pl.BlockSpec((None, tm, n), lambda s, i: (s, i, 0))
    return _call(body, name="ffn_in", grid=(s_, m // tm),
                 in_specs=[pl.BlockSpec((tm, k), lambda s, i: (i, 0)), w_spec, w_spec], out_specs=[o_spec] * 3,
                 out_shape=[_sds((s_, m, n), BF16)] * 3, semantics=("parallel", "parallel"))(a, w_gate, w_up)


def _ffn_down_bwd(d_out, w_down, gate, up, after, tm=1024):
    m, k = d_out.shape
    s_, n, _ = w_down.shape
    tm = min(tm, m)

    def body(d_ref, w_ref, g_ref, u_ref, dg_ref, du_ref):
        rows = pl.ds(pl.multiple_of(pl.program_id(1) * tm, tm), tm)
        dh = _dot(d_ref[rows, :], w_ref[...], 1, 1)
        g = g_ref[...].astype(F32)
        sg = _sigmoid(g)
        dg_ref[...] = (dh * u_ref[...].astype(F32) * sg * (1.0 + g * (1.0 - sg))).astype(BF16)
        du_ref[...] = (dh * g * sg).astype(BF16)

    t_spec = pl.BlockSpec((None, tm, n), lambda s, i: (s, i, 0))
    return _call(body, name="ffn_down_dx", grid=(s_, m // tm),
                 in_specs=[pl.BlockSpec((m, k), lambda s, i: (0, 0)), pl.BlockSpec((None, n, k), lambda s, i: (s, 0, 0)),
                           t_spec, t_spec],
                 out_specs=[t_spec] * 2, out_shape=[_sds((s_, m, n), BF16)] * 2, semantics=("parallel", "parallel"),
                 n_after=len(after))(d_out, w_down, gate, up, *after)


ALL_PEERS = (1, 2, 3, 4, 5, 6, 7)
CHIP_PEERS = (2, 4, 6)
SIBLING = 1
OWN = 0


def _peer(relation):
    x, y, c = lax.axis_index("x"), lax.axis_index("y"), lax.axis_index("c")
    pos = (1 - x if relation & 4 else x, 1 - y if relation & 2 else y, 1 - c if relation & 1 else c)
    return pos, 4 * pos[0] + 2 * pos[1] + pos[2]


def _slot(relation, by_chip):
    pos, device = _peer(relation)
    return 2 * pos[0] + pos[1] if by_chip else device


def _exchange_copies(ins, lands, send_sems, recv_sems, scatter, relations, by_chip=False):
    me = _slot(0, by_chip)

    def copy(a, s, peer, pos, dst_slot):
        return pltpu.make_async_remote_copy(
            src_ref=ins[a].at[peer] if scatter else ins[a], dst_ref=lands[a].at[dst_slot],
            send_sem=send_sems.at[s], recv_sem=recv_sems.at[s], device_id=pos, device_id_type=pl.DeviceIdType.MESH)

    pairs = []
    for k, r in enumerate(relations):
        pos, peer = _peer(r)[0], _slot(r, by_chip)
        for a in range(len(ins)):
            s = a * len(relations) + k
            pairs.append((copy(a, s, peer, pos, me), copy(a, s, peer, pos, peer)))
    return pairs


def _halves_copies(arrays, lands, send_sems, recv_sems):
    sibling, _ = _peer(SIBLING)
    core = lax.axis_index("c")
    pairs = []
    for a, (ref, land) in enumerate(zip(arrays, lands)):
        send = pltpu.make_async_remote_copy(
            src_ref=ref.at[:, pl.ds(1 - core, 1)], dst_ref=land, send_sem=send_sems.at[a], recv_sem=recv_sems.at[a],
            device_id=sibling, device_id_type=pl.DeviceIdType.MESH)
        pairs.append((send, send))
    return pairs


def _forward_copies(lands, send_sems, recv_sems):
    sibling, _ = _peer(SIBLING)

    def copy(a, s, slot):
        return pltpu.make_async_remote_copy(
            src_ref=lands[a].at[slot], dst_ref=lands[a].at[slot], send_sem=send_sems.at[s], recv_sem=recv_sems.at[s],
            device_id=sibling, device_id_type=pl.DeviceIdType.MESH)

    pairs = []
    for k, r in enumerate(CHIP_PEERS):
        _, mine = _peer(r)
        _, theirs = _peer(r | SIBLING)
        for a in range(len(lands)):
            s = a * len(CHIP_PEERS) + k
            pairs.append((copy(a, s, mine), copy(a, s, theirs)))
    return pairs


_HBM_SPEC = pl.BlockSpec(memory_space=pltpu.HBM)
_SEM_SPEC = pl.BlockSpec(memory_space=pltpu.SEMAPHORE)
_SIDE_EFFECT = pltpu.SideEffectType.DATAFLOW_SIDE_EFFECTING


def _split_start(name, operands, n_sem, make_pairs):
    k = len(operands)

    def body(*refs):
        send_sems, recv_sems, token = refs[k], refs[k + 1], refs[-1]
        for send, _ in make_pairs(refs[:k], send_sems, recv_sems):
            send.start()
        token[...] = jnp.zeros_like(token)

    out = pl.pallas_call(
        body, name=name,
        out_shape=(pltpu.SemaphoreType.DMA((n_sem,)), pltpu.SemaphoreType.DMA((n_sem,)),
                   *[pltpu.HBM(a.shape, a.dtype) for a in operands], _sds((SUBLANES, LANES), F32)),
        in_specs=[_HBM_SPEC] * k,
        out_specs=(_SEM_SPEC, _SEM_SPEC, *[_HBM_SPEC] * k, pl.BlockSpec(memory_space=pltpu.VMEM)),
        input_output_aliases={i: 2 + i for i in range(k)},
        compiler_params=pltpu.CompilerParams(has_side_effects=_SIDE_EFFECT),
    )(*[pltpu.with_memory_space_constraint(a, pltpu.HBM) for a in operands])
    return dict(name=name, sems=out[:2], thru=list(out[2:2 + k]), make_pairs=make_pairs), out[-1]


def _split_wait(handle, after):
    thru, make_pairs = handle["thru"], handle["make_pairs"]
    k = len(thru)

    def body(*refs):
        for send, arrival in make_pairs(refs[:k], refs[k], refs[k + 1]):
            send.wait_send()
            arrival.wait_recv()

    return pl.pallas_call(
        body, name=handle["name"] + "_wait", out_shape=[pltpu.HBM(a.shape, a.dtype) for a in thru],
        in_specs=[_HBM_SPEC] * k + [_SEM_SPEC, _SEM_SPEC] + [pl.BlockSpec(memory_space=pl.ANY)] * len(after),
        out_specs=[_HBM_SPEC] * k, input_output_aliases={i: i for i in range(k)},
        compiler_params=pltpu.CompilerParams(has_side_effects=_SIDE_EFFECT),
    )(*thru, *handle["sems"], *after)


def _exchange_start(name, arrays, scatter, relations, by_chip=False):
    n = len(arrays)
    lands = [lax.empty(a.shape if scatter else (N_DEV,) + a.shape, a.dtype) for a in arrays]

    def make_pairs(refs, send_sems, recv_sems):
        return _exchange_copies(refs[:n], refs[n:], send_sems, recv_sems, scatter, relations, by_chip)

    handle, token = _split_start(name, list(arrays) + lands, n * len(relations), make_pairs)
    handle.update(n=n)
    return handle, token


def _halves_start(name, arrays):
    lands = [lax.empty((a.shape[0], 1) + a.shape[2:], a.dtype) for a in arrays]
    n = len(arrays)

    def make_pairs(refs, send_sems, recv_sems):
        return _halves_copies(refs[:n], refs[n:], send_sems, recv_sems)

    return _split_start(name, list(arrays) + lands, n, make_pairs)


def _chip_sum(name, array, landed):
    chips, _, r, c = array.shape
    tr = r // 2 if r > 512 and r % 32 == 0 else r

    def body(a_ref, b_ref, o_ref):
        mine = a_ref[lax.axis_index("c")].astype(F32)
        o_ref[...] = (mine + b_ref[...].astype(F32)).astype(o_ref.dtype)

    return _call(body, name=name, grid=(chips, r // tr),
                 in_specs=[pl.BlockSpec((None, 2, tr, c), lambda k, i: (k, 0, i, 0)),
                           pl.BlockSpec((None, None, tr, c), lambda k, i: (k, 0, i, 0))],
                 out_specs=pl.BlockSpec((None, tr, c), lambda k, i: (k, i, 0)),
                 out_shape=_sds((chips, r, c), BF16), semantics=("parallel", "parallel"))(array, landed)


def _forward_start(name, lands):
    return _split_start(name, list(lands), len(lands) * len(CHIP_PEERS), _forward_copies)


def _exchange_wait(handle, after):
    return _split_wait(handle, after)[handle["n"]:]


def _rope_tables(pos_col):
    t = pos_col.shape[0]
    half = HEAD_DIM // 2
    inv_freq = ROPE_THETA ** (-jnp.arange(half, dtype=F32) / half)
    inv_row = jnp.tile(inv_freq, LANES // half)[None, :]

    def body(pos_ref, inv_ref, cos_ref, sin_ref):
        ang = pos_ref[...] * inv_ref[...]
        cos_ref[...] = jnp.cos(ang)
        sin_ref[...] = jnp.sin(ang)

    tm = min(t, 512)
    return _call(body, name="rope_tables", grid=(t // tm,),
                 in_specs=[pl.BlockSpec((tm, 1), lambda i: (i, 0)), pl.BlockSpec((1, LANES), lambda i: (0, 0))],
                 out_specs=[pl.BlockSpec((tm, LANES), lambda i: (i, 0))] * 2,
                 out_shape=[_sds((t, LANES), F32)] * 2, semantics=("parallel",))(pos_col, inv_row)


def _rot_half(x):
    lane = lax.broadcasted_iota(jnp.int32, x.shape, 1)
    low = (lane % HEAD_DIM) < HEAD_DIM // 2
    return jnp.where(low, -pltpu.roll(x, LANES - HEAD_DIM // 2, 1), pltpu.roll(x, HEAD_DIM // 2, 1))


def _rope(x, cos, sin):
    return x * cos + _rot_half(x) * sin


def _unrope(d, cos, sin):
    return d * cos - _rot_half(d) * sin


def _band_mask(first_block, heads):
    r = lax.broadcasted_iota(jnp.int32, (heads * BLOCK, 2 * BLOCK), 0) % BLOCK
    c = lax.broadcasted_iota(jnp.int32, (heads * BLOCK, 2 * BLOCK), 1)
    diff = r - c + BLOCK
    return (diff >= 0) & (diff < WINDOW) & ((c >= BLOCK) | jnp.logical_not(first_block))


def _attn_specs(t, d_attn, d_in):
    kb, vb = d_attn // D_KV, d_attn // D_KV + 1
    prev = lambda i: jnp.maximum(i - 1, 0)
    return [
        pl.BlockSpec((BLOCK, d_attn), lambda i: (i, 0)),
        pl.BlockSpec((BLOCK, D_KV), lambda i: (i, kb)),
        pl.BlockSpec((BLOCK, D_KV), lambda i: (i, vb)),
        pl.BlockSpec((BLOCK, D_KV), lambda i: (prev(i), kb)),
        pl.BlockSpec((BLOCK, D_KV), lambda i: (prev(i), vb)),
        pl.BlockSpec((BLOCK, LANES), lambda i: (i, 0)),
        pl.BlockSpec((BLOCK, LANES), lambda i: (i, 0)),
        pl.BlockSpec((BLOCK, LANES), lambda i: (prev(i), 0)),
        pl.BlockSpec((BLOCK, LANES), lambda i: (prev(i), 0)),
        pl.BlockSpec((1, LANES), lambda i: (0, 0)),
    ]


def _head(x, h):
    return x[:, h * HEAD_DIM:(h + 1) * HEAD_DIM]


def _attn_heads(q_ref, kc_ref, vc_ref, kp_ref, vp_ref, cq_ref, sq_ref, cp_ref, sp_ref, d_attn):
    cq, sq, cp, sp = cq_ref[...], sq_ref[...], cp_ref[...], sp_ref[...]
    q_rot = [_rope(q_ref[:, j * LANES:(j + 1) * LANES], cq, sq) for j in range(d_attn // LANES)]
    kc_rot = [_rope(kc_ref[:, j * LANES:(j + 1) * LANES], cq, sq) for j in range(D_KV // LANES)]
    kp_rot = [_rope(kp_ref[:, j * LANES:(j + 1) * LANES], cp, sp) for j in range(D_KV // LANES)]
    per = LANES // HEAD_DIM
    q_heads = [_head(q_rot[h // per], h % per).astype(BF16) for h in range(d_attn // HEAD_DIM)]
    kk = [jnp.concatenate([_head(kp_rot[g // per], g % per), _head(kc_rot[g // per], g % per)], axis=0).astype(BF16)
          for g in range(N_KV_HEADS)]
    vv = [jnp.concatenate([_head(vp_ref[...], g), _head(vc_ref[...], g)], axis=0).astype(BF16) for g in range(N_KV_HEADS)]
    return q_heads, kk, vv


def _stack_group(q_heads, sink_ref, group):
    q_all = jnp.concatenate([q_heads[h] for h in group], axis=0)
    sink_all = jnp.concatenate([jnp.broadcast_to(sink_ref[:, h:h + 1], (BLOCK, 1)) for h in group], axis=0)
    return q_all, sink_all


def _softmax_with_sink(q, kk, sink, mask):
    s = _dot(q, kk, 1, 1) * (1.0 / math.sqrt(HEAD_DIM))
    s = jnp.where(mask, s, MASKED)
    m = jnp.maximum(jnp.max(s, axis=-1, keepdims=True), sink)
    p = jnp.exp(s - m)
    e_sink = jnp.exp(sink - m)
    inv = 1.0 / (jnp.sum(p, axis=-1, keepdims=True) + e_sink)
    return p * inv, e_sink * inv


def _attention_fwd(proj, cos, sin, sinks_row, d_attn):
    t, d_in = proj.shape
    n_heads = d_attn // HEAD_DIM
    q_per_kv = n_heads // N_KV_HEADS

    def body(q_ref, kc_ref, vc_ref, kp_ref, vp_ref, cq_ref, sq_ref, cp_ref, sp_ref, sink_ref, o_ref):
        mask = _band_mask(pl.program_id(0) == 0, q_per_kv)
        q_heads, kk, vv = _attn_heads(q_ref, kc_ref, vc_ref, kp_ref, vp_ref, cq_ref, sq_ref, cp_ref, sp_ref, d_attn)
        for g in range(N_KV_HEADS):
            group = range(g * q_per_kv, (g + 1) * q_per_kv)
            q_all, sink_all = _stack_group(q_heads, sink_ref, group)
            probs, _ = _softmax_with_sink(q_all, kk[g], sink_all, mask)
            o_all = _dot(probs.astype(BF16), vv[g], 1, 0)
            for k, h in enumerate(group):
                o_ref[:, h * HEAD_DIM:(h + 1) * HEAD_DIM] = o_all[k * BLOCK:(k + 1) * BLOCK]

    return _call(body, name="attention_fwd", grid=(t // BLOCK,), in_specs=_attn_specs(t, d_attn, d_in),
                 out_specs=pl.BlockSpec((BLOCK, d_attn), lambda i: (i, 0)), out_shape=_sds((t, d_attn), F32),
                 semantics=("parallel",))(proj, proj, proj, proj, proj, cos, sin, cos, sin, sinks_row)


def _attention_bwd(proj, cos, sin, sinks_row, d_out, d_attn):
    t, d_in = proj.shape
    n_heads = d_attn // HEAD_DIM
    q_per_kv = n_heads // N_KV_HEADS
    nb = t // BLOCK
    per = LANES // HEAD_DIM
    stack = q_per_kv

    def body(q_ref, kc_ref, vc_ref, kp_ref, vp_ref, cq_ref, sq_ref, cp_ref, sp_ref, sink_ref, do_ref,
             dq_ref, dk_ref, dv_ref, dsink_ref):
        i = pl.program_id(0)
        mask = _band_mask(i == 0, stack)
        q_heads, kk, vv = _attn_heads(q_ref, kc_ref, vc_ref, kp_ref, vp_ref, cq_ref, sq_ref, cp_ref, sp_ref, d_attn)
        lane = lax.broadcasted_iota(jnp.int32, (1, LANES), 1)
        dsink = jnp.zeros((1, LANES), F32)
        dq_rot, dkk, dvv = [], [], []
        for g in range(N_KV_HEADS):
            dkk_g = jnp.zeros((2 * BLOCK, HEAD_DIM), F32)
            dvv_g = jnp.zeros((2 * BLOCK, HEAD_DIM), F32)
            for first in range(g * q_per_kv, (g + 1) * q_per_kv, stack):
                group = range(first, first + stack)
                q_all, sink_all = _stack_group(q_heads, sink_ref, group)
                probs, p_sink = _softmax_with_sink(q_all, kk[g], sink_all, mask)
                do_all = jnp.concatenate([do_ref[:, h * HEAD_DIM:(h + 1) * HEAD_DIM] for h in group],
                                         axis=0).astype(BF16)
                dp = _dot(do_all, vv[g], 1, 1)
                delta = jnp.sum(probs * dp, axis=-1, keepdims=True)
                ds = (probs * (dp - delta) * (1.0 / math.sqrt(HEAD_DIM))).astype(BF16)
                dq_all = _dot(ds, kk[g], 1, 0)
                dkk_g += _dot(ds, q_all, 0, 0)
                dvv_g += _dot(probs.astype(BF16), do_all, 0, 0)
                sink_term = p_sink * delta
                for k, h in enumerate(group):
                    dq_rot.append(dq_all[k * BLOCK:(k + 1) * BLOCK])
                    part = jnp.sum(sink_term[k * BLOCK:(k + 1) * BLOCK], axis=0, keepdims=True)
                    dsink += jnp.where(lane == h, -part, 0.0)
            dkk.append(dkk_g)
            dvv.append(dvv_g)
        cq, sq, cp, sp = cq_ref[...], sq_ref[...], cp_ref[...], sp_ref[...]
        for j in range(d_attn // LANES):
            d = jnp.concatenate(dq_rot[j * per:(j + 1) * per], axis=1)
            dq_ref[:, j * LANES:(j + 1) * LANES] = _unrope(d, cq, sq)
        for j in range(D_KV // LANES):
            d = jnp.concatenate(dkk[j * per:(j + 1) * per], axis=1)
            dk_ref[0, :, j * LANES:(j + 1) * LANES] = _unrope(d[:BLOCK], cp, sp)
            dk_ref[1, :, j * LANES:(j + 1) * LANES] = _unrope(d[BLOCK:], cq, sq)
            d = jnp.concatenate(dvv[j * per:(j + 1) * per], axis=1)
            dv_ref[0, :, j * LANES:(j + 1) * LANES] = d[:BLOCK]
            dv_ref[1, :, j * LANES:(j + 1) * LANES] = d[BLOCK:]

        @pl.when(i == 0)
        def _():
            dsink_ref[...] = jnp.zeros_like(dsink_ref)

        dsink_ref[...] += dsink

    pair = pl.BlockSpec((2, BLOCK, D_KV), lambda i: (i, 0, 0))
    return _call(body, name="attention_bwd", grid=(nb,),
                 in_specs=_attn_specs(t, d_attn, d_in) + [pl.BlockSpec((BLOCK, d_attn), lambda i: (i, 0))],
                 out_specs=[pl.BlockSpec((BLOCK, d_attn), lambda i: (i, 0)), pair, pair,
                            pl.BlockSpec((1, LANES), lambda i: (0, 0))],
                 out_shape=[_sds((t, d_attn), F32), _sds((2 * nb, BLOCK, D_KV), F32), _sds((2 * nb, BLOCK, D_KV), F32),
                            _sds((1, LANES), F32)],
                 semantics=("arbitrary",))(proj, proj, proj, proj, proj, cos, sin, cos, sin, sinks_row, d_out)


def _assemble_dproj(dq, dk2, dv2, du, d_in, after):
    t, d_attn = dq.shape
    d_ssm = du.shape[1]
    nb = t // BLOCK

    def body(dq_ref, dk_own, dk_next, dv_own, dv_next, du_ref, o_ref):
        has_next = (pl.program_id(0) < nb - 1).astype(F32)
        o_ref[:, :d_attn] = dq_ref[...].astype(BF16)
        o_ref[:, d_attn:d_attn + D_KV] = (dk_own[...] + has_next * dk_next[...]).astype(BF16)
        o_ref[:, d_attn + D_KV:d_attn + 2 * D_KV] = (dv_own[...] + has_next * dv_next[...]).astype(BF16)
        o_ref[:, d_attn + 2 * D_KV:] = du_ref[...].astype(BF16)

    own = pl.BlockSpec((None, BLOCK, D_KV), lambda i: (2 * i + 1, 0, 0))
    nxt = pl.BlockSpec((None, BLOCK, D_KV), lambda i: (jnp.minimum(2 * i + 2, 2 * nb - 1), 0, 0))
    return _call(body, name="assemble_dproj", grid=(nb,),
                 in_specs=[pl.BlockSpec((BLOCK, d_attn), lambda i: (i, 0)), own, nxt, own, nxt,
                           pl.BlockSpec((BLOCK, d_ssm), lambda i: (i, 0))],
                 out_specs=pl.BlockSpec((BLOCK, d_in), lambda i: (i, 0)), out_shape=_sds((t, d_in), BF16),
                 semantics=("parallel",), n_after=len(after))(dq, dk2, dk2, dv2, dv2, du, *after)


def _discretise(ar, ai, ldt, br, bi):
    dt = jnp.exp(ldt)
    mag = jnp.exp(ar * dt)
    lam_re = mag * jnp.cos(ai * dt)
    lam_im = mag * jnp.sin(ai * dt)
    den = ar * ar + ai * ai
    nr = lam_re - 1.0
    ni = lam_im
    f_re = (nr * ar + ni * ai) / den
    f_im = (ni * ar - nr * ai) / den
    return (lam_re, lam_im, [f_re * r - f_im * i for r, i in zip(br, bi)], [f_re * i + f_im * r for r, i in zip(br, bi)])


def _whole(arrays):
    return [pl.BlockSpec(a.shape, lambda *_, nd=len(a.shape): (0,) * nd) for a in arrays]


def _channels(ref):
    groups = ref.shape[0] // SSM_GROUP
    return [ref[pl.ds(p, groups, stride=SSM_GROUP), :] for p in range(SSM_GROUP)]


def _store_channels(ref, values):
    groups = ref.shape[0] // SSM_GROUP
    for p, val in enumerate(values):
        ref[pl.ds(p, groups, stride=SSM_GROUP), :] = val


def _s5_discretise(ar, ai, ldt, br, bi):
    ins = [ar, ai, ldt, br, bi]

    def body(ar_ref, ai_ref, ldt_ref, br_ref, bi_ref, lr_ref, li_ref, bbr_ref, bbi_ref):
        lr, li, bbr, bbi = _discretise(ar_ref[...], ai_ref[...], ldt_ref[...], _channels(br_ref), _channels(bi_ref))
        lr_ref[...] = lr
        li_ref[...] = li
        _store_channels(bbr_ref, bbr)
        _store_channels(bbi_ref, bbi)

    outs = [_sds(ar.shape, F32), _sds(ar.shape, F32), _sds(br.shape, F32), _sds(br.shape, F32)]
    return _call(body, name="s5_discretise", in_specs=_whole(ins), out_specs=_whole(outs), out_shape=outs)(*ins)


def _s5_discretise_bwd(ar, ai, ldt, br, bi, d_lr, d_li, d_bbr, d_bbi):
    ins = [ar, ai, ldt, br, bi, d_lr, d_li, d_bbr, d_bbi]

    def body(ar_ref, ai_ref, ldt_ref, br_ref, bi_ref, dlr_ref, dli_ref, dbbr_ref, dbbi_ref,
             dar_ref, dai_ref, dldt_ref, dbr_ref, dbi_ref):
        _, vjp = jax.vjp(_discretise, ar_ref[...], ai_ref[...], ldt_ref[...], _channels(br_ref), _channels(bi_ref))
        dar, dai, dldt, dbr, dbi = vjp((dlr_ref[...], dli_ref[...], _channels(dbbr_ref), _channels(dbbi_ref)))
        dar_ref[...] = dar
        dai_ref[...] = dai
        dldt_ref[...] = dldt
        _store_channels(dbr_ref, dbr)
        _store_channels(dbi_ref, dbi)

    outs = [_sds(a.shape, F32) for a in (ar, ai, ldt, br, bi)]
    return _call(body, name="s5_discretise_bwd", in_specs=_whole(ins), out_specs=_whole(outs), out_shape=outs)(*ins)


def _cmul(ar, ai, br, bi):
    return ar * br - ai * bi, ar * bi + ai * br


def _load_segmented(ref, tile0, n_tiles, seg):
    return jnp.concatenate([ref[pl.ds(tile0 + j, SUBLANES, stride=seg), :] for j in range(n_tiles)], axis=0)


def _store_segmented(ref, tile0, seg, value):
    for j in range(value.shape[0] // SUBLANES):
        ref[pl.ds(tile0 + j, SUBLANES, stride=seg), :] = value[j * SUBLANES:(j + 1) * SUBLANES, :]


def _fill_powers(lr, li, pr_ref, pi_ref, seg):
    pows = [(lr, li)]
    for _ in range(SUBLANES - 1):
        pows.append(_cmul(pows[-1][0], pows[-1][1], lr, li))
    row = lax.broadcasted_iota(jnp.int32, (SUBLANES, lr.shape[1]), 0)
    tr = jnp.zeros((SUBLANES, lr.shape[1]), F32)
    ti = jnp.zeros((SUBLANES, lr.shape[1]), F32)
    for r in range(SUBLANES):
        tr = jnp.where(row == r, pows[r][0], tr)
        ti = jnp.where(row == r, pows[r][1], ti)
    pr_ref[0:SUBLANES, :] = tr
    pi_ref[0:SUBLANES, :] = ti
    k = SUBLANES
    while k < seg:
        fr, fi = pr_ref[k - 1:k, :], pi_ref[k - 1:k, :]
        for t0 in range(0, k, SUBLANES):
            nr, ni = _cmul(pr_ref[t0:t0 + SUBLANES, :], pi_ref[t0:t0 + SUBLANES, :], fr, fi)
            pr_ref[k + t0:k + t0 + SUBLANES, :] = nr
            pi_ref[k + t0:k + t0 + SUBLANES, :] = ni
        k *= 2


def _scan_segments(sr_ref, si_ref, pr_ref, pi_ref, lr, li, seg, reverse, per_tile=None):
    w = lr.shape[1]
    sign = -1.0 if reverse else 1.0
    lrb = jnp.broadcast_to(lr, (SUBLANES, w))
    lib = jnp.broadcast_to(sign * li, (SUBLANES, w))
    zero = jnp.zeros((SUBLANES, w), F32)

    def tile_rows(j):
        return pl.ds(pl.multiple_of(j * SUBLANES, SUBLANES), SUBLANES)

    steps = 4 if seg % 4 == 0 else 1

    def local(i, carry):
        for u in range(steps):
            j = i * steps + u
            rows = tile_rows(seg - 1 - j if reverse else j)
            pr, pi = _cmul(lrb, lib, carry[0], carry[1])
            carry = (sr_ref[rows, :] + pr, si_ref[rows, :] + pi)
            sr_ref[rows, :] = carry[0]
            si_ref[rows, :] = carry[1]
        return carry

    end_r, end_i = lax.fori_loop(0, seg // steps, local, (zero, zero))
    full_r, full_i = pr_ref[seg - 1:seg, :], sign * pi_ref[seg - 1:seg, :]
    row = lax.broadcasted_iota(jnp.int32, (SUBLANES, w), 0)
    in_r, in_i = zero, zero
    cur_r, cur_i = jnp.zeros((1, w), F32), jnp.zeros((1, w), F32)
    for r in (range(SUBLANES - 2, -1, -1) if reverse else range(1, SUBLANES)):
        src = r + 1 if reverse else r - 1
        pr, pi = _cmul(full_r, full_i, cur_r, cur_i)
        cur_r, cur_i = end_r[src:src + 1, :] + pr, end_i[src:src + 1, :] + pi
        in_r = jnp.where(row == r, cur_r, in_r)
        in_i = jnp.where(row == r, cur_i, in_i)

    def carry_in(j, _):
        rows = tile_rows(j)
        k = seg - 1 - j if reverse else j
        pr, pi = _cmul(pr_ref[pl.ds(k, 1), :], sign * pi_ref[pl.ds(k, 1), :], in_r, in_i)
        xr, xi = sr_ref[rows, :] + pr, si_ref[rows, :] + pi
        sr_ref[rows, :] = xr
        si_ref[rows, :] = xi
        if per_tile is not None:
            per_tile(j, xr, xi)
        return 0

    lax.fori_loop(0, seg, carry_in, 0, unroll=4)


_S5_ROWS = 256


def _s5_in_specs(t, d_attn):
    u_block = (d_attn + 2 * D_KV) // SSM_CH_BLOCK
    blk3 = lambda shape: pl.BlockSpec((None,) + shape, lambda j: (j, 0, 0))
    return [
        pl.BlockSpec((t, SSM_CH_BLOCK), lambda j: (0, u_block + j)),
        blk3((SSM_CH_BLOCK, SSM_ST_BLOCK)), blk3((SSM_CH_BLOCK, SSM_ST_BLOCK)),
        blk3((1, SSM_ST_BLOCK)), blk3((1, SSM_ST_BLOCK)),
        blk3((SSM_ST_BLOCK, SSM_CH_BLOCK)), blk3((SSM_ST_BLOCK, SSM_CH_BLOCK)),
        pl.BlockSpec((1, SSM_CH_BLOCK), lambda j: (0, j)),
    ]


def _chunks(t):
    rows = min(_S5_ROWS, t)
    return rows, lambda i: pl.ds(pl.multiple_of(i * rows, rows), rows)


def _s5_states(u_ref, us_ref, bre_ref, bim_ref, lr_ref, li_ref, sr_ref, si_ref, pr_ref, pi_ref, t):
    seg = t // SUBLANES
    rows, chunk = _chunks(t)
    for c in range(t // rows):
        us_ref[c * rows:(c + 1) * rows, :] = _load_segmented(u_ref, c * rows // SUBLANES, rows // SUBLANES, seg)

    def fill(i, _):
        ub = us_ref[chunk(i), :].astype(BF16)
        sr_ref[chunk(i), :] = _dot(ub, bre_ref[...], 1, 0)
        si_ref[chunk(i), :] = _dot(ub, bim_ref[...], 1, 0)
        return 0

    lax.fori_loop(0, t // rows, fill, 0)
    _fill_powers(lr_ref[...], li_ref[...], pr_ref, pi_ref, seg)
    _scan_segments(sr_ref, si_ref, pr_ref, pi_ref, lr_ref[...], li_ref[...], seg, False)


def _s5_scratch(t):
    state = pltpu.VMEM((t, SSM_ST_BLOCK), F32)
    powers = pltpu.VMEM((t // SUBLANES, SSM_ST_BLOCK), F32)
    return state, powers, pltpu.VMEM((t, SSM_CH_BLOCK), F32)


def _s5_fwd(proj, mats, dskip_row, d_attn, d_ssm):
    t = proj.shape[0]
    seg = t // SUBLANES
    n_blocks = d_ssm // SSM_CH_BLOCK
    rows, chunk = _chunks(t)

    def body(u_ref, bre_ref, bim_ref, lr_ref, li_ref, cre_ref, cim_ref, d_ref, y_ref,
             sr_ref, si_ref, pr_ref, pi_ref, us_ref, ys_ref):
        _s5_states(u_ref, us_ref, bre_ref, bim_ref, lr_ref, li_ref, sr_ref, si_ref, pr_ref, pi_ref, t)

        def emit(i, _):
            ys_ref[chunk(i), :] = (_dot(sr_ref[chunk(i), :].astype(BF16), cre_ref[...], 1, 0)
                                   - _dot(si_ref[chunk(i), :].astype(BF16), cim_ref[...], 1, 0)
                                   + d_ref[...] * us_ref[chunk(i), :])
            return 0

        lax.fori_loop(0, t // rows, emit, 0)
        for c in range(t // rows):
            _store_segmented(y_ref, c * rows // SUBLANES, seg, ys_ref[c * rows:(c + 1) * rows, :])

    state, powers, channels = _s5_scratch(t)
    col = pl.BlockSpec((t, SSM_CH_BLOCK), lambda j: (0, j))
    return _call(body, name="s5_fwd", grid=(n_blocks,), in_specs=_s5_in_specs(t, d_attn), out_specs=col,
                 out_shape=_sds((t, d_ssm), F32), scratch_shapes=[state, state, powers, powers, channels, channels],
                 semantics=("parallel",))(proj, *mats, dskip_row)


def _s5_bwd(proj, mats, dskip_row, y, dz_a, dz_b, d_attn, d_ssm, after):
    t = proj.shape[0]
    seg = t // SUBLANES
    n_blocks = d_ssm // SSM_CH_BLOCK
    rows, chunk = _chunks(t)

    def body(u_ref, bre_ref, bim_ref, lr_ref, li_ref, cre_ref, cim_ref, d_ref, y_ref, dza_ref, dzb_ref,
             du_ref, dbre_ref, dbim_ref, dlr_ref, dli_ref, dcre_ref, dcim_ref, dd_ref,
             sr_ref, si_ref, gr_ref, gi_ref, pr_ref, pi_ref, us_ref, dys_ref, dus_ref, acc_r, acc_i):
        _s5_states(u_ref, us_ref, bre_ref, bim_ref, lr_ref, li_ref, sr_ref, si_ref, pr_ref, pi_ref, t)
        for ref in (dcre_ref, dcim_ref, dbre_ref, dbim_ref, dd_ref, acc_r, acc_i):
            ref[...] = jnp.zeros_like(ref)
        for c in range(t // rows):
            tile0, n_tiles = c * rows // SUBLANES, rows // SUBLANES
            dz = _load_segmented(dza_ref, tile0, n_tiles, seg) + _load_segmented(dzb_ref, tile0, n_tiles, seg)
            dys_ref[c * rows:(c + 1) * rows, :] = dz * _gelu_grad(_load_segmented(y_ref, tile0, n_tiles, seg))

        def through_c(i, _):
            dy = dys_ref[chunk(i), :]
            dd_ref[...] += jnp.sum(dy * us_ref[chunk(i), :], axis=0, keepdims=True)
            dyb = dy.astype(BF16)
            gr_ref[chunk(i), :] = _dot(dyb, cre_ref[...], 1, 1)
            gi_ref[chunk(i), :] = -_dot(dyb, cim_ref[...], 1, 1)
            dcre_ref[...] += _dot(sr_ref[chunk(i), :].astype(BF16), dyb, 0, 0)
            dcim_ref[...] -= _dot(si_ref[chunk(i), :].astype(BF16), dyb, 0, 0)
            return 0

        lax.fori_loop(0, t // rows, through_c, 0)

        row = lax.broadcasted_iota(jnp.int32, (SUBLANES, SSM_ST_BLOCK), 0)
        last = pl.ds((seg - 1) * SUBLANES, SUBLANES)
        wrap = [jnp.where(row == 0, 0.0, pltpu.roll(ref[last, :], 1, 0)) for ref in (sr_ref, si_ref)]

        def lambda_grad(j, g_re, g_im):
            before = pl.ds(pl.multiple_of(jnp.maximum(j - 1, 0) * SUBLANES, SUBLANES), SUBLANES)
            prev_r = jnp.where(j > 0, sr_ref[before, :], wrap[0])
            prev_i = jnp.where(j > 0, si_ref[before, :], wrap[1])
            acc_r[...] += g_re * prev_r + g_im * prev_i
            acc_i[...] += g_im * prev_r - g_re * prev_i

        _scan_segments(gr_ref, gi_ref, pr_ref, pi_ref, lr_ref[...], li_ref[...], seg, True, per_tile=lambda_grad)
        dlr_ref[...] = jnp.sum(acc_r[...], axis=0, keepdims=True)
        dli_ref[...] = jnp.sum(acc_i[...], axis=0, keepdims=True)

        def through_b(i, _):
            ub = us_ref[chunk(i), :].astype(BF16)
            grb, gib = gr_ref[chunk(i), :].astype(BF16), gi_ref[chunk(i), :].astype(BF16)
            dbre_ref[...] += _dot(ub, grb, 0, 0)
            dbim_ref[...] += _dot(ub, gib, 0, 0)
            dus_ref[chunk(i), :] = (_dot(grb, bre_ref[...], 1, 1) + _dot(gib, bim_ref[...], 1, 1)
                                    + d_ref[...] * dys_ref[chunk(i), :])
            return 0

        lax.fori_loop(0, t // rows, through_b, 0)
        for c in range(t // rows):
            _store_segmented(du_ref, c * rows // SUBLANES, seg, dus_ref[c * rows:(c + 1) * rows, :])

    col = pl.BlockSpec((t, SSM_CH_BLOCK), lambda j: (0, j))
    blk3 = lambda shape: pl.BlockSpec((None,) + shape, lambda j: (j, 0, 0))
    state, powers, channels = _s5_scratch(t)
    return _call(
        body, name="s5_bwd", grid=(n_blocks,), in_specs=_s5_in_specs(t, d_attn) + [col, col, col],
        out_specs=[col, blk3((SSM_CH_BLOCK, SSM_ST_BLOCK)), blk3((SSM_CH_BLOCK, SSM_ST_BLOCK)),
                   blk3((1, SSM_ST_BLOCK)), blk3((1, SSM_ST_BLOCK)),
                   blk3((SSM_ST_BLOCK, SSM_CH_BLOCK)), blk3((SSM_ST_BLOCK, SSM_CH_BLOCK)),
                   pl.BlockSpec((1, SSM_CH_BLOCK), lambda j: (0, j))],
        out_shape=[_sds((t, d_ssm), F32),
                   _sds((n_blocks, SSM_CH_BLOCK, SSM_ST_BLOCK), F32), _sds((n_blocks, SSM_CH_BLOCK, SSM_ST_BLOCK), F32),
                   _sds((n_blocks, 1, SSM_ST_BLOCK), F32), _sds((n_blocks, 1, SSM_ST_BLOCK), F32),
                   _sds((n_blocks, SSM_ST_BLOCK, SSM_CH_BLOCK), F32), _sds((n_blocks, SSM_ST_BLOCK, SSM_CH_BLOCK), F32),
                   _sds((1, d_ssm), F32)],
        scratch_shapes=[state, state, state, state, powers, powers, channels, channels, channels,
                        pltpu.VMEM((SUBLANES, SSM_ST_BLOCK), F32), pltpu.VMEM((SUBLANES, SSM_ST_BLOCK), F32)],
        semantics=("parallel",), n_after=len(after))(proj, *mats, dskip_row, y, dz_a, dz_b, *after)


def _by_block(gp_n):
    return gp_n.reshape(-1, GROUPS_PER_BLOCK, SSM_GROUP, SSM_STATE)


def _block_diag_in(bbar):
    eye = jnp.eye(GROUPS_PER_BLOCK, dtype=F32)
    return jnp.einsum("jgpn,gh->jgphn", _by_block(bbar), eye).reshape(-1, SSM_CH_BLOCK, SSM_ST_BLOCK)


def _block_diag_in_t(dense):
    d5 = dense.reshape(-1, GROUPS_PER_BLOCK, SSM_GROUP, GROUPS_PER_BLOCK, SSM_STATE)
    eye = jnp.eye(GROUPS_PER_BLOCK, dtype=F32)
    return jnp.einsum("jgphn,gh->jgpn", d5, eye).reshape(-1, SSM_STATE)


def _block_diag_out(c):
    eye = jnp.eye(GROUPS_PER_BLOCK, dtype=F32)
    return jnp.einsum("jgpn,gh->jgnhp", _by_block(c), eye).reshape(-1, SSM_ST_BLOCK, SSM_CH_BLOCK)


def _block_diag_out_t(dense):
    d5 = dense.reshape(-1, GROUPS_PER_BLOCK, SSM_STATE, GROUPS_PER_BLOCK, SSM_GROUP)
    eye = jnp.eye(GROUPS_PER_BLOCK, dtype=F32)
    return jnp.einsum("jgnhp,gh->jgpn", d5, eye).reshape(-1, SSM_STATE)


def _adamw(w, g, m, v):
    m = ADAM_B1 * m + (1.0 - ADAM_B1) * g
    v = ADAM_B2 * v + (1.0 - ADAM_B2) * (g * g)
    m_hat = m / (1.0 - ADAM_B1 ** ADAM_STEP)
    v_hat = v / (1.0 - ADAM_B2 ** ADAM_STEP)
    delta = -ADAM_LR * (m_hat / (jnp.sqrt(v_hat) + ADAM_EPS) + ADAM_WD * w)
    return delta, m, v


def _adam_sharded(name, parts, w, m, v, tr, row0=0):
    r, c = w.shape
    assert r % tr == 0 and row0 % tr == 0, (name, r, tr, row0)

    def body(p_ref, w_ref, m_ref, v_ref, g_out, d_out, m_out, v_out):
        g = p_ref[0].astype(F32)
        for i in range(1, p_ref.shape[0]):
            g = g + p_ref[i].astype(F32)
        delta, m_new, v_new = _adamw(w_ref[...], g, m_ref[...], v_ref[...])
        g_out[...] = g
        d_out[...] = delta
        m_out[...] = m_new
        v_out[...] = v_new

    tile = pl.BlockSpec((tr, c), lambda i: (i, 0))
    return _call(body, name=name, grid=(r // tr,),
                 in_specs=[pl.BlockSpec((parts.shape[0], tr, c), lambda i: (0, i + row0 // tr, 0)), tile, tile, tile],
                 out_specs=[tile] * 4, out_shape=[_sds((r, c), F32)] * 4, semantics=("parallel",))(parts, w, m, v)


_BIG = ("w_in", "w_glu", "w_o", "w_gate", "w_up", "w_down")
_BY_COLUMNS = ("w_in", "w_gate", "w_up")
_SMALL_VECTORS = ("sinks", "log_dt", "b_glu", "g_attn_out", "g_ssm_out", "g_post_mix", "g_pre_ffn", "g_post_ffn")
_SMALL_MATRICES = ("b_re", "b_im", "c_re", "c_im", "a_re", "a_im")
_ORDER = ("g_pre_mix", "w_in", "sinks", "a_re", "a_im", "log_dt", "b_re", "b_im", "c_re", "c_im", "d_skip", "w_glu",
          "b_glu", "g_attn_out", "g_ssm_out", "w_o", "g_post_mix", "g_pre_ffn", "w_gate", "w_up", "w_down",
          "g_post_ffn")


def _pack_grads(vectors, matrices):
    width = max(a.shape[1] for a in vectors)
    slots, row, lane = [], 0, 0
    for a in vectors:
        span = -(-a.shape[1] // LANES) * LANES
        if lane + span > width:
            row, lane = row + 1, 0
        slots.append((row, lane, a.shape[1]))
        lane += span
    firsts, at = [], 0
    for a in matrices:
        firsts.append(at)
        at += a.shape[0]
    nv = len(vectors)

    def body(*refs):
        vec_out, mat_out = refs[-2], refs[-1]
        vec_out[...] = jnp.zeros_like(vec_out)
        for ref, (r, l, w) in zip(refs[:nv], slots):
            vec_out[r:r + 1, l:l + w] = ref[...]
        for ref, r0 in zip(refs[nv:-2], firsts):
            mat_out[r0:r0 + ref.shape[0], :] = ref[...]

    ins = list(vectors) + list(matrices)
    outs = [_sds((-(-(row + 1) // SUBLANES) * SUBLANES, width), F32), _sds((at, matrices[0].shape[1]), F32)]
    vec_pack, mat_pack = _call(body, name="pack_small_grads", in_specs=_whole(ins), out_specs=_whole(outs),
                               out_shape=outs)(*ins)
    return vec_pack, slots, mat_pack, firsts


def _adam_replicated(sources, found_at, w, m, v, total_at):
    ns, n = len(sources), len(w)

    def body(*refs):
        ins, outs = refs[ns:ns + 3 * n], refs[ns + 3 * n:]
        summed = []
        for p_ref in refs[:ns]:
            g = p_ref[0]
            for k in range(1, N_DEV):
                g = g + p_ref[k]
            summed.append(g)
        for i, (src, row, lane) in enumerate(found_at):
            w_ref, m_ref, v_ref = ins[i], ins[n + i], ins[2 * n + i]
            rows, cols = w_ref.shape
            g = summed[src][row:row + rows, lane:lane + cols]
            delta, m_new, v_new = _adamw(w_ref[...], g, m_ref[...], v_ref[...])
            for o, val in zip(outs[4 * i:4 * i + 4], (g, delta, m_new, v_new)):
                o[...] = val
        t_src, t_row, t_lane, t_width = total_at
        outs[-1][...] = summed[t_src][t_row:t_row + 1, t_lane:t_lane + t_width]

    ins = list(sources) + list(w) + list(m) + list(v)
    outs = [_sds(a.shape, F32) for a in w for _ in range(4)] + [_sds((1, total_at[3]), F32)]
    flat = _call(body, name="adam_replicated", in_specs=_whole(ins), out_specs=_whole(outs), out_shape=outs)(*ins)
    return [tuple(flat[4 * i:4 * i + 4]) for i in range(n)], flat[-1]


def kernel(x, positions, g_pre_mix, w_in, sinks, a_re, a_im, log_dt, b_re, b_im, c_re, c_im, d_skip, w_glu, b_glu, g_attn_out, g_ssm_out, w_o, g_post_mix, g_pre_ffn, w_gate, w_up, w_down, g_post_ffn, loss_target, m_g_pre_mix, m_w_in, m_sinks, m_a_re, m_a_im, m_log_dt, m_b_re, m_b_im, m_c_re, m_c_im, m_d_skip, m_w_glu, m_b_glu, m_g_attn_out, m_g_ssm_out, m_w_o, m_g_post_mix, m_g_pre_ffn, m_w_gate, m_w_up, m_w_down, m_g_post_ffn, v_g_pre_mix, v_w_in, v_sinks, v_a_re, v_a_im, v_log_dt, v_b_re, v_b_im, v_c_re, v_c_im, v_d_skip, v_w_glu, v_b_glu, v_g_attn_out, v_g_ssm_out, v_w_o, v_g_post_mix, v_g_pre_ffn, v_w_gate, v_w_up, v_w_down, v_g_post_ffn):
    given = dict(locals())
    weights = {n: given[n] for n in _ORDER}
    mom_m = {n: given["m_" + n] for n in _ORDER}
    mom_v = {n: given["v_" + n] for n in _ORDER}

    t, d = x.shape[1], x.shape[2]
    d_attn = d // 2
    d_ssm = d - d_attn
    d_in = d_attn + 2 * D_KV + d_ssm
    n_groups = d_ssm // SSM_GROUP
    n_heads = d_attn // HEAD_DIM
    tm = min(256, t)

    x2 = x[0]
    target = loss_target[0]

    def by_rows(n, a):
        return a[0].T if n in _BY_COLUMNS else a[0]

    def start_gather(name, ns, token):
        behind = 0 if token is None else token[0, 0].astype(BF16)
        shards = [by_rows(n, weights[n]).astype(BF16) + behind for n in ns]
        return _exchange_start(name, shards, False, (OWN, SIBLING) + CHIP_PEERS)

    def forward_gather(handle, after):
        return _forward_start(handle["name"] + "_forward", _exchange_wait(handle, after))

    def finish_gather(handle, after):
        return _split_wait(forward_gather(handle, after)[0], [])

    ag_in, token = start_gather("gather_w_in", ["w_in"], None)
    ag_mix, token = start_gather("gather_w_glu_o", ["w_glu", "w_o"], token)
    ag_ffn_in, token = start_gather("gather_w_gate_up", ["w_gate", "w_up"], token)
    ag_down, token = start_gather("gather_w_down", ["w_down"], token)

    xn, = _rows("norm_in", lambda xv, g: ([_rms(xv)[0] * g], []), [x2], [g_pre_mix], [(d, BF16)], [], tm,
                after=[token])
    win_g, = finish_gather(ag_in, [xn])
    w_in_t = win_g.reshape(d_in, d)
    proj = _mm_nt("proj_in", xn, w_in_t, F32)

    cos, sin = _rope_tables(positions.reshape(t, 1).astype(F32))
    sinks_row = jnp.pad(sinks, ((0, 0), (0, LANES - n_heads)))
    attn = _attention_fwd(proj, cos, sin, sinks_row, d_attn)

    def view(n, a):
        if n in ("b_re", "b_im"):
            return jnp.transpose(a[0], (0, 2, 1)).reshape(-1, SSM_STATE)
        if n in ("c_re", "c_im"):
            return a[0].reshape(-1, SSM_STATE)
        return a[0].T if n == "d_skip" else a[0] if a.ndim == 3 else a

    def unview(n, val):
        if n in ("b_re", "b_im"):
            return jnp.transpose(val.reshape(n_groups, SSM_GROUP, SSM_STATE), (0, 2, 1))[None]
        if n in ("c_re", "c_im"):
            return val.reshape(1, n_groups, SSM_GROUP, SSM_STATE)
        return val.T[None] if n == "d_skip" else val[None] if weights[n].ndim == 3 else val

    b_re_v, b_im_v = view("b_re", b_re), view("b_im", b_im)
    ldt_col = log_dt.reshape(n_groups, 1)
    lam_re, lam_im, bbar_re, bbar_im = _s5_discretise(a_re[0], a_im[0], ldt_col, b_re_v, b_im_v)
    n_blocks = n_groups // GROUPS_PER_BLOCK
    mats = [_block_diag_in(bbar_re).astype(BF16), _block_diag_in(bbar_im).astype(BF16),
            lam_re.reshape(n_blocks, 1, SSM_ST_BLOCK), lam_im.reshape(n_blocks, 1, SSM_ST_BLOCK),
            _block_diag_out(view("c_re", c_re)).astype(BF16), _block_diag_out(view("c_im", c_im)).astype(BF16)]
    dskip_row = d_skip.reshape(1, d_ssm)
    forward_mix, _ = forward_gather(ag_mix, [attn])
    y_ssm = _s5_fwd(proj, mats, dskip_row, d_attn, d_ssm)
    gelu_bf16 = lambda yv: _gelu(yv).astype(BF16)
    wglu_g, wo_g = _split_wait(forward_mix, [y_ssm])
    w_glu_full = wglu_g.reshape(d_ssm, d_ssm)
    w_o_full = wo_g.reshape(d, d)
    glu_lin = _mm_nn("glu_gate", y_ssm, w_glu_full, F32, a_fn=gelu_bf16)

    def mix_prep(av, yv, gl, bg, ga, gs):
        ssm = _gelu(yv) * _sigmoid(gl + bg)
        return [jnp.concatenate([_rms(av)[0] * ga, _rms(ssm)[0] * gs], axis=1)], []

    mixed, = _rows("mix_prep", mix_prep, [attn, y_ssm, glu_lin], [b_glu, g_attn_out, g_ssm_out], [(d, BF16)], [], tm)
    mix = _mm_nn("mix_out", mixed, w_o_full, F32)

    def post_mix(xv, mv, gpm, gpf):
        h = xv + _rms(mv)[0] * gpm
        return [h, _rms(h)[0] * gpf], []

    forward_ffn_in, token = forward_gather(ag_ffn_in, [mix])
    h, hn = _rows("post_mix", post_mix, [x2, mix], [g_post_mix, g_pre_ffn], [(d, F32), (d, BF16)], [], tm,
                  after=[token])
    wgate_g, wup_g = _split_wait(forward_ffn_in, [hn])
    gate, up, hid = _ffn_in(hn, wgate_g, wup_g)
    wdown_g, = finish_gather(ag_down, [hid])
    ff = _mm_contract_slots("ffn_down", [(hid, wdown_g)], F32, per_step=2, tm=1024)

    def head(hv, fv, tv, gpo):
        out = hv + _rms(fv)[0] * gpo
        err = out - tv
        dout = err * (1.0 / d)
        dff, dg = _rms_bwd(fv, gpo, dout)
        loss = jnp.zeros((1, LANES), F32) + 0.5 * jnp.sum(err * err) * (1.0 / d)
        return [dff, dout], [dg, loss]

    dff, dh_out, dg_post_ffn, loss_row = _rows("loss_head", head, [h, ff, target], [g_post_ffn],
                                               [(d, BF16), (d, F32)], [d, LANES], tm)

    def swap_halves(name, grads):
        return _halves_start("swap_" + name, [g.reshape(N_DEV // 2, 2, *g.shape[1:]) for g in grads])

    def scatter_chip_sums(name, swap, after):
        both = _split_wait(swap, after)
        half = len(both) // 2
        sums = [_chip_sum("chip_sum_%s_%d" % (name, i), both[i], both[half + i]) for i in range(half)]
        return _exchange_start("scatter_" + name, sums, True, (OWN,) + CHIP_PEERS, by_chip=True)

    dw_down = _mm_slots_tn("ffn_down_dw", hid, dff, BF16)
    swap_down, token = swap_halves("dw_down", [dw_down])
    dgate, dup = _ffn_down_bwd(dff, wdown_g, gate, up, [token])
    rs_down, token = scatter_chip_sums("dw_down", swap_down, [dgate])
    dhn = _mm_contract_slots("ffn_in_dx", [(dgate, wgate_g), (dup, wup_g)], F32, per_step=2, tm=1024, tn=1024,
                             after=[token])
    dw_gate = _mm_slots_tn("ffn_gate_dw", dgate, hn, BF16)
    dw_up = _mm_slots_tn("ffn_up_dw", dup, hn, BF16)
    swap_ffn_in, tok_ffn_in = swap_halves("dw_gate_up", [dw_gate, dw_up])

    def mid_bwd(dho, dhn_, hv, mv, gpf, gpm):
        d1, dgpf = _rms_bwd(hv, gpf, dhn_)
        dh_ = dho + d1
        dmix_, dgpm = _rms_bwd(mv, gpm, dh_)
        return [dh_, dmix_], [dgpf, dgpm]

    dh, dmix, dg_pre_ffn, dg_post_mix = _rows("mid_bwd", mid_bwd, [dh_out, dhn, h, mix], [g_pre_ffn, g_post_mix],
                                              [(d, F32), (d, BF16)], [d, d], tm, after=[tok_ffn_in])

    dmixed = _mm_nt("mix_out_dx", dmix, w_o_full, F32)
    rs_ffn_in, token = scatter_chip_sums("dw_gate_up", swap_ffn_in, [dmixed])
    dw_o = _mm_tn("mix_out_dw", mixed, dmix, BF16, after=[token])
    swap_o, tok_o = swap_halves("dw_o", [dw_o.reshape(N_DEV, d // N_DEV, d)])

    def mix_bwd(dm, av, yv, gl, bg, ga, gs):
        dattn_, dga = _rms_bwd(av, ga, dm[:, :d_attn])
        z = _gelu(yv)
        sg = _sigmoid(gl + bg)
        dssm, dgs = _rms_bwd(z * sg, gs, dm[:, d_attn:])
        dgl = dssm * z * sg * (1.0 - sg)
        return [dattn_, dssm * sg, dgl], [dga, dgs, jnp.sum(dgl, axis=0, keepdims=True)]

    dattn, dz_direct, dglu, dg_attn_out, dg_ssm_out, db_glu = _rows(
        "mix_bwd", mix_bwd, [dmixed, attn, y_ssm, glu_lin], [b_glu, g_attn_out, g_ssm_out],
        [(d_attn, F32), (d_ssm, F32), (d_ssm, BF16)], [d_attn, d_ssm, d_ssm], tm, after=[tok_o])
    dz_glu = _mm_nt("glu_gate_dx", dglu, w_glu_full, F32)
    dw_glu = _mm_tn("glu_gate_dw", y_ssm, dglu, BF16, a_fn=gelu_bf16)
    rs_o, token = scatter_chip_sums("dw_o", swap_o, [dz_glu, dw_glu])

    du, db_re_dense, db_im_dense, dlam_re, dlam_im, dc_re_dense, dc_im_dense, dd_skip = _s5_bwd(
        proj, mats, dskip_row, y_ssm, dz_direct, dz_glu, d_attn, d_ssm, [token])
    da_re, da_im, dlog_dt, db_re_v, db_im_v = _s5_discretise_bwd(
        a_re[0], a_im[0], ldt_col, b_re_v, b_im_v, dlam_re.reshape(n_groups, SSM_STATE),
        dlam_im.reshape(n_groups, SSM_STATE), _block_diag_in_t(db_re_dense), _block_diag_in_t(db_im_dense))
    dq, dk2, dv2, dsinks_row = _attention_bwd(proj, cos, sin, sinks_row, dattn, d_attn)

    small_grads = {
        "sinks": dsinks_row, "a_re": da_re, "a_im": da_im, "log_dt": dlog_dt.reshape(1, n_groups),
        "b_re": db_re_v, "b_im": db_im_v, "c_re": _block_diag_out_t(dc_re_dense),
        "c_im": _block_diag_out_t(dc_im_dense), "d_skip": dd_skip.reshape(n_groups, SSM_GROUP).T, "b_glu": db_glu,
        "g_attn_out": dg_attn_out, "g_ssm_out": dg_ssm_out, "g_post_mix": dg_post_mix, "g_pre_ffn": dg_pre_ffn,
        "g_post_ffn": dg_post_ffn,
    }
    vec_pack, vec_slots, mat_pack, mat_rows = _pack_grads([small_grads[n] for n in _SMALL_VECTORS] + [loss_row],
                                                          [small_grads[n] for n in _SMALL_MATRICES])
    ag_small, token = _exchange_start("gather_small_grads", [vec_pack, mat_pack, small_grads["d_skip"]], False,
                                      (OWN,) + ALL_PEERS)
    dproj = _assemble_dproj(dq, dk2, dv2, du, d_in, [token])

    dw_in = _mm_tn("proj_in_dw", dproj, xn, BF16).reshape(N_DEV, d_in // N_DEV, d)
    swap_in, token = swap_halves("dw_in_glu", [dw_in, dw_glu.reshape(N_DEV, d_ssm // N_DEV, d_ssm)])
    dxn = _mm_nn("proj_in_dx", dproj, w_in_t, F32, after=[token])
    rs_in, token = scatter_chip_sums("dw_in_glu", swap_in, [dxn])

    def x_bwd(dh_, dxn_, xv, g):
        dx, dg = _rms_bwd(xv, g, dxn_)
        return [dh_ + dx], [dg]

    grad_x, dg_pre_mix = _rows("norm_in_bwd", x_bwd, [dh, dxn, x2], [g_pre_mix], [(d, F32)], [d], tm, after=[token])
    ag_last, token = _exchange_start("gather_g_pre_mix_grad", [dg_pre_mix], False, (OWN,) + ALL_PEERS)

    results = {}

    def adam_big(n, parts):
        r = parts.shape[1]
        tr = next((c for c in range(192, 15, -16) if r % c == 0), r)
        results[n] = _adam_sharded("adam_" + n, parts, by_rows(n, weights[n]), by_rows(n, mom_m[n]),
                                   by_rows(n, mom_v[n]), tr)
        return results[n][3]

    done = [grad_x, token]
    adam_big("w_down", _exchange_wait(rs_down, done)[0])
    p_gate, p_up = _exchange_wait(rs_ffn_in, done)
    done = [adam_big("w_gate", p_gate), adam_big("w_up", p_up), results["w_down"][3]]
    done = [adam_big("w_o", _exchange_wait(rs_o, done)[0])]
    vec_parts, mat_parts, dskip_parts = _exchange_wait(ag_small, done)
    first_gain_parts, = _exchange_wait(ag_last, done)
    for n, row0 in zip(_SMALL_MATRICES, mat_rows):
        rows = view(n, weights[n]).shape[0]
        results[n] = _adam_sharded("adam_" + n, mat_parts, view(n, weights[n]), view(n, mom_m[n]), view(n, mom_v[n]),
                                   rows, row0)
    rest = _SMALL_VECTORS + ("d_skip", "g_pre_mix")
    found_at = [(0, row, lane) for row, lane, _ in vec_slots[:-1]] + [(1, 0, 0), (2, 0, 0)]
    updated, loss_sum = _adam_replicated([vec_parts, dskip_parts, first_gain_parts], found_at,
                                         [view(n, weights[n]) for n in rest], [view(n, mom_m[n]) for n in rest],
                                         [view(n, mom_v[n]) for n in rest], (0,) + vec_slots[-1])
    results.update(zip(rest, updated))
    p_in, p_glu = _exchange_wait(rs_in, [results[n][3] for n in _SMALL_MATRICES] + [updated[0][3]])
    adam_big("w_in", p_in)
    adam_big("w_glu", p_glu)

    outs = [loss_sum[0, 0], grad_x[None]]
    for k in range(4):
        for n in _ORDER:
            val = results[n][k]
            outs.append(val.T[None] if n in _BY_COLUMNS else val[None] if n in _BIG else unview(n, val))
    return tuple(outs)
```

```python
import math

import jax
import jax.numpy as jnp
from jax import lax
from jax.experimental import pallas as pl
from jax.experimental.pallas import tpu as pltpu

F32 = jnp.float32
BF16 = jnp.bfloat16

HEAD_DIM = 64
N_KV_HEADS = 4
D_KV = N_KV_HEADS * HEAD_DIM
WINDOW = 128
BLOCK = 128
ROPE_THETA = 10000.0
SSM_GROUP = 16
SSM_STATE = 64
GROUPS_PER_BLOCK = 8
SSM_CH_BLOCK = GROUPS_PER_BLOCK * SSM_GROUP
SSM_ST_BLOCK = GROUPS_PER_BLOCK * SSM_STATE
RMS_EPS = 1e-6
N_DEV = 8
LANES = 128
SUBLANES = 8
MASKED = -1e30

ADAM_LR = 0.001
ADAM_B1 = 0.9
ADAM_B2 = 0.999
ADAM_EPS = 1e-08
ADAM_WD = 0.01
ADAM_STEP = 10

VMEM_LIMIT_BYTES = 56 * 1024 * 1024


def _call(body, *, name, out_shape, in_specs, out_specs, grid=(), scratch_shapes=(), semantics=None, n_after=0):
    params = dict(vmem_limit_bytes=VMEM_LIMIT_BYTES)
    if semantics is not None:
        params["dimension_semantics"] = semantics
    n_in = len(in_specs)
    if n_after:
        inner = body

        def body(*refs):
            inner(*refs[:n_in], *refs[n_in + n_after:])

        in_specs = list(in_specs) + [pl.BlockSpec(memory_space=pl.ANY)] * n_after
    return pl.pallas_call(body, name=name, grid=grid, in_specs=in_specs, out_specs=out_specs, out_shape=out_shape,
                          scratch_shapes=scratch_shapes, compiler_params=pltpu.CompilerParams(**params))


def _sds(shape, dtype):
    return jax.ShapeDtypeStruct(tuple(shape), dtype)


def _dot(a, b, ca, cb):
    return lax.dot_general(a, b, (((ca,), (cb,)), ((), ())), preferred_element_type=F32)


def _rms(x):
    r = lax.rsqrt(jnp.mean(x * x, axis=-1, keepdims=True) + RMS_EPS)
    return x * r, r


def _rms_bwd(x, g, dy):
    xh, r = _rms(x)
    dxh = dy * g
    dx = r * (dxh - xh * jnp.mean(dxh * xh, axis=-1, keepdims=True))
    return dx, jnp.sum(dy * xh, axis=0, keepdims=True)


def _sigmoid(x):
    return 1.0 / (1.0 + jnp.exp(-x))


_GELU_C = math.sqrt(2.0 / math.pi)
_GELU_A = 0.044715


def _gelu(y):
    t = jnp.tanh(_GELU_C * (y + _GELU_A * y * y * y))
    return 0.5 * y * (1.0 + t)


def _gelu_grad(y):
    t = jnp.tanh(_GELU_C * (y + _GELU_A * y * y * y))
    return 0.5 * (1.0 + t) + 0.5 * y * (1.0 - t * t) * _GELU_C * (1.0 + 3.0 * _GELU_A * y * y)


def _rows(name, fn, row_ins, vec_ins, row_outs, acc_widths, tm, after=()):
    rows = row_ins[0].shape[0]
    assert rows % tm == 0, (name, rows, tm)
    n_row, n_vec, n_out, n_acc = len(row_ins), len(vec_ins), len(row_outs), len(acc_widths)

    def body(*refs):
        ins = [r[...] for r in refs[:n_row + n_vec]]
        outs = refs[n_row + n_vec:n_row + n_vec + n_out]
        accs = refs[n_row + n_vec + n_out:]
        row_vals, acc_vals = fn(*ins)
        for o, v in zip(outs, row_vals):
            o[...] = v.astype(o.dtype)
        if n_acc:
            @pl.when(pl.program_id(0) == 0)
            def _():
                for a in accs:
                    a[...] = jnp.zeros_like(a)
            for a, v in zip(accs, acc_vals):
                a[...] += v

    in_specs = [pl.BlockSpec((tm, a.shape[1]), lambda i: (i, 0)) for a in row_ins]
    in_specs += [pl.BlockSpec(v.shape, lambda i: (0, 0)) for v in vec_ins]
    out_specs = [pl.BlockSpec((tm, w), lambda i: (i, 0)) for w, _ in row_outs]
    out_specs += [pl.BlockSpec((1, w), lambda i: (0, 0)) for w in acc_widths]
    out_shape = [_sds((rows, w), dt) for w, dt in row_outs] + [_sds((1, w), F32) for w in acc_widths]
    return _call(body, name=name, grid=(rows // tm,), in_specs=in_specs, out_specs=out_specs, out_shape=out_shape,
                 semantics=("arbitrary",) if n_acc else ("parallel",), n_after=len(after))(*row_ins, *vec_ins, *after)


def _matmul(name, operands, in_specs, product, grid, out_shape, out_spec, acc_shape, after=()):
    nk = grid[-1]
    n_in = len(operands)
    in_place = out_shape.dtype == F32

    def body(*refs):
        ins = [r[...] for r in refs[:n_in]]
        o_ref = refs[n_in]
        if nk == 1:
            o_ref[...] = product(*ins).astype(o_ref.dtype)
            return
        acc = o_ref if in_place else refs[n_in + 1]
        k = pl.program_id(len(grid) - 1)

        @pl.when(k == 0)
        def _():
            acc[...] = jnp.zeros_like(acc)

        acc[...] += product(*ins)

        if not in_place:
            @pl.when(k == nk - 1)
            def _():
                o_ref[...] = acc[...].astype(o_ref.dtype)

    return _call(body, name=name, grid=grid, in_specs=in_specs, out_specs=out_spec, out_shape=out_shape,
                 scratch_shapes=[] if nk == 1 or in_place else [pltpu.VMEM(acc_shape, F32)],
                 semantics=("parallel",) * (len(grid) - 1) + ("arbitrary",), n_after=len(after))(*operands, *after)


def _mm_nn(name, a, b, out_dtype, tm=512, tn=None, a_fn=lambda x: x, after=()):
    m, k = a.shape
    n = b.shape[1]
    tm, tn = min(tm, m), n if tn is None else tn
    return _matmul(name, [a, b],
                   [pl.BlockSpec((tm, k), lambda i, j, s: (i, 0)), pl.BlockSpec((k, tn), lambda i, j, s: (0, j))],
                   lambda x, y: _dot(a_fn(x), y, 1, 0), (m // tm, n // tn, 1), _sds((m, n), out_dtype),
                   pl.BlockSpec((tm, tn), lambda i, j, s: (i, j)), (tm, tn), after)


def _mm_nt(name, a, b, out_dtype, tm=512, tn=None):
    m, k = a.shape
    n = b.shape[0]
    tm, tn = min(tm, m), n if tn is None else tn
    return _matmul(name, [a, b],
                   [pl.BlockSpec((tm, k), lambda i, j, s: (i, 0)), pl.BlockSpec((tn, k), lambda i, j, s: (j, 0))],
                   lambda x, y: _dot(x, y, 1, 1), (m // tm, n // tn, 1), _sds((m, n), out_dtype),
                   pl.BlockSpec((tm, tn), lambda i, j, s: (i, j)), (tm, tn))


def _mm_tn(name, a, b, out_dtype, tm=512, tn=None, tk=2048, a_fn=lambda x: x, after=()):
    k, m = a.shape
    n = b.shape[1]
    tm, tk, tn = min(tm, m), min(tk, k), n if tn is None else tn
    return _matmul(name, [a, b],
                   [pl.BlockSpec((tk, tm), lambda i, j, s: (s, i)), pl.BlockSpec((tk, tn), lambda i, j, s: (s, j))],
                   lambda x, y: _dot(a_fn(x), y, 0, 0), (m // tm, n // tn, k // tk), _sds((m, n), out_dtype),
                   pl.BlockSpec((tm, tn), lambda i, j, s: (i, j)), (tm, tn), after)


def _mm_contract_slots(name, pairs, out_dtype, per_step, tm=512, tn=2048, after=()):
    s_, m, k = pairs[0][0].shape
    n = pairs[0][1].shape[2]
    tm, tn = min(tm, m), min(tn, n)
    ops, specs = [], []
    for a, b in pairs:
        ops += [a, b]
        specs += [pl.BlockSpec((per_step, tm, k), lambda i, j, s: (s, i, 0)),
                  pl.BlockSpec((per_step, k, tn), lambda i, j, s: (s, 0, j))]

    def product(*t):
        return sum(_dot(t[2 * p][q], t[2 * p + 1][q], 1, 0) for p in range(len(pairs)) for q in range(per_step))

    return _matmul(name, ops, specs, product, (m // tm, n // tn, s_ // per_step), _sds((m, n), out_dtype),
                   pl.BlockSpec((tm, tn), lambda i, j, s: (i, j)), (tm, tn), after)


def _mm_slots_tn(name, a, b, out_dtype, tn=2048, tk=2048):
    s_, k, m = a.shape
    n = b.shape[1]
    tn, tk = min(tn, n), min(tk, k)
    return _matmul(name, [a, b],
                   [pl.BlockSpec((None, tk, m), lambda s, j, z: (s, z, 0)), pl.BlockSpec((tk, tn), lambda s, j, z: (z, j))],
                   lambda x, y: _dot(x, y, 0, 0), (s_, n // tn, k // tk), _sds((s_, m, n), out_dtype),
                   pl.BlockSpec((None, m, tn), lambda s, j, z: (s, 0, j)), (m, tn))


def _ffn_in(a, w_gate, w_up, tm=1024):
    m, k = a.shape
    s_, n, _ = w_gate.shape
    tm = min(tm, m)

    def body(a_ref, wg_ref, wu_ref, g_ref, u_ref, h_ref):
        x = a_ref[...]
        g = _dot(x, wg_ref[...], 1, 1)
        u = _dot(x, wu_ref[...], 1, 1)
        g_ref[...] = g.astype(BF16)
        u_ref[...] = u.astype(BF16)
        h_ref[...] = (g * _sigmoid(g) * u).astype(BF16)

    w_spec = pl.BlockSpec((None, n, k), lambda s, i: (s, 0, 0))
    o_spec = pl.BlockSpec((None, tm, n), lambda s, i: (s, i, 0))
    return _call(body, name="ffn_in", grid=(s_, m // tm),
                 in_specs=[pl.BlockSpec((tm, k), lambda s, i: (i, 0)), w_spec, w_spec], out_specs=[o_spec] * 3,
                 out_shape=[_sds((s_, m, n), BF16)] * 3, semantics=("parallel", "parallel"))(a, w_gate, w_up)


def _ffn_down_bwd(d_out, w_down, gate, up, after, tm=1024):
    m, k = d_out.shape
    s_, n, _ = w_down.shape
    tm = min(tm, m)

    def body(d_ref, w_ref, g_ref, u_ref, dg_ref, du_ref):
        rows = pl.ds(pl.multiple_of(pl.program_id(1) * tm, tm), tm)
        dh = _dot(d_ref[rows, :], w_ref[...], 1, 1)
        g = g_ref[...].astype(F32)
        sg = _sigmoid(g)
        dg_ref[...] = (dh * u_ref[...].astype(F32) * sg * (1.0 + g * (1.0 - sg))).astype(BF16)
        du_ref[...] = (dh * g * sg).astype(BF16)

    t_spec = pl.BlockSpec((None, tm, n), lambda s, i: (s, i, 0))
    return _call(body, name="ffn_down_dx", grid=(s_, m // tm),
                 in_specs=[pl.BlockSpec((m, k), lambda s, i: (0, 0)), pl.BlockSpec((None, n, k), lambda s, i: (s, 0, 0)),
                           t_spec, t_spec],
                 out_specs=[t_spec] * 2, out_shape=[_sds((s_, m, n), BF16)] * 2, semantics=("parallel", "parallel"),
                 n_after=len(after))(d_out, w_down, gate, up, *after)


ALL_PEERS = (1, 2, 3, 4, 5, 6, 7)
CHIP_PEERS = (2, 4, 6)
SIBLING = 1
OWN = 0


def _peer(relation):
    x, y, c = lax.axis_index("x"), lax.axis_index("y"), lax.axis_index("c")
    pos = (1 - x if relation & 4 else x, 1 - y if relation & 2 else y, 1 - c if relation & 1 else c)
    return pos, 4 * pos[0] + 2 * pos[1] + pos[2]


def _slot(relation, by_chip):
    pos, device = _peer(relation)
    return 2 * pos[0] + pos[1] if by_chip else device


def _exchange_copies(ins, lands, send_sems, recv_sems, scatter, relations, by_chip=False):
    me = _slot(0, by_chip)

    def copy(a, s, peer, pos, dst_slot):
        return pltpu.make_async_remote_copy(
            src_ref=ins[a].at[peer] if scatter else ins[a], dst_ref=lands[a].at[dst_slot],
            send_sem=send_sems.at[s], recv_sem=recv_sems.at[s], device_id=pos, device_id_type=pl.DeviceIdType.MESH)

    pairs = []
    for k, r in enumerate(relations):
        pos, peer = _peer(r)[0], _slot(r, by_chip)
        for a in range(len(ins)):
            s = a * len(relations) + k
            pairs.append((copy(a, s, peer, pos, me), copy(a, s, peer, pos, peer)))
    return pairs


def _halves_copies(arrays, lands, send_sems, recv_sems):
    sibling, _ = _peer(SIBLING)
    core = lax.axis_index("c")
    pairs = []
    for a, (ref, land) in enumerate(zip(arrays, lands)):
        send = pltpu.make_async_remote_copy(
            src_ref=ref.at[:, pl.ds(1 - core, 1)], dst_ref=land, send_sem=send_sems.at[a], recv_sem=recv_sems.at[a],
            device_id=sibling, device_id_type=pl.DeviceIdType.MESH)
        pairs.append((send, send))
    return pairs


def _forward_copies(lands, send_sems, recv_sems):
    sibling, _ = _peer(SIBLING)

    def copy(a, s, slot):
        return pltpu.make_async_remote_copy(
            src_ref=lands[a].at[slot], dst_ref=lands[a].at[slot], send_sem=send_sems.at[s], recv_sem=recv_sems.at[s],
            device_id=sibling, device_id_type=pl.DeviceIdType.MESH)

    pairs = []
    for k, r in enumerate(CHIP_PEERS):
        _, mine = _peer(r)
        _, theirs = _peer(r | SIBLING)
        for a in range(len(lands)):
            s = a * len(CHIP_PEERS) + k
            pairs.append((copy(a, s, mine), copy(a, s, theirs)))
    return pairs


_HBM_SPEC = pl.BlockSpec(memory_space=pltpu.HBM)
_SEM_SPEC = pl.BlockSpec(memory_space=pltpu.SEMAPHORE)
_SIDE_EFFECT = pltpu.SideEffectType.DATAFLOW_SIDE_EFFECTING


def _split_start(name, operands, n_sem, make_pairs):
    k = len(operands)

    def body(*refs):
        send_sems, recv_sems, token = refs[k], refs[k + 1], refs[-1]
        for send, _ in make_pairs(refs[:k], send_sems, recv_sems):
            send.start()
        token[...] = jnp.zeros_like(token)

    out = pl.pallas_call(
        body, name=name,
        out_shape=(pltpu.SemaphoreType.DMA((n_sem,)), pltpu.SemaphoreType.DMA((n_sem,)),
                   *[pltpu.HBM(a.shape, a.dtype) for a in operands], _sds((SUBLANES, LANES), F32)),
        in_specs=[_HBM_SPEC] * k,
        out_specs=(_SEM_SPEC, _SEM_SPEC, *[_HBM_SPEC] * k, pl.BlockSpec(memory_space=pltpu.VMEM)),
        input_output_aliases={i: 2 + i for i in range(k)},
        compiler_params=pltpu.CompilerParams(has_side_effects=_SIDE_EFFECT),
    )(*[pltpu.with_memory_space_constraint(a, pltpu.HBM) for a in operands])
    return dict(name=name, sems=out[:2], thru=list(out[2:2 + k]), make_pairs=make_pairs), out[-1]


def _split_wait(handle, after):
    thru, make_pairs = handle["thru"], handle["make_pairs"]
    k = len(thru)

    def body(*refs):
        for send, arrival in make_pairs(refs[:k], refs[k], refs[k + 1]):
            send.wait_send()
            arrival.wait_recv()

    return pl.pallas_call(
        body, name=handle["name"] + "_wait", out_shape=[pltpu.HBM(a.shape, a.dtype) for a in thru],
        in_specs=[_HBM_SPEC] * k + [_SEM_SPEC, _SEM_SPEC] + [pl.BlockSpec(memory_space=pl.ANY)] * len(after),
        out_specs=[_HBM_SPEC] * k, input_output_aliases={i: i for i in range(k)},
        compiler_params=pltpu.CompilerParams(has_side_effects=_SIDE_EFFECT),
    )(*thru, *handle["sems"], *after)


def _exchange_start(name, arrays, scatter, relations, by_chip=False):
    n = len(arrays)
    lands = [lax.empty(a.shape if scatter else (N_DEV,) + a.shape, a.dtype) for a in arrays]

    def make_pairs(refs, send_sems, recv_sems):
        return _exchange_copies(refs[:n], refs[n:], send_sems, recv_sems, scatter, relations, by_chip)

    handle, token = _split_start(name, list(arrays) + lands, n * len(relations), make_pairs)
    handle.update(n=n)
    return handle, token


def _halves_start(name, arrays):
    lands = [lax.empty((a.shape[0], 1) + a.shape[2:], a.dtype) for a in arrays]
    n = len(arrays)

    def make_pairs(refs, send_sems, recv_sems):
        return _halves_copies(refs[:n], refs[n:], send_sems, recv_sems)

    return _split_start(name, list(arrays) + lands, n, make_pairs)


def _chip_sum(name, array, landed):
    chips, _, r, c = array.shape
    tr = r // 2 if r > 512 and r % 32 == 0 else r

    def body(a_ref, b_ref, o_ref):
        mine = a_ref[lax.axis_index("c")].astype(F32)
        o_ref[...] = (mine + b_ref[...].astype(F32)).astype(o_ref.dtype)

    return _call(body, name=name, grid=(chips, r // tr),
                 in_specs=[pl.BlockSpec((None, 2, tr, c), lambda k, i: (k, 0, i, 0)),
                           pl.BlockSpec((None, None, tr, c), lambda k, i: (k, 0, i, 0))],
                 out_specs=pl.BlockSpec((None, tr, c), lambda k, i: (k, i, 0)),
                 out_shape=_sds((chips, r, c), BF16), semantics=("parallel", "parallel"))(array, landed)


def _forward_start(name, lands):
    return _split_start(name, list(lands), len(lands) * len(CHIP_PEERS), _forward_copies)


def _exchange_wait(handle, after):
    return _split_wait(handle, after)[handle["n"]:]


def _rope_tables(pos_col):
    t = pos_col.shape[0]
    half = HEAD_DIM // 2
    inv_freq = ROPE_THETA ** (-jnp.arange(half, dtype=F32) / half)
    inv_row = jnp.tile(inv_freq, LANES // half)[None, :]

    def body(pos_ref, inv_ref, cos_ref, sin_ref):
        ang = pos_ref[...] * inv_ref[...]
        cos_ref[...] = jnp.cos(ang)
        sin_ref[...] = jnp.sin(ang)

    tm = min(t, 512)
    return _call(body, name="rope_tables", grid=(t // tm,),
                 in_specs=[pl.BlockSpec((tm, 1), lambda i: (i, 0)), pl.BlockSpec((1, LANES), lambda i: (0, 0))],
                 out_specs=[pl.BlockSpec((tm, LANES), lambda i: (i, 0))] * 2,
                 out_shape=[_sds((t, LANES), F32)] * 2, semantics=("parallel",))(pos_col, inv_row)


def _rot_half(x):
    lane = lax.broadcasted_iota(jnp.int32, x.shape, 1)
    low = (lane % HEAD_DIM) < HEAD_DIM // 2
    return jnp.where(low, -pltpu.roll(x, LANES - HEAD_DIM // 2, 1), pltpu.roll(x, HEAD_DIM // 2, 1))


def _rope(x, cos, sin):
    return x * cos + _rot_half(x) * sin


def _unrope(d, cos, sin):
    return d * cos - _rot_half(d) * sin


def _band_mask(first_block, heads):
    r = lax.broadcasted_iota(jnp.int32, (heads * BLOCK, 2 * BLOCK), 0) % BLOCK
    c = lax.broadcasted_iota(jnp.int32, (heads * BLOCK, 2 * BLOCK), 1)
    diff = r - c + BLOCK
    return (diff >= 0) & (diff < WINDOW) & ((c >= BLOCK) | jnp.logical_not(first_block))


def _attn_specs(t, d_attn, d_in):
    kb, vb = d_attn // D_KV, d_attn // D_KV + 1
    prev = lambda i: jnp.maximum(i - 1, 0)
    return [
        pl.BlockSpec((BLOCK, d_attn), lambda i: (i, 0)),
        pl.BlockSpec((BLOCK, D_KV), lambda i: (i, kb)),
        pl.BlockSpec((BLOCK, D_KV), lambda i: (i, vb)),
        pl.BlockSpec((BLOCK, D_KV), lambda i: (prev(i), kb)),
        pl.BlockSpec((BLOCK, D_KV), lambda i: (prev(i), vb)),
        pl.BlockSpec((BLOCK, LANES), lambda i: (i, 0)),
        pl.BlockSpec((BLOCK, LANES), lambda i: (i, 0)),
        pl.BlockSpec((BLOCK, LANES), lambda i: (prev(i), 0)),
        pl.BlockSpec((BLOCK, LANES), lambda i: (prev(i), 0)),
        pl.BlockSpec((1, LANES), lambda i: (0, 0)),
    ]


def _head(x, h):
    return x[:, h * HEAD_DIM:(h + 1) * HEAD_DIM]


def _attn_heads(q_ref, kc_ref, vc_ref, kp_ref, vp_ref, cq_ref, sq_ref, cp_ref, sp_ref, d_attn):
    cq, sq, cp, sp = cq_ref[...], sq_ref[...], cp_ref[...], sp_ref[...]
    q_rot = [_rope(q_ref[:, j * LANES:(j + 1) * LANES], cq, sq) for j in range(d_attn // LANES)]
    kc_rot = [_rope(kc_ref[:, j * LANES:(j + 1) * LANES], cq, sq) for j in range(D_KV // LANES)]
    kp_rot = [_rope(kp_ref[:, j * LANES:(j + 1) * LANES], cp, sp) for j in range(D_KV // LANES)]
    per = LANES // HEAD_DIM
    q_heads = [_head(q_rot[h // per], h % per).astype(BF16) for h in range(d_attn // HEAD_DIM)]
    kk = [jnp.concatenate([_head(kp_rot[g // per], g % per), _head(kc_rot[g // per], g % per)], axis=0).astype(BF16)
          for g in range(N_KV_HEADS)]
    vv = [jnp.concatenate([_head(vp_ref[...], g), _head(vc_ref[...], g)], axis=0).astype(BF16) for g in range(N_KV_HEADS)]
    return q_heads, kk, vv


def _stack_group(q_heads, sink_ref, group):
    q_all = jnp.concatenate([q_heads[h] for h in group], axis=0)
    sink_all = jnp.concatenate([jnp.broadcast_to(sink_ref[:, h:h + 1], (BLOCK, 1)) for h in group], axis=0)
    return q_all, sink_all


def _softmax_with_sink(q, kk, sink, mask):
    s = _dot(q, kk, 1, 1) * (1.0 / math.sqrt(HEAD_DIM))
    s = jnp.where(mask, s, MASKED)
    m = jnp.maximum(jnp.max(s, axis=-1, keepdims=True), sink)
    p = jnp.exp(s - m)
    e_sink = jnp.exp(sink - m)
    inv = 1.0 / (jnp.sum(p, axis=-1, keepdims=True) + e_sink)
    return p * inv, e_sink * inv


def _attention_fwd(proj, cos, sin, sinks_row, d_attn):
    t, d_in = proj.shape
    n_heads = d_attn // HEAD_DIM
    q_per_kv = n_heads // N_KV_HEADS

    def body(q_ref, kc_ref, vc_ref, kp_ref, vp_ref, cq_ref, sq_ref, cp_ref, sp_ref, sink_ref, o_ref):
        mask = _band_mask(pl.program_id(0) == 0, q_per_kv)
        q_heads, kk, vv = _attn_heads(q_ref, kc_ref, vc_ref, kp_ref, vp_ref, cq_ref, sq_ref, cp_ref, sp_ref, d_attn)
        for g in range(N_KV_HEADS):
            group = range(g * q_per_kv, (g + 1) * q_per_kv)
            q_all, sink_all = _stack_group(q_heads, sink_ref, group)
            probs, _ = _softmax_with_sink(q_all, kk[g], sink_all, mask)
            o_all = _dot(probs.astype(BF16), vv[g], 1, 0)
            for k, h in enumerate(group):
                o_ref[:, h * HEAD_DIM:(h + 1) * HEAD_DIM] = o_all[k * BLOCK:(k + 1) * BLOCK]

    return _call(body, name="attention_fwd", grid=(t // BLOCK,), in_specs=_attn_specs(t, d_attn, d_in),
                 out_specs=pl.BlockSpec((BLOCK, d_attn), lambda i: (i, 0)), out_shape=_sds((t, d_attn), F32),
                 semantics=("parallel",))(proj, proj, proj, proj, proj, cos, sin, cos, sin, sinks_row)


def _attention_bwd(proj, cos, sin, sinks_row, d_out, d_attn):
    t, d_in = proj.shape
    n_heads = d_attn // HEAD_DIM
    q_per_kv = n_heads // N_KV_HEADS
    nb = t // BLOCK
    per = LANES // HEAD_DIM
    stack = q_per_kv

    def body(q_ref, kc_ref, vc_ref, kp_ref, vp_ref, cq_ref, sq_ref, cp_ref, sp_ref, sink_ref, do_ref,
             dq_ref, dk_ref, dv_ref, dsink_ref):
        i = pl.program_id(0)
        mask = _band_mask(i == 0, stack)
        q_heads, kk, vv = _attn_heads(q_ref, kc_ref, vc_ref, kp_ref, vp_ref, cq_ref, sq_ref, cp_ref, sp_ref, d_attn)
        lane = lax.broadcasted_iota(jnp.int32, (1, LANES), 1)
        dsink = jnp.zeros((1, LANES), F32)
        dq_rot, dkk, dvv = [], [], []
        for g in range(N_KV_HEADS):
            dkk_g = jnp.zeros((2 * BLOCK, HEAD_DIM), F32)
            dvv_g = jnp.zeros((2 * BLOCK, HEAD_DIM), F32)
            for first in range(g * q_per_kv, (g + 1) * q_per_kv, stack):
                group = range(first, first + stack)
                q_all, sink_all = _stack_group(q_heads, sink_ref, group)
                probs, p_sink = _softmax_with_sink(q_all, kk[g], sink_all, mask)
                do_all = jnp.concatenate([do_ref[:, h * HEAD_DIM:(h + 1) * HEAD_DIM] for h in group],
                                         axis=0).astype(BF16)
                dp = _dot(do_all, vv[g], 1, 1)
                delta = jnp.sum(probs * dp, axis=-1, keepdims=True)
                ds = (probs * (dp - delta) * (1.0 / math.sqrt(HEAD_DIM))).astype(BF16)
                dq_all = _dot(ds, kk[g], 1, 0)
                dkk_g += _dot(ds, q_all, 0, 0)
                dvv_g += _dot(probs.astype(BF16), do_all, 0, 0)
                sink_term = p_sink * delta
                for k, h in enumerate(group):
                    dq_rot.append(dq_all[k * BLOCK:(k + 1) * BLOCK])
                    part = jnp.sum(sink_term[k * BLOCK:(k + 1) * BLOCK], axis=0, keepdims=True)
                    dsink += jnp.where(lane == h, -part, 0.0)
            dkk.append(dkk_g)
            dvv.append(dvv_g)
        cq, sq, cp, sp = cq_ref[...], sq_ref[...], cp_ref[...], sp_ref[...]
        for j in range(d_attn // LANES):
            d = jnp.concatenate(dq_rot[j * per:(j + 1) * per], axis=1)
            dq_ref[:, j * LANES:(j + 1) * LANES] = _unrope(d, cq, sq)
        for j in range(D_KV // LANES):
            d = jnp.concatenate(dkk[j * per:(j + 1) * per], axis=1)
            dk_ref[0, :, j * LANES:(j + 1) * LANES] = _unrope(d[:BLOCK], cp, sp)
            dk_ref[1, :, j * LANES:(j + 1) * LANES] = _unrope(d[BLOCK:], cq, sq)
            d = jnp.concatenate(dvv[j * per:(j + 1) * per], axis=1)
            dv_ref[0, :, j * LANES:(j + 1) * LANES] = d[:BLOCK]
            dv_ref[1, :, j * LANES:(j + 1) * LANES] = d[BLOCK:]

        @pl.when(i == 0)
        def _():
            dsink_ref[...] = jnp.zeros_like(dsink_ref)

        dsink_ref[...] += dsink

    pair = pl.BlockSpec((2, BLOCK, D_KV), lambda i: (i, 0, 0))
    return _call(body, name="attention_bwd", grid=(nb,),
                 in_specs=_attn_specs(t, d_attn, d_in) + [pl.BlockSpec((BLOCK, d_attn), lambda i: (i, 0))],
                 out_specs=[pl.BlockSpec((BLOCK, d_attn), lambda i: (i, 0)), pair, pair,
                            pl.BlockSpec((1, LANES), lambda i: (0, 0))],
                 out_shape=[_sds((t, d_attn), F32), _sds((2 * nb, BLOCK, D_KV), F32), _sds((2 * nb, BLOCK, D_KV), F32),
                            _sds((1, LANES), F32)],
                 semantics=("arbitrary",))(proj, proj, proj, proj, proj, cos, sin, cos, sin, sinks_row, d_out)


def _assemble_dproj(dq, dk2, dv2, du, d_in, after):
    t, d_attn = dq.shape
    d_ssm = du.shape[1]
    nb = t // BLOCK

    def body(dq_ref, dk_own, dk_next, dv_own, dv_next, du_ref, o_ref):
        has_next = (pl.program_id(0) < nb - 1).astype(F32)
        o_ref[:, :d_attn] = dq_ref[...].astype(BF16)
        o_ref[:, d_attn:d_attn + D_KV] = (dk_own[...] + has_next * dk_next[...]).astype(BF16)
        o_ref[:, d_attn + D_KV:d_attn + 2 * D_KV] = (dv_own[...] + has_next * dv_next[...]).astype(BF16)
        o_ref[:, d_attn + 2 * D_KV:] = du_ref[...].astype(BF16)

    own = pl.BlockSpec((None, BLOCK, D_KV), lambda i: (2 * i + 1, 0, 0))
    nxt = pl.BlockSpec((None, BLOCK, D_KV), lambda i: (jnp.minimum(2 * i + 2, 2 * nb - 1), 0, 0))
    return _call(body, name="assemble_dproj", grid=(nb,),
                 in_specs=[pl.BlockSpec((BLOCK, d_attn), lambda i: (i, 0)), own, nxt, own, nxt,
                           pl.BlockSpec((BLOCK, d_ssm), lambda i: (i, 0))],
                 out_specs=pl.BlockSpec((BLOCK, d_in), lambda i: (i, 0)), out_shape=_sds((t, d_in), BF16),
                 semantics=("parallel",), n_after=len(after))(dq, dk2, dk2, dv2, dv2, du, *after)


def _discretise(ar, ai, ldt, br, bi):
    dt = jnp.exp(ldt)
    mag = jnp.exp(ar * dt)
    lam_re = mag * jnp.cos(ai * dt)
    lam_im = mag * jnp.sin(ai * dt)
    den = ar * ar + ai * ai
    nr = lam_re - 1.0
    ni = lam_im
    f_re = (nr * ar + ni * ai) / den
    f_im = (ni * ar - nr * ai) / den
    return (lam_re, lam_im, [f_re * r - f_im * i for r, i in zip(br, bi)], [f_re * i + f_im * r for r, i in zip(br, bi)])


def _whole(arrays):
    return [pl.BlockSpec(a.shape, lambda *_, nd=len(a.shape): (0,) * nd) for a in arrays]


def _channels(ref):
    groups = ref.shape[0] // SSM_GROUP
    return [ref[pl.ds(p, groups, stride=SSM_GROUP), :] for p in range(SSM_GROUP)]


def _store_channels(ref, values):
    groups = ref.shape[0] // SSM_GROUP
    for p, val in enumerate(values):
        ref[pl.ds(p, groups, stride=SSM_GROUP), :] = val


def _s5_discretise(ar, ai, ldt, br, bi):
    ins = [ar, ai, ldt, br, bi]

    def body(ar_ref, ai_ref, ldt_ref, br_ref, bi_ref, lr_ref, li_ref, bbr_ref, bbi_ref):
        lr, li, bbr, bbi = _discretise(ar_ref[...], ai_ref[...], ldt_ref[...], _channels(br_ref), _channels(bi_ref))
        lr_ref[...] = lr
        li_ref[...] = li
        _store_channels(bbr_ref, bbr)
        _store_channels(bbi_ref, bbi)

    outs = [_sds(ar.shape, F32), _sds(ar.shape, F32), _sds(br.shape, F32), _sds(br.shape, F32)]
    return _call(body, name="s5_discretise", in_specs=_whole(ins), out_specs=_whole(outs), out_shape=outs)(*ins)


def _s5_discretise_bwd(ar, ai, ldt, br, bi, d_lr, d_li, d_bbr, d_bbi):
    ins = [ar, ai, ldt, br, bi, d_lr, d_li, d_bbr, d_bbi]

    def body(ar_ref, ai_ref, ldt_ref, br_ref, bi_ref, dlr_ref, dli_ref, dbbr_ref, dbbi_ref,
             dar_ref, dai_ref, dldt_ref, dbr_ref, dbi_ref):
        _, vjp = jax.vjp(_discretise, ar_ref[...], ai_ref[...], ldt_ref[...], _channels(br_ref), _channels(bi_ref))
        dar, dai, dldt, dbr, dbi = vjp((dlr_ref[...], dli_ref[...], _channels(dbbr_ref), _channels(dbbi_ref)))
        dar_ref[...] = dar
        dai_ref[...] = dai
        dldt_ref[...] = dldt
        _store_channels(dbr_ref, dbr)
        _store_channels(dbi_ref, dbi)

    outs = [_sds(a.shape, F32) for a in (ar, ai, ldt, br, bi)]
    return _call(body, name="s5_discretise_bwd", in_specs=_whole(ins), out_specs=_whole(outs), out_shape=outs)(*ins)


def _cmul(ar, ai, br, bi):
    return ar * br - ai * bi, ar * bi + ai * br


def _load_segmented(ref, tile0, n_tiles, seg):
    return jnp.concatenate([ref[pl.ds(tile0 + j, SUBLANES, stride=seg), :] for j in range(n_tiles)], axis=0)


def _store_segmented(ref, tile0, seg, value):
    for j in range(value.shape[0] // SUBLANES):
        ref[pl.ds(tile0 + j, SUBLANES, stride=seg), :] = value[j * SUBLANES:(j + 1) * SUBLANES, :]


def _fill_powers(lr, li, pr_ref, pi_ref, seg):
    pows = [(lr, li)]
    for _ in range(SUBLANES - 1):
        pows.append(_cmul(pows[-1][0], pows[-1][1], lr, li))
    row = lax.broadcasted_iota(jnp.int32, (SUBLANES, lr.shape[1]), 0)
    tr = jnp.zeros((SUBLANES, lr.shape[1]), F32)
    ti = jnp.zeros((SUBLANES, lr.shape[1]), F32)
    for r in range(SUBLANES):
        tr = jnp.where(row == r, pows[r][0], tr)
        ti = jnp.where(row == r, pows[r][1], ti)
    pr_ref[0:SUBLANES, :] = tr
    pi_ref[0:SUBLANES, :] = ti
    k = SUBLANES
    while k < seg:
        fr, fi = pr_ref[k - 1:k, :], pi_ref[k - 1:k, :]
        for t0 in range(0, k, SUBLANES):
            nr, ni = _cmul(pr_ref[t0:t0 + SUBLANES, :], pi_ref[t0:t0 + SUBLANES, :], fr, fi)
            pr_ref[k + t0:k + t0 + SUBLANES, :] = nr
            pi_ref[k + t0:k + t0 + SUBLANES, :] = ni
        k *= 2


def _scan_segments(sr_ref, si_ref, pr_ref, pi_ref, lr, li, seg, reverse, per_tile=None):
    w = lr.shape[1]
    sign = -1.0 if reverse else 1.0
    lrb = jnp.broadcast_to(lr, (SUBLANES, w))
    lib = jnp.broadcast_to(sign * li, (SUBLANES, w))
    zero = jnp.zeros((SUBLANES, w), F32)

    def tile_rows(j):
        return pl.ds(pl.multiple_of(j * SUBLANES, SUBLANES), SUBLANES)

    steps = 4 if seg % 4 == 0 else 1

    def local(i, carry):
        for u in range(steps):
            j = i * steps + u
            rows = tile_rows(seg - 1 - j if reverse else j)
            pr, pi = _cmul(lrb, lib, carry[0], carry[1])
            carry = (sr_ref[rows, :] + pr, si_ref[rows, :] + pi)
            sr_ref[rows, :] = carry[0]
            si_ref[rows, :] = carry[1]
        return carry

    end_r, end_i = lax.fori_loop(0, seg // steps, local, (zero, zero))
    full_r, full_i = pr_ref[seg - 1:seg, :], sign * pi_ref[seg - 1:seg, :]
    row = lax.broadcasted_iota(jnp.int32, (SUBLANES, w), 0)
    in_r, in_i = zero, zero
    cur_r, cur_i = jnp.zeros((1, w), F32), jnp.zeros((1, w), F32)
    for r in (range(SUBLANES - 2, -1, -1) if reverse else range(1, SUBLANES)):
        src = r + 1 if reverse else r - 1
        pr, pi = _cmul(full_r, full_i, cur_r, cur_i)
        cur_r, cur_i = end_r[src:src + 1, :] + pr, end_i[src:src + 1, :] + pi
        in_r = jnp.where(row == r, cur_r, in_r)
        in_i = jnp.where(row == r, cur_i, in_i)

    def carry_in(j, _):
        rows = tile_rows(j)
        k = seg - 1 - j if reverse else j
        pr, pi = _cmul(pr_ref[pl.ds(k, 1), :], sign * pi_ref[pl.ds(k, 1), :], in_r, in_i)
        xr, xi = sr_ref[rows, :] + pr, si_ref[rows, :] + pi
        sr_ref[rows, :] = xr
        si_ref[rows, :] = xi
        if per_tile is not None:
            per_tile(j, xr, xi)
        return 0

    lax.fori_loop(0, seg, carry_in, 0, unroll=4)


_S5_ROWS = 256


def _s5_in_specs(t, d_attn):
    u_block = (d_attn + 2 * D_KV) // SSM_CH_BLOCK
    blk3 = lambda shape: pl.BlockSpec((None,) + shape, lambda j: (j, 0, 0))
    return [
        pl.BlockSpec((t, SSM_CH_BLOCK), lambda j: (0, u_block + j)),
        blk3((SSM_CH_BLOCK, SSM_ST_BLOCK)), blk3((SSM_CH_BLOCK, SSM_ST_BLOCK)),
        blk3((1, SSM_ST_BLOCK)), blk3((1, SSM_ST_BLOCK)),
        blk3((SSM_ST_BLOCK, SSM_CH_BLOCK)), blk3((SSM_ST_BLOCK, SSM_CH_BLOCK)),
        pl.BlockSpec((1, SSM_CH_BLOCK), lambda j: (0, j)),
    ]


def _chunks(t):
    rows = min(_S5_ROWS, t)
    return rows, lambda i: pl.ds(pl.multiple_of(i * rows, rows), rows)


def _s5_states(u_ref, us_ref, bre_ref, bim_ref, lr_ref, li_ref, sr_ref, si_ref, pr_ref, pi_ref, t):
    seg = t // SUBLANES
    rows, chunk = _chunks(t)
    for c in range(t // rows):
        us_ref[c * rows:(c + 1) * rows, :] = _load_segmented(u_ref, c * rows // SUBLANES, rows // SUBLANES, seg)

    def fill(i, _):
        ub = us_ref[chunk(i), :].astype(BF16)
        sr_ref[chunk(i), :] = _dot(ub, bre_ref[...], 1, 0)
        si_ref[chunk(i), :] = _dot(ub, bim_ref[...], 1, 0)
        return 0

    lax.fori_loop(0, t // rows, fill, 0)
    _fill_powers(lr_ref[...], li_ref[...], pr_ref, pi_ref, seg)
    _scan_segments(sr_ref, si_ref, pr_ref, pi_ref, lr_ref[...], li_ref[...], seg, False)


def _s5_scratch(t):
    state = pltpu.VMEM((t, SSM_ST_BLOCK), F32)
    powers = pltpu.VMEM((t // SUBLANES, SSM_ST_BLOCK), F32)
    return state, powers, pltpu.VMEM((t, SSM_CH_BLOCK), F32)


def _s5_fwd(proj, mats, dskip_row, d_attn, d_ssm):
    t = proj.shape[0]
    seg = t // SUBLANES
    n_blocks = d_ssm // SSM_CH_BLOCK
    rows, chunk = _chunks(t)

    def body(u_ref, bre_ref, bim_ref, lr_ref, li_ref, cre_ref, cim_ref, d_ref, y_ref,
             sr_ref, si_ref, pr_ref, pi_ref, us_ref, ys_ref):
        _s5_states(u_ref, us_ref, bre_ref, bim_ref, lr_ref, li_ref, sr_ref, si_ref, pr_ref, pi_ref, t)

        def emit(i, _):
            ys_ref[chunk(i), :] = (_dot(sr_ref[chunk(i), :].astype(BF16), cre_ref[...], 1, 0)
                                   - _dot(si_ref[chunk(i), :].astype(BF16), cim_ref[...], 1, 0)
                                   + d_ref[...] * us_ref[chunk(i), :])
            return 0

        lax.fori_loop(0, t // rows, emit, 0)
        for c in range(t // rows):
            _store_segmented(y_ref, c * rows // SUBLANES, seg, ys_ref[c * rows:(c + 1) * rows, :])

    state, powers, channels = _s5_scratch(t)
    col = pl.BlockSpec((t, SSM_CH_BLOCK), lambda j: (0, j))
    return _call(body, name="s5_fwd", grid=(n_blocks,), in_specs=_s5_in_specs(t, d_attn), out_specs=col,
                 out_shape=_sds((t, d_ssm), F32), scratch_shapes=[state, state, powers, powers, channels, channels],
                 semantics=("parallel",))(proj, *mats, dskip_row)


def _s5_bwd(proj, mats, dskip_row, y, dz_a, dz_b, d_attn, d_ssm, after):
    t = proj.shape[0]
    seg = t // SUBLANES
    n_blocks = d_ssm // SSM_CH_BLOCK
    rows, chunk = _chunks(t)

    def body(u_ref, bre_ref, bim_ref, lr_ref, li_ref, cre_ref, cim_ref, d_ref, y_ref, dza_ref, dzb_ref,
             du_ref, dbre_ref, dbim_ref, dlr_ref, dli_ref, dcre_ref, dcim_ref, dd_ref,
             sr_ref, si_ref, gr_ref, gi_ref, pr_ref, pi_ref, us_ref, dys_ref, dus_ref, acc_r, acc_i):
        _s5_states(u_ref, us_ref, bre_ref, bim_ref, lr_ref, li_ref, sr_ref, si_ref, pr_ref, pi_ref, t)
        for ref in (dcre_ref, dcim_ref, dbre_ref, dbim_ref, dd_ref, acc_r, acc_i):
            ref[...] = jnp.zeros_like(ref)
        for c in range(t // rows):
            tile0, n_tiles = c * rows // SUBLANES, rows // SUBLANES
            dz = _load_segmented(dza_ref, tile0, n_tiles, seg) + _load_segmented(dzb_ref, tile0, n_tiles, seg)
            dys_ref[c * rows:(c + 1) * rows, :] = dz * _gelu_grad(_load_segmented(y_ref, tile0, n_tiles, seg))

        def through_c(i, _):
            dy = dys_ref[chunk(i), :]
            dd_ref[...] += jnp.sum(dy * us_ref[chunk(i), :], axis=0, keepdims=True)
            dyb = dy.astype(BF16)
            gr_ref[chunk(i), :] = _dot(dyb, cre_ref[...], 1, 1)
            gi_ref[chunk(i), :] = -_dot(dyb, cim_ref[...], 1, 1)
            dcre_ref[...] += _dot(sr_ref[chunk(i), :].astype(BF16), dyb, 0, 0)
            dcim_ref[...] -= _dot(si_ref[chunk(i), :].astype(BF16), dyb, 0, 0)
            return 0

        lax.fori_loop(0, t // rows, through_c, 0)

        row = lax.broadcasted_iota(jnp.int32, (SUBLANES, SSM_ST_BLOCK), 0)
        last = pl.ds((seg - 1) * SUBLANES, SUBLANES)
        wrap = [jnp.where(row == 0, 0.0, pltpu.roll(ref[last, :], 1, 0)) for ref in (sr_ref, si_ref)]

        def lambda_grad(j, g_re, g_im):
            before = pl.ds(pl.multiple_of(jnp.maximum(j - 1, 0) * SUBLANES, SUBLANES), SUBLANES)
            prev_r = jnp.where(j > 0, sr_ref[before, :], wrap[0])
            prev_i = jnp.where(j > 0, si_ref[before, :], wrap[1])
            acc_r[...] += g_re * prev_r + g_im * prev_i
            acc_i[...] += g_im * prev_r - g_re * prev_i

        _scan_segments(gr_ref, gi_ref, pr_ref, pi_ref, lr_ref[...], li_ref[...], seg, True, per_tile=lambda_grad)
        dlr_ref[...] = jnp.sum(acc_r[...], axis=0, keepdims=True)
        dli_ref[...] = jnp.sum(acc_i[...], axis=0, keepdims=True)

        def through_b(i, _):
            ub = us_ref[chunk(i), :].astype(BF16)
            grb, gib = gr_ref[chunk(i), :].astype(BF16), gi_ref[chunk(i), :].astype(BF16)
            dbre_ref[...] += _dot(ub, grb, 0, 0)
            dbim_ref[...] += _dot(ub, gib, 0, 0)
            dus_ref[chunk(i), :] = (_dot(grb, bre_ref[...], 1, 1) + _dot(gib, bim_ref[...], 1, 1)
                                    + d_ref[...] * dys_ref[chunk(i), :])
            return 0

        lax.fori_loop(0, t // rows, through_b, 0)
        for c in range(t // rows):
            _store_segmented(du_ref, c * rows // SUBLANES, seg, dus_ref[c * rows:(c + 1) * rows, :])

    col = pl.BlockSpec((t, SSM_CH_BLOCK), lambda j: (0, j))
    blk3 = lambda shape: pl.BlockSpec((None,) + shape, lambda j: (j, 0, 0))
    state, powers, channels = _s5_scratch(t)
    return _call(
        body, name="s5_bwd", grid=(n_blocks,), in_specs=_s5_in_specs(t, d_attn) + [col, col, col],
        out_specs=[col, blk3((SSM_CH_BLOCK, SSM_ST_BLOCK)), blk3((SSM_CH_BLOCK, SSM_ST_BLOCK)),
                   blk3((1, SSM_ST_BLOCK)), blk3((1, SSM_ST_BLOCK)),
                   blk3((SSM_ST_BLOCK, SSM_CH_BLOCK)), blk3((SSM_ST_BLOCK, SSM_CH_BLOCK)),
                   pl.BlockSpec((1, SSM_CH_BLOCK), lambda j: (0, j))],
        out_shape=[_sds((t, d_ssm), F32),
                   _sds((n_blocks, SSM_CH_BLOCK, SSM_ST_BLOCK), F32), _sds((n_blocks, SSM_CH_BLOCK, SSM_ST_BLOCK), F32),
                   _sds((n_blocks, 1, SSM_ST_BLOCK), F32), _sds((n_blocks, 1, SSM_ST_BLOCK), F32),
                   _sds((n_blocks, SSM_ST_BLOCK, SSM_CH_BLOCK), F32), _sds((n_blocks, SSM_ST_BLOCK, SSM_CH_BLOCK), F32),
                   _sds((1, d_ssm), F32)],
        scratch_shapes=[state, state, state, state, powers, powers, channels, channels, channels,
                        pltpu.VMEM((SUBLANES, SSM_ST_BLOCK), F32), pltpu.VMEM((SUBLANES, SSM_ST_BLOCK), F32)],
        semantics=("parallel",), n_after=len(after))(proj, *mats, dskip_row, y, dz_a, dz_b, *after)


def _by_block(gp_n):
    return gp_n.reshape(-1, GROUPS_PER_BLOCK, SSM_GROUP, SSM_STATE)


def _block_diag_in(bbar):
    eye = jnp.eye(GROUPS_PER_BLOCK, dtype=F32)
    return jnp.einsum("jgpn,gh->jgphn", _by_block(bbar), eye).reshape(-1, SSM_CH_BLOCK, SSM_ST_BLOCK)


def _block_diag_in_t(dense):
    d5 = dense.reshape(-1, GROUPS_PER_BLOCK, SSM_GROUP, GROUPS_PER_BLOCK, SSM_STATE)
    eye = jnp.eye(GROUPS_PER_BLOCK, dtype=F32)
    return jnp.einsum("jgphn,gh->jgpn", d5, eye).reshape(-1, SSM_STATE)


def _block_diag_out(c):
    eye = jnp.eye(GROUPS_PER_BLOCK, dtype=F32)
    return jnp.einsum("jgpn,gh->jgnhp", _by_block(c), eye).reshape(-1, SSM_ST_BLOCK, SSM_CH_BLOCK)


def _block_diag_out_t(dense):
    d5 = dense.reshape(-1, GROUPS_PER_BLOCK, SSM_STATE, GROUPS_PER_BLOCK, SSM_GROUP)
    eye = jnp.eye(GROUPS_PER_BLOCK, dtype=F32)
    return jnp.einsum("jgnhp,gh->jgpn", d5, eye).reshape(-1, SSM_STATE)


def _adamw(w, g, m, v):
    m = ADAM_B1 * m + (1.0 - ADAM_B1) * g
    v = ADAM_B2 * v + (1.0 - ADAM_B2) * (g * g)
    m_hat = m / (1.0 - ADAM_B1 ** ADAM_STEP)
    v_hat = v / (1.0 - ADAM_B2 ** ADAM_STEP)
    delta = -ADAM_LR * (m_hat / (jnp.sqrt(v_hat) + ADAM_EPS) + ADAM_WD * w)
    return delta, m, v


def _adam_sharded(name, parts, w, m, v, tr, row0=0):
    r, c = w.shape
    assert r % tr == 0 and row0 % tr == 0, (name, r, tr, row0)

    def body(p_ref, w_ref, m_ref, v_ref, g_out, d_out, m_out, v_out):
        g = p_ref[0].astype(F32)
        for i in range(1, p_ref.shape[0]):
            g = g + p_ref[i].astype(F32)
        delta, m_new, v_new = _adamw(w_ref[...], g, m_ref[...], v_ref[...])
        g_out[...] = g
        d_out[...] = delta
        m_out[...] = m_new
        v_out[...] = v_new

    tile = pl.BlockSpec((tr, c), lambda i: (i, 0))
    return _call(body, name=name, grid=(r // tr,),
                 in_specs=[pl.BlockSpec((parts.shape[0], tr, c), lambda i: (0, i + row0 // tr, 0)), tile, tile, tile],
                 out_specs=[tile] * 4, out_shape=[_sds((r, c), F32)] * 4, semantics=("parallel",))(parts, w, m, v)


_BIG = ("w_in", "w_glu", "w_o", "w_gate", "w_up", "w_down")
_BY_COLUMNS = ("w_in", "w_gate", "w_up")
_SMALL_VECTORS = ("sinks", "log_dt", "b_glu", "g_attn_out", "g_ssm_out", "g_post_mix", "g_pre_ffn", "g_post_ffn")
_SMALL_MATRICES = ("b_re", "b_im", "c_re", "c_im", "a_re", "a_im")
_ORDER = ("g_pre_mix", "w_in", "sinks", "a_re", "a_im", "log_dt", "b_re", "b_im", "c_re", "c_im", "d_skip", "w_glu",
          "b_glu", "g_attn_out", "g_ssm_out", "w_o", "g_post_mix", "g_pre_ffn", "w_gate", "w_up", "w_down",
          "g_post_ffn")


def _pack_grads(vectors, matrices):
    width = max(a.shape[1] for a in vectors)
    slots, row, lane = [], 0, 0
    for a in vectors:
        span = -(-a.shape[1] // LANES) * LANES
        if lane + span > width:
            row, lane = row + 1, 0
        slots.append((row, lane, a.shape[1]))
        lane += span
    firsts, at = [], 0
    for a in matrices:
        firsts.append(at)
        at += a.shape[0]
    nv = len(vectors)

    def body(*refs):
        vec_out, mat_out = refs[-2], refs[-1]
        vec_out[...] = jnp.zeros_like(vec_out)
        for ref, (r, l, w) in zip(refs[:nv], slots):
            vec_out[r:r + 1, l:l + w] = ref[...]
        for ref, r0 in zip(refs[nv:-2], firsts):
            mat_out[r0:r0 + ref.shape[0], :] = ref[...]

    ins = list(vectors) + list(matrices)
    outs = [_sds((-(-(row + 1) // SUBLANES) * SUBLANES, width), F32), _sds((at, matrices[0].shape[1]), F32)]
    vec_pack, mat_pack = _call(body, name="pack_small_grads", in_specs=_whole(ins), out_specs=_whole(outs),
                               out_shape=outs)(*ins)
    return vec_pack, slots, mat_pack, firsts


def _adam_replicated(sources, found_at, w, m, v, total_at):
    ns, n = len(sources), len(w)

    def body(*refs):
        ins, outs = refs[ns:ns + 3 * n], refs[ns + 3 * n:]
        summed = []
        for p_ref in refs[:ns]:
            g = p_ref[0]
            for k in range(1, N_DEV):
                g = g + p_ref[k]
            summed.append(g)
        for i, (src, row, lane) in enumerate(found_at):
            w_ref, m_ref, v_ref = ins[i], ins[n + i], ins[2 * n + i]
            rows, cols = w_ref.shape
            g = summed[src][row:row + rows, lane:lane + cols]
            delta, m_new, v_new = _adamw(w_ref[...], g, m_ref[...], v_ref[...])
            for o, val in zip(outs[4 * i:4 * i + 4], (g, delta, m_new, v_new)):
                o[...] = val
        t_src, t_row, t_lane, t_width = total_at
        outs[-1][...] = summed[t_src][t_row:t_row + 1, t_lane:t_lane + t_width]

    ins = list(sources) + list(w) + list(m) + list(v)
    outs = [_sds(a.shape, F32) for a in w for _ in range(4)] + [_sds((1, total_at[3]), F32)]
    flat = _call(body, name="adam_replicated", in_specs=_whole(ins), out_specs=_whole(outs), out_shape=outs)(*ins)
    return [tuple(flat[4 * i:4 * i + 4]) for i in range(n)], flat[-1]


def kernel(x, positions, g_pre_mix, w_in, sinks, a_re, a_im, log_dt, b_re, b_im, c_re, c_im, d_skip, w_glu, b_glu, g_attn_out, g_ssm_out, w_o, g_post_mix, g_pre_ffn, w_gate, w_up, w_down, g_post_ffn, loss_target, m_g_pre_mix, m_w_in, m_sinks, m_a_re, m_a_im, m_log_dt, m_b_re, m_b_im, m_c_re, m_c_im, m_d_skip, m_w_glu, m_b_glu, m_g_attn_out, m_g_ssm_out, m_w_o, m_g_post_mix, m_g_pre_ffn, m_w_gate, m_w_up, m_w_down, m_g_post_ffn, v_g_pre_mix, v_w_in, v_sinks, v_a_re, v_a_im, v_log_dt, v_b_re, v_b_im, v_c_re, v_c_im, v_d_skip, v_w_glu, v_b_glu, v_g_attn_out, v_g_ssm_out, v_w_o, v_g_post_mix, v_g_pre_ffn, v_w_gate, v_w_up, v_w_down, v_g_post_ffn):
    given = dict(locals())
    weights = {n: given[n] for n in _ORDER}
    mom_m = {n: given["m_" + n] for n in _ORDER}
    mom_v = {n: given["v_" + n] for n in _ORDER}

    t, d = x.shape[1], x.shape[2]
    d_attn = d // 2
    d_ssm = d - d_attn
    d_in = d_attn + 2 * D_KV + d_ssm
    n_groups = d_ssm // SSM_GROUP
    n_heads = d_attn // HEAD_DIM
    tm = min(256, t)

    x2 = x[0]
    target = loss_target[0]

    def by_rows(n, a):
        return a[0].T if n in _BY_COLUMNS else a[0]

    def start_gather(name, ns, token):
        behind = 0 if token is None else token[0, 0].astype(BF16)
        shards = [by_rows(n, weights[n]).astype(BF16) + behind for n in ns]
        return _exchange_start(name, shards, False, (OWN, SIBLING) + CHIP_PEERS)

    def forward_gather(handle, after):
        return _forward_start(handle["name"] + "_forward", _exchange_wait(handle, after))

    def finish_gather(handle, after):
        return _split_wait(forward_gather(handle, after)[0], [])

    ag_in, token = start_gather("gather_w_in", ["w_in"], None)
    ag_mix, token = start_gather("gather_w_glu_o", ["w_glu", "w_o"], token)
    ag_ffn_in, token = start_gather("gather_w_gate_up", ["w_gate", "w_up"], token)
    ag_down, token = start_gather("gather_w_down", ["w_down"], token)

    xn, = _rows("norm_in", lambda xv, g: ([_rms(xv)[0] * g], []), [x2], [g_pre_mix], [(d, BF16)], [], tm,
                after=[token])
    win_g, = finish_gather(ag_in, [xn])
    w_in_t = win_g.reshape(d_in, d)
    proj = _mm_nt("proj_in", xn, w_in_t, F32)

    cos, sin = _rope_tables(positions.reshape(t, 1).astype(F32))
    sinks_row = jnp.pad(sinks, ((0, 0), (0, LANES - n_heads)))
    attn = _attention_fwd(proj, cos, sin, sinks_row, d_attn)

    def view(n, a):
        if n in ("b_re", "b_im"):
            return jnp.transpose(a[0], (0, 2, 1)).reshape(-1, SSM_STATE)
        if n in ("c_re", "c_im"):
            return a[0].reshape(-1, SSM_STATE)
        return a[0].T if n == "d_skip" else a[0] if a.ndim == 3 else a

    def unview(n, val):
        if n in ("b_re", "b_im"):
            return jnp.transpose(val.reshape(n_groups, SSM_GROUP, SSM_STATE), (0, 2, 1))[None]
        if n in ("c_re", "c_im"):
            return val.reshape(1, n_groups, SSM_GROUP, SSM_STATE)
        return val.T[None] if n == "d_skip" else val[None] if weights[n].ndim == 3 else val

    b_re_v, b_im_v = view("b_re", b_re), view("b_im", b_im)
    ldt_col = log_dt.reshape(n_groups, 1)
    lam_re, lam_im, bbar_re, bbar_im = _s5_discretise(a_re[0], a_im[0], ldt_col, b_re_v, b_im_v)
    n_blocks = n_groups // GROUPS_PER_BLOCK
    mats = [_block_diag_in(bbar_re).astype(BF16), _block_diag_in(bbar_im).astype(BF16),
            lam_re.reshape(n_blocks, 1, SSM_ST_BLOCK), lam_im.reshape(n_blocks, 1, SSM_ST_BLOCK),
            _block_diag_out(view("c_re", c_re)).astype(BF16), _block_diag_out(view("c_im", c_im)).astype(BF16)]
    dskip_row = d_skip.reshape(1, d_ssm)
    forward_mix, _ = forward_gather(ag_mix, [attn])
    y_ssm = _s5_fwd(proj, mats, dskip_row, d_attn, d_ssm)
    gelu_bf16 = lambda yv: _gelu(yv).astype(BF16)
    wglu_g, wo_g = _split_wait(forward_mix, [y_ssm])
    w_glu_full = wglu_g.reshape(d_ssm, d_ssm)
    w_o_full = wo_g.reshape(d, d)
    glu_lin = _mm_nn("glu_gate", y_ssm, w_glu_full, F32, a_fn=gelu_bf16)

    def mix_prep(av, yv, gl, bg, ga, gs):
        ssm = _gelu(yv) * _sigmoid(gl + bg)
        return [jnp.concatenate([_rms(av)[0] * ga, _rms(ssm)[0] * gs], axis=1)], []

    mixed, = _rows("mix_prep", mix_prep, [attn, y_ssm, glu_lin], [b_glu, g_attn_out, g_ssm_out], [(d, BF16)], [], tm)
    mix = _mm_nn("mix_out", mixed, w_o_full, F32)

    def post_mix(xv, mv, gpm, gpf):
        h = xv + _rms(mv)[0] * gpm
        return [h, _rms(h)[0] * gpf], []

    forward_ffn_in, token = forward_gather(ag_ffn_in, [mix])
    h, hn = _rows("post_mix", post_mix, [x2, mix], [g_post_mix, g_pre_ffn], [(d, F32), (d, BF16)], [], tm,
                  after=[token])
    wgate_g, wup_g = _split_wait(forward_ffn_in, [hn])
    gate, up, hid = _ffn_in(hn, wgate_g, wup_g)
    wdown_g, = finish_gather(ag_down, [hid])
    ff = _mm_contract_slots("ffn_down", [(hid, wdown_g)], F32, per_step=2, tm=1024)

    def head(hv, fv, tv, gpo):
        out = hv + _rms(fv)[0] * gpo
        err = out - tv
        dout = err * (1.0 / d)
        dff, dg = _rms_bwd(fv, gpo, dout)
        loss = jnp.zeros((1, LANES), F32) + 0.5 * jnp.sum(err * err) * (1.0 / d)
        return [dff, dout], [dg, loss]

    dff, dh_out, dg_post_ffn, loss_row = _rows("loss_head", head, [h, ff, target], [g_post_ffn],
                                               [(d, BF16), (d, F32)], [d, LANES], tm)

    def swap_halves(name, grads):
        return _halves_start("swap_" + name, [g.reshape(N_DEV // 2, 2, *g.shape[1:]) for g in grads])

    def scatter_chip_sums(name, swap, after):
        both = _split_wait(swap, after)
        half = len(both) // 2
        sums = [_chip_sum("chip_sum_%s_%d" % (name, i), both[i], both[half + i]) for i in range(half)]
        return _exchange_start("scatter_" + name, sums, True, (OWN,) + CHIP_PEERS, by_chip=True)

    dw_down = _mm_slots_tn("ffn_down_dw", hid, dff, BF16)
    swap_down, token = swap_halves("dw_down", [dw_down])
    dgate, dup = _ffn_down_bwd(dff, wdown_g, gate, up, [token])
    rs_down, token = scatter_chip_sums("dw_down", swap_down, [dgate])
    dhn = _mm_contract_slots("ffn_in_dx", [(dgate, wgate_g), (dup, wup_g)], F32, per_step=2, tm=1024, tn=1024,
                             after=[token])
    dw_gate = _mm_slots_tn("ffn_gate_dw", dgate, hn, BF16)
    dw_up = _mm_slots_tn("ffn_up_dw", dup, hn, BF16)
    swap_ffn_in, tok_ffn_in = swap_halves("dw_gate_up", [dw_gate, dw_up])

    def mid_bwd(dho, dhn_, hv, mv, gpf, gpm):
        d1, dgpf = _rms_bwd(hv, gpf, dhn_)
        dh_ = dho + d1
        dmix_, dgpm = _rms_bwd(mv, gpm, dh_)
        return [dh_, dmix_], [dgpf, dgpm]

    dh, dmix, dg_pre_ffn, dg_post_mix = _rows("mid_bwd", mid_bwd, [dh_out, dhn, h, mix], [g_pre_ffn, g_post_mix],
                                              [(d, F32), (d, BF16)], [d, d], tm, after=[tok_ffn_in])

    dmixed = _mm_nt("mix_out_dx", dmix, w_o_full, F32)
    rs_ffn_in, token = scatter_chip_sums("dw_gate_up", swap_ffn_in, [dmixed])
    dw_o = _mm_tn("mix_out_dw", mixed, dmix, BF16, after=[token])
    swap_o, tok_o = swap_halves("dw_o", [dw_o.reshape(N_DEV, d // N_DEV, d)])

    def mix_bwd(dm, av, yv, gl, bg, ga, gs):
        dattn_, dga = _rms_bwd(av, ga, dm[:, :d_attn])
        z = _gelu(yv)
        sg = _sigmoid(gl + bg)
        dssm, dgs = _rms_bwd(z * sg, gs, dm[:, d_attn:])
        dgl = dssm * z * sg * (1.0 - sg)
        return [dattn_, dssm * sg, dgl], [dga, dgs, jnp.sum(dgl, axis=0, keepdims=True)]

    dattn, dz_direct, dglu, dg_attn_out, dg_ssm_out, db_glu = _rows(
        "mix_bwd", mix_bwd, [dmixed, attn, y_ssm, glu_lin], [b_glu, g_attn_out, g_ssm_out],
        [(d_attn, F32), (d_ssm, F32), (d_ssm, BF16)], [d_attn, d_ssm, d_ssm], tm, after=[tok_o])
    dz_glu = _mm_nt("glu_gate_dx", dglu, w_glu_full, F32)
    dw_glu = _mm_tn("glu_gate_dw", y_ssm, dglu, BF16, a_fn=gelu_bf16)
    rs_o, token = scatter_chip_sums("dw_o", swap_o, [dz_glu, dw_glu])

    du, db_re_dense, db_im_dense, dlam_re, dlam_im, dc_re_dense, dc_im_dense, dd_skip = _s5_bwd(
        proj, mats, dskip_row, y_ssm, dz_direct, dz_glu, d_attn, d_ssm, [token])
    da_re, da_im, dlog_dt, db_re_v, db_im_v = _s5_discretise_bwd(
        a_re[0], a_im[0], ldt_col, b_re_v, b_im_v, dlam_re.reshape(n_groups, SSM_STATE),
        dlam_im.reshape(n_groups, SSM_STATE), _block_diag_in_t(db_re_dense), _block_diag_in_t(db_im_dense))
    dq, dk2, dv2, dsinks_row = _attention_bwd(proj, cos, sin, sinks_row, dattn, d_attn)

    small_grads = {
        "sinks": dsinks_row, "a_re": da_re, "a_im": da_im, "log_dt": dlog_dt.reshape(1, n_groups),
        "b_re": db_re_v, "b_im": db_im_v, "c_re": _block_diag_out_t(dc_re_dense),
        "c_im": _block_diag_out_t(dc_im_dense), "d_skip": dd_skip.reshape(n_groups, SSM_GROUP).T, "b_glu": db_glu,
        "g_attn_out": dg_attn_out, "g_ssm_out": dg_ssm_out, "g_post_mix": dg_post_mix, "g_pre_ffn": dg_pre_ffn,
        "g_post_ffn": dg_post_ffn,
    }
    vec_pack, vec_slots, mat_pack, mat_rows = _pack_grads([small_grads[n] for n in _SMALL_VECTORS] + [loss_row],
                                                          [small_grads[n] for n in _SMALL_MATRICES])
    ag_small, token = _exchange_start("gather_small_grads", [vec_pack, mat_pack, small_grads["d_skip"]], False,
                                      (OWN,) + ALL_PEERS)
    dproj = _assemble_dproj(dq, dk2, dv2, du, d_in, [token])

    dw_in = _mm_tn("proj_in_dw", dproj, xn, BF16).reshape(N_DEV, d_in // N_DEV, d)
    swap_in, token = swap_halves("dw_in_glu", [dw_in, dw_glu.reshape(N_DEV, d_ssm // N_DEV, d_ssm)])
    dxn = _mm_nn("proj_in_dx", dproj, w_in_t, F32, after=[token])
    rs_in, token = scatter_chip_sums("dw_in_glu", swap_in, [dxn])

    def x_bwd(dh_, dxn_, xv, g):
        dx, dg = _rms_bwd(xv, g, dxn_)
        return [dh_ + dx], [dg]

    grad_x, dg_pre_mix = _rows("norm_in_bwd", x_bwd, [dh, dxn, x2], [g_pre_mix], [(d, F32)], [d], tm, after=[token])
    ag_last, token = _exchange_start("gather_g_pre_mix_grad", [dg_pre_mix], False, (OWN,) + ALL_PEERS)

    results = {}

    def adam_big(n, parts):
        r = parts.shape[1]
        tr = next((c for c in range(192, 15, -16) if r % c == 0), r)
        results[n] = _adam_sharded("adam_" + n, parts, by_rows(n, weights[n]), by_rows(n, mom_m[n]),
                                   by_rows(n, mom_v[n]), tr)
        return results[n][3]

    done = [grad_x, token]
    adam_big("w_down", _exchange_wait(rs_down, done)[0])
    p_gate, p_up = _exchange_wait(rs_ffn_in, done)
    done = [adam_big("w_gate", p_gate), adam_big("w_up", p_up), results["w_down"][3]]
    done = [adam_big("w_o", _exchange_wait(rs_o, done)[0])]
    vec_parts, mat_parts, dskip_parts = _exchange_wait(ag_small, done)
    for n, row0 in zip(_SMALL_MATRICES, mat_rows):
        rows = view(n, weights[n]).shape[0]
        results[n] = _adam_sharded("adam_" + n, mat_parts, view(n, weights[n]), view(n, mom_m[n]), view(n, mom_v[n]),
                                   rows, row0)
    p_in, p_glu = _exchange_wait(rs_in, [results[n][3] for n in _SMALL_MATRICES])
    done = [adam_big("w_in", p_in), adam_big("w_glu", p_glu)]
    first_gain_parts, = _exchange_wait(ag_last, done)
    rest = _SMALL_VECTORS + ("d_skip", "g_pre_mix")
    found_at = [(0, row, lane) for row, lane, _ in vec_slots[:-1]] + [(1, 0, 0), (2, 0, 0)]
    updated, loss_sum = _adam_replicated([vec_parts, dskip_parts, first_gain_parts], found_at,
                                         [view(n, weights[n]) for n in rest], [view(n, mom_m[n]) for n in rest],
                                         [view(n, mom_v[n]) for n in rest], (0,) + vec_slots[-1])
    results.update(zip(rest, updated))

    outs = [loss_sum[0, 0], grad_x[None]]
    for k in range(4):
        for n in _ORDER:
            val = results[n][k]
            outs.append(val.T[None] if n in _BY_COLUMNS else val[None] if n in _BIG else unview(n, val))
    return tuple(outs)
```

```python
import math

import jax
import jax.numpy as jnp
from jax import lax
from jax.experimental import pallas as pl
from jax.experimental.pallas import tpu as pltpu

F32 = jnp.float32
BF16 = jnp.bfloat16

HEAD_DIM = 64
N_KV_HEADS = 4
D_KV = N_KV_HEADS * HEAD_DIM
WINDOW = 128
BLOCK = 128
ROPE_THETA = 10000.0
SSM_GROUP = 16
SSM_STATE = 64
GROUPS_PER_BLOCK = 8
SSM_CH_BLOCK = GROUPS_PER_BLOCK * SSM_GROUP
SSM_ST_BLOCK = GROUPS_PER_BLOCK * SSM_STATE
RMS_EPS = 1e-6
N_DEV = 8
LANES = 128
SUBLANES = 8
MASKED = -1e30

ADAM_LR = 0.001
ADAM_B1 = 0.9
ADAM_B2 = 0.999
ADAM_EPS = 1e-08
ADAM_WD = 0.01
ADAM_STEP = 10

VMEM_LIMIT_BYTES = 56 * 1024 * 1024


def _call(body, *, name, out_shape, in_specs, out_specs, grid=(), scratch_shapes=(), semantics=None, n_after=0):
    params = dict(vmem_limit_bytes=VMEM_LIMIT_BYTES)
    if semantics is not None:
        params["dimension_semantics"] = semantics
    n_in = len(in_specs)
    if n_after:
        inner = body

        def body(*refs):
            inner(*refs[:n_in], *refs[n_in + n_after:])

        in_specs = list(in_specs) + [pl.BlockSpec(memory_space=pl.ANY)] * n_after
    return pl.pallas_call(body, name=name, grid=grid, in_specs=in_specs, out_specs=out_specs, out_shape=out_shape,
                          scratch_shapes=scratch_shapes, compiler_params=pltpu.CompilerParams(**params))


def _sds(shape, dtype):
    return jax.ShapeDtypeStruct(tuple(shape), dtype)


def _dot(a, b, ca, cb):
    return lax.dot_general(a, b, (((ca,), (cb,)), ((), ())), preferred_element_type=F32)


def _rms(x):
    r = lax.rsqrt(jnp.mean(x * x, axis=-1, keepdims=True) + RMS_EPS)
    return x * r, r


def _rms_bwd(x, g, dy):
    xh, r = _rms(x)
    dxh = dy * g
    dx = r * (dxh - xh * jnp.mean(dxh * xh, axis=-1, keepdims=True))
    return dx, jnp.sum(dy * xh, axis=0, keepdims=True)


def _sigmoid(x):
    return 1.0 / (1.0 + jnp.exp(-x))


_GELU_C = math.sqrt(2.0 / math.pi)
_GELU_A = 0.044715


def _gelu(y):
    t = jnp.tanh(_GELU_C * (y + _GELU_A * y * y * y))
    return 0.5 * y * (1.0 + t)


def _gelu_grad(y):
    t = jnp.tanh(_GELU_C * (y + _GELU_A * y * y * y))
    return 0.5 * (1.0 + t) + 0.5 * y * (1.0 - t * t) * _GELU_C * (1.0 + 3.0 * _GELU_A * y * y)


def _rows(name, fn, row_ins, vec_ins, row_outs, acc_widths, tm, after=()):
    rows = row_ins[0].shape[0]
    assert rows % tm == 0, (name, rows, tm)
    n_row, n_vec, n_out, n_acc = len(row_ins), len(vec_ins), len(row_outs), len(acc_widths)

    def body(*refs):
        ins = [r[...] for r in refs[:n_row + n_vec]]
        outs = refs[n_row + n_vec:n_row + n_vec + n_out]
        accs = refs[n_row + n_vec + n_out:]
        row_vals, acc_vals = fn(*ins)
        for o, v in zip(outs, row_vals):
            o[...] = v.astype(o.dtype)
        if n_acc:
            @pl.when(pl.program_id(0) == 0)
            def _():
                for a in accs:
                    a[...] = jnp.zeros_like(a)
            for a, v in zip(accs, acc_vals):
                a[...] += v

    in_specs = [pl.BlockSpec((tm, a.shape[1]), lambda i: (i, 0)) for a in row_ins]
    in_specs += [pl.BlockSpec(v.shape, lambda i: (0, 0)) for v in vec_ins]
    out_specs = [pl.BlockSpec((tm, w), lambda i: (i, 0)) for w, _ in row_outs]
    out_specs += [pl.BlockSpec((1, w), lambda i: (0, 0)) for w in acc_widths]
    out_shape = [_sds((rows, w), dt) for w, dt in row_outs] + [_sds((1, w), F32) for w in acc_widths]
    return _call(body, name=name, grid=(rows // tm,), in_specs=in_specs, out_specs=out_specs, out_shape=out_shape,
                 semantics=("arbitrary",) if n_acc else ("parallel",), n_after=len(after))(*row_ins, *vec_ins, *after)


def _matmul(name, operands, in_specs, product, grid, out_shape, out_spec, acc_shape, after=()):
    nk = grid[-1]
    n_in = len(operands)
    in_place = out_shape.dtype == F32

    def body(*refs):
        ins = [r[...] for r in refs[:n_in]]
        o_ref = refs[n_in]
        if nk == 1:
            o_ref[...] = product(*ins).astype(o_ref.dtype)
            return
        acc = o_ref if in_place else refs[n_in + 1]
        k = pl.program_id(len(grid) - 1)

        @pl.when(k == 0)
        def _():
            acc[...] = jnp.zeros_like(acc)

        acc[...] += product(*ins)

        if not in_place:
            @pl.when(k == nk - 1)
            def _():
                o_ref[...] = acc[...].astype(o_ref.dtype)

    return _call(body, name=name, grid=grid, in_specs=in_specs, out_specs=out_spec, out_shape=out_shape,
                 scratch_shapes=[] if nk == 1 or in_place else [pltpu.VMEM(acc_shape, F32)],
                 semantics=("parallel",) * (len(grid) - 1) + ("arbitrary",), n_after=len(after))(*operands, *after)


def _mm_nn(name, a, b, out_dtype, tm=512, tn=None, a_fn=lambda x: x, after=()):
    m, k = a.shape
    n = b.shape[1]
    tm, tn = min(tm, m), n if tn is None else tn
    return _matmul(name, [a, b],
                   [pl.BlockSpec((tm, k), lambda i, j, s: (i, 0)), pl.BlockSpec((k, tn), lambda i, j, s: (0, j))],
                   lambda x, y: _dot(a_fn(x), y, 1, 0), (m // tm, n // tn, 1), _sds((m, n), out_dtype),
                   pl.BlockSpec((tm, tn), lambda i, j, s: (i, j)), (tm, tn), after)


def _mm_nt(name, a, b, out_dtype, tm=512, tn=None):
    m, k = a.shape
    n = b.shape[0]
    tm, tn = min(tm, m), n if tn is None else tn
    return _matmul(name, [a, b],
                   [pl.BlockSpec((tm, k), lambda i, j, s: (i, 0)), pl.BlockSpec((tn, k), lambda i, j, s: (j, 0))],
                   lambda x, y: _dot(x, y, 1, 1), (m // tm, n // tn, 1), _sds((m, n), out_dtype),
                   pl.BlockSpec((tm, tn), lambda i, j, s: (i, j)), (tm, tn))


def _mm_tn(name, a, b, out_dtype, tm=512, tn=None, tk=2048, a_fn=lambda x: x, after=()):
    k, m = a.shape
    n = b.shape[1]
    tm, tk, tn = min(tm, m), min(tk, k), n if tn is None else tn
    return _matmul(name, [a, b],
                   [pl.BlockSpec((tk, tm), lambda i, j, s: (s, i)), pl.BlockSpec((tk, tn), lambda i, j, s: (s, j))],
                   lambda x, y: _dot(a_fn(x), y, 0, 0), (m // tm, n // tn, k // tk), _sds((m, n), out_dtype),
                   pl.BlockSpec((tm, tn), lambda i, j, s: (i, j)), (tm, tn), after)


def _mm_contract_slots(name, pairs, out_dtype, per_step, tm=512, tn=2048, after=()):
    s_, m, k = pairs[0][0].shape
    n = pairs[0][1].shape[2]
    tm, tn = min(tm, m), min(tn, n)
    ops, specs = [], []
    for a, b in pairs:
        ops += [a, b]
        specs += [pl.BlockSpec((per_step, tm, k), lambda i, j, s: (s, i, 0)),
                  pl.BlockSpec((per_step, k, tn), lambda i, j, s: (s, 0, j))]

    def product(*t):
        return sum(_dot(t[2 * p][q], t[2 * p + 1][q], 1, 0) for p in range(len(pairs)) for q in range(per_step))

    return _matmul(name, ops, specs, product, (m // tm, n // tn, s_ // per_step), _sds((m, n), out_dtype),
                   pl.BlockSpec((tm, tn), lambda i, j, s: (i, j)), (tm, tn), after)


def _mm_slots_tn(name, a, b, out_dtype, tn=2048, tk=2048):
    s_, k, m = a.shape
    n = b.shape[1]
    tn, tk = min(tn, n), min(tk, k)
    return _matmul(name, [a, b],
                   [pl.BlockSpec((None, tk, m), lambda s, j, z: (s, z, 0)), pl.BlockSpec((tk, tn), lambda s, j, z: (z, j))],
                   lambda x, y: _dot(x, y, 0, 0), (s_, n // tn, k // tk), _sds((s_, m, n), out_dtype),
                   pl.BlockSpec((None, m, tn), lambda s, j, z: (s, 0, j)), (m, tn))


def _ffn_in(a, w_gate, w_up, tm=1024):
    m, k = a.shape
    s_, n, _ = w_gate.shape
    tm = min(tm, m)

    def body(a_ref, wg_ref, wu_ref, g_ref, u_ref, h_ref):
        x = a_ref[...]
        g = _dot(x, wg_ref[...], 1, 1)
        u = _dot(x, wu_ref[...], 1, 1)
        g_ref[...] = g.astype(BF16)
        u_ref[...] = u.astype(BF16)
        h_ref[...] = (g * _sigmoid(g) * u).astype(BF16)

    w_spec = pl.BlockSpec((None, n, k), lambda s, i: (s, 0, 0))
    o_spec = pl.BlockSpec((None, tm, n), lambda s, i: (s, i, 0))
    return _call(body, name="ffn_in", grid=(s_, m // tm),
                 in_specs=[pl.BlockSpec((tm, k), lambda s, i: (i, 0)), w_spec, w_spec], out_specs=[o_spec] * 3,
                 out_shape=[_sds((s_, m, n), BF16)] * 3, semantics=("parallel", "parallel"))(a, w_gate, w_up)


def _ffn_down_bwd(d_out, w_down, gate, up, after, tm=1024):
    m, k = d_out.shape
    s_, n, _ = w_down.shape
    tm = min(tm, m)

    def body(d_ref, w_ref, g_ref, u_ref, dg_ref, du_ref):
        rows = pl.ds(pl.multiple_of(pl.program_id(1) * tm, tm), tm)
        dh = _dot(d_ref[rows, :], w_ref[...], 1, 1)
        g = g_ref[...].astype(F32)
        sg = _sigmoid(g)
        dg_ref[...] = (dh * u_ref[...].astype(F32) * sg * (1.0 + g * (1.0 - sg))).astype(BF16)
        du_ref[...] = (dh * g * sg).astype(BF16)

    t_spec = pl.BlockSpec((None, tm, n), lambda s, i: (s, i, 0))
    return _call(body, name="ffn_down_dx", grid=(s_, m // tm),
                 in_specs=[pl.BlockSpec((m, k), lambda s, i: (0, 0)), pl.BlockSpec((None, n, k), lambda s, i: (s, 0, 0)),
                           t_spec, t_spec],
                 out_specs=[t_spec] * 2, out_shape=[_sds((s_, m, n), BF16)] * 2, semantics=("parallel", "parallel"),
                 n_after=len(after))(d_out, w_down, gate, up, *after)


ALL_PEERS = (1, 2, 3, 4, 5, 6, 7)
CHIP_PEERS = (2, 4, 6)
SIBLING = 1
OWN = 0


def _peer(relation):
    x, y, c = lax.axis_index("x"), lax.axis_index("y"), lax.axis_index("c")
    pos = (1 - x if relation & 4 else x, 1 - y if relation & 2 else y, 1 - c if relation & 1 else c)
    return pos, 4 * pos[0] + 2 * pos[1] + pos[2]


def _slot(relation, by_chip):
    pos, device = _peer(relation)
    return 2 * pos[0] + pos[1] if by_chip else device


def _exchange_copies(ins, lands, send_sems, recv_sems, scatter, relations, by_chip=False):
    me = _slot(0, by_chip)

    def copy(a, s, peer, pos, dst_slot):
        return pltpu.make_async_remote_copy(
            src_ref=ins[a].at[peer] if scatter else ins[a], dst_ref=lands[a].at[dst_slot],
            send_sem=send_sems.at[s], recv_sem=recv_sems.at[s], device_id=pos, device_id_type=pl.DeviceIdType.MESH)

    pairs = []
    for k, r in enumerate(relations):
        pos, peer = _peer(r)[0], _slot(r, by_chip)
        for a in range(len(ins)):
            s = a * len(relations) + k
            pairs.append((copy(a, s, peer, pos, me), copy(a, s, peer, pos, peer)))
    return pairs


def _halves_copies(arrays, lands, send_sems, recv_sems):
    sibling, _ = _peer(SIBLING)
    core = lax.axis_index("c")
    pairs = []
    for a, (ref, land) in enumerate(zip(arrays, lands)):
        send = pltpu.make_async_remote_copy(
            src_ref=ref.at[:, pl.ds(1 - core, 1)], dst_ref=land, send_sem=send_sems.at[a], recv_sem=recv_sems.at[a],
            device_id=sibling, device_id_type=pl.DeviceIdType.MESH)
        pairs.append((send, send))
    return pairs


def _forward_copies(lands, send_sems, recv_sems):
    sibling, _ = _peer(SIBLING)

    def copy(a, s, slot):
        return pltpu.make_async_remote_copy(
            src_ref=lands[a].at[slot], dst_ref=lands[a].at[slot], send_sem=send_sems.at[s], recv_sem=recv_sems.at[s],
            device_id=sibling, device_id_type=pl.DeviceIdType.MESH)

    pairs = []
    for k, r in enumerate(CHIP_PEERS):
        _, mine = _peer(r)
        _, theirs = _peer(r | SIBLING)
        for a in range(len(lands)):
            s = a * len(CHIP_PEERS) + k
            pairs.append((copy(a, s, mine), copy(a, s, theirs)))
    return pairs


_HBM_SPEC = pl.BlockSpec(memory_space=pltpu.HBM)
_SEM_SPEC = pl.BlockSpec(memory_space=pltpu.SEMAPHORE)
_SIDE_EFFECT = pltpu.SideEffectType.DATAFLOW_SIDE_EFFECTING


def _split_start(name, operands, n_sem, make_pairs):
    k = len(operands)

    def body(*refs):
        send_sems, recv_sems, token = refs[k], refs[k + 1], refs[-1]
        for send, _ in make_pairs(refs[:k], send_sems, recv_sems):
            send.start()
        token[...] = jnp.zeros_like(token)

    out = pl.pallas_call(
        body, name=name,
        out_shape=(pltpu.SemaphoreType.DMA((n_sem,)), pltpu.SemaphoreType.DMA((n_sem,)),
                   *[pltpu.HBM(a.shape, a.dtype) for a in operands], _sds((SUBLANES, LANES), F32)),
        in_specs=[_HBM_SPEC] * k,
        out_specs=(_SEM_SPEC, _SEM_SPEC, *[_HBM_SPEC] * k, pl.BlockSpec(memory_space=pltpu.VMEM)),
        input_output_aliases={i: 2 + i for i in range(k)},
        compiler_params=pltpu.CompilerParams(has_side_effects=_SIDE_EFFECT),
    )(*[pltpu.with_memory_space_constraint(a, pltpu.HBM) for a in operands])
    return dict(name=name, sems=out[:2], thru=list(out[2:2 + k]), make_pairs=make_pairs), out[-1]


def _split_wait(handle, after):
    thru, make_pairs = handle["thru"], handle["make_pairs"]
    k = len(thru)

    def body(*refs):
        for send, arrival in make_pairs(refs[:k], refs[k], refs[k + 1]):
            send.wait_send()
            arrival.wait_recv()

    return pl.pallas_call(
        body, name=handle["name"] + "_wait", out_shape=[pltpu.HBM(a.shape, a.dtype) for a in thru],
        in_specs=[_HBM_SPEC] * k + [_SEM_SPEC, _SEM_SPEC] + [pl.BlockSpec(memory_space=pl.ANY)] * len(after),
        out_specs=[_HBM_SPEC] * k, input_output_aliases={i: i for i in range(k)},
        compiler_params=pltpu.CompilerParams(has_side_effects=_SIDE_EFFECT),
    )(*thru, *handle["sems"], *after)


def _exchange_start(name, arrays, scatter, relations, by_chip=False):
    n = len(arrays)
    lands = [lax.empty(a.shape if scatter else (N_DEV,) + a.shape, a.dtype) for a in arrays]

    def make_pairs(refs, send_sems, recv_sems):
        return _exchange_copies(refs[:n], refs[n:], send_sems, recv_sems, scatter, relations, by_chip)

    handle, token = _split_start(name, list(arrays) + lands, n * len(relations), make_pairs)
    handle.update(n=n)
    return handle, token


def _halves_start(name, arrays):
    lands = [lax.empty((a.shape[0], 1) + a.shape[2:], a.dtype) for a in arrays]
    n = len(arrays)

    def make_pairs(refs, send_sems, recv_sems):
        return _halves_copies(refs[:n], refs[n:], send_sems, recv_sems)

    return _split_start(name, list(arrays) + lands, n, make_pairs)


def _chip_sum(name, array, landed):
    chips, _, r, c = array.shape
    tr = r // 2 if r > 512 and r % 32 == 0 else r

    def body(a_ref, b_ref, o_ref):
        mine = a_ref[lax.axis_index("c")].astype(F32)
        o_ref[...] = (mine + b_ref[...].astype(F32)).astype(o_ref.dtype)

    return _call(body, name=name, grid=(chips, r // tr),
                 in_specs=[pl.BlockSpec((None, 2, tr, c), lambda k, i: (k, 0, i, 0)),
                           pl.BlockSpec((None, None, tr, c), lambda k, i: (k, 0, i, 0))],
                 out_specs=pl.BlockSpec((None, tr, c), lambda k, i: (k, i, 0)),
                 out_shape=_sds((chips, r, c), BF16), semantics=("parallel", "parallel"))(array, landed)


def _forward_start(name, lands):
    return _split_start(name, list(lands), len(lands) * len(CHIP_PEERS), _forward_copies)


def _exchange_wait(handle, after):
    return _split_wait(handle, after)[handle["n"]:]


def _rope_tables(pos_col):
    t = pos_col.shape[0]
    half = HEAD_DIM // 2
    inv_freq = ROPE_THETA ** (-jnp.arange(half, dtype=F32) / half)
    inv_row = jnp.tile(inv_freq, LANES // half)[None, :]

    def body(pos_ref, inv_ref, cos_ref, sin_ref):
        ang = pos_ref[...] * inv_ref[...]
        cos_ref[...] = jnp.cos(ang)
        sin_ref[...] = jnp.sin(ang)

    tm = min(t, 512)
    return _call(body, name="rope_tables", grid=(t // tm,),
                 in_specs=[pl.BlockSpec((tm, 1), lambda i: (i, 0)), pl.BlockSpec((1, LANES), lambda i: (0, 0))],
                 out_specs=[pl.BlockSpec((tm, LANES), lambda i: (i, 0))] * 2,
                 out_shape=[_sds((t, LANES), F32)] * 2, semantics=("parallel",))(pos_col, inv_row)


def _rot_half(x):
    lane = lax.broadcasted_iota(jnp.int32, x.shape, 1)
    low = (lane % HEAD_DIM) < HEAD_DIM // 2
    return jnp.where(low, -pltpu.roll(x, LANES - HEAD_DIM // 2, 1), pltpu.roll(x, HEAD_DIM // 2, 1))


def _rope(x, cos, sin):
    return x * cos + _rot_half(x) * sin


def _unrope(d, cos, sin):
    return d * cos - _rot_half(d) * sin


def _band_mask(first_block, heads):
    r = lax.broadcasted_iota(jnp.int32, (heads * BLOCK, 2 * BLOCK), 0) % BLOCK
    c = lax.broadcasted_iota(jnp.int32, (heads * BLOCK, 2 * BLOCK), 1)
    diff = r - c + BLOCK
    return (diff >= 0) & (diff < WINDOW) & ((c >= BLOCK) | jnp.logical_not(first_block))


def _attn_specs(t, d_attn, d_in):
    kb, vb = d_attn // D_KV, d_attn // D_KV + 1
    prev = lambda i: jnp.maximum(i - 1, 0)
    return [
        pl.BlockSpec((BLOCK, d_attn), lambda i: (i, 0)),
        pl.BlockSpec((BLOCK, D_KV), lambda i: (i, kb)),
        pl.BlockSpec((BLOCK, D_KV), lambda i: (i, vb)),
        pl.BlockSpec((BLOCK, D_KV), lambda i: (prev(i), kb)),
        pl.BlockSpec((BLOCK, D_KV), lambda i: (prev(i), vb)),
        pl.BlockSpec((BLOCK, LANES), lambda i: (i, 0)),
        pl.BlockSpec((BLOCK, LANES), lambda i: (i, 0)),
        pl.BlockSpec((BLOCK, LANES), lambda i: (prev(i), 0)),
        pl.BlockSpec((BLOCK, LANES), lambda i: (prev(i), 0)),
        pl.BlockSpec((1, LANES), lambda i: (0, 0)),
    ]


def _head(x, h):
    return x[:, h * HEAD_DIM:(h + 1) * HEAD_DIM]


def _attn_heads(q_ref, kc_ref, vc_ref, kp_ref, vp_ref, cq_ref, sq_ref, cp_ref, sp_ref, d_attn):
    cq, sq, cp, sp = cq_ref[...], sq_ref[...], cp_ref[...], sp_ref[...]
    q_rot = [_rope(q_ref[:, j * LANES:(j + 1) * LANES], cq, sq) for j in range(d_attn // LANES)]
    kc_rot = [_rope(kc_ref[:, j * LANES:(j + 1) * LANES], cq, sq) for j in range(D_KV // LANES)]
    kp_rot = [_rope(kp_ref[:, j * LANES:(j + 1) * LANES], cp, sp) for j in range(D_KV // LANES)]
    per = LANES // HEAD_DIM
    q_heads = [_head(q_rot[h // per], h % per).astype(BF16) for h in range(d_attn // HEAD_DIM)]
    kk = [jnp.concatenate([_head(kp_rot[g // per], g % per), _head(kc_rot[g // per], g % per)], axis=0).astype(BF16)
          for g in range(N_KV_HEADS)]
    vv = [jnp.concatenate([_head(vp_ref[...], g), _head(vc_ref[...], g)], axis=0).astype(BF16) for g in range(N_KV_HEADS)]
    return q_heads, kk, vv


def _stack_group(q_heads, sink_ref, group):
    q_all = jnp.concatenate([q_heads[h] for h in group], axis=0)
    sink_all = jnp.concatenate([jnp.broadcast_to(sink_ref[:, h:h + 1], (BLOCK, 1)) for h in group], axis=0)
    return q_all, sink_all


def _softmax_with_sink(q, kk, sink, mask):
    s = _dot(q, kk, 1, 1) * (1.0 / math.sqrt(HEAD_DIM))
    s = jnp.where(mask, s, MASKED)
    m = jnp.maximum(jnp.max(s, axis=-1, keepdims=True), sink)
    p = jnp.exp(s - m)
    e_sink = jnp.exp(sink - m)
    inv = 1.0 / (jnp.sum(p, axis=-1, keepdims=True) + e_sink)
    return p * inv, e_sink * inv


def _attention_fwd(proj, cos, sin, sinks_row, d_attn):
    t, d_in = proj.shape
    n_heads = d_attn // HEAD_DIM
    q_per_kv = n_heads // N_KV_HEADS

    def body(q_ref, kc_ref, vc_ref, kp_ref, vp_ref, cq_ref, sq_ref, cp_ref, sp_ref, sink_ref, o_ref):
        mask = _band_mask(pl.program_id(0) == 0, q_per_kv)
        q_heads, kk, vv = _attn_heads(q_ref, kc_ref, vc_ref, kp_ref, vp_ref, cq_ref, sq_ref, cp_ref, sp_ref, d_attn)
        for g in range(N_KV_HEADS):
            group = range(g * q_per_kv, (g + 1) * q_per_kv)
            q_all, sink_all = _stack_group(q_heads, sink_ref, group)
            probs, _ = _softmax_with_sink(q_all, kk[g], sink_all, mask)
            o_all = _dot(probs.astype(BF16), vv[g], 1, 0)
            for k, h in enumerate(group):
                o_ref[:, h * HEAD_DIM:(h + 1) * HEAD_DIM] = o_all[k * BLOCK:(k + 1) * BLOCK]

    return _call(body, name="attention_fwd", grid=(t // BLOCK,), in_specs=_attn_specs(t, d_attn, d_in),
                 out_specs=pl.BlockSpec((BLOCK, d_attn), lambda i: (i, 0)), out_shape=_sds((t, d_attn), F32),
                 semantics=("parallel",))(proj, proj, proj, proj, proj, cos, sin, cos, sin, sinks_row)


def _attention_bwd(proj, cos, sin, sinks_row, d_out, d_attn):
    t, d_in = proj.shape
    n_heads = d_attn // HEAD_DIM
    q_per_kv = n_heads // N_KV_HEADS
    nb = t // BLOCK
    per = LANES // HEAD_DIM
    stack = q_per_kv

    def body(q_ref, kc_ref, vc_ref, kp_ref, vp_ref, cq_ref, sq_ref, cp_ref, sp_ref, sink_ref, do_ref,
             dq_ref, dk_ref, dv_ref, dsink_ref):
        i = pl.program_id(0)
        mask = _band_mask(i == 0, stack)
        q_heads, kk, vv = _attn_heads(q_ref, kc_ref, vc_ref, kp_ref, vp_ref, cq_ref, sq_ref, cp_ref, sp_ref, d_attn)
        lane = lax.broadcasted_iota(jnp.int32, (1, LANES), 1)
        dsink = jnp.zeros((1, LANES), F32)
        dq_rot, dkk, dvv = [], [], []
        for g in range(N_KV_HEADS):
            dkk_g = jnp.zeros((2 * BLOCK, HEAD_DIM), F32)
            dvv_g = jnp.zeros((2 * BLOCK, HEAD_DIM), F32)
            for first in range(g * q_per_kv, (g + 1) * q_per_kv, stack):
                group = range(first, first + stack)
                q_all, sink_all = _stack_group(q_heads, sink_ref, group)
                probs, p_sink = _softmax_with_sink(q_all, kk[g], sink_all, mask)
                do_all = jnp.concatenate([do_ref[:, h * HEAD_DIM:(h + 1) * HEAD_DIM] for h in group],
                                         axis=0).astype(BF16)
                dp = _dot(do_all, vv[g], 1, 1)
                delta = jnp.sum(probs * dp, axis=-1, keepdims=True)
                ds = (probs * (dp - delta) * (1.0 / math.sqrt(HEAD_DIM))).astype(BF16)
                dq_all = _dot(ds, kk[g], 1, 0)
                dkk_g += _dot(ds, q_all, 0, 0)
                dvv_g += _dot(probs.astype(BF16), do_all, 0, 0)
                sink_term = p_sink * delta
                for k, h in enumerate(group):
                    dq_rot.append(dq_all[k * BLOCK:(k + 1) * BLOCK])
                    part = jnp.sum(sink_term[k * BLOCK:(k + 1) * BLOCK], axis=0, keepdims=True)
                    dsink += jnp.where(lane == h, -part, 0.0)
            dkk.append(dkk_g)
            dvv.append(dvv_g)
        cq, sq, cp, sp = cq_ref[...], sq_ref[...], cp_ref[...], sp_ref[...]
        for j in range(d_attn // LANES):
            d = jnp.concatenate(dq_rot[j * per:(j + 1) * per], axis=1)
            dq_ref[:, j * LANES:(j + 1) * LANES] = _unrope(d, cq, sq)
        for j in range(D_KV // LANES):
            d = jnp.concatenate(dkk[j * per:(j + 1) * per], axis=1)
            dk_ref[0, :, j * LANES:(j + 1) * LANES] = _unrope(d[:BLOCK], cp, sp)
            dk_ref[1, :, j * LANES:(j + 1) * LANES] = _unrope(d[BLOCK:], cq, sq)
            d = jnp.concatenate(dvv[j * per:(j + 1) * per], axis=1)
            dv_ref[0, :, j * LANES:(j + 1) * LANES] = d[:BLOCK]
            dv_ref[1, :, j * LANES:(j + 1) * LANES] = d[BLOCK:]

        @pl.when(i == 0)
        def _():
            dsink_ref[...] = jnp.zeros_like(dsink_ref)

        dsink_ref[...] += dsink

    pair = pl.BlockSpec((2, BLOCK, D_KV), lambda i: (i, 0, 0))
    return _call(body, name="attention_bwd", grid=(nb,),
                 in_specs=_attn_specs(t, d_attn, d_in) + [pl.BlockSpec((BLOCK, d_attn), lambda i: (i, 0))],
                 out_specs=[pl.BlockSpec((BLOCK, d_attn), lambda i: (i, 0)), pair, pair,
                            pl.BlockSpec((1, LANES), lambda i: (0, 0))],
                 out_shape=[_sds((t, d_attn), F32), _sds((2 * nb, BLOCK, D_KV), F32), _sds((2 * nb, BLOCK, D_KV), F32),
                            _sds((1, LANES), F32)],
                 semantics=("arbitrary",))(proj, proj, proj, proj, proj, cos, sin, cos, sin, sinks_row, d_out)


def _assemble_dproj(dq, dk2, dv2, du, d_in, after):
    t, d_attn = dq.shape
    d_ssm = du.shape[1]
    nb = t // BLOCK

    def body(dq_ref, dk_own, dk_next, dv_own, dv_next, du_ref, o_ref):
        has_next = (pl.program_id(0) < nb - 1).astype(F32)
        o_ref[:, :d_attn] = dq_ref[...].astype(BF16)
        o_ref[:, d_attn:d_attn + D_KV] = (dk_own[...] + has_next * dk_next[...]).astype(BF16)
        o_ref[:, d_attn + D_KV:d_attn + 2 * D_KV] = (dv_own[...] + has_next * dv_next[...]).astype(BF16)
        o_ref[:, d_attn + 2 * D_KV:] = du_ref[...].astype(BF16)

    own = pl.BlockSpec((None, BLOCK, D_KV), lambda i: (2 * i + 1, 0, 0))
    nxt = pl.BlockSpec((None, BLOCK, D_KV), lambda i: (jnp.minimum(2 * i + 2, 2 * nb - 1), 0, 0))
    return _call(body, name="assemble_dproj", grid=(nb,),
                 in_specs=[pl.BlockSpec((BLOCK, d_attn), lambda i: (i, 0)), own, nxt, own, nxt,
                           pl.BlockSpec((BLOCK, d_ssm), lambda i: (i, 0))],
                 out_specs=pl.BlockSpec((BLOCK, d_in), lambda i: (i, 0)), out_shape=_sds((t, d_in), BF16),
                 semantics=("parallel",), n_after=len(after))(dq, dk2, dk2, dv2, dv2, du, *after)


def _discretise(ar, ai, ldt, br, bi):
    dt = jnp.exp(ldt)
    mag = jnp.exp(ar * dt)
    lam_re = mag * jnp.cos(ai * dt)
    lam_im = mag * jnp.sin(ai * dt)
    den = ar * ar + ai * ai
    nr = lam_re - 1.0
    ni = lam_im
    f_re = (nr * ar + ni * ai) / den
    f_im = (ni * ar - nr * ai) / den
    return (lam_re, lam_im, [f_re * r - f_im * i for r, i in zip(br, bi)], [f_re * i + f_im * r for r, i in zip(br, bi)])


def _whole(arrays):
    return [pl.BlockSpec(a.shape, lambda *_, nd=len(a.shape): (0,) * nd) for a in arrays]


def _channels(ref):
    groups = ref.shape[0] // SSM_GROUP
    return [ref[pl.ds(p, groups, stride=SSM_GROUP), :] for p in range(SSM_GROUP)]


def _store_channels(ref, values):
    groups = ref.shape[0] // SSM_GROUP
    for p, val in enumerate(values):
        ref[pl.ds(p, groups, stride=SSM_GROUP), :] = val


def _s5_discretise(ar, ai, ldt, br, bi):
    ins = [ar, ai, ldt, br, bi]

    def body(ar_ref, ai_ref, ldt_ref, br_ref, bi_ref, lr_ref, li_ref, bbr_ref, bbi_ref):
        lr, li, bbr, bbi = _discretise(ar_ref[...], ai_ref[...], ldt_ref[...], _channels(br_ref), _channels(bi_ref))
        lr_ref[...] = lr
        li_ref[...] = li
        _store_channels(bbr_ref, bbr)
        _store_channels(bbi_ref, bbi)

    outs = [_sds(ar.shape, F32), _sds(ar.shape, F32), _sds(br.shape, F32), _sds(br.shape, F32)]
    return _call(body, name="s5_discretise", in_specs=_whole(ins), out_specs=_whole(outs), out_shape=outs)(*ins)


def _s5_discretise_bwd(ar, ai, ldt, br, bi, d_lr, d_li, d_bbr, d_bbi):
    ins = [ar, ai, ldt, br, bi, d_lr, d_li, d_bbr, d_bbi]

    def body(ar_ref, ai_ref, ldt_ref, br_ref, bi_ref, dlr_ref, dli_ref, dbbr_ref, dbbi_ref,
             dar_ref, dai_ref, dldt_ref, dbr_ref, dbi_ref):
        _, vjp = jax.vjp(_discretise, ar_ref[...], ai_ref[...], ldt_ref[...], _channels(br_ref), _channels(bi_ref))
        dar, dai, dldt, dbr, dbi = vjp((dlr_ref[...], dli_ref[...], _channels(dbbr_ref), _channels(dbbi_ref)))
        dar_ref[...] = dar
        dai_ref[...] = dai
        dldt_ref[...] = dldt
        _store_channels(dbr_ref, dbr)
        _store_channels(dbi_ref, dbi)

    outs = [_sds(a.shape, F32) for a in (ar, ai, ldt, br, bi)]
    return _call(body, name="s5_discretise_bwd", in_specs=_whole(ins), out_specs=_whole(outs), out_shape=outs)(*ins)


def _cmul(ar, ai, br, bi):
    return ar * br - ai * bi, ar * bi + ai * br


def _load_segmented(ref, tile0, n_tiles, seg):
    return jnp.concatenate([ref[pl.ds(tile0 + j, SUBLANES, stride=seg), :] for j in range(n_tiles)], axis=0)


def _store_segmented(ref, tile0, seg, value):
    for j in range(value.shape[0] // SUBLANES):
        ref[pl.ds(tile0 + j, SUBLANES, stride=seg), :] = value[j * SUBLANES:(j + 1) * SUBLANES, :]


def _fill_powers(lr, li, pr_ref, pi_ref, seg):
    pows = [(lr, li)]
    for _ in range(SUBLANES - 1):
        pows.append(_cmul(pows[-1][0], pows[-1][1], lr, li))
    row = lax.broadcasted_iota(jnp.int32, (SUBLANES, lr.shape[1]), 0)
    tr = jnp.zeros((SUBLANES, lr.shape[1]), F32)
    ti = jnp.zeros((SUBLANES, lr.shape[1]), F32)
    for r in range(SUBLANES):
        tr = jnp.where(row == r, pows[r][0], tr)
        ti = jnp.where(row == r, pows[r][1], ti)
    pr_ref[0:SUBLANES, :] = tr
    pi_ref[0:SUBLANES, :] = ti
    k = SUBLANES
    while k < seg:
        fr, fi = pr_ref[k - 1:k, :], pi_ref[k - 1:k, :]
        for t0 in range(0, k, SUBLANES):
            nr, ni = _cmul(pr_ref[t0:t0 + SUBLANES, :], pi_ref[t0:t0 + SUBLANES, :], fr, fi)
            pr_ref[k + t0:k + t0 + SUBLANES, :] = nr
            pi_ref[k + t0:k + t0 + SUBLANES, :] = ni
        k *= 2


def _scan_segments(sr_ref, si_ref, pr_ref, pi_ref, lr, li, seg, reverse, per_tile=None):
    w = lr.shape[1]
    sign = -1.0 if reverse else 1.0
    lrb = jnp.broadcast_to(lr, (SUBLANES, w))
    lib = jnp.broadcast_to(sign * li, (SUBLANES, w))
    zero = jnp.zeros((SUBLANES, w), F32)

    def tile_rows(j):
        return pl.ds(pl.multiple_of(j * SUBLANES, SUBLANES), SUBLANES)

    steps = 4 if seg % 4 == 0 else 1

    def local(i, carry):
        for u in range(steps):
            j = i * steps + u
            rows = tile_rows(seg - 1 - j if reverse else j)
            pr, pi = _cmul(lrb, lib, carry[0], carry[1])
            carry = (sr_ref[rows, :] + pr, si_ref[rows, :] + pi)
            sr_ref[rows, :] = carry[0]
            si_ref[rows, :] = carry[1]
        return carry

    end_r, end_i = lax.fori_loop(0, seg // steps, local, (zero, zero))
    full_r, full_i = pr_ref[seg - 1:seg, :], sign * pi_ref[seg - 1:seg, :]
    row = lax.broadcasted_iota(jnp.int32, (SUBLANES, w), 0)
    in_r, in_i = zero, zero
    cur_r, cur_i = jnp.zeros((1, w), F32), jnp.zeros((1, w), F32)
    for r in (range(SUBLANES - 2, -1, -1) if reverse else range(1, SUBLANES)):
        src = r + 1 if reverse else r - 1
        pr, pi = _cmul(full_r, full_i, cur_r, cur_i)
        cur_r, cur_i = end_r[src:src + 1, :] + pr, end_i[src:src + 1, :] + pi
        in_r = jnp.where(row == r, cur_r, in_r)
        in_i = jnp.where(row == r, cur_i, in_i)

    def carry_in(j, _):
        rows = tile_rows(j)
        k = seg - 1 - j if reverse else j
        pr, pi = _cmul(pr_ref[pl.ds(k, 1), :], sign * pi_ref[pl.ds(k, 1), :], in_r, in_i)
        xr, xi = sr_ref[rows, :] + pr, si_ref[rows, :] + pi
        sr_ref[rows, :] = xr
        si_ref[rows, :] = xi
        if per_tile is not None:
            per_tile(j, xr, xi)
        return 0

    lax.fori_loop(0, seg, carry_in, 0, unroll=4)


_S5_ROWS = 2048


def _s5_in_specs(t, d_attn):
    u_block = (d_attn + 2 * D_KV) // SSM_CH_BLOCK
    blk3 = lambda shape: pl.BlockSpec((None,) + shape, lambda j: (j, 0, 0))
    return [
        pl.BlockSpec((t, SSM_CH_BLOCK), lambda j: (0, u_block + j)),
        blk3((SSM_CH_BLOCK, SSM_ST_BLOCK)), blk3((SSM_CH_BLOCK, SSM_ST_BLOCK)),
        blk3((1, SSM_ST_BLOCK)), blk3((1, SSM_ST_BLOCK)),
        blk3((SSM_ST_BLOCK, SSM_CH_BLOCK)), blk3((SSM_ST_BLOCK, SSM_CH_BLOCK)),
        pl.BlockSpec((1, SSM_CH_BLOCK), lambda j: (0, j)),
    ]


def _chunks(t):
    rows = min(_S5_ROWS, t)
    return rows, lambda i: pl.ds(pl.multiple_of(i * rows, rows), rows)


def _s5_states(u_ref, us_ref, bre_ref, bim_ref, lr_ref, li_ref, sr_ref, si_ref, pr_ref, pi_ref, t):
    seg = t // SUBLANES
    rows, chunk = _chunks(t)
    for c in range(t // rows):
        us_ref[c * rows:(c + 1) * rows, :] = _load_segmented(u_ref, c * rows // SUBLANES, rows // SUBLANES, seg)

    def fill(i, _):
        ub = us_ref[chunk(i), :].astype(BF16)
        sr_ref[chunk(i), :] = _dot(ub, bre_ref[...], 1, 0)
        si_ref[chunk(i), :] = _dot(ub, bim_ref[...], 1, 0)
        return 0

    lax.fori_loop(0, t // rows, fill, 0)
    _fill_powers(lr_ref[...], li_ref[...], pr_ref, pi_ref, seg)
    _scan_segments(sr_ref, si_ref, pr_ref, pi_ref, lr_ref[...], li_ref[...], seg, False)


def _s5_scratch(t):
    state = pltpu.VMEM((t, SSM_ST_BLOCK), F32)
    powers = pltpu.VMEM((t // SUBLANES, SSM_ST_BLOCK), F32)
    return state, powers, pltpu.VMEM((t, SSM_CH_BLOCK), F32)


def _s5_fwd(proj, mats, dskip_row, d_attn, d_ssm):
    t = proj.shape[0]
    seg = t // SUBLANES
    n_blocks = d_ssm // SSM_CH_BLOCK
    rows, chunk = _chunks(t)

    def body(u_ref, bre_ref, bim_ref, lr_ref, li_ref, cre_ref, cim_ref, d_ref, y_ref,
             sr_ref, si_ref, pr_ref, pi_ref, us_ref, ys_ref):
        _s5_states(u_ref, us_ref, bre_ref, bim_ref, lr_ref, li_ref, sr_ref, si_ref, pr_ref, pi_ref, t)

        def emit(i, _):
            ys_ref[chunk(i), :] = (_dot(sr_ref[chunk(i), :].astype(BF16), cre_ref[...], 1, 0)
                                   - _dot(si_ref[chunk(i), :].astype(BF16), cim_ref[...], 1, 0)
                                   + d_ref[...] * us_ref[chunk(i), :])
            return 0

        lax.fori_loop(0, t // rows, emit, 0)
        for c in range(t // rows):
            _store_segmented(y_ref, c * rows // SUBLANES, seg, ys_ref[c * rows:(c + 1) * rows, :])

    state, powers, channels = _s5_scratch(t)
    col = pl.BlockSpec((t, SSM_CH_BLOCK), lambda j: (0, j))
    return _call(body, name="s5_fwd", grid=(n_blocks,), in_specs=_s5_in_specs(t, d_attn), out_specs=col,
                 out_shape=_sds((t, d_ssm), F32), scratch_shapes=[state, state, powers, powers, channels, channels],
                 semantics=("parallel",))(proj, *mats, dskip_row)


def _s5_bwd(proj, mats, dskip_row, y, dz_a, dz_b, d_attn, d_ssm, after):
    t = proj.shape[0]
    seg = t // SUBLANES
    n_blocks = d_ssm // SSM_CH_BLOCK
    rows, chunk = _chunks(t)

    def body(u_ref, bre_ref, bim_ref, lr_ref, li_ref, cre_ref, cim_ref, d_ref, y_ref, dza_ref, dzb_ref,
             du_ref, dbre_ref, dbim_ref, dlr_ref, dli_ref, dcre_ref, dcim_ref, dd_ref,
             sr_ref, si_ref, gr_ref, gi_ref, pr_ref, pi_ref, us_ref, dys_ref, dus_ref, acc_r, acc_i):
        _s5_states(u_ref, us_ref, bre_ref, bim_ref, lr_ref, li_ref, sr_ref, si_ref, pr_ref, pi_ref, t)
        for ref in (dcre_ref, dcim_ref, dbre_ref, dbim_ref, dd_ref, acc_r, acc_i):
            ref[...] = jnp.zeros_like(ref)
        for c in range(t // rows):
            tile0, n_tiles = c * rows // SUBLANES, rows // SUBLANES
            dz = _load_segmented(dza_ref, tile0, n_tiles, seg) + _load_segmented(dzb_ref, tile0, n_tiles, seg)
            dys_ref[c * rows:(c + 1) * rows, :] = dz * _gelu_grad(_load_segmented(y_ref, tile0, n_tiles, seg))

        def through_c(i, _):
            dy = dys_ref[chunk(i), :]
            dd_ref[...] += jnp.sum(dy * us_ref[chunk(i), :], axis=0, keepdims=True)
            dyb = dy.astype(BF16)
            gr_ref[chunk(i), :] = _dot(dyb, cre_ref[...], 1, 1)
            gi_ref[chunk(i), :] = -_dot(dyb, cim_ref[...], 1, 1)
            dcre_ref[...] += _dot(sr_ref[chunk(i), :].astype(BF16), dyb, 0, 0)
            dcim_ref[...] -= _dot(si_ref[chunk(i), :].astype(BF16), dyb, 0, 0)
            return 0

        lax.fori_loop(0, t // rows, through_c, 0)

        row = lax.broadcasted_iota(jnp.int32, (SUBLANES, SSM_ST_BLOCK), 0)
        last = pl.ds((seg - 1) * SUBLANES, SUBLANES)
        wrap = [jnp.where(row == 0, 0.0, pltpu.roll(ref[last, :], 1, 0)) for ref in (sr_ref, si_ref)]

        def lambda_grad(j, g_re, g_im):
            before = pl.ds(pl.multiple_of(jnp.maximum(j - 1, 0) * SUBLANES, SUBLANES), SUBLANES)
            prev_r = jnp.where(j > 0, sr_ref[before, :], wrap[0])
            prev_i = jnp.where(j > 0, si_ref[before, :], wrap[1])
            acc_r[...] += g_re * prev_r + g_im * prev_i
            acc_i[...] += g_im * prev_r - g_re * prev_i

        _scan_segments(gr_ref, gi_ref, pr_ref, pi_ref, lr_ref[...], li_ref[...], seg, True, per_tile=lambda_grad)
        dlr_ref[...] = jnp.sum(acc_r[...], axis=0, keepdims=True)
        dli_ref[...] = jnp.sum(acc_i[...], axis=0, keepdims=True)

        def through_b(i, _):
            ub = us_ref[chunk(i), :].astype(BF16)
            grb, gib = gr_ref[chunk(i), :].astype(BF16), gi_ref[chunk(i), :].astype(BF16)
            dbre_ref[...] += _dot(ub, grb, 0, 0)
            dbim_ref[...] += _dot(ub, gib, 0, 0)
            dus_ref[chunk(i), :] = (_dot(grb, bre_ref[...], 1, 1) + _dot(gib, bim_ref[...], 1, 1)
                                    + d_ref[...] * dys_ref[chunk(i), :])
            return 0

        lax.fori_loop(0, t // rows, through_b, 0)
        for c in range(t // rows):
            _store_segmented(du_ref, c * rows // SUBLANES, seg, dus_ref[c * rows:(c + 1) * rows, :])

    col = pl.BlockSpec((t, SSM_CH_BLOCK), lambda j: (0, j))
    blk3 = lambda shape: pl.BlockSpec((None,) + shape, lambda j: (j, 0, 0))
    state, powers, channels = _s5_scratch(t)
    return _call(
        body, name="s5_bwd", grid=(n_blocks,), in_specs=_s5_in_specs(t, d_attn) + [col, col, col],
        out_specs=[col, blk3((SSM_CH_BLOCK, SSM_ST_BLOCK)), blk3((SSM_CH_BLOCK, SSM_ST_BLOCK)),
                   blk3((1, SSM_ST_BLOCK)), blk3((1, SSM_ST_BLOCK)),
                   blk3((SSM_ST_BLOCK, SSM_CH_BLOCK)), blk3((SSM_ST_BLOCK, SSM_CH_BLOCK)),
                   pl.BlockSpec((1, SSM_CH_BLOCK), lambda j: (0, j))],
        out_shape=[_sds((t, d_ssm), F32),
                   _sds((n_blocks, SSM_CH_BLOCK, SSM_ST_BLOCK), F32), _sds((n_blocks, SSM_CH_BLOCK, SSM_ST_BLOCK), F32),
                   _sds((n_blocks, 1, SSM_ST_BLOCK), F32), _sds((n_blocks, 1, SSM_ST_BLOCK), F32),
                   _sds((n_blocks, SSM_ST_BLOCK, SSM_CH_BLOCK), F32), _sds((n_blocks, SSM_ST_BLOCK, SSM_CH_BLOCK), F32),
                   _sds((1, d_ssm), F32)],
        scratch_shapes=[state, state, state, state, powers, powers, channels, channels, channels,
                        pltpu.VMEM((SUBLANES, SSM_ST_BLOCK), F32), pltpu.VMEM((SUBLANES, SSM_ST_BLOCK), F32)],
        semantics=("parallel",), n_after=len(after))(proj, *mats, dskip_row, y, dz_a, dz_b, *after)


def _by_block(gp_n):
    return gp_n.reshape(-1, GROUPS_PER_BLOCK, SSM_GROUP, SSM_STATE)


def _block_diag_in(bbar):
    eye = jnp.eye(GROUPS_PER_BLOCK, dtype=F32)
    return jnp.einsum("jgpn,gh->jgphn", _by_block(bbar), eye).reshape(-1, SSM_CH_BLOCK, SSM_ST_BLOCK)


def _block_diag_in_t(dense):
    d5 = dense.reshape(-1, GROUPS_PER_BLOCK, SSM_GROUP, GROUPS_PER_BLOCK, SSM_STATE)
    eye = jnp.eye(GROUPS_PER_BLOCK, dtype=F32)
    return jnp.einsum("jgphn,gh->jgpn", d5, eye).reshape(-1, SSM_STATE)


def _block_diag_out(c):
    eye = jnp.eye(GROUPS_PER_BLOCK, dtype=F32)
    return jnp.einsum("jgpn,gh->jgnhp", _by_block(c), eye).reshape(-1, SSM_ST_BLOCK, SSM_CH_BLOCK)


def _block_diag_out_t(dense):
    d5 = dense.reshape(-1, GROUPS_PER_BLOCK, SSM_STATE, GROUPS_PER_BLOCK, SSM_GROUP)
    eye = jnp.eye(GROUPS_PER_BLOCK, dtype=F32)
    return jnp.einsum("jgnhp,gh->jgpn", d5, eye).reshape(-1, SSM_STATE)


def _adamw(w, g, m, v):
    m = ADAM_B1 * m + (1.0 - ADAM_B1) * g
    v = ADAM_B2 * v + (1.0 - ADAM_B2) * (g * g)
    m_hat = m / (1.0 - ADAM_B1 ** ADAM_STEP)
    v_hat = v / (1.0 - ADAM_B2 ** ADAM_STEP)
    delta = -ADAM_LR * (m_hat / (jnp.sqrt(v_hat) + ADAM_EPS) + ADAM_WD * w)
    return delta, m, v


def _adam_sharded(name, parts, w, m, v, tr, row0=0):
    r, c = w.shape
    assert r % tr == 0 and row0 % tr == 0, (name, r, tr, row0)

    def body(p_ref, w_ref, m_ref, v_ref, g_out, d_out, m_out, v_out):
        g = p_ref[0].astype(F32)
        for i in range(1, p_ref.shape[0]):
            g = g + p_ref[i].astype(F32)
        delta, m_new, v_new = _adamw(w_ref[...], g, m_ref[...], v_ref[...])
        g_out[...] = g
        d_out[...] = delta
        m_out[...] = m_new
        v_out[...] = v_new

    tile = pl.BlockSpec((tr, c), lambda i: (i, 0))
    return _call(body, name=name, grid=(r // tr,),
                 in_specs=[pl.BlockSpec((parts.shape[0], tr, c), lambda i: (0, i + row0 // tr, 0)), tile, tile, tile],
                 out_specs=[tile] * 4, out_shape=[_sds((r, c), F32)] * 4, semantics=("parallel",))(parts, w, m, v)


_BIG = ("w_in", "w_glu", "w_o", "w_gate", "w_up", "w_down")
_BY_COLUMNS = ("w_in", "w_gate", "w_up")
_SMALL_VECTORS = ("sinks", "log_dt", "b_glu", "g_attn_out", "g_ssm_out", "g_post_mix", "g_pre_ffn", "g_post_ffn")
_SMALL_MATRICES = ("b_re", "b_im", "c_re", "c_im", "a_re", "a_im")
_ORDER = ("g_pre_mix", "w_in", "sinks", "a_re", "a_im", "log_dt", "b_re", "b_im", "c_re", "c_im", "d_skip", "w_glu",
          "b_glu", "g_attn_out", "g_ssm_out", "w_o", "g_post_mix", "g_pre_ffn", "w_gate", "w_up", "w_down",
          "g_post_ffn")


def _pack_grads(vectors, matrices):
    width = max(a.shape[1] for a in vectors)
    slots, row, lane = [], 0, 0
    for a in vectors:
        span = -(-a.shape[1] // LANES) * LANES
        if lane + span > width:
            row, lane = row + 1, 0
        slots.append((row, lane, a.shape[1]))
        lane += span
    firsts, at = [], 0
    for a in matrices:
        firsts.append(at)
        at += a.shape[0]
    nv = len(vectors)

    def body(*refs):
        vec_out, mat_out = refs[-2], refs[-1]
        vec_out[...] = jnp.zeros_like(vec_out)
        for ref, (r, l, w) in zip(refs[:nv], slots):
            vec_out[r:r + 1, l:l + w] = ref[...]
        for ref, r0 in zip(refs[nv:-2], firsts):
            mat_out[r0:r0 + ref.shape[0], :] = ref[...]

    ins = list(vectors) + list(matrices)
    outs = [_sds((-(-(row + 1) // SUBLANES) * SUBLANES, width), F32), _sds((at, matrices[0].shape[1]), F32)]
    vec_pack, mat_pack = _call(body, name="pack_small_grads", in_specs=_whole(ins), out_specs=_whole(outs),
                               out_shape=outs)(*ins)
    return vec_pack, slots, mat_pack, firsts


def _adam_replicated(sources, found_at, w, m, v, total_at):
    ns, n = len(sources), len(w)

    def body(*refs):
        ins, outs = refs[ns:ns + 3 * n], refs[ns + 3 * n:]
        summed = []
        for p_ref in refs[:ns]:
            g = p_ref[0]
            for k in range(1, N_DEV):
                g = g + p_ref[k]
            summed.append(g)
        for i, (src, row, lane) in enumerate(found_at):
            w_ref, m_ref, v_ref = ins[i], ins[n + i], ins[2 * n + i]
            rows, cols = w_ref.shape
            g = summed[src][row:row + rows, lane:lane + cols]
            delta, m_new, v_new = _adamw(w_ref[...], g, m_ref[...], v_ref[...])
            for o, val in zip(outs[4 * i:4 * i + 4], (g, delta, m_new, v_new)):
                o[...] = val
        t_src, t_row, t_lane, t_width = total_at
        outs[-1][...] = summed[t_src][t_row:t_row + 1, t_lane:t_lane + t_width]

    ins = list(sources) + list(w) + list(m) + list(v)
    outs = [_sds(a.shape, F32) for a in w for _ in range(4)] + [_sds((1, total_at[3]), F32)]
    flat = _call(body, name="adam_replicated", in_specs=_whole(ins), out_specs=_whole(outs), out_shape=outs)(*ins)
    return [tuple(flat[4 * i:4 * i + 4]) for i in range(n)], flat[-1]


def kernel(x, positions, g_pre_mix, w_in, sinks, a_re, a_im, log_dt, b_re, b_im, c_re, c_im, d_skip, w_glu, b_glu, g_attn_out, g_ssm_out, w_o, g_post_mix, g_pre_ffn, w_gate, w_up, w_down, g_post_ffn, loss_target, m_g_pre_mix, m_w_in, m_sinks, m_a_re, m_a_im, m_log_dt, m_b_re, m_b_im, m_c_re, m_c_im, m_d_skip, m_w_glu, m_b_glu, m_g_attn_out, m_g_ssm_out, m_w_o, m_g_post_mix, m_g_pre_ffn, m_w_gate, m_w_up, m_w_down, m_g_post_ffn, v_g_pre_mix, v_w_in, v_sinks, v_a_re, v_a_im, v_log_dt, v_b_re, v_b_im, v_c_re, v_c_im, v_d_skip, v_w_glu, v_b_glu, v_g_attn_out, v_g_ssm_out, v_w_o, v_g_post_mix, v_g_pre_ffn, v_w_gate, v_w_up, v_w_down, v_g_post_ffn):
    given = dict(locals())
    weights = {n: given[n] for n in _ORDER}
    mom_m = {n: given["m_" + n] for n in _ORDER}
    mom_v = {n: given["v_" + n] for n in _ORDER}

    t, d = x.shape[1], x.shape[2]
    d_attn = d // 2
    d_ssm = d - d_attn
    d_in = d_attn + 2 * D_KV + d_ssm
    n_groups = d_ssm // SSM_GROUP
    n_heads = d_attn // HEAD_DIM
    tm = min(256, t)

    x2 = x[0]
    target = loss_target[0]

    def by_rows(n, a):
        return a[0].T if n in _BY_COLUMNS else a[0]

    def start_gather(name, ns, token):
        behind = 0 if token is None else token[0, 0].astype(BF16)
        shards = [by_rows(n, weights[n]).astype(BF16) + behind for n in ns]
        return _exchange_start(name, shards, False, (OWN, SIBLING) + CHIP_PEERS)

    def forward_gather(handle, after):
        return _forward_start(handle["name"] + "_forward", _exchange_wait(handle, after))

    def finish_gather(handle, after):
        return _split_wait(forward_gather(handle, after)[0], [])

    ag_in, token = start_gather("gather_w_in", ["w_in"], None)
    ag_mix, token = start_gather("gather_w_glu_o", ["w_glu", "w_o"], token)
    ag_ffn_in, token = start_gather("gather_w_gate_up", ["w_gate", "w_up"], token)
    ag_down, token = start_gather("gather_w_down", ["w_down"], token)

    xn, = _rows("norm_in", lambda xv, g: ([_rms(xv)[0] * g], []), [x2], [g_pre_mix], [(d, BF16)], [], tm,
                after=[token])
    win_g, = finish_gather(ag_in, [xn])
    w_in_t = win_g.reshape(d_in, d)
    proj = _mm_nt("proj_in", xn, w_in_t, F32)

    cos, sin = _rope_tables(positions.reshape(t, 1).astype(F32))
    sinks_row = jnp.pad(sinks, ((0, 0), (0, LANES - n_heads)))
    attn = _attention_fwd(proj, cos, sin, sinks_row, d_attn)

    def view(n, a):
        if n in ("b_re", "b_im"):
            return jnp.transpose(a[0], (0, 2, 1)).reshape(-1, SSM_STATE)
        if n in ("c_re", "c_im"):
            return a[0].reshape(-1, SSM_STATE)
        return a[0].T if n == "d_skip" else a[0] if a.ndim == 3 else a

    def unview(n, val):
        if n in ("b_re", "b_im"):
            return jnp.transpose(val.reshape(n_groups, SSM_GROUP, SSM_STATE), (0, 2, 1))[None]
        if n in ("c_re", "c_im"):
            return val.reshape(1, n_groups, SSM_GROUP, SSM_STATE)
        return val.T[None] if n == "d_skip" else val[None] if weights[n].ndim == 3 else val

    b_re_v, b_im_v = view("b_re", b_re), view("b_im", b_im)
    ldt_col = log_dt.reshape(n_groups, 1)
    lam_re, lam_im, bbar_re, bbar_im = _s5_discretise(a_re[0], a_im[0], ldt_col, b_re_v, b_im_v)
    n_blocks = n_groups // GROUPS_PER_BLOCK
    mats = [_block_diag_in(bbar_re).astype(BF16), _block_diag_in(bbar_im).astype(BF16),
            lam_re.reshape(n_blocks, 1, SSM_ST_BLOCK), lam_im.reshape(n_blocks, 1, SSM_ST_BLOCK),
            _block_diag_out(view("c_re", c_re)).astype(BF16), _block_diag_out(view("c_im", c_im)).astype(BF16)]
    dskip_row = d_skip.reshape(1, d_ssm)
    forward_mix, _ = forward_gather(ag_mix, [attn])
    y_ssm = _s5_fwd(proj, mats, dskip_row, d_attn, d_ssm)
    gelu_bf16 = lambda yv: _gelu(yv).astype(BF16)
    wglu_g, wo_g = _split_wait(forward_mix, [y_ssm])
    w_glu_full = wglu_g.reshape(d_ssm, d_ssm)
    w_o_full = wo_g.reshape(d, d)
    glu_lin = _mm_nn("glu_gate", y_ssm, w_glu_full, F32, a_fn=gelu_bf16)

    def mix_prep(av, yv, gl, bg, ga, gs):
        ssm = _gelu(yv) * _sigmoid(gl + bg)
        return [jnp.concatenate([_rms(av)[0] * ga, _rms(ssm)[0] * gs], axis=1)], []

    mixed, = _rows("mix_prep", mix_prep, [attn, y_ssm, glu_lin], [b_glu, g_attn_out, g_ssm_out], [(d, BF16)], [], tm)
    mix = _mm_nn("mix_out", mixed, w_o_full, F32)

    def post_mix(xv, mv, gpm, gpf):
        h = xv + _rms(mv)[0] * gpm
        return [h, _rms(h)[0] * gpf], []

    forward_ffn_in, token = forward_gather(ag_ffn_in, [mix])
    h, hn = _rows("post_mix", post_mix, [x2, mix], [g_post_mix, g_pre_ffn], [(d, F32), (d, BF16)], [], tm,
                  after=[token])
    wgate_g, wup_g = _split_wait(forward_ffn_in, [hn])
    gate, up, hid = _ffn_in(hn, wgate_g, wup_g)
    wdown_g, = finish_gather(ag_down, [hid])
    ff = _mm_contract_slots("ffn_down", [(hid, wdown_g)], F32, per_step=2, tm=1024)

    def head(hv, fv, tv, gpo):
        out = hv + _rms(fv)[0] * gpo
        err = out - tv
        dout = err * (1.0 / d)
        dff, dg = _rms_bwd(fv, gpo, dout)
        loss = jnp.zeros((1, LANES), F32) + 0.5 * jnp.sum(err * err) * (1.0 / d)
        return [dff, dout], [dg, loss]

    dff, dh_out, dg_post_ffn, loss_row = _rows("loss_head", head, [h, ff, target], [g_post_ffn],
                                               [(d, BF16), (d, F32)], [d, LANES], tm)

    def swap_halves(name, grads):
        return _halves_start("swap_" + name, [g.reshape(N_DEV // 2, 2, *g.shape[1:]) for g in grads])

    def scatter_chip_sums(name, swap, after):
        both = _split_wait(swap, after)
        half = len(both) // 2
        sums = [_chip_sum("chip_sum_%s_%d" % (name, i), both[i], both[half + i]) for i in range(half)]
        return _exchange_start("scatter_" + name, sums, True, (OWN,) + CHIP_PEERS, by_chip=True)

    dw_down = _mm_slots_tn("ffn_down_dw", hid, dff, BF16)
    swap_down, token = swap_halves("dw_down", [dw_down])
    dgate, dup = _ffn_down_bwd(dff, wdown_g, gate, up, [token])
    rs_down, token = scatter_chip_sums("dw_down", swap_down, [dgate])
    dhn = _mm_contract_slots("ffn_in_dx", [(dgate, wgate_g), (dup, wup_g)], F32, per_step=2, tm=1024, tn=1024,
                             after=[token])
    dw_gate = _mm_slots_tn("ffn_gate_dw", dgate, hn, BF16)
    dw_up = _mm_slots_tn("ffn_up_dw", dup, hn, BF16)
    swap_ffn_in, tok_ffn_in = swap_halves("dw_gate_up", [dw_gate, dw_up])

    def mid_bwd(dho, dhn_, hv, mv, gpf, gpm):
        d1, dgpf = _rms_bwd(hv, gpf, dhn_)
        dh_ = dho + d1
        dmix_, dgpm = _rms_bwd(mv, gpm, dh_)
        return [dh_, dmix_], [dgpf, dgpm]

    dh, dmix, dg_pre_ffn, dg_post_mix = _rows("mid_bwd", mid_bwd, [dh_out, dhn, h, mix], [g_pre_ffn, g_post_mix],
                                              [(d, F32), (d, BF16)], [d, d], tm, after=[tok_ffn_in])

    dmixed = _mm_nt("mix_out_dx", dmix, w_o_full, F32)
    rs_ffn_in, token = scatter_chip_sums("dw_gate_up", swap_ffn_in, [dmixed])
    dw_o = _mm_tn("mix_out_dw", mixed, dmix, BF16, after=[token])
    swap_o, tok_o = swap_halves("dw_o", [dw_o.reshape(N_DEV, d // N_DEV, d)])

    def mix_bwd(dm, av, yv, gl, bg, ga, gs):
        dattn_, dga = _rms_bwd(av, ga, dm[:, :d_attn])
        z = _gelu(yv)
        sg = _sigmoid(gl + bg)
        dssm, dgs = _rms_bwd(z * sg, gs, dm[:, d_attn:])
        dgl = dssm * z * sg * (1.0 - sg)
        return [dattn_, dssm * sg, dgl], [dga, dgs, jnp.sum(dgl, axis=0, keepdims=True)]

    dattn, dz_direct, dglu, dg_attn_out, dg_ssm_out, db_glu = _rows(
        "mix_bwd", mix_bwd, [dmixed, attn, y_ssm, glu_lin], [b_glu, g_attn_out, g_ssm_out],
        [(d_attn, F32), (d_ssm, F32), (d_ssm, BF16)], [d_attn, d_ssm, d_ssm], tm, after=[tok_o])
    dz_glu = _mm_nt("glu_gate_dx", dglu, w_glu_full, F32)
    dw_glu = _mm_tn("glu_gate_dw", y_ssm, dglu, BF16, a_fn=gelu_bf16)
    rs_o, token = scatter_chip_sums("dw_o", swap_o, [dz_glu, dw_glu])

    du, db_re_dense, db_im_dense, dlam_re, dlam_im, dc_re_dense, dc_im_dense, dd_skip = _s5_bwd(
        proj, mats, dskip_row, y_ssm, dz_direct, dz_glu, d_attn, d_ssm, [token])
    da_re, da_im, dlog_dt, db_re_v, db_im_v = _s5_discretise_bwd(
        a_re[0], a_im[0], ldt_col, b_re_v, b_im_v, dlam_re.reshape(n_groups, SSM_STATE),
        dlam_im.reshape(n_groups, SSM_STATE), _block_diag_in_t(db_re_dense), _block_diag_in_t(db_im_dense))
    dq, dk2, dv2, dsinks_row = _attention_bwd(proj, cos, sin, sinks_row, dattn, d_attn)

    small_grads = {
        "sinks": dsinks_row, "a_re": da_re, "a_im": da_im, "log_dt": dlog_dt.reshape(1, n_groups),
        "b_re": db_re_v, "b_im": db_im_v, "c_re": _block_diag_out_t(dc_re_dense),
        "c_im": _block_diag_out_t(dc_im_dense), "d_skip": dd_skip.reshape(n_groups, SSM_GROUP).T, "b_glu": db_glu,
        "g_attn_out": dg_attn_out, "g_ssm_out": dg_ssm_out, "g_post_mix": dg_post_mix, "g_pre_ffn": dg_pre_ffn,
        "g_post_ffn": dg_post_ffn,
    }
    vec_pack, vec_slots, mat_pack, mat_rows = _pack_grads([small_grads[n] for n in _SMALL_VECTORS] + [loss_row],
                                                          [small_grads[n] for n in _SMALL_MATRICES])
    ag_small, token = _exchange_start("gather_small_grads", [vec_pack, mat_pack, small_grads["d_skip"]], False,
                                      (OWN,) + ALL_PEERS)
    dproj = _assemble_dproj(dq, dk2, dv2, du, d_in, [token])

    dw_in = _mm_tn("proj_in_dw", dproj, xn, BF16).reshape(N_DEV, d_in // N_DEV, d)
    swap_in, token = swap_halves("dw_in_glu", [dw_in, dw_glu.reshape(N_DEV, d_ssm // N_DEV, d_ssm)])
    dxn = _mm_nn("proj_in_dx", dproj, w_in_t, F32, after=[token])
    rs_in, token = scatter_chip_sums("dw_in_glu", swap_in, [dxn])

    def x_bwd(dh_, dxn_, xv, g):
        dx, dg = _rms_bwd(xv, g, dxn_)
        return [dh_ + dx], [dg]

    grad_x, dg_pre_mix = _rows("norm_in_bwd", x_bwd, [dh, dxn, x2], [g_pre_mix], [(d, F32)], [d], tm, after=[token])
    ag_last, token = _exchange_start("gather_g_pre_mix_grad", [dg_pre_mix], False, (OWN,) + ALL_PEERS)

    results = {}

    def adam_big(n, parts):
        r = parts.shape[1]
        tr = next((c for c in range(192, 15, -16) if r % c == 0), r)
        results[n] = _adam_sharded("adam_" + n, parts, by_rows(n, weights[n]), by_rows(n, mom_m[n]),
                                   by_rows(n, mom_v[n]), tr)
        return results[n][3]

    done = [grad_x, token]
    adam_big("w_down", _exchange_wait(rs_down, done)[0])
    p_gate, p_up = _exchange_wait(rs_ffn_in, done)
    done = [adam_big("w_gate", p_gate), adam_big("w_up", p_up), results["w_down"][3]]
    done = [adam_big("w_o", _exchange_wait(rs_o, done)[0])]
    vec_parts, mat_parts, dskip_parts = _exchange_wait(ag_small, done)
    for n, row0 in zip(_SMALL_MATRICES, mat_rows):
        rows = view(n, weights[n]).shape[0]
        results[n] = _adam_sharded("adam_" + n, mat_parts, view(n, weights[n]), view(n, mom_m[n]), view(n, mom_v[n]),
                                   rows, row0)
    p_in, p_glu = _exchange_wait(rs_in, [results[n][3] for n in _SMALL_MATRICES])
    done = [adam_big("w_in", p_in), adam_big("w_glu", p_glu)]
    first_gain_parts, = _exchange_wait(ag_last, done)
    rest = _SMALL_VECTORS + ("d_skip", "g_pre_mix")
    found_at = [(0, row, lane) for row, lane, _ in vec_slots[:-1]] + [(1, 0, 0), (2, 0, 0)]
    updated, loss_sum = _adam_replicated([vec_parts, dskip_parts, first_gain_parts], found_at,
                                         [view(n, weights[n]) for n in rest], [view(n, mom_m[n]) for n in rest],
                                         [view(n, mom_v[n]) for n in rest], (0,) + vec_slots[-1])
    results.update(zip(rest, updated))

    outs = [loss_sum[0, 0], grad_x[None]]
    for k in range(4):
        for n in _ORDER:
            val = results[n][k]
            outs.append(val.T[None] if n in _BY_COLUMNS else val[None] if n in _BIG else unview(n, val))
    return tuple(outs)
```

```python
import math

import jax
import jax.numpy as jnp
from jax import lax
from jax.experimental import pallas as pl
from jax.experimental.pallas import tpu as pltpu

F32 = jnp.float32
BF16 = jnp.bfloat16

HEAD_DIM = 64
N_KV_HEADS = 4
D_KV = N_KV_HEADS * HEAD_DIM
WINDOW = 128
BLOCK = 128
ROPE_THETA = 10000.0
SSM_GROUP = 16
SSM_STATE = 64
GROUPS_PER_BLOCK = 8
SSM_CH_BLOCK = GROUPS_PER_BLOCK * SSM_GROUP
SSM_ST_BLOCK = GROUPS_PER_BLOCK * SSM_STATE
RMS_EPS = 1e-6
N_DEV = 8
LANES = 128
SUBLANES = 8
MASKED = -1e30

ADAM_LR = 0.001
ADAM_B1 = 0.9
ADAM_B2 = 0.999
ADAM_EPS = 1e-08
ADAM_WD = 0.01
ADAM_STEP = 10

VMEM_LIMIT_BYTES = 56 * 1024 * 1024


def _call(body, *, name, out_shape, in_specs, out_specs, grid=(), scratch_shapes=(), semantics=None, n_after=0):
    params = dict(vmem_limit_bytes=VMEM_LIMIT_BYTES)
    if semantics is not None:
        params["dimension_semantics"] = semantics
    n_in = len(in_specs)
    if n_after:
        inner = body

        def body(*refs):
            inner(*refs[:n_in], *refs[n_in + n_after:])

        in_specs = list(in_specs) + [pl.BlockSpec(memory_space=pl.ANY)] * n_after
    return pl.pallas_call(body, name=name, grid=grid, in_specs=in_specs, out_specs=out_specs, out_shape=out_shape,
                          scratch_shapes=scratch_shapes, compiler_params=pltpu.CompilerParams(**params))


def _sds(shape, dtype):
    return jax.ShapeDtypeStruct(tuple(shape), dtype)


def _dot(a, b, ca, cb):
    return lax.dot_general(a, b, (((ca,), (cb,)), ((), ())), preferred_element_type=F32)


def _rms(x):
    r = lax.rsqrt(jnp.mean(x * x, axis=-1, keepdims=True) + RMS_EPS)
    return x * r, r


def _rms_bwd(x, g, dy):
    xh, r = _rms(x)
    dxh = dy * g
    dx = r * (dxh - xh * jnp.mean(dxh * xh, axis=-1, keepdims=True))
    return dx, jnp.sum(dy * xh, axis=0, keepdims=True)


def _sigmoid(x):
    return 1.0 / (1.0 + jnp.exp(-x))


_GELU_C = math.sqrt(2.0 / math.pi)
_GELU_A = 0.044715


def _gelu(y):
    t = jnp.tanh(_GELU_C * (y + _GELU_A * y * y * y))
    return 0.5 * y * (1.0 + t)


def _gelu_grad(y):
    t = jnp.tanh(_GELU_C * (y + _GELU_A * y * y * y))
    return 0.5 * (1.0 + t) + 0.5 * y * (1.0 - t * t) * _GELU_C * (1.0 + 3.0 * _GELU_A * y * y)


def _rows(name, fn, row_ins, vec_ins, row_outs, acc_widths, tm, after=()):
    rows = row_ins[0].shape[0]
    assert rows % tm == 0, (name, rows, tm)
    n_row, n_vec, n_out, n_acc = len(row_ins), len(vec_ins), len(row_outs), len(acc_widths)

    def body(*refs):
        ins = [r[...] for r in refs[:n_row + n_vec]]
        outs = refs[n_row + n_vec:n_row + n_vec + n_out]
        accs = refs[n_row + n_vec + n_out:]
        row_vals, acc_vals = fn(*ins)
        for o, v in zip(outs, row_vals):
            o[...] = v.astype(o.dtype)
        if n_acc:
            @pl.when(pl.program_id(0) == 0)
            def _():
                for a in accs:
                    a[...] = jnp.zeros_like(a)
            for a, v in zip(accs, acc_vals):
                a[...] += v

    in_specs = [pl.BlockSpec((tm, a.shape[1]), lambda i: (i, 0)) for a in row_ins]
    in_specs += [pl.BlockSpec(v.shape, lambda i: (0, 0)) for v in vec_ins]
    out_specs = [pl.BlockSpec((tm, w), lambda i: (i, 0)) for w, _ in row_outs]
    out_specs += [pl.BlockSpec((1, w), lambda i: (0, 0)) for w in acc_widths]
    out_shape = [_sds((rows, w), dt) for w, dt in row_outs] + [_sds((1, w), F32) for w in acc_widths]
    return _call(body, name=name, grid=(rows // tm,), in_specs=in_specs, out_specs=out_specs, out_shape=out_shape,
                 semantics=("arbitrary",) if n_acc else ("parallel",), n_after=len(after))(*row_ins, *vec_ins, *after)


def _matmul(name, operands, in_specs, product, grid, out_shape, out_spec, acc_shape, after=()):
    nk = grid[-1]
    n_in = len(operands)
    in_place = out_shape.dtype == F32

    def body(*refs):
        ins = [r[...] for r in refs[:n_in]]
        o_ref = refs[n_in]
        if nk == 1:
            o_ref[...] = product(*ins).astype(o_ref.dtype)
            return
        acc = o_ref if in_place else refs[n_in + 1]
        k = pl.program_id(len(grid) - 1)

        @pl.when(k == 0)
        def _():
            acc[...] = jnp.zeros_like(acc)

        acc[...] += product(*ins)

        if not in_place:
            @pl.when(k == nk - 1)
            def _():
                o_ref[...] = acc[...].astype(o_ref.dtype)

    return _call(body, name=name, grid=grid, in_specs=in_specs, out_specs=out_spec, out_shape=out_shape,
                 scratch_shapes=[] if nk == 1 or in_place else [pltpu.VMEM(acc_shape, F32)],
                 semantics=("parallel",) * (len(grid) - 1) + ("arbitrary",), n_after=len(after))(*operands, *after)


def _mm_nn(name, a, b, out_dtype, tm=512, tn=None, a_fn=lambda x: x, after=(), plus=None):
    m, k = a.shape
    n = b.shape[1]
    tm, tn = min(tm, m), n if tn is None else min(tn, n)
    operands = [a, b] + ([] if plus is None else [plus])
    specs = [pl.BlockSpec((tm, k), lambda i, j, s: (i, 0)), pl.BlockSpec((k, tn), lambda i, j, s: (0, j))]
    specs += [] if plus is None else [pl.BlockSpec((tm, tn), lambda i, j, s: (i, j))]
    return _matmul(name, operands, specs, lambda x, y, *p: _dot(a_fn(x), y, 1, 0) + (p[0] if p else 0.0),
                   (m // tm, n // tn, 1), _sds((m, n), out_dtype),
                   pl.BlockSpec((tm, tn), lambda i, j, s: (i, j)), (tm, tn), after)


def _mm_nt(name, a, b, out_dtype, tm=512, tn=None):
    m, k = a.shape
    n = b.shape[0]
    tm, tn = min(tm, m), n if tn is None else tn
    return _matmul(name, [a, b],
                   [pl.BlockSpec((tm, k), lambda i, j, s: (i, 0)), pl.BlockSpec((tn, k), lambda i, j, s: (j, 0))],
                   lambda x, y: _dot(x, y, 1, 1), (m // tm, n // tn, 1), _sds((m, n), out_dtype),
                   pl.BlockSpec((tm, tn), lambda i, j, s: (i, j)), (tm, tn))


def _mm_tn(name, a, b, out_dtype, tm=512, tn=None, tk=2048, a_fn=lambda x: x, after=()):
    k, m = a.shape
    n = b.shape[1]
    tm, tk, tn = min(tm, m), min(tk, k), n if tn is None else tn
    return _matmul(name, [a, b],
                   [pl.BlockSpec((tk, tm), lambda i, j, s: (s, i)), pl.BlockSpec((tk, tn), lambda i, j, s: (s, j))],
                   lambda x, y: _dot(a_fn(x), y, 0, 0), (m // tm, n // tn, k // tk), _sds((m, n), out_dtype),
                   pl.BlockSpec((tm, tn), lambda i, j, s: (i, j)), (tm, tn), after)


def _hidden_tile(f):
    return 512 if f % 512 == 0 else 256


def _ffn_in(a, w_gate, w_up, tm=1024):
    m, k = a.shape
    f = w_gate.shape[0]
    tm, tn = min(tm, m), _hidden_tile(f)

    def body(a_ref, wg_ref, wu_ref, g_ref, u_ref, h_ref):
        x = a_ref[...]
        g = _dot(x, wg_ref[...], 1, 1)
        u = _dot(x, wu_ref[...], 1, 1)
        g_ref[...] = g.astype(BF16)
        u_ref[...] = u.astype(BF16)
        h_ref[...] = (g * _sigmoid(g) * u).astype(BF16)

    w_spec = pl.BlockSpec((tn, k), lambda j, i: (j, 0))
    o_spec = pl.BlockSpec((tm, tn), lambda j, i: (i, j))
    return _call(body, name="ffn_in", grid=(f // tn, m // tm),
                 in_specs=[pl.BlockSpec((tm, k), lambda j, i: (i, 0)), w_spec, w_spec], out_specs=[o_spec] * 3,
                 out_shape=[_sds((m, f), BF16)] * 3, semantics=("parallel", "parallel"))(a, w_gate, w_up)


def _ffn_down_bwd(d_out, w_down, gate, up, after, tm=1024):
    m, k = d_out.shape
    f = w_down.shape[0]
    tm, tn = min(tm, m), _hidden_tile(f)

    def body(d_ref, w_ref, g_ref, u_ref, dg_ref, du_ref):
        rows = pl.ds(pl.multiple_of(pl.program_id(1) * tm, tm), tm)
        dh = _dot(d_ref[rows, :], w_ref[...], 1, 1)
        g = g_ref[...].astype(F32)
        sg = _sigmoid(g)
        dg_ref[...] = (dh * u_ref[...].astype(F32) * sg * (1.0 + g * (1.0 - sg))).astype(BF16)
        du_ref[...] = (dh * g * sg).astype(BF16)

    t_spec = pl.BlockSpec((tm, tn), lambda j, i: (i, j))
    return _call(body, name="ffn_down_dx", grid=(f // tn, m // tm),
                 in_specs=[pl.BlockSpec((m, k), lambda j, i: (0, 0)), pl.BlockSpec((tn, k), lambda j, i: (j, 0)),
                           t_spec, t_spec],
                 out_specs=[t_spec] * 2, out_shape=[_sds((m, f), BF16)] * 2, semantics=("parallel", "parallel"),
                 n_after=len(after))(d_out, w_down, gate, up, *after)


ALL_PEERS = (1, 2, 3, 4, 5, 6, 7)
CHIP_PEERS = (2, 4, 6)
SIBLING = 1
OWN = 0


def _peer(relation):
    x, y, c = lax.axis_index("x"), lax.axis_index("y"), lax.axis_index("c")
    pos = (1 - x if relation & 4 else x, 1 - y if relation & 2 else y, 1 - c if relation & 1 else c)
    return pos, 4 * pos[0] + 2 * pos[1] + pos[2]


def _slot(relation, by_chip):
    pos, device = _peer(relation)
    return 2 * pos[0] + pos[1] if by_chip else device


def _exchange_copies(ins, lands, send_sems, recv_sems, scatter, relations, by_chip=False):
    me = _slot(0, by_chip)

    def copy(a, s, peer, pos, dst_slot):
        return pltpu.make_async_remote_copy(
            src_ref=ins[a].at[peer] if scatter else ins[a], dst_ref=lands[a].at[dst_slot],
            send_sem=send_sems.at[s], recv_sem=recv_sems.at[s], device_id=pos, device_id_type=pl.DeviceIdType.MESH)

    pairs = []
    for k, r in enumerate(relations):
        pos, peer = _peer(r)[0], _slot(r, by_chip)
        for a in range(len(ins)):
            s = a * len(relations) + k
            pairs.append((copy(a, s, peer, pos, me), copy(a, s, peer, pos, peer)))
    return pairs


def _halves_copies(arrays, lands, send_sems, recv_sems):
    sibling, _ = _peer(SIBLING)
    core = lax.axis_index("c")
    pairs = []
    for a, (ref, land) in enumerate(zip(arrays, lands)):
        send = pltpu.make_async_remote_copy(
            src_ref=ref.at[:, pl.ds(1 - core, 1)], dst_ref=land, send_sem=send_sems.at[a], recv_sem=recv_sems.at[a],
            device_id=sibling, device_id_type=pl.DeviceIdType.MESH)
        pairs.append((send, send))
    return pairs


def _forward_copies(lands, send_sems, recv_sems):
    sibling, _ = _peer(SIBLING)

    def copy(a, s, slot):
        return pltpu.make_async_remote_copy(
            src_ref=lands[a].at[slot], dst_ref=lands[a].at[slot], send_sem=send_sems.at[s], recv_sem=recv_sems.at[s],
            device_id=sibling, device_id_type=pl.DeviceIdType.MESH)

    pairs = []
    for k, r in enumerate(CHIP_PEERS):
        _, mine = _peer(r)
        _, theirs = _peer(r | SIBLING)
        for a in range(len(lands)):
            s = a * len(CHIP_PEERS) + k
            pairs.append((copy(a, s, mine), copy(a, s, theirs)))
    return pairs


_HBM_SPEC = pl.BlockSpec(memory_space=pltpu.HBM)
_SEM_SPEC = pl.BlockSpec(memory_space=pltpu.SEMAPHORE)
_SIDE_EFFECT = pltpu.SideEffectType.DATAFLOW_SIDE_EFFECTING


def _split_start(name, operands, n_sem, make_pairs):
    k = len(operands)

    def body(*refs):
        send_sems, recv_sems, token = refs[k], refs[k + 1], refs[-1]
        for send, _ in make_pairs(refs[:k], send_sems, recv_sems):
            send.start()
        token[...] = jnp.zeros_like(token)

    out = pl.pallas_call(
        body, name=name,
        out_shape=(pltpu.SemaphoreType.DMA((n_sem,)), pltpu.SemaphoreType.DMA((n_sem,)),
                   *[pltpu.HBM(a.shape, a.dtype) for a in operands], _sds((SUBLANES, LANES), F32)),
        in_specs=[_HBM_SPEC] * k,
        out_specs=(_SEM_SPEC, _SEM_SPEC, *[_HBM_SPEC] * k, pl.BlockSpec(memory_space=pltpu.VMEM)),
        input_output_aliases={i: 2 + i for i in range(k)},
        compiler_params=pltpu.CompilerParams(has_side_effects=_SIDE_EFFECT),
    )(*[pltpu.with_memory_space_constraint(a, pltpu.HBM) for a in operands])
    return dict(name=name, sems=out[:2], thru=list(out[2:2 + k]), make_pairs=make_pairs), out[-1]


def _split_wait(handle, after):
    thru, make_pairs = handle["thru"], handle["make_pairs"]
    k = len(thru)

    def body(*refs):
        for send, arrival in make_pairs(refs[:k], refs[k], refs[k + 1]):
            send.wait_send()
            arrival.wait_recv()

    return pl.pallas_call(
        body, name=handle["name"] + "_wait", out_shape=[pltpu.HBM(a.shape, a.dtype) for a in thru],
        in_specs=[_HBM_SPEC] * k + [_SEM_SPEC, _SEM_SPEC] + [pl.BlockSpec(memory_space=pl.ANY)] * len(after),
        out_specs=[_HBM_SPEC] * k, input_output_aliases={i: i for i in range(k)},
        compiler_params=pltpu.CompilerParams(has_side_effects=_SIDE_EFFECT),
    )(*thru, *handle["sems"], *after)


def _exchange_start(name, arrays, scatter, relations, by_chip=False):
    n = len(arrays)
    lands = [lax.empty(a.shape if scatter else (N_DEV,) + a.shape, a.dtype) for a in arrays]

    def make_pairs(refs, send_sems, recv_sems):
        return _exchange_copies(refs[:n], refs[n:], send_sems, recv_sems, scatter, relations, by_chip)

    handle, token = _split_start(name, list(arrays) + lands, n * len(relations), make_pairs)
    handle.update(n=n)
    return handle, token


def _halves_start(name, arrays):
    lands = [lax.empty((a.shape[0], 1) + a.shape[2:], a.dtype) for a in arrays]
    n = len(arrays)

    def make_pairs(refs, send_sems, recv_sems):
        return _halves_copies(refs[:n], refs[n:], send_sems, recv_sems)

    return _split_start(name, list(arrays) + lands, n, make_pairs)


def _chip_sum(name, array, landed):
    chips, _, r, c = array.shape
    tr = r // 2 if r > 512 and r % 32 == 0 else r

    def body(a_ref, b_ref, o_ref):
        mine = a_ref[lax.axis_index("c")].astype(F32)
        o_ref[...] = (mine + b_ref[...].astype(F32)).astype(o_ref.dtype)

    return _call(body, name=name, grid=(chips, r // tr),
                 in_specs=[pl.BlockSpec((None, 2, tr, c), lambda k, i: (k, 0, i, 0)),
                           pl.BlockSpec((None, None, tr, c), lambda k, i: (k, 0, i, 0))],
                 out_specs=pl.BlockSpec((None, tr, c), lambda k, i: (k, i, 0)),
                 out_shape=_sds((chips, r, c), BF16), semantics=("parallel", "parallel"))(array, landed)


def _forward_start(name, lands):
    return _split_start(name, list(lands), len(lands) * len(CHIP_PEERS), _forward_copies)


def _exchange_wait(handle, after):
    return _split_wait(handle, after)[handle["n"]:]


def _rope_tables(pos_col):
    t = pos_col.shape[0]
    half = HEAD_DIM // 2
    inv_freq = ROPE_THETA ** (-jnp.arange(half, dtype=F32) / half)
    inv_row = jnp.tile(inv_freq, LANES // half)[None, :]

    def body(pos_ref, inv_ref, cos_ref, sin_ref):
        ang = pos_ref[...] * inv_ref[...]
        cos_ref[...] = jnp.cos(ang)
        sin_ref[...] = jnp.sin(ang)

    tm = min(t, 512)
    return _call(body, name="rope_tables", grid=(t // tm,),
                 in_specs=[pl.BlockSpec((tm, 1), lambda i: (i, 0)), pl.BlockSpec((1, LANES), lambda i: (0, 0))],
                 out_specs=[pl.BlockSpec((tm, LANES), lambda i: (i, 0))] * 2,
                 out_shape=[_sds((t, LANES), F32)] * 2, semantics=("parallel",))(pos_col, inv_row)


def _rot_half(x):
    lane = lax.broadcasted_iota(jnp.int32, x.shape, 1)
    low = (lane % HEAD_DIM) < HEAD_DIM // 2
    return jnp.where(low, -pltpu.roll(x, LANES - HEAD_DIM // 2, 1), pltpu.roll(x, HEAD_DIM // 2, 1))


def _rope(x, cos, sin):
    return x * cos + _rot_half(x) * sin


def _unrope(d, cos, sin):
    return d * cos - _rot_half(d) * sin


def _band_mask(first_block, heads):
    r = lax.broadcasted_iota(jnp.int32, (heads * BLOCK, 2 * BLOCK), 0) % BLOCK
    c = lax.broadcasted_iota(jnp.int32, (heads * BLOCK, 2 * BLOCK), 1)
    diff = r - c + BLOCK
    return (diff >= 0) & (diff < WINDOW) & ((c >= BLOCK) | jnp.logical_not(first_block))


def _attn_specs(t, d_attn, d_in):
    kb, vb = d_attn // D_KV, d_attn // D_KV + 1
    prev = lambda i: jnp.maximum(i - 1, 0)
    return [
        pl.BlockSpec((BLOCK, d_attn), lambda i: (i, 0)),
        pl.BlockSpec((BLOCK, D_KV), lambda i: (i, kb)),
        pl.BlockSpec((BLOCK, D_KV), lambda i: (i, vb)),
        pl.BlockSpec((BLOCK, D_KV), lambda i: (prev(i), kb)),
        pl.BlockSpec((BLOCK, D_KV), lambda i: (prev(i), vb)),
        pl.BlockSpec((BLOCK, LANES), lambda i: (i, 0)),
        pl.BlockSpec((BLOCK, LANES), lambda i: (i, 0)),
        pl.BlockSpec((BLOCK, LANES), lambda i: (prev(i), 0)),
        pl.BlockSpec((BLOCK, LANES), lambda i: (prev(i), 0)),
        pl.BlockSpec((1, LANES), lambda i: (0, 0)),
    ]


def _head(x, h):
    return x[:, h * HEAD_DIM:(h + 1) * HEAD_DIM]


def _attn_heads(q_ref, kc_ref, vc_ref, kp_ref, vp_ref, cq_ref, sq_ref, cp_ref, sp_ref, d_attn):
    cq, sq, cp, sp = cq_ref[...], sq_ref[...], cp_ref[...], sp_ref[...]
    q_rot = [_rope(q_ref[:, j * LANES:(j + 1) * LANES], cq, sq) for j in range(d_attn // LANES)]
    kc_rot = [_rope(kc_ref[:, j * LANES:(j + 1) * LANES], cq, sq) for j in range(D_KV // LANES)]
    kp_rot = [_rope(kp_ref[:, j * LANES:(j + 1) * LANES], cp, sp) for j in range(D_KV // LANES)]
    per = LANES // HEAD_DIM
    q_heads = [_head(q_rot[h // per], h % per).astype(BF16) for h in range(d_attn // HEAD_DIM)]
    kk = [jnp.concatenate([_head(kp_rot[g // per], g % per), _head(kc_rot[g // per], g % per)], axis=0).astype(BF16)
          for g in range(N_KV_HEADS)]
    vv = [jnp.concatenate([_head(vp_ref[...], g), _head(vc_ref[...], g)], axis=0).astype(BF16) for g in range(N_KV_HEADS)]
    return q_heads, kk, vv


def _stack_group(q_heads, sink_ref, group):
    q_all = jnp.concatenate([q_heads[h] for h in group], axis=0)
    sink_all = jnp.concatenate([jnp.broadcast_to(sink_ref[:, h:h + 1], (BLOCK, 1)) for h in group], axis=0)
    return q_all, sink_all


def _softmax_with_sink(q, kk, sink, mask):
    s = _dot(q, kk, 1, 1) * (1.0 / math.sqrt(HEAD_DIM))
    s = jnp.where(mask, s, MASKED)
    m = jnp.maximum(jnp.max(s, axis=-1, keepdims=True), sink)
    p = jnp.exp(s - m)
    e_sink = jnp.exp(sink - m)
    inv = 1.0 / (jnp.sum(p, axis=-1, keepdims=True) + e_sink)
    return p * inv, e_sink * inv


def _attention_fwd(proj, cos, sin, sinks_row, d_attn):
    t, d_in = proj.shape
    n_heads = d_attn // HEAD_DIM
    q_per_kv = n_heads // N_KV_HEADS

    def body(q_ref, kc_ref, vc_ref, kp_ref, vp_ref, cq_ref, sq_ref, cp_ref, sp_ref, sink_ref, o_ref):
        mask = _band_mask(pl.program_id(0) == 0, q_per_kv)
        q_heads, kk, vv = _attn_heads(q_ref, kc_ref, vc_ref, kp_ref, vp_ref, cq_ref, sq_ref, cp_ref, sp_ref, d_attn)
        for g in range(N_KV_HEADS):
            group = range(g * q_per_kv, (g + 1) * q_per_kv)
            q_all, sink_all = _stack_group(q_heads, sink_ref, group)
            probs, _ = _softmax_with_sink(q_all, kk[g], sink_all, mask)
            o_all = _dot(probs.astype(BF16), vv[g], 1, 0)
            for k, h in enumerate(group):
                o_ref[:, h * HEAD_DIM:(h + 1) * HEAD_DIM] = o_all[k * BLOCK:(k + 1) * BLOCK]

    return _call(body, name="attention_fwd", grid=(t // BLOCK,), in_specs=_attn_specs(t, d_attn, d_in),
                 out_specs=pl.BlockSpec((BLOCK, d_attn), lambda i: (i, 0)), out_shape=_sds((t, d_attn), F32),
                 semantics=("parallel",))(proj, proj, proj, proj, proj, cos, sin, cos, sin, sinks_row)


def _attention_bwd(proj, cos, sin, sinks_row, d_out, d_attn):
    t, d_in = proj.shape
    n_heads = d_attn // HEAD_DIM
    q_per_kv = n_heads // N_KV_HEADS
    nb = t // BLOCK
    per = LANES // HEAD_DIM
    stack = q_per_kv

    def body(q_ref, kc_ref, vc_ref, kp_ref, vp_ref, cq_ref, sq_ref, cp_ref, sp_ref, sink_ref, do_ref,
             dq_ref, dk_ref, dv_ref, dsink_ref):
        i = pl.program_id(0)
        mask = _band_mask(i == 0, stack)
        q_heads, kk, vv = _attn_heads(q_ref, kc_ref, vc_ref, kp_ref, vp_ref, cq_ref, sq_ref, cp_ref, sp_ref, d_attn)
        lane = lax.broadcasted_iota(jnp.int32, (1, LANES), 1)
        dsink = jnp.zeros((1, LANES), F32)
        dq_rot, dkk, dvv = [], [], []
        for g in range(N_KV_HEADS):
            dkk_g = jnp.zeros((2 * BLOCK, HEAD_DIM), F32)
            dvv_g = jnp.zeros((2 * BLOCK, HEAD_DIM), F32)
            for first in range(g * q_per_kv, (g + 1) * q_per_kv, stack):
                group = range(first, first + stack)
                q_all, sink_all = _stack_group(q_heads, sink_ref, group)
                probs, p_sink = _softmax_with_sink(q_all, kk[g], sink_all, mask)
                do_all = jnp.concatenate([do_ref[:, h * HEAD_DIM:(h + 1) * HEAD_DIM] for h in group],
                                         axis=0).astype(BF16)
                dp = _dot(do_all, vv[g], 1, 1)
                delta = jnp.sum(probs * dp, axis=-1, keepdims=True)
                ds = (probs * (dp - delta) * (1.0 / math.sqrt(HEAD_DIM))).astype(BF16)
                dq_all = _dot(ds, kk[g], 1, 0)
                dkk_g += _dot(ds, q_all, 0, 0)
                dvv_g += _dot(probs.astype(BF16), do_all, 0, 0)
                sink_term = p_sink * delta
                for k, h in enumerate(group):
                    dq_rot.append(dq_all[k * BLOCK:(k + 1) * BLOCK])
                    part = jnp.sum(sink_term[k * BLOCK:(k + 1) * BLOCK], axis=0, keepdims=True)
                    dsink += jnp.where(lane == h, -part, 0.0)
            dkk.append(dkk_g)
            dvv.append(dvv_g)
        cq, sq, cp, sp = cq_ref[...], sq_ref[...], cp_ref[...], sp_ref[...]
        for j in range(d_attn // LANES):
            d = jnp.concatenate(dq_rot[j * per:(j + 1) * per], axis=1)
            dq_ref[:, j * LANES:(j + 1) * LANES] = _unrope(d, cq, sq)
        for j in range(D_KV // LANES):
            d = jnp.concatenate(dkk[j * per:(j + 1) * per], axis=1)
            dk_ref[0, :, j * LANES:(j + 1) * LANES] = _unrope(d[:BLOCK], cp, sp)
            dk_ref[1, :, j * LANES:(j + 1) * LANES] = _unrope(d[BLOCK:], cq, sq)
            d = jnp.concatenate(dvv[j * per:(j + 1) * per], axis=1)
            dv_ref[0, :, j * LANES:(j + 1) * LANES] = d[:BLOCK]
            dv_ref[1, :, j * LANES:(j + 1) * LANES] = d[BLOCK:]

        @pl.when(i == 0)
        def _():
            dsink_ref[...] = jnp.zeros_like(dsink_ref)

        dsink_ref[...] += dsink

    pair = pl.BlockSpec((2, BLOCK, D_KV), lambda i: (i, 0, 0))
    return _call(body, name="attention_bwd", grid=(nb,),
                 in_specs=_attn_specs(t, d_attn, d_in) + [pl.BlockSpec((BLOCK, d_attn), lambda i: (i, 0))],
                 out_specs=[pl.BlockSpec((BLOCK, d_attn), lambda i: (i, 0)), pair, pair,
                            pl.BlockSpec((1, LANES), lambda i: (0, 0))],
                 out_shape=[_sds((t, d_attn), F32), _sds((2 * nb, BLOCK, D_KV), F32), _sds((2 * nb, BLOCK, D_KV), F32),
                            _sds((1, LANES), F32)],
                 semantics=("arbitrary",))(proj, proj, proj, proj, proj, cos, sin, cos, sin, sinks_row, d_out)


def _assemble_dproj(dq, dk2, dv2, du, d_in, after):
    t, d_attn = dq.shape
    d_ssm = du.shape[1]
    nb = t // BLOCK

    def body(dq_ref, dk_own, dk_next, dv_own, dv_next, du_ref, o_ref):
        has_next = (pl.program_id(0) < nb - 1).astype(F32)
        o_ref[:, :d_attn] = dq_ref[...].astype(BF16)
        o_ref[:, d_attn:d_attn + D_KV] = (dk_own[...] + has_next * dk_next[...]).astype(BF16)
        o_ref[:, d_attn + D_KV:d_attn + 2 * D_KV] = (dv_own[...] + has_next * dv_next[...]).astype(BF16)
        o_ref[:, d_attn + 2 * D_KV:] = du_ref[...].astype(BF16)

    own = pl.BlockSpec((None, BLOCK, D_KV), lambda i: (2 * i + 1, 0, 0))
    nxt = pl.BlockSpec((None, BLOCK, D_KV), lambda i: (jnp.minimum(2 * i + 2, 2 * nb - 1), 0, 0))
    return _call(body, name="assemble_dproj", grid=(nb,),
                 in_specs=[pl.BlockSpec((BLOCK, d_attn), lambda i: (i, 0)), own, nxt, own, nxt,
                           pl.BlockSpec((BLOCK, d_ssm), lambda i: (i, 0))],
                 out_specs=pl.BlockSpec((BLOCK, d_in), lambda i: (i, 0)), out_shape=_sds((t, d_in), BF16),
                 semantics=("parallel",), n_after=len(after))(dq, dk2, dk2, dv2, dv2, du, *after)


def _discretise(ar, ai, ldt, br, bi):
    dt = jnp.exp(ldt)
    mag = jnp.exp(ar * dt)
    lam_re = mag * jnp.cos(ai * dt)
    lam_im = mag * jnp.sin(ai * dt)
    den = ar * ar + ai * ai
    nr = lam_re - 1.0
    ni = lam_im
    f_re = (nr * ar + ni * ai) / den
    f_im = (ni * ar - nr * ai) / den
    return (lam_re, lam_im, [f_re * r - f_im * i for r, i in zip(br, bi)], [f_re * i + f_im * r for r, i in zip(br, bi)])


def _whole(arrays):
    return [pl.BlockSpec(a.shape, lambda *_, nd=len(a.shape): (0,) * nd) for a in arrays]


def _channels(ref):
    groups = ref.shape[0] // SSM_GROUP
    return [ref[pl.ds(p, groups, stride=SSM_GROUP), :] for p in range(SSM_GROUP)]


def _store_channels(ref, values):
    groups = ref.shape[0] // SSM_GROUP
    for p, val in enumerate(values):
        ref[pl.ds(p, groups, stride=SSM_GROUP), :] = val


def _s5_discretise(ar, ai, ldt, br, bi):
    ins = [ar, ai, ldt, br, bi]

    def body(ar_ref, ai_ref, ldt_ref, br_ref, bi_ref, lr_ref, li_ref, bbr_ref, bbi_ref):
        lr, li, bbr, bbi = _discretise(ar_ref[...], ai_ref[...], ldt_ref[...], _channels(br_ref), _channels(bi_ref))
        lr_ref[...] = lr
        li_ref[...] = li
        _store_channels(bbr_ref, bbr)
        _store_channels(bbi_ref, bbi)

    outs = [_sds(ar.shape, F32), _sds(ar.shape, F32), _sds(br.shape, F32), _sds(br.shape, F32)]
    return _call(body, name="s5_discretise", in_specs=_whole(ins), out_specs=_whole(outs), out_shape=outs)(*ins)


def _s5_discretise_bwd(ar, ai, ldt, br, bi, d_lr, d_li, d_bbr, d_bbi):
    ins = [ar, ai, ldt, br, bi, d_lr, d_li, d_bbr, d_bbi]

    def body(ar_ref, ai_ref, ldt_ref, br_ref, bi_ref, dlr_ref, dli_ref, dbbr_ref, dbbi_ref,
             dar_ref, dai_ref, dldt_ref, dbr_ref, dbi_ref):
        _, vjp = jax.vjp(_discretise, ar_ref[...], ai_ref[...], ldt_ref[...], _channels(br_ref), _channels(bi_ref))
        dar, dai, dldt, dbr, dbi = vjp((dlr_ref[...], dli_ref[...], _channels(dbbr_ref), _channels(dbbi_ref)))
        dar_ref[...] = dar
        dai_ref[...] = dai
        dldt_ref[...] = dldt
        _store_channels(dbr_ref, dbr)
        _store_channels(dbi_ref, dbi)

    outs = [_sds(a.shape, F32) for a in (ar, ai, ldt, br, bi)]
    return _call(body, name="s5_discretise_bwd", in_specs=_whole(ins), out_specs=_whole(outs), out_shape=outs)(*ins)


def _cmul(ar, ai, br, bi):
    return ar * br - ai * bi, ar * bi + ai * br


def _load_segmented(ref, tile0, n_tiles, seg):
    return jnp.concatenate([ref[pl.ds(tile0 + j, SUBLANES, stride=seg), :] for j in range(n_tiles)], axis=0)


def _store_segmented(ref, tile0, seg, value):
    for j in range(value.shape[0] // SUBLANES):
        ref[pl.ds(tile0 + j, SUBLANES, stride=seg), :] = value[j * SUBLANES:(j + 1) * SUBLANES, :]


def _fill_powers(lr, li, pr_ref, pi_ref, seg):
    pows = [(lr, li)]
    for _ in range(SUBLANES - 1):
        pows.append(_cmul(pows[-1][0], pows[-1][1], lr, li))
    row = lax.broadcasted_iota(jnp.int32, (SUBLANES, lr.shape[1]), 0)
    tr = jnp.zeros((SUBLANES, lr.shape[1]), F32)
    ti = jnp.zeros((SUBLANES, lr.shape[1]), F32)
    for r in range(SUBLANES):
        tr = jnp.where(row == r, pows[r][0], tr)
        ti = jnp.where(row == r, pows[r][1], ti)
    pr_ref[0:SUBLANES, :] = tr
    pi_ref[0:SUBLANES, :] = ti
    k = SUBLANES
    while k < seg:
        fr, fi = pr_ref[k - 1:k, :], pi_ref[k - 1:k, :]
        for t0 in range(0, k, SUBLANES):
            nr, ni = _cmul(pr_ref[t0:t0 + SUBLANES, :], pi_ref[t0:t0 + SUBLANES, :], fr, fi)
            pr_ref[k + t0:k + t0 + SUBLANES, :] = nr
            pi_ref[k + t0:k + t0 + SUBLANES, :] = ni
        k *= 2


def _scan_segments(sr_ref, si_ref, pr_ref, pi_ref, lr, li, seg, reverse, per_tile=None):
    w = lr.shape[1]
    sign = -1.0 if reverse else 1.0
    lrb = jnp.broadcast_to(lr, (SUBLANES, w))
    lib = jnp.broadcast_to(sign * li, (SUBLANES, w))
    zero = jnp.zeros((SUBLANES, w), F32)

    def tile_rows(j):
        return pl.ds(pl.multiple_of(j * SUBLANES, SUBLANES), SUBLANES)

    steps = 4 if seg % 4 == 0 else 1

    def local(i, carry):
        for u in range(steps):
            j = i * steps + u
            rows = tile_rows(seg - 1 - j if reverse else j)
            pr, pi = _cmul(lrb, lib, carry[0], carry[1])
            carry = (sr_ref[rows, :] + pr, si_ref[rows, :] + pi)
            sr_ref[rows, :] = carry[0]
            si_ref[rows, :] = carry[1]
        return carry

    end_r, end_i = lax.fori_loop(0, seg // steps, local, (zero, zero))
    full_r, full_i = pr_ref[seg - 1:seg, :], sign * pi_ref[seg - 1:seg, :]
    row = lax.broadcasted_iota(jnp.int32, (SUBLANES, w), 0)
    in_r, in_i = zero, zero
    cur_r, cur_i = jnp.zeros((1, w), F32), jnp.zeros((1, w), F32)
    for r in (range(SUBLANES - 2, -1, -1) if reverse else range(1, SUBLANES)):
        src = r + 1 if reverse else r - 1
        pr, pi = _cmul(full_r, full_i, cur_r, cur_i)
        cur_r, cur_i = end_r[src:src + 1, :] + pr, end_i[src:src + 1, :] + pi
        in_r = jnp.where(row == r, cur_r, in_r)
        in_i = jnp.where(row == r, cur_i, in_i)

    def carry_in(j, _):
        rows = tile_rows(j)
        k = seg - 1 - j if reverse else j
        pr, pi = _cmul(pr_ref[pl.ds(k, 1), :], sign * pi_ref[pl.ds(k, 1), :], in_r, in_i)
        xr, xi = sr_ref[rows, :] + pr, si_ref[rows, :] + pi
        sr_ref[rows, :] = xr
        si_ref[rows, :] = xi
        if per_tile is not None:
            per_tile(j, xr, xi)
        return 0

    lax.fori_loop(0, seg, carry_in, 0, unroll=4)


_S5_ROWS = 2048


def _s5_in_specs(t, d_attn):
    u_block = (d_attn + 2 * D_KV) // SSM_CH_BLOCK
    blk3 = lambda shape: pl.BlockSpec((None,) + shape, lambda j: (j, 0, 0))
    return [
        pl.BlockSpec((t, SSM_CH_BLOCK), lambda j: (0, u_block + j)),
        blk3((SSM_CH_BLOCK, SSM_ST_BLOCK)), blk3((SSM_CH_BLOCK, SSM_ST_BLOCK)),
        blk3((1, SSM_ST_BLOCK)), blk3((1, SSM_ST_BLOCK)),
        blk3((SSM_ST_BLOCK, SSM_CH_BLOCK)), blk3((SSM_ST_BLOCK, SSM_CH_BLOCK)),
        pl.BlockSpec((1, SSM_CH_BLOCK), lambda j: (0, j)),
    ]


def _chunks(t):
    rows = min(_S5_ROWS, t)
    return rows, lambda i: pl.ds(pl.multiple_of(i * rows, rows), rows)


def _s5_states(u_ref, us_ref, bre_ref, bim_ref, lr_ref, li_ref, sr_ref, si_ref, pr_ref, pi_ref, t):
    seg = t // SUBLANES
    rows, chunk = _chunks(t)
    for c in range(t // rows):
        us_ref[c * rows:(c + 1) * rows, :] = _load_segmented(u_ref, c * rows // SUBLANES, rows // SUBLANES, seg)

    def fill(i, _):
        ub = us_ref[chunk(i), :].astype(BF16)
        sr_ref[chunk(i), :] = _dot(ub, bre_ref[...], 1, 0)
        si_ref[chunk(i), :] = _dot(ub, bim_ref[...], 1, 0)
        return 0

    lax.fori_loop(0, t // rows, fill, 0)
    _fill_powers(lr_ref[...], li_ref[...], pr_ref, pi_ref, seg)
    _scan_segments(sr_ref, si_ref, pr_ref, pi_ref, lr_ref[...], li_ref[...], seg, False)


def _s5_scratch(t):
    state = pltpu.VMEM((t, SSM_ST_BLOCK), F32)
    powers = pltpu.VMEM((t // SUBLANES, SSM_ST_BLOCK), F32)
    return state, powers, pltpu.VMEM((t, SSM_CH_BLOCK), F32)


def _s5_fwd(proj, mats, dskip_row, d_attn, d_ssm):
    t = proj.shape[0]
    seg = t // SUBLANES
    n_blocks = d_ssm // SSM_CH_BLOCK
    rows, chunk = _chunks(t)

    def body(u_ref, bre_ref, bim_ref, lr_ref, li_ref, cre_ref, cim_ref, d_ref, y_ref,
             sr_ref, si_ref, pr_ref, pi_ref, us_ref, ys_ref):
        _s5_states(u_ref, us_ref, bre_ref, bim_ref, lr_ref, li_ref, sr_ref, si_ref, pr_ref, pi_ref, t)

        def emit(i, _):
            ys_ref[chunk(i), :] = (_dot(sr_ref[chunk(i), :].astype(BF16), cre_ref[...], 1, 0)
                                   - _dot(si_ref[chunk(i), :].astype(BF16), cim_ref[...], 1, 0)
                                   + d_ref[...] * us_ref[chunk(i), :])
            return 0

        lax.fori_loop(0, t // rows, emit, 0)
        for c in range(t // rows):
            _store_segmented(y_ref, c * rows // SUBLANES, seg, ys_ref[c * rows:(c + 1) * rows, :])

    state, powers, channels = _s5_scratch(t)
    col = pl.BlockSpec((t, SSM_CH_BLOCK), lambda j: (0, j))
    return _call(body, name="s5_fwd", grid=(n_blocks,), in_specs=_s5_in_specs(t, d_attn), out_specs=col,
                 out_shape=_sds((t, d_ssm), F32), scratch_shapes=[state, state, powers, powers, channels, channels],
                 semantics=("parallel",))(proj, *mats, dskip_row)


def _s5_bwd(proj, mats, dskip_row, y, dz_a, dz_b, d_attn, d_ssm, after):
    t = proj.shape[0]
    seg = t // SUBLANES
    n_blocks = d_ssm // SSM_CH_BLOCK
    rows, chunk = _chunks(t)

    def body(u_ref, bre_ref, bim_ref, lr_ref, li_ref, cre_ref, cim_ref, d_ref, y_ref, dza_ref, dzb_ref,
             du_ref, dbre_ref, dbim_ref, dlr_ref, dli_ref, dcre_ref, dcim_ref, dd_ref,
             sr_ref, si_ref, gr_ref, gi_ref, pr_ref, pi_ref, us_ref, dys_ref, dus_ref, acc_r, acc_i):
        _s5_states(u_ref, us_ref, bre_ref, bim_ref, lr_ref, li_ref, sr_ref, si_ref, pr_ref, pi_ref, t)
        for ref in (dcre_ref, dcim_ref, dbre_ref, dbim_ref, dd_ref, acc_r, acc_i):
            ref[...] = jnp.zeros_like(ref)
        for c in range(t // rows):
            tile0, n_tiles = c * rows // SUBLANES, rows // SUBLANES
            dz = _load_segmented(dza_ref, tile0, n_tiles, seg) + _load_segmented(dzb_ref, tile0, n_tiles, seg)
            dys_ref[c * rows:(c + 1) * rows, :] = dz * _gelu_grad(_load_segmented(y_ref, tile0, n_tiles, seg))

        def through_c(i, _):
            dy = dys_ref[chunk(i), :]
            dd_ref[...] += jnp.sum(dy * us_ref[chunk(i), :], axis=0, keepdims=True)
            dyb = dy.astype(BF16)
            gr_ref[chunk(i), :] = _dot(dyb, cre_ref[...], 1, 1)
            gi_ref[chunk(i), :] = -_dot(dyb, cim_ref[...], 1, 1)
            dcre_ref[...] += _dot(sr_ref[chunk(i), :].astype(BF16), dyb, 0, 0)
            dcim_ref[...] -= _dot(si_ref[chunk(i), :].astype(BF16), dyb, 0, 0)
            return 0

        lax.fori_loop(0, t // rows, through_c, 0)

        row = lax.broadcasted_iota(jnp.int32, (SUBLANES, SSM_ST_BLOCK), 0)
        last = pl.ds((seg - 1) * SUBLANES, SUBLANES)
        wrap = [jnp.where(row == 0, 0.0, pltpu.roll(ref[last, :], 1, 0)) for ref in (sr_ref, si_ref)]

        def lambda_grad(j, g_re, g_im):
            before = pl.ds(pl.multiple_of(jnp.maximum(j - 1, 0) * SUBLANES, SUBLANES), SUBLANES)
            prev_r = jnp.where(j > 0, sr_ref[before, :], wrap[0])
            prev_i = jnp.where(j > 0, si_ref[before, :], wrap[1])
            acc_r[...] += g_re * prev_r + g_im * prev_i
            acc_i[...] += g_im * prev_r - g_re * prev_i

        _scan_segments(gr_ref, gi_ref, pr_ref, pi_ref, lr_ref[...], li_ref[...], seg, True, per_tile=lambda_grad)
        dlr_ref[...] = jnp.sum(acc_r[...], axis=0, keepdims=True)
        dli_ref[...] = jnp.sum(acc_i[...], axis=0, keepdims=True)

        def through_b(i, _):
            ub = us_ref[chunk(i), :].astype(BF16)
            grb, gib = gr_ref[chunk(i), :].astype(BF16), gi_ref[chunk(i), :].astype(BF16)
            dbre_ref[...] += _dot(ub, grb, 0, 0)
            dbim_ref[...] += _dot(ub, gib, 0, 0)
            dus_ref[chunk(i), :] = (_dot(grb, bre_ref[...], 1, 1) + _dot(gib, bim_ref[...], 1, 1)
                                    + d_ref[...] * dys_ref[chunk(i), :])
            return 0

        lax.fori_loop(0, t // rows, through_b, 0)
        for c in range(t // rows):
            _store_segmented(du_ref, c * rows // SUBLANES, seg, dus_ref[c * rows:(c + 1) * rows, :])

    col = pl.BlockSpec((t, SSM_CH_BLOCK), lambda j: (0, j))
    blk3 = lambda shape: pl.BlockSpec((None,) + shape, lambda j: (j, 0, 0))
    state, powers, channels = _s5_scratch(t)
    return _call(
        body, name="s5_bwd", grid=(n_blocks,), in_specs=_s5_in_specs(t, d_attn) + [col, col, col],
        out_specs=[col, blk3((SSM_CH_BLOCK, SSM_ST_BLOCK)), blk3((SSM_CH_BLOCK, SSM_ST_BLOCK)),
                   blk3((1, SSM_ST_BLOCK)), blk3((1, SSM_ST_BLOCK)),
                   blk3((SSM_ST_BLOCK, SSM_CH_BLOCK)), blk3((SSM_ST_BLOCK, SSM_CH_BLOCK)),
                   pl.BlockSpec((1, SSM_CH_BLOCK), lambda j: (0, j))],
        out_shape=[_sds((t, d_ssm), F32),
                   _sds((n_blocks, SSM_CH_BLOCK, SSM_ST_BLOCK), F32), _sds((n_blocks, SSM_CH_BLOCK, SSM_ST_BLOCK), F32),
                   _sds((n_blocks, 1, SSM_ST_BLOCK), F32), _sds((n_blocks, 1, SSM_ST_BLOCK), F32),
                   _sds((n_blocks, SSM_ST_BLOCK, SSM_CH_BLOCK), F32), _sds((n_blocks, SSM_ST_BLOCK, SSM_CH_BLOCK), F32),
                   _sds((1, d_ssm), F32)],
        scratch_shapes=[state, state, state, state, powers, powers, channels, channels, channels,
                        pltpu.VMEM((SUBLANES, SSM_ST_BLOCK), F32), pltpu.VMEM((SUBLANES, SSM_ST_BLOCK), F32)],
        semantics=("parallel",), n_after=len(after))(proj, *mats, dskip_row, y, dz_a, dz_b, *after)


def _by_block(gp_n):
    return gp_n.reshape(-1, GROUPS_PER_BLOCK, SSM_GROUP, SSM_STATE)


def _block_diag_in(bbar):
    eye = jnp.eye(GROUPS_PER_BLOCK, dtype=F32)
    return jnp.einsum("jgpn,gh->jgphn", _by_block(bbar), eye).reshape(-1, SSM_CH_BLOCK, SSM_ST_BLOCK)


def _block_diag_in_t(dense):
    d5 = dense.reshape(-1, GROUPS_PER_BLOCK, SSM_GROUP, GROUPS_PER_BLOCK, SSM_STATE)
    eye = jnp.eye(GROUPS_PER_BLOCK, dtype=F32)
    return jnp.einsum("jgphn,gh->jgpn", d5, eye).reshape(-1, SSM_STATE)


def _block_diag_out(c):
    eye = jnp.eye(GROUPS_PER_BLOCK, dtype=F32)
    return jnp.einsum("jgpn,gh->jgnhp", _by_block(c), eye).reshape(-1, SSM_ST_BLOCK, SSM_CH_BLOCK)


def _block_diag_out_t(dense):
    d5 = dense.reshape(-1, GROUPS_PER_BLOCK, SSM_STATE, GROUPS_PER_BLOCK, SSM_GROUP)
    eye = jnp.eye(GROUPS_PER_BLOCK, dtype=F32)
    return jnp.einsum("jgnhp,gh->jgpn", d5, eye).reshape(-1, SSM_STATE)


def _adamw(w, g, m, v):
    m = ADAM_B1 * m + (1.0 - ADAM_B1) * g
    v = ADAM_B2 * v + (1.0 - ADAM_B2) * (g * g)
    m_hat = m / (1.0 - ADAM_B1 ** ADAM_STEP)
    v_hat = v / (1.0 - ADAM_B2 ** ADAM_STEP)
    delta = -ADAM_LR * (m_hat / (jnp.sqrt(v_hat) + ADAM_EPS) + ADAM_WD * w)
    return delta, m, v


def _adam_sharded(name, parts, w, m, v, tr, row0=0):
    r, c = w.shape
    assert r % tr == 0 and row0 % tr == 0, (name, r, tr, row0)

    def body(p_ref, w_ref, m_ref, v_ref, g_out, d_out, m_out, v_out):
        g = p_ref[0].astype(F32)
        for i in range(1, p_ref.shape[0]):
            g = g + p_ref[i].astype(F32)
        delta, m_new, v_new = _adamw(w_ref[...], g, m_ref[...], v_ref[...])
        g_out[...] = g
        d_out[...] = delta
        m_out[...] = m_new
        v_out[...] = v_new

    tile = pl.BlockSpec((tr, c), lambda i: (i, 0))
    return _call(body, name=name, grid=(r // tr,),
                 in_specs=[pl.BlockSpec((parts.shape[0], tr, c), lambda i: (0, i + row0 // tr, 0)), tile, tile, tile],
                 out_specs=[tile] * 4, out_shape=[_sds((r, c), F32)] * 4, semantics=("parallel",))(parts, w, m, v)


_BIG = ("w_in", "w_glu", "w_o", "w_gate", "w_up", "w_down")
_BY_COLUMNS = ("w_in", "w_gate", "w_up")
_SMALL_VECTORS = ("sinks", "log_dt", "b_glu", "g_attn_out", "g_ssm_out", "g_post_mix", "g_pre_ffn", "g_post_ffn")
_SMALL_MATRICES = ("b_re", "b_im", "c_re", "c_im", "a_re", "a_im")
_ORDER = ("g_pre_mix", "w_in", "sinks", "a_re", "a_im", "log_dt", "b_re", "b_im", "c_re", "c_im", "d_skip", "w_glu",
          "b_glu", "g_attn_out", "g_ssm_out", "w_o", "g_post_mix", "g_pre_ffn", "w_gate", "w_up", "w_down",
          "g_post_ffn")


def _pack_grads(vectors, matrices):
    width = max(a.shape[1] for a in vectors)
    slots, row, lane = [], 0, 0
    for a in vectors:
        span = -(-a.shape[1] // LANES) * LANES
        if lane + span > width:
            row, lane = row + 1, 0
        slots.append((row, lane, a.shape[1]))
        lane += span
    firsts, at = [], 0
    for a in matrices:
        firsts.append(at)
        at += a.shape[0]
    nv = len(vectors)

    def body(*refs):
        vec_out, mat_out = refs[-2], refs[-1]
        vec_out[...] = jnp.zeros_like(vec_out)
        for ref, (r, l, w) in zip(refs[:nv], slots):
            vec_out[r:r + 1, l:l + w] = ref[...]
        for ref, r0 in zip(refs[nv:-2], firsts):
            mat_out[r0:r0 + ref.shape[0], :] = ref[...]

    ins = list(vectors) + list(matrices)
    outs = [_sds((-(-(row + 1) // SUBLANES) * SUBLANES, width), F32), _sds((at, matrices[0].shape[1]), F32)]
    vec_pack, mat_pack = _call(body, name="pack_small_grads", in_specs=_whole(ins), out_specs=_whole(outs),
                               out_shape=outs)(*ins)
    return vec_pack, slots, mat_pack, firsts


def _adam_replicated(sources, found_at, w, m, v, total_at):
    ns, n = len(sources), len(w)

    def body(*refs):
        ins, outs = refs[ns:ns + 3 * n], refs[ns + 3 * n:]
        summed = []
        for p_ref in refs[:ns]:
            g = p_ref[0]
            for k in range(1, N_DEV):
                g = g + p_ref[k]
            summed.append(g)
        for i, (src, row, lane) in enumerate(found_at):
            w_ref, m_ref, v_ref = ins[i], ins[n + i], ins[2 * n + i]
            rows, cols = w_ref.shape
            g = summed[src][row:row + rows, lane:lane + cols]
            delta, m_new, v_new = _adamw(w_ref[...], g, m_ref[...], v_ref[...])
            for o, val in zip(outs[4 * i:4 * i + 4], (g, delta, m_new, v_new)):
                o[...] = val
        t_src, t_row, t_lane, t_width = total_at
        outs[-1][...] = summed[t_src][t_row:t_row + 1, t_lane:t_lane + t_width]

    ins = list(sources) + list(w) + list(m) + list(v)
    outs = [_sds(a.shape, F32) for a in w for _ in range(4)] + [_sds((1, total_at[3]), F32)]
    flat = _call(body, name="adam_replicated", in_specs=_whole(ins), out_specs=_whole(outs), out_shape=outs)(*ins)
    return [tuple(flat[4 * i:4 * i + 4]) for i in range(n)], flat[-1]


def kernel(x, positions, g_pre_mix, w_in, sinks, a_re, a_im, log_dt, b_re, b_im, c_re, c_im, d_skip, w_glu, b_glu, g_attn_out, g_ssm_out, w_o, g_post_mix, g_pre_ffn, w_gate, w_up, w_down, g_post_ffn, loss_target, m_g_pre_mix, m_w_in, m_sinks, m_a_re, m_a_im, m_log_dt, m_b_re, m_b_im, m_c_re, m_c_im, m_d_skip, m_w_glu, m_b_glu, m_g_attn_out, m_g_ssm_out, m_w_o, m_g_post_mix, m_g_pre_ffn, m_w_gate, m_w_up, m_w_down, m_g_post_ffn, v_g_pre_mix, v_w_in, v_sinks, v_a_re, v_a_im, v_log_dt, v_b_re, v_b_im, v_c_re, v_c_im, v_d_skip, v_w_glu, v_b_glu, v_g_attn_out, v_g_ssm_out, v_w_o, v_g_post_mix, v_g_pre_ffn, v_w_gate, v_w_up, v_w_down, v_g_post_ffn):
    given = dict(locals())
    weights = {n: given[n] for n in _ORDER}
    mom_m = {n: given["m_" + n] for n in _ORDER}
    mom_v = {n: given["v_" + n] for n in _ORDER}

    t, d = x.shape[1], x.shape[2]
    d_attn = d // 2
    d_ssm = d - d_attn
    d_in = d_attn + 2 * D_KV + d_ssm
    n_groups = d_ssm // SSM_GROUP
    n_heads = d_attn // HEAD_DIM
    tm = min(256, t)

    x2 = x[0]
    target = loss_target[0]

    def by_rows(n, a):
        return a[0].T if n in _BY_COLUMNS else a[0]

    def start_gather(name, ns, token):
        behind = 0 if token is None else token[0, 0].astype(BF16)
        shards = [by_rows(n, weights[n]).astype(BF16) + behind for n in ns]
        return _exchange_start(name, shards, False, (OWN, SIBLING) + CHIP_PEERS)

    def forward_gather(handle, after):
        return _forward_start(handle["name"] + "_forward", _exchange_wait(handle, after))

    def finish_gather(handle, after):
        return _split_wait(forward_gather(handle, after)[0], [])

    ag_in, token = start_gather("gather_w_in", ["w_in"], None)
    ag_mix, token = start_gather("gather_w_glu_o", ["w_glu", "w_o"], token)
    ag_ffn_in, token = start_gather("gather_w_gate_up", ["w_gate", "w_up"], token)
    ag_down, token = start_gather("gather_w_down", ["w_down"], token)

    xn, = _rows("norm_in", lambda xv, g: ([_rms(xv)[0] * g], []), [x2], [g_pre_mix], [(d, BF16)], [], tm,
                after=[token])
    win_g, = finish_gather(ag_in, [xn])
    w_in_t = win_g.reshape(d_in, d)
    proj = _mm_nt("proj_in", xn, w_in_t, F32)

    cos, sin = _rope_tables(positions.reshape(t, 1).astype(F32))
    sinks_row = jnp.pad(sinks, ((0, 0), (0, LANES - n_heads)))
    attn = _attention_fwd(proj, cos, sin, sinks_row, d_attn)

    def view(n, a):
        if n in ("b_re", "b_im"):
            return jnp.transpose(a[0], (0, 2, 1)).reshape(-1, SSM_STATE)
        if n in ("c_re", "c_im"):
            return a[0].reshape(-1, SSM_STATE)
        return a[0].T if n == "d_skip" else a[0] if a.ndim == 3 else a

    def unview(n, val):
        if n in ("b_re", "b_im"):
            return jnp.transpose(val.reshape(n_groups, SSM_GROUP, SSM_STATE), (0, 2, 1))[None]
        if n in ("c_re", "c_im"):
            return val.reshape(1, n_groups, SSM_GROUP, SSM_STATE)
        return val.T[None] if n == "d_skip" else val[None] if weights[n].ndim == 3 else val

    b_re_v, b_im_v = view("b_re", b_re), view("b_im", b_im)
    ldt_col = log_dt.reshape(n_groups, 1)
    lam_re, lam_im, bbar_re, bbar_im = _s5_discretise(a_re[0], a_im[0], ldt_col, b_re_v, b_im_v)
    n_blocks = n_groups // GROUPS_PER_BLOCK
    mats = [_block_diag_in(bbar_re).astype(BF16), _block_diag_in(bbar_im).astype(BF16),
            lam_re.reshape(n_blocks, 1, SSM_ST_BLOCK), lam_im.reshape(n_blocks, 1, SSM_ST_BLOCK),
            _block_diag_out(view("c_re", c_re)).astype(BF16), _block_diag_out(view("c_im", c_im)).astype(BF16)]
    dskip_row = d_skip.reshape(1, d_ssm)
    forward_mix, _ = forward_gather(ag_mix, [attn])
    y_ssm = _s5_fwd(proj, mats, dskip_row, d_attn, d_ssm)
    gelu_bf16 = lambda yv: _gelu(yv).astype(BF16)
    wglu_g, wo_g = _split_wait(forward_mix, [y_ssm])
    w_glu_full = wglu_g.reshape(d_ssm, d_ssm)
    w_o_full = wo_g.reshape(d, d)
    glu_lin = _mm_nn("glu_gate", y_ssm, w_glu_full, F32, a_fn=gelu_bf16)

    def mix_prep(av, yv, gl, bg, ga, gs):
        ssm = _gelu(yv) * _sigmoid(gl + bg)
        return [jnp.concatenate([_rms(av)[0] * ga, _rms(ssm)[0] * gs], axis=1)], []

    mixed, = _rows("mix_prep", mix_prep, [attn, y_ssm, glu_lin], [b_glu, g_attn_out, g_ssm_out], [(d, BF16)], [], tm)
    mix = _mm_nn("mix_out", mixed, w_o_full, F32)

    def post_mix(xv, mv, gpm, gpf):
        h = xv + _rms(mv)[0] * gpm
        return [h, _rms(h)[0] * gpf], []

    forward_ffn_in, token = forward_gather(ag_ffn_in, [mix])
    h, hn = _rows("post_mix", post_mix, [x2, mix], [g_post_mix, g_pre_ffn], [(d, F32), (d, BF16)], [], tm,
                  after=[token])
    wgate_g, wup_g = _split_wait(forward_ffn_in, [hn])
    d_ff = N_DEV * wgate_g.shape[1]
    wgate_t, wup_t = wgate_g.reshape(d_ff, d), wup_g.reshape(d_ff, d)
    gate, up, hid = _ffn_in(hn, wgate_t, wup_t)
    wdown_g, = finish_gather(ag_down, [hid])
    wdown_full = wdown_g.reshape(d_ff, d)
    ff = _mm_nn("ffn_down", hid, wdown_full, F32, tm=1024, tn=512)

    def head(hv, fv, tv, gpo):
        out = hv + _rms(fv)[0] * gpo
        err = out - tv
        dout = err * (1.0 / d)
        dff, dg = _rms_bwd(fv, gpo, dout)
        loss = jnp.zeros((1, LANES), F32) + 0.5 * jnp.sum(err * err) * (1.0 / d)
        return [dff, dout], [dg, loss]

    dff, dh_out, dg_post_ffn, loss_row = _rows("loss_head", head, [h, ff, target], [g_post_ffn],
                                               [(d, BF16), (d, F32)], [d, LANES], tm)

    def swap_halves(name, grads):
        return _halves_start("swap_" + name, [g.reshape(N_DEV // 2, 2, *g.shape[1:]) for g in grads])

    def scatter_chip_sums(name, swap, after):
        both = _split_wait(swap, after)
        half = len(both) // 2
        sums = [_chip_sum("chip_sum_%s_%d" % (name, i), both[i], both[half + i]) for i in range(half)]
        return _exchange_start("scatter_" + name, sums, True, (OWN,) + CHIP_PEERS, by_chip=True)

    f_tile = _hidden_tile(d_ff)
    by_owner = lambda g: g.reshape(N_DEV, d_ff // N_DEV, d)
    dw_down = by_owner(_mm_tn("ffn_down_dw", hid, dff, BF16, tm=f_tile))
    swap_down, token = swap_halves("dw_down", [dw_down])
    dgate, dup = _ffn_down_bwd(dff, wdown_full, gate, up, [token])
    rs_down, token = scatter_chip_sums("dw_down", swap_down, [dgate])
    dhn_gate = _mm_nn("ffn_in_dx_gate", dgate, wgate_t, F32, tm=1024, tn=512, after=[token])
    dhn = _mm_nn("ffn_in_dx_up", dup, wup_t, F32, tm=1024, tn=512, plus=dhn_gate)
    dw_gate = by_owner(_mm_tn("ffn_gate_dw", dgate, hn, BF16, tm=f_tile))
    dw_up = by_owner(_mm_tn("ffn_up_dw", dup, hn, BF16, tm=f_tile))
    swap_ffn_in, tok_ffn_in = swap_halves("dw_gate_up", [dw_gate, dw_up])

    def mid_bwd(dho, dhn_, hv, mv, gpf, gpm):
        d1, dgpf = _rms_bwd(hv, gpf, dhn_)
        dh_ = dho + d1
        dmix_, dgpm = _rms_bwd(mv, gpm, dh_)
        return [dh_, dmix_], [dgpf, dgpm]

    dh, dmix, dg_pre_ffn, dg_post_mix = _rows("mid_bwd", mid_bwd, [dh_out, dhn, h, mix], [g_pre_ffn, g_post_mix],
                                              [(d, F32), (d, BF16)], [d, d], tm, after=[tok_ffn_in])

    dmixed = _mm_nt("mix_out_dx", dmix, w_o_full, F32)
    rs_ffn_in, token = scatter_chip_sums("dw_gate_up", swap_ffn_in, [dmixed])
    dw_o = _mm_tn("mix_out_dw", mixed, dmix, BF16, after=[token])
    swap_o, tok_o = swap_halves("dw_o", [dw_o.reshape(N_DEV, d // N_DEV, d)])

    def mix_bwd(dm, av, yv, gl, bg, ga, gs):
        dattn_, dga = _rms_bwd(av, ga, dm[:, :d_attn])
        z = _gelu(yv)
        sg = _sigmoid(gl + bg)
        dssm, dgs = _rms_bwd(z * sg, gs, dm[:, d_attn:])
        dgl = dssm * z * sg * (1.0 - sg)
        return [dattn_, dssm * sg, dgl], [dga, dgs, jnp.sum(dgl, axis=0, keepdims=True)]

    dattn, dz_direct, dglu, dg_attn_out, dg_ssm_out, db_glu = _rows(
        "mix_bwd", mix_bwd, [dmixed, attn, y_ssm, glu_lin], [b_glu, g_attn_out, g_ssm_out],
        [(d_attn, F32), (d_ssm, F32), (d_ssm, BF16)], [d_attn, d_ssm, d_ssm], tm, after=[tok_o])
    dz_glu = _mm_nt("glu_gate_dx", dglu, w_glu_full, F32)
    dw_glu = _mm_tn("glu_gate_dw", y_ssm, dglu, BF16, a_fn=gelu_bf16)
    rs_o, token = scatter_chip_sums("dw_o", swap_o, [dz_glu, dw_glu])

    du, db_re_dense, db_im_dense, dlam_re, dlam_im, dc_re_dense, dc_im_dense, dd_skip = _s5_bwd(
        proj, mats, dskip_row, y_ssm, dz_direct, dz_glu, d_attn, d_ssm, [token])
    da_re, da_im, dlog_dt, db_re_v, db_im_v = _s5_discretise_bwd(
        a_re[0], a_im[0], ldt_col, b_re_v, b_im_v, dlam_re.reshape(n_groups, SSM_STATE),
        dlam_im.reshape(n_groups, SSM_STATE), _block_diag_in_t(db_re_dense), _block_diag_in_t(db_im_dense))
    dq, dk2, dv2, dsinks_row = _attention_bwd(proj, cos, sin, sinks_row, dattn, d_attn)

    small_grads = {
        "sinks": dsinks_row, "a_re": da_re, "a_im": da_im, "log_dt": dlog_dt.reshape(1, n_groups),
        "b_re": db_re_v, "b_im": db_im_v, "c_re": _block_diag_out_t(dc_re_dense),
        "c_im": _block_diag_out_t(dc_im_dense), "d_skip": dd_skip.reshape(n_groups, SSM_GROUP).T, "b_glu": db_glu,
        "g_attn_out": dg_attn_out, "g_ssm_out": dg_ssm_out, "g_post_mix": dg_post_mix, "g_pre_ffn": dg_pre_ffn,
        "g_post_ffn": dg_post_ffn,
    }
    vec_pack, vec_slots, mat_pack, mat_rows = _pack_grads([small_grads[n] for n in _SMALL_VECTORS] + [loss_row],
                                                          [small_grads[n] for n in _SMALL_MATRICES])
    ag_small, token = _exchange_start("gather_small_grads", [vec_pack, mat_pack, small_grads["d_skip"]], False,
                                      (OWN,) + ALL_PEERS)
    dproj = _assemble_dproj(dq, dk2, dv2, du, d_in, [token])

    dw_in = _mm_tn("proj_in_dw", dproj, xn, BF16).reshape(N_DEV, d_in // N_DEV, d)
    swap_in, token = swap_halves("dw_in_glu", [dw_in, dw_glu.reshape(N_DEV, d_ssm // N_DEV, d_ssm)])
    dxn = _mm_nn("proj_in_dx", dproj, w_in_t, F32, after=[token])
    rs_in, token = scatter_chip_sums("dw_in_glu", swap_in, [dxn])

    def x_bwd(dh_, dxn_, xv, g):
        dx, dg = _rms_bwd(xv, g, dxn_)
        return [dh_ + dx], [dg]

    grad_x, dg_pre_mix = _rows("norm_in_bwd", x_bwd, [dh, dxn, x2], [g_pre_mix], [(d, F32)], [d], tm, after=[token])
    ag_last, token = _exchange_start("gather_g_pre_mix_grad", [dg_pre_mix], False, (OWN,) + ALL_PEERS)

    results = {}

    def adam_big(n, parts):
        r = parts.shape[1]
        tr = next((c for c in range(192, 15, -16) if r % c == 0), r)
        results[n] = _adam_sharded("adam_" + n, parts, by_rows(n, weights[n]), by_rows(n, mom_m[n]),
                                   by_rows(n, mom_v[n]), tr)
        return results[n][3]

    done = [grad_x, token]
    adam_big("w_down", _exchange_wait(rs_down, done)[0])
    p_gate, p_up = _exchange_wait(rs_ffn_in, done)
    done = [adam_big("w_gate", p_gate), adam_big("w_up", p_up), results["w_down"][3]]
    done = [adam_big("w_o", _exchange_wait(rs_o, done)[0])]
    vec_parts, mat_parts, dskip_parts = _exchange_wait(ag_small, done)
    for n, row0 in zip(_SMALL_MATRICES, mat_rows):
        rows = view(n, weights[n]).shape[0]
        results[n] = _adam_sharded("adam_" + n, mat_parts, view(n, weights[n]), view(n, mom_m[n]), view(n, mom_v[n]),
                                   rows, row0)
    p_in, p_glu = _exchange_wait(rs_in, [results[n][3] for n in _SMALL_MATRICES])
    done = [adam_big("w_in", p_in), adam_big("w_glu", p_glu)]
    first_gain_parts, = _exchange_wait(ag_last, done)
    rest = _SMALL_VECTORS + ("d_skip", "g_pre_mix")
    found_at = [(0, row, lane) for row, lane, _ in vec_slots[:-1]] + [(1, 0, 0), (2, 0, 0)]
    updated, loss_sum = _adam_replicated([vec_parts, dskip_parts, first_gain_parts], found_at,
                                         [view(n, weights[n]) for n in rest], [view(n, mom_m[n]) for n in rest],
                                         [view(n, mom_v[n]) for n in rest], (0,) + vec_slots[-1])
    results.update(zip(rest, updated))

    outs = [loss_sum[0, 0], grad_x[None]]
    for k in range(4):
        for n in _ORDER:
            val = results[n][k]
            outs.append(val.T[None] if n in _BY_COLUMNS else val[None] if n in _BIG else unview(n, val))
    return tuple(outs)
```

```python
import math

import jax
import jax.numpy as jnp
from jax import lax
from jax.experimental import pallas as pl
from jax.experimental.pallas import tpu as pltpu

F32 = jnp.float32
BF16 = jnp.bfloat16

HEAD_DIM = 64
N_KV_HEADS = 4
D_KV = N_KV_HEADS * HEAD_DIM
WINDOW = 128
BLOCK = 128
ROPE_THETA = 10000.0
SSM_GROUP = 16
SSM_STATE = 64
GROUPS_PER_BLOCK = 8
SSM_CH_BLOCK = GROUPS_PER_BLOCK * SSM_GROUP
SSM_ST_BLOCK = GROUPS_PER_BLOCK * SSM_STATE
RMS_EPS = 1e-6
N_DEV = 8
LANES = 128
SUBLANES = 8
MASKED = -1e30

ADAM_LR = 0.001
ADAM_B1 = 0.9
ADAM_B2 = 0.999
ADAM_EPS = 1e-08
ADAM_WD = 0.01
ADAM_STEP = 10

VMEM_LIMIT_BYTES = 56 * 1024 * 1024


def _call(body, *, name, out_shape, in_specs, out_specs, grid=(), scratch_shapes=(), semantics=None, n_after=0):
    params = dict(vmem_limit_bytes=VMEM_LIMIT_BYTES)
    if semantics is not None:
        params["dimension_semantics"] = semantics
    n_in = len(in_specs)
    if n_after:
        inner = body

        def body(*refs):
            inner(*refs[:n_in], *refs[n_in + n_after:])

        in_specs = list(in_specs) + [pl.BlockSpec(memory_space=pl.ANY)] * n_after
    return pl.pallas_call(body, name=name, grid=grid, in_specs=in_specs, out_specs=out_specs, out_shape=out_shape,
                          scratch_shapes=scratch_shapes, compiler_params=pltpu.CompilerParams(**params))


def _sds(shape, dtype):
    return jax.ShapeDtypeStruct(tuple(shape), dtype)


def _dot(a, b, ca, cb):
    return lax.dot_general(a, b, (((ca,), (cb,)), ((), ())), preferred_element_type=F32)


def _rms(x):
    r = lax.rsqrt(jnp.mean(x * x, axis=-1, keepdims=True) + RMS_EPS)
    return x * r, r


def _rms_bwd(x, g, dy):
    xh, r = _rms(x)
    dxh = dy * g
    dx = r * (dxh - xh * jnp.mean(dxh * xh, axis=-1, keepdims=True))
    return dx, jnp.sum(dy * xh, axis=0, keepdims=True)


def _sigmoid(x):
    return 1.0 / (1.0 + jnp.exp(-x))


_GELU_C = math.sqrt(2.0 / math.pi)
_GELU_A = 0.044715


def _gelu(y):
    t = jnp.tanh(_GELU_C * (y + _GELU_A * y * y * y))
    return 0.5 * y * (1.0 + t)


def _gelu_grad(y):
    t = jnp.tanh(_GELU_C * (y + _GELU_A * y * y * y))
    return 0.5 * (1.0 + t) + 0.5 * y * (1.0 - t * t) * _GELU_C * (1.0 + 3.0 * _GELU_A * y * y)


def _rows(name, fn, row_ins, vec_ins, row_outs, acc_widths, tm, after=()):
    rows = row_ins[0].shape[0]
    assert rows % tm == 0, (name, rows, tm)
    n_row, n_vec, n_out, n_acc = len(row_ins), len(vec_ins), len(row_outs), len(acc_widths)

    def body(*refs):
        ins = [r[...] for r in refs[:n_row + n_vec]]
        outs = refs[n_row + n_vec:n_row + n_vec + n_out]
        accs = refs[n_row + n_vec + n_out:]
        row_vals, acc_vals = fn(*ins)
        for o, v in zip(outs, row_vals):
            o[...] = v.astype(o.dtype)
        if n_acc:
            @pl.when(pl.program_id(0) == 0)
            def _():
                for a in accs:
                    a[...] = jnp.zeros_like(a)
            for a, v in zip(accs, acc_vals):
                a[...] += v

    in_specs = [pl.BlockSpec((tm, a.shape[1]), lambda i: (i, 0)) for a in row_ins]
    in_specs += [pl.BlockSpec(v.shape, lambda i: (0, 0)) for v in vec_ins]
    out_specs = [pl.BlockSpec((tm, w), lambda i: (i, 0)) for w, _ in row_outs]
    out_specs += [pl.BlockSpec((1, w), lambda i: (0, 0)) for w in acc_widths]
    out_shape = [_sds((rows, w), dt) for w, dt in row_outs] + [_sds((1, w), F32) for w in acc_widths]
    return _call(body, name=name, grid=(rows // tm,), in_specs=in_specs, out_specs=out_specs, out_shape=out_shape,
                 semantics=("arbitrary",) if n_acc else ("parallel",), n_after=len(after))(*row_ins, *vec_ins, *after)


def _matmul(name, operands, in_specs, product, grid, out_shape, out_spec, acc_shape, after=()):
    nk = grid[-1]
    n_in = len(operands)
    in_place = out_shape.dtype == F32

    def body(*refs):
        ins = [r[...] for r in refs[:n_in]]
        o_ref = refs[n_in]
        if nk == 1:
            o_ref[...] = product(*ins).astype(o_ref.dtype)
            return
        acc = o_ref if in_place else refs[n_in + 1]
        k = pl.program_id(len(grid) - 1)

        @pl.when(k == 0)
        def _():
            acc[...] = jnp.zeros_like(acc)

        acc[...] += product(*ins)

        if not in_place:
            @pl.when(k == nk - 1)
            def _():
                o_ref[...] = acc[...].astype(o_ref.dtype)

    return _call(body, name=name, grid=grid, in_specs=in_specs, out_specs=out_spec, out_shape=out_shape,
                 scratch_shapes=[] if nk == 1 or in_place else [pltpu.VMEM(acc_shape, F32)],
                 semantics=("parallel",) * (len(grid) - 1) + ("arbitrary",), n_after=len(after))(*operands, *after)


def _mm_nn(name, a, b, out_dtype, tm=512, tn=None, a_fn=lambda x: x, after=(), plus=None):
    m, k = a.shape
    n = b.shape[1]
    tm, tn = min(tm, m), n if tn is None else min(tn, n)
    operands = [a, b] + ([] if plus is None else [plus])
    specs = [pl.BlockSpec((tm, k), lambda i, j, s: (i, 0)), pl.BlockSpec((k, tn), lambda i, j, s: (0, j))]
    specs += [] if plus is None else [pl.BlockSpec((tm, tn), lambda i, j, s: (i, j))]
    return _matmul(name, operands, specs, lambda x, y, *p: _dot(a_fn(x), y, 1, 0) + (p[0] if p else 0.0),
                   (m // tm, n // tn, 1), _sds((m, n), out_dtype),
                   pl.BlockSpec((tm, tn), lambda i, j, s: (i, j)), (tm, tn), after)


def _mm_nt(name, a, b, out_dtype, tm=512, tn=None):
    m, k = a.shape
    n = b.shape[0]
    tm, tn = min(tm, m), n if tn is None else tn
    return _matmul(name, [a, b],
                   [pl.BlockSpec((tm, k), lambda i, j, s: (i, 0)), pl.BlockSpec((tn, k), lambda i, j, s: (j, 0))],
                   lambda x, y: _dot(x, y, 1, 1), (m // tm, n // tn, 1), _sds((m, n), out_dtype),
                   pl.BlockSpec((tm, tn), lambda i, j, s: (i, j)), (tm, tn))


def _mm_tn(name, a, b, out_dtype, tm=512, tn=None, tk=2048, a_fn=lambda x: x, after=()):
    k, m = a.shape
    n = b.shape[1]
    tm, tk, tn = min(tm, m), min(tk, k), n if tn is None else tn
    return _matmul(name, [a, b],
                   [pl.BlockSpec((tk, tm), lambda i, j, s: (s, i)), pl.BlockSpec((tk, tn), lambda i, j, s: (s, j))],
                   lambda x, y: _dot(a_fn(x), y, 0, 0), (m // tm, n // tn, k // tk), _sds((m, n), out_dtype),
                   pl.BlockSpec((tm, tn), lambda i, j, s: (i, j)), (tm, tn), after)


def _hidden_tile(f):
    return 512 if f % 512 == 0 else 256


def _ffn_in(a, w_gate, w_up, tm=2048):
    m, k = a.shape
    f = w_gate.shape[0]
    tm, tn = min(tm, m), _hidden_tile(f)

    def body(a_ref, wg_ref, wu_ref, g_ref, u_ref, h_ref):
        x = a_ref[...]
        g = _dot(x, wg_ref[...], 1, 1)
        u = _dot(x, wu_ref[...], 1, 1)
        g_ref[...] = g.astype(BF16)
        u_ref[...] = u.astype(BF16)
        h_ref[...] = (g * _sigmoid(g) * u).astype(BF16)

    w_spec = pl.BlockSpec((tn, k), lambda j, i: (j, 0))
    o_spec = pl.BlockSpec((tm, tn), lambda j, i: (i, j))
    return _call(body, name="ffn_in", grid=(f // tn, m // tm),
                 in_specs=[pl.BlockSpec((tm, k), lambda j, i: (i, 0)), w_spec, w_spec], out_specs=[o_spec] * 3,
                 out_shape=[_sds((m, f), BF16)] * 3, semantics=("parallel", "parallel"))(a, w_gate, w_up)


def _ffn_down_bwd(d_out, w_down, gate, up, after, tm=2048):
    m, k = d_out.shape
    f = w_down.shape[0]
    tm, tn = min(tm, m), _hidden_tile(f)

    def body(d_ref, w_ref, g_ref, u_ref, dg_ref, du_ref):
        rows = pl.ds(pl.multiple_of(pl.program_id(1) * tm, tm), tm)
        dh = _dot(d_ref[rows, :], w_ref[...], 1, 1)
        g = g_ref[...].astype(F32)
        sg = _sigmoid(g)
        dg_ref[...] = (dh * u_ref[...].astype(F32) * sg * (1.0 + g * (1.0 - sg))).astype(BF16)
        du_ref[...] = (dh * g * sg).astype(BF16)

    t_spec = pl.BlockSpec((tm, tn), lambda j, i: (i, j))
    return _call(body, name="ffn_down_dx", grid=(f // tn, m // tm),
                 in_specs=[pl.BlockSpec((m, k), lambda j, i: (0, 0)), pl.BlockSpec((tn, k), lambda j, i: (j, 0)),
                           t_spec, t_spec],
                 out_specs=[t_spec] * 2, out_shape=[_sds((m, f), BF16)] * 2, semantics=("parallel", "parallel"),
                 n_after=len(after))(d_out, w_down, gate, up, *after)


ALL_PEERS = (1, 2, 3, 4, 5, 6, 7)
CHIP_PEERS = (2, 4, 6)
SIBLING = 1
OWN = 0


def _peer(relation):
    x, y, c = lax.axis_index("x"), lax.axis_index("y"), lax.axis_index("c")
    pos = (1 - x if relation & 4 else x, 1 - y if relation & 2 else y, 1 - c if relation & 1 else c)
    return pos, 4 * pos[0] + 2 * pos[1] + pos[2]


def _slot(relation, by_chip):
    pos, device = _peer(relation)
    return 2 * pos[0] + pos[1] if by_chip else device


def _exchange_copies(ins, lands, send_sems, recv_sems, scatter, relations, by_chip=False):
    me = _slot(0, by_chip)

    def copy(a, s, peer, pos, dst_slot):
        return pltpu.make_async_remote_copy(
            src_ref=ins[a].at[peer] if scatter else ins[a], dst_ref=lands[a].at[dst_slot],
            send_sem=send_sems.at[s], recv_sem=recv_sems.at[s], device_id=pos, device_id_type=pl.DeviceIdType.MESH)

    pairs = []
    for k, r in enumerate(relations):
        pos, peer = _peer(r)[0], _slot(r, by_chip)
        for a in range(len(ins)):
            s = a * len(relations) + k
            pairs.append((copy(a, s, peer, pos, me), copy(a, s, peer, pos, peer)))
    return pairs


def _halves_copies(arrays, lands, send_sems, recv_sems):
    sibling, _ = _peer(SIBLING)
    core = lax.axis_index("c")
    pairs = []
    for a, (ref, land) in enumerate(zip(arrays, lands)):
        send = pltpu.make_async_remote_copy(
            src_ref=ref.at[:, pl.ds(1 - core, 1)], dst_ref=land, send_sem=send_sems.at[a], recv_sem=recv_sems.at[a],
            device_id=sibling, device_id_type=pl.DeviceIdType.MESH)
        pairs.append((send, send))
    return pairs


def _forward_copies(lands, send_sems, recv_sems):
    sibling, _ = _peer(SIBLING)

    def copy(a, s, slot):
        return pltpu.make_async_remote_copy(
            src_ref=lands[a].at[slot], dst_ref=lands[a].at[slot], send_sem=send_sems.at[s], recv_sem=recv_sems.at[s],
            device_id=sibling, device_id_type=pl.DeviceIdType.MESH)

    pairs = []
    for k, r in enumerate(CHIP_PEERS):
        _, mine = _peer(r)
        _, theirs = _peer(r | SIBLING)
        for a in range(len(lands)):
            s = a * len(CHIP_PEERS) + k
            pairs.append((copy(a, s, mine), copy(a, s, theirs)))
    return pairs


_HBM_SPEC = pl.BlockSpec(memory_space=pltpu.HBM)
_SEM_SPEC = pl.BlockSpec(memory_space=pltpu.SEMAPHORE)
_SIDE_EFFECT = pltpu.SideEffectType.DATAFLOW_SIDE_EFFECTING


def _split_start(name, operands, n_sem, make_pairs):
    k = len(operands)

    def body(*refs):
        send_sems, recv_sems, token = refs[k], refs[k + 1], refs[-1]
        for send, _ in make_pairs(refs[:k], send_sems, recv_sems):
            send.start()
        token[...] = jnp.zeros_like(token)

    out = pl.pallas_call(
        body, name=name,
        out_shape=(pltpu.SemaphoreType.DMA((n_sem,)), pltpu.SemaphoreType.DMA((n_sem,)),
                   *[pltpu.HBM(a.shape, a.dtype) for a in operands], _sds((SUBLANES, LANES), F32)),
        in_specs=[_HBM_SPEC] * k,
        out_specs=(_SEM_SPEC, _SEM_SPEC, *[_HBM_SPEC] * k, pl.BlockSpec(memory_space=pltpu.VMEM)),
        input_output_aliases={i: 2 + i for i in range(k)},
        compiler_params=pltpu.CompilerParams(has_side_effects=_SIDE_EFFECT),
    )(*[pltpu.with_memory_space_constraint(a, pltpu.HBM) for a in operands])
    return dict(name=name, sems=out[:2], thru=list(out[2:2 + k]), make_pairs=make_pairs), out[-1]


def _split_wait(handle, after):
    thru, make_pairs = handle["thru"], handle["make_pairs"]
    k = len(thru)

    def body(*refs):
        for send, arrival in make_pairs(refs[:k], refs[k], refs[k + 1]):
            send.wait_send()
            arrival.wait_recv()

    return pl.pallas_call(
        body, name=handle["name"] + "_wait", out_shape=[pltpu.HBM(a.shape, a.dtype) for a in thru],
        in_specs=[_HBM_SPEC] * k + [_SEM_SPEC, _SEM_SPEC] + [pl.BlockSpec(memory_space=pl.ANY)] * len(after),
        out_specs=[_HBM_SPEC] * k, input_output_aliases={i: i for i in range(k)},
        compiler_params=pltpu.CompilerParams(has_side_effects=_SIDE_EFFECT),
    )(*thru, *handle["sems"], *after)


def _exchange_start(name, arrays, scatter, relations, by_chip=False):
    n = len(arrays)
    lands = [lax.empty(a.shape if scatter else (N_DEV,) + a.shape, a.dtype) for a in arrays]

    def make_pairs(refs, send_sems, recv_sems):
        return _exchange_copies(refs[:n], refs[n:], send_sems, recv_sems, scatter, relations, by_chip)

    handle, token = _split_start(name, list(arrays) + lands, n * len(relations), make_pairs)
    handle.update(n=n)
    return handle, token


def _halves_start(name, arrays):
    lands = [lax.empty((a.shape[0], 1) + a.shape[2:], a.dtype) for a in arrays]
    n = len(arrays)

    def make_pairs(refs, send_sems, recv_sems):
        return _halves_copies(refs[:n], refs[n:], send_sems, recv_sems)

    return _split_start(name, list(arrays) + lands, n, make_pairs)


def _chip_sum(name, array, landed):
    chips, _, r, c = array.shape
    tr = r // 2 if r > 512 and r % 32 == 0 else r

    def body(a_ref, b_ref, o_ref):
        mine = a_ref[lax.axis_index("c")].astype(F32)
        o_ref[...] = (mine + b_ref[...].astype(F32)).astype(o_ref.dtype)

    return _call(body, name=name, grid=(chips, r // tr),
                 in_specs=[pl.BlockSpec((None, 2, tr, c), lambda k, i: (k, 0, i, 0)),
                           pl.BlockSpec((None, None, tr, c), lambda k, i: (k, 0, i, 0))],
                 out_specs=pl.BlockSpec((None, tr, c), lambda k, i: (k, i, 0)),
                 out_shape=_sds((chips, r, c), BF16), semantics=("parallel", "parallel"))(array, landed)


def _forward_start(name, lands):
    return _split_start(name, list(lands), len(lands) * len(CHIP_PEERS), _forward_copies)


def _exchange_wait(handle, after):
    return _split_wait(handle, after)[handle["n"]:]


def _rope_tables(pos_col):
    t = pos_col.shape[0]
    half = HEAD_DIM // 2
    inv_freq = ROPE_THETA ** (-jnp.arange(half, dtype=F32) / half)
    inv_row = jnp.tile(inv_freq, LANES // half)[None, :]

    def body(pos_ref, inv_ref, cos_ref, sin_ref):
        ang = pos_ref[...] * inv_ref[...]
        cos_ref[...] = jnp.cos(ang)
        sin_ref[...] = jnp.sin(ang)

    tm = min(t, 512)
    return _call(body, name="rope_tables", grid=(t // tm,),
                 in_specs=[pl.BlockSpec((tm, 1), lambda i: (i, 0)), pl.BlockSpec((1, LANES), lambda i: (0, 0))],
                 out_specs=[pl.BlockSpec((tm, LANES), lambda i: (i, 0))] * 2,
                 out_shape=[_sds((t, LANES), F32)] * 2, semantics=("parallel",))(pos_col, inv_row)


def _rot_half(x):
    lane = lax.broadcasted_iota(jnp.int32, x.shape, 1)
    low = (lane % HEAD_DIM) < HEAD_DIM // 2
    return jnp.where(low, -pltpu.roll(x, LANES - HEAD_DIM // 2, 1), pltpu.roll(x, HEAD_DIM // 2, 1))


def _rope(x, cos, sin):
    return x * cos + _rot_half(x) * sin


def _unrope(d, cos, sin):
    return d * cos - _rot_half(d) * sin


def _band_mask(first_block, heads):
    r = lax.broadcasted_iota(jnp.int32, (heads * BLOCK, 2 * BLOCK), 0) % BLOCK
    c = lax.broadcasted_iota(jnp.int32, (heads * BLOCK, 2 * BLOCK), 1)
    diff = r - c + BLOCK
    return (diff >= 0) & (diff < WINDOW) & ((c >= BLOCK) | jnp.logical_not(first_block))


def _attn_specs(t, d_attn, d_in):
    kb, vb = d_attn // D_KV, d_attn // D_KV + 1
    prev = lambda i: jnp.maximum(i - 1, 0)
    return [
        pl.BlockSpec((BLOCK, d_attn), lambda i: (i, 0)),
        pl.BlockSpec((BLOCK, D_KV), lambda i: (i, kb)),
        pl.BlockSpec((BLOCK, D_KV), lambda i: (i, vb)),
        pl.BlockSpec((BLOCK, D_KV), lambda i: (prev(i), kb)),
        pl.BlockSpec((BLOCK, D_KV), lambda i: (prev(i), vb)),
        pl.BlockSpec((BLOCK, LANES), lambda i: (i, 0)),
        pl.BlockSpec((BLOCK, LANES), lambda i: (i, 0)),
        pl.BlockSpec((BLOCK, LANES), lambda i: (prev(i), 0)),
        pl.BlockSpec((BLOCK, LANES), lambda i: (prev(i), 0)),
        pl.BlockSpec((1, LANES), lambda i: (0, 0)),
    ]


def _head(x, h):
    return x[:, h * HEAD_DIM:(h + 1) * HEAD_DIM]


def _attn_heads(q_ref, kc_ref, vc_ref, kp_ref, vp_ref, cq_ref, sq_ref, cp_ref, sp_ref, d_attn):
    cq, sq, cp, sp = cq_ref[...], sq_ref[...], cp_ref[...], sp_ref[...]
    q_rot = [_rope(q_ref[:, j * LANES:(j + 1) * LANES], cq, sq) for j in range(d_attn // LANES)]
    kc_rot = [_rope(kc_ref[:, j * LANES:(j + 1) * LANES], cq, sq) for j in range(D_KV // LANES)]
    kp_rot = [_rope(kp_ref[:, j * LANES:(j + 1) * LANES], cp, sp) for j in range(D_KV // LANES)]
    per = LANES // HEAD_DIM
    q_heads = [_head(q_rot[h // per], h % per).astype(BF16) for h in range(d_attn // HEAD_DIM)]
    kk = [jnp.concatenate([_head(kp_rot[g // per], g % per), _head(kc_rot[g // per], g % per)], axis=0).astype(BF16)
          for g in range(N_KV_HEADS)]
    vv = [jnp.concatenate([_head(vp_ref[...], g), _head(vc_ref[...], g)], axis=0).astype(BF16) for g in range(N_KV_HEADS)]
    return q_heads, kk, vv


def _stack_group(q_heads, sink_ref, group):
    q_all = jnp.concatenate([q_heads[h] for h in group], axis=0)
    sink_all = jnp.concatenate([jnp.broadcast_to(sink_ref[:, h:h + 1], (BLOCK, 1)) for h in group], axis=0)
    return q_all, sink_all


def _softmax_with_sink(q, kk, sink, mask):
    s = _dot(q, kk, 1, 1) * (1.0 / math.sqrt(HEAD_DIM))
    s = jnp.where(mask, s, MASKED)
    m = jnp.maximum(jnp.max(s, axis=-1, keepdims=True), sink)
    p = jnp.exp(s - m)
    e_sink = jnp.exp(sink - m)
    inv = 1.0 / (jnp.sum(p, axis=-1, keepdims=True) + e_sink)
    return p * inv, e_sink * inv


def _attention_fwd(proj, cos, sin, sinks_row, d_attn):
    t, d_in = proj.shape
    n_heads = d_attn // HEAD_DIM
    q_per_kv = n_heads // N_KV_HEADS

    def body(q_ref, kc_ref, vc_ref, kp_ref, vp_ref, cq_ref, sq_ref, cp_ref, sp_ref, sink_ref, o_ref):
        mask = _band_mask(pl.program_id(0) == 0, q_per_kv)
        q_heads, kk, vv = _attn_heads(q_ref, kc_ref, vc_ref, kp_ref, vp_ref, cq_ref, sq_ref, cp_ref, sp_ref, d_attn)
        for g in range(N_KV_HEADS):
            group = range(g * q_per_kv, (g + 1) * q_per_kv)
            q_all, sink_all = _stack_group(q_heads, sink_ref, group)
            probs, _ = _softmax_with_sink(q_all, kk[g], sink_all, mask)
            o_all = _dot(probs.astype(BF16), vv[g], 1, 0)
            for k, h in enumerate(group):
                o_ref[:, h * HEAD_DIM:(h + 1) * HEAD_DIM] = o_all[k * BLOCK:(k + 1) * BLOCK]

    return _call(body, name="attention_fwd", grid=(t // BLOCK,), in_specs=_attn_specs(t, d_attn, d_in),
                 out_specs=pl.BlockSpec((BLOCK, d_attn), lambda i: (i, 0)), out_shape=_sds((t, d_attn), F32),
                 semantics=("parallel",))(proj, proj, proj, proj, proj, cos, sin, cos, sin, sinks_row)


def _attention_bwd(proj, cos, sin, sinks_row, d_out, d_attn):
    t, d_in = proj.shape
    n_heads = d_attn // HEAD_DIM
    q_per_kv = n_heads // N_KV_HEADS
    nb = t // BLOCK
    per = LANES // HEAD_DIM
    stack = q_per_kv

    def body(q_ref, kc_ref, vc_ref, kp_ref, vp_ref, cq_ref, sq_ref, cp_ref, sp_ref, sink_ref, do_ref,
             dq_ref, dk_ref, dv_ref, dsink_ref):
        i = pl.program_id(0)
        mask = _band_mask(i == 0, stack)
        q_heads, kk, vv = _attn_heads(q_ref, kc_ref, vc_ref, kp_ref, vp_ref, cq_ref, sq_ref, cp_ref, sp_ref, d_attn)
        lane = lax.broadcasted_iota(jnp.int32, (1, LANES), 1)
        dsink = jnp.zeros((1, LANES), F32)
        dq_rot, dkk, dvv = [], [], []
        for g in range(N_KV_HEADS):
            dkk_g = jnp.zeros((2 * BLOCK, HEAD_DIM), F32)
            dvv_g = jnp.zeros((2 * BLOCK, HEAD_DIM), F32)
            for first in range(g * q_per_kv, (g + 1) * q_per_kv, stack):
                group = range(first, first + stack)
                q_all, sink_all = _stack_group(q_heads, sink_ref, group)
                probs, p_sink = _softmax_with_sink(q_all, kk[g], sink_all, mask)
                do_all = jnp.concatenate([do_ref[:, h * HEAD_DIM:(h + 1) * HEAD_DIM] for h in group],
                                         axis=0).astype(BF16)
                dp = _dot(do_all, vv[g], 1, 1)
                delta = jnp.sum(probs * dp, axis=-1, keepdims=True)
                ds = (probs * (dp - delta) * (1.0 / math.sqrt(HEAD_DIM))).astype(BF16)
                dq_all = _dot(ds, kk[g], 1, 0)
                dkk_g += _dot(ds, q_all, 0, 0)
                dvv_g += _dot(probs.astype(BF16), do_all, 0, 0)
                sink_term = p_sink * delta
                for k, h in enumerate(group):
                    dq_rot.append(dq_all[k * BLOCK:(k + 1) * BLOCK])
                    part = jnp.sum(sink_term[k * BLOCK:(k + 1) * BLOCK], axis=0, keepdims=True)
                    dsink += jnp.where(lane == h, -part, 0.0)
            dkk.append(dkk_g)
            dvv.append(dvv_g)
        cq, sq, cp, sp = cq_ref[...], sq_ref[...], cp_ref[...], sp_ref[...]
        for j in range(d_attn // LANES):
            d = jnp.concatenate(dq_rot[j * per:(j + 1) * per], axis=1)
            dq_ref[:, j * LANES:(j + 1) * LANES] = _unrope(d, cq, sq)
        for j in range(D_KV // LANES):
            d = jnp.concatenate(dkk[j * per:(j + 1) * per], axis=1)
            dk_ref[0, :, j * LANES:(j + 1) * LANES] = _unrope(d[:BLOCK], cp, sp)
            dk_ref[1, :, j * LANES:(j + 1) * LANES] = _unrope(d[BLOCK:], cq, sq)
            d = jnp.concatenate(dvv[j * per:(j + 1) * per], axis=1)
            dv_ref[0, :, j * LANES:(j + 1) * LANES] = d[:BLOCK]
            dv_ref[1, :, j * LANES:(j + 1) * LANES] = d[BLOCK:]

        @pl.when(i == 0)
        def _():
            dsink_ref[...] = jnp.zeros_like(dsink_ref)

        dsink_ref[...] += dsink

    pair = pl.BlockSpec((2, BLOCK, D_KV), lambda i: (i, 0, 0))
    return _call(body, name="attention_bwd", grid=(nb,),
                 in_specs=_attn_specs(t, d_attn, d_in) + [pl.BlockSpec((BLOCK, d_attn), lambda i: (i, 0))],
                 out_specs=[pl.BlockSpec((BLOCK, d_attn), lambda i: (i, 0)), pair, pair,
                            pl.BlockSpec((1, LANES), lambda i: (0, 0))],
                 out_shape=[_sds((t, d_attn), F32), _sds((2 * nb, BLOCK, D_KV), F32), _sds((2 * nb, BLOCK, D_KV), F32),
                            _sds((1, LANES), F32)],
                 semantics=("arbitrary",))(proj, proj, proj, proj, proj, cos, sin, cos, sin, sinks_row, d_out)


def _assemble_dproj(dq, dk2, dv2, du, d_in, after):
    t, d_attn = dq.shape
    d_ssm = du.shape[1]
    nb = t // BLOCK

    def body(dq_ref, dk_own, dk_next, dv_own, dv_next, du_ref, o_ref):
        has_next = (pl.program_id(0) < nb - 1).astype(F32)
        o_ref[:, :d_attn] = dq_ref[...].astype(BF16)
        o_ref[:, d_attn:d_attn + D_KV] = (dk_own[...] + has_next * dk_next[...]).astype(BF16)
        o_ref[:, d_attn + D_KV:d_attn + 2 * D_KV] = (dv_own[...] + has_next * dv_next[...]).astype(BF16)
        o_ref[:, d_attn + 2 * D_KV:] = du_ref[...].astype(BF16)

    own = pl.BlockSpec((None, BLOCK, D_KV), lambda i: (2 * i + 1, 0, 0))
    nxt = pl.BlockSpec((None, BLOCK, D_KV), lambda i: (jnp.minimum(2 * i + 2, 2 * nb - 1), 0, 0))
    return _call(body, name="assemble_dproj", grid=(nb,),
                 in_specs=[pl.BlockSpec((BLOCK, d_attn), lambda i: (i, 0)), own, nxt, own, nxt,
                           pl.BlockSpec((BLOCK, d_ssm), lambda i: (i, 0))],
                 out_specs=pl.BlockSpec((BLOCK, d_in), lambda i: (i, 0)), out_shape=_sds((t, d_in), BF16),
                 semantics=("parallel",), n_after=len(after))(dq, dk2, dk2, dv2, dv2, du, *after)


def _discretise(ar, ai, ldt, br, bi):
    dt = jnp.exp(ldt)
    mag = jnp.exp(ar * dt)
    lam_re = mag * jnp.cos(ai * dt)
    lam_im = mag * jnp.sin(ai * dt)
    den = ar * ar + ai * ai
    nr = lam_re - 1.0
    ni = lam_im
    f_re = (nr * ar + ni * ai) / den
    f_im = (ni * ar - nr * ai) / den
    return (lam_re, lam_im, [f_re * r - f_im * i for r, i in zip(br, bi)], [f_re * i + f_im * r for r, i in zip(br, bi)])


def _whole(arrays):
    return [pl.BlockSpec(a.shape, lambda *_, nd=len(a.shape): (0,) * nd) for a in arrays]


def _channels(ref):
    groups = ref.shape[0] // SSM_GROUP
    return [ref[pl.ds(p, groups, stride=SSM_GROUP), :] for p in range(SSM_GROUP)]


def _store_channels(ref, values):
    groups = ref.shape[0] // SSM_GROUP
    for p, val in enumerate(values):
        ref[pl.ds(p, groups, stride=SSM_GROUP), :] = val


def _s5_discretise(ar, ai, ldt, br, bi):
    ins = [ar, ai, ldt, br, bi]

    def body(ar_ref, ai_ref, ldt_ref, br_ref, bi_ref, lr_ref, li_ref, bbr_ref, bbi_ref):
        lr, li, bbr, bbi = _discretise(ar_ref[...], ai_ref[...], ldt_ref[...], _channels(br_ref), _channels(bi_ref))
        lr_ref[...] = lr
        li_ref[...] = li
        _store_channels(bbr_ref, bbr)
        _store_channels(bbi_ref, bbi)

    outs = [_sds(ar.shape, F32), _sds(ar.shape, F32), _sds(br.shape, F32), _sds(br.shape, F32)]
    return _call(body, name="s5_discretise", in_specs=_whole(ins), out_specs=_whole(outs), out_shape=outs)(*ins)


def _s5_discretise_bwd(ar, ai, ldt, br, bi, d_lr, d_li, d_bbr, d_bbi):
    ins = [ar, ai, ldt, br, bi, d_lr, d_li, d_bbr, d_bbi]

    def body(ar_ref, ai_ref, ldt_ref, br_ref, bi_ref, dlr_ref, dli_ref, dbbr_ref, dbbi_ref,
             dar_ref, dai_ref, dldt_ref, dbr_ref, dbi_ref):
        _, vjp = jax.vjp(_discretise, ar_ref[...], ai_ref[...], ldt_ref[...], _channels(br_ref), _channels(bi_ref))
        dar, dai, dldt, dbr, dbi = vjp((dlr_ref[...], dli_ref[...], _channels(dbbr_ref), _channels(dbbi_ref)))
        dar_ref[...] = dar
        dai_ref[...] = dai
        dldt_ref[...] = dldt
        _store_channels(dbr_ref, dbr)
        _store_channels(dbi_ref, dbi)

    outs = [_sds(a.shape, F32) for a in (ar, ai, ldt, br, bi)]
    return _call(body, name="s5_discretise_bwd", in_specs=_whole(ins), out_specs=_whole(outs), out_shape=outs)(*ins)


def _cmul(ar, ai, br, bi):
    return ar * br - ai * bi, ar * bi + ai * br


def _load_segmented(ref, tile0, n_tiles, seg):
    return jnp.concatenate([ref[pl.ds(tile0 + j, SUBLANES, stride=seg), :] for j in range(n_tiles)], axis=0)


def _store_segmented(ref, tile0, seg, value):
    for j in range(value.shape[0] // SUBLANES):
        ref[pl.ds(tile0 + j, SUBLANES, stride=seg), :] = value[j * SUBLANES:(j + 1) * SUBLANES, :]


def _fill_powers(lr, li, pr_ref, pi_ref, seg):
    pows = [(lr, li)]
    for _ in range(SUBLANES - 1):
        pows.append(_cmul(pows[-1][0], pows[-1][1], lr, li))
    row = lax.broadcasted_iota(jnp.int32, (SUBLANES, lr.shape[1]), 0)
    tr = jnp.zeros((SUBLANES, lr.shape[1]), F32)
    ti = jnp.zeros((SUBLANES, lr.shape[1]), F32)
    for r in range(SUBLANES):
        tr = jnp.where(row == r, pows[r][0], tr)
        ti = jnp.where(row == r, pows[r][1], ti)
    pr_ref[0:SUBLANES, :] = tr
    pi_ref[0:SUBLANES, :] = ti
    k = SUBLANES
    while k < seg:
        fr, fi = pr_ref[k - 1:k, :], pi_ref[k - 1:k, :]
        for t0 in range(0, k, SUBLANES):
            nr, ni = _cmul(pr_ref[t0:t0 + SUBLANES, :], pi_ref[t0:t0 + SUBLANES, :], fr, fi)
            pr_ref[k + t0:k + t0 + SUBLANES, :] = nr
            pi_ref[k + t0:k + t0 + SUBLANES, :] = ni
        k *= 2


def _scan_segments(sr_ref, si_ref, pr_ref, pi_ref, lr, li, seg, reverse, per_tile=None):
    w = lr.shape[1]
    sign = -1.0 if reverse else 1.0
    lrb = jnp.broadcast_to(lr, (SUBLANES, w))
    lib = jnp.broadcast_to(sign * li, (SUBLANES, w))
    zero = jnp.zeros((SUBLANES, w), F32)

    def tile_rows(j):
        return pl.ds(pl.multiple_of(j * SUBLANES, SUBLANES), SUBLANES)

    steps = 4 if seg % 4 == 0 else 1

    def local(i, carry):
        for u in range(steps):
            j = i * steps + u
            rows = tile_rows(seg - 1 - j if reverse else j)
            pr, pi = _cmul(lrb, lib, carry[0], carry[1])
            carry = (sr_ref[rows, :] + pr, si_ref[rows, :] + pi)
            sr_ref[rows, :] = carry[0]
            si_ref[rows, :] = carry[1]
        return carry

    end_r, end_i = lax.fori_loop(0, seg // steps, local, (zero, zero))
    full_r, full_i = pr_ref[seg - 1:seg, :], sign * pi_ref[seg - 1:seg, :]
    row = lax.broadcasted_iota(jnp.int32, (SUBLANES, w), 0)
    in_r, in_i = zero, zero
    cur_r, cur_i = jnp.zeros((1, w), F32), jnp.zeros((1, w), F32)
    for r in (range(SUBLANES - 2, -1, -1) if reverse else range(1, SUBLANES)):
        src = r + 1 if reverse else r - 1
        pr, pi = _cmul(full_r, full_i, cur_r, cur_i)
        cur_r, cur_i = end_r[src:src + 1, :] + pr, end_i[src:src + 1, :] + pi
        in_r = jnp.where(row == r, cur_r, in_r)
        in_i = jnp.where(row == r, cur_i, in_i)

    def carry_in(j, _):
        rows = tile_rows(j)
        k = seg - 1 - j if reverse else j
        pr, pi = _cmul(pr_ref[pl.ds(k, 1), :], sign * pi_ref[pl.ds(k, 1), :], in_r, in_i)
        xr, xi = sr_ref[rows, :] + pr, si_ref[rows, :] + pi
        sr_ref[rows, :] = xr
        si_ref[rows, :] = xi
        if per_tile is not None:
            per_tile(j, xr, xi)
        return 0

    lax.fori_loop(0, seg, carry_in, 0, unroll=4)


_S5_ROWS = 2048


def _s5_in_specs(t, d_attn):
    u_block = (d_attn + 2 * D_KV) // SSM_CH_BLOCK
    blk3 = lambda shape: pl.BlockSpec((None,) + shape, lambda j: (j, 0, 0))
    return [
        pl.BlockSpec((t, SSM_CH_BLOCK), lambda j: (0, u_block + j)),
        blk3((SSM_CH_BLOCK, SSM_ST_BLOCK)), blk3((SSM_CH_BLOCK, SSM_ST_BLOCK)),
        blk3((1, SSM_ST_BLOCK)), blk3((1, SSM_ST_BLOCK)),
        blk3((SSM_ST_BLOCK, SSM_CH_BLOCK)), blk3((SSM_ST_BLOCK, SSM_CH_BLOCK)),
        pl.BlockSpec((1, SSM_CH_BLOCK), lambda j: (0, j)),
    ]


def _chunks(t):
    rows = min(_S5_ROWS, t)
    return rows, lambda i: pl.ds(pl.multiple_of(i * rows, rows), rows)


def _s5_states(u_ref, us_ref, bre_ref, bim_ref, lr_ref, li_ref, sr_ref, si_ref, pr_ref, pi_ref, t):
    seg = t // SUBLANES
    rows, chunk = _chunks(t)
    for c in range(t // rows):
        us_ref[c * rows:(c + 1) * rows, :] = _load_segmented(u_ref, c * rows // SUBLANES, rows // SUBLANES, seg)

    def fill(i, _):
        ub = us_ref[chunk(i), :].astype(BF16)
        sr_ref[chunk(i), :] = _dot(ub, bre_ref[...], 1, 0)
        si_ref[chunk(i), :] = _dot(ub, bim_ref[...], 1, 0)
        return 0

    lax.fori_loop(0, t // rows, fill, 0)
    _fill_powers(lr_ref[...], li_ref[...], pr_ref, pi_ref, seg)
    _scan_segments(sr_ref, si_ref, pr_ref, pi_ref, lr_ref[...], li_ref[...], seg, False)


def _s5_scratch(t):
    state = pltpu.VMEM((t, SSM_ST_BLOCK), F32)
    powers = pltpu.VMEM((t // SUBLANES, SSM_ST_BLOCK), F32)
    return state, powers, pltpu.VMEM((t, SSM_CH_BLOCK), F32)


def _s5_fwd(proj, mats, dskip_row, d_attn, d_ssm):
    t = proj.shape[0]
    seg = t // SUBLANES
    n_blocks = d_ssm // SSM_CH_BLOCK
    rows, chunk = _chunks(t)

    def body(u_ref, bre_ref, bim_ref, lr_ref, li_ref, cre_ref, cim_ref, d_ref, y_ref,
             sr_ref, si_ref, pr_ref, pi_ref, us_ref, ys_ref):
        _s5_states(u_ref, us_ref, bre_ref, bim_ref, lr_ref, li_ref, sr_ref, si_ref, pr_ref, pi_ref, t)

        def emit(i, _):
            ys_ref[chunk(i), :] = (_dot(sr_ref[chunk(i), :].astype(BF16), cre_ref[...], 1, 0)
                                   - _dot(si_ref[chunk(i), :].astype(BF16), cim_ref[...], 1, 0)
                                   + d_ref[...] * us_ref[chunk(i), :])
            return 0

        lax.fori_loop(0, t // rows, emit, 0)
        for c in range(t // rows):
            _store_segmented(y_ref, c * rows // SUBLANES, seg, ys_ref[c * rows:(c + 1) * rows, :])

    state, powers, channels = _s5_scratch(t)
    col = pl.BlockSpec((t, SSM_CH_BLOCK), lambda j: (0, j))
    return _call(body, name="s5_fwd", grid=(n_blocks,), in_specs=_s5_in_specs(t, d_attn), out_specs=col,
                 out_shape=_sds((t, d_ssm), F32), scratch_shapes=[state, state, powers, powers, channels, channels],
                 semantics=("parallel",))(proj, *mats, dskip_row)


def _s5_bwd(proj, mats, dskip_row, y, dz_a, dz_b, d_attn, d_ssm, after):
    t = proj.shape[0]
    seg = t // SUBLANES
    n_blocks = d_ssm // SSM_CH_BLOCK
    rows, chunk = _chunks(t)

    def body(u_ref, bre_ref, bim_ref, lr_ref, li_ref, cre_ref, cim_ref, d_ref, y_ref, dza_ref, dzb_ref,
             du_ref, dbre_ref, dbim_ref, dlr_ref, dli_ref, dcre_ref, dcim_ref, dd_ref,
             sr_ref, si_ref, gr_ref, gi_ref, pr_ref, pi_ref, us_ref, dys_ref, dus_ref, acc_r, acc_i):
        _s5_states(u_ref, us_ref, bre_ref, bim_ref, lr_ref, li_ref, sr_ref, si_ref, pr_ref, pi_ref, t)
        for ref in (dcre_ref, dcim_ref, dbre_ref, dbim_ref, dd_ref, acc_r, acc_i):
            ref[...] = jnp.zeros_like(ref)
        for c in range(t // rows):
            tile0, n_tiles = c * rows // SUBLANES, rows // SUBLANES
            dz = _load_segmented(dza_ref, tile0, n_tiles, seg) + _load_segmented(dzb_ref, tile0, n_tiles, seg)
            dys_ref[c * rows:(c + 1) * rows, :] = dz * _gelu_grad(_load_segmented(y_ref, tile0, n_tiles, seg))

        def through_c(i, _):
            dy = dys_ref[chunk(i), :]
            dd_ref[...] += jnp.sum(dy * us_ref[chunk(i), :], axis=0, keepdims=True)
            dyb = dy.astype(BF16)
            gr_ref[chunk(i), :] = _dot(dyb, cre_ref[...], 1, 1)
            gi_ref[chunk(i), :] = -_dot(dyb, cim_ref[...], 1, 1)
            dcre_ref[...] += _dot(sr_ref[chunk(i), :].astype(BF16), dyb, 0, 0)
            dcim_ref[...] -= _dot(si_ref[chunk(i), :].astype(BF16), dyb, 0, 0)
            return 0

        lax.fori_loop(0, t // rows, through_c, 0)

        row = lax.broadcasted_iota(jnp.int32, (SUBLANES, SSM_ST_BLOCK), 0)
        last = pl.ds((seg - 1) * SUBLANES, SUBLANES)
        wrap = [jnp.where(row == 0, 0.0, pltpu.roll(ref[last, :], 1, 0)) for ref in (sr_ref, si_ref)]

        def lambda_grad(j, g_re, g_im):
            before = pl.ds(pl.multiple_of(jnp.maximum(j - 1, 0) * SUBLANES, SUBLANES), SUBLANES)
            prev_r = jnp.where(j > 0, sr_ref[before, :], wrap[0])
            prev_i = jnp.where(j > 0, si_ref[before, :], wrap[1])
            acc_r[...] += g_re * prev_r + g_im * prev_i
            acc_i[...] += g_im * prev_r - g_re * prev_i

        _scan_segments(gr_ref, gi_ref, pr_ref, pi_ref, lr_ref[...], li_ref[...], seg, True, per_tile=lambda_grad)
        dlr_ref[...] = jnp.sum(acc_r[...], axis=0, keepdims=True)
        dli_ref[...] = jnp.sum(acc_i[...], axis=0, keepdims=True)

        def through_b(i, _):
            ub = us_ref[chunk(i), :].astype(BF16)
            grb, gib = gr_ref[chunk(i), :].astype(BF16), gi_ref[chunk(i), :].astype(BF16)
            dbre_ref[...] += _dot(ub, grb, 0, 0)
            dbim_ref[...] += _dot(ub, gib, 0, 0)
            dus_ref[chunk(i), :] = (_dot(grb, bre_ref[...], 1, 1) + _dot(gib, bim_ref[...], 1, 1)
                                    + d_ref[...] * dys_ref[chunk(i), :])
            return 0

        lax.fori_loop(0, t // rows, through_b, 0)
        for c in range(t // rows):
            _store_segmented(du_ref, c * rows // SUBLANES, seg, dus_ref[c * rows:(c + 1) * rows, :])

    col = pl.BlockSpec((t, SSM_CH_BLOCK), lambda j: (0, j))
    blk3 = lambda shape: pl.BlockSpec((None,) + shape, lambda j: (j, 0, 0))
    state, powers, channels = _s5_scratch(t)
    return _call(
        body, name="s5_bwd", grid=(n_blocks,), in_specs=_s5_in_specs(t, d_attn) + [col, col, col],
        out_specs=[col, blk3((SSM_CH_BLOCK, SSM_ST_BLOCK)), blk3((SSM_CH_BLOCK, SSM_ST_BLOCK)),
                   blk3((1, SSM_ST_BLOCK)), blk3((1, SSM_ST_BLOCK)),
                   blk3((SSM_ST_BLOCK, SSM_CH_BLOCK)), blk3((SSM_ST_BLOCK, SSM_CH_BLOCK)),
                   pl.BlockSpec((1, SSM_CH_BLOCK), lambda j: (0, j))],
        out_shape=[_sds((t, d_ssm), F32),
                   _sds((n_blocks, SSM_CH_BLOCK, SSM_ST_BLOCK), F32), _sds((n_blocks, SSM_CH_BLOCK, SSM_ST_BLOCK), F32),
                   _sds((n_blocks, 1, SSM_ST_BLOCK), F32), _sds((n_blocks, 1, SSM_ST_BLOCK), F32),
                   _sds((n_blocks, SSM_ST_BLOCK, SSM_CH_BLOCK), F32), _sds((n_blocks, SSM_ST_BLOCK, SSM_CH_BLOCK), F32),
                   _sds((1, d_ssm), F32)],
        scratch_shapes=[state, state, state, state, powers, powers, channels, channels, channels,
                        pltpu.VMEM((SUBLANES, SSM_ST_BLOCK), F32), pltpu.VMEM((SUBLANES, SSM_ST_BLOCK), F32)],
        semantics=("parallel",), n_after=len(after))(proj, *mats, dskip_row, y, dz_a, dz_b, *after)


def _by_block(gp_n):
    return gp_n.reshape(-1, GROUPS_PER_BLOCK, SSM_GROUP, SSM_STATE)


def _block_diag_in(bbar):
    eye = jnp.eye(GROUPS_PER_BLOCK, dtype=F32)
    return jnp.einsum("jgpn,gh->jgphn", _by_block(bbar), eye).reshape(-1, SSM_CH_BLOCK, SSM_ST_BLOCK)


def _block_diag_in_t(dense):
    d5 = dense.reshape(-1, GROUPS_PER_BLOCK, SSM_GROUP, GROUPS_PER_BLOCK, SSM_STATE)
    eye = jnp.eye(GROUPS_PER_BLOCK, dtype=F32)
    return jnp.einsum("jgphn,gh->jgpn", d5, eye).reshape(-1, SSM_STATE)


def _block_diag_out(c):
    eye = jnp.eye(GROUPS_PER_BLOCK, dtype=F32)
    return jnp.einsum("jgpn,gh->jgnhp", _by_block(c), eye).reshape(-1, SSM_ST_BLOCK, SSM_CH_BLOCK)


def _block_diag_out_t(dense):
    d5 = dense.reshape(-1, GROUPS_PER_BLOCK, SSM_STATE, GROUPS_PER_BLOCK, SSM_GROUP)
    eye = jnp.eye(GROUPS_PER_BLOCK, dtype=F32)
    return jnp.einsum("jgnhp,gh->jgpn", d5, eye).reshape(-1, SSM_STATE)


def _adamw(w, g, m, v):
    m = ADAM_B1 * m + (1.0 - ADAM_B1) * g
    v = ADAM_B2 * v + (1.0 - ADAM_B2) * (g * g)
    m_hat = m / (1.0 - ADAM_B1 ** ADAM_STEP)
    v_hat = v / (1.0 - ADAM_B2 ** ADAM_STEP)
    delta = -ADAM_LR * (m_hat / (jnp.sqrt(v_hat) + ADAM_EPS) + ADAM_WD * w)
    return delta, m, v


def _adam_sharded(name, parts, w, m, v, tr, row0=0):
    r, c = w.shape
    assert r % tr == 0 and row0 % tr == 0, (name, r, tr, row0)

    def body(p_ref, w_ref, m_ref, v_ref, g_out, d_out, m_out, v_out):
        g = p_ref[0].astype(F32)
        for i in range(1, p_ref.shape[0]):
            g = g + p_ref[i].astype(F32)
        delta, m_new, v_new = _adamw(w_ref[...], g, m_ref[...], v_ref[...])
        g_out[...] = g
        d_out[...] = delta
        m_out[...] = m_new
        v_out[...] = v_new

    tile = pl.BlockSpec((tr, c), lambda i: (i, 0))
    return _call(body, name=name, grid=(r // tr,),
                 in_specs=[pl.BlockSpec((parts.shape[0], tr, c), lambda i: (0, i + row0 // tr, 0)), tile, tile, tile],
                 out_specs=[tile] * 4, out_shape=[_sds((r, c), F32)] * 4, semantics=("parallel",))(parts, w, m, v)


_BIG = ("w_in", "w_glu", "w_o", "w_gate", "w_up", "w_down")
_BY_COLUMNS = ("w_in", "w_gate", "w_up")
_SMALL_VECTORS = ("sinks", "log_dt", "b_glu", "g_attn_out", "g_ssm_out", "g_post_mix", "g_pre_ffn", "g_post_ffn")
_SMALL_MATRICES = ("b_re", "b_im", "c_re", "c_im", "a_re", "a_im")
_ORDER = ("g_pre_mix", "w_in", "sinks", "a_re", "a_im", "log_dt", "b_re", "b_im", "c_re", "c_im", "d_skip", "w_glu",
          "b_glu", "g_attn_out", "g_ssm_out", "w_o", "g_post_mix", "g_pre_ffn", "w_gate", "w_up", "w_down",
          "g_post_ffn")


def _pack_grads(vectors, matrices):
    width = max(a.shape[1] for a in vectors)
    slots, row, lane = [], 0, 0
    for a in vectors:
        span = -(-a.shape[1] // LANES) * LANES
        if lane + span > width:
            row, lane = row + 1, 0
        slots.append((row, lane, a.shape[1]))
        lane += span
    firsts, at = [], 0
    for a in matrices:
        firsts.append(at)
        at += a.shape[0]
    nv = len(vectors)

    def body(*refs):
        vec_out, mat_out = refs[-2], refs[-1]
        vec_out[...] = jnp.zeros_like(vec_out)
        for ref, (r, l, w) in zip(refs[:nv], slots):
            vec_out[r:r + 1, l:l + w] = ref[...]
        for ref, r0 in zip(refs[nv:-2], firsts):
            mat_out[r0:r0 + ref.shape[0], :] = ref[...]

    ins = list(vectors) + list(matrices)
    outs = [_sds((-(-(row + 1) // SUBLANES) * SUBLANES, width), F32), _sds((at, matrices[0].shape[1]), F32)]
    vec_pack, mat_pack = _call(body, name="pack_small_grads", in_specs=_whole(ins), out_specs=_whole(outs),
                               out_shape=outs)(*ins)
    return vec_pack, slots, mat_pack, firsts


def _adam_replicated(sources, found_at, w, m, v, total_at):
    ns, n = len(sources), len(w)

    def body(*refs):
        ins, outs = refs[ns:ns + 3 * n], refs[ns + 3 * n:]
        summed = []
        for p_ref in refs[:ns]:
            g = p_ref[0]
            for k in range(1, N_DEV):
                g = g + p_ref[k]
            summed.append(g)
        for i, (src, row, lane) in enumerate(found_at):
            w_ref, m_ref, v_ref = ins[i], ins[n + i], ins[2 * n + i]
            rows, cols = w_ref.shape
            g = summed[src][row:row + rows, lane:lane + cols]
            delta, m_new, v_new = _adamw(w_ref[...], g, m_ref[...], v_ref[...])
            for o, val in zip(outs[4 * i:4 * i + 4], (g, delta, m_new, v_new)):
                o[...] = val
        t_src, t_row, t_lane, t_width = total_at
        outs[-1][...] = summed[t_src][t_row:t_row + 1, t_lane:t_lane + t_width]

    ins = list(sources) + list(w) + list(m) + list(v)
    outs = [_sds(a.shape, F32) for a in w for _ in range(4)] + [_sds((1, total_at[3]), F32)]
    flat = _call(body, name="adam_replicated", in_specs=_whole(ins), out_specs=_whole(outs), out_shape=outs)(*ins)
    return [tuple(flat[4 * i:4 * i + 4]) for i in range(n)], flat[-1]


def kernel(x, positions, g_pre_mix, w_in, sinks, a_re, a_im, log_dt, b_re, b_im, c_re, c_im, d_skip, w_glu, b_glu, g_attn_out, g_ssm_out, w_o, g_post_mix, g_pre_ffn, w_gate, w_up, w_down, g_post_ffn, loss_target, m_g_pre_mix, m_w_in, m_sinks, m_a_re, m_a_im, m_log_dt, m_b_re, m_b_im, m_c_re, m_c_im, m_d_skip, m_w_glu, m_b_glu, m_g_attn_out, m_g_ssm_out, m_w_o, m_g_post_mix, m_g_pre_ffn, m_w_gate, m_w_up, m_w_down, m_g_post_ffn, v_g_pre_mix, v_w_in, v_sinks, v_a_re, v_a_im, v_log_dt, v_b_re, v_b_im, v_c_re, v_c_im, v_d_skip, v_w_glu, v_b_glu, v_g_attn_out, v_g_ssm_out, v_w_o, v_g_post_mix, v_g_pre_ffn, v_w_gate, v_w_up, v_w_down, v_g_post_ffn):
    given = dict(locals())
    weights = {n: given[n] for n in _ORDER}
    mom_m = {n: given["m_" + n] for n in _ORDER}
    mom_v = {n: given["v_" + n] for n in _ORDER}

    t, d = x.shape[1], x.shape[2]
    d_attn = d // 2
    d_ssm = d - d_attn
    d_in = d_attn + 2 * D_KV + d_ssm
    n_groups = d_ssm // SSM_GROUP
    n_heads = d_attn // HEAD_DIM
    tm = min(256, t)

    x2 = x[0]
    target = loss_target[0]

    def by_rows(n, a):
        return a[0].T if n in _BY_COLUMNS else a[0]

    def start_gather(name, ns, token):
        behind = 0 if token is None else token[0, 0].astype(BF16)
        shards = [by_rows(n, weights[n]).astype(BF16) + behind for n in ns]
        return _exchange_start(name, shards, False, (OWN, SIBLING) + CHIP_PEERS)

    def forward_gather(handle, after):
        return _forward_start(handle["name"] + "_forward", _exchange_wait(handle, after))

    def finish_gather(handle, after):
        return _split_wait(forward_gather(handle, after)[0], [])

    ag_in, token = start_gather("gather_w_in", ["w_in"], None)
    ag_mix, token = start_gather("gather_w_glu_o", ["w_glu", "w_o"], token)
    ag_ffn_in, token = start_gather("gather_w_gate_up", ["w_gate", "w_up"], token)
    ag_down, token = start_gather("gather_w_down", ["w_down"], token)

    xn, = _rows("norm_in", lambda xv, g: ([_rms(xv)[0] * g], []), [x2], [g_pre_mix], [(d, BF16)], [], tm,
                after=[token])
    win_g, = finish_gather(ag_in, [xn])
    w_in_t = win_g.reshape(d_in, d)
    proj = _mm_nt("proj_in", xn, w_in_t, F32)

    cos, sin = _rope_tables(positions.reshape(t, 1).astype(F32))
    sinks_row = jnp.pad(sinks, ((0, 0), (0, LANES - n_heads)))
    attn = _attention_fwd(proj, cos, sin, sinks_row, d_attn)

    def view(n, a):
        if n in ("b_re", "b_im"):
            return jnp.transpose(a[0], (0, 2, 1)).reshape(-1, SSM_STATE)
        if n in ("c_re", "c_im"):
            return a[0].reshape(-1, SSM_STATE)
        return a[0].T if n == "d_skip" else a[0] if a.ndim == 3 else a

    def unview(n, val):
        if n in ("b_re", "b_im"):
            return jnp.transpose(val.reshape(n_groups, SSM_GROUP, SSM_STATE), (0, 2, 1))[None]
        if n in ("c_re", "c_im"):
            return val.reshape(1, n_groups, SSM_GROUP, SSM_STATE)
        return val.T[None] if n == "d_skip" else val[None] if weights[n].ndim == 3 else val

    b_re_v, b_im_v = view("b_re", b_re), view("b_im", b_im)
    ldt_col = log_dt.reshape(n_groups, 1)
    lam_re, lam_im, bbar_re, bbar_im = _s5_discretise(a_re[0], a_im[0], ldt_col, b_re_v, b_im_v)
    n_blocks = n_groups // GROUPS_PER_BLOCK
    mats = [_block_diag_in(bbar_re).astype(BF16), _block_diag_in(bbar_im).astype(BF16),
            lam_re.reshape(n_blocks, 1, SSM_ST_BLOCK), lam_im.reshape(n_blocks, 1, SSM_ST_BLOCK),
            _block_diag_out(view("c_re", c_re)).astype(BF16), _block_diag_out(view("c_im", c_im)).astype(BF16)]
    dskip_row = d_skip.reshape(1, d_ssm)
    forward_mix, _ = forward_gather(ag_mix, [attn])
    y_ssm = _s5_fwd(proj, mats, dskip_row, d_attn, d_ssm)
    gelu_bf16 = lambda yv: _gelu(yv).astype(BF16)
    wglu_g, wo_g = _split_wait(forward_mix, [y_ssm])
    w_glu_full = wglu_g.reshape(d_ssm, d_ssm)
    w_o_full = wo_g.reshape(d, d)
    glu_lin = _mm_nn("glu_gate", y_ssm, w_glu_full, F32, a_fn=gelu_bf16)

    def mix_prep(av, yv, gl, bg, ga, gs):
        ssm = _gelu(yv) * _sigmoid(gl + bg)
        return [jnp.concatenate([_rms(av)[0] * ga, _rms(ssm)[0] * gs], axis=1)], []

    mixed, = _rows("mix_prep", mix_prep, [attn, y_ssm, glu_lin], [b_glu, g_attn_out, g_ssm_out], [(d, BF16)], [], tm)
    mix = _mm_nn("mix_out", mixed, w_o_full, F32)

    def post_mix(xv, mv, gpm, gpf):
        h = xv + _rms(mv)[0] * gpm
        return [h, _rms(h)[0] * gpf], []

    forward_ffn_in, token = forward_gather(ag_ffn_in, [mix])
    h, hn = _rows("post_mix", post_mix, [x2, mix], [g_post_mix, g_pre_ffn], [(d, F32), (d, BF16)], [], tm,
                  after=[token])
    wgate_g, wup_g = _split_wait(forward_ffn_in, [hn])
    d_ff = N_DEV * wgate_g.shape[1]
    wgate_t, wup_t = wgate_g.reshape(d_ff, d), wup_g.reshape(d_ff, d)
    gate, up, hid = _ffn_in(hn, wgate_t, wup_t)
    wdown_g, = finish_gather(ag_down, [hid])
    wdown_full = wdown_g.reshape(d_ff, d)
    ff = _mm_nn("ffn_down", hid, wdown_full, F32, tm=1024, tn=512)

    def head(hv, fv, tv, gpo):
        out = hv + _rms(fv)[0] * gpo
        err = out - tv
        dout = err * (1.0 / d)
        dff, dg = _rms_bwd(fv, gpo, dout)
        loss = jnp.zeros((1, LANES), F32) + 0.5 * jnp.sum(err * err) * (1.0 / d)
        return [dff, dout], [dg, loss]

    dff, dh_out, dg_post_ffn, loss_row = _rows("loss_head", head, [h, ff, target], [g_post_ffn],
                                               [(d, BF16), (d, F32)], [d, LANES], tm)

    def swap_halves(name, grads):
        return _halves_start("swap_" + name, [g.reshape(N_DEV // 2, 2, *g.shape[1:]) for g in grads])

    def scatter_chip_sums(name, swap, after):
        both = _split_wait(swap, after)
        half = len(both) // 2
        sums = [_chip_sum("chip_sum_%s_%d" % (name, i), both[i], both[half + i]) for i in range(half)]
        return _exchange_start("scatter_" + name, sums, True, (OWN,) + CHIP_PEERS, by_chip=True)

    f_tile = _hidden_tile(d_ff)
    by_owner = lambda g: g.reshape(N_DEV, d_ff // N_DEV, d)
    dw_down = by_owner(_mm_tn("ffn_down_dw", hid, dff, BF16, tm=f_tile))
    swap_down, token = swap_halves("dw_down", [dw_down])
    dgate, dup = _ffn_down_bwd(dff, wdown_full, gate, up, [token])
    rs_down, token = scatter_chip_sums("dw_down", swap_down, [dgate])
    dhn_gate = _mm_nn("ffn_in_dx_gate", dgate, wgate_t, F32, tm=1024, tn=512, after=[token])
    dhn = _mm_nn("ffn_in_dx_up", dup, wup_t, F32, tm=1024, tn=512, plus=dhn_gate)
    dw_gate = by_owner(_mm_tn("ffn_gate_dw", dgate, hn, BF16, tm=f_tile))
    dw_up = by_owner(_mm_tn("ffn_up_dw", dup, hn, BF16, tm=f_tile))
    swap_ffn_in, tok_ffn_in = swap_halves("dw_gate_up", [dw_gate, dw_up])

    def mid_bwd(dho, dhn_, hv, mv, gpf, gpm):
        d1, dgpf = _rms_bwd(hv, gpf, dhn_)
        dh_ = dho + d1
        dmix_, dgpm = _rms_bwd(mv, gpm, dh_)
        return [dh_, dmix_], [dgpf, dgpm]

    dh, dmix, dg_pre_ffn, dg_post_mix = _rows("mid_bwd", mid_bwd, [dh_out, dhn, h, mix], [g_pre_ffn, g_post_mix],
                                              [(d, F32), (d, BF16)], [d, d], tm, after=[tok_ffn_in])

    dmixed = _mm_nt("mix_out_dx", dmix, w_o_full, F32)
    rs_ffn_in, token = scatter_chip_sums("dw_gate_up", swap_ffn_in, [dmixed])
    dw_o = _mm_tn("mix_out_dw", mixed, dmix, BF16, after=[token])
    swap_o, tok_o = swap_halves("dw_o", [dw_o.reshape(N_DEV, d // N_DEV, d)])

    def mix_bwd(dm, av, yv, gl, bg, ga, gs):
        dattn_, dga = _rms_bwd(av, ga, dm[:, :d_attn])
        z = _gelu(yv)
        sg = _sigmoid(gl + bg)
        dssm, dgs = _rms_bwd(z * sg, gs, dm[:, d_attn:])
        dgl = dssm * z * sg * (1.0 - sg)
        return [dattn_, dssm * sg, dgl], [dga, dgs, jnp.sum(dgl, axis=0, keepdims=True)]

    dattn, dz_direct, dglu, dg_attn_out, dg_ssm_out, db_glu = _rows(
        "mix_bwd", mix_bwd, [dmixed, attn, y_ssm, glu_lin], [b_glu, g_attn_out, g_ssm_out],
        [(d_attn, F32), (d_ssm, F32), (d_ssm, BF16)], [d_attn, d_ssm, d_ssm], tm, after=[tok_o])
    dz_glu = _mm_nt("glu_gate_dx", dglu, w_glu_full, F32)
    dw_glu = _mm_tn("glu_gate_dw", y_ssm, dglu, BF16, a_fn=gelu_bf16)
    rs_o, token = scatter_chip_sums("dw_o", swap_o, [dz_glu, dw_glu])

    du, db_re_dense, db_im_dense, dlam_re, dlam_im, dc_re_dense, dc_im_dense, dd_skip = _s5_bwd(
        proj, mats, dskip_row, y_ssm, dz_direct, dz_glu, d_attn, d_ssm, [token])
    da_re, da_im, dlog_dt, db_re_v, db_im_v = _s5_discretise_bwd(
        a_re[0], a_im[0], ldt_col, b_re_v, b_im_v, dlam_re.reshape(n_groups, SSM_STATE),
        dlam_im.reshape(n_groups, SSM_STATE), _block_diag_in_t(db_re_dense), _block_diag_in_t(db_im_dense))
    dq, dk2, dv2, dsinks_row = _attention_bwd(proj, cos, sin, sinks_row, dattn, d_attn)

    small_grads = {
        "sinks": dsinks_row, "a_re": da_re, "a_im": da_im, "log_dt": dlog_dt.reshape(1, n_groups),
        "b_re": db_re_v, "b_im": db_im_v, "c_re": _block_diag_out_t(dc_re_dense),
        "c_im": _block_diag_out_t(dc_im_dense), "d_skip": dd_skip.reshape(n_groups, SSM_GROUP).T, "b_glu": db_glu,
        "g_attn_out": dg_attn_out, "g_ssm_out": dg_ssm_out, "g_post_mix": dg_post_mix, "g_pre_ffn": dg_pre_ffn,
        "g_post_ffn": dg_post_ffn,
    }
    vec_pack, vec_slots, mat_pack, mat_rows = _pack_grads([small_grads[n] for n in _SMALL_VECTORS] + [loss_row],
                                                          [small_grads[n] for n in _SMALL_MATRICES])
    ag_small, token = _exchange_start("gather_small_grads", [vec_pack, mat_pack, small_grads["d_skip"]], False,
                                      (OWN,) + ALL_PEERS)
    dproj = _assemble_dproj(dq, dk2, dv2, du, d_in, [token])

    dw_in = _mm_tn("proj_in_dw", dproj, xn, BF16).reshape(N_DEV, d_in // N_DEV, d)
    swap_in, token = swap_halves("dw_in_glu", [dw_in, dw_glu.reshape(N_DEV, d_ssm // N_DEV, d_ssm)])
    dxn = _mm_nn("proj_in_dx", dproj, w_in_t, F32, after=[token])
    rs_in, token = scatter_chip_sums("dw_in_glu", swap_in, [dxn])

    def x_bwd(dh_, dxn_, xv, g):
        dx, dg = _rms_bwd(xv, g, dxn_)
        return [dh_ + dx], [dg]

    grad_x, dg_pre_mix = _rows("norm_in_bwd", x_bwd, [dh, dxn, x2], [g_pre_mix], [(d, F32)], [d], tm, after=[token])
    ag_last, token = _exchange_start("gather_g_pre_mix_grad", [dg_pre_mix], False, (OWN,) + ALL_PEERS)

    results = {}

    def adam_big(n, parts):
        r = parts.shape[1]
        tr = next((c for c in range(192, 15, -16) if r % c == 0), r)
        results[n] = _adam_sharded("adam_" + n, parts, by_rows(n, weights[n]), by_rows(n, mom_m[n]),
                                   by_rows(n, mom_v[n]), tr)
        return results[n][3]

    done = [grad_x, token]
    adam_big("w_down", _exchange_wait(rs_down, done)[0])
    p_gate, p_up = _exchange_wait(rs_ffn_in, done)
    done = [adam_big("w_gate", p_gate), adam_big("w_up", p_up), results["w_down"][3]]
    done = [adam_big("w_o", _exchange_wait(rs_o, done)[0])]
    vec_parts, mat_parts, dskip_parts = _exchange_wait(ag_small, done)
    for n, row0 in zip(_SMALL_MATRICES, mat_rows):
        rows = view(n, weights[n]).shape[0]
        results[n] = _adam_sharded("adam_" + n, mat_parts, view(n, weights[n]), view(n, mom_m[n]), view(n, mom_v[n]),
                                   rows, row0)
    p_in, p_glu = _exchange_wait(rs_in, [results[n][3] for n in _SMALL_MATRICES])
    done = [adam_big("w_in", p_in), adam_big("w_glu", p_glu)]
    first_gain_parts, = _exchange_wait(ag_last, done)
    rest = _SMALL_VECTORS + ("d_skip", "g_pre_mix")
    found_at = [(0, row, lane) for row, lane, _ in vec_slots[:-1]] + [(1, 0, 0), (2, 0, 0)]
    updated, loss_sum = _adam_replicated([vec_parts, dskip_parts, first_gain_parts], found_at,
                                         [view(n, weights[n]) for n in rest], [view(n, mom_m[n]) for n in rest],
                                         [view(n, mom_v[n]) for n in rest], (0,) + vec_slots[-1])
    results.update(zip(rest, updated))

    outs = [loss_sum[0, 0], grad_x[None]]
    for k in range(4):
        for n in _ORDER:
            val = results[n][k]
            outs.append(val.T[None] if n in _BY_COLUMNS else val[None] if n in _BIG else unview(n, val))
    return tuple(outs)
```

```python
import math

import jax
import jax.numpy as jnp
from jax import lax
from jax.experimental import pallas as pl
from jax.experimental.pallas import tpu as pltpu

F32 = jnp.float32
BF16 = jnp.bfloat16

HEAD_DIM = 64
N_KV_HEADS = 4
D_KV = N_KV_HEADS * HEAD_DIM
WINDOW = 128
BLOCK = 128
ROPE_THETA = 10000.0
SSM_GROUP = 16
SSM_STATE = 64
GROUPS_PER_BLOCK = 8
SSM_CH_BLOCK = GROUPS_PER_BLOCK * SSM_GROUP
SSM_ST_BLOCK = GROUPS_PER_BLOCK * SSM_STATE
RMS_EPS = 1e-6
N_DEV = 8
LANES = 128
SUBLANES = 8
MASKED = -1e30

ADAM_LR = 0.001
ADAM_B1 = 0.9
ADAM_B2 = 0.999
ADAM_EPS = 1e-08
ADAM_WD = 0.01
ADAM_STEP = 10

VMEM_LIMIT_BYTES = 56 * 1024 * 1024


def _call(body, *, name, out_shape, in_specs, out_specs, grid=(), scratch_shapes=(), semantics=None, n_after=0):
    params = dict(vmem_limit_bytes=VMEM_LIMIT_BYTES)
    if semantics is not None:
        params["dimension_semantics"] = semantics
    n_in = len(in_specs)
    if n_after:
        inner = body

        def body(*refs):
            inner(*refs[:n_in], *refs[n_in + n_after:])

        in_specs = list(in_specs) + [pl.BlockSpec(memory_space=pl.ANY)] * n_after
    return pl.pallas_call(body, name=name, grid=grid, in_specs=in_specs, out_specs=out_specs, out_shape=out_shape,
                          scratch_shapes=scratch_shapes, compiler_params=pltpu.CompilerParams(**params))


def _sds(shape, dtype):
    return jax.ShapeDtypeStruct(tuple(shape), dtype)


def _dot(a, b, ca, cb):
    return lax.dot_general(a, b, (((ca,), (cb,)), ((), ())), preferred_element_type=F32)


def _rms(x):
    r = lax.rsqrt(jnp.mean(x * x, axis=-1, keepdims=True) + RMS_EPS)
    return x * r, r


def _rms_bwd(x, g, dy):
    xh, r = _rms(x)
    dxh = dy * g
    dx = r * (dxh - xh * jnp.mean(dxh * xh, axis=-1, keepdims=True))
    return dx, jnp.sum(dy * xh, axis=0, keepdims=True)


def _sigmoid(x):
    return 1.0 / (1.0 + jnp.exp(-x))


_GELU_C = math.sqrt(2.0 / math.pi)
_GELU_A = 0.044715


def _gelu(y):
    t = jnp.tanh(_GELU_C * (y + _GELU_A * y * y * y))
    return 0.5 * y * (1.0 + t)


def _gelu_grad(y):
    t = jnp.tanh(_GELU_C * (y + _GELU_A * y * y * y))
    return 0.5 * (1.0 + t) + 0.5 * y * (1.0 - t * t) * _GELU_C * (1.0 + 3.0 * _GELU_A * y * y)


def _rows(name, fn, row_ins, vec_ins, row_outs, acc_widths, tm, after=()):
    rows = row_ins[0].shape[0]
    assert rows % tm == 0, (name, rows, tm)
    n_row, n_vec, n_out, n_acc = len(row_ins), len(vec_ins), len(row_outs), len(acc_widths)

    def body(*refs):
        ins = [r[...] for r in refs[:n_row + n_vec]]
        outs = refs[n_row + n_vec:n_row + n_vec + n_out]
        accs = refs[n_row + n_vec + n_out:]
        row_vals, acc_vals = fn(*ins)
        for o, v in zip(outs, row_vals):
            o[...] = v.astype(o.dtype)
        if n_acc:
            @pl.when(pl.program_id(0) == 0)
            def _():
                for a in accs:
                    a[...] = jnp.zeros_like(a)
            for a, v in zip(accs, acc_vals):
                a[...] += v

    in_specs = [pl.BlockSpec((tm, a.shape[1]), lambda i: (i, 0)) for a in row_ins]
    in_specs += [pl.BlockSpec(v.shape, lambda i: (0, 0)) for v in vec_ins]
    out_specs = [pl.BlockSpec((tm, w), lambda i: (i, 0)) for w, _ in row_outs]
    out_specs += [pl.BlockSpec((1, w), lambda i: (0, 0)) for w in acc_widths]
    out_shape = [_sds((rows, w), dt) for w, dt in row_outs] + [_sds((1, w), F32) for w in acc_widths]
    return _call(body, name=name, grid=(rows // tm,), in_specs=in_specs, out_specs=out_specs, out_shape=out_shape,
                 semantics=("arbitrary",) if n_acc else ("parallel",), n_after=len(after))(*row_ins, *vec_ins, *after)


def _matmul(name, operands, in_specs, product, grid, out_shape, out_spec, acc_shape, after=()):
    nk = grid[-1]
    n_in = len(operands)
    in_place = out_shape.dtype == F32

    def body(*refs):
        ins = [r[...] for r in refs[:n_in]]
        o_ref = refs[n_in]
        if nk == 1:
            o_ref[...] = product(*ins).astype(o_ref.dtype)
            return
        acc = o_ref if in_place else refs[n_in + 1]
        k = pl.program_id(len(grid) - 1)

        @pl.when(k == 0)
        def _():
            acc[...] = jnp.zeros_like(acc)

        acc[...] += product(*ins)

        if not in_place:
            @pl.when(k == nk - 1)
            def _():
                o_ref[...] = acc[...].astype(o_ref.dtype)

    return _call(body, name=name, grid=grid, in_specs=in_specs, out_specs=out_spec, out_shape=out_shape,
                 scratch_shapes=[] if nk == 1 or in_place else [pltpu.VMEM(acc_shape, F32)],
                 semantics=("parallel",) * (len(grid) - 1) + ("arbitrary",), n_after=len(after))(*operands, *after)


def _mm_nn(name, a, b, out_dtype, tm=512, tn=None, a_fn=lambda x: x, after=(), plus=None):
    m, k = a.shape
    n = b.shape[1]
    tm, tn = min(tm, m), n if tn is None else min(tn, n)
    operands = [a, b] + ([] if plus is None else [plus])
    specs = [pl.BlockSpec((tm, k), lambda i, j, s: (i, 0)), pl.BlockSpec((k, tn), lambda i, j, s: (0, j))]
    specs += [] if plus is None else [pl.BlockSpec((tm, tn), lambda i, j, s: (i, j))]
    return _matmul(name, operands, specs, lambda x, y, *p: _dot(a_fn(x), y, 1, 0) + (p[0] if p else 0.0),
                   (m // tm, n // tn, 1), _sds((m, n), out_dtype),
                   pl.BlockSpec((tm, tn), lambda i, j, s: (i, j)), (tm, tn), after)


def _mm_nt(name, a, b, out_dtype, tm=512, tn=None):
    m, k = a.shape
    n = b.shape[0]
    tm, tn = min(tm, m), n if tn is None else tn
    return _matmul(name, [a, b],
                   [pl.BlockSpec((tm, k), lambda i, j, s: (i, 0)), pl.BlockSpec((tn, k), lambda i, j, s: (j, 0))],
                   lambda x, y: _dot(x, y, 1, 1), (m // tm, n // tn, 1), _sds((m, n), out_dtype),
                   pl.BlockSpec((tm, tn), lambda i, j, s: (i, j)), (tm, tn))


def _mm_tn(name, a, b, out_dtype, tm=512, tn=None, tk=2048, a_fn=lambda x: x, after=()):
    k, m = a.shape
    n = b.shape[1]
    tm, tk, tn = min(tm, m), min(tk, k), n if tn is None else tn
    return _matmul(name, [a, b],
                   [pl.BlockSpec((tk, tm), lambda i, j, s: (s, i)), pl.BlockSpec((tk, tn), lambda i, j, s: (s, j))],
                   lambda x, y: _dot(a_fn(x), y, 0, 0), (m // tm, n // tn, k // tk), _sds((m, n), out_dtype),
                   pl.BlockSpec((tm, tn), lambda i, j, s: (i, j)), (tm, tn), after)


def _hidden_tile(f):
    return 512 if f % 512 == 0 else 256


def _ffn_in(a, w_gate, w_up, tm=1024):
    m, k = a.shape
    f = w_gate.shape[0]
    tm, tn = min(tm, m), _hidden_tile(f)

    def body(a_ref, wg_ref, wu_ref, g_ref, u_ref, h_ref):
        x = a_ref[...]
        g = _dot(x, wg_ref[...], 1, 1)
        u = _dot(x, wu_ref[...], 1, 1)
        g_ref[...] = g.astype(BF16)
        u_ref[...] = u.astype(BF16)
        h_ref[...] = (g * _sigmoid(g) * u).astype(BF16)

    w_spec = pl.BlockSpec((tn, k), lambda j, i: (j, 0))
    o_spec = pl.BlockSpec((tm, tn), lambda j, i: (i, j))
    return _call(body, name="ffn_in", grid=(f // tn, m // tm),
                 in_specs=[pl.BlockSpec((tm, k), lambda j, i: (i, 0)), w_spec, w_spec], out_specs=[o_spec] * 3,
                 out_shape=[_sds((m, f), BF16)] * 3, semantics=("parallel", "parallel"))(a, w_gate, w_up)


def _ffn_down_bwd(d_out, w_down, gate, up, after, tm=1024):
    m, k = d_out.shape
    f = w_down.shape[0]
    tm, tn = min(tm, m), _hidden_tile(f)

    def body(d_ref, w_ref, g_ref, u_ref, dg_ref, du_ref):
        rows = pl.ds(pl.multiple_of(pl.program_id(1) * tm, tm), tm)
        dh = _dot(d_ref[rows, :], w_ref[...], 1, 1)
        g = g_ref[...].astype(F32)
        sg = _sigmoid(g)
        dg_ref[...] = (dh * u_ref[...].astype(F32) * sg * (1.0 + g * (1.0 - sg))).astype(BF16)
        du_ref[...] = (dh * g * sg).astype(BF16)

    t_spec = pl.BlockSpec((tm, tn), lambda j, i: (i, j))
    return _call(body, name="ffn_down_dx", grid=(f // tn, m // tm),
                 in_specs=[pl.BlockSpec((m, k), lambda j, i: (0, 0)), pl.BlockSpec((tn, k), lambda j, i: (j, 0)),
                           t_spec, t_spec],
                 out_specs=[t_spec] * 2, out_shape=[_sds((m, f), BF16)] * 2, semantics=("parallel", "parallel"),
                 n_after=len(after))(d_out, w_down, gate, up, *after)


ALL_PEERS = (1, 2, 3, 4, 5, 6, 7)
CHIP_PEERS = (2, 4, 6)
SIBLING = 1
OWN = 0


def _peer(relation):
    x, y, c = lax.axis_index("x"), lax.axis_index("y"), lax.axis_index("c")
    pos = (1 - x if relation & 4 else x, 1 - y if relation & 2 else y, 1 - c if relation & 1 else c)
    return pos, 4 * pos[0] + 2 * pos[1] + pos[2]


def _slot(relation, by_chip):
    pos, device = _peer(relation)
    return 2 * pos[0] + pos[1] if by_chip else device


def _exchange_copies(ins, lands, send_sems, recv_sems, scatter, relations, by_chip=False):
    me = _slot(0, by_chip)

    def copy(a, s, peer, pos, dst_slot):
        return pltpu.make_async_remote_copy(
            src_ref=ins[a].at[peer] if scatter else ins[a], dst_ref=lands[a].at[dst_slot],
            send_sem=send_sems.at[s], recv_sem=recv_sems.at[s], device_id=pos, device_id_type=pl.DeviceIdType.MESH)

    pairs = []
    for k, r in enumerate(relations):
        pos, peer = _peer(r)[0], _slot(r, by_chip)
        for a in range(len(ins)):
            s = a * len(relations) + k
            pairs.append((copy(a, s, peer, pos, me), copy(a, s, peer, pos, peer)))
    return pairs


def _halves_copies(arrays, lands, send_sems, recv_sems):
    sibling, _ = _peer(SIBLING)
    core = lax.axis_index("c")
    pairs = []
    for a, (ref, land) in enumerate(zip(arrays, lands)):
        send = pltpu.make_async_remote_copy(
            src_ref=ref.at[:, pl.ds(1 - core, 1)], dst_ref=land, send_sem=send_sems.at[a], recv_sem=recv_sems.at[a],
            device_id=sibling, device_id_type=pl.DeviceIdType.MESH)
        pairs.append((send, send))
    return pairs


def _forward_copies(lands, send_sems, recv_sems):
    sibling, _ = _peer(SIBLING)

    def copy(a, s, slot):
        return pltpu.make_async_remote_copy(
            src_ref=lands[a].at[slot], dst_ref=lands[a].at[slot], send_sem=send_sems.at[s], recv_sem=recv_sems.at[s],
            device_id=sibling, device_id_type=pl.DeviceIdType.MESH)

    pairs = []
    for k, r in enumerate(CHIP_PEERS):
        _, mine = _peer(r)
        _, theirs = _peer(r | SIBLING)
        for a in range(len(lands)):
            s = a * len(CHIP_PEERS) + k
            pairs.append((copy(a, s, mine), copy(a, s, theirs)))
    return pairs


_HBM_SPEC = pl.BlockSpec(memory_space=pltpu.HBM)
_SEM_SPEC = pl.BlockSpec(memory_space=pltpu.SEMAPHORE)
_SIDE_EFFECT = pltpu.SideEffectType.DATAFLOW_SIDE_EFFECTING


def _split_start(name, operands, n_sem, make_pairs):
    k = len(operands)

    def body(*refs):
        send_sems, recv_sems, token = refs[k], refs[k + 1], refs[-1]
        for send, _ in make_pairs(refs[:k], send_sems, recv_sems):
            send.start()
        token[...] = jnp.zeros_like(token)

    out = pl.pallas_call(
        body, name=name,
        out_shape=(pltpu.SemaphoreType.DMA((n_sem,)), pltpu.SemaphoreType.DMA((n_sem,)),
                   *[pltpu.HBM(a.shape, a.dtype) for a in operands], _sds((SUBLANES, LANES), F32)),
        in_specs=[_HBM_SPEC] * k,
        out_specs=(_SEM_SPEC, _SEM_SPEC, *[_HBM_SPEC] * k, pl.BlockSpec(memory_space=pltpu.VMEM)),
        input_output_aliases={i: 2 + i for i in range(k)},
        compiler_params=pltpu.CompilerParams(has_side_effects=_SIDE_EFFECT),
    )(*[pltpu.with_memory_space_constraint(a, pltpu.HBM) for a in operands])
    return dict(name=name, sems=out[:2], thru=list(out[2:2 + k]), make_pairs=make_pairs), out[-1]


def _split_wait(handle, after):
    thru, make_pairs = handle["thru"], handle["make_pairs"]
    k = len(thru)

    def body(*refs):
        for send, arrival in make_pairs(refs[:k], refs[k], refs[k + 1]):
            send.wait_send()
            arrival.wait_recv()

    return pl.pallas_call(
        body, name=handle["name"] + "_wait", out_shape=[pltpu.HBM(a.shape, a.dtype) for a in thru],
        in_specs=[_HBM_SPEC] * k + [_SEM_SPEC, _SEM_SPEC] + [pl.BlockSpec(memory_space=pl.ANY)] * len(after),
        out_specs=[_HBM_SPEC] * k, input_output_aliases={i: i for i in range(k)},
        compiler_params=pltpu.CompilerParams(has_side_effects=_SIDE_EFFECT),
    )(*thru, *handle["sems"], *after)


def _exchange_start(name, arrays, scatter, relations, by_chip=False):
    n = len(arrays)
    lands = [lax.empty(a.shape if scatter else (N_DEV,) + a.shape, a.dtype) for a in arrays]

    def make_pairs(refs, send_sems, recv_sems):
        return _exchange_copies(refs[:n], refs[n:], send_sems, recv_sems, scatter, relations, by_chip)

    handle, token = _split_start(name, list(arrays) + lands, n * len(relations), make_pairs)
    handle.update(n=n)
    return handle, token


def _halves_start(name, arrays):
    lands = [lax.empty((a.shape[0], 1) + a.shape[2:], a.dtype) for a in arrays]
    n = len(arrays)

    def make_pairs(refs, send_sems, recv_sems):
        return _halves_copies(refs[:n], refs[n:], send_sems, recv_sems)

    return _split_start(name, list(arrays) + lands, n, make_pairs)


def _chip_sum(name, array, landed):
    chips, _, r, c = array.shape
    tr = r

    def body(a_ref, b_ref, o_ref):
        mine = a_ref[lax.axis_index("c")].astype(F32)
        o_ref[...] = (mine + b_ref[...].astype(F32)).astype(o_ref.dtype)

    return _call(body, name=name, grid=(chips, r // tr),
                 in_specs=[pl.BlockSpec((None, 2, tr, c), lambda k, i: (k, 0, i, 0)),
                           pl.BlockSpec((None, None, tr, c), lambda k, i: (k, 0, i, 0))],
                 out_specs=pl.BlockSpec((None, tr, c), lambda k, i: (k, i, 0)),
                 out_shape=_sds((chips, r, c), BF16), semantics=("parallel", "parallel"))(array, landed)


def _forward_start(name, lands):
    return _split_start(name, list(lands), len(lands) * len(CHIP_PEERS), _forward_copies)


def _exchange_wait(handle, after):
    return _split_wait(handle, after)[handle["n"]:]


def _rope_tables(pos_col):
    t = pos_col.shape[0]
    half = HEAD_DIM // 2
    inv_freq = ROPE_THETA ** (-jnp.arange(half, dtype=F32) / half)
    inv_row = jnp.tile(inv_freq, LANES // half)[None, :]

    def body(pos_ref, inv_ref, cos_ref, sin_ref):
        ang = pos_ref[...] * inv_ref[...]
        cos_ref[...] = jnp.cos(ang)
        sin_ref[...] = jnp.sin(ang)

    tm = min(t, 512)
    return _call(body, name="rope_tables", grid=(t // tm,),
                 in_specs=[pl.BlockSpec((tm, 1), lambda i: (i, 0)), pl.BlockSpec((1, LANES), lambda i: (0, 0))],
                 out_specs=[pl.BlockSpec((tm, LANES), lambda i: (i, 0))] * 2,
                 out_shape=[_sds((t, LANES), F32)] * 2, semantics=("parallel",))(pos_col, inv_row)


def _rot_half(x):
    lane = lax.broadcasted_iota(jnp.int32, x.shape, 1)
    low = (lane % HEAD_DIM) < HEAD_DIM // 2
    return jnp.where(low, -pltpu.roll(x, LANES - HEAD_DIM // 2, 1), pltpu.roll(x, HEAD_DIM // 2, 1))


def _rope(x, cos, sin):
    return x * cos + _rot_half(x) * sin


def _unrope(d, cos, sin):
    return d * cos - _rot_half(d) * sin


def _band_mask(first_block, heads):
    r = lax.broadcasted_iota(jnp.int32, (heads * BLOCK, 2 * BLOCK), 0) % BLOCK
    c = lax.broadcasted_iota(jnp.int32, (heads * BLOCK, 2 * BLOCK), 1)
    diff = r - c + BLOCK
    return (diff >= 0) & (diff < WINDOW) & ((c >= BLOCK) | jnp.logical_not(first_block))


def _attn_specs(t, d_attn, d_in):
    kb, vb = d_attn // D_KV, d_attn // D_KV + 1
    prev = lambda i: jnp.maximum(i - 1, 0)
    return [
        pl.BlockSpec((BLOCK, d_attn), lambda i: (i, 0)),
        pl.BlockSpec((BLOCK, D_KV), lambda i: (i, kb)),
        pl.BlockSpec((BLOCK, D_KV), lambda i: (i, vb)),
        pl.BlockSpec((BLOCK, D_KV), lambda i: (prev(i), kb)),
        pl.BlockSpec((BLOCK, D_KV), lambda i: (prev(i), vb)),
        pl.BlockSpec((BLOCK, LANES), lambda i: (i, 0)),
        pl.BlockSpec((BLOCK, LANES), lambda i: (i, 0)),
        pl.BlockSpec((BLOCK, LANES), lambda i: (prev(i), 0)),
        pl.BlockSpec((BLOCK, LANES), lambda i: (prev(i), 0)),
        pl.BlockSpec((1, LANES), lambda i: (0, 0)),
    ]


def _head(x, h):
    return x[:, h * HEAD_DIM:(h + 1) * HEAD_DIM]


def _attn_heads(q_ref, kc_ref, vc_ref, kp_ref, vp_ref, cq_ref, sq_ref, cp_ref, sp_ref, d_attn):
    cq, sq, cp, sp = cq_ref[...], sq_ref[...], cp_ref[...], sp_ref[...]
    q_rot = [_rope(q_ref[:, j * LANES:(j + 1) * LANES], cq, sq) for j in range(d_attn // LANES)]
    kc_rot = [_rope(kc_ref[:, j * LANES:(j + 1) * LANES], cq, sq) for j in range(D_KV // LANES)]
    kp_rot = [_rope(kp_ref[:, j * LANES:(j + 1) * LANES], cp, sp) for j in range(D_KV // LANES)]
    per = LANES // HEAD_DIM
    q_heads = [_head(q_rot[h // per], h % per).astype(BF16) for h in range(d_attn // HEAD_DIM)]
    kk = [jnp.concatenate([_head(kp_rot[g // per], g % per), _head(kc_rot[g // per], g % per)], axis=0).astype(BF16)
          for g in range(N_KV_HEADS)]
    vv = [jnp.concatenate([_head(vp_ref[...], g), _head(vc_ref[...], g)], axis=0).astype(BF16) for g in range(N_KV_HEADS)]
    return q_heads, kk, vv


def _stack_group(q_heads, sink_ref, group):
    q_all = jnp.concatenate([q_heads[h] for h in group], axis=0)
    sink_all = jnp.concatenate([jnp.broadcast_to(sink_ref[:, h:h + 1], (BLOCK, 1)) for h in group], axis=0)
    return q_all, sink_all


def _softmax_with_sink(q, kk, sink, mask):
    s = _dot(q, kk, 1, 1) * (1.0 / math.sqrt(HEAD_DIM))
    s = jnp.where(mask, s, MASKED)
    m = jnp.maximum(jnp.max(s, axis=-1, keepdims=True), sink)
    p = jnp.exp(s - m)
    e_sink = jnp.exp(sink - m)
    inv = 1.0 / (jnp.sum(p, axis=-1, keepdims=True) + e_sink)
    return p * inv, e_sink * inv


def _attention_fwd(proj, cos, sin, sinks_row, d_attn):
    t, d_in = proj.shape
    n_heads = d_attn // HEAD_DIM
    q_per_kv = n_heads // N_KV_HEADS

    def body(q_ref, kc_ref, vc_ref, kp_ref, vp_ref, cq_ref, sq_ref, cp_ref, sp_ref, sink_ref, o_ref):
        mask = _band_mask(pl.program_id(0) == 0, q_per_kv)
        q_heads, kk, vv = _attn_heads(q_ref, kc_ref, vc_ref, kp_ref, vp_ref, cq_ref, sq_ref, cp_ref, sp_ref, d_attn)
        for g in range(N_KV_HEADS):
            group = range(g * q_per_kv, (g + 1) * q_per_kv)
            q_all, sink_all = _stack_group(q_heads, sink_ref, group)
            probs, _ = _softmax_with_sink(q_all, kk[g], sink_all, mask)
            o_all = _dot(probs.astype(BF16), vv[g], 1, 0)
            for k, h in enumerate(group):
                o_ref[:, h * HEAD_DIM:(h + 1) * HEAD_DIM] = o_all[k * BLOCK:(k + 1) * BLOCK]

    return _call(body, name="attention_fwd", grid=(t // BLOCK,), in_specs=_attn_specs(t, d_attn, d_in),
                 out_specs=pl.BlockSpec((BLOCK, d_attn), lambda i: (i, 0)), out_shape=_sds((t, d_attn), F32),
                 semantics=("parallel",))(proj, proj, proj, proj, proj, cos, sin, cos, sin, sinks_row)


def _attention_bwd(proj, cos, sin, sinks_row, d_out, d_attn):
    t, d_in = proj.shape
    n_heads = d_attn // HEAD_DIM
    q_per_kv = n_heads // N_KV_HEADS
    nb = t // BLOCK
    per = LANES // HEAD_DIM
    stack = q_per_kv

    def body(q_ref, kc_ref, vc_ref, kp_ref, vp_ref, cq_ref, sq_ref, cp_ref, sp_ref, sink_ref, do_ref,
             dq_ref, dk_ref, dv_ref, dsink_ref):
        i = pl.program_id(0)
        mask = _band_mask(i == 0, stack)
        q_heads, kk, vv = _attn_heads(q_ref, kc_ref, vc_ref, kp_ref, vp_ref, cq_ref, sq_ref, cp_ref, sp_ref, d_attn)
        lane = lax.broadcasted_iota(jnp.int32, (1, LANES), 1)
        dsink = jnp.zeros((1, LANES), F32)
        dq_rot, dkk, dvv = [], [], []
        for g in range(N_KV_HEADS):
            dkk_g = jnp.zeros((2 * BLOCK, HEAD_DIM), F32)
            dvv_g = jnp.zeros((2 * BLOCK, HEAD_DIM), F32)
            for first in range(g * q_per_kv, (g + 1) * q_per_kv, stack):
                group = range(first, first + stack)
                q_all, sink_all = _stack_group(q_heads, sink_ref, group)
                probs, p_sink = _softmax_with_sink(q_all, kk[g], sink_all, mask)
                do_all = jnp.concatenate([do_ref[:, h * HEAD_DIM:(h + 1) * HEAD_DIM] for h in group],
                                         axis=0).astype(BF16)
                dp = _dot(do_all, vv[g], 1, 1)
                delta = jnp.sum(probs * dp, axis=-1, keepdims=True)
                ds = (probs * (dp - delta) * (1.0 / math.sqrt(HEAD_DIM))).astype(BF16)
                dq_all = _dot(ds, kk[g], 1, 0)
                dkk_g += _dot(ds, q_all, 0, 0)
                dvv_g += _dot(probs.astype(BF16), do_all, 0, 0)
                sink_term = p_sink * delta
                for k, h in enumerate(group):
                    dq_rot.append(dq_all[k * BLOCK:(k + 1) * BLOCK])
                    part = jnp.sum(sink_term[k * BLOCK:(k + 1) * BLOCK], axis=0, keepdims=True)
                    dsink += jnp.where(lane == h, -part, 0.0)
            dkk.append(dkk_g)
            dvv.append(dvv_g)
        cq, sq, cp, sp = cq_ref[...], sq_ref[...], cp_ref[...], sp_ref[...]
        for j in range(d_attn // LANES):
            d = jnp.concatenate(dq_rot[j * per:(j + 1) * per], axis=1)
            dq_ref[:, j * LANES:(j + 1) * LANES] = _unrope(d, cq, sq)
        for j in range(D_KV // LANES):
            d = jnp.concatenate(dkk[j * per:(j + 1) * per], axis=1)
            dk_ref[0, :, j * LANES:(j + 1) * LANES] = _unrope(d[:BLOCK], cp, sp)
            dk_ref[1, :, j * LANES:(j + 1) * LANES] = _unrope(d[BLOCK:], cq, sq)
            d = jnp.concatenate(dvv[j * per:(j + 1) * per], axis=1)
            dv_ref[0, :, j * LANES:(j + 1) * LANES] = d[:BLOCK]
            dv_ref[1, :, j * LANES:(j + 1) * LANES] = d[BLOCK:]

        @pl.when(i == 0)
        def _():
            dsink_ref[...] = jnp.zeros_like(dsink_ref)

        dsink_ref[...] += dsink

    pair = pl.BlockSpec((2, BLOCK, D_KV), lambda i: (i, 0, 0))
    return _call(body, name="attention_bwd", grid=(nb,),
                 in_specs=_attn_specs(t, d_attn, d_in) + [pl.BlockSpec((BLOCK, d_attn), lambda i: (i, 0))],
                 out_specs=[pl.BlockSpec((BLOCK, d_attn), lambda i: (i, 0)), pair, pair,
                            pl.BlockSpec((1, LANES), lambda i: (0, 0))],
                 out_shape=[_sds((t, d_attn), F32), _sds((2 * nb, BLOCK, D_KV), F32), _sds((2 * nb, BLOCK, D_KV), F32),
                            _sds((1, LANES), F32)],
                 semantics=("arbitrary",))(proj, proj, proj, proj, proj, cos, sin, cos, sin, sinks_row, d_out)


def _assemble_dproj(dq, dk2, dv2, du, d_in, after):
    t, d_attn = dq.shape
    d_ssm = du.shape[1]
    nb = t // BLOCK

    def body(dq_ref, dk_own, dk_next, dv_own, dv_next, du_ref, o_ref):
        has_next = (pl.program_id(0) < nb - 1).astype(F32)
        o_ref[:, :d_attn] = dq_ref[...].astype(BF16)
        o_ref[:, d_attn:d_attn + D_KV] = (dk_own[...] + has_next * dk_next[...]).astype(BF16)
        o_ref[:, d_attn + D_KV:d_attn + 2 * D_KV] = (dv_own[...] + has_next * dv_next[...]).astype(BF16)
        o_ref[:, d_attn + 2 * D_KV:] = du_ref[...].astype(BF16)

    own = pl.BlockSpec((None, BLOCK, D_KV), lambda i: (2 * i + 1, 0, 0))
    nxt = pl.BlockSpec((None, BLOCK, D_KV), lambda i: (jnp.minimum(2 * i + 2, 2 * nb - 1), 0, 0))
    return _call(body, name="assemble_dproj", grid=(nb,),
                 in_specs=[pl.BlockSpec((BLOCK, d_attn), lambda i: (i, 0)), own, nxt, own, nxt,
                           pl.BlockSpec((BLOCK, d_ssm), lambda i: (i, 0))],
                 out_specs=pl.BlockSpec((BLOCK, d_in), lambda i: (i, 0)), out_shape=_sds((t, d_in), BF16),
                 semantics=("parallel",), n_after=len(after))(dq, dk2, dk2, dv2, dv2, du, *after)


def _discretise(ar, ai, ldt, br, bi):
    dt = jnp.exp(ldt)
    mag = jnp.exp(ar * dt)
    lam_re = mag * jnp.cos(ai * dt)
    lam_im = mag * jnp.sin(ai * dt)
    den = ar * ar + ai * ai
    nr = lam_re - 1.0
    ni = lam_im
    f_re = (nr * ar + ni * ai) / den
    f_im = (ni * ar - nr * ai) / den
    return (lam_re, lam_im, [f_re * r - f_im * i for r, i in zip(br, bi)], [f_re * i + f_im * r for r, i in zip(br, bi)])


def _whole(arrays):
    return [pl.BlockSpec(a.shape, lambda *_, nd=len(a.shape): (0,) * nd) for a in arrays]


def _channels(ref):
    groups = ref.shape[0] // SSM_GROUP
    return [ref[pl.ds(p, groups, stride=SSM_GROUP), :] for p in range(SSM_GROUP)]


def _store_channels(ref, values):
    groups = ref.shape[0] // SSM_GROUP
    for p, val in enumerate(values):
        ref[pl.ds(p, groups, stride=SSM_GROUP), :] = val


def _s5_discretise(ar, ai, ldt, br, bi):
    ins = [ar, ai, ldt, br, bi]

    def body(ar_ref, ai_ref, ldt_ref, br_ref, bi_ref, lr_ref, li_ref, bbr_ref, bbi_ref):
        lr, li, bbr, bbi = _discretise(ar_ref[...], ai_ref[...], ldt_ref[...], _channels(br_ref), _channels(bi_ref))
        lr_ref[...] = lr
        li_ref[...] = li
        _store_channels(bbr_ref, bbr)
        _store_channels(bbi_ref, bbi)

    outs = [_sds(ar.shape, F32), _sds(ar.shape, F32), _sds(br.shape, F32), _sds(br.shape, F32)]
    return _call(body, name="s5_discretise", in_specs=_whole(ins), out_specs=_whole(outs), out_shape=outs)(*ins)


def _s5_discretise_bwd(ar, ai, ldt, br, bi, d_lr, d_li, d_bbr, d_bbi):
    ins = [ar, ai, ldt, br, bi, d_lr, d_li, d_bbr, d_bbi]

    def body(ar_ref, ai_ref, ldt_ref, br_ref, bi_ref, dlr_ref, dli_ref, dbbr_ref, dbbi_ref,
             dar_ref, dai_ref, dldt_ref, dbr_ref, dbi_ref):
        _, vjp = jax.vjp(_discretise, ar_ref[...], ai_ref[...], ldt_ref[...], _channels(br_ref), _channels(bi_ref))
        dar, dai, dldt, dbr, dbi = vjp((dlr_ref[...], dli_ref[...], _channels(dbbr_ref), _channels(dbbi_ref)))
        dar_ref[...] = dar
        dai_ref[...] = dai
        dldt_ref[...] = dldt
        _store_channels(dbr_ref, dbr)
        _store_channels(dbi_ref, dbi)

    outs = [_sds(a.shape, F32) for a in (ar, ai, ldt, br, bi)]
    return _call(body, name="s5_discretise_bwd", in_specs=_whole(ins), out_specs=_whole(outs), out_shape=outs)(*ins)


def _cmul(ar, ai, br, bi):
    return ar * br - ai * bi, ar * bi + ai * br


def _load_segmented(ref, tile0, n_tiles, seg):
    return jnp.concatenate([ref[pl.ds(tile0 + j, SUBLANES, stride=seg), :] for j in range(n_tiles)], axis=0)


def _store_segmented(ref, tile0, seg, value):
    for j in range(value.shape[0] // SUBLANES):
        ref[pl.ds(tile0 + j, SUBLANES, stride=seg), :] = value[j * SUBLANES:(j + 1) * SUBLANES, :]


def _fill_powers(lr, li, pr_ref, pi_ref, seg):
    pows = [(lr, li)]
    for _ in range(SUBLANES - 1):
        pows.append(_cmul(pows[-1][0], pows[-1][1], lr, li))
    row = lax.broadcasted_iota(jnp.int32, (SUBLANES, lr.shape[1]), 0)
    tr = jnp.zeros((SUBLANES, lr.shape[1]), F32)
    ti = jnp.zeros((SUBLANES, lr.shape[1]), F32)
    for r in range(SUBLANES):
        tr = jnp.where(row == r, pows[r][0], tr)
        ti = jnp.where(row == r, pows[r][1], ti)
    pr_ref[0:SUBLANES, :] = tr
    pi_ref[0:SUBLANES, :] = ti
    k = SUBLANES
    while k < seg:
        fr, fi = pr_ref[k - 1:k, :], pi_ref[k - 1:k, :]
        for t0 in range(0, k, SUBLANES):
            nr, ni = _cmul(pr_ref[t0:t0 + SUBLANES, :], pi_ref[t0:t0 + SUBLANES, :], fr, fi)
            pr_ref[k + t0:k + t0 + SUBLANES, :] = nr
            pi_ref[k + t0:k + t0 + SUBLANES, :] = ni
        k *= 2


def _scan_segments(sr_ref, si_ref, pr_ref, pi_ref, lr, li, seg, reverse, per_tile=None):
    w = lr.shape[1]
    sign = -1.0 if reverse else 1.0
    lrb = jnp.broadcast_to(lr, (SUBLANES, w))
    lib = jnp.broadcast_to(sign * li, (SUBLANES, w))
    zero = jnp.zeros((SUBLANES, w), F32)

    def tile_rows(j):
        return pl.ds(pl.multiple_of(j * SUBLANES, SUBLANES), SUBLANES)

    steps = 4 if seg % 4 == 0 else 1

    def local(i, carry):
        for u in range(steps):
            j = i * steps + u
            rows = tile_rows(seg - 1 - j if reverse else j)
            pr, pi = _cmul(lrb, lib, carry[0], carry[1])
            carry = (sr_ref[rows, :] + pr, si_ref[rows, :] + pi)
            sr_ref[rows, :] = carry[0]
            si_ref[rows, :] = carry[1]
        return carry

    end_r, end_i = lax.fori_loop(0, seg // steps, local, (zero, zero))
    full_r, full_i = pr_ref[seg - 1:seg, :], sign * pi_ref[seg - 1:seg, :]
    row = lax.broadcasted_iota(jnp.int32, (SUBLANES, w), 0)
    in_r, in_i = zero, zero
    cur_r, cur_i = jnp.zeros((1, w), F32), jnp.zeros((1, w), F32)
    for r in (range(SUBLANES - 2, -1, -1) if reverse else range(1, SUBLANES)):
        src = r + 1 if reverse else r - 1
        pr, pi = _cmul(full_r, full_i, cur_r, cur_i)
        cur_r, cur_i = end_r[src:src + 1, :] + pr, end_i[src:src + 1, :] + pi
        in_r = jnp.where(row == r, cur_r, in_r)
        in_i = jnp.where(row == r, cur_i, in_i)

    def carry_in(j, _):
        rows = tile_rows(j)
        k = seg - 1 - j if reverse else j
        pr, pi = _cmul(pr_ref[pl.ds(k, 1), :], sign * pi_ref[pl.ds(k, 1), :], in_r, in_i)
        xr, xi = sr_ref[rows, :] + pr, si_ref[rows, :] + pi
        sr_ref[rows, :] = xr
        si_ref[rows, :] = xi
        if per_tile is not None:
            per_tile(j, xr, xi)
        return 0

    lax.fori_loop(0, seg, carry_in, 0, unroll=4)


_S5_ROWS = 2048


def _s5_in_specs(t, d_attn):
    u_block = (d_attn + 2 * D_KV) // SSM_CH_BLOCK
    blk3 = lambda shape: pl.BlockSpec((None,) + shape, lambda j: (j, 0, 0))
    return [
        pl.BlockSpec((t, SSM_CH_BLOCK), lambda j: (0, u_block + j)),
        blk3((SSM_CH_BLOCK, SSM_ST_BLOCK)), blk3((SSM_CH_BLOCK, SSM_ST_BLOCK)),
        blk3((1, SSM_ST_BLOCK)), blk3((1, SSM_ST_BLOCK)),
        blk3((SSM_ST_BLOCK, SSM_CH_BLOCK)), blk3((SSM_ST_BLOCK, SSM_CH_BLOCK)),
        pl.BlockSpec((1, SSM_CH_BLOCK), lambda j: (0, j)),
    ]


def _chunks(t):
    rows = min(_S5_ROWS, t)
    return rows, lambda i: pl.ds(pl.multiple_of(i * rows, rows), rows)


def _s5_states(u_ref, us_ref, bre_ref, bim_ref, lr_ref, li_ref, sr_ref, si_ref, pr_ref, pi_ref, t):
    seg = t // SUBLANES
    rows, chunk = _chunks(t)
    for c in range(t // rows):
        us_ref[c * rows:(c + 1) * rows, :] = _load_segmented(u_ref, c * rows // SUBLANES, rows // SUBLANES, seg)

    def fill(i, _):
        ub = us_ref[chunk(i), :].astype(BF16)
        sr_ref[chunk(i), :] = _dot(ub, bre_ref[...], 1, 0)
        si_ref[chunk(i), :] = _dot(ub, bim_ref[...], 1, 0)
        return 0

    lax.fori_loop(0, t // rows, fill, 0)
    _fill_powers(lr_ref[...], li_ref[...], pr_ref, pi_ref, seg)
    _scan_segments(sr_ref, si_ref, pr_ref, pi_ref, lr_ref[...], li_ref[...], seg, False)


def _s5_scratch(t):
    state = pltpu.VMEM((t, SSM_ST_BLOCK), F32)
    powers = pltpu.VMEM((t // SUBLANES, SSM_ST_BLOCK), F32)
    return state, powers, pltpu.VMEM((t, SSM_CH_BLOCK), F32)


def _s5_fwd(proj, mats, dskip_row, d_attn, d_ssm):
    t = proj.shape[0]
    seg = t // SUBLANES
    n_blocks = d_ssm // SSM_CH_BLOCK
    rows, chunk = _chunks(t)

    def body(u_ref, bre_ref, bim_ref, lr_ref, li_ref, cre_ref, cim_ref, d_ref, y_ref,
             sr_ref, si_ref, pr_ref, pi_ref, us_ref, ys_ref):
        _s5_states(u_ref, us_ref, bre_ref, bim_ref, lr_ref, li_ref, sr_ref, si_ref, pr_ref, pi_ref, t)

        def emit(i, _):
            ys_ref[chunk(i), :] = (_dot(sr_ref[chunk(i), :].astype(BF16), cre_ref[...], 1, 0)
                                   - _dot(si_ref[chunk(i), :].astype(BF16), cim_ref[...], 1, 0)
                                   + d_ref[...] * us_ref[chunk(i), :])
            return 0

        lax.fori_loop(0, t // rows, emit, 0)
        for c in range(t // rows):
            _store_segmented(y_ref, c * rows // SUBLANES, seg, ys_ref[c * rows:(c + 1) * rows, :])

    state, powers, channels = _s5_scratch(t)
    col = pl.BlockSpec((t, SSM_CH_BLOCK), lambda j: (0, j))
    return _call(body, name="s5_fwd", grid=(n_blocks,), in_specs=_s5_in_specs(t, d_attn), out_specs=col,
                 out_shape=_sds((t, d_ssm), F32), scratch_shapes=[state, state, powers, powers, channels, channels],
                 semantics=("parallel",))(proj, *mats, dskip_row)


def _s5_bwd(proj, mats, dskip_row, y, dz_a, dz_b, d_attn, d_ssm, after):
    t = proj.shape[0]
    seg = t // SUBLANES
    n_blocks = d_ssm // SSM_CH_BLOCK
    rows, chunk = _chunks(t)

    def body(u_ref, bre_ref, bim_ref, lr_ref, li_ref, cre_ref, cim_ref, d_ref, y_ref, dza_ref, dzb_ref,
             du_ref, dbre_ref, dbim_ref, dlr_ref, dli_ref, dcre_ref, dcim_ref, dd_ref,
             sr_ref, si_ref, gr_ref, gi_ref, pr_ref, pi_ref, us_ref, dys_ref, dus_ref, acc_r, acc_i):
        _s5_states(u_ref, us_ref, bre_ref, bim_ref, lr_ref, li_ref, sr_ref, si_ref, pr_ref, pi_ref, t)
        for ref in (dcre_ref, dcim_ref, dbre_ref, dbim_ref, dd_ref, acc_r, acc_i):
            ref[...] = jnp.zeros_like(ref)
        for c in range(t // rows):
            tile0, n_tiles = c * rows // SUBLANES, rows // SUBLANES
            dz = _load_segmented(dza_ref, tile0, n_tiles, seg) + _load_segmented(dzb_ref, tile0, n_tiles, seg)
            dys_ref[c * rows:(c + 1) * rows, :] = dz * _gelu_grad(_load_segmented(y_ref, tile0, n_tiles, seg))

        def through_c(i, _):
            dy = dys_ref[chunk(i), :]
            dd_ref[...] += jnp.sum(dy * us_ref[chunk(i), :], axis=0, keepdims=True)
            dyb = dy.astype(BF16)
            gr_ref[chunk(i), :] = _dot(dyb, cre_ref[...], 1, 1)
            gi_ref[chunk(i), :] = -_dot(dyb, cim_ref[...], 1, 1)
            dcre_ref[...] += _dot(sr_ref[chunk(i), :].astype(BF16), dyb, 0, 0)
            dcim_ref[...] -= _dot(si_ref[chunk(i), :].astype(BF16), dyb, 0, 0)
            return 0

        lax.fori_loop(0, t // rows, through_c, 0)

        row = lax.broadcasted_iota(jnp.int32, (SUBLANES, SSM_ST_BLOCK), 0)
        last = pl.ds((seg - 1) * SUBLANES, SUBLANES)
        wrap = [jnp.where(row == 0, 0.0, pltpu.roll(ref[last, :], 1, 0)) for ref in (sr_ref, si_ref)]

        def lambda_grad(j, g_re, g_im):
            before = pl.ds(pl.multiple_of(jnp.maximum(j - 1, 0) * SUBLANES, SUBLANES), SUBLANES)
            prev_r = jnp.where(j > 0, sr_ref[before, :], wrap[0])
            prev_i = jnp.where(j > 0, si_ref[before, :], wrap[1])
            acc_r[...] += g_re * prev_r + g_im * prev_i
            acc_i[...] += g_im * prev_r - g_re * prev_i

        _scan_segments(gr_ref, gi_ref, pr_ref, pi_ref, lr_ref[...], li_ref[...], seg, True, per_tile=lambda_grad)
        dlr_ref[...] = jnp.sum(acc_r[...], axis=0, keepdims=True)
        dli_ref[...] = jnp.sum(acc_i[...], axis=0, keepdims=True)

        def through_b(i, _):
            ub = us_ref[chunk(i), :].astype(BF16)
            grb, gib = gr_ref[chunk(i), :].astype(BF16), gi_ref[chunk(i), :].astype(BF16)
            dbre_ref[...] += _dot(ub, grb, 0, 0)
            dbim_ref[...] += _dot(ub, gib, 0, 0)
            dus_ref[chunk(i), :] = (_dot(grb, bre_ref[...], 1, 1) + _dot(gib, bim_ref[...], 1, 1)
                                    + d_ref[...] * dys_ref[chunk(i), :])
            return 0

        lax.fori_loop(0, t // rows, through_b, 0)
        for c in range(t // rows):
            _store_segmented(du_ref, c * rows // SUBLANES, seg, dus_ref[c * rows:(c + 1) * rows, :])

    col = pl.BlockSpec((t, SSM_CH_BLOCK), lambda j: (0, j))
    blk3 = lambda shape: pl.BlockSpec((None,) + shape, lambda j: (j, 0, 0))
    state, powers, channels = _s5_scratch(t)
    return _call(
        body, name="s5_bwd", grid=(n_blocks,), in_specs=_s5_in_specs(t, d_attn) + [col, col, col],
        out_specs=[col, blk3((SSM_CH_BLOCK, SSM_ST_BLOCK)), blk3((SSM_CH_BLOCK, SSM_ST_BLOCK)),
                   blk3((1, SSM_ST_BLOCK)), blk3((1, SSM_ST_BLOCK)),
                   blk3((SSM_ST_BLOCK, SSM_CH_BLOCK)), blk3((SSM_ST_BLOCK, SSM_CH_BLOCK)),
                   pl.BlockSpec((1, SSM_CH_BLOCK), lambda j: (0, j))],
        out_shape=[_sds((t, d_ssm), F32),
                   _sds((n_blocks, SSM_CH_BLOCK, SSM_ST_BLOCK), F32), _sds((n_blocks, SSM_CH_BLOCK, SSM_ST_BLOCK), F32),
                   _sds((n_blocks, 1, SSM_ST_BLOCK), F32), _sds((n_blocks, 1, SSM_ST_BLOCK), F32),
                   _sds((n_blocks, SSM_ST_BLOCK, SSM_CH_BLOCK), F32), _sds((n_blocks, SSM_ST_BLOCK, SSM_CH_BLOCK), F32),
                   _sds((1, d_ssm), F32)],
        scratch_shapes=[state, state, state, state, powers, powers, channels, channels, channels,
                        pltpu.VMEM((SUBLANES, SSM_ST_BLOCK), F32), pltpu.VMEM((SUBLANES, SSM_ST_BLOCK), F32)],
        semantics=("parallel",), n_after=len(after))(proj, *mats, dskip_row, y, dz_a, dz_b, *after)


def _by_block(gp_n):
    return gp_n.reshape(-1, GROUPS_PER_BLOCK, SSM_GROUP, SSM_STATE)


def _block_diag_in(bbar):
    eye = jnp.eye(GROUPS_PER_BLOCK, dtype=F32)
    return jnp.einsum("jgpn,gh->jgphn", _by_block(bbar), eye).reshape(-1, SSM_CH_BLOCK, SSM_ST_BLOCK)


def _block_diag_in_t(dense):
    d5 = dense.reshape(-1, GROUPS_PER_BLOCK, SSM_GROUP, GROUPS_PER_BLOCK, SSM_STATE)
    eye = jnp.eye(GROUPS_PER_BLOCK, dtype=F32)
    return jnp.einsum("jgphn,gh->jgpn", d5, eye).reshape(-1, SSM_STATE)


def _block_diag_out(c):
    eye = jnp.eye(GROUPS_PER_BLOCK, dtype=F32)
    return jnp.einsum("jgpn,gh->jgnhp", _by_block(c), eye).reshape(-1, SSM_ST_BLOCK, SSM_CH_BLOCK)


def _block_diag_out_t(dense):
    d5 = dense.reshape(-1, GROUPS_PER_BLOCK, SSM_STATE, GROUPS_PER_BLOCK, SSM_GROUP)
    eye = jnp.eye(GROUPS_PER_BLOCK, dtype=F32)
    return jnp.einsum("jgnhp,gh->jgpn", d5, eye).reshape(-1, SSM_STATE)


def _adamw(w, g, m, v):
    m = ADAM_B1 * m + (1.0 - ADAM_B1) * g
    v = ADAM_B2 * v + (1.0 - ADAM_B2) * (g * g)
    m_hat = m / (1.0 - ADAM_B1 ** ADAM_STEP)
    v_hat = v / (1.0 - ADAM_B2 ** ADAM_STEP)
    delta = -ADAM_LR * (m_hat / (jnp.sqrt(v_hat) + ADAM_EPS) + ADAM_WD * w)
    return delta, m, v


def _adam_sharded(name, parts, w, m, v, tr, row0=0):
    r, c = w.shape
    assert r % tr == 0 and row0 % tr == 0, (name, r, tr, row0)

    def body(p_ref, w_ref, m_ref, v_ref, g_out, d_out, m_out, v_out):
        g = p_ref[0].astype(F32)
        for i in range(1, p_ref.shape[0]):
            g = g + p_ref[i].astype(F32)
        delta, m_new, v_new = _adamw(w_ref[...], g, m_ref[...], v_ref[...])
        g_out[...] = g
        d_out[...] = delta
        m_out[...] = m_new
        v_out[...] = v_new

    tile = pl.BlockSpec((tr, c), lambda i: (i, 0))
    return _call(body, name=name, grid=(r // tr,),
                 in_specs=[pl.BlockSpec((parts.shape[0], tr, c), lambda i: (0, i + row0 // tr, 0)), tile, tile, tile],
                 out_specs=[tile] * 4, out_shape=[_sds((r, c), F32)] * 4, semantics=("parallel",))(parts, w, m, v)


_BIG = ("w_in", "w_glu", "w_o", "w_gate", "w_up", "w_down")
_BY_COLUMNS = ("w_in", "w_gate", "w_up")
_SMALL_VECTORS = ("sinks", "log_dt", "b_glu", "g_attn_out", "g_ssm_out", "g_post_mix", "g_pre_ffn", "g_post_ffn")
_SMALL_MATRICES = ("b_re", "b_im", "c_re", "c_im", "a_re", "a_im")
_ORDER = ("g_pre_mix", "w_in", "sinks", "a_re", "a_im", "log_dt", "b_re", "b_im", "c_re", "c_im", "d_skip", "w_glu",
          "b_glu", "g_attn_out", "g_ssm_out", "w_o", "g_post_mix", "g_pre_ffn", "w_gate", "w_up", "w_down",
          "g_post_ffn")


def _pack_grads(vectors, matrices):
    width = max(a.shape[1] for a in vectors)
    slots, row, lane = [], 0, 0
    for a in vectors:
        span = -(-a.shape[1] // LANES) * LANES
        if lane + span > width:
            row, lane = row + 1, 0
        slots.append((row, lane, a.shape[1]))
        lane += span
    firsts, at = [], 0
    for a in matrices:
        firsts.append(at)
        at += a.shape[0]
    nv = len(vectors)

    def body(*refs):
        vec_out, mat_out = refs[-2], refs[-1]
        vec_out[...] = jnp.zeros_like(vec_out)
        for ref, (r, l, w) in zip(refs[:nv], slots):
            vec_out[r:r + 1, l:l + w] = ref[...]
        for ref, r0 in zip(refs[nv:-2], firsts):
            mat_out[r0:r0 + ref.shape[0], :] = ref[...]

    ins = list(vectors) + list(matrices)
    outs = [_sds((-(-(row + 1) // SUBLANES) * SUBLANES, width), F32), _sds((at, matrices[0].shape[1]), F32)]
    vec_pack, mat_pack = _call(body, name="pack_small_grads", in_specs=_whole(ins), out_specs=_whole(outs),
                               out_shape=outs)(*ins)
    return vec_pack, slots, mat_pack, firsts


def _adam_replicated(sources, found_at, w, m, v, total_at):
    ns, n = len(sources), len(w)

    def body(*refs):
        ins, outs = refs[ns:ns + 3 * n], refs[ns + 3 * n:]
        summed = []
        for p_ref in refs[:ns]:
            g = p_ref[0]
            for k in range(1, N_DEV):
                g = g + p_ref[k]
            summed.append(g)
        for i, (src, row, lane) in enumerate(found_at):
            w_ref, m_ref, v_ref = ins[i], ins[n + i], ins[2 * n + i]
            rows, cols = w_ref.shape
            g = summed[src][row:row + rows, lane:lane + cols]
            delta, m_new, v_new = _adamw(w_ref[...], g, m_ref[...], v_ref[...])
            for o, val in zip(outs[4 * i:4 * i + 4], (g, delta, m_new, v_new)):
                o[...] = val
        t_src, t_row, t_lane, t_width = total_at
        outs[-1][...] = summed[t_src][t_row:t_row + 1, t_lane:t_lane + t_width]

    ins = list(sources) + list(w) + list(m) + list(v)
    outs = [_sds(a.shape, F32) for a in w for _ in range(4)] + [_sds((1, total_at[3]), F32)]
    flat = _call(body, name="adam_replicated", in_specs=_whole(ins), out_specs=_whole(outs), out_shape=outs)(*ins)
    return [tuple(flat[4 * i:4 * i + 4]) for i in range(n)], flat[-1]


def kernel(x, positions, g_pre_mix, w_in, sinks, a_re, a_im, log_dt, b_re, b_im, c_re, c_im, d_skip, w_glu, b_glu, g_attn_out, g_ssm_out, w_o, g_post_mix, g_pre_ffn, w_gate, w_up, w_down, g_post_ffn, loss_target, m_g_pre_mix, m_w_in, m_sinks, m_a_re, m_a_im, m_log_dt, m_b_re, m_b_im, m_c_re, m_c_im, m_d_skip, m_w_glu, m_b_glu, m_g_attn_out, m_g_ssm_out, m_w_o, m_g_post_mix, m_g_pre_ffn, m_w_gate, m_w_up, m_w_down, m_g_post_ffn, v_g_pre_mix, v_w_in, v_sinks, v_a_re, v_a_im, v_log_dt, v_b_re, v_b_im, v_c_re, v_c_im, v_d_skip, v_w_glu, v_b_glu, v_g_attn_out, v_g_ssm_out, v_w_o, v_g_post_mix, v_g_pre_ffn, v_w_gate, v_w_up, v_w_down, v_g_post_ffn):
    given = dict(locals())
    weights = {n: given[n] for n in _ORDER}
    mom_m = {n: given["m_" + n] for n in _ORDER}
    mom_v = {n: given["v_" + n] for n in _ORDER}

    t, d = x.shape[1], x.shape[2]
    d_attn = d // 2
    d_ssm = d - d_attn
    d_in = d_attn + 2 * D_KV + d_ssm
    n_groups = d_ssm // SSM_GROUP
    n_heads = d_attn // HEAD_DIM
    tm = min(256, t)

    x2 = x[0]
    target = loss_target[0]

    def by_rows(n, a):
        return a[0].T if n in _BY_COLUMNS else a[0]

    def start_gather(name, ns, token):
        behind = 0 if token is None else token[0, 0].astype(BF16)
        shards = [by_rows(n, weights[n]).astype(BF16) + behind for n in ns]
        return _exchange_start(name, shards, False, (OWN, SIBLING) + CHIP_PEERS)

    def forward_gather(handle, after):
        return _forward_start(handle["name"] + "_forward", _exchange_wait(handle, after))

    def finish_gather(handle, after):
        return _split_wait(forward_gather(handle, after)[0], [])

    ag_in, token = start_gather("gather_w_in", ["w_in"], None)
    ag_mix, token = start_gather("gather_w_glu_o", ["w_glu", "w_o"], token)
    ag_ffn_in, token = start_gather("gather_w_gate_up", ["w_gate", "w_up"], token)
    ag_down, token = start_gather("gather_w_down", ["w_down"], token)

    xn, = _rows("norm_in", lambda xv, g: ([_rms(xv)[0] * g], []), [x2], [g_pre_mix], [(d, BF16)], [], tm,
                after=[token])
    win_g, = finish_gather(ag_in, [xn])
    w_in_t = win_g.reshape(d_in, d)
    proj = _mm_nt("proj_in", xn, w_in_t, F32)

    cos, sin = _rope_tables(positions.reshape(t, 1).astype(F32))
    sinks_row = jnp.pad(sinks, ((0, 0), (0, LANES - n_heads)))
    attn = _attention_fwd(proj, cos, sin, sinks_row, d_attn)

    def view(n, a):
        if n in ("b_re", "b_im"):
            return jnp.transpose(a[0], (0, 2, 1)).reshape(-1, SSM_STATE)
        if n in ("c_re", "c_im"):
            return a[0].reshape(-1, SSM_STATE)
        return a[0].T if n == "d_skip" else a[0] if a.ndim == 3 else a

    def unview(n, val):
        if n in ("b_re", "b_im"):
            return jnp.transpose(val.reshape(n_groups, SSM_GROUP, SSM_STATE), (0, 2, 1))[None]
        if n in ("c_re", "c_im"):
            return val.reshape(1, n_groups, SSM_GROUP, SSM_STATE)
        return val.T[None] if n == "d_skip" else val[None] if weights[n].ndim == 3 else val

    b_re_v, b_im_v = view("b_re", b_re), view("b_im", b_im)
    ldt_col = log_dt.reshape(n_groups, 1)
    lam_re, lam_im, bbar_re, bbar_im = _s5_discretise(a_re[0], a_im[0], ldt_col, b_re_v, b_im_v)
    n_blocks = n_groups // GROUPS_PER_BLOCK
    mats = [_block_diag_in(bbar_re).astype(BF16), _block_diag_in(bbar_im).astype(BF16),
            lam_re.reshape(n_blocks, 1, SSM_ST_BLOCK), lam_im.reshape(n_blocks, 1, SSM_ST_BLOCK),
            _block_diag_out(view("c_re", c_re)).astype(BF16), _block_diag_out(view("c_im", c_im)).astype(BF16)]
    dskip_row = d_skip.reshape(1, d_ssm)
    forward_mix, _ = forward_gather(ag_mix, [attn])
    y_ssm = _s5_fwd(proj, mats, dskip_row, d_attn, d_ssm)
    gelu_bf16 = lambda yv: _gelu(yv).astype(BF16)
    wglu_g, wo_g = _split_wait(forward_mix, [y_ssm])
    w_glu_full = wglu_g.reshape(d_ssm, d_ssm)
    w_o_full = wo_g.reshape(d, d)
    glu_lin = _mm_nn("glu_gate", y_ssm, w_glu_full, F32, a_fn=gelu_bf16)

    def mix_prep(av, yv, gl, bg, ga, gs):
        ssm = _gelu(yv) * _sigmoid(gl + bg)
        return [jnp.concatenate([_rms(av)[0] * ga, _rms(ssm)[0] * gs], axis=1)], []

    mixed, = _rows("mix_prep", mix_prep, [attn, y_ssm, glu_lin], [b_glu, g_attn_out, g_ssm_out], [(d, BF16)], [], tm)
    mix = _mm_nn("mix_out", mixed, w_o_full, F32)

    def post_mix(xv, mv, gpm, gpf):
        h = xv + _rms(mv)[0] * gpm
        return [h, _rms(h)[0] * gpf], []

    forward_ffn_in, token = forward_gather(ag_ffn_in, [mix])
    h, hn = _rows("post_mix", post_mix, [x2, mix], [g_post_mix, g_pre_ffn], [(d, F32), (d, BF16)], [], tm,
                  after=[token])
    wgate_g, wup_g = _split_wait(forward_ffn_in, [hn])
    d_ff = N_DEV * wgate_g.shape[1]
    wgate_t, wup_t = wgate_g.reshape(d_ff, d), wup_g.reshape(d_ff, d)
    gate, up, hid = _ffn_in(hn, wgate_t, wup_t)
    wdown_g, = finish_gather(ag_down, [hid])
    wdown_full = wdown_g.reshape(d_ff, d)
    ff = _mm_nn("ffn_down", hid, wdown_full, F32, tm=1024, tn=512)

    def head(hv, fv, tv, gpo):
        out = hv + _rms(fv)[0] * gpo
        err = out - tv
        dout = err * (1.0 / d)
        dff, dg = _rms_bwd(fv, gpo, dout)
        loss = jnp.zeros((1, LANES), F32) + 0.5 * jnp.sum(err * err) * (1.0 / d)
        return [dff, dout], [dg, loss]

    dff, dh_out, dg_post_ffn, loss_row = _rows("loss_head", head, [h, ff, target], [g_post_ffn],
                                               [(d, BF16), (d, F32)], [d, LANES], tm)

    def swap_halves(name, grads):
        return _halves_start("swap_" + name, [g.reshape(N_DEV // 2, 2, *g.shape[1:]) for g in grads])

    def scatter_chip_sums(name, swap, after):
        both = _split_wait(swap, after)
        half = len(both) // 2
        sums = [_chip_sum("chip_sum_%s_%d" % (name, i), both[i], both[half + i]) for i in range(half)]
        return _exchange_start("scatter_" + name, sums, True, (OWN,) + CHIP_PEERS, by_chip=True)

    f_tile = _hidden_tile(d_ff)
    by_owner = lambda g: g.reshape(N_DEV, d_ff // N_DEV, d)
    dw_down = by_owner(_mm_tn("ffn_down_dw", hid, dff, BF16, tm=f_tile))
    swap_down, token = swap_halves("dw_down", [dw_down])
    dgate, dup = _ffn_down_bwd(dff, wdown_full, gate, up, [token])
    rs_down, token = scatter_chip_sums("dw_down", swap_down, [dgate])
    dhn_gate = _mm_nn("ffn_in_dx_gate", dgate, wgate_t, F32, tm=1024, tn=512, after=[token])
    dhn = _mm_nn("ffn_in_dx_up", dup, wup_t, F32, tm=1024, tn=512, plus=dhn_gate)
    dw_gate = by_owner(_mm_tn("ffn_gate_dw", dgate, hn, BF16, tm=f_tile))
    dw_up = by_owner(_mm_tn("ffn_up_dw", dup, hn, BF16, tm=f_tile))
    swap_ffn_in, tok_ffn_in = swap_halves("dw_gate_up", [dw_gate, dw_up])

    def mid_bwd(dho, dhn_, hv, mv, gpf, gpm):
        d1, dgpf = _rms_bwd(hv, gpf, dhn_)
        dh_ = dho + d1
        dmix_, dgpm = _rms_bwd(mv, gpm, dh_)
        return [dh_, dmix_], [dgpf, dgpm]

    dh, dmix, dg_pre_ffn, dg_post_mix = _rows("mid_bwd", mid_bwd, [dh_out, dhn, h, mix], [g_pre_ffn, g_post_mix],
                                              [(d, F32), (d, BF16)], [d, d], tm, after=[tok_ffn_in])

    dmixed = _mm_nt("mix_out_dx", dmix, w_o_full, F32)
    rs_ffn_in, token = scatter_chip_sums("dw_gate_up", swap_ffn_in, [dmixed])
    dw_o = _mm_tn("mix_out_dw", mixed, dmix, BF16, after=[token])
    swap_o, tok_o = swap_halves("dw_o", [dw_o.reshape(N_DEV, d // N_DEV, d)])

    def mix_bwd(dm, av, yv, gl, bg, ga, gs):
        dattn_, dga = _rms_bwd(av, ga, dm[:, :d_attn])
        z = _gelu(yv)
        sg = _sigmoid(gl + bg)
        dssm, dgs = _rms_bwd(z * sg, gs, dm[:, d_attn:])
        dgl = dssm * z * sg * (1.0 - sg)
        return [dattn_, dssm * sg, dgl], [dga, dgs, jnp.sum(dgl, axis=0, keepdims=True)]

    dattn, dz_direct, dglu, dg_attn_out, dg_ssm_out, db_glu = _rows(
        "mix_bwd", mix_bwd, [dmixed, attn, y_ssm, glu_lin], [b_glu, g_attn_out, g_ssm_out],
        [(d_attn, F32), (d_ssm, F32), (d_ssm, BF16)], [d_attn, d_ssm, d_ssm], tm, after=[tok_o])
    dz_glu = _mm_nt("glu_gate_dx", dglu, w_glu_full, F32)
    dw_glu = _mm_tn("glu_gate_dw", y_ssm, dglu, BF16, a_fn=gelu_bf16)
    rs_o, token = scatter_chip_sums("dw_o", swap_o, [dz_glu, dw_glu])

    du, db_re_dense, db_im_dense, dlam_re, dlam_im, dc_re_dense, dc_im_dense, dd_skip = _s5_bwd(
        proj, mats, dskip_row, y_ssm, dz_direct, dz_glu, d_attn, d_ssm, [token])
    da_re, da_im, dlog_dt, db_re_v, db_im_v = _s5_discretise_bwd(
        a_re[0], a_im[0], ldt_col, b_re_v, b_im_v, dlam_re.reshape(n_groups, SSM_STATE),
        dlam_im.reshape(n_groups, SSM_STATE), _block_diag_in_t(db_re_dense), _block_diag_in_t(db_im_dense))
    dq, dk2, dv2, dsinks_row = _attention_bwd(proj, cos, sin, sinks_row, dattn, d_attn)

    small_grads = {
        "sinks": dsinks_row, "a_re": da_re, "a_im": da_im, "log_dt": dlog_dt.reshape(1, n_groups),
        "b_re": db_re_v, "b_im": db_im_v, "c_re": _block_diag_out_t(dc_re_dense),
        "c_im": _block_diag_out_t(dc_im_dense), "d_skip": dd_skip.reshape(n_groups, SSM_GROUP).T, "b_glu": db_glu,
        "g_attn_out": dg_attn_out, "g_ssm_out": dg_ssm_out, "g_post_mix": dg_post_mix, "g_pre_ffn": dg_pre_ffn,
        "g_post_ffn": dg_post_ffn,
    }
    vec_pack, vec_slots, mat_pack, mat_rows = _pack_grads([small_grads[n] for n in _SMALL_VECTORS] + [loss_row],
                                                          [small_grads[n] for n in _SMALL_MATRICES])
    ag_small, token = _exchange_start("gather_small_grads", [vec_pack, mat_pack, small_grads["d_skip"]], False,
                                      (OWN,) + ALL_PEERS)
    dproj = _assemble_dproj(dq, dk2, dv2, du, d_in, [token])

    dw_in = _mm_tn("proj_in_dw", dproj, xn, BF16).reshape(N_DEV, d_in // N_DEV, d)
    swap_in, token = swap_halves("dw_in_glu", [dw_in, dw_glu.reshape(N_DEV, d_ssm // N_DEV, d_ssm)])
    dxn = _mm_nn("proj_in_dx", dproj, w_in_t, F32, after=[token])
    rs_in, token = scatter_chip_sums("dw_in_glu", swap_in, [dxn])

    def x_bwd(dh_, dxn_, xv, g):
        dx, dg = _rms_bwd(xv, g, dxn_)
        return [dh_ + dx], [dg]

    grad_x, dg_pre_mix = _rows("norm_in_bwd", x_bwd, [dh, dxn, x2], [g_pre_mix], [(d, F32)], [d], tm, after=[token])
    ag_last, token = _exchange_start("gather_g_pre_mix_grad", [dg_pre_mix], False, (OWN,) + ALL_PEERS)

    results = {}

    def adam_big(n, parts):
        r = parts.shape[1]
        tr = next((c for c in range(192, 15, -16) if r % c == 0), r)
        results[n] = _adam_sharded("adam_" + n, parts, by_rows(n, weights[n]), by_rows(n, mom_m[n]),
                                   by_rows(n, mom_v[n]), tr)
        return results[n][3]

    done = [grad_x, token]
    adam_big("w_down", _exchange_wait(rs_down, done)[0])
    p_gate, p_up = _exchange_wait(rs_ffn_in, done)
    done = [adam_big("w_gate", p_gate), adam_big("w_up", p_up), results["w_down"][3]]
    done = [adam_big("w_o", _exchange_wait(rs_o, done)[0])]
    vec_parts, mat_parts, dskip_parts = _exchange_wait(ag_small, done)
    for n, row0 in zip(_SMALL_MATRICES, mat_rows):
        rows = view(n, weights[n]).shape[0]
        results[n] = _adam_sharded("adam_" + n, mat_parts, view(n, weights[n]), view(n, mom_m[n]), view(n, mom_v[n]),
                                   rows, row0)
    p_in, p_glu = _exchange_wait(rs_in, [results[n][3] for n in _SMALL_MATRICES])
    done = [adam_big("w_in", p_in), adam_big("w_glu", p_glu)]
    first_gain_parts, = _exchange_wait(ag_last, done)
    rest = _SMALL_VECTORS + ("d_skip", "g_pre_mix")
    found_at = [(0, row, lane) for row, lane, _ in vec_slots[:-1]] + [(1, 0, 0), (2, 0, 0)]
    updated, loss_sum = _adam_replicated([vec_parts, dskip_parts, first_gain_parts], found_at,
                                         [view(n, weights[n]) for n in rest], [view(n, mom_m[n]) for n in rest],
                                         [view(n, mom_v[n]) for n in rest], (0,) + vec_slots[-1])
    results.update(zip(rest, updated))

    outs = [loss_sum[0, 0], grad_x[None]]
    for k in range(4):
        for n in _ORDER:
            val = results[n][k]
            outs.append(val.T[None] if n in _BY_COLUMNS else val[None] if n in _BIG else unview(n, val))
    return tuple(outs)
```

```python
import math

import jax
import jax.numpy as jnp
from jax import lax
from jax.experimental import pallas as pl
from jax.experimental.pallas import tpu as pltpu

F32 = jnp.float32
BF16 = jnp.bfloat16

HEAD_DIM = 64
N_KV_HEADS = 4
D_KV = N_KV_HEADS * HEAD_DIM
WINDOW = 128
BLOCK = 128
ROPE_THETA = 10000.0
SSM_GROUP = 16
SSM_STATE = 64
GROUPS_PER_BLOCK = 8
SSM_CH_BLOCK = GROUPS_PER_BLOCK * SSM_GROUP
SSM_ST_BLOCK = GROUPS_PER_BLOCK * SSM_STATE
RMS_EPS = 1e-6
N_DEV = 8
LANES = 128
SUBLANES = 8
MASKED = -1e30

ADAM_LR = 0.001
ADAM_B1 = 0.9
ADAM_B2 = 0.999
ADAM_EPS = 1e-08
ADAM_WD = 0.01
ADAM_STEP = 10

VMEM_LIMIT_BYTES = 56 * 1024 * 1024


def _call(body, *, name, out_shape, in_specs, out_specs, grid=(), scratch_shapes=(), semantics=None, n_after=0):
    params = dict(vmem_limit_bytes=VMEM_LIMIT_BYTES)
    if semantics is not None:
        params["dimension_semantics"] = semantics
    n_in = len(in_specs)
    if n_after:
        inner = body

        def body(*refs):
            inner(*refs[:n_in], *refs[n_in + n_after:])

        in_specs = list(in_specs) + [pl.BlockSpec(memory_space=pl.ANY)] * n_after
    return pl.pallas_call(body, name=name, grid=grid, in_specs=in_specs, out_specs=out_specs, out_shape=out_shape,
                          scratch_shapes=scratch_shapes, compiler_params=pltpu.CompilerParams(**params))


def _sds(shape, dtype):
    return jax.ShapeDtypeStruct(tuple(shape), dtype)


def _dot(a, b, ca, cb):
    return lax.dot_general(a, b, (((ca,), (cb,)), ((), ())), preferred_element_type=F32)


def _rms(x):
    r = lax.rsqrt(jnp.mean(x * x, axis=-1, keepdims=True) + RMS_EPS)
    return x * r, r


def _rms_bwd(x, g, dy):
    xh, r = _rms(x)
    dxh = dy * g
    dx = r * (dxh - xh * jnp.mean(dxh * xh, axis=-1, keepdims=True))
    return dx, jnp.sum(dy * xh, axis=0, keepdims=True)


def _sigmoid(x):
    return 1.0 / (1.0 + jnp.exp(-x))


_GELU_C = math.sqrt(2.0 / math.pi)
_GELU_A = 0.044715


def _gelu(y):
    t = jnp.tanh(_GELU_C * (y + _GELU_A * y * y * y))
    return 0.5 * y * (1.0 + t)


def _gelu_grad(y):
    t = jnp.tanh(_GELU_C * (y + _GELU_A * y * y * y))
    return 0.5 * (1.0 + t) + 0.5 * y * (1.0 - t * t) * _GELU_C * (1.0 + 3.0 * _GELU_A * y * y)


def _rows(name, fn, row_ins, vec_ins, row_outs, acc_widths, tm, after=()):
    rows = row_ins[0].shape[0]
    assert rows % tm == 0, (name, rows, tm)
    n_row, n_vec, n_out, n_acc = len(row_ins), len(vec_ins), len(row_outs), len(acc_widths)

    def body(*refs):
        ins = [r[...] for r in refs[:n_row + n_vec]]
        outs = refs[n_row + n_vec:n_row + n_vec + n_out]
        accs = refs[n_row + n_vec + n_out:]
        row_vals, acc_vals = fn(*ins)
        for o, v in zip(outs, row_vals):
            o[...] = v.astype(o.dtype)
        if n_acc:
            @pl.when(pl.program_id(0) == 0)
            def _():
                for a in accs:
                    a[...] = jnp.zeros_like(a)
            for a, v in zip(accs, acc_vals):
                a[...] += v

    in_specs = [pl.BlockSpec((tm, a.shape[1]), lambda i: (i, 0)) for a in row_ins]
    in_specs += [pl.BlockSpec(v.shape, lambda i: (0, 0)) for v in vec_ins]
    out_specs = [pl.BlockSpec((tm, w), lambda i: (i, 0)) for w, _ in row_outs]
    out_specs += [pl.BlockSpec((1, w), lambda i: (0, 0)) for w in acc_widths]
    out_shape = [_sds((rows, w), dt) for w, dt in row_outs] + [_sds((1, w), F32) for w in acc_widths]
    return _call(body, name=name, grid=(rows // tm,), in_specs=in_specs, out_specs=out_specs, out_shape=out_shape,
                 semantics=("arbitrary",) if n_acc else ("parallel",), n_after=len(after))(*row_ins, *vec_ins, *after)


def _matmul(name, operands, in_specs, product, grid, out_shape, out_spec, acc_shape, after=()):
    nk = grid[-1]
    n_in = len(operands)
    in_place = out_shape.dtype == F32

    def body(*refs):
        ins = [r[...] for r in refs[:n_in]]
        o_ref = refs[n_in]
        if nk == 1:
            o_ref[...] = product(*ins).astype(o_ref.dtype)
            return
        acc = o_ref if in_place else refs[n_in + 1]
        k = pl.program_id(len(grid) - 1)

        @pl.when(k == 0)
        def _():
            acc[...] = jnp.zeros_like(acc)

        acc[...] += product(*ins)

        if not in_place:
            @pl.when(k == nk - 1)
            def _():
                o_ref[...] = acc[...].astype(o_ref.dtype)

    return _call(body, name=name, grid=grid, in_specs=in_specs, out_specs=out_spec, out_shape=out_shape,
                 scratch_shapes=[] if nk == 1 or in_place else [pltpu.VMEM(acc_shape, F32)],
                 semantics=("parallel",) * (len(grid) - 1) + ("arbitrary",), n_after=len(after))(*operands, *after)


def _mm_nn(name, a, b, out_dtype, tm=512, tn=None, a_fn=lambda x: x, after=(), plus=None):
    m, k = a.shape
    n = b.shape[1]
    tm, tn = min(tm, m), n if tn is None else min(tn, n)
    operands = [a, b] + ([] if plus is None else [plus])
    specs = [pl.BlockSpec((tm, k), lambda i, j, s: (i, 0)), pl.BlockSpec((k, tn), lambda i, j, s: (0, j))]
    specs += [] if plus is None else [pl.BlockSpec((tm, tn), lambda i, j, s: (i, j))]
    return _matmul(name, operands, specs, lambda x, y, *p: _dot(a_fn(x), y, 1, 0) + (p[0] if p else 0.0),
                   (m // tm, n // tn, 1), _sds((m, n), out_dtype),
                   pl.BlockSpec((tm, tn), lambda i, j, s: (i, j)), (tm, tn), after)


def _mm_nt(name, a, b, out_dtype, tm=512, tn=None):
    m, k = a.shape
    n = b.shape[0]
    tm, tn = min(tm, m), n if tn is None else tn
    return _matmul(name, [a, b],
                   [pl.BlockSpec((tm, k), lambda i, j, s: (i, 0)), pl.BlockSpec((tn, k), lambda i, j, s: (j, 0))],
                   lambda x, y: _dot(x, y, 1, 1), (m // tm, n // tn, 1), _sds((m, n), out_dtype),
                   pl.BlockSpec((tm, tn), lambda i, j, s: (i, j)), (tm, tn))


def _mm_tn(name, a, b, out_dtype, tm=512, tn=None, tk=2048, a_fn=lambda x: x, after=()):
    k, m = a.shape
    n = b.shape[1]
    tm, tk, tn = min(tm, m), min(tk, k), n if tn is None else tn
    return _matmul(name, [a, b],
                   [pl.BlockSpec((tk, tm), lambda i, j, s: (s, i)), pl.BlockSpec((tk, tn), lambda i, j, s: (s, j))],
                   lambda x, y: _dot(a_fn(x), y, 0, 0), (m // tm, n // tn, k // tk), _sds((m, n), out_dtype),
                   pl.BlockSpec((tm, tn), lambda i, j, s: (i, j)), (tm, tn), after)


def _hidden_tile(f):
    return 512 if f % 512 == 0 else 256


def _ffn_in(a, w_gate, w_up, after=(), tm=1024):
    m, k = a.shape
    f = w_gate.shape[0]
    tm, tn = min(tm, m), _hidden_tile(f)

    def body(a_ref, wg_ref, wu_ref, g_ref, u_ref, h_ref):
        x = a_ref[...]
        g = _dot(x, wg_ref[...], 1, 1)
        u = _dot(x, wu_ref[...], 1, 1)
        g_ref[...] = g.astype(BF16)
        u_ref[...] = u.astype(BF16)
        h_ref[...] = (g * _sigmoid(g) * u).astype(BF16)

    w_spec = pl.BlockSpec((tn, k), lambda j, i: (j, 0))
    o_spec = pl.BlockSpec((tm, tn), lambda j, i: (i, j))
    return _call(body, name="ffn_in", grid=(f // tn, m // tm),
                 in_specs=[pl.BlockSpec((tm, k), lambda j, i: (i, 0)), w_spec, w_spec], out_specs=[o_spec] * 3,
                 out_shape=[_sds((m, f), BF16)] * 3, semantics=("parallel", "parallel"),
                 n_after=len(after))(a, w_gate, w_up, *after)


def _ffn_down_bwd(d_out, w_down, gate, up, after, tm=1024):
    m, k = d_out.shape
    f = w_down.shape[0]
    tm, tn = min(tm, m), _hidden_tile(f)

    def body(d_ref, w_ref, g_ref, u_ref, dg_ref, du_ref):
        rows = pl.ds(pl.multiple_of(pl.program_id(1) * tm, tm), tm)
        dh = _dot(d_ref[rows, :], w_ref[...], 1, 1)
        g = g_ref[...].astype(F32)
        sg = _sigmoid(g)
        dg_ref[...] = (dh * u_ref[...].astype(F32) * sg * (1.0 + g * (1.0 - sg))).astype(BF16)
        du_ref[...] = (dh * g * sg).astype(BF16)

    t_spec = pl.BlockSpec((tm, tn), lambda j, i: (i, j))
    return _call(body, name="ffn_down_dx", grid=(f // tn, m // tm),
                 in_specs=[pl.BlockSpec((m, k), lambda j, i: (0, 0)), pl.BlockSpec((tn, k), lambda j, i: (j, 0)),
                           t_spec, t_spec],
                 out_specs=[t_spec] * 2, out_shape=[_sds((m, f), BF16)] * 2, semantics=("parallel", "parallel"),
                 n_after=len(after))(d_out, w_down, gate, up, *after)


ALL_PEERS = (1, 2, 3, 4, 5, 6, 7)
CHIP_PEERS = (2, 4, 6)
SIBLING = 1
OWN = 0


def _peer(relation):
    x, y, c = lax.axis_index("x"), lax.axis_index("y"), lax.axis_index("c")
    pos = (1 - x if relation & 4 else x, 1 - y if relation & 2 else y, 1 - c if relation & 1 else c)
    return pos, 4 * pos[0] + 2 * pos[1] + pos[2]


def _slot(relation, by_chip):
    pos, device = _peer(relation)
    return 2 * pos[0] + pos[1] if by_chip else device


def _exchange_copies(ins, lands, send_sems, recv_sems, scatter, relations, by_chip=False):
    me = _slot(0, by_chip)

    def copy(a, s, peer, pos, dst_slot):
        return pltpu.make_async_remote_copy(
            src_ref=ins[a].at[peer] if scatter else ins[a], dst_ref=lands[a].at[dst_slot],
            send_sem=send_sems.at[s], recv_sem=recv_sems.at[s], device_id=pos, device_id_type=pl.DeviceIdType.MESH)

    pairs = []
    for k, r in enumerate(relations):
        pos, peer = _peer(r)[0], _slot(r, by_chip)
        for a in range(len(ins)):
            s = a * len(relations) + k
            pairs.append((copy(a, s, peer, pos, me), copy(a, s, peer, pos, peer)))
    return pairs


def _halves_copies(arrays, lands, send_sems, recv_sems):
    sibling, _ = _peer(SIBLING)
    core = lax.axis_index("c")
    pairs = []
    for a, (ref, land) in enumerate(zip(arrays, lands)):
        send = pltpu.make_async_remote_copy(
            src_ref=ref.at[:, pl.ds(1 - core, 1)], dst_ref=land, send_sem=send_sems.at[a], recv_sem=recv_sems.at[a],
            device_id=sibling, device_id_type=pl.DeviceIdType.MESH)
        pairs.append((send, send))
    return pairs


def _forward_copies(lands, send_sems, recv_sems):
    sibling, _ = _peer(SIBLING)

    def copy(a, s, slot):
        return pltpu.make_async_remote_copy(
            src_ref=lands[a].at[slot], dst_ref=lands[a].at[slot], send_sem=send_sems.at[s], recv_sem=recv_sems.at[s],
            device_id=sibling, device_id_type=pl.DeviceIdType.MESH)

    pairs = []
    for k, r in enumerate(CHIP_PEERS):
        _, mine = _peer(r)
        _, theirs = _peer(r | SIBLING)
        for a in range(len(lands)):
            s = a * len(CHIP_PEERS) + k
            pairs.append((copy(a, s, mine), copy(a, s, theirs)))
    return pairs


_HBM_SPEC = pl.BlockSpec(memory_space=pltpu.HBM)
_SEM_SPEC = pl.BlockSpec(memory_space=pltpu.SEMAPHORE)
_SIDE_EFFECT = pltpu.SideEffectType.DATAFLOW_SIDE_EFFECTING


def _split_start(name, operands, n_sem, make_pairs):
    k = len(operands)

    def body(*refs):
        send_sems, recv_sems, token = refs[k], refs[k + 1], refs[-1]
        for send, _ in make_pairs(refs[:k], send_sems, recv_sems):
            send.start()
        token[...] = jnp.zeros_like(token)

    out = pl.pallas_call(
        body, name=name,
        out_shape=(pltpu.SemaphoreType.DMA((n_sem,)), pltpu.SemaphoreType.DMA((n_sem,)),
                   *[pltpu.HBM(a.shape, a.dtype) for a in operands], _sds((SUBLANES, LANES), F32)),
        in_specs=[_HBM_SPEC] * k,
        out_specs=(_SEM_SPEC, _SEM_SPEC, *[_HBM_SPEC] * k, pl.BlockSpec(memory_space=pltpu.VMEM)),
        input_output_aliases={i: 2 + i for i in range(k)},
        compiler_params=pltpu.CompilerParams(has_side_effects=_SIDE_EFFECT),
    )(*[pltpu.with_memory_space_constraint(a, pltpu.HBM) for a in operands])
    return dict(name=name, sems=out[:2], thru=list(out[2:2 + k]), make_pairs=make_pairs), out[-1]


def _split_wait(handle, after):
    thru, make_pairs = handle["thru"], handle["make_pairs"]
    k = len(thru)

    def body(*refs):
        for send, arrival in make_pairs(refs[:k], refs[k], refs[k + 1]):
            send.wait_send()
            arrival.wait_recv()

    return pl.pallas_call(
        body, name=handle["name"] + "_wait", out_shape=[pltpu.HBM(a.shape, a.dtype) for a in thru],
        in_specs=[_HBM_SPEC] * k + [_SEM_SPEC, _SEM_SPEC] + [pl.BlockSpec(memory_space=pl.ANY)] * len(after),
        out_specs=[_HBM_SPEC] * k, input_output_aliases={i: i for i in range(k)},
        compiler_params=pltpu.CompilerParams(has_side_effects=_SIDE_EFFECT),
    )(*thru, *handle["sems"], *after)


def _exchange_start(name, arrays, scatter, relations, by_chip=False):
    n = len(arrays)
    lands = [lax.empty(a.shape if scatter else (N_DEV,) + a.shape, a.dtype) for a in arrays]

    def make_pairs(refs, send_sems, recv_sems):
        return _exchange_copies(refs[:n], refs[n:], send_sems, recv_sems, scatter, relations, by_chip)

    handle, token = _split_start(name, list(arrays) + lands, n * len(relations), make_pairs)
    handle.update(n=n)
    return handle, token


def _halves_start(name, arrays):
    lands = [lax.empty((a.shape[0], 1) + a.shape[2:], a.dtype) for a in arrays]
    n = len(arrays)

    def make_pairs(refs, send_sems, recv_sems):
        return _halves_copies(refs[:n], refs[n:], send_sems, recv_sems)

    return _split_start(name, list(arrays) + lands, n, make_pairs)


def _chip_sum(name, array, landed):
    chips, _, r, c = array.shape
    tr = r

    def body(a_ref, b_ref, o_ref):
        mine = a_ref[lax.axis_index("c")].astype(F32)
        o_ref[...] = (mine + b_ref[...].astype(F32)).astype(o_ref.dtype)

    return _call(body, name=name, grid=(chips, r // tr),
                 in_specs=[pl.BlockSpec((None, 2, tr, c), lambda k, i: (k, 0, i, 0)),
                           pl.BlockSpec((None, None, tr, c), lambda k, i: (k, 0, i, 0))],
                 out_specs=pl.BlockSpec((None, tr, c), lambda k, i: (k, i, 0)),
                 out_shape=_sds((chips, r, c), BF16), semantics=("parallel", "parallel"))(array, landed)


def _forward_start(name, lands):
    return _split_start(name, list(lands), len(lands) * len(CHIP_PEERS), _forward_copies)


def _exchange_wait(handle, after):
    return _split_wait(handle, after)[handle["n"]:]


def _rope_tables(pos_col):
    t = pos_col.shape[0]
    half = HEAD_DIM // 2
    inv_freq = ROPE_THETA ** (-jnp.arange(half, dtype=F32) / half)
    inv_row = jnp.tile(inv_freq, LANES // half)[None, :]

    def body(pos_ref, inv_ref, cos_ref, sin_ref):
        ang = pos_ref[...] * inv_ref[...]
        cos_ref[...] = jnp.cos(ang)
        sin_ref[...] = jnp.sin(ang)

    tm = min(t, 512)
    return _call(body, name="rope_tables", grid=(t // tm,),
                 in_specs=[pl.BlockSpec((tm, 1), lambda i: (i, 0)), pl.BlockSpec((1, LANES), lambda i: (0, 0))],
                 out_specs=[pl.BlockSpec((tm, LANES), lambda i: (i, 0))] * 2,
                 out_shape=[_sds((t, LANES), F32)] * 2, semantics=("parallel",))(pos_col, inv_row)


def _rot_half(x):
    lane = lax.broadcasted_iota(jnp.int32, x.shape, 1)
    low = (lane % HEAD_DIM) < HEAD_DIM // 2
    return jnp.where(low, -pltpu.roll(x, LANES - HEAD_DIM // 2, 1), pltpu.roll(x, HEAD_DIM // 2, 1))


def _rope(x, cos, sin):
    return x * cos + _rot_half(x) * sin


def _unrope(d, cos, sin):
    return d * cos - _rot_half(d) * sin


def _band_mask(first_block, heads):
    r = lax.broadcasted_iota(jnp.int32, (heads * BLOCK, 2 * BLOCK), 0) % BLOCK
    c = lax.broadcasted_iota(jnp.int32, (heads * BLOCK, 2 * BLOCK), 1)
    diff = r - c + BLOCK
    return (diff >= 0) & (diff < WINDOW) & ((c >= BLOCK) | jnp.logical_not(first_block))


def _attn_specs(t, d_attn, d_in):
    kb, vb = d_attn // D_KV, d_attn // D_KV + 1
    prev = lambda i: jnp.maximum(i - 1, 0)
    return [
        pl.BlockSpec((BLOCK, d_attn), lambda i: (i, 0)),
        pl.BlockSpec((BLOCK, D_KV), lambda i: (i, kb)),
        pl.BlockSpec((BLOCK, D_KV), lambda i: (i, vb)),
        pl.BlockSpec((BLOCK, D_KV), lambda i: (prev(i), kb)),
        pl.BlockSpec((BLOCK, D_KV), lambda i: (prev(i), vb)),
        pl.BlockSpec((BLOCK, LANES), lambda i: (i, 0)),
        pl.BlockSpec((BLOCK, LANES), lambda i: (i, 0)),
        pl.BlockSpec((BLOCK, LANES), lambda i: (prev(i), 0)),
        pl.BlockSpec((BLOCK, LANES), lambda i: (prev(i), 0)),
        pl.BlockSpec((1, LANES), lambda i: (0, 0)),
    ]


def _head(x, h):
    return x[:, h * HEAD_DIM:(h + 1) * HEAD_DIM]


def _attn_heads(q_ref, kc_ref, vc_ref, kp_ref, vp_ref, cq_ref, sq_ref, cp_ref, sp_ref, d_attn):
    cq, sq, cp, sp = cq_ref[...], sq_ref[...], cp_ref[...], sp_ref[...]
    q_rot = [_rope(q_ref[:, j * LANES:(j + 1) * LANES], cq, sq) for j in range(d_attn // LANES)]
    kc_rot = [_rope(kc_ref[:, j * LANES:(j + 1) * LANES], cq, sq) for j in range(D_KV // LANES)]
    kp_rot = [_rope(kp_ref[:, j * LANES:(j + 1) * LANES], cp, sp) for j in range(D_KV // LANES)]
    per = LANES // HEAD_DIM
    q_heads = [_head(q_rot[h // per], h % per).astype(BF16) for h in range(d_attn // HEAD_DIM)]
    kk = [jnp.concatenate([_head(kp_rot[g // per], g % per), _head(kc_rot[g // per], g % per)], axis=0).astype(BF16)
          for g in range(N_KV_HEADS)]
    vv = [jnp.concatenate([_head(vp_ref[...], g), _head(vc_ref[...], g)], axis=0).astype(BF16) for g in range(N_KV_HEADS)]
    return q_heads, kk, vv


def _stack_group(q_heads, sink_ref, group):
    q_all = jnp.concatenate([q_heads[h] for h in group], axis=0)
    sink_all = jnp.concatenate([jnp.broadcast_to(sink_ref[:, h:h + 1], (BLOCK, 1)) for h in group], axis=0)
    return q_all, sink_all


def _softmax_with_sink(q, kk, sink, mask):
    s = _dot(q, kk, 1, 1) * (1.0 / math.sqrt(HEAD_DIM))
    s = jnp.where(mask, s, MASKED)
    m = jnp.maximum(jnp.max(s, axis=-1, keepdims=True), sink)
    p = jnp.exp(s - m)
    e_sink = jnp.exp(sink - m)
    inv = 1.0 / (jnp.sum(p, axis=-1, keepdims=True) + e_sink)
    return p * inv, e_sink * inv


def _attention_fwd(proj, cos, sin, sinks_row, d_attn):
    t, d_in = proj.shape
    n_heads = d_attn // HEAD_DIM
    q_per_kv = n_heads // N_KV_HEADS

    def body(q_ref, kc_ref, vc_ref, kp_ref, vp_ref, cq_ref, sq_ref, cp_ref, sp_ref, sink_ref, o_ref):
        mask = _band_mask(pl.program_id(0) == 0, q_per_kv)
        q_heads, kk, vv = _attn_heads(q_ref, kc_ref, vc_ref, kp_ref, vp_ref, cq_ref, sq_ref, cp_ref, sp_ref, d_attn)
        for g in range(N_KV_HEADS):
            group = range(g * q_per_kv, (g + 1) * q_per_kv)
            q_all, sink_all = _stack_group(q_heads, sink_ref, group)
            probs, _ = _softmax_with_sink(q_all, kk[g], sink_all, mask)
            o_all = _dot(probs.astype(BF16), vv[g], 1, 0)
            for k, h in enumerate(group):
                o_ref[:, h * HEAD_DIM:(h + 1) * HEAD_DIM] = o_all[k * BLOCK:(k + 1) * BLOCK]

    return _call(body, name="attention_fwd", grid=(t // BLOCK,), in_specs=_attn_specs(t, d_attn, d_in),
                 out_specs=pl.BlockSpec((BLOCK, d_attn), lambda i: (i, 0)), out_shape=_sds((t, d_attn), F32),
                 semantics=("parallel",))(proj, proj, proj, proj, proj, cos, sin, cos, sin, sinks_row)


def _attention_bwd(proj, cos, sin, sinks_row, d_out, d_attn):
    t, d_in = proj.shape
    n_heads = d_attn // HEAD_DIM
    q_per_kv = n_heads // N_KV_HEADS
    nb = t // BLOCK
    per = LANES // HEAD_DIM
    stack = q_per_kv

    def body(q_ref, kc_ref, vc_ref, kp_ref, vp_ref, cq_ref, sq_ref, cp_ref, sp_ref, sink_ref, do_ref,
             dq_ref, dk_ref, dv_ref, dsink_ref):
        i = pl.program_id(0)
        mask = _band_mask(i == 0, stack)
        q_heads, kk, vv = _attn_heads(q_ref, kc_ref, vc_ref, kp_ref, vp_ref, cq_ref, sq_ref, cp_ref, sp_ref, d_attn)
        lane = lax.broadcasted_iota(jnp.int32, (1, LANES), 1)
        dsink = jnp.zeros((1, LANES), F32)
        dq_rot, dkk, dvv = [], [], []
        for g in range(N_KV_HEADS):
            dkk_g = jnp.zeros((2 * BLOCK, HEAD_DIM), F32)
            dvv_g = jnp.zeros((2 * BLOCK, HEAD_DIM), F32)
            for first in range(g * q_per_kv, (g + 1) * q_per_kv, stack):
                group = range(first, first + stack)
                q_all, sink_all = _stack_group(q_heads, sink_ref, group)
                probs, p_sink = _softmax_with_sink(q_all, kk[g], sink_all, mask)
                do_all = jnp.concatenate([do_ref[:, h * HEAD_DIM:(h + 1) * HEAD_DIM] for h in group],
                                         axis=0).astype(BF16)
                dp = _dot(do_all, vv[g], 1, 1)
                delta = jnp.sum(probs * dp, axis=-1, keepdims=True)
                ds = (probs * (dp - delta) * (1.0 / math.sqrt(HEAD_DIM))).astype(BF16)
                dq_all = _dot(ds, kk[g], 1, 0)
                dkk_g += _dot(ds, q_all, 0, 0)
                dvv_g += _dot(probs.astype(BF16), do_all, 0, 0)
                sink_term = p_sink * delta
                for k, h in enumerate(group):
                    dq_rot.append(dq_all[k * BLOCK:(k + 1) * BLOCK])
                    part = jnp.sum(sink_term[k * BLOCK:(k + 1) * BLOCK], axis=0, keepdims=True)
                    dsink += jnp.where(lane == h, -part, 0.0)
            dkk.append(dkk_g)
            dvv.append(dvv_g)
        cq, sq, cp, sp = cq_ref[...], sq_ref[...], cp_ref[...], sp_ref[...]
        for j in range(d_attn // LANES):
            d = jnp.concatenate(dq_rot[j * per:(j + 1) * per], axis=1)
            dq_ref[:, j * LANES:(j + 1) * LANES] = _unrope(d, cq, sq)
        for j in range(D_KV // LANES):
            d = jnp.concatenate(dkk[j * per:(j + 1) * per], axis=1)
            dk_ref[0, :, j * LANES:(j + 1) * LANES] = _unrope(d[:BLOCK], cp, sp)
            dk_ref[1, :, j * LANES:(j + 1) * LANES] = _unrope(d[BLOCK:], cq, sq)
            d = jnp.concatenate(dvv[j * per:(j + 1) * per], axis=1)
            dv_ref[0, :, j * LANES:(j + 1) * LANES] = d[:BLOCK]
            dv_ref[1, :, j * LANES:(j + 1) * LANES] = d[BLOCK:]

        @pl.when(i == 0)
        def _():
            dsink_ref[...] = jnp.zeros_like(dsink_ref)

        dsink_ref[...] += dsink

    pair = pl.BlockSpec((2, BLOCK, D_KV), lambda i: (i, 0, 0))
    return _call(body, name="attention_bwd", grid=(nb,),
                 in_specs=_attn_specs(t, d_attn, d_in) + [pl.BlockSpec((BLOCK, d_attn), lambda i: (i, 0))],
                 out_specs=[pl.BlockSpec((BLOCK, d_attn), lambda i: (i, 0)), pair, pair,
                            pl.BlockSpec((1, LANES), lambda i: (0, 0))],
                 out_shape=[_sds((t, d_attn), F32), _sds((2 * nb, BLOCK, D_KV), F32), _sds((2 * nb, BLOCK, D_KV), F32),
                            _sds((1, LANES), F32)],
                 semantics=("arbitrary",))(proj, proj, proj, proj, proj, cos, sin, cos, sin, sinks_row, d_out)


def _assemble_dproj(dq, dk2, dv2, du, d_in, after):
    t, d_attn = dq.shape
    d_ssm = du.shape[1]
    nb = t // BLOCK

    def body(dq_ref, dk_own, dk_next, dv_own, dv_next, du_ref, o_ref):
        has_next = (pl.program_id(0) < nb - 1).astype(F32)
        o_ref[:, :d_attn] = dq_ref[...].astype(BF16)
        o_ref[:, d_attn:d_attn + D_KV] = (dk_own[...] + has_next * dk_next[...]).astype(BF16)
        o_ref[:, d_attn + D_KV:d_attn + 2 * D_KV] = (dv_own[...] + has_next * dv_next[...]).astype(BF16)
        o_ref[:, d_attn + 2 * D_KV:] = du_ref[...].astype(BF16)

    own = pl.BlockSpec((None, BLOCK, D_KV), lambda i: (2 * i + 1, 0, 0))
    nxt = pl.BlockSpec((None, BLOCK, D_KV), lambda i: (jnp.minimum(2 * i + 2, 2 * nb - 1), 0, 0))
    return _call(body, name="assemble_dproj", grid=(nb,),
                 in_specs=[pl.BlockSpec((BLOCK, d_attn), lambda i: (i, 0)), own, nxt, own, nxt,
                           pl.BlockSpec((BLOCK, d_ssm), lambda i: (i, 0))],
                 out_specs=pl.BlockSpec((BLOCK, d_in), lambda i: (i, 0)), out_shape=_sds((t, d_in), BF16),
                 semantics=("parallel",), n_after=len(after))(dq, dk2, dk2, dv2, dv2, du, *after)


def _discretise(ar, ai, ldt, br, bi):
    dt = jnp.exp(ldt)
    mag = jnp.exp(ar * dt)
    lam_re = mag * jnp.cos(ai * dt)
    lam_im = mag * jnp.sin(ai * dt)
    den = ar * ar + ai * ai
    nr = lam_re - 1.0
    ni = lam_im
    f_re = (nr * ar + ni * ai) / den
    f_im = (ni * ar - nr * ai) / den
    return (lam_re, lam_im, [f_re * r - f_im * i for r, i in zip(br, bi)], [f_re * i + f_im * r for r, i in zip(br, bi)])


def _whole(arrays):
    return [pl.BlockSpec(a.shape, lambda *_, nd=len(a.shape): (0,) * nd) for a in arrays]


def _channels(ref):
    groups = ref.shape[0] // SSM_GROUP
    return [ref[pl.ds(p, groups, stride=SSM_GROUP), :] for p in range(SSM_GROUP)]


def _store_channels(ref, values):
    groups = ref.shape[0] // SSM_GROUP
    for p, val in enumerate(values):
        ref[pl.ds(p, groups, stride=SSM_GROUP), :] = val


def _s5_discretise(ar, ai, ldt, br, bi):
    ins = [ar, ai, ldt, br, bi]

    def body(ar_ref, ai_ref, ldt_ref, br_ref, bi_ref, lr_ref, li_ref, bbr_ref, bbi_ref):
        lr, li, bbr, bbi = _discretise(ar_ref[...], ai_ref[...], ldt_ref[...], _channels(br_ref), _channels(bi_ref))
        lr_ref[...] = lr
        li_ref[...] = li
        _store_channels(bbr_ref, bbr)
        _store_channels(bbi_ref, bbi)

    outs = [_sds(ar.shape, F32), _sds(ar.shape, F32), _sds(br.shape, F32), _sds(br.shape, F32)]
    return _call(body, name="s5_discretise", in_specs=_whole(ins), out_specs=_whole(outs), out_shape=outs)(*ins)


def _s5_discretise_bwd(ar, ai, ldt, br, bi, d_lr, d_li, d_bbr, d_bbi):
    ins = [ar, ai, ldt, br, bi, d_lr, d_li, d_bbr, d_bbi]

    def body(ar_ref, ai_ref, ldt_ref, br_ref, bi_ref, dlr_ref, dli_ref, dbbr_ref, dbbi_ref,
             dar_ref, dai_ref, dldt_ref, dbr_ref, dbi_ref):
        _, vjp = jax.vjp(_discretise, ar_ref[...], ai_ref[...], ldt_ref[...], _channels(br_ref), _channels(bi_ref))
        dar, dai, dldt, dbr, dbi = vjp((dlr_ref[...], dli_ref[...], _channels(dbbr_ref), _channels(dbbi_ref)))
        dar_ref[...] = dar
        dai_ref[...] = dai
        dldt_ref[...] = dldt
        _store_channels(dbr_ref, dbr)
        _store_channels(dbi_ref, dbi)

    outs = [_sds(a.shape, F32) for a in (ar, ai, ldt, br, bi)]
    return _call(body, name="s5_discretise_bwd", in_specs=_whole(ins), out_specs=_whole(outs), out_shape=outs)(*ins)


def _cmul(ar, ai, br, bi):
    return ar * br - ai * bi, ar * bi + ai * br


def _load_segmented(ref, tile0, n_tiles, seg):
    return jnp.concatenate([ref[pl.ds(tile0 + j, SUBLANES, stride=seg), :] for j in range(n_tiles)], axis=0)


def _store_segmented(ref, tile0, seg, value):
    for j in range(value.shape[0] // SUBLANES):
        ref[pl.ds(tile0 + j, SUBLANES, stride=seg), :] = value[j * SUBLANES:(j + 1) * SUBLANES, :]


def _fill_powers(lr, li, pr_ref, pi_ref, seg):
    pows = [(lr, li)]
    for _ in range(SUBLANES - 1):
        pows.append(_cmul(pows[-1][0], pows[-1][1], lr, li))
    row = lax.broadcasted_iota(jnp.int32, (SUBLANES, lr.shape[1]), 0)
    tr = jnp.zeros((SUBLANES, lr.shape[1]), F32)
    ti = jnp.zeros((SUBLANES, lr.shape[1]), F32)
    for r in range(SUBLANES):
        tr = jnp.where(row == r, pows[r][0], tr)
        ti = jnp.where(row == r, pows[r][1], ti)
    pr_ref[0:SUBLANES, :] = tr
    pi_ref[0:SUBLANES, :] = ti
    k = SUBLANES
    while k < seg:
        fr, fi = pr_ref[k - 1:k, :], pi_ref[k - 1:k, :]
        for t0 in range(0, k, SUBLANES):
            nr, ni = _cmul(pr_ref[t0:t0 + SUBLANES, :], pi_ref[t0:t0 + SUBLANES, :], fr, fi)
            pr_ref[k + t0:k + t0 + SUBLANES, :] = nr
            pi_ref[k + t0:k + t0 + SUBLANES, :] = ni
        k *= 2


def _scan_segments(sr_ref, si_ref, pr_ref, pi_ref, lr, li, seg, reverse, per_tile=None):
    w = lr.shape[1]
    sign = -1.0 if reverse else 1.0
    lrb = jnp.broadcast_to(lr, (SUBLANES, w))
    lib = jnp.broadcast_to(sign * li, (SUBLANES, w))
    zero = jnp.zeros((SUBLANES, w), F32)

    def tile_rows(j):
        return pl.ds(pl.multiple_of(j * SUBLANES, SUBLANES), SUBLANES)

    steps = 4 if seg % 4 == 0 else 1

    def local(i, carry):
        for u in range(steps):
            j = i * steps + u
            rows = tile_rows(seg - 1 - j if reverse else j)
            pr, pi = _cmul(lrb, lib, carry[0], carry[1])
            carry = (sr_ref[rows, :] + pr, si_ref[rows, :] + pi)
            sr_ref[rows, :] = carry[0]
            si_ref[rows, :] = carry[1]
        return carry

    end_r, end_i = lax.fori_loop(0, seg // steps, local, (zero, zero))
    full_r, full_i = pr_ref[seg - 1:seg, :], sign * pi_ref[seg - 1:seg, :]
    row = lax.broadcasted_iota(jnp.int32, (SUBLANES, w), 0)
    in_r, in_i = zero, zero
    cur_r, cur_i = jnp.zeros((1, w), F32), jnp.zeros((1, w), F32)
    for r in (range(SUBLANES - 2, -1, -1) if reverse else range(1, SUBLANES)):
        src = r + 1 if reverse else r - 1
        pr, pi = _cmul(full_r, full_i, cur_r, cur_i)
        cur_r, cur_i = end_r[src:src + 1, :] + pr, end_i[src:src + 1, :] + pi
        in_r = jnp.where(row == r, cur_r, in_r)
        in_i = jnp.where(row == r, cur_i, in_i)

    def carry_in(j, _):
        rows = tile_rows(j)
        k = seg - 1 - j if reverse else j
        pr, pi = _cmul(pr_ref[pl.ds(k, 1), :], sign * pi_ref[pl.ds(k, 1), :], in_r, in_i)
        xr, xi = sr_ref[rows, :] + pr, si_ref[rows, :] + pi
        sr_ref[rows, :] = xr
        si_ref[rows, :] = xi
        if per_tile is not None:
            per_tile(j, xr, xi)
        return 0

    lax.fori_loop(0, seg, carry_in, 0, unroll=4)


_S5_ROWS = 2048


def _s5_in_specs(t, d_attn):
    u_block = (d_attn + 2 * D_KV) // SSM_CH_BLOCK
    blk3 = lambda shape: pl.BlockSpec((None,) + shape, lambda j: (j, 0, 0))
    return [
        pl.BlockSpec((t, SSM_CH_BLOCK), lambda j: (0, u_block + j)),
        blk3((SSM_CH_BLOCK, SSM_ST_BLOCK)), blk3((SSM_CH_BLOCK, SSM_ST_BLOCK)),
        blk3((1, SSM_ST_BLOCK)), blk3((1, SSM_ST_BLOCK)),
        blk3((SSM_ST_BLOCK, SSM_CH_BLOCK)), blk3((SSM_ST_BLOCK, SSM_CH_BLOCK)),
        pl.BlockSpec((1, SSM_CH_BLOCK), lambda j: (0, j)),
    ]


def _chunks(t):
    rows = min(_S5_ROWS, t)
    return rows, lambda i: pl.ds(pl.multiple_of(i * rows, rows), rows)


def _s5_states(u_ref, us_ref, bre_ref, bim_ref, lr_ref, li_ref, sr_ref, si_ref, pr_ref, pi_ref, t):
    seg = t // SUBLANES
    rows, chunk = _chunks(t)
    for c in range(t // rows):
        us_ref[c * rows:(c + 1) * rows, :] = _load_segmented(u_ref, c * rows // SUBLANES, rows // SUBLANES, seg)

    def fill(i, _):
        ub = us_ref[chunk(i), :].astype(BF16)
        sr_ref[chunk(i), :] = _dot(ub, bre_ref[...], 1, 0)
        si_ref[chunk(i), :] = _dot(ub, bim_ref[...], 1, 0)
        return 0

    lax.fori_loop(0, t // rows, fill, 0)
    _fill_powers(lr_ref[...], li_ref[...], pr_ref, pi_ref, seg)
    _scan_segments(sr_ref, si_ref, pr_ref, pi_ref, lr_ref[...], li_ref[...], seg, False)


def _s5_scratch(t):
    state = pltpu.VMEM((t, SSM_ST_BLOCK), F32)
    powers = pltpu.VMEM((t // SUBLANES, SSM_ST_BLOCK), F32)
    return state, powers, pltpu.VMEM((t, SSM_CH_BLOCK), F32)


def _s5_fwd(proj, mats, dskip_row, d_attn, d_ssm):
    t = proj.shape[0]
    seg = t // SUBLANES
    n_blocks = d_ssm // SSM_CH_BLOCK
    rows, chunk = _chunks(t)

    def body(u_ref, bre_ref, bim_ref, lr_ref, li_ref, cre_ref, cim_ref, d_ref, y_ref,
             sr_ref, si_ref, pr_ref, pi_ref, us_ref, ys_ref):
        _s5_states(u_ref, us_ref, bre_ref, bim_ref, lr_ref, li_ref, sr_ref, si_ref, pr_ref, pi_ref, t)

        def emit(i, _):
            ys_ref[chunk(i), :] = (_dot(sr_ref[chunk(i), :].astype(BF16), cre_ref[...], 1, 0)
                                   - _dot(si_ref[chunk(i), :].astype(BF16), cim_ref[...], 1, 0)
                                   + d_ref[...] * us_ref[chunk(i), :])
            return 0

        lax.fori_loop(0, t // rows, emit, 0)
        for c in range(t // rows):
            _store_segmented(y_ref, c * rows // SUBLANES, seg, ys_ref[c * rows:(c + 1) * rows, :])

    state, powers, channels = _s5_scratch(t)
    col = pl.BlockSpec((t, SSM_CH_BLOCK), lambda j: (0, j))
    return _call(body, name="s5_fwd", grid=(n_blocks,), in_specs=_s5_in_specs(t, d_attn), out_specs=col,
                 out_shape=_sds((t, d_ssm), F32), scratch_shapes=[state, state, powers, powers, channels, channels],
                 semantics=("parallel",))(proj, *mats, dskip_row)


def _s5_bwd(proj, mats, dskip_row, y, dz_a, dz_b, d_attn, d_ssm, after):
    t = proj.shape[0]
    seg = t // SUBLANES
    n_blocks = d_ssm // SSM_CH_BLOCK
    rows, chunk = _chunks(t)

    def body(u_ref, bre_ref, bim_ref, lr_ref, li_ref, cre_ref, cim_ref, d_ref, y_ref, dza_ref, dzb_ref,
             du_ref, dbre_ref, dbim_ref, dlr_ref, dli_ref, dcre_ref, dcim_ref, dd_ref,
             sr_ref, si_ref, gr_ref, gi_ref, pr_ref, pi_ref, us_ref, dys_ref, dus_ref, acc_r, acc_i):
        _s5_states(u_ref, us_ref, bre_ref, bim_ref, lr_ref, li_ref, sr_ref, si_ref, pr_ref, pi_ref, t)
        for ref in (dcre_ref, dcim_ref, dbre_ref, dbim_ref, dd_ref, acc_r, acc_i):
            ref[...] = jnp.zeros_like(ref)
        for c in range(t // rows):
            tile0, n_tiles = c * rows // SUBLANES, rows // SUBLANES
            dz = _load_segmented(dza_ref, tile0, n_tiles, seg) + _load_segmented(dzb_ref, tile0, n_tiles, seg)
            dys_ref[c * rows:(c + 1) * rows, :] = dz * _gelu_grad(_load_segmented(y_ref, tile0, n_tiles, seg))

        def through_c(i, _):
            dy = dys_ref[chunk(i), :]
            dd_ref[...] += jnp.sum(dy * us_ref[chunk(i), :], axis=0, keepdims=True)
            dyb = dy.astype(BF16)
            gr_ref[chunk(i), :] = _dot(dyb, cre_ref[...], 1, 1)
            gi_ref[chunk(i), :] = -_dot(dyb, cim_ref[...], 1, 1)
            dcre_ref[...] += _dot(sr_ref[chunk(i), :].astype(BF16), dyb, 0, 0)
            dcim_ref[...] -= _dot(si_ref[chunk(i), :].astype(BF16), dyb, 0, 0)
            return 0

        lax.fori_loop(0, t // rows, through_c, 0)

        row = lax.broadcasted_iota(jnp.int32, (SUBLANES, SSM_ST_BLOCK), 0)
        last = pl.ds((seg - 1) * SUBLANES, SUBLANES)
        wrap = [jnp.where(row == 0, 0.0, pltpu.roll(ref[last, :], 1, 0)) for ref in (sr_ref, si_ref)]

        def lambda_grad(j, g_re, g_im):
            before = pl.ds(pl.multiple_of(jnp.maximum(j - 1, 0) * SUBLANES, SUBLANES), SUBLANES)
            prev_r = jnp.where(j > 0, sr_ref[before, :], wrap[0])
            prev_i = jnp.where(j > 0, si_ref[before, :], wrap[1])
            acc_r[...] += g_re * prev_r + g_im * prev_i
            acc_i[...] += g_im * prev_r - g_re * prev_i

        _scan_segments(gr_ref, gi_ref, pr_ref, pi_ref, lr_ref[...], li_ref[...], seg, True, per_tile=lambda_grad)
        dlr_ref[...] = jnp.sum(acc_r[...], axis=0, keepdims=True)
        dli_ref[...] = jnp.sum(acc_i[...], axis=0, keepdims=True)

        def through_b(i, _):
            ub = us_ref[chunk(i), :].astype(BF16)
            grb, gib = gr_ref[chunk(i), :].astype(BF16), gi_ref[chunk(i), :].astype(BF16)
            dbre_ref[...] += _dot(ub, grb, 0, 0)
            dbim_ref[...] += _dot(ub, gib, 0, 0)
            dus_ref[chunk(i), :] = (_dot(grb, bre_ref[...], 1, 1) + _dot(gib, bim_ref[...], 1, 1)
                                    + d_ref[...] * dys_ref[chunk(i), :])
            return 0

        lax.fori_loop(0, t // rows, through_b, 0)
        for c in range(t // rows):
            _store_segmented(du_ref, c * rows // SUBLANES, seg, dus_ref[c * rows:(c + 1) * rows, :])

    col = pl.BlockSpec((t, SSM_CH_BLOCK), lambda j: (0, j))
    blk3 = lambda shape: pl.BlockSpec((None,) + shape, lambda j: (j, 0, 0))
    state, powers, channels = _s5_scratch(t)
    return _call(
        body, name="s5_bwd", grid=(n_blocks,), in_specs=_s5_in_specs(t, d_attn) + [col, col, col],
        out_specs=[col, blk3((SSM_CH_BLOCK, SSM_ST_BLOCK)), blk3((SSM_CH_BLOCK, SSM_ST_BLOCK)),
                   blk3((1, SSM_ST_BLOCK)), blk3((1, SSM_ST_BLOCK)),
                   blk3((SSM_ST_BLOCK, SSM_CH_BLOCK)), blk3((SSM_ST_BLOCK, SSM_CH_BLOCK)),
                   pl.BlockSpec((1, SSM_CH_BLOCK), lambda j: (0, j))],
        out_shape=[_sds((t, d_ssm), F32),
                   _sds((n_blocks, SSM_CH_BLOCK, SSM_ST_BLOCK), F32), _sds((n_blocks, SSM_CH_BLOCK, SSM_ST_BLOCK), F32),
                   _sds((n_blocks, 1, SSM_ST_BLOCK), F32), _sds((n_blocks, 1, SSM_ST_BLOCK), F32),
                   _sds((n_blocks, SSM_ST_BLOCK, SSM_CH_BLOCK), F32), _sds((n_blocks, SSM_ST_BLOCK, SSM_CH_BLOCK), F32),
                   _sds((1, d_ssm), F32)],
        scratch_shapes=[state, state, state, state, powers, powers, channels, channels, channels,
                        pltpu.VMEM((SUBLANES, SSM_ST_BLOCK), F32), pltpu.VMEM((SUBLANES, SSM_ST_BLOCK), F32)],
        semantics=("parallel",), n_after=len(after))(proj, *mats, dskip_row, y, dz_a, dz_b, *after)


def _by_block(gp_n):
    return gp_n.reshape(-1, GROUPS_PER_BLOCK, SSM_GROUP, SSM_STATE)


def _block_diag_in(bbar):
    eye = jnp.eye(GROUPS_PER_BLOCK, dtype=F32)
    return jnp.einsum("jgpn,gh->jgphn", _by_block(bbar), eye).reshape(-1, SSM_CH_BLOCK, SSM_ST_BLOCK)


def _block_diag_in_t(dense):
    d5 = dense.reshape(-1, GROUPS_PER_BLOCK, SSM_GROUP, GROUPS_PER_BLOCK, SSM_STATE)
    eye = jnp.eye(GROUPS_PER_BLOCK, dtype=F32)
    return jnp.einsum("jgphn,gh->jgpn", d5, eye).reshape(-1, SSM_STATE)


def _block_diag_out(c):
    eye = jnp.eye(GROUPS_PER_BLOCK, dtype=F32)
    return jnp.einsum("jgpn,gh->jgnhp", _by_block(c), eye).reshape(-1, SSM_ST_BLOCK, SSM_CH_BLOCK)


def _block_diag_out_t(dense):
    d5 = dense.reshape(-1, GROUPS_PER_BLOCK, SSM_STATE, GROUPS_PER_BLOCK, SSM_GROUP)
    eye = jnp.eye(GROUPS_PER_BLOCK, dtype=F32)
    return jnp.einsum("jgnhp,gh->jgpn", d5, eye).reshape(-1, SSM_STATE)


def _adamw(w, g, m, v):
    m = ADAM_B1 * m + (1.0 - ADAM_B1) * g
    v = ADAM_B2 * v + (1.0 - ADAM_B2) * (g * g)
    m_hat = m / (1.0 - ADAM_B1 ** ADAM_STEP)
    v_hat = v / (1.0 - ADAM_B2 ** ADAM_STEP)
    delta = -ADAM_LR * (m_hat / (jnp.sqrt(v_hat) + ADAM_EPS) + ADAM_WD * w)
    return delta, m, v


def _adam_sharded(name, parts, w, m, v, tr, row0=0):
    r, c = w.shape
    assert r % tr == 0 and row0 % tr == 0, (name, r, tr, row0)

    def body(p_ref, w_ref, m_ref, v_ref, g_out, d_out, m_out, v_out):
        g = p_ref[0].astype(F32)
        for i in range(1, p_ref.shape[0]):
            g = g + p_ref[i].astype(F32)
        delta, m_new, v_new = _adamw(w_ref[...], g, m_ref[...], v_ref[...])
        g_out[...] = g
        d_out[...] = delta
        m_out[...] = m_new
        v_out[...] = v_new

    tile = pl.BlockSpec((tr, c), lambda i: (i, 0))
    return _call(body, name=name, grid=(r // tr,),
                 in_specs=[pl.BlockSpec((parts.shape[0], tr, c), lambda i: (0, i + row0 // tr, 0)), tile, tile, tile],
                 out_specs=[tile] * 4, out_shape=[_sds((r, c), F32)] * 4, semantics=("parallel",))(parts, w, m, v)


_BIG = ("w_in", "w_glu", "w_o", "w_gate", "w_up", "w_down")
_BY_COLUMNS = ("w_in", "w_gate", "w_up")
_SMALL_VECTORS = ("sinks", "log_dt", "b_glu", "g_attn_out", "g_ssm_out", "g_post_mix", "g_pre_ffn", "g_post_ffn")
_SMALL_MATRICES = ("b_re", "b_im", "c_re", "c_im", "a_re", "a_im")
_ORDER = ("g_pre_mix", "w_in", "sinks", "a_re", "a_im", "log_dt", "b_re", "b_im", "c_re", "c_im", "d_skip", "w_glu",
          "b_glu", "g_attn_out", "g_ssm_out", "w_o", "g_post_mix", "g_pre_ffn", "w_gate", "w_up", "w_down",
          "g_post_ffn")


def _pack_grads(vectors, matrices):
    width = max(a.shape[1] for a in vectors)
    slots, row, lane = [], 0, 0
    for a in vectors:
        span = -(-a.shape[1] // LANES) * LANES
        if lane + span > width:
            row, lane = row + 1, 0
        slots.append((row, lane, a.shape[1]))
        lane += span
    firsts, at = [], 0
    for a in matrices:
        firsts.append(at)
        at += a.shape[0]
    nv = len(vectors)

    def body(*refs):
        vec_out, mat_out = refs[-2], refs[-1]
        vec_out[...] = jnp.zeros_like(vec_out)
        for ref, (r, l, w) in zip(refs[:nv], slots):
            vec_out[r:r + 1, l:l + w] = ref[...]
        for ref, r0 in zip(refs[nv:-2], firsts):
            mat_out[r0:r0 + ref.shape[0], :] = ref[...]

    ins = list(vectors) + list(matrices)
    outs = [_sds((-(-(row + 1) // SUBLANES) * SUBLANES, width), F32), _sds((at, matrices[0].shape[1]), F32)]
    vec_pack, mat_pack = _call(body, name="pack_small_grads", in_specs=_whole(ins), out_specs=_whole(outs),
                               out_shape=outs)(*ins)
    return vec_pack, slots, mat_pack, firsts


def _adam_replicated(sources, found_at, w, m, v, total_at):
    ns, n = len(sources), len(w)

    def body(*refs):
        ins, outs = refs[ns:ns + 3 * n], refs[ns + 3 * n:]
        summed = []
        for p_ref in refs[:ns]:
            g = p_ref[0]
            for k in range(1, N_DEV):
                g = g + p_ref[k]
            summed.append(g)
        for i, (src, row, lane) in enumerate(found_at):
            w_ref, m_ref, v_ref = ins[i], ins[n + i], ins[2 * n + i]
            rows, cols = w_ref.shape
            g = summed[src][row:row + rows, lane:lane + cols]
            delta, m_new, v_new = _adamw(w_ref[...], g, m_ref[...], v_ref[...])
            for o, val in zip(outs[4 * i:4 * i + 4], (g, delta, m_new, v_new)):
                o[...] = val
        t_src, t_row, t_lane, t_width = total_at
        outs[-1][...] = summed[t_src][t_row:t_row + 1, t_lane:t_lane + t_width]

    ins = list(sources) + list(w) + list(m) + list(v)
    outs = [_sds(a.shape, F32) for a in w for _ in range(4)] + [_sds((1, total_at[3]), F32)]
    flat = _call(body, name="adam_replicated", in_specs=_whole(ins), out_specs=_whole(outs), out_shape=outs)(*ins)
    return [tuple(flat[4 * i:4 * i + 4]) for i in range(n)], flat[-1]


def kernel(x, positions, g_pre_mix, w_in, sinks, a_re, a_im, log_dt, b_re, b_im, c_re, c_im, d_skip, w_glu, b_glu, g_attn_out, g_ssm_out, w_o, g_post_mix, g_pre_ffn, w_gate, w_up, w_down, g_post_ffn, loss_target, m_g_pre_mix, m_w_in, m_sinks, m_a_re, m_a_im, m_log_dt, m_b_re, m_b_im, m_c_re, m_c_im, m_d_skip, m_w_glu, m_b_glu, m_g_attn_out, m_g_ssm_out, m_w_o, m_g_post_mix, m_g_pre_ffn, m_w_gate, m_w_up, m_w_down, m_g_post_ffn, v_g_pre_mix, v_w_in, v_sinks, v_a_re, v_a_im, v_log_dt, v_b_re, v_b_im, v_c_re, v_c_im, v_d_skip, v_w_glu, v_b_glu, v_g_attn_out, v_g_ssm_out, v_w_o, v_g_post_mix, v_g_pre_ffn, v_w_gate, v_w_up, v_w_down, v_g_post_ffn):
    given = dict(locals())
    weights = {n: given[n] for n in _ORDER}
    mom_m = {n: given["m_" + n] for n in _ORDER}
    mom_v = {n: given["v_" + n] for n in _ORDER}

    t, d = x.shape[1], x.shape[2]
    d_attn = d // 2
    d_ssm = d - d_attn
    d_in = d_attn + 2 * D_KV + d_ssm
    n_groups = d_ssm // SSM_GROUP
    n_heads = d_attn // HEAD_DIM
    tm = min(256, t)

    x2 = x[0]
    target = loss_target[0]

    def by_rows(n, a):
        return a[0].T if n in _BY_COLUMNS else a[0]

    def start_gather(name, ns, token):
        behind = 0 if token is None else token[0, 0].astype(BF16)
        shards = [by_rows(n, weights[n]).astype(BF16) + behind for n in ns]
        return _exchange_start(name, shards, False, (OWN, SIBLING) + CHIP_PEERS)

    def forward_gather(handle, after):
        return _forward_start(handle["name"] + "_forward", _exchange_wait(handle, after))

    def finish_gather(handle, after):
        return _split_wait(forward_gather(handle, after)[0], [])

    ag_in, token = start_gather("gather_w_in", ["w_in"], None)
    ag_mix, token = start_gather("gather_w_glu_o", ["w_glu", "w_o"], token)
    ag_ffn_in, token = start_gather("gather_w_gate_up", ["w_gate", "w_up"], token)
    ag_down, token = start_gather("gather_w_down", ["w_down"], token)

    xn, = _rows("norm_in", lambda xv, g: ([_rms(xv)[0] * g], []), [x2], [g_pre_mix], [(d, BF16)], [], tm,
                after=[token])
    win_g, = finish_gather(ag_in, [xn])
    w_in_t = win_g.reshape(d_in, d)
    proj = _mm_nt("proj_in", xn, w_in_t, F32)

    cos, sin = _rope_tables(positions.reshape(t, 1).astype(F32))
    sinks_row = jnp.pad(sinks, ((0, 0), (0, LANES - n_heads)))
    attn = _attention_fwd(proj, cos, sin, sinks_row, d_attn)

    def view(n, a):
        if n in ("b_re", "b_im"):
            return jnp.transpose(a[0], (0, 2, 1)).reshape(-1, SSM_STATE)
        if n in ("c_re", "c_im"):
            return a[0].reshape(-1, SSM_STATE)
        return a[0].T if n == "d_skip" else a[0] if a.ndim == 3 else a

    def unview(n, val):
        if n in ("b_re", "b_im"):
            return jnp.transpose(val.reshape(n_groups, SSM_GROUP, SSM_STATE), (0, 2, 1))[None]
        if n in ("c_re", "c_im"):
            return val.reshape(1, n_groups, SSM_GROUP, SSM_STATE)
        return val.T[None] if n == "d_skip" else val[None] if weights[n].ndim == 3 else val

    b_re_v, b_im_v = view("b_re", b_re), view("b_im", b_im)
    ldt_col = log_dt.reshape(n_groups, 1)
    lam_re, lam_im, bbar_re, bbar_im = _s5_discretise(a_re[0], a_im[0], ldt_col, b_re_v, b_im_v)
    n_blocks = n_groups // GROUPS_PER_BLOCK
    mats = [_block_diag_in(bbar_re).astype(BF16), _block_diag_in(bbar_im).astype(BF16),
            lam_re.reshape(n_blocks, 1, SSM_ST_BLOCK), lam_im.reshape(n_blocks, 1, SSM_ST_BLOCK),
            _block_diag_out(view("c_re", c_re)).astype(BF16), _block_diag_out(view("c_im", c_im)).astype(BF16)]
    dskip_row = d_skip.reshape(1, d_ssm)
    forward_mix, _ = forward_gather(ag_mix, [attn])
    y_ssm = _s5_fwd(proj, mats, dskip_row, d_attn, d_ssm)
    gelu_bf16 = lambda yv: _gelu(yv).astype(BF16)
    wglu_g, wo_g = _split_wait(forward_mix, [y_ssm])
    w_glu_full = wglu_g.reshape(d_ssm, d_ssm)
    w_o_full = wo_g.reshape(d, d)
    glu_lin = _mm_nn("glu_gate", y_ssm, w_glu_full, F32, a_fn=gelu_bf16)

    def mix_prep(av, yv, gl, bg, ga, gs):
        ssm = _gelu(yv) * _sigmoid(gl + bg)
        return [jnp.concatenate([_rms(av)[0] * ga, _rms(ssm)[0] * gs], axis=1)], []

    mixed, = _rows("mix_prep", mix_prep, [attn, y_ssm, glu_lin], [b_glu, g_attn_out, g_ssm_out], [(d, BF16)], [], tm)
    mix = _mm_nn("mix_out", mixed, w_o_full, F32)

    def post_mix(xv, mv, gpm, gpf):
        h = xv + _rms(mv)[0] * gpm
        return [h, _rms(h)[0] * gpf], []

    forward_ffn_in, token = forward_gather(ag_ffn_in, [mix])
    h, hn = _rows("post_mix", post_mix, [x2, mix], [g_post_mix, g_pre_ffn], [(d, F32), (d, BF16)], [], tm,
                  after=[token])
    wgate_g, wup_g = _split_wait(forward_ffn_in, [hn])
    d_ff = N_DEV * wgate_g.shape[1]
    wgate_t, wup_t = wgate_g.reshape(d_ff, d), wup_g.reshape(d_ff, d)
    forward_down, token = forward_gather(ag_down, [hn])
    gate, up, hid = _ffn_in(hn, wgate_t, wup_t, [token])
    wdown_g, = _split_wait(forward_down, [hid])
    wdown_full = wdown_g.reshape(d_ff, d)
    ff = _mm_nn("ffn_down", hid, wdown_full, F32, tm=1024, tn=512)

    def head(hv, fv, tv, gpo):
        out = hv + _rms(fv)[0] * gpo
        err = out - tv
        dout = err * (1.0 / d)
        dff, dg = _rms_bwd(fv, gpo, dout)
        loss = jnp.zeros((1, LANES), F32) + 0.5 * jnp.sum(err * err) * (1.0 / d)
        return [dff, dout], [dg, loss]

    dff, dh_out, dg_post_ffn, loss_row = _rows("loss_head", head, [h, ff, target], [g_post_ffn],
                                               [(d, BF16), (d, F32)], [d, LANES], tm)

    def swap_halves(name, grads):
        return _halves_start("swap_" + name, [g.reshape(N_DEV // 2, 2, *g.shape[1:]) for g in grads])

    def scatter_chip_sums(name, swap, after):
        both = _split_wait(swap, after)
        half = len(both) // 2
        sums = [_chip_sum("chip_sum_%s_%d" % (name, i), both[i], both[half + i]) for i in range(half)]
        return _exchange_start("scatter_" + name, sums, True, (OWN,) + CHIP_PEERS, by_chip=True)

    f_tile = _hidden_tile(d_ff)
    by_owner = lambda g: g.reshape(N_DEV, d_ff // N_DEV, d)
    dw_down = by_owner(_mm_tn("ffn_down_dw", hid, dff, BF16, tm=f_tile))
    swap_down, token = swap_halves("dw_down", [dw_down])
    dgate, dup = _ffn_down_bwd(dff, wdown_full, gate, up, [token])
    rs_down, token = scatter_chip_sums("dw_down", swap_down, [dgate])
    dhn_gate = _mm_nn("ffn_in_dx_gate", dgate, wgate_t, F32, tm=1024, tn=512, after=[token])
    dhn = _mm_nn("ffn_in_dx_up", dup, wup_t, F32, tm=1024, tn=512, plus=dhn_gate)
    dw_gate = by_owner(_mm_tn("ffn_gate_dw", dgate, hn, BF16, tm=f_tile))
    dw_up = by_owner(_mm_tn("ffn_up_dw", dup, hn, BF16, tm=f_tile))
    swap_ffn_in, tok_ffn_in = swap_halves("dw_gate_up", [dw_gate, dw_up])

    def mid_bwd(dho, dhn_, hv, mv, gpf, gpm):
        d1, dgpf = _rms_bwd(hv, gpf, dhn_)
        dh_ = dho + d1
        dmix_, dgpm = _rms_bwd(mv, gpm, dh_)
        return [dh_, dmix_], [dgpf, dgpm]

    dh, dmix, dg_pre_ffn, dg_post_mix = _rows("mid_bwd", mid_bwd, [dh_out, dhn, h, mix], [g_pre_ffn, g_post_mix],
                                              [(d, F32), (d, BF16)], [d, d], tm, after=[tok_ffn_in])

    dmixed = _mm_nt("mix_out_dx", dmix, w_o_full, F32)
    rs_ffn_in, token = scatter_chip_sums("dw_gate_up", swap_ffn_in, [dmixed])
    dw_o = _mm_tn("mix_out_dw", mixed, dmix, BF16, after=[token])
    swap_o, tok_o = swap_halves("dw_o", [dw_o.reshape(N_DEV, d // N_DEV, d)])

    def mix_bwd(dm, av, yv, gl, bg, ga, gs):
        dattn_, dga = _rms_bwd(av, ga, dm[:, :d_attn])
        z = _gelu(yv)
        sg = _sigmoid(gl + bg)
        dssm, dgs = _rms_bwd(z * sg, gs, dm[:, d_attn:])
        dgl = dssm * z * sg * (1.0 - sg)
        return [dattn_, dssm * sg, dgl], [dga, dgs, jnp.sum(dgl, axis=0, keepdims=True)]

    dattn, dz_direct, dglu, dg_attn_out, dg_ssm_out, db_glu = _rows(
        "mix_bwd", mix_bwd, [dmixed, attn, y_ssm, glu_lin], [b_glu, g_attn_out, g_ssm_out],
        [(d_attn, F32), (d_ssm, F32), (d_ssm, BF16)], [d_attn, d_ssm, d_ssm], tm, after=[tok_o])
    dz_glu = _mm_nt("glu_gate_dx", dglu, w_glu_full, F32)
    dw_glu = _mm_tn("glu_gate_dw", y_ssm, dglu, BF16, a_fn=gelu_bf16)
    rs_o, token = scatter_chip_sums("dw_o", swap_o, [dz_glu, dw_glu])

    du, db_re_dense, db_im_dense, dlam_re, dlam_im, dc_re_dense, dc_im_dense, dd_skip = _s5_bwd(
        proj, mats, dskip_row, y_ssm, dz_direct, dz_glu, d_attn, d_ssm, [token])
    da_re, da_im, dlog_dt, db_re_v, db_im_v = _s5_discretise_bwd(
        a_re[0], a_im[0], ldt_col, b_re_v, b_im_v, dlam_re.reshape(n_groups, SSM_STATE),
        dlam_im.reshape(n_groups, SSM_STATE), _block_diag_in_t(db_re_dense), _block_diag_in_t(db_im_dense))
    dq, dk2, dv2, dsinks_row = _attention_bwd(proj, cos, sin, sinks_row, dattn, d_attn)

    small_grads = {
        "sinks": dsinks_row, "a_re": da_re, "a_im": da_im, "log_dt": dlog_dt.reshape(1, n_groups),
        "b_re": db_re_v, "b_im": db_im_v, "c_re": _block_diag_out_t(dc_re_dense),
        "c_im": _block_diag_out_t(dc_im_dense), "d_skip": dd_skip.reshape(n_groups, SSM_GROUP).T, "b_glu": db_glu,
        "g_attn_out": dg_attn_out, "g_ssm_out": dg_ssm_out, "g_post_mix": dg_post_mix, "g_pre_ffn": dg_pre_ffn,
        "g_post_ffn": dg_post_ffn,
    }
    vec_pack, vec_slots, mat_pack, mat_rows = _pack_grads([small_grads[n] for n in _SMALL_VECTORS] + [loss_row],
                                                          [small_grads[n] for n in _SMALL_MATRICES])
    ag_small, token = _exchange_start("gather_small_grads", [vec_pack, mat_pack, small_grads["d_skip"]], False,
                                      (OWN,) + ALL_PEERS)
    dproj = _assemble_dproj(dq, dk2, dv2, du, d_in, [token])

    dw_in = _mm_tn("proj_in_dw", dproj, xn, BF16).reshape(N_DEV, d_in // N_DEV, d)
    swap_in, token = swap_halves("dw_in_glu", [dw_in, dw_glu.reshape(N_DEV, d_ssm // N_DEV, d_ssm)])
    dxn = _mm_nn("proj_in_dx", dproj, w_in_t, F32, after=[token])
    rs_in, token = scatter_chip_sums("dw_in_glu", swap_in, [dxn])

    def x_bwd(dh_, dxn_, xv, g):
        dx, dg = _rms_bwd(xv, g, dxn_)
        return [dh_ + dx], [dg]

    grad_x, dg_pre_mix = _rows("norm_in_bwd", x_bwd, [dh, dxn, x2], [g_pre_mix], [(d, F32)], [d], tm, after=[token])
    ag_last, token = _exchange_start("gather_g_pre_mix_grad", [dg_pre_mix], False, (OWN,) + ALL_PEERS)

    results = {}

    def adam_big(n, parts):
        r = parts.shape[1]
        tr = next((c for c in range(192, 15, -16) if r % c == 0), r)
        results[n] = _adam_sharded("adam_" + n, parts, by_rows(n, weights[n]), by_rows(n, mom_m[n]),
                                   by_rows(n, mom_v[n]), tr)
        return results[n][3]

    done = [grad_x, token]
    adam_big("w_down", _exchange_wait(rs_down, done)[0])
    p_gate, p_up = _exchange_wait(rs_ffn_in, done)
    done = [adam_big("w_gate", p_gate), adam_big("w_up", p_up), results["w_down"][3]]
    done = [adam_big("w_o", _exchange_wait(rs_o, done)[0])]
    vec_parts, mat_parts, dskip_parts = _exchange_wait(ag_small, done)
    for n, row0 in zip(_SMALL_MATRICES, mat_rows):
        rows = view(n, weights[n]).shape[0]
        results[n] = _adam_sharded("adam_" + n, mat_parts, view(n, weights[n]), view(n, mom_m[n]), view(n, mom_v[n]),
                                   rows, row0)
    p_in, p_glu = _exchange_wait(rs_in, [results[n][3] for n in _SMALL_MATRICES])
    done = [adam_big("w_in", p_in), adam_big("w_glu", p_glu)]
    first_gain_parts, = _exchange_wait(ag_last, done)
    rest = _SMALL_VECTORS + ("d_skip", "g_pre_mix")
    found_at = [(0, row, lane) for row, lane, _ in vec_slots[:-1]] + [(1, 0, 0), (2, 0, 0)]
    updated, loss_sum = _adam_replicated([vec_parts, dskip_parts, first_gain_parts], found_at,
                                         [view(n, weights[n]) for n in rest], [view(n, mom_m[n]) for n in rest],
                                         [view(n, mom_v[n]) for n in rest], (0,) + vec_slots[-1])
    results.update(zip(rest, updated))

    outs = [loss_sum[0, 0], grad_x[None]]
    for k in range(4):
        for n in _ORDER:
            val = results[n][k]
            outs.append(val.T[None] if n in _BY_COLUMNS else val[None] if n in _BIG else unview(n, val))
    return tuple(outs)
```

```python
import math

import jax
import jax.numpy as jnp
from jax import lax
from jax.experimental import pallas as pl
from jax.experimental.pallas import tpu as pltpu

F32 = jnp.float32
BF16 = jnp.bfloat16

HEAD_DIM = 64
N_KV_HEADS = 4
D_KV = N_KV_HEADS * HEAD_DIM
WINDOW = 128
BLOCK = 128
ROPE_THETA = 10000.0
SSM_GROUP = 16
SSM_STATE = 64
GROUPS_PER_BLOCK = 8
SSM_CH_BLOCK = GROUPS_PER_BLOCK * SSM_GROUP
SSM_ST_BLOCK = GROUPS_PER_BLOCK * SSM_STATE
RMS_EPS = 1e-6
N_DEV = 8
LANES = 128
SUBLANES = 8
MASKED = -1e30

ADAM_LR = 0.001
ADAM_B1 = 0.9
ADAM_B2 = 0.999
ADAM_EPS = 1e-08
ADAM_WD = 0.01
ADAM_STEP = 10

VMEM_LIMIT_BYTES = 56 * 1024 * 1024


def _call(body, *, name, out_shape, in_specs, out_specs, grid=(), scratch_shapes=(), semantics=None, n_after=0):
    params = dict(vmem_limit_bytes=VMEM_LIMIT_BYTES)
    if semantics is not None:
        params["dimension_semantics"] = semantics
    n_in = len(in_specs)
    if n_after:
        inner = body

        def body(*refs):
            inner(*refs[:n_in], *refs[n_in + n_after:])

        in_specs = list(in_specs) + [pl.BlockSpec(memory_space=pl.ANY)] * n_after
    return pl.pallas_call(body, name=name, grid=grid, in_specs=in_specs, out_specs=out_specs, out_shape=out_shape,
                          scratch_shapes=scratch_shapes, compiler_params=pltpu.CompilerParams(**params))


def _sds(shape, dtype):
    return jax.ShapeDtypeStruct(tuple(shape), dtype)


def _dot(a, b, ca, cb):
    return lax.dot_general(a, b, (((ca,), (cb,)), ((), ())), preferred_element_type=F32)


def _rms(x):
    r = lax.rsqrt(jnp.mean(x * x, axis=-1, keepdims=True) + RMS_EPS)
    return x * r, r


def _rms_bwd(x, g, dy):
    xh, r = _rms(x)
    dxh = dy * g
    dx = r * (dxh - xh * jnp.mean(dxh * xh, axis=-1, keepdims=True))
    return dx, jnp.sum(dy * xh, axis=0, keepdims=True)


def _sigmoid(x):
    return 1.0 / (1.0 + jnp.exp(-x))


_GELU_C = math.sqrt(2.0 / math.pi)
_GELU_A = 0.044715


def _gelu(y):
    t = jnp.tanh(_GELU_C * (y + _GELU_A * y * y * y))
    return 0.5 * y * (1.0 + t)


def _gelu_grad(y):
    t = jnp.tanh(_GELU_C * (y + _GELU_A * y * y * y))
    return 0.5 * (1.0 + t) + 0.5 * y * (1.0 - t * t) * _GELU_C * (1.0 + 3.0 * _GELU_A * y * y)


ROW_BUFFERS = 3


def _rows_piped(name, fn, row_ins, vec_ins, row_outs, acc_widths, tm, after):
    rows = row_ins[0].shape[0]
    n_steps = rows // tm
    n_row, n_vec, n_out, n_acc = len(row_ins), len(vec_ins), len(row_outs), len(acc_widths)

    def body(*refs):
        in_hbm, vecs = refs[:n_row], refs[n_row:n_row + n_vec]
        out_hbm = refs[n_row + n_vec:n_row + n_vec + n_out]
        accs = refs[n_row + n_vec + n_out:n_row + n_vec + n_out + n_acc]
        scratch = refs[n_row + n_vec + n_out + n_acc:]
        in_buf, out_buf = scratch[:n_row], scratch[n_row:n_row + n_out]
        in_sem, out_sem = scratch[n_row + n_out], scratch[n_row + n_out + 1]

        def tile_rows(step):
            return pl.ds(pl.multiple_of(step * tm, tm), tm)

        def fetch(a, step, slot):
            return pltpu.make_async_copy(in_hbm[a].at[tile_rows(step), :], in_buf[a].at[slot], in_sem.at[a, slot])

        def write(o, step, slot):
            return pltpu.make_async_copy(out_buf[o].at[slot], out_hbm[o].at[tile_rows(step), :], out_sem.at[o, slot])

        for s in range(ROW_BUFFERS):
            for a in range(n_row):
                fetch(a, s, s).start()
        for acc in accs:
            acc[...] = jnp.zeros_like(acc)
        vec_vals = [v[...] for v in vecs]

        def step(i, _):
            slot, out_slot = i % ROW_BUFFERS, i % 2
            for a in range(n_row):
                fetch(a, i, slot).wait()
            row_vals, acc_vals = fn(*[in_buf[a][slot] for a in range(n_row)], *vec_vals)

            @pl.when(i >= 2)
            def _():
                for o in range(n_out):
                    write(o, i - 2, out_slot).wait()

            for o, val in enumerate(row_vals):
                out_buf[o][out_slot] = val.astype(out_buf[o].dtype)
                write(o, i, out_slot).start()
            for acc, val in zip(accs, acc_vals):
                acc[...] += val

            @pl.when(i + ROW_BUFFERS < n_steps)
            def _():
                for a in range(n_row):
                    fetch(a, i + ROW_BUFFERS, slot).start()

            return 0

        lax.fori_loop(0, n_steps, step, 0)
        for s in range(n_steps - 2, n_steps):
            for o in range(n_out):
                write(o, s, s % 2).wait()

    any_spec = pl.BlockSpec(memory_space=pl.ANY)
    in_specs = [any_spec] * n_row + [pl.BlockSpec(v.shape, lambda: (0, 0)) for v in vec_ins]
    out_specs = [any_spec] * n_out + [pl.BlockSpec((1, w), lambda: (0, 0)) for w in acc_widths]
    out_shape = [_sds((rows, w), dt) for w, dt in row_outs] + [_sds((1, w), F32) for w in acc_widths]
    scratch = [pltpu.VMEM((ROW_BUFFERS, tm, a.shape[1]), a.dtype) for a in row_ins]
    scratch += [pltpu.VMEM((2, tm, w), dt) for w, dt in row_outs]
    scratch += [pltpu.SemaphoreType.DMA((n_row, ROW_BUFFERS)), pltpu.SemaphoreType.DMA((max(n_out, 1), 2))]
    return _call(body, name=name, in_specs=in_specs, out_specs=out_specs, out_shape=out_shape, scratch_shapes=scratch,
                 n_after=len(after))(*row_ins, *vec_ins, *after)


def _rows(name, fn, row_ins, vec_ins, row_outs, acc_widths, tm, after=()):
    rows = row_ins[0].shape[0]
    assert rows % tm == 0, (name, rows, tm)
    n_row, n_vec, n_out, n_acc = len(row_ins), len(vec_ins), len(row_outs), len(acc_widths)
    if rows // tm >= ROW_BUFFERS:
        return _rows_piped(name, fn, row_ins, vec_ins, row_outs, acc_widths, tm, after)

    def body(*refs):
        ins = [r[...] for r in refs[:n_row + n_vec]]
        outs = refs[n_row + n_vec:n_row + n_vec + n_out]
        accs = refs[n_row + n_vec + n_out:]
        row_vals, acc_vals = fn(*ins)
        for o, v in zip(outs, row_vals):
            o[...] = v.astype(o.dtype)
        if n_acc:
            @pl.when(pl.program_id(0) == 0)
            def _():
                for a in accs:
                    a[...] = jnp.zeros_like(a)
            for a, v in zip(accs, acc_vals):
                a[...] += v

    in_specs = [pl.BlockSpec((tm, a.shape[1]), lambda i: (i, 0)) for a in row_ins]
    in_specs += [pl.BlockSpec(v.shape, lambda i: (0, 0)) for v in vec_ins]
    out_specs = [pl.BlockSpec((tm, w), lambda i: (i, 0)) for w, _ in row_outs]
    out_specs += [pl.BlockSpec((1, w), lambda i: (0, 0)) for w in acc_widths]
    out_shape = [_sds((rows, w), dt) for w, dt in row_outs] + [_sds((1, w), F32) for w in acc_widths]
    return _call(body, name=name, grid=(rows // tm,), in_specs=in_specs, out_specs=out_specs, out_shape=out_shape,
                 semantics=("arbitrary",) if n_acc else ("parallel",), n_after=len(after))(*row_ins, *vec_ins, *after)


def _matmul(name, operands, in_specs, product, grid, out_shape, out_spec, acc_shape, after=()):
    nk = grid[-1]
    n_in = len(operands)
    in_place = out_shape.dtype == F32

    def body(*refs):
        ins = [r[...] for r in refs[:n_in]]
        o_ref = refs[n_in]
        if nk == 1:
            o_ref[...] = product(*ins).astype(o_ref.dtype)
            return
        acc = o_ref if in_place else refs[n_in + 1]
        k = pl.program_id(len(grid) - 1)

        @pl.when(k == 0)
        def _():
            acc[...] = jnp.zeros_like(acc)

        acc[...] += product(*ins)

        if not in_place:
            @pl.when(k == nk - 1)
            def _():
                o_ref[...] = acc[...].astype(o_ref.dtype)

    return _call(body, name=name, grid=grid, in_specs=in_specs, out_specs=out_spec, out_shape=out_shape,
                 scratch_shapes=[] if nk == 1 or in_place else [pltpu.VMEM(acc_shape, F32)],
                 semantics=("parallel",) * (len(grid) - 1) + ("arbitrary",), n_after=len(after))(*operands, *after)


def _mm_nn(name, a, b, out_dtype, tm=512, tn=None, a_fn=lambda x: x, after=(), plus=None):
    m, k = a.shape
    n = b.shape[1]
    tm, tn = min(tm, m), n if tn is None else min(tn, n)
    operands = [a, b] + ([] if plus is None else [plus])
    specs = [pl.BlockSpec((tm, k), lambda i, j, s: (i, 0)), pl.BlockSpec((k, tn), lambda i, j, s: (0, j))]
    specs += [] if plus is None else [pl.BlockSpec((tm, tn), lambda i, j, s: (i, j))]
    return _matmul(name, operands, specs, lambda x, y, *p: _dot(a_fn(x), y, 1, 0) + (p[0] if p else 0.0),
                   (m // tm, n // tn, 1), _sds((m, n), out_dtype),
                   pl.BlockSpec((tm, tn), lambda i, j, s: (i, j)), (tm, tn), after)


def _mm_nt(name, a, b, out_dtype, tm=512, tn=None):
    m, k = a.shape
    n = b.shape[0]
    tm, tn = min(tm, m), n if tn is None else tn
    return _matmul(name, [a, b],
                   [pl.BlockSpec((tm, k), lambda i, j, s: (i, 0)), pl.BlockSpec((tn, k), lambda i, j, s: (j, 0))],
                   lambda x, y: _dot(x, y, 1, 1), (m // tm, n // tn, 1), _sds((m, n), out_dtype),
                   pl.BlockSpec((tm, tn), lambda i, j, s: (i, j)), (tm, tn))


def _mm_tn(name, a, b, out_dtype, tm=512, tn=None, tk=2048, a_fn=lambda x: x, after=()):
    k, m = a.shape
    n = b.shape[1]
    tm, tk, tn = min(tm, m), min(tk, k), n if tn is None else tn
    return _matmul(name, [a, b],
                   [pl.BlockSpec((tk, tm), lambda i, j, s: (s, i)), pl.BlockSpec((tk, tn), lambda i, j, s: (s, j))],
                   lambda x, y: _dot(a_fn(x), y, 0, 0), (m // tm, n // tn, k // tk), _sds((m, n), out_dtype),
                   pl.BlockSpec((tm, tn), lambda i, j, s: (i, j)), (tm, tn), after)


def _hidden_tile(f):
    return 512 if f % 512 == 0 else 256


def _ffn_in(a, w_gate, w_up, tm=1024):
    m, k = a.shape
    f = w_gate.shape[0]
    tm, tn = min(tm, m), _hidden_tile(f)

    def body(a_ref, wg_ref, wu_ref, g_ref, u_ref, h_ref):
        x = a_ref[...]
        g = _dot(x, wg_ref[...], 1, 1)
        u = _dot(x, wu_ref[...], 1, 1)
        g_ref[...] = g.astype(BF16)
        u_ref[...] = u.astype(BF16)
        h_ref[...] = (g * _sigmoid(g) * u).astype(BF16)

    w_spec = pl.BlockSpec((tn, k), lambda j, i: (j, 0))
    o_spec = pl.BlockSpec((tm, tn), lambda j, i: (i, j))
    return _call(body, name="ffn_in", grid=(f // tn, m // tm),
                 in_specs=[pl.BlockSpec((tm, k), lambda j, i: (i, 0)), w_spec, w_spec], out_specs=[o_spec] * 3,
                 out_shape=[_sds((m, f), BF16)] * 3, semantics=("parallel", "parallel"))(a, w_gate, w_up)


def _ffn_down_bwd(d_out, w_down, gate, up, after, tm=1024):
    m, k = d_out.shape
    f = w_down.shape[0]
    tm, tn = min(tm, m), _hidden_tile(f)

    def body(d_ref, w_ref, g_ref, u_ref, dg_ref, du_ref):
        rows = pl.ds(pl.multiple_of(pl.program_id(1) * tm, tm), tm)
        dh = _dot(d_ref[rows, :], w_ref[...], 1, 1)
        g = g_ref[...].astype(F32)
        sg = _sigmoid(g)
        dg_ref[...] = (dh * u_ref[...].astype(F32) * sg * (1.0 + g * (1.0 - sg))).astype(BF16)
        du_ref[...] = (dh * g * sg).astype(BF16)

    t_spec = pl.BlockSpec((tm, tn), lambda j, i: (i, j))
    return _call(body, name="ffn_down_dx", grid=(f // tn, m // tm),
                 in_specs=[pl.BlockSpec((m, k), lambda j, i: (0, 0)), pl.BlockSpec((tn, k), lambda j, i: (j, 0)),
                           t_spec, t_spec],
                 out_specs=[t_spec] * 2, out_shape=[_sds((m, f), BF16)] * 2, semantics=("parallel", "parallel"),
                 n_after=len(after))(d_out, w_down, gate, up, *after)


ALL_PEERS = (1, 2, 3, 4, 5, 6, 7)
CHIP_PEERS = (2, 4, 6)
SIBLING = 1
OWN = 0


def _peer(relation):
    x, y, c = lax.axis_index("x"), lax.axis_index("y"), lax.axis_index("c")
    pos = (1 - x if relation & 4 else x, 1 - y if relation & 2 else y, 1 - c if relation & 1 else c)
    return pos, 4 * pos[0] + 2 * pos[1] + pos[2]


def _slot(relation, by_chip):
    pos, device = _peer(relation)
    return 2 * pos[0] + pos[1] if by_chip else device


def _exchange_copies(ins, lands, send_sems, recv_sems, scatter, relations, by_chip=False):
    me = _slot(0, by_chip)

    def copy(a, s, peer, pos, dst_slot):
        return pltpu.make_async_remote_copy(
            src_ref=ins[a].at[peer] if scatter else ins[a], dst_ref=lands[a].at[dst_slot],
            send_sem=send_sems.at[s], recv_sem=recv_sems.at[s], device_id=pos, device_id_type=pl.DeviceIdType.MESH)

    pairs = []
    for k, r in enumerate(relations):
        pos, peer = _peer(r)[0], _slot(r, by_chip)
        for a in range(len(ins)):
            s = a * len(relations) + k
            pairs.append((copy(a, s, peer, pos, me), copy(a, s, peer, pos, peer)))
    return pairs


def _halves_copies(arrays, lands, send_sems, recv_sems):
    sibling, _ = _peer(SIBLING)
    core = lax.axis_index("c")
    pairs = []
    for a, (ref, land) in enumerate(zip(arrays, lands)):
        send = pltpu.make_async_remote_copy(
            src_ref=ref.at[:, pl.ds(1 - core, 1)], dst_ref=land, send_sem=send_sems.at[a], recv_sem=recv_sems.at[a],
            device_id=sibling, device_id_type=pl.DeviceIdType.MESH)
        pairs.append((send, send))
    return pairs


def _forward_copies(lands, send_sems, recv_sems):
    sibling, _ = _peer(SIBLING)

    def copy(a, s, slot):
        return pltpu.make_async_remote_copy(
            src_ref=lands[a].at[slot], dst_ref=lands[a].at[slot], send_sem=send_sems.at[s], recv_sem=recv_sems.at[s],
            device_id=sibling, device_id_type=pl.DeviceIdType.MESH)

    pairs = []
    for k, r in enumerate(CHIP_PEERS):
        _, mine = _peer(r)
        _, theirs = _peer(r | SIBLING)
        for a in range(len(lands)):
            s = a * len(CHIP_PEERS) + k
            pairs.append((copy(a, s, mine), copy(a, s, theirs)))
    return pairs


_HBM_SPEC = pl.BlockSpec(memory_space=pltpu.HBM)
_SEM_SPEC = pl.BlockSpec(memory_space=pltpu.SEMAPHORE)
_SIDE_EFFECT = pltpu.SideEffectType.DATAFLOW_SIDE_EFFECTING


def _split_start(name, operands, n_sem, make_pairs):
    k = len(operands)

    def body(*refs):
        send_sems, recv_sems, token = refs[k], refs[k + 1], refs[-1]
        for send, _ in make_pairs(refs[:k], send_sems, recv_sems):
            send.start()
        token[...] = jnp.zeros_like(token)

    out = pl.pallas_call(
        body, name=name,
        out_shape=(pltpu.SemaphoreType.DMA((n_sem,)), pltpu.SemaphoreType.DMA((n_sem,)),
                   *[pltpu.HBM(a.shape, a.dtype) for a in operands], _sds((SUBLANES, LANES), F32)),
        in_specs=[_HBM_SPEC] * k,
        out_specs=(_SEM_SPEC, _SEM_SPEC, *[_HBM_SPEC] * k, pl.BlockSpec(memory_space=pltpu.VMEM)),
        input_output_aliases={i: 2 + i for i in range(k)},
        compiler_params=pltpu.CompilerParams(has_side_effects=_SIDE_EFFECT),
    )(*[pltpu.with_memory_space_constraint(a, pltpu.HBM) for a in operands])
    return dict(name=name, sems=out[:2], thru=list(out[2:2 + k]), make_pairs=make_pairs), out[-1]


def _split_wait(handle, after):
    thru, make_pairs = handle["thru"], handle["make_pairs"]
    k = len(thru)

    def body(*refs):
        for send, arrival in make_pairs(refs[:k], refs[k], refs[k + 1]):
            send.wait_send()
            arrival.wait_recv()

    return pl.pallas_call(
        body, name=handle["name"] + "_wait", out_shape=[pltpu.HBM(a.shape, a.dtype) for a in thru],
        in_specs=[_HBM_SPEC] * k + [_SEM_SPEC, _SEM_SPEC] + [pl.BlockSpec(memory_space=pl.ANY)] * len(after),
        out_specs=[_HBM_SPEC] * k, input_output_aliases={i: i for i in range(k)},
        compiler_params=pltpu.CompilerParams(has_side_effects=_SIDE_EFFECT),
    )(*thru, *handle["sems"], *after)


def _exchange_start(name, arrays, scatter, relations, by_chip=False):
    n = len(arrays)
    lands = [lax.empty(a.shape if scatter else (N_DEV,) + a.shape, a.dtype) for a in arrays]

    def make_pairs(refs, send_sems, recv_sems):
        return _exchange_copies(refs[:n], refs[n:], send_sems, recv_sems, scatter, relations, by_chip)

    handle, token = _split_start(name, list(arrays) + lands, n * len(relations), make_pairs)
    handle.update(n=n)
    return handle, token


def _halves_start(name, arrays):
    lands = [lax.empty((a.shape[0], 1) + a.shape[2:], a.dtype) for a in arrays]
    n = len(arrays)

    def make_pairs(refs, send_sems, recv_sems):
        return _halves_copies(refs[:n], refs[n:], send_sems, recv_sems)

    return _split_start(name, list(arrays) + lands, n, make_pairs)


def _chip_sum(name, array, landed):
    chips, _, r, c = array.shape
    tr = r

    def body(a_ref, b_ref, o_ref):
        mine = a_ref[lax.axis_index("c")].astype(F32)
        o_ref[...] = (mine + b_ref[...].astype(F32)).astype(o_ref.dtype)

    return _call(body, name=name, grid=(chips, r // tr),
                 in_specs=[pl.BlockSpec((None, 2, tr, c), lambda k, i: (k, 0, i, 0)),
                           pl.BlockSpec((None, None, tr, c), lambda k, i: (k, 0, i, 0))],
                 out_specs=pl.BlockSpec((None, tr, c), lambda k, i: (k, i, 0)),
                 out_shape=_sds((chips, r, c), BF16), semantics=("parallel", "parallel"))(array, landed)


def _forward_start(name, lands):
    return _split_start(name, list(lands), len(lands) * len(CHIP_PEERS), _forward_copies)


def _exchange_wait(handle, after):
    return _split_wait(handle, after)[handle["n"]:]


def _rope_tables(pos_col):
    t = pos_col.shape[0]
    half = HEAD_DIM // 2
    inv_freq = ROPE_THETA ** (-jnp.arange(half, dtype=F32) / half)
    inv_row = jnp.tile(inv_freq, LANES // half)[None, :]

    def body(pos_ref, inv_ref, cos_ref, sin_ref):
        ang = pos_ref[...] * inv_ref[...]
        cos_ref[...] = jnp.cos(ang)
        sin_ref[...] = jnp.sin(ang)

    tm = min(t, 512)
    return _call(body, name="rope_tables", grid=(t // tm,),
                 in_specs=[pl.BlockSpec((tm, 1), lambda i: (i, 0)), pl.BlockSpec((1, LANES), lambda i: (0, 0))],
                 out_specs=[pl.BlockSpec((tm, LANES), lambda i: (i, 0))] * 2,
                 out_shape=[_sds((t, LANES), F32)] * 2, semantics=("parallel",))(pos_col, inv_row)


def _rot_half(x):
    lane = lax.broadcasted_iota(jnp.int32, x.shape, 1)
    low = (lane % HEAD_DIM) < HEAD_DIM // 2
    return jnp.where(low, -pltpu.roll(x, LANES - HEAD_DIM // 2, 1), pltpu.roll(x, HEAD_DIM // 2, 1))


def _rope(x, cos, sin):
    return x * cos + _rot_half(x) * sin


def _unrope(d, cos, sin):
    return d * cos - _rot_half(d) * sin


def _band_mask(first_block, heads):
    r = lax.broadcasted_iota(jnp.int32, (heads * BLOCK, 2 * BLOCK), 0) % BLOCK
    c = lax.broadcasted_iota(jnp.int32, (heads * BLOCK, 2 * BLOCK), 1)
    diff = r - c + BLOCK
    return (diff >= 0) & (diff < WINDOW) & ((c >= BLOCK) | jnp.logical_not(first_block))


def _attn_specs(t, d_attn, d_in):
    kb, vb = d_attn // D_KV, d_attn // D_KV + 1
    prev = lambda i: jnp.maximum(i - 1, 0)
    return [
        pl.BlockSpec((BLOCK, d_attn), lambda i: (i, 0)),
        pl.BlockSpec((BLOCK, D_KV), lambda i: (i, kb)),
        pl.BlockSpec((BLOCK, D_KV), lambda i: (i, vb)),
        pl.BlockSpec((BLOCK, D_KV), lambda i: (prev(i), kb)),
        pl.BlockSpec((BLOCK, D_KV), lambda i: (prev(i), vb)),
        pl.BlockSpec((BLOCK, LANES), lambda i: (i, 0)),
        pl.BlockSpec((BLOCK, LANES), lambda i: (i, 0)),
        pl.BlockSpec((BLOCK, LANES), lambda i: (prev(i), 0)),
        pl.BlockSpec((BLOCK, LANES), lambda i: (prev(i), 0)),
        pl.BlockSpec((1, LANES), lambda i: (0, 0)),
    ]


def _head(x, h):
    return x[:, h * HEAD_DIM:(h + 1) * HEAD_DIM]


def _attn_heads(q_ref, kc_ref, vc_ref, kp_ref, vp_ref, cq_ref, sq_ref, cp_ref, sp_ref, d_attn):
    cq, sq, cp, sp = cq_ref[...], sq_ref[...], cp_ref[...], sp_ref[...]
    q_rot = [_rope(q_ref[:, j * LANES:(j + 1) * LANES], cq, sq) for j in range(d_attn // LANES)]
    kc_rot = [_rope(kc_ref[:, j * LANES:(j + 1) * LANES], cq, sq) for j in range(D_KV // LANES)]
    kp_rot = [_rope(kp_ref[:, j * LANES:(j + 1) * LANES], cp, sp) for j in range(D_KV // LANES)]
    per = LANES // HEAD_DIM
    q_heads = [_head(q_rot[h // per], h % per).astype(BF16) for h in range(d_attn // HEAD_DIM)]
    kk = [jnp.concatenate([_head(kp_rot[g // per], g % per), _head(kc_rot[g // per], g % per)], axis=0).astype(BF16)
          for g in range(N_KV_HEADS)]
    vv = [jnp.concatenate([_head(vp_ref[...], g), _head(vc_ref[...], g)], axis=0).astype(BF16) for g in range(N_KV_HEADS)]
    return q_heads, kk, vv


def _stack_group(q_heads, sink_ref, group):
    q_all = jnp.concatenate([q_heads[h] for h in group], axis=0)
    sink_all = jnp.concatenate([jnp.broadcast_to(sink_ref[:, h:h + 1], (BLOCK, 1)) for h in group], axis=0)
    return q_all, sink_all


def _softmax_with_sink(q, kk, sink, mask):
    s = _dot(q, kk, 1, 1) * (1.0 / math.sqrt(HEAD_DIM))
    s = jnp.where(mask, s, MASKED)
    m = jnp.maximum(jnp.max(s, axis=-1, keepdims=True), sink)
    p = jnp.exp(s - m)
    e_sink = jnp.exp(sink - m)
    inv = 1.0 / (jnp.sum(p, axis=-1, keepdims=True) + e_sink)
    return p * inv, e_sink * inv


def _attention_fwd(proj, cos, sin, sinks_row, d_attn):
    t, d_in = proj.shape
    n_heads = d_attn // HEAD_DIM
    q_per_kv = n_heads // N_KV_HEADS

    def body(q_ref, kc_ref, vc_ref, kp_ref, vp_ref, cq_ref, sq_ref, cp_ref, sp_ref, sink_ref, o_ref):
        mask = _band_mask(pl.program_id(0) == 0, q_per_kv)
        q_heads, kk, vv = _attn_heads(q_ref, kc_ref, vc_ref, kp_ref, vp_ref, cq_ref, sq_ref, cp_ref, sp_ref, d_attn)
        for g in range(N_KV_HEADS):
            group = range(g * q_per_kv, (g + 1) * q_per_kv)
            q_all, sink_all = _stack_group(q_heads, sink_ref, group)
            probs, _ = _softmax_with_sink(q_all, kk[g], sink_all, mask)
            o_all = _dot(probs.astype(BF16), vv[g], 1, 0)
            for k, h in enumerate(group):
                o_ref[:, h * HEAD_DIM:(h + 1) * HEAD_DIM] = o_all[k * BLOCK:(k + 1) * BLOCK]

    return _call(body, name="attention_fwd", grid=(t // BLOCK,), in_specs=_attn_specs(t, d_attn, d_in),
                 out_specs=pl.BlockSpec((BLOCK, d_attn), lambda i: (i, 0)), out_shape=_sds((t, d_attn), F32),
                 semantics=("parallel",))(proj, proj, proj, proj, proj, cos, sin, cos, sin, sinks_row)


def _attention_bwd(proj, cos, sin, sinks_row, d_out, d_attn):
    t, d_in = proj.shape
    n_heads = d_attn // HEAD_DIM
    q_per_kv = n_heads // N_KV_HEADS
    nb = t // BLOCK
    per = LANES // HEAD_DIM
    stack = q_per_kv

    def body(q_ref, kc_ref, vc_ref, kp_ref, vp_ref, cq_ref, sq_ref, cp_ref, sp_ref, sink_ref, do_ref,
             dq_ref, dk_ref, dv_ref, dsink_ref):
        i = pl.program_id(0)
        mask = _band_mask(i == 0, stack)
        q_heads, kk, vv = _attn_heads(q_ref, kc_ref, vc_ref, kp_ref, vp_ref, cq_ref, sq_ref, cp_ref, sp_ref, d_attn)
        lane = lax.broadcasted_iota(jnp.int32, (1, LANES), 1)
        dsink = jnp.zeros((1, LANES), F32)
        dq_rot, dkk, dvv = [], [], []
        for g in range(N_KV_HEADS):
            dkk_g = jnp.zeros((2 * BLOCK, HEAD_DIM), F32)
            dvv_g = jnp.zeros((2 * BLOCK, HEAD_DIM), F32)
            for first in range(g * q_per_kv, (g + 1) * q_per_kv, stack):
                group = range(first, first + stack)
                q_all, sink_all = _stack_group(q_heads, sink_ref, group)
                probs, p_sink = _softmax_with_sink(q_all, kk[g], sink_all, mask)
                do_all = jnp.concatenate([do_ref[:, h * HEAD_DIM:(h + 1) * HEAD_DIM] for h in group],
                                         axis=0).astype(BF16)
                dp = _dot(do_all, vv[g], 1, 1)
                delta = jnp.sum(probs * dp, axis=-1, keepdims=True)
                ds = (probs * (dp - delta) * (1.0 / math.sqrt(HEAD_DIM))).astype(BF16)
                dq_all = _dot(ds, kk[g], 1, 0)
                dkk_g += _dot(ds, q_all, 0, 0)
                dvv_g += _dot(probs.astype(BF16), do_all, 0, 0)
                sink_term = p_sink * delta
                for k, h in enumerate(group):
                    dq_rot.append(dq_all[k * BLOCK:(k + 1) * BLOCK])
                    part = jnp.sum(sink_term[k * BLOCK:(k + 1) * BLOCK], axis=0, keepdims=True)
                    dsink += jnp.where(lane == h, -part, 0.0)
            dkk.append(dkk_g)
            dvv.append(dvv_g)
        cq, sq, cp, sp = cq_ref[...], sq_ref[...], cp_ref[...], sp_ref[...]
        for j in range(d_attn // LANES):
            d = jnp.concatenate(dq_rot[j * per:(j + 1) * per], axis=1)
            dq_ref[:, j * LANES:(j + 1) * LANES] = _unrope(d, cq, sq)
        for j in range(D_KV // LANES):
            d = jnp.concatenate(dkk[j * per:(j + 1) * per], axis=1)
            dk_ref[0, :, j * LANES:(j + 1) * LANES] = _unrope(d[:BLOCK], cp, sp)
            dk_ref[1, :, j * LANES:(j + 1) * LANES] = _unrope(d[BLOCK:], cq, sq)
            d = jnp.concatenate(dvv[j * per:(j + 1) * per], axis=1)
            dv_ref[0, :, j * LANES:(j + 1) * LANES] = d[:BLOCK]
            dv_ref[1, :, j * LANES:(j + 1) * LANES] = d[BLOCK:]

        @pl.when(i == 0)
        def _():
            dsink_ref[...] = jnp.zeros_like(dsink_ref)

        dsink_ref[...] += dsink

    pair = pl.BlockSpec((2, BLOCK, D_KV), lambda i: (i, 0, 0))
    return _call(body, name="attention_bwd", grid=(nb,),
                 in_specs=_attn_specs(t, d_attn, d_in) + [pl.BlockSpec((BLOCK, d_attn), lambda i: (i, 0))],
                 out_specs=[pl.BlockSpec((BLOCK, d_attn), lambda i: (i, 0)), pair, pair,
                            pl.BlockSpec((1, LANES), lambda i: (0, 0))],
                 out_shape=[_sds((t, d_attn), F32), _sds((2 * nb, BLOCK, D_KV), F32), _sds((2 * nb, BLOCK, D_KV), F32),
                            _sds((1, LANES), F32)],
                 semantics=("arbitrary",))(proj, proj, proj, proj, proj, cos, sin, cos, sin, sinks_row, d_out)


def _assemble_dproj(dq, dk2, dv2, du, d_in, after):
    t, d_attn = dq.shape
    d_ssm = du.shape[1]
    nb = t // BLOCK

    def body(dq_ref, dk_own, dk_next, dv_own, dv_next, du_ref, o_ref):
        has_next = (pl.program_id(0) < nb - 1).astype(F32)
        o_ref[:, :d_attn] = dq_ref[...].astype(BF16)
        o_ref[:, d_attn:d_attn + D_KV] = (dk_own[...] + has_next * dk_next[...]).astype(BF16)
        o_ref[:, d_attn + D_KV:d_attn + 2 * D_KV] = (dv_own[...] + has_next * dv_next[...]).astype(BF16)
        o_ref[:, d_attn + 2 * D_KV:] = du_ref[...].astype(BF16)

    own = pl.BlockSpec((None, BLOCK, D_KV), lambda i: (2 * i + 1, 0, 0))
    nxt = pl.BlockSpec((None, BLOCK, D_KV), lambda i: (jnp.minimum(2 * i + 2, 2 * nb - 1), 0, 0))
    return _call(body, name="assemble_dproj", grid=(nb,),
                 in_specs=[pl.BlockSpec((BLOCK, d_attn), lambda i: (i, 0)), own, nxt, own, nxt,
                           pl.BlockSpec((BLOCK, d_ssm), lambda i: (i, 0))],
                 out_specs=pl.BlockSpec((BLOCK, d_in), lambda i: (i, 0)), out_shape=_sds((t, d_in), BF16),
                 semantics=("parallel",), n_after=len(after))(dq, dk2, dk2, dv2, dv2, du, *after)


def _discretise(ar, ai, ldt, br, bi):
    dt = jnp.exp(ldt)
    mag = jnp.exp(ar * dt)
    lam_re = mag * jnp.cos(ai * dt)
    lam_im = mag * jnp.sin(ai * dt)
    den = ar * ar + ai * ai
    nr = lam_re - 1.0
    ni = lam_im
    f_re = (nr * ar + ni * ai) / den
    f_im = (ni * ar - nr * ai) / den
    return (lam_re, lam_im, [f_re * r - f_im * i for r, i in zip(br, bi)], [f_re * i + f_im * r for r, i in zip(br, bi)])


def _whole(arrays):
    return [pl.BlockSpec(a.shape, lambda *_, nd=len(a.shape): (0,) * nd) for a in arrays]


def _channels(ref):
    groups = ref.shape[0] // SSM_GROUP
    return [ref[pl.ds(p, groups, stride=SSM_GROUP), :] for p in range(SSM_GROUP)]


def _store_channels(ref, values):
    groups = ref.shape[0] // SSM_GROUP
    for p, val in enumerate(values):
        ref[pl.ds(p, groups, stride=SSM_GROUP), :] = val


def _s5_discretise(ar, ai, ldt, br, bi):
    ins = [ar, ai, ldt, br, bi]

    def body(ar_ref, ai_ref, ldt_ref, br_ref, bi_ref, lr_ref, li_ref, bbr_ref, bbi_ref):
        lr, li, bbr, bbi = _discretise(ar_ref[...], ai_ref[...], ldt_ref[...], _channels(br_ref), _channels(bi_ref))
        lr_ref[...] = lr
        li_ref[...] = li
        _store_channels(bbr_ref, bbr)
        _store_channels(bbi_ref, bbi)

    outs = [_sds(ar.shape, F32), _sds(ar.shape, F32), _sds(br.shape, F32), _sds(br.shape, F32)]
    return _call(body, name="s5_discretise", in_specs=_whole(ins), out_specs=_whole(outs), out_shape=outs)(*ins)


def _s5_discretise_bwd(ar, ai, ldt, br, bi, d_lr, d_li, d_bbr, d_bbi):
    ins = [ar, ai, ldt, br, bi, d_lr, d_li, d_bbr, d_bbi]

    def body(ar_ref, ai_ref, ldt_ref, br_ref, bi_ref, dlr_ref, dli_ref, dbbr_ref, dbbi_ref,
             dar_ref, dai_ref, dldt_ref, dbr_ref, dbi_ref):
        _, vjp = jax.vjp(_discretise, ar_ref[...], ai_ref[...], ldt_ref[...], _channels(br_ref), _channels(bi_ref))
        dar, dai, dldt, dbr, dbi = vjp((dlr_ref[...], dli_ref[...], _channels(dbbr_ref), _channels(dbbi_ref)))
        dar_ref[...] = dar
        dai_ref[...] = dai
        dldt_ref[...] = dldt
        _store_channels(dbr_ref, dbr)
        _store_channels(dbi_ref, dbi)

    outs = [_sds(a.shape, F32) for a in (ar, ai, ldt, br, bi)]
    return _call(body, name="s5_discretise_bwd", in_specs=_whole(ins), out_specs=_whole(outs), out_shape=outs)(*ins)


def _cmul(ar, ai, br, bi):
    return ar * br - ai * bi, ar * bi + ai * br


def _load_segmented(ref, tile0, n_tiles, seg):
    return jnp.concatenate([ref[pl.ds(tile0 + j, SUBLANES, stride=seg), :] for j in range(n_tiles)], axis=0)


def _store_segmented(ref, tile0, seg, value):
    for j in range(value.shape[0] // SUBLANES):
        ref[pl.ds(tile0 + j, SUBLANES, stride=seg), :] = value[j * SUBLANES:(j + 1) * SUBLANES, :]


def _fill_powers(lr, li, pr_ref, pi_ref, seg):
    pows = [(lr, li)]
    for _ in range(SUBLANES - 1):
        pows.append(_cmul(pows[-1][0], pows[-1][1], lr, li))
    row = lax.broadcasted_iota(jnp.int32, (SUBLANES, lr.shape[1]), 0)
    tr = jnp.zeros((SUBLANES, lr.shape[1]), F32)
    ti = jnp.zeros((SUBLANES, lr.shape[1]), F32)
    for r in range(SUBLANES):
        tr = jnp.where(row == r, pows[r][0], tr)
        ti = jnp.where(row == r, pows[r][1], ti)
    pr_ref[0:SUBLANES, :] = tr
    pi_ref[0:SUBLANES, :] = ti
    k = SUBLANES
    while k < seg:
        fr, fi = pr_ref[k - 1:k, :], pi_ref[k - 1:k, :]
        for t0 in range(0, k, SUBLANES):
            nr, ni = _cmul(pr_ref[t0:t0 + SUBLANES, :], pi_ref[t0:t0 + SUBLANES, :], fr, fi)
            pr_ref[k + t0:k + t0 + SUBLANES, :] = nr
            pi_ref[k + t0:k + t0 + SUBLANES, :] = ni
        k *= 2


def _scan_segments(sr_ref, si_ref, pr_ref, pi_ref, lr, li, seg, reverse, per_tile=None):
    w = lr.shape[1]
    sign = -1.0 if reverse else 1.0
    lrb = jnp.broadcast_to(lr, (SUBLANES, w))
    lib = jnp.broadcast_to(sign * li, (SUBLANES, w))
    zero = jnp.zeros((SUBLANES, w), F32)

    def tile_rows(j):
        return pl.ds(pl.multiple_of(j * SUBLANES, SUBLANES), SUBLANES)

    steps = 4 if seg % 4 == 0 else 1

    def local(i, carry):
        for u in range(steps):
            j = i * steps + u
            rows = tile_rows(seg - 1 - j if reverse else j)
            pr, pi = _cmul(lrb, lib, carry[0], carry[1])
            carry = (sr_ref[rows, :] + pr, si_ref[rows, :] + pi)
            sr_ref[rows, :] = carry[0]
            si_ref[rows, :] = carry[1]
        return carry

    end_r, end_i = lax.fori_loop(0, seg // steps, local, (zero, zero))
    full_r, full_i = pr_ref[seg - 1:seg, :], sign * pi_ref[seg - 1:seg, :]
    row = lax.broadcasted_iota(jnp.int32, (SUBLANES, w), 0)
    in_r, in_i = zero, zero
    cur_r, cur_i = jnp.zeros((1, w), F32), jnp.zeros((1, w), F32)
    for r in (range(SUBLANES - 2, -1, -1) if reverse else range(1, SUBLANES)):
        src = r + 1 if reverse else r - 1
        pr, pi = _cmul(full_r, full_i, cur_r, cur_i)
        cur_r, cur_i = end_r[src:src + 1, :] + pr, end_i[src:src + 1, :] + pi
        in_r = jnp.where(row == r, cur_r, in_r)
        in_i = jnp.where(row == r, cur_i, in_i)

    def carry_in(j, _):
        rows = tile_rows(j)
        k = seg - 1 - j if reverse else j
        pr, pi = _cmul(pr_ref[pl.ds(k, 1), :], sign * pi_ref[pl.ds(k, 1), :], in_r, in_i)
        xr, xi = sr_ref[rows, :] + pr, si_ref[rows, :] + pi
        sr_ref[rows, :] = xr
        si_ref[rows, :] = xi
        if per_tile is not None:
            per_tile(j, xr, xi)
        return 0

    lax.fori_loop(0, seg, carry_in, 0, unroll=4)


_S5_ROWS = 2048


def _s5_in_specs(t, d_attn):
    u_block = (d_attn + 2 * D_KV) // SSM_CH_BLOCK
    blk3 = lambda shape: pl.BlockSpec((None,) + shape, lambda j: (j, 0, 0))
    return [
        pl.BlockSpec((t, SSM_CH_BLOCK), lambda j: (0, u_block + j)),
        blk3((SSM_CH_BLOCK, SSM_ST_BLOCK)), blk3((SSM_CH_BLOCK, SSM_ST_BLOCK)),
        blk3((1, SSM_ST_BLOCK)), blk3((1, SSM_ST_BLOCK)),
        blk3((SSM_ST_BLOCK, SSM_CH_BLOCK)), blk3((SSM_ST_BLOCK, SSM_CH_BLOCK)),
        pl.BlockSpec((1, SSM_CH_BLOCK), lambda j: (0, j)),
    ]


def _chunks(t):
    rows = min(_S5_ROWS, t)
    return rows, lambda i: pl.ds(pl.multiple_of(i * rows, rows), rows)


def _s5_states(u_ref, us_ref, bre_ref, bim_ref, lr_ref, li_ref, sr_ref, si_ref, pr_ref, pi_ref, t):
    seg = t // SUBLANES
    rows, chunk = _chunks(t)
    for c in range(t // rows):
        us_ref[c * rows:(c + 1) * rows, :] = _load_segmented(u_ref, c * rows // SUBLANES, rows // SUBLANES, seg)

    def fill(i, _):
        ub = us_ref[chunk(i), :].astype(BF16)
        sr_ref[chunk(i), :] = _dot(ub, bre_ref[...], 1, 0)
        si_ref[chunk(i), :] = _dot(ub, bim_ref[...], 1, 0)
        return 0

    lax.fori_loop(0, t // rows, fill, 0)
    _fill_powers(lr_ref[...], li_ref[...], pr_ref, pi_ref, seg)
    _scan_segments(sr_ref, si_ref, pr_ref, pi_ref, lr_ref[...], li_ref[...], seg, False)


def _s5_scratch(t):
    state = pltpu.VMEM((t, SSM_ST_BLOCK), F32)
    powers = pltpu.VMEM((t // SUBLANES, SSM_ST_BLOCK), F32)
    return state, powers, pltpu.VMEM((t, SSM_CH_BLOCK), F32)


def _s5_fwd(proj, mats, dskip_row, d_attn, d_ssm):
    t = proj.shape[0]
    seg = t // SUBLANES
    n_blocks = d_ssm // SSM_CH_BLOCK
    rows, chunk = _chunks(t)

    def body(u_ref, bre_ref, bim_ref, lr_ref, li_ref, cre_ref, cim_ref, d_ref, y_ref,
             sr_ref, si_ref, pr_ref, pi_ref, us_ref, ys_ref):
        _s5_states(u_ref, us_ref, bre_ref, bim_ref, lr_ref, li_ref, sr_ref, si_ref, pr_ref, pi_ref, t)

        def emit(i, _):
            ys_ref[chunk(i), :] = (_dot(sr_ref[chunk(i), :].astype(BF16), cre_ref[...], 1, 0)
                                   - _dot(si_ref[chunk(i), :].astype(BF16), cim_ref[...], 1, 0)
                                   + d_ref[...] * us_ref[chunk(i), :])
            return 0

        lax.fori_loop(0, t // rows, emit, 0)
        for c in range(t // rows):
            _store_segmented(y_ref, c * rows // SUBLANES, seg, ys_ref[c * rows:(c + 1) * rows, :])

    state, powers, channels = _s5_scratch(t)
    col = pl.BlockSpec((t, SSM_CH_BLOCK), lambda j: (0, j))
    return _call(body, name="s5_fwd", grid=(n_blocks,), in_specs=_s5_in_specs(t, d_attn), out_specs=col,
                 out_shape=_sds((t, d_ssm), F32), scratch_shapes=[state, state, powers, powers, channels, channels],
                 semantics=("parallel",))(proj, *mats, dskip_row)


def _s5_bwd(proj, mats, dskip_row, y, dz_a, dz_b, d_attn, d_ssm, after):
    t = proj.shape[0]
    seg = t // SUBLANES
    n_blocks = d_ssm // SSM_CH_BLOCK
    rows, chunk = _chunks(t)

    def body(u_ref, bre_ref, bim_ref, lr_ref, li_ref, cre_ref, cim_ref, d_ref, y_ref, dza_ref, dzb_ref,
             du_ref, dbre_ref, dbim_ref, dlr_ref, dli_ref, dcre_ref, dcim_ref, dd_ref,
             sr_ref, si_ref, gr_ref, gi_ref, pr_ref, pi_ref, us_ref, dys_ref, dus_ref, acc_r, acc_i):
        _s5_states(u_ref, us_ref, bre_ref, bim_ref, lr_ref, li_ref, sr_ref, si_ref, pr_ref, pi_ref, t)
        for ref in (dcre_ref, dcim_ref, dbre_ref, dbim_ref, dd_ref, acc_r, acc_i):
            ref[...] = jnp.zeros_like(ref)
        for c in range(t // rows):
            tile0, n_tiles = c * rows // SUBLANES, rows // SUBLANES
            dz = _load_segmented(dza_ref, tile0, n_tiles, seg) + _load_segmented(dzb_ref, tile0, n_tiles, seg)
            dys_ref[c * rows:(c + 1) * rows, :] = dz * _gelu_grad(_load_segmented(y_ref, tile0, n_tiles, seg))

        def through_c(i, _):
            dy = dys_ref[chunk(i), :]
            dd_ref[...] += jnp.sum(dy * us_ref[chunk(i), :], axis=0, keepdims=True)
            dyb = dy.astype(BF16)
            gr_ref[chunk(i), :] = _dot(dyb, cre_ref[...], 1, 1)
            gi_ref[chunk(i), :] = -_dot(dyb, cim_ref[...], 1, 1)
            dcre_ref[...] += _dot(sr_ref[chunk(i), :].astype(BF16), dyb, 0, 0)
            dcim_ref[...] -= _dot(si_ref[chunk(i), :].astype(BF16), dyb, 0, 0)
            return 0

        lax.fori_loop(0, t // rows, through_c, 0)

        row = lax.broadcasted_iota(jnp.int32, (SUBLANES, SSM_ST_BLOCK), 0)
        last = pl.ds((seg - 1) * SUBLANES, SUBLANES)
        wrap = [jnp.where(row == 0, 0.0, pltpu.roll(ref[last, :], 1, 0)) for ref in (sr_ref, si_ref)]

        def lambda_grad(j, g_re, g_im):
            before = pl.ds(pl.multiple_of(jnp.maximum(j - 1, 0) * SUBLANES, SUBLANES), SUBLANES)
            prev_r = jnp.where(j > 0, sr_ref[before, :], wrap[0])
            prev_i = jnp.where(j > 0, si_ref[before, :], wrap[1])
            acc_r[...] += g_re * prev_r + g_im * prev_i
            acc_i[...] += g_im * prev_r - g_re * prev_i

        _scan_segments(gr_ref, gi_ref, pr_ref, pi_ref, lr_ref[...], li_ref[...], seg, True, per_tile=lambda_grad)
        dlr_ref[...] = jnp.sum(acc_r[...], axis=0, keepdims=True)
        dli_ref[...] = jnp.sum(acc_i[...], axis=0, keepdims=True)

        def through_b(i, _):
            ub = us_ref[chunk(i), :].astype(BF16)
            grb, gib = gr_ref[chunk(i), :].astype(BF16), gi_ref[chunk(i), :].astype(BF16)
            dbre_ref[...] += _dot(ub, grb, 0, 0)
            dbim_ref[...] += _dot(ub, gib, 0, 0)
            dus_ref[chunk(i), :] = (_dot(grb, bre_ref[...], 1, 1) + _dot(gib, bim_ref[...], 1, 1)
                                    + d_ref[...] * dys_ref[chunk(i), :])
            return 0

        lax.fori_loop(0, t // rows, through_b, 0)
        for c in range(t // rows):
            _store_segmented(du_ref, c * rows // SUBLANES, seg, dus_ref[c * rows:(c + 1) * rows, :])

    col = pl.BlockSpec((t, SSM_CH_BLOCK), lambda j: (0, j))
    blk3 = lambda shape: pl.BlockSpec((None,) + shape, lambda j: (j, 0, 0))
    state, powers, channels = _s5_scratch(t)
    return _call(
        body, name="s5_bwd", grid=(n_blocks,), in_specs=_s5_in_specs(t, d_attn) + [col, col, col],
        out_specs=[col, blk3((SSM_CH_BLOCK, SSM_ST_BLOCK)), blk3((SSM_CH_BLOCK, SSM_ST_BLOCK)),
                   blk3((1, SSM_ST_BLOCK)), blk3((1, SSM_ST_BLOCK)),
                   blk3((SSM_ST_BLOCK, SSM_CH_BLOCK)), blk3((SSM_ST_BLOCK, SSM_CH_BLOCK)),
                   pl.BlockSpec((1, SSM_CH_BLOCK), lambda j: (0, j))],
        out_shape=[_sds((t, d_ssm), F32),
                   _sds((n_blocks, SSM_CH_BLOCK, SSM_ST_BLOCK), F32), _sds((n_blocks, SSM_CH_BLOCK, SSM_ST_BLOCK), F32),
                   _sds((n_blocks, 1, SSM_ST_BLOCK), F32), _sds((n_blocks, 1, SSM_ST_BLOCK), F32),
                   _sds((n_blocks, SSM_ST_BLOCK, SSM_CH_BLOCK), F32), _sds((n_blocks, SSM_ST_BLOCK, SSM_CH_BLOCK), F32),
                   _sds((1, d_ssm), F32)],
        scratch_shapes=[state, state, state, state, powers, powers, channels, channels, channels,
                        pltpu.VMEM((SUBLANES, SSM_ST_BLOCK), F32), pltpu.VMEM((SUBLANES, SSM_ST_BLOCK), F32)],
        semantics=("parallel",), n_after=len(after))(proj, *mats, dskip_row, y, dz_a, dz_b, *after)


def _by_block(gp_n):
    return gp_n.reshape(-1, GROUPS_PER_BLOCK, SSM_GROUP, SSM_STATE)


def _block_diag_in(bbar):
    eye = jnp.eye(GROUPS_PER_BLOCK, dtype=F32)
    return jnp.einsum("jgpn,gh->jgphn", _by_block(bbar), eye).reshape(-1, SSM_CH_BLOCK, SSM_ST_BLOCK)


def _block_diag_in_t(dense):
    d5 = dense.reshape(-1, GROUPS_PER_BLOCK, SSM_GROUP, GROUPS_PER_BLOCK, SSM_STATE)
    eye = jnp.eye(GROUPS_PER_BLOCK, dtype=F32)
    return jnp.einsum("jgphn,gh->jgpn", d5, eye).reshape(-1, SSM_STATE)


def _block_diag_out(c):
    eye = jnp.eye(GROUPS_PER_BLOCK, dtype=F32)
    return jnp.einsum("jgpn,gh->jgnhp", _by_block(c), eye).reshape(-1, SSM_ST_BLOCK, SSM_CH_BLOCK)


def _block_diag_out_t(dense):
    d5 = dense.reshape(-1, GROUPS_PER_BLOCK, SSM_STATE, GROUPS_PER_BLOCK, SSM_GROUP)
    eye = jnp.eye(GROUPS_PER_BLOCK, dtype=F32)
    return jnp.einsum("jgnhp,gh->jgpn", d5, eye).reshape(-1, SSM_STATE)


def _adamw(w, g, m, v):
    m = ADAM_B1 * m + (1.0 - ADAM_B1) * g
    v = ADAM_B2 * v + (1.0 - ADAM_B2) * (g * g)
    m_hat = m / (1.0 - ADAM_B1 ** ADAM_STEP)
    v_hat = v / (1.0 - ADAM_B2 ** ADAM_STEP)
    delta = -ADAM_LR * (m_hat / (jnp.sqrt(v_hat) + ADAM_EPS) + ADAM_WD * w)
    return delta, m, v


def _adam_sharded(name, parts, w, m, v, tr, row0=0):
    r, c = w.shape
    assert r % tr == 0 and row0 % tr == 0, (name, r, tr, row0)

    def body(p_ref, w_ref, m_ref, v_ref, g_out, d_out, m_out, v_out):
        g = p_ref[0].astype(F32)
        for i in range(1, p_ref.shape[0]):
            g = g + p_ref[i].astype(F32)
        delta, m_new, v_new = _adamw(w_ref[...], g, m_ref[...], v_ref[...])
        g_out[...] = g
        d_out[...] = delta
        m_out[...] = m_new
        v_out[...] = v_new

    tile = pl.BlockSpec((tr, c), lambda i: (i, 0))
    return _call(body, name=name, grid=(r // tr,),
                 in_specs=[pl.BlockSpec((parts.shape[0], tr, c), lambda i: (0, i + row0 // tr, 0)), tile, tile, tile],
                 out_specs=[tile] * 4, out_shape=[_sds((r, c), F32)] * 4, semantics=("parallel",))(parts, w, m, v)


_BIG = ("w_in", "w_glu", "w_o", "w_gate", "w_up", "w_down")
_BY_COLUMNS = ("w_in", "w_gate", "w_up")
_SMALL_VECTORS = ("sinks", "log_dt", "b_glu", "g_attn_out", "g_ssm_out", "g_post_mix", "g_pre_ffn", "g_post_ffn")
_SMALL_MATRICES = ("b_re", "b_im", "c_re", "c_im", "a_re", "a_im")
_ORDER = ("g_pre_mix", "w_in", "sinks", "a_re", "a_im", "log_dt", "b_re", "b_im", "c_re", "c_im", "d_skip", "w_glu",
          "b_glu", "g_attn_out", "g_ssm_out", "w_o", "g_post_mix", "g_pre_ffn", "w_gate", "w_up", "w_down",
          "g_post_ffn")


def _pack_grads(vectors, matrices):
    width = max(a.shape[1] for a in vectors)
    slots, row, lane = [], 0, 0
    for a in vectors:
        span = -(-a.shape[1] // LANES) * LANES
        if lane + span > width:
            row, lane = row + 1, 0
        slots.append((row, lane, a.shape[1]))
        lane += span
    firsts, at = [], 0
    for a in matrices:
        firsts.append(at)
        at += a.shape[0]
    nv = len(vectors)

    def body(*refs):
        vec_out, mat_out = refs[-2], refs[-1]
        vec_out[...] = jnp.zeros_like(vec_out)
        for ref, (r, l, w) in zip(refs[:nv], slots):
            vec_out[r:r + 1, l:l + w] = ref[...]
        for ref, r0 in zip(refs[nv:-2], firsts):
            mat_out[r0:r0 + ref.shape[0], :] = ref[...]

    ins = list(vectors) + list(matrices)
    outs = [_sds((-(-(row + 1) // SUBLANES) * SUBLANES, width), F32), _sds((at, matrices[0].shape[1]), F32)]
    vec_pack, mat_pack = _call(body, name="pack_small_grads", in_specs=_whole(ins), out_specs=_whole(outs),
                               out_shape=outs)(*ins)
    return vec_pack, slots, mat_pack, firsts


def _adam_replicated(sources, found_at, w, m, v, total_at):
    ns, n = len(sources), len(w)

    def body(*refs):
        ins, outs = refs[ns:ns + 3 * n], refs[ns + 3 * n:]
        summed = []
        for p_ref in refs[:ns]:
            g = p_ref[0]
            for k in range(1, N_DEV):
                g = g + p_ref[k]
            summed.append(g)
        for i, (src, row, lane) in enumerate(found_at):
            w_ref, m_ref, v_ref = ins[i], ins[n + i], ins[2 * n + i]
            rows, cols = w_ref.shape
            g = summed[src][row:row + rows, lane:lane + cols]
            delta, m_new, v_new = _adamw(w_ref[...], g, m_ref[...], v_ref[...])
            for o, val in zip(outs[4 * i:4 * i + 4], (g, delta, m_new, v_new)):
                o[...] = val
        t_src, t_row, t_lane, t_width = total_at
        outs[-1][...] = summed[t_src][t_row:t_row + 1, t_lane:t_lane + t_width]

    ins = list(sources) + list(w) + list(m) + list(v)
    outs = [_sds(a.shape, F32) for a in w for _ in range(4)] + [_sds((1, total_at[3]), F32)]
    flat = _call(body, name="adam_replicated", in_specs=_whole(ins), out_specs=_whole(outs), out_shape=outs)(*ins)
    return [tuple(flat[4 * i:4 * i + 4]) for i in range(n)], flat[-1]


def kernel(x, positions, g_pre_mix, w_in, sinks, a_re, a_im, log_dt, b_re, b_im, c_re, c_im, d_skip, w_glu, b_glu, g_attn_out, g_ssm_out, w_o, g_post_mix, g_pre_ffn, w_gate, w_up, w_down, g_post_ffn, loss_target, m_g_pre_mix, m_w_in, m_sinks, m_a_re, m_a_im, m_log_dt, m_b_re, m_b_im, m_c_re, m_c_im, m_d_skip, m_w_glu, m_b_glu, m_g_attn_out, m_g_ssm_out, m_w_o, m_g_post_mix, m_g_pre_ffn, m_w_gate, m_w_up, m_w_down, m_g_post_ffn, v_g_pre_mix, v_w_in, v_sinks, v_a_re, v_a_im, v_log_dt, v_b_re, v_b_im, v_c_re, v_c_im, v_d_skip, v_w_glu, v_b_glu, v_g_attn_out, v_g_ssm_out, v_w_o, v_g_post_mix, v_g_pre_ffn, v_w_gate, v_w_up, v_w_down, v_g_post_ffn):
    given = dict(locals())
    weights = {n: given[n] for n in _ORDER}
    mom_m = {n: given["m_" + n] for n in _ORDER}
    mom_v = {n: given["v_" + n] for n in _ORDER}

    t, d = x.shape[1], x.shape[2]
    d_attn = d // 2
    d_ssm = d - d_attn
    d_in = d_attn + 2 * D_KV + d_ssm
    n_groups = d_ssm // SSM_GROUP
    n_heads = d_attn // HEAD_DIM
    tm = min(256, t)

    x2 = x[0]
    target = loss_target[0]

    def by_rows(n, a):
        return a[0].T if n in _BY_COLUMNS else a[0]

    def start_gather(name, ns, token):
        behind = 0 if token is None else token[0, 0].astype(BF16)
        shards = [by_rows(n, weights[n]).astype(BF16) + behind for n in ns]
        return _exchange_start(name, shards, False, (OWN, SIBLING) + CHIP_PEERS)

    def forward_gather(handle, after):
        return _forward_start(handle["name"] + "_forward", _exchange_wait(handle, after))

    def finish_gather(handle, after):
        return _split_wait(forward_gather(handle, after)[0], [])

    ag_in, token = start_gather("gather_w_in", ["w_in"], None)
    ag_mix, token = start_gather("gather_w_glu_o", ["w_glu", "w_o"], token)
    ag_ffn_in, token = start_gather("gather_w_gate_up", ["w_gate", "w_up"], token)
    ag_down, token = start_gather("gather_w_down", ["w_down"], token)

    xn, = _rows("norm_in", lambda xv, g: ([_rms(xv)[0] * g], []), [x2], [g_pre_mix], [(d, BF16)], [], tm,
                after=[token])
    win_g, = finish_gather(ag_in, [xn])
    w_in_t = win_g.reshape(d_in, d)
    proj = _mm_nt("proj_in", xn, w_in_t, F32)

    cos, sin = _rope_tables(positions.reshape(t, 1).astype(F32))
    sinks_row = jnp.pad(sinks, ((0, 0), (0, LANES - n_heads)))
    attn = _attention_fwd(proj, cos, sin, sinks_row, d_attn)

    def view(n, a):
        if n in ("b_re", "b_im"):
            return jnp.transpose(a[0], (0, 2, 1)).reshape(-1, SSM_STATE)
        if n in ("c_re", "c_im"):
            return a[0].reshape(-1, SSM_STATE)
        return a[0].T if n == "d_skip" else a[0] if a.ndim == 3 else a

    def unview(n, val):
        if n in ("b_re", "b_im"):
            return jnp.transpose(val.reshape(n_groups, SSM_GROUP, SSM_STATE), (0, 2, 1))[None]
        if n in ("c_re", "c_im"):
            return val.reshape(1, n_groups, SSM_GROUP, SSM_STATE)
        return val.T[None] if n == "d_skip" else val[None] if weights[n].ndim == 3 else val

    b_re_v, b_im_v = view("b_re", b_re), view("b_im", b_im)
    ldt_col = log_dt.reshape(n_groups, 1)
    lam_re, lam_im, bbar_re, bbar_im = _s5_discretise(a_re[0], a_im[0], ldt_col, b_re_v, b_im_v)
    n_blocks = n_groups // GROUPS_PER_BLOCK
    mats = [_block_diag_in(bbar_re).astype(BF16), _block_diag_in(bbar_im).astype(BF16),
            lam_re.reshape(n_blocks, 1, SSM_ST_BLOCK), lam_im.reshape(n_blocks, 1, SSM_ST_BLOCK),
            _block_diag_out(view("c_re", c_re)).astype(BF16), _block_diag_out(view("c_im", c_im)).astype(BF16)]
    dskip_row = d_skip.reshape(1, d_ssm)
    forward_mix, _ = forward_gather(ag_mix, [attn])
    y_ssm = _s5_fwd(proj, mats, dskip_row, d_attn, d_ssm)
    gelu_bf16 = lambda yv: _gelu(yv).astype(BF16)
    wglu_g, wo_g = _split_wait(forward_mix, [y_ssm])
    w_glu_full = wglu_g.reshape(d_ssm, d_ssm)
    w_o_full = wo_g.reshape(d, d)
    glu_lin = _mm_nn("glu_gate", y_ssm, w_glu_full, F32, a_fn=gelu_bf16)

    def mix_prep(av, yv, gl, bg, ga, gs):
        ssm = _gelu(yv) * _sigmoid(gl + bg)
        return [jnp.concatenate([_rms(av)[0] * ga, _rms(ssm)[0] * gs], axis=1)], []

    mixed, = _rows("mix_prep", mix_prep, [attn, y_ssm, glu_lin], [b_glu, g_attn_out, g_ssm_out], [(d, BF16)], [], tm)
    mix = _mm_nn("mix_out", mixed, w_o_full, F32)

    def post_mix(xv, mv, gpm, gpf):
        h = xv + _rms(mv)[0] * gpm
        return [h, _rms(h)[0] * gpf], []

    forward_ffn_in, token = forward_gather(ag_ffn_in, [mix])
    h, hn = _rows("post_mix", post_mix, [x2, mix], [g_post_mix, g_pre_ffn], [(d, F32), (d, BF16)], [], tm,
                  after=[token])
    wgate_g, wup_g = _split_wait(forward_ffn_in, [hn])
    d_ff = N_DEV * wgate_g.shape[1]
    wgate_t, wup_t = wgate_g.reshape(d_ff, d), wup_g.reshape(d_ff, d)
    gate, up, hid = _ffn_in(hn, wgate_t, wup_t)
    wdown_g, = finish_gather(ag_down, [hid])
    wdown_full = wdown_g.reshape(d_ff, d)
    ff = _mm_nn("ffn_down", hid, wdown_full, F32, tm=1024, tn=512)

    def head(hv, fv, tv, gpo):
        out = hv + _rms(fv)[0] * gpo
        err = out - tv
        dout = err * (1.0 / d)
        dff, dg = _rms_bwd(fv, gpo, dout)
        loss = jnp.zeros((1, LANES), F32) + 0.5 * jnp.sum(err * err) * (1.0 / d)
        return [dff, dout], [dg, loss]

    dff, dh_out, dg_post_ffn, loss_row = _rows("loss_head", head, [h, ff, target], [g_post_ffn],
                                               [(d, BF16), (d, F32)], [d, LANES], tm)

    def swap_halves(name, grads):
        return _halves_start("swap_" + name, [g.reshape(N_DEV // 2, 2, *g.shape[1:]) for g in grads])

    def scatter_chip_sums(name, swap, after):
        both = _split_wait(swap, after)
        half = len(both) // 2
        sums = [_chip_sum("chip_sum_%s_%d" % (name, i), both[i], both[half + i]) for i in range(half)]
        return _exchange_start("scatter_" + name, sums, True, (OWN,) + CHIP_PEERS, by_chip=True)

    f_tile = _hidden_tile(d_ff)
    by_owner = lambda g: g.reshape(N_DEV, d_ff // N_DEV, d)
    dw_down = by_owner(_mm_tn("ffn_down_dw", hid, dff, BF16, tm=f_tile))
    swap_down, token = swap_halves("dw_down", [dw_down])
    dgate, dup = _ffn_down_bwd(dff, wdown_full, gate, up, [token])
    rs_down, token = scatter_chip_sums("dw_down", swap_down, [dgate])
    dhn_gate = _mm_nn("ffn_in_dx_gate", dgate, wgate_t, F32, tm=1024, tn=512, after=[token])
    dhn = _mm_nn("ffn_in_dx_up", dup, wup_t, F32, tm=1024, tn=512, plus=dhn_gate)
    dw_gate = by_owner(_mm_tn("ffn_gate_dw", dgate, hn, BF16, tm=f_tile))
    dw_up = by_owner(_mm_tn("ffn_up_dw", dup, hn, BF16, tm=f_tile))
    swap_ffn_in, tok_ffn_in = swap_halves("dw_gate_up", [dw_gate, dw_up])

    def mid_bwd(dho, dhn_, hv, mv, gpf, gpm):
        d1, dgpf = _rms_bwd(hv, gpf, dhn_)
        dh_ = dho + d1
        dmix_, dgpm = _rms_bwd(mv, gpm, dh_)
        return [dh_, dmix_], [dgpf, dgpm]

    dh, dmix, dg_pre_ffn, dg_post_mix = _rows("mid_bwd", mid_bwd, [dh_out, dhn, h, mix], [g_pre_ffn, g_post_mix],
                                              [(d, F32), (d, BF16)], [d, d], tm, after=[tok_ffn_in])

    dmixed = _mm_nt("mix_out_dx", dmix, w_o_full, F32)
    rs_ffn_in, token = scatter_chip_sums("dw_gate_up", swap_ffn_in, [dmixed])
    dw_o = _mm_tn("mix_out_dw", mixed, dmix, BF16, after=[token])
    swap_o, tok_o = swap_halves("dw_o", [dw_o.reshape(N_DEV, d // N_DEV, d)])

    def mix_bwd(dm, av, yv, gl, bg, ga, gs):
        dattn_, dga = _rms_bwd(av, ga, dm[:, :d_attn])
        z = _gelu(yv)
        sg = _sigmoid(gl + bg)
        dssm, dgs = _rms_bwd(z * sg, gs, dm[:, d_attn:])
        dgl = dssm * z * sg * (1.0 - sg)
        return [dattn_, dssm * sg, dgl], [dga, dgs, jnp.sum(dgl, axis=0, keepdims=True)]

    dattn, dz_direct, dglu, dg_attn_out, dg_ssm_out, db_glu = _rows(
        "mix_bwd", mix_bwd, [dmixed, attn, y_ssm, glu_lin], [b_glu, g_attn_out, g_ssm_out],
        [(d_attn, F32), (d_ssm, F32), (d_ssm, BF16)], [d_attn, d_ssm, d_ssm], tm, after=[tok_o])
    dz_glu = _mm_nt("glu_gate_dx", dglu, w_glu_full, F32)
    dw_glu = _mm_tn("glu_gate_dw", y_ssm, dglu, BF16, a_fn=gelu_bf16)
    rs_o, token = scatter_chip_sums("dw_o", swap_o, [dz_glu, dw_glu])

    du, db_re_dense, db_im_dense, dlam_re, dlam_im, dc_re_dense, dc_im_dense, dd_skip = _s5_bwd(
        proj, mats, dskip_row, y_ssm, dz_direct, dz_glu, d_attn, d_ssm, [token])
    da_re, da_im, dlog_dt, db_re_v, db_im_v = _s5_discretise_bwd(
        a_re[0], a_im[0], ldt_col, b_re_v, b_im_v, dlam_re.reshape(n_groups, SSM_STATE),
        dlam_im.reshape(n_groups, SSM_STATE), _block_diag_in_t(db_re_dense), _block_diag_in_t(db_im_dense))
    dq, dk2, dv2, dsinks_row = _attention_bwd(proj, cos, sin, sinks_row, dattn, d_attn)

    small_grads = {
        "sinks": dsinks_row, "a_re": da_re, "a_im": da_im, "log_dt": dlog_dt.reshape(1, n_groups),
        "b_re": db_re_v, "b_im": db_im_v, "c_re": _block_diag_out_t(dc_re_dense),
        "c_im": _block_diag_out_t(dc_im_dense), "d_skip": dd_skip.reshape(n_groups, SSM_GROUP).T, "b_glu": db_glu,
        "g_attn_out": dg_attn_out, "g_ssm_out": dg_ssm_out, "g_post_mix": dg_post_mix, "g_pre_ffn": dg_pre_ffn,
        "g_post_ffn": dg_post_ffn,
    }
    vec_pack, vec_slots, mat_pack, mat_rows = _pack_grads([small_grads[n] for n in _SMALL_VECTORS] + [loss_row],
                                                          [small_grads[n] for n in _SMALL_MATRICES])
    ag_small, token = _exchange_start("gather_small_grads", [vec_pack, mat_pack, small_grads["d_skip"]], False,
                                      (OWN,) + ALL_PEERS)
    dproj = _assemble_dproj(dq, dk2, dv2, du, d_in, [token])

    dw_in = _mm_tn("proj_in_dw", dproj, xn, BF16).reshape(N_DEV, d_in // N_DEV, d)
    swap_in, token = swap_halves("dw_in_glu", [dw_in, dw_glu.reshape(N_DEV, d_ssm // N_DEV, d_ssm)])
    dxn = _mm_nn("proj_in_dx", dproj, w_in_t, F32, after=[token])
    rs_in, token = scatter_chip_sums("dw_in_glu", swap_in, [dxn])

    def x_bwd(dh_, dxn_, xv, g):
        dx, dg = _rms_bwd(xv, g, dxn_)
        return [dh_ + dx], [dg]

    grad_x, dg_pre_mix = _rows("norm_in_bwd", x_bwd, [dh, dxn, x2], [g_pre_mix], [(d, F32)], [d], tm, after=[token])
    ag_last, token = _exchange_start("gather_g_pre_mix_grad", [dg_pre_mix], False, (OWN,) + ALL_PEERS)

    results = {}

    def adam_big(n, parts):
        r = parts.shape[1]
        tr = next((c for c in range(192, 15, -16) if r % c == 0), r)
        results[n] = _adam_sharded("adam_" + n, parts, by_rows(n, weights[n]), by_rows(n, mom_m[n]),
                                   by_rows(n, mom_v[n]), tr)
        return results[n][3]

    done = [grad_x, token]
    adam_big("w_down", _exchange_wait(rs_down, done)[0])
    p_gate, p_up = _exchange_wait(rs_ffn_in, done)
    done = [adam_big("w_gate", p_gate), adam_big("w_up", p_up), results["w_down"][3]]
    done = [adam_big("w_o", _exchange_wait(rs_o, done)[0])]
    vec_parts, mat_parts, dskip_parts = _exchange_wait(ag_small, done)
    for n, row0 in zip(_SMALL_MATRICES, mat_rows):
        rows = view(n, weights[n]).shape[0]
        results[n] = _adam_sharded("adam_" + n, mat_parts, view(n, weights[n]), view(n, mom_m[n]), view(n, mom_v[n]),
                                   rows, row0)
    p_in, p_glu = _exchange_wait(rs_in, [results[n][3] for n in _SMALL_MATRICES])
    done = [adam_big("w_in", p_in), adam_big("w_glu", p_glu)]
    first_gain_parts, = _exchange_wait(ag_last, done)
    rest = _SMALL_VECTORS + ("d_skip", "g_pre_mix")
    found_at = [(0, row, lane) for row, lane, _ in vec_slots[:-1]] + [(1, 0, 0), (2, 0, 0)]
    updated, loss_sum = _adam_replicated([vec_parts, dskip_parts, first_gain_parts], found_at,
                                         [view(n, weights[n]) for n in rest], [view(n, mom_m[n]) for n in rest],
                                         [view(n, mom_v[n]) for n in rest], (0,) + vec_slots[-1])
    results.update(zip(rest, updated))

    outs = [loss_sum[0, 0], grad_x[None]]
    for k in range(4):
        for n in _ORDER:
            val = results[n][k]
            outs.append(val.T[None] if n in _BY_COLUMNS else val[None] if n in _BIG else unview(n, val))
    return tuple(outs)
```

```python
import math

import jax
import jax.numpy as jnp
from jax import lax
from jax.experimental import pallas as pl
from jax.experimental.pallas import tpu as pltpu

F32 = jnp.float32
BF16 = jnp.bfloat16

HEAD_DIM = 64
N_KV_HEADS = 4
D_KV = N_KV_HEADS * HEAD_DIM
WINDOW = 128
BLOCK = 128
ROPE_THETA = 10000.0
SSM_GROUP = 16
SSM_STATE = 64
GROUPS_PER_BLOCK = 8
SSM_CH_BLOCK = GROUPS_PER_BLOCK * SSM_GROUP
SSM_ST_BLOCK = GROUPS_PER_BLOCK * SSM_STATE
RMS_EPS = 1e-6
N_DEV = 8
LANES = 128
SUBLANES = 8
MASKED = -1e30

ADAM_LR = 0.001
ADAM_B1 = 0.9
ADAM_B2 = 0.999
ADAM_EPS = 1e-08
ADAM_WD = 0.01
ADAM_STEP = 10

VMEM_LIMIT_BYTES = 56 * 1024 * 1024


def _call(body, *, name, out_shape, in_specs, out_specs, grid=(), scratch_shapes=(), semantics=None, n_after=0):
    params = dict(vmem_limit_bytes=VMEM_LIMIT_BYTES)
    if semantics is not None:
        params["dimension_semantics"] = semantics
    n_in = len(in_specs)
    if n_after:
        inner = body

        def body(*refs):
            inner(*refs[:n_in], *refs[n_in + n_after:])

        in_specs = list(in_specs) + [pl.BlockSpec(memory_space=pl.ANY)] * n_after
    return pl.pallas_call(body, name=name, grid=grid, in_specs=in_specs, out_specs=out_specs, out_shape=out_shape,
                          scratch_shapes=scratch_shapes, compiler_params=pltpu.CompilerParams(**params))


def _sds(shape, dtype):
    return jax.ShapeDtypeStruct(tuple(shape), dtype)


def _dot(a, b, ca, cb):
    return lax.dot_general(a, b, (((ca,), (cb,)), ((), ())), preferred_element_type=F32)


def _rms(x):
    r = lax.rsqrt(jnp.mean(x * x, axis=-1, keepdims=True) + RMS_EPS)
    return x * r, r


def _rms_bwd(x, g, dy):
    xh, r = _rms(x)
    dxh = dy * g
    dx = r * (dxh - xh * jnp.mean(dxh * xh, axis=-1, keepdims=True))
    return dx, jnp.sum(dy * xh, axis=0, keepdims=True)


def _sigmoid(x):
    return 1.0 / (1.0 + jnp.exp(-x))


_GELU_C = math.sqrt(2.0 / math.pi)
_GELU_A = 0.044715


def _gelu(y):
    t = jnp.tanh(_GELU_C * (y + _GELU_A * y * y * y))
    return 0.5 * y * (1.0 + t)


def _gelu_grad(y):
    t = jnp.tanh(_GELU_C * (y + _GELU_A * y * y * y))
    return 0.5 * (1.0 + t) + 0.5 * y * (1.0 - t * t) * _GELU_C * (1.0 + 3.0 * _GELU_A * y * y)


ROW_BUFFERS = 3


def _rows_piped(name, fn, row_ins, vec_ins, row_outs, acc_widths, tm, after):
    rows = row_ins[0].shape[0]
    n_steps = rows // tm
    n_row, n_vec, n_out, n_acc = len(row_ins), len(vec_ins), len(row_outs), len(acc_widths)

    def body(*refs):
        in_hbm, vecs = refs[:n_row], refs[n_row:n_row + n_vec]
        out_hbm = refs[n_row + n_vec:n_row + n_vec + n_out]
        accs = refs[n_row + n_vec + n_out:n_row + n_vec + n_out + n_acc]
        scratch = refs[n_row + n_vec + n_out + n_acc:]
        in_buf, out_buf = scratch[:n_row], scratch[n_row:n_row + n_out]
        in_sem, out_sem = scratch[n_row + n_out], scratch[n_row + n_out + 1]

        def tile_rows(step):
            return pl.ds(pl.multiple_of(step * tm, tm), tm)

        def fetch(a, step, slot):
            return pltpu.make_async_copy(in_hbm[a].at[tile_rows(step), :], in_buf[a].at[slot], in_sem.at[a, slot])

        def write(o, step, slot):
            return pltpu.make_async_copy(out_buf[o].at[slot], out_hbm[o].at[tile_rows(step), :], out_sem.at[o, slot])

        for s in range(ROW_BUFFERS):
            for a in range(n_row):
                fetch(a, s, s).start()
        for acc in accs:
            acc[...] = jnp.zeros_like(acc)
        vec_vals = [v[...] for v in vecs]

        def step(i, _):
            slot, out_slot = i % ROW_BUFFERS, i % 2
            for a in range(n_row):
                fetch(a, i, slot).wait()
            row_vals, acc_vals = fn(*[in_buf[a][slot] for a in range(n_row)], *vec_vals)

            @pl.when(i >= 2)
            def _():
                for o in range(n_out):
                    write(o, i - 2, out_slot).wait()

            for o, val in enumerate(row_vals):
                out_buf[o][out_slot] = val.astype(out_buf[o].dtype)
                write(o, i, out_slot).start()
            for acc, val in zip(accs, acc_vals):
                acc[...] += val

            @pl.when(i + ROW_BUFFERS < n_steps)
            def _():
                for a in range(n_row):
                    fetch(a, i + ROW_BUFFERS, slot).start()

            return 0

        lax.fori_loop(0, n_steps, step, 0)
        for s in range(n_steps - 2, n_steps):
            for o in range(n_out):
                write(o, s, s % 2).wait()

    any_spec = pl.BlockSpec(memory_space=pl.ANY)
    in_specs = [any_spec] * n_row + [pl.BlockSpec(v.shape, lambda: (0, 0)) for v in vec_ins]
    out_specs = [any_spec] * n_out + [pl.BlockSpec((1, w), lambda: (0, 0)) for w in acc_widths]
    out_shape = [_sds((rows, w), dt) for w, dt in row_outs] + [_sds((1, w), F32) for w in acc_widths]
    scratch = [pltpu.VMEM((ROW_BUFFERS, tm, a.shape[1]), a.dtype) for a in row_ins]
    scratch += [pltpu.VMEM((2, tm, w), dt) for w, dt in row_outs]
    scratch += [pltpu.SemaphoreType.DMA((n_row, ROW_BUFFERS)), pltpu.SemaphoreType.DMA((max(n_out, 1), 2))]
    return _call(body, name=name, in_specs=in_specs, out_specs=out_specs, out_shape=out_shape, scratch_shapes=scratch,
                 n_after=len(after))(*row_ins, *vec_ins, *after)


def _rows(name, fn, row_ins, vec_ins, row_outs, acc_widths, tm, after=()):
    rows = row_ins[0].shape[0]
    assert rows % tm == 0, (name, rows, tm)
    n_row, n_vec, n_out, n_acc = len(row_ins), len(vec_ins), len(row_outs), len(acc_widths)
    if rows // tm >= ROW_BUFFERS:
        return _rows_piped(name, fn, row_ins, vec_ins, row_outs, acc_widths, tm, after)

    def body(*refs):
        ins = [r[...] for r in refs[:n_row + n_vec]]
        outs = refs[n_row + n_vec:n_row + n_vec + n_out]
        accs = refs[n_row + n_vec + n_out:]
        row_vals, acc_vals = fn(*ins)
        for o, v in zip(outs, row_vals):
            o[...] = v.astype(o.dtype)
        if n_acc:
            @pl.when(pl.program_id(0) == 0)
            def _():
                for a in accs:
                    a[...] = jnp.zeros_like(a)
            for a, v in zip(accs, acc_vals):
                a[...] += v

    in_specs = [pl.BlockSpec((tm, a.shape[1]), lambda i: (i, 0)) for a in row_ins]
    in_specs += [pl.BlockSpec(v.shape, lambda i: (0, 0)) for v in vec_ins]
    out_specs = [pl.BlockSpec((tm, w), lambda i: (i, 0)) for w, _ in row_outs]
    out_specs += [pl.BlockSpec((1, w), lambda i: (0, 0)) for w in acc_widths]
    out_shape = [_sds((rows, w), dt) for w, dt in row_outs] + [_sds((1, w), F32) for w in acc_widths]
    return _call(body, name=name, grid=(rows // tm,), in_specs=in_specs, out_specs=out_specs, out_shape=out_shape,
                 semantics=("arbitrary",) if n_acc else ("parallel",), n_after=len(after))(*row_ins, *vec_ins, *after)


def _matmul(name, operands, in_specs, product, grid, out_shape, out_spec, acc_shape, after=()):
    nk = grid[-1]
    n_in = len(operands)
    in_place = out_shape.dtype == F32

    def body(*refs):
        ins = [r[...] for r in refs[:n_in]]
        o_ref = refs[n_in]
        if nk == 1:
            o_ref[...] = product(*ins).astype(o_ref.dtype)
            return
        acc = o_ref if in_place else refs[n_in + 1]
        k = pl.program_id(len(grid) - 1)

        @pl.when(k == 0)
        def _():
            acc[...] = jnp.zeros_like(acc)

        acc[...] += product(*ins)

        if not in_place:
            @pl.when(k == nk - 1)
            def _():
                o_ref[...] = acc[...].astype(o_ref.dtype)

    return _call(body, name=name, grid=grid, in_specs=in_specs, out_specs=out_spec, out_shape=out_shape,
                 scratch_shapes=[] if nk == 1 or in_place else [pltpu.VMEM(acc_shape, F32)],
                 semantics=("parallel",) * (len(grid) - 1) + ("arbitrary",), n_after=len(after))(*operands, *after)


def _mm_nn(name, a, b, out_dtype, tm=512, tn=None, a_fn=lambda x: x, after=(), plus=None):
    m, k = a.shape
    n = b.shape[1]
    tm, tn = min(tm, m), n if tn is None else min(tn, n)
    operands = [a, b] + ([] if plus is None else [plus])
    specs = [pl.BlockSpec((tm, k), lambda i, j, s: (i, 0)), pl.BlockSpec((k, tn), lambda i, j, s: (0, j))]
    specs += [] if plus is None else [pl.BlockSpec((tm, tn), lambda i, j, s: (i, j))]
    return _matmul(name, operands, specs, lambda x, y, *p: _dot(a_fn(x), y, 1, 0) + (p[0] if p else 0.0),
                   (m // tm, n // tn, 1), _sds((m, n), out_dtype),
                   pl.BlockSpec((tm, tn), lambda i, j, s: (i, j)), (tm, tn), after)


def _mm_nt(name, a, b, out_dtype, tm=512, tn=None):
    m, k = a.shape
    n = b.shape[0]
    tm, tn = min(tm, m), n if tn is None else tn
    return _matmul(name, [a, b],
                   [pl.BlockSpec((tm, k), lambda i, j, s: (i, 0)), pl.BlockSpec((tn, k), lambda i, j, s: (j, 0))],
                   lambda x, y: _dot(x, y, 1, 1), (m // tm, n // tn, 1), _sds((m, n), out_dtype),
                   pl.BlockSpec((tm, tn), lambda i, j, s: (i, j)), (tm, tn))


def _mm_tn(name, a, b, out_dtype, tm=512, tn=None, tk=2048, a_fn=lambda x: x, after=()):
    k, m = a.shape
    n = b.shape[1]
    tm, tk, tn = min(tm, m), min(tk, k), n if tn is None else tn
    return _matmul(name, [a, b],
                   [pl.BlockSpec((tk, tm), lambda i, j, s: (s, i)), pl.BlockSpec((tk, tn), lambda i, j, s: (s, j))],
                   lambda x, y: _dot(a_fn(x), y, 0, 0), (m // tm, n // tn, k // tk), _sds((m, n), out_dtype),
                   pl.BlockSpec((tm, tn), lambda i, j, s: (i, j)), (tm, tn), after)


def _hidden_tile(f):
    return 512 if f % 512 == 0 else 256


def _ffn_in(a, w_gate, w_up, tm=1024):
    m, k = a.shape
    f = w_gate.shape[0]
    tm, tn = min(tm, m), _hidden_tile(f)

    def body(a_ref, wg_ref, wu_ref, g_ref, u_ref, h_ref):
        x = a_ref[...]
        g = _dot(x, wg_ref[...], 1, 1)
        u = _dot(x, wu_ref[...], 1, 1)
        g_ref[...] = g.astype(BF16)
        u_ref[...] = u.astype(BF16)
        h_ref[...] = (g * _sigmoid(g) * u).astype(BF16)

    w_spec = pl.BlockSpec((tn, k), lambda j, i: (j, 0))
    o_spec = pl.BlockSpec((tm, tn), lambda j, i: (i, j))
    return _call(body, name="ffn_in", grid=(f // tn, m // tm),
                 in_specs=[pl.BlockSpec((tm, k), lambda j, i: (i, 0)), w_spec, w_spec], out_specs=[o_spec] * 3,
                 out_shape=[_sds((m, f), BF16)] * 3, semantics=("parallel", "parallel"))(a, w_gate, w_up)


def _ffn_down_bwd(d_out, w_down, gate, up, after, tm=1024):
    m, k = d_out.shape
    f = w_down.shape[0]
    tm, tn = min(tm, m), _hidden_tile(f)

    def body(d_ref, w_ref, g_ref, u_ref, dg_ref, du_ref):
        rows = pl.ds(pl.multiple_of(pl.program_id(1) * tm, tm), tm)
        dh = _dot(d_ref[rows, :], w_ref[...], 1, 1)
        g = g_ref[...].astype(F32)
        sg = _sigmoid(g)
        dg_ref[...] = (dh * u_ref[...].astype(F32) * sg * (1.0 + g * (1.0 - sg))).astype(BF16)
        du_ref[...] = (dh * g * sg).astype(BF16)

    t_spec = pl.BlockSpec((tm, tn), lambda j, i: (i, j))
    return _call(body, name="ffn_down_dx", grid=(f // tn, m // tm),
                 in_specs=[pl.BlockSpec((m, k), lambda j, i: (0, 0)), pl.BlockSpec((tn, k), lambda j, i: (j, 0)),
                           t_spec, t_spec],
                 out_specs=[t_spec] * 2, out_shape=[_sds((m, f), BF16)] * 2, semantics=("parallel", "parallel"),
                 n_after=len(after))(d_out, w_down, gate, up, *after)


ALL_PEERS = (1, 2, 3, 4, 5, 6, 7)
CHIP_PEERS = (2, 4, 6)
SIBLING = 1
OWN = 0


def _peer(relation):
    x, y, c = lax.axis_index("x"), lax.axis_index("y"), lax.axis_index("c")
    pos = (1 - x if relation & 4 else x, 1 - y if relation & 2 else y, 1 - c if relation & 1 else c)
    return pos, 4 * pos[0] + 2 * pos[1] + pos[2]


def _slot(relation, by_chip):
    pos, device = _peer(relation)
    return 2 * pos[0] + pos[1] if by_chip else device


def _exchange_copies(ins, lands, send_sems, recv_sems, scatter, relations, by_chip=False):
    me = _slot(0, by_chip)

    def copy(a, s, peer, pos, dst_slot):
        return pltpu.make_async_remote_copy(
            src_ref=ins[a].at[peer] if scatter else ins[a], dst_ref=lands[a].at[dst_slot],
            send_sem=send_sems.at[s], recv_sem=recv_sems.at[s], device_id=pos, device_id_type=pl.DeviceIdType.MESH)

    pairs = []
    for k, r in enumerate(relations):
        pos, peer = _peer(r)[0], _slot(r, by_chip)
        for a in range(len(ins)):
            s = a * len(relations) + k
            pairs.append((copy(a, s, peer, pos, me), copy(a, s, peer, pos, peer)))
    return pairs


def _halves_copies(arrays, lands, send_sems, recv_sems):
    sibling, _ = _peer(SIBLING)
    core = lax.axis_index("c")
    pairs = []
    for a, (ref, land) in enumerate(zip(arrays, lands)):
        send = pltpu.make_async_remote_copy(
            src_ref=ref.at[:, pl.ds(1 - core, 1)], dst_ref=land, send_sem=send_sems.at[a], recv_sem=recv_sems.at[a],
            device_id=sibling, device_id_type=pl.DeviceIdType.MESH)
        pairs.append((send, send))
    return pairs


def _forward_copies(lands, send_sems, recv_sems):
    sibling, _ = _peer(SIBLING)

    def copy(a, s, slot):
        return pltpu.make_async_remote_copy(
            src_ref=lands[a].at[slot], dst_ref=lands[a].at[slot], send_sem=send_sems.at[s], recv_sem=recv_sems.at[s],
            device_id=sibling, device_id_type=pl.DeviceIdType.MESH)

    pairs = []
    for k, r in enumerate(CHIP_PEERS):
        _, mine = _peer(r)
        _, theirs = _peer(r | SIBLING)
        for a in range(len(lands)):
            s = a * len(CHIP_PEERS) + k
            pairs.append((copy(a, s, mine), copy(a, s, theirs)))
    return pairs


_HBM_SPEC = pl.BlockSpec(memory_space=pltpu.HBM)
_SEM_SPEC = pl.BlockSpec(memory_space=pltpu.SEMAPHORE)
_SIDE_EFFECT = pltpu.SideEffectType.DATAFLOW_SIDE_EFFECTING


def _split_start(name, operands, n_sem, make_pairs):
    k = len(operands)

    def body(*refs):
        send_sems, recv_sems, token = refs[k], refs[k + 1], refs[-1]
        for send, _ in make_pairs(refs[:k], send_sems, recv_sems):
            send.start()
        token[...] = jnp.zeros_like(token)

    out = pl.pallas_call(
        body, name=name,
        out_shape=(pltpu.SemaphoreType.DMA((n_sem,)), pltpu.SemaphoreType.DMA((n_sem,)),
                   *[pltpu.HBM(a.shape, a.dtype) for a in operands], _sds((SUBLANES, LANES), F32)),
        in_specs=[_HBM_SPEC] * k,
        out_specs=(_SEM_SPEC, _SEM_SPEC, *[_HBM_SPEC] * k, pl.BlockSpec(memory_space=pltpu.VMEM)),
        input_output_aliases={i: 2 + i for i in range(k)},
        compiler_params=pltpu.CompilerParams(has_side_effects=_SIDE_EFFECT),
    )(*[pltpu.with_memory_space_constraint(a, pltpu.HBM) for a in operands])
    return dict(name=name, sems=out[:2], thru=list(out[2:2 + k]), make_pairs=make_pairs), out[-1]


def _split_wait(handle, after):
    thru, make_pairs = handle["thru"], handle["make_pairs"]
    k = len(thru)

    def body(*refs):
        for send, arrival in make_pairs(refs[:k], refs[k], refs[k + 1]):
            send.wait_send()
            arrival.wait_recv()

    return pl.pallas_call(
        body, name=handle["name"] + "_wait", out_shape=[pltpu.HBM(a.shape, a.dtype) for a in thru],
        in_specs=[_HBM_SPEC] * k + [_SEM_SPEC, _SEM_SPEC] + [pl.BlockSpec(memory_space=pl.ANY)] * len(after),
        out_specs=[_HBM_SPEC] * k, input_output_aliases={i: i for i in range(k)},
        compiler_params=pltpu.CompilerParams(has_side_effects=_SIDE_EFFECT),
    )(*thru, *handle["sems"], *after)


def _exchange_start(name, arrays, scatter, relations, by_chip=False):
    n = len(arrays)
    lands = [lax.empty(a.shape if scatter else (N_DEV,) + a.shape, a.dtype) for a in arrays]

    def make_pairs(refs, send_sems, recv_sems):
        return _exchange_copies(refs[:n], refs[n:], send_sems, recv_sems, scatter, relations, by_chip)

    handle, token = _split_start(name, list(arrays) + lands, n * len(relations), make_pairs)
    handle.update(n=n)
    return handle, token


def _halves_start(name, arrays):
    lands = [lax.empty((a.shape[0], 1) + a.shape[2:], a.dtype) for a in arrays]
    n = len(arrays)

    def make_pairs(refs, send_sems, recv_sems):
        return _halves_copies(refs[:n], refs[n:], send_sems, recv_sems)

    return _split_start(name, list(arrays) + lands, n, make_pairs)


def _chip_sum(name, array, landed):
    chips, _, r, c = array.shape
    tr = r

    def body(a_ref, b_ref, o_ref):
        mine = a_ref[lax.axis_index("c")].astype(F32)
        o_ref[...] = (mine + b_ref[...].astype(F32)).astype(o_ref.dtype)

    return _call(body, name=name, grid=(chips, r // tr),
                 in_specs=[pl.BlockSpec((None, 2, tr, c), lambda k, i: (k, 0, i, 0)),
                           pl.BlockSpec((None, None, tr, c), lambda k, i: (k, 0, i, 0))],
                 out_specs=pl.BlockSpec((None, tr, c), lambda k, i: (k, i, 0)),
                 out_shape=_sds((chips, r, c), BF16), semantics=("parallel", "parallel"))(array, landed)


def _forward_start(name, lands):
    return _split_start(name, list(lands), len(lands) * len(CHIP_PEERS), _forward_copies)


def _exchange_wait(handle, after):
    return _split_wait(handle, after)[handle["n"]:]


def _rope_tables(pos_col):
    t = pos_col.shape[0]
    half = HEAD_DIM // 2
    inv_freq = ROPE_THETA ** (-jnp.arange(half, dtype=F32) / half)
    inv_row = jnp.tile(inv_freq, LANES // half)[None, :]

    def body(pos_ref, inv_ref, cos_ref, sin_ref):
        ang = pos_ref[...] * inv_ref[...]
        cos_ref[...] = jnp.cos(ang)
        sin_ref[...] = jnp.sin(ang)

    tm = min(t, 512)
    return _call(body, name="rope_tables", grid=(t // tm,),
                 in_specs=[pl.BlockSpec((tm, 1), lambda i: (i, 0)), pl.BlockSpec((1, LANES), lambda i: (0, 0))],
                 out_specs=[pl.BlockSpec((tm, LANES), lambda i: (i, 0))] * 2,
                 out_shape=[_sds((t, LANES), F32)] * 2, semantics=("parallel",))(pos_col, inv_row)


def _rot_half(x):
    lane = lax.broadcasted_iota(jnp.int32, x.shape, 1)
    low = (lane % HEAD_DIM) < HEAD_DIM // 2
    return jnp.where(low, -pltpu.roll(x, LANES - HEAD_DIM // 2, 1), pltpu.roll(x, HEAD_DIM // 2, 1))


def _rope(x, cos, sin):
    return x * cos + _rot_half(x) * sin


def _unrope(d, cos, sin):
    return d * cos - _rot_half(d) * sin


def _band_mask(first_block, heads):
    r = lax.broadcasted_iota(jnp.int32, (heads * BLOCK, 2 * BLOCK), 0) % BLOCK
    c = lax.broadcasted_iota(jnp.int32, (heads * BLOCK, 2 * BLOCK), 1)
    diff = r - c + BLOCK
    return (diff >= 0) & (diff < WINDOW) & ((c >= BLOCK) | jnp.logical_not(first_block))


def _attn_specs(t, d_attn, d_in):
    kb, vb = d_attn // D_KV, d_attn // D_KV + 1
    prev = lambda i: jnp.maximum(i - 1, 0)
    return [
        pl.BlockSpec((BLOCK, d_attn), lambda i: (i, 0)),
        pl.BlockSpec((BLOCK, D_KV), lambda i: (i, kb)),
        pl.BlockSpec((BLOCK, D_KV), lambda i: (i, vb)),
        pl.BlockSpec((BLOCK, D_KV), lambda i: (prev(i), kb)),
        pl.BlockSpec((BLOCK, D_KV), lambda i: (prev(i), vb)),
        pl.BlockSpec((BLOCK, LANES), lambda i: (i, 0)),
        pl.BlockSpec((BLOCK, LANES), lambda i: (i, 0)),
        pl.BlockSpec((BLOCK, LANES), lambda i: (prev(i), 0)),
        pl.BlockSpec((BLOCK, LANES), lambda i: (prev(i), 0)),
        pl.BlockSpec((1, LANES), lambda i: (0, 0)),
    ]


def _head(x, h):
    return x[:, h * HEAD_DIM:(h + 1) * HEAD_DIM]


def _attn_heads(q_ref, kc_ref, vc_ref, kp_ref, vp_ref, cq_ref, sq_ref, cp_ref, sp_ref, d_attn):
    cq, sq, cp, sp = cq_ref[...], sq_ref[...], cp_ref[...], sp_ref[...]
    q_rot = [_rope(q_ref[:, j * LANES:(j + 1) * LANES], cq, sq) for j in range(d_attn // LANES)]
    kc_rot = [_rope(kc_ref[:, j * LANES:(j + 1) * LANES], cq, sq) for j in range(D_KV // LANES)]
    kp_rot = [_rope(kp_ref[:, j * LANES:(j + 1) * LANES], cp, sp) for j in range(D_KV // LANES)]
    per = LANES // HEAD_DIM
    q_heads = [_head(q_rot[h // per], h % per).astype(BF16) for h in range(d_attn // HEAD_DIM)]
    kk = [jnp.concatenate([_head(kp_rot[g // per], g % per), _head(kc_rot[g // per], g % per)], axis=0).astype(BF16)
          for g in range(N_KV_HEADS)]
    vv = [jnp.concatenate([_head(vp_ref[...], g), _head(vc_ref[...], g)], axis=0).astype(BF16) for g in range(N_KV_HEADS)]
    return q_heads, kk, vv


def _stack_group(q_heads, sink_ref, group):
    q_all = jnp.concatenate([q_heads[h] for h in group], axis=0)
    sink_all = jnp.concatenate([jnp.broadcast_to(sink_ref[:, h:h + 1], (BLOCK, 1)) for h in group], axis=0)
    return q_all, sink_all


def _softmax_with_sink(q, kk, sink, mask):
    s = _dot(q, kk, 1, 1) * (1.0 / math.sqrt(HEAD_DIM))
    s = jnp.where(mask, s, MASKED)
    m = jnp.maximum(jnp.max(s, axis=-1, keepdims=True), sink)
    p = jnp.exp(s - m)
    e_sink = jnp.exp(sink - m)
    inv = 1.0 / (jnp.sum(p, axis=-1, keepdims=True) + e_sink)
    return p * inv, e_sink * inv


def _attention_fwd(proj, cos, sin, sinks_row, d_attn):
    t, d_in = proj.shape
    n_heads = d_attn // HEAD_DIM
    q_per_kv = n_heads // N_KV_HEADS

    def body(q_ref, kc_ref, vc_ref, kp_ref, vp_ref, cq_ref, sq_ref, cp_ref, sp_ref, sink_ref, o_ref):
        mask = _band_mask(pl.program_id(0) == 0, q_per_kv)
        q_heads, kk, vv = _attn_heads(q_ref, kc_ref, vc_ref, kp_ref, vp_ref, cq_ref, sq_ref, cp_ref, sp_ref, d_attn)
        for g in range(N_KV_HEADS):
            group = range(g * q_per_kv, (g + 1) * q_per_kv)
            q_all, sink_all = _stack_group(q_heads, sink_ref, group)
            probs, _ = _softmax_with_sink(q_all, kk[g], sink_all, mask)
            o_all = _dot(probs.astype(BF16), vv[g], 1, 0)
            for k, h in enumerate(group):
                o_ref[:, h * HEAD_DIM:(h + 1) * HEAD_DIM] = o_all[k * BLOCK:(k + 1) * BLOCK]

    return _call(body, name="attention_fwd", grid=(t // BLOCK,), in_specs=_attn_specs(t, d_attn, d_in),
                 out_specs=pl.BlockSpec((BLOCK, d_attn), lambda i: (i, 0)), out_shape=_sds((t, d_attn), F32),
                 semantics=("parallel",))(proj, proj, proj, proj, proj, cos, sin, cos, sin, sinks_row)


def _attention_bwd(proj, cos, sin, sinks_row, d_out, d_attn):
    t, d_in = proj.shape
    n_heads = d_attn // HEAD_DIM
    q_per_kv = n_heads // N_KV_HEADS
    nb = t // BLOCK
    per = LANES // HEAD_DIM
    stack = q_per_kv

    def body(q_ref, kc_ref, vc_ref, kp_ref, vp_ref, cq_ref, sq_ref, cp_ref, sp_ref, sink_ref, do_ref,
             dq_ref, dk_ref, dv_ref, dsink_ref):
        i = pl.program_id(0)
        mask = _band_mask(i == 0, stack)
        q_heads, kk, vv = _attn_heads(q_ref, kc_ref, vc_ref, kp_ref, vp_ref, cq_ref, sq_ref, cp_ref, sp_ref, d_attn)
        lane = lax.broadcasted_iota(jnp.int32, (1, LANES), 1)
        dsink = jnp.zeros((1, LANES), F32)
        dq_rot, dkk, dvv = [], [], []
        for g in range(N_KV_HEADS):
            dkk_g = jnp.zeros((2 * BLOCK, HEAD_DIM), F32)
            dvv_g = jnp.zeros((2 * BLOCK, HEAD_DIM), F32)
            for first in range(g * q_per_kv, (g + 1) * q_per_kv, stack):
                group = range(first, first + stack)
                q_all, sink_all = _stack_group(q_heads, sink_ref, group)
                probs, p_sink = _softmax_with_sink(q_all, kk[g], sink_all, mask)
                do_all = jnp.concatenate([do_ref[:, h * HEAD_DIM:(h + 1) * HEAD_DIM] for h in group],
                                         axis=0).astype(BF16)
                dp = _dot(do_all, vv[g], 1, 1)
                delta = jnp.sum(probs * dp, axis=-1, keepdims=True)
                ds = (probs * (dp - delta) * (1.0 / math.sqrt(HEAD_DIM))).astype(BF16)
                dq_all = _dot(ds, kk[g], 1, 0)
                dkk_g += _dot(ds, q_all, 0, 0)
                dvv_g += _dot(probs.astype(BF16), do_all, 0, 0)
                sink_term = p_sink * delta
                for k, h in enumerate(group):
                    dq_rot.append(dq_all[k * BLOCK:(k + 1) * BLOCK])
                    part = jnp.sum(sink_term[k * BLOCK:(k + 1) * BLOCK], axis=0, keepdims=True)
                    dsink += jnp.where(lane == h, -part, 0.0)
            dkk.append(dkk_g)
            dvv.append(dvv_g)
        cq, sq, cp, sp = cq_ref[...], sq_ref[...], cp_ref[...], sp_ref[...]
        for j in range(d_attn // LANES):
            d = jnp.concatenate(dq_rot[j * per:(j + 1) * per], axis=1)
            dq_ref[:, j * LANES:(j + 1) * LANES] = _unrope(d, cq, sq)
        for j in range(D_KV // LANES):
            d = jnp.concatenate(dkk[j * per:(j + 1) * per], axis=1)
            dk_ref[0, :, j * LANES:(j + 1) * LANES] = _unrope(d[:BLOCK], cp, sp)
            dk_ref[1, :, j * LANES:(j + 1) * LANES] = _unrope(d[BLOCK:], cq, sq)
            d = jnp.concatenate(dvv[j * per:(j + 1) * per], axis=1)
            dv_ref[0, :, j * LANES:(j + 1) * LANES] = d[:BLOCK]
            dv_ref[1, :, j * LANES:(j + 1) * LANES] = d[BLOCK:]

        @pl.when(i == 0)
        def _():
            dsink_ref[...] = jnp.zeros_like(dsink_ref)

        dsink_ref[...] += dsink

    pair = pl.BlockSpec((2, BLOCK, D_KV), lambda i: (i, 0, 0))
    return _call(body, name="attention_bwd", grid=(nb,),
                 in_specs=_attn_specs(t, d_attn, d_in) + [pl.BlockSpec((BLOCK, d_attn), lambda i: (i, 0))],
                 out_specs=[pl.BlockSpec((BLOCK, d_attn), lambda i: (i, 0)), pair, pair,
                            pl.BlockSpec((1, LANES), lambda i: (0, 0))],
                 out_shape=[_sds((t, d_attn), F32), _sds((2 * nb, BLOCK, D_KV), F32), _sds((2 * nb, BLOCK, D_KV), F32),
                            _sds((1, LANES), F32)],
                 semantics=("arbitrary",))(proj, proj, proj, proj, proj, cos, sin, cos, sin, sinks_row, d_out)


def _assemble_dproj(dq, dk2, dv2, du, d_in, after):
    t, d_attn = dq.shape
    d_ssm = du.shape[1]
    nb = t // BLOCK

    def body(dq_ref, dk_own, dk_next, dv_own, dv_next, du_ref, o_ref):
        has_next = (pl.program_id(0) < nb - 1).astype(F32)
        o_ref[:, :d_attn] = dq_ref[...].astype(BF16)
        o_ref[:, d_attn:d_attn + D_KV] = (dk_own[...] + has_next * dk_next[...]).astype(BF16)
        o_ref[:, d_attn + D_KV:d_attn + 2 * D_KV] = (dv_own[...] + has_next * dv_next[...]).astype(BF16)
        o_ref[:, d_attn + 2 * D_KV:] = du_ref[...].astype(BF16)

    own = pl.BlockSpec((None, BLOCK, D_KV), lambda i: (2 * i + 1, 0, 0))
    nxt = pl.BlockSpec((None, BLOCK, D_KV), lambda i: (jnp.minimum(2 * i + 2, 2 * nb - 1), 0, 0))
    return _call(body, name="assemble_dproj", grid=(nb,),
                 in_specs=[pl.BlockSpec((BLOCK, d_attn), lambda i: (i, 0)), own, nxt, own, nxt,
                           pl.BlockSpec((BLOCK, d_ssm), lambda i: (i, 0))],
                 out_specs=pl.BlockSpec((BLOCK, d_in), lambda i: (i, 0)), out_shape=_sds((t, d_in), BF16),
                 semantics=("parallel",), n_after=len(after))(dq, dk2, dk2, dv2, dv2, du, *after)


def _discretise(ar, ai, ldt, br, bi):
    dt = jnp.exp(ldt)
    mag = jnp.exp(ar * dt)
    lam_re = mag * jnp.cos(ai * dt)
    lam_im = mag * jnp.sin(ai * dt)
    den = ar * ar + ai * ai
    nr = lam_re - 1.0
    ni = lam_im
    f_re = (nr * ar + ni * ai) / den
    f_im = (ni * ar - nr * ai) / den
    return (lam_re, lam_im, [f_re * r - f_im * i for r, i in zip(br, bi)], [f_re * i + f_im * r for r, i in zip(br, bi)])


def _whole(arrays):
    return [pl.BlockSpec(a.shape, lambda *_, nd=len(a.shape): (0,) * nd) for a in arrays]


def _channels(ref):
    groups = ref.shape[0] // SSM_GROUP
    return [ref[pl.ds(p, groups, stride=SSM_GROUP), :] for p in range(SSM_GROUP)]


def _store_channels(ref, values):
    groups = ref.shape[0] // SSM_GROUP
    for p, val in enumerate(values):
        ref[pl.ds(p, groups, stride=SSM_GROUP), :] = val


def _s5_discretise(ar, ai, ldt, br, bi):
    ins = [ar, ai, ldt, br, bi]

    def body(ar_ref, ai_ref, ldt_ref, br_ref, bi_ref, lr_ref, li_ref, bbr_ref, bbi_ref):
        lr, li, bbr, bbi = _discretise(ar_ref[...], ai_ref[...], ldt_ref[...], _channels(br_ref), _channels(bi_ref))
        lr_ref[...] = lr
        li_ref[...] = li
        _store_channels(bbr_ref, bbr)
        _store_channels(bbi_ref, bbi)

    outs = [_sds(ar.shape, F32), _sds(ar.shape, F32), _sds(br.shape, F32), _sds(br.shape, F32)]
    return _call(body, name="s5_discretise", in_specs=_whole(ins), out_specs=_whole(outs), out_shape=outs)(*ins)


def _s5_discretise_bwd(ar, ai, ldt, br, bi, d_lr, d_li, d_bbr, d_bbi):
    ins = [ar, ai, ldt, br, bi, d_lr, d_li, d_bbr, d_bbi]

    def body(ar_ref, ai_ref, ldt_ref, br_ref, bi_ref, dlr_ref, dli_ref, dbbr_ref, dbbi_ref,
             dar_ref, dai_ref, dldt_ref, dbr_ref, dbi_ref):
        _, vjp = jax.vjp(_discretise, ar_ref[...], ai_ref[...], ldt_ref[...], _channels(br_ref), _channels(bi_ref))
        dar, dai, dldt, dbr, dbi = vjp((dlr_ref[...], dli_ref[...], _channels(dbbr_ref), _channels(dbbi_ref)))
        dar_ref[...] = dar
        dai_ref[...] = dai
        dldt_ref[...] = dldt
        _store_channels(dbr_ref, dbr)
        _store_channels(dbi_ref, dbi)

    outs = [_sds(a.shape, F32) for a in (ar, ai, ldt, br, bi)]
    return _call(body, name="s5_discretise_bwd", in_specs=_whole(ins), out_specs=_whole(outs), out_shape=outs)(*ins)


def _cmul(ar, ai, br, bi):
    return ar * br - ai * bi, ar * bi + ai * br


def _load_segmented(ref, tile0, n_tiles, seg):
    return jnp.concatenate([ref[pl.ds(tile0 + j, SUBLANES, stride=seg), :] for j in range(n_tiles)], axis=0)


def _store_segmented(ref, tile0, seg, value):
    for j in range(value.shape[0] // SUBLANES):
        ref[pl.ds(tile0 + j, SUBLANES, stride=seg), :] = value[j * SUBLANES:(j + 1) * SUBLANES, :]


def _fill_powers(lr, li, pr_ref, pi_ref, seg):
    pows = [(lr, li)]
    for _ in range(SUBLANES - 1):
        pows.append(_cmul(pows[-1][0], pows[-1][1], lr, li))
    row = lax.broadcasted_iota(jnp.int32, (SUBLANES, lr.shape[1]), 0)
    tr = jnp.zeros((SUBLANES, lr.shape[1]), F32)
    ti = jnp.zeros((SUBLANES, lr.shape[1]), F32)
    for r in range(SUBLANES):
        tr = jnp.where(row == r, pows[r][0], tr)
        ti = jnp.where(row == r, pows[r][1], ti)
    pr_ref[0:SUBLANES, :] = tr
    pi_ref[0:SUBLANES, :] = ti
    k = SUBLANES
    while k < seg:
        fr, fi = pr_ref[k - 1:k, :], pi_ref[k - 1:k, :]
        for t0 in range(0, k, SUBLANES):
            nr, ni = _cmul(pr_ref[t0:t0 + SUBLANES, :], pi_ref[t0:t0 + SUBLANES, :], fr, fi)
            pr_ref[k + t0:k + t0 + SUBLANES, :] = nr
            pi_ref[k + t0:k + t0 + SUBLANES, :] = ni
        k *= 2


def _scan_segments(sr_ref, si_ref, pr_ref, pi_ref, lr, li, seg, reverse, per_tile=None):
    w = lr.shape[1]
    sign = -1.0 if reverse else 1.0
    lrb = jnp.broadcast_to(lr, (SUBLANES, w))
    lib = jnp.broadcast_to(sign * li, (SUBLANES, w))
    zero = jnp.zeros((SUBLANES, w), F32)

    def tile_rows(j):
        return pl.ds(pl.multiple_of(j * SUBLANES, SUBLANES), SUBLANES)

    steps = 4 if seg % 4 == 0 else 1

    def local(i, carry):
        for u in range(steps):
            j = i * steps + u
            rows = tile_rows(seg - 1 - j if reverse else j)
            pr, pi = _cmul(lrb, lib, carry[0], carry[1])
            carry = (sr_ref[rows, :] + pr, si_ref[rows, :] + pi)
            sr_ref[rows, :] = carry[0]
            si_ref[rows, :] = carry[1]
        return carry

    end_r, end_i = lax.fori_loop(0, seg // steps, local, (zero, zero))
    full_r, full_i = pr_ref[seg - 1:seg, :], sign * pi_ref[seg - 1:seg, :]
    row = lax.broadcasted_iota(jnp.int32, (SUBLANES, w), 0)
    in_r, in_i = zero, zero
    cur_r, cur_i = jnp.zeros((1, w), F32), jnp.zeros((1, w), F32)
    for r in (range(SUBLANES - 2, -1, -1) if reverse else range(1, SUBLANES)):
        src = r + 1 if reverse else r - 1
        pr, pi = _cmul(full_r, full_i, cur_r, cur_i)
        cur_r, cur_i = end_r[src:src + 1, :] + pr, end_i[src:src + 1, :] + pi
        in_r = jnp.where(row == r, cur_r, in_r)
        in_i = jnp.where(row == r, cur_i, in_i)

    def carry_in(j, _):
        rows = tile_rows(j)
        k = seg - 1 - j if reverse else j
        pr, pi = _cmul(pr_ref[pl.ds(k, 1), :], sign * pi_ref[pl.ds(k, 1), :], in_r, in_i)
        xr, xi = sr_ref[rows, :] + pr, si_ref[rows, :] + pi
        sr_ref[rows, :] = xr
        si_ref[rows, :] = xi
        if per_tile is not None:
            per_tile(j, xr, xi)
        return 0

    lax.fori_loop(0, seg, carry_in, 0, unroll=4)


_S5_ROWS = 2048


def _s5_in_specs(t, d_attn):
    u_block = (d_attn + 2 * D_KV) // SSM_CH_BLOCK
    blk3 = lambda shape: pl.BlockSpec((None,) + shape, lambda j: (j, 0, 0))
    return [
        pl.BlockSpec((t, SSM_CH_BLOCK), lambda j: (0, u_block + j)),
        blk3((SSM_CH_BLOCK, SSM_ST_BLOCK)), blk3((SSM_CH_BLOCK, SSM_ST_BLOCK)),
        blk3((1, SSM_ST_BLOCK)), blk3((1, SSM_ST_BLOCK)),
        blk3((SSM_ST_BLOCK, SSM_CH_BLOCK)), blk3((SSM_ST_BLOCK, SSM_CH_BLOCK)),
        pl.BlockSpec((1, SSM_CH_BLOCK), lambda j: (0, j)),
    ]


def _chunks(t):
    rows = min(_S5_ROWS, t)
    return rows, lambda i: pl.ds(pl.multiple_of(i * rows, rows), rows)


def _s5_states(u_ref, us_ref, bre_ref, bim_ref, lr_ref, li_ref, sr_ref, si_ref, pr_ref, pi_ref, t):
    seg = t // SUBLANES
    rows, chunk = _chunks(t)
    for c in range(t // rows):
        us_ref[c * rows:(c + 1) * rows, :] = _load_segmented(u_ref, c * rows // SUBLANES, rows // SUBLANES, seg)

    def fill(i, _):
        ub = us_ref[chunk(i), :].astype(BF16)
        sr_ref[chunk(i), :] = _dot(ub, bre_ref[...], 1, 0)
        si_ref[chunk(i), :] = _dot(ub, bim_ref[...], 1, 0)
        return 0

    lax.fori_loop(0, t // rows, fill, 0)
    _fill_powers(lr_ref[...], li_ref[...], pr_ref, pi_ref, seg)
    _scan_segments(sr_ref, si_ref, pr_ref, pi_ref, lr_ref[...], li_ref[...], seg, False)


def _s5_scratch(t):
    state = pltpu.VMEM((t, SSM_ST_BLOCK), F32)
    powers = pltpu.VMEM((t // SUBLANES, SSM_ST_BLOCK), F32)
    return state, powers, pltpu.VMEM((t, SSM_CH_BLOCK), F32)


def _s5_fwd(proj, mats, dskip_row, d_attn, d_ssm):
    t = proj.shape[0]
    seg = t // SUBLANES
    n_blocks = d_ssm // SSM_CH_BLOCK
    rows, chunk = _chunks(t)

    def body(u_ref, bre_ref, bim_ref, lr_ref, li_ref, cre_ref, cim_ref, d_ref, y_ref,
             sr_ref, si_ref, pr_ref, pi_ref, us_ref, ys_ref):
        _s5_states(u_ref, us_ref, bre_ref, bim_ref, lr_ref, li_ref, sr_ref, si_ref, pr_ref, pi_ref, t)

        def emit(i, _):
            ys_ref[chunk(i), :] = (_dot(sr_ref[chunk(i), :].astype(BF16), cre_ref[...], 1, 0)
                                   - _dot(si_ref[chunk(i), :].astype(BF16), cim_ref[...], 1, 0)
                                   + d_ref[...] * us_ref[chunk(i), :])
            return 0

        lax.fori_loop(0, t // rows, emit, 0)
        for c in range(t // rows):
            _store_segmented(y_ref, c * rows // SUBLANES, seg, ys_ref[c * rows:(c + 1) * rows, :])

    state, powers, channels = _s5_scratch(t)
    col = pl.BlockSpec((t, SSM_CH_BLOCK), lambda j: (0, j))
    return _call(body, name="s5_fwd", grid=(n_blocks,), in_specs=_s5_in_specs(t, d_attn), out_specs=col,
                 out_shape=_sds((t, d_ssm), F32), scratch_shapes=[state, state, powers, powers, channels, channels],
                 semantics=("parallel",))(proj, *mats, dskip_row)


def _s5_bwd(proj, mats, dskip_row, y, dz_a, dz_b, d_attn, d_ssm, after):
    t = proj.shape[0]
    seg = t // SUBLANES
    n_blocks = d_ssm // SSM_CH_BLOCK
    rows, chunk = _chunks(t)

    def body(u_ref, bre_ref, bim_ref, lr_ref, li_ref, cre_ref, cim_ref, d_ref, y_ref, dza_ref, dzb_ref,
             du_ref, dbre_ref, dbim_ref, dlr_ref, dli_ref, dcre_ref, dcim_ref, dd_ref,
             sr_ref, si_ref, gr_ref, gi_ref, pr_ref, pi_ref, us_ref, dys_ref, dus_ref, acc_r, acc_i):
        _s5_states(u_ref, us_ref, bre_ref, bim_ref, lr_ref, li_ref, sr_ref, si_ref, pr_ref, pi_ref, t)
        for ref in (dcre_ref, dcim_ref, dbre_ref, dbim_ref, dd_ref, acc_r, acc_i):
            ref[...] = jnp.zeros_like(ref)
        for c in range(t // rows):
            tile0, n_tiles = c * rows // SUBLANES, rows // SUBLANES
            dz = _load_segmented(dza_ref, tile0, n_tiles, seg) + _load_segmented(dzb_ref, tile0, n_tiles, seg)
            dys_ref[c * rows:(c + 1) * rows, :] = dz * _gelu_grad(_load_segmented(y_ref, tile0, n_tiles, seg))

        def through_c(i, _):
            dy = dys_ref[chunk(i), :]
            dd_ref[...] += jnp.sum(dy * us_ref[chunk(i), :], axis=0, keepdims=True)
            dyb = dy.astype(BF16)
            gr_ref[chunk(i), :] = _dot(dyb, cre_ref[...], 1, 1)
            gi_ref[chunk(i), :] = -_dot(dyb, cim_ref[...], 1, 1)
            dcre_ref[...] += _dot(sr_ref[chunk(i), :].astype(BF16), dyb, 0, 0)
            dcim_ref[...] -= _dot(si_ref[chunk(i), :].astype(BF16), dyb, 0, 0)
            return 0

        lax.fori_loop(0, t // rows, through_c, 0)

        row = lax.broadcasted_iota(jnp.int32, (SUBLANES, SSM_ST_BLOCK), 0)
        last = pl.ds((seg - 1) * SUBLANES, SUBLANES)
        wrap = [jnp.where(row == 0, 0.0, pltpu.roll(ref[last, :], 1, 0)) for ref in (sr_ref, si_ref)]

        def lambda_grad(j, g_re, g_im):
            before = pl.ds(pl.multiple_of(jnp.maximum(j - 1, 0) * SUBLANES, SUBLANES), SUBLANES)
            prev_r = jnp.where(j > 0, sr_ref[before, :], wrap[0])
            prev_i = jnp.where(j > 0, si_ref[before, :], wrap[1])
            acc_r[...] += g_re * prev_r + g_im * prev_i
            acc_i[...] += g_im * prev_r - g_re * prev_i

        _scan_segments(gr_ref, gi_ref, pr_ref, pi_ref, lr_ref[...], li_ref[...], seg, True, per_tile=lambda_grad)
        dlr_ref[...] = jnp.sum(acc_r[...], axis=0, keepdims=True)
        dli_ref[...] = jnp.sum(acc_i[...], axis=0, keepdims=True)

        def through_b(i, _):
            ub = us_ref[chunk(i), :].astype(BF16)
            grb, gib = gr_ref[chunk(i), :].astype(BF16), gi_ref[chunk(i), :].astype(BF16)
            dbre_ref[...] += _dot(ub, grb, 0, 0)
            dbim_ref[...] += _dot(ub, gib, 0, 0)
            dus_ref[chunk(i), :] = (_dot(grb, bre_ref[...], 1, 1) + _dot(gib, bim_ref[...], 1, 1)
                                    + d_ref[...] * dys_ref[chunk(i), :])
            return 0

        lax.fori_loop(0, t // rows, through_b, 0)
        for c in range(t // rows):
            _store_segmented(du_ref, c * rows // SUBLANES, seg, dus_ref[c * rows:(c + 1) * rows, :])

    col = pl.BlockSpec((t, SSM_CH_BLOCK), lambda j: (0, j))
    blk3 = lambda shape: pl.BlockSpec((None,) + shape, lambda j: (j, 0, 0))
    state, powers, channels = _s5_scratch(t)
    return _call(
        body, name="s5_bwd", grid=(n_blocks,), in_specs=_s5_in_specs(t, d_attn) + [col, col, col],
        out_specs=[col, blk3((SSM_CH_BLOCK, SSM_ST_BLOCK)), blk3((SSM_CH_BLOCK, SSM_ST_BLOCK)),
                   blk3((1, SSM_ST_BLOCK)), blk3((1, SSM_ST_BLOCK)),
                   blk3((SSM_ST_BLOCK, SSM_CH_BLOCK)), blk3((SSM_ST_BLOCK, SSM_CH_BLOCK)),
                   pl.BlockSpec((1, SSM_CH_BLOCK), lambda j: (0, j))],
        out_shape=[_sds((t, d_ssm), F32),
                   _sds((n_blocks, SSM_CH_BLOCK, SSM_ST_BLOCK), F32), _sds((n_blocks, SSM_CH_BLOCK, SSM_ST_BLOCK), F32),
                   _sds((n_blocks, 1, SSM_ST_BLOCK), F32), _sds((n_blocks, 1, SSM_ST_BLOCK), F32),
                   _sds((n_blocks, SSM_ST_BLOCK, SSM_CH_BLOCK), F32), _sds((n_blocks, SSM_ST_BLOCK, SSM_CH_BLOCK), F32),
                   _sds((1, d_ssm), F32)],
        scratch_shapes=[state, state, state, state, powers, powers, channels, channels, channels,
                        pltpu.VMEM((SUBLANES, SSM_ST_BLOCK), F32), pltpu.VMEM((SUBLANES, SSM_ST_BLOCK), F32)],
        semantics=("parallel",), n_after=len(after))(proj, *mats, dskip_row, y, dz_a, dz_b, *after)


def _by_block(gp_n):
    return gp_n.reshape(-1, GROUPS_PER_BLOCK, SSM_GROUP, SSM_STATE)


def _block_diag_in(bbar):
    eye = jnp.eye(GROUPS_PER_BLOCK, dtype=F32)
    return jnp.einsum("jgpn,gh->jgphn", _by_block(bbar), eye).reshape(-1, SSM_CH_BLOCK, SSM_ST_BLOCK)


def _block_diag_in_t(dense):
    d5 = dense.reshape(-1, GROUPS_PER_BLOCK, SSM_GROUP, GROUPS_PER_BLOCK, SSM_STATE)
    eye = jnp.eye(GROUPS_PER_BLOCK, dtype=F32)
    return jnp.einsum("jgphn,gh->jgpn", d5, eye).reshape(-1, SSM_STATE)


def _block_diag_out(c):
    eye = jnp.eye(GROUPS_PER_BLOCK, dtype=F32)
    return jnp.einsum("jgpn,gh->jgnhp", _by_block(c), eye).reshape(-1, SSM_ST_BLOCK, SSM_CH_BLOCK)


def _block_diag_out_t(dense):
    d5 = dense.reshape(-1, GROUPS_PER_BLOCK, SSM_STATE, GROUPS_PER_BLOCK, SSM_GROUP)
    eye = jnp.eye(GROUPS_PER_BLOCK, dtype=F32)
    return jnp.einsum("jgnhp,gh->jgpn", d5, eye).reshape(-1, SSM_STATE)


def _adamw(w, g, m, v):
    m = ADAM_B1 * m + (1.0 - ADAM_B1) * g
    v = ADAM_B2 * v + (1.0 - ADAM_B2) * (g * g)
    m_hat = m / (1.0 - ADAM_B1 ** ADAM_STEP)
    v_hat = v / (1.0 - ADAM_B2 ** ADAM_STEP)
    delta = -ADAM_LR * (m_hat / (jnp.sqrt(v_hat) + ADAM_EPS) + ADAM_WD * w)
    return delta, m, v


def _adam_sharded(name, parts, w, m, v, tr, row0=0):
    r, c = w.shape
    assert r % tr == 0 and row0 % tr == 0, (name, r, tr, row0)

    def body(p_ref, w_ref, m_ref, v_ref, g_out, d_out, m_out, v_out):
        g = p_ref[0].astype(F32)
        for i in range(1, p_ref.shape[0]):
            g = g + p_ref[i].astype(F32)
        delta, m_new, v_new = _adamw(w_ref[...], g, m_ref[...], v_ref[...])
        g_out[...] = g
        d_out[...] = delta
        m_out[...] = m_new
        v_out[...] = v_new

    tile = pl.BlockSpec((tr, c), lambda i: (i, 0))
    return _call(body, name=name, grid=(r // tr,),
                 in_specs=[pl.BlockSpec((parts.shape[0], tr, c), lambda i: (0, i + row0 // tr, 0)), tile, tile, tile],
                 out_specs=[tile] * 4, out_shape=[_sds((r, c), F32)] * 4, semantics=("parallel",))(parts, w, m, v)


_BIG = ("w_in", "w_glu", "w_o", "w_gate", "w_up", "w_down")
_BY_COLUMNS = ("w_in", "w_gate", "w_up")
_SMALL_VECTORS = ("sinks", "log_dt", "b_glu", "g_attn_out", "g_ssm_out", "g_post_mix", "g_pre_ffn", "g_post_ffn")
_SMALL_MATRICES = ("b_re", "b_im", "c_re", "c_im", "a_re", "a_im")
_ORDER = ("g_pre_mix", "w_in", "sinks", "a_re", "a_im", "log_dt", "b_re", "b_im", "c_re", "c_im", "d_skip", "w_glu",
          "b_glu", "g_attn_out", "g_ssm_out", "w_o", "g_post_mix", "g_pre_ffn", "w_gate", "w_up", "w_down",
          "g_post_ffn")


def _pack_grads(vectors, matrices):
    width = max(a.shape[1] for a in vectors)
    slots, row, lane = [], 0, 0
    for a in vectors:
        span = -(-a.shape[1] // LANES) * LANES
        if lane + span > width:
            row, lane = row + 1, 0
        slots.append((row, lane, a.shape[1]))
        lane += span
    firsts, at = [], 0
    for a in matrices:
        firsts.append(at)
        at += a.shape[0]
    nv = len(vectors)

    def body(*refs):
        vec_out, mat_out = refs[-2], refs[-1]
        vec_out[...] = jnp.zeros_like(vec_out)
        for ref, (r, l, w) in zip(refs[:nv], slots):
            vec_out[r:r + 1, l:l + w] = ref[...]
        for ref, r0 in zip(refs[nv:-2], firsts):
            mat_out[r0:r0 + ref.shape[0], :] = ref[...]

    ins = list(vectors) + list(matrices)
    outs = [_sds((-(-(row + 1) // SUBLANES) * SUBLANES, width), F32), _sds((at, matrices[0].shape[1]), F32)]
    vec_pack, mat_pack = _call(body, name="pack_small_grads", in_specs=_whole(ins), out_specs=_whole(outs),
                               out_shape=outs)(*ins)
    return vec_pack, slots, mat_pack, firsts


def _adam_replicated(sources, found_at, w, m, v, total_at):
    ns, n = len(sources), len(w)

    def body(*refs):
        ins, outs = refs[ns:ns + 3 * n], refs[ns + 3 * n:]
        summed = []
        for p_ref in refs[:ns]:
            g = p_ref[0]
            for k in range(1, N_DEV):
                g = g + p_ref[k]
            summed.append(g)
        for i, (src, row, lane) in enumerate(found_at):
            w_ref, m_ref, v_ref = ins[i], ins[n + i], ins[2 * n + i]
            rows, cols = w_ref.shape
            g = summed[src][row:row + rows, lane:lane + cols]
            delta, m_new, v_new = _adamw(w_ref[...], g, m_ref[...], v_ref[...])
            for o, val in zip(outs[4 * i:4 * i + 4], (g, delta, m_new, v_new)):
                o[...] = val
        t_src, t_row, t_lane, t_width = total_at
        outs[-1][...] = summed[t_src][t_row:t_row + 1, t_lane:t_lane + t_width]

    ins = list(sources) + list(w) + list(m) + list(v)
    outs = [_sds(a.shape, F32) for a in w for _ in range(4)] + [_sds((1, total_at[3]), F32)]
    flat = _call(body, name="adam_replicated", in_specs=_whole(ins), out_specs=_whole(outs), out_shape=outs)(*ins)
    return [tuple(flat[4 * i:4 * i + 4]) for i in range(n)], flat[-1]


def kernel(x, positions, g_pre_mix, w_in, sinks, a_re, a_im, log_dt, b_re, b_im, c_re, c_im, d_skip, w_glu, b_glu, g_attn_out, g_ssm_out, w_o, g_post_mix, g_pre_ffn, w_gate, w_up, w_down, g_post_ffn, loss_target, m_g_pre_mix, m_w_in, m_sinks, m_a_re, m_a_im, m_log_dt, m_b_re, m_b_im, m_c_re, m_c_im, m_d_skip, m_w_glu, m_b_glu, m_g_attn_out, m_g_ssm_out, m_w_o, m_g_post_mix, m_g_pre_ffn, m_w_gate, m_w_up, m_w_down, m_g_post_ffn, v_g_pre_mix, v_w_in, v_sinks, v_a_re, v_a_im, v_log_dt, v_b_re, v_b_im, v_c_re, v_c_im, v_d_skip, v_w_glu, v_b_glu, v_g_attn_out, v_g_ssm_out, v_w_o, v_g_post_mix, v_g_pre_ffn, v_w_gate, v_w_up, v_w_down, v_g_post_ffn):
    given = dict(locals())
    weights = {n: given[n] for n in _ORDER}
    mom_m = {n: given["m_" + n] for n in _ORDER}
    mom_v = {n: given["v_" + n] for n in _ORDER}

    t, d = x.shape[1], x.shape[2]
    d_attn = d // 2
    d_ssm = d - d_attn
    d_in = d_attn + 2 * D_KV + d_ssm
    n_groups = d_ssm // SSM_GROUP
    n_heads = d_attn // HEAD_DIM
    tm = min(128, t)

    x2 = x[0]
    target = loss_target[0]

    def by_rows(n, a):
        return a[0].T if n in _BY_COLUMNS else a[0]

    def start_gather(name, ns, token):
        behind = 0 if token is None else token[0, 0].astype(BF16)
        shards = [by_rows(n, weights[n]).astype(BF16) + behind for n in ns]
        return _exchange_start(name, shards, False, (OWN, SIBLING) + CHIP_PEERS)

    def forward_gather(handle, after):
        return _forward_start(handle["name"] + "_forward", _exchange_wait(handle, after))

    def finish_gather(handle, after):
        return _split_wait(forward_gather(handle, after)[0], [])

    ag_in, token = start_gather("gather_w_in", ["w_in"], None)
    ag_mix, token = start_gather("gather_w_glu_o", ["w_glu", "w_o"], token)
    ag_ffn_in, token = start_gather("gather_w_gate_up", ["w_gate", "w_up"], token)
    ag_down, token = start_gather("gather_w_down", ["w_down"], token)

    xn, = _rows("norm_in", lambda xv, g: ([_rms(xv)[0] * g], []), [x2], [g_pre_mix], [(d, BF16)], [], tm,
                after=[token])
    win_g, = finish_gather(ag_in, [xn])
    w_in_t = win_g.reshape(d_in, d)
    proj = _mm_nt("proj_in", xn, w_in_t, F32)

    cos, sin = _rope_tables(positions.reshape(t, 1).astype(F32))
    sinks_row = jnp.pad(sinks, ((0, 0), (0, LANES - n_heads)))
    attn = _attention_fwd(proj, cos, sin, sinks_row, d_attn)

    def view(n, a):
        if n in ("b_re", "b_im"):
            return jnp.transpose(a[0], (0, 2, 1)).reshape(-1, SSM_STATE)
        if n in ("c_re", "c_im"):
            return a[0].reshape(-1, SSM_STATE)
        return a[0].T if n == "d_skip" else a[0] if a.ndim == 3 else a

    def unview(n, val):
        if n in ("b_re", "b_im"):
            return jnp.transpose(val.reshape(n_groups, SSM_GROUP, SSM_STATE), (0, 2, 1))[None]
        if n in ("c_re", "c_im"):
            return val.reshape(1, n_groups, SSM_GROUP, SSM_STATE)
        return val.T[None] if n == "d_skip" else val[None] if weights[n].ndim == 3 else val

    b_re_v, b_im_v = view("b_re", b_re), view("b_im", b_im)
    ldt_col = log_dt.reshape(n_groups, 1)
    lam_re, lam_im, bbar_re, bbar_im = _s5_discretise(a_re[0], a_im[0], ldt_col, b_re_v, b_im_v)
    n_blocks = n_groups // GROUPS_PER_BLOCK
    mats = [_block_diag_in(bbar_re).astype(BF16), _block_diag_in(bbar_im).astype(BF16),
            lam_re.reshape(n_blocks, 1, SSM_ST_BLOCK), lam_im.reshape(n_blocks, 1, SSM_ST_BLOCK),
            _block_diag_out(view("c_re", c_re)).astype(BF16), _block_diag_out(view("c_im", c_im)).astype(BF16)]
    dskip_row = d_skip.reshape(1, d_ssm)
    forward_mix, _ = forward_gather(ag_mix, [attn])
    y_ssm = _s5_fwd(proj, mats, dskip_row, d_attn, d_ssm)
    gelu_bf16 = lambda yv: _gelu(yv).astype(BF16)
    wglu_g, wo_g = _split_wait(forward_mix, [y_ssm])
    w_glu_full = wglu_g.reshape(d_ssm, d_ssm)
    w_o_full = wo_g.reshape(d, d)
    glu_lin = _mm_nn("glu_gate", y_ssm, w_glu_full, F32, a_fn=gelu_bf16)

    def mix_prep(av, yv, gl, bg, ga, gs):
        ssm = _gelu(yv) * _sigmoid(gl + bg)
        return [jnp.concatenate([_rms(av)[0] * ga, _rms(ssm)[0] * gs], axis=1)], []

    mixed, = _rows("mix_prep", mix_prep, [attn, y_ssm, glu_lin], [b_glu, g_attn_out, g_ssm_out], [(d, BF16)], [], tm)
    mix = _mm_nn("mix_out", mixed, w_o_full, F32)

    def post_mix(xv, mv, gpm, gpf):
        h = xv + _rms(mv)[0] * gpm
        return [h, _rms(h)[0] * gpf], []

    forward_ffn_in, token = forward_gather(ag_ffn_in, [mix])
    h, hn = _rows("post_mix", post_mix, [x2, mix], [g_post_mix, g_pre_ffn], [(d, F32), (d, BF16)], [], tm,
                  after=[token])
    wgate_g, wup_g = _split_wait(forward_ffn_in, [hn])
    d_ff = N_DEV * wgate_g.shape[1]
    wgate_t, wup_t = wgate_g.reshape(d_ff, d), wup_g.reshape(d_ff, d)
    gate, up, hid = _ffn_in(hn, wgate_t, wup_t)
    wdown_g, = finish_gather(ag_down, [hid])
    wdown_full = wdown_g.reshape(d_ff, d)
    ff = _mm_nn("ffn_down", hid, wdown_full, F32, tm=1024, tn=512)

    def head(hv, fv, tv, gpo):
        out = hv + _rms(fv)[0] * gpo
        err = out - tv
        dout = err * (1.0 / d)
        dff, dg = _rms_bwd(fv, gpo, dout)
        loss = jnp.zeros((1, LANES), F32) + 0.5 * jnp.sum(err * err) * (1.0 / d)
        return [dff, dout], [dg, loss]

    dff, dh_out, dg_post_ffn, loss_row = _rows("loss_head", head, [h, ff, target], [g_post_ffn],
                                               [(d, BF16), (d, F32)], [d, LANES], tm)

    def swap_halves(name, grads):
        return _halves_start("swap_" + name, [g.reshape(N_DEV // 2, 2, *g.shape[1:]) for g in grads])

    def scatter_chip_sums(name, swap, after):
        both = _split_wait(swap, after)
        half = len(both) // 2
        sums = [_chip_sum("chip_sum_%s_%d" % (name, i), both[i], both[half + i]) for i in range(half)]
        return _exchange_start("scatter_" + name, sums, True, (OWN,) + CHIP_PEERS, by_chip=True)

    f_tile = _hidden_tile(d_ff)
    by_owner = lambda g: g.reshape(N_DEV, d_ff // N_DEV, d)
    dw_down = by_owner(_mm_tn("ffn_down_dw", hid, dff, BF16, tm=f_tile))
    swap_down, token = swap_halves("dw_down", [dw_down])
    dgate, dup = _ffn_down_bwd(dff, wdown_full, gate, up, [token])
    rs_down, token = scatter_chip_sums("dw_down", swap_down, [dgate])
    dhn_gate = _mm_nn("ffn_in_dx_gate", dgate, wgate_t, F32, tm=1024, tn=512, after=[token])
    dhn = _mm_nn("ffn_in_dx_up", dup, wup_t, F32, tm=1024, tn=512, plus=dhn_gate)
    dw_gate = by_owner(_mm_tn("ffn_gate_dw", dgate, hn, BF16, tm=f_tile))
    dw_up = by_owner(_mm_tn("ffn_up_dw", dup, hn, BF16, tm=f_tile))
    swap_ffn_in, tok_ffn_in = swap_halves("dw_gate_up", [dw_gate, dw_up])

    def mid_bwd(dho, dhn_, hv, mv, gpf, gpm):
        d1, dgpf = _rms_bwd(hv, gpf, dhn_)
        dh_ = dho + d1
        dmix_, dgpm = _rms_bwd(mv, gpm, dh_)
        return [dh_, dmix_], [dgpf, dgpm]

    dh, dmix, dg_pre_ffn, dg_post_mix = _rows("mid_bwd", mid_bwd, [dh_out, dhn, h, mix], [g_pre_ffn, g_post_mix],
                                              [(d, F32), (d, BF16)], [d, d], tm, after=[tok_ffn_in])

    dmixed = _mm_nt("mix_out_dx", dmix, w_o_full, F32)
    rs_ffn_in, token = scatter_chip_sums("dw_gate_up", swap_ffn_in, [dmixed])
    dw_o = _mm_tn("mix_out_dw", mixed, dmix, BF16, after=[token])
    swap_o, tok_o = swap_halves("dw_o", [dw_o.reshape(N_DEV, d // N_DEV, d)])

    def mix_bwd(dm, av, yv, gl, bg, ga, gs):
        dattn_, dga = _rms_bwd(av, ga, dm[:, :d_attn])
        z = _gelu(yv)
        sg = _sigmoid(gl + bg)
        dssm, dgs = _rms_bwd(z * sg, gs, dm[:, d_attn:])
        dgl = dssm * z * sg * (1.0 - sg)
        return [dattn_, dssm * sg, dgl], [dga, dgs, jnp.sum(dgl, axis=0, keepdims=True)]

    dattn, dz_direct, dglu, dg_attn_out, dg_ssm_out, db_glu = _rows(
        "mix_bwd", mix_bwd, [dmixed, attn, y_ssm, glu_lin], [b_glu, g_attn_out, g_ssm_out],
        [(d_attn, F32), (d_ssm, F32), (d_ssm, BF16)], [d_attn, d_ssm, d_ssm], tm, after=[tok_o])
    dz_glu = _mm_nt("glu_gate_dx", dglu, w_glu_full, F32)
    dw_glu = _mm_tn("glu_gate_dw", y_ssm, dglu, BF16, a_fn=gelu_bf16)
    rs_o, token = scatter_chip_sums("dw_o", swap_o, [dz_glu, dw_glu])

    du, db_re_dense, db_im_dense, dlam_re, dlam_im, dc_re_dense, dc_im_dense, dd_skip = _s5_bwd(
        proj, mats, dskip_row, y_ssm, dz_direct, dz_glu, d_attn, d_ssm, [token])
    da_re, da_im, dlog_dt, db_re_v, db_im_v = _s5_discretise_bwd(
        a_re[0], a_im[0], ldt_col, b_re_v, b_im_v, dlam_re.reshape(n_groups, SSM_STATE),
        dlam_im.reshape(n_groups, SSM_STATE), _block_diag_in_t(db_re_dense), _block_diag_in_t(db_im_dense))
    dq, dk2, dv2, dsinks_row = _attention_bwd(proj, cos, sin, sinks_row, dattn, d_attn)

    small_grads = {
        "sinks": dsinks_row, "a_re": da_re, "a_im": da_im, "log_dt": dlog_dt.reshape(1, n_groups),
        "b_re": db_re_v, "b_im": db_im_v, "c_re": _block_diag_out_t(dc_re_dense),
        "c_im": _block_diag_out_t(dc_im_dense), "d_skip": dd_skip.reshape(n_groups, SSM_GROUP).T, "b_glu": db_glu,
        "g_attn_out": dg_attn_out, "g_ssm_out": dg_ssm_out, "g_post_mix": dg_post_mix, "g_pre_ffn": dg_pre_ffn,
        "g_post_ffn": dg_post_ffn,
    }
    vec_pack, vec_slots, mat_pack, mat_rows = _pack_grads([small_grads[n] for n in _SMALL_VECTORS] + [loss_row],
                                                          [small_grads[n] for n in _SMALL_MATRICES])
    ag_small, token = _exchange_start("gather_small_grads", [vec_pack, mat_pack, small_grads["d_skip"]], False,
                                      (OWN,) + ALL_PEERS)
    dproj = _assemble_dproj(dq, dk2, dv2, du, d_in, [token])

    dw_in = _mm_tn("proj_in_dw", dproj, xn, BF16).reshape(N_DEV, d_in // N_DEV, d)
    swap_in, token = swap_halves("dw_in_glu", [dw_in, dw_glu.reshape(N_DEV, d_ssm // N_DEV, d_ssm)])
    dxn = _mm_nn("proj_in_dx", dproj, w_in_t, F32, after=[token])
    rs_in, token = scatter_chip_sums("dw_in_glu", swap_in, [dxn])

    def x_bwd(dh_, dxn_, xv, g):
        dx, dg = _rms_bwd(xv, g, dxn_)
        return [dh_ + dx], [dg]

    grad_x, dg_pre_mix = _rows("norm_in_bwd", x_bwd, [dh, dxn, x2], [g_pre_mix], [(d, F32)], [d], tm, after=[token])
    ag_last, token = _exchange_start("gather_g_pre_mix_grad", [dg_pre_mix], False, (OWN,) + ALL_PEERS)

    results = {}

    def adam_big(n, parts):
        r = parts.shape[1]
        tr = next((c for c in range(192, 15, -16) if r % c == 0), r)
        results[n] = _adam_sharded("adam_" + n, parts, by_rows(n, weights[n]), by_rows(n, mom_m[n]),
                                   by_rows(n, mom_v[n]), tr)
        return results[n][3]

    done = [grad_x, token]
    adam_big("w_down", _exchange_wait(rs_down, done)[0])
    p_gate, p_up = _exchange_wait(rs_ffn_in, done)
    done = [adam_big("w_gate", p_gate), adam_big("w_up", p_up), results["w_down"][3]]
    done = [adam_big("w_o", _exchange_wait(rs_o, done)[0])]
    vec_parts, mat_parts, dskip_parts = _exchange_wait(ag_small, done)
    for n, row0 in zip(_SMALL_MATRICES, mat_rows):
        rows = view(n, weights[n]).shape[0]
        results[n] = _adam_sharded("adam_" + n, mat_parts, view(n, weights[n]), view(n, mom_m[n]), view(n, mom_v[n]),
                                   rows, row0)
    p_in, p_glu = _exchange_wait(rs_in, [results[n][3] for n in _SMALL_MATRICES])
    done = [adam_big("w_in", p_in), adam_big("w_glu", p_glu)]
    first_gain_parts, = _exchange_wait(ag_last, done)
    rest = _SMALL_VECTORS + ("d_skip", "g_pre_mix")
    found_at = [(0, row, lane) for row, lane, _ in vec_slots[:-1]] + [(1, 0, 0), (2, 0, 0)]
    updated, loss_sum = _adam_replicated([vec_parts, dskip_parts, first_gain_parts], found_at,
                                         [view(n, weights[n]) for n in rest], [view(n, mom_m[n]) for n in rest],
                                         [view(n, mom_v[n]) for n in rest], (0,) + vec_slots[-1])
    results.update(zip(rest, updated))

    outs = [loss_sum[0, 0], grad_x[None]]
    for k in range(4):
        for n in _ORDER:
            val = results[n][k]
            outs.append(val.T[None] if n in _BY_COLUMNS else val[None] if n in _BIG else unview(n, val))
    return tuple(outs)
```

```python
import math

import jax
import jax.numpy as jnp
from jax import lax
from jax.experimental import pallas as pl
from jax.experimental.pallas import tpu as pltpu

F32 = jnp.float32
BF16 = jnp.bfloat16

HEAD_DIM = 64
N_KV_HEADS = 4
D_KV = N_KV_HEADS * HEAD_DIM
WINDOW = 128
BLOCK = 128
ROPE_THETA = 10000.0
SSM_GROUP = 16
SSM_STATE = 64
GROUPS_PER_BLOCK = 8
SSM_CH_BLOCK = GROUPS_PER_BLOCK * SSM_GROUP
SSM_ST_BLOCK = GROUPS_PER_BLOCK * SSM_STATE
RMS_EPS = 1e-6
N_DEV = 8
LANES = 128
SUBLANES = 8
MASKED = -1e30

ADAM_LR = 0.001
ADAM_B1 = 0.9
ADAM_B2 = 0.999
ADAM_EPS = 1e-08
ADAM_WD = 0.01
ADAM_STEP = 10

VMEM_LIMIT_BYTES = 56 * 1024 * 1024


def _call(body, *, name, out_shape, in_specs, out_specs, grid=(), scratch_shapes=(), semantics=None, n_after=0):
    params = dict(vmem_limit_bytes=VMEM_LIMIT_BYTES)
    if semantics is not None:
        params["dimension_semantics"] = semantics
    n_in = len(in_specs)
    if n_after:
        inner = body

        def body(*refs):
            inner(*refs[:n_in], *refs[n_in + n_after:])

        in_specs = list(in_specs) + [pl.BlockSpec(memory_space=pl.ANY)] * n_after
    return pl.pallas_call(body, name=name, grid=grid, in_specs=in_specs, out_specs=out_specs, out_shape=out_shape,
                          scratch_shapes=scratch_shapes, compiler_params=pltpu.CompilerParams(**params))


def _sds(shape, dtype):
    return jax.ShapeDtypeStruct(tuple(shape), dtype)


def _dot(a, b, ca, cb):
    return lax.dot_general(a, b, (((ca,), (cb,)), ((), ())), preferred_element_type=F32)


def _rms(x):
    r = lax.rsqrt(jnp.mean(x * x, axis=-1, keepdims=True) + RMS_EPS)
    return x * r, r


def _rms_bwd(x, g, dy):
    xh, r = _rms(x)
    dxh = dy * g
    dx = r * (dxh - xh * jnp.mean(dxh * xh, axis=-1, keepdims=True))
    return dx, jnp.sum(dy * xh, axis=0, keepdims=True)


def _sigmoid(x):
    return 1.0 / (1.0 + jnp.exp(-x))


_GELU_C = math.sqrt(2.0 / math.pi)
_GELU_A = 0.044715


def _gelu(y):
    t = jnp.tanh(_GELU_C * (y + _GELU_A * y * y * y))
    return 0.5 * y * (1.0 + t)


def _gelu_grad(y):
    t = jnp.tanh(_GELU_C * (y + _GELU_A * y * y * y))
    return 0.5 * (1.0 + t) + 0.5 * y * (1.0 - t * t) * _GELU_C * (1.0 + 3.0 * _GELU_A * y * y)


ROW_BUFFERS = 4


def _rows_piped(name, fn, row_ins, vec_ins, row_outs, acc_widths, tm, after):
    rows = row_ins[0].shape[0]
    n_steps = rows // tm
    n_row, n_vec, n_out, n_acc = len(row_ins), len(vec_ins), len(row_outs), len(acc_widths)

    def body(*refs):
        in_hbm, vecs = refs[:n_row], refs[n_row:n_row + n_vec]
        out_hbm = refs[n_row + n_vec:n_row + n_vec + n_out]
        accs = refs[n_row + n_vec + n_out:n_row + n_vec + n_out + n_acc]
        scratch = refs[n_row + n_vec + n_out + n_acc:]
        in_buf, out_buf = scratch[:n_row], scratch[n_row:n_row + n_out]
        in_sem, out_sem = scratch[n_row + n_out], scratch[n_row + n_out + 1]

        def tile_rows(step):
            return pl.ds(pl.multiple_of(step * tm, tm), tm)

        def fetch(a, step, slot):
            return pltpu.make_async_copy(in_hbm[a].at[tile_rows(step), :], in_buf[a].at[slot], in_sem.at[a, slot])

        def write(o, step, slot):
            return pltpu.make_async_copy(out_buf[o].at[slot], out_hbm[o].at[tile_rows(step), :], out_sem.at[o, slot])

        for s in range(ROW_BUFFERS):
            for a in range(n_row):
                fetch(a, s, s).start()
        for acc in accs:
            acc[...] = jnp.zeros_like(acc)
        vec_vals = [v[...] for v in vecs]

        def step(i, _):
            slot, out_slot = i % ROW_BUFFERS, i % 2
            for a in range(n_row):
                fetch(a, i, slot).wait()
            row_vals, acc_vals = fn(*[in_buf[a][slot] for a in range(n_row)], *vec_vals)

            @pl.when(i >= 2)
            def _():
                for o in range(n_out):
                    write(o, i - 2, out_slot).wait()

            for o, val in enumerate(row_vals):
                out_buf[o][out_slot] = val.astype(out_buf[o].dtype)
                write(o, i, out_slot).start()
            for acc, val in zip(accs, acc_vals):
                acc[...] += val

            @pl.when(i + ROW_BUFFERS < n_steps)
            def _():
                for a in range(n_row):
                    fetch(a, i + ROW_BUFFERS, slot).start()

            return 0

        lax.fori_loop(0, n_steps, step, 0)
        for s in range(n_steps - 2, n_steps):
            for o in range(n_out):
                write(o, s, s % 2).wait()

    any_spec = pl.BlockSpec(memory_space=pl.ANY)
    in_specs = [any_spec] * n_row + [pl.BlockSpec(v.shape, lambda: (0, 0)) for v in vec_ins]
    out_specs = [any_spec] * n_out + [pl.BlockSpec((1, w), lambda: (0, 0)) for w in acc_widths]
    out_shape = [_sds((rows, w), dt) for w, dt in row_outs] + [_sds((1, w), F32) for w in acc_widths]
    scratch = [pltpu.VMEM((ROW_BUFFERS, tm, a.shape[1]), a.dtype) for a in row_ins]
    scratch += [pltpu.VMEM((2, tm, w), dt) for w, dt in row_outs]
    scratch += [pltpu.SemaphoreType.DMA((n_row, ROW_BUFFERS)), pltpu.SemaphoreType.DMA((max(n_out, 1), 2))]
    return _call(body, name=name, in_specs=in_specs, out_specs=out_specs, out_shape=out_shape, scratch_shapes=scratch,
                 n_after=len(after))(*row_ins, *vec_ins, *after)


def _rows(name, fn, row_ins, vec_ins, row_outs, acc_widths, tm, after=()):
    rows = row_ins[0].shape[0]
    assert rows % tm == 0, (name, rows, tm)
    n_row, n_vec, n_out, n_acc = len(row_ins), len(vec_ins), len(row_outs), len(acc_widths)
    if rows // tm >= ROW_BUFFERS:
        return _rows_piped(name, fn, row_ins, vec_ins, row_outs, acc_widths, tm, after)

    def body(*refs):
        ins = [r[...] for r in refs[:n_row + n_vec]]
        outs = refs[n_row + n_vec:n_row + n_vec + n_out]
        accs = refs[n_row + n_vec + n_out:]
        row_vals, acc_vals = fn(*ins)
        for o, v in zip(outs, row_vals):
            o[...] = v.astype(o.dtype)
        if n_acc:
            @pl.when(pl.program_id(0) == 0)
            def _():
                for a in accs:
                    a[...] = jnp.zeros_like(a)
            for a, v in zip(accs, acc_vals):
                a[...] += v

    in_specs = [pl.BlockSpec((tm, a.shape[1]), lambda i: (i, 0)) for a in row_ins]
    in_specs += [pl.BlockSpec(v.shape, lambda i: (0, 0)) for v in vec_ins]
    out_specs = [pl.BlockSpec((tm, w), lambda i: (i, 0)) for w, _ in row_outs]
    out_specs += [pl.BlockSpec((1, w), lambda i: (0, 0)) for w in acc_widths]
    out_shape = [_sds((rows, w), dt) for w, dt in row_outs] + [_sds((1, w), F32) for w in acc_widths]
    return _call(body, name=name, grid=(rows // tm,), in_specs=in_specs, out_specs=out_specs, out_shape=out_shape,
                 semantics=("arbitrary",) if n_acc else ("parallel",), n_after=len(after))(*row_ins, *vec_ins, *after)


def _matmul(name, operands, in_specs, product, grid, out_shape, out_spec, acc_shape, after=()):
    nk = grid[-1]
    n_in = len(operands)
    in_place = out_shape.dtype == F32

    def body(*refs):
        ins = [r[...] for r in refs[:n_in]]
        o_ref = refs[n_in]
        if nk == 1:
            o_ref[...] = product(*ins).astype(o_ref.dtype)
            return
        acc = o_ref if in_place else refs[n_in + 1]
        k = pl.program_id(len(grid) - 1)

        @pl.when(k == 0)
        def _():
            acc[...] = jnp.zeros_like(acc)

        acc[...] += product(*ins)

        if not in_place:
            @pl.when(k == nk - 1)
            def _():
                o_ref[...] = acc[...].astype(o_ref.dtype)

    return _call(body, name=name, grid=grid, in_specs=in_specs, out_specs=out_spec, out_shape=out_shape,
                 scratch_shapes=[] if nk == 1 or in_place else [pltpu.VMEM(acc_shape, F32)],
                 semantics=("parallel",) * (len(grid) - 1) + ("arbitrary",), n_after=len(after))(*operands, *after)


def _mm_nn(name, a, b, out_dtype, tm=512, tn=None, a_fn=lambda x: x, after=(), plus=None):
    m, k = a.shape
    n = b.shape[1]
    tm, tn = min(tm, m), n if tn is None else min(tn, n)
    operands = [a, b] + ([] if plus is None else [plus])
    specs = [pl.BlockSpec((tm, k), lambda i, j, s: (i, 0)), pl.BlockSpec((k, tn), lambda i, j, s: (0, j))]
    specs += [] if plus is None else [pl.BlockSpec((tm, tn), lambda i, j, s: (i, j))]
    return _matmul(name, operands, specs, lambda x, y, *p: _dot(a_fn(x), y, 1, 0) + (p[0] if p else 0.0),
                   (m // tm, n // tn, 1), _sds((m, n), out_dtype),
                   pl.BlockSpec((tm, tn), lambda i, j, s: (i, j)), (tm, tn), after)


def _mm_nt(name, a, b, out_dtype, tm=512, tn=None):
    m, k = a.shape
    n = b.shape[0]
    tm, tn = min(tm, m), n if tn is None else tn
    return _matmul(name, [a, b],
                   [pl.BlockSpec((tm, k), lambda i, j, s: (i, 0)), pl.BlockSpec((tn, k), lambda i, j, s: (j, 0))],
                   lambda x, y: _dot(x, y, 1, 1), (m // tm, n // tn, 1), _sds((m, n), out_dtype),
                   pl.BlockSpec((tm, tn), lambda i, j, s: (i, j)), (tm, tn))


def _mm_tn(name, a, b, out_dtype, tm=512, tn=None, tk=2048, a_fn=lambda x: x, after=()):
    k, m = a.shape
    n = b.shape[1]
    tm, tk, tn = min(tm, m), min(tk, k), n if tn is None else tn
    return _matmul(name, [a, b],
                   [pl.BlockSpec((tk, tm), lambda i, j, s: (s, i)), pl.BlockSpec((tk, tn), lambda i, j, s: (s, j))],
                   lambda x, y: _dot(a_fn(x), y, 0, 0), (m // tm, n // tn, k // tk), _sds((m, n), out_dtype),
                   pl.BlockSpec((tm, tn), lambda i, j, s: (i, j)), (tm, tn), after)


def _hidden_tile(f):
    return 512 if f % 512 == 0 else 256


def _ffn_in(a, w_gate, w_up, tm=1024):
    m, k = a.shape
    f = w_gate.shape[0]
    tm, tn = min(tm, m), _hidden_tile(f)

    def body(a_ref, wg_ref, wu_ref, g_ref, u_ref, h_ref):
        x = a_ref[...]
        g = _dot(x, wg_ref[...], 1, 1)
        u = _dot(x, wu_ref[...], 1, 1)
        g_ref[...] = g.astype(BF16)
        u_ref[...] = u.astype(BF16)
        h_ref[...] = (g * _sigmoid(g) * u).astype(BF16)

    w_spec = pl.BlockSpec((tn, k), lambda j, i: (j, 0))
    o_spec = pl.BlockSpec((tm, tn), lambda j, i: (i, j))
    return _call(body, name="ffn_in", grid=(f // tn, m // tm),
                 in_specs=[pl.BlockSpec((tm, k), lambda j, i: (i, 0)), w_spec, w_spec], out_specs=[o_spec] * 3,
                 out_shape=[_sds((m, f), BF16)] * 3, semantics=("parallel", "parallel"))(a, w_gate, w_up)


def _ffn_down_bwd(d_out, w_down, gate, up, after, tm=1024):
    m, k = d_out.shape
    f = w_down.shape[0]
    tm, tn = min(tm, m), _hidden_tile(f)

    def body(d_ref, w_ref, g_ref, u_ref, dg_ref, du_ref):
        rows = pl.ds(pl.multiple_of(pl.program_id(1) * tm, tm), tm)
        dh = _dot(d_ref[rows, :], w_ref[...], 1, 1)
        g = g_ref[...].astype(F32)
        sg = _sigmoid(g)
        dg_ref[...] = (dh * u_ref[...].astype(F32) * sg * (1.0 + g * (1.0 - sg))).astype(BF16)
        du_ref[...] = (dh * g * sg).astype(BF16)

    t_spec = pl.BlockSpec((tm, tn), lambda j, i: (i, j))
    return _call(body, name="ffn_down_dx", grid=(f // tn, m // tm),
                 in_specs=[pl.BlockSpec((m, k), lambda j, i: (0, 0)), pl.BlockSpec((tn, k), lambda j, i: (j, 0)),
                           t_spec, t_spec],
                 out_specs=[t_spec] * 2, out_shape=[_sds((m, f), BF16)] * 2, semantics=("parallel", "parallel"),
                 n_after=len(after))(d_out, w_down, gate, up, *after)


ALL_PEERS = (1, 2, 3, 4, 5, 6, 7)
CHIP_PEERS = (2, 4, 6)
SIBLING = 1
OWN = 0


def _peer(relation):
    x, y, c = lax.axis_index("x"), lax.axis_index("y"), lax.axis_index("c")
    pos = (1 - x if relation & 4 else x, 1 - y if relation & 2 else y, 1 - c if relation & 1 else c)
    return pos, 4 * pos[0] + 2 * pos[1] + pos[2]


def _slot(relation, by_chip):
    pos, device = _peer(relation)
    return 2 * pos[0] + pos[1] if by_chip else device


def _exchange_copies(ins, lands, send_sems, recv_sems, scatter, relations, by_chip=False):
    me = _slot(0, by_chip)

    def copy(a, s, peer, pos, dst_slot):
        return pltpu.make_async_remote_copy(
            src_ref=ins[a].at[peer] if scatter else ins[a], dst_ref=lands[a].at[dst_slot],
            send_sem=send_sems.at[s], recv_sem=recv_sems.at[s], device_id=pos, device_id_type=pl.DeviceIdType.MESH)

    pairs = []
    for k, r in enumerate(relations):
        pos, peer = _peer(r)[0], _slot(r, by_chip)
        for a in range(len(ins)):
            s = a * len(relations) + k
            pairs.append((copy(a, s, peer, pos, me), copy(a, s, peer, pos, peer)))
    return pairs


def _halves_copies(arrays, lands, send_sems, recv_sems):
    sibling, _ = _peer(SIBLING)
    core = lax.axis_index("c")
    pairs = []
    for a, (ref, land) in enumerate(zip(arrays, lands)):
        send = pltpu.make_async_remote_copy(
            src_ref=ref.at[:, pl.ds(1 - core, 1)], dst_ref=land, send_sem=send_sems.at[a], recv_sem=recv_sems.at[a],
            device_id=sibling, device_id_type=pl.DeviceIdType.MESH)
        pairs.append((send, send))
    return pairs


def _forward_copies(lands, send_sems, recv_sems):
    sibling, _ = _peer(SIBLING)

    def copy(a, s, slot):
        return pltpu.make_async_remote_copy(
            src_ref=lands[a].at[slot], dst_ref=lands[a].at[slot], send_sem=send_sems.at[s], recv_sem=recv_sems.at[s],
            device_id=sibling, device_id_type=pl.DeviceIdType.MESH)

    pairs = []
    for k, r in enumerate(CHIP_PEERS):
        _, mine = _peer(r)
        _, theirs = _peer(r | SIBLING)
        for a in range(len(lands)):
            s = a * len(CHIP_PEERS) + k
            pairs.append((copy(a, s, mine), copy(a, s, theirs)))
    return pairs


_HBM_SPEC = pl.BlockSpec(memory_space=pltpu.HBM)
_SEM_SPEC = pl.BlockSpec(memory_space=pltpu.SEMAPHORE)
_SIDE_EFFECT = pltpu.SideEffectType.DATAFLOW_SIDE_EFFECTING


def _split_start(name, operands, n_sem, make_pairs):
    k = len(operands)

    def body(*refs):
        send_sems, recv_sems, token = refs[k], refs[k + 1], refs[-1]
        for send, _ in make_pairs(refs[:k], send_sems, recv_sems):
            send.start()
        token[...] = jnp.zeros_like(token)

    out = pl.pallas_call(
        body, name=name,
        out_shape=(pltpu.SemaphoreType.DMA((n_sem,)), pltpu.SemaphoreType.DMA((n_sem,)),
                   *[pltpu.HBM(a.shape, a.dtype) for a in operands], _sds((SUBLANES, LANES), F32)),
        in_specs=[_HBM_SPEC] * k,
        out_specs=(_SEM_SPEC, _SEM_SPEC, *[_HBM_SPEC] * k, pl.BlockSpec(memory_space=pltpu.VMEM)),
        input_output_aliases={i: 2 + i for i in range(k)},
        compiler_params=pltpu.CompilerParams(has_side_effects=_SIDE_EFFECT),
    )(*[pltpu.with_memory_space_constraint(a, pltpu.HBM) for a in operands])
    return dict(name=name, sems=out[:2], thru=list(out[2:2 + k]), make_pairs=make_pairs), out[-1]


def _split_wait(handle, after):
    thru, make_pairs = handle["thru"], handle["make_pairs"]
    k = len(thru)

    def body(*refs):
        for send, arrival in make_pairs(refs[:k], refs[k], refs[k + 1]):
            send.wait_send()
            arrival.wait_recv()

    return pl.pallas_call(
        body, name=handle["name"] + "_wait", out_shape=[pltpu.HBM(a.shape, a.dtype) for a in thru],
        in_specs=[_HBM_SPEC] * k + [_SEM_SPEC, _SEM_SPEC] + [pl.BlockSpec(memory_space=pl.ANY)] * len(after),
        out_specs=[_HBM_SPEC] * k, input_output_aliases={i: i for i in range(k)},
        compiler_params=pltpu.CompilerParams(has_side_effects=_SIDE_EFFECT),
    )(*thru, *handle["sems"], *after)


def _exchange_start(name, arrays, scatter, relations, by_chip=False):
    n = len(arrays)
    lands = [lax.empty(a.shape if scatter else (N_DEV,) + a.shape, a.dtype) for a in arrays]

    def make_pairs(refs, send_sems, recv_sems):
        return _exchange_copies(refs[:n], refs[n:], send_sems, recv_sems, scatter, relations, by_chip)

    handle, token = _split_start(name, list(arrays) + lands, n * len(relations), make_pairs)
    handle.update(n=n)
    return handle, token


def _halves_start(name, arrays):
    lands = [lax.empty((a.shape[0], 1) + a.shape[2:], a.dtype) for a in arrays]
    n = len(arrays)

    def make_pairs(refs, send_sems, recv_sems):
        return _halves_copies(refs[:n], refs[n:], send_sems, recv_sems)

    return _split_start(name, list(arrays) + lands, n, make_pairs)


def _chip_sum(name, array, landed):
    chips, _, r, c = array.shape
    tr = r

    def body(a_ref, b_ref, o_ref):
        mine = a_ref[lax.axis_index("c")].astype(F32)
        o_ref[...] = (mine + b_ref[...].astype(F32)).astype(o_ref.dtype)

    return _call(body, name=name, grid=(chips, r // tr),
                 in_specs=[pl.BlockSpec((None, 2, tr, c), lambda k, i: (k, 0, i, 0)),
                           pl.BlockSpec((None, None, tr, c), lambda k, i: (k, 0, i, 0))],
                 out_specs=pl.BlockSpec((None, tr, c), lambda k, i: (k, i, 0)),
                 out_shape=_sds((chips, r, c), BF16), semantics=("parallel", "parallel"))(array, landed)


def _forward_start(name, lands):
    return _split_start(name, list(lands), len(lands) * len(CHIP_PEERS), _forward_copies)


def _exchange_wait(handle, after):
    return _split_wait(handle, after)[handle["n"]:]


def _rope_tables(pos_col):
    t = pos_col.shape[0]
    half = HEAD_DIM // 2
    inv_freq = ROPE_THETA ** (-jnp.arange(half, dtype=F32) / half)
    inv_row = jnp.tile(inv_freq, LANES // half)[None, :]

    def body(pos_ref, inv_ref, cos_ref, sin_ref):
        ang = pos_ref[...] * inv_ref[...]
        cos_ref[...] = jnp.cos(ang)
        sin_ref[...] = jnp.sin(ang)

    tm = min(t, 512)
    return _call(body, name="rope_tables", grid=(t // tm,),
                 in_specs=[pl.BlockSpec((tm, 1), lambda i: (i, 0)), pl.BlockSpec((1, LANES), lambda i: (0, 0))],
                 out_specs=[pl.BlockSpec((tm, LANES), lambda i: (i, 0))] * 2,
                 out_shape=[_sds((t, LANES), F32)] * 2, semantics=("parallel",))(pos_col, inv_row)


def _rot_half(x):
    lane = lax.broadcasted_iota(jnp.int32, x.shape, 1)
    low = (lane % HEAD_DIM) < HEAD_DIM // 2
    return jnp.where(low, -pltpu.roll(x, LANES - HEAD_DIM // 2, 1), pltpu.roll(x, HEAD_DIM // 2, 1))


def _rope(x, cos, sin):
    return x * cos + _rot_half(x) * sin


def _unrope(d, cos, sin):
    return d * cos - _rot_half(d) * sin


def _band_mask(first_block, heads):
    r = lax.broadcasted_iota(jnp.int32, (heads * BLOCK, 2 * BLOCK), 0) % BLOCK
    c = lax.broadcasted_iota(jnp.int32, (heads * BLOCK, 2 * BLOCK), 1)
    diff = r - c + BLOCK
    return (diff >= 0) & (diff < WINDOW) & ((c >= BLOCK) | jnp.logical_not(first_block))


def _attn_specs(t, d_attn, d_in):
    kb, vb = d_attn // D_KV, d_attn // D_KV + 1
    prev = lambda i: jnp.maximum(i - 1, 0)
    return [
        pl.BlockSpec((BLOCK, d_attn), lambda i: (i, 0)),
        pl.BlockSpec((BLOCK, D_KV), lambda i: (i, kb)),
        pl.BlockSpec((BLOCK, D_KV), lambda i: (i, vb)),
        pl.BlockSpec((BLOCK, D_KV), lambda i: (prev(i), kb)),
        pl.BlockSpec((BLOCK, D_KV), lambda i: (prev(i), vb)),
        pl.BlockSpec((BLOCK, LANES), lambda i: (i, 0)),
        pl.BlockSpec((BLOCK, LANES), lambda i: (i, 0)),
        pl.BlockSpec((BLOCK, LANES), lambda i: (prev(i), 0)),
        pl.BlockSpec((BLOCK, LANES), lambda i: (prev(i), 0)),
        pl.BlockSpec((1, LANES), lambda i: (0, 0)),
    ]


def _head(x, h):
    return x[:, h * HEAD_DIM:(h + 1) * HEAD_DIM]


def _attn_heads(q_ref, kc_ref, vc_ref, kp_ref, vp_ref, cq_ref, sq_ref, cp_ref, sp_ref, d_attn):
    cq, sq, cp, sp = cq_ref[...], sq_ref[...], cp_ref[...], sp_ref[...]
    q_rot = [_rope(q_ref[:, j * LANES:(j + 1) * LANES], cq, sq) for j in range(d_attn // LANES)]
    kc_rot = [_rope(kc_ref[:, j * LANES:(j + 1) * LANES], cq, sq) for j in range(D_KV // LANES)]
    kp_rot = [_rope(kp_ref[:, j * LANES:(j + 1) * LANES], cp, sp) for j in range(D_KV // LANES)]
    per = LANES // HEAD_DIM
    q_heads = [_head(q_rot[h // per], h % per).astype(BF16) for h in range(d_attn // HEAD_DIM)]
    kk = [jnp.concatenate([_head(kp_rot[g // per], g % per), _head(kc_rot[g // per], g % per)], axis=0).astype(BF16)
          for g in range(N_KV_HEADS)]
    vv = [jnp.concatenate([_head(vp_ref[...], g), _head(vc_ref[...], g)], axis=0).astype(BF16) for g in range(N_KV_HEADS)]
    return q_heads, kk, vv


def _stack_group(q_heads, sink_ref, group):
    q_all = jnp.concatenate([q_heads[h] for h in group], axis=0)
    sink_all = jnp.concatenate([jnp.broadcast_to(sink_ref[:, h:h + 1], (BLOCK, 1)) for h in group], axis=0)
    return q_all, sink_all


def _softmax_with_sink(q, kk, sink, mask):
    s = _dot(q, kk, 1, 1) * (1.0 / math.sqrt(HEAD_DIM))
    s = jnp.where(mask, s, MASKED)
    m = jnp.maximum(jnp.max(s, axis=-1, keepdims=True), sink)
    p = jnp.exp(s - m)
    e_sink = jnp.exp(sink - m)
    inv = 1.0 / (jnp.sum(p, axis=-1, keepdims=True) + e_sink)
    return p * inv, e_sink * inv


def _attention_fwd(proj, cos, sin, sinks_row, d_attn):
    t, d_in = proj.shape
    n_heads = d_attn // HEAD_DIM
    q_per_kv = n_heads // N_KV_HEADS

    def body(q_ref, kc_ref, vc_ref, kp_ref, vp_ref, cq_ref, sq_ref, cp_ref, sp_ref, sink_ref, o_ref):
        mask = _band_mask(pl.program_id(0) == 0, q_per_kv)
        q_heads, kk, vv = _attn_heads(q_ref, kc_ref, vc_ref, kp_ref, vp_ref, cq_ref, sq_ref, cp_ref, sp_ref, d_attn)
        for g in range(N_KV_HEADS):
            group = range(g * q_per_kv, (g + 1) * q_per_kv)
            q_all, sink_all = _stack_group(q_heads, sink_ref, group)
            probs, _ = _softmax_with_sink(q_all, kk[g], sink_all, mask)
            o_all = _dot(probs.astype(BF16), vv[g], 1, 0)
            for k, h in enumerate(group):
                o_ref[:, h * HEAD_DIM:(h + 1) * HEAD_DIM] = o_all[k * BLOCK:(k + 1) * BLOCK]

    return _call(body, name="attention_fwd", grid=(t // BLOCK,), in_specs=_attn_specs(t, d_attn, d_in),
                 out_specs=pl.BlockSpec((BLOCK, d_attn), lambda i: (i, 0)), out_shape=_sds((t, d_attn), F32),
                 semantics=("parallel",))(proj, proj, proj, proj, proj, cos, sin, cos, sin, sinks_row)


def _attention_bwd(proj, cos, sin, sinks_row, d_out, d_attn):
    t, d_in = proj.shape
    n_heads = d_attn // HEAD_DIM
    q_per_kv = n_heads // N_KV_HEADS
    nb = t // BLOCK
    per = LANES // HEAD_DIM
    stack = q_per_kv

    def body(q_ref, kc_ref, vc_ref, kp_ref, vp_ref, cq_ref, sq_ref, cp_ref, sp_ref, sink_ref, do_ref,
             dq_ref, dk_ref, dv_ref, dsink_ref):
        i = pl.program_id(0)
        mask = _band_mask(i == 0, stack)
        q_heads, kk, vv = _attn_heads(q_ref, kc_ref, vc_ref, kp_ref, vp_ref, cq_ref, sq_ref, cp_ref, sp_ref, d_attn)
        lane = lax.broadcasted_iota(jnp.int32, (1, LANES), 1)
        dsink = jnp.zeros((1, LANES), F32)
        dq_rot, dkk, dvv = [], [], []
        for g in range(N_KV_HEADS):
            dkk_g = jnp.zeros((2 * BLOCK, HEAD_DIM), F32)
            dvv_g = jnp.zeros((2 * BLOCK, HEAD_DIM), F32)
            for first in range(g * q_per_kv, (g + 1) * q_per_kv, stack):
                group = range(first, first + stack)
                q_all, sink_all = _stack_group(q_heads, sink_ref, group)
                probs, p_sink = _softmax_with_sink(q_all, kk[g], sink_all, mask)
                do_all = jnp.concatenate([do_ref[:, h * HEAD_DIM:(h + 1) * HEAD_DIM] for h in group],
                                         axis=0).astype(BF16)
                dp = _dot(do_all, vv[g], 1, 1)
                delta = jnp.sum(probs * dp, axis=-1, keepdims=True)
                ds = (probs * (dp - delta) * (1.0 / math.sqrt(HEAD_DIM))).astype(BF16)
                dq_all = _dot(ds, kk[g], 1, 0)
                dkk_g += _dot(ds, q_all, 0, 0)
                dvv_g += _dot(probs.astype(BF16), do_all, 0, 0)
                sink_term = p_sink * delta
                for k, h in enumerate(group):
                    dq_rot.append(dq_all[k * BLOCK:(k + 1) * BLOCK])
                    part = jnp.sum(sink_term[k * BLOCK:(k + 1) * BLOCK], axis=0, keepdims=True)
                    dsink += jnp.where(lane == h, -part, 0.0)
            dkk.append(dkk_g)
            dvv.append(dvv_g)
        cq, sq, cp, sp = cq_ref[...], sq_ref[...], cp_ref[...], sp_ref[...]
        for j in range(d_attn // LANES):
            d = jnp.concatenate(dq_rot[j * per:(j + 1) * per], axis=1)
            dq_ref[:, j * LANES:(j + 1) * LANES] = _unrope(d, cq, sq)
        for j in range(D_KV // LANES):
            d = jnp.concatenate(dkk[j * per:(j + 1) * per], axis=1)
            dk_ref[0, :, j * LANES:(j + 1) * LANES] = _unrope(d[:BLOCK], cp, sp)
            dk_ref[1, :, j * LANES:(j + 1) * LANES] = _unrope(d[BLOCK:], cq, sq)
            d = jnp.concatenate(dvv[j * per:(j + 1) * per], axis=1)
            dv_ref[0, :, j * LANES:(j + 1) * LANES] = d[:BLOCK]
            dv_ref[1, :, j * LANES:(j + 1) * LANES] = d[BLOCK:]

        @pl.when(i == 0)
        def _():
            dsink_ref[...] = jnp.zeros_like(dsink_ref)

        dsink_ref[...] += dsink

    pair = pl.BlockSpec((2, BLOCK, D_KV), lambda i: (i, 0, 0))
    return _call(body, name="attention_bwd", grid=(nb,),
                 in_specs=_attn_specs(t, d_attn, d_in) + [pl.BlockSpec((BLOCK, d_attn), lambda i: (i, 0))],
                 out_specs=[pl.BlockSpec((BLOCK, d_attn), lambda i: (i, 0)), pair, pair,
                            pl.BlockSpec((1, LANES), lambda i: (0, 0))],
                 out_shape=[_sds((t, d_attn), F32), _sds((2 * nb, BLOCK, D_KV), F32), _sds((2 * nb, BLOCK, D_KV), F32),
                            _sds((1, LANES), F32)],
                 semantics=("arbitrary",))(proj, proj, proj, proj, proj, cos, sin, cos, sin, sinks_row, d_out)


def _assemble_dproj(dq, dk2, dv2, du, d_in, after):
    t, d_attn = dq.shape
    d_ssm = du.shape[1]
    nb = t // BLOCK

    def body(dq_ref, dk_own, dk_next, dv_own, dv_next, du_ref, o_ref):
        has_next = (pl.program_id(0) < nb - 1).astype(F32)
        o_ref[:, :d_attn] = dq_ref[...].astype(BF16)
        o_ref[:, d_attn:d_attn + D_KV] = (dk_own[...] + has_next * dk_next[...]).astype(BF16)
        o_ref[:, d_attn + D_KV:d_attn + 2 * D_KV] = (dv_own[...] + has_next * dv_next[...]).astype(BF16)
        o_ref[:, d_attn + 2 * D_KV:] = du_ref[...].astype(BF16)

    own = pl.BlockSpec((None, BLOCK, D_KV), lambda i: (2 * i + 1, 0, 0))
    nxt = pl.BlockSpec((None, BLOCK, D_KV), lambda i: (jnp.minimum(2 * i + 2, 2 * nb - 1), 0, 0))
    return _call(body, name="assemble_dproj", grid=(nb,),
                 in_specs=[pl.BlockSpec((BLOCK, d_attn), lambda i: (i, 0)), own, nxt, own, nxt,
                           pl.BlockSpec((BLOCK, d_ssm), lambda i: (i, 0))],
                 out_specs=pl.BlockSpec((BLOCK, d_in), lambda i: (i, 0)), out_shape=_sds((t, d_in), BF16),
                 semantics=("parallel",), n_after=len(after))(dq, dk2, dk2, dv2, dv2, du, *after)


def _discretise(ar, ai, ldt, br, bi):
    dt = jnp.exp(ldt)
    mag = jnp.exp(ar * dt)
    lam_re = mag * jnp.cos(ai * dt)
    lam_im = mag * jnp.sin(ai * dt)
    den = ar * ar + ai * ai
    nr = lam_re - 1.0
    ni = lam_im
    f_re = (nr * ar + ni * ai) / den
    f_im = (ni * ar - nr * ai) / den
    return (lam_re, lam_im, [f_re * r - f_im * i for r, i in zip(br, bi)], [f_re * i + f_im * r for r, i in zip(br, bi)])


def _whole(arrays):
    return [pl.BlockSpec(a.shape, lambda *_, nd=len(a.shape): (0,) * nd) for a in arrays]


def _channels(ref):
    groups = ref.shape[0] // SSM_GROUP
    return [ref[pl.ds(p, groups, stride=SSM_GROUP), :] for p in range(SSM_GROUP)]


def _store_channels(ref, values):
    groups = ref.shape[0] // SSM_GROUP
    for p, val in enumerate(values):
        ref[pl.ds(p, groups, stride=SSM_GROUP), :] = val


def _s5_discretise(ar, ai, ldt, br, bi):
    ins = [ar, ai, ldt, br, bi]

    def body(ar_ref, ai_ref, ldt_ref, br_ref, bi_ref, lr_ref, li_ref, bbr_ref, bbi_ref):
        lr, li, bbr, bbi = _discretise(ar_ref[...], ai_ref[...], ldt_ref[...], _channels(br_ref), _channels(bi_ref))
        lr_ref[...] = lr
        li_ref[...] = li
        _store_channels(bbr_ref, bbr)
        _store_channels(bbi_ref, bbi)

    outs = [_sds(ar.shape, F32), _sds(ar.shape, F32), _sds(br.shape, F32), _sds(br.shape, F32)]
    return _call(body, name="s5_discretise", in_specs=_whole(ins), out_specs=_whole(outs), out_shape=outs)(*ins)


def _s5_discretise_bwd(ar, ai, ldt, br, bi, d_lr, d_li, d_bbr, d_bbi):
    ins = [ar, ai, ldt, br, bi, d_lr, d_li, d_bbr, d_bbi]

    def body(ar_ref, ai_ref, ldt_ref, br_ref, bi_ref, dlr_ref, dli_ref, dbbr_ref, dbbi_ref,
             dar_ref, dai_ref, dldt_ref, dbr_ref, dbi_ref):
        _, vjp = jax.vjp(_discretise, ar_ref[...], ai_ref[...], ldt_ref[...], _channels(br_ref), _channels(bi_ref))
        dar, dai, dldt, dbr, dbi = vjp((dlr_ref[...], dli_ref[...], _channels(dbbr_ref), _channels(dbbi_ref)))
        dar_ref[...] = dar
        dai_ref[...] = dai
        dldt_ref[...] = dldt
        _store_channels(dbr_ref, dbr)
        _store_channels(dbi_ref, dbi)

    outs = [_sds(a.shape, F32) for a in (ar, ai, ldt, br, bi)]
    return _call(body, name="s5_discretise_bwd", in_specs=_whole(ins), out_specs=_whole(outs), out_shape=outs)(*ins)


def _cmul(ar, ai, br, bi):
    return ar * br - ai * bi, ar * bi + ai * br


def _load_segmented(ref, tile0, n_tiles, seg):
    return jnp.concatenate([ref[pl.ds(tile0 + j, SUBLANES, stride=seg), :] for j in range(n_tiles)], axis=0)


def _store_segmented(ref, tile0, seg, value):
    for j in range(value.shape[0] // SUBLANES):
        ref[pl.ds(tile0 + j, SUBLANES, stride=seg), :] = value[j * SUBLANES:(j + 1) * SUBLANES, :]


def _fill_powers(lr, li, pr_ref, pi_ref, seg):
    pows = [(lr, li)]
    for _ in range(SUBLANES - 1):
        pows.append(_cmul(pows[-1][0], pows[-1][1], lr, li))
    row = lax.broadcasted_iota(jnp.int32, (SUBLANES, lr.shape[1]), 0)
    tr = jnp.zeros((SUBLANES, lr.shape[1]), F32)
    ti = jnp.zeros((SUBLANES, lr.shape[1]), F32)
    for r in range(SUBLANES):
        tr = jnp.where(row == r, pows[r][0], tr)
        ti = jnp.where(row == r, pows[r][1], ti)
    pr_ref[0:SUBLANES, :] = tr
    pi_ref[0:SUBLANES, :] = ti
    k = SUBLANES
    while k < seg:
        fr, fi = pr_ref[k - 1:k, :], pi_ref[k - 1:k, :]
        for t0 in range(0, k, SUBLANES):
            nr, ni = _cmul(pr_ref[t0:t0 + SUBLANES, :], pi_ref[t0:t0 + SUBLANES, :], fr, fi)
            pr_ref[k + t0:k + t0 + SUBLANES, :] = nr
            pi_ref[k + t0:k + t0 + SUBLANES, :] = ni
        k *= 2


def _scan_segments(sr_ref, si_ref, pr_ref, pi_ref, lr, li, seg, reverse, per_tile=None):
    w = lr.shape[1]
    sign = -1.0 if reverse else 1.0
    lrb = jnp.broadcast_to(lr, (SUBLANES, w))
    lib = jnp.broadcast_to(sign * li, (SUBLANES, w))
    zero = jnp.zeros((SUBLANES, w), F32)

    def tile_rows(j):
        return pl.ds(pl.multiple_of(j * SUBLANES, SUBLANES), SUBLANES)

    steps = 4 if seg % 4 == 0 else 1

    def local(i, carry):
        for u in range(steps):
            j = i * steps + u
            rows = tile_rows(seg - 1 - j if reverse else j)
            pr, pi = _cmul(lrb, lib, carry[0], carry[1])
            carry = (sr_ref[rows, :] + pr, si_ref[rows, :] + pi)
            sr_ref[rows, :] = carry[0]
            si_ref[rows, :] = carry[1]
        return carry

    end_r, end_i = lax.fori_loop(0, seg // steps, local, (zero, zero))
    full_r, full_i = pr_ref[seg - 1:seg, :], sign * pi_ref[seg - 1:seg, :]
    row = lax.broadcasted_iota(jnp.int32, (SUBLANES, w), 0)
    in_r, in_i = zero, zero
    cur_r, cur_i = jnp.zeros((1, w), F32), jnp.zeros((1, w), F32)
    for r in (range(SUBLANES - 2, -1, -1) if reverse else range(1, SUBLANES)):
        src = r + 1 if reverse else r - 1
        pr, pi = _cmul(full_r, full_i, cur_r, cur_i)
        cur_r, cur_i = end_r[src:src + 1, :] + pr, end_i[src:src + 1, :] + pi
        in_r = jnp.where(row == r, cur_r, in_r)
        in_i = jnp.where(row == r, cur_i, in_i)

    def carry_in(j, _):
        rows = tile_rows(j)
        k = seg - 1 - j if reverse else j
        pr, pi = _cmul(pr_ref[pl.ds(k, 1), :], sign * pi_ref[pl.ds(k, 1), :], in_r, in_i)
        xr, xi = sr_ref[rows, :] + pr, si_ref[rows, :] + pi
        sr_ref[rows, :] = xr
        si_ref[rows, :] = xi
        if per_tile is not None:
            per_tile(j, xr, xi)
        return 0

    lax.fori_loop(0, seg, carry_in, 0, unroll=4)


_S5_ROWS = 2048


def _s5_in_specs(t, d_attn):
    u_block = (d_attn + 2 * D_KV) // SSM_CH_BLOCK
    blk3 = lambda shape: pl.BlockSpec((None,) + shape, lambda j: (j, 0, 0))
    return [
        pl.BlockSpec((t, SSM_CH_BLOCK), lambda j: (0, u_block + j)),
        blk3((SSM_CH_BLOCK, SSM_ST_BLOCK)), blk3((SSM_CH_BLOCK, SSM_ST_BLOCK)),
        blk3((1, SSM_ST_BLOCK)), blk3((1, SSM_ST_BLOCK)),
        blk3((SSM_ST_BLOCK, SSM_CH_BLOCK)), blk3((SSM_ST_BLOCK, SSM_CH_BLOCK)),
        pl.BlockSpec((1, SSM_CH_BLOCK), lambda j: (0, j)),
    ]


def _chunks(t):
    rows = min(_S5_ROWS, t)
    return rows, lambda i: pl.ds(pl.multiple_of(i * rows, rows), rows)


def _s5_states(u_ref, us_ref, bre_ref, bim_ref, lr_ref, li_ref, sr_ref, si_ref, pr_ref, pi_ref, t):
    seg = t // SUBLANES
    rows, chunk = _chunks(t)
    for c in range(t // rows):
        us_ref[c * rows:(c + 1) * rows, :] = _load_segmented(u_ref, c * rows // SUBLANES, rows // SUBLANES, seg)

    def fill(i, _):
        ub = us_ref[chunk(i), :].astype(BF16)
        sr_ref[chunk(i), :] = _dot(ub, bre_ref[...], 1, 0)
        si_ref[chunk(i), :] = _dot(ub, bim_ref[...], 1, 0)
        return 0

    lax.fori_loop(0, t // rows, fill, 0)
    _fill_powers(lr_ref[...], li_ref[...], pr_ref, pi_ref, seg)
    _scan_segments(sr_ref, si_ref, pr_ref, pi_ref, lr_ref[...], li_ref[...], seg, False)


def _s5_scratch(t):
    state = pltpu.VMEM((t, SSM_ST_BLOCK), F32)
    powers = pltpu.VMEM((t // SUBLANES, SSM_ST_BLOCK), F32)
    return state, powers, pltpu.VMEM((t, SSM_CH_BLOCK), F32)


def _s5_fwd(proj, mats, dskip_row, d_attn, d_ssm):
    t = proj.shape[0]
    seg = t // SUBLANES
    n_blocks = d_ssm // SSM_CH_BLOCK
    rows, chunk = _chunks(t)

    def body(u_ref, bre_ref, bim_ref, lr_ref, li_ref, cre_ref, cim_ref, d_ref, y_ref,
             sr_ref, si_ref, pr_ref, pi_ref, us_ref, ys_ref):
        _s5_states(u_ref, us_ref, bre_ref, bim_ref, lr_ref, li_ref, sr_ref, si_ref, pr_ref, pi_ref, t)

        def emit(i, _):
            ys_ref[chunk(i), :] = (_dot(sr_ref[chunk(i), :].astype(BF16), cre_ref[...], 1, 0)
                                   - _dot(si_ref[chunk(i), :].astype(BF16), cim_ref[...], 1, 0)
                                   + d_ref[...] * us_ref[chunk(i), :])
            return 0

        lax.fori_loop(0, t // rows, emit, 0)
        for c in range(t // rows):
            _store_segmented(y_ref, c * rows // SUBLANES, seg, ys_ref[c * rows:(c + 1) * rows, :])

    state, powers, channels = _s5_scratch(t)
    col = pl.BlockSpec((t, SSM_CH_BLOCK), lambda j: (0, j))
    return _call(body, name="s5_fwd", grid=(n_blocks,), in_specs=_s5_in_specs(t, d_attn), out_specs=col,
                 out_shape=_sds((t, d_ssm), F32), scratch_shapes=[state, state, powers, powers, channels, channels],
                 semantics=("parallel",))(proj, *mats, dskip_row)


def _s5_bwd(proj, mats, dskip_row, y, dz_a, dz_b, d_attn, d_ssm, after):
    t = proj.shape[0]
    seg = t // SUBLANES
    n_blocks = d_ssm // SSM_CH_BLOCK
    rows, chunk = _chunks(t)

    def body(u_ref, bre_ref, bim_ref, lr_ref, li_ref, cre_ref, cim_ref, d_ref, y_ref, dza_ref, dzb_ref,
             du_ref, dbre_ref, dbim_ref, dlr_ref, dli_ref, dcre_ref, dcim_ref, dd_ref,
             sr_ref, si_ref, gr_ref, gi_ref, pr_ref, pi_ref, us_ref, dys_ref, dus_ref, acc_r, acc_i):
        _s5_states(u_ref, us_ref, bre_ref, bim_ref, lr_ref, li_ref, sr_ref, si_ref, pr_ref, pi_ref, t)
        for ref in (dcre_ref, dcim_ref, dbre_ref, dbim_ref, dd_ref, acc_r, acc_i):
            ref[...] = jnp.zeros_like(ref)
        for c in range(t // rows):
            tile0, n_tiles = c * rows // SUBLANES, rows // SUBLANES
            dz = _load_segmented(dza_ref, tile0, n_tiles, seg) + _load_segmented(dzb_ref, tile0, n_tiles, seg)
            dys_ref[c * rows:(c + 1) * rows, :] = dz * _gelu_grad(_load_segmented(y_ref, tile0, n_tiles, seg))

        def through_c(i, _):
            dy = dys_ref[chunk(i), :]
            dd_ref[...] += jnp.sum(dy * us_ref[chunk(i), :], axis=0, keepdims=True)
            dyb = dy.astype(BF16)
            gr_ref[chunk(i), :] = _dot(dyb, cre_ref[...], 1, 1)
            gi_ref[chunk(i), :] = -_dot(dyb, cim_ref[...], 1, 1)
            dcre_ref[...] += _dot(sr_ref[chunk(i), :].astype(BF16), dyb, 0, 0)
            dcim_ref[...] -= _dot(si_ref[chunk(i), :].astype(BF16), dyb, 0, 0)
            return 0

        lax.fori_loop(0, t // rows, through_c, 0)

        row = lax.broadcasted_iota(jnp.int32, (SUBLANES, SSM_ST_BLOCK), 0)
        last = pl.ds((seg - 1) * SUBLANES, SUBLANES)
        wrap = [jnp.where(row == 0, 0.0, pltpu.roll(ref[last, :], 1, 0)) for ref in (sr_ref, si_ref)]

        def lambda_grad(j, g_re, g_im):
            before = pl.ds(pl.multiple_of(jnp.maximum(j - 1, 0) * SUBLANES, SUBLANES), SUBLANES)
            prev_r = jnp.where(j > 0, sr_ref[before, :], wrap[0])
            prev_i = jnp.where(j > 0, si_ref[before, :], wrap[1])
            acc_r[...] += g_re * prev_r + g_im * prev_i
            acc_i[...] += g_im * prev_r - g_re * prev_i

        _scan_segments(gr_ref, gi_ref, pr_ref, pi_ref, lr_ref[...], li_ref[...], seg, True, per_tile=lambda_grad)
        dlr_ref[...] = jnp.sum(acc_r[...], axis=0, keepdims=True)
        dli_ref[...] = jnp.sum(acc_i[...], axis=0, keepdims=True)

        def through_b(i, _):
            ub = us_ref[chunk(i), :].astype(BF16)
            grb, gib = gr_ref[chunk(i), :].astype(BF16), gi_ref[chunk(i), :].astype(BF16)
            dbre_ref[...] += _dot(ub, grb, 0, 0)
            dbim_ref[...] += _dot(ub, gib, 0, 0)
            dus_ref[chunk(i), :] = (_dot(grb, bre_ref[...], 1, 1) + _dot(gib, bim_ref[...], 1, 1)
                                    + d_ref[...] * dys_ref[chunk(i), :])
            return 0

        lax.fori_loop(0, t // rows, through_b, 0)
        for c in range(t // rows):
            _store_segmented(du_ref, c * rows // SUBLANES, seg, dus_ref[c * rows:(c + 1) * rows, :])

    col = pl.BlockSpec((t, SSM_CH_BLOCK), lambda j: (0, j))
    blk3 = lambda shape: pl.BlockSpec((None,) + shape, lambda j: (j, 0, 0))
    state, powers, channels = _s5_scratch(t)
    return _call(
        body, name="s5_bwd", grid=(n_blocks,), in_specs=_s5_in_specs(t, d_attn) + [col, col, col],
        out_specs=[col, blk3((SSM_CH_BLOCK, SSM_ST_BLOCK)), blk3((SSM_CH_BLOCK, SSM_ST_BLOCK)),
                   blk3((1, SSM_ST_BLOCK)), blk3((1, SSM_ST_BLOCK)),
                   blk3((SSM_ST_BLOCK, SSM_CH_BLOCK)), blk3((SSM_ST_BLOCK, SSM_CH_BLOCK)),
                   pl.BlockSpec((1, SSM_CH_BLOCK), lambda j: (0, j))],
        out_shape=[_sds((t, d_ssm), F32),
                   _sds((n_blocks, SSM_CH_BLOCK, SSM_ST_BLOCK), F32), _sds((n_blocks, SSM_CH_BLOCK, SSM_ST_BLOCK), F32),
                   _sds((n_blocks, 1, SSM_ST_BLOCK), F32), _sds((n_blocks, 1, SSM_ST_BLOCK), F32),
                   _sds((n_blocks, SSM_ST_BLOCK, SSM_CH_BLOCK), F32), _sds((n_blocks, SSM_ST_BLOCK, SSM_CH_BLOCK), F32),
                   _sds((1, d_ssm), F32)],
        scratch_shapes=[state, state, state, state, powers, powers, channels, channels, channels,
                        pltpu.VMEM((SUBLANES, SSM_ST_BLOCK), F32), pltpu.VMEM((SUBLANES, SSM_ST_BLOCK), F32)],
        semantics=("parallel",), n_after=len(after))(proj, *mats, dskip_row, y, dz_a, dz_b, *after)


def _by_block(gp_n):
    return gp_n.reshape(-1, GROUPS_PER_BLOCK, SSM_GROUP, SSM_STATE)


def _block_diag_in(bbar):
    eye = jnp.eye(GROUPS_PER_BLOCK, dtype=F32)
    return jnp.einsum("jgpn,gh->jgphn", _by_block(bbar), eye).reshape(-1, SSM_CH_BLOCK, SSM_ST_BLOCK)


def _block_diag_in_t(dense):
    d5 = dense.reshape(-1, GROUPS_PER_BLOCK, SSM_GROUP, GROUPS_PER_BLOCK, SSM_STATE)
    eye = jnp.eye(GROUPS_PER_BLOCK, dtype=F32)
    return jnp.einsum("jgphn,gh->jgpn", d5, eye).reshape(-1, SSM_STATE)


def _block_diag_out(c):
    eye = jnp.eye(GROUPS_PER_BLOCK, dtype=F32)
    return jnp.einsum("jgpn,gh->jgnhp", _by_block(c), eye).reshape(-1, SSM_ST_BLOCK, SSM_CH_BLOCK)


def _block_diag_out_t(dense):
    d5 = dense.reshape(-1, GROUPS_PER_BLOCK, SSM_STATE, GROUPS_PER_BLOCK, SSM_GROUP)
    eye = jnp.eye(GROUPS_PER_BLOCK, dtype=F32)
    return jnp.einsum("jgnhp,gh->jgpn", d5, eye).reshape(-1, SSM_STATE)


def _adamw(w, g, m, v):
    m = ADAM_B1 * m + (1.0 - ADAM_B1) * g
    v = ADAM_B2 * v + (1.0 - ADAM_B2) * (g * g)
    m_hat = m / (1.0 - ADAM_B1 ** ADAM_STEP)
    v_hat = v / (1.0 - ADAM_B2 ** ADAM_STEP)
    delta = -ADAM_LR * (m_hat / (jnp.sqrt(v_hat) + ADAM_EPS) + ADAM_WD * w)
    return delta, m, v


def _adam_sharded(name, parts, w, m, v, tr, row0=0):
    r, c = w.shape
    assert r % tr == 0 and row0 % tr == 0, (name, r, tr, row0)

    def body(p_ref, w_ref, m_ref, v_ref, g_out, d_out, m_out, v_out):
        g = p_ref[0].astype(F32)
        for i in range(1, p_ref.shape[0]):
            g = g + p_ref[i].astype(F32)
        delta, m_new, v_new = _adamw(w_ref[...], g, m_ref[...], v_ref[...])
        g_out[...] = g
        d_out[...] = delta
        m_out[...] = m_new
        v_out[...] = v_new

    tile = pl.BlockSpec((tr, c), lambda i: (i, 0))
    return _call(body, name=name, grid=(r // tr,),
                 in_specs=[pl.BlockSpec((parts.shape[0], tr, c), lambda i: (0, i + row0 // tr, 0)), tile, tile, tile],
                 out_specs=[tile] * 4, out_shape=[_sds((r, c), F32)] * 4, semantics=("parallel",))(parts, w, m, v)


_BIG = ("w_in", "w_glu", "w_o", "w_gate", "w_up", "w_down")
_BY_COLUMNS = ("w_in", "w_gate", "w_up")
_SMALL_VECTORS = ("sinks", "log_dt", "b_glu", "g_attn_out", "g_ssm_out", "g_post_mix", "g_pre_ffn", "g_post_ffn")
_SMALL_MATRICES = ("b_re", "b_im", "c_re", "c_im", "a_re", "a_im")
_ORDER = ("g_pre_mix", "w_in", "sinks", "a_re", "a_im", "log_dt", "b_re", "b_im", "c_re", "c_im", "d_skip", "w_glu",
          "b_glu", "g_attn_out", "g_ssm_out", "w_o", "g_post_mix", "g_pre_ffn", "w_gate", "w_up", "w_down",
          "g_post_ffn")


def _pack_grads(vectors, matrices):
    width = max(a.shape[1] for a in vectors)
    slots, row, lane = [], 0, 0
    for a in vectors:
        span = -(-a.shape[1] // LANES) * LANES
        if lane + span > width:
            row, lane = row + 1, 0
        slots.append((row, lane, a.shape[1]))
        lane += span
    firsts, at = [], 0
    for a in matrices:
        firsts.append(at)
        at += a.shape[0]
    nv = len(vectors)

    def body(*refs):
        vec_out, mat_out = refs[-2], refs[-1]
        vec_out[...] = jnp.zeros_like(vec_out)
        for ref, (r, l, w) in zip(refs[:nv], slots):
            vec_out[r:r + 1, l:l + w] = ref[...]
        for ref, r0 in zip(refs[nv:-2], firsts):
            mat_out[r0:r0 + ref.shape[0], :] = ref[...]

    ins = list(vectors) + list(matrices)
    outs = [_sds((-(-(row + 1) // SUBLANES) * SUBLANES, width), F32), _sds((at, matrices[0].shape[1]), F32)]
    vec_pack, mat_pack = _call(body, name="pack_small_grads", in_specs=_whole(ins), out_specs=_whole(outs),
                               out_shape=outs)(*ins)
    return vec_pack, slots, mat_pack, firsts


def _adam_replicated(sources, found_at, w, m, v, total_at):
    ns, n = len(sources), len(w)

    def body(*refs):
        ins, outs = refs[ns:ns + 3 * n], refs[ns + 3 * n:]
        summed = []
        for p_ref in refs[:ns]:
            g = p_ref[0]
            for k in range(1, N_DEV):
                g = g + p_ref[k]
            summed.append(g)
        for i, (src, row, lane) in enumerate(found_at):
            w_ref, m_ref, v_ref = ins[i], ins[n + i], ins[2 * n + i]
            rows, cols = w_ref.shape
            g = summed[src][row:row + rows, lane:lane + cols]
            delta, m_new, v_new = _adamw(w_ref[...], g, m_ref[...], v_ref[...])
            for o, val in zip(outs[4 * i:4 * i + 4], (g, delta, m_new, v_new)):
                o[...] = val
        t_src, t_row, t_lane, t_width = total_at
        outs[-1][...] = summed[t_src][t_row:t_row + 1, t_lane:t_lane + t_width]

    ins = list(sources) + list(w) + list(m) + list(v)
    outs = [_sds(a.shape, F32) for a in w for _ in range(4)] + [_sds((1, total_at[3]), F32)]
    flat = _call(body, name="adam_replicated", in_specs=_whole(ins), out_specs=_whole(outs), out_shape=outs)(*ins)
    return [tuple(flat[4 * i:4 * i + 4]) for i in range(n)], flat[-1]


def kernel(x, positions, g_pre_mix, w_in, sinks, a_re, a_im, log_dt, b_re, b_im, c_re, c_im, d_skip, w_glu, b_glu, g_attn_out, g_ssm_out, w_o, g_post_mix, g_pre_ffn, w_gate, w_up, w_down, g_post_ffn, loss_target, m_g_pre_mix, m_w_in, m_sinks, m_a_re, m_a_im, m_log_dt, m_b_re, m_b_im, m_c_re, m_c_im, m_d_skip, m_w_glu, m_b_glu, m_g_attn_out, m_g_ssm_out, m_w_o, m_g_post_mix, m_g_pre_ffn, m_w_gate, m_w_up, m_w_down, m_g_post_ffn, v_g_pre_mix, v_w_in, v_sinks, v_a_re, v_a_im, v_log_dt, v_b_re, v_b_im, v_c_re, v_c_im, v_d_skip, v_w_glu, v_b_glu, v_g_attn_out, v_g_ssm_out, v_w_o, v_g_post_mix, v_g_pre_ffn, v_w_gate, v_w_up, v_w_down, v_g_post_ffn):
    given = dict(locals())
    weights = {n: given[n] for n in _ORDER}
    mom_m = {n: given["m_" + n] for n in _ORDER}
    mom_v = {n: given["v_" + n] for n in _ORDER}

    t, d = x.shape[1], x.shape[2]
    d_attn = d // 2
    d_ssm = d - d_attn
    d_in = d_attn + 2 * D_KV + d_ssm
    n_groups = d_ssm // SSM_GROUP
    n_heads = d_attn // HEAD_DIM
    tm = min(128, t)

    x2 = x[0]
    target = loss_target[0]

    def by_rows(n, a):
        return a[0].T if n in _BY_COLUMNS else a[0]

    def start_gather(name, ns, token):
        behind = 0 if token is None else token[0, 0].astype(BF16)
        shards = [by_rows(n, weights[n]).astype(BF16) + behind for n in ns]
        return _exchange_start(name, shards, False, (OWN, SIBLING) + CHIP_PEERS)

    def forward_gather(handle, after):
        return _forward_start(handle["name"] + "_forward", _exchange_wait(handle, after))

    def finish_gather(handle, after):
        return _split_wait(forward_gather(handle, after)[0], [])

    ag_in, token = start_gather("gather_w_in", ["w_in"], None)
    ag_mix, token = start_gather("gather_w_glu_o", ["w_glu", "w_o"], token)
    ag_ffn_in, token = start_gather("gather_w_gate_up", ["w_gate", "w_up"], token)
    ag_down, token = start_gather("gather_w_down", ["w_down"], token)

    xn, = _rows("norm_in", lambda xv, g: ([_rms(xv)[0] * g], []), [x2], [g_pre_mix], [(d, BF16)], [], tm,
                after=[token])
    win_g, = finish_gather(ag_in, [xn])
    w_in_t = win_g.reshape(d_in, d)
    proj = _mm_nt("proj_in", xn, w_in_t, F32)

    cos, sin = _rope_tables(positions.reshape(t, 1).astype(F32))
    sinks_row = jnp.pad(sinks, ((0, 0), (0, LANES - n_heads)))
    attn = _attention_fwd(proj, cos, sin, sinks_row, d_attn)

    def view(n, a):
        if n in ("b_re", "b_im"):
            return jnp.transpose(a[0], (0, 2, 1)).reshape(-1, SSM_STATE)
        if n in ("c_re", "c_im"):
            return a[0].reshape(-1, SSM_STATE)
        return a[0].T if n == "d_skip" else a[0] if a.ndim == 3 else a

    def unview(n, val):
        if n in ("b_re", "b_im"):
            return jnp.transpose(val.reshape(n_groups, SSM_GROUP, SSM_STATE), (0, 2, 1))[None]
        if n in ("c_re", "c_im"):
            return val.reshape(1, n_groups, SSM_GROUP, SSM_STATE)
        return val.T[None] if n == "d_skip" else val[None] if weights[n].ndim == 3 else val

    b_re_v, b_im_v = view("b_re", b_re), view("b_im", b_im)
    ldt_col = log_dt.reshape(n_groups, 1)
    lam_re, lam_im, bbar_re, bbar_im = _s5_discretise(a_re[0], a_im[0], ldt_col, b_re_v, b_im_v)
    n_blocks = n_groups // GROUPS_PER_BLOCK
    mats = [_block_diag_in(bbar_re).astype(BF16), _block_diag_in(bbar_im).astype(BF16),
            lam_re.reshape(n_blocks, 1, SSM_ST_BLOCK), lam_im.reshape(n_blocks, 1, SSM_ST_BLOCK),
            _block_diag_out(view("c_re", c_re)).astype(BF16), _block_diag_out(view("c_im", c_im)).astype(BF16)]
    dskip_row = d_skip.reshape(1, d_ssm)
    forward_mix, _ = forward_gather(ag_mix, [attn])
    y_ssm = _s5_fwd(proj, mats, dskip_row, d_attn, d_ssm)
    gelu_bf16 = lambda yv: _gelu(yv).astype(BF16)
    wglu_g, wo_g = _split_wait(forward_mix, [y_ssm])
    w_glu_full = wglu_g.reshape(d_ssm, d_ssm)
    w_o_full = wo_g.reshape(d, d)
    glu_lin = _mm_nn("glu_gate", y_ssm, w_glu_full, F32, a_fn=gelu_bf16)

    def mix_prep(av, yv, gl, bg, ga, gs):
        ssm = _gelu(yv) * _sigmoid(gl + bg)
        return [jnp.concatenate([_rms(av)[0] * ga, _rms(ssm)[0] * gs], axis=1)], []

    mixed, = _rows("mix_prep", mix_prep, [attn, y_ssm, glu_lin], [b_glu, g_attn_out, g_ssm_out], [(d, BF16)], [], tm)
    mix = _mm_nn("mix_out", mixed, w_o_full, F32)

    def post_mix(xv, mv, gpm, gpf):
        h = xv + _rms(mv)[0] * gpm
        return [h, _rms(h)[0] * gpf], []

    forward_ffn_in, token = forward_gather(ag_ffn_in, [mix])
    h, hn = _rows("post_mix", post_mix, [x2, mix], [g_post_mix, g_pre_ffn], [(d, F32), (d, BF16)], [], tm,
                  after=[token])
    wgate_g, wup_g = _split_wait(forward_ffn_in, [hn])
    d_ff = N_DEV * wgate_g.shape[1]
    wgate_t, wup_t = wgate_g.reshape(d_ff, d), wup_g.reshape(d_ff, d)
    gate, up, hid = _ffn_in(hn, wgate_t, wup_t)
    wdown_g, = finish_gather(ag_down, [hid])
    wdown_full = wdown_g.reshape(d_ff, d)
    ff = _mm_nn("ffn_down", hid, wdown_full, F32, tm=1024, tn=512)

    def head(hv, fv, tv, gpo):
        out = hv + _rms(fv)[0] * gpo
        err = out - tv
        dout = err * (1.0 / d)
        dff, dg = _rms_bwd(fv, gpo, dout)
        loss = jnp.zeros((1, LANES), F32) + 0.5 * jnp.sum(err * err) * (1.0 / d)
        return [dff, dout], [dg, loss]

    dff, dh_out, dg_post_ffn, loss_row = _rows("loss_head", head, [h, ff, target], [g_post_ffn],
                                               [(d, BF16), (d, F32)], [d, LANES], tm)

    def swap_halves(name, grads):
        return _halves_start("swap_" + name, [g.reshape(N_DEV // 2, 2, *g.shape[1:]) for g in grads])

    def scatter_chip_sums(name, swap, after):
        both = _split_wait(swap, after)
        half = len(both) // 2
        sums = [_chip_sum("chip_sum_%s_%d" % (name, i), both[i], both[half + i]) for i in range(half)]
        return _exchange_start("scatter_" + name, sums, True, (OWN,) + CHIP_PEERS, by_chip=True)

    f_tile = _hidden_tile(d_ff)
    by_owner = lambda g: g.reshape(N_DEV, d_ff // N_DEV, d)
    dw_down = by_owner(_mm_tn("ffn_down_dw", hid, dff, BF16, tm=f_tile))
    swap_down, token = swap_halves("dw_down", [dw_down])
    dgate, dup = _ffn_down_bwd(dff, wdown_full, gate, up, [token])
    rs_down, token = scatter_chip_sums("dw_down", swap_down, [dgate])
    dhn_gate = _mm_nn("ffn_in_dx_gate", dgate, wgate_t, F32, tm=1024, tn=512, after=[token])
    dhn = _mm_nn("ffn_in_dx_up", dup, wup_t, F32, tm=1024, tn=512, plus=dhn_gate)
    dw_gate = by_owner(_mm_tn("ffn_gate_dw", dgate, hn, BF16, tm=f_tile))
    dw_up = by_owner(_mm_tn("ffn_up_dw", dup, hn, BF16, tm=f_tile))
    swap_ffn_in, tok_ffn_in = swap_halves("dw_gate_up", [dw_gate, dw_up])

    def mid_bwd(dho, dhn_, hv, mv, gpf, gpm):
        d1, dgpf = _rms_bwd(hv, gpf, dhn_)
        dh_ = dho + d1
        dmix_, dgpm = _rms_bwd(mv, gpm, dh_)
        return [dh_, dmix_], [dgpf, dgpm]

    dh, dmix, dg_pre_ffn, dg_post_mix = _rows("mid_bwd", mid_bwd, [dh_out, dhn, h, mix], [g_pre_ffn, g_post_mix],
                                              [(d, F32), (d, BF16)], [d, d], tm, after=[tok_ffn_in])

    dmixed = _mm_nt("mix_out_dx", dmix, w_o_full, F32)
    rs_ffn_in, token = scatter_chip_sums("dw_gate_up", swap_ffn_in, [dmixed])
    dw_o = _mm_tn("mix_out_dw", mixed, dmix, BF16, after=[token])
    swap_o, tok_o = swap_halves("dw_o", [dw_o.reshape(N_DEV, d // N_DEV, d)])

    def mix_bwd(dm, av, yv, gl, bg, ga, gs):
        dattn_, dga = _rms_bwd(av, ga, dm[:, :d_attn])
        z = _gelu(yv)
        sg = _sigmoid(gl + bg)
        dssm, dgs = _rms_bwd(z * sg, gs, dm[:, d_attn:])
        dgl = dssm * z * sg * (1.0 - sg)
        return [dattn_, dssm * sg, dgl], [dga, dgs, jnp.sum(dgl, axis=0, keepdims=True)]

    dattn, dz_direct, dglu, dg_attn_out, dg_ssm_out, db_glu = _rows(
        "mix_bwd", mix_bwd, [dmixed, attn, y_ssm, glu_lin], [b_glu, g_attn_out, g_ssm_out],
        [(d_attn, F32), (d_ssm, F32), (d_ssm, BF16)], [d_attn, d_ssm, d_ssm], tm, after=[tok_o])
    dz_glu = _mm_nt("glu_gate_dx", dglu, w_glu_full, F32)
    dw_glu = _mm_tn("glu_gate_dw", y_ssm, dglu, BF16, a_fn=gelu_bf16)
    rs_o, token = scatter_chip_sums("dw_o", swap_o, [dz_glu, dw_glu])

    du, db_re_dense, db_im_dense, dlam_re, dlam_im, dc_re_dense, dc_im_dense, dd_skip = _s5_bwd(
        proj, mats, dskip_row, y_ssm, dz_direct, dz_glu, d_attn, d_ssm, [token])
    da_re, da_im, dlog_dt, db_re_v, db_im_v = _s5_discretise_bwd(
        a_re[0], a_im[0], ldt_col, b_re_v, b_im_v, dlam_re.reshape(n_groups, SSM_STATE),
        dlam_im.reshape(n_groups, SSM_STATE), _block_diag_in_t(db_re_dense), _block_diag_in_t(db_im_dense))
    dq, dk2, dv2, dsinks_row = _attention_bwd(proj, cos, sin, sinks_row, dattn, d_attn)

    small_grads = {
        "sinks": dsinks_row, "a_re": da_re, "a_im": da_im, "log_dt": dlog_dt.reshape(1, n_groups),
        "b_re": db_re_v, "b_im": db_im_v, "c_re": _block_diag_out_t(dc_re_dense),
        "c_im": _block_diag_out_t(dc_im_dense), "d_skip": dd_skip.reshape(n_groups, SSM_GROUP).T, "b_glu": db_glu,
        "g_attn_out": dg_attn_out, "g_ssm_out": dg_ssm_out, "g_post_mix": dg_post_mix, "g_pre_ffn": dg_pre_ffn,
        "g_post_ffn": dg_post_ffn,
    }
    vec_pack, vec_slots, mat_pack, mat_rows = _pack_grads([small_grads[n] for n in _SMALL_VECTORS] + [loss_row],
                                                          [small_grads[n] for n in _SMALL_MATRICES])
    ag_small, token = _exchange_start("gather_small_grads", [vec_pack, mat_pack, small_grads["d_skip"]], False,
                                      (OWN,) + ALL_PEERS)
    dproj = _assemble_dproj(dq, dk2, dv2, du, d_in, [token])

    dw_in = _mm_tn("proj_in_dw", dproj, xn, BF16).reshape(N_DEV, d_in // N_DEV, d)
    swap_in, token = swap_halves("dw_in_glu", [dw_in, dw_glu.reshape(N_DEV, d_ssm // N_DEV, d_ssm)])
    dxn = _mm_nn("proj_in_dx", dproj, w_in_t, F32, after=[token])
    rs_in, token = scatter_chip_sums("dw_in_glu", swap_in, [dxn])

    def x_bwd(dh_, dxn_, xv, g):
        dx, dg = _rms_bwd(xv, g, dxn_)
        return [dh_ + dx], [dg]

    grad_x, dg_pre_mix = _rows("norm_in_bwd", x_bwd, [dh, dxn, x2], [g_pre_mix], [(d, F32)], [d], tm, after=[token])
    ag_last, token = _exchange_start("gather_g_pre_mix_grad", [dg_pre_mix], False, (OWN,) + ALL_PEERS)

    results = {}

    def adam_big(n, parts):
        r = parts.shape[1]
        tr = next((c for c in range(192, 15, -16) if r % c == 0), r)
        results[n] = _adam_sharded("adam_" + n, parts, by_rows(n, weights[n]), by_rows(n, mom_m[n]),
                                   by_rows(n, mom_v[n]), tr)
        return results[n][3]

    done = [grad_x, token]
    adam_big("w_down", _exchange_wait(rs_down, done)[0])
    p_gate, p_up = _exchange_wait(rs_ffn_in, done)
    done = [adam_big("w_gate", p_gate), adam_big("w_up", p_up), results["w_down"][3]]
    done = [adam_big("w_o", _exchange_wait(rs_o, done)[0])]
    vec_parts, mat_parts, dskip_parts = _exchange_wait(ag_small, done)
    for n, row0 in zip(_SMALL_MATRICES, mat_rows):
        rows = view(n, weights[n]).shape[0]
        results[n] = _adam_sharded("adam_" + n, mat_parts, view(n, weights[n]), view(n, mom_m[n]), view(n, mom_v[n]),
                                   rows, row0)
    p_in, p_glu = _exchange_wait(rs_in, [results[n][3] for n in _SMALL_MATRICES])
    done = [adam_big("w_in", p_in), adam_big("w_glu", p_glu)]
    first_gain_parts, = _exchange_wait(ag_last, done)
    rest = _SMALL_VECTORS + ("d_skip", "g_pre_mix")
    found_at = [(0, row, lane) for row, lane, _ in vec_slots[:-1]] + [(1, 0, 0), (2, 0, 0)]
    updated, loss_sum = _adam_replicated([vec_parts, dskip_parts, first_gain_parts], found_at,
                                         [view(n, weights[n]) for n in rest], [view(n, mom_m[n]) for n in rest],
                                         [view(n, mom_v[n]) for n in rest], (0,) + vec_slots[-1])
    results.update(zip(rest, updated))

    outs = [loss_sum[0, 0], grad_x[None]]
    for k in range(4):
        for n in _ORDER:
            val = results[n][k]
            outs.append(val.T[None] if n in _BY_COLUMNS else val[None] if n in _BIG else unview(n, val))
    return tuple(outs)
```
